```python
import jax, jax.numpy as jnp
from jax import lax
import numpy as np

D_MODEL = 1024
BATCH = 8
SEQ = 4096
DEPTH = 1

CHUNK = 64
N_META = 16
D_A = 1024
D_B = 1024
CONV_A = 31
CONV_B = 3
EPS = 1e-6
SPLITS = (D_A, D_A, D_A, D_B, D_B, D_B, D_B, D_MODEL, D_MODEL)
D_IN = sum(SPLITS)

kernel_name = "hybrid_gated_conformer_shortconv_block"


def _rmsnorm(x, g):
    xf = x.astype(jnp.float32)
    y = xf * lax.rsqrt(jnp.mean(xf * xf, axis=-1, keepdims=True) + EPS)
    return (y * g.astype(jnp.float32)).astype(x.dtype)


def _layernorm(x, g, b):
    xf = x.astype(jnp.float32)
    mu = jnp.mean(xf, axis=-1, keepdims=True)
    var = jnp.mean(jnp.square(xf - mu), axis=-1, keepdims=True)
    y = (xf - mu) * lax.rsqrt(var + EPS)
    return (y * g.astype(jnp.float32) + b.astype(jnp.float32)).astype(x.dtype)


def _causal_dwconv(x, w):
    k = w.shape[0]
    return lax.conv_general_dilated(
        x, w.astype(x.dtype)[:, None, :], window_strides=(1,), padding=[(k - 1, 0)],
        dimension_numbers=("NWC", "WIO", "NWC"), feature_group_count=x.shape[-1])


def _fwd_setup_inputs(seed: int = 0) -> dict:
    key = jax.random.key(seed)
    ks = jax.random.split(key, 16)
    f = jnp.float32
    L = DEPTH
    nrm = lambda k, shp, s: jax.random.normal(k, shp, f) * s
    return {
        "x": jax.random.normal(ks[0], (BATCH, SEQ, D_MODEL), f),
        "meta_tokens": nrm(ks[1], (N_META, D_MODEL), 1.0),
        "norm_g": 1.0 + nrm(ks[2], (L, D_MODEL), 0.02),
        "w_in": nrm(ks[3], (L, D_MODEL, D_IN), D_MODEL ** -0.5),
        "conv_a_w": nrm(ks[4], (L, CONV_A, D_A), CONV_A ** -0.5),
        "conv_a_b": nrm(ks[5], (L, D_A), 0.02),
        "ln_a_g": 1.0 + nrm(ks[6], (L, D_A), 0.02),
        "ln_a_b": nrm(ks[7], (L, D_A), 0.02),
        "w_a_out": nrm(ks[8], (L, D_A, D_MODEL), D_A ** -0.5),
        "b_a_out": nrm(ks[9], (L, D_MODEL), 0.02),
        "conv_b_w": nrm(ks[10], (L, CONV_B, D_B), CONV_B ** -0.5),
        "w_b_out": nrm(ks[11], (L, D_B, D_MODEL), D_B ** -0.5),
        "w_out": nrm(ks[12], (L, D_MODEL, D_MODEL), D_MODEL ** -0.5),
        "final_g": 1.0 + nrm(ks[13], (D_MODEL,), 0.02),
    }


def _fwd_reference(x, meta_tokens, norm_g, w_in, conv_a_w, conv_a_b, ln_a_g, ln_a_b,
              w_a_out, b_a_out, conv_b_w, w_b_out, w_out, final_g):
    bsz = x.shape[0]
    meta = jnp.broadcast_to(meta_tokens.astype(x.dtype)[None], (bsz, N_META, D_MODEL))
    s = jnp.concatenate([meta, x], axis=1)
    idx = np.cumsum(SPLITS)[:-1].tolist()
    for l in range(DEPTH):
        h = _rmsnorm(s, norm_g[l])
        proj = jnp.einsum("bld,de->ble", h, w_in[l])
        a_val, a_glu, a_z, b_B, b_C, b_x, b_z, g_a, g_b = jnp.split(proj, idx, axis=-1)
        ua = a_val * jax.nn.sigmoid(a_glu)
        ua = _causal_dwconv(ua, conv_a_w[l]) + conv_a_b[l]
        ua = jax.nn.silu(_layernorm(ua, ln_a_g[l], ln_a_b[l]))
        ya = jnp.einsum("blc,cd->bld", ua * jax.nn.silu(a_z), w_a_out[l]) + b_a_out[l]
        ub = b_B * _causal_dwconv(b_C * b_x, conv_b_w[l])
        yb = jnp.einsum("blc,cd->bld", ub * jax.nn.silu(b_z), w_b_out[l])
        m = jax.nn.sigmoid(g_a) * ya + jax.nn.sigmoid(g_b) * yb
        s = s + jnp.einsum("bld,de->ble", m, w_out[l])
    y = _rmsnorm(s, final_g)
    return y[:, N_META:, :]


import jax as _jax
import jax.numpy as _jnp

TWIN_FORMAT = 'train_step'
FWD_PARAMS = ['x', 'meta_tokens', 'norm_g', 'w_in', 'conv_a_w', 'conv_a_b', 'ln_a_g', 'ln_a_b', 'w_a_out', 'b_a_out', 'conv_b_w', 'w_b_out', 'w_out', 'final_g']
TWIN_WEIGHTS = ['meta_tokens', 'norm_g', 'w_in', 'conv_a_w', 'conv_a_b', 'ln_a_g', 'ln_a_b', 'w_a_out', 'b_a_out', 'conv_b_w', 'w_b_out', 'w_out', 'final_g']
TWIN_DIFF_INPUT = 'x'
TWIN_INPUTS = ['x', 'meta_tokens', 'norm_g', 'w_in', 'conv_a_w', 'conv_a_b', 'ln_a_g', 'ln_a_b', 'w_a_out', 'b_a_out', 'conv_b_w', 'w_b_out', 'w_out', 'final_g', 'loss_target', 'm_meta_tokens', 'm_norm_g', 'm_w_in', 'm_conv_a_w', 'm_conv_a_b', 'm_ln_a_g', 'm_ln_a_b', 'm_w_a_out', 'm_b_a_out', 'm_conv_b_w', 'm_w_b_out', 'm_w_out', 'm_final_g', 'v_meta_tokens', 'v_norm_g', 'v_w_in', 'v_conv_a_w', 'v_conv_a_b', 'v_ln_a_g', 'v_ln_a_b', 'v_w_a_out', 'v_b_a_out', 'v_conv_b_w', 'v_w_b_out', 'v_w_out', 'v_final_g']
TWIN_OUTPUTS = ['loss', 'grad_x', 'grad_meta_tokens', 'grad_norm_g', 'grad_w_in', 'grad_conv_a_w', 'grad_conv_a_b', 'grad_ln_a_g', 'grad_ln_a_b', 'grad_w_a_out', 'grad_b_a_out', 'grad_conv_b_w', 'grad_w_b_out', 'grad_w_out', 'grad_final_g', 'delta_meta_tokens', 'delta_norm_g', 'delta_w_in', 'delta_conv_a_w', 'delta_conv_a_b', 'delta_ln_a_g', 'delta_ln_a_b', 'delta_w_a_out', 'delta_b_a_out', 'delta_conv_b_w', 'delta_w_b_out', 'delta_w_out', 'delta_final_g', 'new_m_meta_tokens', 'new_m_norm_g', 'new_m_w_in', 'new_m_conv_a_w', 'new_m_conv_a_b', 'new_m_ln_a_g', 'new_m_ln_a_b', 'new_m_w_a_out', 'new_m_b_a_out', 'new_m_conv_b_w', 'new_m_w_b_out', 'new_m_w_out', 'new_m_final_g', 'new_v_meta_tokens', 'new_v_norm_g', 'new_v_w_in', 'new_v_conv_a_w', 'new_v_conv_a_b', 'new_v_ln_a_g', 'new_v_ln_a_b', 'new_v_w_a_out', 'new_v_b_a_out', 'new_v_conv_b_w', 'new_v_w_b_out', 'new_v_w_out', 'new_v_final_g']
TWIN_LEAF_KINDS = {'loss': 'loss', 'grad_x': 'grad_x', 'grad_meta_tokens': 'grad_w', 'grad_norm_g': 'grad_w', 'grad_w_in': 'grad_w', 'grad_conv_a_w': 'grad_w', 'grad_conv_a_b': 'grad_w', 'grad_ln_a_g': 'grad_w', 'grad_ln_a_b': 'grad_w', 'grad_w_a_out': 'grad_w', 'grad_b_a_out': 'grad_w', 'grad_conv_b_w': 'grad_w', 'grad_w_b_out': 'grad_w', 'grad_w_out': 'grad_w', 'grad_final_g': 'grad_w', 'delta_meta_tokens': 'delta_w', 'delta_norm_g': 'delta_w', 'delta_w_in': 'delta_w', 'delta_conv_a_w': 'delta_w', 'delta_conv_a_b': 'delta_w', 'delta_ln_a_g': 'delta_w', 'delta_ln_a_b': 'delta_w', 'delta_w_a_out': 'delta_w', 'delta_b_a_out': 'delta_w', 'delta_conv_b_w': 'delta_w', 'delta_w_b_out': 'delta_w', 'delta_w_out': 'delta_w', 'delta_final_g': 'delta_w', 'new_m_meta_tokens': 'new_m', 'new_m_norm_g': 'new_m', 'new_m_w_in': 'new_m', 'new_m_conv_a_w': 'new_m', 'new_m_conv_a_b': 'new_m', 'new_m_ln_a_g': 'new_m', 'new_m_ln_a_b': 'new_m', 'new_m_w_a_out': 'new_m', 'new_m_b_a_out': 'new_m', 'new_m_conv_b_w': 'new_m', 'new_m_w_b_out': 'new_m', 'new_m_w_out': 'new_m', 'new_m_final_g': 'new_m', 'new_v_meta_tokens': 'new_v', 'new_v_norm_g': 'new_v', 'new_v_w_in': 'new_v', 'new_v_conv_a_w': 'new_v', 'new_v_conv_a_b': 'new_v', 'new_v_ln_a_g': 'new_v', 'new_v_ln_a_b': 'new_v', 'new_v_w_a_out': 'new_v', 'new_v_b_a_out': 'new_v', 'new_v_conv_b_w': 'new_v', 'new_v_w_b_out': 'new_v', 'new_v_w_out': 'new_v', 'new_v_final_g': 'new_v'}


def _forward(args):
    return _fwd_reference(*[args[k] for k in FWD_PARAMS])


def _output_shape():
    def fwd():
        inp = _fwd_setup_inputs(0)
        return _fwd_reference(*[inp[k] for k in FWD_PARAMS])
    out = _jax.eval_shape(fwd)
    return out.shape, out.dtype

N_MICROBATCH = 1
ADAM_LR = 0.001
ADAM_B1 = 0.9
ADAM_B2 = 0.999
ADAM_EPS = 1e-08
ADAM_WD = 0.01
ADAM_STEP = 10
PER_EXAMPLE_BATCH_AXIS = {'x': 0, 'loss_target': 0}
SHARED_INPUTS = []
_WEIGHT_DTYPES = {'meta_tokens': _jnp.float32, 'norm_g': _jnp.float32, 'w_in': _jnp.float32, 'conv_a_w': _jnp.float32, 'conv_a_b': _jnp.float32, 'ln_a_g': _jnp.float32, 'ln_a_b': _jnp.float32, 'w_a_out': _jnp.float32, 'b_a_out': _jnp.float32, 'conv_b_w': _jnp.float32, 'w_b_out': _jnp.float32, 'w_out': _jnp.float32, 'final_g': _jnp.float32}
MOMENT_SCALE = {'meta_tokens': 2.381233e-03, 'norm_g': 1.249860e-01, 'w_in': 4.004686e-02, 'conv_a_w': 3.287001e-02, 'conv_a_b': 6.181423e-02, 'ln_a_g': 3.868083e-02, 'ln_a_b': 3.344138e-02, 'w_a_out': 3.223579e-02, 'b_a_out': 9.236053e-02, 'conv_b_w': 5.318666e-02, 'w_b_out': 5.304782e-02, 'w_out': 6.205696e-02, 'final_g': 3.200660e+01}


def _to_microbatches(a, axis):
    t = _jnp.moveaxis(a, axis, 0)
    t = t.reshape((N_MICROBATCH, t.shape[0] // N_MICROBATCH) + t.shape[1:])
    return _jnp.moveaxis(t, 1, axis + 1)


def setup_inputs(seed: int = 0) -> dict:
    inp = _fwd_setup_inputs(seed)
    key = _jax.random.fold_in(_jax.random.key(seed), 7919)
    shape, _ = _output_shape()
    out = dict(inp)
    out["loss_target"] = _jax.random.normal(_jax.random.fold_in(key, 0), shape, _jnp.float32)
    for i, name in enumerate(TWIN_WEIGHTS):
        w = inp[name].astype(_jnp.float32)
        if MOMENT_SCALE is None:
            s = _jnp.sqrt(_jnp.mean(_jnp.square(w)) + 1e-30)
        else:
            s = MOMENT_SCALE[name]
        km, kv = _jax.random.split(_jax.random.fold_in(key, i + 1))
        out[name] = w
        out["m_" + name] = s * _jax.random.normal(km, w.shape, _jnp.float32)
        out["v_" + name] = (s * s) * _jax.random.uniform(kv, w.shape, _jnp.float32, 0.5, 1.5)
    if N_MICROBATCH > 1:
        for name, axis in PER_EXAMPLE_BATCH_AXIS.items():
            out[name] = _to_microbatches(out[name], axis)
    return {'x': out['x'], 'meta_tokens': out['meta_tokens'], 'norm_g': out['norm_g'], 'w_in': out['w_in'], 'conv_a_w': out['conv_a_w'], 'conv_a_b': out['conv_a_b'], 'ln_a_g': out['ln_a_g'], 'ln_a_b': out['ln_a_b'], 'w_a_out': out['w_a_out'], 'b_a_out': out['b_a_out'], 'conv_b_w': out['conv_b_w'], 'w_b_out': out['w_b_out'], 'w_out': out['w_out'], 'final_g': out['final_g'], 'loss_target': out['loss_target'], 'm_meta_tokens': out['m_meta_tokens'], 'm_norm_g': out['m_norm_g'], 'm_w_in': out['m_w_in'], 'm_conv_a_w': out['m_conv_a_w'], 'm_conv_a_b': out['m_conv_a_b'], 'm_ln_a_g': out['m_ln_a_g'], 'm_ln_a_b': out['m_ln_a_b'], 'm_w_a_out': out['m_w_a_out'], 'm_b_a_out': out['m_b_a_out'], 'm_conv_b_w': out['m_conv_b_w'], 'm_w_b_out': out['m_w_b_out'], 'm_w_out': out['m_w_out'], 'm_final_g': out['m_final_g'], 'v_meta_tokens': out['v_meta_tokens'], 'v_norm_g': out['v_norm_g'], 'v_w_in': out['v_w_in'], 'v_conv_a_w': out['v_conv_a_w'], 'v_conv_a_b': out['v_conv_a_b'], 'v_ln_a_g': out['v_ln_a_g'], 'v_ln_a_b': out['v_ln_a_b'], 'v_w_a_out': out['v_w_a_out'], 'v_b_a_out': out['v_b_a_out'], 'v_conv_b_w': out['v_conv_b_w'], 'v_w_b_out': out['v_w_b_out'], 'v_w_out': out['v_w_out'], 'v_final_g': out['v_final_g']}


def _loss(weights, diff, rest, loss_target):
    with _jax.named_scope("forward"):
        args = {**rest, TWIN_DIFF_INPUT: diff, **{k: w.astype(_WEIGHT_DTYPES[k]) for k, w in weights.items()}}
        y = _forward(args)
    with _jax.named_scope("loss_head"):
        err = _jnp.square(y.astype(_jnp.float32) - loss_target)
        return 0.5 * _jnp.sum(_jnp.mean(err, axis=-1)) if err.ndim else 0.5 * err


def _adamw(w, g, m, v):
    m = ADAM_B1 * m + (1.0 - ADAM_B1) * g
    v = ADAM_B2 * v + (1.0 - ADAM_B2) * _jnp.square(g)
    m_hat = m / (1.0 - ADAM_B1 ** ADAM_STEP)
    v_hat = v / (1.0 - ADAM_B2 ** ADAM_STEP)
    delta = -ADAM_LR * (m_hat / (_jnp.sqrt(v_hat) + ADAM_EPS) + ADAM_WD * w)
    return delta, m, v


def reference(x, meta_tokens, norm_g, w_in, conv_a_w, conv_a_b, ln_a_g, ln_a_b, w_a_out, b_a_out, conv_b_w, w_b_out, w_out, final_g, loss_target, m_meta_tokens, m_norm_g, m_w_in, m_conv_a_w, m_conv_a_b, m_ln_a_g, m_ln_a_b, m_w_a_out, m_b_a_out, m_conv_b_w, m_w_b_out, m_w_out, m_final_g, v_meta_tokens, v_norm_g, v_w_in, v_conv_a_w, v_conv_a_b, v_ln_a_g, v_ln_a_b, v_w_a_out, v_b_a_out, v_conv_b_w, v_w_b_out, v_w_out, v_final_g):
    given = dict(x=x, meta_tokens=meta_tokens, norm_g=norm_g, w_in=w_in, conv_a_w=conv_a_w, conv_a_b=conv_a_b, ln_a_g=ln_a_g, ln_a_b=ln_a_b, w_a_out=w_a_out, b_a_out=b_a_out, conv_b_w=conv_b_w, w_b_out=w_b_out, w_out=w_out, final_g=final_g, loss_target=loss_target, m_meta_tokens=m_meta_tokens, m_norm_g=m_norm_g, m_w_in=m_w_in, m_conv_a_w=m_conv_a_w, m_conv_a_b=m_conv_a_b, m_ln_a_g=m_ln_a_g, m_ln_a_b=m_ln_a_b, m_w_a_out=m_w_a_out, m_b_a_out=m_b_a_out, m_conv_b_w=m_conv_b_w, m_w_b_out=m_w_b_out, m_w_out=m_w_out, m_final_g=m_final_g, v_meta_tokens=v_meta_tokens, v_norm_g=v_norm_g, v_w_in=v_w_in, v_conv_a_w=v_conv_a_w, v_conv_a_b=v_conv_a_b, v_ln_a_g=v_ln_a_g, v_ln_a_b=v_ln_a_b, v_w_a_out=v_w_a_out, v_b_a_out=v_b_a_out, v_conv_b_w=v_conv_b_w, v_w_b_out=v_w_b_out, v_w_out=v_w_out, v_final_g=v_final_g)
    weights = {n: given[n] for n in TWIN_WEIGHTS}
    shared = {n: given[n] for n in SHARED_INPUTS}
    per_example = {n: given[n] for n in ['x']}
    grad_fn = _jax.value_and_grad(_loss, argnums=(0, 1))

    def one_microbatch(ex, loss_target):
        ex = dict(ex)
        diff = ex.pop(TWIN_DIFF_INPUT)
        return grad_fn(weights, diff, {**shared, **ex}, loss_target)

    if N_MICROBATCH == 1:
        loss, (grad_w, grad_x) = one_microbatch(per_example, given["loss_target"])
    else:
        def body(carry, xs):
            loss_sum, grad_sum = carry
            l_k, (gw_k, gx_k) = one_microbatch(xs[0], xs[1])
            with _jax.named_scope("update"):
                return (loss_sum + l_k, _jax.tree.map(_jnp.add, grad_sum, gw_k)), gx_k

        init = (_jnp.zeros((), _jnp.float32), _jax.tree.map(_jnp.zeros_like, weights))
        (loss, grad_w), grad_x = _jax.lax.scan(body, init, (per_example, given["loss_target"]))
    with _jax.named_scope("update"):
        delta_w, new_m, new_v = {}, {}, {}
        for n in TWIN_WEIGHTS:
            delta_w[n], new_m[n], new_v[n] = _adamw(weights[n], grad_w[n], given["m_" + n], given["v_" + n])
    return (loss, grad_x, *[grad_w[n] for n in TWIN_WEIGHTS], *[delta_w[n] for n in TWIN_WEIGHTS],
            *[new_m[n] for n in TWIN_WEIGHTS], *[new_v[n] for n in TWIN_WEIGHTS])
```

```python
import functools

import jax
import jax.numpy as jnp
from jax import lax
from jax.experimental import pallas as pl
from jax.experimental.pallas import tpu as pltpu

D_MODEL = 1024
N_META = 16
N_DEV = 8
D_IN = 9 * D_MODEL
COLS = D_IN // N_DEV
ROWS_OUT = D_MODEL // N_DEV
CONV_A = 31
CONV_B = 3
EPS = 1e-6

ADAM_LR = 0.001
ADAM_B1 = 0.9
ADAM_B2 = 0.999
ADAM_EPS = 1e-08
ADAM_WD = 0.01
ADAM_STEP = 10

TILE = 128
LANES = 128
N_CHUNK = D_MODEL // LANES
ROW_CHUNK = 32
SUBLANES = 8
VMEM_LIMIT = 56 * 1024 * 1024

ROW_FINAL_G, ROW_B_A_OUT, ROW_LN_G, ROW_LN_B, ROW_CONV_A_B, ROW_LOSS = 0, 1, 2, 3, 4, 5
ROW_NORM_G_X, ROW_NORM_G_META = 8, 16
ROW_CONV_A_W, ROW_CONV_B_W, ROW_META = 24, 56, 64
SMALL_ROWS = 80

MESH = pl.DeviceIdType.MESH
_ANY = pl.BlockSpec(memory_space=pl.ANY)
_VMEM = pl.BlockSpec(memory_space=pltpu.VMEM)
BF16 = jnp.bfloat16
F32 = jnp.float32


def _sigmoid(v):
    return jax.nn.sigmoid(v)


def _dot(a, b):
    return jnp.dot(a, b, preferred_element_type=F32)


def _dot_nt(a, b):
    return lax.dot_general(a, b, (((1,), (1,)), ((), ())), preferred_element_type=F32)


def _dot_tn(a, b):
    return lax.dot_general(a, b, (((0,), (0,)), ((), ())), preferred_element_type=F32)


def _colsum(v):
    return jnp.sum(v, axis=0, keepdims=True)


def _rowmean(v):
    return jnp.mean(v, axis=-1, keepdims=True)


def _fold8(v):
    parts = [v[SUBLANES * g:SUBLANES * (g + 1)] for g in range(v.shape[0] // SUBLANES)]
    return functools.reduce(jnp.add, parts)


def _all_gather(shard, name):
    m, n = shard.shape

    def body(x_ref, out_ref, send_sems, recv_sems, local_sem):
        x, y, c = lax.axis_index("x"), lax.axis_index("y"), lax.axis_index("c")
        me, sibling = (x, y, c), (x, y, 1 - c)
        chips = [(1 - x, y), (x, 1 - y), (1 - x, 1 - y)]

        def slot(px, py, pc):
            return out_ref.at[4 * px + 2 * py + pc]

        def copy(k, block, to, src=None):
            return pltpu.make_async_remote_copy(
                src_ref=slot(*block) if src is None else src, dst_ref=slot(*block),
                send_sem=send_sems.at[k], recv_sem=recv_sems.at[k], device_id=to, device_id_type=MESH)

        mine = pltpu.make_async_copy(x_ref, slot(*me), local_sem)
        mine.start()
        first = [copy(0, me, sibling, src=x_ref)]
        first += [copy(1 + j, me, (*chip, c), src=x_ref) for j, chip in enumerate(chips)]
        for cp in first:
            cp.start()
        passed = [copy(4 + j, (*chip, c), sibling) for j, chip in enumerate(chips)]
        for j, chip in enumerate(chips):
            copy(1 + j, (*chip, c), me).wait_recv()
            passed[j].start()
        copy(0, sibling, me).wait_recv()
        for j, chip in enumerate(chips):
            copy(4 + j, (*chip, 1 - c), me).wait_recv()
        for cp in first + passed:
            cp.wait_send()
        mine.wait()

    return pl.pallas_call(
        body, name=name,
        out_shape=jax.ShapeDtypeStruct((N_DEV, m, n), shard.dtype),
        in_specs=[_ANY], out_specs=_ANY,
        scratch_shapes=[pltpu.SemaphoreType.DMA((7,)), pltpu.SemaphoreType.DMA((7,)), pltpu.SemaphoreType.DMA(())],
    )(shard)


def _sibling_exchange(partials, name):
    n_arr = len(partials)

    def body(*refs):
        srcs, dsts = refs[:n_arr], refs[n_arr:2 * n_arr]
        send_sems, recv_sems = refs[2 * n_arr:]
        x, y, c = lax.axis_index("x"), lax.axis_index("y"), lax.axis_index("c")
        copies = []
        for a in range(n_arr):
            for q in range(4):
                copies.append(pltpu.make_async_remote_copy(
                    src_ref=srcs[a].at[2 * q + (1 - c)], dst_ref=dsts[a].at[q],
                    send_sem=send_sems.at[4 * a + q], recv_sem=recv_sems.at[4 * a + q],
                    device_id=(x, y, 1 - c), device_id_type=MESH))
        for cp in copies:
            cp.start()
        for cp in copies:
            cp.wait()

    return pl.pallas_call(
        body, name=name,
        out_shape=[jax.ShapeDtypeStruct((4,) + p.shape[1:], p.dtype) for p in partials],
        in_specs=[_ANY] * n_arr, out_specs=[_ANY] * n_arr,
        scratch_shapes=[pltpu.SemaphoreType.DMA((4 * n_arr,)), pltpu.SemaphoreType.DMA((4 * n_arr,))],
    )(*partials)


def _chip_exchange(parts, name):
    n_arr = len(parts)

    def body(*refs):
        srcs, dsts = refs[:n_arr], refs[n_arr:2 * n_arr]
        send_sems, recv_sems = refs[2 * n_arr:]
        x, y, c = lax.axis_index("x"), lax.axis_index("y"), lax.axis_index("c")
        targets = [(x, 1 - y, c), (1 - x, y, c), (1 - x, 1 - y, c)]
        copies = []
        for a in range(n_arr):
            for k in range(3):
                copies.append(pltpu.make_async_remote_copy(
                    src_ref=srcs[a].at[k], dst_ref=dsts[a].at[k],
                    send_sem=send_sems.at[3 * a + k], recv_sem=recv_sems.at[3 * a + k],
                    device_id=targets[k], device_id_type=MESH))
        for cp in copies:
            cp.start()
        for cp in copies:
            cp.wait()

    return pl.pallas_call(
        body, name=name,
        out_shape=[jax.ShapeDtypeStruct(p.shape, p.dtype) for p in parts],
        in_specs=[_ANY] * n_arr, out_specs=[_ANY] * n_arr,
        scratch_shapes=[pltpu.SemaphoreType.DMA((3 * n_arr,)), pltpu.SemaphoreType.DMA((3 * n_arr,))],
    )(*parts)


def _chip_partial(pos, mine, theirs, relations, out_dtype, row_tile, name):
    _, m, n = mine.shape
    q0 = relations[0]

    def chip_of(qi, pos_ref):
        q = qi + q0
        return pos_ref[0] ^ (q >> 1), pos_ref[1] ^ (q & 1)

    def mine_map(qi, t, pos_ref):
        px, py = chip_of(qi, pos_ref)
        return 4 * px + 2 * py + pos_ref[2], t, 0

    def theirs_map(qi, t, pos_ref):
        px, py = chip_of(qi, pos_ref)
        return 2 * px + py, t, 0

    def body(pos_ref, a_ref, b_ref, o_ref):
        o_ref[...] = (a_ref[...] + b_ref[...]).astype(out_dtype)

    return pl.pallas_call(
        body, name=name,
        out_shape=jax.ShapeDtypeStruct((len(relations), m, n), out_dtype),
        grid_spec=pltpu.PrefetchScalarGridSpec(
            num_scalar_prefetch=1, grid=(len(relations), m // row_tile),
            in_specs=[pl.BlockSpec((None, row_tile, n), mine_map), pl.BlockSpec((None, row_tile, n), theirs_map)],
            out_specs=pl.BlockSpec((None, row_tile, n), lambda qi, t, pos_ref: (qi, t, 0))),
        compiler_params=pltpu.CompilerParams(dimension_semantics=("arbitrary", "arbitrary")),
    )(pos, mine, theirs)


def _reduce_scatter(pos, partials, row_tiles, tag):
    theirs = _sibling_exchange(partials, f"rs_sibling_{tag}")
    own = [_chip_partial(pos, p, t, (0,), F32, rt, f"rs_own_{tag}{a}")
           for a, (p, t, rt) in enumerate(zip(partials, theirs, row_tiles))]
    parts = [_chip_partial(pos, p, t, (1, 2, 3), BF16, rt, f"rs_parts_{tag}{a}")
             for a, (p, t, rt) in enumerate(zip(partials, theirs, row_tiles))]
    landed = _chip_exchange(parts, f"rs_chips_{tag}")
    return list(zip(own, landed))


def _rmsnorm_fwd(x2d, meta_tile, norm_g, n_tiles):
    def body(x_ref, meta_ref, g_ref, h_ref):
        i = pl.program_id(0)
        s0 = jnp.where(i == n_tiles - 1, meta_ref[...], x_ref[...])
        r = lax.rsqrt(_rowmean(s0 * s0) + EPS)
        h_ref[...] = ((s0 * r) * g_ref[...]).astype(BF16)

    return pl.pallas_call(
        body, name="rmsnorm_fwd", grid=(n_tiles,),
        out_shape=jax.ShapeDtypeStruct((n_tiles * TILE, D_MODEL), BF16),
        in_specs=[pl.BlockSpec((TILE, D_MODEL), lambda i: (jnp.minimum(i, n_tiles - 2), 0)), _VMEM, _VMEM],
        out_specs=pl.BlockSpec((TILE, D_MODEL), lambda i: (i, 0)),
        compiler_params=pltpu.CompilerParams(dimension_semantics=("arbitrary",)),
    )(x2d, meta_tile, norm_g)


def _proj_fwd(h, w_in_all, row_tile):
    tp = h.shape[0]

    def body(h_ref, w_ref, o_ref):
        o_ref[...] = _dot(h_ref[...], w_ref[...])

    return pl.pallas_call(
        body, name="proj_fwd", grid=(N_DEV, tp // row_tile),
        out_shape=jax.ShapeDtypeStruct((tp, D_IN), F32),
        in_specs=[pl.BlockSpec((row_tile, D_MODEL), lambda j, t: (t, 0)),
                  pl.BlockSpec((None, D_MODEL, COLS), lambda j, t: (j, 0, 0))],
        out_specs=pl.BlockSpec((row_tile, COLS), lambda j, t: (t, j)),
        compiler_params=pltpu.CompilerParams(dimension_semantics=("arbitrary", "arbitrary"),
                                             vmem_limit_bytes=VMEM_LIMIT),
    )(h, w_in_all)


C_AVAL, C_AGLU, C_AZ, C_BB, C_BC, C_BX, C_BZ, C_GA, C_GB = (k * D_MODEL for k in range(9))
S_AZ, S_BB, S_BZ, S_GA, S_GB = (k * D_MODEL for k in range(5))


def _fused_pass(proj, x2d, tgt2d, meta_tile, conv_a_w, conv_a_b, ln_a_g, ln_a_b, b_a_out, conv_b_w, final_g,
                w_a, w_b, w_o, n_tiles):
    T = TILE
    tp = n_tiles * T
    inv_d = 1.0 / D_MODEL

    def block_of(tile):
        return jnp.where(tile == 0, n_tiles - 1, tile - 1)

    def cur(i):
        return block_of(jnp.minimum(i, n_tiles - 1))

    def prev(i):
        return block_of(jnp.clip(i - 1, 0, n_tiles - 1))

    def xblk(i):
        return jnp.maximum(jnp.minimum(i, n_tiles - 1) - 1, 0)

    def body(proj_ref, x_ref, tgt_ref, meta_ref, caw_ref, cab_ref, lng_ref, lnb_ref, bao_ref, cbw_ref, fg_ref,
             wa_ref, wb_ref, wo_ref,
             dproj_ref, ds1_ref, lhs_ref, rhs_ref, vec_ref, dcaw_ref, dcbw_ref,
             ua0_buf, cb_buf, dua1_buf, dc3_buf, aprev, cprev, stage, ua1_buf, c3_buf, xhat_buf, rstd_buf,
             dpa_buf, dpb_buf, dcaw8, dcbw8):
        i = pl.program_id(0)

        @pl.when(i == 0)
        def _init():
            for buf in (ua0_buf, cb_buf, dua1_buf, dc3_buf, aprev, cprev, dcaw8, dcbw8):
                buf[...] = jnp.zeros(buf.shape, buf.dtype)
            vec_ref[...] = jnp.zeros(vec_ref.shape, F32)

        @pl.when(i >= 1)
        def _emit_stage():
            dproj_ref[:, C_AZ:C_BC] = stage[:, S_AZ:S_BZ]
            dproj_ref[:, C_BZ:D_IN] = stage[:, S_BZ:S_GB + D_MODEL]

        @pl.when(i < n_tiles)
        def _front():
            def conv_chunk(cc, carry):
                c0 = pl.multiple_of(cc * LANES, LANES)
                lanes = pl.ds(c0, LANES)

                def col(base):
                    return pl.ds(pl.multiple_of(base + cc * LANES, LANES), LANES)

                ua0 = proj_ref[:, col(C_AVAL)] * _sigmoid(proj_ref[:, col(C_AGLU)])
                ua0_buf[T:2 * T, lanes] = ua0
                acc = jnp.broadcast_to(cab_ref[:, lanes], (T, LANES))
                for k in range(CONV_A):
                    acc = acc + caw_ref[k:k + 1, lanes] * ua0_buf[pl.ds(T - (CONV_A - 1) + k, T), lanes]
                ua1_buf[:, lanes] = acc
                cb = proj_ref[:, col(C_BC)] * proj_ref[:, col(C_BX)]
                cb_buf[T:2 * T, lanes] = cb
                acc3 = cbw_ref[0:1, lanes] * cb_buf[pl.ds(T - 2, T), lanes]
                for k in range(1, CONV_B):
                    acc3 = acc3 + cbw_ref[k:k + 1, lanes] * cb_buf[pl.ds(T - (CONV_B - 1) + k, T), lanes]
                c3_buf[:, lanes] = acc3
                return carry

            lax.fori_loop(0, N_CHUNK, conv_chunk, 0)

            def gate_rows(rc, carry):
                rows = pl.ds(pl.multiple_of(rc * ROW_CHUNK, ROW_CHUNK), ROW_CHUNK)
                ua1 = ua1_buf[rows, :]
                xc = ua1 - _rowmean(ua1)
                rstd = lax.rsqrt(_rowmean(xc * xc) + EPS)
                xhat = xc * rstd
                xhat_buf[rows, :] = xhat
                rstd_buf[rows, :] = rstd
                ua2 = xhat * lng_ref[...] + lnb_ref[...]
                ua3 = ua2 * _sigmoid(ua2)
                a_z = proj_ref[rows, C_AZ:C_AZ + D_MODEL]
                lhs_ref[0, rows, :] = (ua3 * (a_z * _sigmoid(a_z))).astype(BF16)
                b_z = proj_ref[rows, C_BZ:C_BZ + D_MODEL]
                ub = proj_ref[rows, C_BB:C_BB + D_MODEL] * c3_buf[rows, :]
                lhs_ref[1, rows, :] = (ub * (b_z * _sigmoid(b_z))).astype(BF16)
                return carry

            lax.fori_loop(0, T // ROW_CHUNK, gate_rows, 0)

            ya = _dot(lhs_ref[0], wa_ref[...]) + bao_ref[...]
            yb = _dot(lhs_ref[1], wb_ref[...])
            sga = _sigmoid(proj_ref[:, C_GA:C_GA + D_MODEL])
            sgb = _sigmoid(proj_ref[:, C_GB:C_GB + D_MODEL])
            m_b = (sga * ya + sgb * yb).astype(BF16)
            lhs_ref[2] = m_b
            s0 = jnp.where(i == 0, meta_ref[...], x_ref[...])
            s1 = s0 + _dot(m_b, wo_ref[...])
            r1 = lax.rsqrt(_rowmean(s1 * s1) + EPS)
            y = (s1 * r1) * fg_ref[...]
            is_token = (i >= 1).astype(F32)
            err = (y - tgt_ref[...]) * is_token
            vec_ref[ROW_LOSS:ROW_LOSS + 1, :] += (0.5 * inv_d) * _colsum(err * err)
            dy = err * inv_d
            vec_ref[ROW_FINAL_G:ROW_FINAL_G + 1, :] += _colsum(dy * (s1 * r1))
            gy = dy * fg_ref[...]
            ds1 = r1 * gy - s1 * ((r1 * r1 * r1) * _rowmean(gy * s1))
            ds1_ref[...] = ds1
            ds1_b = ds1.astype(BF16)
            rhs_ref[2] = ds1_b
            dm = _dot_nt(ds1_b, wo_ref[...])
            dya = dm * sga
            dyb = dm * sgb
            stage[:, S_GA:S_GA + D_MODEL] = (dm * ya * (sga * (1.0 - sga))).astype(BF16)
            stage[:, S_GB:S_GB + D_MODEL] = (dm * yb * (sgb * (1.0 - sgb))).astype(BF16)
            vec_ref[ROW_B_A_OUT:ROW_B_A_OUT + 1, :] += _colsum(dya)
            dya_b = dya.astype(BF16)
            dyb_b = dyb.astype(BF16)
            rhs_ref[0] = dya_b
            rhs_ref[1] = dyb_b
            dpa_buf[...] = _dot_nt(dya_b, wa_ref[...])
            dpb_buf[...] = _dot_nt(dyb_b, wb_ref[...])

            def gate_rows_bwd(rc, carry):
                r0 = pl.multiple_of(rc * ROW_CHUNK, ROW_CHUNK)
                rows = pl.ds(r0, ROW_CHUNK)
                later = pl.ds(pl.multiple_of(T + rc * ROW_CHUNK, ROW_CHUNK), ROW_CHUNK)
                xhat = xhat_buf[rows, :]
                ua2 = xhat * lng_ref[...] + lnb_ref[...]
                sg2 = _sigmoid(ua2)
                ua3 = ua2 * sg2
                a_z = proj_ref[rows, C_AZ:C_AZ + D_MODEL]
                sz = _sigmoid(a_z)
                dpa = dpa_buf[rows, :]
                stage[rows, S_AZ:S_AZ + D_MODEL] = (dpa * ua3 * (sz * (1.0 + a_z * (1.0 - sz)))).astype(BF16)
                dua2 = dpa * (a_z * sz) * (sg2 * (1.0 + ua2 * (1.0 - sg2)))
                vec_ref[ROW_LN_G:ROW_LN_G + 1, :] += _colsum(dua2 * xhat)
                vec_ref[ROW_LN_B:ROW_LN_B + 1, :] += _colsum(dua2)
                dxh = dua2 * lng_ref[...]
                dua1 = rstd_buf[rows, :] * (dxh - _rowmean(dxh) - xhat * _rowmean(dxh * xhat))
                vec_ref[ROW_CONV_A_B:ROW_CONV_A_B + 1, :] += _colsum(dua1)
                dua1_buf[later, :] = dua1
                b_z = proj_ref[rows, C_BZ:C_BZ + D_MODEL]
                sbz = _sigmoid(b_z)
                b_b = proj_ref[rows, C_BB:C_BB + D_MODEL]
                c3 = c3_buf[rows, :]
                dpb = dpb_buf[rows, :]
                stage[rows, S_BZ:S_BZ + D_MODEL] = (dpb * (b_b * c3) * (sbz * (1.0 + b_z * (1.0 - sbz)))).astype(BF16)
                dub = dpb * (b_z * sbz)
                stage[rows, S_BB:S_BB + D_MODEL] = (dub * c3).astype(BF16)
                dc3_buf[later, :] = dub * b_b
                return carry

            lax.fori_loop(0, T // ROW_CHUNK, gate_rows_bwd, 0)

        @pl.when(i == n_tiles)
        def _no_later_tile():
            dua1_buf[T:2 * T, :] = jnp.zeros((T, D_MODEL), F32)
            dc3_buf[T:2 * T, :] = jnp.zeros((T, D_MODEL), F32)

        @pl.when(i >= 1)
        def _lagged():
            def convt_chunk(cc, carry):
                c0 = pl.multiple_of(cc * LANES, LANES)
                lanes = pl.ds(c0, LANES)

                def col(base):
                    return pl.ds(pl.multiple_of(base + cc * LANES, LANES), LANES)

                ua0 = ua0_buf[0:T, lanes]
                acc = jnp.zeros((T, LANES), F32)
                for j in range(CONV_A):
                    k = CONV_A - 1 - j
                    later = dua1_buf[pl.ds(j, T), lanes]
                    acc = acc + caw_ref[k:k + 1, lanes] * later
                    dcaw8[SUBLANES * k:SUBLANES * (k + 1), lanes] += _fold8(ua0 * later)
                a_val = aprev[:, col(0)]
                sg = _sigmoid(aprev[:, col(D_MODEL)])
                dproj_ref[:, col(C_AVAL)] = (acc * sg).astype(BF16)
                dproj_ref[:, col(C_AGLU)] = (acc * a_val * (sg * (1.0 - sg))).astype(BF16)

                cb = cb_buf[0:T, lanes]
                acc3 = jnp.zeros((T, LANES), F32)
                for j in range(CONV_B):
                    k = CONV_B - 1 - j
                    later = dc3_buf[pl.ds(j, T), lanes]
                    acc3 = acc3 + cbw_ref[k:k + 1, lanes] * later
                    dcbw8[SUBLANES * k:SUBLANES * (k + 1), lanes] += _fold8(cb * later)
                dproj_ref[:, col(C_BC)] = (acc3 * cprev[:, col(D_MODEL)]).astype(BF16)
                dproj_ref[:, col(C_BX)] = (acc3 * cprev[:, col(0)]).astype(BF16)
                return carry

            lax.fori_loop(0, N_CHUNK, convt_chunk, 0)

        for buf in (ua0_buf, cb_buf, dua1_buf, dc3_buf):
            buf[0:T, :] = buf[T:2 * T, :]
        aprev[...] = proj_ref[:, C_AVAL:C_AZ]
        cprev[...] = proj_ref[:, C_BC:C_BZ]

        @pl.when(i == n_tiles)
        def _finish():
            for k in range(CONV_A):
                dcaw_ref[k:k + 1, :] = _colsum(dcaw8[SUBLANES * k:SUBLANES * (k + 1), :])
            dcaw_ref[CONV_A:CONV_A + 1, :] = jnp.zeros((1, D_MODEL), F32)
            for k in range(CONV_B):
                dcbw_ref[k:k + 1, :] = _colsum(dcbw8[SUBLANES * k:SUBLANES * (k + 1), :])
            dcbw_ref[CONV_B:SUBLANES, :] = jnp.zeros((SUBLANES - CONV_B, D_MODEL), F32)

    tile_in = lambda width: pl.BlockSpec((T, width), lambda i: (cur(i), 0))
    return pl.pallas_call(
        body, name="fused_pass", grid=(n_tiles + 1,),
        out_shape=[
            jax.ShapeDtypeStruct((tp, D_IN), BF16),
            jax.ShapeDtypeStruct((tp, D_MODEL), F32),
            jax.ShapeDtypeStruct((3, tp, D_MODEL), BF16),
            jax.ShapeDtypeStruct((3, tp, D_MODEL), BF16),
            jax.ShapeDtypeStruct((SUBLANES, D_MODEL), F32),
            jax.ShapeDtypeStruct((32, D_MODEL), F32),
            jax.ShapeDtypeStruct((SUBLANES, D_MODEL), F32),
        ],
        in_specs=[
            tile_in(D_IN),
            pl.BlockSpec((T, D_MODEL), lambda i: (xblk(i), 0)),
            pl.BlockSpec((T, D_MODEL), lambda i: (xblk(i), 0)),
            _VMEM, _VMEM, _VMEM, _VMEM, _VMEM, _VMEM, _VMEM, _VMEM, _VMEM, _VMEM, _VMEM,
        ],
        out_specs=[
            pl.BlockSpec((T, D_IN), lambda i: (prev(i), 0)),
            pl.BlockSpec((T, D_MODEL), lambda i: (cur(i), 0)),
            pl.BlockSpec((3, T, D_MODEL), lambda i: (0, cur(i), 0)),
            pl.BlockSpec((3, T, D_MODEL), lambda i: (0, cur(i), 0)),
            _VMEM, _VMEM, _VMEM,
        ],
        scratch_shapes=[
            pltpu.VMEM((2 * T, D_MODEL), F32),
            pltpu.VMEM((2 * T, D_MODEL), F32),
            pltpu.VMEM((2 * T, D_MODEL), F32),
            pltpu.VMEM((2 * T, D_MODEL), F32),
            pltpu.VMEM((T, 2 * D_MODEL), F32),
            pltpu.VMEM((T, 2 * D_MODEL), F32),
            pltpu.VMEM((T, 5 * D_MODEL), BF16),
            pltpu.VMEM((T, D_MODEL), F32),
            pltpu.VMEM((T, D_MODEL), F32),
            pltpu.VMEM((T, D_MODEL), F32),
            pltpu.VMEM((T, 1), F32),
            pltpu.VMEM((T, D_MODEL), F32),
            pltpu.VMEM((T, D_MODEL), F32),
            pltpu.VMEM((32 * SUBLANES, D_MODEL), F32),
            pltpu.VMEM((SUBLANES * SUBLANES, D_MODEL), F32),
        ],
        compiler_params=pltpu.CompilerParams(dimension_semantics=("arbitrary",), vmem_limit_bytes=VMEM_LIMIT),
    )(proj, x2d, tgt2d, meta_tile, conv_a_w, conv_a_b, ln_a_g, ln_a_b, b_a_out, conv_b_w, final_g, w_a, w_b, w_o)


def _input_bwd(dproj, ds1, s0, norm_g, w_in_all, row_tile, n_tiles, block0, name):
    def body(dp_ref, ds1_ref, s0_ref, g_ref, w_ref, out_ref, vec_ref):
        t = pl.program_id(0)

        @pl.when(t == 0)
        def _():
            vec_ref[...] = jnp.zeros(vec_ref.shape, F32)

        dh = _dot_nt(dp_ref[:, 0:COLS], w_ref[0])
        for j in range(1, N_DEV):
            dh = dh + _dot_nt(dp_ref[:, j * COLS:(j + 1) * COLS], w_ref[j])
        s0v = s0_ref[...]
        r = lax.rsqrt(_rowmean(s0v * s0v) + EPS)
        gh = dh * g_ref[...]
        out_ref[...] = ds1_ref[...] + r * gh - s0v * ((r * r * r) * _rowmean(gh * s0v))
        vec_ref[0:1, :] += _colsum(dh * (s0v * r))

    return pl.pallas_call(
        body, name=name, grid=(n_tiles,),
        out_shape=[jax.ShapeDtypeStruct(s0.shape, F32), jax.ShapeDtypeStruct((SUBLANES, D_MODEL), F32)],
        in_specs=[pl.BlockSpec((row_tile, D_IN), lambda t: (block0 + t, 0)),
                  pl.BlockSpec((row_tile, D_MODEL), lambda t: (block0 + t, 0)),
                  pl.BlockSpec((row_tile, D_MODEL), lambda t: (t, 0)),
                  _VMEM, _VMEM],
        out_specs=[pl.BlockSpec((row_tile, D_MODEL), lambda t: (t, 0)), _VMEM],
        compiler_params=pltpu.CompilerParams(dimension_semantics=("arbitrary",), vmem_limit_bytes=VMEM_LIMIT),
    )(dproj, ds1, s0, norm_g, w_in_all)


def _grad_w_in(h, dproj, k_tile):
    tp = h.shape[0]

    def body(h_ref, dp_ref, o_ref):
        @pl.when(pl.program_id(1) == 0)
        def _():
            o_ref[...] = jnp.zeros(o_ref.shape, F32)

        o_ref[...] += _dot_tn(h_ref[...], dp_ref[...])

    return pl.pallas_call(
        body, name="grad_w_in", grid=(N_DEV, tp // k_tile),
        out_shape=jax.ShapeDtypeStruct((N_DEV, D_MODEL, COLS), F32),
        in_specs=[pl.BlockSpec((k_tile, D_MODEL), lambda j, k: (k, 0)),
                  pl.BlockSpec((k_tile, COLS), lambda j, k: (k, j))],
        out_specs=pl.BlockSpec((None, D_MODEL, COLS), lambda j, k: (j, 0, 0)),
        compiler_params=pltpu.CompilerParams(dimension_semantics=("arbitrary", "arbitrary"),
                                             vmem_limit_bytes=VMEM_LIMIT),
    )(h, dproj)


def _grad_w_out(lhs, rhs, k_tile):
    tp = lhs.shape[1]

    def body(a_ref, b_ref, o_ref):
        @pl.when(pl.program_id(1) == 0)
        def _():
            o_ref[...] = jnp.zeros(o_ref.shape, F32)

        o_ref[...] += _dot_tn(a_ref[...], b_ref[...]).reshape(N_DEV, ROWS_OUT, D_MODEL)

    return pl.pallas_call(
        body, name="grad_w_out", grid=(3, tp // k_tile),
        out_shape=jax.ShapeDtypeStruct((N_DEV, 3, ROWS_OUT, D_MODEL), F32),
        in_specs=[pl.BlockSpec((None, k_tile, D_MODEL), lambda w, k: (w, k, 0)),
                  pl.BlockSpec((None, k_tile, D_MODEL), lambda w, k: (w, k, 0))],
        out_specs=pl.BlockSpec((N_DEV, None, ROWS_OUT, D_MODEL), lambda w, k: (0, w, 0, 0)),
        compiler_params=pltpu.CompilerParams(dimension_semantics=("arbitrary", "arbitrary"),
                                             vmem_limit_bytes=VMEM_LIMIT),
    )(lhs, rhs)


def _adamw_math(w, g, m, v):
    m = ADAM_B1 * m + (1.0 - ADAM_B1) * g
    v = ADAM_B2 * v + (1.0 - ADAM_B2) * (g * g)
    m_hat = m / (1.0 - ADAM_B1 ** ADAM_STEP)
    v_hat = v / (1.0 - ADAM_B2 ** ADAM_STEP)
    delta = -ADAM_LR * (m_hat / (jnp.sqrt(v_hat) + ADAM_EPS) + ADAM_WD * w)
    return delta, m, v


def _adamw_sharded(own, landed, w, m, v, row_tile, block0, name):
    rows, n = w.shape

    def body(own_ref, land_ref, w_ref, m_ref, v_ref, g_out, d_out, m_out, v_out):
        g = own_ref[...]
        for k in range(3):
            g = g + land_ref[k].astype(F32)
        delta, m_new, v_new = _adamw_math(w_ref[...], g, m_ref[...], v_ref[...])
        g_out[...] = g
        d_out[...] = delta
        m_out[...] = m_new
        v_out[...] = v_new

    tile = pl.BlockSpec((row_tile, n), lambda t: (t, 0))
    return pl.pallas_call(
        body, name=name, grid=(rows // row_tile,),
        out_shape=[jax.ShapeDtypeStruct((rows, n), F32)] * 4,
        in_specs=[pl.BlockSpec((None, row_tile, n), lambda t: (0, block0 + t, 0)),
                  pl.BlockSpec((3, row_tile, n), lambda t: (0, block0 + t, 0)),
                  tile, tile, tile],
        out_specs=[tile] * 4,
        compiler_params=pltpu.CompilerParams(dimension_semantics=("arbitrary",)),
    )(own, landed, w, m, v)


def _adamw_small(gathered, gathered_cols, params):
    n_par = len(params)

    def body(*refs):
        g_ref, gc_ref = refs[0], refs[1]
        ins = refs[2:2 + 3 * n_par]
        outs = refs[2 + 3 * n_par:]
        loss_ref = outs[4 * n_par]

        def reduced(ref, row, n_rows):
            g = ref[0, row:row + n_rows, :]
            for d in range(1, N_DEV):
                g = g + ref[d, row:row + n_rows, :]
            return g

        for p, (row, n_rows, sharded, _, _, _) in enumerate(params):
            g = reduced(gc_ref if sharded else g_ref, row, n_rows)
            if row == ROW_NORM_G_X:
                g = g + reduced(g_ref, ROW_NORM_G_META, n_rows)
            w_ref, m_ref, v_ref = ins[3 * p:3 * p + 3]
            delta, m_new, v_new = _adamw_math(w_ref[...], g, m_ref[...], v_ref[...])
            outs[4 * p][...] = g
            outs[4 * p + 1][...] = delta
            outs[4 * p + 2][...] = m_new
            outs[4 * p + 3][...] = v_new
        loss = jnp.sum(reduced(g_ref, ROW_LOSS, 1), axis=1, keepdims=True)
        loss_ref[...] = jnp.broadcast_to(loss, loss_ref.shape)

    out_shape = []
    for (_, _, _, w, _, _) in params:
        out_shape += [jax.ShapeDtypeStruct(w.shape, F32)] * 4
    out_shape.append(jax.ShapeDtypeStruct((1, LANES), F32))
    flat = [a for (_, _, _, w, m, v) in params for a in (w, m, v)]
    return pl.pallas_call(
        body, name="adamw_small", out_shape=out_shape,
        in_specs=[_VMEM] * (2 + len(flat)), out_specs=[_VMEM] * len(out_shape),
    )(gathered, gathered_cols, *flat)


def _pad_rows(a, rows):
    return jnp.concatenate([a, jnp.zeros((rows - a.shape[0], a.shape[1]), a.dtype)], axis=0)


def kernel(x, meta_tokens, norm_g, w_in, conv_a_w, conv_a_b, ln_a_g, ln_a_b, w_a_out, b_a_out, conv_b_w, w_b_out, w_out, final_g, loss_target, m_meta_tokens, m_norm_g, m_w_in, m_conv_a_w, m_conv_a_b, m_ln_a_g, m_ln_a_b, m_w_a_out, m_b_a_out, m_conv_b_w, m_w_b_out, m_w_out, m_final_g, v_meta_tokens, v_norm_g, v_w_in, v_conv_a_w, v_conv_a_b, v_ln_a_g, v_ln_a_b, v_w_a_out, v_b_a_out, v_conv_b_w, v_w_b_out, v_w_out, v_final_g):
    seq = x.shape[1]
    assert x.shape == (1, seq, D_MODEL) and seq % TILE == 0 and w_in.shape == (1, D_MODEL, COLS)
    n_tiles = seq // TILE + 1
    tp = n_tiles * TILE
    pos = jnp.stack([lax.axis_index("x"), lax.axis_index("y"), lax.axis_index("c")]).astype(jnp.int32)
    me = 4 * pos[0] + 2 * pos[1] + pos[2]
    x2d = x[0]
    tgt2d = loss_target[0]

    w_in_all = _all_gather(w_in[0].astype(BF16), "gather_w_in")
    w_out_all = _all_gather(
        jnp.concatenate([w_a_out[0], w_b_out[0], w_out[0]], axis=0).astype(BF16), "gather_w_out")
    w_out_all = w_out_all.reshape(N_DEV, 3, ROWS_OUT, D_MODEL).transpose(1, 0, 2, 3).reshape(3, D_MODEL, D_MODEL)
    small = jnp.concatenate([meta_tokens, _pad_rows(conv_a_w[0], 32), _pad_rows(conv_b_w[0], SUBLANES)], axis=0)
    small_all = _all_gather(small, "gather_small")
    small_all = small_all.transpose(1, 0, 2).reshape(small.shape[0], D_MODEL)
    meta_full, conv_a_full, conv_b_full = small_all[0:N_META], small_all[N_META:N_META + 32], small_all[N_META + 32:]
    meta_tile = jnp.concatenate([jnp.zeros((TILE - N_META, D_MODEL), F32), meta_full], axis=0)
    final_g2 = final_g.reshape(1, D_MODEL)

    h = _rmsnorm_fwd(x2d, meta_tile, norm_g, n_tiles)
    proj = _proj_fwd(h, w_in_all, tp // 3)
    dproj, ds1, lhs, rhs, vec, d_conv_a, d_conv_b = _fused_pass(
        proj, x2d, tgt2d, meta_tile, conv_a_full, conv_a_b, ln_a_g, ln_a_b, b_a_out, conv_b_full, final_g2,
        w_out_all[0], w_out_all[1], w_out_all[2], n_tiles)
    x_tile = min(256, seq)
    grad_x, vec_x = _input_bwd(dproj, ds1, x2d, norm_g, w_in_all, x_tile, seq // x_tile, 0, "input_bwd_x")
    d_meta_tile, vec_meta = _input_bwd(dproj, ds1, meta_tile, norm_g, w_in_all, TILE, 1, n_tiles - 1,
                                       "input_bwd_meta")
    gw_in = _grad_w_in(h, dproj, tp // 3)
    gw_out = _grad_w_out(lhs, rhs, tp // 3).reshape(N_DEV, 3 * ROWS_OUT, D_MODEL)

    (own_in, land_in), (own_out, land_out) = _reduce_scatter(pos, [gw_in, gw_out], [256, ROWS_OUT], "w")
    small_g = jnp.concatenate([vec, vec_x, vec_meta, d_conv_a, d_conv_b, d_meta_tile[TILE - N_META:]], axis=0)
    small_g_all = _all_gather(small_g, "gather_small_grads")
    small_g_cols = lax.dynamic_slice_in_dim(small_g_all, me * LANES, LANES, axis=2)

    res_in = _adamw_sharded(own_in, land_in, w_in[0], m_w_in[0], v_w_in[0], 128, 0, "adamw_w_in")
    res_out = [
        _adamw_sharded(own_out, land_out, w[0], m[0], v[0], ROWS_OUT, k, f"adamw_w_out{k}")
        for k, (w, m, v) in enumerate([(w_a_out, m_w_a_out, v_w_a_out), (w_b_out, m_w_b_out, v_w_b_out),
                                       (w_out, m_w_out, v_w_out)])]
    params = [
        (ROW_META, N_META, True, meta_tokens, m_meta_tokens, v_meta_tokens),
        (ROW_NORM_G_X, 1, False, norm_g, m_norm_g, v_norm_g),
        (ROW_CONV_A_W, CONV_A, True, conv_a_w[0], m_conv_a_w[0], v_conv_a_w[0]),
        (ROW_CONV_A_B, 1, False, conv_a_b, m_conv_a_b, v_conv_a_b),
        (ROW_LN_G, 1, False, ln_a_g, m_ln_a_g, v_ln_a_g),
        (ROW_LN_B, 1, False, ln_a_b, m_ln_a_b, v_ln_a_b),
        (ROW_B_A_OUT, 1, False, b_a_out, m_b_a_out, v_b_a_out),
        (ROW_CONV_B_W, CONV_B, True, conv_b_w[0], m_conv_b_w[0], v_conv_b_w[0]),
        (ROW_FINAL_G, 1, False, final_g2, m_final_g.reshape(1, D_MODEL), v_final_g.reshape(1, D_MODEL)),
    ]
    res_small = _adamw_small(small_g_all, small_g_cols, params)
    loss = res_small[-1][0, 0]

    def small_res(p, kind, shape):
        return res_small[4 * p + kind].reshape(shape)

    per_weight = []
    for kind in range(4):
        per_weight.append([
            small_res(0, kind, meta_tokens.shape),
            small_res(1, kind, norm_g.shape),
            res_in[kind].reshape(w_in.shape),
            small_res(2, kind, conv_a_w.shape),
            small_res(3, kind, conv_a_b.shape),
            small_res(4, kind, ln_a_g.shape),
            small_res(5, kind, ln_a_b.shape),
            res_out[0][kind].reshape(w_a_out.shape),
            small_res(6, kind, b_a_out.shape),
            small_res(7, kind, conv_b_w.shape),
            res_out[1][kind].reshape(w_b_out.shape),
            res_out[2][kind].reshape(w_out.shape),
            small_res(8, kind, final_g.shape),
        ])
    return (loss, grad_x.reshape(x.shape), *per_weight[0], *per_weight[1], *per_weight[2], *per_weight[3])
```

```python
import functools

import jax
import jax.numpy as jnp
from jax import lax
from jax.experimental import pallas as pl
from jax.experimental.pallas import tpu as pltpu

D_MODEL = 1024
N_META = 16
N_DEV = 8
D_IN = 9 * D_MODEL
COLS = D_IN // N_DEV
ROWS_OUT = D_MODEL // N_DEV
CONV_A = 31
CONV_B = 3
EPS = 1e-6

ADAM_LR = 0.001
ADAM_B1 = 0.9
ADAM_B2 = 0.999
ADAM_EPS = 1e-08
ADAM_WD = 0.01
ADAM_STEP = 10

TILE = 128
LANES = 128
N_CHUNK = D_MODEL // LANES
ROW_CHUNK = 32
HALO = 32
SUBLANES = 8
VMEM_LIMIT = 56 * 1024 * 1024

ROW_FINAL_G, ROW_B_A_OUT, ROW_LN_G, ROW_LN_B, ROW_CONV_A_B, ROW_LOSS = 0, 1, 2, 3, 4, 5
ROW_NORM_G_X, ROW_NORM_G_META = 8, 16
ROW_CONV_A_W, ROW_CONV_B_W, ROW_META = 24, 56, 64
SMALL_ROWS = 80

MESH = pl.DeviceIdType.MESH
_ANY = pl.BlockSpec(memory_space=pl.ANY)
_VMEM = pl.BlockSpec(memory_space=pltpu.VMEM)


def _resident(shape):
    return pl.BlockSpec(shape, lambda *_: (0,) * len(shape), pipeline_mode=pl.Buffered(1))
BF16 = jnp.bfloat16
F32 = jnp.float32


def _sigmoid(v):
    return jax.nn.sigmoid(v)


def _dot(a, b):
    return jnp.dot(a, b, preferred_element_type=F32)


def _dot_nt(a, b):
    return lax.dot_general(a, b, (((1,), (1,)), ((), ())), preferred_element_type=F32)


def _dot_tn(a, b):
    return lax.dot_general(a, b, (((0,), (0,)), ((), ())), preferred_element_type=F32)


def _colsum(v):
    return jnp.sum(v, axis=0, keepdims=True)


def _rowmean(v):
    parts = [v[:, LANES * c:LANES * (c + 1)] for c in range(v.shape[1] // LANES)]
    return jnp.sum(functools.reduce(jnp.add, parts), axis=-1, keepdims=True) * (1.0 / v.shape[1])


def _fold8(v):
    parts = [v[SUBLANES * g:SUBLANES * (g + 1)] for g in range(v.shape[0] // SUBLANES)]
    return functools.reduce(jnp.add, parts)


def _all_gather(shard, name):
    m, n = shard.shape

    def body(x_ref, out_ref, send_sems, recv_sems, local_sem):
        x, y, c = lax.axis_index("x"), lax.axis_index("y"), lax.axis_index("c")
        me, sibling = (x, y, c), (x, y, 1 - c)
        chips = [(1 - x, y), (x, 1 - y), (1 - x, 1 - y)]

        def slot(px, py, pc):
            return out_ref.at[4 * px + 2 * py + pc]

        def copy(k, block, to, src=None):
            return pltpu.make_async_remote_copy(
                src_ref=slot(*block) if src is None else src, dst_ref=slot(*block),
                send_sem=send_sems.at[k], recv_sem=recv_sems.at[k], device_id=to, device_id_type=MESH)

        mine = pltpu.make_async_copy(x_ref, slot(*me), local_sem)
        mine.start()
        first = [copy(0, me, sibling, src=x_ref)]
        first += [copy(1 + j, me, (*chip, c), src=x_ref) for j, chip in enumerate(chips)]
        for cp in first:
            cp.start()
        passed = [copy(4 + j, (*chip, c), sibling) for j, chip in enumerate(chips)]
        for j, chip in enumerate(chips):
            copy(1 + j, (*chip, c), me).wait_recv()
            passed[j].start()
        copy(0, sibling, me).wait_recv()
        for j, chip in enumerate(chips):
            copy(4 + j, (*chip, 1 - c), me).wait_recv()
        for cp in first + passed:
            cp.wait_send()
        mine.wait()

    return pl.pallas_call(
        body, name=name,
        out_shape=jax.ShapeDtypeStruct((N_DEV, m, n), shard.dtype),
        in_specs=[_ANY], out_specs=_ANY,
        scratch_shapes=[pltpu.SemaphoreType.DMA((7,)), pltpu.SemaphoreType.DMA((7,)), pltpu.SemaphoreType.DMA(())],
    )(shard)


def _sibling_copies(srcs, dsts, send_sems, recv_sems):
    x, y, c = lax.axis_index("x"), lax.axis_index("y"), lax.axis_index("c")
    return [pltpu.make_async_remote_copy(
        src_ref=src.at[2 * q + (1 - c)], dst_ref=dst.at[q],
        send_sem=send_sems.at[4 * a + q], recv_sem=recv_sems.at[4 * a + q],
        device_id=(x, y, 1 - c), device_id_type=MESH)
        for a, (src, dst) in enumerate(zip(srcs, dsts)) for q in range(4)]


def _chip_copies(srcs, dsts, send_sems, recv_sems):
    x, y, c = lax.axis_index("x"), lax.axis_index("y"), lax.axis_index("c")
    targets = [(x, 1 - y, c), (1 - x, y, c), (1 - x, 1 - y, c)]
    return [pltpu.make_async_remote_copy(
        src_ref=src.at[k], dst_ref=dst.at[k],
        send_sem=send_sems.at[3 * a + k], recv_sem=recv_sems.at[3 * a + k],
        device_id=targets[k], device_id_type=MESH)
        for a, (src, dst) in enumerate(zip(srcs, dsts)) for k in range(3)]


_EXCHANGES = {"sibling": (4, _sibling_copies, 4), "chips": (3, _chip_copies, 3)}


def _exchange_shapes(kind, arrays):
    per_array, _, slots = _EXCHANGES[kind]
    out_shape = [jax.ShapeDtypeStruct((slots,) + a.shape[1:], a.dtype) for a in arrays]
    sems = [pltpu.SemaphoreType.DMA((per_array * len(arrays),))] * 2
    return out_shape, sems


def _exchange(kind, arrays, name):
    n_arr = len(arrays)
    out_shape, sems = _exchange_shapes(kind, arrays)

    def body(*refs):
        copies = _EXCHANGES[kind][1](refs[:n_arr], refs[n_arr:2 * n_arr], *refs[2 * n_arr:])
        for cp in copies:
            cp.start()
        for cp in copies:
            cp.wait()

    return pl.pallas_call(body, name=name, out_shape=out_shape, in_specs=[_ANY] * n_arr,
                          out_specs=[_ANY] * n_arr, scratch_shapes=sems)(*arrays)


def _riding(body, n_in, n_out, kind, n_arr, is_first, is_last):
    def wrapped(*refs):
        ins, srcs = refs[:n_in], refs[n_in:n_in + n_arr]
        outs = refs[n_in + n_arr:n_in + n_arr + n_out]
        dsts = refs[n_in + n_arr + n_out:n_in + 2 * n_arr + n_out]
        scratch, sems = refs[n_in + 2 * n_arr + n_out:-2], refs[-2:]

        @pl.when(is_first())
        def _():
            for cp in _EXCHANGES[kind][1](srcs, dsts, *sems):
                cp.start()

        body(*ins, *outs, *scratch)

        @pl.when(is_last())
        def _():
            for cp in _EXCHANGES[kind][1](srcs, dsts, *sems):
                cp.wait()

    return wrapped


def _chip_partial(pos, mine, theirs, relations, out_dtype, row_tile, name):
    _, m, n = mine.shape
    q0 = relations[0]

    def chip_of(qi, pos_ref):
        q = qi + q0
        return pos_ref[0] ^ (q >> 1), pos_ref[1] ^ (q & 1)

    def mine_map(qi, t, pos_ref):
        px, py = chip_of(qi, pos_ref)
        return 4 * px + 2 * py + pos_ref[2], t, 0

    def theirs_map(qi, t, pos_ref):
        px, py = chip_of(qi, pos_ref)
        return 2 * px + py, t, 0

    def body(pos_ref, a_ref, b_ref, o_ref):
        o_ref[...] = (a_ref[...] + b_ref[...]).astype(out_dtype)

    return pl.pallas_call(
        body, name=name,
        out_shape=jax.ShapeDtypeStruct((len(relations), m, n), out_dtype),
        grid_spec=pltpu.PrefetchScalarGridSpec(
            num_scalar_prefetch=1, grid=(len(relations), m // row_tile),
            in_specs=[pl.BlockSpec((None, row_tile, n), mine_map), pl.BlockSpec((None, row_tile, n), theirs_map)],
            out_specs=pl.BlockSpec((None, row_tile, n), lambda qi, t, pos_ref: (qi, t, 0))),
        compiler_params=pltpu.CompilerParams(dimension_semantics=("arbitrary", "arbitrary")),
    )(pos, mine, theirs)


def _rmsnorm_fwd(x2d, meta_tile, norm_g, n_tiles):
    def body(x_ref, meta_ref, g_ref, h_ref):
        i = pl.program_id(0)
        s0 = jnp.where(i == n_tiles - 1, meta_ref[...], x_ref[...])
        r = lax.rsqrt(_rowmean(s0 * s0) + EPS)
        h_ref[...] = ((s0 * r) * g_ref[...]).astype(BF16)

    return pl.pallas_call(
        body, name="rmsnorm_fwd", grid=(n_tiles,),
        out_shape=jax.ShapeDtypeStruct((n_tiles * TILE, D_MODEL), BF16),
        in_specs=[pl.BlockSpec((TILE, D_MODEL), lambda i: (jnp.minimum(i, n_tiles - 2), 0)), _VMEM, _VMEM],
        out_specs=pl.BlockSpec((TILE, D_MODEL), lambda i: (i, 0)),
        compiler_params=pltpu.CompilerParams(dimension_semantics=("arbitrary",)),
    )(x2d, meta_tile, norm_g)


def _gather_proj(pos, h, w_in_shard, w_out_shard, row_tile):
    tp = h.shape[0]
    n_row = tp // row_tile

    def device_at(s, pos_ref):
        far = s >= 2
        j = jnp.where(far, (s - 2) % 3, 0)
        fx = jnp.where(far & (j != 1), 1, 0)
        fy = jnp.where(far & (j != 0), 1, 0)
        fc = jnp.where(far, (s - 2) // 3, s)
        return 4 * (pos_ref[0] ^ fx) + 2 * (pos_ref[1] ^ fy) + (pos_ref[2] ^ fc)

    def body(pos_ref, h_ref, win_ref, wout_ref, proj_ref, win_all, wout_all, wbuf, send_sems, recv_sems, local_sems):
        s, t = pl.program_id(0), pl.program_id(1)
        x, y, c = lax.axis_index("x"), lax.axis_index("y"), lax.axis_index("c")
        me, sibling = (x, y, c), (x, y, 1 - c)
        chips = [(1 - x, y), (x, 1 - y), (1 - x, 1 - y)]
        shards, gathered = (win_ref, wout_ref), (win_all, wout_all)

        def slot(a, px, py, pc):
            return gathered[a].at[4 * px + 2 * py + pc]

        def copy(a, k, block, to, from_shard=False):
            return pltpu.make_async_remote_copy(
                src_ref=shards[a] if from_shard else slot(a, *block), dst_ref=slot(a, *block),
                send_sem=send_sems.at[7 * a + k], recv_sem=recv_sems.at[7 * a + k], device_id=to, device_id_type=MESH)

        def keep(a):
            return pltpu.make_async_copy(shards[a], slot(a, *me), local_sems.at[a])

        def load(src, buf):
            cp = pltpu.make_async_copy(src, wbuf.at[buf], local_sems.at[2 + buf])
            cp.start()
            cp.wait()

        order = [me, sibling] + [(*chip, c) for chip in chips] + [(*chip, 1 - c) for chip in chips]
        for step, block in enumerate(order):
            @pl.when((s == step) & (t == 0))
            def _(step=step, block=block):
                if step == 0:
                    for a in range(2):
                        keep(a).start()
                        copy(a, 0, me, sibling, from_shard=True).start()
                        for j, chip in enumerate(chips):
                            copy(a, 1 + j, me, (*chip, c), from_shard=True).start()
                    load(win_ref, 0)
                    return
                if step == 1:
                    copy(0, 0, sibling, me).wait_recv()
                elif step <= 4:
                    copy(0, step - 1, block, me).wait_recv()
                    copy(0, step + 2, block, sibling).start()
                else:
                    copy(0, step - 1, block, me).wait_recv()
                    far = (*chips[step - 5], c)
                    copy(1, step - 4, far, me).wait_recv()
                    copy(1, step - 1, far, sibling).start()
                load(slot(0, *block), step % 2)

        proj_ref[...] = _dot(h_ref[...], wbuf[s % 2])

        @pl.when((s == N_DEV - 1) & (t == n_row - 1))
        def _():
            copy(1, 0, sibling, me).wait_recv()
            for j, chip in enumerate(chips):
                copy(1, 4 + j, (*chip, 1 - c), me).wait_recv()
            for a in range(2):
                copy(a, 0, me, sibling, from_shard=True).wait_send()
                for j, chip in enumerate(chips):
                    copy(a, 1 + j, me, (*chip, c), from_shard=True).wait_send()
                    copy(a, 4 + j, (*chip, c), sibling).wait_send()
                keep(a).wait()

    return pl.pallas_call(
        body, name="gather_proj",
        out_shape=[jax.ShapeDtypeStruct((tp, D_IN), F32),
                   jax.ShapeDtypeStruct((N_DEV,) + w_in_shard.shape, BF16),
                   jax.ShapeDtypeStruct((N_DEV,) + w_out_shard.shape, BF16)],
        grid_spec=pltpu.PrefetchScalarGridSpec(
            num_scalar_prefetch=1, grid=(N_DEV, n_row),
            in_specs=[pl.BlockSpec((row_tile, D_MODEL), lambda s, t, pos_ref: (t, 0)), _ANY, _ANY],
            out_specs=[pl.BlockSpec((row_tile, COLS), lambda s, t, pos_ref: (t, device_at(s, pos_ref))), _ANY, _ANY],
            scratch_shapes=[pltpu.VMEM((2, D_MODEL, COLS), BF16),
                            pltpu.SemaphoreType.DMA((14,)), pltpu.SemaphoreType.DMA((14,)),
                            pltpu.SemaphoreType.DMA((4,))]),
        compiler_params=pltpu.CompilerParams(dimension_semantics=("arbitrary", "arbitrary"),
                                             vmem_limit_bytes=VMEM_LIMIT),
    )(pos, h, w_in_shard, w_out_shard)


C_AVAL, C_AGLU, C_AZ, C_BB, C_BC, C_BX, C_BZ, C_GA, C_GB = (k * D_MODEL for k in range(9))
S_AZ, S_BB, S_BZ, S_GA, S_GB = (k * D_MODEL for k in range(5))


def _fused_pass(proj, x2d, tgt2d, meta_tile, conv_a_w, conv_a_b, ln_a_g, ln_a_b, b_a_out, conv_b_w, final_g,
                w_a, w_b, w_o, w_a_t, w_b_t, w_o_t, n_tiles):
    T = TILE
    tp = n_tiles * T
    inv_d = 1.0 / D_MODEL

    def block_of(tile):
        return jnp.where(tile == 0, n_tiles - 1, tile - 1)

    def cur(i):
        return block_of(jnp.minimum(i, n_tiles - 1))

    def prev(i):
        return block_of(jnp.clip(i - 1, 0, n_tiles - 1))

    def xblk(i):
        return jnp.maximum(jnp.minimum(i, n_tiles - 1) - 1, 0)

    def body(proj_ref, x_ref, tgt_ref, meta_ref, caw_ref, cab_ref, lng_ref, lnb_ref, bao_ref, cbw_ref, fg_ref,
             wa_ref, wb_ref, wo_ref, wat_ref, wbt_ref, wot_ref,
             dproj_ref, ds1_ref, lhs_ref, rhs_ref, vec_ref, dcaw_ref, dcbw_ref,
             ua0_buf, cb_buf, dua1_buf, dc3_buf, aprev, cprev, stage, ua1_buf, c3_buf, xhat_buf, rstd_buf,
             dpa_buf, dpb_buf, dcaw8, dcbw8, shift_buf):
        i = pl.program_id(0)

        @pl.when(i == 0)
        def _init():
            for buf in (ua0_buf, cb_buf, dua1_buf, dc3_buf, aprev, cprev, dcaw8, dcbw8):
                buf[...] = jnp.zeros(buf.shape, buf.dtype)
            vec_ref[...] = jnp.zeros(vec_ref.shape, F32)

        @pl.when(i >= 1)
        def _emit_stage():
            dproj_ref[:, C_AZ:C_BC] = stage[:, S_AZ:S_BZ]
            dproj_ref[:, C_BZ:D_IN] = stage[:, S_BZ:S_GB + D_MODEL]

        @pl.when(i < n_tiles)
        def _front():
            def conv_chunk(cc, carry):
                c0 = pl.multiple_of(cc * LANES, LANES)
                lanes = pl.ds(c0, LANES)

                def col(base):
                    return pl.ds(pl.multiple_of(base + cc * LANES, LANES), LANES)

                ua0 = proj_ref[:, col(C_AVAL)] * _sigmoid(proj_ref[:, col(C_AGLU)])
                ua0_buf[T:2 * T, lanes] = ua0
                acc = jnp.broadcast_to(cab_ref[:, lanes], (T, LANES))
                lead = HALO - (CONV_A - 1)
                for r in range(SUBLANES):
                    taps = [k for k in range(CONV_A) if (k + lead) % SUBLANES == r]
                    rows = T + SUBLANES * max((k + lead) // SUBLANES for k in taps)
                    if r:
                        shift_buf[r, 0:rows, :] = ua0_buf[pl.ds(T - HALO + r, rows), lanes]
                    for k in taps:
                        q = (k + lead) // SUBLANES
                        if r:
                            win = shift_buf[r, SUBLANES * q:SUBLANES * q + T, :]
                        else:
                            win = ua0_buf[pl.ds(T - HALO + SUBLANES * q, T), lanes]
                        acc = acc + caw_ref[k:k + 1, lanes] * win
                ua1_buf[:, lanes] = acc
                cb = proj_ref[:, col(C_BC)] * proj_ref[:, col(C_BX)]
                cb_buf[T:2 * T, lanes] = cb
                acc3 = cbw_ref[0:1, lanes] * cb_buf[pl.ds(T - 2, T), lanes]
                for k in range(1, CONV_B):
                    acc3 = acc3 + cbw_ref[k:k + 1, lanes] * cb_buf[pl.ds(T - (CONV_B - 1) + k, T), lanes]
                c3_buf[:, lanes] = acc3
                return carry

            lax.fori_loop(0, N_CHUNK, conv_chunk, 0)

            def gate_rows(rc, carry):
                rows = pl.ds(pl.multiple_of(rc * ROW_CHUNK, ROW_CHUNK), ROW_CHUNK)
                ua1 = ua1_buf[rows, :]
                xc = ua1 - _rowmean(ua1)
                rstd = lax.rsqrt(_rowmean(xc * xc) + EPS)
                xhat = xc * rstd
                xhat_buf[rows, :] = xhat
                rstd_buf[rows, :] = rstd
                ua2 = xhat * lng_ref[...] + lnb_ref[...]
                ua3 = ua2 * _sigmoid(ua2)
                a_z = proj_ref[rows, C_AZ:C_AZ + D_MODEL]
                lhs_ref[0, rows, :] = (ua3 * (a_z * _sigmoid(a_z))).astype(BF16)
                b_z = proj_ref[rows, C_BZ:C_BZ + D_MODEL]
                ub = proj_ref[rows, C_BB:C_BB + D_MODEL] * c3_buf[rows, :]
                lhs_ref[1, rows, :] = (ub * (b_z * _sigmoid(b_z))).astype(BF16)
                return carry

            lax.fori_loop(0, T // ROW_CHUNK, gate_rows, 0)

            ya = _dot(lhs_ref[0], wa_ref[...]) + bao_ref[...]
            yb = _dot(lhs_ref[1], wb_ref[...])
            sga = _sigmoid(proj_ref[:, C_GA:C_GA + D_MODEL])
            sgb = _sigmoid(proj_ref[:, C_GB:C_GB + D_MODEL])
            m_b = (sga * ya + sgb * yb).astype(BF16)
            lhs_ref[2] = m_b
            s0 = jnp.where(i == 0, meta_ref[...], x_ref[...])
            s1 = s0 + _dot(m_b, wo_ref[...])
            r1 = lax.rsqrt(_rowmean(s1 * s1) + EPS)
            y = (s1 * r1) * fg_ref[...]
            is_token = (i >= 1).astype(F32)
            err = (y - tgt_ref[...]) * is_token
            vec_ref[ROW_LOSS:ROW_LOSS + 1, :] += (0.5 * inv_d) * _colsum(err * err)
            dy = err * inv_d
            vec_ref[ROW_FINAL_G:ROW_FINAL_G + 1, :] += _colsum(dy * (s1 * r1))
            gy = dy * fg_ref[...]
            ds1 = r1 * gy - s1 * ((r1 * r1 * r1) * _rowmean(gy * s1))
            ds1_ref[...] = ds1
            ds1_b = ds1.astype(BF16)
            rhs_ref[2] = ds1_b
            dm = _dot(ds1_b, wot_ref[...])
            dya = dm * sga
            dyb = dm * sgb
            stage[:, S_GA:S_GA + D_MODEL] = (dm * ya * (sga * (1.0 - sga))).astype(BF16)
            stage[:, S_GB:S_GB + D_MODEL] = (dm * yb * (sgb * (1.0 - sgb))).astype(BF16)
            vec_ref[ROW_B_A_OUT:ROW_B_A_OUT + 1, :] += _colsum(dya)
            dya_b = dya.astype(BF16)
            dyb_b = dyb.astype(BF16)
            rhs_ref[0] = dya_b
            rhs_ref[1] = dyb_b
            dpa_buf[...] = _dot(dya_b, wat_ref[...])
            dpb_buf[...] = _dot(dyb_b, wbt_ref[...])

            def gate_rows_bwd(rc, carry):
                r0 = pl.multiple_of(rc * ROW_CHUNK, ROW_CHUNK)
                rows = pl.ds(r0, ROW_CHUNK)
                later = pl.ds(pl.multiple_of(T + rc * ROW_CHUNK, ROW_CHUNK), ROW_CHUNK)
                xhat = xhat_buf[rows, :]
                ua2 = xhat * lng_ref[...] + lnb_ref[...]
                sg2 = _sigmoid(ua2)
                ua3 = ua2 * sg2
                a_z = proj_ref[rows, C_AZ:C_AZ + D_MODEL]
                sz = _sigmoid(a_z)
                dpa = dpa_buf[rows, :]
                stage[rows, S_AZ:S_AZ + D_MODEL] = (dpa * ua3 * (sz * (1.0 + a_z * (1.0 - sz)))).astype(BF16)
                dua2 = dpa * (a_z * sz) * (sg2 * (1.0 + ua2 * (1.0 - sg2)))
                vec_ref[ROW_LN_G:ROW_LN_G + 1, :] += _colsum(dua2 * xhat)
                vec_ref[ROW_LN_B:ROW_LN_B + 1, :] += _colsum(dua2)
                dxh = dua2 * lng_ref[...]
                dua1 = rstd_buf[rows, :] * (dxh - _rowmean(dxh) - xhat * _rowmean(dxh * xhat))
                vec_ref[ROW_CONV_A_B:ROW_CONV_A_B + 1, :] += _colsum(dua1)
                dua1_buf[later, :] = dua1
                b_z = proj_ref[rows, C_BZ:C_BZ + D_MODEL]
                sbz = _sigmoid(b_z)
                b_b = proj_ref[rows, C_BB:C_BB + D_MODEL]
                c3 = c3_buf[rows, :]
                dpb = dpb_buf[rows, :]
                stage[rows, S_BZ:S_BZ + D_MODEL] = (dpb * (b_b * c3) * (sbz * (1.0 + b_z * (1.0 - sbz)))).astype(BF16)
                dub = dpb * (b_z * sbz)
                stage[rows, S_BB:S_BB + D_MODEL] = (dub * c3).astype(BF16)
                dc3_buf[later, :] = dub * b_b
                return carry

            lax.fori_loop(0, T // ROW_CHUNK, gate_rows_bwd, 0)

        @pl.when(i == n_tiles)
        def _no_later_tile():
            dua1_buf[T:2 * T, :] = jnp.zeros((T, D_MODEL), F32)
            dc3_buf[T:2 * T, :] = jnp.zeros((T, D_MODEL), F32)

        @pl.when(i >= 1)
        def _lagged():
            def convt_chunk(cc, carry):
                c0 = pl.multiple_of(cc * LANES, LANES)
                lanes = pl.ds(c0, LANES)

                def col(base):
                    return pl.ds(pl.multiple_of(base + cc * LANES, LANES), LANES)

                ua0 = ua0_buf[0:T, lanes]
                acc = jnp.zeros((T, LANES), F32)
                for r in range(SUBLANES):
                    shifts = [j for j in range(CONV_A) if j % SUBLANES == r]
                    rows = T + shifts[-1] - r
                    if r:
                        shift_buf[r, 0:rows, :] = dua1_buf[pl.ds(r, rows), lanes]
                    for j in shifts:
                        k = CONV_A - 1 - j
                        if r:
                            later = shift_buf[r, j - r:j - r + T, :]
                        else:
                            later = dua1_buf[pl.ds(j, T), lanes]
                        acc = acc + caw_ref[k:k + 1, lanes] * later
                        dcaw8[SUBLANES * k:SUBLANES * (k + 1), lanes] += _fold8(ua0 * later)
                a_val = aprev[:, col(0)]
                sg = _sigmoid(aprev[:, col(D_MODEL)])
                dproj_ref[:, col(C_AVAL)] = (acc * sg).astype(BF16)
                dproj_ref[:, col(C_AGLU)] = (acc * a_val * (sg * (1.0 - sg))).astype(BF16)

                cb = cb_buf[0:T, lanes]
                acc3 = jnp.zeros((T, LANES), F32)
                for j in range(CONV_B):
                    k = CONV_B - 1 - j
                    later = dc3_buf[pl.ds(j, T), lanes]
                    acc3 = acc3 + cbw_ref[k:k + 1, lanes] * later
                    dcbw8[SUBLANES * k:SUBLANES * (k + 1), lanes] += _fold8(cb * later)
                dproj_ref[:, col(C_BC)] = (acc3 * cprev[:, col(D_MODEL)]).astype(BF16)
                dproj_ref[:, col(C_BX)] = (acc3 * cprev[:, col(0)]).astype(BF16)
                return carry

            lax.fori_loop(0, N_CHUNK, convt_chunk, 0)

        for buf in (ua0_buf, cb_buf, dua1_buf, dc3_buf):
            buf[0:T, :] = buf[T:2 * T, :]
        aprev[...] = proj_ref[:, C_AVAL:C_AZ]
        cprev[...] = proj_ref[:, C_BC:C_BZ]

        @pl.when(i == n_tiles)
        def _finish():
            for k in range(CONV_A):
                dcaw_ref[k:k + 1, :] = _colsum(dcaw8[SUBLANES * k:SUBLANES * (k + 1), :])
            dcaw_ref[CONV_A:CONV_A + 1, :] = jnp.zeros((1, D_MODEL), F32)
            for k in range(CONV_B):
                dcbw_ref[k:k + 1, :] = _colsum(dcbw8[SUBLANES * k:SUBLANES * (k + 1), :])
            dcbw_ref[CONV_B:SUBLANES, :] = jnp.zeros((SUBLANES - CONV_B, D_MODEL), F32)

    tile_in = lambda width: pl.BlockSpec((T, width), lambda i: (cur(i), 0))
    return pl.pallas_call(
        body, name="fused_pass", grid=(n_tiles + 1,),
        out_shape=[
            jax.ShapeDtypeStruct((tp, D_IN), BF16),
            jax.ShapeDtypeStruct((tp, D_MODEL), F32),
            jax.ShapeDtypeStruct((3, tp, D_MODEL), BF16),
            jax.ShapeDtypeStruct((3, tp, D_MODEL), BF16),
            jax.ShapeDtypeStruct((SUBLANES, D_MODEL), F32),
            jax.ShapeDtypeStruct((32, D_MODEL), F32),
            jax.ShapeDtypeStruct((SUBLANES, D_MODEL), F32),
        ],
        in_specs=[
            tile_in(D_IN),
            pl.BlockSpec((T, D_MODEL), lambda i: (xblk(i), 0)),
            pl.BlockSpec((T, D_MODEL), lambda i: (xblk(i), 0)),
            _VMEM, _VMEM, _VMEM, _VMEM, _VMEM, _VMEM, _VMEM, _VMEM,
            *[_resident((D_MODEL, D_MODEL)) for _ in range(6)],
        ],
        out_specs=[
            pl.BlockSpec((T, D_IN), lambda i: (prev(i), 0)),
            pl.BlockSpec((T, D_MODEL), lambda i: (cur(i), 0)),
            pl.BlockSpec((3, T, D_MODEL), lambda i: (0, cur(i), 0)),
            pl.BlockSpec((3, T, D_MODEL), lambda i: (0, cur(i), 0)),
            _VMEM, _VMEM, _VMEM,
        ],
        scratch_shapes=[
            pltpu.VMEM((2 * T, D_MODEL), F32),
            pltpu.VMEM((2 * T, D_MODEL), F32),
            pltpu.VMEM((2 * T, D_MODEL), F32),
            pltpu.VMEM((2 * T, D_MODEL), F32),
            pltpu.VMEM((T, 2 * D_MODEL), F32),
            pltpu.VMEM((T, 2 * D_MODEL), F32),
            pltpu.VMEM((T, 5 * D_MODEL), BF16),
            pltpu.VMEM((T, D_MODEL), F32),
            pltpu.VMEM((T, D_MODEL), F32),
            pltpu.VMEM((T, D_MODEL), F32),
            pltpu.VMEM((T, 1), F32),
            pltpu.VMEM((T, D_MODEL), F32),
            pltpu.VMEM((T, D_MODEL), F32),
            pltpu.VMEM((32 * SUBLANES, D_MODEL), F32),
            pltpu.VMEM((SUBLANES * SUBLANES, D_MODEL), F32),
            pltpu.VMEM((SUBLANES, T + HALO, LANES), F32),
        ],
        compiler_params=pltpu.CompilerParams(dimension_semantics=("arbitrary",), vmem_limit_bytes=VMEM_LIMIT),
    )(proj, x2d, tgt2d, meta_tile, conv_a_w, conv_a_b, ln_a_g, ln_a_b, b_a_out, conv_b_w, final_g,
      w_a, w_b, w_o, w_a_t, w_b_t, w_o_t)


def _input_bwd(dproj, ds1, s0, norm_g, w_in_all, row_tile, n_tiles, block0, name, ride=()):
    def body(dp_ref, ds1_ref, s0_ref, g_ref, w_ref, out_ref, vec_ref):
        t = pl.program_id(0)

        @pl.when(t == 0)
        def _():
            vec_ref[...] = jnp.zeros(vec_ref.shape, F32)

        dh = _dot_nt(dp_ref[:, 0:COLS], w_ref[0])
        for j in range(1, N_DEV):
            dh = dh + _dot_nt(dp_ref[:, j * COLS:(j + 1) * COLS], w_ref[j])
        s0v = s0_ref[...]
        r = lax.rsqrt(_rowmean(s0v * s0v) + EPS)
        gh = dh * g_ref[...]
        out_ref[...] = ds1_ref[...] + r * gh - s0v * ((r * r * r) * _rowmean(gh * s0v))
        vec_ref[0:1, :] += _colsum(dh * (s0v * r))

    n_arr = len(ride)
    ride_shapes, ride_sems = _exchange_shapes("chips", ride) if ride else ([], [])
    if ride:
        body = _riding(body, 5, 2, "chips", n_arr, lambda: pl.program_id(0) == 0,
                       lambda: pl.program_id(0) == n_tiles - 1)
    return pl.pallas_call(
        body, name=name, grid=(n_tiles,),
        out_shape=[jax.ShapeDtypeStruct(s0.shape, F32), jax.ShapeDtypeStruct((SUBLANES, D_MODEL), F32)] + ride_shapes,
        in_specs=[pl.BlockSpec((row_tile, D_IN), lambda t: (block0 + t, 0)),
                  pl.BlockSpec((row_tile, D_MODEL), lambda t: (block0 + t, 0)),
                  pl.BlockSpec((row_tile, D_MODEL), lambda t: (t, 0)),
                  _VMEM, _resident((N_DEV, D_MODEL, COLS))] + [_ANY] * n_arr,
        out_specs=[pl.BlockSpec((row_tile, D_MODEL), lambda t: (t, 0)), _VMEM] + [_ANY] * n_arr,
        scratch_shapes=ride_sems,
        compiler_params=pltpu.CompilerParams(dimension_semantics=("arbitrary",), vmem_limit_bytes=VMEM_LIMIT),
    )(dproj, ds1, s0, norm_g, w_in_all, *ride)


def _grad_w_in(h, dproj, k_tile, ride=()):
    tp = h.shape[0]
    n_k = tp // k_tile

    def body(h_ref, dp_ref, o_ref):
        @pl.when(pl.program_id(1) == 0)
        def _():
            o_ref[...] = jnp.zeros(o_ref.shape, F32)

        o_ref[...] += _dot_tn(h_ref[...], dp_ref[...])

    n_arr = len(ride)
    ride_shapes, ride_sems = _exchange_shapes("sibling", ride) if ride else ([], [])
    if ride:
        body = _riding(body, 2, 1, "sibling", n_arr,
                       lambda: (pl.program_id(0) == 0) & (pl.program_id(1) == 0),
                       lambda: (pl.program_id(0) == N_DEV - 1) & (pl.program_id(1) == n_k - 1))
    return pl.pallas_call(
        body, name="grad_w_in", grid=(N_DEV, n_k),
        out_shape=[jax.ShapeDtypeStruct((N_DEV, D_MODEL, COLS), F32)] + ride_shapes,
        in_specs=[pl.BlockSpec((k_tile, D_MODEL), lambda j, k: (k, 0)),
                  pl.BlockSpec((k_tile, COLS), lambda j, k: (k, j))] + [_ANY] * n_arr,
        out_specs=[pl.BlockSpec((None, D_MODEL, COLS), lambda j, k: (j, 0, 0))] + [_ANY] * n_arr,
        scratch_shapes=ride_sems,
        compiler_params=pltpu.CompilerParams(dimension_semantics=("arbitrary", "arbitrary"),
                                             vmem_limit_bytes=VMEM_LIMIT),
    )(h, dproj, *ride)


def _grad_w_out(lhs, rhs, k_tile):
    tp = lhs.shape[1]

    def body(a_ref, b_ref, o_ref):
        @pl.when(pl.program_id(1) == 0)
        def _():
            o_ref[...] = jnp.zeros(o_ref.shape, F32)

        o_ref[...] += _dot_tn(a_ref[...], b_ref[...]).reshape(N_DEV, ROWS_OUT, D_MODEL)

    return pl.pallas_call(
        body, name="grad_w_out", grid=(3, tp // k_tile),
        out_shape=jax.ShapeDtypeStruct((N_DEV, 3, ROWS_OUT, D_MODEL), F32),
        in_specs=[pl.BlockSpec((None, k_tile, D_MODEL), lambda w, k: (w, k, 0)),
                  pl.BlockSpec((None, k_tile, D_MODEL), lambda w, k: (w, k, 0))],
        out_specs=pl.BlockSpec((N_DEV, None, ROWS_OUT, D_MODEL), lambda w, k: (0, w, 0, 0)),
        compiler_params=pltpu.CompilerParams(dimension_semantics=("arbitrary", "arbitrary"),
                                             vmem_limit_bytes=VMEM_LIMIT),
    )(lhs, rhs)


def _adamw_math(w, g, m, v):
    m = ADAM_B1 * m + (1.0 - ADAM_B1) * g
    v = ADAM_B2 * v + (1.0 - ADAM_B2) * (g * g)
    m_hat = m / (1.0 - ADAM_B1 ** ADAM_STEP)
    v_hat = v / (1.0 - ADAM_B2 ** ADAM_STEP)
    delta = -ADAM_LR * (m_hat / (jnp.sqrt(v_hat) + ADAM_EPS) + ADAM_WD * w)
    return delta, m, v


def _adamw_sharded(own, landed, w, m, v, row_tile, block0, name):
    rows, n = w.shape

    def body(own_ref, land_ref, w_ref, m_ref, v_ref, g_out, d_out, m_out, v_out):
        g = own_ref[...]
        for k in range(3):
            g = g + land_ref[k].astype(F32)
        delta, m_new, v_new = _adamw_math(w_ref[...], g, m_ref[...], v_ref[...])
        g_out[...] = g
        d_out[...] = delta
        m_out[...] = m_new
        v_out[...] = v_new

    tile = pl.BlockSpec((row_tile, n), lambda t: (t, 0))
    return pl.pallas_call(
        body, name=name, grid=(rows // row_tile,),
        out_shape=[jax.ShapeDtypeStruct((rows, n), F32)] * 4,
        in_specs=[pl.BlockSpec((None, row_tile, n), lambda t: (0, block0 + t, 0)),
                  pl.BlockSpec((3, row_tile, n), lambda t: (0, block0 + t, 0)),
                  tile, tile, tile],
        out_specs=[tile] * 4,
        compiler_params=pltpu.CompilerParams(dimension_semantics=("arbitrary",)),
    )(own, landed, w, m, v)


def _adamw_small(gathered, gathered_cols, params):
    n_par = len(params)

    def body(*refs):
        g_ref, gc_ref = refs[0], refs[1]
        ins = refs[2:2 + 3 * n_par]
        outs = refs[2 + 3 * n_par:]
        loss_ref = outs[4 * n_par]

        def reduced(ref, row, n_rows):
            g = ref[0, row:row + n_rows, :]
            for d in range(1, N_DEV):
                g = g + ref[d, row:row + n_rows, :]
            return g

        for p, (row, n_rows, sharded, _, _, _) in enumerate(params):
            g = reduced(gc_ref if sharded else g_ref, row, n_rows)
            if row == ROW_NORM_G_X:
                g = g + reduced(g_ref, ROW_NORM_G_META, n_rows)
            w_ref, m_ref, v_ref = ins[3 * p:3 * p + 3]
            delta, m_new, v_new = _adamw_math(w_ref[...], g, m_ref[...], v_ref[...])
            outs[4 * p][...] = g
            outs[4 * p + 1][...] = delta
            outs[4 * p + 2][...] = m_new
            outs[4 * p + 3][...] = v_new
        loss = jnp.sum(reduced(g_ref, ROW_LOSS, 1), axis=1, keepdims=True)
        loss_ref[...] = jnp.broadcast_to(loss, loss_ref.shape)

    out_shape = []
    for (_, _, _, w, _, _) in params:
        out_shape += [jax.ShapeDtypeStruct(w.shape, F32)] * 4
    out_shape.append(jax.ShapeDtypeStruct((1, LANES), F32))
    flat = [a for (_, _, _, w, m, v) in params for a in (w, m, v)]
    return pl.pallas_call(
        body, name="adamw_small", out_shape=out_shape,
        in_specs=[_VMEM] * (2 + len(flat)), out_specs=[_VMEM] * len(out_shape),
    )(gathered, gathered_cols, *flat)


def _pad_rows(a, rows):
    return jnp.concatenate([a, jnp.zeros((rows - a.shape[0], a.shape[1]), a.dtype)], axis=0)


def kernel(x, meta_tokens, norm_g, w_in, conv_a_w, conv_a_b, ln_a_g, ln_a_b, w_a_out, b_a_out, conv_b_w, w_b_out, w_out, final_g, loss_target, m_meta_tokens, m_norm_g, m_w_in, m_conv_a_w, m_conv_a_b, m_ln_a_g, m_ln_a_b, m_w_a_out, m_b_a_out, m_conv_b_w, m_w_b_out, m_w_out, m_final_g, v_meta_tokens, v_norm_g, v_w_in, v_conv_a_w, v_conv_a_b, v_ln_a_g, v_ln_a_b, v_w_a_out, v_b_a_out, v_conv_b_w, v_w_b_out, v_w_out, v_final_g):
    seq = x.shape[1]
    assert x.shape == (1, seq, D_MODEL) and seq % TILE == 0 and w_in.shape == (1, D_MODEL, COLS)
    n_tiles = seq // TILE + 1
    tp = n_tiles * TILE
    pos = jnp.stack([lax.axis_index("x"), lax.axis_index("y"), lax.axis_index("c")]).astype(jnp.int32)
    me = 4 * pos[0] + 2 * pos[1] + pos[2]
    x2d = x[0]
    tgt2d = loss_target[0]

    small = jnp.concatenate([meta_tokens, _pad_rows(conv_a_w[0], 32), _pad_rows(conv_b_w[0], SUBLANES)], axis=0)
    small_all = _all_gather(small, "gather_small")
    small_all = small_all.transpose(1, 0, 2).reshape(small.shape[0], D_MODEL)
    meta_full, conv_a_full, conv_b_full = small_all[0:N_META], small_all[N_META:N_META + 32], small_all[N_META + 32:]
    meta_tile = jnp.concatenate([jnp.zeros((TILE - N_META, D_MODEL), F32), meta_full], axis=0)
    final_g2 = final_g.reshape(1, D_MODEL)

    h = _rmsnorm_fwd(x2d, meta_tile, norm_g, n_tiles)
    w_out_shard = jnp.concatenate([w_a_out[0], w_b_out[0], w_out[0]], axis=0).astype(BF16)
    proj, w_in_all, w_out_all = _gather_proj(pos, h, w_in[0].astype(BF16), w_out_shard, tp // 3)
    w_out_all = w_out_all.reshape(N_DEV, 3, ROWS_OUT, D_MODEL).transpose(1, 0, 2, 3).reshape(3, D_MODEL, D_MODEL)
    w_out_all_t = w_out_all.transpose(0, 2, 1)
    dproj, ds1, lhs, rhs, vec, d_conv_a, d_conv_b = _fused_pass(
        proj, x2d, tgt2d, meta_tile, conv_a_full, conv_a_b, ln_a_g, ln_a_b, b_a_out, conv_b_full, final_g2,
        w_out_all[0], w_out_all[1], w_out_all[2], w_out_all_t[0], w_out_all_t[1], w_out_all_t[2], n_tiles)
    k_tile = tp // 3
    gw_out = _grad_w_out(lhs, rhs, k_tile).reshape(N_DEV, 3 * ROWS_OUT, D_MODEL)
    gw_in, their_out = _grad_w_in(h, dproj, k_tile, ride=(gw_out,))
    (their_in,) = _exchange("sibling", [gw_in], "rs_sibling_w_in")
    own_in = _chip_partial(pos, gw_in, their_in, (0,), F32, 256, "rs_own_w_in")
    parts_in = _chip_partial(pos, gw_in, their_in, (1, 2, 3), BF16, 256, "rs_parts_w_in")
    own_out = _chip_partial(pos, gw_out, their_out, (0,), F32, ROWS_OUT, "rs_own_w_out")
    parts_out = _chip_partial(pos, gw_out, their_out, (1, 2, 3), BF16, ROWS_OUT, "rs_parts_w_out")
    x_tile = min(256, seq)
    grad_x, vec_x, land_in, land_out = _input_bwd(
        dproj, ds1, x2d, norm_g, w_in_all, x_tile, seq // x_tile, 0, "input_bwd_x", ride=(parts_in, parts_out))
    d_meta_tile, vec_meta = _input_bwd(dproj, ds1, meta_tile, norm_g, w_in_all, TILE, 1, n_tiles - 1,
                                       "input_bwd_meta")

    small_g = jnp.concatenate([vec, vec_x, vec_meta, d_conv_a, d_conv_b, d_meta_tile[TILE - N_META:]], axis=0)
    small_g_all = _all_gather(small_g, "gather_small_grads")
    small_g_cols = lax.dynamic_slice_in_dim(small_g_all, me * LANES, LANES, axis=2)

    res_in = _adamw_sharded(own_in, land_in, w_in[0], m_w_in[0], v_w_in[0], 128, 0, "adamw_w_in")
    res_out = [
        _adamw_sharded(own_out, land_out, w[0], m[0], v[0], ROWS_OUT, k, f"adamw_w_out{k}")
        for k, (w, m, v) in enumerate([(w_a_out, m_w_a_out, v_w_a_out), (w_b_out, m_w_b_out, v_w_b_out),
                                       (w_out, m_w_out, v_w_out)])]
    params = [
        (ROW_META, N_META, True, meta_tokens, m_meta_tokens, v_meta_tokens),
        (ROW_NORM_G_X, 1, False, norm_g, m_norm_g, v_norm_g),
        (ROW_CONV_A_W, CONV_A, True, conv_a_w[0], m_conv_a_w[0], v_conv_a_w[0]),
        (ROW_CONV_A_B, 1, False, conv_a_b, m_conv_a_b, v_conv_a_b),
        (ROW_LN_G, 1, False, ln_a_g, m_ln_a_g, v_ln_a_g),
        (ROW_LN_B, 1, False, ln_a_b, m_ln_a_b, v_ln_a_b),
        (ROW_B_A_OUT, 1, False, b_a_out, m_b_a_out, v_b_a_out),
        (ROW_CONV_B_W, CONV_B, True, conv_b_w[0], m_conv_b_w[0], v_conv_b_w[0]),
        (ROW_FINAL_G, 1, False, final_g2, m_final_g.reshape(1, D_MODEL), v_final_g.reshape(1, D_MODEL)),
    ]
    res_small = _adamw_small(small_g_all, small_g_cols, params)
    loss = res_small[-1][0, 0]

    def small_res(p, kind, shape):
        return res_small[4 * p + kind].reshape(shape)

    per_weight = []
    for kind in range(4):
        per_weight.append([
            small_res(0, kind, meta_tokens.shape),
            small_res(1, kind, norm_g.shape),
            res_in[kind].reshape(w_in.shape),
            small_res(2, kind, conv_a_w.shape),
            small_res(3, kind, conv_a_b.shape),
            small_res(4, kind, ln_a_g.shape),
            small_res(5, kind, ln_a_b.shape),
            res_out[0][kind].reshape(w_a_out.shape),
            small_res(6, kind, b_a_out.shape),
            small_res(7, kind, conv_b_w.shape),
            res_out[1][kind].reshape(w_b_out.shape),
            res_out[2][kind].reshape(w_out.shape),
            small_res(8, kind, final_g.shape),
        ])
    return (loss, grad_x.reshape(x.shape), *per_weight[0], *per_weight[1], *per_weight[2], *per_weight[3])
```

```python
import functools

import jax
import jax.numpy as jnp
from jax import lax
from jax.experimental import pallas as pl
from jax.experimental.pallas import tpu as pltpu

D_MODEL = 1024
N_META = 16
N_DEV = 8
D_IN = 9 * D_MODEL
COLS = D_IN // N_DEV
ROWS_OUT = D_MODEL // N_DEV
CONV_A = 31
CONV_B = 3
EPS = 1e-6

ADAM_LR = 0.001
ADAM_B1 = 0.9
ADAM_B2 = 0.999
ADAM_EPS = 1e-08
ADAM_WD = 0.01
ADAM_STEP = 10

TILE = 128
LANES = 128
N_CHUNK = D_MODEL // LANES
ROW_CHUNK = 32
HALO = 32
SUBLANES = 8
VMEM_LIMIT = 56 * 1024 * 1024

ROW_FINAL_G, ROW_B_A_OUT, ROW_LN_G, ROW_LN_B, ROW_CONV_A_B, ROW_LOSS = 0, 1, 2, 3, 4, 5
ROW_NORM_G_X, ROW_NORM_G_META = 8, 16
ROW_CONV_A_W, ROW_CONV_B_W, ROW_META = 24, 56, 64
SMALL_ROWS = 80

MESH = pl.DeviceIdType.MESH
_ANY = pl.BlockSpec(memory_space=pl.ANY)
_VMEM = pl.BlockSpec(memory_space=pltpu.VMEM)


def _resident(shape):
    return pl.BlockSpec(shape, lambda *_: (0,) * len(shape), pipeline_mode=pl.Buffered(1))
BF16 = jnp.bfloat16
F32 = jnp.float32


def _sigmoid(v):
    return jax.nn.sigmoid(v)


def _dot(a, b):
    return jnp.dot(a, b, preferred_element_type=F32)


def _dot_nt(a, b):
    return lax.dot_general(a, b, (((1,), (1,)), ((), ())), preferred_element_type=F32)


def _dot_tn(a, b):
    return lax.dot_general(a, b, (((0,), (0,)), ((), ())), preferred_element_type=F32)


def _colsum(v):
    return jnp.sum(v, axis=0, keepdims=True)


def _rowmean(v):
    parts = [v[:, LANES * c:LANES * (c + 1)] for c in range(v.shape[1] // LANES)]
    return jnp.sum(functools.reduce(jnp.add, parts), axis=-1, keepdims=True) * (1.0 / v.shape[1])


def _fold8(v):
    parts = [v[SUBLANES * g:SUBLANES * (g + 1)] for g in range(v.shape[0] // SUBLANES)]
    return functools.reduce(jnp.add, parts)


def _all_gather(shard, name, ride=()):
    m, n = shard.shape
    n_arr = len(ride)
    ride_shapes, ride_sems = _exchange_shapes("chips", ride) if ride else ([], [])

    def body(*refs):
        x_ref, out_ref = refs[0], refs[1 + n_arr]
        send_sems, recv_sems, local_sem = refs[2 + 2 * n_arr:5 + 2 * n_arr]
        riding = _chip_copies(refs[1:1 + n_arr], refs[2 + n_arr:2 + 2 * n_arr], *refs[5 + 2 * n_arr:]) if ride else []
        for cp in riding:
            cp.start()
        x, y, c = lax.axis_index("x"), lax.axis_index("y"), lax.axis_index("c")
        me, sibling = (x, y, c), (x, y, 1 - c)
        chips = [(1 - x, y), (x, 1 - y), (1 - x, 1 - y)]

        def slot(px, py, pc):
            return out_ref.at[4 * px + 2 * py + pc]

        def copy(k, block, to, src=None):
            return pltpu.make_async_remote_copy(
                src_ref=slot(*block) if src is None else src, dst_ref=slot(*block),
                send_sem=send_sems.at[k], recv_sem=recv_sems.at[k], device_id=to, device_id_type=MESH)

        mine = pltpu.make_async_copy(x_ref, slot(*me), local_sem)
        mine.start()
        first = [copy(0, me, sibling, src=x_ref)]
        first += [copy(1 + j, me, (*chip, c), src=x_ref) for j, chip in enumerate(chips)]
        for cp in first:
            cp.start()
        passed = [copy(4 + j, (*chip, c), sibling) for j, chip in enumerate(chips)]
        for j, chip in enumerate(chips):
            copy(1 + j, (*chip, c), me).wait_recv()
            passed[j].start()
        copy(0, sibling, me).wait_recv()
        for j, chip in enumerate(chips):
            copy(4 + j, (*chip, 1 - c), me).wait_recv()
        for cp in first + passed:
            cp.wait_send()
        mine.wait()
        for cp in riding:
            cp.wait()

    res = pl.pallas_call(
        body, name=name,
        out_shape=[jax.ShapeDtypeStruct((N_DEV, m, n), shard.dtype)] + ride_shapes,
        in_specs=[_ANY] * (1 + n_arr), out_specs=[_ANY] * (1 + n_arr),
        scratch_shapes=[pltpu.SemaphoreType.DMA((7,)), pltpu.SemaphoreType.DMA((7,)), pltpu.SemaphoreType.DMA(())]
        + ride_sems,
    )(shard, *ride)
    return res if ride else res[0]


def _sibling_copies(srcs, dsts, send_sems, recv_sems):
    x, y, c = lax.axis_index("x"), lax.axis_index("y"), lax.axis_index("c")
    return [pltpu.make_async_remote_copy(
        src_ref=src.at[2 * q + (1 - c)], dst_ref=dst.at[q],
        send_sem=send_sems.at[4 * a + q], recv_sem=recv_sems.at[4 * a + q],
        device_id=(x, y, 1 - c), device_id_type=MESH)
        for a, (src, dst) in enumerate(zip(srcs, dsts)) for q in range(4)]


def _chip_copies(srcs, dsts, send_sems, recv_sems):
    x, y, c = lax.axis_index("x"), lax.axis_index("y"), lax.axis_index("c")
    targets = [(x, 1 - y, c), (1 - x, y, c), (1 - x, 1 - y, c)]
    return [pltpu.make_async_remote_copy(
        src_ref=src.at[k], dst_ref=dst.at[k],
        send_sem=send_sems.at[3 * a + k], recv_sem=recv_sems.at[3 * a + k],
        device_id=targets[k], device_id_type=MESH)
        for a, (src, dst) in enumerate(zip(srcs, dsts)) for k in range(3)]


def _sibling_half_copies(srcs, dsts, send_sems, recv_sems):
    x, y, c = lax.axis_index("x"), lax.axis_index("y"), lax.axis_index("c")
    return [pltpu.make_async_remote_copy(
        src_ref=src.at[q], dst_ref=dst.at[q],
        send_sem=send_sems.at[4 * a + q], recv_sem=recv_sems.at[4 * a + q],
        device_id=(x, y, 1 - c), device_id_type=MESH)
        for a, (src, dst) in enumerate(zip(srcs, dsts)) for q in range(4)]


_EXCHANGES = {"sibling": (4, _sibling_copies, 4), "sibling_half": (4, _sibling_half_copies, 4),
              "chips": (3, _chip_copies, 3)}


def _exchange_shapes(kind, arrays):
    per_array, _, slots = _EXCHANGES[kind]
    out_shape = [jax.ShapeDtypeStruct((slots,) + a.shape[1:], a.dtype) for a in arrays]
    sems = [pltpu.SemaphoreType.DMA((per_array * len(arrays),))] * 2
    return out_shape, sems


def _exchange(kind, arrays, name):
    n_arr = len(arrays)
    out_shape, sems = _exchange_shapes(kind, arrays)

    def body(*refs):
        copies = _EXCHANGES[kind][1](refs[:n_arr], refs[n_arr:2 * n_arr], *refs[2 * n_arr:])
        for cp in copies:
            cp.start()
        for cp in copies:
            cp.wait()

    return pl.pallas_call(body, name=name, out_shape=out_shape, in_specs=[_ANY] * n_arr,
                          out_specs=[_ANY] * n_arr, scratch_shapes=sems)(*arrays)


def _riding(body, n_in, n_out, kind, n_arr, is_first, is_last):
    def wrapped(*refs):
        ins, srcs = refs[:n_in], refs[n_in:n_in + n_arr]
        outs = refs[n_in + n_arr:n_in + n_arr + n_out]
        dsts = refs[n_in + n_arr + n_out:n_in + 2 * n_arr + n_out]
        scratch, sems = refs[n_in + 2 * n_arr + n_out:-2], refs[-2:]

        @pl.when(is_first())
        def _():
            for cp in _EXCHANGES[kind][1](srcs, dsts, *sems):
                cp.start()

        body(*ins, *outs, *scratch)

        @pl.when(is_last())
        def _():
            for cp in _EXCHANGES[kind][1](srcs, dsts, *sems):
                cp.wait()

    return wrapped


def _chip_partial(pos, mine, theirs, relations, out_dtype, row_tile, name):
    n_slots, m, n = mine.shape
    q0 = relations[0]

    def chip_of(qi, pos_ref):
        q = qi + q0
        return pos_ref[0] ^ (q >> 1), pos_ref[1] ^ (q & 1)

    def mine_map(qi, t, pos_ref):
        px, py = chip_of(qi, pos_ref)
        return (4 * px + 2 * py + pos_ref[2] if n_slots == N_DEV else 2 * px + py), t, 0

    def theirs_map(qi, t, pos_ref):
        px, py = chip_of(qi, pos_ref)
        return 2 * px + py, t, 0

    def body(pos_ref, a_ref, b_ref, o_ref):
        o_ref[...] = (a_ref[...] + b_ref[...]).astype(out_dtype)

    return pl.pallas_call(
        body, name=name,
        out_shape=jax.ShapeDtypeStruct((len(relations), m, n), out_dtype),
        grid_spec=pltpu.PrefetchScalarGridSpec(
            num_scalar_prefetch=1, grid=(len(relations), m // row_tile),
            in_specs=[pl.BlockSpec((None, row_tile, n), mine_map), pl.BlockSpec((None, row_tile, n), theirs_map)],
            out_specs=pl.BlockSpec((None, row_tile, n), lambda qi, t, pos_ref: (qi, t, 0))),
        compiler_params=pltpu.CompilerParams(dimension_semantics=("arbitrary", "arbitrary")),
    )(pos, mine, theirs)


def _gather_norm_proj(pos, x2d, meta_tile, norm_g, w_in_shard, w_out_shards, n_chunk):
    seq = x2d.shape[0]
    n_tiles = seq // TILE + 1
    tp = n_tiles * TILE
    third = COLS // 3
    units = ([(0, u) for u in range(3)] + [(1, u) for u in range(3)]
             + [(2 + j, u) for u in range(3) for j in range(3)]
             + [(5 + j, u) for u in range(3) for j in range(3)])
    n_steps = n_tiles + len(units)
    chunk = tp // n_chunk

    def unit_of(g):
        m = jnp.maximum(g - n_tiles, 0)
        near = m < 6
        v = jnp.where(m < 15, m - 6, m - 15)
        s = jnp.where(near, m // 3, jnp.where(m < 15, 2, 5) + v % 3)
        u = jnp.where(near, m % 3, v // 3)
        return s, u

    def column_block(g, pos_ref):
        s, u = unit_of(g)
        far = s >= 2
        j = jnp.where(far, (s - 2) % 3, 0)
        fx = jnp.where(far & (j != 1), 1, 0)
        fy = jnp.where(far & (j != 0), 1, 0)
        fc = jnp.where(far, (s - 2) // 3, s)
        device = 4 * (pos_ref[0] ^ fx) + 2 * (pos_ref[1] ^ fy) + (pos_ref[2] ^ fc)
        return 3 * device + u

    def body(pos_ref, x_ref, meta_ref, g_ref, win_ref, wa_ref, wb_ref, wo_ref,
             h_ref, proj_ref, win_all, wa_all, wb_all, wo_all,
             h_all, wbuf, send_sems, recv_sems, local_sems):
        g = pl.program_id(0)
        x, y, c = lax.axis_index("x"), lax.axis_index("y"), lax.axis_index("c")
        me, sibling = (x, y, c), (x, y, 1 - c)
        chips = [(1 - x, y), (x, 1 - y), (1 - x, 1 - y)]
        shards = (win_ref, wa_ref, wb_ref, wo_ref)
        gathered = (win_all, wa_all, wb_all, wo_all)

        def part(ref, a, u):
            return ref.at[:, pl.ds(third * u, third)] if a == 0 else ref

        def slot(a, block, u):
            px, py, pc = block
            return part(gathered[a].at[4 * px + 2 * py + pc], a, u)

        def sem(a, k, u):
            return 3 * k + u if a == 0 else 21 + 7 * (a - 1) + k

        def copy(a, k, block, to, u=0, from_shard=False):
            return pltpu.make_async_remote_copy(
                src_ref=part(shards[a], a, u) if from_shard else slot(a, block, u), dst_ref=slot(a, block, u),
                send_sem=send_sems.at[sem(a, k, u)], recv_sem=recv_sems.at[sem(a, k, u)],
                device_id=to, device_id_type=MESH)

        def keep(a):
            px, py, pc = me
            return pltpu.make_async_copy(shards[a], gathered[a].at[4 * px + 2 * py + pc], local_sems.at[a])

        def load(src, buf):
            cp = pltpu.make_async_copy(src, wbuf.at[buf], local_sems.at[4 + buf])
            cp.start()
            cp.wait()

        targets = [sibling] + [(*chip, c) for chip in chips]

        @pl.when(g == 0)
        def _():
            for a in range(4):
                keep(a).start()
            for u in range(3):
                for k, to in enumerate(targets):
                    copy(0, k, me, to, u, from_shard=True).start()
            for a in range(1, 4):
                for k, to in enumerate(targets):
                    copy(a, k, me, to, from_shard=True).start()

        @pl.when(g < n_tiles)
        def _():
            s0 = jnp.where(g == n_tiles - 1, meta_ref[...], x_ref[...])
            r = lax.rsqrt(_rowmean(s0 * s0) + EPS)
            h = ((s0 * r) * g_ref[...]).astype(BF16)
            h_ref[...] = h
            h_all[pl.ds(pl.multiple_of(g * TILE, TILE), TILE), :] = h

        for m, (s, u) in enumerate(units):
            @pl.when(g == n_tiles + m)
            def _(m=m, s=s, u=u):
                if s == 0:
                    load(part(win_ref, 0, u), m % 2)
                    return
                if s == 1:
                    block = sibling
                    copy(0, 0, block, me, u).wait_recv()
                elif s <= 4:
                    block = (*chips[s - 2], c)
                    copy(0, s - 1, block, me, u).wait_recv()
                    copy(0, s + 2, block, sibling, u).start()
                else:
                    block = (*chips[s - 5], 1 - c)
                    copy(0, s - 1, block, me, u).wait_recv()
                    if u == 0:
                        for a in range(1, 4):
                            copy(a, s - 4, (*chips[s - 5], c), me).wait_recv()
                            copy(a, s - 1, (*chips[s - 5], c), sibling).start()
                load(slot(0, block, u), m % 2)

        @pl.when(g >= n_tiles)
        def _():
            w = wbuf[(g - n_tiles) % 2]
            for r in range(n_chunk):
                proj_ref[r * chunk:(r + 1) * chunk, :] = _dot(h_all[r * chunk:(r + 1) * chunk, :], w)

        @pl.when(g == n_steps - 1)
        def _():
            for a in range(1, 4):
                copy(a, 0, sibling, me).wait_recv()
                for j, chip in enumerate(chips):
                    copy(a, 4 + j, (*chip, 1 - c), me).wait_recv()
            for a in range(4):
                for u in range(3 if a == 0 else 1):
                    for k, to in enumerate(targets):
                        copy(a, k, me, to, u, from_shard=True).wait_send()
                    for j, chip in enumerate(chips):
                        copy(a, 4 + j, (*chip, c), sibling, u).wait_send()
                keep(a).wait()

    n_x = n_tiles - 1
    return pl.pallas_call(
        body, name="gather_norm_proj",
        out_shape=[jax.ShapeDtypeStruct((tp, D_MODEL), BF16), jax.ShapeDtypeStruct((tp, D_IN), F32),
                   jax.ShapeDtypeStruct((N_DEV,) + w_in_shard.shape, BF16)]
                  + [jax.ShapeDtypeStruct((N_DEV,) + w.shape, BF16) for w in w_out_shards],
        grid_spec=pltpu.PrefetchScalarGridSpec(
            num_scalar_prefetch=1, grid=(n_steps,),
            in_specs=[pl.BlockSpec((TILE, D_MODEL), lambda g, pos_ref: (jnp.minimum(g, n_x - 1), 0)),
                      _VMEM, _VMEM, _ANY, _ANY, _ANY, _ANY],
            out_specs=[pl.BlockSpec((TILE, D_MODEL), lambda g, pos_ref: (jnp.minimum(g, n_tiles - 1), 0)),
                       pl.BlockSpec((tp, third), lambda g, pos_ref: (0, column_block(g, pos_ref))),
                       _ANY, _ANY, _ANY, _ANY],
            scratch_shapes=[pltpu.VMEM((tp, D_MODEL), BF16), pltpu.VMEM((2, D_MODEL, third), BF16),
                            pltpu.SemaphoreType.DMA((42,)), pltpu.SemaphoreType.DMA((42,)),
                            pltpu.SemaphoreType.DMA((6,))]),
        compiler_params=pltpu.CompilerParams(dimension_semantics=("arbitrary",), vmem_limit_bytes=VMEM_LIMIT),
    )(pos, x2d, meta_tile, norm_g, w_in_shard, *w_out_shards)


C_AVAL, C_AGLU, C_AZ, C_BB, C_BC, C_BX, C_BZ, C_GA, C_GB = (k * D_MODEL for k in range(9))
S_AZ, S_BB, S_BZ, S_GA, S_GB = (k * D_MODEL for k in range(5))


def _fused_pass(proj, x2d, tgt2d, meta_tile, conv_a_w, conv_a_b, ln_a_g, ln_a_b, b_a_out, conv_b_w, final_g,
                w_a, w_b, w_o, w_a_t, w_b_t, w_o_t, n_tiles):
    T = TILE
    tp = n_tiles * T
    inv_d = 1.0 / D_MODEL

    def block_of(tile):
        return jnp.where(tile == 0, n_tiles - 1, tile - 1)

    def cur(i):
        return block_of(jnp.minimum(i, n_tiles - 1))

    def prev(i):
        return block_of(jnp.clip(i - 1, 0, n_tiles - 1))

    def xblk(i):
        return jnp.maximum(jnp.minimum(i, n_tiles - 1) - 1, 0)

    def body(proj_ref, x_ref, tgt_ref, meta_ref, caw_ref, cab_ref, lng_ref, lnb_ref, bao_ref, cbw_ref, fg_ref,
             wa_ref, wb_ref, wo_ref, wat_ref, wbt_ref, wot_ref,
             dproj_ref, ds1_ref, lhs_ref, rhs_ref, vec_ref, dcaw_ref, dcbw_ref,
             ua0_buf, cb_buf, dua1_buf, dc3_buf, aprev, cprev, stage, ua1_buf, c3_buf, xhat_buf, rstd_buf,
             dpa_buf, dpb_buf, dcaw8, dcbw8, shift_buf):
        i = pl.program_id(0)

        @pl.when(i == 0)
        def _init():
            for buf in (ua0_buf, cb_buf, dua1_buf, dc3_buf, aprev, cprev, dcaw8, dcbw8):
                buf[...] = jnp.zeros(buf.shape, buf.dtype)
            vec_ref[...] = jnp.zeros(vec_ref.shape, F32)

        @pl.when(i >= 1)
        def _emit_stage():
            dproj_ref[:, C_AZ:C_BC] = stage[:, S_AZ:S_BZ]
            dproj_ref[:, C_BZ:D_IN] = stage[:, S_BZ:S_GB + D_MODEL]

        @pl.when(i < n_tiles)
        def _front():
            def conv_chunk(cc, carry):
                c0 = pl.multiple_of(cc * LANES, LANES)
                lanes = pl.ds(c0, LANES)

                def col(base):
                    return pl.ds(pl.multiple_of(base + cc * LANES, LANES), LANES)

                ua0 = proj_ref[:, col(C_AVAL)] * _sigmoid(proj_ref[:, col(C_AGLU)])
                ua0_buf[T:2 * T, lanes] = ua0
                acc = jnp.broadcast_to(cab_ref[:, lanes], (T, LANES))
                lead = HALO - (CONV_A - 1)
                for r in range(SUBLANES):
                    taps = [k for k in range(CONV_A) if (k + lead) % SUBLANES == r]
                    rows = T + SUBLANES * max((k + lead) // SUBLANES for k in taps)
                    if r:
                        shift_buf[r, 0:rows, :] = ua0_buf[pl.ds(T - HALO + r, rows), lanes]
                    for k in taps:
                        q = (k + lead) // SUBLANES
                        if r:
                            win = shift_buf[r, SUBLANES * q:SUBLANES * q + T, :]
                        else:
                            win = ua0_buf[pl.ds(T - HALO + SUBLANES * q, T), lanes]
                        acc = acc + caw_ref[k:k + 1, lanes] * win
                ua1_buf[:, lanes] = acc
                cb = proj_ref[:, col(C_BC)] * proj_ref[:, col(C_BX)]
                cb_buf[T:2 * T, lanes] = cb
                acc3 = cbw_ref[0:1, lanes] * cb_buf[pl.ds(T - 2, T), lanes]
                for k in range(1, CONV_B):
                    acc3 = acc3 + cbw_ref[k:k + 1, lanes] * cb_buf[pl.ds(T - (CONV_B - 1) + k, T), lanes]
                c3_buf[:, lanes] = acc3
                return carry

            lax.fori_loop(0, N_CHUNK, conv_chunk, 0)

            def gate_rows(rc, carry):
                rows = pl.ds(pl.multiple_of(rc * ROW_CHUNK, ROW_CHUNK), ROW_CHUNK)
                ua1 = ua1_buf[rows, :]
                xc = ua1 - _rowmean(ua1)
                rstd = lax.rsqrt(_rowmean(xc * xc) + EPS)
                xhat = xc * rstd
                xhat_buf[rows, :] = xhat
                rstd_buf[rows, :] = rstd
                ua2 = xhat * lng_ref[...] + lnb_ref[...]
                ua3 = ua2 * _sigmoid(ua2)
                a_z = proj_ref[rows, C_AZ:C_AZ + D_MODEL]
                lhs_ref[0, rows, :] = (ua3 * (a_z * _sigmoid(a_z))).astype(BF16)
                b_z = proj_ref[rows, C_BZ:C_BZ + D_MODEL]
                ub = proj_ref[rows, C_BB:C_BB + D_MODEL] * c3_buf[rows, :]
                lhs_ref[1, rows, :] = (ub * (b_z * _sigmoid(b_z))).astype(BF16)
                return carry

            lax.fori_loop(0, T // ROW_CHUNK, gate_rows, 0)

            ya = _dot(lhs_ref[0], wa_ref[...]) + bao_ref[...]
            yb = _dot(lhs_ref[1], wb_ref[...])
            sga = _sigmoid(proj_ref[:, C_GA:C_GA + D_MODEL])
            sgb = _sigmoid(proj_ref[:, C_GB:C_GB + D_MODEL])
            m_b = (sga * ya + sgb * yb).astype(BF16)
            lhs_ref[2] = m_b
            s0 = jnp.where(i == 0, meta_ref[...], x_ref[...])
            s1 = s0 + _dot(m_b, wo_ref[...])
            r1 = lax.rsqrt(_rowmean(s1 * s1) + EPS)
            y = (s1 * r1) * fg_ref[...]
            is_token = (i >= 1).astype(F32)
            err = (y - tgt_ref[...]) * is_token
            vec_ref[ROW_LOSS:ROW_LOSS + 1, :] += (0.5 * inv_d) * _colsum(err * err)
            dy = err * inv_d
            vec_ref[ROW_FINAL_G:ROW_FINAL_G + 1, :] += _colsum(dy * (s1 * r1))
            gy = dy * fg_ref[...]
            ds1 = r1 * gy - s1 * ((r1 * r1 * r1) * _rowmean(gy * s1))
            ds1_ref[...] = ds1
            ds1_b = ds1.astype(BF16)
            rhs_ref[2] = ds1_b
            dm = _dot(ds1_b, wot_ref[...])
            dya = dm * sga
            dyb = dm * sgb
            stage[:, S_GA:S_GA + D_MODEL] = (dm * ya * (sga * (1.0 - sga))).astype(BF16)
            stage[:, S_GB:S_GB + D_MODEL] = (dm * yb * (sgb * (1.0 - sgb))).astype(BF16)
            vec_ref[ROW_B_A_OUT:ROW_B_A_OUT + 1, :] += _colsum(dya)
            dya_b = dya.astype(BF16)
            dyb_b = dyb.astype(BF16)
            rhs_ref[0] = dya_b
            rhs_ref[1] = dyb_b
            dpa_buf[...] = _dot(dya_b, wat_ref[...])
            dpb_buf[...] = _dot(dyb_b, wbt_ref[...])

            def gate_rows_bwd(rc, carry):
                r0 = pl.multiple_of(rc * ROW_CHUNK, ROW_CHUNK)
                rows = pl.ds(r0, ROW_CHUNK)
                later = pl.ds(pl.multiple_of(T + rc * ROW_CHUNK, ROW_CHUNK), ROW_CHUNK)
                xhat = xhat_buf[rows, :]
                ua2 = xhat * lng_ref[...] + lnb_ref[...]
                sg2 = _sigmoid(ua2)
                ua3 = ua2 * sg2
                a_z = proj_ref[rows, C_AZ:C_AZ + D_MODEL]
                sz = _sigmoid(a_z)
                dpa = dpa_buf[rows, :]
                stage[rows, S_AZ:S_AZ + D_MODEL] = (dpa * ua3 * (sz * (1.0 + a_z * (1.0 - sz)))).astype(BF16)
                dua2 = dpa * (a_z * sz) * (sg2 * (1.0 + ua2 * (1.0 - sg2)))
                vec_ref[ROW_LN_G:ROW_LN_G + 1, :] += _colsum(dua2 * xhat)
                vec_ref[ROW_LN_B:ROW_LN_B + 1, :] += _colsum(dua2)
                dxh = dua2 * lng_ref[...]
                dua1 = rstd_buf[rows, :] * (dxh - _rowmean(dxh) - xhat * _rowmean(dxh * xhat))
                vec_ref[ROW_CONV_A_B:ROW_CONV_A_B + 1, :] += _colsum(dua1)
                dua1_buf[later, :] = dua1
                b_z = proj_ref[rows, C_BZ:C_BZ + D_MODEL]
                sbz = _sigmoid(b_z)
                b_b = proj_ref[rows, C_BB:C_BB + D_MODEL]
                c3 = c3_buf[rows, :]
                dpb = dpb_buf[rows, :]
                stage[rows, S_BZ:S_BZ + D_MODEL] = (dpb * (b_b * c3) * (sbz * (1.0 + b_z * (1.0 - sbz)))).astype(BF16)
                dub = dpb * (b_z * sbz)
                stage[rows, S_BB:S_BB + D_MODEL] = (dub * c3).astype(BF16)
                dc3_buf[later, :] = dub * b_b
                return carry

            lax.fori_loop(0, T // ROW_CHUNK, gate_rows_bwd, 0)

        @pl.when(i == n_tiles)
        def _no_later_tile():
            dua1_buf[T:2 * T, :] = jnp.zeros((T, D_MODEL), F32)
            dc3_buf[T:2 * T, :] = jnp.zeros((T, D_MODEL), F32)

        @pl.when(i >= 1)
        def _lagged():
            def convt_chunk(cc, carry):
                c0 = pl.multiple_of(cc * LANES, LANES)
                lanes = pl.ds(c0, LANES)

                def col(base):
                    return pl.ds(pl.multiple_of(base + cc * LANES, LANES), LANES)

                ua0 = ua0_buf[0:T, lanes]
                acc = jnp.zeros((T, LANES), F32)
                for r in range(SUBLANES):
                    shifts = [j for j in range(CONV_A) if j % SUBLANES == r]
                    rows = T + shifts[-1] - r
                    if r:
                        shift_buf[r, 0:rows, :] = dua1_buf[pl.ds(r, rows), lanes]
                    for j in shifts:
                        k = CONV_A - 1 - j
                        if r:
                            later = shift_buf[r, j - r:j - r + T, :]
                        else:
                            later = dua1_buf[pl.ds(j, T), lanes]
                        acc = acc + caw_ref[k:k + 1, lanes] * later
                        dcaw8[SUBLANES * k:SUBLANES * (k + 1), lanes] += _fold8(ua0 * later)
                a_val = aprev[:, col(0)]
                sg = _sigmoid(aprev[:, col(D_MODEL)])
                dproj_ref[:, col(C_AVAL)] = (acc * sg).astype(BF16)
                dproj_ref[:, col(C_AGLU)] = (acc * a_val * (sg * (1.0 - sg))).astype(BF16)

                cb = cb_buf[0:T, lanes]
                acc3 = jnp.zeros((T, LANES), F32)
                for j in range(CONV_B):
                    k = CONV_B - 1 - j
                    later = dc3_buf[pl.ds(j, T), lanes]
                    acc3 = acc3 + cbw_ref[k:k + 1, lanes] * later
                    dcbw8[SUBLANES * k:SUBLANES * (k + 1), lanes] += _fold8(cb * later)
                dproj_ref[:, col(C_BC)] = (acc3 * cprev[:, col(D_MODEL)]).astype(BF16)
                dproj_ref[:, col(C_BX)] = (acc3 * cprev[:, col(0)]).astype(BF16)
                return carry

            lax.fori_loop(0, N_CHUNK, convt_chunk, 0)

        for buf in (ua0_buf, cb_buf, dua1_buf, dc3_buf):
            buf[0:T, :] = buf[T:2 * T, :]
        aprev[...] = proj_ref[:, C_AVAL:C_AZ]
        cprev[...] = proj_ref[:, C_BC:C_BZ]

        @pl.when(i == n_tiles)
        def _finish():
            for k in range(CONV_A):
                dcaw_ref[k:k + 1, :] = _colsum(dcaw8[SUBLANES * k:SUBLANES * (k + 1), :])
            dcaw_ref[CONV_A:CONV_A + 1, :] = jnp.zeros((1, D_MODEL), F32)
            for k in range(CONV_B):
                dcbw_ref[k:k + 1, :] = _colsum(dcbw8[SUBLANES * k:SUBLANES * (k + 1), :])
            dcbw_ref[CONV_B:SUBLANES, :] = jnp.zeros((SUBLANES - CONV_B, D_MODEL), F32)

    tile_in = lambda width: pl.BlockSpec((T, width), lambda i: (cur(i), 0))
    return pl.pallas_call(
        body, name="fused_pass", grid=(n_tiles + 1,),
        out_shape=[
            jax.ShapeDtypeStruct((tp, D_IN), BF16),
            jax.ShapeDtypeStruct((tp, D_MODEL), F32),
            jax.ShapeDtypeStruct((3, tp, D_MODEL), BF16),
            jax.ShapeDtypeStruct((3, tp, D_MODEL), BF16),
            jax.ShapeDtypeStruct((SUBLANES, D_MODEL), F32),
            jax.ShapeDtypeStruct((32, D_MODEL), F32),
            jax.ShapeDtypeStruct((SUBLANES, D_MODEL), F32),
        ],
        in_specs=[
            tile_in(D_IN),
            pl.BlockSpec((T, D_MODEL), lambda i: (xblk(i), 0)),
            pl.BlockSpec((T, D_MODEL), lambda i: (xblk(i), 0)),
            _VMEM, _VMEM, _VMEM, _VMEM, _VMEM, _VMEM, _VMEM, _VMEM,
            *[_resident((D_MODEL, D_MODEL)) for _ in range(6)],
        ],
        out_specs=[
            pl.BlockSpec((T, D_IN), lambda i: (prev(i), 0)),
            pl.BlockSpec((T, D_MODEL), lambda i: (cur(i), 0)),
            pl.BlockSpec((3, T, D_MODEL), lambda i: (0, cur(i), 0)),
            pl.BlockSpec((3, T, D_MODEL), lambda i: (0, cur(i), 0)),
            _VMEM, _VMEM, _VMEM,
        ],
        scratch_shapes=[
            pltpu.VMEM((2 * T, D_MODEL), F32),
            pltpu.VMEM((2 * T, D_MODEL), F32),
            pltpu.VMEM((2 * T, D_MODEL), F32),
            pltpu.VMEM((2 * T, D_MODEL), F32),
            pltpu.VMEM((T, 2 * D_MODEL), F32),
            pltpu.VMEM((T, 2 * D_MODEL), F32),
            pltpu.VMEM((T, 5 * D_MODEL), BF16),
            pltpu.VMEM((T, D_MODEL), F32),
            pltpu.VMEM((T, D_MODEL), F32),
            pltpu.VMEM((T, D_MODEL), F32),
            pltpu.VMEM((T, 1), F32),
            pltpu.VMEM((T, D_MODEL), F32),
            pltpu.VMEM((T, D_MODEL), F32),
            pltpu.VMEM((32 * SUBLANES, D_MODEL), F32),
            pltpu.VMEM((SUBLANES * SUBLANES, D_MODEL), F32),
            pltpu.VMEM((SUBLANES, T + HALO, LANES), F32),
        ],
        compiler_params=pltpu.CompilerParams(dimension_semantics=("arbitrary",), vmem_limit_bytes=VMEM_LIMIT),
    )(proj, x2d, tgt2d, meta_tile, conv_a_w, conv_a_b, ln_a_g, ln_a_b, b_a_out, conv_b_w, final_g,
      w_a, w_b, w_o, w_a_t, w_b_t, w_o_t)


def _input_bwd(dproj, ds1, s0, norm_g, w_in_all, row_tile, n_tiles, block0, name, ride=()):
    def body(dp_ref, ds1_ref, s0_ref, g_ref, w_ref, out_ref, vec_ref):
        t = pl.program_id(0)

        @pl.when(t == 0)
        def _():
            vec_ref[...] = jnp.zeros(vec_ref.shape, F32)

        dh = _dot_nt(dp_ref[:, 0:COLS], w_ref[0])
        for j in range(1, N_DEV):
            dh = dh + _dot_nt(dp_ref[:, j * COLS:(j + 1) * COLS], w_ref[j])
        s0v = s0_ref[...]
        r = lax.rsqrt(_rowmean(s0v * s0v) + EPS)
        gh = dh * g_ref[...]
        out_ref[...] = ds1_ref[...] + r * gh - s0v * ((r * r * r) * _rowmean(gh * s0v))
        vec_ref[0:1, :] += _colsum(dh * (s0v * r))

    n_arr = len(ride)
    ride_shapes, ride_sems = _exchange_shapes("chips", ride) if ride else ([], [])
    if ride:
        body = _riding(body, 5, 2, "chips", n_arr, lambda: pl.program_id(0) == 0,
                       lambda: pl.program_id(0) == n_tiles - 1)
    return pl.pallas_call(
        body, name=name, grid=(n_tiles,),
        out_shape=[jax.ShapeDtypeStruct(s0.shape, F32), jax.ShapeDtypeStruct((SUBLANES, D_MODEL), F32)] + ride_shapes,
        in_specs=[pl.BlockSpec((row_tile, D_IN), lambda t: (block0 + t, 0)),
                  pl.BlockSpec((row_tile, D_MODEL), lambda t: (block0 + t, 0)),
                  pl.BlockSpec((row_tile, D_MODEL), lambda t: (t, 0)),
                  _VMEM, _resident((N_DEV, D_MODEL, COLS))] + [_ANY] * n_arr,
        out_specs=[pl.BlockSpec((row_tile, D_MODEL), lambda t: (t, 0)), _VMEM] + [_ANY] * n_arr,
        scratch_shapes=ride_sems,
        compiler_params=pltpu.CompilerParams(dimension_semantics=("arbitrary",), vmem_limit_bytes=VMEM_LIMIT),
    )(dproj, ds1, s0, norm_g, w_in_all, *ride)


def _grad_w_in_half(pos, h, dproj, k_tile, other_side, ride_kind, ride, name):
    tp = h.shape[0]
    n_k = tp // k_tile

    def column_block(q, k, pos_ref):
        return k, 2 * q + (1 - pos_ref[2] if other_side else pos_ref[2])

    def body(pos_ref, h_ref, dp_ref, o_ref):
        @pl.when(pl.program_id(1) == 0)
        def _():
            o_ref[...] = jnp.zeros(o_ref.shape, F32)

        o_ref[...] += _dot_tn(h_ref[...], dp_ref[...])

    n_arr = len(ride)
    ride_shapes, ride_sems = _exchange_shapes(ride_kind, ride)
    body = _riding(body, 3, 1, ride_kind, n_arr,
                   lambda: (pl.program_id(0) == 0) & (pl.program_id(1) == 0),
                   lambda: (pl.program_id(0) == 3) & (pl.program_id(1) == n_k - 1))
    return pl.pallas_call(
        body, name=name,
        out_shape=[jax.ShapeDtypeStruct((4, D_MODEL, COLS), F32)] + ride_shapes,
        grid_spec=pltpu.PrefetchScalarGridSpec(
            num_scalar_prefetch=1, grid=(4, n_k),
            in_specs=[pl.BlockSpec((k_tile, D_MODEL), lambda q, k, pos_ref: (k, 0)),
                      pl.BlockSpec((k_tile, COLS), column_block)] + [_ANY] * n_arr,
            out_specs=[pl.BlockSpec((None, D_MODEL, COLS), lambda q, k, pos_ref: (q, 0, 0))] + [_ANY] * n_arr,
            scratch_shapes=ride_sems),
        compiler_params=pltpu.CompilerParams(dimension_semantics=("arbitrary", "arbitrary"),
                                             vmem_limit_bytes=VMEM_LIMIT),
    )(pos, h, dproj, *ride)


def _grad_w_out(lhs, rhs, k_tile):
    tp = lhs.shape[1]

    def body(a_ref, b_ref, o_ref):
        @pl.when(pl.program_id(1) == 0)
        def _():
            o_ref[...] = jnp.zeros(o_ref.shape, F32)

        o_ref[...] += _dot_tn(a_ref[...], b_ref[...]).reshape(N_DEV, ROWS_OUT, D_MODEL)

    return pl.pallas_call(
        body, name="grad_w_out", grid=(3, tp // k_tile),
        out_shape=jax.ShapeDtypeStruct((N_DEV, 3, ROWS_OUT, D_MODEL), F32),
        in_specs=[pl.BlockSpec((None, k_tile, D_MODEL), lambda w, k: (w, k, 0)),
                  pl.BlockSpec((None, k_tile, D_MODEL), lambda w, k: (w, k, 0))],
        out_specs=pl.BlockSpec((N_DEV, None, ROWS_OUT, D_MODEL), lambda w, k: (0, w, 0, 0)),
        compiler_params=pltpu.CompilerParams(dimension_semantics=("arbitrary", "arbitrary"),
                                             vmem_limit_bytes=VMEM_LIMIT),
    )(lhs, rhs)


def _adamw_math(w, g, m, v):
    m = ADAM_B1 * m + (1.0 - ADAM_B1) * g
    v = ADAM_B2 * v + (1.0 - ADAM_B2) * (g * g)
    m_hat = m / (1.0 - ADAM_B1 ** ADAM_STEP)
    v_hat = v / (1.0 - ADAM_B2 ** ADAM_STEP)
    delta = -ADAM_LR * (m_hat / (jnp.sqrt(v_hat) + ADAM_EPS) + ADAM_WD * w)
    return delta, m, v


def _adamw_sharded(own, landed, w, m, v, row_tile, block0, name):
    rows, n = w.shape

    def body(own_ref, land_ref, w_ref, m_ref, v_ref, g_out, d_out, m_out, v_out):
        g = own_ref[...]
        for k in range(3):
            g = g + land_ref[k].astype(F32)
        delta, m_new, v_new = _adamw_math(w_ref[...], g, m_ref[...], v_ref[...])
        g_out[...] = g
        d_out[...] = delta
        m_out[...] = m_new
        v_out[...] = v_new

    tile = pl.BlockSpec((row_tile, n), lambda t: (t, 0))
    return pl.pallas_call(
        body, name=name, grid=(rows // row_tile,),
        out_shape=[jax.ShapeDtypeStruct((rows, n), F32)] * 4,
        in_specs=[pl.BlockSpec((None, row_tile, n), lambda t: (0, block0 + t, 0)),
                  pl.BlockSpec((3, row_tile, n), lambda t: (0, block0 + t, 0)),
                  tile, tile, tile],
        out_specs=[tile] * 4,
        compiler_params=pltpu.CompilerParams(dimension_semantics=("arbitrary",)),
    )(own, landed, w, m, v)


def _adamw_small(gathered, gathered_cols, params):
    n_par = len(params)

    def body(*refs):
        g_ref, gc_ref = refs[0], refs[1]
        ins = refs[2:2 + 3 * n_par]
        outs = refs[2 + 3 * n_par:]
        loss_ref = outs[4 * n_par]

        def reduced(ref, row, n_rows):
            g = ref[0, row:row + n_rows, :]
            for d in range(1, N_DEV):
                g = g + ref[d, row:row + n_rows, :]
            return g

        for p, (row, n_rows, sharded, _, _, _) in enumerate(params):
            g = reduced(gc_ref if sharded else g_ref, row, n_rows)
            if row == ROW_NORM_G_X:
                g = g + reduced(g_ref, ROW_NORM_G_META, n_rows)
            w_ref, m_ref, v_ref = ins[3 * p:3 * p + 3]
            delta, m_new, v_new = _adamw_math(w_ref[...], g, m_ref[...], v_ref[...])
            outs[4 * p][...] = g
            outs[4 * p + 1][...] = delta
            outs[4 * p + 2][...] = m_new
            outs[4 * p + 3][...] = v_new
        loss = jnp.sum(reduced(g_ref, ROW_LOSS, 1), axis=1, keepdims=True)
        loss_ref[...] = jnp.broadcast_to(loss, loss_ref.shape)

    out_shape = []
    for (_, _, _, w, _, _) in params:
        out_shape += [jax.ShapeDtypeStruct(w.shape, F32)] * 4
    out_shape.append(jax.ShapeDtypeStruct((1, LANES), F32))
    flat = [a for (_, _, _, w, m, v) in params for a in (w, m, v)]
    return pl.pallas_call(
        body, name="adamw_small", out_shape=out_shape,
        in_specs=[_VMEM] * (2 + len(flat)), out_specs=[_VMEM] * len(out_shape),
    )(gathered, gathered_cols, *flat)


def _pad_rows(a, rows):
    return jnp.concatenate([a, jnp.zeros((rows - a.shape[0], a.shape[1]), a.dtype)], axis=0)


def kernel(x, meta_tokens, norm_g, w_in, conv_a_w, conv_a_b, ln_a_g, ln_a_b, w_a_out, b_a_out, conv_b_w, w_b_out, w_out, final_g, loss_target, m_meta_tokens, m_norm_g, m_w_in, m_conv_a_w, m_conv_a_b, m_ln_a_g, m_ln_a_b, m_w_a_out, m_b_a_out, m_conv_b_w, m_w_b_out, m_w_out, m_final_g, v_meta_tokens, v_norm_g, v_w_in, v_conv_a_w, v_conv_a_b, v_ln_a_g, v_ln_a_b, v_w_a_out, v_b_a_out, v_conv_b_w, v_w_b_out, v_w_out, v_final_g):
    seq = x.shape[1]
    assert x.shape == (1, seq, D_MODEL) and seq % TILE == 0 and w_in.shape == (1, D_MODEL, COLS)
    n_tiles = seq // TILE + 1
    tp = n_tiles * TILE
    pos = jnp.stack([lax.axis_index("x"), lax.axis_index("y"), lax.axis_index("c")]).astype(jnp.int32)
    me = 4 * pos[0] + 2 * pos[1] + pos[2]
    x2d = x[0]
    tgt2d = loss_target[0]

    small = jnp.concatenate([meta_tokens, _pad_rows(conv_a_w[0], 32), _pad_rows(conv_b_w[0], SUBLANES)], axis=0)
    small_all = _all_gather(small, "gather_small")
    small_all = small_all.transpose(1, 0, 2).reshape(small.shape[0], D_MODEL)
    meta_full, conv_a_full, conv_b_full = small_all[0:N_META], small_all[N_META:N_META + 32], small_all[N_META + 32:]
    meta_tile = jnp.concatenate([jnp.zeros((TILE - N_META, D_MODEL), F32), meta_full], axis=0)
    final_g2 = final_g.reshape(1, D_MODEL)

    w_out_shards = [w[0].astype(BF16) for w in (w_a_out, w_b_out, w_out)]
    h, proj, w_in_all, *w_out_all = _gather_norm_proj(pos, x2d, meta_tile, norm_g, w_in[0].astype(BF16),
                                                      w_out_shards, 3)
    w_out_all = [w.reshape(D_MODEL, D_MODEL) for w in w_out_all]
    w_out_all_t = [w.T for w in w_out_all]
    dproj, ds1, lhs, rhs, vec, d_conv_a, d_conv_b = _fused_pass(
        proj, x2d, tgt2d, meta_tile, conv_a_full, conv_a_b, ln_a_g, ln_a_b, b_a_out, conv_b_full, final_g2,
        w_out_all[0], w_out_all[1], w_out_all[2], w_out_all_t[0], w_out_all_t[1], w_out_all_t[2], n_tiles)
    k_tile = tp // 3
    gw_out = _grad_w_out(lhs, rhs, k_tile).reshape(N_DEV, 3 * ROWS_OUT, D_MODEL)
    gw_far, their_out = _grad_w_in_half(pos, h, dproj, k_tile, True, "sibling", (gw_out,), "grad_w_in_far")
    gw_near, their_in = _grad_w_in_half(pos, h, dproj, k_tile, False, "sibling_half", (gw_far,), "grad_w_in_near")
    own_in = _chip_partial(pos, gw_near, their_in, (0,), F32, 256, "rs_own_w_in")
    parts_in = _chip_partial(pos, gw_near, their_in, (1, 2, 3), BF16, 256, "rs_parts_w_in")
    own_out = _chip_partial(pos, gw_out, their_out, (0,), F32, ROWS_OUT, "rs_own_w_out")
    parts_out = _chip_partial(pos, gw_out, their_out, (1, 2, 3), BF16, ROWS_OUT, "rs_parts_w_out")
    x_tile = min(256, seq)
    grad_x, vec_x, land_in = _input_bwd(
        dproj, ds1, x2d, norm_g, w_in_all, x_tile, seq // x_tile, 0, "input_bwd_x", ride=(parts_in,))
    d_meta_tile, vec_meta = _input_bwd(dproj, ds1, meta_tile, norm_g, w_in_all, TILE, 1, n_tiles - 1,
                                       "input_bwd_meta")

    small_g = jnp.concatenate([vec, vec_x, vec_meta, d_conv_a, d_conv_b, d_meta_tile[TILE - N_META:]], axis=0)
    small_g_all, land_out = _all_gather(small_g, "gather_small_grads", ride=(parts_out,))
    small_g_cols = lax.dynamic_slice_in_dim(small_g_all, me * LANES, LANES, axis=2)

    res_in = _adamw_sharded(own_in, land_in, w_in[0], m_w_in[0], v_w_in[0], 128, 0, "adamw_w_in")
    res_out = [
        _adamw_sharded(own_out, land_out, w[0], m[0], v[0], ROWS_OUT, k, f"adamw_w_out{k}")
        for k, (w, m, v) in enumerate([(w_a_out, m_w_a_out, v_w_a_out), (w_b_out, m_w_b_out, v_w_b_out),
                                       (w_out, m_w_out, v_w_out)])]
    params = [
        (ROW_META, N_META, True, meta_tokens, m_meta_tokens, v_meta_tokens),
        (ROW_NORM_G_X, 1, False, norm_g, m_norm_g, v_norm_g),
        (ROW_CONV_A_W, CONV_A, True, conv_a_w[0], m_conv_a_w[0], v_conv_a_w[0]),
        (ROW_CONV_A_B, 1, False, conv_a_b, m_conv_a_b, v_conv_a_b),
        (ROW_LN_G, 1, False, ln_a_g, m_ln_a_g, v_ln_a_g),
        (ROW_LN_B, 1, False, ln_a_b, m_ln_a_b, v_ln_a_b),
        (ROW_B_A_OUT, 1, False, b_a_out, m_b_a_out, v_b_a_out),
        (ROW_CONV_B_W, CONV_B, True, conv_b_w[0], m_conv_b_w[0], v_conv_b_w[0]),
        (ROW_FINAL_G, 1, False, final_g2, m_final_g.reshape(1, D_MODEL), v_final_g.reshape(1, D_MODEL)),
    ]
    res_small = _adamw_small(small_g_all, small_g_cols, params)
    loss = res_small[-1][0, 0]

    def small_res(p, kind, shape):
        return res_small[4 * p + kind].reshape(shape)

    per_weight = []
    for kind in range(4):
        per_weight.append([
            small_res(0, kind, meta_tokens.shape),
            small_res(1, kind, norm_g.shape),
            res_in[kind].reshape(w_in.shape),
            small_res(2, kind, conv_a_w.shape),
            small_res(3, kind, conv_a_b.shape),
            small_res(4, kind, ln_a_g.shape),
            small_res(5, kind, ln_a_b.shape),
            res_out[0][kind].reshape(w_a_out.shape),
            small_res(6, kind, b_a_out.shape),
            small_res(7, kind, conv_b_w.shape),
            res_out[1][kind].reshape(w_b_out.shape),
            res_out[2][kind].reshape(w_out.shape),
            small_res(8, kind, final_g.shape),
        ])
    return (loss, grad_x.reshape(x.shape), *per_weight[0], *per_weight[1], *per_weight[2], *per_weight[3])
```

```python
import functools

import jax
import jax.numpy as jnp
from jax import lax
from jax.experimental import pallas as pl
from jax.experimental.pallas import tpu as pltpu

D_MODEL = 1024
N_META = 16
N_DEV = 8
D_IN = 9 * D_MODEL
COLS = D_IN // N_DEV
ROWS_OUT = D_MODEL // N_DEV
CONV_A = 31
CONV_B = 3
EPS = 1e-6

ADAM_LR = 0.001
ADAM_B1 = 0.9
ADAM_B2 = 0.999
ADAM_EPS = 1e-08
ADAM_WD = 0.01
ADAM_STEP = 10

TILE = 128
LANES = 128
N_CHUNK = D_MODEL // LANES
ROW_CHUNK = 128
HALO = 32
SUBLANES = 8
VMEM_LIMIT = 56 * 1024 * 1024

ROW_FINAL_G, ROW_B_A_OUT, ROW_LN_G, ROW_LN_B, ROW_CONV_A_B, ROW_LOSS = 0, 1, 2, 3, 4, 5
ROW_NORM_G_X, ROW_NORM_G_META = 8, 16
ROW_CONV_A_W, ROW_CONV_B_W, ROW_META = 24, 56, 64
SMALL_ROWS = 80

MESH = pl.DeviceIdType.MESH
_ANY = pl.BlockSpec(memory_space=pl.ANY)
_VMEM = pl.BlockSpec(memory_space=pltpu.VMEM)


def _resident(shape):
    return pl.BlockSpec(shape, lambda *_: (0,) * len(shape), pipeline_mode=pl.Buffered(1))
BF16 = jnp.bfloat16
F32 = jnp.float32


def _sigmoid(v):
    return jax.nn.sigmoid(v)


def _dot(a, b):
    return jnp.dot(a, b, preferred_element_type=F32)


def _dot_nt(a, b):
    return lax.dot_general(a, b, (((1,), (1,)), ((), ())), preferred_element_type=F32)


def _dot_tn(a, b):
    return lax.dot_general(a, b, (((0,), (0,)), ((), ())), preferred_element_type=F32)


def _colsum(v):
    return jnp.sum(v, axis=0, keepdims=True)


def _rowmean(v):
    parts = [v[:, LANES * c:LANES * (c + 1)] for c in range(v.shape[1] // LANES)]
    return jnp.sum(functools.reduce(jnp.add, parts), axis=-1, keepdims=True) * (1.0 / v.shape[1])


def _fold8(v):
    parts = [v[SUBLANES * g:SUBLANES * (g + 1)] for g in range(v.shape[0] // SUBLANES)]
    return functools.reduce(jnp.add, parts)


def _all_gather(shard, name, ride=()):
    m, n = shard.shape
    n_arr = len(ride)
    ride_shapes, ride_sems = _exchange_shapes("chips", ride) if ride else ([], [])

    def body(*refs):
        x_ref, out_ref = refs[0], refs[1 + n_arr]
        send_sems, recv_sems, local_sem = refs[2 + 2 * n_arr:5 + 2 * n_arr]
        riding = _chip_copies(refs[1:1 + n_arr], refs[2 + n_arr:2 + 2 * n_arr], *refs[5 + 2 * n_arr:]) if ride else []
        for cp in riding:
            cp.start()
        x, y, c = lax.axis_index("x"), lax.axis_index("y"), lax.axis_index("c")
        me, sibling = (x, y, c), (x, y, 1 - c)
        chips = [(1 - x, y), (x, 1 - y), (1 - x, 1 - y)]

        def slot(px, py, pc):
            return out_ref.at[4 * px + 2 * py + pc]

        def copy(k, block, to, src=None):
            return pltpu.make_async_remote_copy(
                src_ref=slot(*block) if src is None else src, dst_ref=slot(*block),
                send_sem=send_sems.at[k], recv_sem=recv_sems.at[k], device_id=to, device_id_type=MESH)

        mine = pltpu.make_async_copy(x_ref, slot(*me), local_sem)
        mine.start()
        first = [copy(0, me, sibling, src=x_ref)]
        first += [copy(1 + j, me, (*chip, c), src=x_ref) for j, chip in enumerate(chips)]
        for cp in first:
            cp.start()
        passed = [copy(4 + j, (*chip, c), sibling) for j, chip in enumerate(chips)]
        for j, chip in enumerate(chips):
            copy(1 + j, (*chip, c), me).wait_recv()
            passed[j].start()
        copy(0, sibling, me).wait_recv()
        for j, chip in enumerate(chips):
            copy(4 + j, (*chip, 1 - c), me).wait_recv()
        for cp in first + passed:
            cp.wait_send()
        mine.wait()
        for cp in riding:
            cp.wait()

    res = pl.pallas_call(
        body, name=name,
        out_shape=[jax.ShapeDtypeStruct((N_DEV, m, n), shard.dtype)] + ride_shapes,
        in_specs=[_ANY] * (1 + n_arr), out_specs=[_ANY] * (1 + n_arr),
        scratch_shapes=[pltpu.SemaphoreType.DMA((7,)), pltpu.SemaphoreType.DMA((7,)), pltpu.SemaphoreType.DMA(())]
        + ride_sems,
    )(shard, *ride)
    return res if ride else res[0]


def _sibling_copies(srcs, dsts, send_sems, recv_sems):
    x, y, c = lax.axis_index("x"), lax.axis_index("y"), lax.axis_index("c")
    return [pltpu.make_async_remote_copy(
        src_ref=src.at[2 * q + (1 - c)], dst_ref=dst.at[q],
        send_sem=send_sems.at[4 * a + q], recv_sem=recv_sems.at[4 * a + q],
        device_id=(x, y, 1 - c), device_id_type=MESH)
        for a, (src, dst) in enumerate(zip(srcs, dsts)) for q in range(4)]


def _chip_copies(srcs, dsts, send_sems, recv_sems):
    x, y, c = lax.axis_index("x"), lax.axis_index("y"), lax.axis_index("c")
    targets = [(x, 1 - y, c), (1 - x, y, c), (1 - x, 1 - y, c)]
    return [pltpu.make_async_remote_copy(
        src_ref=src.at[k], dst_ref=dst.at[k],
        send_sem=send_sems.at[3 * a + k], recv_sem=recv_sems.at[3 * a + k],
        device_id=targets[k], device_id_type=MESH)
        for a, (src, dst) in enumerate(zip(srcs, dsts)) for k in range(3)]


def _sibling_half_copies(srcs, dsts, send_sems, recv_sems):
    x, y, c = lax.axis_index("x"), lax.axis_index("y"), lax.axis_index("c")
    return [pltpu.make_async_remote_copy(
        src_ref=src.at[q], dst_ref=dst.at[q],
        send_sem=send_sems.at[4 * a + q], recv_sem=recv_sems.at[4 * a + q],
        device_id=(x, y, 1 - c), device_id_type=MESH)
        for a, (src, dst) in enumerate(zip(srcs, dsts)) for q in range(4)]


_EXCHANGES = {"sibling": (4, _sibling_copies, 4), "sibling_half": (4, _sibling_half_copies, 4),
              "chips": (3, _chip_copies, 3)}


def _exchange_shapes(kind, arrays):
    per_array, _, slots = _EXCHANGES[kind]
    out_shape = [jax.ShapeDtypeStruct((slots,) + a.shape[1:], a.dtype) for a in arrays]
    sems = [pltpu.SemaphoreType.DMA((per_array * len(arrays),))] * 2
    return out_shape, sems


def _exchange(kind, arrays, name):
    n_arr = len(arrays)
    out_shape, sems = _exchange_shapes(kind, arrays)

    def body(*refs):
        copies = _EXCHANGES[kind][1](refs[:n_arr], refs[n_arr:2 * n_arr], *refs[2 * n_arr:])
        for cp in copies:
            cp.start()
        for cp in copies:
            cp.wait()

    return pl.pallas_call(body, name=name, out_shape=out_shape, in_specs=[_ANY] * n_arr,
                          out_specs=[_ANY] * n_arr, scratch_shapes=sems)(*arrays)


def _ride_shapes(rides):
    shapes, sems = [], []
    for kind, arrays in rides:
        ride_shapes, ride_sems = _exchange_shapes(kind, arrays)
        shapes += ride_shapes
        sems += ride_sems
    return shapes, sems


def _riding(body, n_in, n_out, rides, is_first, is_last):
    counts = [len(arrays) for _, arrays in rides]
    n_arr = sum(counts)

    def wrapped(*refs):
        ins, srcs = refs[:n_in], refs[n_in:n_in + n_arr]
        outs = refs[n_in + n_arr:n_in + n_arr + n_out]
        dsts = refs[n_in + n_arr + n_out:n_in + 2 * n_arr + n_out]
        first_sem = len(refs) - 2 * len(rides)
        scratch, sems = refs[n_in + 2 * n_arr + n_out:first_sem], refs[first_sem:]

        def copies():
            made, at = [], 0
            for r, ((kind, _), n) in enumerate(zip(rides, counts)):
                made += _EXCHANGES[kind][1](srcs[at:at + n], dsts[at:at + n], sems[2 * r], sems[2 * r + 1])
                at += n
            return made

        @pl.when(is_first())
        def _():
            for cp in copies():
                cp.start()

        body(*ins, *outs, *scratch)

        @pl.when(is_last())
        def _():
            for cp in copies():
                cp.wait()

    return wrapped


def _chip_partial(pos, mine, theirs, relations, out_dtype, row_tile, name):
    n_slots, m, n = mine.shape
    q0 = relations[0]

    def chip_of(qi, pos_ref):
        q = qi + q0
        return pos_ref[0] ^ (q >> 1), pos_ref[1] ^ (q & 1)

    def mine_map(qi, t, pos_ref):
        px, py = chip_of(qi, pos_ref)
        return (4 * px + 2 * py + pos_ref[2] if n_slots == N_DEV else 2 * px + py), t, 0

    def theirs_map(qi, t, pos_ref):
        px, py = chip_of(qi, pos_ref)
        return 2 * px + py, t, 0

    def body(pos_ref, a_ref, b_ref, o_ref):
        o_ref[...] = (a_ref[...] + b_ref[...]).astype(out_dtype)

    return pl.pallas_call(
        body, name=name,
        out_shape=jax.ShapeDtypeStruct((len(relations), m, n), out_dtype),
        grid_spec=pltpu.PrefetchScalarGridSpec(
            num_scalar_prefetch=1, grid=(len(relations), m // row_tile),
            in_specs=[pl.BlockSpec((None, row_tile, n), mine_map), pl.BlockSpec((None, row_tile, n), theirs_map)],
            out_specs=pl.BlockSpec((None, row_tile, n), lambda qi, t, pos_ref: (qi, t, 0))),
        compiler_params=pltpu.CompilerParams(dimension_semantics=("arbitrary", "arbitrary")),
    )(pos, mine, theirs)


PARTS = ((0, 512), (512, 640))


def _gather_norm_proj(pos, x2d, meta_tile, norm_g, w_in_shard, w_out_shards, n_chunk):
    seq = x2d.shape[0]
    n_tiles = seq // TILE + 1
    tp = n_tiles * TILE
    n_parts = len(PARTS)
    widest = max(width for _, width in PARTS)
    units = ([(s, u) for s in range(2) for u in range(n_parts)]
             + [(2 + j, u) for u in range(n_parts) for j in range(3)]
             + [(5 + j, u) for u in range(n_parts) for j in range(3)])
    n_units = len(units)
    n_steps = n_tiles + n_units
    chunk = tp // n_chunk

    def body(pos_ref, x_ref, meta_ref, g_ref, win_ref, wa_ref, wb_ref, wo_ref,
             h_ref, proj_ref, win_all, wa_all, wb_all, wo_all,
             h_all, wbuf, rbuf, send_sems, recv_sems, local_sems):
        g = pl.program_id(0)
        x, y, c = lax.axis_index("x"), lax.axis_index("y"), lax.axis_index("c")
        me, sibling = (x, y, c), (x, y, 1 - c)
        chips = [(1 - x, y), (x, 1 - y), (1 - x, 1 - y)]
        shards = (win_ref, wa_ref, wb_ref, wo_ref)
        gathered = (win_all, wa_all, wb_all, wo_all)
        blocks = [me, sibling] + [(*chip, c) for chip in chips] + [(*chip, 1 - c) for chip in chips]

        def index(block):
            px, py, pc = block
            return 4 * px + 2 * py + pc

        def part(ref, a, u):
            return ref.at[:, pl.ds(PARTS[u][0], PARTS[u][1])] if a == 0 else ref

        def slot(a, block, u):
            return part(gathered[a].at[index(block)], a, u)

        def sem(a, k, u):
            return n_parts * k + u if a == 0 else 7 * n_parts + 7 * (a - 1) + k

        def copy(a, k, block, to, u=0, from_shard=False):
            return pltpu.make_async_remote_copy(
                src_ref=part(shards[a], a, u) if from_shard else slot(a, block, u), dst_ref=slot(a, block, u),
                send_sem=send_sems.at[sem(a, k, u)], recv_sem=recv_sems.at[sem(a, k, u)],
                device_id=to, device_id_type=MESH)

        def keep(a):
            return pltpu.make_async_copy(shards[a], gathered[a].at[index(me)], local_sems.at[a])

        def load(m):
            s, u = units[m]
            src = part(win_ref, 0, u) if s == 0 else slot(0, blocks[s], u)
            return pltpu.make_async_copy(src, wbuf.at[m % 2, :, 0:PARTS[u][1]], local_sems.at[4 + m % 2])

        def store(m):
            s, u = units[m]
            col0 = pl.multiple_of(index(blocks[s]) * COLS + PARTS[u][0], LANES)
            return pltpu.make_async_copy(rbuf.at[m % 2, :, 0:PARTS[u][1]],
                                         proj_ref.at[:, pl.ds(col0, PARTS[u][1])], local_sems.at[6 + m % 2])

        def arrive(m):
            s, u = units[m]
            if s == 1:
                copy(0, 0, sibling, me, u).wait_recv()
            elif 2 <= s <= 4:
                copy(0, s - 1, blocks[s], me, u).wait_recv()
                copy(0, s + 2, blocks[s], sibling, u).start()
            elif s >= 5:
                copy(0, s - 1, blocks[s], me, u).wait_recv()
                if u == 0:
                    for a in range(1, 4):
                        copy(a, s - 4, blocks[s - 3], me).wait_recv()
                        copy(a, s - 1, blocks[s - 3], sibling).start()

        targets = [sibling] + [(*chip, c) for chip in chips]

        @pl.when(g == 0)
        def _():
            for a in range(4):
                keep(a).start()
            for u in range(n_parts):
                for k, to in enumerate(targets):
                    copy(0, k, me, to, u, from_shard=True).start()
            for a in range(1, 4):
                for k, to in enumerate(targets):
                    copy(a, k, me, to, from_shard=True).start()
            load(0).start()

        @pl.when(g < n_tiles)
        def _():
            s0 = jnp.where(g == n_tiles - 1, meta_ref[...], x_ref[...])
            r = lax.rsqrt(_rowmean(s0 * s0) + EPS)
            h = ((s0 * r) * g_ref[...]).astype(BF16)
            h_ref[...] = h
            h_all[pl.ds(pl.multiple_of(g * TILE, TILE), TILE), :] = h

        for m in range(n_units):
            @pl.when(g == n_tiles + m)
            def _(m=m):
                load(m).wait()
                if m + 1 < n_units:
                    arrive(m + 1)
                    load(m + 1).start()
                if m >= 2:
                    store(m - 2).wait()

        m_now = jnp.maximum(g - n_tiles, 0)
        u_now = jnp.where(m_now < 2 * n_parts, m_now % n_parts, ((m_now - 2 * n_parts) // 3) % n_parts)
        for u, (_, width) in enumerate(PARTS):
            @pl.when((g >= n_tiles) & (u_now == u))
            def _(width=width):
                w = wbuf[m_now % 2, :, 0:width]
                for r in range(n_chunk):
                    rbuf[m_now % 2, r * chunk:(r + 1) * chunk, 0:width] = _dot(h_all[r * chunk:(r + 1) * chunk, :], w)

        for m in range(n_units):
            @pl.when(g == n_tiles + m)
            def _(m=m):
                store(m).start()

        @pl.when(g == n_steps - 1)
        def _():
            store(n_units - 2).wait()
            store(n_units - 1).wait()
            for a in range(1, 4):
                copy(a, 0, sibling, me).wait_recv()
                for j in range(3):
                    copy(a, 4 + j, blocks[5 + j], me).wait_recv()
            for a in range(4):
                for u in range(n_parts if a == 0 else 1):
                    for k, to in enumerate(targets):
                        copy(a, k, me, to, u, from_shard=True).wait_send()
                    for j in range(3):
                        copy(a, 4 + j, blocks[2 + j], sibling, u).wait_send()
                keep(a).wait()

    n_x = n_tiles - 1
    return pl.pallas_call(
        body, name="gather_norm_proj",
        out_shape=[jax.ShapeDtypeStruct((tp, D_MODEL), BF16), jax.ShapeDtypeStruct((tp, D_IN), F32),
                   jax.ShapeDtypeStruct((N_DEV,) + w_in_shard.shape, BF16)]
                  + [jax.ShapeDtypeStruct((N_DEV,) + w.shape, BF16) for w in w_out_shards],
        grid_spec=pltpu.PrefetchScalarGridSpec(
            num_scalar_prefetch=1, grid=(n_steps,),
            in_specs=[pl.BlockSpec((TILE, D_MODEL), lambda g, pos_ref: (jnp.minimum(g, n_x - 1), 0)),
                      _VMEM, _VMEM, _ANY, _ANY, _ANY, _ANY],
            out_specs=[pl.BlockSpec((TILE, D_MODEL), lambda g, pos_ref: (jnp.minimum(g, n_tiles - 1), 0)),
                       _ANY, _ANY, _ANY, _ANY, _ANY],
            scratch_shapes=[pltpu.VMEM((tp, D_MODEL), BF16), pltpu.VMEM((2, D_MODEL, widest), BF16),
                            pltpu.VMEM((2, tp, widest), F32),
                            pltpu.SemaphoreType.DMA((7 * n_parts + 21,)), pltpu.SemaphoreType.DMA((7 * n_parts + 21,)),
                            pltpu.SemaphoreType.DMA((8,))]),
        compiler_params=pltpu.CompilerParams(dimension_semantics=("arbitrary",), vmem_limit_bytes=VMEM_LIMIT),
    )(pos, x2d, meta_tile, norm_g, w_in_shard, *w_out_shards)


C_AVAL, C_AGLU, C_AZ, C_BB, C_BC, C_BX, C_BZ, C_GA, C_GB = (k * D_MODEL for k in range(9))
S_AZ, S_BB, S_BZ, S_GA, S_GB = (k * D_MODEL for k in range(5))


def _fused_pass(proj, x2d, tgt2d, meta_tile, conv_a_w, conv_a_b, ln_a_g, ln_a_b, b_a_out, conv_b_w, final_g,
                w_a, w_b, w_o, w_a_t, w_b_t, w_o_t, n_tiles):
    T = TILE
    tp = n_tiles * T
    inv_d = 1.0 / D_MODEL

    def block_of(tile):
        return jnp.where(tile == 0, n_tiles - 1, tile - 1)

    def cur(i):
        return block_of(jnp.minimum(i, n_tiles - 1))

    def prev(i):
        return block_of(jnp.clip(i - 1, 0, n_tiles - 1))

    def xblk(i):
        return jnp.maximum(jnp.minimum(i, n_tiles - 1) - 1, 0)

    def body(proj_ref, x_ref, tgt_ref, meta_ref, caw_ref, cab_ref, lng_ref, lnb_ref, bao_ref, cbw_ref, fg_ref,
             wa_ref, wb_ref, wo_ref, wat_ref, wbt_ref, wot_ref,
             dproj_ref, ds1_ref, lhs_ref, rhs_ref, vec_ref, dcaw_ref, dcbw_ref,
             ua0_buf, cb_buf, dua1_buf, dc3_buf, aprev, cprev, stage, ua1_buf, c3_buf, xhat_buf, rstd_buf,
             dpa_buf, dpb_buf, dcaw8, dcbw8, shift_buf):
        i = pl.program_id(0)

        @pl.when(i == 0)
        def _init():
            for buf in (ua0_buf, cb_buf, dua1_buf, dc3_buf, aprev, cprev, dcaw8, dcbw8):
                buf[...] = jnp.zeros(buf.shape, buf.dtype)
            vec_ref[...] = jnp.zeros(vec_ref.shape, F32)

        @pl.when(i >= 1)
        def _emit_stage():
            dproj_ref[:, C_AZ:C_BC] = stage[:, S_AZ:S_BZ]
            dproj_ref[:, C_BZ:D_IN] = stage[:, S_BZ:S_GB + D_MODEL]

        @pl.when(i < n_tiles)
        def _front():
            def conv_chunk(cc, carry):
                c0 = pl.multiple_of(cc * LANES, LANES)
                lanes = pl.ds(c0, LANES)

                def col(base):
                    return pl.ds(pl.multiple_of(base + cc * LANES, LANES), LANES)

                ua0 = proj_ref[:, col(C_AVAL)] * _sigmoid(proj_ref[:, col(C_AGLU)])
                ua0_buf[T:2 * T, lanes] = ua0
                acc = jnp.broadcast_to(cab_ref[:, lanes], (T, LANES))
                lead = HALO - (CONV_A - 1)
                for r in range(SUBLANES):
                    taps = [k for k in range(CONV_A) if (k + lead) % SUBLANES == r]
                    rows = T + SUBLANES * max((k + lead) // SUBLANES for k in taps)
                    if r:
                        shift_buf[r, 0:rows, :] = ua0_buf[pl.ds(T - HALO + r, rows), lanes]
                    for k in taps:
                        q = (k + lead) // SUBLANES
                        if r:
                            win = shift_buf[r, SUBLANES * q:SUBLANES * q + T, :]
                        else:
                            win = ua0_buf[pl.ds(T - HALO + SUBLANES * q, T), lanes]
                        acc = acc + caw_ref[k:k + 1, lanes] * win
                ua1_buf[:, lanes] = acc
                cb = proj_ref[:, col(C_BC)] * proj_ref[:, col(C_BX)]
                cb_buf[T:2 * T, lanes] = cb
                acc3 = cbw_ref[0:1, lanes] * cb_buf[pl.ds(T - 2, T), lanes]
                for k in range(1, CONV_B):
                    acc3 = acc3 + cbw_ref[k:k + 1, lanes] * cb_buf[pl.ds(T - (CONV_B - 1) + k, T), lanes]
                c3_buf[:, lanes] = acc3
                return carry

            lax.fori_loop(0, N_CHUNK, conv_chunk, 0)

            def gate_rows(rc, carry):
                rows = pl.ds(pl.multiple_of(rc * ROW_CHUNK, ROW_CHUNK), ROW_CHUNK)
                ua1 = ua1_buf[rows, :]
                xc = ua1 - _rowmean(ua1)
                rstd = lax.rsqrt(_rowmean(xc * xc) + EPS)
                xhat = xc * rstd
                xhat_buf[rows, :] = xhat
                rstd_buf[rows, :] = rstd
                ua2 = xhat * lng_ref[...] + lnb_ref[...]
                ua3 = ua2 * _sigmoid(ua2)
                a_z = proj_ref[rows, C_AZ:C_AZ + D_MODEL]
                lhs_ref[0, rows, :] = (ua3 * (a_z * _sigmoid(a_z))).astype(BF16)
                b_z = proj_ref[rows, C_BZ:C_BZ + D_MODEL]
                ub = proj_ref[rows, C_BB:C_BB + D_MODEL] * c3_buf[rows, :]
                lhs_ref[1, rows, :] = (ub * (b_z * _sigmoid(b_z))).astype(BF16)
                return carry

            lax.fori_loop(0, T // ROW_CHUNK, gate_rows, 0)

            ya = _dot(lhs_ref[0], wa_ref[...]) + bao_ref[...]
            yb = _dot(lhs_ref[1], wb_ref[...])
            sga = _sigmoid(proj_ref[:, C_GA:C_GA + D_MODEL])
            sgb = _sigmoid(proj_ref[:, C_GB:C_GB + D_MODEL])
            m_b = (sga * ya + sgb * yb).astype(BF16)
            lhs_ref[2] = m_b
            s0 = jnp.where(i == 0, meta_ref[...], x_ref[...])
            s1 = s0 + _dot(m_b, wo_ref[...])
            r1 = lax.rsqrt(_rowmean(s1 * s1) + EPS)
            y = (s1 * r1) * fg_ref[...]
            is_token = (i >= 1).astype(F32)
            err = (y - tgt_ref[...]) * is_token
            vec_ref[ROW_LOSS:ROW_LOSS + 1, :] += (0.5 * inv_d) * _colsum(err * err)
            dy = err * inv_d
            vec_ref[ROW_FINAL_G:ROW_FINAL_G + 1, :] += _colsum(dy * (s1 * r1))
            gy = dy * fg_ref[...]
            ds1 = r1 * gy - s1 * ((r1 * r1 * r1) * _rowmean(gy * s1))
            ds1_ref[...] = ds1
            ds1_b = ds1.astype(BF16)
            rhs_ref[2] = ds1_b
            dm = _dot(ds1_b, wot_ref[...])
            dya = dm * sga
            dyb = dm * sgb
            stage[:, S_GA:S_GA + D_MODEL] = (dm * ya * (sga * (1.0 - sga))).astype(BF16)
            stage[:, S_GB:S_GB + D_MODEL] = (dm * yb * (sgb * (1.0 - sgb))).astype(BF16)
            vec_ref[ROW_B_A_OUT:ROW_B_A_OUT + 1, :] += _colsum(dya)
            dya_b = dya.astype(BF16)
            dyb_b = dyb.astype(BF16)
            rhs_ref[0] = dya_b
            rhs_ref[1] = dyb_b
            dpa_buf[...] = _dot(dya_b, wat_ref[...])
            dpb_buf[...] = _dot(dyb_b, wbt_ref[...])

            def gate_rows_bwd(rc, carry):
                r0 = pl.multiple_of(rc * ROW_CHUNK, ROW_CHUNK)
                rows = pl.ds(r0, ROW_CHUNK)
                later = pl.ds(pl.multiple_of(T + rc * ROW_CHUNK, ROW_CHUNK), ROW_CHUNK)
                xhat = xhat_buf[rows, :]
                ua2 = xhat * lng_ref[...] + lnb_ref[...]
                sg2 = _sigmoid(ua2)
                ua3 = ua2 * sg2
                a_z = proj_ref[rows, C_AZ:C_AZ + D_MODEL]
                sz = _sigmoid(a_z)
                dpa = dpa_buf[rows, :]
                stage[rows, S_AZ:S_AZ + D_MODEL] = (dpa * ua3 * (sz * (1.0 + a_z * (1.0 - sz)))).astype(BF16)
                dua2 = dpa * (a_z * sz) * (sg2 * (1.0 + ua2 * (1.0 - sg2)))
                vec_ref[ROW_LN_G:ROW_LN_G + 1, :] += _colsum(dua2 * xhat)
                vec_ref[ROW_LN_B:ROW_LN_B + 1, :] += _colsum(dua2)
                dxh = dua2 * lng_ref[...]
                dua1 = rstd_buf[rows, :] * (dxh - _rowmean(dxh) - xhat * _rowmean(dxh * xhat))
                vec_ref[ROW_CONV_A_B:ROW_CONV_A_B + 1, :] += _colsum(dua1)
                dua1_buf[later, :] = dua1
                b_z = proj_ref[rows, C_BZ:C_BZ + D_MODEL]
                sbz = _sigmoid(b_z)
                b_b = proj_ref[rows, C_BB:C_BB + D_MODEL]
                c3 = c3_buf[rows, :]
                dpb = dpb_buf[rows, :]
                stage[rows, S_BZ:S_BZ + D_MODEL] = (dpb * (b_b * c3) * (sbz * (1.0 + b_z * (1.0 - sbz)))).astype(BF16)
                dub = dpb * (b_z * sbz)
                stage[rows, S_BB:S_BB + D_MODEL] = (dub * c3).astype(BF16)
                dc3_buf[later, :] = dub * b_b
                return carry

            lax.fori_loop(0, T // ROW_CHUNK, gate_rows_bwd, 0)

        @pl.when(i == n_tiles)
        def _no_later_tile():
            dua1_buf[T:2 * T, :] = jnp.zeros((T, D_MODEL), F32)
            dc3_buf[T:2 * T, :] = jnp.zeros((T, D_MODEL), F32)

        @pl.when(i >= 1)
        def _lagged():
            def convt_chunk(cc, carry):
                c0 = pl.multiple_of(cc * LANES, LANES)
                lanes = pl.ds(c0, LANES)

                def col(base):
                    return pl.ds(pl.multiple_of(base + cc * LANES, LANES), LANES)

                ua0 = ua0_buf[0:T, lanes]
                acc = jnp.zeros((T, LANES), F32)
                for r in range(SUBLANES):
                    shifts = [j for j in range(CONV_A) if j % SUBLANES == r]
                    rows = T + shifts[-1] - r
                    if r:
                        shift_buf[r, 0:rows, :] = dua1_buf[pl.ds(r, rows), lanes]
                    for j in shifts:
                        k = CONV_A - 1 - j
                        if r:
                            later = shift_buf[r, j - r:j - r + T, :]
                        else:
                            later = dua1_buf[pl.ds(j, T), lanes]
                        acc = acc + caw_ref[k:k + 1, lanes] * later
                        dcaw8[SUBLANES * k:SUBLANES * (k + 1), lanes] += _fold8(ua0 * later)
                a_val = aprev[:, col(0)]
                sg = _sigmoid(aprev[:, col(D_MODEL)])
                dproj_ref[:, col(C_AVAL)] = (acc * sg).astype(BF16)
                dproj_ref[:, col(C_AGLU)] = (acc * a_val * (sg * (1.0 - sg))).astype(BF16)

                cb = cb_buf[0:T, lanes]
                acc3 = jnp.zeros((T, LANES), F32)
                for j in range(CONV_B):
                    k = CONV_B - 1 - j
                    later = dc3_buf[pl.ds(j, T), lanes]
                    acc3 = acc3 + cbw_ref[k:k + 1, lanes] * later
                    dcbw8[SUBLANES * k:SUBLANES * (k + 1), lanes] += _fold8(cb * later)
                dproj_ref[:, col(C_BC)] = (acc3 * cprev[:, col(D_MODEL)]).astype(BF16)
                dproj_ref[:, col(C_BX)] = (acc3 * cprev[:, col(0)]).astype(BF16)
                return carry

            lax.fori_loop(0, N_CHUNK, convt_chunk, 0)

        for buf in (ua0_buf, cb_buf, dua1_buf, dc3_buf):
            buf[0:T, :] = buf[T:2 * T, :]
        aprev[...] = proj_ref[:, C_AVAL:C_AZ]
        cprev[...] = proj_ref[:, C_BC:C_BZ]

        @pl.when(i == n_tiles)
        def _finish():
            for k in range(CONV_A):
                dcaw_ref[k:k + 1, :] = _colsum(dcaw8[SUBLANES * k:SUBLANES * (k + 1), :])
            dcaw_ref[CONV_A:CONV_A + 1, :] = jnp.zeros((1, D_MODEL), F32)
            for k in range(CONV_B):
                dcbw_ref[k:k + 1, :] = _colsum(dcbw8[SUBLANES * k:SUBLANES * (k + 1), :])
            dcbw_ref[CONV_B:SUBLANES, :] = jnp.zeros((SUBLANES - CONV_B, D_MODEL), F32)

    tile_in = lambda width: pl.BlockSpec((T, width), lambda i: (cur(i), 0))
    return pl.pallas_call(
        body, name="fused_pass", grid=(n_tiles + 1,),
        out_shape=[
            jax.ShapeDtypeStruct((tp, D_IN), BF16),
            jax.ShapeDtypeStruct((tp, D_MODEL), F32),
            jax.ShapeDtypeStruct((3, tp, D_MODEL), BF16),
            jax.ShapeDtypeStruct((3, tp, D_MODEL), BF16),
            jax.ShapeDtypeStruct((SUBLANES, D_MODEL), F32),
            jax.ShapeDtypeStruct((32, D_MODEL), F32),
            jax.ShapeDtypeStruct((SUBLANES, D_MODEL), F32),
        ],
        in_specs=[
            tile_in(D_IN),
            pl.BlockSpec((T, D_MODEL), lambda i: (xblk(i), 0)),
            pl.BlockSpec((T, D_MODEL), lambda i: (xblk(i), 0)),
            _VMEM, _VMEM, _VMEM, _VMEM, _VMEM, _VMEM, _VMEM, _VMEM,
            *[_resident((D_MODEL, D_MODEL)) for _ in range(6)],
        ],
        out_specs=[
            pl.BlockSpec((T, D_IN), lambda i: (prev(i), 0)),
            pl.BlockSpec((T, D_MODEL), lambda i: (cur(i), 0)),
            pl.BlockSpec((3, T, D_MODEL), lambda i: (0, cur(i), 0)),
            pl.BlockSpec((3, T, D_MODEL), lambda i: (0, cur(i), 0)),
            _VMEM, _VMEM, _VMEM,
        ],
        scratch_shapes=[
            pltpu.VMEM((2 * T, D_MODEL), F32),
            pltpu.VMEM((2 * T, D_MODEL), F32),
            pltpu.VMEM((2 * T, D_MODEL), F32),
            pltpu.VMEM((2 * T, D_MODEL), F32),
            pltpu.VMEM((T, 2 * D_MODEL), F32),
            pltpu.VMEM((T, 2 * D_MODEL), F32),
            pltpu.VMEM((T, 5 * D_MODEL), BF16),
            pltpu.VMEM((T, D_MODEL), F32),
            pltpu.VMEM((T, D_MODEL), F32),
            pltpu.VMEM((T, D_MODEL), F32),
            pltpu.VMEM((T, 1), F32),
            pltpu.VMEM((T, D_MODEL), F32),
            pltpu.VMEM((T, D_MODEL), F32),
            pltpu.VMEM((32 * SUBLANES, D_MODEL), F32),
            pltpu.VMEM((SUBLANES * SUBLANES, D_MODEL), F32),
            pltpu.VMEM((SUBLANES, T + HALO, LANES), F32),
        ],
        compiler_params=pltpu.CompilerParams(dimension_semantics=("arbitrary",), vmem_limit_bytes=VMEM_LIMIT),
    )(proj, x2d, tgt2d, meta_tile, conv_a_w, conv_a_b, ln_a_g, ln_a_b, b_a_out, conv_b_w, final_g,
      w_a, w_b, w_o, w_a_t, w_b_t, w_o_t)


def _input_bwd(dproj, ds1, s0, norm_g, w_in_all, row_tile, n_tiles, block0, name, ride=()):
    def body(dp_ref, ds1_ref, s0_ref, g_ref, w_ref, out_ref, vec_ref):
        t = pl.program_id(0)

        @pl.when(t == 0)
        def _():
            vec_ref[...] = jnp.zeros(vec_ref.shape, F32)

        dh = _dot_nt(dp_ref[:, 0:COLS], w_ref[0])
        for j in range(1, N_DEV):
            dh = dh + _dot_nt(dp_ref[:, j * COLS:(j + 1) * COLS], w_ref[j])
        s0v = s0_ref[...]
        r = lax.rsqrt(_rowmean(s0v * s0v) + EPS)
        gh = dh * g_ref[...]
        out_ref[...] = ds1_ref[...] + r * gh - s0v * ((r * r * r) * _rowmean(gh * s0v))
        vec_ref[0:1, :] += _colsum(dh * (s0v * r))

    n_arr = len(ride)
    ride_shapes, ride_sems = _ride_shapes([("chips", ride)]) if ride else ([], [])
    if ride:
        body = _riding(body, 5, 2, [("chips", ride)], lambda: pl.program_id(0) == 0,
                       lambda: pl.program_id(0) == n_tiles - 1)
    return pl.pallas_call(
        body, name=name, grid=(n_tiles,),
        out_shape=[jax.ShapeDtypeStruct(s0.shape, F32), jax.ShapeDtypeStruct((SUBLANES, D_MODEL), F32)] + ride_shapes,
        in_specs=[pl.BlockSpec((row_tile, D_IN), lambda t: (block0 + t, 0)),
                  pl.BlockSpec((row_tile, D_MODEL), lambda t: (block0 + t, 0)),
                  pl.BlockSpec((row_tile, D_MODEL), lambda t: (t, 0)),
                  _VMEM, _resident((N_DEV, D_MODEL, COLS))] + [_ANY] * n_arr,
        out_specs=[pl.BlockSpec((row_tile, D_MODEL), lambda t: (t, 0)), _VMEM] + [_ANY] * n_arr,
        scratch_shapes=ride_sems,
        compiler_params=pltpu.CompilerParams(dimension_semantics=("arbitrary",), vmem_limit_bytes=VMEM_LIMIT),
    )(dproj, ds1, s0, norm_g, w_in_all, *ride)


def _grad_w_in_half(pos, h, dproj, k_tile, other_side, rides, name):
    tp = h.shape[0]
    n_k = tp // k_tile

    def column_block(q, k, pos_ref):
        return k, 2 * q + (1 - pos_ref[2] if other_side else pos_ref[2])

    def body(pos_ref, h_ref, dp_ref, o_ref):
        @pl.when(pl.program_id(1) == 0)
        def _():
            o_ref[...] = jnp.zeros(o_ref.shape, F32)

        o_ref[...] += _dot_tn(h_ref[...], dp_ref[...])

    ride = [a for _, arrays in rides for a in arrays]
    n_arr = len(ride)
    ride_shapes, ride_sems = _ride_shapes(rides)
    body = _riding(body, 3, 1, rides,
                   lambda: (pl.program_id(0) == 0) & (pl.program_id(1) == 0),
                   lambda: (pl.program_id(0) == 3) & (pl.program_id(1) == n_k - 1))
    return pl.pallas_call(
        body, name=name,
        out_shape=[jax.ShapeDtypeStruct((4, D_MODEL, COLS), F32)] + ride_shapes,
        grid_spec=pltpu.PrefetchScalarGridSpec(
            num_scalar_prefetch=1, grid=(4, n_k),
            in_specs=[pl.BlockSpec((k_tile, D_MODEL), lambda q, k, pos_ref: (k, 0)),
                      pl.BlockSpec((k_tile, COLS), column_block)] + [_ANY] * n_arr,
            out_specs=[pl.BlockSpec((None, D_MODEL, COLS), lambda q, k, pos_ref: (q, 0, 0))] + [_ANY] * n_arr,
            scratch_shapes=ride_sems),
        compiler_params=pltpu.CompilerParams(dimension_semantics=("arbitrary", "arbitrary"),
                                             vmem_limit_bytes=VMEM_LIMIT),
    )(pos, h, dproj, *ride)


def _grad_w_out(lhs, rhs, k_tile):
    tp = lhs.shape[1]

    def body(a_ref, b_ref, o_ref):
        @pl.when(pl.program_id(1) == 0)
        def _():
            o_ref[...] = jnp.zeros(o_ref.shape, F32)

        o_ref[...] += _dot_tn(a_ref[...], b_ref[...]).reshape(N_DEV, ROWS_OUT, D_MODEL)

    return pl.pallas_call(
        body, name="grad_w_out", grid=(3, tp // k_tile),
        out_shape=jax.ShapeDtypeStruct((N_DEV, 3, ROWS_OUT, D_MODEL), F32),
        in_specs=[pl.BlockSpec((None, k_tile, D_MODEL), lambda w, k: (w, k, 0)),
                  pl.BlockSpec((None, k_tile, D_MODEL), lambda w, k: (w, k, 0))],
        out_specs=pl.BlockSpec((N_DEV, None, ROWS_OUT, D_MODEL), lambda w, k: (0, w, 0, 0)),
        compiler_params=pltpu.CompilerParams(dimension_semantics=("arbitrary", "arbitrary"),
                                             vmem_limit_bytes=VMEM_LIMIT),
    )(lhs, rhs)


def _adamw_math(w, g, m, v):
    m = ADAM_B1 * m + (1.0 - ADAM_B1) * g
    v = ADAM_B2 * v + (1.0 - ADAM_B2) * (g * g)
    m_hat = m / (1.0 - ADAM_B1 ** ADAM_STEP)
    v_hat = v / (1.0 - ADAM_B2 ** ADAM_STEP)
    delta = -ADAM_LR * (m_hat / (jnp.sqrt(v_hat) + ADAM_EPS) + ADAM_WD * w)
    return delta, m, v


def _adamw_sharded(own, landed, w, m, v, row_tile, block0, name):
    rows, n = w.shape

    def body(own_ref, land_ref, w_ref, m_ref, v_ref, g_out, d_out, m_out, v_out):
        g = own_ref[...]
        for k in range(3):
            g = g + land_ref[k].astype(F32)
        delta, m_new, v_new = _adamw_math(w_ref[...], g, m_ref[...], v_ref[...])
        g_out[...] = g
        d_out[...] = delta
        m_out[...] = m_new
        v_out[...] = v_new

    tile = pl.BlockSpec((row_tile, n), lambda t: (t, 0))
    return pl.pallas_call(
        body, name=name, grid=(rows // row_tile,),
        out_shape=[jax.ShapeDtypeStruct((rows, n), F32)] * 4,
        in_specs=[pl.BlockSpec((None, row_tile, n), lambda t: (0, block0 + t, 0)),
                  pl.BlockSpec((3, row_tile, n), lambda t: (0, block0 + t, 0)),
                  tile, tile, tile],
        out_specs=[tile] * 4,
        compiler_params=pltpu.CompilerParams(dimension_semantics=("arbitrary",)),
    )(own, landed, w, m, v)


def _adamw_small(gathered, gathered_cols, params):
    n_par = len(params)

    def body(*refs):
        g_ref, gc_ref = refs[0], refs[1]
        ins = refs[2:2 + 3 * n_par]
        outs = refs[2 + 3 * n_par:]
        loss_ref = outs[4 * n_par]

        def reduced(ref, row, n_rows):
            g = ref[0, row:row + n_rows, :]
            for d in range(1, N_DEV):
                g = g + ref[d, row:row + n_rows, :]
            return g

        for p, (row, n_rows, sharded, _, _, _) in enumerate(params):
            g = reduced(gc_ref if sharded else g_ref, row, n_rows)
            if row == ROW_NORM_G_X:
                g = g + reduced(g_ref, ROW_NORM_G_META, n_rows)
            w_ref, m_ref, v_ref = ins[3 * p:3 * p + 3]
            delta, m_new, v_new = _adamw_math(w_ref[...], g, m_ref[...], v_ref[...])
            outs[4 * p][...] = g
            outs[4 * p + 1][...] = delta
            outs[4 * p + 2][...] = m_new
            outs[4 * p + 3][...] = v_new
        loss = jnp.sum(reduced(g_ref, ROW_LOSS, 1), axis=1, keepdims=True)
        loss_ref[...] = jnp.broadcast_to(loss, loss_ref.shape)

    out_shape = []
    for (_, _, _, w, _, _) in params:
        out_shape += [jax.ShapeDtypeStruct(w.shape, F32)] * 4
    out_shape.append(jax.ShapeDtypeStruct((1, LANES), F32))
    flat = [a for (_, _, _, w, m, v) in params for a in (w, m, v)]
    return pl.pallas_call(
        body, name="adamw_small", out_shape=out_shape,
        in_specs=[_VMEM] * (2 + len(flat)), out_specs=[_VMEM] * len(out_shape),
    )(gathered, gathered_cols, *flat)


def _pad_rows(a, rows):
    return jnp.concatenate([a, jnp.zeros((rows - a.shape[0], a.shape[1]), a.dtype)], axis=0)


def kernel(x, meta_tokens, norm_g, w_in, conv_a_w, conv_a_b, ln_a_g, ln_a_b, w_a_out, b_a_out, conv_b_w, w_b_out, w_out, final_g, loss_target, m_meta_tokens, m_norm_g, m_w_in, m_conv_a_w, m_conv_a_b, m_ln_a_g, m_ln_a_b, m_w_a_out, m_b_a_out, m_conv_b_w, m_w_b_out, m_w_out, m_final_g, v_meta_tokens, v_norm_g, v_w_in, v_conv_a_w, v_conv_a_b, v_ln_a_g, v_ln_a_b, v_w_a_out, v_b_a_out, v_conv_b_w, v_w_b_out, v_w_out, v_final_g):
    seq = x.shape[1]
    assert x.shape == (1, seq, D_MODEL) and seq % TILE == 0 and w_in.shape == (1, D_MODEL, COLS)
    n_tiles = seq // TILE + 1
    tp = n_tiles * TILE
    pos = jnp.stack([lax.axis_index("x"), lax.axis_index("y"), lax.axis_index("c")]).astype(jnp.int32)
    me = 4 * pos[0] + 2 * pos[1] + pos[2]
    x2d = x[0]
    tgt2d = loss_target[0]

    small = jnp.concatenate([meta_tokens, _pad_rows(conv_a_w[0], 32), _pad_rows(conv_b_w[0], SUBLANES)], axis=0)
    small_all = _all_gather(small, "gather_small")
    small_all = small_all.transpose(1, 0, 2).reshape(small.shape[0], D_MODEL)
    meta_full, conv_a_full, conv_b_full = small_all[0:N_META], small_all[N_META:N_META + 32], small_all[N_META + 32:]
    meta_tile = jnp.concatenate([jnp.zeros((TILE - N_META, D_MODEL), F32), meta_full], axis=0)
    final_g2 = final_g.reshape(1, D_MODEL)

    w_out_shards = [w[0].astype(BF16) for w in (w_a_out, w_b_out, w_out)]
    h, proj, w_in_all, *w_out_all = _gather_norm_proj(pos, x2d, meta_tile, norm_g, w_in[0].astype(BF16),
                                                      w_out_shards, 3)
    w_out_all = [w.reshape(D_MODEL, D_MODEL) for w in w_out_all]
    w_out_all_t = [w.T for w in w_out_all]
    dproj, ds1, lhs, rhs, vec, d_conv_a, d_conv_b = _fused_pass(
        proj, x2d, tgt2d, meta_tile, conv_a_full, conv_a_b, ln_a_g, ln_a_b, b_a_out, conv_b_full, final_g2,
        w_out_all[0], w_out_all[1], w_out_all[2], w_out_all_t[0], w_out_all_t[1], w_out_all_t[2], n_tiles)
    k_tile = tp // 3
    gw_out = _grad_w_out(lhs, rhs, k_tile).reshape(N_DEV, 3 * ROWS_OUT, D_MODEL)
    gw_far, their_out = _grad_w_in_half(pos, h, dproj, k_tile, True, [("sibling", (gw_out,))], "grad_w_in_far")
    own_out = _chip_partial(pos, gw_out, their_out, (0,), F32, ROWS_OUT, "rs_own_w_out")
    parts_out = _chip_partial(pos, gw_out, their_out, (1, 2, 3), BF16, ROWS_OUT, "rs_parts_w_out")
    gw_near, their_in, land_out = _grad_w_in_half(
        pos, h, dproj, k_tile, False, [("sibling_half", (gw_far,)), ("chips", (parts_out,))], "grad_w_in_near")
    own_in = _chip_partial(pos, gw_near, their_in, (0,), F32, 256, "rs_own_w_in")
    parts_in = _chip_partial(pos, gw_near, their_in, (1, 2, 3), BF16, 256, "rs_parts_w_in")
    x_tile = min(256, seq)
    grad_x, vec_x, land_in = _input_bwd(
        dproj, ds1, x2d, norm_g, w_in_all, x_tile, seq // x_tile, 0, "input_bwd_x", ride=(parts_in,))
    d_meta_tile, vec_meta = _input_bwd(dproj, ds1, meta_tile, norm_g, w_in_all, TILE, 1, n_tiles - 1,
                                       "input_bwd_meta")

    small_g = jnp.concatenate([vec, vec_x, vec_meta, d_conv_a, d_conv_b, d_meta_tile[TILE - N_META:]], axis=0)
    small_g_all = _all_gather(small_g, "gather_small_grads")
    small_g_cols = lax.dynamic_slice_in_dim(small_g_all, me * LANES, LANES, axis=2)

    res_in = _adamw_sharded(own_in, land_in, w_in[0], m_w_in[0], v_w_in[0], 128, 0, "adamw_w_in")
    res_out = [
        _adamw_sharded(own_out, land_out, w[0], m[0], v[0], ROWS_OUT, k, f"adamw_w_out{k}")
        for k, (w, m, v) in enumerate([(w_a_out, m_w_a_out, v_w_a_out), (w_b_out, m_w_b_out, v_w_b_out),
                                       (w_out, m_w_out, v_w_out)])]
    params = [
        (ROW_META, N_META, True, meta_tokens, m_meta_tokens, v_meta_tokens),
        (ROW_NORM_G_X, 1, False, norm_g, m_norm_g, v_norm_g),
        (ROW_CONV_A_W, CONV_A, True, conv_a_w[0], m_conv_a_w[0], v_conv_a_w[0]),
        (ROW_CONV_A_B, 1, False, conv_a_b, m_conv_a_b, v_conv_a_b),
        (ROW_LN_G, 1, False, ln_a_g, m_ln_a_g, v_ln_a_g),
        (ROW_LN_B, 1, False, ln_a_b, m_ln_a_b, v_ln_a_b),
        (ROW_B_A_OUT, 1, False, b_a_out, m_b_a_out, v_b_a_out),
        (ROW_CONV_B_W, CONV_B, True, conv_b_w[0], m_conv_b_w[0], v_conv_b_w[0]),
        (ROW_FINAL_G, 1, False, final_g2, m_final_g.reshape(1, D_MODEL), v_final_g.reshape(1, D_MODEL)),
    ]
    res_small = _adamw_small(small_g_all, small_g_cols, params)
    loss = res_small[-1][0, 0]

    def small_res(p, kind, shape):
        return res_small[4 * p + kind].reshape(shape)

    per_weight = []
    for kind in range(4):
        per_weight.append([
            small_res(0, kind, meta_tokens.shape),
            small_res(1, kind, norm_g.shape),
            res_in[kind].reshape(w_in.shape),
            small_res(2, kind, conv_a_w.shape),
            small_res(3, kind, conv_a_b.shape),
            small_res(4, kind, ln_a_g.shape),
            small_res(5, kind, ln_a_b.shape),
            res_out[0][kind].reshape(w_a_out.shape),
            small_res(6, kind, b_a_out.shape),
            small_res(7, kind, conv_b_w.shape),
            res_out[1][kind].reshape(w_b_out.shape),
            res_out[2][kind].reshape(w_out.shape),
            small_res(8, kind, final_g.shape),
        ])
    return (loss, grad_x.reshape(x.shape), *per_weight[0], *per_weight[1], *per_weight[2], *per_weight[3])
```

```python
import functools

import jax
import jax.numpy as jnp
from jax import lax
from jax.experimental import pallas as pl
from jax.experimental.pallas import tpu as pltpu

D_MODEL = 1024
N_META = 16
N_DEV = 8
D_IN = 9 * D_MODEL
COLS = D_IN // N_DEV
ROWS_OUT = D_MODEL // N_DEV
CONV_A = 31
CONV_B = 3
EPS = 1e-6

ADAM_LR = 0.001
ADAM_B1 = 0.9
ADAM_B2 = 0.999
ADAM_EPS = 1e-08
ADAM_WD = 0.01
ADAM_STEP = 10

TILE = 128
LANES = 128
N_CHUNK = D_MODEL // LANES
ROW_CHUNK = 128
HALO = 32
SUBLANES = 8
VMEM_LIMIT = 56 * 1024 * 1024

ROW_FINAL_G, ROW_B_A_OUT, ROW_LN_G, ROW_LN_B, ROW_CONV_A_B, ROW_LOSS = 0, 1, 2, 3, 4, 5
ROW_NORM_G_X, ROW_NORM_G_META = 8, 16
ROW_CONV_A_W, ROW_CONV_B_W, ROW_META = 24, 56, 64
SMALL_ROWS = 80

MESH = pl.DeviceIdType.MESH
_ANY = pl.BlockSpec(memory_space=pl.ANY)
_VMEM = pl.BlockSpec(memory_space=pltpu.VMEM)


def _resident(shape):
    return pl.BlockSpec(shape, lambda *_: (0,) * len(shape), pipeline_mode=pl.Buffered(1))
BF16 = jnp.bfloat16
F32 = jnp.float32


def _sigmoid(v):
    return jax.nn.sigmoid(v)


def _dot(a, b):
    return jnp.dot(a, b, preferred_element_type=F32)


def _dot_nt(a, b):
    return lax.dot_general(a, b, (((1,), (1,)), ((), ())), preferred_element_type=F32)


def _dot_tn(a, b):
    return lax.dot_general(a, b, (((0,), (0,)), ((), ())), preferred_element_type=F32)


def _colsum(v):
    return jnp.sum(v, axis=0, keepdims=True)


def _rowmean(v):
    parts = [v[:, LANES * c:LANES * (c + 1)] for c in range(v.shape[1] // LANES)]
    return jnp.sum(functools.reduce(jnp.add, parts), axis=-1, keepdims=True) * (1.0 / v.shape[1])


def _fold8(v):
    parts = [v[SUBLANES * g:SUBLANES * (g + 1)] for g in range(v.shape[0] // SUBLANES)]
    return functools.reduce(jnp.add, parts)


def _all_gather(shard, name, ride=()):
    m, n = shard.shape
    n_arr = len(ride)
    ride_shapes, ride_sems = _exchange_shapes("chips", ride) if ride else ([], [])

    def body(*refs):
        x_ref, out_ref = refs[0], refs[1 + n_arr]
        send_sems, recv_sems, local_sem = refs[2 + 2 * n_arr:5 + 2 * n_arr]
        riding = _chip_copies(refs[1:1 + n_arr], refs[2 + n_arr:2 + 2 * n_arr], *refs[5 + 2 * n_arr:]) if ride else []
        for cp in riding:
            cp.start()
        x, y, c = lax.axis_index("x"), lax.axis_index("y"), lax.axis_index("c")
        me, sibling = (x, y, c), (x, y, 1 - c)
        chips = [(1 - x, y), (x, 1 - y), (1 - x, 1 - y)]

        def slot(px, py, pc):
            return out_ref.at[4 * px + 2 * py + pc]

        def copy(k, block, to, src=None):
            return pltpu.make_async_remote_copy(
                src_ref=slot(*block) if src is None else src, dst_ref=slot(*block),
                send_sem=send_sems.at[k], recv_sem=recv_sems.at[k], device_id=to, device_id_type=MESH)

        mine = pltpu.make_async_copy(x_ref, slot(*me), local_sem)
        mine.start()
        first = [copy(0, me, sibling, src=x_ref)]
        first += [copy(1 + j, me, (*chip, c), src=x_ref) for j, chip in enumerate(chips)]
        for cp in first:
            cp.start()
        passed = [copy(4 + j, (*chip, c), sibling) for j, chip in enumerate(chips)]
        for j, chip in enumerate(chips):
            copy(1 + j, (*chip, c), me).wait_recv()
            passed[j].start()
        copy(0, sibling, me).wait_recv()
        for j, chip in enumerate(chips):
            copy(4 + j, (*chip, 1 - c), me).wait_recv()
        for cp in first + passed:
            cp.wait_send()
        mine.wait()
        for cp in riding:
            cp.wait()

    res = pl.pallas_call(
        body, name=name,
        out_shape=[jax.ShapeDtypeStruct((N_DEV, m, n), shard.dtype)] + ride_shapes,
        in_specs=[_ANY] * (1 + n_arr), out_specs=[_ANY] * (1 + n_arr),
        scratch_shapes=[pltpu.SemaphoreType.DMA((7,)), pltpu.SemaphoreType.DMA((7,)), pltpu.SemaphoreType.DMA(())]
        + ride_sems,
    )(shard, *ride)
    return res if ride else res[0]


def _sibling_copies(srcs, dsts, send_sems, recv_sems):
    x, y, c = lax.axis_index("x"), lax.axis_index("y"), lax.axis_index("c")
    return [pltpu.make_async_remote_copy(
        src_ref=src.at[2 * q + (1 - c)], dst_ref=dst.at[q],
        send_sem=send_sems.at[4 * a + q], recv_sem=recv_sems.at[4 * a + q],
        device_id=(x, y, 1 - c), device_id_type=MESH)
        for a, (src, dst) in enumerate(zip(srcs, dsts)) for q in range(4)]


def _chip_copies(srcs, dsts, send_sems, recv_sems):
    x, y, c = lax.axis_index("x"), lax.axis_index("y"), lax.axis_index("c")
    targets = [(x, 1 - y, c), (1 - x, y, c), (1 - x, 1 - y, c)]
    return [pltpu.make_async_remote_copy(
        src_ref=src.at[k], dst_ref=dst.at[k],
        send_sem=send_sems.at[3 * a + k], recv_sem=recv_sems.at[3 * a + k],
        device_id=targets[k], device_id_type=MESH)
        for a, (src, dst) in enumerate(zip(srcs, dsts)) for k in range(3)]


def _sibling_half_copies(srcs, dsts, send_sems, recv_sems):
    x, y, c = lax.axis_index("x"), lax.axis_index("y"), lax.axis_index("c")
    return [pltpu.make_async_remote_copy(
        src_ref=src.at[q], dst_ref=dst.at[q],
        send_sem=send_sems.at[4 * a + q], recv_sem=recv_sems.at[4 * a + q],
        device_id=(x, y, 1 - c), device_id_type=MESH)
        for a, (src, dst) in enumerate(zip(srcs, dsts)) for q in range(4)]


_EXCHANGES = {"sibling": (4, _sibling_copies, 4), "sibling_half": (4, _sibling_half_copies, 4),
              "chips": (3, _chip_copies, 3)}


def _exchange_shapes(kind, arrays):
    per_array, _, slots = _EXCHANGES[kind]
    out_shape = [jax.ShapeDtypeStruct((slots,) + a.shape[1:], a.dtype) for a in arrays]
    sems = [pltpu.SemaphoreType.DMA((per_array * len(arrays),))] * 2
    return out_shape, sems


def _ride_shapes(rides):
    shapes, sems = [], []
    for kind, arrays in rides:
        ride_shapes, ride_sems = _exchange_shapes(kind, arrays)
        shapes += ride_shapes
        sems += ride_sems
    return shapes, sems


def _riding(body, n_in, n_out, rides, is_first, is_last):
    counts = [len(arrays) for _, arrays in rides]
    n_arr = sum(counts)

    def wrapped(*refs):
        ins, srcs = refs[:n_in], refs[n_in:n_in + n_arr]
        outs = refs[n_in + n_arr:n_in + n_arr + n_out]
        dsts = refs[n_in + n_arr + n_out:n_in + 2 * n_arr + n_out]
        first_sem = len(refs) - 2 * len(rides)
        scratch, sems = refs[n_in + 2 * n_arr + n_out:first_sem], refs[first_sem:]

        def copies():
            made, at = [], 0
            for r, ((kind, _), n) in enumerate(zip(rides, counts)):
                made += _EXCHANGES[kind][1](srcs[at:at + n], dsts[at:at + n], sems[2 * r], sems[2 * r + 1])
                at += n
            return made

        @pl.when(is_first())
        def _():
            for cp in copies():
                cp.start()

        body(*ins, *outs, *scratch)

        @pl.when(is_last())
        def _():
            for cp in copies():
                cp.wait()

    return wrapped


def _chip_partial(pos, mine, theirs, relations, out_dtype, row_tile, name):
    n_slots, m, n = mine.shape
    q0 = relations[0]

    def chip_of(qi, pos_ref):
        q = qi + q0
        return pos_ref[0] ^ (q >> 1), pos_ref[1] ^ (q & 1)

    def mine_map(qi, t, pos_ref):
        px, py = chip_of(qi, pos_ref)
        return (4 * px + 2 * py + pos_ref[2] if n_slots == N_DEV else 2 * px + py), t, 0

    def theirs_map(qi, t, pos_ref):
        px, py = chip_of(qi, pos_ref)
        return 2 * px + py, t, 0

    def body(pos_ref, a_ref, b_ref, o_ref):
        o_ref[...] = (a_ref[...] + b_ref[...]).astype(out_dtype)

    return pl.pallas_call(
        body, name=name,
        out_shape=jax.ShapeDtypeStruct((len(relations), m, n), out_dtype),
        grid_spec=pltpu.PrefetchScalarGridSpec(
            num_scalar_prefetch=1, grid=(len(relations), m // row_tile),
            in_specs=[pl.BlockSpec((None, row_tile, n), mine_map), pl.BlockSpec((None, row_tile, n), theirs_map)],
            out_specs=pl.BlockSpec((None, row_tile, n), lambda qi, t, pos_ref: (qi, t, 0))),
        compiler_params=pltpu.CompilerParams(dimension_semantics=("arbitrary", "arbitrary")),
    )(pos, mine, theirs)


PARTS = ((0, 512), (512, 640))


def _gather_norm_proj(pos, x2d, meta_tile, norm_g, w_in_shard, w_out_shards, n_chunk):
    seq = x2d.shape[0]
    n_tiles = seq // TILE + 1
    tp = n_tiles * TILE
    n_parts = len(PARTS)
    widest = max(width for _, width in PARTS)
    units = ([(s, u) for s in range(2) for u in range(n_parts)]
             + [(2 + j, u) for u in range(n_parts) for j in range(3)]
             + [(5 + j, u) for u in range(n_parts) for j in range(3)])
    n_units = len(units)
    n_steps = n_tiles + n_units
    chunk = tp // n_chunk

    def body(pos_ref, x_ref, meta_ref, g_ref, win_ref, wa_ref, wb_ref, wo_ref,
             ht_ref, proj_ref, win_all, wa_all, wb_all, wo_all,
             h_all, wbuf, rbuf, send_sems, recv_sems, local_sems):
        g = pl.program_id(0)
        x, y, c = lax.axis_index("x"), lax.axis_index("y"), lax.axis_index("c")
        me, sibling = (x, y, c), (x, y, 1 - c)
        chips = [(1 - x, y), (x, 1 - y), (1 - x, 1 - y)]
        shards = (win_ref, wa_ref, wb_ref, wo_ref)
        gathered = (win_all, wa_all, wb_all, wo_all)
        blocks = [me, sibling] + [(*chip, c) for chip in chips] + [(*chip, 1 - c) for chip in chips]

        def index(block):
            px, py, pc = block
            return 4 * px + 2 * py + pc

        def part(ref, a, u):
            return ref.at[:, pl.ds(PARTS[u][0], PARTS[u][1])] if a == 0 else ref

        def slot(a, block, u):
            return part(gathered[a].at[index(block)], a, u)

        def sem(a, k, u):
            return n_parts * k + u if a == 0 else 7 * n_parts + 7 * (a - 1) + k

        def copy(a, k, block, to, u=0, from_shard=False):
            return pltpu.make_async_remote_copy(
                src_ref=part(shards[a], a, u) if from_shard else slot(a, block, u), dst_ref=slot(a, block, u),
                send_sem=send_sems.at[sem(a, k, u)], recv_sem=recv_sems.at[sem(a, k, u)],
                device_id=to, device_id_type=MESH)

        def keep(a):
            return pltpu.make_async_copy(shards[a], gathered[a].at[index(me)], local_sems.at[a])

        def load(m):
            s, u = units[m]
            src = part(win_ref, 0, u) if s == 0 else slot(0, blocks[s], u)
            return pltpu.make_async_copy(src, wbuf.at[m % 2, :, 0:PARTS[u][1]], local_sems.at[4 + m % 2])

        def store(m):
            s, u = units[m]
            col0 = pl.multiple_of(index(blocks[s]) * COLS + PARTS[u][0], LANES)
            return pltpu.make_async_copy(rbuf.at[m % 2, :, 0:PARTS[u][1]],
                                         proj_ref.at[:, pl.ds(col0, PARTS[u][1])], local_sems.at[6 + m % 2])

        def arrive(m):
            s, u = units[m]
            if s == 1:
                copy(0, 0, sibling, me, u).wait_recv()
            elif 2 <= s <= 4:
                copy(0, s - 1, blocks[s], me, u).wait_recv()
                copy(0, s + 2, blocks[s], sibling, u).start()
            elif s >= 5:
                copy(0, s - 1, blocks[s], me, u).wait_recv()
                if u == 0:
                    for a in range(1, 4):
                        copy(a, s - 4, blocks[s - 3], me).wait_recv()
                        copy(a, s - 1, blocks[s - 3], sibling).start()

        targets = [sibling] + [(*chip, c) for chip in chips]

        @pl.when(g == 0)
        def _():
            for a in range(4):
                keep(a).start()
            for u in range(n_parts):
                for k, to in enumerate(targets):
                    copy(0, k, me, to, u, from_shard=True).start()
            for a in range(1, 4):
                for k, to in enumerate(targets):
                    copy(a, k, me, to, from_shard=True).start()
            load(0).start()

        @pl.when(g < n_tiles)
        def _():
            s0 = jnp.where(g == n_tiles - 1, meta_ref[...], x_ref[...])
            r = lax.rsqrt(_rowmean(s0 * s0) + EPS)
            h32 = (s0 * r) * g_ref[...]
            ht_ref[...] = h32.T.astype(BF16)
            h_all[pl.ds(pl.multiple_of(g * TILE, TILE), TILE), :] = h32.astype(BF16)

        for m in range(n_units):
            @pl.when(g == n_tiles + m)
            def _(m=m):
                load(m).wait()
                if m + 1 < n_units:
                    arrive(m + 1)
                    load(m + 1).start()
                if m >= 2:
                    store(m - 2).wait()

        m_now = jnp.maximum(g - n_tiles, 0)
        u_now = jnp.where(m_now < 2 * n_parts, m_now % n_parts, ((m_now - 2 * n_parts) // 3) % n_parts)
        for u, (_, width) in enumerate(PARTS):
            @pl.when((g >= n_tiles) & (u_now == u))
            def _(width=width):
                w = wbuf[m_now % 2, :, 0:width]
                for r in range(n_chunk):
                    rbuf[m_now % 2, r * chunk:(r + 1) * chunk, 0:width] = _dot(h_all[r * chunk:(r + 1) * chunk, :], w)

        for m in range(n_units):
            @pl.when(g == n_tiles + m)
            def _(m=m):
                store(m).start()

        @pl.when(g == n_steps - 1)
        def _():
            store(n_units - 2).wait()
            store(n_units - 1).wait()
            for a in range(1, 4):
                copy(a, 0, sibling, me).wait_recv()
                for j in range(3):
                    copy(a, 4 + j, blocks[5 + j], me).wait_recv()
            for a in range(4):
                for u in range(n_parts if a == 0 else 1):
                    for k, to in enumerate(targets):
                        copy(a, k, me, to, u, from_shard=True).wait_send()
                    for j in range(3):
                        copy(a, 4 + j, blocks[2 + j], sibling, u).wait_send()
                keep(a).wait()

    n_x = n_tiles - 1
    return pl.pallas_call(
        body, name="gather_norm_proj",
        out_shape=[jax.ShapeDtypeStruct((D_MODEL, tp), BF16), jax.ShapeDtypeStruct((tp, D_IN), F32),
                   jax.ShapeDtypeStruct((N_DEV,) + w_in_shard.shape, BF16)]
                  + [jax.ShapeDtypeStruct((N_DEV,) + w.shape, BF16) for w in w_out_shards],
        grid_spec=pltpu.PrefetchScalarGridSpec(
            num_scalar_prefetch=1, grid=(n_steps,),
            in_specs=[pl.BlockSpec((TILE, D_MODEL), lambda g, pos_ref: (jnp.minimum(g, n_x - 1), 0)),
                      _VMEM, _VMEM, _ANY, _ANY, _ANY, _ANY],
            out_specs=[pl.BlockSpec((D_MODEL, TILE), lambda g, pos_ref: (0, jnp.minimum(g, n_tiles - 1))),
                       _ANY, _ANY, _ANY, _ANY, _ANY],
            scratch_shapes=[pltpu.VMEM((tp, D_MODEL), BF16), pltpu.VMEM((2, D_MODEL, widest), BF16),
                            pltpu.VMEM((2, tp, widest), F32),
                            pltpu.SemaphoreType.DMA((7 * n_parts + 21,)), pltpu.SemaphoreType.DMA((7 * n_parts + 21,)),
                            pltpu.SemaphoreType.DMA((8,))]),
        compiler_params=pltpu.CompilerParams(dimension_semantics=("arbitrary",), vmem_limit_bytes=VMEM_LIMIT),
    )(pos, x2d, meta_tile, norm_g, w_in_shard, *w_out_shards)


C_AVAL, C_AGLU, C_AZ, C_BB, C_BC, C_BX, C_BZ, C_GA, C_GB = (k * D_MODEL for k in range(9))
S_AZ, S_BB, S_BZ, S_GA, S_GB = (k * D_MODEL for k in range(5))


def _fused_pass(proj, x2d, tgt2d, meta_tile, conv_a_w, conv_a_b, ln_a_g, ln_a_b, b_a_out, conv_b_w, final_g,
                w_a, w_b, w_o, w_a_t, w_b_t, w_o_t, n_tiles):
    T = TILE
    tp = n_tiles * T
    inv_d = 1.0 / D_MODEL

    def block_of(tile):
        return jnp.where(tile == 0, n_tiles - 1, tile - 1)

    def cur(i):
        return block_of(jnp.minimum(i, n_tiles - 1))

    def prev(i):
        return block_of(jnp.clip(i - 1, 0, n_tiles - 1))

    def xblk(i):
        return jnp.maximum(jnp.minimum(i, n_tiles - 1) - 1, 0)

    def body(proj_ref, x_ref, tgt_ref, meta_ref, caw_ref, cab_ref, lng_ref, lnb_ref, bao_ref, cbw_ref, fg_ref,
             wa_ref, wb_ref, wo_ref, wat_ref, wbt_ref, wot_ref,
             dproj_ref, ds1_ref, lhs_ref, rhs_ref, vec_ref, dcaw_ref, dcbw_ref,
             ua0_buf, cb_buf, dua1_buf, dc3_buf, aprev, cprev, stage, ua1_buf, c3_buf, xhat_buf, rstd_buf,
             dpa_buf, dpb_buf, dcaw8, dcbw8, shift_buf):
        i = pl.program_id(0)

        @pl.when(i == 0)
        def _init():
            for buf in (ua0_buf, cb_buf, dua1_buf, dc3_buf, aprev, cprev, dcaw8, dcbw8):
                buf[...] = jnp.zeros(buf.shape, buf.dtype)
            vec_ref[...] = jnp.zeros(vec_ref.shape, F32)

        @pl.when(i >= 1)
        def _emit_stage():
            dproj_ref[:, C_AZ:C_BC] = stage[:, S_AZ:S_BZ]
            dproj_ref[:, C_BZ:D_IN] = stage[:, S_BZ:S_GB + D_MODEL]

        @pl.when(i < n_tiles)
        def _front():
            def conv_chunk(cc, carry):
                c0 = pl.multiple_of(cc * LANES, LANES)
                lanes = pl.ds(c0, LANES)

                def col(base):
                    return pl.ds(pl.multiple_of(base + cc * LANES, LANES), LANES)

                ua0 = proj_ref[:, col(C_AVAL)] * _sigmoid(proj_ref[:, col(C_AGLU)])
                ua0_buf[T:2 * T, lanes] = ua0
                acc = jnp.broadcast_to(cab_ref[:, lanes], (T, LANES))
                lead = HALO - (CONV_A - 1)
                for r in range(SUBLANES):
                    taps = [k for k in range(CONV_A) if (k + lead) % SUBLANES == r]
                    rows = T + SUBLANES * max((k + lead) // SUBLANES for k in taps)
                    if r:
                        shift_buf[r, 0:rows, :] = ua0_buf[pl.ds(T - HALO + r, rows), lanes]
                    for k in taps:
                        q = (k + lead) // SUBLANES
                        if r:
                            win = shift_buf[r, SUBLANES * q:SUBLANES * q + T, :]
                        else:
                            win = ua0_buf[pl.ds(T - HALO + SUBLANES * q, T), lanes]
                        acc = acc + caw_ref[k:k + 1, lanes] * win
                ua1_buf[:, lanes] = acc
                cb = proj_ref[:, col(C_BC)] * proj_ref[:, col(C_BX)]
                cb_buf[T:2 * T, lanes] = cb
                acc3 = cbw_ref[0:1, lanes] * cb_buf[pl.ds(T - 2, T), lanes]
                for k in range(1, CONV_B):
                    acc3 = acc3 + cbw_ref[k:k + 1, lanes] * cb_buf[pl.ds(T - (CONV_B - 1) + k, T), lanes]
                c3_buf[:, lanes] = acc3
                return carry

            lax.fori_loop(0, N_CHUNK, conv_chunk, 0)

            def gate_rows(rc, carry):
                rows = pl.ds(pl.multiple_of(rc * ROW_CHUNK, ROW_CHUNK), ROW_CHUNK)
                ua1 = ua1_buf[rows, :]
                xc = ua1 - _rowmean(ua1)
                rstd = lax.rsqrt(_rowmean(xc * xc) + EPS)
                xhat = xc * rstd
                xhat_buf[rows, :] = xhat
                rstd_buf[rows, :] = rstd
                ua2 = xhat * lng_ref[...] + lnb_ref[...]
                ua3 = ua2 * _sigmoid(ua2)
                a_z = proj_ref[rows, C_AZ:C_AZ + D_MODEL]
                lhs_ref[0, rows, :] = (ua3 * (a_z * _sigmoid(a_z))).astype(BF16)
                b_z = proj_ref[rows, C_BZ:C_BZ + D_MODEL]
                ub = proj_ref[rows, C_BB:C_BB + D_MODEL] * c3_buf[rows, :]
                lhs_ref[1, rows, :] = (ub * (b_z * _sigmoid(b_z))).astype(BF16)
                return carry

            lax.fori_loop(0, T // ROW_CHUNK, gate_rows, 0)

            ya = _dot(lhs_ref[0], wa_ref[...]) + bao_ref[...]
            yb = _dot(lhs_ref[1], wb_ref[...])
            sga = _sigmoid(proj_ref[:, C_GA:C_GA + D_MODEL])
            sgb = _sigmoid(proj_ref[:, C_GB:C_GB + D_MODEL])
            m_b = (sga * ya + sgb * yb).astype(BF16)
            lhs_ref[2] = m_b
            s0 = jnp.where(i == 0, meta_ref[...], x_ref[...])
            s1 = s0 + _dot(m_b, wo_ref[...])
            r1 = lax.rsqrt(_rowmean(s1 * s1) + EPS)
            y = (s1 * r1) * fg_ref[...]
            is_token = (i >= 1).astype(F32)
            err = (y - tgt_ref[...]) * is_token
            vec_ref[ROW_LOSS:ROW_LOSS + 1, :] += (0.5 * inv_d) * _colsum(err * err)
            dy = err * inv_d
            vec_ref[ROW_FINAL_G:ROW_FINAL_G + 1, :] += _colsum(dy * (s1 * r1))
            gy = dy * fg_ref[...]
            ds1 = r1 * gy - s1 * ((r1 * r1 * r1) * _rowmean(gy * s1))
            ds1_ref[...] = ds1
            ds1_b = ds1.astype(BF16)
            rhs_ref[2] = ds1_b
            dm = _dot(ds1_b, wot_ref[...])
            dya = dm * sga
            dyb = dm * sgb
            stage[:, S_GA:S_GA + D_MODEL] = (dm * ya * (sga * (1.0 - sga))).astype(BF16)
            stage[:, S_GB:S_GB + D_MODEL] = (dm * yb * (sgb * (1.0 - sgb))).astype(BF16)
            vec_ref[ROW_B_A_OUT:ROW_B_A_OUT + 1, :] += _colsum(dya)
            dya_b = dya.astype(BF16)
            dyb_b = dyb.astype(BF16)
            rhs_ref[0] = dya_b
            rhs_ref[1] = dyb_b
            dpa_buf[...] = _dot(dya_b, wat_ref[...])
            dpb_buf[...] = _dot(dyb_b, wbt_ref[...])

            def gate_rows_bwd(rc, carry):
                r0 = pl.multiple_of(rc * ROW_CHUNK, ROW_CHUNK)
                rows = pl.ds(r0, ROW_CHUNK)
                later = pl.ds(pl.multiple_of(T + rc * ROW_CHUNK, ROW_CHUNK), ROW_CHUNK)
                xhat = xhat_buf[rows, :]
                ua2 = xhat * lng_ref[...] + lnb_ref[...]
                sg2 = _sigmoid(ua2)
                ua3 = ua2 * sg2
                a_z = proj_ref[rows, C_AZ:C_AZ + D_MODEL]
                sz = _sigmoid(a_z)
                dpa = dpa_buf[rows, :]
                stage[rows, S_AZ:S_AZ + D_MODEL] = (dpa * ua3 * (sz * (1.0 + a_z * (1.0 - sz)))).astype(BF16)
                dua2 = dpa * (a_z * sz) * (sg2 * (1.0 + ua2 * (1.0 - sg2)))
                vec_ref[ROW_LN_G:ROW_LN_G + 1, :] += _colsum(dua2 * xhat)
                vec_ref[ROW_LN_B:ROW_LN_B + 1, :] += _colsum(dua2)
                dxh = dua2 * lng_ref[...]
                dua1 = rstd_buf[rows, :] * (dxh - _rowmean(dxh) - xhat * _rowmean(dxh * xhat))
                vec_ref[ROW_CONV_A_B:ROW_CONV_A_B + 1, :] += _colsum(dua1)
                dua1_buf[later, :] = dua1
                b_z = proj_ref[rows, C_BZ:C_BZ + D_MODEL]
                sbz = _sigmoid(b_z)
                b_b = proj_ref[rows, C_BB:C_BB + D_MODEL]
                c3 = c3_buf[rows, :]
                dpb = dpb_buf[rows, :]
                stage[rows, S_BZ:S_BZ + D_MODEL] = (dpb * (b_b * c3) * (sbz * (1.0 + b_z * (1.0 - sbz)))).astype(BF16)
                dub = dpb * (b_z * sbz)
                stage[rows, S_BB:S_BB + D_MODEL] = (dub * c3).astype(BF16)
                dc3_buf[later, :] = dub * b_b
                return carry

            lax.fori_loop(0, T // ROW_CHUNK, gate_rows_bwd, 0)

        @pl.when(i == n_tiles)
        def _no_later_tile():
            dua1_buf[T:2 * T, :] = jnp.zeros((T, D_MODEL), F32)
            dc3_buf[T:2 * T, :] = jnp.zeros((T, D_MODEL), F32)

        @pl.when(i >= 1)
        def _lagged():
            def convt_chunk(cc, carry):
                c0 = pl.multiple_of(cc * LANES, LANES)
                lanes = pl.ds(c0, LANES)

                def col(base):
                    return pl.ds(pl.multiple_of(base + cc * LANES, LANES), LANES)

                ua0 = ua0_buf[0:T, lanes]
                acc = jnp.zeros((T, LANES), F32)
                for r in range(SUBLANES):
                    shifts = [j for j in range(CONV_A) if j % SUBLANES == r]
                    rows = T + shifts[-1] - r
                    if r:
                        shift_buf[r, 0:rows, :] = dua1_buf[pl.ds(r, rows), lanes]
                    for j in shifts:
                        k = CONV_A - 1 - j
                        if r:
                            later = shift_buf[r, j - r:j - r + T, :]
                        else:
                            later = dua1_buf[pl.ds(j, T), lanes]
                        acc = acc + caw_ref[k:k + 1, lanes] * later
                        dcaw8[SUBLANES * k:SUBLANES * (k + 1), lanes] += _fold8(ua0 * later)
                a_val = aprev[:, col(0)]
                sg = _sigmoid(aprev[:, col(D_MODEL)])
                dproj_ref[:, col(C_AVAL)] = (acc * sg).astype(BF16)
                dproj_ref[:, col(C_AGLU)] = (acc * a_val * (sg * (1.0 - sg))).astype(BF16)

                cb = cb_buf[0:T, lanes]
                acc3 = jnp.zeros((T, LANES), F32)
                for j in range(CONV_B):
                    k = CONV_B - 1 - j
                    later = dc3_buf[pl.ds(j, T), lanes]
                    acc3 = acc3 + cbw_ref[k:k + 1, lanes] * later
                    dcbw8[SUBLANES * k:SUBLANES * (k + 1), lanes] += _fold8(cb * later)
                dproj_ref[:, col(C_BC)] = (acc3 * cprev[:, col(D_MODEL)]).astype(BF16)
                dproj_ref[:, col(C_BX)] = (acc3 * cprev[:, col(0)]).astype(BF16)
                return carry

            lax.fori_loop(0, N_CHUNK, convt_chunk, 0)

        for buf in (ua0_buf, cb_buf, dua1_buf, dc3_buf):
            buf[0:T, :] = buf[T:2 * T, :]
        aprev[...] = proj_ref[:, C_AVAL:C_AZ]
        cprev[...] = proj_ref[:, C_BC:C_BZ]

        @pl.when(i == n_tiles)
        def _finish():
            for k in range(CONV_A):
                dcaw_ref[k:k + 1, :] = _colsum(dcaw8[SUBLANES * k:SUBLANES * (k + 1), :])
            dcaw_ref[CONV_A:CONV_A + 1, :] = jnp.zeros((1, D_MODEL), F32)
            for k in range(CONV_B):
                dcbw_ref[k:k + 1, :] = _colsum(dcbw8[SUBLANES * k:SUBLANES * (k + 1), :])
            dcbw_ref[CONV_B:SUBLANES, :] = jnp.zeros((SUBLANES - CONV_B, D_MODEL), F32)

    tile_in = lambda width: pl.BlockSpec((T, width), lambda i: (cur(i), 0))
    return pl.pallas_call(
        body, name="fused_pass", grid=(n_tiles + 1,),
        out_shape=[
            jax.ShapeDtypeStruct((tp, D_IN), BF16),
            jax.ShapeDtypeStruct((tp, D_MODEL), F32),
            jax.ShapeDtypeStruct((3, tp, D_MODEL), BF16),
            jax.ShapeDtypeStruct((3, tp, D_MODEL), BF16),
            jax.ShapeDtypeStruct((SUBLANES, D_MODEL), F32),
            jax.ShapeDtypeStruct((32, D_MODEL), F32),
            jax.ShapeDtypeStruct((SUBLANES, D_MODEL), F32),
        ],
        in_specs=[
            tile_in(D_IN),
            pl.BlockSpec((T, D_MODEL), lambda i: (xblk(i), 0)),
            pl.BlockSpec((T, D_MODEL), lambda i: (xblk(i), 0)),
            _VMEM, _VMEM, _VMEM, _VMEM, _VMEM, _VMEM, _VMEM, _VMEM,
            *[_resident((D_MODEL, D_MODEL)) for _ in range(6)],
        ],
        out_specs=[
            pl.BlockSpec((T, D_IN), lambda i: (prev(i), 0)),
            pl.BlockSpec((T, D_MODEL), lambda i: (cur(i), 0)),
            pl.BlockSpec((3, T, D_MODEL), lambda i: (0, cur(i), 0)),
            pl.BlockSpec((3, T, D_MODEL), lambda i: (0, cur(i), 0)),
            _VMEM, _VMEM, _VMEM,
        ],
        scratch_shapes=[
            pltpu.VMEM((2 * T, D_MODEL), F32),
            pltpu.VMEM((2 * T, D_MODEL), F32),
            pltpu.VMEM((2 * T, D_MODEL), F32),
            pltpu.VMEM((2 * T, D_MODEL), F32),
            pltpu.VMEM((T, 2 * D_MODEL), F32),
            pltpu.VMEM((T, 2 * D_MODEL), F32),
            pltpu.VMEM((T, 5 * D_MODEL), BF16),
            pltpu.VMEM((T, D_MODEL), F32),
            pltpu.VMEM((T, D_MODEL), F32),
            pltpu.VMEM((T, D_MODEL), F32),
            pltpu.VMEM((T, 1), F32),
            pltpu.VMEM((T, D_MODEL), F32),
            pltpu.VMEM((T, D_MODEL), F32),
            pltpu.VMEM((32 * SUBLANES, D_MODEL), F32),
            pltpu.VMEM((SUBLANES * SUBLANES, D_MODEL), F32),
            pltpu.VMEM((SUBLANES, T + HALO, LANES), F32),
        ],
        compiler_params=pltpu.CompilerParams(dimension_semantics=("arbitrary",), vmem_limit_bytes=VMEM_LIMIT),
    )(proj, x2d, tgt2d, meta_tile, conv_a_w, conv_a_b, ln_a_g, ln_a_b, b_a_out, conv_b_w, final_g,
      w_a, w_b, w_o, w_a_t, w_b_t, w_o_t)


def _input_bwd(dproj, ds1, s0, norm_g, w_in_all, row_tile, n_tiles, block0, name, ride=()):
    def body(dp_ref, ds1_ref, s0_ref, g_ref, w_ref, out_ref, vec_ref):
        t = pl.program_id(0)

        @pl.when(t == 0)
        def _():
            vec_ref[...] = jnp.zeros(vec_ref.shape, F32)

        dh = _dot_nt(dp_ref[:, 0:COLS], w_ref[0])
        for j in range(1, N_DEV):
            dh = dh + _dot_nt(dp_ref[:, j * COLS:(j + 1) * COLS], w_ref[j])
        s0v = s0_ref[...]
        r = lax.rsqrt(_rowmean(s0v * s0v) + EPS)
        gh = dh * g_ref[...]
        out_ref[...] = ds1_ref[...] + r * gh - s0v * ((r * r * r) * _rowmean(gh * s0v))
        vec_ref[0:1, :] += _colsum(dh * (s0v * r))

    n_arr = len(ride)
    ride_shapes, ride_sems = _ride_shapes([("chips", ride)]) if ride else ([], [])
    if ride:
        body = _riding(body, 5, 2, [("chips", ride)], lambda: pl.program_id(0) == 0,
                       lambda: pl.program_id(0) == n_tiles - 1)
    return pl.pallas_call(
        body, name=name, grid=(n_tiles,),
        out_shape=[jax.ShapeDtypeStruct(s0.shape, F32), jax.ShapeDtypeStruct((SUBLANES, D_MODEL), F32)] + ride_shapes,
        in_specs=[pl.BlockSpec((row_tile, D_IN), lambda t: (block0 + t, 0)),
                  pl.BlockSpec((row_tile, D_MODEL), lambda t: (block0 + t, 0)),
                  pl.BlockSpec((row_tile, D_MODEL), lambda t: (t, 0)),
                  _VMEM, _resident((N_DEV, D_MODEL, COLS))] + [_ANY] * n_arr,
        out_specs=[pl.BlockSpec((row_tile, D_MODEL), lambda t: (t, 0)), _VMEM] + [_ANY] * n_arr,
        scratch_shapes=ride_sems,
        compiler_params=pltpu.CompilerParams(dimension_semantics=("arbitrary",), vmem_limit_bytes=VMEM_LIMIT),
    )(dproj, ds1, s0, norm_g, w_in_all, *ride)


def _grad_w_in_half(pos, h_t, dproj, k_tile, other_side, rides, name):
    tp = h_t.shape[1]
    n_k = tp // k_tile

    def column_block(q, k, pos_ref):
        return k, 2 * q + (1 - pos_ref[2] if other_side else pos_ref[2])

    def body(pos_ref, h_ref, dp_ref, o_ref):
        @pl.when(pl.program_id(1) == 0)
        def _():
            o_ref[...] = jnp.zeros(o_ref.shape, F32)

        o_ref[...] += _dot(h_ref[...], dp_ref[...])

    ride = [a for _, arrays in rides for a in arrays]
    n_arr = len(ride)
    ride_shapes, ride_sems = _ride_shapes(rides)
    body = _riding(body, 3, 1, rides,
                   lambda: (pl.program_id(0) == 0) & (pl.program_id(1) == 0),
                   lambda: (pl.program_id(0) == 3) & (pl.program_id(1) == n_k - 1))
    return pl.pallas_call(
        body, name=name,
        out_shape=[jax.ShapeDtypeStruct((4, D_MODEL, COLS), F32)] + ride_shapes,
        grid_spec=pltpu.PrefetchScalarGridSpec(
            num_scalar_prefetch=1, grid=(4, n_k),
            in_specs=[pl.BlockSpec((D_MODEL, k_tile), lambda q, k, pos_ref: (0, k)),
                      pl.BlockSpec((k_tile, COLS), column_block)] + [_ANY] * n_arr,
            out_specs=[pl.BlockSpec((None, D_MODEL, COLS), lambda q, k, pos_ref: (q, 0, 0))] + [_ANY] * n_arr,
            scratch_shapes=ride_sems),
        compiler_params=pltpu.CompilerParams(dimension_semantics=("arbitrary", "arbitrary"),
                                             vmem_limit_bytes=VMEM_LIMIT),
    )(pos, h_t, dproj, *ride)


def _grad_w_out(lhs, rhs, k_tile):
    tp = lhs.shape[1]

    def body(a_ref, b_ref, o_ref):
        @pl.when(pl.program_id(1) == 0)
        def _():
            o_ref[...] = jnp.zeros(o_ref.shape, F32)

        o_ref[...] += _dot_tn(a_ref[...], b_ref[...]).reshape(N_DEV, ROWS_OUT, D_MODEL)

    return pl.pallas_call(
        body, name="grad_w_out", grid=(3, tp // k_tile),
        out_shape=jax.ShapeDtypeStruct((N_DEV, 3, ROWS_OUT, D_MODEL), F32),
        in_specs=[pl.BlockSpec((None, k_tile, D_MODEL), lambda w, k: (w, k, 0)),
                  pl.BlockSpec((None, k_tile, D_MODEL), lambda w, k: (w, k, 0))],
        out_specs=pl.BlockSpec((N_DEV, None, ROWS_OUT, D_MODEL), lambda w, k: (0, w, 0, 0)),
        compiler_params=pltpu.CompilerParams(dimension_semantics=("arbitrary", "arbitrary"),
                                             vmem_limit_bytes=VMEM_LIMIT),
    )(lhs, rhs)


def _adamw_math(w, g, m, v):
    m = ADAM_B1 * m + (1.0 - ADAM_B1) * g
    v = ADAM_B2 * v + (1.0 - ADAM_B2) * (g * g)
    m_hat = m / (1.0 - ADAM_B1 ** ADAM_STEP)
    v_hat = v / (1.0 - ADAM_B2 ** ADAM_STEP)
    delta = -ADAM_LR * (m_hat / (jnp.sqrt(v_hat) + ADAM_EPS) + ADAM_WD * w)
    return delta, m, v


def _adamw_sharded(pos, mine, theirs, landed, w, m, v, row_tile, block0, name):
    rows, n = w.shape
    n_slots = mine.shape[0]

    def mine_map(t, pos_ref):
        chip = 2 * pos_ref[0] + pos_ref[1]
        return (2 * chip + pos_ref[2] if n_slots == N_DEV else chip), block0 + t, 0

    def theirs_map(t, pos_ref):
        return 2 * pos_ref[0] + pos_ref[1], block0 + t, 0

    def body(pos_ref, mine_ref, theirs_ref, land_ref, w_ref, m_ref, v_ref, g_out, d_out, m_out, v_out):
        g = mine_ref[...] + theirs_ref[...]
        for k in range(3):
            g = g + land_ref[k].astype(F32)
        delta, m_new, v_new = _adamw_math(w_ref[...], g, m_ref[...], v_ref[...])
        g_out[...] = g
        d_out[...] = delta
        m_out[...] = m_new
        v_out[...] = v_new

    tile = pl.BlockSpec((row_tile, n), lambda t, pos_ref: (t, 0))
    return pl.pallas_call(
        body, name=name,
        out_shape=[jax.ShapeDtypeStruct((rows, n), F32)] * 4,
        grid_spec=pltpu.PrefetchScalarGridSpec(
            num_scalar_prefetch=1, grid=(rows // row_tile,),
            in_specs=[pl.BlockSpec((None, row_tile, n), mine_map), pl.BlockSpec((None, row_tile, n), theirs_map),
                      pl.BlockSpec((3, row_tile, n), lambda t, pos_ref: (0, block0 + t, 0)), tile, tile, tile],
            out_specs=[tile] * 4),
        compiler_params=pltpu.CompilerParams(dimension_semantics=("arbitrary",)),
    )(pos, mine, theirs, landed, w, m, v)


def _adamw_small(gathered, gathered_cols, params):
    n_par = len(params)

    def body(*refs):
        g_ref, gc_ref = refs[0], refs[1]
        ins = refs[2:2 + 3 * n_par]
        outs = refs[2 + 3 * n_par:]
        loss_ref = outs[4 * n_par]

        def reduced(ref, row, n_rows):
            g = ref[0, row:row + n_rows, :]
            for d in range(1, N_DEV):
                g = g + ref[d, row:row + n_rows, :]
            return g

        for p, (row, n_rows, sharded, _, _, _) in enumerate(params):
            g = reduced(gc_ref if sharded else g_ref, row, n_rows)
            if row == ROW_NORM_G_X:
                g = g + reduced(g_ref, ROW_NORM_G_META, n_rows)
            w_ref, m_ref, v_ref = ins[3 * p:3 * p + 3]
            delta, m_new, v_new = _adamw_math(w_ref[...], g, m_ref[...], v_ref[...])
            outs[4 * p][...] = g
            outs[4 * p + 1][...] = delta
            outs[4 * p + 2][...] = m_new
            outs[4 * p + 3][...] = v_new
        loss = jnp.sum(reduced(g_ref, ROW_LOSS, 1), axis=1, keepdims=True)
        loss_ref[...] = jnp.broadcast_to(loss, loss_ref.shape)

    out_shape = []
    for (_, _, _, w, _, _) in params:
        out_shape += [jax.ShapeDtypeStruct(w.shape, F32)] * 4
    out_shape.append(jax.ShapeDtypeStruct((1, LANES), F32))
    flat = [a for (_, _, _, w, m, v) in params for a in (w, m, v)]
    return pl.pallas_call(
        body, name="adamw_small", out_shape=out_shape,
        in_specs=[_VMEM] * (2 + len(flat)), out_specs=[_VMEM] * len(out_shape),
    )(gathered, gathered_cols, *flat)


def _pad_rows(a, rows):
    return jnp.concatenate([a, jnp.zeros((rows - a.shape[0], a.shape[1]), a.dtype)], axis=0)


def kernel(x, meta_tokens, norm_g, w_in, conv_a_w, conv_a_b, ln_a_g, ln_a_b, w_a_out, b_a_out, conv_b_w, w_b_out, w_out, final_g, loss_target, m_meta_tokens, m_norm_g, m_w_in, m_conv_a_w, m_conv_a_b, m_ln_a_g, m_ln_a_b, m_w_a_out, m_b_a_out, m_conv_b_w, m_w_b_out, m_w_out, m_final_g, v_meta_tokens, v_norm_g, v_w_in, v_conv_a_w, v_conv_a_b, v_ln_a_g, v_ln_a_b, v_w_a_out, v_b_a_out, v_conv_b_w, v_w_b_out, v_w_out, v_final_g):
    seq = x.shape[1]
    assert x.shape == (1, seq, D_MODEL) and seq % TILE == 0 and w_in.shape == (1, D_MODEL, COLS)
    n_tiles = seq // TILE + 1
    tp = n_tiles * TILE
    pos = jnp.stack([lax.axis_index("x"), lax.axis_index("y"), lax.axis_index("c")]).astype(jnp.int32)
    me = 4 * pos[0] + 2 * pos[1] + pos[2]
    x2d = x[0]
    tgt2d = loss_target[0]

    small = jnp.concatenate([meta_tokens, _pad_rows(conv_a_w[0], 32), _pad_rows(conv_b_w[0], SUBLANES)], axis=0)
    small_all = _all_gather(small, "gather_small")
    small_all = small_all.transpose(1, 0, 2).reshape(small.shape[0], D_MODEL)
    meta_full, conv_a_full, conv_b_full = small_all[0:N_META], small_all[N_META:N_META + 32], small_all[N_META + 32:]
    meta_tile = jnp.concatenate([jnp.zeros((TILE - N_META, D_MODEL), F32), meta_full], axis=0)
    final_g2 = final_g.reshape(1, D_MODEL)

    w_out_shards = [w[0].astype(BF16) for w in (w_a_out, w_b_out, w_out)]
    h_t, proj, w_in_all, *w_out_all = _gather_norm_proj(pos, x2d, meta_tile, norm_g, w_in[0].astype(BF16),
                                                      w_out_shards, 3)
    w_out_all = [w.reshape(D_MODEL, D_MODEL) for w in w_out_all]
    w_out_all_t = [w.T for w in w_out_all]
    dproj, ds1, lhs, rhs, vec, d_conv_a, d_conv_b = _fused_pass(
        proj, x2d, tgt2d, meta_tile, conv_a_full, conv_a_b, ln_a_g, ln_a_b, b_a_out, conv_b_full, final_g2,
        w_out_all[0], w_out_all[1], w_out_all[2], w_out_all_t[0], w_out_all_t[1], w_out_all_t[2], n_tiles)
    k_tile = tp // 3
    gw_out = _grad_w_out(lhs, rhs, k_tile).reshape(N_DEV, 3 * ROWS_OUT, D_MODEL)
    gw_far, their_out = _grad_w_in_half(pos, h_t, dproj, k_tile, True, [("sibling", (gw_out,))], "grad_w_in_far")
    parts_out = _chip_partial(pos, gw_out, their_out, (1, 2, 3), BF16, ROWS_OUT, "rs_parts_w_out")
    gw_near, their_in, land_out = _grad_w_in_half(
        pos, h_t, dproj, k_tile, False, [("sibling_half", (gw_far,)), ("chips", (parts_out,))], "grad_w_in_near")
    parts_in = _chip_partial(pos, gw_near, their_in, (1, 2, 3), BF16, 256, "rs_parts_w_in")
    x_tile = min(512, seq)
    grad_x, vec_x, land_in = _input_bwd(
        dproj, ds1, x2d, norm_g, w_in_all, x_tile, seq // x_tile, 0, "input_bwd_x", ride=(parts_in,))
    d_meta_tile, vec_meta = _input_bwd(dproj, ds1, meta_tile, norm_g, w_in_all, TILE, 1, n_tiles - 1,
                                       "input_bwd_meta")

    small_g = jnp.concatenate([vec, vec_x, vec_meta, d_conv_a, d_conv_b, d_meta_tile[TILE - N_META:]], axis=0)
    small_g_all = _all_gather(small_g, "gather_small_grads")
    small_g_cols = lax.dynamic_slice_in_dim(small_g_all, me * LANES, LANES, axis=2)

    res_in = _adamw_sharded(pos, gw_near, their_in, land_in, w_in[0], m_w_in[0], v_w_in[0], 128, 0, "adamw_w_in")
    res_out = [
        _adamw_sharded(pos, gw_out, their_out, land_out, w[0], m[0], v[0], ROWS_OUT, k, f"adamw_w_out{k}")
        for k, (w, m, v) in enumerate([(w_a_out, m_w_a_out, v_w_a_out), (w_b_out, m_w_b_out, v_w_b_out),
                                       (w_out, m_w_out, v_w_out)])]
    params = [
        (ROW_META, N_META, True, meta_tokens, m_meta_tokens, v_meta_tokens),
        (ROW_NORM_G_X, 1, False, norm_g, m_norm_g, v_norm_g),
        (ROW_CONV_A_W, CONV_A, True, conv_a_w[0], m_conv_a_w[0], v_conv_a_w[0]),
        (ROW_CONV_A_B, 1, False, conv_a_b, m_conv_a_b, v_conv_a_b),
        (ROW_LN_G, 1, False, ln_a_g, m_ln_a_g, v_ln_a_g),
        (ROW_LN_B, 1, False, ln_a_b, m_ln_a_b, v_ln_a_b),
        (ROW_B_A_OUT, 1, False, b_a_out, m_b_a_out, v_b_a_out),
        (ROW_CONV_B_W, CONV_B, True, conv_b_w[0], m_conv_b_w[0], v_conv_b_w[0]),
        (ROW_FINAL_G, 1, False, final_g2, m_final_g.reshape(1, D_MODEL), v_final_g.reshape(1, D_MODEL)),
    ]
    res_small = _adamw_small(small_g_all, small_g_cols, params)
    loss = res_small[-1][0, 0]

    def small_res(p, kind, shape):
        return res_small[4 * p + kind].reshape(shape)

    per_weight = []
    for kind in range(4):
        per_weight.append([
            small_res(0, kind, meta_tokens.shape),
            small_res(1, kind, norm_g.shape),
            res_in[kind].reshape(w_in.shape),
            small_res(2, kind, conv_a_w.shape),
            small_res(3, kind, conv_a_b.shape),
            small_res(4, kind, ln_a_g.shape),
            small_res(5, kind, ln_a_b.shape),
            res_out[0][kind].reshape(w_a_out.shape),
            small_res(6, kind, b_a_out.shape),
            small_res(7, kind, conv_b_w.shape),
            res_out[1][kind].reshape(w_b_out.shape),
            res_out[2][kind].reshape(w_out.shape),
            small_res(8, kind, final_g.shape),
        ])
    return (loss, grad_x.reshape(x.shape), *per_weight[0], *per_weight[1], *per_weight[2], *per_weight[3])
```

```python
import functools

import jax
import jax.numpy as jnp
from jax import lax
from jax.experimental import pallas as pl
from jax.experimental.pallas import tpu as pltpu

D_MODEL = 1024
N_META = 16
N_DEV = 8
D_IN = 9 * D_MODEL
COLS = D_IN // N_DEV
ROWS_OUT = D_MODEL // N_DEV
CONV_A = 31
CONV_B = 3
EPS = 1e-6

ADAM_LR = 0.001
ADAM_B1 = 0.9
ADAM_B2 = 0.999
ADAM_EPS = 1e-08
ADAM_WD = 0.01
ADAM_STEP = 10

TILE = 128
LANES = 128
N_CHUNK = D_MODEL // LANES
HALO = 32
SUBLANES = 8
VMEM_LIMIT = 56 * 1024 * 1024

ROW_FINAL_G, ROW_B_A_OUT, ROW_LN_G, ROW_LN_B, ROW_CONV_A_B, ROW_LOSS = 0, 1, 2, 3, 4, 5
ROW_NORM_G = 8
ROW_CONV_A_W, ROW_CONV_B_W, ROW_META = 16, 48, 56
SMALL_ROWS = 72

MESH = pl.DeviceIdType.MESH
_ANY = pl.BlockSpec(memory_space=pl.ANY)
_VMEM = pl.BlockSpec(memory_space=pltpu.VMEM)


def _resident(shape):
    return pl.BlockSpec(shape, lambda *_: (0,) * len(shape), pipeline_mode=pl.Buffered(1))
BF16 = jnp.bfloat16
F32 = jnp.float32


def _sigmoid(v):
    return jax.nn.sigmoid(v)


def _dot(a, b):
    return jnp.dot(a, b, preferred_element_type=F32)


def _dot_nt(a, b):
    return lax.dot_general(a, b, (((1,), (1,)), ((), ())), preferred_element_type=F32)


def _dot_tn(a, b):
    return lax.dot_general(a, b, (((0,), (0,)), ((), ())), preferred_element_type=F32)


def _colsum(v):
    return jnp.sum(v, axis=0, keepdims=True)


def _rowmean(v):
    parts = [v[:, LANES * c:LANES * (c + 1)] for c in range(v.shape[1] // LANES)]
    return jnp.sum(functools.reduce(jnp.add, parts), axis=-1, keepdims=True) * (1.0 / v.shape[1])


def _fold8(v):
    parts = [v[SUBLANES * g:SUBLANES * (g + 1)] for g in range(v.shape[0] // SUBLANES)]
    return functools.reduce(jnp.add, parts)


def _all_gather(shard, name, ride=()):
    m, n = shard.shape
    n_arr = len(ride)
    ride_shapes, ride_sems = _exchange_shapes("chips", ride) if ride else ([], [])

    def body(*refs):
        x_ref, out_ref = refs[0], refs[1 + n_arr]
        send_sems, recv_sems, local_sem = refs[2 + 2 * n_arr:5 + 2 * n_arr]
        riding = _chip_copies(refs[1:1 + n_arr], refs[2 + n_arr:2 + 2 * n_arr], *refs[5 + 2 * n_arr:]) if ride else []
        for cp in riding:
            cp.start()
        x, y, c = lax.axis_index("x"), lax.axis_index("y"), lax.axis_index("c")
        me, sibling = (x, y, c), (x, y, 1 - c)
        chips = [(1 - x, y), (x, 1 - y), (1 - x, 1 - y)]

        def slot(px, py, pc):
            return out_ref.at[4 * px + 2 * py + pc]

        def copy(k, block, to, src=None):
            return pltpu.make_async_remote_copy(
                src_ref=slot(*block) if src is None else src, dst_ref=slot(*block),
                send_sem=send_sems.at[k], recv_sem=recv_sems.at[k], device_id=to, device_id_type=MESH)

        mine = pltpu.make_async_copy(x_ref, slot(*me), local_sem)
        mine.start()
        first = [copy(0, me, sibling, src=x_ref)]
        first += [copy(1 + j, me, (*chip, c), src=x_ref) for j, chip in enumerate(chips)]
        for cp in first:
            cp.start()
        passed = [copy(4 + j, (*chip, c), sibling) for j, chip in enumerate(chips)]
        for j, chip in enumerate(chips):
            copy(1 + j, (*chip, c), me).wait_recv()
            passed[j].start()
        copy(0, sibling, me).wait_recv()
        for j, chip in enumerate(chips):
            copy(4 + j, (*chip, 1 - c), me).wait_recv()
        for cp in first + passed:
            cp.wait_send()
        mine.wait()
        for cp in riding:
            cp.wait()

    res = pl.pallas_call(
        body, name=name,
        out_shape=[jax.ShapeDtypeStruct((N_DEV, m, n), shard.dtype)] + ride_shapes,
        in_specs=[_ANY] * (1 + n_arr), out_specs=[_ANY] * (1 + n_arr),
        scratch_shapes=[pltpu.SemaphoreType.DMA((7,)), pltpu.SemaphoreType.DMA((7,)), pltpu.SemaphoreType.DMA(())]
        + ride_sems,
    )(shard, *ride)
    return res if ride else res[0]


def _sibling_copies(srcs, dsts, send_sems, recv_sems):
    x, y, c = lax.axis_index("x"), lax.axis_index("y"), lax.axis_index("c")
    return [pltpu.make_async_remote_copy(
        src_ref=src.at[2 * q + (1 - c)], dst_ref=dst.at[q],
        send_sem=send_sems.at[4 * a + q], recv_sem=recv_sems.at[4 * a + q],
        device_id=(x, y, 1 - c), device_id_type=MESH)
        for a, (src, dst) in enumerate(zip(srcs, dsts)) for q in range(4)]


def _chip_copies(srcs, dsts, send_sems, recv_sems):
    x, y, c = lax.axis_index("x"), lax.axis_index("y"), lax.axis_index("c")
    targets = [(x, 1 - y, c), (1 - x, y, c), (1 - x, 1 - y, c)]
    return [pltpu.make_async_remote_copy(
        src_ref=src.at[k], dst_ref=dst.at[k],
        send_sem=send_sems.at[3 * a + k], recv_sem=recv_sems.at[3 * a + k],
        device_id=targets[k], device_id_type=MESH)
        for a, (src, dst) in enumerate(zip(srcs, dsts)) for k in range(3)]


def _sibling_half_copies(srcs, dsts, send_sems, recv_sems):
    x, y, c = lax.axis_index("x"), lax.axis_index("y"), lax.axis_index("c")
    return [pltpu.make_async_remote_copy(
        src_ref=src.at[q], dst_ref=dst.at[q],
        send_sem=send_sems.at[4 * a + q], recv_sem=recv_sems.at[4 * a + q],
        device_id=(x, y, 1 - c), device_id_type=MESH)
        for a, (src, dst) in enumerate(zip(srcs, dsts)) for q in range(4)]


_EXCHANGES = {"sibling": (4, _sibling_copies, 4), "sibling_half": (4, _sibling_half_copies, 4),
              "chips": (3, _chip_copies, 3)}


def _exchange_shapes(kind, arrays):
    per_array, _, slots = _EXCHANGES[kind]
    out_shape = [jax.ShapeDtypeStruct((slots,) + a.shape[1:], a.dtype) for a in arrays]
    sems = [pltpu.SemaphoreType.DMA((per_array * len(arrays),))] * 2
    return out_shape, sems


def _ride_shapes(rides):
    shapes, sems = [], []
    for kind, arrays in rides:
        ride_shapes, ride_sems = _exchange_shapes(kind, arrays)
        shapes += ride_shapes
        sems += ride_sems
    return shapes, sems


def _riding(body, n_in, n_out, rides, is_first, is_last):
    counts = [len(arrays) for _, arrays in rides]
    n_arr = sum(counts)

    def wrapped(*refs):
        ins, srcs = refs[:n_in], refs[n_in:n_in + n_arr]
        outs = refs[n_in + n_arr:n_in + n_arr + n_out]
        dsts = refs[n_in + n_arr + n_out:n_in + 2 * n_arr + n_out]
        first_sem = len(refs) - 2 * len(rides)
        scratch, sems = refs[n_in + 2 * n_arr + n_out:first_sem], refs[first_sem:]

        def copies():
            made, at = [], 0
            for r, ((kind, _), n) in enumerate(zip(rides, counts)):
                made += _EXCHANGES[kind][1](srcs[at:at + n], dsts[at:at + n], sems[2 * r], sems[2 * r + 1])
                at += n
            return made

        @pl.when(is_first())
        def _():
            for cp in copies():
                cp.start()

        body(*ins, *outs, *scratch)

        @pl.when(is_last())
        def _():
            for cp in copies():
                cp.wait()

    return wrapped


def _chip_partial(pos, mine, theirs, relations, out_dtype, row_tile, name):
    n_slots, m, n = mine.shape
    q0 = relations[0]

    def chip_of(qi, pos_ref):
        q = qi + q0
        return pos_ref[0] ^ (q >> 1), pos_ref[1] ^ (q & 1)

    def mine_map(qi, t, pos_ref):
        px, py = chip_of(qi, pos_ref)
        return (4 * px + 2 * py + pos_ref[2] if n_slots == N_DEV else 2 * px + py), t, 0

    def theirs_map(qi, t, pos_ref):
        px, py = chip_of(qi, pos_ref)
        return 2 * px + py, t, 0

    def body(pos_ref, a_ref, b_ref, o_ref):
        o_ref[...] = (a_ref[...] + b_ref[...]).astype(out_dtype)

    return pl.pallas_call(
        body, name=name,
        out_shape=jax.ShapeDtypeStruct((len(relations), m, n), out_dtype),
        grid_spec=pltpu.PrefetchScalarGridSpec(
            num_scalar_prefetch=1, grid=(len(relations), m // row_tile),
            in_specs=[pl.BlockSpec((None, row_tile, n), mine_map), pl.BlockSpec((None, row_tile, n), theirs_map)],
            out_specs=pl.BlockSpec((None, row_tile, n), lambda qi, t, pos_ref: (qi, t, 0))),
        compiler_params=pltpu.CompilerParams(dimension_semantics=("arbitrary", "arbitrary")),
    )(pos, mine, theirs)


PARTS = ((0, 512), (512, 640))


def _gather_norm_proj(pos, x2d, meta_tile, norm_g, w_in_shard, w_out_shards, n_chunk):
    seq = x2d.shape[0]
    n_tiles = seq // TILE + 1
    tp = n_tiles * TILE
    n_parts = len(PARTS)
    widest = max(width for _, width in PARTS)
    units = ([(s, u) for s in range(2) for u in range(n_parts)]
             + [(2 + j, u) for u in range(n_parts) for j in range(3)]
             + [(5 + j, u) for u in range(n_parts) for j in range(3)])
    n_units = len(units)
    n_steps = n_tiles + n_units
    chunk = tp // n_chunk

    def body(pos_ref, x_ref, meta_ref, g_ref, win_ref, wa_ref, wb_ref, wo_ref,
             ht_ref, proj_ref, win_all, wa_all, wb_all, wo_all,
             h_all, wbuf, rbuf, send_sems, recv_sems, local_sems):
        g = pl.program_id(0)
        x, y, c = lax.axis_index("x"), lax.axis_index("y"), lax.axis_index("c")
        me, sibling = (x, y, c), (x, y, 1 - c)
        chips = [(1 - x, y), (x, 1 - y), (1 - x, 1 - y)]
        shards = (win_ref, wa_ref, wb_ref, wo_ref)
        gathered = (win_all, wa_all, wb_all, wo_all)
        blocks = [me, sibling] + [(*chip, c) for chip in chips] + [(*chip, 1 - c) for chip in chips]

        def index(block):
            px, py, pc = block
            return 4 * px + 2 * py + pc

        def part(ref, a, u):
            return ref.at[:, pl.ds(PARTS[u][0], PARTS[u][1])] if a == 0 else ref

        def slot(a, block, u):
            return part(gathered[a].at[index(block)], a, u)

        def sem(a, k, u):
            return n_parts * k + u if a == 0 else 7 * n_parts + 7 * (a - 1) + k

        def copy(a, k, block, to, u=0, from_shard=False):
            return pltpu.make_async_remote_copy(
                src_ref=part(shards[a], a, u) if from_shard else slot(a, block, u), dst_ref=slot(a, block, u),
                send_sem=send_sems.at[sem(a, k, u)], recv_sem=recv_sems.at[sem(a, k, u)],
                device_id=to, device_id_type=MESH)

        def keep(a):
            return pltpu.make_async_copy(shards[a], gathered[a].at[index(me)], local_sems.at[a])

        def load(m):
            s, u = units[m]
            src = part(win_ref, 0, u) if s == 0 else slot(0, blocks[s], u)
            return pltpu.make_async_copy(src, wbuf.at[m % 2, :, 0:PARTS[u][1]], local_sems.at[4 + m % 2])

        def store(m):
            s, u = units[m]
            col0 = pl.multiple_of(index(blocks[s]) * COLS + PARTS[u][0], LANES)
            return pltpu.make_async_copy(rbuf.at[m % 2, :, 0:PARTS[u][1]],
                                         proj_ref.at[:, pl.ds(col0, PARTS[u][1])], local_sems.at[6 + m % 2])

        def arrive(m):
            s, u = units[m]
            if s == 1:
                copy(0, 0, sibling, me, u).wait_recv()
            elif 2 <= s <= 4:
                copy(0, s - 1, blocks[s], me, u).wait_recv()
                copy(0, s + 2, blocks[s], sibling, u).start()
            elif s >= 5:
                copy(0, s - 1, blocks[s], me, u).wait_recv()
                if u == 0:
                    for a in range(1, 4):
                        copy(a, s - 4, blocks[s - 3], me).wait_recv()
                        copy(a, s - 1, blocks[s - 3], sibling).start()

        targets = [sibling] + [(*chip, c) for chip in chips]

        @pl.when(g == 0)
        def _():
            for a in range(4):
                keep(a).start()
            for u in range(n_parts):
                for k, to in enumerate(targets):
                    copy(0, k, me, to, u, from_shard=True).start()
            for a in range(1, 4):
                for k, to in enumerate(targets):
                    copy(a, k, me, to, from_shard=True).start()
            load(0).start()

        @pl.when(g < n_tiles)
        def _():
            s0 = jnp.where(g == n_tiles - 1, meta_ref[...], x_ref[...])
            r = lax.rsqrt(_rowmean(s0 * s0) + EPS)
            h32 = (s0 * r) * g_ref[...]
            ht_ref[...] = h32.T.astype(BF16)
            h_all[pl.ds(pl.multiple_of(g * TILE, TILE), TILE), :] = h32.astype(BF16)

        for m in range(n_units):
            @pl.when(g == n_tiles + m)
            def _(m=m):
                load(m).wait()
                if m + 1 < n_units:
                    arrive(m + 1)
                    load(m + 1).start()
                if m >= 2:
                    store(m - 2).wait()

        m_now = jnp.maximum(g - n_tiles, 0)
        u_now = jnp.where(m_now < 2 * n_parts, m_now % n_parts, ((m_now - 2 * n_parts) // 3) % n_parts)
        for u, (_, width) in enumerate(PARTS):
            @pl.when((g >= n_tiles) & (u_now == u))
            def _(width=width):
                w = wbuf[m_now % 2, :, 0:width]
                for r in range(n_chunk):
                    rbuf[m_now % 2, r * chunk:(r + 1) * chunk, 0:width] = _dot(h_all[r * chunk:(r + 1) * chunk, :], w)

        for m in range(n_units):
            @pl.when(g == n_tiles + m)
            def _(m=m):
                store(m).start()

        @pl.when(g == n_steps - 1)
        def _():
            store(n_units - 2).wait()
            store(n_units - 1).wait()
            for a in range(1, 4):
                copy(a, 0, sibling, me).wait_recv()
                for j in range(3):
                    copy(a, 4 + j, blocks[5 + j], me).wait_recv()
            for a in range(4):
                for u in range(n_parts if a == 0 else 1):
                    for k, to in enumerate(targets):
                        copy(a, k, me, to, u, from_shard=True).wait_send()
                    for j in range(3):
                        copy(a, 4 + j, blocks[2 + j], sibling, u).wait_send()
                keep(a).wait()

    n_x = n_tiles - 1
    return pl.pallas_call(
        body, name="gather_norm_proj",
        out_shape=[jax.ShapeDtypeStruct((D_MODEL, tp), BF16), jax.ShapeDtypeStruct((tp, D_IN), F32),
                   jax.ShapeDtypeStruct((N_DEV,) + w_in_shard.shape, BF16)]
                  + [jax.ShapeDtypeStruct((N_DEV,) + w.shape, BF16) for w in w_out_shards],
        grid_spec=pltpu.PrefetchScalarGridSpec(
            num_scalar_prefetch=1, grid=(n_steps,),
            in_specs=[pl.BlockSpec((TILE, D_MODEL), lambda g, pos_ref: (jnp.minimum(g, n_x - 1), 0)),
                      _VMEM, _VMEM, _ANY, _ANY, _ANY, _ANY],
            out_specs=[pl.BlockSpec((D_MODEL, TILE), lambda g, pos_ref: (0, jnp.minimum(g, n_tiles - 1))),
                       _ANY, _ANY, _ANY, _ANY, _ANY],
            scratch_shapes=[pltpu.VMEM((tp, D_MODEL), BF16), pltpu.VMEM((2, D_MODEL, widest), BF16),
                            pltpu.VMEM((2, tp, widest), F32),
                            pltpu.SemaphoreType.DMA((7 * n_parts + 21,)), pltpu.SemaphoreType.DMA((7 * n_parts + 21,)),
                            pltpu.SemaphoreType.DMA((8,))]),
        compiler_params=pltpu.CompilerParams(dimension_semantics=("arbitrary",), vmem_limit_bytes=VMEM_LIMIT),
    )(pos, x2d, meta_tile, norm_g, w_in_shard, *w_out_shards)


C_AVAL, C_AGLU, C_AZ, C_BB, C_BC, C_BX, C_BZ, C_GA, C_GB = (k * D_MODEL for k in range(9))
S_AZ, S_BB, S_BZ, S_GA, S_GB = (k * D_MODEL for k in range(5))


def _fused_pass(proj, x2d, tgt2d, meta_tile, conv_a_w, conv_a_b, ln_a_g, ln_a_b, b_a_out, conv_b_w, final_g,
                w_a, w_b, w_o, w_a_t, w_b_t, w_o_t, n_tiles):
    T = TILE
    tp = n_tiles * T
    inv_d = 1.0 / D_MODEL

    def block_of(tile):
        return jnp.where(tile == 0, n_tiles - 1, tile - 1)

    def cur(i):
        return block_of(jnp.minimum(i, n_tiles - 1))

    def prev(i):
        return block_of(jnp.clip(i - 1, 0, n_tiles - 1))

    def xblk(i):
        return jnp.maximum(jnp.minimum(i, n_tiles - 1) - 1, 0)

    def body(proj_ref, x_ref, tgt_ref, meta_ref, caw_ref, cab_ref, lng_ref, lnb_ref, bao_ref, cbw_ref, fg_ref,
             wa_ref, wb_ref, wo_ref, wat_ref, wbt_ref, wot_ref,
             dproj_ref, ds1_ref, lhs_ref, rhs_ref, vec_ref, dcaw_ref, dcbw_ref,
             ua0_buf, cb_buf, dua1_buf, dc3_buf, aprev, cprev, stage, ua1_buf, c3_buf,
             dpa_buf, dpb_buf, dcaw8, dcbw8, shift_buf):
        i = pl.program_id(0)

        @pl.when(i == 0)
        def _init():
            for buf in (ua0_buf, cb_buf, dua1_buf, dc3_buf, aprev, cprev, dcaw8, dcbw8):
                buf[...] = jnp.zeros(buf.shape, buf.dtype)
            vec_ref[...] = jnp.zeros(vec_ref.shape, F32)

        @pl.when(i >= 1)
        def _emit_stage():
            dproj_ref[:, C_AZ:C_BC] = stage[:, S_AZ:S_BZ]
            dproj_ref[:, C_BZ:D_IN] = stage[:, S_BZ:S_GB + D_MODEL]

        @pl.when(i < n_tiles)
        def _front():
            def conv_chunk(cc, carry):
                c0 = pl.multiple_of(cc * LANES, LANES)
                lanes = pl.ds(c0, LANES)

                def col(base):
                    return pl.ds(pl.multiple_of(base + cc * LANES, LANES), LANES)

                ua0 = proj_ref[:, col(C_AVAL)] * _sigmoid(proj_ref[:, col(C_AGLU)])
                ua0_buf[T:2 * T, lanes] = ua0
                acc = jnp.broadcast_to(cab_ref[:, lanes], (T, LANES))
                lead = HALO - (CONV_A - 1)
                for r in range(SUBLANES):
                    taps = [k for k in range(CONV_A) if (k + lead) % SUBLANES == r]
                    rows = T + SUBLANES * max((k + lead) // SUBLANES for k in taps)
                    if r:
                        shift_buf[r, 0:rows, :] = ua0_buf[pl.ds(T - HALO + r, rows), lanes]
                    for k in taps:
                        q = (k + lead) // SUBLANES
                        if r:
                            win = shift_buf[r, SUBLANES * q:SUBLANES * q + T, :]
                        else:
                            win = ua0_buf[pl.ds(T - HALO + SUBLANES * q, T), lanes]
                        acc = acc + caw_ref[k:k + 1, lanes] * win
                ua1_buf[:, lanes] = acc
                cb = proj_ref[:, col(C_BC)] * proj_ref[:, col(C_BX)]
                cb_buf[T:2 * T, lanes] = cb
                acc3 = cbw_ref[0:1, lanes] * cb_buf[pl.ds(T - 2, T), lanes]
                for k in range(1, CONV_B):
                    acc3 = acc3 + cbw_ref[k:k + 1, lanes] * cb_buf[pl.ds(T - (CONV_B - 1) + k, T), lanes]
                c3_buf[:, lanes] = acc3
                return carry

            lax.fori_loop(0, N_CHUNK, conv_chunk, 0)

            ua1 = ua1_buf[...]
            xc = ua1 - _rowmean(ua1)
            rstd = lax.rsqrt(_rowmean(xc * xc) + EPS)
            xhat = xc * rstd
            ua2 = xhat * lng_ref[...] + lnb_ref[...]
            sg2 = _sigmoid(ua2)
            ua3 = ua2 * sg2
            a_z = proj_ref[:, C_AZ:C_AZ + D_MODEL]
            sz = _sigmoid(a_z)
            silu_az = a_z * sz
            lhs_ref[0] = (ua3 * silu_az).astype(BF16)
            b_z = proj_ref[:, C_BZ:C_BZ + D_MODEL]
            sbz = _sigmoid(b_z)
            silu_bz = b_z * sbz
            b_b = proj_ref[:, C_BB:C_BB + D_MODEL]
            c3 = c3_buf[...]
            ub = b_b * c3
            lhs_ref[1] = (ub * silu_bz).astype(BF16)

            ya = _dot(lhs_ref[0], wa_ref[...]) + bao_ref[...]
            yb = _dot(lhs_ref[1], wb_ref[...])
            sga = _sigmoid(proj_ref[:, C_GA:C_GA + D_MODEL])
            sgb = _sigmoid(proj_ref[:, C_GB:C_GB + D_MODEL])
            m_b = (sga * ya + sgb * yb).astype(BF16)
            lhs_ref[2] = m_b
            s0 = jnp.where(i == 0, meta_ref[...], x_ref[...])
            s1 = s0 + _dot(m_b, wo_ref[...])
            r1 = lax.rsqrt(_rowmean(s1 * s1) + EPS)
            y = (s1 * r1) * fg_ref[...]
            is_token = (i >= 1).astype(F32)
            err = (y - tgt_ref[...]) * is_token
            vec_ref[ROW_LOSS:ROW_LOSS + 1, :] += (0.5 * inv_d) * _colsum(err * err)
            dy = err * inv_d
            vec_ref[ROW_FINAL_G:ROW_FINAL_G + 1, :] += _colsum(dy * (s1 * r1))
            gy = dy * fg_ref[...]
            ds1 = r1 * gy - s1 * ((r1 * r1 * r1) * _rowmean(gy * s1))
            ds1_ref[...] = ds1
            ds1_b = ds1.astype(BF16)
            rhs_ref[2] = ds1_b
            dm = _dot(ds1_b, wot_ref[...])
            dya = dm * sga
            dyb = dm * sgb
            stage[:, S_GA:S_GA + D_MODEL] = (dya * ya * (1.0 - sga)).astype(BF16)
            stage[:, S_GB:S_GB + D_MODEL] = (dyb * yb * (1.0 - sgb)).astype(BF16)
            vec_ref[ROW_B_A_OUT:ROW_B_A_OUT + 1, :] += _colsum(dya)
            dya_b = dya.astype(BF16)
            dyb_b = dyb.astype(BF16)
            rhs_ref[0] = dya_b
            rhs_ref[1] = dyb_b
            dpa_buf[...] = _dot(dya_b, wat_ref[...])
            dpb_buf[...] = _dot(dyb_b, wbt_ref[...])

            dpa = dpa_buf[...]
            stage[:, S_AZ:S_AZ + D_MODEL] = (dpa * ua3 * (sz + silu_az * (1.0 - sz))).astype(BF16)
            dua2 = dpa * silu_az * (sg2 + ua3 * (1.0 - sg2))
            vec_ref[ROW_LN_G:ROW_LN_G + 1, :] += _colsum(dua2 * xhat)
            vec_ref[ROW_LN_B:ROW_LN_B + 1, :] += _colsum(dua2)
            dxh = dua2 * lng_ref[...]
            dua1 = rstd * (dxh - _rowmean(dxh) - xhat * _rowmean(dxh * xhat))
            vec_ref[ROW_CONV_A_B:ROW_CONV_A_B + 1, :] += _colsum(dua1)
            dua1_buf[T:2 * T, :] = dua1
            dpb = dpb_buf[...]
            stage[:, S_BZ:S_BZ + D_MODEL] = (dpb * ub * (sbz + silu_bz * (1.0 - sbz))).astype(BF16)
            dub = dpb * silu_bz
            stage[:, S_BB:S_BB + D_MODEL] = (dub * c3).astype(BF16)
            dc3_buf[T:2 * T, :] = dub * b_b

        @pl.when(i == n_tiles)
        def _no_later_tile():
            dua1_buf[T:2 * T, :] = jnp.zeros((T, D_MODEL), F32)
            dc3_buf[T:2 * T, :] = jnp.zeros((T, D_MODEL), F32)

        @pl.when(i >= 1)
        def _lagged():
            def convt_chunk(cc, carry):
                c0 = pl.multiple_of(cc * LANES, LANES)
                lanes = pl.ds(c0, LANES)

                def col(base):
                    return pl.ds(pl.multiple_of(base + cc * LANES, LANES), LANES)

                ua0 = ua0_buf[0:T, lanes]
                acc = jnp.zeros((T, LANES), F32)
                for r in range(SUBLANES):
                    shifts = [j for j in range(CONV_A) if j % SUBLANES == r]
                    rows = T + shifts[-1] - r
                    if r:
                        shift_buf[r, 0:rows, :] = dua1_buf[pl.ds(r, rows), lanes]
                    for j in shifts:
                        k = CONV_A - 1 - j
                        if r:
                            later = shift_buf[r, j - r:j - r + T, :]
                        else:
                            later = dua1_buf[pl.ds(j, T), lanes]
                        acc = acc + caw_ref[k:k + 1, lanes] * later
                        dcaw8[SUBLANES * k:SUBLANES * (k + 1), lanes] += _fold8(ua0 * later)
                a_val = aprev[:, col(0)]
                sg = _sigmoid(aprev[:, col(D_MODEL)])
                dproj_ref[:, col(C_AVAL)] = (acc * sg).astype(BF16)
                dproj_ref[:, col(C_AGLU)] = (acc * a_val * (sg * (1.0 - sg))).astype(BF16)

                cb = cb_buf[0:T, lanes]
                acc3 = jnp.zeros((T, LANES), F32)
                for j in range(CONV_B):
                    k = CONV_B - 1 - j
                    later = dc3_buf[pl.ds(j, T), lanes]
                    acc3 = acc3 + cbw_ref[k:k + 1, lanes] * later
                    dcbw8[SUBLANES * k:SUBLANES * (k + 1), lanes] += _fold8(cb * later)
                dproj_ref[:, col(C_BC)] = (acc3 * cprev[:, col(D_MODEL)]).astype(BF16)
                dproj_ref[:, col(C_BX)] = (acc3 * cprev[:, col(0)]).astype(BF16)
                return carry

            lax.fori_loop(0, N_CHUNK, convt_chunk, 0)

        for buf in (ua0_buf, cb_buf, dua1_buf, dc3_buf):
            buf[0:T, :] = buf[T:2 * T, :]
        aprev[...] = proj_ref[:, C_AVAL:C_AZ]
        cprev[...] = proj_ref[:, C_BC:C_BZ]

        @pl.when(i == n_tiles)
        def _finish():
            for k in range(CONV_A):
                dcaw_ref[k:k + 1, :] = _colsum(dcaw8[SUBLANES * k:SUBLANES * (k + 1), :])
            dcaw_ref[CONV_A:CONV_A + 1, :] = jnp.zeros((1, D_MODEL), F32)
            for k in range(CONV_B):
                dcbw_ref[k:k + 1, :] = _colsum(dcbw8[SUBLANES * k:SUBLANES * (k + 1), :])
            dcbw_ref[CONV_B:SUBLANES, :] = jnp.zeros((SUBLANES - CONV_B, D_MODEL), F32)

    tile_in = lambda width: pl.BlockSpec((T, width), lambda i: (cur(i), 0))
    return pl.pallas_call(
        body, name="fused_pass", grid=(n_tiles + 1,),
        out_shape=[
            jax.ShapeDtypeStruct((tp, D_IN), BF16),
            jax.ShapeDtypeStruct((tp, D_MODEL), F32),
            jax.ShapeDtypeStruct((3, tp, D_MODEL), BF16),
            jax.ShapeDtypeStruct((3, tp, D_MODEL), BF16),
            jax.ShapeDtypeStruct((SUBLANES, D_MODEL), F32),
            jax.ShapeDtypeStruct((32, D_MODEL), F32),
            jax.ShapeDtypeStruct((SUBLANES, D_MODEL), F32),
        ],
        in_specs=[
            tile_in(D_IN),
            pl.BlockSpec((T, D_MODEL), lambda i: (xblk(i), 0)),
            pl.BlockSpec((T, D_MODEL), lambda i: (xblk(i), 0)),
            _VMEM, _VMEM, _VMEM, _VMEM, _VMEM, _VMEM, _VMEM, _VMEM,
            *[_resident((D_MODEL, D_MODEL)) for _ in range(6)],
        ],
        out_specs=[
            pl.BlockSpec((T, D_IN), lambda i: (prev(i), 0)),
            pl.BlockSpec((T, D_MODEL), lambda i: (cur(i), 0)),
            pl.BlockSpec((3, T, D_MODEL), lambda i: (0, cur(i), 0)),
            pl.BlockSpec((3, T, D_MODEL), lambda i: (0, cur(i), 0)),
            _VMEM, _VMEM, _VMEM,
        ],
        scratch_shapes=[
            pltpu.VMEM((2 * T, D_MODEL), F32),
            pltpu.VMEM((2 * T, D_MODEL), F32),
            pltpu.VMEM((2 * T, D_MODEL), F32),
            pltpu.VMEM((2 * T, D_MODEL), F32),
            pltpu.VMEM((T, 2 * D_MODEL), F32),
            pltpu.VMEM((T, 2 * D_MODEL), F32),
            pltpu.VMEM((T, 5 * D_MODEL), BF16),
            pltpu.VMEM((T, D_MODEL), F32),
            pltpu.VMEM((T, D_MODEL), F32),
            pltpu.VMEM((T, D_MODEL), F32),
            pltpu.VMEM((T, D_MODEL), F32),
            pltpu.VMEM((32 * SUBLANES, D_MODEL), F32),
            pltpu.VMEM((SUBLANES * SUBLANES, D_MODEL), F32),
            pltpu.VMEM((SUBLANES, T + HALO, LANES), F32),
        ],
        compiler_params=pltpu.CompilerParams(dimension_semantics=("arbitrary",), vmem_limit_bytes=VMEM_LIMIT),
    )(proj, x2d, tgt2d, meta_tile, conv_a_w, conv_a_b, ln_a_g, ln_a_b, b_a_out, conv_b_w, final_g,
      w_a, w_b, w_o, w_a_t, w_b_t, w_o_t)


def _input_bwd(dproj, ds1, x2d, meta_tile, norm_g, w_in_all, row_tile, ride):
    seq = x2d.shape[0]
    n_steps = seq // row_tile
    meta_block = seq // TILE

    def backward(dp_ref, ds1_ref, s0_ref, g_ref, w_ref, out_ref, vec_ref):
        dh = _dot_nt(dp_ref[:, 0:COLS], w_ref[0])
        for j in range(1, N_DEV):
            dh = dh + _dot_nt(dp_ref[:, j * COLS:(j + 1) * COLS], w_ref[j])
        s0v = s0_ref[...]
        r = lax.rsqrt(_rowmean(s0v * s0v) + EPS)
        gh = dh * g_ref[...]
        out_ref[...] = ds1_ref[...] + r * gh - s0v * ((r * r * r) * _rowmean(gh * s0v))
        vec_ref[0:1, :] += _colsum(dh * (s0v * r))

    def body(dp_ref, ds1_ref, x_ref, dpm_ref, ds1m_ref, meta_ref, g_ref, w_ref, gx_ref, gmeta_ref, vec_ref):
        t = pl.program_id(0)

        @pl.when(t == 0)
        def _():
            vec_ref[...] = jnp.zeros(vec_ref.shape, F32)

        backward(dp_ref, ds1_ref, x_ref, g_ref, w_ref, gx_ref, vec_ref)

        @pl.when(t == n_steps - 1)
        def _():
            backward(dpm_ref, ds1m_ref, meta_ref, g_ref, w_ref, gmeta_ref, vec_ref)

    rides = [("chips", ride)]
    ride_shapes, ride_sems = _ride_shapes(rides)
    body = _riding(body, 8, 3, rides, lambda: pl.program_id(0) == 0, lambda: pl.program_id(0) == n_steps - 1)
    return pl.pallas_call(
        body, name="input_bwd", grid=(n_steps,),
        out_shape=[jax.ShapeDtypeStruct(x2d.shape, F32), jax.ShapeDtypeStruct(meta_tile.shape, F32),
                   jax.ShapeDtypeStruct((SUBLANES, D_MODEL), F32)] + ride_shapes,
        in_specs=[pl.BlockSpec((row_tile, D_IN), lambda t: (t, 0)),
                  pl.BlockSpec((row_tile, D_MODEL), lambda t: (t, 0)),
                  pl.BlockSpec((row_tile, D_MODEL), lambda t: (t, 0)),
                  pl.BlockSpec((TILE, D_IN), lambda t: (meta_block, 0)),
                  pl.BlockSpec((TILE, D_MODEL), lambda t: (meta_block, 0)),
                  _VMEM, _VMEM, _resident((N_DEV, D_MODEL, COLS))] + [_ANY] * len(ride),
        out_specs=[pl.BlockSpec((row_tile, D_MODEL), lambda t: (t, 0)), _VMEM, _VMEM] + [_ANY] * len(ride),
        scratch_shapes=ride_sems,
        compiler_params=pltpu.CompilerParams(dimension_semantics=("arbitrary",), vmem_limit_bytes=VMEM_LIMIT),
    )(dproj, ds1, x2d, dproj, ds1, meta_tile, norm_g, w_in_all, *ride)


def _grad_w_in_half(pos, h_t, dproj, k_tile, other_side, rides, name):
    tp = h_t.shape[1]
    n_k = tp // k_tile

    def column_block(q, k, pos_ref):
        return k, 2 * q + (1 - pos_ref[2] if other_side else pos_ref[2])

    def body(pos_ref, h_ref, dp_ref, o_ref):
        @pl.when(pl.program_id(1) == 0)
        def _():
            o_ref[...] = jnp.zeros(o_ref.shape, F32)

        o_ref[...] += _dot(h_ref[...], dp_ref[...])

    ride = [a for _, arrays in rides for a in arrays]
    n_arr = len(ride)
    ride_shapes, ride_sems = _ride_shapes(rides)
    body = _riding(body, 3, 1, rides,
                   lambda: (pl.program_id(0) == 0) & (pl.program_id(1) == 0),
                   lambda: (pl.program_id(0) == 3) & (pl.program_id(1) == n_k - 1))
    return pl.pallas_call(
        body, name=name,
        out_shape=[jax.ShapeDtypeStruct((4, D_MODEL, COLS), F32)] + ride_shapes,
        grid_spec=pltpu.PrefetchScalarGridSpec(
            num_scalar_prefetch=1, grid=(4, n_k),
            in_specs=[pl.BlockSpec((D_MODEL, k_tile), lambda q, k, pos_ref: (0, k)),
                      pl.BlockSpec((k_tile, COLS), column_block)] + [_ANY] * n_arr,
            out_specs=[pl.BlockSpec((None, D_MODEL, COLS), lambda q, k, pos_ref: (q, 0, 0))] + [_ANY] * n_arr,
            scratch_shapes=ride_sems),
        compiler_params=pltpu.CompilerParams(dimension_semantics=("arbitrary", "arbitrary"),
                                             vmem_limit_bytes=VMEM_LIMIT),
    )(pos, h_t, dproj, *ride)


def _grad_w_out(lhs, rhs, k_tile):
    tp = lhs.shape[1]

    def body(a_ref, b_ref, o_ref):
        @pl.when(pl.program_id(1) == 0)
        def _():
            o_ref[...] = jnp.zeros(o_ref.shape, F32)

        o_ref[...] += _dot_tn(a_ref[...], b_ref[...]).reshape(N_DEV, ROWS_OUT, D_MODEL)

    return pl.pallas_call(
        body, name="grad_w_out", grid=(3, tp // k_tile),
        out_shape=jax.ShapeDtypeStruct((N_DEV, 3, ROWS_OUT, D_MODEL), F32),
        in_specs=[pl.BlockSpec((None, k_tile, D_MODEL), lambda w, k: (w, k, 0)),
                  pl.BlockSpec((None, k_tile, D_MODEL), lambda w, k: (w, k, 0))],
        out_specs=pl.BlockSpec((N_DEV, None, ROWS_OUT, D_MODEL), lambda w, k: (0, w, 0, 0)),
        compiler_params=pltpu.CompilerParams(dimension_semantics=("arbitrary", "arbitrary"),
                                             vmem_limit_bytes=VMEM_LIMIT),
    )(lhs, rhs)


def _adamw_math(w, g, m, v):
    m = ADAM_B1 * m + (1.0 - ADAM_B1) * g
    v = ADAM_B2 * v + (1.0 - ADAM_B2) * (g * g)
    m_hat = m / (1.0 - ADAM_B1 ** ADAM_STEP)
    v_hat = v / (1.0 - ADAM_B2 ** ADAM_STEP)
    delta = -ADAM_LR * (m_hat / (jnp.sqrt(v_hat) + ADAM_EPS) + ADAM_WD * w)
    return delta, m, v


def _adamw_sharded(pos, mine, theirs, landed, w, m, v, row_tile, block0, name):
    rows, n = w.shape
    n_slots = mine.shape[0]

    def mine_map(t, pos_ref):
        chip = 2 * pos_ref[0] + pos_ref[1]
        return (2 * chip + pos_ref[2] if n_slots == N_DEV else chip), block0 + t, 0

    def theirs_map(t, pos_ref):
        return 2 * pos_ref[0] + pos_ref[1], block0 + t, 0

    def body(pos_ref, mine_ref, theirs_ref, land_ref, w_ref, m_ref, v_ref, g_out, d_out, m_out, v_out):
        g = mine_ref[...] + theirs_ref[...]
        for k in range(3):
            g = g + land_ref[k].astype(F32)
        delta, m_new, v_new = _adamw_math(w_ref[...], g, m_ref[...], v_ref[...])
        g_out[...] = g
        d_out[...] = delta
        m_out[...] = m_new
        v_out[...] = v_new

    tile = pl.BlockSpec((row_tile, n), lambda t, pos_ref: (t, 0))
    return pl.pallas_call(
        body, name=name,
        out_shape=[jax.ShapeDtypeStruct((rows, n), F32)] * 4,
        grid_spec=pltpu.PrefetchScalarGridSpec(
            num_scalar_prefetch=1, grid=(rows // row_tile,),
            in_specs=[pl.BlockSpec((None, row_tile, n), mine_map), pl.BlockSpec((None, row_tile, n), theirs_map),
                      pl.BlockSpec((3, row_tile, n), lambda t, pos_ref: (0, block0 + t, 0)), tile, tile, tile],
            out_specs=[tile] * 4),
        compiler_params=pltpu.CompilerParams(dimension_semantics=("arbitrary",)),
    )(pos, mine, theirs, landed, w, m, v)


def _adamw_small(gathered, gathered_cols, params):
    n_par = len(params)

    def body(*refs):
        g_ref, gc_ref = refs[0], refs[1]
        ins = refs[2:2 + 3 * n_par]
        outs = refs[2 + 3 * n_par:]
        loss_ref = outs[4 * n_par]

        def reduced(ref, row, n_rows):
            g = ref[0, row:row + n_rows, :]
            for d in range(1, N_DEV):
                g = g + ref[d, row:row + n_rows, :]
            return g

        for p, (row, n_rows, sharded, _, _, _) in enumerate(params):
            g = reduced(gc_ref if sharded else g_ref, row, n_rows)
            w_ref, m_ref, v_ref = ins[3 * p:3 * p + 3]
            delta, m_new, v_new = _adamw_math(w_ref[...], g, m_ref[...], v_ref[...])
            outs[4 * p][...] = g
            outs[4 * p + 1][...] = delta
            outs[4 * p + 2][...] = m_new
            outs[4 * p + 3][...] = v_new
        loss = jnp.sum(reduced(g_ref, ROW_LOSS, 1), axis=1, keepdims=True)
        loss_ref[...] = jnp.broadcast_to(loss, loss_ref.shape)

    out_shape = []
    for (_, _, _, w, _, _) in params:
        out_shape += [jax.ShapeDtypeStruct(w.shape, F32)] * 4
    out_shape.append(jax.ShapeDtypeStruct((1, LANES), F32))
    flat = [a for (_, _, _, w, m, v) in params for a in (w, m, v)]
    return pl.pallas_call(
        body, name="adamw_small", out_shape=out_shape,
        in_specs=[_VMEM] * (2 + len(flat)), out_specs=[_VMEM] * len(out_shape),
    )(gathered, gathered_cols, *flat)


def _pad_rows(a, rows):
    return jnp.concatenate([a, jnp.zeros((rows - a.shape[0], a.shape[1]), a.dtype)], axis=0)


def kernel(x, meta_tokens, norm_g, w_in, conv_a_w, conv_a_b, ln_a_g, ln_a_b, w_a_out, b_a_out, conv_b_w, w_b_out, w_out, final_g, loss_target, m_meta_tokens, m_norm_g, m_w_in, m_conv_a_w, m_conv_a_b, m_ln_a_g, m_ln_a_b, m_w_a_out, m_b_a_out, m_conv_b_w, m_w_b_out, m_w_out, m_final_g, v_meta_tokens, v_norm_g, v_w_in, v_conv_a_w, v_conv_a_b, v_ln_a_g, v_ln_a_b, v_w_a_out, v_b_a_out, v_conv_b_w, v_w_b_out, v_w_out, v_final_g):
    seq = x.shape[1]
    assert x.shape == (1, seq, D_MODEL) and seq % TILE == 0 and w_in.shape == (1, D_MODEL, COLS)
    n_tiles = seq // TILE + 1
    tp = n_tiles * TILE
    pos = jnp.stack([lax.axis_index("x"), lax.axis_index("y"), lax.axis_index("c")]).astype(jnp.int32)
    me = 4 * pos[0] + 2 * pos[1] + pos[2]
    x2d = x[0]
    tgt2d = loss_target[0]

    small = jnp.concatenate([meta_tokens, _pad_rows(conv_a_w[0], 32), _pad_rows(conv_b_w[0], SUBLANES)], axis=0)
    small_all = _all_gather(small, "gather_small")
    small_all = small_all.transpose(1, 0, 2).reshape(small.shape[0], D_MODEL)
    meta_full, conv_a_full, conv_b_full = small_all[0:N_META], small_all[N_META:N_META + 32], small_all[N_META + 32:]
    meta_tile = jnp.concatenate([jnp.zeros((TILE - N_META, D_MODEL), F32), meta_full], axis=0)
    final_g2 = final_g.reshape(1, D_MODEL)

    w_out_shards = [w[0].astype(BF16) for w in (w_a_out, w_b_out, w_out)]
    h_t, proj, w_in_all, *w_out_all = _gather_norm_proj(pos, x2d, meta_tile, norm_g, w_in[0].astype(BF16),
                                                      w_out_shards, 3)
    w_out_all = [w.reshape(D_MODEL, D_MODEL) for w in w_out_all]
    w_out_all_t = [w.T for w in w_out_all]
    dproj, ds1, lhs, rhs, vec, d_conv_a, d_conv_b = _fused_pass(
        proj, x2d, tgt2d, meta_tile, conv_a_full, conv_a_b, ln_a_g, ln_a_b, b_a_out, conv_b_full, final_g2,
        w_out_all[0], w_out_all[1], w_out_all[2], w_out_all_t[0], w_out_all_t[1], w_out_all_t[2], n_tiles)
    k_tile = tp // 3
    gw_out = _grad_w_out(lhs, rhs, k_tile).reshape(N_DEV, 3 * ROWS_OUT, D_MODEL)
    gw_far, their_out = _grad_w_in_half(pos, h_t, dproj, k_tile, True, [("sibling", (gw_out,))], "grad_w_in_far")
    parts_out = _chip_partial(pos, gw_out, their_out, (1, 2, 3), BF16, ROWS_OUT, "rs_parts_w_out")
    gw_near, their_in, land_out = _grad_w_in_half(
        pos, h_t, dproj, k_tile, False, [("sibling_half", (gw_far,)), ("chips", (parts_out,))], "grad_w_in_near")
    parts_in = _chip_partial(pos, gw_near, their_in, (1, 2, 3), BF16, 256, "rs_parts_w_in")
    grad_x, d_meta_tile, vec_in, land_in = _input_bwd(dproj, ds1, x2d, meta_tile, norm_g, w_in_all, min(256, seq),
                                                      (parts_in,))

    small_g = jnp.concatenate([vec, vec_in, d_conv_a, d_conv_b, d_meta_tile[TILE - N_META:]], axis=0)
    small_g_all = _all_gather(small_g, "gather_small_grads")
    small_g_cols = lax.dynamic_slice_in_dim(small_g_all, me * LANES, LANES, axis=2)

    res_in = _adamw_sharded(pos, gw_near, their_in, land_in, w_in[0], m_w_in[0], v_w_in[0], 128, 0, "adamw_w_in")
    res_out = [
        _adamw_sharded(pos, gw_out, their_out, land_out, w[0], m[0], v[0], ROWS_OUT, k, f"adamw_w_out{k}")
        for k, (w, m, v) in enumerate([(w_a_out, m_w_a_out, v_w_a_out), (w_b_out, m_w_b_out, v_w_b_out),
                                       (w_out, m_w_out, v_w_out)])]
    params = [
        (ROW_META, N_META, True, meta_tokens, m_meta_tokens, v_meta_tokens),
        (ROW_NORM_G, 1, False, norm_g, m_norm_g, v_norm_g),
        (ROW_CONV_A_W, CONV_A, True, conv_a_w[0], m_conv_a_w[0], v_conv_a_w[0]),
        (ROW_CONV_A_B, 1, False, conv_a_b, m_conv_a_b, v_conv_a_b),
        (ROW_LN_G, 1, False, ln_a_g, m_ln_a_g, v_ln_a_g),
        (ROW_LN_B, 1, False, ln_a_b, m_ln_a_b, v_ln_a_b),
        (ROW_B_A_OUT, 1, False, b_a_out, m_b_a_out, v_b_a_out),
        (ROW_CONV_B_W, CONV_B, True, conv_b_w[0], m_conv_b_w[0], v_conv_b_w[0]),
        (ROW_FINAL_G, 1, False, final_g2, m_final_g.reshape(1, D_MODEL), v_final_g.reshape(1, D_MODEL)),
    ]
    res_small = _adamw_small(small_g_all, small_g_cols, params)
    loss = res_small[-1][0, 0]

    def small_res(p, kind, shape):
        return res_small[4 * p + kind].reshape(shape)

    per_weight = []
    for kind in range(4):
        per_weight.append([
            small_res(0, kind, meta_tokens.shape),
            small_res(1, kind, norm_g.shape),
            res_in[kind].reshape(w_in.shape),
            small_res(2, kind, conv_a_w.shape),
            small_res(3, kind, conv_a_b.shape),
            small_res(4, kind, ln_a_g.shape),
            small_res(5, kind, ln_a_b.shape),
            res_out[0][kind].reshape(w_a_out.shape),
            small_res(6, kind, b_a_out.shape),
            small_res(7, kind, conv_b_w.shape),
            res_out[1][kind].reshape(w_b_out.shape),
            res_out[2][kind].reshape(w_out.shape),
            small_res(8, kind, final_g.shape),
        ])
    return (loss, grad_x.reshape(x.shape), *per_weight[0], *per_weight[1], *per_weight[2], *per_weight[3])
```

```python
import functools

import jax
import jax.numpy as jnp
from jax import lax
from jax.experimental import pallas as pl
from jax.experimental.pallas import tpu as pltpu

D_MODEL = 1024
N_META = 16
N_DEV = 8
D_IN = 9 * D_MODEL
COLS = D_IN // N_DEV
ROWS_OUT = D_MODEL // N_DEV
CONV_A = 31
CONV_B = 3
EPS = 1e-6

ADAM_LR = 0.001
ADAM_B1 = 0.9
ADAM_B2 = 0.999
ADAM_EPS = 1e-08
ADAM_WD = 0.01
ADAM_STEP = 10

TILE = 128
LANES = 128
N_CHUNK = D_MODEL // LANES
HALO = 32
SUBLANES = 8
VMEM_LIMIT = 56 * 1024 * 1024

ROW_FINAL_G, ROW_B_A_OUT, ROW_LN_G, ROW_LN_B, ROW_CONV_A_B, ROW_LOSS = 0, 1, 2, 3, 4, 5
ROW_NORM_G = 8
ROW_CONV_A_W, ROW_CONV_B_W, ROW_META = 16, 48, 56
SMALL_ROWS = 72

MESH = pl.DeviceIdType.MESH
_ANY = pl.BlockSpec(memory_space=pl.ANY)
_VMEM = pl.BlockSpec(memory_space=pltpu.VMEM)


def _resident(shape):
    return pl.BlockSpec(shape, lambda *_: (0,) * len(shape), pipeline_mode=pl.Buffered(1))
BF16 = jnp.bfloat16
F32 = jnp.float32


def _sigmoid(v):
    return jax.nn.sigmoid(v)


def _dot(a, b):
    return jnp.dot(a, b, preferred_element_type=F32)


def _dot_nt(a, b):
    return lax.dot_general(a, b, (((1,), (1,)), ((), ())), preferred_element_type=F32)


def _dot_tn(a, b):
    return lax.dot_general(a, b, (((0,), (0,)), ((), ())), preferred_element_type=F32)


def _colsum(v):
    return jnp.sum(v, axis=0, keepdims=True)


def _rowmean(v):
    parts = [v[:, LANES * c:LANES * (c + 1)] for c in range(v.shape[1] // LANES)]
    return jnp.sum(functools.reduce(jnp.add, parts), axis=-1, keepdims=True) * (1.0 / v.shape[1])


def _fold8(v):
    parts = [v[SUBLANES * g:SUBLANES * (g + 1)] for g in range(v.shape[0] // SUBLANES)]
    return functools.reduce(jnp.add, parts)


def _all_gather(shard, name, ride=()):
    m, n = shard.shape
    n_arr = len(ride)
    ride_shapes, ride_sems = _exchange_shapes("chips", ride) if ride else ([], [])

    def body(*refs):
        x_ref, out_ref = refs[0], refs[1 + n_arr]
        send_sems, recv_sems, local_sem = refs[2 + 2 * n_arr:5 + 2 * n_arr]
        riding = _chip_copies(refs[1:1 + n_arr], refs[2 + n_arr:2 + 2 * n_arr], *refs[5 + 2 * n_arr:]) if ride else []
        for cp in riding:
            cp.start()
        x, y, c = lax.axis_index("x"), lax.axis_index("y"), lax.axis_index("c")
        me, sibling = (x, y, c), (x, y, 1 - c)
        chips = [(1 - x, y), (x, 1 - y), (1 - x, 1 - y)]

        def slot(px, py, pc):
            return out_ref.at[4 * px + 2 * py + pc]

        def copy(k, block, to, src=None):
            return pltpu.make_async_remote_copy(
                src_ref=slot(*block) if src is None else src, dst_ref=slot(*block),
                send_sem=send_sems.at[k], recv_sem=recv_sems.at[k], device_id=to, device_id_type=MESH)

        mine = pltpu.make_async_copy(x_ref, slot(*me), local_sem)
        mine.start()
        first = [copy(0, me, sibling, src=x_ref)]
        first += [copy(1 + j, me, (*chip, c), src=x_ref) for j, chip in enumerate(chips)]
        for cp in first:
            cp.start()
        passed = [copy(4 + j, (*chip, c), sibling) for j, chip in enumerate(chips)]
        for j, chip in enumerate(chips):
            copy(1 + j, (*chip, c), me).wait_recv()
            passed[j].start()
        copy(0, sibling, me).wait_recv()
        for j, chip in enumerate(chips):
            copy(4 + j, (*chip, 1 - c), me).wait_recv()
        for cp in first + passed:
            cp.wait_send()
        mine.wait()
        for cp in riding:
            cp.wait()

    res = pl.pallas_call(
        body, name=name,
        out_shape=[jax.ShapeDtypeStruct((N_DEV, m, n), shard.dtype)] + ride_shapes,
        in_specs=[_ANY] * (1 + n_arr), out_specs=[_ANY] * (1 + n_arr),
        scratch_shapes=[pltpu.SemaphoreType.DMA((7,)), pltpu.SemaphoreType.DMA((7,)), pltpu.SemaphoreType.DMA(())]
        + ride_sems,
    )(shard, *ride)
    return res if ride else res[0]


def _sibling_copies(srcs, dsts, send_sems, recv_sems):
    x, y, c = lax.axis_index("x"), lax.axis_index("y"), lax.axis_index("c")
    return [pltpu.make_async_remote_copy(
        src_ref=src.at[2 * q + (1 - c)], dst_ref=dst.at[q],
        send_sem=send_sems.at[4 * a + q], recv_sem=recv_sems.at[4 * a + q],
        device_id=(x, y, 1 - c), device_id_type=MESH)
        for a, (src, dst) in enumerate(zip(srcs, dsts)) for q in range(4)]


def _chip_copies(srcs, dsts, send_sems, recv_sems):
    x, y, c = lax.axis_index("x"), lax.axis_index("y"), lax.axis_index("c")
    targets = [(x, 1 - y, c), (1 - x, y, c), (1 - x, 1 - y, c)]
    return [pltpu.make_async_remote_copy(
        src_ref=src.at[k], dst_ref=dst.at[k],
        send_sem=send_sems.at[3 * a + k], recv_sem=recv_sems.at[3 * a + k],
        device_id=targets[k], device_id_type=MESH)
        for a, (src, dst) in enumerate(zip(srcs, dsts)) for k in range(3)]


def _sibling_half_copies(srcs, dsts, send_sems, recv_sems):
    x, y, c = lax.axis_index("x"), lax.axis_index("y"), lax.axis_index("c")
    return [pltpu.make_async_remote_copy(
        src_ref=src.at[q], dst_ref=dst.at[q],
        send_sem=send_sems.at[4 * a + q], recv_sem=recv_sems.at[4 * a + q],
        device_id=(x, y, 1 - c), device_id_type=MESH)
        for a, (src, dst) in enumerate(zip(srcs, dsts)) for q in range(4)]


_EXCHANGES = {"sibling": (4, _sibling_copies, 4), "sibling_half": (4, _sibling_half_copies, 4),
              "chips": (3, _chip_copies, 3)}


def _exchange_shapes(kind, arrays):
    per_array, _, slots = _EXCHANGES[kind]
    out_shape = [jax.ShapeDtypeStruct((slots,) + a.shape[1:], a.dtype) for a in arrays]
    sems = [pltpu.SemaphoreType.DMA((per_array * len(arrays),))] * 2
    return out_shape, sems


def _ride_shapes(rides):
    shapes, sems = [], []
    for kind, arrays in rides:
        ride_shapes, ride_sems = _exchange_shapes(kind, arrays)
        shapes += ride_shapes
        sems += ride_sems
    return shapes, sems


def _riding(body, n_in, n_out, rides, is_first, is_last):
    counts = [len(arrays) for _, arrays in rides]
    n_arr = sum(counts)

    def wrapped(*refs):
        ins, srcs = refs[:n_in], refs[n_in:n_in + n_arr]
        outs = refs[n_in + n_arr:n_in + n_arr + n_out]
        dsts = refs[n_in + n_arr + n_out:n_in + 2 * n_arr + n_out]
        first_sem = len(refs) - 2 * len(rides)
        scratch, sems = refs[n_in + 2 * n_arr + n_out:first_sem], refs[first_sem:]

        def copies():
            made, at = [], 0
            for r, ((kind, _), n) in enumerate(zip(rides, counts)):
                made += _EXCHANGES[kind][1](srcs[at:at + n], dsts[at:at + n], sems[2 * r], sems[2 * r + 1])
                at += n
            return made

        @pl.when(is_first())
        def _():
            for cp in copies():
                cp.start()

        body(*ins, *outs, *scratch)

        @pl.when(is_last())
        def _():
            for cp in copies():
                cp.wait()

    return wrapped


def _chip_partial(pos, mine, theirs, relations, out_dtype, row_tile, name):
    n_slots, m, n = mine.shape
    q0 = relations[0]

    def chip_of(qi, pos_ref):
        q = qi + q0
        return pos_ref[0] ^ (q >> 1), pos_ref[1] ^ (q & 1)

    def mine_map(qi, t, pos_ref):
        px, py = chip_of(qi, pos_ref)
        return (4 * px + 2 * py + pos_ref[2] if n_slots == N_DEV else 2 * px + py), t, 0

    def theirs_map(qi, t, pos_ref):
        px, py = chip_of(qi, pos_ref)
        return 2 * px + py, t, 0

    def body(pos_ref, a_ref, b_ref, o_ref):
        o_ref[...] = (a_ref[...] + b_ref[...]).astype(out_dtype)

    return pl.pallas_call(
        body, name=name,
        out_shape=jax.ShapeDtypeStruct((len(relations), m, n), out_dtype),
        grid_spec=pltpu.PrefetchScalarGridSpec(
            num_scalar_prefetch=1, grid=(len(relations), m // row_tile),
            in_specs=[pl.BlockSpec((None, row_tile, n), mine_map), pl.BlockSpec((None, row_tile, n), theirs_map)],
            out_specs=pl.BlockSpec((None, row_tile, n), lambda qi, t, pos_ref: (qi, t, 0))),
        compiler_params=pltpu.CompilerParams(dimension_semantics=("arbitrary", "arbitrary")),
    )(pos, mine, theirs)


PARTS = ((0, 512), (512, 640))


def _gather_norm_proj(pos, x2d, meta_tile, norm_g, w_in_shard, w_out_shards, n_chunk):
    seq = x2d.shape[0]
    n_tiles = seq // TILE + 1
    tp = n_tiles * TILE
    n_parts = len(PARTS)
    widest = max(width for _, width in PARTS)
    units = [(s, u) for s in range(2) for u in range(n_parts)]
    for first in (2, 5):
        units += [(first + j, u) for u in range(n_parts) for j in range(2)] + [(first + 2, u) for u in range(n_parts)]
    n_units = len(units)
    n_steps = n_tiles + n_units
    chunk = tp // n_chunk

    def body(pos_ref, x_ref, meta_ref, g_ref, win_ref, wa_ref, wb_ref, wo_ref,
             ht_ref, proj_ref, win_all, wa_all, wb_all, wo_all,
             h_all, wbuf, rbuf, send_sems, recv_sems, local_sems):
        g = pl.program_id(0)
        x, y, c = lax.axis_index("x"), lax.axis_index("y"), lax.axis_index("c")
        me, sibling = (x, y, c), (x, y, 1 - c)
        chips = [(1 - x, y), (x, 1 - y), (1 - x, 1 - y)]
        shards = (win_ref, wa_ref, wb_ref, wo_ref)
        gathered = (win_all, wa_all, wb_all, wo_all)
        blocks = [me, sibling] + [(*chip, c) for chip in chips] + [(*chip, 1 - c) for chip in chips]

        def index(block):
            px, py, pc = block
            return 4 * px + 2 * py + pc

        def part(ref, a, u):
            return ref.at[:, pl.ds(PARTS[u][0], PARTS[u][1])] if a == 0 else ref

        def slot(a, block, u):
            return part(gathered[a].at[index(block)], a, u)

        def sem(a, k, u):
            return n_parts * k + u if a == 0 else 7 * n_parts + 7 * (a - 1) + k

        def copy(a, k, block, to, u=0, from_shard=False):
            return pltpu.make_async_remote_copy(
                src_ref=part(shards[a], a, u) if from_shard else slot(a, block, u), dst_ref=slot(a, block, u),
                send_sem=send_sems.at[sem(a, k, u)], recv_sem=recv_sems.at[sem(a, k, u)],
                device_id=to, device_id_type=MESH)

        def keep(a):
            return pltpu.make_async_copy(shards[a], gathered[a].at[index(me)], local_sems.at[a])

        def load(m):
            s, u = units[m]
            src = part(win_ref, 0, u) if s == 0 else slot(0, blocks[s], u)
            return pltpu.make_async_copy(src, wbuf.at[m % 2, :, 0:PARTS[u][1]], local_sems.at[4 + m % 2])

        def store(m):
            s, u = units[m]
            col0 = pl.multiple_of(index(blocks[s]) * COLS + PARTS[u][0], LANES)
            return pltpu.make_async_copy(rbuf.at[m % 2, :, 0:PARTS[u][1]],
                                         proj_ref.at[:, pl.ds(col0, PARTS[u][1])], local_sems.at[6 + m % 2])

        def by_x(a, u):
            return u == 0 if a == 0 else a < 3

        def relay(a, u=0):
            src, to = (blocks[3], blocks[2]) if by_x(a, u) else (blocks[2], blocks[3])
            return copy(a, 3, src, to, u)

        def arrive(m):
            s, u = units[m]
            if s == 1:
                copy(0, 0, sibling, me, u).wait_recv()
            elif 2 <= s <= 4:
                copy(0, s - 1, blocks[s], me, u).wait_recv()
                copy(0, s + 2, blocks[s], sibling, u).start()
                if s < 4 and by_x(0, u) == (s == 3):
                    relay(0, u).start()
            elif s >= 5:
                copy(0, s - 1, blocks[s], me, u).wait_recv()
                if u == 0:
                    for a in range(1, 4):
                        copy(a, s - 4, blocks[s - 3], me).wait_recv()
                        copy(a, s - 1, blocks[s - 3], sibling).start()
                        if s < 7 and by_x(a, 0) == (s == 6):
                            relay(a).start()

        targets = [sibling, blocks[2], blocks[3]]

        @pl.when(g == 0)
        def _():
            for a in range(4):
                keep(a).start()
            for u in range(n_parts):
                for k, to in enumerate(targets):
                    copy(0, k, me, to, u, from_shard=True).start()
            for a in range(1, 4):
                for k, to in enumerate(targets):
                    copy(a, k, me, to, from_shard=True).start()
            load(0).start()

        @pl.when(g < n_tiles)
        def _():
            s0 = jnp.where(g == n_tiles - 1, meta_ref[...], x_ref[...])
            r = lax.rsqrt(_rowmean(s0 * s0) + EPS)
            h32 = (s0 * r) * g_ref[...]
            ht_ref[...] = h32.T.astype(BF16)
            h_all[pl.ds(pl.multiple_of(g * TILE, TILE), TILE), :] = h32.astype(BF16)

        for m in range(n_units):
            @pl.when(g == n_tiles + m)
            def _(m=m):
                load(m).wait()
                if m + 1 < n_units:
                    arrive(m + 1)
                    load(m + 1).start()
                if m >= 2:
                    store(m - 2).wait()

        m_now = jnp.maximum(g - n_tiles, 0)
        u_now = functools.reduce(jnp.add, [jnp.where(m_now == m, u, 0) for m, (_, u) in enumerate(units)])
        for u, (_, width) in enumerate(PARTS):
            @pl.when((g >= n_tiles) & (u_now == u))
            def _(width=width):
                w = wbuf[m_now % 2, :, 0:width]
                for r in range(n_chunk):
                    rbuf[m_now % 2, r * chunk:(r + 1) * chunk, 0:width] = _dot(h_all[r * chunk:(r + 1) * chunk, :], w)

        for m in range(n_units):
            @pl.when(g == n_tiles + m)
            def _(m=m):
                store(m).start()

        @pl.when(g == n_steps - 1)
        def _():
            store(n_units - 2).wait()
            store(n_units - 1).wait()
            for a in range(1, 4):
                copy(a, 0, sibling, me).wait_recv()
                for j in range(3):
                    copy(a, 4 + j, blocks[5 + j], me).wait_recv()
            for a in range(4):
                for u in range(n_parts if a == 0 else 1):
                    for k, to in enumerate(targets):
                        copy(a, k, me, to, u, from_shard=True).wait_send()
                    relay(a, u).wait_send()
                    for j in range(3):
                        copy(a, 4 + j, blocks[2 + j], sibling, u).wait_send()
                keep(a).wait()

    n_x = n_tiles - 1
    return pl.pallas_call(
        body, name="gather_norm_proj",
        out_shape=[jax.ShapeDtypeStruct((D_MODEL, tp), BF16), jax.ShapeDtypeStruct((tp, D_IN), F32),
                   jax.ShapeDtypeStruct((N_DEV,) + w_in_shard.shape, BF16)]
                  + [jax.ShapeDtypeStruct((N_DEV,) + w.shape, BF16) for w in w_out_shards],
        grid_spec=pltpu.PrefetchScalarGridSpec(
            num_scalar_prefetch=1, grid=(n_steps,),
            in_specs=[pl.BlockSpec((TILE, D_MODEL), lambda g, pos_ref: (jnp.minimum(g, n_x - 1), 0)),
                      _VMEM, _VMEM, _ANY, _ANY, _ANY, _ANY],
            out_specs=[pl.BlockSpec((D_MODEL, TILE), lambda g, pos_ref: (0, jnp.minimum(g, n_tiles - 1))),
                       _ANY, _ANY, _ANY, _ANY, _ANY],
            scratch_shapes=[pltpu.VMEM((tp, D_MODEL), BF16), pltpu.VMEM((2, D_MODEL, widest), BF16),
                            pltpu.VMEM((2, tp, widest), F32),
                            pltpu.SemaphoreType.DMA((7 * n_parts + 21,)), pltpu.SemaphoreType.DMA((7 * n_parts + 21,)),
                            pltpu.SemaphoreType.DMA((8,))]),
        compiler_params=pltpu.CompilerParams(dimension_semantics=("arbitrary",), vmem_limit_bytes=VMEM_LIMIT),
    )(pos, x2d, meta_tile, norm_g, w_in_shard, *w_out_shards)


C_AVAL, C_AGLU, C_AZ, C_BB, C_BC, C_BX, C_BZ, C_GA, C_GB = (k * D_MODEL for k in range(9))
S_AZ, S_BB, S_BZ, S_GA, S_GB = (k * D_MODEL for k in range(5))


def _fused_pass(proj, x2d, tgt2d, meta_tile, conv_a_w, conv_a_b, ln_a_g, ln_a_b, b_a_out, conv_b_w, final_g,
                w_a, w_b, w_o, w_a_t, w_b_t, w_o_t, n_tiles):
    T = TILE
    tp = n_tiles * T
    inv_d = 1.0 / D_MODEL

    def block_of(tile):
        return jnp.where(tile == 0, n_tiles - 1, tile - 1)

    def cur(i):
        return block_of(jnp.minimum(i, n_tiles - 1))

    def prev(i):
        return block_of(jnp.clip(i - 1, 0, n_tiles - 1))

    def xblk(i):
        return jnp.maximum(jnp.minimum(i, n_tiles - 1) - 1, 0)

    def body(proj_ref, x_ref, tgt_ref, meta_ref, caw_ref, cab_ref, lng_ref, lnb_ref, bao_ref, cbw_ref, fg_ref,
             wa_ref, wb_ref, wo_ref, wat_ref, wbt_ref, wot_ref,
             dproj_ref, ds1_ref, lhs_ref, rhs_ref, vec_ref, dcaw_ref, dcbw_ref,
             ua0_buf, cb_buf, dua1_buf, dc3_buf, aprev, cprev, stage, ua1_buf, c3_buf,
             dpa_buf, dpb_buf, dcaw8, dcbw8, shift_buf):
        i = pl.program_id(0)

        @pl.when(i == 0)
        def _init():
            for buf in (ua0_buf, cb_buf, dua1_buf, dc3_buf, aprev, cprev, dcaw8, dcbw8):
                buf[...] = jnp.zeros(buf.shape, buf.dtype)
            vec_ref[...] = jnp.zeros(vec_ref.shape, F32)

        @pl.when(i >= 1)
        def _emit_stage():
            dproj_ref[:, C_AZ:C_BC] = stage[:, S_AZ:S_BZ]
            dproj_ref[:, C_BZ:D_IN] = stage[:, S_BZ:S_GB + D_MODEL]

        @pl.when(i < n_tiles)
        def _front():
            def conv_chunk(cc, carry):
                c0 = pl.multiple_of(cc * LANES, LANES)
                lanes = pl.ds(c0, LANES)

                def col(base):
                    return pl.ds(pl.multiple_of(base + cc * LANES, LANES), LANES)

                ua0 = proj_ref[:, col(C_AVAL)] * _sigmoid(proj_ref[:, col(C_AGLU)])
                ua0_buf[T:2 * T, lanes] = ua0
                acc = jnp.broadcast_to(cab_ref[:, lanes], (T, LANES))
                lead = HALO - (CONV_A - 1)
                for r in range(SUBLANES):
                    taps = [k for k in range(CONV_A) if (k + lead) % SUBLANES == r]
                    rows = T + SUBLANES * max((k + lead) // SUBLANES for k in taps)
                    if r:
                        shift_buf[r, 0:rows, :] = ua0_buf[pl.ds(T - HALO + r, rows), lanes]
                    for k in taps:
                        q = (k + lead) // SUBLANES
                        if r:
                            win = shift_buf[r, SUBLANES * q:SUBLANES * q + T, :]
                        else:
                            win = ua0_buf[pl.ds(T - HALO + SUBLANES * q, T), lanes]
                        acc = acc + caw_ref[k:k + 1, lanes] * win
                ua1_buf[:, lanes] = acc
                cb = proj_ref[:, col(C_BC)] * proj_ref[:, col(C_BX)]
                cb_buf[T:2 * T, lanes] = cb
                acc3 = cbw_ref[0:1, lanes] * cb_buf[pl.ds(T - 2, T), lanes]
                for k in range(1, CONV_B):
                    acc3 = acc3 + cbw_ref[k:k + 1, lanes] * cb_buf[pl.ds(T - (CONV_B - 1) + k, T), lanes]
                c3_buf[:, lanes] = acc3
                return carry

            lax.fori_loop(0, N_CHUNK, conv_chunk, 0)

            ua1 = ua1_buf[...]
            xc = ua1 - _rowmean(ua1)
            rstd = lax.rsqrt(_rowmean(xc * xc) + EPS)
            xhat = xc * rstd
            ua2 = xhat * lng_ref[...] + lnb_ref[...]
            sg2 = _sigmoid(ua2)
            ua3 = ua2 * sg2
            a_z = proj_ref[:, C_AZ:C_AZ + D_MODEL]
            sz = _sigmoid(a_z)
            silu_az = a_z * sz
            lhs_ref[0] = (ua3 * silu_az).astype(BF16)
            b_z = proj_ref[:, C_BZ:C_BZ + D_MODEL]
            sbz = _sigmoid(b_z)
            silu_bz = b_z * sbz
            b_b = proj_ref[:, C_BB:C_BB + D_MODEL]
            c3 = c3_buf[...]
            ub = b_b * c3
            lhs_ref[1] = (ub * silu_bz).astype(BF16)

            ya = _dot(lhs_ref[0], wa_ref[...]) + bao_ref[...]
            yb = _dot(lhs_ref[1], wb_ref[...])
            sga = _sigmoid(proj_ref[:, C_GA:C_GA + D_MODEL])
            sgb = _sigmoid(proj_ref[:, C_GB:C_GB + D_MODEL])
            m_b = (sga * ya + sgb * yb).astype(BF16)
            lhs_ref[2] = m_b
            s0 = jnp.where(i == 0, meta_ref[...], x_ref[...])
            s1 = s0 + _dot(m_b, wo_ref[...])
            r1 = lax.rsqrt(_rowmean(s1 * s1) + EPS)
            y = (s1 * r1) * fg_ref[...]
            is_token = (i >= 1).astype(F32)
            err = (y - tgt_ref[...]) * is_token
            vec_ref[ROW_LOSS:ROW_LOSS + 1, :] += (0.5 * inv_d) * _colsum(err * err)
            dy = err * inv_d
            vec_ref[ROW_FINAL_G:ROW_FINAL_G + 1, :] += _colsum(dy * (s1 * r1))
            gy = dy * fg_ref[...]
            ds1 = r1 * gy - s1 * ((r1 * r1 * r1) * _rowmean(gy * s1))
            ds1_ref[...] = ds1
            ds1_b = ds1.astype(BF16)
            rhs_ref[2] = ds1_b
            dm = _dot(ds1_b, wot_ref[...])
            dya = dm * sga
            dyb = dm * sgb
            stage[:, S_GA:S_GA + D_MODEL] = (dya * ya * (1.0 - sga)).astype(BF16)
            stage[:, S_GB:S_GB + D_MODEL] = (dyb * yb * (1.0 - sgb)).astype(BF16)
            vec_ref[ROW_B_A_OUT:ROW_B_A_OUT + 1, :] += _colsum(dya)
            dya_b = dya.astype(BF16)
            dyb_b = dyb.astype(BF16)
            rhs_ref[0] = dya_b
            rhs_ref[1] = dyb_b
            dpa_buf[...] = _dot(dya_b, wat_ref[...])
            dpb_buf[...] = _dot(dyb_b, wbt_ref[...])

            dpa = dpa_buf[...]
            stage[:, S_AZ:S_AZ + D_MODEL] = (dpa * ua3 * (sz + silu_az * (1.0 - sz))).astype(BF16)
            dua2 = dpa * silu_az * (sg2 + ua3 * (1.0 - sg2))
            vec_ref[ROW_LN_G:ROW_LN_G + 1, :] += _colsum(dua2 * xhat)
            vec_ref[ROW_LN_B:ROW_LN_B + 1, :] += _colsum(dua2)
            dxh = dua2 * lng_ref[...]
            dua1 = rstd * (dxh - _rowmean(dxh) - xhat * _rowmean(dxh * xhat))
            vec_ref[ROW_CONV_A_B:ROW_CONV_A_B + 1, :] += _colsum(dua1)
            dua1_buf[T:2 * T, :] = dua1
            dpb = dpb_buf[...]
            stage[:, S_BZ:S_BZ + D_MODEL] = (dpb * ub * (sbz + silu_bz * (1.0 - sbz))).astype(BF16)
            dub = dpb * silu_bz
            stage[:, S_BB:S_BB + D_MODEL] = (dub * c3).astype(BF16)
            dc3_buf[T:2 * T, :] = dub * b_b

        @pl.when(i == n_tiles)
        def _no_later_tile():
            dua1_buf[T:2 * T, :] = jnp.zeros((T, D_MODEL), F32)
            dc3_buf[T:2 * T, :] = jnp.zeros((T, D_MODEL), F32)

        @pl.when(i >= 1)
        def _lagged():
            def convt_chunk(cc, carry):
                c0 = pl.multiple_of(cc * LANES, LANES)
                lanes = pl.ds(c0, LANES)

                def col(base):
                    return pl.ds(pl.multiple_of(base + cc * LANES, LANES), LANES)

                ua0 = ua0_buf[0:T, lanes]
                acc = jnp.zeros((T, LANES), F32)
                for r in range(SUBLANES):
                    shifts = [j for j in range(CONV_A) if j % SUBLANES == r]
                    rows = T + shifts[-1] - r
                    if r:
                        shift_buf[r, 0:rows, :] = dua1_buf[pl.ds(r, rows), lanes]
                    for j in shifts:
                        k = CONV_A - 1 - j
                        if r:
                            later = shift_buf[r, j - r:j - r + T, :]
                        else:
                            later = dua1_buf[pl.ds(j, T), lanes]
                        acc = acc + caw_ref[k:k + 1, lanes] * later
                        dcaw8[SUBLANES * k:SUBLANES * (k + 1), lanes] += _fold8(ua0 * later)
                a_val = aprev[:, col(0)]
                sg = _sigmoid(aprev[:, col(D_MODEL)])
                dproj_ref[:, col(C_AVAL)] = (acc * sg).astype(BF16)
                dproj_ref[:, col(C_AGLU)] = (acc * a_val * (sg * (1.0 - sg))).astype(BF16)

                cb = cb_buf[0:T, lanes]
                acc3 = jnp.zeros((T, LANES), F32)
                for j in range(CONV_B):
                    k = CONV_B - 1 - j
                    later = dc3_buf[pl.ds(j, T), lanes]
                    acc3 = acc3 + cbw_ref[k:k + 1, lanes] * later
                    dcbw8[SUBLANES * k:SUBLANES * (k + 1), lanes] += _fold8(cb * later)
                dproj_ref[:, col(C_BC)] = (acc3 * cprev[:, col(D_MODEL)]).astype(BF16)
                dproj_ref[:, col(C_BX)] = (acc3 * cprev[:, col(0)]).astype(BF16)
                return carry

            lax.fori_loop(0, N_CHUNK, convt_chunk, 0)

        for buf in (ua0_buf, cb_buf, dua1_buf, dc3_buf):
            buf[0:T, :] = buf[T:2 * T, :]
        aprev[...] = proj_ref[:, C_AVAL:C_AZ]
        cprev[...] = proj_ref[:, C_BC:C_BZ]

        @pl.when(i == n_tiles)
        def _finish():
            for k in range(CONV_A):
                dcaw_ref[k:k + 1, :] = _colsum(dcaw8[SUBLANES * k:SUBLANES * (k + 1), :])
            dcaw_ref[CONV_A:CONV_A + 1, :] = jnp.zeros((1, D_MODEL), F32)
            for k in range(CONV_B):
                dcbw_ref[k:k + 1, :] = _colsum(dcbw8[SUBLANES * k:SUBLANES * (k + 1), :])
            dcbw_ref[CONV_B:SUBLANES, :] = jnp.zeros((SUBLANES - CONV_B, D_MODEL), F32)

    tile_in = lambda width: pl.BlockSpec((T, width), lambda i: (cur(i), 0))
    return pl.pallas_call(
        body, name="fused_pass", grid=(n_tiles + 1,),
        out_shape=[
            jax.ShapeDtypeStruct((tp, D_IN), BF16),
            jax.ShapeDtypeStruct((tp, D_MODEL), F32),
            jax.ShapeDtypeStruct((3, tp, D_MODEL), BF16),
            jax.ShapeDtypeStruct((3, tp, D_MODEL), BF16),
            jax.ShapeDtypeStruct((SUBLANES, D_MODEL), F32),
            jax.ShapeDtypeStruct((32, D_MODEL), F32),
            jax.ShapeDtypeStruct((SUBLANES, D_MODEL), F32),
        ],
        in_specs=[
            tile_in(D_IN),
            pl.BlockSpec((T, D_MODEL), lambda i: (xblk(i), 0)),
            pl.BlockSpec((T, D_MODEL), lambda i: (xblk(i), 0)),
            _VMEM, _VMEM, _VMEM, _VMEM, _VMEM, _VMEM, _VMEM, _VMEM,
            *[_resident((D_MODEL, D_MODEL)) for _ in range(6)],
        ],
        out_specs=[
            pl.BlockSpec((T, D_IN), lambda i: (prev(i), 0)),
            pl.BlockSpec((T, D_MODEL), lambda i: (cur(i), 0)),
            pl.BlockSpec((3, T, D_MODEL), lambda i: (0, cur(i), 0)),
            pl.BlockSpec((3, T, D_MODEL), lambda i: (0, cur(i), 0)),
            _VMEM, _VMEM, _VMEM,
        ],
        scratch_shapes=[
            pltpu.VMEM((2 * T, D_MODEL), F32),
            pltpu.VMEM((2 * T, D_MODEL), F32),
            pltpu.VMEM((2 * T, D_MODEL), F32),
            pltpu.VMEM((2 * T, D_MODEL), F32),
            pltpu.VMEM((T, 2 * D_MODEL), F32),
            pltpu.VMEM((T, 2 * D_MODEL), F32),
            pltpu.VMEM((T, 5 * D_MODEL), BF16),
            pltpu.VMEM((T, D_MODEL), F32),
            pltpu.VMEM((T, D_MODEL), F32),
            pltpu.VMEM((T, D_MODEL), F32),
            pltpu.VMEM((T, D_MODEL), F32),
            pltpu.VMEM((32 * SUBLANES, D_MODEL), F32),
            pltpu.VMEM((SUBLANES * SUBLANES, D_MODEL), F32),
            pltpu.VMEM((SUBLANES, T + HALO, LANES), F32),
        ],
        compiler_params=pltpu.CompilerParams(dimension_semantics=("arbitrary",), vmem_limit_bytes=VMEM_LIMIT),
    )(proj, x2d, tgt2d, meta_tile, conv_a_w, conv_a_b, ln_a_g, ln_a_b, b_a_out, conv_b_w, final_g,
      w_a, w_b, w_o, w_a_t, w_b_t, w_o_t)


def _input_bwd(dproj, ds1, x2d, meta_tile, norm_g, w_in_all, row_tile, ride):
    seq = x2d.shape[0]
    n_steps = seq // row_tile
    meta_block = seq // TILE

    def backward(dp_ref, ds1_ref, s0_ref, g_ref, w_ref, out_ref, vec_ref):
        dh = _dot_nt(dp_ref[:, 0:COLS], w_ref[0])
        for j in range(1, N_DEV):
            dh = dh + _dot_nt(dp_ref[:, j * COLS:(j + 1) * COLS], w_ref[j])
        s0v = s0_ref[...]
        r = lax.rsqrt(_rowmean(s0v * s0v) + EPS)
        gh = dh * g_ref[...]
        out_ref[...] = ds1_ref[...] + r * gh - s0v * ((r * r * r) * _rowmean(gh * s0v))
        vec_ref[0:1, :] += _colsum(dh * (s0v * r))

    def body(dp_ref, ds1_ref, x_ref, dpm_ref, ds1m_ref, meta_ref, g_ref, w_ref, gx_ref, gmeta_ref, vec_ref):
        t = pl.program_id(0)

        @pl.when(t == 0)
        def _():
            vec_ref[...] = jnp.zeros(vec_ref.shape, F32)

        backward(dp_ref, ds1_ref, x_ref, g_ref, w_ref, gx_ref, vec_ref)

        @pl.when(t == n_steps - 1)
        def _():
            backward(dpm_ref, ds1m_ref, meta_ref, g_ref, w_ref, gmeta_ref, vec_ref)

    rides = [("chips", ride)]
    ride_shapes, ride_sems = _ride_shapes(rides)
    body = _riding(body, 8, 3, rides, lambda: pl.program_id(0) == 0, lambda: pl.program_id(0) == n_steps - 1)
    return pl.pallas_call(
        body, name="input_bwd", grid=(n_steps,),
        out_shape=[jax.ShapeDtypeStruct(x2d.shape, F32), jax.ShapeDtypeStruct(meta_tile.shape, F32),
                   jax.ShapeDtypeStruct((SUBLANES, D_MODEL), F32)] + ride_shapes,
        in_specs=[pl.BlockSpec((row_tile, D_IN), lambda t: (t, 0)),
                  pl.BlockSpec((row_tile, D_MODEL), lambda t: (t, 0)),
                  pl.BlockSpec((row_tile, D_MODEL), lambda t: (t, 0)),
                  pl.BlockSpec((TILE, D_IN), lambda t: (meta_block, 0)),
                  pl.BlockSpec((TILE, D_MODEL), lambda t: (meta_block, 0)),
                  _VMEM, _VMEM, _resident((N_DEV, D_MODEL, COLS))] + [_ANY] * len(ride),
        out_specs=[pl.BlockSpec((row_tile, D_MODEL), lambda t: (t, 0)), _VMEM, _VMEM] + [_ANY] * len(ride),
        scratch_shapes=ride_sems,
        compiler_params=pltpu.CompilerParams(dimension_semantics=("arbitrary",), vmem_limit_bytes=VMEM_LIMIT),
    )(dproj, ds1, x2d, dproj, ds1, meta_tile, norm_g, w_in_all, *ride)


def _grad_w_in_half(pos, h_t, dproj, k_tile, other_side, rides, name):
    tp = h_t.shape[1]
    n_k = tp // k_tile

    def column_block(q, k, pos_ref):
        return k, 2 * q + (1 - pos_ref[2] if other_side else pos_ref[2])

    def body(pos_ref, h_ref, dp_ref, o_ref):
        @pl.when(pl.program_id(1) == 0)
        def _():
            o_ref[...] = jnp.zeros(o_ref.shape, F32)

        o_ref[...] += _dot(h_ref[...], dp_ref[...])

    ride = [a for _, arrays in rides for a in arrays]
    n_arr = len(ride)
    ride_shapes, ride_sems = _ride_shapes(rides)
    body = _riding(body, 3, 1, rides,
                   lambda: (pl.program_id(0) == 0) & (pl.program_id(1) == 0),
                   lambda: (pl.program_id(0) == 3) & (pl.program_id(1) == n_k - 1))
    return pl.pallas_call(
        body, name=name,
        out_shape=[jax.ShapeDtypeStruct((4, D_MODEL, COLS), F32)] + ride_shapes,
        grid_spec=pltpu.PrefetchScalarGridSpec(
            num_scalar_prefetch=1, grid=(4, n_k),
            in_specs=[pl.BlockSpec((D_MODEL, k_tile), lambda q, k, pos_ref: (0, k)),
                      pl.BlockSpec((k_tile, COLS), column_block)] + [_ANY] * n_arr,
            out_specs=[pl.BlockSpec((None, D_MODEL, COLS), lambda q, k, pos_ref: (q, 0, 0))] + [_ANY] * n_arr,
            scratch_shapes=ride_sems),
        compiler_params=pltpu.CompilerParams(dimension_semantics=("arbitrary", "arbitrary"),
                                             vmem_limit_bytes=VMEM_LIMIT),
    )(pos, h_t, dproj, *ride)


def _grad_w_out(lhs, rhs, k_tile):
    tp = lhs.shape[1]

    def body(a_ref, b_ref, o_ref):
        @pl.when(pl.program_id(1) == 0)
        def _():
            o_ref[...] = jnp.zeros(o_ref.shape, F32)

        o_ref[...] += _dot_tn(a_ref[...], b_ref[...]).reshape(N_DEV, ROWS_OUT, D_MODEL)

    return pl.pallas_call(
        body, name="grad_w_out", grid=(3, tp // k_tile),
        out_shape=jax.ShapeDtypeStruct((N_DEV, 3, ROWS_OUT, D_MODEL), F32),
        in_specs=[pl.BlockSpec((None, k_tile, D_MODEL), lambda w, k: (w, k, 0)),
                  pl.BlockSpec((None, k_tile, D_MODEL), lambda w, k: (w, k, 0))],
        out_specs=pl.BlockSpec((N_DEV, None, ROWS_OUT, D_MODEL), lambda w, k: (0, w, 0, 0)),
        compiler_params=pltpu.CompilerParams(dimension_semantics=("arbitrary", "arbitrary"),
                                             vmem_limit_bytes=VMEM_LIMIT),
    )(lhs, rhs)


def _adamw_math(w, g, m, v):
    m = ADAM_B1 * m + (1.0 - ADAM_B1) * g
    v = ADAM_B2 * v + (1.0 - ADAM_B2) * (g * g)
    m_hat = m / (1.0 - ADAM_B1 ** ADAM_STEP)
    v_hat = v / (1.0 - ADAM_B2 ** ADAM_STEP)
    delta = -ADAM_LR * (m_hat / (jnp.sqrt(v_hat) + ADAM_EPS) + ADAM_WD * w)
    return delta, m, v


def _adamw_sharded(pos, mine, theirs, landed, w, m, v, row_tile, block0, name):
    rows, n = w.shape
    n_slots = mine.shape[0]

    def mine_map(t, pos_ref):
        chip = 2 * pos_ref[0] + pos_ref[1]
        return (2 * chip + pos_ref[2] if n_slots == N_DEV else chip), block0 + t, 0

    def theirs_map(t, pos_ref):
        return 2 * pos_ref[0] + pos_ref[1], block0 + t, 0

    def body(pos_ref, mine_ref, theirs_ref, land_ref, w_ref, m_ref, v_ref, g_out, d_out, m_out, v_out):
        g = mine_ref[...] + theirs_ref[...]
        for k in range(3):
            g = g + land_ref[k].astype(F32)
        delta, m_new, v_new = _adamw_math(w_ref[...], g, m_ref[...], v_ref[...])
        g_out[...] = g
        d_out[...] = delta
        m_out[...] = m_new
        v_out[...] = v_new

    tile = pl.BlockSpec((row_tile, n), lambda t, pos_ref: (t, 0))
    return pl.pallas_call(
        body, name=name,
        out_shape=[jax.ShapeDtypeStruct((rows, n), F32)] * 4,
        grid_spec=pltpu.PrefetchScalarGridSpec(
            num_scalar_prefetch=1, grid=(rows // row_tile,),
            in_specs=[pl.BlockSpec((None, row_tile, n), mine_map), pl.BlockSpec((None, row_tile, n), theirs_map),
                      pl.BlockSpec((3, row_tile, n), lambda t, pos_ref: (0, block0 + t, 0)), tile, tile, tile],
            out_specs=[tile] * 4),
        compiler_params=pltpu.CompilerParams(dimension_semantics=("arbitrary",)),
    )(pos, mine, theirs, landed, w, m, v)


def _adamw_small(gathered, gathered_cols, params):
    n_par = len(params)

    def body(*refs):
        g_ref, gc_ref = refs[0], refs[1]
        ins = refs[2:2 + 3 * n_par]
        outs = refs[2 + 3 * n_par:]
        loss_ref = outs[4 * n_par]

        def reduced(ref, row, n_rows):
            g = ref[0, row:row + n_rows, :]
            for d in range(1, N_DEV):
                g = g + ref[d, row:row + n_rows, :]
            return g

        for p, (row, n_rows, sharded, _, _, _) in enumerate(params):
            g = reduced(gc_ref if sharded else g_ref, row, n_rows)
            w_ref, m_ref, v_ref = ins[3 * p:3 * p + 3]
            delta, m_new, v_new = _adamw_math(w_ref[...], g, m_ref[...], v_ref[...])
            outs[4 * p][...] = g
            outs[4 * p + 1][...] = delta
            outs[4 * p + 2][...] = m_new
            outs[4 * p + 3][...] = v_new
        loss = jnp.sum(reduced(g_ref, ROW_LOSS, 1), axis=1, keepdims=True)
        loss_ref[...] = jnp.broadcast_to(loss, loss_ref.shape)

    out_shape = []
    for (_, _, _, w, _, _) in params:
        out_shape += [jax.ShapeDtypeStruct(w.shape, F32)] * 4
    out_shape.append(jax.ShapeDtypeStruct((1, LANES), F32))
    flat = [a for (_, _, _, w, m, v) in params for a in (w, m, v)]
    return pl.pallas_call(
        body, name="adamw_small", out_shape=out_shape,
        in_specs=[_VMEM] * (2 + len(flat)), out_specs=[_VMEM] * len(out_shape),
    )(gathered, gathered_cols, *flat)


def _pad_rows(a, rows):
    return jnp.concatenate([a, jnp.zeros((rows - a.shape[0], a.shape[1]), a.dtype)], axis=0)


def kernel(x, meta_tokens, norm_g, w_in, conv_a_w, conv_a_b, ln_a_g, ln_a_b, w_a_out, b_a_out, conv_b_w, w_b_out, w_out, final_g, loss_target, m_meta_tokens, m_norm_g, m_w_in, m_conv_a_w, m_conv_a_b, m_ln_a_g, m_ln_a_b, m_w_a_out, m_b_a_out, m_conv_b_w, m_w_b_out, m_w_out, m_final_g, v_meta_tokens, v_norm_g, v_w_in, v_conv_a_w, v_conv_a_b, v_ln_a_g, v_ln_a_b, v_w_a_out, v_b_a_out, v_conv_b_w, v_w_b_out, v_w_out, v_final_g):
    seq = x.shape[1]
    assert x.shape == (1, seq, D_MODEL) and seq % TILE == 0 and w_in.shape == (1, D_MODEL, COLS)
    n_tiles = seq // TILE + 1
    tp = n_tiles * TILE
    pos = jnp.stack([lax.axis_index("x"), lax.axis_index("y"), lax.axis_index("c")]).astype(jnp.int32)
    me = 4 * pos[0] + 2 * pos[1] + pos[2]
    x2d = x[0]
    tgt2d = loss_target[0]

    small = jnp.concatenate([meta_tokens, _pad_rows(conv_a_w[0], 32), _pad_rows(conv_b_w[0], SUBLANES)], axis=0)
    small_all = _all_gather(small, "gather_small")
    small_all = small_all.transpose(1, 0, 2).reshape(small.shape[0], D_MODEL)
    meta_full, conv_a_full, conv_b_full = small_all[0:N_META], small_all[N_META:N_META + 32], small_all[N_META + 32:]
    meta_tile = jnp.concatenate([jnp.zeros((TILE - N_META, D_MODEL), F32), meta_full], axis=0)
    final_g2 = final_g.reshape(1, D_MODEL)

    w_out_shards = [w[0].astype(BF16) for w in (w_a_out, w_b_out, w_out)]
    h_t, proj, w_in_all, *w_out_all = _gather_norm_proj(pos, x2d, meta_tile, norm_g, w_in[0].astype(BF16),
                                                      w_out_shards, 3)
    w_out_all = [w.reshape(D_MODEL, D_MODEL) for w in w_out_all]
    w_out_all_t = [w.T for w in w_out_all]
    dproj, ds1, lhs, rhs, vec, d_conv_a, d_conv_b = _fused_pass(
        proj, x2d, tgt2d, meta_tile, conv_a_full, conv_a_b, ln_a_g, ln_a_b, b_a_out, conv_b_full, final_g2,
        w_out_all[0], w_out_all[1], w_out_all[2], w_out_all_t[0], w_out_all_t[1], w_out_all_t[2], n_tiles)
    k_tile = tp // 3
    gw_out = _grad_w_out(lhs, rhs, k_tile).reshape(N_DEV, 3 * ROWS_OUT, D_MODEL)
    gw_far, their_out = _grad_w_in_half(pos, h_t, dproj, k_tile, True, [("sibling", (gw_out,))], "grad_w_in_far")
    parts_out = _chip_partial(pos, gw_out, their_out, (1, 2, 3), BF16, ROWS_OUT, "rs_parts_w_out")
    gw_near, their_in, land_out = _grad_w_in_half(
        pos, h_t, dproj, k_tile, False, [("sibling_half", (gw_far,)), ("chips", (parts_out,))], "grad_w_in_near")
    parts_in = _chip_partial(pos, gw_near, their_in, (1, 2, 3), BF16, 256, "rs_parts_w_in")
    grad_x, d_meta_tile, vec_in, land_in = _input_bwd(dproj, ds1, x2d, meta_tile, norm_g, w_in_all, min(256, seq),
                                                      (parts_in,))

    small_g = jnp.concatenate([vec, vec_in, d_conv_a, d_conv_b, d_meta_tile[TILE - N_META:]], axis=0)
    small_g_all = _all_gather(small_g, "gather_small_grads")
    small_g_cols = lax.dynamic_slice_in_dim(small_g_all, me * LANES, LANES, axis=2)

    res_in = _adamw_sharded(pos, gw_near, their_in, land_in, w_in[0], m_w_in[0], v_w_in[0], 128, 0, "adamw_w_in")
    res_out = [
        _adamw_sharded(pos, gw_out, their_out, land_out, w[0], m[0], v[0], ROWS_OUT, k, f"adamw_w_out{k}")
        for k, (w, m, v) in enumerate([(w_a_out, m_w_a_out, v_w_a_out), (w_b_out, m_w_b_out, v_w_b_out),
                                       (w_out, m_w_out, v_w_out)])]
    params = [
        (ROW_META, N_META, True, meta_tokens, m_meta_tokens, v_meta_tokens),
        (ROW_NORM_G, 1, False, norm_g, m_norm_g, v_norm_g),
        (ROW_CONV_A_W, CONV_A, True, conv_a_w[0], m_conv_a_w[0], v_conv_a_w[0]),
        (ROW_CONV_A_B, 1, False, conv_a_b, m_conv_a_b, v_conv_a_b),
        (ROW_LN_G, 1, False, ln_a_g, m_ln_a_g, v_ln_a_g),
        (ROW_LN_B, 1, False, ln_a_b, m_ln_a_b, v_ln_a_b),
        (ROW_B_A_OUT, 1, False, b_a_out, m_b_a_out, v_b_a_out),
        (ROW_CONV_B_W, CONV_B, True, conv_b_w[0], m_conv_b_w[0], v_conv_b_w[0]),
        (ROW_FINAL_G, 1, False, final_g2, m_final_g.reshape(1, D_MODEL), v_final_g.reshape(1, D_MODEL)),
    ]
    res_small = _adamw_small(small_g_all, small_g_cols, params)
    loss = res_small[-1][0, 0]

    def small_res(p, kind, shape):
        return res_small[4 * p + kind].reshape(shape)

    per_weight = []
    for kind in range(4):
        per_weight.append([
            small_res(0, kind, meta_tokens.shape),
            small_res(1, kind, norm_g.shape),
            res_in[kind].reshape(w_in.shape),
            small_res(2, kind, conv_a_w.shape),
            small_res(3, kind, conv_a_b.shape),
            small_res(4, kind, ln_a_g.shape),
            small_res(5, kind, ln_a_b.shape),
            res_out[0][kind].reshape(w_a_out.shape),
            small_res(6, kind, b_a_out.shape),
            small_res(7, kind, conv_b_w.shape),
            res_out[1][kind].reshape(w_b_out.shape),
            res_out[2][kind].reshape(w_out.shape),
            small_res(8, kind, final_g.shape),
        ])
    return (loss, grad_x.reshape(x.shape), *per_weight[0], *per_weight[1], *per_weight[2], *per_weight[3])
```

```python
import functools

import jax
import jax.numpy as jnp
from jax import lax
from jax.experimental import pallas as pl
from jax.experimental.pallas import tpu as pltpu

D_MODEL = 1024
N_META = 16
N_DEV = 8
D_IN = 9 * D_MODEL
COLS = D_IN // N_DEV
ROWS_OUT = D_MODEL // N_DEV
CONV_A = 31
CONV_B = 3
EPS = 1e-6

ADAM_LR = 0.001
ADAM_B1 = 0.9
ADAM_B2 = 0.999
ADAM_EPS = 1e-08
ADAM_WD = 0.01
ADAM_STEP = 10

TILE = 128
LANES = 128
N_CHUNK = D_MODEL // LANES
HALO = 32
SUBLANES = 8
VMEM_LIMIT = 56 * 1024 * 1024

ROW_FINAL_G, ROW_B_A_OUT, ROW_LN_G, ROW_LN_B, ROW_CONV_A_B, ROW_LOSS = 0, 1, 2, 3, 4, 5
ROW_CONV_A_W, ROW_CONV_B_W = 8, 40
ROW_NORM_G, ROW_META = 0, 8

MESH = pl.DeviceIdType.MESH
_ANY = pl.BlockSpec(memory_space=pl.ANY)
_VMEM = pl.BlockSpec(memory_space=pltpu.VMEM)


def _resident(shape):
    return pl.BlockSpec(shape, lambda *_: (0,) * len(shape), pipeline_mode=pl.Buffered(1))
BF16 = jnp.bfloat16
F32 = jnp.float32


def _sigmoid(v):
    return jax.nn.sigmoid(v)


def _dot(a, b):
    return jnp.dot(a, b, preferred_element_type=F32)


def _dot_nt(a, b):
    return lax.dot_general(a, b, (((1,), (1,)), ((), ())), preferred_element_type=F32)


def _dot_tn(a, b):
    return lax.dot_general(a, b, (((0,), (0,)), ((), ())), preferred_element_type=F32)


def _colsum(v):
    return jnp.sum(v, axis=0, keepdims=True)


def _rowmean(v):
    parts = [v[:, LANES * c:LANES * (c + 1)] for c in range(v.shape[1] // LANES)]
    return jnp.sum(functools.reduce(jnp.add, parts), axis=-1, keepdims=True) * (1.0 / v.shape[1])


def _fold8(v):
    parts = [v[SUBLANES * g:SUBLANES * (g + 1)] for g in range(v.shape[0] // SUBLANES)]
    return functools.reduce(jnp.add, parts)


def _all_gather(shard, name, ride=()):
    m, n = shard.shape
    n_arr = len(ride)
    ride_shapes, ride_sems = _exchange_shapes("chips", ride) if ride else ([], [])

    def body(*refs):
        x_ref, out_ref = refs[0], refs[1 + n_arr]
        send_sems, recv_sems, local_sem = refs[2 + 2 * n_arr:5 + 2 * n_arr]
        riding = _chip_copies(refs[1:1 + n_arr], refs[2 + n_arr:2 + 2 * n_arr], *refs[5 + 2 * n_arr:]) if ride else []
        for cp in riding:
            cp.start()
        x, y, c = lax.axis_index("x"), lax.axis_index("y"), lax.axis_index("c")
        me, sibling = (x, y, c), (x, y, 1 - c)
        chips = [(1 - x, y), (x, 1 - y), (1 - x, 1 - y)]

        def slot(px, py, pc):
            return out_ref.at[4 * px + 2 * py + pc]

        def copy(k, block, to, src=None):
            return pltpu.make_async_remote_copy(
                src_ref=slot(*block) if src is None else src, dst_ref=slot(*block),
                send_sem=send_sems.at[k], recv_sem=recv_sems.at[k], device_id=to, device_id_type=MESH)

        mine = pltpu.make_async_copy(x_ref, slot(*me), local_sem)
        mine.start()
        first = [copy(0, me, sibling, src=x_ref)]
        first += [copy(1 + j, me, (*chip, c), src=x_ref) for j, chip in enumerate(chips)]
        for cp in first:
            cp.start()
        passed = [copy(4 + j, (*chip, c), sibling) for j, chip in enumerate(chips)]
        for j, chip in enumerate(chips):
            copy(1 + j, (*chip, c), me).wait_recv()
            passed[j].start()
        copy(0, sibling, me).wait_recv()
        for j, chip in enumerate(chips):
            copy(4 + j, (*chip, 1 - c), me).wait_recv()
        for cp in first + passed:
            cp.wait_send()
        mine.wait()
        for cp in riding:
            cp.wait()

    res = pl.pallas_call(
        body, name=name,
        out_shape=[jax.ShapeDtypeStruct((N_DEV, m, n), shard.dtype)] + ride_shapes,
        in_specs=[_ANY] * (1 + n_arr), out_specs=[_ANY] * (1 + n_arr),
        scratch_shapes=[pltpu.SemaphoreType.DMA((7,)), pltpu.SemaphoreType.DMA((7,)), pltpu.SemaphoreType.DMA(())]
        + ride_sems,
    )(shard, *ride)
    return res if ride else res[0]


def _sibling_copies(srcs, dsts, send_sems, recv_sems):
    x, y, c = lax.axis_index("x"), lax.axis_index("y"), lax.axis_index("c")
    return [pltpu.make_async_remote_copy(
        src_ref=src.at[2 * q + (1 - c)], dst_ref=dst.at[q],
        send_sem=send_sems.at[4 * a + q], recv_sem=recv_sems.at[4 * a + q],
        device_id=(x, y, 1 - c), device_id_type=MESH)
        for a, (src, dst) in enumerate(zip(srcs, dsts)) for q in range(4)]


def _chip_copies(srcs, dsts, send_sems, recv_sems):
    x, y, c = lax.axis_index("x"), lax.axis_index("y"), lax.axis_index("c")
    targets = [(x, 1 - y, c), (1 - x, y, c), (1 - x, 1 - y, c)]
    return [pltpu.make_async_remote_copy(
        src_ref=src.at[k], dst_ref=dst.at[k],
        send_sem=send_sems.at[3 * a + k], recv_sem=recv_sems.at[3 * a + k],
        device_id=targets[k], device_id_type=MESH)
        for a, (src, dst) in enumerate(zip(srcs, dsts)) for k in range(3)]


def _sibling_half_copies(srcs, dsts, send_sems, recv_sems):
    x, y, c = lax.axis_index("x"), lax.axis_index("y"), lax.axis_index("c")
    return [pltpu.make_async_remote_copy(
        src_ref=src.at[q], dst_ref=dst.at[q],
        send_sem=send_sems.at[4 * a + q], recv_sem=recv_sems.at[4 * a + q],
        device_id=(x, y, 1 - c), device_id_type=MESH)
        for a, (src, dst) in enumerate(zip(srcs, dsts)) for q in range(4)]


def _all_copies(srcs, dsts, send_sems, recv_sems):
    x, y, c = lax.axis_index("x"), lax.axis_index("y"), lax.axis_index("c")
    mine = 4 * x + 2 * y + c
    copies = []
    for a, (src, dst) in enumerate(zip(srcs, dsts)):
        copies.append(pltpu.make_async_copy(src.at[0], dst.at[mine], send_sems.at[N_DEV * a]))
        for k in range(1, N_DEV):
            copies.append(pltpu.make_async_remote_copy(
                src_ref=src.at[0], dst_ref=dst.at[mine],
                send_sem=send_sems.at[N_DEV * a + k], recv_sem=recv_sems.at[N_DEV * a + k],
                device_id=(x ^ (k >> 2), y ^ ((k >> 1) & 1), c ^ (k & 1)), device_id_type=MESH))
    return copies


_EXCHANGES = {"sibling": (4, _sibling_copies, 4), "sibling_half": (4, _sibling_half_copies, 4),
              "chips": (3, _chip_copies, 3), "all": (N_DEV, _all_copies, N_DEV)}


def _exchange_shapes(kind, arrays):
    per_array, _, slots = _EXCHANGES[kind]
    out_shape = [jax.ShapeDtypeStruct((slots,) + a.shape[1:], a.dtype) for a in arrays]
    sems = [pltpu.SemaphoreType.DMA((per_array * len(arrays),))] * 2
    return out_shape, sems


def _ride_shapes(rides):
    shapes, sems = [], []
    for kind, arrays in rides:
        ride_shapes, ride_sems = _exchange_shapes(kind, arrays)
        shapes += ride_shapes
        sems += ride_sems
    return shapes, sems


def _riding(body, n_in, n_out, rides, is_first, is_last):
    counts = [len(arrays) for _, arrays in rides]
    n_arr = sum(counts)

    def wrapped(*refs):
        ins, srcs = refs[:n_in], refs[n_in:n_in + n_arr]
        outs = refs[n_in + n_arr:n_in + n_arr + n_out]
        dsts = refs[n_in + n_arr + n_out:n_in + 2 * n_arr + n_out]
        first_sem = len(refs) - 2 * len(rides)
        scratch, sems = refs[n_in + 2 * n_arr + n_out:first_sem], refs[first_sem:]

        def copies():
            made, at = [], 0
            for r, ((kind, _), n) in enumerate(zip(rides, counts)):
                made += _EXCHANGES[kind][1](srcs[at:at + n], dsts[at:at + n], sems[2 * r], sems[2 * r + 1])
                at += n
            return made

        @pl.when(is_first())
        def _():
            for cp in copies():
                cp.start()

        body(*ins, *outs, *scratch)

        @pl.when(is_last())
        def _():
            for cp in copies():
                cp.wait()

    return wrapped


def _chip_partial(pos, mine, theirs, relations, out_dtype, row_tile, name):
    n_slots, m, n = mine.shape
    q0 = relations[0]

    def chip_of(qi, pos_ref):
        q = qi + q0
        return pos_ref[0] ^ (q >> 1), pos_ref[1] ^ (q & 1)

    def mine_map(qi, t, pos_ref):
        px, py = chip_of(qi, pos_ref)
        return (4 * px + 2 * py + pos_ref[2] if n_slots == N_DEV else 2 * px + py), t, 0

    def theirs_map(qi, t, pos_ref):
        px, py = chip_of(qi, pos_ref)
        return 2 * px + py, t, 0

    def body(pos_ref, a_ref, b_ref, o_ref):
        o_ref[...] = (a_ref[...] + b_ref[...]).astype(out_dtype)

    return pl.pallas_call(
        body, name=name,
        out_shape=jax.ShapeDtypeStruct((len(relations), m, n), out_dtype),
        grid_spec=pltpu.PrefetchScalarGridSpec(
            num_scalar_prefetch=1, grid=(len(relations), m // row_tile),
            in_specs=[pl.BlockSpec((None, row_tile, n), mine_map), pl.BlockSpec((None, row_tile, n), theirs_map)],
            out_specs=pl.BlockSpec((None, row_tile, n), lambda qi, t, pos_ref: (qi, t, 0))),
        compiler_params=pltpu.CompilerParams(dimension_semantics=("arbitrary", "arbitrary")),
    )(pos, mine, theirs)


PARTS = ((0, 512), (512, 640))


def _gather_norm_proj(pos, x2d, meta_tile, norm_g, w_in_shard, w_out_shards, n_chunk):
    seq = x2d.shape[0]
    n_tiles = seq // TILE + 1
    tp = n_tiles * TILE
    n_parts = len(PARTS)
    widest = max(width for _, width in PARTS)
    units = [(s, u) for s in range(2) for u in range(n_parts)]
    for first in (2, 5):
        units += [(first + j, u) for u in range(n_parts) for j in range(2)] + [(first + 2, u) for u in range(n_parts)]
    n_units = len(units)
    n_steps = n_tiles + n_units
    chunk = tp // n_chunk

    def body(pos_ref, x_ref, meta_ref, g_ref, win_ref, wa_ref, wb_ref, wo_ref,
             ht_ref, proj_ref, win_all, wa_all, wb_all, wo_all,
             h_all, wbuf, rbuf, send_sems, recv_sems, local_sems):
        g = pl.program_id(0)
        x, y, c = lax.axis_index("x"), lax.axis_index("y"), lax.axis_index("c")
        me, sibling = (x, y, c), (x, y, 1 - c)
        chips = [(1 - x, y), (x, 1 - y), (1 - x, 1 - y)]
        shards = (win_ref, wa_ref, wb_ref, wo_ref)
        gathered = (win_all, wa_all, wb_all, wo_all)
        blocks = [me, sibling] + [(*chip, c) for chip in chips] + [(*chip, 1 - c) for chip in chips]

        def index(block):
            px, py, pc = block
            return 4 * px + 2 * py + pc

        def part(ref, a, u):
            return ref.at[:, pl.ds(PARTS[u][0], PARTS[u][1])] if a == 0 else ref

        def slot(a, block, u):
            return part(gathered[a].at[index(block)], a, u)

        def sem(a, k, u):
            return n_parts * k + u if a == 0 else 7 * n_parts + 7 * (a - 1) + k

        def copy(a, k, block, to, u=0, from_shard=False):
            return pltpu.make_async_remote_copy(
                src_ref=part(shards[a], a, u) if from_shard else slot(a, block, u), dst_ref=slot(a, block, u),
                send_sem=send_sems.at[sem(a, k, u)], recv_sem=recv_sems.at[sem(a, k, u)],
                device_id=to, device_id_type=MESH)

        def keep(a):
            return pltpu.make_async_copy(shards[a], gathered[a].at[index(me)], local_sems.at[a])

        def load(m):
            s, u = units[m]
            src = part(win_ref, 0, u) if s == 0 else slot(0, blocks[s], u)
            return pltpu.make_async_copy(src, wbuf.at[m % 2, :, 0:PARTS[u][1]], local_sems.at[4 + m % 2])

        def store(m):
            s, u = units[m]
            col0 = pl.multiple_of(index(blocks[s]) * COLS + PARTS[u][0], LANES)
            return pltpu.make_async_copy(rbuf.at[m % 2, :, 0:PARTS[u][1]],
                                         proj_ref.at[:, pl.ds(col0, PARTS[u][1])], local_sems.at[6 + m % 2])

        def by_x(a, u):
            return u == 0 if a == 0 else a < 3

        def relay(a, u=0):
            src, to = (blocks[3], blocks[2]) if by_x(a, u) else (blocks[2], blocks[3])
            return copy(a, 3, src, to, u)

        def arrive(m):
            s, u = units[m]
            if s == 1:
                copy(0, 0, sibling, me, u).wait_recv()
            elif 2 <= s <= 4:
                copy(0, s - 1, blocks[s], me, u).wait_recv()
                copy(0, s + 2, blocks[s], sibling, u).start()
                if s < 4 and by_x(0, u) == (s == 3):
                    relay(0, u).start()
            elif s >= 5:
                copy(0, s - 1, blocks[s], me, u).wait_recv()
                if u == 0:
                    for a in range(1, 4):
                        copy(a, s - 4, blocks[s - 3], me).wait_recv()
                        copy(a, s - 1, blocks[s - 3], sibling).start()
                        if s < 7 and by_x(a, 0) == (s == 6):
                            relay(a).start()

        targets = [sibling, blocks[2], blocks[3]]

        @pl.when(g == 0)
        def _():
            for a in range(4):
                keep(a).start()
            for u in range(n_parts):
                for k, to in enumerate(targets):
                    copy(0, k, me, to, u, from_shard=True).start()
            for a in range(1, 4):
                for k, to in enumerate(targets):
                    copy(a, k, me, to, from_shard=True).start()
            load(0).start()

        @pl.when(g < n_tiles)
        def _():
            s0 = jnp.where(g == n_tiles - 1, meta_ref[...], x_ref[...])
            r = lax.rsqrt(_rowmean(s0 * s0) + EPS)
            h32 = (s0 * r) * g_ref[...]
            ht_ref[...] = h32.T.astype(BF16)
            h_all[pl.ds(pl.multiple_of(g * TILE, TILE), TILE), :] = h32.astype(BF16)

        for m in range(n_units):
            @pl.when(g == n_tiles + m)
            def _(m=m):
                load(m).wait()
                if m + 1 < n_units:
                    arrive(m + 1)
                    load(m + 1).start()
                if m >= 2:
                    store(m - 2).wait()

        m_now = jnp.maximum(g - n_tiles, 0)
        u_now = functools.reduce(jnp.add, [jnp.where(m_now == m, u, 0) for m, (_, u) in enumerate(units)])
        for u, (_, width) in enumerate(PARTS):
            @pl.when((g >= n_tiles) & (u_now == u))
            def _(width=width):
                w = wbuf[m_now % 2, :, 0:width]
                for r in range(n_chunk):
                    rbuf[m_now % 2, r * chunk:(r + 1) * chunk, 0:width] = _dot(h_all[r * chunk:(r + 1) * chunk, :], w)

        for m in range(n_units):
            @pl.when(g == n_tiles + m)
            def _(m=m):
                store(m).start()

        @pl.when(g == n_steps - 1)
        def _():
            store(n_units - 2).wait()
            store(n_units - 1).wait()
            for a in range(1, 4):
                copy(a, 0, sibling, me).wait_recv()
                for j in range(3):
                    copy(a, 4 + j, blocks[5 + j], me).wait_recv()
            for a in range(4):
                for u in range(n_parts if a == 0 else 1):
                    for k, to in enumerate(targets):
                        copy(a, k, me, to, u, from_shard=True).wait_send()
                    relay(a, u).wait_send()
                    for j in range(3):
                        copy(a, 4 + j, blocks[2 + j], sibling, u).wait_send()
                keep(a).wait()

    n_x = n_tiles - 1
    return pl.pallas_call(
        body, name="gather_norm_proj",
        out_shape=[jax.ShapeDtypeStruct((D_MODEL, tp), BF16), jax.ShapeDtypeStruct((tp, D_IN), F32),
                   jax.ShapeDtypeStruct((N_DEV,) + w_in_shard.shape, BF16)]
                  + [jax.ShapeDtypeStruct((N_DEV,) + w.shape, BF16) for w in w_out_shards],
        grid_spec=pltpu.PrefetchScalarGridSpec(
            num_scalar_prefetch=1, grid=(n_steps,),
            in_specs=[pl.BlockSpec((TILE, D_MODEL), lambda g, pos_ref: (jnp.minimum(g, n_x - 1), 0)),
                      _VMEM, _VMEM, _ANY, _ANY, _ANY, _ANY],
            out_specs=[pl.BlockSpec((D_MODEL, TILE), lambda g, pos_ref: (0, jnp.minimum(g, n_tiles - 1))),
                       _ANY, _ANY, _ANY, _ANY, _ANY],
            scratch_shapes=[pltpu.VMEM((tp, D_MODEL), BF16), pltpu.VMEM((2, D_MODEL, widest), BF16),
                            pltpu.VMEM((2, tp, widest), F32),
                            pltpu.SemaphoreType.DMA((7 * n_parts + 21,)), pltpu.SemaphoreType.DMA((7 * n_parts + 21,)),
                            pltpu.SemaphoreType.DMA((8,))]),
        compiler_params=pltpu.CompilerParams(dimension_semantics=("arbitrary",), vmem_limit_bytes=VMEM_LIMIT),
    )(pos, x2d, meta_tile, norm_g, w_in_shard, *w_out_shards)


C_AVAL, C_AGLU, C_AZ, C_BB, C_BC, C_BX, C_BZ, C_GA, C_GB = (k * D_MODEL for k in range(9))
S_AZ, S_BB, S_BZ, S_GA, S_GB = (k * D_MODEL for k in range(5))


def _fused_pass(proj, x2d, tgt2d, meta_tile, conv_a_w, conv_a_b, ln_a_g, ln_a_b, b_a_out, conv_b_w, final_g,
                w_a, w_b, w_o, w_a_t, w_b_t, w_o_t, n_tiles):
    T = TILE
    tp = n_tiles * T
    inv_d = 1.0 / D_MODEL

    def block_of(tile):
        return jnp.where(tile == 0, n_tiles - 1, tile - 1)

    def cur(i):
        return block_of(jnp.minimum(i, n_tiles - 1))

    def prev(i):
        return block_of(jnp.clip(i - 1, 0, n_tiles - 1))

    def xblk(i):
        return jnp.maximum(jnp.minimum(i, n_tiles - 1) - 1, 0)

    def body(proj_ref, x_ref, tgt_ref, meta_ref, caw_ref, cab_ref, lng_ref, lnb_ref, bao_ref, cbw_ref, fg_ref,
             wa_ref, wb_ref, wo_ref, wat_ref, wbt_ref, wot_ref,
             dproj_ref, ds1_ref, lhs_ref, rhs_ref, vec_ref, dcaw_ref, dcbw_ref,
             ua0_buf, cb_buf, dua1_buf, dc3_buf, aprev, cprev, stage, ua1_buf, c3_buf,
             dpa_buf, dpb_buf, dcaw8, dcbw8, shift_buf):
        i = pl.program_id(0)

        @pl.when(i == 0)
        def _init():
            for buf in (ua0_buf, cb_buf, dua1_buf, dc3_buf, aprev, cprev, dcaw8, dcbw8):
                buf[...] = jnp.zeros(buf.shape, buf.dtype)
            vec_ref[...] = jnp.zeros(vec_ref.shape, F32)

        @pl.when(i >= 1)
        def _emit_stage():
            dproj_ref[:, C_AZ:C_BC] = stage[:, S_AZ:S_BZ]
            dproj_ref[:, C_BZ:D_IN] = stage[:, S_BZ:S_GB + D_MODEL]

        @pl.when(i < n_tiles)
        def _front():
            def conv_chunk(cc, carry):
                c0 = pl.multiple_of(cc * LANES, LANES)
                lanes = pl.ds(c0, LANES)

                def col(base):
                    return pl.ds(pl.multiple_of(base + cc * LANES, LANES), LANES)

                ua0 = proj_ref[:, col(C_AVAL)] * _sigmoid(proj_ref[:, col(C_AGLU)])
                ua0_buf[T:2 * T, lanes] = ua0
                acc = jnp.broadcast_to(cab_ref[:, lanes], (T, LANES))
                lead = HALO - (CONV_A - 1)
                for r in range(SUBLANES):
                    taps = [k for k in range(CONV_A) if (k + lead) % SUBLANES == r]
                    rows = T + SUBLANES * max((k + lead) // SUBLANES for k in taps)
                    if r:
                        shift_buf[r, 0:rows, :] = ua0_buf[pl.ds(T - HALO + r, rows), lanes]
                    for k in taps:
                        q = (k + lead) // SUBLANES
                        if r:
                            win = shift_buf[r, SUBLANES * q:SUBLANES * q + T, :]
                        else:
                            win = ua0_buf[pl.ds(T - HALO + SUBLANES * q, T), lanes]
                        acc = acc + caw_ref[k:k + 1, lanes] * win
                ua1_buf[:, lanes] = acc
                cb = proj_ref[:, col(C_BC)] * proj_ref[:, col(C_BX)]
                cb_buf[T:2 * T, lanes] = cb
                acc3 = cbw_ref[0:1, lanes] * cb_buf[pl.ds(T - 2, T), lanes]
                for k in range(1, CONV_B):
                    acc3 = acc3 + cbw_ref[k:k + 1, lanes] * cb_buf[pl.ds(T - (CONV_B - 1) + k, T), lanes]
                c3_buf[:, lanes] = acc3
                return carry

            lax.fori_loop(0, N_CHUNK, conv_chunk, 0)

            ua1 = ua1_buf[...]
            xc = ua1 - _rowmean(ua1)
            rstd = lax.rsqrt(_rowmean(xc * xc) + EPS)
            xhat = xc * rstd
            ua2 = xhat * lng_ref[...] + lnb_ref[...]
            sg2 = _sigmoid(ua2)
            ua3 = ua2 * sg2
            a_z = proj_ref[:, C_AZ:C_AZ + D_MODEL]
            sz = _sigmoid(a_z)
            silu_az = a_z * sz
            lhs_ref[0] = (ua3 * silu_az).astype(BF16)
            b_z = proj_ref[:, C_BZ:C_BZ + D_MODEL]
            sbz = _sigmoid(b_z)
            silu_bz = b_z * sbz
            b_b = proj_ref[:, C_BB:C_BB + D_MODEL]
            c3 = c3_buf[...]
            ub = b_b * c3
            lhs_ref[1] = (ub * silu_bz).astype(BF16)

            ya = _dot(lhs_ref[0], wa_ref[...]) + bao_ref[...]
            yb = _dot(lhs_ref[1], wb_ref[...])
            sga = _sigmoid(proj_ref[:, C_GA:C_GA + D_MODEL])
            sgb = _sigmoid(proj_ref[:, C_GB:C_GB + D_MODEL])
            m_b = (sga * ya + sgb * yb).astype(BF16)
            lhs_ref[2] = m_b
            s0 = jnp.where(i == 0, meta_ref[...], x_ref[...])
            s1 = s0 + _dot(m_b, wo_ref[...])
            r1 = lax.rsqrt(_rowmean(s1 * s1) + EPS)
            y = (s1 * r1) * fg_ref[...]
            is_token = (i >= 1).astype(F32)
            err = (y - tgt_ref[...]) * is_token
            vec_ref[ROW_LOSS:ROW_LOSS + 1, :] += (0.5 * inv_d) * _colsum(err * err)
            dy = err * inv_d
            vec_ref[ROW_FINAL_G:ROW_FINAL_G + 1, :] += _colsum(dy * (s1 * r1))
            gy = dy * fg_ref[...]
            ds1 = r1 * gy - s1 * ((r1 * r1 * r1) * _rowmean(gy * s1))
            ds1_ref[...] = ds1
            ds1_b = ds1.astype(BF16)
            rhs_ref[2] = ds1_b
            dm = _dot(ds1_b, wot_ref[...])
            dya = dm * sga
            dyb = dm * sgb
            stage[:, S_GA:S_GA + D_MODEL] = (dya * ya * (1.0 - sga)).astype(BF16)
            stage[:, S_GB:S_GB + D_MODEL] = (dyb * yb * (1.0 - sgb)).astype(BF16)
            vec_ref[ROW_B_A_OUT:ROW_B_A_OUT + 1, :] += _colsum(dya)
            dya_b = dya.astype(BF16)
            dyb_b = dyb.astype(BF16)
            rhs_ref[0] = dya_b
            rhs_ref[1] = dyb_b
            dpa_buf[...] = _dot(dya_b, wat_ref[...])
            dpb_buf[...] = _dot(dyb_b, wbt_ref[...])

            dpa = dpa_buf[...]
            stage[:, S_AZ:S_AZ + D_MODEL] = (dpa * ua3 * (sz + silu_az * (1.0 - sz))).astype(BF16)
            dua2 = dpa * silu_az * (sg2 + ua3 * (1.0 - sg2))
            vec_ref[ROW_LN_G:ROW_LN_G + 1, :] += _colsum(dua2 * xhat)
            vec_ref[ROW_LN_B:ROW_LN_B + 1, :] += _colsum(dua2)
            dxh = dua2 * lng_ref[...]
            dua1 = rstd * (dxh - _rowmean(dxh) - xhat * _rowmean(dxh * xhat))
            vec_ref[ROW_CONV_A_B:ROW_CONV_A_B + 1, :] += _colsum(dua1)
            dua1_buf[T:2 * T, :] = dua1
            dpb = dpb_buf[...]
            stage[:, S_BZ:S_BZ + D_MODEL] = (dpb * ub * (sbz + silu_bz * (1.0 - sbz))).astype(BF16)
            dub = dpb * silu_bz
            stage[:, S_BB:S_BB + D_MODEL] = (dub * c3).astype(BF16)
            dc3_buf[T:2 * T, :] = dub * b_b

        @pl.when(i == n_tiles)
        def _no_later_tile():
            dua1_buf[T:2 * T, :] = jnp.zeros((T, D_MODEL), F32)
            dc3_buf[T:2 * T, :] = jnp.zeros((T, D_MODEL), F32)

        @pl.when(i >= 1)
        def _lagged():
            def convt_chunk(cc, carry):
                c0 = pl.multiple_of(cc * LANES, LANES)
                lanes = pl.ds(c0, LANES)

                def col(base):
                    return pl.ds(pl.multiple_of(base + cc * LANES, LANES), LANES)

                ua0 = ua0_buf[0:T, lanes]
                acc = jnp.zeros((T, LANES), F32)
                for r in range(SUBLANES):
                    shifts = [j for j in range(CONV_A) if j % SUBLANES == r]
                    rows = T + shifts[-1] - r
                    if r:
                        shift_buf[r, 0:rows, :] = dua1_buf[pl.ds(r, rows), lanes]
                    for j in shifts:
                        k = CONV_A - 1 - j
                        if r:
                            later = shift_buf[r, j - r:j - r + T, :]
                        else:
                            later = dua1_buf[pl.ds(j, T), lanes]
                        acc = acc + caw_ref[k:k + 1, lanes] * later
                        dcaw8[SUBLANES * k:SUBLANES * (k + 1), lanes] += _fold8(ua0 * later)
                a_val = aprev[:, col(0)]
                sg = _sigmoid(aprev[:, col(D_MODEL)])
                dproj_ref[:, col(C_AVAL)] = (acc * sg).astype(BF16)
                dproj_ref[:, col(C_AGLU)] = (acc * a_val * (sg * (1.0 - sg))).astype(BF16)

                cb = cb_buf[0:T, lanes]
                acc3 = jnp.zeros((T, LANES), F32)
                for j in range(CONV_B):
                    k = CONV_B - 1 - j
                    later = dc3_buf[pl.ds(j, T), lanes]
                    acc3 = acc3 + cbw_ref[k:k + 1, lanes] * later
                    dcbw8[SUBLANES * k:SUBLANES * (k + 1), lanes] += _fold8(cb * later)
                dproj_ref[:, col(C_BC)] = (acc3 * cprev[:, col(D_MODEL)]).astype(BF16)
                dproj_ref[:, col(C_BX)] = (acc3 * cprev[:, col(0)]).astype(BF16)
                return carry

            lax.fori_loop(0, N_CHUNK, convt_chunk, 0)

        for buf in (ua0_buf, cb_buf, dua1_buf, dc3_buf):
            buf[0:T, :] = buf[T:2 * T, :]
        aprev[...] = proj_ref[:, C_AVAL:C_AZ]
        cprev[...] = proj_ref[:, C_BC:C_BZ]

        @pl.when(i == n_tiles)
        def _finish():
            for k in range(CONV_A):
                dcaw_ref[k:k + 1, :] = _colsum(dcaw8[SUBLANES * k:SUBLANES * (k + 1), :])
            dcaw_ref[CONV_A:CONV_A + 1, :] = jnp.zeros((1, D_MODEL), F32)
            for k in range(CONV_B):
                dcbw_ref[k:k + 1, :] = _colsum(dcbw8[SUBLANES * k:SUBLANES * (k + 1), :])
            dcbw_ref[CONV_B:SUBLANES, :] = jnp.zeros((SUBLANES - CONV_B, D_MODEL), F32)

    tile_in = lambda width: pl.BlockSpec((T, width), lambda i: (cur(i), 0))
    return pl.pallas_call(
        body, name="fused_pass", grid=(n_tiles + 1,),
        out_shape=[
            jax.ShapeDtypeStruct((tp, D_IN), BF16),
            jax.ShapeDtypeStruct((tp, D_MODEL), F32),
            jax.ShapeDtypeStruct((3, tp, D_MODEL), BF16),
            jax.ShapeDtypeStruct((3, tp, D_MODEL), BF16),
            jax.ShapeDtypeStruct((SUBLANES, D_MODEL), F32),
            jax.ShapeDtypeStruct((32, D_MODEL), F32),
            jax.ShapeDtypeStruct((SUBLANES, D_MODEL), F32),
        ],
        in_specs=[
            tile_in(D_IN),
            pl.BlockSpec((T, D_MODEL), lambda i: (xblk(i), 0)),
            pl.BlockSpec((T, D_MODEL), lambda i: (xblk(i), 0)),
            _VMEM, _VMEM, _VMEM, _VMEM, _VMEM, _VMEM, _VMEM, _VMEM,
            *[_resident((D_MODEL, D_MODEL)) for _ in range(6)],
        ],
        out_specs=[
            pl.BlockSpec((T, D_IN), lambda i: (prev(i), 0)),
            pl.BlockSpec((T, D_MODEL), lambda i: (cur(i), 0)),
            pl.BlockSpec((3, T, D_MODEL), lambda i: (0, cur(i), 0)),
            pl.BlockSpec((3, T, D_MODEL), lambda i: (0, cur(i), 0)),
            _VMEM, _VMEM, _VMEM,
        ],
        scratch_shapes=[
            pltpu.VMEM((2 * T, D_MODEL), F32),
            pltpu.VMEM((2 * T, D_MODEL), F32),
            pltpu.VMEM((2 * T, D_MODEL), F32),
            pltpu.VMEM((2 * T, D_MODEL), F32),
            pltpu.VMEM((T, 2 * D_MODEL), F32),
            pltpu.VMEM((T, 2 * D_MODEL), F32),
            pltpu.VMEM((T, 5 * D_MODEL), BF16),
            pltpu.VMEM((T, D_MODEL), F32),
            pltpu.VMEM((T, D_MODEL), F32),
            pltpu.VMEM((T, D_MODEL), F32),
            pltpu.VMEM((T, D_MODEL), F32),
            pltpu.VMEM((32 * SUBLANES, D_MODEL), F32),
            pltpu.VMEM((SUBLANES * SUBLANES, D_MODEL), F32),
            pltpu.VMEM((SUBLANES, T + HALO, LANES), F32),
        ],
        compiler_params=pltpu.CompilerParams(dimension_semantics=("arbitrary",), vmem_limit_bytes=VMEM_LIMIT),
    )(proj, x2d, tgt2d, meta_tile, conv_a_w, conv_a_b, ln_a_g, ln_a_b, b_a_out, conv_b_w, final_g,
      w_a, w_b, w_o, w_a_t, w_b_t, w_o_t)


def _input_bwd(dproj, ds1, x2d, meta_tile, norm_g, w_in_all, row_tile, ride):
    seq = x2d.shape[0]
    n_steps = seq // row_tile
    meta_block = seq // TILE

    def backward(dp_ref, ds1_ref, s0_ref, g_ref, w_ref, out_ref, vec_ref):
        dh = _dot_nt(dp_ref[:, 0:COLS], w_ref[0])
        for j in range(1, N_DEV):
            dh = dh + _dot_nt(dp_ref[:, j * COLS:(j + 1) * COLS], w_ref[j])
        s0v = s0_ref[...]
        r = lax.rsqrt(_rowmean(s0v * s0v) + EPS)
        gh = dh * g_ref[...]
        out_ref[...] = ds1_ref[...] + r * gh - s0v * ((r * r * r) * _rowmean(gh * s0v))
        vec_ref[0:1, :] += _colsum(dh * (s0v * r))

    def body(dp_ref, ds1_ref, x_ref, dpm_ref, ds1m_ref, meta_ref, g_ref, w_ref, gx_ref, gmeta_ref, vec_ref):
        t = pl.program_id(0)

        @pl.when(t == 0)
        def _():
            vec_ref[...] = jnp.zeros(vec_ref.shape, F32)

        backward(dp_ref, ds1_ref, x_ref, g_ref, w_ref, gx_ref, vec_ref)

        @pl.when(t == n_steps - 1)
        def _():
            backward(dpm_ref, ds1m_ref, meta_ref, g_ref, w_ref, gmeta_ref, vec_ref)

    rides = [("chips", ride)]
    ride_shapes, ride_sems = _ride_shapes(rides)
    body = _riding(body, 8, 3, rides, lambda: pl.program_id(0) == 0, lambda: pl.program_id(0) == n_steps - 1)
    return pl.pallas_call(
        body, name="input_bwd", grid=(n_steps,),
        out_shape=[jax.ShapeDtypeStruct(x2d.shape, F32), jax.ShapeDtypeStruct(meta_tile.shape, F32),
                   jax.ShapeDtypeStruct((SUBLANES, D_MODEL), F32)] + ride_shapes,
        in_specs=[pl.BlockSpec((row_tile, D_IN), lambda t: (t, 0)),
                  pl.BlockSpec((row_tile, D_MODEL), lambda t: (t, 0)),
                  pl.BlockSpec((row_tile, D_MODEL), lambda t: (t, 0)),
                  pl.BlockSpec((TILE, D_IN), lambda t: (meta_block, 0)),
                  pl.BlockSpec((TILE, D_MODEL), lambda t: (meta_block, 0)),
                  _VMEM, _VMEM, _resident((N_DEV, D_MODEL, COLS))] + [_ANY] * len(ride),
        out_specs=[pl.BlockSpec((row_tile, D_MODEL), lambda t: (t, 0)), _VMEM, _VMEM] + [_ANY] * len(ride),
        scratch_shapes=ride_sems,
        compiler_params=pltpu.CompilerParams(dimension_semantics=("arbitrary",), vmem_limit_bytes=VMEM_LIMIT),
    )(dproj, ds1, x2d, dproj, ds1, meta_tile, norm_g, w_in_all, *ride)


def _grad_w_in_half(pos, h_t, dproj, k_tile, other_side, rides, name):
    tp = h_t.shape[1]
    n_k = tp // k_tile

    def column_block(q, k, pos_ref):
        return k, 2 * q + (1 - pos_ref[2] if other_side else pos_ref[2])

    def body(pos_ref, h_ref, dp_ref, o_ref):
        @pl.when(pl.program_id(1) == 0)
        def _():
            o_ref[...] = jnp.zeros(o_ref.shape, F32)

        o_ref[...] += _dot(h_ref[...], dp_ref[...])

    ride = [a for _, arrays in rides for a in arrays]
    n_arr = len(ride)
    ride_shapes, ride_sems = _ride_shapes(rides)
    body = _riding(body, 3, 1, rides,
                   lambda: (pl.program_id(0) == 0) & (pl.program_id(1) == 0),
                   lambda: (pl.program_id(0) == 3) & (pl.program_id(1) == n_k - 1))
    return pl.pallas_call(
        body, name=name,
        out_shape=[jax.ShapeDtypeStruct((4, D_MODEL, COLS), F32)] + ride_shapes,
        grid_spec=pltpu.PrefetchScalarGridSpec(
            num_scalar_prefetch=1, grid=(4, n_k),
            in_specs=[pl.BlockSpec((D_MODEL, k_tile), lambda q, k, pos_ref: (0, k)),
                      pl.BlockSpec((k_tile, COLS), column_block)] + [_ANY] * n_arr,
            out_specs=[pl.BlockSpec((None, D_MODEL, COLS), lambda q, k, pos_ref: (q, 0, 0))] + [_ANY] * n_arr,
            scratch_shapes=ride_sems),
        compiler_params=pltpu.CompilerParams(dimension_semantics=("arbitrary", "arbitrary"),
                                             vmem_limit_bytes=VMEM_LIMIT),
    )(pos, h_t, dproj, *ride)


def _grad_w_out(lhs, rhs, k_tile):
    tp = lhs.shape[1]

    def body(a_ref, b_ref, o_ref):
        @pl.when(pl.program_id(1) == 0)
        def _():
            o_ref[...] = jnp.zeros(o_ref.shape, F32)

        o_ref[...] += _dot_tn(a_ref[...], b_ref[...]).reshape(N_DEV, ROWS_OUT, D_MODEL)

    return pl.pallas_call(
        body, name="grad_w_out", grid=(3, tp // k_tile),
        out_shape=jax.ShapeDtypeStruct((N_DEV, 3, ROWS_OUT, D_MODEL), F32),
        in_specs=[pl.BlockSpec((None, k_tile, D_MODEL), lambda w, k: (w, k, 0)),
                  pl.BlockSpec((None, k_tile, D_MODEL), lambda w, k: (w, k, 0))],
        out_specs=pl.BlockSpec((N_DEV, None, ROWS_OUT, D_MODEL), lambda w, k: (0, w, 0, 0)),
        compiler_params=pltpu.CompilerParams(dimension_semantics=("arbitrary", "arbitrary"),
                                             vmem_limit_bytes=VMEM_LIMIT),
    )(lhs, rhs)


def _adamw_math(w, g, m, v):
    m = ADAM_B1 * m + (1.0 - ADAM_B1) * g
    v = ADAM_B2 * v + (1.0 - ADAM_B2) * (g * g)
    m_hat = m / (1.0 - ADAM_B1 ** ADAM_STEP)
    v_hat = v / (1.0 - ADAM_B2 ** ADAM_STEP)
    delta = -ADAM_LR * (m_hat / (jnp.sqrt(v_hat) + ADAM_EPS) + ADAM_WD * w)
    return delta, m, v


def _adamw_sharded(pos, mine, theirs, landed, weights, row_tile, name):
    rows, n = weights[0][0].shape
    n_slots = mine.shape[0]
    per_shard = rows // row_tile
    assert per_shard == 1 or len(weights) == 1

    def mine_map(j, t, pos_ref):
        chip = 2 * pos_ref[0] + pos_ref[1]
        return (2 * chip + pos_ref[2] if n_slots == N_DEV else chip), j * per_shard + t, 0

    def theirs_map(j, t, pos_ref):
        return 2 * pos_ref[0] + pos_ref[1], j * per_shard + t, 0

    def body(pos_ref, mine_ref, theirs_ref, land_ref, *refs):
        ins, outs = refs[:3 * len(weights)], refs[3 * len(weights):]
        g = mine_ref[...] + theirs_ref[...]
        for k in range(3):
            g = g + land_ref[k].astype(F32)
        for j in range(len(weights)):
            @pl.when(pl.program_id(0) == j)
            def _(j=j):
                w_ref, m_ref, v_ref = ins[3 * j:3 * j + 3]
                delta, m_new, v_new = _adamw_math(w_ref[...], g, m_ref[...], v_ref[...])
                for ref, val in zip(outs[4 * j:4 * j + 4], (g, delta, m_new, v_new)):
                    ref[...] = val

    tile = pl.BlockSpec((row_tile, n), lambda j, t, pos_ref: (t, 0))
    res = pl.pallas_call(
        body, name=name,
        out_shape=[jax.ShapeDtypeStruct((rows, n), F32)] * (4 * len(weights)),
        grid_spec=pltpu.PrefetchScalarGridSpec(
            num_scalar_prefetch=1, grid=(len(weights), per_shard),
            in_specs=[pl.BlockSpec((None, row_tile, n), mine_map), pl.BlockSpec((None, row_tile, n), theirs_map),
                      pl.BlockSpec((3, row_tile, n), lambda j, t, pos_ref: (0, j * per_shard + t, 0))]
            + [tile] * (3 * len(weights)),
            out_specs=[tile] * (4 * len(weights))),
        compiler_params=pltpu.CompilerParams(dimension_semantics=("arbitrary", "arbitrary")),
    )(pos, mine, theirs, landed, *[a for wmv in weights for a in wmv])
    return [res[4 * j:4 * j + 4] for j in range(len(weights))]


def _adamw_small(gathered, gathered_cols, params):
    n_par, n_src = len(params), len(gathered)

    def body(*refs):
        g_refs, gc_refs = refs[:n_src], refs[n_src:2 * n_src]
        ins = refs[2 * n_src:2 * n_src + 3 * n_par]
        outs = refs[2 * n_src + 3 * n_par:]
        loss_ref = outs[4 * n_par]

        def reduced(ref, row, n_rows):
            g = ref[0, row:row + n_rows, :]
            for d in range(1, N_DEV):
                g = g + ref[d, row:row + n_rows, :]
            return g

        for p, (src, row, n_rows, sharded, _, _, _) in enumerate(params):
            g = reduced((gc_refs if sharded else g_refs)[src], row, n_rows)
            w_ref, m_ref, v_ref = ins[3 * p:3 * p + 3]
            delta, m_new, v_new = _adamw_math(w_ref[...], g, m_ref[...], v_ref[...])
            outs[4 * p][...] = g
            outs[4 * p + 1][...] = delta
            outs[4 * p + 2][...] = m_new
            outs[4 * p + 3][...] = v_new
        loss = jnp.sum(reduced(g_refs[0], ROW_LOSS, 1), axis=1, keepdims=True)
        loss_ref[...] = jnp.broadcast_to(loss, loss_ref.shape)

    out_shape = []
    for (_, _, _, _, w, _, _) in params:
        out_shape += [jax.ShapeDtypeStruct(w.shape, F32)] * 4
    out_shape.append(jax.ShapeDtypeStruct((1, LANES), F32))
    flat = [a for (_, _, _, _, w, m, v) in params for a in (w, m, v)]
    return pl.pallas_call(
        body, name="adamw_small", out_shape=out_shape,
        in_specs=[_VMEM] * (2 * n_src + len(flat)), out_specs=[_VMEM] * len(out_shape),
    )(*gathered, *gathered_cols, *flat)


def _pad_rows(a, rows):
    return jnp.concatenate([a, jnp.zeros((rows - a.shape[0], a.shape[1]), a.dtype)], axis=0)


def kernel(x, meta_tokens, norm_g, w_in, conv_a_w, conv_a_b, ln_a_g, ln_a_b, w_a_out, b_a_out, conv_b_w, w_b_out, w_out, final_g, loss_target, m_meta_tokens, m_norm_g, m_w_in, m_conv_a_w, m_conv_a_b, m_ln_a_g, m_ln_a_b, m_w_a_out, m_b_a_out, m_conv_b_w, m_w_b_out, m_w_out, m_final_g, v_meta_tokens, v_norm_g, v_w_in, v_conv_a_w, v_conv_a_b, v_ln_a_g, v_ln_a_b, v_w_a_out, v_b_a_out, v_conv_b_w, v_w_b_out, v_w_out, v_final_g):
    seq = x.shape[1]
    assert x.shape == (1, seq, D_MODEL) and seq % TILE == 0 and w_in.shape == (1, D_MODEL, COLS)
    n_tiles = seq // TILE + 1
    tp = n_tiles * TILE
    pos = jnp.stack([lax.axis_index("x"), lax.axis_index("y"), lax.axis_index("c")]).astype(jnp.int32)
    me = 4 * pos[0] + 2 * pos[1] + pos[2]
    x2d = x[0]
    tgt2d = loss_target[0]

    small = jnp.concatenate([meta_tokens, _pad_rows(conv_a_w[0], 32), _pad_rows(conv_b_w[0], SUBLANES)], axis=0)
    small_all = _all_gather(small, "gather_small")
    small_all = small_all.transpose(1, 0, 2).reshape(small.shape[0], D_MODEL)
    meta_full, conv_a_full, conv_b_full = small_all[0:N_META], small_all[N_META:N_META + 32], small_all[N_META + 32:]
    meta_tile = jnp.concatenate([jnp.zeros((TILE - N_META, D_MODEL), F32), meta_full], axis=0)
    final_g2 = final_g.reshape(1, D_MODEL)

    w_out_shards = [w[0].astype(BF16) for w in (w_a_out, w_b_out, w_out)]
    h_t, proj, w_in_all, *w_out_all = _gather_norm_proj(pos, x2d, meta_tile, norm_g, w_in[0].astype(BF16),
                                                      w_out_shards, 3)
    w_out_all = [w.reshape(D_MODEL, D_MODEL) for w in w_out_all]
    w_out_all_t = [w.T for w in w_out_all]
    dproj, ds1, lhs, rhs, vec, d_conv_a, d_conv_b = _fused_pass(
        proj, x2d, tgt2d, meta_tile, conv_a_full, conv_a_b, ln_a_g, ln_a_b, b_a_out, conv_b_full, final_g2,
        w_out_all[0], w_out_all[1], w_out_all[2], w_out_all_t[0], w_out_all_t[1], w_out_all_t[2], n_tiles)
    k_tile = tp // 3
    gw_out = _grad_w_out(lhs, rhs, k_tile).reshape(N_DEV, 3 * ROWS_OUT, D_MODEL)
    small_a = jnp.concatenate([vec, d_conv_a, d_conv_b], axis=0)[None]
    gw_far, their_out, small_a_all = _grad_w_in_half(
        pos, h_t, dproj, k_tile, True, [("sibling", (gw_out,)), ("all", (small_a,))], "grad_w_in_far")
    parts_out = _chip_partial(pos, gw_out, their_out, (1, 2, 3), BF16, ROWS_OUT, "rs_parts_w_out")
    gw_near, their_in, land_out = _grad_w_in_half(
        pos, h_t, dproj, k_tile, False, [("sibling_half", (gw_far,)), ("chips", (parts_out,))], "grad_w_in_near")
    parts_in = _chip_partial(pos, gw_near, their_in, (1, 2, 3), BF16, 256, "rs_parts_w_in")
    grad_x, d_meta_tile, vec_in, land_in = _input_bwd(dproj, ds1, x2d, meta_tile, norm_g, w_in_all, min(256, seq),
                                                      (parts_in,))

    small_b = jnp.concatenate([vec_in, d_meta_tile[TILE - N_META:]], axis=0)
    small_b_all = _all_gather(small_b, "gather_small_grads")
    small_all = [small_a_all, small_b_all]
    small_cols = [lax.dynamic_slice_in_dim(g, me * LANES, LANES, axis=2) for g in small_all]

    (res_in,) = _adamw_sharded(pos, gw_near, their_in, land_in, [(w_in[0], m_w_in[0], v_w_in[0])], 128, "adamw_w_in")
    res_out = _adamw_sharded(
        pos, gw_out, their_out, land_out,
        [(w_a_out[0], m_w_a_out[0], v_w_a_out[0]), (w_b_out[0], m_w_b_out[0], v_w_b_out[0]),
         (w_out[0], m_w_out[0], v_w_out[0])], ROWS_OUT, "adamw_w_out")
    params = [
        (1, ROW_META, N_META, True, meta_tokens, m_meta_tokens, v_meta_tokens),
        (1, ROW_NORM_G, 1, False, norm_g, m_norm_g, v_norm_g),
        (0, ROW_CONV_A_W, CONV_A, True, conv_a_w[0], m_conv_a_w[0], v_conv_a_w[0]),
        (0, ROW_CONV_A_B, 1, False, conv_a_b, m_conv_a_b, v_conv_a_b),
        (0, ROW_LN_G, 1, False, ln_a_g, m_ln_a_g, v_ln_a_g),
        (0, ROW_LN_B, 1, False, ln_a_b, m_ln_a_b, v_ln_a_b),
        (0, ROW_B_A_OUT, 1, False, b_a_out, m_b_a_out, v_b_a_out),
        (0, ROW_CONV_B_W, CONV_B, True, conv_b_w[0], m_conv_b_w[0], v_conv_b_w[0]),
        (0, ROW_FINAL_G, 1, False, final_g2, m_final_g.reshape(1, D_MODEL), v_final_g.reshape(1, D_MODEL)),
    ]
    res_small = _adamw_small(small_all, small_cols, params)
    loss = res_small[-1][0, 0]

    def small_res(p, kind, shape):
        return res_small[4 * p + kind].reshape(shape)

    per_weight = []
    for kind in range(4):
        per_weight.append([
            small_res(0, kind, meta_tokens.shape),
            small_res(1, kind, norm_g.shape),
            res_in[kind].reshape(w_in.shape),
            small_res(2, kind, conv_a_w.shape),
            small_res(3, kind, conv_a_b.shape),
            small_res(4, kind, ln_a_g.shape),
            small_res(5, kind, ln_a_b.shape),
            res_out[0][kind].reshape(w_a_out.shape),
            small_res(6, kind, b_a_out.shape),
            small_res(7, kind, conv_b_w.shape),
            res_out[1][kind].reshape(w_b_out.shape),
            res_out[2][kind].reshape(w_out.shape),
            small_res(8, kind, final_g.shape),
        ])
    return (loss, grad_x.reshape(x.shape), *per_weight[0], *per_weight[1], *per_weight[2], *per_weight[3])
```

```python
import functools

import jax
import jax.numpy as jnp
from jax import lax
from jax.experimental import pallas as pl
from jax.experimental.pallas import tpu as pltpu

D_MODEL = 1024
N_META = 16
N_DEV = 8
D_IN = 9 * D_MODEL
COLS = D_IN // N_DEV
ROWS_OUT = D_MODEL // N_DEV
CONV_A = 31
CONV_B = 3
EPS = 1e-6

ADAM_LR = 0.001
ADAM_B1 = 0.9
ADAM_B2 = 0.999
ADAM_EPS = 1e-08
ADAM_WD = 0.01
ADAM_STEP = 10

TILE = 128
LANES = 128
N_CHUNK = D_MODEL // LANES
HALO = 32
SUBLANES = 8
VMEM_LIMIT = 56 * 1024 * 1024

ROW_FINAL_G, ROW_B_A_OUT, ROW_LN_G, ROW_LN_B, ROW_CONV_A_B, ROW_LOSS = 0, 1, 2, 3, 4, 5
ROW_CONV_A_W, ROW_CONV_B_W, SMALL_A_ROWS = 8, 40, 48
ROW_NORM_G, ROW_META, SMALL_B_ROWS = 0, 8, 24

MESH = pl.DeviceIdType.MESH
_ANY = pl.BlockSpec(memory_space=pl.ANY)
_VMEM = pl.BlockSpec(memory_space=pltpu.VMEM)
BF16 = jnp.bfloat16
F32 = jnp.float32


def _resident(shape):
    return pl.BlockSpec(shape, lambda *_: (0,) * len(shape), pipeline_mode=pl.Buffered(1))


def _sigmoid(v):
    return jax.nn.sigmoid(v)


def _dot(a, b):
    return jnp.dot(a, b, preferred_element_type=F32)


def _dot_nt(a, b):
    return lax.dot_general(a, b, (((1,), (1,)), ((), ())), preferred_element_type=F32)


def _dot_tn(a, b):
    return lax.dot_general(a, b, (((0,), (0,)), ((), ())), preferred_element_type=F32)


def _colsum(v):
    return jnp.sum(v, axis=0, keepdims=True)


def _rowmean(v):
    parts = [v[:, LANES * c:LANES * (c + 1)] for c in range(v.shape[1] // LANES)]
    return jnp.sum(functools.reduce(jnp.add, parts), axis=-1, keepdims=True) * (1.0 / v.shape[1])


def _fold8(v):
    parts = [v[SUBLANES * g:SUBLANES * (g + 1)] for g in range(v.shape[0] // SUBLANES)]
    return functools.reduce(jnp.add, parts)


def _sibling_copies(srcs, dsts, send_sems, recv_sems):
    x, y, c = lax.axis_index("x"), lax.axis_index("y"), lax.axis_index("c")
    return [pltpu.make_async_remote_copy(
        src_ref=src.at[2 * q + (1 - c)], dst_ref=dst.at[q],
        send_sem=send_sems.at[4 * a + q], recv_sem=recv_sems.at[4 * a + q],
        device_id=(x, y, 1 - c), device_id_type=MESH)
        for a, (src, dst) in enumerate(zip(srcs, dsts)) for q in range(4)]


def _chip_copies(srcs, dsts, send_sems, recv_sems):
    x, y, c = lax.axis_index("x"), lax.axis_index("y"), lax.axis_index("c")
    targets = [(x, 1 - y, c), (1 - x, y, c), (1 - x, 1 - y, c)]
    return [pltpu.make_async_remote_copy(
        src_ref=src.at[k], dst_ref=dst.at[k],
        send_sem=send_sems.at[3 * a + k], recv_sem=recv_sems.at[3 * a + k],
        device_id=targets[k], device_id_type=MESH)
        for a, (src, dst) in enumerate(zip(srcs, dsts)) for k in range(3)]


def _sibling_half_copies(srcs, dsts, send_sems, recv_sems):
    x, y, c = lax.axis_index("x"), lax.axis_index("y"), lax.axis_index("c")
    return [pltpu.make_async_remote_copy(
        src_ref=src.at[q], dst_ref=dst.at[q],
        send_sem=send_sems.at[4 * a + q], recv_sem=recv_sems.at[4 * a + q],
        device_id=(x, y, 1 - c), device_id_type=MESH)
        for a, (src, dst) in enumerate(zip(srcs, dsts)) for q in range(4)]


def _all_copies(srcs, dsts, send_sems, recv_sems):
    x, y, c = lax.axis_index("x"), lax.axis_index("y"), lax.axis_index("c")
    mine = 4 * x + 2 * y + c
    copies = []
    for a, (src, dst) in enumerate(zip(srcs, dsts)):
        copies.append(pltpu.make_async_copy(src.at[0], dst.at[mine], send_sems.at[N_DEV * a]))
        for k in range(1, N_DEV):
            copies.append(pltpu.make_async_remote_copy(
                src_ref=src.at[0], dst_ref=dst.at[mine],
                send_sem=send_sems.at[N_DEV * a + k], recv_sem=recv_sems.at[N_DEV * a + k],
                device_id=(x ^ (k >> 2), y ^ ((k >> 1) & 1), c ^ (k & 1)), device_id_type=MESH))
    return copies


_EXCHANGES = {"sibling": (4, _sibling_copies, 4), "sibling_half": (4, _sibling_half_copies, 4),
              "chips": (3, _chip_copies, 3), "all": (N_DEV, _all_copies, N_DEV)}


def _exchange_shapes(kind, arrays):
    per_array, _, slots = _EXCHANGES[kind]
    out_shape = [jax.ShapeDtypeStruct((slots,) + a.shape[1:], a.dtype) for a in arrays]
    sems = [pltpu.SemaphoreType.DMA((per_array * len(arrays),))] * 2
    return out_shape, sems


def _ride_shapes(rides):
    shapes, sems = [], []
    for kind, arrays in rides:
        ride_shapes, ride_sems = _exchange_shapes(kind, arrays)
        shapes += ride_shapes
        sems += ride_sems
    return shapes, sems


def _riding(body, n_in, n_out, rides, is_first, is_last):
    counts = [len(arrays) for _, arrays in rides]
    n_arr = sum(counts)

    def wrapped(*refs):
        ins, srcs = refs[:n_in], refs[n_in:n_in + n_arr]
        outs = refs[n_in + n_arr:n_in + n_arr + n_out]
        dsts = refs[n_in + n_arr + n_out:n_in + 2 * n_arr + n_out]
        first_sem = len(refs) - 2 * len(rides)
        scratch, sems = refs[n_in + 2 * n_arr + n_out:first_sem], refs[first_sem:]

        def copies():
            made, at = [], 0
            for r, ((kind, _), n) in enumerate(zip(rides, counts)):
                made += _EXCHANGES[kind][1](srcs[at:at + n], dsts[at:at + n], sems[2 * r], sems[2 * r + 1])
                at += n
            return made

        @pl.when(is_first())
        def _():
            for cp in copies():
                cp.start()

        body(*ins, *outs, *scratch)

        @pl.when(is_last())
        def _():
            for cp in copies():
                cp.wait()

    return wrapped


def _chip_partial(pos, mine, theirs, relations, out_dtype, row_tile, name):
    n_slots, m, n = mine.shape
    q0 = relations[0]

    def chip_of(qi, pos_ref):
        q = qi + q0
        return pos_ref[0] ^ (q >> 1), pos_ref[1] ^ (q & 1)

    def mine_map(qi, t, pos_ref):
        px, py = chip_of(qi, pos_ref)
        return (4 * px + 2 * py + pos_ref[2] if n_slots == N_DEV else 2 * px + py), t, 0

    def theirs_map(qi, t, pos_ref):
        px, py = chip_of(qi, pos_ref)
        return 2 * px + py, t, 0

    def body(pos_ref, a_ref, b_ref, o_ref):
        o_ref[...] = (a_ref[...] + b_ref[...]).astype(out_dtype)

    return pl.pallas_call(
        body, name=name,
        out_shape=jax.ShapeDtypeStruct((len(relations), m, n), out_dtype),
        grid_spec=pltpu.PrefetchScalarGridSpec(
            num_scalar_prefetch=1, grid=(len(relations), m // row_tile),
            in_specs=[pl.BlockSpec((None, row_tile, n), mine_map), pl.BlockSpec((None, row_tile, n), theirs_map)],
            out_specs=pl.BlockSpec((None, row_tile, n), lambda qi, t, pos_ref: (qi, t, 0))),
        compiler_params=pltpu.CompilerParams(dimension_semantics=("arbitrary", "arbitrary")),
    )(pos, mine, theirs)


PARTS = ((0, 512), (512, 640))


def _gather_norm_proj(pos, x2d, small_shard, norm_g, w_in_shard, w_out_shards, n_chunk):
    seq = x2d.shape[0]
    n_tiles = seq // TILE + 1
    tp = n_tiles * TILE
    n_parts = len(PARTS)
    widest = max(width for _, width in PARTS)
    units = [(s, u) for s in range(2) for u in range(n_parts)]
    for first in (2, 5):
        units += [(first + j, u) for u in range(n_parts) for j in range(2)] + [(first + 2, u) for u in range(n_parts)]
    n_units = len(units)
    n_steps = n_tiles + n_units
    chunk = tp // n_chunk

    def body(pos_ref, x_ref, g_ref, small_ref, win_ref, wa_ref, wb_ref, wo_ref,
             ht_ref, proj_ref, meta_ref, small_all, win_all, wa_all, wb_all, wo_all,
             h_all, wbuf, rbuf, small_buf, send_sems, recv_sems, local_sems):
        g = pl.program_id(0)
        x, y, c = lax.axis_index("x"), lax.axis_index("y"), lax.axis_index("c")
        me, sibling = (x, y, c), (x, y, 1 - c)
        chips = [(1 - x, y), (x, 1 - y), (1 - x, 1 - y)]
        shards = (win_ref, wa_ref, wb_ref, wo_ref, small_ref)
        gathered = (win_all, wa_all, wb_all, wo_all, small_all)
        n_arrays = len(shards)
        blocks = [me, sibling] + [(*chip, c) for chip in chips] + [(*chip, 1 - c) for chip in chips]

        def index(block):
            px, py, pc = block
            return 4 * px + 2 * py + pc

        def part(ref, a, u):
            return ref.at[:, pl.ds(PARTS[u][0], PARTS[u][1])] if a == 0 else ref

        def slot(a, block, u):
            return part(gathered[a].at[index(block)], a, u)

        def sem(a, k, u):
            return n_parts * k + u if a == 0 else 7 * n_parts + 7 * (a - 1) + k

        def copy(a, k, block, to, u=0, from_shard=False):
            return pltpu.make_async_remote_copy(
                src_ref=part(shards[a], a, u) if from_shard else slot(a, block, u), dst_ref=slot(a, block, u),
                send_sem=send_sems.at[sem(a, k, u)], recv_sem=recv_sems.at[sem(a, k, u)],
                device_id=to, device_id_type=MESH)

        def keep(a):
            return pltpu.make_async_copy(shards[a], gathered[a].at[index(me)], local_sems.at[a])

        def load(m):
            s, u = units[m]
            src = part(win_ref, 0, u) if s == 0 else slot(0, blocks[s], u)
            return pltpu.make_async_copy(src, wbuf.at[m % 2, :, 0:PARTS[u][1]], local_sems.at[n_arrays + m % 2])

        def store(m):
            s, u = units[m]
            col0 = pl.multiple_of(index(blocks[s]) * COLS + PARTS[u][0], LANES)
            return pltpu.make_async_copy(rbuf.at[m % 2, :, 0:PARTS[u][1]],
                                         proj_ref.at[:, pl.ds(col0, PARTS[u][1])], local_sems.at[n_arrays + 2 + m % 2])

        def by_x(a, u):
            return u == 0 if a == 0 else a < 3

        def relay(a, u=0):
            src, to = (blocks[3], blocks[2]) if by_x(a, u) else (blocks[2], blocks[3])
            return copy(a, 3, src, to, u)

        def arrive(m):
            s, u = units[m]
            if s == 1:
                copy(0, 0, sibling, me, u).wait_recv()
            elif 2 <= s <= 4:
                copy(0, s - 1, blocks[s], me, u).wait_recv()
                copy(0, s + 2, blocks[s], sibling, u).start()
                if s < 4 and by_x(0, u) == (s == 3):
                    relay(0, u).start()
            elif s >= 5:
                copy(0, s - 1, blocks[s], me, u).wait_recv()
                if u == 0:
                    for a in range(1, 4):
                        copy(a, s - 4, blocks[s - 3], me).wait_recv()
                        copy(a, s - 1, blocks[s - 3], sibling).start()
                        if s < 7 and by_x(a, 0) == (s == 6):
                            relay(a).start()

        targets = [sibling, blocks[2], blocks[3]]

        @pl.when(g == 0)
        def _():
            for a in range(n_arrays):
                keep(a).start()
            for k, to in enumerate(targets):
                copy(4, k, me, to, from_shard=True).start()
            for u in range(n_parts):
                for k, to in enumerate(targets):
                    copy(0, k, me, to, u, from_shard=True).start()
            for a in range(1, 4):
                for k, to in enumerate(targets):
                    copy(a, k, me, to, from_shard=True).start()
            load(0).start()

        @pl.when(g == min(2, n_tiles - 2))
        def _():
            for s in (2, 3):
                copy(4, s - 1, blocks[s], me).wait_recv()
                copy(4, s + 2, blocks[s], sibling).start()
            relay(4).start()

        @pl.when(g == min(8, n_tiles - 2))
        def _():
            copy(4, 3, blocks[4], me).wait_recv()
            copy(4, 6, blocks[4], sibling).start()

        @pl.when(g == n_tiles - 2)
        def _():
            copy(4, 0, sibling, me).wait_recv()
            for j in range(3):
                copy(4, 4 + j, blocks[5 + j], me).wait_recv()
            keep(4).wait()
            fetch = pltpu.make_async_copy(small_all, small_buf, local_sems.at[n_arrays + 4])
            fetch.start()
            fetch.wait()
            meta_ref[0:TILE - N_META, :] = jnp.zeros((TILE - N_META, D_MODEL), F32)
            meta_ref[TILE - N_META:TILE, :] = jnp.concatenate([small_buf[d, 0:N_META, :] for d in range(N_DEV)], axis=1)

        @pl.when(g < n_tiles)
        def _():
            s0 = jnp.where(g == n_tiles - 1, meta_ref[...], x_ref[...])
            r = lax.rsqrt(_rowmean(s0 * s0) + EPS)
            h32 = (s0 * r) * g_ref[...]
            ht_ref[...] = h32.T.astype(BF16)
            h_all[pl.ds(pl.multiple_of(g * TILE, TILE), TILE), :] = h32.astype(BF16)

        for m in range(n_units):
            @pl.when(g == n_tiles + m)
            def _(m=m):
                load(m).wait()
                if m + 1 < n_units:
                    arrive(m + 1)
                    load(m + 1).start()
                if m >= 2:
                    store(m - 2).wait()

        m_now = jnp.maximum(g - n_tiles, 0)
        u_now = functools.reduce(jnp.add, [jnp.where(m_now == m, u, 0) for m, (_, u) in enumerate(units)])
        for u, (_, width) in enumerate(PARTS):
            @pl.when((g >= n_tiles) & (u_now == u))
            def _(width=width):
                w = wbuf[m_now % 2, :, 0:width]
                for r in range(n_chunk):
                    rbuf[m_now % 2, r * chunk:(r + 1) * chunk, 0:width] = _dot(h_all[r * chunk:(r + 1) * chunk, :], w)

        for m in range(n_units):
            @pl.when(g == n_tiles + m)
            def _(m=m):
                store(m).start()

        @pl.when(g == n_steps - 1)
        def _():
            store(n_units - 2).wait()
            store(n_units - 1).wait()
            for a in range(1, 4):
                copy(a, 0, sibling, me).wait_recv()
                for j in range(3):
                    copy(a, 4 + j, blocks[5 + j], me).wait_recv()
            for a in range(n_arrays):
                for u in range(n_parts if a == 0 else 1):
                    for k, to in enumerate(targets):
                        copy(a, k, me, to, u, from_shard=True).wait_send()
                    relay(a, u).wait_send()
                    for j in range(3):
                        copy(a, 4 + j, blocks[2 + j], sibling, u).wait_send()
                if a < 4:
                    keep(a).wait()

    n_x = n_tiles - 1
    return pl.pallas_call(
        body, name="gather_norm_proj",
        out_shape=[jax.ShapeDtypeStruct((D_MODEL, tp), BF16), jax.ShapeDtypeStruct((tp, D_IN), F32),
                   jax.ShapeDtypeStruct((TILE, D_MODEL), F32), jax.ShapeDtypeStruct((N_DEV,) + small_shard.shape, F32),
                   jax.ShapeDtypeStruct((N_DEV,) + w_in_shard.shape, BF16)]
                  + [jax.ShapeDtypeStruct((N_DEV,) + w.shape, BF16) for w in w_out_shards],
        grid_spec=pltpu.PrefetchScalarGridSpec(
            num_scalar_prefetch=1, grid=(n_steps,),
            in_specs=[pl.BlockSpec((TILE, D_MODEL), lambda g, pos_ref: (jnp.minimum(g, n_x - 1), 0)),
                      _VMEM, _ANY, _ANY, _ANY, _ANY, _ANY],
            out_specs=[pl.BlockSpec((D_MODEL, TILE), lambda g, pos_ref: (0, jnp.minimum(g, n_tiles - 1))),
                       _ANY, _VMEM, _ANY, _ANY, _ANY, _ANY, _ANY],
            scratch_shapes=[pltpu.VMEM((tp, D_MODEL), BF16), pltpu.VMEM((2, D_MODEL, widest), BF16),
                            pltpu.VMEM((2, tp, widest), F32), pltpu.VMEM((N_DEV,) + small_shard.shape, F32),
                            pltpu.SemaphoreType.DMA((7 * n_parts + 28,)), pltpu.SemaphoreType.DMA((7 * n_parts + 28,)),
                            pltpu.SemaphoreType.DMA((10,))]),
        compiler_params=pltpu.CompilerParams(dimension_semantics=("arbitrary",), vmem_limit_bytes=VMEM_LIMIT),
    )(pos, x2d, norm_g, small_shard, w_in_shard, *w_out_shards)


C_AVAL, C_AGLU, C_AZ, C_BB, C_BC, C_BX, C_BZ, C_GA, C_GB = (k * D_MODEL for k in range(9))
S_AZ, S_BB, S_BZ, S_GA, S_GB = (k * D_MODEL for k in range(5))


def _fused_pass(proj, x2d, tgt2d, meta_tile, conv_a_w, conv_a_b, ln_a_g, ln_a_b, b_a_out, conv_b_w, final_g,
                w_a, w_b, w_o, w_a_t, w_b_t, w_o_t, n_tiles):
    T = TILE
    tp = n_tiles * T
    inv_d = 1.0 / D_MODEL

    def block_of(tile):
        return jnp.where(tile == 0, n_tiles - 1, tile - 1)

    def cur(i):
        return block_of(jnp.minimum(i, n_tiles - 1))

    def prev(i):
        return block_of(jnp.clip(i - 1, 0, n_tiles - 1))

    def xblk(i):
        return jnp.maximum(jnp.minimum(i, n_tiles - 1) - 1, 0)

    def body(proj_ref, x_ref, tgt_ref, meta_ref, caw_ref, cab_ref, lng_ref, lnb_ref, bao_ref, cbw_ref, fg_ref,
             wa_ref, wb_ref, wo_ref, wat_ref, wbt_ref, wot_ref,
             dproj_ref, ds1_ref, lhs_ref, rhs_ref, small_ref,
             ua0_buf, cb_buf, dua1_buf, dc3_buf, aprev, cprev, stage, ua1_buf, c3_buf,
             dpa_buf, dpb_buf, dcaw8, dcbw8, shift_buf):
        i = pl.program_id(0)

        @pl.when(i == 0)
        def _init():
            for buf in (ua0_buf, cb_buf, dua1_buf, dc3_buf, aprev, cprev, dcaw8, dcbw8):
                buf[...] = jnp.zeros(buf.shape, buf.dtype)
            small_ref[...] = jnp.zeros(small_ref.shape, F32)

        @pl.when(i >= 1)
        def _emit_stage():
            dproj_ref[:, C_AZ:C_BC] = stage[:, S_AZ:S_BZ]
            dproj_ref[:, C_BZ:D_IN] = stage[:, S_BZ:S_GB + D_MODEL]

        @pl.when(i < n_tiles)
        def _front():
            def conv_chunk(cc, carry):
                c0 = pl.multiple_of(cc * LANES, LANES)
                lanes = pl.ds(c0, LANES)

                def col(base):
                    return pl.ds(pl.multiple_of(base + cc * LANES, LANES), LANES)

                ua0 = proj_ref[:, col(C_AVAL)] * _sigmoid(proj_ref[:, col(C_AGLU)])
                ua0_buf[T:2 * T, lanes] = ua0
                acc = jnp.broadcast_to(cab_ref[:, lanes], (T, LANES))
                lead = HALO - (CONV_A - 1)
                for r in range(SUBLANES):
                    taps = [k for k in range(CONV_A) if (k + lead) % SUBLANES == r]
                    rows = T + SUBLANES * max((k + lead) // SUBLANES for k in taps)
                    if r:
                        shift_buf[r, 0:rows, :] = ua0_buf[pl.ds(T - HALO + r, rows), lanes]
                    for k in taps:
                        q = (k + lead) // SUBLANES
                        if r:
                            win = shift_buf[r, SUBLANES * q:SUBLANES * q + T, :]
                        else:
                            win = ua0_buf[pl.ds(T - HALO + SUBLANES * q, T), lanes]
                        acc = acc + caw_ref[k:k + 1, lanes] * win
                ua1_buf[:, lanes] = acc
                cb = proj_ref[:, col(C_BC)] * proj_ref[:, col(C_BX)]
                cb_buf[T:2 * T, lanes] = cb
                acc3 = cbw_ref[0:1, lanes] * cb_buf[pl.ds(T - 2, T), lanes]
                for k in range(1, CONV_B):
                    acc3 = acc3 + cbw_ref[k:k + 1, lanes] * cb_buf[pl.ds(T - (CONV_B - 1) + k, T), lanes]
                c3_buf[:, lanes] = acc3
                return carry

            lax.fori_loop(0, N_CHUNK, conv_chunk, 0)

            ua1 = ua1_buf[...]
            xc = ua1 - _rowmean(ua1)
            rstd = lax.rsqrt(_rowmean(xc * xc) + EPS)
            xhat = xc * rstd
            ua2 = xhat * lng_ref[...] + lnb_ref[...]
            sg2 = _sigmoid(ua2)
            ua3 = ua2 * sg2
            a_z = proj_ref[:, C_AZ:C_AZ + D_MODEL]
            sz = _sigmoid(a_z)
            silu_az = a_z * sz
            lhs_ref[0] = (ua3 * silu_az).astype(BF16)
            b_z = proj_ref[:, C_BZ:C_BZ + D_MODEL]
            sbz = _sigmoid(b_z)
            silu_bz = b_z * sbz
            b_b = proj_ref[:, C_BB:C_BB + D_MODEL]
            c3 = c3_buf[...]
            ub = b_b * c3
            lhs_ref[1] = (ub * silu_bz).astype(BF16)

            ya = _dot(lhs_ref[0], wa_ref[...]) + bao_ref[...]
            yb = _dot(lhs_ref[1], wb_ref[...])
            sga = _sigmoid(proj_ref[:, C_GA:C_GA + D_MODEL])
            sgb = _sigmoid(proj_ref[:, C_GB:C_GB + D_MODEL])
            m_b = (sga * ya + sgb * yb).astype(BF16)
            lhs_ref[2] = m_b
            s0 = jnp.where(i == 0, meta_ref[...], x_ref[...])
            s1 = s0 + _dot(m_b, wo_ref[...])
            r1 = lax.rsqrt(_rowmean(s1 * s1) + EPS)
            y = (s1 * r1) * fg_ref[...]
            is_token = (i >= 1).astype(F32)
            err = (y - tgt_ref[...]) * is_token
            small_ref[ROW_LOSS:ROW_LOSS + 1, :] += (0.5 * inv_d) * _colsum(err * err)
            dy = err * inv_d
            small_ref[ROW_FINAL_G:ROW_FINAL_G + 1, :] += _colsum(dy * (s1 * r1))
            gy = dy * fg_ref[...]
            ds1 = r1 * gy - s1 * ((r1 * r1 * r1) * _rowmean(gy * s1))
            ds1_ref[...] = ds1
            ds1_b = ds1.astype(BF16)
            rhs_ref[2] = ds1_b
            dm = _dot(ds1_b, wot_ref[...])
            dya = dm * sga
            dyb = dm * sgb
            stage[:, S_GA:S_GA + D_MODEL] = (dya * ya * (1.0 - sga)).astype(BF16)
            stage[:, S_GB:S_GB + D_MODEL] = (dyb * yb * (1.0 - sgb)).astype(BF16)
            small_ref[ROW_B_A_OUT:ROW_B_A_OUT + 1, :] += _colsum(dya)
            dya_b = dya.astype(BF16)
            dyb_b = dyb.astype(BF16)
            rhs_ref[0] = dya_b
            rhs_ref[1] = dyb_b
            dpa_buf[...] = _dot(dya_b, wat_ref[...])
            dpb_buf[...] = _dot(dyb_b, wbt_ref[...])

            dpa = dpa_buf[...]
            stage[:, S_AZ:S_AZ + D_MODEL] = (dpa * ua3 * (sz + silu_az * (1.0 - sz))).astype(BF16)
            dua2 = dpa * silu_az * (sg2 + ua3 * (1.0 - sg2))
            small_ref[ROW_LN_G:ROW_LN_G + 1, :] += _colsum(dua2 * xhat)
            small_ref[ROW_LN_B:ROW_LN_B + 1, :] += _colsum(dua2)
            dxh = dua2 * lng_ref[...]
            dua1 = rstd * (dxh - _rowmean(dxh) - xhat * _rowmean(dxh * xhat))
            small_ref[ROW_CONV_A_B:ROW_CONV_A_B + 1, :] += _colsum(dua1)
            dua1_buf[T:2 * T, :] = dua1
            dpb = dpb_buf[...]
            stage[:, S_BZ:S_BZ + D_MODEL] = (dpb * ub * (sbz + silu_bz * (1.0 - sbz))).astype(BF16)
            dub = dpb * silu_bz
            stage[:, S_BB:S_BB + D_MODEL] = (dub * c3).astype(BF16)
            dc3_buf[T:2 * T, :] = dub * b_b

        @pl.when(i == n_tiles)
        def _no_later_tile():
            dua1_buf[T:2 * T, :] = jnp.zeros((T, D_MODEL), F32)
            dc3_buf[T:2 * T, :] = jnp.zeros((T, D_MODEL), F32)

        @pl.when(i >= 1)
        def _lagged():
            def convt_chunk(cc, carry):
                c0 = pl.multiple_of(cc * LANES, LANES)
                lanes = pl.ds(c0, LANES)

                def col(base):
                    return pl.ds(pl.multiple_of(base + cc * LANES, LANES), LANES)

                ua0 = ua0_buf[0:T, lanes]
                acc = jnp.zeros((T, LANES), F32)
                for r in range(SUBLANES):
                    shifts = [j for j in range(CONV_A) if j % SUBLANES == r]
                    rows = T + shifts[-1] - r
                    if r:
                        shift_buf[r, 0:rows, :] = dua1_buf[pl.ds(r, rows), lanes]
                    for j in shifts:
                        k = CONV_A - 1 - j
                        if r:
                            later = shift_buf[r, j - r:j - r + T, :]
                        else:
                            later = dua1_buf[pl.ds(j, T), lanes]
                        acc = acc + caw_ref[k:k + 1, lanes] * later
                        dcaw8[SUBLANES * k:SUBLANES * (k + 1), lanes] += _fold8(ua0 * later)
                a_val = aprev[:, col(0)]
                sg = _sigmoid(aprev[:, col(D_MODEL)])
                dproj_ref[:, col(C_AVAL)] = (acc * sg).astype(BF16)
                dproj_ref[:, col(C_AGLU)] = (acc * a_val * (sg * (1.0 - sg))).astype(BF16)

                cb = cb_buf[0:T, lanes]
                acc3 = jnp.zeros((T, LANES), F32)
                for j in range(CONV_B):
                    k = CONV_B - 1 - j
                    later = dc3_buf[pl.ds(j, T), lanes]
                    acc3 = acc3 + cbw_ref[k:k + 1, lanes] * later
                    dcbw8[SUBLANES * k:SUBLANES * (k + 1), lanes] += _fold8(cb * later)
                dproj_ref[:, col(C_BC)] = (acc3 * cprev[:, col(D_MODEL)]).astype(BF16)
                dproj_ref[:, col(C_BX)] = (acc3 * cprev[:, col(0)]).astype(BF16)
                return carry

            lax.fori_loop(0, N_CHUNK, convt_chunk, 0)

        for buf in (ua0_buf, cb_buf, dua1_buf, dc3_buf):
            buf[0:T, :] = buf[T:2 * T, :]
        aprev[...] = proj_ref[:, C_AVAL:C_AZ]
        cprev[...] = proj_ref[:, C_BC:C_BZ]

        @pl.when(i == n_tiles)
        def _finish():
            for k in range(CONV_A):
                small_ref[ROW_CONV_A_W + k:ROW_CONV_A_W + k + 1, :] = _colsum(dcaw8[SUBLANES * k:SUBLANES * (k + 1), :])
            for k in range(CONV_B):
                small_ref[ROW_CONV_B_W + k:ROW_CONV_B_W + k + 1, :] = _colsum(dcbw8[SUBLANES * k:SUBLANES * (k + 1), :])

    tile_in = lambda width: pl.BlockSpec((T, width), lambda i: (cur(i), 0))
    return pl.pallas_call(
        body, name="fused_pass", grid=(n_tiles + 1,),
        out_shape=[
            jax.ShapeDtypeStruct((tp, D_IN), BF16),
            jax.ShapeDtypeStruct((tp, D_MODEL), F32),
            jax.ShapeDtypeStruct((3, tp, D_MODEL), BF16),
            jax.ShapeDtypeStruct((3, tp, D_MODEL), BF16),
            jax.ShapeDtypeStruct((SMALL_A_ROWS, D_MODEL), F32),
        ],
        in_specs=[
            tile_in(D_IN),
            pl.BlockSpec((T, D_MODEL), lambda i: (xblk(i), 0)),
            pl.BlockSpec((T, D_MODEL), lambda i: (xblk(i), 0)),
            _VMEM, _VMEM, _VMEM, _VMEM, _VMEM, _VMEM, _VMEM, _VMEM,
            *[_resident((D_MODEL, D_MODEL)) for _ in range(6)],
        ],
        out_specs=[
            pl.BlockSpec((T, D_IN), lambda i: (prev(i), 0)),
            pl.BlockSpec((T, D_MODEL), lambda i: (cur(i), 0)),
            pl.BlockSpec((3, T, D_MODEL), lambda i: (0, cur(i), 0)),
            pl.BlockSpec((3, T, D_MODEL), lambda i: (0, cur(i), 0)),
            _VMEM,
        ],
        scratch_shapes=[
            pltpu.VMEM((2 * T, D_MODEL), F32),
            pltpu.VMEM((2 * T, D_MODEL), F32),
            pltpu.VMEM((2 * T, D_MODEL), F32),
            pltpu.VMEM((2 * T, D_MODEL), F32),
            pltpu.VMEM((T, 2 * D_MODEL), F32),
            pltpu.VMEM((T, 2 * D_MODEL), F32),
            pltpu.VMEM((T, 5 * D_MODEL), BF16),
            pltpu.VMEM((T, D_MODEL), F32),
            pltpu.VMEM((T, D_MODEL), F32),
            pltpu.VMEM((T, D_MODEL), F32),
            pltpu.VMEM((T, D_MODEL), F32),
            pltpu.VMEM((32 * SUBLANES, D_MODEL), F32),
            pltpu.VMEM((SUBLANES * SUBLANES, D_MODEL), F32),
            pltpu.VMEM((SUBLANES, T + HALO, LANES), F32),
        ],
        compiler_params=pltpu.CompilerParams(dimension_semantics=("arbitrary",), vmem_limit_bytes=VMEM_LIMIT),
    )(proj, x2d, tgt2d, meta_tile, conv_a_w, conv_a_b, ln_a_g, ln_a_b, b_a_out, conv_b_w, final_g,
      w_a, w_b, w_o, w_a_t, w_b_t, w_o_t)


def _input_bwd(dproj, ds1, x2d, meta_tile, norm_g, w_in_all, row_tile, ride):
    seq = x2d.shape[0]
    n_steps = seq // row_tile
    meta_block = seq // TILE

    def backward(dp_ref, ds1_ref, s0_ref, g_ref, w_ref, out_ref, vec_ref):
        dh = _dot_nt(dp_ref[:, 0:COLS], w_ref[0])
        for j in range(1, N_DEV):
            dh = dh + _dot_nt(dp_ref[:, j * COLS:(j + 1) * COLS], w_ref[j])
        s0v = s0_ref[...]
        r = lax.rsqrt(_rowmean(s0v * s0v) + EPS)
        gh = dh * g_ref[...]
        out_ref[...] = ds1_ref[...] + r * gh - s0v * ((r * r * r) * _rowmean(gh * s0v))
        vec_ref[ROW_NORM_G:ROW_NORM_G + 1, :] += _colsum(dh * (s0v * r))

    def body(dp_ref, ds1_ref, x_ref, dpm_ref, ds1m_ref, meta_ref, g_ref, w_ref, gx_ref, small_all_ref,
             gmeta_buf, small_buf, send_sems, recv_sems):
        t = pl.program_id(0)
        small = small_buf.at[0]

        @pl.when(t == 0)
        def _():
            small[...] = jnp.zeros(small.shape, F32)

        backward(dp_ref, ds1_ref, x_ref, g_ref, w_ref, gx_ref, small)

        @pl.when(t == n_steps - 1)
        def _():
            backward(dpm_ref, ds1m_ref, meta_ref, g_ref, w_ref, gmeta_buf, small)
            small[ROW_META:ROW_META + N_META, :] = gmeta_buf[TILE - N_META:TILE, :]
            copies = _all_copies([small_buf], [small_all_ref], send_sems, recv_sems)
            for cp in copies:
                cp.start()
            for cp in copies:
                cp.wait()

    rides = [("chips", ride)]
    ride_shapes, ride_sems = _ride_shapes(rides)
    body = _riding(body, 8, 2, rides, lambda: pl.program_id(0) == 0, lambda: pl.program_id(0) == n_steps - 1)
    return pl.pallas_call(
        body, name="input_bwd", grid=(n_steps,),
        out_shape=[jax.ShapeDtypeStruct(x2d.shape, F32),
                   jax.ShapeDtypeStruct((N_DEV, SMALL_B_ROWS, D_MODEL), F32)] + ride_shapes,
        in_specs=[pl.BlockSpec((row_tile, D_IN), lambda t: (t, 0)),
                  pl.BlockSpec((row_tile, D_MODEL), lambda t: (t, 0)),
                  pl.BlockSpec((row_tile, D_MODEL), lambda t: (t, 0)),
                  pl.BlockSpec((TILE, D_IN), lambda t: (meta_block, 0)),
                  pl.BlockSpec((TILE, D_MODEL), lambda t: (meta_block, 0)),
                  _VMEM, _VMEM, _resident((N_DEV, D_MODEL, COLS))] + [_ANY] * len(ride),
        out_specs=[pl.BlockSpec((row_tile, D_MODEL), lambda t: (t, 0)), _ANY] + [_ANY] * len(ride),
        scratch_shapes=[pltpu.VMEM((TILE, D_MODEL), F32), pltpu.VMEM((1, SMALL_B_ROWS, D_MODEL), F32),
                        pltpu.SemaphoreType.DMA((N_DEV,)), pltpu.SemaphoreType.DMA((N_DEV,))] + ride_sems,
        compiler_params=pltpu.CompilerParams(dimension_semantics=("arbitrary",), vmem_limit_bytes=VMEM_LIMIT),
    )(dproj, ds1, x2d, dproj, ds1, meta_tile, norm_g, w_in_all, *ride)


def _grad_w_in_half(pos, h_t, dproj, k_tile, other_side, rides, name):
    tp = h_t.shape[1]
    n_k = tp // k_tile

    def column_block(q, k, pos_ref):
        return k, 2 * q + (1 - pos_ref[2] if other_side else pos_ref[2])

    def body(pos_ref, h_ref, dp_ref, o_ref):
        @pl.when(pl.program_id(1) == 0)
        def _():
            o_ref[...] = jnp.zeros(o_ref.shape, F32)

        o_ref[...] += _dot(h_ref[...], dp_ref[...])

    ride = [a for _, arrays in rides for a in arrays]
    n_arr = len(ride)
    ride_shapes, ride_sems = _ride_shapes(rides)
    body = _riding(body, 3, 1, rides,
                   lambda: (pl.program_id(0) == 0) & (pl.program_id(1) == 0),
                   lambda: (pl.program_id(0) == 3) & (pl.program_id(1) == n_k - 1))
    return pl.pallas_call(
        body, name=name,
        out_shape=[jax.ShapeDtypeStruct((4, D_MODEL, COLS), F32)] + ride_shapes,
        grid_spec=pltpu.PrefetchScalarGridSpec(
            num_scalar_prefetch=1, grid=(4, n_k),
            in_specs=[pl.BlockSpec((D_MODEL, k_tile), lambda q, k, pos_ref: (0, k)),
                      pl.BlockSpec((k_tile, COLS), column_block)] + [_ANY] * n_arr,
            out_specs=[pl.BlockSpec((None, D_MODEL, COLS), lambda q, k, pos_ref: (q, 0, 0))] + [_ANY] * n_arr,
            scratch_shapes=ride_sems),
        compiler_params=pltpu.CompilerParams(dimension_semantics=("arbitrary", "arbitrary"),
                                             vmem_limit_bytes=VMEM_LIMIT),
    )(pos, h_t, dproj, *ride)


def _grad_w_out(lhs, rhs, k_tile):
    tp = lhs.shape[1]

    def body(a_ref, b_ref, o_ref):
        @pl.when(pl.program_id(1) == 0)
        def _():
            o_ref[...] = jnp.zeros(o_ref.shape, F32)

        o_ref[...] += _dot_tn(a_ref[...], b_ref[...]).reshape(N_DEV, ROWS_OUT, D_MODEL)

    return pl.pallas_call(
        body, name="grad_w_out", grid=(3, tp // k_tile),
        out_shape=jax.ShapeDtypeStruct((N_DEV, 3, ROWS_OUT, D_MODEL), F32),
        in_specs=[pl.BlockSpec((None, k_tile, D_MODEL), lambda w, k: (w, k, 0)),
                  pl.BlockSpec((None, k_tile, D_MODEL), lambda w, k: (w, k, 0))],
        out_specs=pl.BlockSpec((N_DEV, None, ROWS_OUT, D_MODEL), lambda w, k: (0, w, 0, 0)),
        compiler_params=pltpu.CompilerParams(dimension_semantics=("arbitrary", "arbitrary"),
                                             vmem_limit_bytes=VMEM_LIMIT),
    )(lhs, rhs)


def _adamw_math(w, g, m, v):
    m = ADAM_B1 * m + (1.0 - ADAM_B1) * g
    v = ADAM_B2 * v + (1.0 - ADAM_B2) * (g * g)
    m_hat = m / (1.0 - ADAM_B1 ** ADAM_STEP)
    v_hat = v / (1.0 - ADAM_B2 ** ADAM_STEP)
    delta = -ADAM_LR * (m_hat / (jnp.sqrt(v_hat) + ADAM_EPS) + ADAM_WD * w)
    return delta, m, v


def _adamw_sharded(pos, mine, theirs, landed, weights, row_tile, name):
    rows, n = weights[0][0].shape
    n_slots = mine.shape[0]
    per_shard = rows // row_tile
    assert per_shard == 1 or len(weights) == 1

    def mine_map(j, t, pos_ref):
        chip = 2 * pos_ref[0] + pos_ref[1]
        return (2 * chip + pos_ref[2] if n_slots == N_DEV else chip), j * per_shard + t, 0

    def theirs_map(j, t, pos_ref):
        return 2 * pos_ref[0] + pos_ref[1], j * per_shard + t, 0

    def body(pos_ref, mine_ref, theirs_ref, land_ref, *refs):
        ins, outs = refs[:3 * len(weights)], refs[3 * len(weights):]
        g = mine_ref[...] + theirs_ref[...]
        for k in range(3):
            g = g + land_ref[k].astype(F32)
        for j in range(len(weights)):
            @pl.when(pl.program_id(0) == j)
            def _(j=j):
                w_ref, m_ref, v_ref = ins[3 * j:3 * j + 3]
                delta, m_new, v_new = _adamw_math(w_ref[...], g, m_ref[...], v_ref[...])
                for ref, val in zip(outs[4 * j:4 * j + 4], (g, delta, m_new, v_new)):
                    ref[...] = val

    tile = pl.BlockSpec((row_tile, n), lambda j, t, pos_ref: (t, 0))
    res = pl.pallas_call(
        body, name=name,
        out_shape=[jax.ShapeDtypeStruct((rows, n), F32)] * (4 * len(weights)),
        grid_spec=pltpu.PrefetchScalarGridSpec(
            num_scalar_prefetch=1, grid=(len(weights), per_shard),
            in_specs=[pl.BlockSpec((None, row_tile, n), mine_map), pl.BlockSpec((None, row_tile, n), theirs_map),
                      pl.BlockSpec((3, row_tile, n), lambda j, t, pos_ref: (0, j * per_shard + t, 0))]
            + [tile] * (3 * len(weights)),
            out_specs=[tile] * (4 * len(weights))),
        compiler_params=pltpu.CompilerParams(dimension_semantics=("arbitrary", "arbitrary")),
    )(pos, mine, theirs, landed, *[a for wmv in weights for a in wmv])
    return [res[4 * j:4 * j + 4] for j in range(len(weights))]


def _adamw_small(gathered, gathered_cols, params):
    n_par, n_src = len(params), len(gathered)

    def body(*refs):
        g_refs, gc_refs = refs[:n_src], refs[n_src:2 * n_src]
        ins = refs[2 * n_src:2 * n_src + 3 * n_par]
        outs = refs[2 * n_src + 3 * n_par:]
        loss_ref = outs[4 * n_par]

        def reduced(ref, row, n_rows):
            g = ref[0, row:row + n_rows, :]
            for d in range(1, N_DEV):
                g = g + ref[d, row:row + n_rows, :]
            return g

        for p, (src, row, n_rows, sharded, _, _, _) in enumerate(params):
            g = reduced((gc_refs if sharded else g_refs)[src], row, n_rows)
            w_ref, m_ref, v_ref = ins[3 * p:3 * p + 3]
            delta, m_new, v_new = _adamw_math(w_ref[...], g, m_ref[...], v_ref[...])
            outs[4 * p][...] = g
            outs[4 * p + 1][...] = delta
            outs[4 * p + 2][...] = m_new
            outs[4 * p + 3][...] = v_new
        loss = jnp.sum(reduced(g_refs[0], ROW_LOSS, 1), axis=1, keepdims=True)
        loss_ref[...] = jnp.broadcast_to(loss, loss_ref.shape)

    out_shape = []
    for (_, _, _, _, w, _, _) in params:
        out_shape += [jax.ShapeDtypeStruct(w.shape, F32)] * 4
    out_shape.append(jax.ShapeDtypeStruct((1, LANES), F32))
    flat = [a for (_, _, _, _, w, m, v) in params for a in (w, m, v)]
    return pl.pallas_call(
        body, name="adamw_small", out_shape=out_shape,
        in_specs=[_VMEM] * (2 * n_src + len(flat)), out_specs=[_VMEM] * len(out_shape),
    )(*gathered, *gathered_cols, *flat)


def _pad_rows(a, rows):
    return jnp.concatenate([a, jnp.zeros((rows - a.shape[0], a.shape[1]), a.dtype)], axis=0)


def kernel(x, meta_tokens, norm_g, w_in, conv_a_w, conv_a_b, ln_a_g, ln_a_b, w_a_out, b_a_out, conv_b_w, w_b_out, w_out, final_g, loss_target, m_meta_tokens, m_norm_g, m_w_in, m_conv_a_w, m_conv_a_b, m_ln_a_g, m_ln_a_b, m_w_a_out, m_b_a_out, m_conv_b_w, m_w_b_out, m_w_out, m_final_g, v_meta_tokens, v_norm_g, v_w_in, v_conv_a_w, v_conv_a_b, v_ln_a_g, v_ln_a_b, v_w_a_out, v_b_a_out, v_conv_b_w, v_w_b_out, v_w_out, v_final_g):
    seq = x.shape[1]
    assert x.shape == (1, seq, D_MODEL) and seq % TILE == 0 and w_in.shape == (1, D_MODEL, COLS)
    n_tiles = seq // TILE + 1
    tp = n_tiles * TILE
    pos = jnp.stack([lax.axis_index("x"), lax.axis_index("y"), lax.axis_index("c")]).astype(jnp.int32)
    me = 4 * pos[0] + 2 * pos[1] + pos[2]
    x2d = x[0]
    tgt2d = loss_target[0]

    small = jnp.concatenate([meta_tokens, _pad_rows(conv_a_w[0], 32), _pad_rows(conv_b_w[0], SUBLANES)], axis=0)
    final_g2 = final_g.reshape(1, D_MODEL)

    w_out_shards = [w[0].astype(BF16) for w in (w_a_out, w_b_out, w_out)]
    h_t, proj, meta_tile, small_params, w_in_all, *w_out_all = _gather_norm_proj(
        pos, x2d, small, norm_g, w_in[0].astype(BF16), w_out_shards, 3)
    small_params = small_params.transpose(1, 0, 2).reshape(small.shape[0], D_MODEL)
    conv_a_full, conv_b_full = small_params[N_META:N_META + 32], small_params[N_META + 32:]
    w_out_all = [w.reshape(D_MODEL, D_MODEL) for w in w_out_all]
    w_out_all_t = [w.T for w in w_out_all]
    dproj, ds1, lhs, rhs, small_a = _fused_pass(
        proj, x2d, tgt2d, meta_tile, conv_a_full, conv_a_b, ln_a_g, ln_a_b, b_a_out, conv_b_full, final_g2,
        w_out_all[0], w_out_all[1], w_out_all[2], w_out_all_t[0], w_out_all_t[1], w_out_all_t[2], n_tiles)
    k_tile = tp // 3
    gw_out = _grad_w_out(lhs, rhs, k_tile).reshape(N_DEV, 3 * ROWS_OUT, D_MODEL)
    gw_far, their_out, small_a_all = _grad_w_in_half(
        pos, h_t, dproj, k_tile, True, [("sibling", (gw_out,)), ("all", (small_a[None],))], "grad_w_in_far")
    parts_out = _chip_partial(pos, gw_out, their_out, (1, 2, 3), BF16, ROWS_OUT, "rs_parts_w_out")
    gw_near, their_in, land_out = _grad_w_in_half(
        pos, h_t, dproj, k_tile, False, [("sibling_half", (gw_far,)), ("chips", (parts_out,))], "grad_w_in_near")
    parts_in = _chip_partial(pos, gw_near, their_in, (1, 2, 3), BF16, 256, "rs_parts_w_in")
    grad_x, small_b_all, land_in = _input_bwd(dproj, ds1, x2d, meta_tile, norm_g, w_in_all, min(256, seq), (parts_in,))

    small_grads = [small_a_all, small_b_all]
    small_cols = [lax.dynamic_slice_in_dim(g, me * LANES, LANES, axis=2) for g in small_grads]

    (res_in,) = _adamw_sharded(pos, gw_near, their_in, land_in, [(w_in[0], m_w_in[0], v_w_in[0])], 128, "adamw_w_in")
    res_out = _adamw_sharded(
        pos, gw_out, their_out, land_out,
        [(w_a_out[0], m_w_a_out[0], v_w_a_out[0]), (w_b_out[0], m_w_b_out[0], v_w_b_out[0]),
         (w_out[0], m_w_out[0], v_w_out[0])], ROWS_OUT, "adamw_w_out")
    params = [
        (1, ROW_META, N_META, True, meta_tokens, m_meta_tokens, v_meta_tokens),
        (1, ROW_NORM_G, 1, False, norm_g, m_norm_g, v_norm_g),
        (0, ROW_CONV_A_W, CONV_A, True, conv_a_w[0], m_conv_a_w[0], v_conv_a_w[0]),
        (0, ROW_CONV_A_B, 1, False, conv_a_b, m_conv_a_b, v_conv_a_b),
        (0, ROW_LN_G, 1, False, ln_a_g, m_ln_a_g, v_ln_a_g),
        (0, ROW_LN_B, 1, False, ln_a_b, m_ln_a_b, v_ln_a_b),
        (0, ROW_B_A_OUT, 1, False, b_a_out, m_b_a_out, v_b_a_out),
        (0, ROW_CONV_B_W, CONV_B, True, conv_b_w[0], m_conv_b_w[0], v_conv_b_w[0]),
        (0, ROW_FINAL_G, 1, False, final_g2, m_final_g.reshape(1, D_MODEL), v_final_g.reshape(1, D_MODEL)),
    ]
    res_small = _adamw_small(small_grads, small_cols, params)
    loss = res_small[-1][0, 0]

    def small_res(p, kind, shape):
        return res_small[4 * p + kind].reshape(shape)

    per_weight = []
    for kind in range(4):
        per_weight.append([
            small_res(0, kind, meta_tokens.shape),
            small_res(1, kind, norm_g.shape),
            res_in[kind].reshape(w_in.shape),
            small_res(2, kind, conv_a_w.shape),
            small_res(3, kind, conv_a_b.shape),
            small_res(4, kind, ln_a_g.shape),
            small_res(5, kind, ln_a_b.shape),
            res_out[0][kind].reshape(w_a_out.shape),
            small_res(6, kind, b_a_out.shape),
            small_res(7, kind, conv_b_w.shape),
            res_out[1][kind].reshape(w_b_out.shape),
            res_out[2][kind].reshape(w_out.shape),
            small_res(8, kind, final_g.shape),
        ])
    return (loss, grad_x.reshape(x.shape), *per_weight[0], *per_weight[1], *per_weight[2], *per_weight[3])
```

```python
import functools

import jax
import jax.numpy as jnp
from jax import lax
from jax.experimental import pallas as pl
from jax.experimental.pallas import tpu as pltpu

D_MODEL = 1024
N_META = 16
N_DEV = 8
D_IN = 9 * D_MODEL
COLS = D_IN // N_DEV
ROWS_OUT = D_MODEL // N_DEV
CONV_A = 31
CONV_B = 3
EPS = 1e-6

ADAM_LR = 0.001
ADAM_B1 = 0.9
ADAM_B2 = 0.999
ADAM_EPS = 1e-08
ADAM_WD = 0.01
ADAM_STEP = 10

TILE = 128
LANES = 128
N_CHUNK = D_MODEL // LANES
HALO = 32
SUBLANES = 8
VMEM_LIMIT = 56 * 1024 * 1024

ROW_FINAL_G, ROW_B_A_OUT, ROW_LN_G, ROW_LN_B, ROW_CONV_A_B, ROW_LOSS = 0, 1, 2, 3, 4, 5
ROW_CONV_A_W, ROW_CONV_B_W, SMALL_A_ROWS = 8, 40, 48
ROW_NORM_G, ROW_META, SMALL_B_ROWS = 0, 8, 24

MESH = pl.DeviceIdType.MESH
_ANY = pl.BlockSpec(memory_space=pl.ANY)
_VMEM = pl.BlockSpec(memory_space=pltpu.VMEM)
BF16 = jnp.bfloat16
F32 = jnp.float32


def _resident(shape):
    return pl.BlockSpec(shape, lambda *_: (0,) * len(shape), pipeline_mode=pl.Buffered(1))


def _sigmoid(v):
    return jax.nn.sigmoid(v)


def _dot(a, b):
    return jnp.dot(a, b, preferred_element_type=F32)


def _dot_nt(a, b):
    return lax.dot_general(a, b, (((1,), (1,)), ((), ())), preferred_element_type=F32)


def _dot_tn(a, b):
    return lax.dot_general(a, b, (((0,), (0,)), ((), ())), preferred_element_type=F32)


def _colsum(v):
    return jnp.sum(v, axis=0, keepdims=True)


def _rowmean(v):
    parts = [v[:, LANES * c:LANES * (c + 1)] for c in range(v.shape[1] // LANES)]
    return jnp.sum(functools.reduce(jnp.add, parts), axis=-1, keepdims=True) * (1.0 / v.shape[1])


def _fold8(v):
    parts = [v[SUBLANES * g:SUBLANES * (g + 1)] for g in range(v.shape[0] // SUBLANES)]
    return functools.reduce(jnp.add, parts)


def _sibling_copies(srcs, dsts, send_sems, recv_sems):
    x, y, c = lax.axis_index("x"), lax.axis_index("y"), lax.axis_index("c")
    return [pltpu.make_async_remote_copy(
        src_ref=src.at[2 * q + (1 - c)], dst_ref=dst.at[q],
        send_sem=send_sems.at[4 * a + q], recv_sem=recv_sems.at[4 * a + q],
        device_id=(x, y, 1 - c), device_id_type=MESH)
        for a, (src, dst) in enumerate(zip(srcs, dsts)) for q in range(4)]


def _chip_copies(srcs, dsts, send_sems, recv_sems):
    x, y, c = lax.axis_index("x"), lax.axis_index("y"), lax.axis_index("c")
    targets = [(x, 1 - y, c), (1 - x, y, c), (1 - x, 1 - y, c)]
    return [pltpu.make_async_remote_copy(
        src_ref=src.at[k], dst_ref=dst.at[k],
        send_sem=send_sems.at[3 * a + k], recv_sem=recv_sems.at[3 * a + k],
        device_id=targets[k], device_id_type=MESH)
        for a, (src, dst) in enumerate(zip(srcs, dsts)) for k in range(3)]


def _sibling_half_copies(srcs, dsts, send_sems, recv_sems):
    x, y, c = lax.axis_index("x"), lax.axis_index("y"), lax.axis_index("c")
    return [pltpu.make_async_remote_copy(
        src_ref=src.at[q], dst_ref=dst.at[q],
        send_sem=send_sems.at[4 * a + q], recv_sem=recv_sems.at[4 * a + q],
        device_id=(x, y, 1 - c), device_id_type=MESH)
        for a, (src, dst) in enumerate(zip(srcs, dsts)) for q in range(4)]


def _all_copies(srcs, dsts, send_sems, recv_sems):
    x, y, c = lax.axis_index("x"), lax.axis_index("y"), lax.axis_index("c")
    mine = 4 * x + 2 * y + c
    copies = []
    for a, (src, dst) in enumerate(zip(srcs, dsts)):
        copies.append(pltpu.make_async_copy(src.at[0], dst.at[mine], send_sems.at[N_DEV * a]))
        for k in range(1, N_DEV):
            copies.append(pltpu.make_async_remote_copy(
                src_ref=src.at[0], dst_ref=dst.at[mine],
                send_sem=send_sems.at[N_DEV * a + k], recv_sem=recv_sems.at[N_DEV * a + k],
                device_id=(x ^ (k >> 2), y ^ ((k >> 1) & 1), c ^ (k & 1)), device_id_type=MESH))
    return copies


_EXCHANGES = {"sibling": (4, _sibling_copies, 4), "sibling_half": (4, _sibling_half_copies, 4),
              "chips": (3, _chip_copies, 3), "all": (N_DEV, _all_copies, N_DEV)}


def _exchange_shapes(kind, arrays):
    per_array, _, slots = _EXCHANGES[kind]
    out_shape = [jax.ShapeDtypeStruct((slots,) + a.shape[1:], a.dtype) for a in arrays]
    sems = [pltpu.SemaphoreType.DMA((per_array * len(arrays),))] * 2
    return out_shape, sems


def _ride_shapes(rides):
    shapes, sems = [], []
    for kind, arrays in rides:
        ride_shapes, ride_sems = _exchange_shapes(kind, arrays)
        shapes += ride_shapes
        sems += ride_sems
    return shapes, sems


def _riding(body, n_in, n_out, rides, is_first, is_last):
    counts = [len(arrays) for _, arrays in rides]
    n_arr = sum(counts)

    def wrapped(*refs):
        ins, srcs = refs[:n_in], refs[n_in:n_in + n_arr]
        outs = refs[n_in + n_arr:n_in + n_arr + n_out]
        dsts = refs[n_in + n_arr + n_out:n_in + 2 * n_arr + n_out]
        first_sem = len(refs) - 2 * len(rides)
        scratch, sems = refs[n_in + 2 * n_arr + n_out:first_sem], refs[first_sem:]

        def copies():
            made, at = [], 0
            for r, ((kind, _), n) in enumerate(zip(rides, counts)):
                made += _EXCHANGES[kind][1](srcs[at:at + n], dsts[at:at + n], sems[2 * r], sems[2 * r + 1])
                at += n
            return made

        @pl.when(is_first())
        def _():
            for cp in copies():
                cp.start()

        body(*ins, *outs, *scratch)

        @pl.when(is_last())
        def _():
            for cp in copies():
                cp.wait()

    return wrapped


def _chip_partial(pos, mine, theirs, relations, out_dtype, row_tile, name):
    n_slots, m, n = mine.shape
    q0 = relations[0]

    def chip_of(qi, pos_ref):
        q = qi + q0
        return pos_ref[0] ^ (q >> 1), pos_ref[1] ^ (q & 1)

    def mine_map(qi, t, pos_ref):
        px, py = chip_of(qi, pos_ref)
        return (4 * px + 2 * py + pos_ref[2] if n_slots == N_DEV else 2 * px + py), t, 0

    def theirs_map(qi, t, pos_ref):
        px, py = chip_of(qi, pos_ref)
        return 2 * px + py, t, 0

    def body(pos_ref, a_ref, b_ref, o_ref):
        o_ref[...] = (a_ref[...] + b_ref[...]).astype(out_dtype)

    return pl.pallas_call(
        body, name=name,
        out_shape=jax.ShapeDtypeStruct((len(relations), m, n), out_dtype),
        grid_spec=pltpu.PrefetchScalarGridSpec(
            num_scalar_prefetch=1, grid=(len(relations), m // row_tile),
            in_specs=[pl.BlockSpec((None, row_tile, n), mine_map), pl.BlockSpec((None, row_tile, n), theirs_map)],
            out_specs=pl.BlockSpec((None, row_tile, n), lambda qi, t, pos_ref: (qi, t, 0))),
        compiler_params=pltpu.CompilerParams(dimension_semantics=("arbitrary", "arbitrary")),
    )(pos, mine, theirs)


PARTS = ((0, 512), (512, 640))


def _gather_norm_proj(pos, x2d, small_shard, norm_g, w_in_shard, w_out_shards, n_chunk):
    seq = x2d.shape[0]
    n_tiles = seq // TILE + 1
    tp = n_tiles * TILE
    n_parts = len(PARTS)
    widest = max(width for _, width in PARTS)
    units = [(s, u) for s in range(2) for u in range(n_parts)]
    for first in (2, 5):
        units += [(first + j, u) for u in range(n_parts) for j in range(2)] + [(first + 2, u) for u in range(n_parts)]
    n_units = len(units)
    n_steps = n_tiles + n_units
    chunk = tp // n_chunk

    def body(pos_ref, x_ref, g_ref, small_ref, win_ref, wa_ref, wb_ref, wo_ref,
             ht_ref, proj_ref, meta_ref, small_all, win_all, wa_all, wb_all, wo_all,
             h_all, wbuf, rbuf, small_buf, send_sems, recv_sems, local_sems, small_send, small_recv):
        g = pl.program_id(0)
        x, y, c = lax.axis_index("x"), lax.axis_index("y"), lax.axis_index("c")
        me, sibling = (x, y, c), (x, y, 1 - c)
        chips = [(1 - x, y), (x, 1 - y), (1 - x, 1 - y)]
        shards = (win_ref, wa_ref, wb_ref, wo_ref)
        gathered = (win_all, wa_all, wb_all, wo_all)
        n_arrays = len(shards)
        blocks = [me, sibling] + [(*chip, c) for chip in chips] + [(*chip, 1 - c) for chip in chips]

        def index(block):
            px, py, pc = block
            return 4 * px + 2 * py + pc

        def part(ref, a, u):
            return ref.at[:, pl.ds(PARTS[u][0], PARTS[u][1])] if a == 0 else ref

        def slot(a, block, u):
            return part(gathered[a].at[index(block)], a, u)

        def sem(a, k, u):
            return n_parts * k + u if a == 0 else 7 * n_parts + 7 * (a - 1) + k

        def copy(a, k, block, to, u=0, from_shard=False):
            return pltpu.make_async_remote_copy(
                src_ref=part(shards[a], a, u) if from_shard else slot(a, block, u), dst_ref=slot(a, block, u),
                send_sem=send_sems.at[sem(a, k, u)], recv_sem=recv_sems.at[sem(a, k, u)],
                device_id=to, device_id_type=MESH)

        def keep(a):
            return pltpu.make_async_copy(shards[a], gathered[a].at[index(me)], local_sems.at[a])

        def load(m):
            s, u = units[m]
            src = part(win_ref, 0, u) if s == 0 else slot(0, blocks[s], u)
            return pltpu.make_async_copy(src, wbuf.at[m % 2, :, 0:PARTS[u][1]], local_sems.at[n_arrays + m % 2])

        def store(m):
            s, u = units[m]
            col0 = pl.multiple_of(index(blocks[s]) * COLS + PARTS[u][0], LANES)
            return pltpu.make_async_copy(rbuf.at[m % 2, :, 0:PARTS[u][1]],
                                         proj_ref.at[:, pl.ds(col0, PARTS[u][1])], local_sems.at[n_arrays + 2 + m % 2])

        def by_x(a, u):
            return u == 0 if a == 0 else a < 3

        def relay(a, u=0):
            src, to = (blocks[3], blocks[2]) if by_x(a, u) else (blocks[2], blocks[3])
            return copy(a, 3, src, to, u)

        def arrive(m):
            s, u = units[m]
            if s == 1:
                copy(0, 0, sibling, me, u).wait_recv()
            elif 2 <= s <= 4:
                copy(0, s - 1, blocks[s], me, u).wait_recv()
                copy(0, s + 2, blocks[s], sibling, u).start()
                if s < 4 and by_x(0, u) == (s == 3):
                    relay(0, u).start()
            elif s >= 5:
                copy(0, s - 1, blocks[s], me, u).wait_recv()
                if u == 0:
                    for a in range(1, 4):
                        copy(a, s - 4, blocks[s - 3], me).wait_recv()
                        copy(a, s - 1, blocks[s - 3], sibling).start()
                        if s < 7 and by_x(a, 0) == (s == 6):
                            relay(a).start()

        targets = [sibling, blocks[2], blocks[3]]

        def small_copies():
            return _all_copies([small_ref], [small_all], small_send, small_recv)

        @pl.when(g == 0)
        def _():
            for cp in small_copies():
                cp.start()
            for a in range(n_arrays):
                keep(a).start()
            for u in range(n_parts):
                for k, to in enumerate(targets):
                    copy(0, k, me, to, u, from_shard=True).start()
            for a in range(1, 4):
                for k, to in enumerate(targets):
                    copy(a, k, me, to, from_shard=True).start()
            load(0).start()

        @pl.when(g == n_tiles - 2)
        def _():
            for cp in small_copies():
                cp.wait()
            fetch = pltpu.make_async_copy(small_all, small_buf, local_sems.at[n_arrays + 4])
            fetch.start()
            fetch.wait()
            meta_ref[0:TILE - N_META, :] = jnp.zeros((TILE - N_META, D_MODEL), F32)
            meta_ref[TILE - N_META:TILE, :] = jnp.concatenate([small_buf[d, 0:N_META, :] for d in range(N_DEV)], axis=1)

        @pl.when(g < n_tiles)
        def _():
            s0 = jnp.where(g == n_tiles - 1, meta_ref[...], x_ref[...])
            r = lax.rsqrt(_rowmean(s0 * s0) + EPS)
            h32 = (s0 * r) * g_ref[...]
            ht_ref[...] = h32.T.astype(BF16)
            h_all[pl.ds(pl.multiple_of(g * TILE, TILE), TILE), :] = h32.astype(BF16)

        for m in range(n_units):
            @pl.when(g == n_tiles + m)
            def _(m=m):
                load(m).wait()
                if m + 1 < n_units:
                    arrive(m + 1)
                    load(m + 1).start()
                if m >= 2:
                    store(m - 2).wait()

        m_now = jnp.maximum(g - n_tiles, 0)
        u_now = functools.reduce(jnp.add, [jnp.where(m_now == m, u, 0) for m, (_, u) in enumerate(units)])
        for u, (_, width) in enumerate(PARTS):
            @pl.when((g >= n_tiles) & (u_now == u))
            def _(width=width):
                w = wbuf[m_now % 2, :, 0:width]
                for r in range(n_chunk):
                    rbuf[m_now % 2, r * chunk:(r + 1) * chunk, 0:width] = _dot(h_all[r * chunk:(r + 1) * chunk, :], w)

        for m in range(n_units):
            @pl.when(g == n_tiles + m)
            def _(m=m):
                store(m).start()

        @pl.when(g == n_steps - 1)
        def _():
            store(n_units - 2).wait()
            store(n_units - 1).wait()
            for a in range(1, 4):
                copy(a, 0, sibling, me).wait_recv()
                for j in range(3):
                    copy(a, 4 + j, blocks[5 + j], me).wait_recv()
            for a in range(n_arrays):
                for u in range(n_parts if a == 0 else 1):
                    for k, to in enumerate(targets):
                        copy(a, k, me, to, u, from_shard=True).wait_send()
                    relay(a, u).wait_send()
                    for j in range(3):
                        copy(a, 4 + j, blocks[2 + j], sibling, u).wait_send()
                keep(a).wait()

    n_x = n_tiles - 1
    return pl.pallas_call(
        body, name="gather_norm_proj",
        out_shape=[jax.ShapeDtypeStruct((D_MODEL, tp), BF16), jax.ShapeDtypeStruct((tp, D_IN), F32),
                   jax.ShapeDtypeStruct((TILE, D_MODEL), F32), jax.ShapeDtypeStruct((N_DEV,) + small_shard.shape[1:], F32),
                   jax.ShapeDtypeStruct((N_DEV,) + w_in_shard.shape, BF16)]
                  + [jax.ShapeDtypeStruct((N_DEV,) + w.shape, BF16) for w in w_out_shards],
        grid_spec=pltpu.PrefetchScalarGridSpec(
            num_scalar_prefetch=1, grid=(n_steps,),
            in_specs=[pl.BlockSpec((TILE, D_MODEL), lambda g, pos_ref: (jnp.minimum(g, n_x - 1), 0)),
                      _VMEM, _ANY, _ANY, _ANY, _ANY, _ANY],
            out_specs=[pl.BlockSpec((D_MODEL, TILE), lambda g, pos_ref: (0, jnp.minimum(g, n_tiles - 1))),
                       _ANY, _VMEM, _ANY, _ANY, _ANY, _ANY, _ANY],
            scratch_shapes=[pltpu.VMEM((tp, D_MODEL), BF16), pltpu.VMEM((2, D_MODEL, widest), BF16),
                            pltpu.VMEM((2, tp, widest), F32), pltpu.VMEM((N_DEV,) + small_shard.shape[1:], F32),
                            pltpu.SemaphoreType.DMA((7 * n_parts + 21,)), pltpu.SemaphoreType.DMA((7 * n_parts + 21,)),
                            pltpu.SemaphoreType.DMA((9,)),
                            pltpu.SemaphoreType.DMA((N_DEV,)), pltpu.SemaphoreType.DMA((N_DEV,))]),
        compiler_params=pltpu.CompilerParams(dimension_semantics=("arbitrary",), vmem_limit_bytes=VMEM_LIMIT),
    )(pos, x2d, norm_g, small_shard, w_in_shard, *w_out_shards)


C_AVAL, C_AGLU, C_AZ, C_BB, C_BC, C_BX, C_BZ, C_GA, C_GB = (k * D_MODEL for k in range(9))
S_AZ, S_BB, S_BZ, S_GA, S_GB = (k * D_MODEL for k in range(5))


def _fused_pass(proj, x2d, tgt2d, meta_tile, conv_a_w, conv_a_b, ln_a_g, ln_a_b, b_a_out, conv_b_w, final_g,
                w_a, w_b, w_o, w_a_t, w_b_t, w_o_t, n_tiles):
    T = TILE
    tp = n_tiles * T
    inv_d = 1.0 / D_MODEL

    def block_of(tile):
        return jnp.where(tile == 0, n_tiles - 1, tile - 1)

    def cur(i):
        return block_of(jnp.minimum(i, n_tiles - 1))

    def prev(i):
        return block_of(jnp.clip(i - 1, 0, n_tiles - 1))

    def xblk(i):
        return jnp.maximum(jnp.minimum(i, n_tiles - 1) - 1, 0)

    def body(proj_ref, x_ref, tgt_ref, meta_ref, caw_ref, cab_ref, lng_ref, lnb_ref, bao_ref, cbw_ref, fg_ref,
             wa_ref, wb_ref, wo_ref, wat_ref, wbt_ref, wot_ref,
             dproj_ref, ds1_ref, lhs_ref, rhs_ref, small_ref,
             ua0_buf, cb_buf, dua1_buf, dc3_buf, aprev, cprev, stage, ua1_buf, c3_buf,
             dpa_buf, dpb_buf, dcaw8, dcbw8, shift_buf):
        i = pl.program_id(0)

        @pl.when(i == 0)
        def _init():
            for buf in (ua0_buf, cb_buf, dua1_buf, dc3_buf, aprev, cprev, dcaw8, dcbw8):
                buf[...] = jnp.zeros(buf.shape, buf.dtype)
            small_ref[...] = jnp.zeros(small_ref.shape, F32)

        @pl.when(i >= 1)
        def _emit_stage():
            dproj_ref[:, C_AZ:C_BC] = stage[:, S_AZ:S_BZ]
            dproj_ref[:, C_BZ:D_IN] = stage[:, S_BZ:S_GB + D_MODEL]

        @pl.when(i < n_tiles)
        def _front():
            def conv_chunk(cc, carry):
                c0 = pl.multiple_of(cc * LANES, LANES)
                lanes = pl.ds(c0, LANES)

                def col(base):
                    return pl.ds(pl.multiple_of(base + cc * LANES, LANES), LANES)

                ua0 = proj_ref[:, col(C_AVAL)] * _sigmoid(proj_ref[:, col(C_AGLU)])
                ua0_buf[T:2 * T, lanes] = ua0
                acc = jnp.broadcast_to(cab_ref[:, lanes], (T, LANES))
                lead = HALO - (CONV_A - 1)
                for r in range(SUBLANES):
                    taps = [k for k in range(CONV_A) if (k + lead) % SUBLANES == r]
                    rows = T + SUBLANES * max((k + lead) // SUBLANES for k in taps)
                    if r:
                        shift_buf[r, 0:rows, :] = ua0_buf[pl.ds(T - HALO + r, rows), lanes]
                    for k in taps:
                        q = (k + lead) // SUBLANES
                        if r:
                            win = shift_buf[r, SUBLANES * q:SUBLANES * q + T, :]
                        else:
                            win = ua0_buf[pl.ds(T - HALO + SUBLANES * q, T), lanes]
                        acc = acc + caw_ref[k:k + 1, lanes] * win
                ua1_buf[:, lanes] = acc
                cb = proj_ref[:, col(C_BC)] * proj_ref[:, col(C_BX)]
                cb_buf[T:2 * T, lanes] = cb
                acc3 = cbw_ref[0:1, lanes] * cb_buf[pl.ds(T - 2, T), lanes]
                for k in range(1, CONV_B):
                    acc3 = acc3 + cbw_ref[k:k + 1, lanes] * cb_buf[pl.ds(T - (CONV_B - 1) + k, T), lanes]
                c3_buf[:, lanes] = acc3
                return carry

            lax.fori_loop(0, N_CHUNK, conv_chunk, 0)

            ua1 = ua1_buf[...]
            xc = ua1 - _rowmean(ua1)
            rstd = lax.rsqrt(_rowmean(xc * xc) + EPS)
            xhat = xc * rstd
            ua2 = xhat * lng_ref[...] + lnb_ref[...]
            sg2 = _sigmoid(ua2)
            ua3 = ua2 * sg2
            a_z = proj_ref[:, C_AZ:C_AZ + D_MODEL]
            sz = _sigmoid(a_z)
            silu_az = a_z * sz
            lhs_ref[0] = (ua3 * silu_az).astype(BF16)
            b_z = proj_ref[:, C_BZ:C_BZ + D_MODEL]
            sbz = _sigmoid(b_z)
            silu_bz = b_z * sbz
            b_b = proj_ref[:, C_BB:C_BB + D_MODEL]
            c3 = c3_buf[...]
            ub = b_b * c3
            lhs_ref[1] = (ub * silu_bz).astype(BF16)

            ya = _dot(lhs_ref[0], wa_ref[...]) + bao_ref[...]
            yb = _dot(lhs_ref[1], wb_ref[...])
            sga = _sigmoid(proj_ref[:, C_GA:C_GA + D_MODEL])
            sgb = _sigmoid(proj_ref[:, C_GB:C_GB + D_MODEL])
            m_b = (sga * ya + sgb * yb).astype(BF16)
            lhs_ref[2] = m_b
            s0 = jnp.where(i == 0, meta_ref[...], x_ref[...])
            s1 = s0 + _dot(m_b, wo_ref[...])
            r1 = lax.rsqrt(_rowmean(s1 * s1) + EPS)
            y = (s1 * r1) * fg_ref[...]
            is_token = (i >= 1).astype(F32)
            err = (y - tgt_ref[...]) * is_token
            small_ref[ROW_LOSS:ROW_LOSS + 1, :] += (0.5 * inv_d) * _colsum(err * err)
            dy = err * inv_d
            small_ref[ROW_FINAL_G:ROW_FINAL_G + 1, :] += _colsum(dy * (s1 * r1))
            gy = dy * fg_ref[...]
            ds1 = r1 * gy - s1 * ((r1 * r1 * r1) * _rowmean(gy * s1))
            ds1_ref[...] = ds1
            ds1_b = ds1.astype(BF16)
            rhs_ref[2] = ds1_b
            dm = _dot(ds1_b, wot_ref[...])
            dya = dm * sga
            dyb = dm * sgb
            stage[:, S_GA:S_GA + D_MODEL] = (dya * ya * (1.0 - sga)).astype(BF16)
            stage[:, S_GB:S_GB + D_MODEL] = (dyb * yb * (1.0 - sgb)).astype(BF16)
            small_ref[ROW_B_A_OUT:ROW_B_A_OUT + 1, :] += _colsum(dya)
            dya_b = dya.astype(BF16)
            dyb_b = dyb.astype(BF16)
            rhs_ref[0] = dya_b
            rhs_ref[1] = dyb_b
            dpa_buf[...] = _dot(dya_b, wat_ref[...])
            dpb_buf[...] = _dot(dyb_b, wbt_ref[...])

            dpa = dpa_buf[...]
            stage[:, S_AZ:S_AZ + D_MODEL] = (dpa * ua3 * (sz + silu_az * (1.0 - sz))).astype(BF16)
            dua2 = dpa * silu_az * (sg2 + ua3 * (1.0 - sg2))
            small_ref[ROW_LN_G:ROW_LN_G + 1, :] += _colsum(dua2 * xhat)
            small_ref[ROW_LN_B:ROW_LN_B + 1, :] += _colsum(dua2)
            dxh = dua2 * lng_ref[...]
            dua1 = rstd * (dxh - _rowmean(dxh) - xhat * _rowmean(dxh * xhat))
            small_ref[ROW_CONV_A_B:ROW_CONV_A_B + 1, :] += _colsum(dua1)
            dua1_buf[T:2 * T, :] = dua1
            dpb = dpb_buf[...]
            stage[:, S_BZ:S_BZ + D_MODEL] = (dpb * ub * (sbz + silu_bz * (1.0 - sbz))).astype(BF16)
            dub = dpb * silu_bz
            stage[:, S_BB:S_BB + D_MODEL] = (dub * c3).astype(BF16)
            dc3_buf[T:2 * T, :] = dub * b_b

        @pl.when(i == n_tiles)
        def _no_later_tile():
            dua1_buf[T:2 * T, :] = jnp.zeros((T, D_MODEL), F32)
            dc3_buf[T:2 * T, :] = jnp.zeros((T, D_MODEL), F32)

        @pl.when(i >= 1)
        def _lagged():
            def convt_chunk(cc, carry):
                c0 = pl.multiple_of(cc * LANES, LANES)
                lanes = pl.ds(c0, LANES)

                def col(base):
                    return pl.ds(pl.multiple_of(base + cc * LANES, LANES), LANES)

                ua0 = ua0_buf[0:T, lanes]
                acc = jnp.zeros((T, LANES), F32)
                for r in range(SUBLANES):
                    shifts = [j for j in range(CONV_A) if j % SUBLANES == r]
                    rows = T + shifts[-1] - r
                    if r:
                        shift_buf[r, 0:rows, :] = dua1_buf[pl.ds(r, rows), lanes]
                    for j in shifts:
                        k = CONV_A - 1 - j
                        if r:
                            later = shift_buf[r, j - r:j - r + T, :]
                        else:
                            later = dua1_buf[pl.ds(j, T), lanes]
                        acc = acc + caw_ref[k:k + 1, lanes] * later
                        dcaw8[SUBLANES * k:SUBLANES * (k + 1), lanes] += _fold8(ua0 * later)
                a_val = aprev[:, col(0)]
                sg = _sigmoid(aprev[:, col(D_MODEL)])
                dproj_ref[:, col(C_AVAL)] = (acc * sg).astype(BF16)
                dproj_ref[:, col(C_AGLU)] = (acc * a_val * (sg * (1.0 - sg))).astype(BF16)

                cb = cb_buf[0:T, lanes]
                acc3 = jnp.zeros((T, LANES), F32)
                for j in range(CONV_B):
                    k = CONV_B - 1 - j
                    later = dc3_buf[pl.ds(j, T), lanes]
                    acc3 = acc3 + cbw_ref[k:k + 1, lanes] * later
                    dcbw8[SUBLANES * k:SUBLANES * (k + 1), lanes] += _fold8(cb * later)
                dproj_ref[:, col(C_BC)] = (acc3 * cprev[:, col(D_MODEL)]).astype(BF16)
                dproj_ref[:, col(C_BX)] = (acc3 * cprev[:, col(0)]).astype(BF16)
                return carry

            lax.fori_loop(0, N_CHUNK, convt_chunk, 0)

        for buf in (ua0_buf, cb_buf, dua1_buf, dc3_buf):
            buf[0:T, :] = buf[T:2 * T, :]
        aprev[...] = proj_ref[:, C_AVAL:C_AZ]
        cprev[...] = proj_ref[:, C_BC:C_BZ]

        @pl.when(i == n_tiles)
        def _finish():
            for k in range(CONV_A):
                small_ref[ROW_CONV_A_W + k:ROW_CONV_A_W + k + 1, :] = _colsum(dcaw8[SUBLANES * k:SUBLANES * (k + 1), :])
            for k in range(CONV_B):
                small_ref[ROW_CONV_B_W + k:ROW_CONV_B_W + k + 1, :] = _colsum(dcbw8[SUBLANES * k:SUBLANES * (k + 1), :])

    tile_in = lambda width: pl.BlockSpec((T, width), lambda i: (cur(i), 0))
    return pl.pallas_call(
        body, name="fused_pass", grid=(n_tiles + 1,),
        out_shape=[
            jax.ShapeDtypeStruct((tp, D_IN), BF16),
            jax.ShapeDtypeStruct((tp, D_MODEL), F32),
            jax.ShapeDtypeStruct((3, tp, D_MODEL), BF16),
            jax.ShapeDtypeStruct((3, tp, D_MODEL), BF16),
            jax.ShapeDtypeStruct((SMALL_A_ROWS, D_MODEL), F32),
        ],
        in_specs=[
            tile_in(D_IN),
            pl.BlockSpec((T, D_MODEL), lambda i: (xblk(i), 0)),
            pl.BlockSpec((T, D_MODEL), lambda i: (xblk(i), 0)),
            _VMEM, _VMEM, _VMEM, _VMEM, _VMEM, _VMEM, _VMEM, _VMEM,
            *[_resident((D_MODEL, D_MODEL)) for _ in range(6)],
        ],
        out_specs=[
            pl.BlockSpec((T, D_IN), lambda i: (prev(i), 0)),
            pl.BlockSpec((T, D_MODEL), lambda i: (cur(i), 0)),
            pl.BlockSpec((3, T, D_MODEL), lambda i: (0, cur(i), 0)),
            pl.BlockSpec((3, T, D_MODEL), lambda i: (0, cur(i), 0)),
            _VMEM,
        ],
        scratch_shapes=[
            pltpu.VMEM((2 * T, D_MODEL), F32),
            pltpu.VMEM((2 * T, D_MODEL), F32),
            pltpu.VMEM((2 * T, D_MODEL), F32),
            pltpu.VMEM((2 * T, D_MODEL), F32),
            pltpu.VMEM((T, 2 * D_MODEL), F32),
            pltpu.VMEM((T, 2 * D_MODEL), F32),
            pltpu.VMEM((T, 5 * D_MODEL), BF16),
            pltpu.VMEM((T, D_MODEL), F32),
            pltpu.VMEM((T, D_MODEL), F32),
            pltpu.VMEM((T, D_MODEL), F32),
            pltpu.VMEM((T, D_MODEL), F32),
            pltpu.VMEM((32 * SUBLANES, D_MODEL), F32),
            pltpu.VMEM((SUBLANES * SUBLANES, D_MODEL), F32),
            pltpu.VMEM((SUBLANES, T + HALO, LANES), F32),
        ],
        compiler_params=pltpu.CompilerParams(dimension_semantics=("arbitrary",), vmem_limit_bytes=VMEM_LIMIT),
    )(proj, x2d, tgt2d, meta_tile, conv_a_w, conv_a_b, ln_a_g, ln_a_b, b_a_out, conv_b_w, final_g,
      w_a, w_b, w_o, w_a_t, w_b_t, w_o_t)


def _input_bwd(dproj, ds1, x2d, meta_tile, norm_g, w_in_all, row_tile, ride):
    seq = x2d.shape[0]
    n_steps = seq // row_tile
    meta_block = seq // TILE

    def backward(dp_ref, ds1_ref, s0_ref, g_ref, w_ref, out_ref, vec_ref):
        dh = _dot_nt(dp_ref[:, 0:COLS], w_ref[0])
        for j in range(1, N_DEV):
            dh = dh + _dot_nt(dp_ref[:, j * COLS:(j + 1) * COLS], w_ref[j])
        s0v = s0_ref[...]
        r = lax.rsqrt(_rowmean(s0v * s0v) + EPS)
        gh = dh * g_ref[...]
        out_ref[...] = ds1_ref[...] + r * gh - s0v * ((r * r * r) * _rowmean(gh * s0v))
        vec_ref[ROW_NORM_G:ROW_NORM_G + 1, :] += _colsum(dh * (s0v * r))

    def body(dp_ref, ds1_ref, x_ref, dpm_ref, ds1m_ref, meta_ref, g_ref, w_ref, gx_ref, small_all_ref,
             gmeta_buf, small_buf, send_sems, recv_sems):
        t = pl.program_id(0)
        small = small_buf.at[0]

        @pl.when(t == 0)
        def _():
            small[...] = jnp.zeros(small.shape, F32)

        backward(dp_ref, ds1_ref, x_ref, g_ref, w_ref, gx_ref, small)

        @pl.when(t == n_steps - 1)
        def _():
            backward(dpm_ref, ds1m_ref, meta_ref, g_ref, w_ref, gmeta_buf, small)
            small[ROW_META:ROW_META + N_META, :] = gmeta_buf[TILE - N_META:TILE, :]
            copies = _all_copies([small_buf], [small_all_ref], send_sems, recv_sems)
            for cp in copies:
                cp.start()
            for cp in copies:
                cp.wait()

    rides = [("chips", ride)]
    ride_shapes, ride_sems = _ride_shapes(rides)
    body = _riding(body, 8, 2, rides, lambda: pl.program_id(0) == 0, lambda: pl.program_id(0) == n_steps - 1)
    return pl.pallas_call(
        body, name="input_bwd", grid=(n_steps,),
        out_shape=[jax.ShapeDtypeStruct(x2d.shape, F32),
                   jax.ShapeDtypeStruct((N_DEV, SMALL_B_ROWS, D_MODEL), F32)] + ride_shapes,
        in_specs=[pl.BlockSpec((row_tile, D_IN), lambda t: (t, 0)),
                  pl.BlockSpec((row_tile, D_MODEL), lambda t: (t, 0)),
                  pl.BlockSpec((row_tile, D_MODEL), lambda t: (t, 0)),
                  pl.BlockSpec((TILE, D_IN), lambda t: (meta_block, 0)),
                  pl.BlockSpec((TILE, D_MODEL), lambda t: (meta_block, 0)),
                  _VMEM, _VMEM, _resident((N_DEV, D_MODEL, COLS))] + [_ANY] * len(ride),
        out_specs=[pl.BlockSpec((row_tile, D_MODEL), lambda t: (t, 0)), _ANY] + [_ANY] * len(ride),
        scratch_shapes=[pltpu.VMEM((TILE, D_MODEL), F32), pltpu.VMEM((1, SMALL_B_ROWS, D_MODEL), F32),
                        pltpu.SemaphoreType.DMA((N_DEV,)), pltpu.SemaphoreType.DMA((N_DEV,))] + ride_sems,
        compiler_params=pltpu.CompilerParams(dimension_semantics=("arbitrary",), vmem_limit_bytes=VMEM_LIMIT),
    )(dproj, ds1, x2d, dproj, ds1, meta_tile, norm_g, w_in_all, *ride)


def _grad_w_in_half(pos, h_t, dproj, k_tile, other_side, rides, name):
    tp = h_t.shape[1]
    n_k = tp // k_tile

    def column_block(q, k, pos_ref):
        return k, 2 * q + (1 - pos_ref[2] if other_side else pos_ref[2])

    def body(pos_ref, h_ref, dp_ref, o_ref):
        @pl.when(pl.program_id(1) == 0)
        def _():
            o_ref[...] = jnp.zeros(o_ref.shape, F32)

        o_ref[...] += _dot(h_ref[...], dp_ref[...])

    ride = [a for _, arrays in rides for a in arrays]
    n_arr = len(ride)
    ride_shapes, ride_sems = _ride_shapes(rides)
    body = _riding(body, 3, 1, rides,
                   lambda: (pl.program_id(0) == 0) & (pl.program_id(1) == 0),
                   lambda: (pl.program_id(0) == 3) & (pl.program_id(1) == n_k - 1))
    return pl.pallas_call(
        body, name=name,
        out_shape=[jax.ShapeDtypeStruct((4, D_MODEL, COLS), F32)] + ride_shapes,
        grid_spec=pltpu.PrefetchScalarGridSpec(
            num_scalar_prefetch=1, grid=(4, n_k),
            in_specs=[pl.BlockSpec((D_MODEL, k_tile), lambda q, k, pos_ref: (0, k)),
                      pl.BlockSpec((k_tile, COLS), column_block)] + [_ANY] * n_arr,
            out_specs=[pl.BlockSpec((None, D_MODEL, COLS), lambda q, k, pos_ref: (q, 0, 0))] + [_ANY] * n_arr,
            scratch_shapes=ride_sems),
        compiler_params=pltpu.CompilerParams(dimension_semantics=("arbitrary", "arbitrary"),
                                             vmem_limit_bytes=VMEM_LIMIT),
    )(pos, h_t, dproj, *ride)


def _grad_w_out(lhs, rhs, k_tile):
    tp = lhs.shape[1]

    def body(a_ref, b_ref, o_ref):
        @pl.when(pl.program_id(1) == 0)
        def _():
            o_ref[...] = jnp.zeros(o_ref.shape, F32)

        o_ref[...] += _dot_tn(a_ref[...], b_ref[...]).reshape(N_DEV, ROWS_OUT, D_MODEL)

    return pl.pallas_call(
        body, name="grad_w_out", grid=(3, tp // k_tile),
        out_shape=jax.ShapeDtypeStruct((N_DEV, 3, ROWS_OUT, D_MODEL), F32),
        in_specs=[pl.BlockSpec((None, k_tile, D_MODEL), lambda w, k: (w, k, 0)),
                  pl.BlockSpec((None, k_tile, D_MODEL), lambda w, k: (w, k, 0))],
        out_specs=pl.BlockSpec((N_DEV, None, ROWS_OUT, D_MODEL), lambda w, k: (0, w, 0, 0)),
        compiler_params=pltpu.CompilerParams(dimension_semantics=("arbitrary", "arbitrary"),
                                             vmem_limit_bytes=VMEM_LIMIT),
    )(lhs, rhs)


def _adamw_math(w, g, m, v):
    m = ADAM_B1 * m + (1.0 - ADAM_B1) * g
    v = ADAM_B2 * v + (1.0 - ADAM_B2) * (g * g)
    m_hat = m / (1.0 - ADAM_B1 ** ADAM_STEP)
    v_hat = v / (1.0 - ADAM_B2 ** ADAM_STEP)
    delta = -ADAM_LR * (m_hat / (jnp.sqrt(v_hat) + ADAM_EPS) + ADAM_WD * w)
    return delta, m, v


def _adamw_sharded(pos, mine, theirs, landed, weights, row_tile, name):
    rows, n = weights[0][0].shape
    n_slots = mine.shape[0]
    per_shard = rows // row_tile
    assert per_shard == 1 or len(weights) == 1

    def mine_map(j, t, pos_ref):
        chip = 2 * pos_ref[0] + pos_ref[1]
        return (2 * chip + pos_ref[2] if n_slots == N_DEV else chip), j * per_shard + t, 0

    def theirs_map(j, t, pos_ref):
        return 2 * pos_ref[0] + pos_ref[1], j * per_shard + t, 0

    def body(pos_ref, mine_ref, theirs_ref, land_ref, *refs):
        ins, outs = refs[:3 * len(weights)], refs[3 * len(weights):]
        g = mine_ref[...] + theirs_ref[...]
        for k in range(3):
            g = g + land_ref[k].astype(F32)
        for j in range(len(weights)):
            @pl.when(pl.program_id(0) == j)
            def _(j=j):
                w_ref, m_ref, v_ref = ins[3 * j:3 * j + 3]
                delta, m_new, v_new = _adamw_math(w_ref[...], g, m_ref[...], v_ref[...])
                for ref, val in zip(outs[4 * j:4 * j + 4], (g, delta, m_new, v_new)):
                    ref[...] = val

    tile = pl.BlockSpec((row_tile, n), lambda j, t, pos_ref: (t, 0))
    res = pl.pallas_call(
        body, name=name,
        out_shape=[jax.ShapeDtypeStruct((rows, n), F32)] * (4 * len(weights)),
        grid_spec=pltpu.PrefetchScalarGridSpec(
            num_scalar_prefetch=1, grid=(len(weights), per_shard),
            in_specs=[pl.BlockSpec((None, row_tile, n), mine_map), pl.BlockSpec((None, row_tile, n), theirs_map),
                      pl.BlockSpec((3, row_tile, n), lambda j, t, pos_ref: (0, j * per_shard + t, 0))]
            + [tile] * (3 * len(weights)),
            out_specs=[tile] * (4 * len(weights))),
        compiler_params=pltpu.CompilerParams(dimension_semantics=("arbitrary", "arbitrary")),
    )(pos, mine, theirs, landed, *[a for wmv in weights for a in wmv])
    return [res[4 * j:4 * j + 4] for j in range(len(weights))]


def _adamw_small(gathered, gathered_cols, params):
    n_par, n_src = len(params), len(gathered)

    def body(*refs):
        g_refs, gc_refs = refs[:n_src], refs[n_src:2 * n_src]
        ins = refs[2 * n_src:2 * n_src + 3 * n_par]
        outs = refs[2 * n_src + 3 * n_par:]
        loss_ref = outs[4 * n_par]

        def reduced(ref, row, n_rows):
            g = ref[0, row:row + n_rows, :]
            for d in range(1, N_DEV):
                g = g + ref[d, row:row + n_rows, :]
            return g

        for p, (src, row, n_rows, sharded, _, _, _) in enumerate(params):
            g = reduced((gc_refs if sharded else g_refs)[src], row, n_rows)
            w_ref, m_ref, v_ref = ins[3 * p:3 * p + 3]
            delta, m_new, v_new = _adamw_math(w_ref[...], g, m_ref[...], v_ref[...])
            outs[4 * p][...] = g
            outs[4 * p + 1][...] = delta
            outs[4 * p + 2][...] = m_new
            outs[4 * p + 3][...] = v_new
        loss = jnp.sum(reduced(g_refs[0], ROW_LOSS, 1), axis=1, keepdims=True)
        loss_ref[...] = jnp.broadcast_to(loss, loss_ref.shape)

    out_shape = []
    for (_, _, _, _, w, _, _) in params:
        out_shape += [jax.ShapeDtypeStruct(w.shape, F32)] * 4
    out_shape.append(jax.ShapeDtypeStruct((1, LANES), F32))
    flat = [a for (_, _, _, _, w, m, v) in params for a in (w, m, v)]
    return pl.pallas_call(
        body, name="adamw_small", out_shape=out_shape,
        in_specs=[_VMEM] * (2 * n_src + len(flat)), out_specs=[_VMEM] * len(out_shape),
    )(*gathered, *gathered_cols, *flat)


def _pad_rows(a, rows):
    return jnp.concatenate([a, jnp.zeros((rows - a.shape[0], a.shape[1]), a.dtype)], axis=0)


def kernel(x, meta_tokens, norm_g, w_in, conv_a_w, conv_a_b, ln_a_g, ln_a_b, w_a_out, b_a_out, conv_b_w, w_b_out, w_out, final_g, loss_target, m_meta_tokens, m_norm_g, m_w_in, m_conv_a_w, m_conv_a_b, m_ln_a_g, m_ln_a_b, m_w_a_out, m_b_a_out, m_conv_b_w, m_w_b_out, m_w_out, m_final_g, v_meta_tokens, v_norm_g, v_w_in, v_conv_a_w, v_conv_a_b, v_ln_a_g, v_ln_a_b, v_w_a_out, v_b_a_out, v_conv_b_w, v_w_b_out, v_w_out, v_final_g):
    seq = x.shape[1]
    assert x.shape == (1, seq, D_MODEL) and seq % TILE == 0 and w_in.shape == (1, D_MODEL, COLS)
    n_tiles = seq // TILE + 1
    tp = n_tiles * TILE
    pos = jnp.stack([lax.axis_index("x"), lax.axis_index("y"), lax.axis_index("c")]).astype(jnp.int32)
    me = 4 * pos[0] + 2 * pos[1] + pos[2]
    x2d = x[0]
    tgt2d = loss_target[0]

    small = jnp.concatenate([meta_tokens, _pad_rows(conv_a_w[0], 32), _pad_rows(conv_b_w[0], SUBLANES)], axis=0)
    final_g2 = final_g.reshape(1, D_MODEL)

    w_out_shards = [w[0].astype(BF16) for w in (w_a_out, w_b_out, w_out)]
    h_t, proj, meta_tile, small_params, w_in_all, *w_out_all = _gather_norm_proj(
        pos, x2d, small[None], norm_g, w_in[0].astype(BF16), w_out_shards, 3)
    small_params = small_params.transpose(1, 0, 2).reshape(small.shape[0], D_MODEL)
    conv_a_full, conv_b_full = small_params[N_META:N_META + 32], small_params[N_META + 32:]
    w_out_all = [w.reshape(D_MODEL, D_MODEL) for w in w_out_all]
    w_out_all_t = [w.T for w in w_out_all]
    dproj, ds1, lhs, rhs, small_a = _fused_pass(
        proj, x2d, tgt2d, meta_tile, conv_a_full, conv_a_b, ln_a_g, ln_a_b, b_a_out, conv_b_full, final_g2,
        w_out_all[0], w_out_all[1], w_out_all[2], w_out_all_t[0], w_out_all_t[1], w_out_all_t[2], n_tiles)
    k_tile = tp // 3
    gw_out = _grad_w_out(lhs, rhs, k_tile).reshape(N_DEV, 3 * ROWS_OUT, D_MODEL)
    gw_far, their_out, small_a_all = _grad_w_in_half(
        pos, h_t, dproj, k_tile, True, [("sibling", (gw_out,)), ("all", (small_a[None],))], "grad_w_in_far")
    parts_out = _chip_partial(pos, gw_out, their_out, (1, 2, 3), BF16, ROWS_OUT, "rs_parts_w_out")
    gw_near, their_in, land_out = _grad_w_in_half(
        pos, h_t, dproj, k_tile, False, [("sibling_half", (gw_far,)), ("chips", (parts_out,))], "grad_w_in_near")
    parts_in = _chip_partial(pos, gw_near, their_in, (1, 2, 3), BF16, 256, "rs_parts_w_in")
    grad_x, small_b_all, land_in = _input_bwd(dproj, ds1, x2d, meta_tile, norm_g, w_in_all, min(256, seq), (parts_in,))

    small_grads = [small_a_all, small_b_all]
    small_cols = [lax.dynamic_slice_in_dim(g, me * LANES, LANES, axis=2) for g in small_grads]

    (res_in,) = _adamw_sharded(pos, gw_near, their_in, land_in, [(w_in[0], m_w_in[0], v_w_in[0])], 128, "adamw_w_in")
    res_out = _adamw_sharded(
        pos, gw_out, their_out, land_out,
        [(w_a_out[0], m_w_a_out[0], v_w_a_out[0]), (w_b_out[0], m_w_b_out[0], v_w_b_out[0]),
         (w_out[0], m_w_out[0], v_w_out[0])], ROWS_OUT, "adamw_w_out")
    params = [
        (1, ROW_META, N_META, True, meta_tokens, m_meta_tokens, v_meta_tokens),
        (1, ROW_NORM_G, 1, False, norm_g, m_norm_g, v_norm_g),
        (0, ROW_CONV_A_W, CONV_A, True, conv_a_w[0], m_conv_a_w[0], v_conv_a_w[0]),
        (0, ROW_CONV_A_B, 1, False, conv_a_b, m_conv_a_b, v_conv_a_b),
        (0, ROW_LN_G, 1, False, ln_a_g, m_ln_a_g, v_ln_a_g),
        (0, ROW_LN_B, 1, False, ln_a_b, m_ln_a_b, v_ln_a_b),
        (0, ROW_B_A_OUT, 1, False, b_a_out, m_b_a_out, v_b_a_out),
        (0, ROW_CONV_B_W, CONV_B, True, conv_b_w[0], m_conv_b_w[0], v_conv_b_w[0]),
        (0, ROW_FINAL_G, 1, False, final_g2, m_final_g.reshape(1, D_MODEL), v_final_g.reshape(1, D_MODEL)),
    ]
    res_small = _adamw_small(small_grads, small_cols, params)
    loss = res_small[-1][0, 0]

    def small_res(p, kind, shape):
        return res_small[4 * p + kind].reshape(shape)

    per_weight = []
    for kind in range(4):
        per_weight.append([
            small_res(0, kind, meta_tokens.shape),
            small_res(1, kind, norm_g.shape),
            res_in[kind].reshape(w_in.shape),
            small_res(2, kind, conv_a_w.shape),
            small_res(3, kind, conv_a_b.shape),
            small_res(4, kind, ln_a_g.shape),
            small_res(5, kind, ln_a_b.shape),
            res_out[0][kind].reshape(w_a_out.shape),
            small_res(6, kind, b_a_out.shape),
            small_res(7, kind, conv_b_w.shape),
            res_out[1][kind].reshape(w_b_out.shape),
            res_out[2][kind].reshape(w_out.shape),
            small_res(8, kind, final_g.shape),
        ])
    return (loss, grad_x.reshape(x.shape), *per_weight[0], *per_weight[1], *per_weight[2], *per_weight[3])
```

```python
import functools

import jax
import jax.numpy as jnp
from jax import lax
from jax.experimental import pallas as pl
from jax.experimental.pallas import tpu as pltpu

D_MODEL = 1024
N_META = 16
N_DEV = 8
D_IN = 9 * D_MODEL
COLS = D_IN // N_DEV
ROWS_OUT = D_MODEL // N_DEV
CONV_A = 31
CONV_B = 3
EPS = 1e-6

ADAM_LR = 0.001
ADAM_B1 = 0.9
ADAM_B2 = 0.999
ADAM_EPS = 1e-08
ADAM_WD = 0.01
ADAM_STEP = 10

TILE = 128
LANES = 128
N_CHUNK = D_MODEL // LANES
HALO = 32
SUBLANES = 8
VMEM_LIMIT = 56 * 1024 * 1024

ROW_FINAL_G, ROW_B_A_OUT, ROW_LN_G, ROW_LN_B, ROW_CONV_A_B, ROW_LOSS = 0, 1, 2, 3, 4, 5
ROW_CONV_A_W, ROW_CONV_B_W, SMALL_A_ROWS = 8, 40, 48
ROW_NORM_G, ROW_META, SMALL_B_ROWS = 0, 8, 24

MESH = pl.DeviceIdType.MESH
_ANY = pl.BlockSpec(memory_space=pl.ANY)
_VMEM = pl.BlockSpec(memory_space=pltpu.VMEM)
BF16 = jnp.bfloat16
F32 = jnp.float32


def _resident(shape):
    return pl.BlockSpec(shape, lambda *_: (0,) * len(shape), pipeline_mode=pl.Buffered(1))


def _sigmoid(v):
    return jax.nn.sigmoid(v)


def _dot(a, b):
    return jnp.dot(a, b, preferred_element_type=F32)


def _dot_nt(a, b):
    return lax.dot_general(a, b, (((1,), (1,)), ((), ())), preferred_element_type=F32)


def _dot_tn(a, b):
    return lax.dot_general(a, b, (((0,), (0,)), ((), ())), preferred_element_type=F32)


def _colsum(v):
    return jnp.sum(v, axis=0, keepdims=True)


def _rowmean(v):
    parts = [v[:, LANES * c:LANES * (c + 1)] for c in range(v.shape[1] // LANES)]
    return jnp.sum(functools.reduce(jnp.add, parts), axis=-1, keepdims=True) * (1.0 / v.shape[1])


def _fold8(v):
    parts = [v[SUBLANES * g:SUBLANES * (g + 1)] for g in range(v.shape[0] // SUBLANES)]
    return functools.reduce(jnp.add, parts)


def _sibling_copies(srcs, dsts, send_sems, recv_sems):
    x, y, c = lax.axis_index("x"), lax.axis_index("y"), lax.axis_index("c")
    return [pltpu.make_async_remote_copy(
        src_ref=src.at[2 * q + (1 - c)], dst_ref=dst.at[q],
        send_sem=send_sems.at[4 * a + q], recv_sem=recv_sems.at[4 * a + q],
        device_id=(x, y, 1 - c), device_id_type=MESH)
        for a, (src, dst) in enumerate(zip(srcs, dsts)) for q in range(4)]


def _chip_copies(srcs, dsts, send_sems, recv_sems):
    x, y, c = lax.axis_index("x"), lax.axis_index("y"), lax.axis_index("c")
    targets = [(x, 1 - y, c), (1 - x, y, c), (1 - x, 1 - y, c)]
    return [pltpu.make_async_remote_copy(
        src_ref=src.at[k], dst_ref=dst.at[k],
        send_sem=send_sems.at[3 * a + k], recv_sem=recv_sems.at[3 * a + k],
        device_id=targets[k], device_id_type=MESH)
        for a, (src, dst) in enumerate(zip(srcs, dsts)) for k in range(3)]


def _sibling_half_copies(srcs, dsts, send_sems, recv_sems):
    x, y, c = lax.axis_index("x"), lax.axis_index("y"), lax.axis_index("c")
    return [pltpu.make_async_remote_copy(
        src_ref=src.at[q], dst_ref=dst.at[q],
        send_sem=send_sems.at[4 * a + q], recv_sem=recv_sems.at[4 * a + q],
        device_id=(x, y, 1 - c), device_id_type=MESH)
        for a, (src, dst) in enumerate(zip(srcs, dsts)) for q in range(4)]


def _all_copies(srcs, dsts, send_sems, recv_sems):
    x, y, c = lax.axis_index("x"), lax.axis_index("y"), lax.axis_index("c")
    mine = 4 * x + 2 * y + c
    copies = []
    for a, (src, dst) in enumerate(zip(srcs, dsts)):
        copies.append(pltpu.make_async_copy(src.at[0], dst.at[mine], send_sems.at[N_DEV * a]))
        for k in range(1, N_DEV):
            copies.append(pltpu.make_async_remote_copy(
                src_ref=src.at[0], dst_ref=dst.at[mine],
                send_sem=send_sems.at[N_DEV * a + k], recv_sem=recv_sems.at[N_DEV * a + k],
                device_id=(x ^ (k >> 2), y ^ ((k >> 1) & 1), c ^ (k & 1)), device_id_type=MESH))
    return copies


_EXCHANGES = {"sibling": (4, _sibling_copies, 4), "sibling_half": (4, _sibling_half_copies, 4),
              "chips": (3, _chip_copies, 3), "all": (N_DEV, _all_copies, N_DEV)}


def _exchange_shapes(kind, arrays):
    per_array, _, slots = _EXCHANGES[kind]
    out_shape = [jax.ShapeDtypeStruct((slots,) + a.shape[1:], a.dtype) for a in arrays]
    sems = [pltpu.SemaphoreType.DMA((per_array * len(arrays),))] * 2
    return out_shape, sems


def _ride_shapes(rides):
    shapes, sems = [], []
    for kind, arrays in rides:
        ride_shapes, ride_sems = _exchange_shapes(kind, arrays)
        shapes += ride_shapes
        sems += ride_sems
    return shapes, sems


def _riding(body, n_in, n_out, rides, is_first, is_last):
    counts = [len(arrays) for _, arrays in rides]
    n_arr = sum(counts)

    def wrapped(*refs):
        ins, srcs = refs[:n_in], refs[n_in:n_in + n_arr]
        outs = refs[n_in + n_arr:n_in + n_arr + n_out]
        dsts = refs[n_in + n_arr + n_out:n_in + 2 * n_arr + n_out]
        first_sem = len(refs) - 2 * len(rides)
        scratch, sems = refs[n_in + 2 * n_arr + n_out:first_sem], refs[first_sem:]

        def copies():
            made, at = [], 0
            for r, ((kind, _), n) in enumerate(zip(rides, counts)):
                made += _EXCHANGES[kind][1](srcs[at:at + n], dsts[at:at + n], sems[2 * r], sems[2 * r + 1])
                at += n
            return made

        @pl.when(is_first())
        def _():
            for cp in copies():
                cp.start()

        body(*ins, *outs, *scratch)

        @pl.when(is_last())
        def _():
            for cp in copies():
                cp.wait()

    return wrapped


def _partial_buffers(mine, row_tile):
    n = mine.shape[2]
    return [pltpu.VMEM((2, 2, row_tile, n), F32), pltpu.VMEM((2, row_tile, n), BF16),
            pltpu.SemaphoreType.DMA((4,)), pltpu.SemaphoreType.DMA((2,)),
            pltpu.SemaphoreType.DMA((3,)), pltpu.SemaphoreType.DMA((3,))]


def _partials_and_send(mine_ref, theirs_ref, parts_ref, land_ref, in_buf, out_buf, in_sems, out_sems, send_sems,
                       recv_sems, row_tile):
    x, y, c = lax.axis_index("x"), lax.axis_index("y"), lax.axis_index("c")
    n_slots, m, _ = mine_ref.shape
    n_row = m // row_tile
    tiles = [(q, r) for q in range(3) for r in range(n_row)]
    copies = _chip_copies([parts_ref], [land_ref], send_sems, recv_sems)

    def chip(q):
        return 2 * (x ^ ((q + 1) >> 1)) + (y ^ ((q + 1) & 1))

    def loads(i):
        q, r = tiles[i]
        rows, s = pl.ds(r * row_tile, row_tile), i % 2
        mine_slot = 2 * chip(q) + c if n_slots == N_DEV else chip(q)
        return [pltpu.make_async_copy(mine_ref.at[mine_slot, rows, :], in_buf.at[s, 0], in_sems.at[2 * s]),
                pltpu.make_async_copy(theirs_ref.at[chip(q), rows, :], in_buf.at[s, 1], in_sems.at[2 * s + 1])]

    def store(i):
        q, r = tiles[i]
        return pltpu.make_async_copy(out_buf.at[i % 2], parts_ref.at[q, pl.ds(r * row_tile, row_tile), :],
                                     out_sems.at[i % 2])

    for cp in loads(0):
        cp.start()
    open_stores = []
    for i, (q, r) in enumerate(tiles):
        if i + 1 < len(tiles):
            for cp in loads(i + 1):
                cp.start()
        for cp in loads(i):
            cp.wait()
        if i - 2 in open_stores:
            store(i - 2).wait()
            open_stores.remove(i - 2)
        out_buf[i % 2] = (in_buf[i % 2, 0] + in_buf[i % 2, 1]).astype(BF16)
        store(i).start()
        open_stores.append(i)
        if r == n_row - 1:
            for j in open_stores:
                store(j).wait()
            open_stores = []
            copies[q].start()
    return copies


PARTS = ((0, 512), (512, 640))


def _gather_norm_proj(pos, x2d, small_shard, norm_g, w_in_shard, w_out_shards, n_chunk):
    seq = x2d.shape[0]
    n_tiles = seq // TILE + 1
    tp = n_tiles * TILE
    n_parts = len(PARTS)
    widest = max(width for _, width in PARTS)
    units = [(s, u) for s in range(2) for u in range(n_parts)]
    for first in (2, 5):
        units += [(first + j, u) for u in range(n_parts) for j in range(2)] + [(first + 2, u) for u in range(n_parts)]
    n_units = len(units)
    n_steps = n_tiles + n_units
    chunk = tp // n_chunk

    def body(pos_ref, x_ref, g_ref, small_ref, win_ref, wa_ref, wb_ref, wo_ref,
             ht_ref, proj_ref, meta_ref, small_all, win_all, wa_all, wb_all, wo_all,
             h_all, wbuf, rbuf, small_buf, send_sems, recv_sems, local_sems, small_send, small_recv):
        g = pl.program_id(0)
        x, y, c = lax.axis_index("x"), lax.axis_index("y"), lax.axis_index("c")
        me, sibling = (x, y, c), (x, y, 1 - c)
        chips = [(1 - x, y), (x, 1 - y), (1 - x, 1 - y)]
        shards = (win_ref, wa_ref, wb_ref, wo_ref)
        gathered = (win_all, wa_all, wb_all, wo_all)
        n_arrays = len(shards)
        blocks = [me, sibling] + [(*chip, c) for chip in chips] + [(*chip, 1 - c) for chip in chips]

        def index(block):
            px, py, pc = block
            return 4 * px + 2 * py + pc

        def part(ref, a, u):
            return ref.at[:, pl.ds(PARTS[u][0], PARTS[u][1])] if a == 0 else ref

        def slot(a, block, u):
            return part(gathered[a].at[index(block)], a, u)

        def sem(a, k, u):
            return n_parts * k + u if a == 0 else 7 * n_parts + 7 * (a - 1) + k

        def copy(a, k, block, to, u=0, from_shard=False):
            return pltpu.make_async_remote_copy(
                src_ref=part(shards[a], a, u) if from_shard else slot(a, block, u), dst_ref=slot(a, block, u),
                send_sem=send_sems.at[sem(a, k, u)], recv_sem=recv_sems.at[sem(a, k, u)],
                device_id=to, device_id_type=MESH)

        def keep(a):
            return pltpu.make_async_copy(shards[a], gathered[a].at[index(me)], local_sems.at[a])

        def load(m):
            s, u = units[m]
            src = part(win_ref, 0, u) if s == 0 else slot(0, blocks[s], u)
            return pltpu.make_async_copy(src, wbuf.at[m % 2, :, 0:PARTS[u][1]], local_sems.at[n_arrays + m % 2])

        def store(m):
            s, u = units[m]
            col0 = pl.multiple_of(index(blocks[s]) * COLS + PARTS[u][0], LANES)
            return pltpu.make_async_copy(rbuf.at[m % 2, :, 0:PARTS[u][1]],
                                         proj_ref.at[:, pl.ds(col0, PARTS[u][1])], local_sems.at[n_arrays + 2 + m % 2])

        def by_x(a, u):
            return u == 0 if a == 0 else a < 3

        def relay(a, u=0):
            src, to = (blocks[3], blocks[2]) if by_x(a, u) else (blocks[2], blocks[3])
            return copy(a, 3, src, to, u)

        def arrive(m):
            s, u = units[m]
            if s == 1:
                copy(0, 0, sibling, me, u).wait_recv()
            elif 2 <= s <= 4:
                copy(0, s - 1, blocks[s], me, u).wait_recv()
                copy(0, s + 2, blocks[s], sibling, u).start()
                if s < 4 and by_x(0, u) == (s == 3):
                    relay(0, u).start()
            elif s >= 5:
                copy(0, s - 1, blocks[s], me, u).wait_recv()
                if u == 0:
                    for a in range(1, 4):
                        copy(a, s - 4, blocks[s - 3], me).wait_recv()
                        copy(a, s - 1, blocks[s - 3], sibling).start()
                        if s < 7 and by_x(a, 0) == (s == 6):
                            relay(a).start()

        targets = [sibling, blocks[2], blocks[3]]

        def small_copies():
            return _all_copies([small_ref], [small_all], small_send, small_recv)

        @pl.when(g == 0)
        def _():
            for cp in small_copies():
                cp.start()
            for a in range(n_arrays):
                keep(a).start()
            for u in range(n_parts):
                for k, to in enumerate(targets):
                    copy(0, k, me, to, u, from_shard=True).start()
            for a in range(1, 4):
                for k, to in enumerate(targets):
                    copy(a, k, me, to, from_shard=True).start()
            load(0).start()

        @pl.when(g == n_tiles - 2)
        def _():
            for cp in small_copies():
                cp.wait()
            fetch = pltpu.make_async_copy(small_all, small_buf, local_sems.at[n_arrays + 4])
            fetch.start()
            fetch.wait()
            meta_ref[0:TILE - N_META, :] = jnp.zeros((TILE - N_META, D_MODEL), F32)
            meta_ref[TILE - N_META:TILE, :] = jnp.concatenate([small_buf[d, 0:N_META, :] for d in range(N_DEV)], axis=1)

        @pl.when(g < n_tiles)
        def _():
            s0 = jnp.where(g == n_tiles - 1, meta_ref[...], x_ref[...])
            r = lax.rsqrt(_rowmean(s0 * s0) + EPS)
            h32 = (s0 * r) * g_ref[...]
            ht_ref[...] = h32.T.astype(BF16)
            h_all[pl.ds(pl.multiple_of(g * TILE, TILE), TILE), :] = h32.astype(BF16)

        for m in range(n_units):
            @pl.when(g == n_tiles + m)
            def _(m=m):
                load(m).wait()
                if m + 1 < n_units:
                    arrive(m + 1)
                    load(m + 1).start()
                if m >= 2:
                    store(m - 2).wait()

        m_now = jnp.maximum(g - n_tiles, 0)
        u_now = functools.reduce(jnp.add, [jnp.where(m_now == m, u, 0) for m, (_, u) in enumerate(units)])
        for u, (_, width) in enumerate(PARTS):
            @pl.when((g >= n_tiles) & (u_now == u))
            def _(width=width):
                w = wbuf[m_now % 2, :, 0:width]
                for r in range(n_chunk):
                    rbuf[m_now % 2, r * chunk:(r + 1) * chunk, 0:width] = _dot(h_all[r * chunk:(r + 1) * chunk, :], w)

        for m in range(n_units):
            @pl.when(g == n_tiles + m)
            def _(m=m):
                store(m).start()

        @pl.when(g == n_steps - 1)
        def _():
            store(n_units - 2).wait()
            store(n_units - 1).wait()
            for a in range(1, 4):
                copy(a, 0, sibling, me).wait_recv()
                for j in range(3):
                    copy(a, 4 + j, blocks[5 + j], me).wait_recv()
            for a in range(n_arrays):
                for u in range(n_parts if a == 0 else 1):
                    for k, to in enumerate(targets):
                        copy(a, k, me, to, u, from_shard=True).wait_send()
                    relay(a, u).wait_send()
                    for j in range(3):
                        copy(a, 4 + j, blocks[2 + j], sibling, u).wait_send()
                keep(a).wait()

    n_x = n_tiles - 1
    return pl.pallas_call(
        body, name="gather_norm_proj",
        out_shape=[jax.ShapeDtypeStruct((D_MODEL, tp), BF16), jax.ShapeDtypeStruct((tp, D_IN), F32),
                   jax.ShapeDtypeStruct((TILE, D_MODEL), F32), jax.ShapeDtypeStruct((N_DEV,) + small_shard.shape[1:], F32),
                   jax.ShapeDtypeStruct((N_DEV,) + w_in_shard.shape, BF16)]
                  + [jax.ShapeDtypeStruct((N_DEV,) + w.shape, BF16) for w in w_out_shards],
        grid_spec=pltpu.PrefetchScalarGridSpec(
            num_scalar_prefetch=1, grid=(n_steps,),
            in_specs=[pl.BlockSpec((TILE, D_MODEL), lambda g, pos_ref: (jnp.minimum(g, n_x - 1), 0)),
                      _VMEM, _ANY, _ANY, _ANY, _ANY, _ANY],
            out_specs=[pl.BlockSpec((D_MODEL, TILE), lambda g, pos_ref: (0, jnp.minimum(g, n_tiles - 1))),
                       _ANY, _VMEM, _ANY, _ANY, _ANY, _ANY, _ANY],
            scratch_shapes=[pltpu.VMEM((tp, D_MODEL), BF16), pltpu.VMEM((2, D_MODEL, widest), BF16),
                            pltpu.VMEM((2, tp, widest), F32), pltpu.VMEM((N_DEV,) + small_shard.shape[1:], F32),
                            pltpu.SemaphoreType.DMA((7 * n_parts + 21,)), pltpu.SemaphoreType.DMA((7 * n_parts + 21,)),
                            pltpu.SemaphoreType.DMA((9,)),
                            pltpu.SemaphoreType.DMA((N_DEV,)), pltpu.SemaphoreType.DMA((N_DEV,))]),
        compiler_params=pltpu.CompilerParams(dimension_semantics=("arbitrary",), vmem_limit_bytes=VMEM_LIMIT),
    )(pos, x2d, norm_g, small_shard, w_in_shard, *w_out_shards)


C_AVAL, C_AGLU, C_AZ, C_BB, C_BC, C_BX, C_BZ, C_GA, C_GB = (k * D_MODEL for k in range(9))
S_AZ, S_BB, S_BZ, S_GA, S_GB = (k * D_MODEL for k in range(5))


def _fused_pass(proj, x2d, tgt2d, meta_tile, conv_a_w, conv_a_b, ln_a_g, ln_a_b, b_a_out, conv_b_w, final_g,
                w_a, w_b, w_o, w_a_t, w_b_t, w_o_t, n_tiles):
    T = TILE
    tp = n_tiles * T
    inv_d = 1.0 / D_MODEL

    def block_of(tile):
        return jnp.where(tile == 0, n_tiles - 1, tile - 1)

    def cur(i):
        return block_of(jnp.minimum(i, n_tiles - 1))

    def prev(i):
        return block_of(jnp.clip(i - 1, 0, n_tiles - 1))

    def xblk(i):
        return jnp.maximum(jnp.minimum(i, n_tiles - 1) - 1, 0)

    def body(proj_ref, x_ref, tgt_ref, meta_ref, caw_ref, cab_ref, lng_ref, lnb_ref, bao_ref, cbw_ref, fg_ref,
             wa_ref, wb_ref, wo_ref, wat_ref, wbt_ref, wot_ref,
             dproj_ref, ds1_ref, lhs_ref, rhs_ref, small_ref,
             ua0_buf, cb_buf, dua1_buf, dc3_buf, aprev, cprev, stage, ua1_buf, c3_buf,
             dpa_buf, dpb_buf, dcaw8, dcbw8, shift_buf):
        i = pl.program_id(0)

        @pl.when(i == 0)
        def _init():
            for buf in (ua0_buf, cb_buf, dua1_buf, dc3_buf, aprev, cprev, dcaw8, dcbw8):
                buf[...] = jnp.zeros(buf.shape, buf.dtype)
            small_ref[...] = jnp.zeros(small_ref.shape, F32)

        @pl.when(i >= 1)
        def _emit_stage():
            dproj_ref[:, C_AZ:C_BC] = stage[:, S_AZ:S_BZ]
            dproj_ref[:, C_BZ:D_IN] = stage[:, S_BZ:S_GB + D_MODEL]

        @pl.when(i < n_tiles)
        def _front():
            def conv_chunk(cc, carry):
                c0 = pl.multiple_of(cc * LANES, LANES)
                lanes = pl.ds(c0, LANES)

                def col(base):
                    return pl.ds(pl.multiple_of(base + cc * LANES, LANES), LANES)

                ua0 = proj_ref[:, col(C_AVAL)] * _sigmoid(proj_ref[:, col(C_AGLU)])
                ua0_buf[T:2 * T, lanes] = ua0
                acc = jnp.broadcast_to(cab_ref[:, lanes], (T, LANES))
                lead = HALO - (CONV_A - 1)
                for r in range(SUBLANES):
                    taps = [k for k in range(CONV_A) if (k + lead) % SUBLANES == r]
                    rows = T + SUBLANES * max((k + lead) // SUBLANES for k in taps)
                    if r:
                        shift_buf[r, 0:rows, :] = ua0_buf[pl.ds(T - HALO + r, rows), lanes]
                    for k in taps:
                        q = (k + lead) // SUBLANES
                        if r:
                            win = shift_buf[r, SUBLANES * q:SUBLANES * q + T, :]
                        else:
                            win = ua0_buf[pl.ds(T - HALO + SUBLANES * q, T), lanes]
                        acc = acc + caw_ref[k:k + 1, lanes] * win
                ua1_buf[:, lanes] = acc
                cb = proj_ref[:, col(C_BC)] * proj_ref[:, col(C_BX)]
                cb_buf[T:2 * T, lanes] = cb
                acc3 = cbw_ref[0:1, lanes] * cb_buf[pl.ds(T - 2, T), lanes]
                for k in range(1, CONV_B):
                    acc3 = acc3 + cbw_ref[k:k + 1, lanes] * cb_buf[pl.ds(T - (CONV_B - 1) + k, T), lanes]
                c3_buf[:, lanes] = acc3
                return carry

            lax.fori_loop(0, N_CHUNK, conv_chunk, 0)

            ua1 = ua1_buf[...]
            xc = ua1 - _rowmean(ua1)
            rstd = lax.rsqrt(_rowmean(xc * xc) + EPS)
            xhat = xc * rstd
            ua2 = xhat * lng_ref[...] + lnb_ref[...]
            sg2 = _sigmoid(ua2)
            ua3 = ua2 * sg2
            a_z = proj_ref[:, C_AZ:C_AZ + D_MODEL]
            sz = _sigmoid(a_z)
            silu_az = a_z * sz
            lhs_ref[0] = (ua3 * silu_az).astype(BF16)
            b_z = proj_ref[:, C_BZ:C_BZ + D_MODEL]
            sbz = _sigmoid(b_z)
            silu_bz = b_z * sbz
            b_b = proj_ref[:, C_BB:C_BB + D_MODEL]
            c3 = c3_buf[...]
            ub = b_b * c3
            lhs_ref[1] = (ub * silu_bz).astype(BF16)

            ya = _dot(lhs_ref[0], wa_ref[...]) + bao_ref[...]
            yb = _dot(lhs_ref[1], wb_ref[...])
            sga = _sigmoid(proj_ref[:, C_GA:C_GA + D_MODEL])
            sgb = _sigmoid(proj_ref[:, C_GB:C_GB + D_MODEL])
            m_b = (sga * ya + sgb * yb).astype(BF16)
            lhs_ref[2] = m_b
            s0 = jnp.where(i == 0, meta_ref[...], x_ref[...])
            s1 = s0 + _dot(m_b, wo_ref[...])
            r1 = lax.rsqrt(_rowmean(s1 * s1) + EPS)
            y = (s1 * r1) * fg_ref[...]
            is_token = (i >= 1).astype(F32)
            err = (y - tgt_ref[...]) * is_token
            small_ref[ROW_LOSS:ROW_LOSS + 1, :] += (0.5 * inv_d) * _colsum(err * err)
            dy = err * inv_d
            small_ref[ROW_FINAL_G:ROW_FINAL_G + 1, :] += _colsum(dy * (s1 * r1))
            gy = dy * fg_ref[...]
            ds1 = r1 * gy - s1 * ((r1 * r1 * r1) * _rowmean(gy * s1))
            ds1_ref[...] = ds1
            ds1_b = ds1.astype(BF16)
            rhs_ref[2] = ds1_b
            dm = _dot(ds1_b, wot_ref[...])
            dya = dm * sga
            dyb = dm * sgb
            stage[:, S_GA:S_GA + D_MODEL] = (dya * ya * (1.0 - sga)).astype(BF16)
            stage[:, S_GB:S_GB + D_MODEL] = (dyb * yb * (1.0 - sgb)).astype(BF16)
            small_ref[ROW_B_A_OUT:ROW_B_A_OUT + 1, :] += _colsum(dya)
            dya_b = dya.astype(BF16)
            dyb_b = dyb.astype(BF16)
            rhs_ref[0] = dya_b
            rhs_ref[1] = dyb_b
            dpa_buf[...] = _dot(dya_b, wat_ref[...])
            dpb_buf[...] = _dot(dyb_b, wbt_ref[...])

            dpa = dpa_buf[...]
            stage[:, S_AZ:S_AZ + D_MODEL] = (dpa * ua3 * (sz + silu_az * (1.0 - sz))).astype(BF16)
            dua2 = dpa * silu_az * (sg2 + ua3 * (1.0 - sg2))
            small_ref[ROW_LN_G:ROW_LN_G + 1, :] += _colsum(dua2 * xhat)
            small_ref[ROW_LN_B:ROW_LN_B + 1, :] += _colsum(dua2)
            dxh = dua2 * lng_ref[...]
            dua1 = rstd * (dxh - _rowmean(dxh) - xhat * _rowmean(dxh * xhat))
            small_ref[ROW_CONV_A_B:ROW_CONV_A_B + 1, :] += _colsum(dua1)
            dua1_buf[T:2 * T, :] = dua1
            dpb = dpb_buf[...]
            stage[:, S_BZ:S_BZ + D_MODEL] = (dpb * ub * (sbz + silu_bz * (1.0 - sbz))).astype(BF16)
            dub = dpb * silu_bz
            stage[:, S_BB:S_BB + D_MODEL] = (dub * c3).astype(BF16)
            dc3_buf[T:2 * T, :] = dub * b_b

        @pl.when(i == n_tiles)
        def _no_later_tile():
            dua1_buf[T:2 * T, :] = jnp.zeros((T, D_MODEL), F32)
            dc3_buf[T:2 * T, :] = jnp.zeros((T, D_MODEL), F32)

        @pl.when(i >= 1)
        def _lagged():
            def convt_chunk(cc, carry):
                c0 = pl.multiple_of(cc * LANES, LANES)
                lanes = pl.ds(c0, LANES)

                def col(base):
                    return pl.ds(pl.multiple_of(base + cc * LANES, LANES), LANES)

                ua0 = ua0_buf[0:T, lanes]
                acc = jnp.zeros((T, LANES), F32)
                for r in range(SUBLANES):
                    shifts = [j for j in range(CONV_A) if j % SUBLANES == r]
                    rows = T + shifts[-1] - r
                    if r:
                        shift_buf[r, 0:rows, :] = dua1_buf[pl.ds(r, rows), lanes]
                    for j in shifts:
                        k = CONV_A - 1 - j
                        if r:
                            later = shift_buf[r, j - r:j - r + T, :]
                        else:
                            later = dua1_buf[pl.ds(j, T), lanes]
                        acc = acc + caw_ref[k:k + 1, lanes] * later
                        dcaw8[SUBLANES * k:SUBLANES * (k + 1), lanes] += _fold8(ua0 * later)
                a_val = aprev[:, col(0)]
                sg = _sigmoid(aprev[:, col(D_MODEL)])
                dproj_ref[:, col(C_AVAL)] = (acc * sg).astype(BF16)
                dproj_ref[:, col(C_AGLU)] = (acc * a_val * (sg * (1.0 - sg))).astype(BF16)

                cb = cb_buf[0:T, lanes]
                acc3 = jnp.zeros((T, LANES), F32)
                for j in range(CONV_B):
                    k = CONV_B - 1 - j
                    later = dc3_buf[pl.ds(j, T), lanes]
                    acc3 = acc3 + cbw_ref[k:k + 1, lanes] * later
                    dcbw8[SUBLANES * k:SUBLANES * (k + 1), lanes] += _fold8(cb * later)
                dproj_ref[:, col(C_BC)] = (acc3 * cprev[:, col(D_MODEL)]).astype(BF16)
                dproj_ref[:, col(C_BX)] = (acc3 * cprev[:, col(0)]).astype(BF16)
                return carry

            lax.fori_loop(0, N_CHUNK, convt_chunk, 0)

        for buf in (ua0_buf, cb_buf, dua1_buf, dc3_buf):
            buf[0:T, :] = buf[T:2 * T, :]
        aprev[...] = proj_ref[:, C_AVAL:C_AZ]
        cprev[...] = proj_ref[:, C_BC:C_BZ]

        @pl.when(i == n_tiles)
        def _finish():
            for k in range(CONV_A):
                small_ref[ROW_CONV_A_W + k:ROW_CONV_A_W + k + 1, :] = _colsum(dcaw8[SUBLANES * k:SUBLANES * (k + 1), :])
            for k in range(CONV_B):
                small_ref[ROW_CONV_B_W + k:ROW_CONV_B_W + k + 1, :] = _colsum(dcbw8[SUBLANES * k:SUBLANES * (k + 1), :])

    tile_in = lambda width: pl.BlockSpec((T, width), lambda i: (cur(i), 0))
    return pl.pallas_call(
        body, name="fused_pass", grid=(n_tiles + 1,),
        out_shape=[
            jax.ShapeDtypeStruct((tp, D_IN), BF16),
            jax.ShapeDtypeStruct((tp, D_MODEL), F32),
            jax.ShapeDtypeStruct((3, tp, D_MODEL), BF16),
            jax.ShapeDtypeStruct((3, tp, D_MODEL), BF16),
            jax.ShapeDtypeStruct((SMALL_A_ROWS, D_MODEL), F32),
        ],
        in_specs=[
            tile_in(D_IN),
            pl.BlockSpec((T, D_MODEL), lambda i: (xblk(i), 0)),
            pl.BlockSpec((T, D_MODEL), lambda i: (xblk(i), 0)),
            _VMEM, _VMEM, _VMEM, _VMEM, _VMEM, _VMEM, _VMEM, _VMEM,
            *[_resident((D_MODEL, D_MODEL)) for _ in range(6)],
        ],
        out_specs=[
            pl.BlockSpec((T, D_IN), lambda i: (prev(i), 0)),
            pl.BlockSpec((T, D_MODEL), lambda i: (cur(i), 0)),
            pl.BlockSpec((3, T, D_MODEL), lambda i: (0, cur(i), 0)),
            pl.BlockSpec((3, T, D_MODEL), lambda i: (0, cur(i), 0)),
            _VMEM,
        ],
        scratch_shapes=[
            pltpu.VMEM((2 * T, D_MODEL), F32),
            pltpu.VMEM((2 * T, D_MODEL), F32),
            pltpu.VMEM((2 * T, D_MODEL), F32),
            pltpu.VMEM((2 * T, D_MODEL), F32),
            pltpu.VMEM((T, 2 * D_MODEL), F32),
            pltpu.VMEM((T, 2 * D_MODEL), F32),
            pltpu.VMEM((T, 5 * D_MODEL), BF16),
            pltpu.VMEM((T, D_MODEL), F32),
            pltpu.VMEM((T, D_MODEL), F32),
            pltpu.VMEM((T, D_MODEL), F32),
            pltpu.VMEM((T, D_MODEL), F32),
            pltpu.VMEM((32 * SUBLANES, D_MODEL), F32),
            pltpu.VMEM((SUBLANES * SUBLANES, D_MODEL), F32),
            pltpu.VMEM((SUBLANES, T + HALO, LANES), F32),
        ],
        compiler_params=pltpu.CompilerParams(dimension_semantics=("arbitrary",), vmem_limit_bytes=VMEM_LIMIT),
    )(proj, x2d, tgt2d, meta_tile, conv_a_w, conv_a_b, ln_a_g, ln_a_b, b_a_out, conv_b_w, final_g,
      w_a, w_b, w_o, w_a_t, w_b_t, w_o_t)


def _input_bwd(dproj, ds1, x2d, meta_tile, norm_g, w_in_all, row_tile, mine, theirs, partial_tile):
    seq = x2d.shape[0]
    n_steps = seq // row_tile
    meta_block = seq // TILE
    parts_shape = jax.ShapeDtypeStruct((3,) + mine.shape[1:], BF16)

    def backward(dp_ref, ds1_ref, s0_ref, g_ref, w_ref, out_ref, vec_ref):
        dh = _dot_nt(dp_ref[:, 0:COLS], w_ref[0])
        for j in range(1, N_DEV):
            dh = dh + _dot_nt(dp_ref[:, j * COLS:(j + 1) * COLS], w_ref[j])
        s0v = s0_ref[...]
        r = lax.rsqrt(_rowmean(s0v * s0v) + EPS)
        gh = dh * g_ref[...]
        out_ref[...] = ds1_ref[...] + r * gh - s0v * ((r * r * r) * _rowmean(gh * s0v))
        vec_ref[ROW_NORM_G:ROW_NORM_G + 1, :] += _colsum(dh * (s0v * r))

    def body(dp_ref, ds1_ref, x_ref, dpm_ref, ds1m_ref, meta_ref, g_ref, w_ref, mine_ref, theirs_ref,
             gx_ref, small_all_ref, parts_ref, land_ref,
             gmeta_buf, small_buf, send_sems, recv_sems, *partial_scratch):
        t = pl.program_id(0)
        small = small_buf.at[0]

        @pl.when(t == 0)
        def _():
            small[...] = jnp.zeros(small.shape, F32)
            _partials_and_send(mine_ref, theirs_ref, parts_ref, land_ref, *partial_scratch, partial_tile)

        backward(dp_ref, ds1_ref, x_ref, g_ref, w_ref, gx_ref, small)

        @pl.when(t == n_steps - 1)
        def _():
            backward(dpm_ref, ds1m_ref, meta_ref, g_ref, w_ref, gmeta_buf, small)
            small[ROW_META:ROW_META + N_META, :] = gmeta_buf[TILE - N_META:TILE, :]
            copies = _all_copies([small_buf], [small_all_ref], send_sems, recv_sems)
            for cp in copies:
                cp.start()
            for cp in copies + _chip_copies([parts_ref], [land_ref], *partial_scratch[4:]):
                cp.wait()

    return pl.pallas_call(
        body, name="input_bwd", grid=(n_steps,),
        out_shape=[jax.ShapeDtypeStruct(x2d.shape, F32), jax.ShapeDtypeStruct((N_DEV, SMALL_B_ROWS, D_MODEL), F32),
                   parts_shape, parts_shape],
        in_specs=[pl.BlockSpec((row_tile, D_IN), lambda t: (t, 0)),
                  pl.BlockSpec((row_tile, D_MODEL), lambda t: (t, 0)),
                  pl.BlockSpec((row_tile, D_MODEL), lambda t: (t, 0)),
                  pl.BlockSpec((TILE, D_IN), lambda t: (meta_block, 0)),
                  pl.BlockSpec((TILE, D_MODEL), lambda t: (meta_block, 0)),
                  _VMEM, _VMEM, _resident((N_DEV, D_MODEL, COLS)), _ANY, _ANY],
        out_specs=[pl.BlockSpec((row_tile, D_MODEL), lambda t: (t, 0)), _ANY, _ANY, _ANY],
        scratch_shapes=[pltpu.VMEM((TILE, D_MODEL), F32), pltpu.VMEM((1, SMALL_B_ROWS, D_MODEL), F32),
                        pltpu.SemaphoreType.DMA((N_DEV,)), pltpu.SemaphoreType.DMA((N_DEV,))]
        + _partial_buffers(mine, partial_tile),
        compiler_params=pltpu.CompilerParams(dimension_semantics=("arbitrary",), vmem_limit_bytes=VMEM_LIMIT),
    )(dproj, ds1, x2d, dproj, ds1, meta_tile, norm_g, w_in_all, mine, theirs)


def _grad_w_in_half(pos, h_t, dproj, k_tile, other_side, rides, name, partials=None):
    tp = h_t.shape[1]
    n_k = tp // k_tile
    first = lambda: (pl.program_id(0) == 0) & (pl.program_id(1) == 0)
    last = lambda: (pl.program_id(0) == 3) & (pl.program_id(1) == n_k - 1)

    def column_block(q, k, pos_ref):
        return k, 2 * q + (1 - pos_ref[2] if other_side else pos_ref[2])

    def matmul(h_ref, dp_ref, o_ref):
        @pl.when(pl.program_id(1) == 0)
        def _():
            o_ref[...] = jnp.zeros(o_ref.shape, F32)

        o_ref[...] += _dot(h_ref[...], dp_ref[...])

    def body(pos_ref, h_ref, dp_ref, o_ref):
        matmul(h_ref, dp_ref, o_ref)

    def body_with_partials(pos_ref, h_ref, dp_ref, mine_ref, theirs_ref, o_ref, parts_ref, land_ref, *scratch):
        @pl.when(first())
        def _():
            _partials_and_send(mine_ref, theirs_ref, parts_ref, land_ref, *scratch, partials[2])

        matmul(h_ref, dp_ref, o_ref)

        @pl.when(last())
        def _():
            for cp in _chip_copies([parts_ref], [land_ref], *scratch[4:]):
                cp.wait()

    extra_in, extra_out, extra_scratch = [], [], []
    if partials:
        mine, theirs, partial_tile = partials
        extra_in = [mine, theirs]
        extra_out = [jax.ShapeDtypeStruct((3,) + mine.shape[1:], BF16)] * 2
        extra_scratch = _partial_buffers(mine, partial_tile)
    ride = [a for _, arrays in rides for a in arrays]
    n_arr = len(ride)
    ride_shapes, ride_sems = _ride_shapes(rides)
    body = _riding(body_with_partials if partials else body, 3 + len(extra_in), 1 + len(extra_out), rides, first, last)
    return pl.pallas_call(
        body, name=name,
        out_shape=[jax.ShapeDtypeStruct((4, D_MODEL, COLS), F32)] + extra_out + ride_shapes,
        grid_spec=pltpu.PrefetchScalarGridSpec(
            num_scalar_prefetch=1, grid=(4, n_k),
            in_specs=[pl.BlockSpec((D_MODEL, k_tile), lambda q, k, pos_ref: (0, k)),
                      pl.BlockSpec((k_tile, COLS), column_block)] + [_ANY] * (len(extra_in) + n_arr),
            out_specs=[pl.BlockSpec((None, D_MODEL, COLS), lambda q, k, pos_ref: (q, 0, 0))]
            + [_ANY] * (len(extra_out) + n_arr),
            scratch_shapes=extra_scratch + ride_sems),
        compiler_params=pltpu.CompilerParams(dimension_semantics=("arbitrary", "arbitrary"),
                                             vmem_limit_bytes=VMEM_LIMIT),
    )(pos, h_t, dproj, *extra_in, *ride)


def _grad_w_out(lhs, rhs, k_tile):
    tp = lhs.shape[1]

    def body(a_ref, b_ref, o_ref):
        @pl.when(pl.program_id(1) == 0)
        def _():
            o_ref[...] = jnp.zeros(o_ref.shape, F32)

        o_ref[...] += _dot_tn(a_ref[...], b_ref[...]).reshape(N_DEV, ROWS_OUT, D_MODEL)

    return pl.pallas_call(
        body, name="grad_w_out", grid=(3, tp // k_tile),
        out_shape=jax.ShapeDtypeStruct((N_DEV, 3, ROWS_OUT, D_MODEL), F32),
        in_specs=[pl.BlockSpec((None, k_tile, D_MODEL), lambda w, k: (w, k, 0)),
                  pl.BlockSpec((None, k_tile, D_MODEL), lambda w, k: (w, k, 0))],
        out_specs=pl.BlockSpec((N_DEV, None, ROWS_OUT, D_MODEL), lambda w, k: (0, w, 0, 0)),
        compiler_params=pltpu.CompilerParams(dimension_semantics=("arbitrary", "arbitrary"),
                                             vmem_limit_bytes=VMEM_LIMIT),
    )(lhs, rhs)


def _adamw_math(w, g, m, v):
    m = ADAM_B1 * m + (1.0 - ADAM_B1) * g
    v = ADAM_B2 * v + (1.0 - ADAM_B2) * (g * g)
    m_hat = m / (1.0 - ADAM_B1 ** ADAM_STEP)
    v_hat = v / (1.0 - ADAM_B2 ** ADAM_STEP)
    delta = -ADAM_LR * (m_hat / (jnp.sqrt(v_hat) + ADAM_EPS) + ADAM_WD * w)
    return delta, m, v


def _adamw_sharded(pos, mine, theirs, landed, weights, row_tile, name):
    rows, n = weights[0][0].shape
    n_slots = mine.shape[0]
    per_shard = rows // row_tile
    assert per_shard == 1 or len(weights) == 1

    def mine_map(j, t, pos_ref):
        chip = 2 * pos_ref[0] + pos_ref[1]
        return (2 * chip + pos_ref[2] if n_slots == N_DEV else chip), j * per_shard + t, 0

    def theirs_map(j, t, pos_ref):
        return 2 * pos_ref[0] + pos_ref[1], j * per_shard + t, 0

    def body(pos_ref, mine_ref, theirs_ref, land_ref, *refs):
        ins, outs = refs[:3 * len(weights)], refs[3 * len(weights):]
        g = mine_ref[...] + theirs_ref[...]
        for k in range(3):
            g = g + land_ref[k].astype(F32)
        for j in range(len(weights)):
            @pl.when(pl.program_id(0) == j)
            def _(j=j):
                w_ref, m_ref, v_ref = ins[3 * j:3 * j + 3]
                delta, m_new, v_new = _adamw_math(w_ref[...], g, m_ref[...], v_ref[...])
                for ref, val in zip(outs[4 * j:4 * j + 4], (g, delta, m_new, v_new)):
                    ref[...] = val

    tile = pl.BlockSpec((row_tile, n), lambda j, t, pos_ref: (t, 0))
    res = pl.pallas_call(
        body, name=name,
        out_shape=[jax.ShapeDtypeStruct((rows, n), F32)] * (4 * len(weights)),
        grid_spec=pltpu.PrefetchScalarGridSpec(
            num_scalar_prefetch=1, grid=(len(weights), per_shard),
            in_specs=[pl.BlockSpec((None, row_tile, n), mine_map), pl.BlockSpec((None, row_tile, n), theirs_map),
                      pl.BlockSpec((3, row_tile, n), lambda j, t, pos_ref: (0, j * per_shard + t, 0))]
            + [tile] * (3 * len(weights)),
            out_specs=[tile] * (4 * len(weights))),
        compiler_params=pltpu.CompilerParams(dimension_semantics=("arbitrary", "arbitrary")),
    )(pos, mine, theirs, landed, *[a for wmv in weights for a in wmv])
    return [res[4 * j:4 * j + 4] for j in range(len(weights))]


def _adamw_small(gathered, gathered_cols, params):
    n_par, n_src = len(params), len(gathered)

    def body(*refs):
        g_refs, gc_refs = refs[:n_src], refs[n_src:2 * n_src]
        ins = refs[2 * n_src:2 * n_src + 3 * n_par]
        outs = refs[2 * n_src + 3 * n_par:]
        loss_ref = outs[4 * n_par]

        def reduced(ref, row, n_rows):
            g = ref[0, row:row + n_rows, :]
            for d in range(1, N_DEV):
                g = g + ref[d, row:row + n_rows, :]
            return g

        for p, (src, row, n_rows, sharded, _, _, _) in enumerate(params):
            g = reduced((gc_refs if sharded else g_refs)[src], row, n_rows)
            w_ref, m_ref, v_ref = ins[3 * p:3 * p + 3]
            delta, m_new, v_new = _adamw_math(w_ref[...], g, m_ref[...], v_ref[...])
            outs[4 * p][...] = g
            outs[4 * p + 1][...] = delta
            outs[4 * p + 2][...] = m_new
            outs[4 * p + 3][...] = v_new
        loss = jnp.sum(reduced(g_refs[0], ROW_LOSS, 1), axis=1, keepdims=True)
        loss_ref[...] = jnp.broadcast_to(loss, loss_ref.shape)

    out_shape = []
    for (_, _, _, _, w, _, _) in params:
        out_shape += [jax.ShapeDtypeStruct(w.shape, F32)] * 4
    out_shape.append(jax.ShapeDtypeStruct((1, LANES), F32))
    flat = [a for (_, _, _, _, w, m, v) in params for a in (w, m, v)]
    return pl.pallas_call(
        body, name="adamw_small", out_shape=out_shape,
        in_specs=[_VMEM] * (2 * n_src + len(flat)), out_specs=[_VMEM] * len(out_shape),
    )(*gathered, *gathered_cols, *flat)


def _pad_rows(a, rows):
    return jnp.concatenate([a, jnp.zeros((rows - a.shape[0], a.shape[1]), a.dtype)], axis=0)


def kernel(x, meta_tokens, norm_g, w_in, conv_a_w, conv_a_b, ln_a_g, ln_a_b, w_a_out, b_a_out, conv_b_w, w_b_out, w_out, final_g, loss_target, m_meta_tokens, m_norm_g, m_w_in, m_conv_a_w, m_conv_a_b, m_ln_a_g, m_ln_a_b, m_w_a_out, m_b_a_out, m_conv_b_w, m_w_b_out, m_w_out, m_final_g, v_meta_tokens, v_norm_g, v_w_in, v_conv_a_w, v_conv_a_b, v_ln_a_g, v_ln_a_b, v_w_a_out, v_b_a_out, v_conv_b_w, v_w_b_out, v_w_out, v_final_g):
    seq = x.shape[1]
    assert x.shape == (1, seq, D_MODEL) and seq % TILE == 0 and w_in.shape == (1, D_MODEL, COLS)
    n_tiles = seq // TILE + 1
    tp = n_tiles * TILE
    pos = jnp.stack([lax.axis_index("x"), lax.axis_index("y"), lax.axis_index("c")]).astype(jnp.int32)
    me = 4 * pos[0] + 2 * pos[1] + pos[2]
    x2d = x[0]
    tgt2d = loss_target[0]

    small = jnp.concatenate([meta_tokens, _pad_rows(conv_a_w[0], 32), _pad_rows(conv_b_w[0], SUBLANES)], axis=0)
    final_g2 = final_g.reshape(1, D_MODEL)

    w_out_shards = [w[0].astype(BF16) for w in (w_a_out, w_b_out, w_out)]
    h_t, proj, meta_tile, small_params, w_in_all, *w_out_all = _gather_norm_proj(
        pos, x2d, small[None], norm_g, w_in[0].astype(BF16), w_out_shards, 3)
    small_params = small_params.transpose(1, 0, 2).reshape(small.shape[0], D_MODEL)
    conv_a_full, conv_b_full = small_params[N_META:N_META + 32], small_params[N_META + 32:]
    w_out_all = [w.reshape(D_MODEL, D_MODEL) for w in w_out_all]
    w_out_all_t = [w.T for w in w_out_all]
    dproj, ds1, lhs, rhs, small_a = _fused_pass(
        proj, x2d, tgt2d, meta_tile, conv_a_full, conv_a_b, ln_a_g, ln_a_b, b_a_out, conv_b_full, final_g2,
        w_out_all[0], w_out_all[1], w_out_all[2], w_out_all_t[0], w_out_all_t[1], w_out_all_t[2], n_tiles)
    k_tile = tp // 3
    gw_out = _grad_w_out(lhs, rhs, k_tile).reshape(N_DEV, 3 * ROWS_OUT, D_MODEL)
    gw_far, their_out, small_a_all = _grad_w_in_half(
        pos, h_t, dproj, k_tile, True, [("sibling", (gw_out,)), ("all", (small_a[None],))], "grad_w_in_far")
    gw_near, _, land_out, their_in = _grad_w_in_half(
        pos, h_t, dproj, k_tile, False, [("sibling_half", (gw_far,))], "grad_w_in_near",
        partials=(gw_out, their_out, ROWS_OUT))
    grad_x, small_b_all, _, land_in = _input_bwd(dproj, ds1, x2d, meta_tile, norm_g, w_in_all, min(256, seq),
                                                 gw_near, their_in, 256)

    small_grads = [small_a_all, small_b_all]
    small_cols = [lax.dynamic_slice_in_dim(g, me * LANES, LANES, axis=2) for g in small_grads]

    (res_in,) = _adamw_sharded(pos, gw_near, their_in, land_in, [(w_in[0], m_w_in[0], v_w_in[0])], 128, "adamw_w_in")
    res_out = _adamw_sharded(
        pos, gw_out, their_out, land_out,
        [(w_a_out[0], m_w_a_out[0], v_w_a_out[0]), (w_b_out[0], m_w_b_out[0], v_w_b_out[0]),
         (w_out[0], m_w_out[0], v_w_out[0])], ROWS_OUT, "adamw_w_out")
    params = [
        (1, ROW_META, N_META, True, meta_tokens, m_meta_tokens, v_meta_tokens),
        (1, ROW_NORM_G, 1, False, norm_g, m_norm_g, v_norm_g),
        (0, ROW_CONV_A_W, CONV_A, True, conv_a_w[0], m_conv_a_w[0], v_conv_a_w[0]),
        (0, ROW_CONV_A_B, 1, False, conv_a_b, m_conv_a_b, v_conv_a_b),
        (0, ROW_LN_G, 1, False, ln_a_g, m_ln_a_g, v_ln_a_g),
        (0, ROW_LN_B, 1, False, ln_a_b, m_ln_a_b, v_ln_a_b),
        (0, ROW_B_A_OUT, 1, False, b_a_out, m_b_a_out, v_b_a_out),
        (0, ROW_CONV_B_W, CONV_B, True, conv_b_w[0], m_conv_b_w[0], v_conv_b_w[0]),
        (0, ROW_FINAL_G, 1, False, final_g2, m_final_g.reshape(1, D_MODEL), v_final_g.reshape(1, D_MODEL)),
    ]
    res_small = _adamw_small(small_grads, small_cols, params)
    loss = res_small[-1][0, 0]

    def small_res(p, kind, shape):
        return res_small[4 * p + kind].reshape(shape)

    per_weight = []
    for kind in range(4):
        per_weight.append([
            small_res(0, kind, meta_tokens.shape),
            small_res(1, kind, norm_g.shape),
            res_in[kind].reshape(w_in.shape),
            small_res(2, kind, conv_a_w.shape),
            small_res(3, kind, conv_a_b.shape),
            small_res(4, kind, ln_a_g.shape),
            small_res(5, kind, ln_a_b.shape),
            res_out[0][kind].reshape(w_a_out.shape),
            small_res(6, kind, b_a_out.shape),
            small_res(7, kind, conv_b_w.shape),
            res_out[1][kind].reshape(w_b_out.shape),
            res_out[2][kind].reshape(w_out.shape),
            small_res(8, kind, final_g.shape),
        ])
    return (loss, grad_x.reshape(x.shape), *per_weight[0], *per_weight[1], *per_weight[2], *per_weight[3])
```

```python
import functools

import jax
import jax.numpy as jnp
from jax import lax
from jax.experimental import pallas as pl
from jax.experimental.pallas import tpu as pltpu

D_MODEL = 1024
N_META = 16
N_DEV = 8
D_IN = 9 * D_MODEL
COLS = D_IN // N_DEV
ROWS_OUT = D_MODEL // N_DEV
CONV_A = 31
CONV_B = 3
EPS = 1e-6

ADAM_LR = 0.001
ADAM_B1 = 0.9
ADAM_B2 = 0.999
ADAM_EPS = 1e-08
ADAM_WD = 0.01
ADAM_STEP = 10

TILE = 128
LANES = 128
N_CHUNK = D_MODEL // LANES
HALO = 32
SUBLANES = 8
VMEM_LIMIT = 56 * 1024 * 1024

ROW_FINAL_G, ROW_B_A_OUT, ROW_LN_G, ROW_LN_B, ROW_CONV_A_B, ROW_LOSS = 0, 1, 2, 3, 4, 5
ROW_CONV_A_W, ROW_CONV_B_W, SMALL_A_ROWS = 8, 40, 48
ROW_NORM_G, ROW_META, SMALL_B_ROWS = 0, 8, 24

MESH = pl.DeviceIdType.MESH
_ANY = pl.BlockSpec(memory_space=pl.ANY)
_VMEM = pl.BlockSpec(memory_space=pltpu.VMEM)
BF16 = jnp.bfloat16
F32 = jnp.float32


def _resident(shape):
    return pl.BlockSpec(shape, lambda *_: (0,) * len(shape), pipeline_mode=pl.Buffered(1))


def _sigmoid(v):
    return jax.nn.sigmoid(v)


def _dot(a, b):
    return jnp.dot(a, b, preferred_element_type=F32)


def _dot_nt(a, b):
    return lax.dot_general(a, b, (((1,), (1,)), ((), ())), preferred_element_type=F32)


def _dot_tn(a, b):
    return lax.dot_general(a, b, (((0,), (0,)), ((), ())), preferred_element_type=F32)


def _colsum(v):
    return jnp.sum(v, axis=0, keepdims=True)


def _rowmean(v):
    parts = [v[:, LANES * c:LANES * (c + 1)] for c in range(v.shape[1] // LANES)]
    return jnp.sum(functools.reduce(jnp.add, parts), axis=-1, keepdims=True) * (1.0 / v.shape[1])


def _fold8(v):
    parts = [v[SUBLANES * g:SUBLANES * (g + 1)] for g in range(v.shape[0] // SUBLANES)]
    return functools.reduce(jnp.add, parts)


def _sibling_copies(srcs, dsts, send_sems, recv_sems):
    x, y, c = lax.axis_index("x"), lax.axis_index("y"), lax.axis_index("c")
    return [pltpu.make_async_remote_copy(
        src_ref=src.at[2 * q + (1 - c)], dst_ref=dst.at[q],
        send_sem=send_sems.at[4 * a + q], recv_sem=recv_sems.at[4 * a + q],
        device_id=(x, y, 1 - c), device_id_type=MESH)
        for a, (src, dst) in enumerate(zip(srcs, dsts)) for q in range(4)]


def _chip_copies(srcs, dsts, send_sems, recv_sems):
    x, y, c = lax.axis_index("x"), lax.axis_index("y"), lax.axis_index("c")
    targets = [(x, 1 - y, c), (1 - x, y, c), (1 - x, 1 - y, c)]
    return [pltpu.make_async_remote_copy(
        src_ref=src.at[k], dst_ref=dst.at[k],
        send_sem=send_sems.at[3 * a + k], recv_sem=recv_sems.at[3 * a + k],
        device_id=targets[k], device_id_type=MESH)
        for a, (src, dst) in enumerate(zip(srcs, dsts)) for k in range(3)]


def _sibling_half_copies(srcs, dsts, send_sems, recv_sems):
    x, y, c = lax.axis_index("x"), lax.axis_index("y"), lax.axis_index("c")
    return [pltpu.make_async_remote_copy(
        src_ref=src.at[q], dst_ref=dst.at[q],
        send_sem=send_sems.at[4 * a + q], recv_sem=recv_sems.at[4 * a + q],
        device_id=(x, y, 1 - c), device_id_type=MESH)
        for a, (src, dst) in enumerate(zip(srcs, dsts)) for q in range(4)]


def _all_copies(srcs, dsts, send_sems, recv_sems):
    x, y, c = lax.axis_index("x"), lax.axis_index("y"), lax.axis_index("c")
    mine = 4 * x + 2 * y + c
    copies = []
    for a, (src, dst) in enumerate(zip(srcs, dsts)):
        copies.append(pltpu.make_async_copy(src.at[0], dst.at[mine], send_sems.at[N_DEV * a]))
        for k in range(1, N_DEV):
            copies.append(pltpu.make_async_remote_copy(
                src_ref=src.at[0], dst_ref=dst.at[mine],
                send_sem=send_sems.at[N_DEV * a + k], recv_sem=recv_sems.at[N_DEV * a + k],
                device_id=(x ^ (k >> 2), y ^ ((k >> 1) & 1), c ^ (k & 1)), device_id_type=MESH))
    return copies


_EXCHANGES = {"sibling": (4, _sibling_copies, 4), "sibling_half": (4, _sibling_half_copies, 4),
              "chips": (3, _chip_copies, 3), "all": (N_DEV, _all_copies, N_DEV)}


def _exchange_shapes(kind, arrays):
    per_array, _, slots = _EXCHANGES[kind]
    out_shape = [jax.ShapeDtypeStruct((slots,) + a.shape[1:], a.dtype) for a in arrays]
    sems = [pltpu.SemaphoreType.DMA((per_array * len(arrays),))] * 2
    return out_shape, sems


def _ride_shapes(rides):
    shapes, sems = [], []
    for kind, arrays in rides:
        ride_shapes, ride_sems = _exchange_shapes(kind, arrays)
        shapes += ride_shapes
        sems += ride_sems
    return shapes, sems


def _riding(body, n_in, n_out, rides, is_first, is_last):
    counts = [len(arrays) for _, arrays in rides]
    n_arr = sum(counts)

    def wrapped(*refs):
        ins, srcs = refs[:n_in], refs[n_in:n_in + n_arr]
        outs = refs[n_in + n_arr:n_in + n_arr + n_out]
        dsts = refs[n_in + n_arr + n_out:n_in + 2 * n_arr + n_out]
        first_sem = len(refs) - 2 * len(rides)
        scratch, sems = refs[n_in + 2 * n_arr + n_out:first_sem], refs[first_sem:]

        def copies():
            made, at = [], 0
            for r, ((kind, _), n) in enumerate(zip(rides, counts)):
                made += _EXCHANGES[kind][1](srcs[at:at + n], dsts[at:at + n], sems[2 * r], sems[2 * r + 1])
                at += n
            return made

        @pl.when(is_first())
        def _():
            for cp in copies():
                cp.start()

        body(*ins, *outs, *scratch)

        @pl.when(is_last())
        def _():
            for cp in copies():
                cp.wait()

    return wrapped


def _chip_partial(pos, mine, theirs, relations, out_dtype, row_tile, name):
    n_slots, m, n = mine.shape
    q0 = relations[0]

    def chip_of(qi, pos_ref):
        q = qi + q0
        return pos_ref[0] ^ (q >> 1), pos_ref[1] ^ (q & 1)

    def mine_map(qi, t, pos_ref):
        px, py = chip_of(qi, pos_ref)
        return (4 * px + 2 * py + pos_ref[2] if n_slots == N_DEV else 2 * px + py), t, 0

    def theirs_map(qi, t, pos_ref):
        px, py = chip_of(qi, pos_ref)
        return 2 * px + py, t, 0

    def body(pos_ref, a_ref, b_ref, o_ref):
        o_ref[...] = (a_ref[...] + b_ref[...]).astype(out_dtype)

    return pl.pallas_call(
        body, name=name,
        out_shape=jax.ShapeDtypeStruct((len(relations), m, n), out_dtype),
        grid_spec=pltpu.PrefetchScalarGridSpec(
            num_scalar_prefetch=1, grid=(len(relations), m // row_tile),
            in_specs=[pl.BlockSpec((None, row_tile, n), mine_map), pl.BlockSpec((None, row_tile, n), theirs_map)],
            out_specs=pl.BlockSpec((None, row_tile, n), lambda qi, t, pos_ref: (qi, t, 0))),
        compiler_params=pltpu.CompilerParams(dimension_semantics=("arbitrary", "arbitrary")),
    )(pos, mine, theirs)


PARTS = ((0, 512), (512, 640))


def _gather_norm_proj(pos, x2d, small_shard, norm_g, w_in_shard, w_out_shards, n_chunk):
    seq = x2d.shape[0]
    n_tiles = seq // TILE + 1
    tp = n_tiles * TILE
    n_parts = len(PARTS)
    widest = max(width for _, width in PARTS)
    units = [(s, u) for s in range(2) for u in range(n_parts)]
    for first in (2, 5):
        units += [(first + j, u) for u in range(n_parts) for j in range(2)] + [(first + 2, u) for u in range(n_parts)]
    n_units = len(units)
    n_steps = n_tiles + n_units
    chunk = tp // n_chunk

    def body(pos_ref, x_ref, g_ref, small_ref, win_ref, wa_ref, wb_ref, wo_ref,
             ht_ref, proj_ref, meta_ref, small_all, win_all, wa_all, wb_all, wo_all,
             h_all, wbuf, rbuf, small_buf, send_sems, recv_sems, local_sems, small_send, small_recv):
        g = pl.program_id(0)
        x, y, c = lax.axis_index("x"), lax.axis_index("y"), lax.axis_index("c")
        me, sibling = (x, y, c), (x, y, 1 - c)
        chips = [(1 - x, y), (x, 1 - y), (1 - x, 1 - y)]
        shards = (win_ref, wa_ref, wb_ref, wo_ref)
        gathered = (win_all, wa_all, wb_all, wo_all)
        n_arrays = len(shards)
        blocks = [me, sibling] + [(*chip, c) for chip in chips] + [(*chip, 1 - c) for chip in chips]

        def index(block):
            px, py, pc = block
            return 4 * px + 2 * py + pc

        def part(ref, a, u):
            return ref.at[:, pl.ds(PARTS[u][0], PARTS[u][1])] if a == 0 else ref

        def slot(a, block, u):
            return part(gathered[a].at[index(block)], a, u)

        def sem(a, k, u):
            return n_parts * k + u if a == 0 else 7 * n_parts + 7 * (a - 1) + k

        def copy(a, k, block, to, u=0, from_shard=False):
            return pltpu.make_async_remote_copy(
                src_ref=part(shards[a], a, u) if from_shard else slot(a, block, u), dst_ref=slot(a, block, u),
                send_sem=send_sems.at[sem(a, k, u)], recv_sem=recv_sems.at[sem(a, k, u)],
                device_id=to, device_id_type=MESH)

        def keep(a):
            return pltpu.make_async_copy(shards[a], gathered[a].at[index(me)], local_sems.at[a])

        def load(m):
            s, u = units[m]
            src = part(win_ref, 0, u) if s == 0 else slot(0, blocks[s], u)
            return pltpu.make_async_copy(src, wbuf.at[m % 2, :, 0:PARTS[u][1]], local_sems.at[n_arrays + m % 2])

        def store(m):
            s, u = units[m]
            col0 = pl.multiple_of(index(blocks[s]) * COLS + PARTS[u][0], LANES)
            return pltpu.make_async_copy(rbuf.at[m % 2, :, 0:PARTS[u][1]],
                                         proj_ref.at[:, pl.ds(col0, PARTS[u][1])], local_sems.at[n_arrays + 2 + m % 2])

        def by_x(a, u):
            return u == 0 if a == 0 else a < 3

        def relay(a, u=0):
            src, to = (blocks[3], blocks[2]) if by_x(a, u) else (blocks[2], blocks[3])
            return copy(a, 3, src, to, u)

        def arrive(m):
            s, u = units[m]
            if s == 1:
                copy(0, 0, sibling, me, u).wait_recv()
            elif 2 <= s <= 4:
                copy(0, s - 1, blocks[s], me, u).wait_recv()
                copy(0, s + 2, blocks[s], sibling, u).start()
                if s < 4 and by_x(0, u) == (s == 3):
                    relay(0, u).start()
            elif s >= 5:
                copy(0, s - 1, blocks[s], me, u).wait_recv()
                if u == 0:
                    for a in range(1, 4):
                        copy(a, s - 4, blocks[s - 3], me).wait_recv()
                        copy(a, s - 1, blocks[s - 3], sibling).start()
                        if s < 7 and by_x(a, 0) == (s == 6):
                            relay(a).start()

        targets = [sibling, blocks[2], blocks[3]]

        def small_copies():
            return _all_copies([small_ref], [small_all], small_send, small_recv)

        @pl.when(g == 0)
        def _():
            for cp in small_copies():
                cp.start()
            for a in range(n_arrays):
                keep(a).start()
            for u in range(n_parts):
                for k, to in enumerate(targets):
                    copy(0, k, me, to, u, from_shard=True).start()
            for a in range(1, 4):
                for k, to in enumerate(targets):
                    copy(a, k, me, to, from_shard=True).start()
            load(0).start()

        @pl.when(g == n_tiles - 2)
        def _():
            for cp in small_copies():
                cp.wait()
            fetch = pltpu.make_async_copy(small_all, small_buf, local_sems.at[n_arrays + 4])
            fetch.start()
            fetch.wait()
            meta_ref[0:TILE - N_META, :] = jnp.zeros((TILE - N_META, D_MODEL), F32)
            meta_ref[TILE - N_META:TILE, :] = jnp.concatenate([small_buf[d, 0:N_META, :] for d in range(N_DEV)], axis=1)

        @pl.when(g < n_tiles)
        def _():
            s0 = jnp.where(g == n_tiles - 1, meta_ref[...], x_ref[...])
            r = lax.rsqrt(_rowmean(s0 * s0) + EPS)
            h32 = (s0 * r) * g_ref[...]
            ht_ref[...] = h32.T.astype(BF16)
            h_all[pl.ds(pl.multiple_of(g * TILE, TILE), TILE), :] = h32.astype(BF16)

        for m in range(n_units):
            @pl.when(g == n_tiles + m)
            def _(m=m):
                load(m).wait()
                if m + 1 < n_units:
                    arrive(m + 1)
                    load(m + 1).start()
                if m >= 2:
                    store(m - 2).wait()

        m_now = jnp.maximum(g - n_tiles, 0)
        u_now = functools.reduce(jnp.add, [jnp.where(m_now == m, u, 0) for m, (_, u) in enumerate(units)])
        for u, (_, width) in enumerate(PARTS):
            @pl.when((g >= n_tiles) & (u_now == u))
            def _(width=width):
                w = wbuf[m_now % 2, :, 0:width]
                for r in range(n_chunk):
                    rbuf[m_now % 2, r * chunk:(r + 1) * chunk, 0:width] = _dot(h_all[r * chunk:(r + 1) * chunk, :], w)

        for m in range(n_units):
            @pl.when(g == n_tiles + m)
            def _(m=m):
                store(m).start()

        @pl.when(g == n_steps - 1)
        def _():
            store(n_units - 2).wait()
            store(n_units - 1).wait()
            for a in range(1, 4):
                copy(a, 0, sibling, me).wait_recv()
                for j in range(3):
                    copy(a, 4 + j, blocks[5 + j], me).wait_recv()
            for a in range(n_arrays):
                for u in range(n_parts if a == 0 else 1):
                    for k, to in enumerate(targets):
                        copy(a, k, me, to, u, from_shard=True).wait_send()
                    relay(a, u).wait_send()
                    for j in range(3):
                        copy(a, 4 + j, blocks[2 + j], sibling, u).wait_send()
                keep(a).wait()

    n_x = n_tiles - 1
    return pl.pallas_call(
        body, name="gather_norm_proj",
        out_shape=[jax.ShapeDtypeStruct((D_MODEL, tp), BF16), jax.ShapeDtypeStruct((tp, D_IN), F32),
                   jax.ShapeDtypeStruct((TILE, D_MODEL), F32), jax.ShapeDtypeStruct((N_DEV,) + small_shard.shape[1:], F32),
                   jax.ShapeDtypeStruct((N_DEV,) + w_in_shard.shape, BF16)]
                  + [jax.ShapeDtypeStruct((N_DEV,) + w.shape, BF16) for w in w_out_shards],
        grid_spec=pltpu.PrefetchScalarGridSpec(
            num_scalar_prefetch=1, grid=(n_steps,),
            in_specs=[pl.BlockSpec((TILE, D_MODEL), lambda g, pos_ref: (jnp.minimum(g, n_x - 1), 0)),
                      _VMEM, _ANY, _ANY, _ANY, _ANY, _ANY],
            out_specs=[pl.BlockSpec((D_MODEL, TILE), lambda g, pos_ref: (0, jnp.minimum(g, n_tiles - 1))),
                       _ANY, _VMEM, _ANY, _ANY, _ANY, _ANY, _ANY],
            scratch_shapes=[pltpu.VMEM((tp, D_MODEL), BF16), pltpu.VMEM((2, D_MODEL, widest), BF16),
                            pltpu.VMEM((2, tp, widest), F32), pltpu.VMEM((N_DEV,) + small_shard.shape[1:], F32),
                            pltpu.SemaphoreType.DMA((7 * n_parts + 21,)), pltpu.SemaphoreType.DMA((7 * n_parts + 21,)),
                            pltpu.SemaphoreType.DMA((9,)),
                            pltpu.SemaphoreType.DMA((N_DEV,)), pltpu.SemaphoreType.DMA((N_DEV,))]),
        compiler_params=pltpu.CompilerParams(dimension_semantics=("arbitrary",), vmem_limit_bytes=VMEM_LIMIT),
    )(pos, x2d, norm_g, small_shard, w_in_shard, *w_out_shards)


C_AVAL, C_AGLU, C_AZ, C_BB, C_BC, C_BX, C_BZ, C_GA, C_GB = (k * D_MODEL for k in range(9))
S_AZ, S_BB, S_BZ, S_GA, S_GB = (k * D_MODEL for k in range(5))


def _fused_pass(proj, x2d, tgt2d, meta_tile, conv_a_w, conv_a_b, ln_a_g, ln_a_b, b_a_out, conv_b_w, final_g,
                w_a, w_b, w_o, w_a_t, w_b_t, w_o_t, n_tiles):
    T = TILE
    tp = n_tiles * T
    inv_d = 1.0 / D_MODEL

    def block_of(tile):
        return jnp.where(tile == 0, n_tiles - 1, tile - 1)

    def cur(i):
        return block_of(jnp.minimum(i, n_tiles - 1))

    def prev(i):
        return block_of(jnp.clip(i - 1, 0, n_tiles - 1))

    def xblk(i):
        return jnp.maximum(jnp.minimum(i, n_tiles - 1) - 1, 0)

    def body(proj_ref, aprev, cprev, x_ref, tgt_ref, meta_ref, caw_ref, cab_ref, lng_ref, lnb_ref, bao_ref, cbw_ref,
             fg_ref, wa_ref, wb_ref, wo_ref, wat_ref, wbt_ref, wot_ref,
             dproj_ref, ds1_ref, lhs_ref, rhs_ref, small_ref,
             ua0_buf, cb_buf, dua1_buf, dc3_buf, stage, ua1_buf, c3_buf,
             dpa_buf, dpb_buf, dcaw8, dcbw8, shift_buf):
        i = pl.program_id(0)
        this, before = i % 2, 1 - i % 2

        @pl.when(i == 0)
        def _init():
            for buf in (ua0_buf, cb_buf, dua1_buf, dc3_buf, dcaw8, dcbw8):
                buf[...] = jnp.zeros(buf.shape, buf.dtype)
            small_ref[...] = jnp.zeros(small_ref.shape, F32)

        @pl.when(i >= 1)
        def _emit_stage():
            dproj_ref[:, C_AZ:C_BC] = stage[:, S_AZ:S_BZ]
            dproj_ref[:, C_BZ:D_IN] = stage[:, S_BZ:S_GB + D_MODEL]

        @pl.when(i < n_tiles)
        def _front():
            def conv_chunk(cc, carry):
                c0 = pl.multiple_of(cc * LANES, LANES)
                lanes = pl.ds(c0, LANES)

                def col(base):
                    return pl.ds(pl.multiple_of(base + cc * LANES, LANES), LANES)

                ua0 = proj_ref[:, col(C_AVAL)] * _sigmoid(proj_ref[:, col(C_AGLU)])
                ua0_buf[this, 0:HALO, lanes] = ua0_buf[before, T:T + HALO, lanes]
                ua0_buf[this, HALO:HALO + T, lanes] = ua0
                acc = jnp.broadcast_to(cab_ref[:, lanes], (T, LANES))
                lead = HALO - (CONV_A - 1)
                for r in range(SUBLANES):
                    taps = [k for k in range(CONV_A) if (k + lead) % SUBLANES == r]
                    rows = T + SUBLANES * max((k + lead) // SUBLANES for k in taps)
                    if r:
                        shift_buf[r, 0:rows, :] = ua0_buf[this, pl.ds(r, rows), lanes]
                    for k in taps:
                        q = (k + lead) // SUBLANES
                        if r:
                            win = shift_buf[r, SUBLANES * q:SUBLANES * q + T, :]
                        else:
                            win = ua0_buf[this, pl.ds(SUBLANES * q, T), lanes]
                        acc = acc + caw_ref[k:k + 1, lanes] * win
                ua1_buf[:, lanes] = acc
                cb = proj_ref[:, col(C_BC)] * proj_ref[:, col(C_BX)]
                cb_buf[this, 0:SUBLANES, lanes] = cb_buf[before, T:T + SUBLANES, lanes]
                cb_buf[this, SUBLANES:SUBLANES + T, lanes] = cb
                lead_b = SUBLANES - (CONV_B - 1)
                acc3 = cbw_ref[0:1, lanes] * cb_buf[this, pl.ds(lead_b, T), lanes]
                for k in range(1, CONV_B):
                    acc3 = acc3 + cbw_ref[k:k + 1, lanes] * cb_buf[this, pl.ds(lead_b + k, T), lanes]
                c3_buf[:, lanes] = acc3
                return carry

            lax.fori_loop(0, N_CHUNK, conv_chunk, 0)

            ua1 = ua1_buf[...]
            xc = ua1 - _rowmean(ua1)
            rstd = lax.rsqrt(_rowmean(xc * xc) + EPS)
            xhat = xc * rstd
            ua2 = xhat * lng_ref[...] + lnb_ref[...]
            sg2 = _sigmoid(ua2)
            ua3 = ua2 * sg2
            a_z = proj_ref[:, C_AZ:C_AZ + D_MODEL]
            sz = _sigmoid(a_z)
            silu_az = a_z * sz
            lhs_ref[0] = (ua3 * silu_az).astype(BF16)
            b_z = proj_ref[:, C_BZ:C_BZ + D_MODEL]
            sbz = _sigmoid(b_z)
            silu_bz = b_z * sbz
            b_b = proj_ref[:, C_BB:C_BB + D_MODEL]
            c3 = c3_buf[...]
            ub = b_b * c3
            lhs_ref[1] = (ub * silu_bz).astype(BF16)

            ya = _dot(lhs_ref[0], wa_ref[...]) + bao_ref[...]
            yb = _dot(lhs_ref[1], wb_ref[...])
            sga = _sigmoid(proj_ref[:, C_GA:C_GA + D_MODEL])
            sgb = _sigmoid(proj_ref[:, C_GB:C_GB + D_MODEL])
            m_b = (sga * ya + sgb * yb).astype(BF16)
            lhs_ref[2] = m_b
            s0 = jnp.where(i == 0, meta_ref[...], x_ref[...])
            s1 = s0 + _dot(m_b, wo_ref[...])
            r1 = lax.rsqrt(_rowmean(s1 * s1) + EPS)
            y = (s1 * r1) * fg_ref[...]
            is_token = (i >= 1).astype(F32)
            err = (y - tgt_ref[...]) * is_token
            small_ref[ROW_LOSS:ROW_LOSS + 1, :] += (0.5 * inv_d) * _colsum(err * err)
            dy = err * inv_d
            small_ref[ROW_FINAL_G:ROW_FINAL_G + 1, :] += _colsum(dy * (s1 * r1))
            gy = dy * fg_ref[...]
            ds1 = r1 * gy - s1 * ((r1 * r1 * r1) * _rowmean(gy * s1))
            ds1_ref[...] = ds1
            ds1_b = ds1.astype(BF16)
            rhs_ref[2] = ds1_b
            dm = _dot(ds1_b, wot_ref[...])
            dya = dm * sga
            dyb = dm * sgb
            stage[:, S_GA:S_GA + D_MODEL] = (dya * ya * (1.0 - sga)).astype(BF16)
            stage[:, S_GB:S_GB + D_MODEL] = (dyb * yb * (1.0 - sgb)).astype(BF16)
            small_ref[ROW_B_A_OUT:ROW_B_A_OUT + 1, :] += _colsum(dya)
            dya_b = dya.astype(BF16)
            dyb_b = dyb.astype(BF16)
            rhs_ref[0] = dya_b
            rhs_ref[1] = dyb_b
            dpa_buf[...] = _dot(dya_b, wat_ref[...])
            dpb_buf[...] = _dot(dyb_b, wbt_ref[...])

            dpa = dpa_buf[...]
            stage[:, S_AZ:S_AZ + D_MODEL] = (dpa * ua3 * (sz + silu_az * (1.0 - sz))).astype(BF16)
            dua2 = dpa * silu_az * (sg2 + ua3 * (1.0 - sg2))
            small_ref[ROW_LN_G:ROW_LN_G + 1, :] += _colsum(dua2 * xhat)
            small_ref[ROW_LN_B:ROW_LN_B + 1, :] += _colsum(dua2)
            dxh = dua2 * lng_ref[...]
            dua1 = rstd * (dxh - _rowmean(dxh) - xhat * _rowmean(dxh * xhat))
            small_ref[ROW_CONV_A_B:ROW_CONV_A_B + 1, :] += _colsum(dua1)
            dua1_buf[this, 0:T, :] = dua1
            dua1_buf[before, T:T + HALO, :] = dua1[0:HALO]
            dpb = dpb_buf[...]
            stage[:, S_BZ:S_BZ + D_MODEL] = (dpb * ub * (sbz + silu_bz * (1.0 - sbz))).astype(BF16)
            dub = dpb * silu_bz
            stage[:, S_BB:S_BB + D_MODEL] = (dub * c3).astype(BF16)
            dc3 = dub * b_b
            dc3_buf[this, 0:T, :] = dc3
            dc3_buf[before, T:T + SUBLANES, :] = dc3[0:SUBLANES]

        @pl.when(i == n_tiles)
        def _no_later_tile():
            dua1_buf[before, T:T + HALO, :] = jnp.zeros((HALO, D_MODEL), F32)
            dc3_buf[before, T:T + SUBLANES, :] = jnp.zeros((SUBLANES, D_MODEL), F32)

        @pl.when(i >= 1)
        def _lagged():
            def convt_chunk(cc, carry):
                c0 = pl.multiple_of(cc * LANES, LANES)
                lanes = pl.ds(c0, LANES)

                def col(base):
                    return pl.ds(pl.multiple_of(base + cc * LANES, LANES), LANES)

                ua0 = ua0_buf[before, HALO:HALO + T, lanes]
                acc = jnp.zeros((T, LANES), F32)
                for r in range(SUBLANES):
                    shifts = [j for j in range(CONV_A) if j % SUBLANES == r]
                    rows = T + shifts[-1] - r
                    if r:
                        shift_buf[r, 0:rows, :] = dua1_buf[before, pl.ds(r, rows), lanes]
                    for j in shifts:
                        k = CONV_A - 1 - j
                        if r:
                            later = shift_buf[r, j - r:j - r + T, :]
                        else:
                            later = dua1_buf[before, pl.ds(j, T), lanes]
                        acc = acc + caw_ref[k:k + 1, lanes] * later
                        dcaw8[SUBLANES * k:SUBLANES * (k + 1), lanes] += _fold8(ua0 * later)
                a_val = aprev[:, col(0)]
                sg = _sigmoid(aprev[:, col(D_MODEL)])
                dproj_ref[:, col(C_AVAL)] = (acc * sg).astype(BF16)
                dproj_ref[:, col(C_AGLU)] = (acc * a_val * (sg * (1.0 - sg))).astype(BF16)

                cb = cb_buf[before, SUBLANES:SUBLANES + T, lanes]
                acc3 = jnp.zeros((T, LANES), F32)
                for j in range(CONV_B):
                    k = CONV_B - 1 - j
                    later = dc3_buf[before, pl.ds(j, T), lanes]
                    acc3 = acc3 + cbw_ref[k:k + 1, lanes] * later
                    dcbw8[SUBLANES * k:SUBLANES * (k + 1), lanes] += _fold8(cb * later)
                dproj_ref[:, col(C_BC)] = (acc3 * cprev[:, col(D_MODEL)]).astype(BF16)
                dproj_ref[:, col(C_BX)] = (acc3 * cprev[:, col(0)]).astype(BF16)
                return carry

            lax.fori_loop(0, N_CHUNK, convt_chunk, 0)

        @pl.when(i == n_tiles)
        def _finish():
            for k in range(CONV_A):
                small_ref[ROW_CONV_A_W + k:ROW_CONV_A_W + k + 1, :] = _colsum(dcaw8[SUBLANES * k:SUBLANES * (k + 1), :])
            for k in range(CONV_B):
                small_ref[ROW_CONV_B_W + k:ROW_CONV_B_W + k + 1, :] = _colsum(dcbw8[SUBLANES * k:SUBLANES * (k + 1), :])

    pair = 2 * D_MODEL
    return pl.pallas_call(
        body, name="fused_pass", grid=(n_tiles + 1,),
        out_shape=[
            jax.ShapeDtypeStruct((tp, D_IN), BF16),
            jax.ShapeDtypeStruct((tp, D_MODEL), F32),
            jax.ShapeDtypeStruct((3, tp, D_MODEL), BF16),
            jax.ShapeDtypeStruct((3, tp, D_MODEL), BF16),
            jax.ShapeDtypeStruct((SMALL_A_ROWS, D_MODEL), F32),
        ],
        in_specs=[
            pl.BlockSpec((T, D_IN), lambda i: (cur(i), 0)),
            pl.BlockSpec((T, pair), lambda i: (prev(i), C_AVAL // pair)),
            pl.BlockSpec((T, pair), lambda i: (prev(i), C_BC // pair)),
            pl.BlockSpec((T, D_MODEL), lambda i: (xblk(i), 0)),
            pl.BlockSpec((T, D_MODEL), lambda i: (xblk(i), 0)),
            _VMEM, _VMEM, _VMEM, _VMEM, _VMEM, _VMEM, _VMEM, _VMEM,
            *[_resident((D_MODEL, D_MODEL)) for _ in range(6)],
        ],
        out_specs=[
            pl.BlockSpec((T, D_IN), lambda i: (prev(i), 0)),
            pl.BlockSpec((T, D_MODEL), lambda i: (cur(i), 0)),
            pl.BlockSpec((3, T, D_MODEL), lambda i: (0, cur(i), 0)),
            pl.BlockSpec((3, T, D_MODEL), lambda i: (0, cur(i), 0)),
            _VMEM,
        ],
        scratch_shapes=[
            pltpu.VMEM((2, HALO + T, D_MODEL), F32),
            pltpu.VMEM((2, SUBLANES + T, D_MODEL), F32),
            pltpu.VMEM((2, T + HALO, D_MODEL), F32),
            pltpu.VMEM((2, T + SUBLANES, D_MODEL), F32),
            pltpu.VMEM((T, 5 * D_MODEL), BF16),
            pltpu.VMEM((T, D_MODEL), F32),
            pltpu.VMEM((T, D_MODEL), F32),
            pltpu.VMEM((T, D_MODEL), F32),
            pltpu.VMEM((T, D_MODEL), F32),
            pltpu.VMEM((32 * SUBLANES, D_MODEL), F32),
            pltpu.VMEM((SUBLANES * SUBLANES, D_MODEL), F32),
            pltpu.VMEM((SUBLANES, T + HALO, LANES), F32),
        ],
        compiler_params=pltpu.CompilerParams(dimension_semantics=("arbitrary",), vmem_limit_bytes=VMEM_LIMIT),
    )(proj, proj, proj, x2d, tgt2d, meta_tile, conv_a_w, conv_a_b, ln_a_g, ln_a_b, b_a_out, conv_b_w, final_g,
      w_a, w_b, w_o, w_a_t, w_b_t, w_o_t)


def _input_bwd(dproj, ds1, x2d, meta_tile, norm_g, w_in_all, row_tile, ride):
    seq = x2d.shape[0]
    n_steps = seq // row_tile
    meta_block = seq // TILE

    def backward(dp_ref, ds1_ref, s0_ref, g_ref, w_ref, out_ref, vec_ref):
        dh = _dot_nt(dp_ref[:, 0:COLS], w_ref[0])
        for j in range(1, N_DEV):
            dh = dh + _dot_nt(dp_ref[:, j * COLS:(j + 1) * COLS], w_ref[j])
        s0v = s0_ref[...]
        r = lax.rsqrt(_rowmean(s0v * s0v) + EPS)
        gh = dh * g_ref[...]
        out_ref[...] = ds1_ref[...] + r * gh - s0v * ((r * r * r) * _rowmean(gh * s0v))
        vec_ref[ROW_NORM_G:ROW_NORM_G + 1, :] += _colsum(dh * (s0v * r))

    def body(dp_ref, ds1_ref, x_ref, dpm_ref, ds1m_ref, meta_ref, g_ref, w_ref, gx_ref, small_all_ref,
             gmeta_buf, small_buf, send_sems, recv_sems):
        t = pl.program_id(0)
        small = small_buf.at[0]

        @pl.when(t == 0)
        def _():
            small[...] = jnp.zeros(small.shape, F32)

        backward(dp_ref, ds1_ref, x_ref, g_ref, w_ref, gx_ref, small)

        @pl.when(t == n_steps - 1)
        def _():
            backward(dpm_ref, ds1m_ref, meta_ref, g_ref, w_ref, gmeta_buf, small)
            small[ROW_META:ROW_META + N_META, :] = gmeta_buf[TILE - N_META:TILE, :]
            copies = _all_copies([small_buf], [small_all_ref], send_sems, recv_sems)
            for cp in copies:
                cp.start()
            for cp in copies:
                cp.wait()

    rides = [("chips", ride)]
    ride_shapes, ride_sems = _ride_shapes(rides)
    body = _riding(body, 8, 2, rides, lambda: pl.program_id(0) == 0, lambda: pl.program_id(0) == n_steps - 1)
    return pl.pallas_call(
        body, name="input_bwd", grid=(n_steps,),
        out_shape=[jax.ShapeDtypeStruct(x2d.shape, F32),
                   jax.ShapeDtypeStruct((N_DEV, SMALL_B_ROWS, D_MODEL), F32)] + ride_shapes,
        in_specs=[pl.BlockSpec((row_tile, D_IN), lambda t: (t, 0)),
                  pl.BlockSpec((row_tile, D_MODEL), lambda t: (t, 0)),
                  pl.BlockSpec((row_tile, D_MODEL), lambda t: (t, 0)),
                  pl.BlockSpec((TILE, D_IN), lambda t: (meta_block, 0)),
                  pl.BlockSpec((TILE, D_MODEL), lambda t: (meta_block, 0)),
                  _VMEM, _VMEM, _resident((N_DEV, D_MODEL, COLS))] + [_ANY] * len(ride),
        out_specs=[pl.BlockSpec((row_tile, D_MODEL), lambda t: (t, 0)), _ANY] + [_ANY] * len(ride),
        scratch_shapes=[pltpu.VMEM((TILE, D_MODEL), F32), pltpu.VMEM((1, SMALL_B_ROWS, D_MODEL), F32),
                        pltpu.SemaphoreType.DMA((N_DEV,)), pltpu.SemaphoreType.DMA((N_DEV,))] + ride_sems,
        compiler_params=pltpu.CompilerParams(dimension_semantics=("arbitrary",), vmem_limit_bytes=VMEM_LIMIT),
    )(dproj, ds1, x2d, dproj, ds1, meta_tile, norm_g, w_in_all, *ride)


def _grad_w_in_half(pos, h_t, dproj, k_tile, other_side, rides, name):
    tp = h_t.shape[1]
    n_k = tp // k_tile

    def column_block(q, k, pos_ref):
        return k, 2 * q + (1 - pos_ref[2] if other_side else pos_ref[2])

    def body(pos_ref, h_ref, dp_ref, o_ref):
        @pl.when(pl.program_id(1) == 0)
        def _():
            o_ref[...] = jnp.zeros(o_ref.shape, F32)

        o_ref[...] += _dot(h_ref[...], dp_ref[...])

    ride = [a for _, arrays in rides for a in arrays]
    n_arr = len(ride)
    ride_shapes, ride_sems = _ride_shapes(rides)
    body = _riding(body, 3, 1, rides,
                   lambda: (pl.program_id(0) == 0) & (pl.program_id(1) == 0),
                   lambda: (pl.program_id(0) == 3) & (pl.program_id(1) == n_k - 1))
    return pl.pallas_call(
        body, name=name,
        out_shape=[jax.ShapeDtypeStruct((4, D_MODEL, COLS), F32)] + ride_shapes,
        grid_spec=pltpu.PrefetchScalarGridSpec(
            num_scalar_prefetch=1, grid=(4, n_k),
            in_specs=[pl.BlockSpec((D_MODEL, k_tile), lambda q, k, pos_ref: (0, k)),
                      pl.BlockSpec((k_tile, COLS), column_block)] + [_ANY] * n_arr,
            out_specs=[pl.BlockSpec((None, D_MODEL, COLS), lambda q, k, pos_ref: (q, 0, 0))] + [_ANY] * n_arr,
            scratch_shapes=ride_sems),
        compiler_params=pltpu.CompilerParams(dimension_semantics=("arbitrary", "arbitrary"),
                                             vmem_limit_bytes=VMEM_LIMIT),
    )(pos, h_t, dproj, *ride)


def _grad_w_out(lhs, rhs, k_tile):
    tp = lhs.shape[1]

    def body(a_ref, b_ref, o_ref):
        @pl.when(pl.program_id(1) == 0)
        def _():
            o_ref[...] = jnp.zeros(o_ref.shape, F32)

        o_ref[...] += _dot_tn(a_ref[...], b_ref[...]).reshape(N_DEV, ROWS_OUT, D_MODEL)

    return pl.pallas_call(
        body, name="grad_w_out", grid=(3, tp // k_tile),
        out_shape=jax.ShapeDtypeStruct((N_DEV, 3, ROWS_OUT, D_MODEL), F32),
        in_specs=[pl.BlockSpec((None, k_tile, D_MODEL), lambda w, k: (w, k, 0)),
                  pl.BlockSpec((None, k_tile, D_MODEL), lambda w, k: (w, k, 0))],
        out_specs=pl.BlockSpec((N_DEV, None, ROWS_OUT, D_MODEL), lambda w, k: (0, w, 0, 0)),
        compiler_params=pltpu.CompilerParams(dimension_semantics=("arbitrary", "arbitrary"),
                                             vmem_limit_bytes=VMEM_LIMIT),
    )(lhs, rhs)


def _adamw_math(w, g, m, v):
    m = ADAM_B1 * m + (1.0 - ADAM_B1) * g
    v = ADAM_B2 * v + (1.0 - ADAM_B2) * (g * g)
    m_hat = m / (1.0 - ADAM_B1 ** ADAM_STEP)
    v_hat = v / (1.0 - ADAM_B2 ** ADAM_STEP)
    delta = -ADAM_LR * (m_hat / (jnp.sqrt(v_hat) + ADAM_EPS) + ADAM_WD * w)
    return delta, m, v


def _adamw_sharded(pos, mine, theirs, landed, weights, row_tile, name):
    rows, n = weights[0][0].shape
    n_slots = mine.shape[0]
    per_shard = rows // row_tile
    assert per_shard == 1 or len(weights) == 1

    def mine_map(j, t, pos_ref):
        chip = 2 * pos_ref[0] + pos_ref[1]
        return (2 * chip + pos_ref[2] if n_slots == N_DEV else chip), j * per_shard + t, 0

    def theirs_map(j, t, pos_ref):
        return 2 * pos_ref[0] + pos_ref[1], j * per_shard + t, 0

    def body(pos_ref, mine_ref, theirs_ref, land_ref, *refs):
        ins, outs = refs[:3 * len(weights)], refs[3 * len(weights):]
        g = mine_ref[...] + theirs_ref[...]
        for k in range(3):
            g = g + land_ref[k].astype(F32)
        for j in range(len(weights)):
            @pl.when(pl.program_id(0) == j)
            def _(j=j):
                w_ref, m_ref, v_ref = ins[3 * j:3 * j + 3]
                delta, m_new, v_new = _adamw_math(w_ref[...], g, m_ref[...], v_ref[...])
                for ref, val in zip(outs[4 * j:4 * j + 4], (g, delta, m_new, v_new)):
                    ref[...] = val

    tile = pl.BlockSpec((row_tile, n), lambda j, t, pos_ref: (t, 0))
    res = pl.pallas_call(
        body, name=name,
        out_shape=[jax.ShapeDtypeStruct((rows, n), F32)] * (4 * len(weights)),
        grid_spec=pltpu.PrefetchScalarGridSpec(
            num_scalar_prefetch=1, grid=(len(weights), per_shard),
            in_specs=[pl.BlockSpec((None, row_tile, n), mine_map), pl.BlockSpec((None, row_tile, n), theirs_map),
                      pl.BlockSpec((3, row_tile, n), lambda j, t, pos_ref: (0, j * per_shard + t, 0))]
            + [tile] * (3 * len(weights)),
            out_specs=[tile] * (4 * len(weights))),
        compiler_params=pltpu.CompilerParams(dimension_semantics=("arbitrary", "arbitrary")),
    )(pos, mine, theirs, landed, *[a for wmv in weights for a in wmv])
    return [res[4 * j:4 * j + 4] for j in range(len(weights))]


def _adamw_small(gathered, gathered_cols, params):
    n_par, n_src = len(params), len(gathered)

    def body(*refs):
        g_refs, gc_refs = refs[:n_src], refs[n_src:2 * n_src]
        ins = refs[2 * n_src:2 * n_src + 3 * n_par]
        outs = refs[2 * n_src + 3 * n_par:]
        loss_ref = outs[4 * n_par]

        def reduced(ref, row, n_rows):
            g = ref[0, row:row + n_rows, :]
            for d in range(1, N_DEV):
                g = g + ref[d, row:row + n_rows, :]
            return g

        for p, (src, row, n_rows, sharded, _, _, _) in enumerate(params):
            g = reduced((gc_refs if sharded else g_refs)[src], row, n_rows)
            w_ref, m_ref, v_ref = ins[3 * p:3 * p + 3]
            delta, m_new, v_new = _adamw_math(w_ref[...], g, m_ref[...], v_ref[...])
            outs[4 * p][...] = g
            outs[4 * p + 1][...] = delta
            outs[4 * p + 2][...] = m_new
            outs[4 * p + 3][...] = v_new
        loss = jnp.sum(reduced(g_refs[0], ROW_LOSS, 1), axis=1, keepdims=True)
        loss_ref[...] = jnp.broadcast_to(loss, loss_ref.shape)

    out_shape = []
    for (_, _, _, _, w, _, _) in params:
        out_shape += [jax.ShapeDtypeStruct(w.shape, F32)] * 4
    out_shape.append(jax.ShapeDtypeStruct((1, LANES), F32))
    flat = [a for (_, _, _, _, w, m, v) in params for a in (w, m, v)]
    return pl.pallas_call(
        body, name="adamw_small", out_shape=out_shape,
        in_specs=[_VMEM] * (2 * n_src + len(flat)), out_specs=[_VMEM] * len(out_shape),
    )(*gathered, *gathered_cols, *flat)


def _pad_rows(a, rows):
    return jnp.concatenate([a, jnp.zeros((rows - a.shape[0], a.shape[1]), a.dtype)], axis=0)


def kernel(x, meta_tokens, norm_g, w_in, conv_a_w, conv_a_b, ln_a_g, ln_a_b, w_a_out, b_a_out, conv_b_w, w_b_out, w_out, final_g, loss_target, m_meta_tokens, m_norm_g, m_w_in, m_conv_a_w, m_conv_a_b, m_ln_a_g, m_ln_a_b, m_w_a_out, m_b_a_out, m_conv_b_w, m_w_b_out, m_w_out, m_final_g, v_meta_tokens, v_norm_g, v_w_in, v_conv_a_w, v_conv_a_b, v_ln_a_g, v_ln_a_b, v_w_a_out, v_b_a_out, v_conv_b_w, v_w_b_out, v_w_out, v_final_g):
    seq = x.shape[1]
    assert x.shape == (1, seq, D_MODEL) and seq % TILE == 0 and w_in.shape == (1, D_MODEL, COLS)
    n_tiles = seq // TILE + 1
    tp = n_tiles * TILE
    pos = jnp.stack([lax.axis_index("x"), lax.axis_index("y"), lax.axis_index("c")]).astype(jnp.int32)
    me = 4 * pos[0] + 2 * pos[1] + pos[2]
    x2d = x[0]
    tgt2d = loss_target[0]

    small = jnp.concatenate([meta_tokens, _pad_rows(conv_a_w[0], 32), _pad_rows(conv_b_w[0], SUBLANES)], axis=0)
    final_g2 = final_g.reshape(1, D_MODEL)

    w_out_shards = [w[0].astype(BF16) for w in (w_a_out, w_b_out, w_out)]
    h_t, proj, meta_tile, small_params, w_in_all, *w_out_all = _gather_norm_proj(
        pos, x2d, small[None], norm_g, w_in[0].astype(BF16), w_out_shards, 3)
    small_params = small_params.transpose(1, 0, 2).reshape(small.shape[0], D_MODEL)
    conv_a_full, conv_b_full = small_params[N_META:N_META + 32], small_params[N_META + 32:]
    w_out_all = [w.reshape(D_MODEL, D_MODEL) for w in w_out_all]
    w_out_all_t = [w.T for w in w_out_all]
    dproj, ds1, lhs, rhs, small_a = _fused_pass(
        proj, x2d, tgt2d, meta_tile, conv_a_full, conv_a_b, ln_a_g, ln_a_b, b_a_out, conv_b_full, final_g2,
        w_out_all[0], w_out_all[1], w_out_all[2], w_out_all_t[0], w_out_all_t[1], w_out_all_t[2], n_tiles)
    k_tile = tp // 3
    gw_out = _grad_w_out(lhs, rhs, k_tile).reshape(N_DEV, 3 * ROWS_OUT, D_MODEL)
    gw_far, their_out, small_a_all = _grad_w_in_half(
        pos, h_t, dproj, k_tile, True, [("sibling", (gw_out,)), ("all", (small_a[None],))], "grad_w_in_far")
    parts_out = _chip_partial(pos, gw_out, their_out, (1, 2, 3), BF16, ROWS_OUT, "rs_parts_w_out")
    gw_near, their_in, land_out = _grad_w_in_half(
        pos, h_t, dproj, k_tile, False, [("sibling_half", (gw_far,)), ("chips", (parts_out,))], "grad_w_in_near")
    parts_in = _chip_partial(pos, gw_near, their_in, (1, 2, 3), BF16, 256, "rs_parts_w_in")
    grad_x, small_b_all, land_in = _input_bwd(dproj, ds1, x2d, meta_tile, norm_g, w_in_all, min(256, seq), (parts_in,))

    small_grads = [small_a_all, small_b_all]
    small_cols = [lax.dynamic_slice_in_dim(g, me * LANES, LANES, axis=2) for g in small_grads]

    (res_in,) = _adamw_sharded(pos, gw_near, their_in, land_in, [(w_in[0], m_w_in[0], v_w_in[0])], 128, "adamw_w_in")
    res_out = _adamw_sharded(
        pos, gw_out, their_out, land_out,
        [(w_a_out[0], m_w_a_out[0], v_w_a_out[0]), (w_b_out[0], m_w_b_out[0], v_w_b_out[0]),
         (w_out[0], m_w_out[0], v_w_out[0])], ROWS_OUT, "adamw_w_out")
    params = [
        (1, ROW_META, N_META, True, meta_tokens, m_meta_tokens, v_meta_tokens),
        (1, ROW_NORM_G, 1, False, norm_g, m_norm_g, v_norm_g),
        (0, ROW_CONV_A_W, CONV_A, True, conv_a_w[0], m_conv_a_w[0], v_conv_a_w[0]),
        (0, ROW_CONV_A_B, 1, False, conv_a_b, m_conv_a_b, v_conv_a_b),
        (0, ROW_LN_G, 1, False, ln_a_g, m_ln_a_g, v_ln_a_g),
        (0, ROW_LN_B, 1, False, ln_a_b, m_ln_a_b, v_ln_a_b),
        (0, ROW_B_A_OUT, 1, False, b_a_out, m_b_a_out, v_b_a_out),
        (0, ROW_CONV_B_W, CONV_B, True, conv_b_w[0], m_conv_b_w[0], v_conv_b_w[0]),
        (0, ROW_FINAL_G, 1, False, final_g2, m_final_g.reshape(1, D_MODEL), v_final_g.reshape(1, D_MODEL)),
    ]
    res_small = _adamw_small(small_grads, small_cols, params)
    loss = res_small[-1][0, 0]

    def small_res(p, kind, shape):
        return res_small[4 * p + kind].reshape(shape)

    per_weight = []
    for kind in range(4):
        per_weight.append([
            small_res(0, kind, meta_tokens.shape),
            small_res(1, kind, norm_g.shape),
            res_in[kind].reshape(w_in.shape),
            small_res(2, kind, conv_a_w.shape),
            small_res(3, kind, conv_a_b.shape),
            small_res(4, kind, ln_a_g.shape),
            small_res(5, kind, ln_a_b.shape),
            res_out[0][kind].reshape(w_a_out.shape),
            small_res(6, kind, b_a_out.shape),
            small_res(7, kind, conv_b_w.shape),
            res_out[1][kind].reshape(w_b_out.shape),
            res_out[2][kind].reshape(w_out.shape),
            small_res(8, kind, final_g.shape),
        ])
    return (loss, grad_x.reshape(x.shape), *per_weight[0], *per_weight[1], *per_weight[2], *per_weight[3])
```

```python
import functools

import jax
import jax.numpy as jnp
from jax import lax
from jax.experimental import pallas as pl
from jax.experimental.pallas import tpu as pltpu

D_MODEL = 1024
N_META = 16
N_DEV = 8
D_IN = 9 * D_MODEL
COLS = D_IN // N_DEV
ROWS_OUT = D_MODEL // N_DEV
CONV_A = 31
CONV_B = 3
EPS = 1e-6

ADAM_LR = 0.001
ADAM_B1 = 0.9
ADAM_B2 = 0.999
ADAM_EPS = 1e-08
ADAM_WD = 0.01
ADAM_STEP = 10

TILE = 128
LANES = 128
N_CHUNK = D_MODEL // LANES
HALO = 32
SUBLANES = 8
VMEM_LIMIT = 56 * 1024 * 1024

ROW_FINAL_G, ROW_B_A_OUT, ROW_LN_G, ROW_LN_B, ROW_CONV_A_B, ROW_LOSS = 0, 1, 2, 3, 4, 5
ROW_CONV_A_W, ROW_CONV_B_W, SMALL_A_ROWS = 8, 40, 48
ROW_NORM_G, ROW_META, SMALL_B_ROWS = 0, 8, 24

MESH = pl.DeviceIdType.MESH
_ANY = pl.BlockSpec(memory_space=pl.ANY)
_VMEM = pl.BlockSpec(memory_space=pltpu.VMEM)
BF16 = jnp.bfloat16
F32 = jnp.float32


def _resident(shape):
    return pl.BlockSpec(shape, lambda *_: (0,) * len(shape), pipeline_mode=pl.Buffered(1))


def _sigmoid(v):
    return jax.nn.sigmoid(v)


def _dot(a, b):
    return jnp.dot(a, b, preferred_element_type=F32)


def _dot_nt(a, b):
    return lax.dot_general(a, b, (((1,), (1,)), ((), ())), preferred_element_type=F32)


def _dot_tn(a, b):
    return lax.dot_general(a, b, (((0,), (0,)), ((), ())), preferred_element_type=F32)


def _colsum(v):
    return jnp.sum(v, axis=0, keepdims=True)


def _rowmean(v):
    parts = [v[:, LANES * c:LANES * (c + 1)] for c in range(v.shape[1] // LANES)]
    return jnp.sum(functools.reduce(jnp.add, parts), axis=-1, keepdims=True) * (1.0 / v.shape[1])


def _fold8(v):
    parts = [v[SUBLANES * g:SUBLANES * (g + 1)] for g in range(v.shape[0] // SUBLANES)]
    return functools.reduce(jnp.add, parts)


def _sibling_copies(srcs, dsts, send_sems, recv_sems):
    x, y, c = lax.axis_index("x"), lax.axis_index("y"), lax.axis_index("c")
    return [pltpu.make_async_remote_copy(
        src_ref=src.at[2 * q + (1 - c)], dst_ref=dst.at[q],
        send_sem=send_sems.at[4 * a + q], recv_sem=recv_sems.at[4 * a + q],
        device_id=(x, y, 1 - c), device_id_type=MESH)
        for a, (src, dst) in enumerate(zip(srcs, dsts)) for q in range(4)]


def _chip_copies(srcs, dsts, send_sems, recv_sems):
    x, y, c = lax.axis_index("x"), lax.axis_index("y"), lax.axis_index("c")
    targets = [(x, 1 - y, c), (1 - x, y, c), (1 - x, 1 - y, c)]
    return [pltpu.make_async_remote_copy(
        src_ref=src.at[k], dst_ref=dst.at[k],
        send_sem=send_sems.at[3 * a + k], recv_sem=recv_sems.at[3 * a + k],
        device_id=targets[k], device_id_type=MESH)
        for a, (src, dst) in enumerate(zip(srcs, dsts)) for k in range(3)]


def _sibling_half_copies(srcs, dsts, send_sems, recv_sems):
    x, y, c = lax.axis_index("x"), lax.axis_index("y"), lax.axis_index("c")
    return [pltpu.make_async_remote_copy(
        src_ref=src.at[q], dst_ref=dst.at[q],
        send_sem=send_sems.at[4 * a + q], recv_sem=recv_sems.at[4 * a + q],
        device_id=(x, y, 1 - c), device_id_type=MESH)
        for a, (src, dst) in enumerate(zip(srcs, dsts)) for q in range(4)]


def _all_copies(srcs, dsts, send_sems, recv_sems):
    x, y, c = lax.axis_index("x"), lax.axis_index("y"), lax.axis_index("c")
    mine = 4 * x + 2 * y + c
    copies = []
    for a, (src, dst) in enumerate(zip(srcs, dsts)):
        copies.append(pltpu.make_async_copy(src.at[0], dst.at[mine], send_sems.at[N_DEV * a]))
        for k in range(1, N_DEV):
            copies.append(pltpu.make_async_remote_copy(
                src_ref=src.at[0], dst_ref=dst.at[mine],
                send_sem=send_sems.at[N_DEV * a + k], recv_sem=recv_sems.at[N_DEV * a + k],
                device_id=(x ^ (k >> 2), y ^ ((k >> 1) & 1), c ^ (k & 1)), device_id_type=MESH))
    return copies


_EXCHANGES = {"sibling": (4, _sibling_copies, 4), "sibling_half": (4, _sibling_half_copies, 4),
              "chips": (3, _chip_copies, 3), "all": (N_DEV, _all_copies, N_DEV)}


def _exchange_shapes(kind, arrays):
    per_array, _, slots = _EXCHANGES[kind]
    out_shape = [jax.ShapeDtypeStruct((slots,) + a.shape[1:], a.dtype) for a in arrays]
    sems = [pltpu.SemaphoreType.DMA((per_array * len(arrays),))] * 2
    return out_shape, sems


def _ride_shapes(rides):
    shapes, sems = [], []
    for kind, arrays in rides:
        ride_shapes, ride_sems = _exchange_shapes(kind, arrays)
        shapes += ride_shapes
        sems += ride_sems
    return shapes, sems


def _riding(body, n_in, n_out, rides, is_first, is_last):
    counts = [len(arrays) for _, arrays in rides]
    n_arr = sum(counts)

    def wrapped(*refs):
        ins, srcs = refs[:n_in], refs[n_in:n_in + n_arr]
        outs = refs[n_in + n_arr:n_in + n_arr + n_out]
        dsts = refs[n_in + n_arr + n_out:n_in + 2 * n_arr + n_out]
        first_sem = len(refs) - 2 * len(rides)
        scratch, sems = refs[n_in + 2 * n_arr + n_out:first_sem], refs[first_sem:]

        def copies():
            made, at = [], 0
            for r, ((kind, _), n) in enumerate(zip(rides, counts)):
                made += _EXCHANGES[kind][1](srcs[at:at + n], dsts[at:at + n], sems[2 * r], sems[2 * r + 1])
                at += n
            return made

        @pl.when(is_first())
        def _():
            for cp in copies():
                cp.start()

        body(*ins, *outs, *scratch)

        @pl.when(is_last())
        def _():
            for cp in copies():
                cp.wait()

    return wrapped


_HBM = pl.BlockSpec(memory_space=pltpu.HBM)
_SEM = pl.BlockSpec(memory_space=pltpu.SEMAPHORE)
_FLOWS = pltpu.SideEffectType.DATAFLOW_SIDE_EFFECTING


def _chips_start(parts, name):
    def body(parts_ref, land_ref, send_sems, recv_sems, parts_thru, land_thru, token):
        for cp in _chip_copies([parts_ref], [land_ref], send_sems, recv_sems):
            cp.start()
        token[...] = jnp.zeros(token.shape, token.dtype)

    hbm = pltpu.HBM(parts.shape, parts.dtype)
    return pl.pallas_call(
        body, name=name,
        out_shape=(pltpu.SemaphoreType.DMA((3,)), pltpu.SemaphoreType.DMA((3,)), hbm, hbm,
                   jax.ShapeDtypeStruct((SUBLANES, LANES), F32)),
        in_specs=(_HBM, _HBM), out_specs=(_SEM, _SEM, _HBM, _HBM, _VMEM),
        input_output_aliases={0: 2, 1: 3},
        compiler_params=pltpu.CompilerParams(has_side_effects=_FLOWS),
    )(pltpu.with_memory_space_constraint(parts, pltpu.HBM),
      pltpu.with_memory_space_constraint(lax.empty(parts.shape, parts.dtype), pltpu.HBM))


def _chips_wait(send_sems, recv_sems, parts_thru, land_thru, after, name):
    def body(parts_ref, land_ref, send_sems, recv_sems, after_ref, parts_dead, land_out):
        for cp in _chip_copies([parts_ref], [land_ref], send_sems, recv_sems):
            cp.wait_send()
            cp.wait_recv()

    hbm = pltpu.HBM(parts_thru.shape, parts_thru.dtype)
    return pl.pallas_call(
        body, name=name, out_shape=(hbm, hbm),
        in_specs=(_HBM, _HBM, _SEM, _SEM, _ANY), out_specs=(_HBM, _HBM), input_output_aliases={0: 0, 1: 1},
        compiler_params=pltpu.CompilerParams(has_side_effects=_FLOWS),
    )(parts_thru, land_thru, send_sems, recv_sems, after)[1]


def _exchange(kind, arrays, name):
    n_arr = len(arrays)
    out_shape, sems = _exchange_shapes(kind, arrays)

    def body(*refs):
        copies = _EXCHANGES[kind][1](refs[:n_arr], refs[n_arr:2 * n_arr], *refs[2 * n_arr:])
        for cp in copies:
            cp.start()
        for cp in copies:
            cp.wait()

    return pl.pallas_call(body, name=name, out_shape=out_shape, in_specs=[_ANY] * n_arr,
                          out_specs=[_ANY] * n_arr, scratch_shapes=sems)(*arrays)


def _chip_partial(pos, mine, theirs, relations, out_dtype, row_tile, name):
    n_slots, m, n = mine.shape
    q0 = relations[0]

    def chip_of(qi, pos_ref):
        q = qi + q0
        return pos_ref[0] ^ (q >> 1), pos_ref[1] ^ (q & 1)

    def mine_map(qi, t, pos_ref):
        px, py = chip_of(qi, pos_ref)
        return (4 * px + 2 * py + pos_ref[2] if n_slots == N_DEV else 2 * px + py), t, 0

    def theirs_map(qi, t, pos_ref):
        px, py = chip_of(qi, pos_ref)
        return 2 * px + py, t, 0

    def body(pos_ref, a_ref, b_ref, o_ref):
        o_ref[...] = (a_ref[...] + b_ref[...]).astype(out_dtype)

    return pl.pallas_call(
        body, name=name,
        out_shape=jax.ShapeDtypeStruct((len(relations), m, n), out_dtype),
        grid_spec=pltpu.PrefetchScalarGridSpec(
            num_scalar_prefetch=1, grid=(len(relations), m // row_tile),
            in_specs=[pl.BlockSpec((None, row_tile, n), mine_map), pl.BlockSpec((None, row_tile, n), theirs_map)],
            out_specs=pl.BlockSpec((None, row_tile, n), lambda qi, t, pos_ref: (qi, t, 0))),
        compiler_params=pltpu.CompilerParams(dimension_semantics=("arbitrary", "arbitrary")),
    )(pos, mine, theirs)


PARTS = ((0, 512), (512, 640))


def _gather_norm_proj(pos, x2d, small_shard, norm_g, w_in_shard, w_out_shards, n_chunk):
    seq = x2d.shape[0]
    n_tiles = seq // TILE + 1
    tp = n_tiles * TILE
    n_parts = len(PARTS)
    widest = max(width for _, width in PARTS)
    units = [(s, u) for s in range(2) for u in range(n_parts)]
    for first in (2, 5):
        units += [(first + j, u) for u in range(n_parts) for j in range(2)] + [(first + 2, u) for u in range(n_parts)]
    n_units = len(units)
    n_steps = n_tiles + n_units
    chunk = tp // n_chunk

    def body(pos_ref, x_ref, g_ref, small_ref, win_ref, wa_ref, wb_ref, wo_ref,
             ht_ref, proj_ref, meta_ref, small_all, win_all, wa_all, wb_all, wo_all,
             h_all, wbuf, rbuf, small_buf, send_sems, recv_sems, local_sems, small_send, small_recv):
        g = pl.program_id(0)
        x, y, c = lax.axis_index("x"), lax.axis_index("y"), lax.axis_index("c")
        me, sibling = (x, y, c), (x, y, 1 - c)
        chips = [(1 - x, y), (x, 1 - y), (1 - x, 1 - y)]
        shards = (win_ref, wa_ref, wb_ref, wo_ref)
        gathered = (win_all, wa_all, wb_all, wo_all)
        n_arrays = len(shards)
        blocks = [me, sibling] + [(*chip, c) for chip in chips] + [(*chip, 1 - c) for chip in chips]

        def index(block):
            px, py, pc = block
            return 4 * px + 2 * py + pc

        def part(ref, a, u):
            return ref.at[:, pl.ds(PARTS[u][0], PARTS[u][1])] if a == 0 else ref

        def slot(a, block, u):
            return part(gathered[a].at[index(block)], a, u)

        def sem(a, k, u):
            return n_parts * k + u if a == 0 else 7 * n_parts + 7 * (a - 1) + k

        def copy(a, k, block, to, u=0, from_shard=False):
            return pltpu.make_async_remote_copy(
                src_ref=part(shards[a], a, u) if from_shard else slot(a, block, u), dst_ref=slot(a, block, u),
                send_sem=send_sems.at[sem(a, k, u)], recv_sem=recv_sems.at[sem(a, k, u)],
                device_id=to, device_id_type=MESH)

        def keep(a):
            return pltpu.make_async_copy(shards[a], gathered[a].at[index(me)], local_sems.at[a])

        def load(m):
            s, u = units[m]
            src = part(win_ref, 0, u) if s == 0 else slot(0, blocks[s], u)
            return pltpu.make_async_copy(src, wbuf.at[m % 2, :, 0:PARTS[u][1]], local_sems.at[n_arrays + m % 2])

        def store(m):
            s, u = units[m]
            col0 = pl.multiple_of(index(blocks[s]) * COLS + PARTS[u][0], LANES)
            return pltpu.make_async_copy(rbuf.at[m % 2, :, 0:PARTS[u][1]],
                                         proj_ref.at[:, pl.ds(col0, PARTS[u][1])], local_sems.at[n_arrays + 2 + m % 2])

        def by_x(a, u):
            return u == 0 if a == 0 else a < 3

        def relay(a, u=0):
            src, to = (blocks[3], blocks[2]) if by_x(a, u) else (blocks[2], blocks[3])
            return copy(a, 3, src, to, u)

        def arrive(m):
            s, u = units[m]
            if s == 1:
                copy(0, 0, sibling, me, u).wait_recv()
            elif 2 <= s <= 4:
                copy(0, s - 1, blocks[s], me, u).wait_recv()
                copy(0, s + 2, blocks[s], sibling, u).start()
                if s < 4 and by_x(0, u) == (s == 3):
                    relay(0, u).start()
            elif s >= 5:
                copy(0, s - 1, blocks[s], me, u).wait_recv()
                if u == 0:
                    for a in range(1, 4):
                        copy(a, s - 4, blocks[s - 3], me).wait_recv()
                        copy(a, s - 1, blocks[s - 3], sibling).start()
                        if s < 7 and by_x(a, 0) == (s == 6):
                            relay(a).start()

        targets = [sibling, blocks[2], blocks[3]]

        def small_copies():
            return _all_copies([small_ref], [small_all], small_send, small_recv)

        @pl.when(g == 0)
        def _():
            for cp in small_copies():
                cp.start()
            for a in range(n_arrays):
                keep(a).start()
            for u in range(n_parts):
                for k, to in enumerate(targets):
                    copy(0, k, me, to, u, from_shard=True).start()
            for a in range(1, 4):
                for k, to in enumerate(targets):
                    copy(a, k, me, to, from_shard=True).start()
            load(0).start()

        @pl.when(g == n_tiles - 2)
        def _():
            for cp in small_copies():
                cp.wait()
            fetch = pltpu.make_async_copy(small_all, small_buf, local_sems.at[n_arrays + 4])
            fetch.start()
            fetch.wait()
            meta_ref[0:TILE - N_META, :] = jnp.zeros((TILE - N_META, D_MODEL), F32)
            meta_ref[TILE - N_META:TILE, :] = jnp.concatenate([small_buf[d, 0:N_META, :] for d in range(N_DEV)], axis=1)

        @pl.when(g < n_tiles)
        def _():
            s0 = jnp.where(g == n_tiles - 1, meta_ref[...], x_ref[...])
            r = lax.rsqrt(_rowmean(s0 * s0) + EPS)
            h32 = (s0 * r) * g_ref[...]
            ht_ref[...] = h32.T.astype(BF16)
            h_all[pl.ds(pl.multiple_of(g * TILE, TILE), TILE), :] = h32.astype(BF16)

        for m in range(n_units):
            @pl.when(g == n_tiles + m)
            def _(m=m):
                load(m).wait()
                if m + 1 < n_units:
                    arrive(m + 1)
                    load(m + 1).start()
                if m >= 2:
                    store(m - 2).wait()

        m_now = jnp.maximum(g - n_tiles, 0)
        u_now = functools.reduce(jnp.add, [jnp.where(m_now == m, u, 0) for m, (_, u) in enumerate(units)])
        for u, (_, width) in enumerate(PARTS):
            @pl.when((g >= n_tiles) & (u_now == u))
            def _(width=width):
                w = wbuf[m_now % 2, :, 0:width]
                for r in range(n_chunk):
                    rbuf[m_now % 2, r * chunk:(r + 1) * chunk, 0:width] = _dot(h_all[r * chunk:(r + 1) * chunk, :], w)

        for m in range(n_units):
            @pl.when(g == n_tiles + m)
            def _(m=m):
                store(m).start()

        @pl.when(g == n_steps - 1)
        def _():
            store(n_units - 2).wait()
            store(n_units - 1).wait()
            for a in range(1, 4):
                copy(a, 0, sibling, me).wait_recv()
                for j in range(3):
                    copy(a, 4 + j, blocks[5 + j], me).wait_recv()
            for a in range(n_arrays):
                for u in range(n_parts if a == 0 else 1):
                    for k, to in enumerate(targets):
                        copy(a, k, me, to, u, from_shard=True).wait_send()
                    relay(a, u).wait_send()
                    for j in range(3):
                        copy(a, 4 + j, blocks[2 + j], sibling, u).wait_send()
                keep(a).wait()

    n_x = n_tiles - 1
    return pl.pallas_call(
        body, name="gather_norm_proj",
        out_shape=[jax.ShapeDtypeStruct((D_MODEL, tp), BF16), jax.ShapeDtypeStruct((tp, D_IN), F32),
                   jax.ShapeDtypeStruct((TILE, D_MODEL), F32), jax.ShapeDtypeStruct((N_DEV,) + small_shard.shape[1:], F32),
                   jax.ShapeDtypeStruct((N_DEV,) + w_in_shard.shape, BF16)]
                  + [jax.ShapeDtypeStruct((N_DEV,) + w.shape, BF16) for w in w_out_shards],
        grid_spec=pltpu.PrefetchScalarGridSpec(
            num_scalar_prefetch=1, grid=(n_steps,),
            in_specs=[pl.BlockSpec((TILE, D_MODEL), lambda g, pos_ref: (jnp.minimum(g, n_x - 1), 0)),
                      _VMEM, _ANY, _ANY, _ANY, _ANY, _ANY],
            out_specs=[pl.BlockSpec((D_MODEL, TILE), lambda g, pos_ref: (0, jnp.minimum(g, n_tiles - 1))),
                       _ANY, _VMEM, _ANY, _ANY, _ANY, _ANY, _ANY],
            scratch_shapes=[pltpu.VMEM((tp, D_MODEL), BF16), pltpu.VMEM((2, D_MODEL, widest), BF16),
                            pltpu.VMEM((2, tp, widest), F32), pltpu.VMEM((N_DEV,) + small_shard.shape[1:], F32),
                            pltpu.SemaphoreType.DMA((7 * n_parts + 21,)), pltpu.SemaphoreType.DMA((7 * n_parts + 21,)),
                            pltpu.SemaphoreType.DMA((9,)),
                            pltpu.SemaphoreType.DMA((N_DEV,)), pltpu.SemaphoreType.DMA((N_DEV,))]),
        compiler_params=pltpu.CompilerParams(dimension_semantics=("arbitrary",), vmem_limit_bytes=VMEM_LIMIT),
    )(pos, x2d, norm_g, small_shard, w_in_shard, *w_out_shards)


C_AVAL, C_AGLU, C_AZ, C_BB, C_BC, C_BX, C_BZ, C_GA, C_GB = (k * D_MODEL for k in range(9))
S_AZ, S_BB, S_BZ, S_GA, S_GB = (k * D_MODEL for k in range(5))


def _fused_pass(proj, x2d, tgt2d, meta_tile, conv_a_w, conv_a_b, ln_a_g, ln_a_b, b_a_out, conv_b_w, final_g,
                w_a, w_b, w_o, w_a_t, w_b_t, w_o_t, n_tiles):
    T = TILE
    tp = n_tiles * T
    inv_d = 1.0 / D_MODEL

    def block_of(tile):
        return jnp.where(tile == 0, n_tiles - 1, tile - 1)

    def cur(i):
        return block_of(jnp.minimum(i, n_tiles - 1))

    def prev(i):
        return block_of(jnp.clip(i - 1, 0, n_tiles - 1))

    def xblk(i):
        return jnp.maximum(jnp.minimum(i, n_tiles - 1) - 1, 0)

    def body(proj_ref, aprev, cprev, x_ref, tgt_ref, meta_ref, caw_ref, cab_ref, lng_ref, lnb_ref, bao_ref, cbw_ref,
             fg_ref, wa_ref, wb_ref, wo_ref, wat_ref, wbt_ref, wot_ref,
             dproj_ref, ds1_ref, lhs_ref, rhs_ref, small_ref,
             ua0_buf, cb_buf, dua1_buf, dc3_buf, stage, ua1_buf, c3_buf,
             dpa_buf, dpb_buf, dcaw8, dcbw8, shift_buf):
        i = pl.program_id(0)
        this, before = i % 2, 1 - i % 2

        @pl.when(i == 0)
        def _init():
            for buf in (ua0_buf, cb_buf, dua1_buf, dc3_buf, dcaw8, dcbw8):
                buf[...] = jnp.zeros(buf.shape, buf.dtype)
            small_ref[...] = jnp.zeros(small_ref.shape, F32)

        @pl.when(i >= 1)
        def _emit_stage():
            dproj_ref[:, C_AZ:C_BC] = stage[:, S_AZ:S_BZ]
            dproj_ref[:, C_BZ:D_IN] = stage[:, S_BZ:S_GB + D_MODEL]

        @pl.when(i < n_tiles)
        def _front():
            def conv_chunk(cc, carry):
                c0 = pl.multiple_of(cc * LANES, LANES)
                lanes = pl.ds(c0, LANES)

                def col(base):
                    return pl.ds(pl.multiple_of(base + cc * LANES, LANES), LANES)

                ua0 = proj_ref[:, col(C_AVAL)] * _sigmoid(proj_ref[:, col(C_AGLU)])
                ua0_buf[this, 0:HALO, lanes] = ua0_buf[before, T:T + HALO, lanes]
                ua0_buf[this, HALO:HALO + T, lanes] = ua0
                acc = jnp.broadcast_to(cab_ref[:, lanes], (T, LANES))
                lead = HALO - (CONV_A - 1)
                for r in range(SUBLANES):
                    taps = [k for k in range(CONV_A) if (k + lead) % SUBLANES == r]
                    rows = T + SUBLANES * max((k + lead) // SUBLANES for k in taps)
                    if r:
                        shift_buf[r, 0:rows, :] = ua0_buf[this, pl.ds(r, rows), lanes]
                    for k in taps:
                        q = (k + lead) // SUBLANES
                        if r:
                            win = shift_buf[r, SUBLANES * q:SUBLANES * q + T, :]
                        else:
                            win = ua0_buf[this, pl.ds(SUBLANES * q, T), lanes]
                        acc = acc + caw_ref[k:k + 1, lanes] * win
                ua1_buf[:, lanes] = acc
                cb = proj_ref[:, col(C_BC)] * proj_ref[:, col(C_BX)]
                cb_buf[this, 0:SUBLANES, lanes] = cb_buf[before, T:T + SUBLANES, lanes]
                cb_buf[this, SUBLANES:SUBLANES + T, lanes] = cb
                lead_b = SUBLANES - (CONV_B - 1)
                acc3 = cbw_ref[0:1, lanes] * cb_buf[this, pl.ds(lead_b, T), lanes]
                for k in range(1, CONV_B):
                    acc3 = acc3 + cbw_ref[k:k + 1, lanes] * cb_buf[this, pl.ds(lead_b + k, T), lanes]
                c3_buf[:, lanes] = acc3
                return carry

            lax.fori_loop(0, N_CHUNK, conv_chunk, 0)

            ua1 = ua1_buf[...]
            xc = ua1 - _rowmean(ua1)
            rstd = lax.rsqrt(_rowmean(xc * xc) + EPS)
            xhat = xc * rstd
            ua2 = xhat * lng_ref[...] + lnb_ref[...]
            sg2 = _sigmoid(ua2)
            ua3 = ua2 * sg2
            a_z = proj_ref[:, C_AZ:C_AZ + D_MODEL]
            sz = _sigmoid(a_z)
            silu_az = a_z * sz
            lhs_ref[0] = (ua3 * silu_az).astype(BF16)
            b_z = proj_ref[:, C_BZ:C_BZ + D_MODEL]
            sbz = _sigmoid(b_z)
            silu_bz = b_z * sbz
            b_b = proj_ref[:, C_BB:C_BB + D_MODEL]
            c3 = c3_buf[...]
            ub = b_b * c3
            lhs_ref[1] = (ub * silu_bz).astype(BF16)

            ya = _dot(lhs_ref[0], wa_ref[...]) + bao_ref[...]
            yb = _dot(lhs_ref[1], wb_ref[...])
            sga = _sigmoid(proj_ref[:, C_GA:C_GA + D_MODEL])
            sgb = _sigmoid(proj_ref[:, C_GB:C_GB + D_MODEL])
            m_b = (sga * ya + sgb * yb).astype(BF16)
            lhs_ref[2] = m_b
            s0 = jnp.where(i == 0, meta_ref[...], x_ref[...])
            s1 = s0 + _dot(m_b, wo_ref[...])
            r1 = lax.rsqrt(_rowmean(s1 * s1) + EPS)
            y = (s1 * r1) * fg_ref[...]
            is_token = (i >= 1).astype(F32)
            err = (y - tgt_ref[...]) * is_token
            small_ref[ROW_LOSS:ROW_LOSS + 1, :] += (0.5 * inv_d) * _colsum(err * err)
            dy = err * inv_d
            small_ref[ROW_FINAL_G:ROW_FINAL_G + 1, :] += _colsum(dy * (s1 * r1))
            gy = dy * fg_ref[...]
            ds1 = r1 * gy - s1 * ((r1 * r1 * r1) * _rowmean(gy * s1))
            ds1_ref[...] = ds1
            ds1_b = ds1.astype(BF16)
            rhs_ref[2] = ds1_b
            dm = _dot(ds1_b, wot_ref[...])
            dya = dm * sga
            dyb = dm * sgb
            stage[:, S_GA:S_GA + D_MODEL] = (dya * ya * (1.0 - sga)).astype(BF16)
            stage[:, S_GB:S_GB + D_MODEL] = (dyb * yb * (1.0 - sgb)).astype(BF16)
            small_ref[ROW_B_A_OUT:ROW_B_A_OUT + 1, :] += _colsum(dya)
            dya_b = dya.astype(BF16)
            dyb_b = dyb.astype(BF16)
            rhs_ref[0] = dya_b
            rhs_ref[1] = dyb_b
            dpa_buf[...] = _dot(dya_b, wat_ref[...])
            dpb_buf[...] = _dot(dyb_b, wbt_ref[...])

            dpa = dpa_buf[...]
            stage[:, S_AZ:S_AZ + D_MODEL] = (dpa * ua3 * (sz + silu_az * (1.0 - sz))).astype(BF16)
            dua2 = dpa * silu_az * (sg2 + ua3 * (1.0 - sg2))
            small_ref[ROW_LN_G:ROW_LN_G + 1, :] += _colsum(dua2 * xhat)
            small_ref[ROW_LN_B:ROW_LN_B + 1, :] += _colsum(dua2)
            dxh = dua2 * lng_ref[...]
            dua1 = rstd * (dxh - _rowmean(dxh) - xhat * _rowmean(dxh * xhat))
            small_ref[ROW_CONV_A_B:ROW_CONV_A_B + 1, :] += _colsum(dua1)
            dua1_buf[this, 0:T, :] = dua1
            dua1_buf[before, T:T + HALO, :] = dua1[0:HALO]
            dpb = dpb_buf[...]
            stage[:, S_BZ:S_BZ + D_MODEL] = (dpb * ub * (sbz + silu_bz * (1.0 - sbz))).astype(BF16)
            dub = dpb * silu_bz
            stage[:, S_BB:S_BB + D_MODEL] = (dub * c3).astype(BF16)
            dc3 = dub * b_b
            dc3_buf[this, 0:T, :] = dc3
            dc3_buf[before, T:T + SUBLANES, :] = dc3[0:SUBLANES]

        @pl.when(i == n_tiles)
        def _no_later_tile():
            dua1_buf[before, T:T + HALO, :] = jnp.zeros((HALO, D_MODEL), F32)
            dc3_buf[before, T:T + SUBLANES, :] = jnp.zeros((SUBLANES, D_MODEL), F32)

        @pl.when(i >= 1)
        def _lagged():
            def convt_chunk(cc, carry):
                c0 = pl.multiple_of(cc * LANES, LANES)
                lanes = pl.ds(c0, LANES)

                def col(base):
                    return pl.ds(pl.multiple_of(base + cc * LANES, LANES), LANES)

                ua0 = ua0_buf[before, HALO:HALO + T, lanes]
                acc = jnp.zeros((T, LANES), F32)
                for r in range(SUBLANES):
                    shifts = [j for j in range(CONV_A) if j % SUBLANES == r]
                    rows = T + shifts[-1] - r
                    if r:
                        shift_buf[r, 0:rows, :] = dua1_buf[before, pl.ds(r, rows), lanes]
                    for j in shifts:
                        k = CONV_A - 1 - j
                        if r:
                            later = shift_buf[r, j - r:j - r + T, :]
                        else:
                            later = dua1_buf[before, pl.ds(j, T), lanes]
                        acc = acc + caw_ref[k:k + 1, lanes] * later
                        dcaw8[SUBLANES * k:SUBLANES * (k + 1), lanes] += _fold8(ua0 * later)
                a_val = aprev[:, col(0)]
                sg = _sigmoid(aprev[:, col(D_MODEL)])
                dproj_ref[:, col(C_AVAL)] = (acc * sg).astype(BF16)
                dproj_ref[:, col(C_AGLU)] = (acc * a_val * (sg * (1.0 - sg))).astype(BF16)

                cb = cb_buf[before, SUBLANES:SUBLANES + T, lanes]
                acc3 = jnp.zeros((T, LANES), F32)
                for j in range(CONV_B):
                    k = CONV_B - 1 - j
                    later = dc3_buf[before, pl.ds(j, T), lanes]
                    acc3 = acc3 + cbw_ref[k:k + 1, lanes] * later
                    dcbw8[SUBLANES * k:SUBLANES * (k + 1), lanes] += _fold8(cb * later)
                dproj_ref[:, col(C_BC)] = (acc3 * cprev[:, col(D_MODEL)]).astype(BF16)
                dproj_ref[:, col(C_BX)] = (acc3 * cprev[:, col(0)]).astype(BF16)
                return carry

            lax.fori_loop(0, N_CHUNK, convt_chunk, 0)

        @pl.when(i == n_tiles)
        def _finish():
            for k in range(CONV_A):
                small_ref[ROW_CONV_A_W + k:ROW_CONV_A_W + k + 1, :] = _colsum(dcaw8[SUBLANES * k:SUBLANES * (k + 1), :])
            for k in range(CONV_B):
                small_ref[ROW_CONV_B_W + k:ROW_CONV_B_W + k + 1, :] = _colsum(dcbw8[SUBLANES * k:SUBLANES * (k + 1), :])

    pair = 2 * D_MODEL
    return pl.pallas_call(
        body, name="fused_pass", grid=(n_tiles + 1,),
        out_shape=[
            jax.ShapeDtypeStruct((tp, D_IN), BF16),
            jax.ShapeDtypeStruct((tp, D_MODEL), F32),
            jax.ShapeDtypeStruct((3, tp, D_MODEL), BF16),
            jax.ShapeDtypeStruct((3, tp, D_MODEL), BF16),
            jax.ShapeDtypeStruct((SMALL_A_ROWS, D_MODEL), F32),
        ],
        in_specs=[
            pl.BlockSpec((T, D_IN), lambda i: (cur(i), 0)),
            pl.BlockSpec((T, pair), lambda i: (prev(i), C_AVAL // pair)),
            pl.BlockSpec((T, pair), lambda i: (prev(i), C_BC // pair)),
            pl.BlockSpec((T, D_MODEL), lambda i: (xblk(i), 0)),
            pl.BlockSpec((T, D_MODEL), lambda i: (xblk(i), 0)),
            _VMEM, _VMEM, _VMEM, _VMEM, _VMEM, _VMEM, _VMEM, _VMEM,
            *[_resident((D_MODEL, D_MODEL)) for _ in range(6)],
        ],
        out_specs=[
            pl.BlockSpec((T, D_IN), lambda i: (prev(i), 0)),
            pl.BlockSpec((T, D_MODEL), lambda i: (cur(i), 0)),
            pl.BlockSpec((3, T, D_MODEL), lambda i: (0, cur(i), 0)),
            pl.BlockSpec((3, T, D_MODEL), lambda i: (0, cur(i), 0)),
            _VMEM,
        ],
        scratch_shapes=[
            pltpu.VMEM((2, HALO + T, D_MODEL), F32),
            pltpu.VMEM((2, SUBLANES + T, D_MODEL), F32),
            pltpu.VMEM((2, T + HALO, D_MODEL), F32),
            pltpu.VMEM((2, T + SUBLANES, D_MODEL), F32),
            pltpu.VMEM((T, 5 * D_MODEL), BF16),
            pltpu.VMEM((T, D_MODEL), F32),
            pltpu.VMEM((T, D_MODEL), F32),
            pltpu.VMEM((T, D_MODEL), F32),
            pltpu.VMEM((T, D_MODEL), F32),
            pltpu.VMEM((32 * SUBLANES, D_MODEL), F32),
            pltpu.VMEM((SUBLANES * SUBLANES, D_MODEL), F32),
            pltpu.VMEM((SUBLANES, T + HALO, LANES), F32),
        ],
        compiler_params=pltpu.CompilerParams(dimension_semantics=("arbitrary",), vmem_limit_bytes=VMEM_LIMIT),
    )(proj, proj, proj, x2d, tgt2d, meta_tile, conv_a_w, conv_a_b, ln_a_g, ln_a_b, b_a_out, conv_b_w, final_g,
      w_a, w_b, w_o, w_a_t, w_b_t, w_o_t)


def _input_bwd(dproj, ds1, x2d, meta_tile, norm_g, w_in_all, row_tile):
    seq = x2d.shape[0]
    n_steps = seq // row_tile
    meta_block = seq // TILE

    def backward(dp_ref, ds1_ref, s0_ref, g_ref, w_ref, out_ref, vec_ref):
        dh = _dot_nt(dp_ref[:, 0:COLS], w_ref[0])
        for j in range(1, N_DEV):
            dh = dh + _dot_nt(dp_ref[:, j * COLS:(j + 1) * COLS], w_ref[j])
        s0v = s0_ref[...]
        r = lax.rsqrt(_rowmean(s0v * s0v) + EPS)
        gh = dh * g_ref[...]
        out_ref[...] = ds1_ref[...] + r * gh - s0v * ((r * r * r) * _rowmean(gh * s0v))
        vec_ref[ROW_NORM_G:ROW_NORM_G + 1, :] += _colsum(dh * (s0v * r))

    def body(dp_ref, ds1_ref, x_ref, dpm_ref, ds1m_ref, meta_ref, g_ref, w_ref, gx_ref, small_ref, gmeta_buf):
        t = pl.program_id(0)

        @pl.when(t == 0)
        def _():
            small_ref[...] = jnp.zeros(small_ref.shape, F32)

        backward(dp_ref, ds1_ref, x_ref, g_ref, w_ref, gx_ref, small_ref)

        @pl.when(t == n_steps - 1)
        def _():
            backward(dpm_ref, ds1m_ref, meta_ref, g_ref, w_ref, gmeta_buf, small_ref)
            small_ref[ROW_META:ROW_META + N_META, :] = gmeta_buf[TILE - N_META:TILE, :]

    return pl.pallas_call(
        body, name="input_bwd", grid=(n_steps,),
        out_shape=[jax.ShapeDtypeStruct(x2d.shape, F32), jax.ShapeDtypeStruct((SMALL_B_ROWS, D_MODEL), F32)],
        in_specs=[pl.BlockSpec((row_tile, D_IN), lambda t: (t, 0)),
                  pl.BlockSpec((row_tile, D_MODEL), lambda t: (t, 0)),
                  pl.BlockSpec((row_tile, D_MODEL), lambda t: (t, 0)),
                  pl.BlockSpec((TILE, D_IN), lambda t: (meta_block, 0)),
                  pl.BlockSpec((TILE, D_MODEL), lambda t: (meta_block, 0)),
                  _VMEM, _VMEM, _resident((N_DEV, D_MODEL, COLS))],
        out_specs=[pl.BlockSpec((row_tile, D_MODEL), lambda t: (t, 0)), _VMEM],
        scratch_shapes=[pltpu.VMEM((TILE, D_MODEL), F32)],
        compiler_params=pltpu.CompilerParams(dimension_semantics=("arbitrary",), vmem_limit_bytes=VMEM_LIMIT),
    )(dproj, ds1, x2d, dproj, ds1, meta_tile, norm_g, w_in_all)


def _grad_w_in_half(pos, h_t, dproj, k_tile, other_side, rides, name):
    tp = h_t.shape[1]
    n_k = tp // k_tile

    def column_block(q, k, pos_ref):
        return k, 2 * q + (1 - pos_ref[2] if other_side else pos_ref[2])

    def body(pos_ref, h_ref, dp_ref, o_ref):
        @pl.when(pl.program_id(1) == 0)
        def _():
            o_ref[...] = jnp.zeros(o_ref.shape, F32)

        o_ref[...] += _dot(h_ref[...], dp_ref[...])

    ride = [a for _, arrays in rides for a in arrays]
    n_arr = len(ride)
    ride_shapes, ride_sems = _ride_shapes(rides)
    body = _riding(body, 3, 1, rides,
                   lambda: (pl.program_id(0) == 0) & (pl.program_id(1) == 0),
                   lambda: (pl.program_id(0) == 3) & (pl.program_id(1) == n_k - 1))
    return pl.pallas_call(
        body, name=name,
        out_shape=[jax.ShapeDtypeStruct((4, D_MODEL, COLS), F32)] + ride_shapes,
        grid_spec=pltpu.PrefetchScalarGridSpec(
            num_scalar_prefetch=1, grid=(4, n_k),
            in_specs=[pl.BlockSpec((D_MODEL, k_tile), lambda q, k, pos_ref: (0, k)),
                      pl.BlockSpec((k_tile, COLS), column_block)] + [_ANY] * n_arr,
            out_specs=[pl.BlockSpec((None, D_MODEL, COLS), lambda q, k, pos_ref: (q, 0, 0))] + [_ANY] * n_arr,
            scratch_shapes=ride_sems),
        compiler_params=pltpu.CompilerParams(dimension_semantics=("arbitrary", "arbitrary"),
                                             vmem_limit_bytes=VMEM_LIMIT),
    )(pos, h_t, dproj, *ride)


def _grad_w_out(lhs, rhs, k_tile):
    tp = lhs.shape[1]

    def body(a_ref, b_ref, o_ref):
        @pl.when(pl.program_id(1) == 0)
        def _():
            o_ref[...] = jnp.zeros(o_ref.shape, F32)

        o_ref[...] += _dot_tn(a_ref[...], b_ref[...]).reshape(N_DEV, ROWS_OUT, D_MODEL)

    return pl.pallas_call(
        body, name="grad_w_out", grid=(3, tp // k_tile),
        out_shape=jax.ShapeDtypeStruct((N_DEV, 3, ROWS_OUT, D_MODEL), F32),
        in_specs=[pl.BlockSpec((None, k_tile, D_MODEL), lambda w, k: (w, k, 0)),
                  pl.BlockSpec((None, k_tile, D_MODEL), lambda w, k: (w, k, 0))],
        out_specs=pl.BlockSpec((N_DEV, None, ROWS_OUT, D_MODEL), lambda w, k: (0, w, 0, 0)),
        compiler_params=pltpu.CompilerParams(dimension_semantics=("arbitrary", "arbitrary"),
                                             vmem_limit_bytes=VMEM_LIMIT),
    )(lhs, rhs)


def _adamw_math(w, g, m, v):
    m = ADAM_B1 * m + (1.0 - ADAM_B1) * g
    v = ADAM_B2 * v + (1.0 - ADAM_B2) * (g * g)
    m_hat = m / (1.0 - ADAM_B1 ** ADAM_STEP)
    v_hat = v / (1.0 - ADAM_B2 ** ADAM_STEP)
    delta = -ADAM_LR * (m_hat / (jnp.sqrt(v_hat) + ADAM_EPS) + ADAM_WD * w)
    return delta, m, v


def _adamw_sharded(pos, mine, theirs, landed, weights, row_tile, name):
    rows, n = weights[0][0].shape
    n_slots = mine.shape[0]
    per_shard = rows // row_tile
    assert per_shard == 1 or len(weights) == 1

    def mine_map(j, t, pos_ref):
        chip = 2 * pos_ref[0] + pos_ref[1]
        return (2 * chip + pos_ref[2] if n_slots == N_DEV else chip), j * per_shard + t, 0

    def theirs_map(j, t, pos_ref):
        return 2 * pos_ref[0] + pos_ref[1], j * per_shard + t, 0

    def body(pos_ref, mine_ref, theirs_ref, land_ref, *refs):
        ins, outs = refs[:3 * len(weights)], refs[3 * len(weights):]
        g = mine_ref[...] + theirs_ref[...]
        for k in range(3):
            g = g + land_ref[k].astype(F32)
        for j in range(len(weights)):
            @pl.when(pl.program_id(0) == j)
            def _(j=j):
                w_ref, m_ref, v_ref = ins[3 * j:3 * j + 3]
                delta, m_new, v_new = _adamw_math(w_ref[...], g, m_ref[...], v_ref[...])
                for ref, val in zip(outs[4 * j:4 * j + 4], (g, delta, m_new, v_new)):
                    ref[...] = val

    tile = pl.BlockSpec((row_tile, n), lambda j, t, pos_ref: (t, 0))
    res = pl.pallas_call(
        body, name=name,
        out_shape=[jax.ShapeDtypeStruct((rows, n), F32)] * (4 * len(weights)),
        grid_spec=pltpu.PrefetchScalarGridSpec(
            num_scalar_prefetch=1, grid=(len(weights), per_shard),
            in_specs=[pl.BlockSpec((None, row_tile, n), mine_map), pl.BlockSpec((None, row_tile, n), theirs_map),
                      pl.BlockSpec((3, row_tile, n), lambda j, t, pos_ref: (0, j * per_shard + t, 0))]
            + [tile] * (3 * len(weights)),
            out_specs=[tile] * (4 * len(weights))),
        compiler_params=pltpu.CompilerParams(dimension_semantics=("arbitrary", "arbitrary")),
    )(pos, mine, theirs, landed, *[a for wmv in weights for a in wmv])
    return [res[4 * j:4 * j + 4] for j in range(len(weights))]


def _adamw_small(gathered, gathered_cols, params):
    n_par, n_src = len(params), len(gathered)

    def body(*refs):
        g_refs, gc_refs = refs[:n_src], refs[n_src:2 * n_src]
        ins = refs[2 * n_src:2 * n_src + 3 * n_par]
        outs = refs[2 * n_src + 3 * n_par:]
        loss_ref = outs[4 * n_par]

        def reduced(ref, row, n_rows):
            g = ref[0, row:row + n_rows, :]
            for d in range(1, N_DEV):
                g = g + ref[d, row:row + n_rows, :]
            return g

        for p, (src, row, n_rows, sharded, _, _, _) in enumerate(params):
            g = reduced((gc_refs if sharded else g_refs)[src], row, n_rows)
            w_ref, m_ref, v_ref = ins[3 * p:3 * p + 3]
            delta, m_new, v_new = _adamw_math(w_ref[...], g, m_ref[...], v_ref[...])
            outs[4 * p][...] = g
            outs[4 * p + 1][...] = delta
            outs[4 * p + 2][...] = m_new
            outs[4 * p + 3][...] = v_new
        loss = jnp.sum(reduced(g_refs[0], ROW_LOSS, 1), axis=1, keepdims=True)
        loss_ref[...] = jnp.broadcast_to(loss, loss_ref.shape)

    out_shape = []
    for (_, _, _, _, w, _, _) in params:
        out_shape += [jax.ShapeDtypeStruct(w.shape, F32)] * 4
    out_shape.append(jax.ShapeDtypeStruct((1, LANES), F32))
    flat = [a for (_, _, _, _, w, m, v) in params for a in (w, m, v)]
    return pl.pallas_call(
        body, name="adamw_small", out_shape=out_shape,
        in_specs=[_VMEM] * (2 * n_src + len(flat)), out_specs=[_VMEM] * len(out_shape),
    )(*gathered, *gathered_cols, *flat)


def _pad_rows(a, rows):
    return jnp.concatenate([a, jnp.zeros((rows - a.shape[0], a.shape[1]), a.dtype)], axis=0)


def kernel(x, meta_tokens, norm_g, w_in, conv_a_w, conv_a_b, ln_a_g, ln_a_b, w_a_out, b_a_out, conv_b_w, w_b_out, w_out, final_g, loss_target, m_meta_tokens, m_norm_g, m_w_in, m_conv_a_w, m_conv_a_b, m_ln_a_g, m_ln_a_b, m_w_a_out, m_b_a_out, m_conv_b_w, m_w_b_out, m_w_out, m_final_g, v_meta_tokens, v_norm_g, v_w_in, v_conv_a_w, v_conv_a_b, v_ln_a_g, v_ln_a_b, v_w_a_out, v_b_a_out, v_conv_b_w, v_w_b_out, v_w_out, v_final_g):
    seq = x.shape[1]
    assert x.shape == (1, seq, D_MODEL) and seq % TILE == 0 and w_in.shape == (1, D_MODEL, COLS)
    n_tiles = seq // TILE + 1
    tp = n_tiles * TILE
    pos = jnp.stack([lax.axis_index("x"), lax.axis_index("y"), lax.axis_index("c")]).astype(jnp.int32)
    me = 4 * pos[0] + 2 * pos[1] + pos[2]
    x2d = x[0]
    tgt2d = loss_target[0]

    small = jnp.concatenate([meta_tokens, _pad_rows(conv_a_w[0], 32), _pad_rows(conv_b_w[0], SUBLANES)], axis=0)
    final_g2 = final_g.reshape(1, D_MODEL)

    w_out_shards = [w[0].astype(BF16) for w in (w_a_out, w_b_out, w_out)]
    h_t, proj, meta_tile, small_params, w_in_all, *w_out_all = _gather_norm_proj(
        pos, x2d, small[None], norm_g, w_in[0].astype(BF16), w_out_shards, 3)
    small_params = small_params.transpose(1, 0, 2).reshape(small.shape[0], D_MODEL)
    conv_a_full, conv_b_full = small_params[N_META:N_META + 32], small_params[N_META + 32:]
    w_out_all = [w.reshape(D_MODEL, D_MODEL) for w in w_out_all]
    w_out_all_t = [w.T for w in w_out_all]
    dproj, ds1, lhs, rhs, small_a = _fused_pass(
        proj, x2d, tgt2d, meta_tile, conv_a_full, conv_a_b, ln_a_g, ln_a_b, b_a_out, conv_b_full, final_g2,
        w_out_all[0], w_out_all[1], w_out_all[2], w_out_all_t[0], w_out_all_t[1], w_out_all_t[2], n_tiles)
    k_tile = tp // 3
    gw_out = _grad_w_out(lhs, rhs, k_tile).reshape(N_DEV, 3 * ROWS_OUT, D_MODEL)
    gw_far, their_out, small_a_all = _grad_w_in_half(
        pos, h_t, dproj, k_tile, True, [("sibling", (gw_out,)), ("all", (small_a[None],))], "grad_w_in_far")
    parts_out = _chip_partial(pos, gw_out, their_out, (1, 2, 3), BF16, ROWS_OUT, "rs_parts_w_out")
    gw_near, their_in, land_out = _grad_w_in_half(
        pos, h_t, dproj, k_tile, False, [("sibling_half", (gw_far,)), ("chips", (parts_out,))], "grad_w_in_near")
    parts_in = _chip_partial(pos, gw_near, their_in, (1, 2, 3), BF16, 256, "rs_parts_w_in")
    send_sems, recv_sems, parts_in, land_in, token = _chips_start(parts_in, "rs_chips_start")
    grad_x, small_b = _input_bwd(dproj, ds1, x2d, meta_tile, norm_g + token[0, 0], w_in_all, min(256, seq))
    (small_b_all,) = _exchange("all", [small_b[None]], "gather_small_grads")

    small_grads = [small_a_all, small_b_all]
    small_cols = [lax.dynamic_slice_in_dim(g, me * LANES, LANES, axis=2) for g in small_grads]

    res_out = _adamw_sharded(
        pos, gw_out, their_out, land_out,
        [(w_a_out[0], m_w_a_out[0], v_w_a_out[0]), (w_b_out[0], m_w_b_out[0], v_w_b_out[0]),
         (w_out[0], m_w_out[0], v_w_out[0])], ROWS_OUT, "adamw_w_out")
    land_in = _chips_wait(send_sems, recv_sems, parts_in, land_in, res_out[2][0], "rs_chips_wait")
    (res_in,) = _adamw_sharded(pos, gw_near, their_in, land_in, [(w_in[0], m_w_in[0], v_w_in[0])], 128, "adamw_w_in")
    params = [
        (1, ROW_META, N_META, True, meta_tokens, m_meta_tokens, v_meta_tokens),
        (1, ROW_NORM_G, 1, False, norm_g, m_norm_g, v_norm_g),
        (0, ROW_CONV_A_W, CONV_A, True, conv_a_w[0], m_conv_a_w[0], v_conv_a_w[0]),
        (0, ROW_CONV_A_B, 1, False, conv_a_b, m_conv_a_b, v_conv_a_b),
        (0, ROW_LN_G, 1, False, ln_a_g, m_ln_a_g, v_ln_a_g),
        (0, ROW_LN_B, 1, False, ln_a_b, m_ln_a_b, v_ln_a_b),
        (0, ROW_B_A_OUT, 1, False, b_a_out, m_b_a_out, v_b_a_out),
        (0, ROW_CONV_B_W, CONV_B, True, conv_b_w[0], m_conv_b_w[0], v_conv_b_w[0]),
        (0, ROW_FINAL_G, 1, False, final_g2, m_final_g.reshape(1, D_MODEL), v_final_g.reshape(1, D_MODEL)),
    ]
    res_small = _adamw_small(small_grads, small_cols, params)
    loss = res_small[-1][0, 0]

    def small_res(p, kind, shape):
        return res_small[4 * p + kind].reshape(shape)

    per_weight = []
    for kind in range(4):
        per_weight.append([
            small_res(0, kind, meta_tokens.shape),
            small_res(1, kind, norm_g.shape),
            res_in[kind].reshape(w_in.shape),
            small_res(2, kind, conv_a_w.shape),
            small_res(3, kind, conv_a_b.shape),
            small_res(4, kind, ln_a_g.shape),
            small_res(5, kind, ln_a_b.shape),
            res_out[0][kind].reshape(w_a_out.shape),
            small_res(6, kind, b_a_out.shape),
            small_res(7, kind, conv_b_w.shape),
            res_out[1][kind].reshape(w_b_out.shape),
            res_out[2][kind].reshape(w_out.shape),
            small_res(8, kind, final_g.shape),
        ])
    return (loss, grad_x.reshape(x.shape), *per_weight[0], *per_weight[1], *per_weight[2], *per_weight[3])
```

```python
import functools

import jax
import jax.numpy as jnp
from jax import lax
from jax.experimental import pallas as pl
from jax.experimental.pallas import tpu as pltpu

D_MODEL = 1024
N_META = 16
N_DEV = 8
D_IN = 9 * D_MODEL
COLS = D_IN // N_DEV
ROWS_OUT = D_MODEL // N_DEV
CONV_A = 31
CONV_B = 3
EPS = 1e-6

ADAM_LR = 0.001
ADAM_B1 = 0.9
ADAM_B2 = 0.999
ADAM_EPS = 1e-08
ADAM_WD = 0.01
ADAM_STEP = 10

TILE = 128
LANES = 128
N_CHUNK = D_MODEL // LANES
HALO = 32
SUBLANES = 8
VMEM_LIMIT = 56 * 1024 * 1024

ROW_FINAL_G, ROW_B_A_OUT, ROW_LN_G, ROW_LN_B, ROW_CONV_A_B, ROW_LOSS = 0, 1, 2, 3, 4, 5
ROW_CONV_A_W, ROW_CONV_B_W, SMALL_A_ROWS = 8, 40, 48
ROW_NORM_G, ROW_META, SMALL_B_ROWS = 0, 8, 24

MESH = pl.DeviceIdType.MESH
_ANY = pl.BlockSpec(memory_space=pl.ANY)
_VMEM = pl.BlockSpec(memory_space=pltpu.VMEM)
BF16 = jnp.bfloat16
F32 = jnp.float32


def _resident(shape):
    return pl.BlockSpec(shape, lambda *_: (0,) * len(shape), pipeline_mode=pl.Buffered(1))


def _sigmoid(v):
    return jax.nn.sigmoid(v)


def _dot(a, b):
    return jnp.dot(a, b, preferred_element_type=F32)


def _dot_nt(a, b):
    return lax.dot_general(a, b, (((1,), (1,)), ((), ())), preferred_element_type=F32)


def _dot_tn(a, b):
    return lax.dot_general(a, b, (((0,), (0,)), ((), ())), preferred_element_type=F32)


def _colsum(v):
    return jnp.sum(v, axis=0, keepdims=True)


def _rowmean(v):
    parts = [v[:, LANES * c:LANES * (c + 1)] for c in range(v.shape[1] // LANES)]
    return jnp.sum(functools.reduce(jnp.add, parts), axis=-1, keepdims=True) * (1.0 / v.shape[1])


def _fold8(v):
    parts = [v[SUBLANES * g:SUBLANES * (g + 1)] for g in range(v.shape[0] // SUBLANES)]
    return functools.reduce(jnp.add, parts)


def _sibling_copies(srcs, dsts, send_sems, recv_sems):
    x, y, c = lax.axis_index("x"), lax.axis_index("y"), lax.axis_index("c")
    return [pltpu.make_async_remote_copy(
        src_ref=src.at[2 * q + (1 - c)], dst_ref=dst.at[q],
        send_sem=send_sems.at[4 * a + q], recv_sem=recv_sems.at[4 * a + q],
        device_id=(x, y, 1 - c), device_id_type=MESH)
        for a, (src, dst) in enumerate(zip(srcs, dsts)) for q in range(4)]


def _chip_copies(srcs, dsts, send_sems, recv_sems):
    x, y, c = lax.axis_index("x"), lax.axis_index("y"), lax.axis_index("c")
    targets = [(x, 1 - y, c), (1 - x, y, c), (1 - x, 1 - y, c)]
    return [pltpu.make_async_remote_copy(
        src_ref=src.at[k], dst_ref=dst.at[k],
        send_sem=send_sems.at[3 * a + k], recv_sem=recv_sems.at[3 * a + k],
        device_id=targets[k], device_id_type=MESH)
        for a, (src, dst) in enumerate(zip(srcs, dsts)) for k in range(3)]


def _sibling_half_copies(srcs, dsts, send_sems, recv_sems):
    x, y, c = lax.axis_index("x"), lax.axis_index("y"), lax.axis_index("c")
    return [pltpu.make_async_remote_copy(
        src_ref=src.at[q], dst_ref=dst.at[q],
        send_sem=send_sems.at[4 * a + q], recv_sem=recv_sems.at[4 * a + q],
        device_id=(x, y, 1 - c), device_id_type=MESH)
        for a, (src, dst) in enumerate(zip(srcs, dsts)) for q in range(4)]


def _all_copies(srcs, dsts, send_sems, recv_sems):
    x, y, c = lax.axis_index("x"), lax.axis_index("y"), lax.axis_index("c")
    mine = 4 * x + 2 * y + c
    copies = []
    for a, (src, dst) in enumerate(zip(srcs, dsts)):
        copies.append(pltpu.make_async_copy(src.at[0], dst.at[mine], send_sems.at[N_DEV * a]))
        for k in range(1, N_DEV):
            copies.append(pltpu.make_async_remote_copy(
                src_ref=src.at[0], dst_ref=dst.at[mine],
                send_sem=send_sems.at[N_DEV * a + k], recv_sem=recv_sems.at[N_DEV * a + k],
                device_id=(x ^ (k >> 2), y ^ ((k >> 1) & 1), c ^ (k & 1)), device_id_type=MESH))
    return copies


_EXCHANGES = {"sibling": (4, _sibling_copies, 4), "sibling_half": (4, _sibling_half_copies, 4),
              "chips": (3, _chip_copies, 3), "all": (N_DEV, _all_copies, N_DEV)}


def _exchange_shapes(kind, arrays):
    per_array, _, slots = _EXCHANGES[kind]
    out_shape = [jax.ShapeDtypeStruct((slots,) + a.shape[1:], a.dtype) for a in arrays]
    sems = [pltpu.SemaphoreType.DMA((per_array * len(arrays),))] * 2
    return out_shape, sems


def _ride_shapes(rides):
    shapes, sems = [], []
    for kind, arrays in rides:
        ride_shapes, ride_sems = _exchange_shapes(kind, arrays)
        shapes += ride_shapes
        sems += ride_sems
    return shapes, sems


def _riding(body, n_in, n_out, rides, is_first, is_last):
    counts = [len(arrays) for _, arrays in rides]
    n_arr = sum(counts)

    def wrapped(*refs):
        ins, srcs = refs[:n_in], refs[n_in:n_in + n_arr]
        outs = refs[n_in + n_arr:n_in + n_arr + n_out]
        dsts = refs[n_in + n_arr + n_out:n_in + 2 * n_arr + n_out]
        first_sem = len(refs) - 2 * len(rides)
        scratch, sems = refs[n_in + 2 * n_arr + n_out:first_sem], refs[first_sem:]

        def copies():
            made, at = [], 0
            for r, ((kind, _), n) in enumerate(zip(rides, counts)):
                made += _EXCHANGES[kind][1](srcs[at:at + n], dsts[at:at + n], sems[2 * r], sems[2 * r + 1])
                at += n
            return made

        @pl.when(is_first())
        def _():
            for cp in copies():
                cp.start()

        body(*ins, *outs, *scratch)

        @pl.when(is_last())
        def _():
            for cp in copies():
                cp.wait()

    return wrapped


_HBM = pl.BlockSpec(memory_space=pltpu.HBM)
_SEM = pl.BlockSpec(memory_space=pltpu.SEMAPHORE)
_FLOWS = pltpu.SideEffectType.DATAFLOW_SIDE_EFFECTING


def _chips_start(parts, name):
    def body(parts_ref, land_ref, send_sems, recv_sems, parts_thru, land_thru, token):
        for cp in _chip_copies([parts_ref], [land_ref], send_sems, recv_sems):
            cp.start()
        token[...] = jnp.zeros(token.shape, token.dtype)

    hbm = pltpu.HBM(parts.shape, parts.dtype)
    return pl.pallas_call(
        body, name=name,
        out_shape=(pltpu.SemaphoreType.DMA((3,)), pltpu.SemaphoreType.DMA((3,)), hbm, hbm,
                   jax.ShapeDtypeStruct((SUBLANES, LANES), F32)),
        in_specs=(_HBM, _HBM), out_specs=(_SEM, _SEM, _HBM, _HBM, _VMEM),
        input_output_aliases={0: 2, 1: 3},
        compiler_params=pltpu.CompilerParams(has_side_effects=_FLOWS),
    )(pltpu.with_memory_space_constraint(parts, pltpu.HBM),
      pltpu.with_memory_space_constraint(lax.empty(parts.shape, parts.dtype), pltpu.HBM))


def _chips_wait(send_sems, recv_sems, parts_thru, land_thru, after, name):
    def body(parts_ref, land_ref, send_sems, recv_sems, after_ref, parts_dead, land_out):
        for cp in _chip_copies([parts_ref], [land_ref], send_sems, recv_sems):
            cp.wait_send()
            cp.wait_recv()

    hbm = pltpu.HBM(parts_thru.shape, parts_thru.dtype)
    return pl.pallas_call(
        body, name=name, out_shape=(hbm, hbm),
        in_specs=(_HBM, _HBM, _SEM, _SEM, _ANY), out_specs=(_HBM, _HBM), input_output_aliases={0: 0, 1: 1},
        compiler_params=pltpu.CompilerParams(has_side_effects=_FLOWS),
    )(parts_thru, land_thru, send_sems, recv_sems, after)[1]


def _exchange(kind, arrays, name):
    n_arr = len(arrays)
    out_shape, sems = _exchange_shapes(kind, arrays)

    def body(*refs):
        copies = _EXCHANGES[kind][1](refs[:n_arr], refs[n_arr:2 * n_arr], *refs[2 * n_arr:])
        for cp in copies:
            cp.start()
        for cp in copies:
            cp.wait()

    return pl.pallas_call(body, name=name, out_shape=out_shape, in_specs=[_ANY] * n_arr,
                          out_specs=[_ANY] * n_arr, scratch_shapes=sems)(*arrays)


def _chip_partial(pos, mine, theirs, relations, out_dtype, row_tile, name):
    n_slots, m, n = mine.shape
    q0 = relations[0]

    def chip_of(qi, pos_ref):
        q = qi + q0
        return pos_ref[0] ^ (q >> 1), pos_ref[1] ^ (q & 1)

    def mine_map(qi, t, pos_ref):
        px, py = chip_of(qi, pos_ref)
        return (4 * px + 2 * py + pos_ref[2] if n_slots == N_DEV else 2 * px + py), t, 0

    def theirs_map(qi, t, pos_ref):
        px, py = chip_of(qi, pos_ref)
        return 2 * px + py, t, 0

    def body(pos_ref, a_ref, b_ref, o_ref):
        o_ref[...] = (a_ref[...] + b_ref[...]).astype(out_dtype)

    return pl.pallas_call(
        body, name=name,
        out_shape=jax.ShapeDtypeStruct((len(relations), m, n), out_dtype),
        grid_spec=pltpu.PrefetchScalarGridSpec(
            num_scalar_prefetch=1, grid=(len(relations), m // row_tile),
            in_specs=[pl.BlockSpec((None, row_tile, n), mine_map), pl.BlockSpec((None, row_tile, n), theirs_map)],
            out_specs=pl.BlockSpec((None, row_tile, n), lambda qi, t, pos_ref: (qi, t, 0))),
        compiler_params=pltpu.CompilerParams(dimension_semantics=("arbitrary", "arbitrary")),
    )(pos, mine, theirs)


PARTS = ((0, 512), (512, 640))


def _gather_norm_proj(pos, x2d, small_shard, norm_g, w_in_shard, w_out_shards, n_chunk):
    seq = x2d.shape[0]
    n_tiles = seq // TILE + 1
    tp = n_tiles * TILE
    n_parts = len(PARTS)
    widest = max(width for _, width in PARTS)
    units = [(s, u) for s in range(2) for u in range(n_parts)]
    for first in (2, 5):
        units += [(first + j, u) for u in range(n_parts) for j in range(2)] + [(first + 2, u) for u in range(n_parts)]
    n_units = len(units)
    n_steps = n_tiles + n_units
    chunk = tp // n_chunk

    def body(pos_ref, x_ref, g_ref, small_ref, win_ref, wa_ref, wb_ref, wo_ref,
             ht_ref, proj_ref, meta_ref, small_all, win_all, wa_all, wb_all, wo_all,
             h_all, wbuf, rbuf, small_buf, send_sems, recv_sems, local_sems, small_send, small_recv):
        g = pl.program_id(0)
        x, y, c = lax.axis_index("x"), lax.axis_index("y"), lax.axis_index("c")
        me, sibling = (x, y, c), (x, y, 1 - c)
        chips = [(1 - x, y), (x, 1 - y), (1 - x, 1 - y)]
        shards = (win_ref, wa_ref, wb_ref, wo_ref)
        gathered = (win_all, wa_all, wb_all, wo_all)
        n_arrays = len(shards)
        blocks = [me, sibling] + [(*chip, c) for chip in chips] + [(*chip, 1 - c) for chip in chips]

        def index(block):
            px, py, pc = block
            return 4 * px + 2 * py + pc

        def part(ref, a, u):
            return ref.at[:, pl.ds(PARTS[u][0], PARTS[u][1])] if a == 0 else ref

        def slot(a, block, u):
            return part(gathered[a].at[index(block)], a, u)

        def sem(a, k, u):
            return n_parts * k + u if a == 0 else 7 * n_parts + 7 * (a - 1) + k

        def copy(a, k, block, to, u=0, from_shard=False):
            return pltpu.make_async_remote_copy(
                src_ref=part(shards[a], a, u) if from_shard else slot(a, block, u), dst_ref=slot(a, block, u),
                send_sem=send_sems.at[sem(a, k, u)], recv_sem=recv_sems.at[sem(a, k, u)],
                device_id=to, device_id_type=MESH)

        def keep(a):
            return pltpu.make_async_copy(shards[a], gathered[a].at[index(me)], local_sems.at[a])

        def load(m):
            s, u = units[m]
            src = part(win_ref, 0, u) if s == 0 else slot(0, blocks[s], u)
            return pltpu.make_async_copy(src, wbuf.at[m % 2, :, 0:PARTS[u][1]], local_sems.at[n_arrays + m % 2])

        def store(m):
            s, u = units[m]
            col0 = pl.multiple_of(index(blocks[s]) * COLS + PARTS[u][0], LANES)
            return pltpu.make_async_copy(rbuf.at[m % 2, :, 0:PARTS[u][1]],
                                         proj_ref.at[:, pl.ds(col0, PARTS[u][1])], local_sems.at[n_arrays + 2 + m % 2])

        def by_x(a, u):
            return u == 0 if a == 0 else a < 3

        def relay(a, u=0):
            src, to = (blocks[3], blocks[2]) if by_x(a, u) else (blocks[2], blocks[3])
            return copy(a, 3, src, to, u)

        def arrive(m):
            s, u = units[m]
            if s == 1:
                copy(0, 0, sibling, me, u).wait_recv()
            elif 2 <= s <= 4:
                copy(0, s - 1, blocks[s], me, u).wait_recv()
                copy(0, s + 2, blocks[s], sibling, u).start()
                if s < 4 and by_x(0, u) == (s == 3):
                    relay(0, u).start()
            elif s >= 5:
                copy(0, s - 1, blocks[s], me, u).wait_recv()
                if u == 0:
                    for a in range(1, 4):
                        copy(a, s - 4, blocks[s - 3], me).wait_recv()
                        copy(a, s - 1, blocks[s - 3], sibling).start()
                        if s < 7 and by_x(a, 0) == (s == 6):
                            relay(a).start()

        targets = [sibling, blocks[2], blocks[3]]

        def small_copies():
            return _all_copies([small_ref], [small_all], small_send, small_recv)

        @pl.when(g == 0)
        def _():
            for cp in small_copies():
                cp.start()
            for a in range(n_arrays):
                keep(a).start()
            for u in range(n_parts):
                for k, to in enumerate(targets):
                    copy(0, k, me, to, u, from_shard=True).start()
            for a in range(1, 4):
                for k, to in enumerate(targets):
                    copy(a, k, me, to, from_shard=True).start()
            load(0).start()

        @pl.when(g == n_tiles - 2)
        def _():
            for cp in small_copies():
                cp.wait()
            fetch = pltpu.make_async_copy(small_all, small_buf, local_sems.at[n_arrays + 4])
            fetch.start()
            fetch.wait()
            meta_ref[0:TILE - N_META, :] = jnp.zeros((TILE - N_META, D_MODEL), F32)
            meta_ref[TILE - N_META:TILE, :] = jnp.concatenate([small_buf[d, 0:N_META, :] for d in range(N_DEV)], axis=1)

        @pl.when(g < n_tiles)
        def _():
            s0 = jnp.where(g == n_tiles - 1, meta_ref[...], x_ref[...])
            r = lax.rsqrt(_rowmean(s0 * s0) + EPS)
            h32 = (s0 * r) * g_ref[...]
            ht_ref[...] = h32.T.astype(BF16)
            h_all[pl.ds(pl.multiple_of(g * TILE, TILE), TILE), :] = h32.astype(BF16)

        for m in range(n_units):
            @pl.when(g == n_tiles + m)
            def _(m=m):
                load(m).wait()
                if m + 1 < n_units:
                    arrive(m + 1)
                    load(m + 1).start()
                if m >= 2:
                    store(m - 2).wait()

        m_now = jnp.maximum(g - n_tiles, 0)
        u_now = functools.reduce(jnp.add, [jnp.where(m_now == m, u, 0) for m, (_, u) in enumerate(units)])
        for u, (_, width) in enumerate(PARTS):
            @pl.when((g >= n_tiles) & (u_now == u))
            def _(width=width):
                w = wbuf[m_now % 2, :, 0:width]
                for r in range(n_chunk):
                    rbuf[m_now % 2, r * chunk:(r + 1) * chunk, 0:width] = _dot(h_all[r * chunk:(r + 1) * chunk, :], w)

        for m in range(n_units):
            @pl.when(g == n_tiles + m)
            def _(m=m):
                store(m).start()

        @pl.when(g == n_steps - 1)
        def _():
            store(n_units - 2).wait()
            store(n_units - 1).wait()
            for a in range(1, 4):
                copy(a, 0, sibling, me).wait_recv()
                for j in range(3):
                    copy(a, 4 + j, blocks[5 + j], me).wait_recv()
            for a in range(n_arrays):
                for u in range(n_parts if a == 0 else 1):
                    for k, to in enumerate(targets):
                        copy(a, k, me, to, u, from_shard=True).wait_send()
                    relay(a, u).wait_send()
                    for j in range(3):
                        copy(a, 4 + j, blocks[2 + j], sibling, u).wait_send()
                keep(a).wait()

    n_x = n_tiles - 1
    return pl.pallas_call(
        body, name="gather_norm_proj",
        out_shape=[jax.ShapeDtypeStruct((D_MODEL, tp), BF16), jax.ShapeDtypeStruct((tp, D_IN), F32),
                   jax.ShapeDtypeStruct((TILE, D_MODEL), F32), jax.ShapeDtypeStruct((N_DEV,) + small_shard.shape[1:], F32),
                   jax.ShapeDtypeStruct((N_DEV,) + w_in_shard.shape, BF16)]
                  + [jax.ShapeDtypeStruct((N_DEV,) + w.shape, BF16) for w in w_out_shards],
        grid_spec=pltpu.PrefetchScalarGridSpec(
            num_scalar_prefetch=1, grid=(n_steps,),
            in_specs=[pl.BlockSpec((TILE, D_MODEL), lambda g, pos_ref: (jnp.minimum(g, n_x - 1), 0)),
                      _VMEM, _ANY, _ANY, _ANY, _ANY, _ANY],
            out_specs=[pl.BlockSpec((D_MODEL, TILE), lambda g, pos_ref: (0, jnp.minimum(g, n_tiles - 1))),
                       _ANY, _VMEM, _ANY, _ANY, _ANY, _ANY, _ANY],
            scratch_shapes=[pltpu.VMEM((tp, D_MODEL), BF16), pltpu.VMEM((2, D_MODEL, widest), BF16),
                            pltpu.VMEM((2, tp, widest), F32), pltpu.VMEM((N_DEV,) + small_shard.shape[1:], F32),
                            pltpu.SemaphoreType.DMA((7 * n_parts + 21,)), pltpu.SemaphoreType.DMA((7 * n_parts + 21,)),
                            pltpu.SemaphoreType.DMA((9,)),
                            pltpu.SemaphoreType.DMA((N_DEV,)), pltpu.SemaphoreType.DMA((N_DEV,))]),
        compiler_params=pltpu.CompilerParams(dimension_semantics=("arbitrary",), vmem_limit_bytes=VMEM_LIMIT),
    )(pos, x2d, norm_g, small_shard, w_in_shard, *w_out_shards)


C_AVAL, C_AGLU, C_AZ, C_BB, C_BC, C_BX, C_BZ, C_GA, C_GB = (k * D_MODEL for k in range(9))
S_AZ, S_BB, S_BZ, S_GA, S_GB = (k * D_MODEL for k in range(5))


def _fused_pass(proj, x2d, tgt2d, meta_tile, conv_a_w, conv_a_b, ln_a_g, ln_a_b, b_a_out, conv_b_w, final_g,
                w_a, w_b, w_o, w_a_t, w_b_t, w_o_t, n_tiles):
    T = TILE
    tp = n_tiles * T
    inv_d = 1.0 / D_MODEL

    def block_of(tile):
        return jnp.where(tile == 0, n_tiles - 1, tile - 1)

    def cur(i):
        return block_of(jnp.minimum(i, n_tiles - 1))

    def prev(i):
        return block_of(jnp.clip(i - 1, 0, n_tiles - 1))

    def xblk(i):
        return jnp.maximum(jnp.minimum(i, n_tiles - 1) - 1, 0)

    def body(proj_ref, aprev, cprev, x_ref, tgt_ref, meta_ref, caw_ref, cab_ref, lng_ref, lnb_ref, bao_ref, cbw_ref,
             fg_ref, wa_ref, wb_ref, wo_ref, wat_ref, wbt_ref, wot_ref,
             dproj_ref, ds1_ref, lhs_ref, rhs_ref, small_ref,
             ua0_buf, cb_buf, dua1_buf, dc3_buf, stage, ua1_buf, c3_buf,
             dpa_buf, dpb_buf, dcaw8, dcbw8, shift_buf):
        i = pl.program_id(0)
        this, before = i % 2, 1 - i % 2

        @pl.when(i == 0)
        def _init():
            for buf in (ua0_buf, cb_buf, dua1_buf, dc3_buf, dcaw8, dcbw8):
                buf[...] = jnp.zeros(buf.shape, buf.dtype)
            small_ref[...] = jnp.zeros(small_ref.shape, F32)

        @pl.when(i >= 1)
        def _emit_stage():
            dproj_ref[:, C_AZ:C_BC] = stage[:, S_AZ:S_BZ]
            dproj_ref[:, C_BZ:D_IN] = stage[:, S_BZ:S_GB + D_MODEL]

        @pl.when(i < n_tiles)
        def _front():
            def conv_chunk(cc, carry):
                c0 = pl.multiple_of(cc * LANES, LANES)
                lanes = pl.ds(c0, LANES)

                def col(base):
                    return pl.ds(pl.multiple_of(base + cc * LANES, LANES), LANES)

                ua0 = proj_ref[:, col(C_AVAL)] * _sigmoid(proj_ref[:, col(C_AGLU)])
                ua0_buf[this, 0:HALO, lanes] = ua0_buf[before, T:T + HALO, lanes]
                ua0_buf[this, HALO:HALO + T, lanes] = ua0
                acc = jnp.broadcast_to(cab_ref[:, lanes], (T, LANES))
                lead = HALO - (CONV_A - 1)
                for r in range(SUBLANES):
                    taps = [k for k in range(CONV_A) if (k + lead) % SUBLANES == r]
                    rows = T + SUBLANES * max((k + lead) // SUBLANES for k in taps)
                    if r:
                        shift_buf[r, 0:rows, :] = ua0_buf[this, pl.ds(r, rows), lanes]
                    for k in taps:
                        q = (k + lead) // SUBLANES
                        if r:
                            win = shift_buf[r, SUBLANES * q:SUBLANES * q + T, :]
                        else:
                            win = ua0_buf[this, pl.ds(SUBLANES * q, T), lanes]
                        acc = acc + caw_ref[k:k + 1, lanes] * win
                ua1_buf[:, lanes] = acc
                cb = proj_ref[:, col(C_BC)] * proj_ref[:, col(C_BX)]
                cb_buf[this, 0:SUBLANES, lanes] = cb_buf[before, T:T + SUBLANES, lanes]
                cb_buf[this, SUBLANES:SUBLANES + T, lanes] = cb
                lead_b = SUBLANES - (CONV_B - 1)
                acc3 = cbw_ref[0:1, lanes] * cb_buf[this, pl.ds(lead_b, T), lanes]
                for k in range(1, CONV_B):
                    acc3 = acc3 + cbw_ref[k:k + 1, lanes] * cb_buf[this, pl.ds(lead_b + k, T), lanes]
                c3_buf[:, lanes] = acc3
                return carry

            lax.fori_loop(0, N_CHUNK, conv_chunk, 0)

            ua1 = ua1_buf[...]
            xc = ua1 - _rowmean(ua1)
            rstd = lax.rsqrt(_rowmean(xc * xc) + EPS)
            xhat = xc * rstd
            ua2 = xhat * lng_ref[...] + lnb_ref[...]
            sg2 = _sigmoid(ua2)
            ua3 = ua2 * sg2
            a_z = proj_ref[:, C_AZ:C_AZ + D_MODEL]
            sz = _sigmoid(a_z)
            silu_az = a_z * sz
            lhs_ref[0] = (ua3 * silu_az).astype(BF16)
            b_z = proj_ref[:, C_BZ:C_BZ + D_MODEL]
            sbz = _sigmoid(b_z)
            silu_bz = b_z * sbz
            b_b = proj_ref[:, C_BB:C_BB + D_MODEL]
            c3 = c3_buf[...]
            ub = b_b * c3
            lhs_ref[1] = (ub * silu_bz).astype(BF16)

            ya = _dot(lhs_ref[0], wa_ref[...]) + bao_ref[...]
            yb = _dot(lhs_ref[1], wb_ref[...])
            sga = _sigmoid(proj_ref[:, C_GA:C_GA + D_MODEL])
            sgb = _sigmoid(proj_ref[:, C_GB:C_GB + D_MODEL])
            m_b = (sga * ya + sgb * yb).astype(BF16)
            lhs_ref[2] = m_b
            s0 = jnp.where(i == 0, meta_ref[...], x_ref[...])
            s1 = s0 + _dot(m_b, wo_ref[...])
            r1 = lax.rsqrt(_rowmean(s1 * s1) + EPS)
            y = (s1 * r1) * fg_ref[...]
            is_token = (i >= 1).astype(F32)
            err = (y - tgt_ref[...]) * is_token
            small_ref[ROW_LOSS:ROW_LOSS + 1, :] += (0.5 * inv_d) * _colsum(err * err)
            dy = err * inv_d
            small_ref[ROW_FINAL_G:ROW_FINAL_G + 1, :] += _colsum(dy * (s1 * r1))
            gy = dy * fg_ref[...]
            ds1 = r1 * gy - s1 * ((r1 * r1 * r1) * _rowmean(gy * s1))
            ds1_ref[...] = ds1
            ds1_b = ds1.astype(BF16)
            rhs_ref[2] = ds1_b
            dm = _dot(ds1_b, wot_ref[...])
            dya = dm * sga
            dyb = dm * sgb
            stage[:, S_GA:S_GA + D_MODEL] = (dya * ya * (1.0 - sga)).astype(BF16)
            stage[:, S_GB:S_GB + D_MODEL] = (dyb * yb * (1.0 - sgb)).astype(BF16)
            small_ref[ROW_B_A_OUT:ROW_B_A_OUT + 1, :] += _colsum(dya)
            dya_b = dya.astype(BF16)
            dyb_b = dyb.astype(BF16)
            rhs_ref[0] = dya_b
            rhs_ref[1] = dyb_b
            dpa_buf[...] = _dot(dya_b, wat_ref[...])
            dpb_buf[...] = _dot(dyb_b, wbt_ref[...])

            dpa = dpa_buf[...]
            stage[:, S_AZ:S_AZ + D_MODEL] = (dpa * ua3 * (sz + silu_az * (1.0 - sz))).astype(BF16)
            dua2 = dpa * silu_az * (sg2 + ua3 * (1.0 - sg2))
            small_ref[ROW_LN_G:ROW_LN_G + 1, :] += _colsum(dua2 * xhat)
            small_ref[ROW_LN_B:ROW_LN_B + 1, :] += _colsum(dua2)
            dxh = dua2 * lng_ref[...]
            dua1 = rstd * (dxh - _rowmean(dxh) - xhat * _rowmean(dxh * xhat))
            small_ref[ROW_CONV_A_B:ROW_CONV_A_B + 1, :] += _colsum(dua1)
            dua1_buf[this, 0:T, :] = dua1
            dua1_buf[before, T:T + HALO, :] = dua1[0:HALO]
            dpb = dpb_buf[...]
            stage[:, S_BZ:S_BZ + D_MODEL] = (dpb * ub * (sbz + silu_bz * (1.0 - sbz))).astype(BF16)
            dub = dpb * silu_bz
            stage[:, S_BB:S_BB + D_MODEL] = (dub * c3).astype(BF16)
            dc3 = dub * b_b
            dc3_buf[this, 0:T, :] = dc3
            dc3_buf[before, T:T + SUBLANES, :] = dc3[0:SUBLANES]

        @pl.when(i == n_tiles)
        def _no_later_tile():
            dua1_buf[before, T:T + HALO, :] = jnp.zeros((HALO, D_MODEL), F32)
            dc3_buf[before, T:T + SUBLANES, :] = jnp.zeros((SUBLANES, D_MODEL), F32)

        @pl.when(i >= 1)
        def _lagged():
            def convt_chunk(cc, carry):
                c0 = pl.multiple_of(cc * LANES, LANES)
                lanes = pl.ds(c0, LANES)

                def col(base):
                    return pl.ds(pl.multiple_of(base + cc * LANES, LANES), LANES)

                ua0 = ua0_buf[before, HALO:HALO + T, lanes]
                acc = jnp.zeros((T, LANES), F32)
                for r in range(SUBLANES):
                    shifts = [j for j in range(CONV_A) if j % SUBLANES == r]
                    rows = T + shifts[-1] - r
                    if r:
                        shift_buf[r, 0:rows, :] = dua1_buf[before, pl.ds(r, rows), lanes]
                    for j in shifts:
                        k = CONV_A - 1 - j
                        if r:
                            later = shift_buf[r, j - r:j - r + T, :]
                        else:
                            later = dua1_buf[before, pl.ds(j, T), lanes]
                        acc = acc + caw_ref[k:k + 1, lanes] * later
                        dcaw8[SUBLANES * k:SUBLANES * (k + 1), lanes] += _fold8(ua0 * later)
                a_val = aprev[:, col(0)]
                sg = _sigmoid(aprev[:, col(D_MODEL)])
                dproj_ref[:, col(C_AVAL)] = (acc * sg).astype(BF16)
                dproj_ref[:, col(C_AGLU)] = (acc * a_val * (sg * (1.0 - sg))).astype(BF16)

                cb = cb_buf[before, SUBLANES:SUBLANES + T, lanes]
                acc3 = jnp.zeros((T, LANES), F32)
                for j in range(CONV_B):
                    k = CONV_B - 1 - j
                    later = dc3_buf[before, pl.ds(j, T), lanes]
                    acc3 = acc3 + cbw_ref[k:k + 1, lanes] * later
                    dcbw8[SUBLANES * k:SUBLANES * (k + 1), lanes] += _fold8(cb * later)
                dproj_ref[:, col(C_BC)] = (acc3 * cprev[:, col(D_MODEL)]).astype(BF16)
                dproj_ref[:, col(C_BX)] = (acc3 * cprev[:, col(0)]).astype(BF16)
                return carry

            lax.fori_loop(0, N_CHUNK, convt_chunk, 0)

        @pl.when(i == n_tiles)
        def _finish():
            for k in range(CONV_A):
                small_ref[ROW_CONV_A_W + k:ROW_CONV_A_W + k + 1, :] = _colsum(dcaw8[SUBLANES * k:SUBLANES * (k + 1), :])
            for k in range(CONV_B):
                small_ref[ROW_CONV_B_W + k:ROW_CONV_B_W + k + 1, :] = _colsum(dcbw8[SUBLANES * k:SUBLANES * (k + 1), :])

    pair = 2 * D_MODEL
    return pl.pallas_call(
        body, name="fused_pass", grid=(n_tiles + 1,),
        out_shape=[
            jax.ShapeDtypeStruct((tp, D_IN), BF16),
            jax.ShapeDtypeStruct((tp, D_MODEL), F32),
            jax.ShapeDtypeStruct((3, tp, D_MODEL), BF16),
            jax.ShapeDtypeStruct((3, tp, D_MODEL), BF16),
            jax.ShapeDtypeStruct((SMALL_A_ROWS, D_MODEL), F32),
        ],
        in_specs=[
            pl.BlockSpec((T, D_IN), lambda i: (cur(i), 0)),
            pl.BlockSpec((T, pair), lambda i: (prev(i), C_AVAL // pair)),
            pl.BlockSpec((T, pair), lambda i: (prev(i), C_BC // pair)),
            pl.BlockSpec((T, D_MODEL), lambda i: (xblk(i), 0)),
            pl.BlockSpec((T, D_MODEL), lambda i: (xblk(i), 0)),
            _VMEM, _VMEM, _VMEM, _VMEM, _VMEM, _VMEM, _VMEM, _VMEM,
            *[_resident((D_MODEL, D_MODEL)) for _ in range(6)],
        ],
        out_specs=[
            pl.BlockSpec((T, D_IN), lambda i: (prev(i), 0)),
            pl.BlockSpec((T, D_MODEL), lambda i: (cur(i), 0)),
            pl.BlockSpec((3, T, D_MODEL), lambda i: (0, cur(i), 0)),
            pl.BlockSpec((3, T, D_MODEL), lambda i: (0, cur(i), 0)),
            _VMEM,
        ],
        scratch_shapes=[
            pltpu.VMEM((2, HALO + T, D_MODEL), F32),
            pltpu.VMEM((2, SUBLANES + T, D_MODEL), F32),
            pltpu.VMEM((2, T + HALO, D_MODEL), F32),
            pltpu.VMEM((2, T + SUBLANES, D_MODEL), F32),
            pltpu.VMEM((T, 5 * D_MODEL), BF16),
            pltpu.VMEM((T, D_MODEL), F32),
            pltpu.VMEM((T, D_MODEL), F32),
            pltpu.VMEM((T, D_MODEL), F32),
            pltpu.VMEM((T, D_MODEL), F32),
            pltpu.VMEM((32 * SUBLANES, D_MODEL), F32),
            pltpu.VMEM((SUBLANES * SUBLANES, D_MODEL), F32),
            pltpu.VMEM((SUBLANES, T + HALO, LANES), F32),
        ],
        compiler_params=pltpu.CompilerParams(dimension_semantics=("arbitrary",), vmem_limit_bytes=VMEM_LIMIT),
    )(proj, proj, proj, x2d, tgt2d, meta_tile, conv_a_w, conv_a_b, ln_a_g, ln_a_b, b_a_out, conv_b_w, final_g,
      w_a, w_b, w_o, w_a_t, w_b_t, w_o_t)


def _input_bwd(dproj, ds1, x2d, meta_tile, norm_g, w_in_all, row_tile):
    seq = x2d.shape[0]
    n_steps = seq // row_tile
    meta_block = seq // TILE

    def backward(dp_ref, ds1_ref, s0_ref, g_ref, w_ref, out_ref, vec_ref):
        dh = _dot_nt(dp_ref[:, 0:COLS], w_ref[0])
        for j in range(1, N_DEV):
            dh = dh + _dot_nt(dp_ref[:, j * COLS:(j + 1) * COLS], w_ref[j])
        s0v = s0_ref[...]
        r = lax.rsqrt(_rowmean(s0v * s0v) + EPS)
        gh = dh * g_ref[...]
        out_ref[...] = ds1_ref[...] + r * gh - s0v * ((r * r * r) * _rowmean(gh * s0v))
        vec_ref[ROW_NORM_G:ROW_NORM_G + 1, :] += _colsum(dh * (s0v * r))

    def body(dp_ref, ds1_ref, x_ref, dpm_ref, ds1m_ref, meta_ref, g_ref, w_ref, gx_ref, small_ref, gmeta_buf):
        t = pl.program_id(0)

        @pl.when(t == 0)
        def _():
            small_ref[...] = jnp.zeros(small_ref.shape, F32)

        backward(dp_ref, ds1_ref, x_ref, g_ref, w_ref, gx_ref, small_ref)

        @pl.when(t == n_steps - 1)
        def _():
            backward(dpm_ref, ds1m_ref, meta_ref, g_ref, w_ref, gmeta_buf, small_ref)
            small_ref[ROW_META:ROW_META + N_META, :] = gmeta_buf[TILE - N_META:TILE, :]

    return pl.pallas_call(
        body, name="input_bwd", grid=(n_steps,),
        out_shape=[jax.ShapeDtypeStruct(x2d.shape, F32), jax.ShapeDtypeStruct((SMALL_B_ROWS, D_MODEL), F32)],
        in_specs=[pl.BlockSpec((row_tile, D_IN), lambda t: (t, 0)),
                  pl.BlockSpec((row_tile, D_MODEL), lambda t: (t, 0)),
                  pl.BlockSpec((row_tile, D_MODEL), lambda t: (t, 0)),
                  pl.BlockSpec((TILE, D_IN), lambda t: (meta_block, 0)),
                  pl.BlockSpec((TILE, D_MODEL), lambda t: (meta_block, 0)),
                  _VMEM, _VMEM, _resident((N_DEV, D_MODEL, COLS))],
        out_specs=[pl.BlockSpec((row_tile, D_MODEL), lambda t: (t, 0)), _VMEM],
        scratch_shapes=[pltpu.VMEM((TILE, D_MODEL), F32)],
        compiler_params=pltpu.CompilerParams(dimension_semantics=("arbitrary",), vmem_limit_bytes=VMEM_LIMIT),
    )(dproj, ds1, x2d, dproj, ds1, meta_tile, norm_g, w_in_all)


def _grad_w_in_half(pos, h_t, dproj, k_tile, other_side, rides, name):
    tp = h_t.shape[1]
    n_k = tp // k_tile

    def column_block(q, k, pos_ref):
        return k, 2 * q + (1 - pos_ref[2] if other_side else pos_ref[2])

    def body(pos_ref, h_ref, dp_ref, o_ref):
        @pl.when(pl.program_id(1) == 0)
        def _():
            o_ref[...] = jnp.zeros(o_ref.shape, F32)

        o_ref[...] += _dot(h_ref[...], dp_ref[...])

    ride = [a for _, arrays in rides for a in arrays]
    n_arr = len(ride)
    ride_shapes, ride_sems = _ride_shapes(rides)
    body = _riding(body, 3, 1, rides,
                   lambda: (pl.program_id(0) == 0) & (pl.program_id(1) == 0),
                   lambda: (pl.program_id(0) == 3) & (pl.program_id(1) == n_k - 1))
    return pl.pallas_call(
        body, name=name,
        out_shape=[jax.ShapeDtypeStruct((4, D_MODEL, COLS), F32)] + ride_shapes,
        grid_spec=pltpu.PrefetchScalarGridSpec(
            num_scalar_prefetch=1, grid=(4, n_k),
            in_specs=[pl.BlockSpec((D_MODEL, k_tile), lambda q, k, pos_ref: (0, k)),
                      pl.BlockSpec((k_tile, COLS), column_block)] + [_ANY] * n_arr,
            out_specs=[pl.BlockSpec((None, D_MODEL, COLS), lambda q, k, pos_ref: (q, 0, 0))] + [_ANY] * n_arr,
            scratch_shapes=ride_sems),
        compiler_params=pltpu.CompilerParams(dimension_semantics=("arbitrary", "arbitrary"),
                                             vmem_limit_bytes=VMEM_LIMIT),
    )(pos, h_t, dproj, *ride)


def _grad_w_out(lhs, rhs, k_tile):
    tp = lhs.shape[1]

    def body(a_ref, b_ref, o_ref):
        @pl.when(pl.program_id(1) == 0)
        def _():
            o_ref[...] = jnp.zeros(o_ref.shape, F32)

        o_ref[...] += _dot_tn(a_ref[...], b_ref[...]).reshape(N_DEV, ROWS_OUT, D_MODEL)

    return pl.pallas_call(
        body, name="grad_w_out", grid=(3, tp // k_tile),
        out_shape=jax.ShapeDtypeStruct((N_DEV, 3, ROWS_OUT, D_MODEL), F32),
        in_specs=[pl.BlockSpec((None, k_tile, D_MODEL), lambda w, k: (w, k, 0)),
                  pl.BlockSpec((None, k_tile, D_MODEL), lambda w, k: (w, k, 0))],
        out_specs=pl.BlockSpec((N_DEV, None, ROWS_OUT, D_MODEL), lambda w, k: (0, w, 0, 0)),
        compiler_params=pltpu.CompilerParams(dimension_semantics=("arbitrary", "arbitrary"),
                                             vmem_limit_bytes=VMEM_LIMIT),
    )(lhs, rhs)


def _adamw_math(w, g, m, v):
    m = ADAM_B1 * m + (1.0 - ADAM_B1) * g
    v = ADAM_B2 * v + (1.0 - ADAM_B2) * (g * g)
    m_hat = m / (1.0 - ADAM_B1 ** ADAM_STEP)
    v_hat = v / (1.0 - ADAM_B2 ** ADAM_STEP)
    delta = -ADAM_LR * (m_hat / (jnp.sqrt(v_hat) + ADAM_EPS) + ADAM_WD * w)
    return delta, m, v


def _adamw_sharded(pos, mine, theirs, landed, weights, row_tile, name):
    rows, n = weights[0][0].shape
    n_slots = mine.shape[0]
    per_shard = rows // row_tile
    assert per_shard == 1 or len(weights) == 1

    def mine_map(j, t, pos_ref):
        chip = 2 * pos_ref[0] + pos_ref[1]
        return (2 * chip + pos_ref[2] if n_slots == N_DEV else chip), j * per_shard + t, 0

    def theirs_map(j, t, pos_ref):
        return 2 * pos_ref[0] + pos_ref[1], j * per_shard + t, 0

    def body(pos_ref, mine_ref, theirs_ref, land_ref, *refs):
        ins, outs = refs[:3 * len(weights)], refs[3 * len(weights):]
        g = mine_ref[...] + theirs_ref[...]
        for k in range(3):
            g = g + land_ref[k].astype(F32)
        for j in range(len(weights)):
            @pl.when(pl.program_id(0) == j)
            def _(j=j):
                w_ref, m_ref, v_ref = ins[3 * j:3 * j + 3]
                delta, m_new, v_new = _adamw_math(w_ref[...], g, m_ref[...], v_ref[...])
                for ref, val in zip(outs[4 * j:4 * j + 4], (g, delta, m_new, v_new)):
                    ref[...] = val

    tile = pl.BlockSpec((row_tile, n), lambda j, t, pos_ref: (t, 0))
    res = pl.pallas_call(
        body, name=name,
        out_shape=[jax.ShapeDtypeStruct((rows, n), F32)] * (4 * len(weights)),
        grid_spec=pltpu.PrefetchScalarGridSpec(
            num_scalar_prefetch=1, grid=(len(weights), per_shard),
            in_specs=[pl.BlockSpec((None, row_tile, n), mine_map), pl.BlockSpec((None, row_tile, n), theirs_map),
                      pl.BlockSpec((3, row_tile, n), lambda j, t, pos_ref: (0, j * per_shard + t, 0))]
            + [tile] * (3 * len(weights)),
            out_specs=[tile] * (4 * len(weights))),
        compiler_params=pltpu.CompilerParams(dimension_semantics=("arbitrary", "arbitrary")),
    )(pos, mine, theirs, landed, *[a for wmv in weights for a in wmv])
    return [res[4 * j:4 * j + 4] for j in range(len(weights))]


def _adamw_small(gathered, gathered_cols, params):
    n_par, n_src = len(params), len(gathered)

    def body(*refs):
        g_refs, gc_refs = refs[:n_src], refs[n_src:2 * n_src]
        ins = refs[2 * n_src:2 * n_src + 3 * n_par]
        outs = refs[2 * n_src + 3 * n_par:]
        loss_ref = outs[4 * n_par]

        def reduced(ref, row, n_rows):
            g = ref[0, row:row + n_rows, :]
            for d in range(1, N_DEV):
                g = g + ref[d, row:row + n_rows, :]
            return g

        for p, (src, row, n_rows, sharded, _, _, _) in enumerate(params):
            g = reduced((gc_refs if sharded else g_refs)[src], row, n_rows)
            w_ref, m_ref, v_ref = ins[3 * p:3 * p + 3]
            delta, m_new, v_new = _adamw_math(w_ref[...], g, m_ref[...], v_ref[...])
            outs[4 * p][...] = g
            outs[4 * p + 1][...] = delta
            outs[4 * p + 2][...] = m_new
            outs[4 * p + 3][...] = v_new
        loss = jnp.sum(reduced(g_refs[0], ROW_LOSS, 1), axis=1, keepdims=True)
        loss_ref[...] = jnp.broadcast_to(loss, loss_ref.shape)

    out_shape = []
    for (_, _, _, _, w, _, _) in params:
        out_shape += [jax.ShapeDtypeStruct(w.shape, F32)] * 4
    out_shape.append(jax.ShapeDtypeStruct((1, LANES), F32))
    flat = [a for (_, _, _, _, w, m, v) in params for a in (w, m, v)]
    return pl.pallas_call(
        body, name="adamw_small", out_shape=out_shape,
        in_specs=[_VMEM] * (2 * n_src + len(flat)), out_specs=[_VMEM] * len(out_shape),
    )(*gathered, *gathered_cols, *flat)


def _pad_rows(a, rows):
    return jnp.concatenate([a, jnp.zeros((rows - a.shape[0], a.shape[1]), a.dtype)], axis=0)


def kernel(x, meta_tokens, norm_g, w_in, conv_a_w, conv_a_b, ln_a_g, ln_a_b, w_a_out, b_a_out, conv_b_w, w_b_out, w_out, final_g, loss_target, m_meta_tokens, m_norm_g, m_w_in, m_conv_a_w, m_conv_a_b, m_ln_a_g, m_ln_a_b, m_w_a_out, m_b_a_out, m_conv_b_w, m_w_b_out, m_w_out, m_final_g, v_meta_tokens, v_norm_g, v_w_in, v_conv_a_w, v_conv_a_b, v_ln_a_g, v_ln_a_b, v_w_a_out, v_b_a_out, v_conv_b_w, v_w_b_out, v_w_out, v_final_g):
    seq = x.shape[1]
    assert x.shape == (1, seq, D_MODEL) and seq % TILE == 0 and w_in.shape == (1, D_MODEL, COLS)
    n_tiles = seq // TILE + 1
    tp = n_tiles * TILE
    pos = jnp.stack([lax.axis_index("x"), lax.axis_index("y"), lax.axis_index("c")]).astype(jnp.int32)
    me = 4 * pos[0] + 2 * pos[1] + pos[2]
    x2d = x[0]
    tgt2d = loss_target[0]

    small = jnp.concatenate([meta_tokens, _pad_rows(conv_a_w[0], 32), _pad_rows(conv_b_w[0], SUBLANES)], axis=0)
    final_g2 = final_g.reshape(1, D_MODEL)

    w_out_shards = [w[0].astype(BF16) for w in (w_a_out, w_b_out, w_out)]
    h_t, proj, meta_tile, small_params, w_in_all, *w_out_all = _gather_norm_proj(
        pos, x2d, small[None], norm_g, w_in[0].astype(BF16), w_out_shards, 3)
    small_params = small_params.transpose(1, 0, 2).reshape(small.shape[0], D_MODEL)
    conv_a_full, conv_b_full = small_params[N_META:N_META + 32], small_params[N_META + 32:]
    w_out_all = [w.reshape(D_MODEL, D_MODEL) for w in w_out_all]
    w_out_all_t = [w.T for w in w_out_all]
    dproj, ds1, lhs, rhs, small_a = _fused_pass(
        proj, x2d, tgt2d, meta_tile, conv_a_full, conv_a_b, ln_a_g, ln_a_b, b_a_out, conv_b_full, final_g2,
        w_out_all[0], w_out_all[1], w_out_all[2], w_out_all_t[0], w_out_all_t[1], w_out_all_t[2], n_tiles)
    k_tile = tp // 3
    gw_out = _grad_w_out(lhs, rhs, k_tile).reshape(N_DEV, 3 * ROWS_OUT, D_MODEL)
    gw_far, their_out, small_a_all = _grad_w_in_half(
        pos, h_t, dproj, k_tile, True, [("sibling", (gw_out,)), ("all", (small_a[None],))], "grad_w_in_far")
    parts_out = _chip_partial(pos, gw_out, their_out, (1, 2, 3), BF16, ROWS_OUT, "rs_parts_w_out")
    gw_near, their_in, land_out = _grad_w_in_half(
        pos, h_t, dproj, k_tile, False, [("sibling_half", (gw_far,)), ("chips", (parts_out,))], "grad_w_in_near")
    parts_in = _chip_partial(pos, gw_near, their_in, (1, 2, 3), BF16, 256, "rs_parts_w_in")
    send_sems, recv_sems, parts_in, land_in, token = _chips_start(parts_in, "rs_chips_start")
    grad_x, small_b = _input_bwd(dproj, ds1, x2d, meta_tile, norm_g + token[0, 0], w_in_all, min(256, seq))
    (small_b_all,) = _exchange("all", [small_b[None]], "gather_small_grads")

    small_grads = [small_a_all, small_b_all]
    small_cols = [lax.dynamic_slice_in_dim(g, me * LANES, LANES, axis=2) for g in small_grads]

    res_out = _adamw_sharded(
        pos, gw_out, their_out, land_out,
        [(w_a_out[0], m_w_a_out[0], v_w_a_out[0]), (w_b_out[0], m_w_b_out[0], v_w_b_out[0]),
         (w_out[0], m_w_out[0], v_w_out[0])], ROWS_OUT, "adamw_w_out")
    land_in = _chips_wait(send_sems, recv_sems, parts_in, land_in, small_b_all, "rs_chips_wait")
    (res_in,) = _adamw_sharded(pos, gw_near, their_in, land_in, [(w_in[0], m_w_in[0], v_w_in[0])], 128, "adamw_w_in")
    params = [
        (1, ROW_META, N_META, True, meta_tokens, m_meta_tokens, v_meta_tokens),
        (1, ROW_NORM_G, 1, False, norm_g, m_norm_g, v_norm_g),
        (0, ROW_CONV_A_W, CONV_A, True, conv_a_w[0], m_conv_a_w[0], v_conv_a_w[0]),
        (0, ROW_CONV_A_B, 1, False, conv_a_b, m_conv_a_b, v_conv_a_b),
        (0, ROW_LN_G, 1, False, ln_a_g, m_ln_a_g, v_ln_a_g),
        (0, ROW_LN_B, 1, False, ln_a_b, m_ln_a_b, v_ln_a_b),
        (0, ROW_B_A_OUT, 1, False, b_a_out, m_b_a_out, v_b_a_out),
        (0, ROW_CONV_B_W, CONV_B, True, conv_b_w[0], m_conv_b_w[0], v_conv_b_w[0]),
        (0, ROW_FINAL_G, 1, False, final_g2, m_final_g.reshape(1, D_MODEL), v_final_g.reshape(1, D_MODEL)),
    ]
    res_small = _adamw_small(small_grads, small_cols, params)
    loss = res_small[-1][0, 0]

    def small_res(p, kind, shape):
        return res_small[4 * p + kind].reshape(shape)

    per_weight = []
    for kind in range(4):
        per_weight.append([
            small_res(0, kind, meta_tokens.shape),
            small_res(1, kind, norm_g.shape),
            res_in[kind].reshape(w_in.shape),
            small_res(2, kind, conv_a_w.shape),
            small_res(3, kind, conv_a_b.shape),
            small_res(4, kind, ln_a_g.shape),
            small_res(5, kind, ln_a_b.shape),
            res_out[0][kind].reshape(w_a_out.shape),
            small_res(6, kind, b_a_out.shape),
            small_res(7, kind, conv_b_w.shape),
            res_out[1][kind].reshape(w_b_out.shape),
            res_out[2][kind].reshape(w_out.shape),
            small_res(8, kind, final_g.shape),
        ])
    return (loss, grad_x.reshape(x.shape), *per_weight[0], *per_weight[1], *per_weight[2], *per_weight[3])
```

```python
import functools

import jax
import jax.numpy as jnp
from jax import lax
from jax.experimental import pallas as pl
from jax.experimental.pallas import tpu as pltpu

D_MODEL = 1024
N_META = 16
N_DEV = 8
D_IN = 9 * D_MODEL
COLS = D_IN // N_DEV
ROWS_OUT = D_MODEL // N_DEV
CONV_A = 31
CONV_B = 3
EPS = 1e-6

ADAM_LR = 0.001
ADAM_B1 = 0.9
ADAM_B2 = 0.999
ADAM_EPS = 1e-08
ADAM_WD = 0.01
ADAM_STEP = 10

TILE = 128
LANES = 128
N_CHUNK = D_MODEL // LANES
HALO = 32
SUBLANES = 8
VMEM_LIMIT = 56 * 1024 * 1024

ROW_FINAL_G, ROW_B_A_OUT, ROW_LN_G, ROW_LN_B, ROW_CONV_A_B, ROW_LOSS = 0, 1, 2, 3, 4, 5
ROW_CONV_A_W, ROW_CONV_B_W, SMALL_A_ROWS = 8, 40, 48
ROW_NORM_G, ROW_META, SMALL_B_ROWS = 0, 8, 24

MESH = pl.DeviceIdType.MESH
_ANY = pl.BlockSpec(memory_space=pl.ANY)
_VMEM = pl.BlockSpec(memory_space=pltpu.VMEM)
BF16 = jnp.bfloat16
F32 = jnp.float32


def _resident(shape):
    return pl.BlockSpec(shape, lambda *_: (0,) * len(shape), pipeline_mode=pl.Buffered(1))


def _sigmoid(v):
    return jax.nn.sigmoid(v)


def _dot(a, b):
    return jnp.dot(a, b, preferred_element_type=F32)


def _dot_nt(a, b):
    return lax.dot_general(a, b, (((1,), (1,)), ((), ())), preferred_element_type=F32)


def _dot_tn(a, b):
    return lax.dot_general(a, b, (((0,), (0,)), ((), ())), preferred_element_type=F32)


def _colsum(v):
    return jnp.sum(v, axis=0, keepdims=True)


def _rowmean(v):
    parts = [v[:, LANES * c:LANES * (c + 1)] for c in range(v.shape[1] // LANES)]
    return jnp.sum(functools.reduce(jnp.add, parts), axis=-1, keepdims=True) * (1.0 / v.shape[1])


def _fold8(v):
    parts = [v[SUBLANES * g:SUBLANES * (g + 1)] for g in range(v.shape[0] // SUBLANES)]
    return functools.reduce(jnp.add, parts)


def _sibling_copies(srcs, dsts, send_sems, recv_sems):
    x, y, c = lax.axis_index("x"), lax.axis_index("y"), lax.axis_index("c")
    return [pltpu.make_async_remote_copy(
        src_ref=src.at[2 * q + (1 - c)], dst_ref=dst.at[q],
        send_sem=send_sems.at[4 * a + q], recv_sem=recv_sems.at[4 * a + q],
        device_id=(x, y, 1 - c), device_id_type=MESH)
        for a, (src, dst) in enumerate(zip(srcs, dsts)) for q in range(4)]


def _chip_copies(srcs, dsts, send_sems, recv_sems):
    x, y, c = lax.axis_index("x"), lax.axis_index("y"), lax.axis_index("c")
    targets = [(x, 1 - y, c), (1 - x, y, c), (1 - x, 1 - y, c)]
    return [pltpu.make_async_remote_copy(
        src_ref=src.at[k], dst_ref=dst.at[k],
        send_sem=send_sems.at[3 * a + k], recv_sem=recv_sems.at[3 * a + k],
        device_id=targets[k], device_id_type=MESH)
        for a, (src, dst) in enumerate(zip(srcs, dsts)) for k in range(3)]


def _sibling_half_copies(srcs, dsts, send_sems, recv_sems):
    x, y, c = lax.axis_index("x"), lax.axis_index("y"), lax.axis_index("c")
    return [pltpu.make_async_remote_copy(
        src_ref=src.at[q], dst_ref=dst.at[q],
        send_sem=send_sems.at[4 * a + q], recv_sem=recv_sems.at[4 * a + q],
        device_id=(x, y, 1 - c), device_id_type=MESH)
        for a, (src, dst) in enumerate(zip(srcs, dsts)) for q in range(4)]


def _all_copies(srcs, dsts, send_sems, recv_sems):
    x, y, c = lax.axis_index("x"), lax.axis_index("y"), lax.axis_index("c")
    mine = 4 * x + 2 * y + c
    copies = []
    for a, (src, dst) in enumerate(zip(srcs, dsts)):
        copies.append(pltpu.make_async_copy(src.at[0], dst.at[mine], send_sems.at[N_DEV * a]))
        for k in range(1, N_DEV):
            copies.append(pltpu.make_async_remote_copy(
                src_ref=src.at[0], dst_ref=dst.at[mine],
                send_sem=send_sems.at[N_DEV * a + k], recv_sem=recv_sems.at[N_DEV * a + k],
                device_id=(x ^ (k >> 2), y ^ ((k >> 1) & 1), c ^ (k & 1)), device_id_type=MESH))
    return copies


_EXCHANGES = {"sibling": (4, _sibling_copies, 4), "sibling_half": (4, _sibling_half_copies, 4),
              "chips": (3, _chip_copies, 3), "all": (N_DEV, _all_copies, N_DEV)}


def _exchange_shapes(kind, arrays):
    per_array, _, slots = _EXCHANGES[kind]
    out_shape = [jax.ShapeDtypeStruct((slots,) + a.shape[1:], a.dtype) for a in arrays]
    sems = [pltpu.SemaphoreType.DMA((per_array * len(arrays),))] * 2
    return out_shape, sems


def _ride_shapes(rides):
    shapes, sems = [], []
    for kind, arrays in rides:
        ride_shapes, ride_sems = _exchange_shapes(kind, arrays)
        shapes += ride_shapes
        sems += ride_sems
    return shapes, sems


def _riding(body, n_in, n_out, rides, is_first, is_last):
    counts = [len(arrays) for _, arrays in rides]
    n_arr = sum(counts)

    def wrapped(*refs):
        ins, srcs = refs[:n_in], refs[n_in:n_in + n_arr]
        outs = refs[n_in + n_arr:n_in + n_arr + n_out]
        dsts = refs[n_in + n_arr + n_out:n_in + 2 * n_arr + n_out]
        first_sem = len(refs) - 2 * len(rides)
        scratch, sems = refs[n_in + 2 * n_arr + n_out:first_sem], refs[first_sem:]

        def copies():
            made, at = [], 0
            for r, ((kind, _), n) in enumerate(zip(rides, counts)):
                made += _EXCHANGES[kind][1](srcs[at:at + n], dsts[at:at + n], sems[2 * r], sems[2 * r + 1])
                at += n
            return made

        @pl.when(is_first())
        def _():
            for cp in copies():
                cp.start()

        body(*ins, *outs, *scratch)

        @pl.when(is_last())
        def _():
            for cp in copies():
                cp.wait()

    return wrapped


_HBM = pl.BlockSpec(memory_space=pltpu.HBM)
_SEM = pl.BlockSpec(memory_space=pltpu.SEMAPHORE)
_FLOWS = pltpu.SideEffectType.DATAFLOW_SIDE_EFFECTING


def _chips_start(parts, name):
    def body(parts_ref, land_ref, send_sems, recv_sems, parts_thru, land_thru, token):
        for cp in _chip_copies([parts_ref], [land_ref], send_sems, recv_sems):
            cp.start()
        token[...] = jnp.zeros(token.shape, token.dtype)

    hbm = pltpu.HBM(parts.shape, parts.dtype)
    return pl.pallas_call(
        body, name=name,
        out_shape=(pltpu.SemaphoreType.DMA((3,)), pltpu.SemaphoreType.DMA((3,)), hbm, hbm,
                   jax.ShapeDtypeStruct((SUBLANES, LANES), F32)),
        in_specs=(_HBM, _HBM), out_specs=(_SEM, _SEM, _HBM, _HBM, _VMEM),
        input_output_aliases={0: 2, 1: 3},
        compiler_params=pltpu.CompilerParams(has_side_effects=_FLOWS),
    )(pltpu.with_memory_space_constraint(parts, pltpu.HBM),
      pltpu.with_memory_space_constraint(lax.empty(parts.shape, parts.dtype), pltpu.HBM))


def _chips_wait(send_sems, recv_sems, parts_thru, land_thru, after, name):
    def body(parts_ref, land_ref, send_sems, recv_sems, after_ref, parts_dead, land_out):
        for cp in _chip_copies([parts_ref], [land_ref], send_sems, recv_sems):
            cp.wait_send()
            cp.wait_recv()

    hbm = pltpu.HBM(parts_thru.shape, parts_thru.dtype)
    return pl.pallas_call(
        body, name=name, out_shape=(hbm, hbm),
        in_specs=(_HBM, _HBM, _SEM, _SEM, _ANY), out_specs=(_HBM, _HBM), input_output_aliases={0: 0, 1: 1},
        compiler_params=pltpu.CompilerParams(has_side_effects=_FLOWS),
    )(parts_thru, land_thru, send_sems, recv_sems, after)[1]


def _exchange(kind, arrays, after, name):
    n_arr = len(arrays)
    out_shape, sems = _exchange_shapes(kind, arrays)

    def body(*refs):
        copies = _EXCHANGES[kind][1](refs[:n_arr], refs[n_arr + 1:2 * n_arr + 1], *refs[2 * n_arr + 1:])
        for cp in copies:
            cp.start()
        for cp in copies:
            cp.wait()

    return pl.pallas_call(body, name=name, out_shape=out_shape, in_specs=[_ANY] * (n_arr + 1),
                          out_specs=[_ANY] * n_arr, scratch_shapes=sems)(*arrays, after)


def _chip_partial(pos, mine, theirs, relations, out_dtype, row_tile, name):
    n_slots, m, n = mine.shape
    q0 = relations[0]

    def chip_of(qi, pos_ref):
        q = qi + q0
        return pos_ref[0] ^ (q >> 1), pos_ref[1] ^ (q & 1)

    def mine_map(qi, t, pos_ref):
        px, py = chip_of(qi, pos_ref)
        return (4 * px + 2 * py + pos_ref[2] if n_slots == N_DEV else 2 * px + py), t, 0

    def theirs_map(qi, t, pos_ref):
        px, py = chip_of(qi, pos_ref)
        return 2 * px + py, t, 0

    def body(pos_ref, a_ref, b_ref, o_ref):
        o_ref[...] = (a_ref[...] + b_ref[...]).astype(out_dtype)

    return pl.pallas_call(
        body, name=name,
        out_shape=jax.ShapeDtypeStruct((len(relations), m, n), out_dtype),
        grid_spec=pltpu.PrefetchScalarGridSpec(
            num_scalar_prefetch=1, grid=(len(relations), m // row_tile),
            in_specs=[pl.BlockSpec((None, row_tile, n), mine_map), pl.BlockSpec((None, row_tile, n), theirs_map)],
            out_specs=pl.BlockSpec((None, row_tile, n), lambda qi, t, pos_ref: (qi, t, 0))),
        compiler_params=pltpu.CompilerParams(dimension_semantics=("arbitrary", "arbitrary")),
    )(pos, mine, theirs)


PARTS = ((0, 512), (512, 640))


def _gather_norm_proj(pos, x2d, small_shard, norm_g, w_in_shard, w_out_shards, n_chunk):
    seq = x2d.shape[0]
    n_tiles = seq // TILE + 1
    tp = n_tiles * TILE
    n_parts = len(PARTS)
    widest = max(width for _, width in PARTS)
    units = [(s, u) for s in range(2) for u in range(n_parts)]
    for first in (2, 5):
        units += [(first + j, u) for u in range(n_parts) for j in range(2)] + [(first + 2, u) for u in range(n_parts)]
    n_units = len(units)
    n_steps = n_tiles + n_units
    chunk = tp // n_chunk

    def body(pos_ref, x_ref, g_ref, small_ref, win_ref, wa_ref, wb_ref, wo_ref,
             ht_ref, proj_ref, meta_ref, small_all, win_all, wa_all, wb_all, wo_all,
             h_all, wbuf, rbuf, small_buf, send_sems, recv_sems, local_sems, small_send, small_recv):
        g = pl.program_id(0)
        x, y, c = lax.axis_index("x"), lax.axis_index("y"), lax.axis_index("c")
        me, sibling = (x, y, c), (x, y, 1 - c)
        chips = [(1 - x, y), (x, 1 - y), (1 - x, 1 - y)]
        shards = (win_ref, wa_ref, wb_ref, wo_ref)
        gathered = (win_all, wa_all, wb_all, wo_all)
        n_arrays = len(shards)
        blocks = [me, sibling] + [(*chip, c) for chip in chips] + [(*chip, 1 - c) for chip in chips]

        def index(block):
            px, py, pc = block
            return 4 * px + 2 * py + pc

        def part(ref, a, u):
            return ref.at[:, pl.ds(PARTS[u][0], PARTS[u][1])] if a == 0 else ref

        def slot(a, block, u):
            return part(gathered[a].at[index(block)], a, u)

        def sem(a, k, u):
            return n_parts * k + u if a == 0 else 7 * n_parts + 7 * (a - 1) + k

        def copy(a, k, block, to, u=0, from_shard=False):
            return pltpu.make_async_remote_copy(
                src_ref=part(shards[a], a, u) if from_shard else slot(a, block, u), dst_ref=slot(a, block, u),
                send_sem=send_sems.at[sem(a, k, u)], recv_sem=recv_sems.at[sem(a, k, u)],
                device_id=to, device_id_type=MESH)

        def keep(a):
            return pltpu.make_async_copy(shards[a], gathered[a].at[index(me)], local_sems.at[a])

        def load(m):
            s, u = units[m]
            src = part(win_ref, 0, u) if s == 0 else slot(0, blocks[s], u)
            return pltpu.make_async_copy(src, wbuf.at[m % 2, :, 0:PARTS[u][1]], local_sems.at[n_arrays + m % 2])

        def store(m):
            s, u = units[m]
            col0 = pl.multiple_of(index(blocks[s]) * COLS + PARTS[u][0], LANES)
            return pltpu.make_async_copy(rbuf.at[m % 2, :, 0:PARTS[u][1]],
                                         proj_ref.at[:, pl.ds(col0, PARTS[u][1])], local_sems.at[n_arrays + 2 + m % 2])

        def by_x(a, u):
            return u == 0 if a == 0 else a < 3

        def relay(a, u=0):
            src, to = (blocks[3], blocks[2]) if by_x(a, u) else (blocks[2], blocks[3])
            return copy(a, 3, src, to, u)

        def arrive(m):
            s, u = units[m]
            if s == 1:
                copy(0, 0, sibling, me, u).wait_recv()
            elif 2 <= s <= 4:
                copy(0, s - 1, blocks[s], me, u).wait_recv()
                copy(0, s + 2, blocks[s], sibling, u).start()
                if s < 4 and by_x(0, u) == (s == 3):
                    relay(0, u).start()
            elif s >= 5:
                copy(0, s - 1, blocks[s], me, u).wait_recv()
                if u == 0:
                    for a in range(1, 4):
                        copy(a, s - 4, blocks[s - 3], me).wait_recv()
                        copy(a, s - 1, blocks[s - 3], sibling).start()
                        if s < 7 and by_x(a, 0) == (s == 6):
                            relay(a).start()

        targets = [sibling, blocks[2], blocks[3]]

        def small_copies():
            return _all_copies([small_ref], [small_all], small_send, small_recv)

        @pl.when(g == 0)
        def _():
            for cp in small_copies():
                cp.start()
            for a in range(n_arrays):
                keep(a).start()
            for u in range(n_parts):
                for k, to in enumerate(targets):
                    copy(0, k, me, to, u, from_shard=True).start()
            for a in range(1, 4):
                for k, to in enumerate(targets):
                    copy(a, k, me, to, from_shard=True).start()
            load(0).start()

        @pl.when(g == n_tiles - 2)
        def _():
            for cp in small_copies():
                cp.wait()
            fetch = pltpu.make_async_copy(small_all, small_buf, local_sems.at[n_arrays + 4])
            fetch.start()
            fetch.wait()
            meta_ref[0:TILE - N_META, :] = jnp.zeros((TILE - N_META, D_MODEL), F32)
            meta_ref[TILE - N_META:TILE, :] = jnp.concatenate([small_buf[d, 0:N_META, :] for d in range(N_DEV)], axis=1)

        @pl.when(g < n_tiles)
        def _():
            s0 = jnp.where(g == n_tiles - 1, meta_ref[...], x_ref[...])
            r = lax.rsqrt(_rowmean(s0 * s0) + EPS)
            h32 = (s0 * r) * g_ref[...]
            ht_ref[...] = h32.T.astype(BF16)
            h_all[pl.ds(pl.multiple_of(g * TILE, TILE), TILE), :] = h32.astype(BF16)

        for m in range(n_units):
            @pl.when(g == n_tiles + m)
            def _(m=m):
                load(m).wait()
                if m + 1 < n_units:
                    arrive(m + 1)
                    load(m + 1).start()
                if m >= 2:
                    store(m - 2).wait()

        m_now = jnp.maximum(g - n_tiles, 0)
        u_now = functools.reduce(jnp.add, [jnp.where(m_now == m, u, 0) for m, (_, u) in enumerate(units)])
        for u, (_, width) in enumerate(PARTS):
            @pl.when((g >= n_tiles) & (u_now == u))
            def _(width=width):
                w = wbuf[m_now % 2, :, 0:width]
                for r in range(n_chunk):
                    rbuf[m_now % 2, r * chunk:(r + 1) * chunk, 0:width] = _dot(h_all[r * chunk:(r + 1) * chunk, :], w)

        for m in range(n_units):
            @pl.when(g == n_tiles + m)
            def _(m=m):
                store(m).start()

        @pl.when(g == n_steps - 1)
        def _():
            store(n_units - 2).wait()
            store(n_units - 1).wait()
            for a in range(1, 4):
                copy(a, 0, sibling, me).wait_recv()
                for j in range(3):
                    copy(a, 4 + j, blocks[5 + j], me).wait_recv()
            for a in range(n_arrays):
                for u in range(n_parts if a == 0 else 1):
                    for k, to in enumerate(targets):
                        copy(a, k, me, to, u, from_shard=True).wait_send()
                    relay(a, u).wait_send()
                    for j in range(3):
                        copy(a, 4 + j, blocks[2 + j], sibling, u).wait_send()
                keep(a).wait()

    n_x = n_tiles - 1
    return pl.pallas_call(
        body, name="gather_norm_proj",
        out_shape=[jax.ShapeDtypeStruct((D_MODEL, tp), BF16), jax.ShapeDtypeStruct((tp, D_IN), F32),
                   jax.ShapeDtypeStruct((TILE, D_MODEL), F32), jax.ShapeDtypeStruct((N_DEV,) + small_shard.shape[1:], F32),
                   jax.ShapeDtypeStruct((N_DEV,) + w_in_shard.shape, BF16)]
                  + [jax.ShapeDtypeStruct((N_DEV,) + w.shape, BF16) for w in w_out_shards],
        grid_spec=pltpu.PrefetchScalarGridSpec(
            num_scalar_prefetch=1, grid=(n_steps,),
            in_specs=[pl.BlockSpec((TILE, D_MODEL), lambda g, pos_ref: (jnp.minimum(g, n_x - 1), 0)),
                      _VMEM, _ANY, _ANY, _ANY, _ANY, _ANY],
            out_specs=[pl.BlockSpec((D_MODEL, TILE), lambda g, pos_ref: (0, jnp.minimum(g, n_tiles - 1))),
                       _ANY, _VMEM, _ANY, _ANY, _ANY, _ANY, _ANY],
            scratch_shapes=[pltpu.VMEM((tp, D_MODEL), BF16), pltpu.VMEM((2, D_MODEL, widest), BF16),
                            pltpu.VMEM((2, tp, widest), F32), pltpu.VMEM((N_DEV,) + small_shard.shape[1:], F32),
                            pltpu.SemaphoreType.DMA((7 * n_parts + 21,)), pltpu.SemaphoreType.DMA((7 * n_parts + 21,)),
                            pltpu.SemaphoreType.DMA((9,)),
                            pltpu.SemaphoreType.DMA((N_DEV,)), pltpu.SemaphoreType.DMA((N_DEV,))]),
        compiler_params=pltpu.CompilerParams(dimension_semantics=("arbitrary",), vmem_limit_bytes=VMEM_LIMIT),
    )(pos, x2d, norm_g, small_shard, w_in_shard, *w_out_shards)


C_AVAL, C_AGLU, C_AZ, C_BB, C_BC, C_BX, C_BZ, C_GA, C_GB = (k * D_MODEL for k in range(9))
S_AZ, S_BB, S_BZ, S_GA, S_GB = (k * D_MODEL for k in range(5))


def _fused_pass(proj, x2d, tgt2d, meta_tile, conv_a_w, conv_a_b, ln_a_g, ln_a_b, b_a_out, conv_b_w, final_g,
                w_a, w_b, w_o, w_a_t, w_b_t, w_o_t, n_tiles):
    T = TILE
    tp = n_tiles * T
    inv_d = 1.0 / D_MODEL

    def block_of(tile):
        return jnp.where(tile == 0, n_tiles - 1, tile - 1)

    def cur(i):
        return block_of(jnp.minimum(i, n_tiles - 1))

    def prev(i):
        return block_of(jnp.clip(i - 1, 0, n_tiles - 1))

    def xblk(i):
        return jnp.maximum(jnp.minimum(i, n_tiles - 1) - 1, 0)

    def body(proj_ref, aprev, cprev, x_ref, tgt_ref, meta_ref, caw_ref, cab_ref, lng_ref, lnb_ref, bao_ref, cbw_ref,
             fg_ref, wa_ref, wb_ref, wo_ref, wat_ref, wbt_ref, wot_ref,
             dproj_ref, ds1_ref, lhs_ref, rhs_ref, small_ref,
             ua0_buf, cb_buf, dua1_buf, dc3_buf, stage, ua1_buf, c3_buf,
             dpa_buf, dpb_buf, dcaw8, dcbw8, shift_buf):
        i = pl.program_id(0)
        this, before = i % 2, 1 - i % 2

        @pl.when(i == 0)
        def _init():
            for buf in (ua0_buf, cb_buf, dua1_buf, dc3_buf, dcaw8, dcbw8):
                buf[...] = jnp.zeros(buf.shape, buf.dtype)
            small_ref[...] = jnp.zeros(small_ref.shape, F32)

        @pl.when(i >= 1)
        def _emit_stage():
            dproj_ref[:, C_AZ:C_BC] = stage[:, S_AZ:S_BZ]
            dproj_ref[:, C_BZ:D_IN] = stage[:, S_BZ:S_GB + D_MODEL]

        @pl.when(i < n_tiles)
        def _front():
            def conv_chunk(cc, carry):
                c0 = pl.multiple_of(cc * LANES, LANES)
                lanes = pl.ds(c0, LANES)

                def col(base):
                    return pl.ds(pl.multiple_of(base + cc * LANES, LANES), LANES)

                ua0 = proj_ref[:, col(C_AVAL)] * _sigmoid(proj_ref[:, col(C_AGLU)])
                ua0_buf[this, 0:HALO, lanes] = ua0_buf[before, T:T + HALO, lanes]
                ua0_buf[this, HALO:HALO + T, lanes] = ua0
                acc = jnp.broadcast_to(cab_ref[:, lanes], (T, LANES))
                lead = HALO - (CONV_A - 1)
                for r in range(SUBLANES):
                    taps = [k for k in range(CONV_A) if (k + lead) % SUBLANES == r]
                    rows = T + SUBLANES * max((k + lead) // SUBLANES for k in taps)
                    if r:
                        shift_buf[r, 0:rows, :] = ua0_buf[this, pl.ds(r, rows), lanes]
                    for k in taps:
                        q = (k + lead) // SUBLANES
                        if r:
                            win = shift_buf[r, SUBLANES * q:SUBLANES * q + T, :]
                        else:
                            win = ua0_buf[this, pl.ds(SUBLANES * q, T), lanes]
                        acc = acc + caw_ref[k:k + 1, lanes] * win
                ua1_buf[:, lanes] = acc
                cb = proj_ref[:, col(C_BC)] * proj_ref[:, col(C_BX)]
                cb_buf[this, 0:SUBLANES, lanes] = cb_buf[before, T:T + SUBLANES, lanes]
                cb_buf[this, SUBLANES:SUBLANES + T, lanes] = cb
                lead_b = SUBLANES - (CONV_B - 1)
                acc3 = cbw_ref[0:1, lanes] * cb_buf[this, pl.ds(lead_b, T), lanes]
                for k in range(1, CONV_B):
                    acc3 = acc3 + cbw_ref[k:k + 1, lanes] * cb_buf[this, pl.ds(lead_b + k, T), lanes]
                c3_buf[:, lanes] = acc3
                return carry

            lax.fori_loop(0, N_CHUNK, conv_chunk, 0)

            ua1 = ua1_buf[...]
            xc = ua1 - _rowmean(ua1)
            rstd = lax.rsqrt(_rowmean(xc * xc) + EPS)
            xhat = xc * rstd
            ua2 = xhat * lng_ref[...] + lnb_ref[...]
            sg2 = _sigmoid(ua2)
            ua3 = ua2 * sg2
            a_z = proj_ref[:, C_AZ:C_AZ + D_MODEL]
            sz = _sigmoid(a_z)
            silu_az = a_z * sz
            lhs_ref[0] = (ua3 * silu_az).astype(BF16)
            b_z = proj_ref[:, C_BZ:C_BZ + D_MODEL]
            sbz = _sigmoid(b_z)
            silu_bz = b_z * sbz
            b_b = proj_ref[:, C_BB:C_BB + D_MODEL]
            c3 = c3_buf[...]
            ub = b_b * c3
            lhs_ref[1] = (ub * silu_bz).astype(BF16)

            ya = _dot(lhs_ref[0], wa_ref[...]) + bao_ref[...]
            yb = _dot(lhs_ref[1], wb_ref[...])
            sga = _sigmoid(proj_ref[:, C_GA:C_GA + D_MODEL])
            sgb = _sigmoid(proj_ref[:, C_GB:C_GB + D_MODEL])
            m_b = (sga * ya + sgb * yb).astype(BF16)
            lhs_ref[2] = m_b
            s0 = jnp.where(i == 0, meta_ref[...], x_ref[...])
            s1 = s0 + _dot(m_b, wo_ref[...])
            r1 = lax.rsqrt(_rowmean(s1 * s1) + EPS)
            y = (s1 * r1) * fg_ref[...]
            is_token = (i >= 1).astype(F32)
            err = (y - tgt_ref[...]) * is_token
            small_ref[ROW_LOSS:ROW_LOSS + 1, :] += (0.5 * inv_d) * _colsum(err * err)
            dy = err * inv_d
            small_ref[ROW_FINAL_G:ROW_FINAL_G + 1, :] += _colsum(dy * (s1 * r1))
            gy = dy * fg_ref[...]
            ds1 = r1 * gy - s1 * ((r1 * r1 * r1) * _rowmean(gy * s1))
            ds1_ref[...] = ds1
            ds1_b = ds1.astype(BF16)
            rhs_ref[2] = ds1_b
            dm = _dot(ds1_b, wot_ref[...])
            dya = dm * sga
            dyb = dm * sgb
            stage[:, S_GA:S_GA + D_MODEL] = (dya * ya * (1.0 - sga)).astype(BF16)
            stage[:, S_GB:S_GB + D_MODEL] = (dyb * yb * (1.0 - sgb)).astype(BF16)
            small_ref[ROW_B_A_OUT:ROW_B_A_OUT + 1, :] += _colsum(dya)
            dya_b = dya.astype(BF16)
            dyb_b = dyb.astype(BF16)
            rhs_ref[0] = dya_b
            rhs_ref[1] = dyb_b
            dpa_buf[...] = _dot(dya_b, wat_ref[...])
            dpb_buf[...] = _dot(dyb_b, wbt_ref[...])

            dpa = dpa_buf[...]
            stage[:, S_AZ:S_AZ + D_MODEL] = (dpa * ua3 * (sz + silu_az * (1.0 - sz))).astype(BF16)
            dua2 = dpa * silu_az * (sg2 + ua3 * (1.0 - sg2))
            small_ref[ROW_LN_G:ROW_LN_G + 1, :] += _colsum(dua2 * xhat)
            small_ref[ROW_LN_B:ROW_LN_B + 1, :] += _colsum(dua2)
            dxh = dua2 * lng_ref[...]
            dua1 = rstd * (dxh - _rowmean(dxh) - xhat * _rowmean(dxh * xhat))
            small_ref[ROW_CONV_A_B:ROW_CONV_A_B + 1, :] += _colsum(dua1)
            dua1_buf[this, 0:T, :] = dua1
            dua1_buf[before, T:T + HALO, :] = dua1[0:HALO]
            dpb = dpb_buf[...]
            stage[:, S_BZ:S_BZ + D_MODEL] = (dpb * ub * (sbz + silu_bz * (1.0 - sbz))).astype(BF16)
            dub = dpb * silu_bz
            stage[:, S_BB:S_BB + D_MODEL] = (dub * c3).astype(BF16)
            dc3 = dub * b_b
            dc3_buf[this, 0:T, :] = dc3
            dc3_buf[before, T:T + SUBLANES, :] = dc3[0:SUBLANES]

        @pl.when(i == n_tiles)
        def _no_later_tile():
            dua1_buf[before, T:T + HALO, :] = jnp.zeros((HALO, D_MODEL), F32)
            dc3_buf[before, T:T + SUBLANES, :] = jnp.zeros((SUBLANES, D_MODEL), F32)

        @pl.when(i >= 1)
        def _lagged():
            def convt_chunk(cc, carry):
                c0 = pl.multiple_of(cc * LANES, LANES)
                lanes = pl.ds(c0, LANES)

                def col(base):
                    return pl.ds(pl.multiple_of(base + cc * LANES, LANES), LANES)

                ua0 = ua0_buf[before, HALO:HALO + T, lanes]
                acc = jnp.zeros((T, LANES), F32)
                for r in range(SUBLANES):
                    shifts = [j for j in range(CONV_A) if j % SUBLANES == r]
                    rows = T + shifts[-1] - r
                    if r:
                        shift_buf[r, 0:rows, :] = dua1_buf[before, pl.ds(r, rows), lanes]
                    for j in shifts:
                        k = CONV_A - 1 - j
                        if r:
                            later = shift_buf[r, j - r:j - r + T, :]
                        else:
                            later = dua1_buf[before, pl.ds(j, T), lanes]
                        acc = acc + caw_ref[k:k + 1, lanes] * later
                        dcaw8[SUBLANES * k:SUBLANES * (k + 1), lanes] += _fold8(ua0 * later)
                a_val = aprev[:, col(0)]
                sg = _sigmoid(aprev[:, col(D_MODEL)])
                dproj_ref[:, col(C_AVAL)] = (acc * sg).astype(BF16)
                dproj_ref[:, col(C_AGLU)] = (acc * a_val * (sg * (1.0 - sg))).astype(BF16)

                cb = cb_buf[before, SUBLANES:SUBLANES + T, lanes]
                acc3 = jnp.zeros((T, LANES), F32)
                for j in range(CONV_B):
                    k = CONV_B - 1 - j
                    later = dc3_buf[before, pl.ds(j, T), lanes]
                    acc3 = acc3 + cbw_ref[k:k + 1, lanes] * later
                    dcbw8[SUBLANES * k:SUBLANES * (k + 1), lanes] += _fold8(cb * later)
                dproj_ref[:, col(C_BC)] = (acc3 * cprev[:, col(D_MODEL)]).astype(BF16)
                dproj_ref[:, col(C_BX)] = (acc3 * cprev[:, col(0)]).astype(BF16)
                return carry

            lax.fori_loop(0, N_CHUNK, convt_chunk, 0)

        @pl.when(i == n_tiles)
        def _finish():
            for k in range(CONV_A):
                small_ref[ROW_CONV_A_W + k:ROW_CONV_A_W + k + 1, :] = _colsum(dcaw8[SUBLANES * k:SUBLANES * (k + 1), :])
            for k in range(CONV_B):
                small_ref[ROW_CONV_B_W + k:ROW_CONV_B_W + k + 1, :] = _colsum(dcbw8[SUBLANES * k:SUBLANES * (k + 1), :])

    pair = 2 * D_MODEL
    return pl.pallas_call(
        body, name="fused_pass", grid=(n_tiles + 1,),
        out_shape=[
            jax.ShapeDtypeStruct((tp, D_IN), BF16),
            jax.ShapeDtypeStruct((tp, D_MODEL), F32),
            jax.ShapeDtypeStruct((3, tp, D_MODEL), BF16),
            jax.ShapeDtypeStruct((3, tp, D_MODEL), BF16),
            jax.ShapeDtypeStruct((SMALL_A_ROWS, D_MODEL), F32),
        ],
        in_specs=[
            pl.BlockSpec((T, D_IN), lambda i: (cur(i), 0)),
            pl.BlockSpec((T, pair), lambda i: (prev(i), C_AVAL // pair)),
            pl.BlockSpec((T, pair), lambda i: (prev(i), C_BC // pair)),
            pl.BlockSpec((T, D_MODEL), lambda i: (xblk(i), 0)),
            pl.BlockSpec((T, D_MODEL), lambda i: (xblk(i), 0)),
            _VMEM, _VMEM, _VMEM, _VMEM, _VMEM, _VMEM, _VMEM, _VMEM,
            *[_resident((D_MODEL, D_MODEL)) for _ in range(6)],
        ],
        out_specs=[
            pl.BlockSpec((T, D_IN), lambda i: (prev(i), 0)),
            pl.BlockSpec((T, D_MODEL), lambda i: (cur(i), 0)),
            pl.BlockSpec((3, T, D_MODEL), lambda i: (0, cur(i), 0)),
            pl.BlockSpec((3, T, D_MODEL), lambda i: (0, cur(i), 0)),
            _VMEM,
        ],
        scratch_shapes=[
            pltpu.VMEM((2, HALO + T, D_MODEL), F32),
            pltpu.VMEM((2, SUBLANES + T, D_MODEL), F32),
            pltpu.VMEM((2, T + HALO, D_MODEL), F32),
            pltpu.VMEM((2, T + SUBLANES, D_MODEL), F32),
            pltpu.VMEM((T, 5 * D_MODEL), BF16),
            pltpu.VMEM((T, D_MODEL), F32),
            pltpu.VMEM((T, D_MODEL), F32),
            pltpu.VMEM((T, D_MODEL), F32),
            pltpu.VMEM((T, D_MODEL), F32),
            pltpu.VMEM((32 * SUBLANES, D_MODEL), F32),
            pltpu.VMEM((SUBLANES * SUBLANES, D_MODEL), F32),
            pltpu.VMEM((SUBLANES, T + HALO, LANES), F32),
        ],
        compiler_params=pltpu.CompilerParams(dimension_semantics=("arbitrary",), vmem_limit_bytes=VMEM_LIMIT),
    )(proj, proj, proj, x2d, tgt2d, meta_tile, conv_a_w, conv_a_b, ln_a_g, ln_a_b, b_a_out, conv_b_w, final_g,
      w_a, w_b, w_o, w_a_t, w_b_t, w_o_t)


def _input_bwd(dproj, ds1, x2d, meta_tile, norm_g, w_in_all, row_tile):
    seq = x2d.shape[0]
    n_steps = seq // row_tile
    meta_block = seq // TILE

    def backward(dp_ref, ds1_ref, s0_ref, g_ref, w_ref, out_ref, vec_ref):
        dh = _dot_nt(dp_ref[:, 0:COLS], w_ref[0])
        for j in range(1, N_DEV):
            dh = dh + _dot_nt(dp_ref[:, j * COLS:(j + 1) * COLS], w_ref[j])
        s0v = s0_ref[...]
        r = lax.rsqrt(_rowmean(s0v * s0v) + EPS)
        gh = dh * g_ref[...]
        out_ref[...] = ds1_ref[...] + r * gh - s0v * ((r * r * r) * _rowmean(gh * s0v))
        vec_ref[ROW_NORM_G:ROW_NORM_G + 1, :] += _colsum(dh * (s0v * r))

    def body(dp_ref, ds1_ref, x_ref, dpm_ref, ds1m_ref, meta_ref, g_ref, w_ref, gx_ref, small_ref, gmeta_buf):
        t = pl.program_id(0)

        @pl.when(t == 0)
        def _():
            small_ref[...] = jnp.zeros(small_ref.shape, F32)

        backward(dp_ref, ds1_ref, x_ref, g_ref, w_ref, gx_ref, small_ref)

        @pl.when(t == n_steps - 1)
        def _():
            backward(dpm_ref, ds1m_ref, meta_ref, g_ref, w_ref, gmeta_buf, small_ref)
            small_ref[ROW_META:ROW_META + N_META, :] = gmeta_buf[TILE - N_META:TILE, :]

    return pl.pallas_call(
        body, name="input_bwd", grid=(n_steps,),
        out_shape=[jax.ShapeDtypeStruct(x2d.shape, F32), jax.ShapeDtypeStruct((SMALL_B_ROWS, D_MODEL), F32)],
        in_specs=[pl.BlockSpec((row_tile, D_IN), lambda t: (t, 0)),
                  pl.BlockSpec((row_tile, D_MODEL), lambda t: (t, 0)),
                  pl.BlockSpec((row_tile, D_MODEL), lambda t: (t, 0)),
                  pl.BlockSpec((TILE, D_IN), lambda t: (meta_block, 0)),
                  pl.BlockSpec((TILE, D_MODEL), lambda t: (meta_block, 0)),
                  _VMEM, _VMEM, _resident((N_DEV, D_MODEL, COLS))],
        out_specs=[pl.BlockSpec((row_tile, D_MODEL), lambda t: (t, 0)), _VMEM],
        scratch_shapes=[pltpu.VMEM((TILE, D_MODEL), F32)],
        compiler_params=pltpu.CompilerParams(dimension_semantics=("arbitrary",), vmem_limit_bytes=VMEM_LIMIT),
    )(dproj, ds1, x2d, dproj, ds1, meta_tile, norm_g, w_in_all)


def _grad_w_in_half(pos, h_t, dproj, k_tile, other_side, rides, name):
    tp = h_t.shape[1]
    n_k = tp // k_tile

    def column_block(q, k, pos_ref):
        return k, 2 * q + (1 - pos_ref[2] if other_side else pos_ref[2])

    def body(pos_ref, h_ref, dp_ref, o_ref):
        @pl.when(pl.program_id(1) == 0)
        def _():
            o_ref[...] = jnp.zeros(o_ref.shape, F32)

        o_ref[...] += _dot(h_ref[...], dp_ref[...])

    ride = [a for _, arrays in rides for a in arrays]
    n_arr = len(ride)
    ride_shapes, ride_sems = _ride_shapes(rides)
    body = _riding(body, 3, 1, rides,
                   lambda: (pl.program_id(0) == 0) & (pl.program_id(1) == 0),
                   lambda: (pl.program_id(0) == 3) & (pl.program_id(1) == n_k - 1))
    return pl.pallas_call(
        body, name=name,
        out_shape=[jax.ShapeDtypeStruct((4, D_MODEL, COLS), F32)] + ride_shapes,
        grid_spec=pltpu.PrefetchScalarGridSpec(
            num_scalar_prefetch=1, grid=(4, n_k),
            in_specs=[pl.BlockSpec((D_MODEL, k_tile), lambda q, k, pos_ref: (0, k)),
                      pl.BlockSpec((k_tile, COLS), column_block)] + [_ANY] * n_arr,
            out_specs=[pl.BlockSpec((None, D_MODEL, COLS), lambda q, k, pos_ref: (q, 0, 0))] + [_ANY] * n_arr,
            scratch_shapes=ride_sems),
        compiler_params=pltpu.CompilerParams(dimension_semantics=("arbitrary", "arbitrary"),
                                             vmem_limit_bytes=VMEM_LIMIT),
    )(pos, h_t, dproj, *ride)


def _grad_w_out(lhs, rhs, k_tile):
    tp = lhs.shape[1]

    def body(a_ref, b_ref, o_ref):
        @pl.when(pl.program_id(1) == 0)
        def _():
            o_ref[...] = jnp.zeros(o_ref.shape, F32)

        o_ref[...] += _dot_tn(a_ref[...], b_ref[...]).reshape(N_DEV, ROWS_OUT, D_MODEL)

    return pl.pallas_call(
        body, name="grad_w_out", grid=(3, tp // k_tile),
        out_shape=jax.ShapeDtypeStruct((N_DEV, 3, ROWS_OUT, D_MODEL), F32),
        in_specs=[pl.BlockSpec((None, k_tile, D_MODEL), lambda w, k: (w, k, 0)),
                  pl.BlockSpec((None, k_tile, D_MODEL), lambda w, k: (w, k, 0))],
        out_specs=pl.BlockSpec((N_DEV, None, ROWS_OUT, D_MODEL), lambda w, k: (0, w, 0, 0)),
        compiler_params=pltpu.CompilerParams(dimension_semantics=("arbitrary", "arbitrary"),
                                             vmem_limit_bytes=VMEM_LIMIT),
    )(lhs, rhs)


def _adamw_math(w, g, m, v):
    m = ADAM_B1 * m + (1.0 - ADAM_B1) * g
    v = ADAM_B2 * v + (1.0 - ADAM_B2) * (g * g)
    m_hat = m / (1.0 - ADAM_B1 ** ADAM_STEP)
    v_hat = v / (1.0 - ADAM_B2 ** ADAM_STEP)
    delta = -ADAM_LR * (m_hat / (jnp.sqrt(v_hat) + ADAM_EPS) + ADAM_WD * w)
    return delta, m, v


def _adamw_sharded(pos, mine, theirs, landed, weights, row_tile, name, after=None):
    order = [] if after is None else [after]
    rows, n = weights[0][0].shape
    n_slots = mine.shape[0]
    per_shard = rows // row_tile
    assert per_shard == 1 or len(weights) == 1

    def mine_map(j, t, pos_ref):
        chip = 2 * pos_ref[0] + pos_ref[1]
        return (2 * chip + pos_ref[2] if n_slots == N_DEV else chip), j * per_shard + t, 0

    def theirs_map(j, t, pos_ref):
        return 2 * pos_ref[0] + pos_ref[1], j * per_shard + t, 0

    def body(pos_ref, mine_ref, theirs_ref, land_ref, *refs):
        ins, outs = refs[:3 * len(weights)], refs[3 * len(weights) + len(order):]
        g = mine_ref[...] + theirs_ref[...]
        for k in range(3):
            g = g + land_ref[k].astype(F32)
        for j in range(len(weights)):
            @pl.when(pl.program_id(0) == j)
            def _(j=j):
                w_ref, m_ref, v_ref = ins[3 * j:3 * j + 3]
                delta, m_new, v_new = _adamw_math(w_ref[...], g, m_ref[...], v_ref[...])
                for ref, val in zip(outs[4 * j:4 * j + 4], (g, delta, m_new, v_new)):
                    ref[...] = val

    tile = pl.BlockSpec((row_tile, n), lambda j, t, pos_ref: (t, 0))
    res = pl.pallas_call(
        body, name=name,
        out_shape=[jax.ShapeDtypeStruct((rows, n), F32)] * (4 * len(weights)),
        grid_spec=pltpu.PrefetchScalarGridSpec(
            num_scalar_prefetch=1, grid=(len(weights), per_shard),
            in_specs=[pl.BlockSpec((None, row_tile, n), mine_map), pl.BlockSpec((None, row_tile, n), theirs_map),
                      pl.BlockSpec((3, row_tile, n), lambda j, t, pos_ref: (0, j * per_shard + t, 0))]
            + [tile] * (3 * len(weights)) + [_ANY] * len(order),
            out_specs=[tile] * (4 * len(weights))),
        compiler_params=pltpu.CompilerParams(dimension_semantics=("arbitrary", "arbitrary")),
    )(pos, mine, theirs, landed, *[a for wmv in weights for a in wmv], *order)
    return [res[4 * j:4 * j + 4] for j in range(len(weights))]


def _adamw_small(gathered, gathered_cols, params):
    n_par, n_src = len(params), len(gathered)

    def body(*refs):
        g_refs, gc_refs = refs[:n_src], refs[n_src:2 * n_src]
        ins = refs[2 * n_src:2 * n_src + 3 * n_par]
        outs = refs[2 * n_src + 3 * n_par:]
        loss_ref = outs[4 * n_par]

        def reduced(ref, row, n_rows):
            g = ref[0, row:row + n_rows, :]
            for d in range(1, N_DEV):
                g = g + ref[d, row:row + n_rows, :]
            return g

        for p, (src, row, n_rows, sharded, _, _, _) in enumerate(params):
            g = reduced((gc_refs if sharded else g_refs)[src], row, n_rows)
            w_ref, m_ref, v_ref = ins[3 * p:3 * p + 3]
            delta, m_new, v_new = _adamw_math(w_ref[...], g, m_ref[...], v_ref[...])
            outs[4 * p][...] = g
            outs[4 * p + 1][...] = delta
            outs[4 * p + 2][...] = m_new
            outs[4 * p + 3][...] = v_new
        loss = jnp.sum(reduced(g_refs[0], ROW_LOSS, 1), axis=1, keepdims=True)
        loss_ref[...] = jnp.broadcast_to(loss, loss_ref.shape)

    out_shape = []
    for (_, _, _, _, w, _, _) in params:
        out_shape += [jax.ShapeDtypeStruct(w.shape, F32)] * 4
    out_shape.append(jax.ShapeDtypeStruct((1, LANES), F32))
    flat = [a for (_, _, _, _, w, m, v) in params for a in (w, m, v)]
    return pl.pallas_call(
        body, name="adamw_small", out_shape=out_shape,
        in_specs=[_VMEM] * (2 * n_src + len(flat)), out_specs=[_VMEM] * len(out_shape),
    )(*gathered, *gathered_cols, *flat)


def _pad_rows(a, rows):
    return jnp.concatenate([a, jnp.zeros((rows - a.shape[0], a.shape[1]), a.dtype)], axis=0)


def kernel(x, meta_tokens, norm_g, w_in, conv_a_w, conv_a_b, ln_a_g, ln_a_b, w_a_out, b_a_out, conv_b_w, w_b_out, w_out, final_g, loss_target, m_meta_tokens, m_norm_g, m_w_in, m_conv_a_w, m_conv_a_b, m_ln_a_g, m_ln_a_b, m_w_a_out, m_b_a_out, m_conv_b_w, m_w_b_out, m_w_out, m_final_g, v_meta_tokens, v_norm_g, v_w_in, v_conv_a_w, v_conv_a_b, v_ln_a_g, v_ln_a_b, v_w_a_out, v_b_a_out, v_conv_b_w, v_w_b_out, v_w_out, v_final_g):
    seq = x.shape[1]
    assert x.shape == (1, seq, D_MODEL) and seq % TILE == 0 and w_in.shape == (1, D_MODEL, COLS)
    n_tiles = seq // TILE + 1
    tp = n_tiles * TILE
    pos = jnp.stack([lax.axis_index("x"), lax.axis_index("y"), lax.axis_index("c")]).astype(jnp.int32)
    me = 4 * pos[0] + 2 * pos[1] + pos[2]
    x2d = x[0]
    tgt2d = loss_target[0]

    small = jnp.concatenate([meta_tokens, _pad_rows(conv_a_w[0], 32), _pad_rows(conv_b_w[0], SUBLANES)], axis=0)
    final_g2 = final_g.reshape(1, D_MODEL)

    w_out_shards = [w[0].astype(BF16) for w in (w_a_out, w_b_out, w_out)]
    h_t, proj, meta_tile, small_params, w_in_all, *w_out_all = _gather_norm_proj(
        pos, x2d, small[None], norm_g, w_in[0].astype(BF16), w_out_shards, 3)
    small_params = small_params.transpose(1, 0, 2).reshape(small.shape[0], D_MODEL)
    conv_a_full, conv_b_full = small_params[N_META:N_META + 32], small_params[N_META + 32:]
    w_out_all = [w.reshape(D_MODEL, D_MODEL) for w in w_out_all]
    w_out_all_t = [w.T for w in w_out_all]
    dproj, ds1, lhs, rhs, small_a = _fused_pass(
        proj, x2d, tgt2d, meta_tile, conv_a_full, conv_a_b, ln_a_g, ln_a_b, b_a_out, conv_b_full, final_g2,
        w_out_all[0], w_out_all[1], w_out_all[2], w_out_all_t[0], w_out_all_t[1], w_out_all_t[2], n_tiles)
    k_tile = tp // 3
    gw_out = _grad_w_out(lhs, rhs, k_tile).reshape(N_DEV, 3 * ROWS_OUT, D_MODEL)
    gw_far, their_out, small_a_all = _grad_w_in_half(
        pos, h_t, dproj, k_tile, True, [("sibling", (gw_out,)), ("all", (small_a[None],))], "grad_w_in_far")
    parts_out = _chip_partial(pos, gw_out, their_out, (1, 2, 3), BF16, ROWS_OUT, "rs_parts_w_out")
    gw_near, their_in, land_out = _grad_w_in_half(
        pos, h_t, dproj, k_tile, False, [("sibling_half", (gw_far,)), ("chips", (parts_out,))], "grad_w_in_near")
    parts_in = _chip_partial(pos, gw_near, their_in, (1, 2, 3), BF16, 256, "rs_parts_w_in")
    send_sems, recv_sems, parts_in, land_in, token = _chips_start(parts_in, "rs_chips_start")
    grad_x, small_b = _input_bwd(dproj, ds1, x2d, meta_tile, norm_g + token[0, 0], w_in_all, min(256, seq))

    res_out = _adamw_sharded(
        pos, gw_out, their_out, land_out,
        [(w_a_out[0], m_w_a_out[0], v_w_a_out[0]), (w_b_out[0], m_w_b_out[0], v_w_b_out[0]),
         (w_out[0], m_w_out[0], v_w_out[0])], ROWS_OUT, "adamw_w_out", after=small_b)
    (small_b_all,) = _exchange("all", [small_b[None]], res_out[2][0], "gather_small_grads")
    small_grads = [small_a_all, small_b_all]
    small_cols = [lax.dynamic_slice_in_dim(g, me * LANES, LANES, axis=2) for g in small_grads]
    land_in = _chips_wait(send_sems, recv_sems, parts_in, land_in, small_b_all, "rs_chips_wait")
    (res_in,) = _adamw_sharded(pos, gw_near, their_in, land_in, [(w_in[0], m_w_in[0], v_w_in[0])], 128, "adamw_w_in")
    params = [
        (1, ROW_META, N_META, True, meta_tokens, m_meta_tokens, v_meta_tokens),
        (1, ROW_NORM_G, 1, False, norm_g, m_norm_g, v_norm_g),
        (0, ROW_CONV_A_W, CONV_A, True, conv_a_w[0], m_conv_a_w[0], v_conv_a_w[0]),
        (0, ROW_CONV_A_B, 1, False, conv_a_b, m_conv_a_b, v_conv_a_b),
        (0, ROW_LN_G, 1, False, ln_a_g, m_ln_a_g, v_ln_a_g),
        (0, ROW_LN_B, 1, False, ln_a_b, m_ln_a_b, v_ln_a_b),
        (0, ROW_B_A_OUT, 1, False, b_a_out, m_b_a_out, v_b_a_out),
        (0, ROW_CONV_B_W, CONV_B, True, conv_b_w[0], m_conv_b_w[0], v_conv_b_w[0]),
        (0, ROW_FINAL_G, 1, False, final_g2, m_final_g.reshape(1, D_MODEL), v_final_g.reshape(1, D_MODEL)),
    ]
    res_small = _adamw_small(small_grads, small_cols, params)
    loss = res_small[-1][0, 0]

    def small_res(p, kind, shape):
        return res_small[4 * p + kind].reshape(shape)

    per_weight = []
    for kind in range(4):
        per_weight.append([
            small_res(0, kind, meta_tokens.shape),
            small_res(1, kind, norm_g.shape),
            res_in[kind].reshape(w_in.shape),
            small_res(2, kind, conv_a_w.shape),
            small_res(3, kind, conv_a_b.shape),
            small_res(4, kind, ln_a_g.shape),
            small_res(5, kind, ln_a_b.shape),
            res_out[0][kind].reshape(w_a_out.shape),
            small_res(6, kind, b_a_out.shape),
            small_res(7, kind, conv_b_w.shape),
            res_out[1][kind].reshape(w_b_out.shape),
            res_out[2][kind].reshape(w_out.shape),
            small_res(8, kind, final_g.shape),
        ])
    return (loss, grad_x.reshape(x.shape), *per_weight[0], *per_weight[1], *per_weight[2], *per_weight[3])
```

```python
import functools

import jax
import jax.numpy as jnp
from jax import lax
from jax.experimental import pallas as pl
from jax.experimental.pallas import tpu as pltpu

D_MODEL = 1024
N_META = 16
N_DEV = 8
D_IN = 9 * D_MODEL
COLS = D_IN // N_DEV
ROWS_OUT = D_MODEL // N_DEV
CONV_A = 31
CONV_B = 3
EPS = 1e-6

ADAM_LR = 0.001
ADAM_B1 = 0.9
ADAM_B2 = 0.999
ADAM_EPS = 1e-08
ADAM_WD = 0.01
ADAM_STEP = 10

TILE = 128
LANES = 128
N_CHUNK = D_MODEL // LANES
HALO = 32
SUBLANES = 8
VMEM_LIMIT = 56 * 1024 * 1024

ROW_FINAL_G, ROW_B_A_OUT, ROW_LN_G, ROW_LN_B, ROW_CONV_A_B, ROW_LOSS = 0, 1, 2, 3, 4, 5
ROW_CONV_A_W, ROW_CONV_B_W, SMALL_A_ROWS = 8, 40, 48
ROW_NORM_G, ROW_META, SMALL_B_ROWS = 0, 8, 24

MESH = pl.DeviceIdType.MESH
_ANY = pl.BlockSpec(memory_space=pl.ANY)
_VMEM = pl.BlockSpec(memory_space=pltpu.VMEM)
BF16 = jnp.bfloat16
F32 = jnp.float32


def _resident(shape):
    return pl.BlockSpec(shape, lambda *_: (0,) * len(shape), pipeline_mode=pl.Buffered(1))


def _sigmoid(v):
    return jax.nn.sigmoid(v)


def _dot(a, b):
    return jnp.dot(a, b, preferred_element_type=F32)


def _dot_nt(a, b):
    return lax.dot_general(a, b, (((1,), (1,)), ((), ())), preferred_element_type=F32)


def _dot_tn(a, b):
    return lax.dot_general(a, b, (((0,), (0,)), ((), ())), preferred_element_type=F32)


def _colsum(v):
    return jnp.sum(v, axis=0, keepdims=True)


def _rowmean(v):
    parts = [v[:, LANES * c:LANES * (c + 1)] for c in range(v.shape[1] // LANES)]
    return jnp.sum(functools.reduce(jnp.add, parts), axis=-1, keepdims=True) * (1.0 / v.shape[1])


def _fold8(v):
    parts = [v[SUBLANES * g:SUBLANES * (g + 1)] for g in range(v.shape[0] // SUBLANES)]
    return functools.reduce(jnp.add, parts)


def _sibling_copies(srcs, dsts, send_sems, recv_sems):
    x, y, c = lax.axis_index("x"), lax.axis_index("y"), lax.axis_index("c")
    return [pltpu.make_async_remote_copy(
        src_ref=src.at[2 * q + (1 - c)], dst_ref=dst.at[q],
        send_sem=send_sems.at[4 * a + q], recv_sem=recv_sems.at[4 * a + q],
        device_id=(x, y, 1 - c), device_id_type=MESH)
        for a, (src, dst) in enumerate(zip(srcs, dsts)) for q in range(4)]


def _chip_copies(srcs, dsts, send_sems, recv_sems):
    x, y, c = lax.axis_index("x"), lax.axis_index("y"), lax.axis_index("c")
    targets = [(x, 1 - y, c), (1 - x, y, c), (1 - x, 1 - y, c)]
    return [pltpu.make_async_remote_copy(
        src_ref=src.at[k], dst_ref=dst.at[k],
        send_sem=send_sems.at[3 * a + k], recv_sem=recv_sems.at[3 * a + k],
        device_id=targets[k], device_id_type=MESH)
        for a, (src, dst) in enumerate(zip(srcs, dsts)) for k in range(3)]


def _sibling_half_copies(srcs, dsts, send_sems, recv_sems):
    x, y, c = lax.axis_index("x"), lax.axis_index("y"), lax.axis_index("c")
    return [pltpu.make_async_remote_copy(
        src_ref=src.at[q], dst_ref=dst.at[q],
        send_sem=send_sems.at[4 * a + q], recv_sem=recv_sems.at[4 * a + q],
        device_id=(x, y, 1 - c), device_id_type=MESH)
        for a, (src, dst) in enumerate(zip(srcs, dsts)) for q in range(4)]


def _all_copies(srcs, dsts, send_sems, recv_sems):
    x, y, c = lax.axis_index("x"), lax.axis_index("y"), lax.axis_index("c")
    mine = 4 * x + 2 * y + c
    copies = []
    for a, (src, dst) in enumerate(zip(srcs, dsts)):
        copies.append(pltpu.make_async_copy(src.at[0], dst.at[mine], send_sems.at[N_DEV * a]))
        for k in range(1, N_DEV):
            copies.append(pltpu.make_async_remote_copy(
                src_ref=src.at[0], dst_ref=dst.at[mine],
                send_sem=send_sems.at[N_DEV * a + k], recv_sem=recv_sems.at[N_DEV * a + k],
                device_id=(x ^ (k >> 2), y ^ ((k >> 1) & 1), c ^ (k & 1)), device_id_type=MESH))
    return copies


_EXCHANGES = {"sibling": (4, _sibling_copies, 4), "sibling_half": (4, _sibling_half_copies, 4),
              "chips": (3, _chip_copies, 3), "all": (N_DEV, _all_copies, N_DEV)}


def _exchange_shapes(kind, arrays):
    per_array, _, slots = _EXCHANGES[kind]
    out_shape = [jax.ShapeDtypeStruct((slots,) + a.shape[1:], a.dtype) for a in arrays]
    sems = [pltpu.SemaphoreType.DMA((per_array * len(arrays),))] * 2
    return out_shape, sems


def _ride_shapes(rides):
    shapes, sems = [], []
    for kind, arrays in rides:
        ride_shapes, ride_sems = _exchange_shapes(kind, arrays)
        shapes += ride_shapes
        sems += ride_sems
    return shapes, sems


def _riding(body, n_in, n_out, rides, is_first, is_last):
    counts = [len(arrays) for _, arrays in rides]
    n_arr = sum(counts)

    def wrapped(*refs):
        ins, srcs = refs[:n_in], refs[n_in:n_in + n_arr]
        outs = refs[n_in + n_arr:n_in + n_arr + n_out]
        dsts = refs[n_in + n_arr + n_out:n_in + 2 * n_arr + n_out]
        first_sem = len(refs) - 2 * len(rides)
        scratch, sems = refs[n_in + 2 * n_arr + n_out:first_sem], refs[first_sem:]

        def copies():
            made, at = [], 0
            for r, ((kind, _), n) in enumerate(zip(rides, counts)):
                made += _EXCHANGES[kind][1](srcs[at:at + n], dsts[at:at + n], sems[2 * r], sems[2 * r + 1])
                at += n
            return made

        @pl.when(is_first())
        def _():
            for cp in copies():
                cp.start()

        body(*ins, *outs, *scratch)

        @pl.when(is_last())
        def _():
            for cp in copies():
                cp.wait()

    return wrapped


_HBM = pl.BlockSpec(memory_space=pltpu.HBM)
_SEM = pl.BlockSpec(memory_space=pltpu.SEMAPHORE)
_FLOWS = pltpu.SideEffectType.DATAFLOW_SIDE_EFFECTING


def _start_exchanges(rides, name):
    arrays = [a for _, group in rides for a in group]
    shapes, sems = _ride_shapes(rides)
    n_arr, n_sem = len(arrays), len(sems)

    def body(*refs):
        srcs, lands = refs[:n_arr], refs[n_arr:2 * n_arr]
        sem_refs, token = refs[2 * n_arr:2 * n_arr + n_sem], refs[-1]
        at = 0
        for r, (kind, group) in enumerate(rides):
            n = len(group)
            for cp in _EXCHANGES[kind][1](srcs[at:at + n], lands[at:at + n], sem_refs[2 * r], sem_refs[2 * r + 1]):
                cp.start()
            at += n
        token[...] = jnp.zeros(token.shape, token.dtype)

    in_hbm = [pltpu.HBM(a.shape, a.dtype) for a in arrays]
    land_hbm = [pltpu.HBM(sh.shape, sh.dtype) for sh in shapes]
    res = pl.pallas_call(
        body, name=name,
        out_shape=(*sems, *in_hbm, *land_hbm, jax.ShapeDtypeStruct((SUBLANES, LANES), F32)),
        in_specs=[_HBM] * (2 * n_arr), out_specs=(*[_SEM] * n_sem, *[_HBM] * (2 * n_arr), _VMEM),
        input_output_aliases={i: n_sem + i for i in range(2 * n_arr)},
        compiler_params=pltpu.CompilerParams(has_side_effects=_FLOWS),
    )(*[pltpu.with_memory_space_constraint(a, pltpu.HBM) for a in arrays],
      *[pltpu.with_memory_space_constraint(lax.empty(sh.shape, sh.dtype), pltpu.HBM) for sh in shapes])
    return res[:n_sem], res[n_sem:n_sem + n_arr], res[n_sem + n_arr:n_sem + 2 * n_arr], res[-1]


def _wait_exchanges(kinds, sems, arrays, lands, after, name):
    n_arr, n_sem = len(arrays), len(sems)

    def body(*refs):
        srcs, dsts = refs[:n_arr], refs[n_arr:2 * n_arr]
        sem_refs = refs[2 * n_arr:2 * n_arr + n_sem]
        at = 0
        for r, (kind, n) in enumerate(kinds):
            for cp in _EXCHANGES[kind][1](srcs[at:at + n], dsts[at:at + n], sem_refs[2 * r], sem_refs[2 * r + 1]):
                cp.wait()
            at += n

    hbm = [pltpu.HBM(a.shape, a.dtype) for a in (*arrays, *lands)]
    return pl.pallas_call(
        body, name=name, out_shape=tuple(hbm),
        in_specs=[_HBM] * (2 * n_arr) + [_SEM] * n_sem + [_ANY], out_specs=tuple([_HBM] * (2 * n_arr)),
        input_output_aliases={i: i for i in range(2 * n_arr)},
        compiler_params=pltpu.CompilerParams(has_side_effects=_FLOWS),
    )(*arrays, *lands, *sems, after)[n_arr:]


def _exchange(kind, arrays, after, name):
    n_arr = len(arrays)
    out_shape, sems = _exchange_shapes(kind, arrays)

    def body(*refs):
        copies = _EXCHANGES[kind][1](refs[:n_arr], refs[n_arr + 1:2 * n_arr + 1], *refs[2 * n_arr + 1:])
        for cp in copies:
            cp.start()
        for cp in copies:
            cp.wait()

    return pl.pallas_call(body, name=name, out_shape=out_shape, in_specs=[_ANY] * (n_arr + 1),
                          out_specs=[_ANY] * n_arr, scratch_shapes=sems)(*arrays, after)


def _chip_partial(pos, mine, theirs, relations, out_dtype, row_tile, name):
    n_slots, m, n = mine.shape
    q0 = relations[0]

    def chip_of(qi, pos_ref):
        q = qi + q0
        return pos_ref[0] ^ (q >> 1), pos_ref[1] ^ (q & 1)

    def mine_map(qi, t, pos_ref):
        px, py = chip_of(qi, pos_ref)
        return (4 * px + 2 * py + pos_ref[2] if n_slots == N_DEV else 2 * px + py), t, 0

    def theirs_map(qi, t, pos_ref):
        px, py = chip_of(qi, pos_ref)
        return 2 * px + py, t, 0

    def body(pos_ref, a_ref, b_ref, o_ref):
        o_ref[...] = (a_ref[...] + b_ref[...]).astype(out_dtype)

    return pl.pallas_call(
        body, name=name,
        out_shape=jax.ShapeDtypeStruct((len(relations), m, n), out_dtype),
        grid_spec=pltpu.PrefetchScalarGridSpec(
            num_scalar_prefetch=1, grid=(len(relations), m // row_tile),
            in_specs=[pl.BlockSpec((None, row_tile, n), mine_map), pl.BlockSpec((None, row_tile, n), theirs_map)],
            out_specs=pl.BlockSpec((None, row_tile, n), lambda qi, t, pos_ref: (qi, t, 0))),
        compiler_params=pltpu.CompilerParams(dimension_semantics=("arbitrary", "arbitrary")),
    )(pos, mine, theirs)


PARTS = ((0, 512), (512, 640))


def _gather_norm_proj(pos, x2d, small_shard, norm_g, w_in_shard, w_out_shards, n_chunk):
    seq = x2d.shape[0]
    n_tiles = seq // TILE + 1
    tp = n_tiles * TILE
    n_parts = len(PARTS)
    widest = max(width for _, width in PARTS)
    units = [(s, u) for s in range(2) for u in range(n_parts)]
    for first in (2, 5):
        units += [(first + j, u) for u in range(n_parts) for j in range(2)] + [(first + 2, u) for u in range(n_parts)]
    n_units = len(units)
    n_steps = n_tiles + n_units
    chunk = tp // n_chunk

    def body(pos_ref, x_ref, g_ref, small_ref, win_ref, wa_ref, wb_ref, wo_ref,
             ht_ref, proj_ref, meta_ref, small_all, win_all, wa_all, wb_all, wo_all,
             h_all, wbuf, rbuf, small_buf, send_sems, recv_sems, local_sems, small_send, small_recv):
        g = pl.program_id(0)
        x, y, c = lax.axis_index("x"), lax.axis_index("y"), lax.axis_index("c")
        me, sibling = (x, y, c), (x, y, 1 - c)
        chips = [(1 - x, y), (x, 1 - y), (1 - x, 1 - y)]
        shards = (win_ref, wa_ref, wb_ref, wo_ref)
        gathered = (win_all, wa_all, wb_all, wo_all)
        n_arrays = len(shards)
        blocks = [me, sibling] + [(*chip, c) for chip in chips] + [(*chip, 1 - c) for chip in chips]

        def index(block):
            px, py, pc = block
            return 4 * px + 2 * py + pc

        def part(ref, a, u):
            return ref.at[:, pl.ds(PARTS[u][0], PARTS[u][1])] if a == 0 else ref

        def slot(a, block, u):
            return part(gathered[a].at[index(block)], a, u)

        def sem(a, k, u):
            return n_parts * k + u if a == 0 else 7 * n_parts + 7 * (a - 1) + k

        def copy(a, k, block, to, u=0, from_shard=False):
            return pltpu.make_async_remote_copy(
                src_ref=part(shards[a], a, u) if from_shard else slot(a, block, u), dst_ref=slot(a, block, u),
                send_sem=send_sems.at[sem(a, k, u)], recv_sem=recv_sems.at[sem(a, k, u)],
                device_id=to, device_id_type=MESH)

        def keep(a):
            return pltpu.make_async_copy(shards[a], gathered[a].at[index(me)], local_sems.at[a])

        def load(m):
            s, u = units[m]
            src = part(win_ref, 0, u) if s == 0 else slot(0, blocks[s], u)
            return pltpu.make_async_copy(src, wbuf.at[m % 2, :, 0:PARTS[u][1]], local_sems.at[n_arrays + m % 2])

        def store(m):
            s, u = units[m]
            col0 = pl.multiple_of(index(blocks[s]) * COLS + PARTS[u][0], LANES)
            return pltpu.make_async_copy(rbuf.at[m % 2, :, 0:PARTS[u][1]],
                                         proj_ref.at[:, pl.ds(col0, PARTS[u][1])], local_sems.at[n_arrays + 2 + m % 2])

        def by_x(a, u):
            return u == 0 if a == 0 else a < 3

        def relay(a, u=0):
            src, to = (blocks[3], blocks[2]) if by_x(a, u) else (blocks[2], blocks[3])
            return copy(a, 3, src, to, u)

        def arrive(m):
            s, u = units[m]
            if s == 1:
                copy(0, 0, sibling, me, u).wait_recv()
            elif 2 <= s <= 4:
                copy(0, s - 1, blocks[s], me, u).wait_recv()
                copy(0, s + 2, blocks[s], sibling, u).start()
                if s < 4 and by_x(0, u) == (s == 3):
                    relay(0, u).start()
            elif s >= 5:
                copy(0, s - 1, blocks[s], me, u).wait_recv()
                if u == 0:
                    for a in range(1, 4):
                        copy(a, s - 4, blocks[s - 3], me).wait_recv()
                        copy(a, s - 1, blocks[s - 3], sibling).start()
                        if s < 7 and by_x(a, 0) == (s == 6):
                            relay(a).start()

        targets = [sibling, blocks[2], blocks[3]]

        def small_copies():
            return _all_copies([small_ref], [small_all], small_send, small_recv)

        @pl.when(g == 0)
        def _():
            for cp in small_copies():
                cp.start()
            for a in range(n_arrays):
                keep(a).start()
            for u in range(n_parts):
                for k, to in enumerate(targets):
                    copy(0, k, me, to, u, from_shard=True).start()
            for a in range(1, 4):
                for k, to in enumerate(targets):
                    copy(a, k, me, to, from_shard=True).start()
            load(0).start()

        @pl.when(g == n_tiles - 2)
        def _():
            for cp in small_copies():
                cp.wait()
            fetch = pltpu.make_async_copy(small_all, small_buf, local_sems.at[n_arrays + 4])
            fetch.start()
            fetch.wait()
            meta_ref[0:TILE - N_META, :] = jnp.zeros((TILE - N_META, D_MODEL), F32)
            meta_ref[TILE - N_META:TILE, :] = jnp.concatenate([small_buf[d, 0:N_META, :] for d in range(N_DEV)], axis=1)

        @pl.when(g < n_tiles)
        def _():
            s0 = jnp.where(g == n_tiles - 1, meta_ref[...], x_ref[...])
            r = lax.rsqrt(_rowmean(s0 * s0) + EPS)
            h32 = (s0 * r) * g_ref[...]
            ht_ref[...] = h32.T.astype(BF16)
            h_all[pl.ds(pl.multiple_of(g * TILE, TILE), TILE), :] = h32.astype(BF16)

        for m in range(n_units):
            @pl.when(g == n_tiles + m)
            def _(m=m):
                load(m).wait()
                if m + 1 < n_units:
                    arrive(m + 1)
                    load(m + 1).start()
                if m >= 2:
                    store(m - 2).wait()

        m_now = jnp.maximum(g - n_tiles, 0)
        u_now = functools.reduce(jnp.add, [jnp.where(m_now == m, u, 0) for m, (_, u) in enumerate(units)])
        for u, (_, width) in enumerate(PARTS):
            @pl.when((g >= n_tiles) & (u_now == u))
            def _(width=width):
                w = wbuf[m_now % 2, :, 0:width]
                for r in range(n_chunk):
                    rbuf[m_now % 2, r * chunk:(r + 1) * chunk, 0:width] = _dot(h_all[r * chunk:(r + 1) * chunk, :], w)

        for m in range(n_units):
            @pl.when(g == n_tiles + m)
            def _(m=m):
                store(m).start()

        @pl.when(g == n_steps - 1)
        def _():
            store(n_units - 2).wait()
            store(n_units - 1).wait()
            for a in range(1, 4):
                copy(a, 0, sibling, me).wait_recv()
                for j in range(3):
                    copy(a, 4 + j, blocks[5 + j], me).wait_recv()
            for a in range(n_arrays):
                for u in range(n_parts if a == 0 else 1):
                    for k, to in enumerate(targets):
                        copy(a, k, me, to, u, from_shard=True).wait_send()
                    relay(a, u).wait_send()
                    for j in range(3):
                        copy(a, 4 + j, blocks[2 + j], sibling, u).wait_send()
                keep(a).wait()

    n_x = n_tiles - 1
    return pl.pallas_call(
        body, name="gather_norm_proj",
        out_shape=[jax.ShapeDtypeStruct((D_MODEL, tp), BF16), jax.ShapeDtypeStruct((tp, D_IN), F32),
                   jax.ShapeDtypeStruct((TILE, D_MODEL), F32), jax.ShapeDtypeStruct((N_DEV,) + small_shard.shape[1:], F32),
                   jax.ShapeDtypeStruct((N_DEV,) + w_in_shard.shape, BF16)]
                  + [jax.ShapeDtypeStruct((N_DEV,) + w.shape, BF16) for w in w_out_shards],
        grid_spec=pltpu.PrefetchScalarGridSpec(
            num_scalar_prefetch=1, grid=(n_steps,),
            in_specs=[pl.BlockSpec((TILE, D_MODEL), lambda g, pos_ref: (jnp.minimum(g, n_x - 1), 0)),
                      _VMEM, _ANY, _ANY, _ANY, _ANY, _ANY],
            out_specs=[pl.BlockSpec((D_MODEL, TILE), lambda g, pos_ref: (0, jnp.minimum(g, n_tiles - 1))),
                       _ANY, _VMEM, _ANY, _ANY, _ANY, _ANY, _ANY],
            scratch_shapes=[pltpu.VMEM((tp, D_MODEL), BF16), pltpu.VMEM((2, D_MODEL, widest), BF16),
                            pltpu.VMEM((2, tp, widest), F32), pltpu.VMEM((N_DEV,) + small_shard.shape[1:], F32),
                            pltpu.SemaphoreType.DMA((7 * n_parts + 21,)), pltpu.SemaphoreType.DMA((7 * n_parts + 21,)),
                            pltpu.SemaphoreType.DMA((9,)),
                            pltpu.SemaphoreType.DMA((N_DEV,)), pltpu.SemaphoreType.DMA((N_DEV,))]),
        compiler_params=pltpu.CompilerParams(dimension_semantics=("arbitrary",), vmem_limit_bytes=VMEM_LIMIT),
    )(pos, x2d, norm_g, small_shard, w_in_shard, *w_out_shards)


C_AVAL, C_AGLU, C_AZ, C_BB, C_BC, C_BX, C_BZ, C_GA, C_GB = (k * D_MODEL for k in range(9))
S_AZ, S_BB, S_BZ, S_GA, S_GB = (k * D_MODEL for k in range(5))


def _fused_pass(proj, x2d, tgt2d, meta_tile, conv_a_w, conv_a_b, ln_a_g, ln_a_b, b_a_out, conv_b_w, final_g,
                w_a, w_b, w_o, w_a_t, w_b_t, w_o_t, n_tiles):
    T = TILE
    tp = n_tiles * T
    inv_d = 1.0 / D_MODEL

    def block_of(tile):
        return jnp.where(tile == 0, n_tiles - 1, tile - 1)

    def cur(i):
        return block_of(jnp.minimum(i, n_tiles - 1))

    def prev(i):
        return block_of(jnp.clip(i - 1, 0, n_tiles - 1))

    def xblk(i):
        return jnp.maximum(jnp.minimum(i, n_tiles - 1) - 1, 0)

    def body(proj_ref, aprev, cprev, x_ref, tgt_ref, meta_ref, caw_ref, cab_ref, lng_ref, lnb_ref, bao_ref, cbw_ref,
             fg_ref, wa_ref, wb_ref, wo_ref, wat_ref, wbt_ref, wot_ref,
             dproj_ref, ds1_ref, lhs_ref, rhs_ref, small_ref,
             ua0_buf, cb_buf, dua1_buf, dc3_buf, stage, ua1_buf, c3_buf,
             dpa_buf, dpb_buf, dcaw8, dcbw8, shift_buf):
        i = pl.program_id(0)
        this, before = i % 2, 1 - i % 2

        @pl.when(i == 0)
        def _init():
            for buf in (ua0_buf, cb_buf, dua1_buf, dc3_buf, dcaw8, dcbw8):
                buf[...] = jnp.zeros(buf.shape, buf.dtype)
            small_ref[...] = jnp.zeros(small_ref.shape, F32)

        @pl.when(i >= 1)
        def _emit_stage():
            dproj_ref[:, C_AZ:C_BC] = stage[:, S_AZ:S_BZ]
            dproj_ref[:, C_BZ:D_IN] = stage[:, S_BZ:S_GB + D_MODEL]

        @pl.when(i < n_tiles)
        def _front():
            def conv_chunk(cc, carry):
                c0 = pl.multiple_of(cc * LANES, LANES)
                lanes = pl.ds(c0, LANES)

                def col(base):
                    return pl.ds(pl.multiple_of(base + cc * LANES, LANES), LANES)

                ua0 = proj_ref[:, col(C_AVAL)] * _sigmoid(proj_ref[:, col(C_AGLU)])
                ua0_buf[this, 0:HALO, lanes] = ua0_buf[before, T:T + HALO, lanes]
                ua0_buf[this, HALO:HALO + T, lanes] = ua0
                acc = jnp.broadcast_to(cab_ref[:, lanes], (T, LANES))
                lead = HALO - (CONV_A - 1)
                for r in range(SUBLANES):
                    taps = [k for k in range(CONV_A) if (k + lead) % SUBLANES == r]
                    rows = T + SUBLANES * max((k + lead) // SUBLANES for k in taps)
                    if r:
                        shift_buf[r, 0:rows, :] = ua0_buf[this, pl.ds(r, rows), lanes]
                    for k in taps:
                        q = (k + lead) // SUBLANES
                        if r:
                            win = shift_buf[r, SUBLANES * q:SUBLANES * q + T, :]
                        else:
                            win = ua0_buf[this, pl.ds(SUBLANES * q, T), lanes]
                        acc = acc + caw_ref[k:k + 1, lanes] * win
                ua1_buf[:, lanes] = acc
                cb = proj_ref[:, col(C_BC)] * proj_ref[:, col(C_BX)]
                cb_buf[this, 0:SUBLANES, lanes] = cb_buf[before, T:T + SUBLANES, lanes]
                cb_buf[this, SUBLANES:SUBLANES + T, lanes] = cb
                lead_b = SUBLANES - (CONV_B - 1)
                acc3 = cbw_ref[0:1, lanes] * cb_buf[this, pl.ds(lead_b, T), lanes]
                for k in range(1, CONV_B):
                    acc3 = acc3 + cbw_ref[k:k + 1, lanes] * cb_buf[this, pl.ds(lead_b + k, T), lanes]
                c3_buf[:, lanes] = acc3
                return carry

            lax.fori_loop(0, N_CHUNK, conv_chunk, 0)

            ua1 = ua1_buf[...]
            xc = ua1 - _rowmean(ua1)
            rstd = lax.rsqrt(_rowmean(xc * xc) + EPS)
            xhat = xc * rstd
            ua2 = xhat * lng_ref[...] + lnb_ref[...]
            sg2 = _sigmoid(ua2)
            ua3 = ua2 * sg2
            a_z = proj_ref[:, C_AZ:C_AZ + D_MODEL]
            sz = _sigmoid(a_z)
            silu_az = a_z * sz
            lhs_ref[0] = (ua3 * silu_az).astype(BF16)
            b_z = proj_ref[:, C_BZ:C_BZ + D_MODEL]
            sbz = _sigmoid(b_z)
            silu_bz = b_z * sbz
            b_b = proj_ref[:, C_BB:C_BB + D_MODEL]
            c3 = c3_buf[...]
            ub = b_b * c3
            lhs_ref[1] = (ub * silu_bz).astype(BF16)

            ya = _dot(lhs_ref[0], wa_ref[...]) + bao_ref[...]
            yb = _dot(lhs_ref[1], wb_ref[...])
            sga = _sigmoid(proj_ref[:, C_GA:C_GA + D_MODEL])
            sgb = _sigmoid(proj_ref[:, C_GB:C_GB + D_MODEL])
            m_b = (sga * ya + sgb * yb).astype(BF16)
            lhs_ref[2] = m_b
            s0 = jnp.where(i == 0, meta_ref[...], x_ref[...])
            s1 = s0 + _dot(m_b, wo_ref[...])
            r1 = lax.rsqrt(_rowmean(s1 * s1) + EPS)
            y = (s1 * r1) * fg_ref[...]
            is_token = (i >= 1).astype(F32)
            err = (y - tgt_ref[...]) * is_token
            small_ref[ROW_LOSS:ROW_LOSS + 1, :] += (0.5 * inv_d) * _colsum(err * err)
            dy = err * inv_d
            small_ref[ROW_FINAL_G:ROW_FINAL_G + 1, :] += _colsum(dy * (s1 * r1))
            gy = dy * fg_ref[...]
            ds1 = r1 * gy - s1 * ((r1 * r1 * r1) * _rowmean(gy * s1))
            ds1_ref[...] = ds1
            ds1_b = ds1.astype(BF16)
            rhs_ref[2] = ds1_b
            dm = _dot(ds1_b, wot_ref[...])
            dya = dm * sga
            dyb = dm * sgb
            stage[:, S_GA:S_GA + D_MODEL] = (dya * ya * (1.0 - sga)).astype(BF16)
            stage[:, S_GB:S_GB + D_MODEL] = (dyb * yb * (1.0 - sgb)).astype(BF16)
            small_ref[ROW_B_A_OUT:ROW_B_A_OUT + 1, :] += _colsum(dya)
            dya_b = dya.astype(BF16)
            dyb_b = dyb.astype(BF16)
            rhs_ref[0] = dya_b
            rhs_ref[1] = dyb_b
            dpa_buf[...] = _dot(dya_b, wat_ref[...])
            dpb_buf[...] = _dot(dyb_b, wbt_ref[...])

            dpa = dpa_buf[...]
            stage[:, S_AZ:S_AZ + D_MODEL] = (dpa * ua3 * (sz + silu_az * (1.0 - sz))).astype(BF16)
            dua2 = dpa * silu_az * (sg2 + ua3 * (1.0 - sg2))
            small_ref[ROW_LN_G:ROW_LN_G + 1, :] += _colsum(dua2 * xhat)
            small_ref[ROW_LN_B:ROW_LN_B + 1, :] += _colsum(dua2)
            dxh = dua2 * lng_ref[...]
            dua1 = rstd * (dxh - _rowmean(dxh) - xhat * _rowmean(dxh * xhat))
            small_ref[ROW_CONV_A_B:ROW_CONV_A_B + 1, :] += _colsum(dua1)
            dua1_buf[this, 0:T, :] = dua1
            dua1_buf[before, T:T + HALO, :] = dua1[0:HALO]
            dpb = dpb_buf[...]
            stage[:, S_BZ:S_BZ + D_MODEL] = (dpb * ub * (sbz + silu_bz * (1.0 - sbz))).astype(BF16)
            dub = dpb * silu_bz
            stage[:, S_BB:S_BB + D_MODEL] = (dub * c3).astype(BF16)
            dc3 = dub * b_b
            dc3_buf[this, 0:T, :] = dc3
            dc3_buf[before, T:T + SUBLANES, :] = dc3[0:SUBLANES]

        @pl.when(i == n_tiles)
        def _no_later_tile():
            dua1_buf[before, T:T + HALO, :] = jnp.zeros((HALO, D_MODEL), F32)
            dc3_buf[before, T:T + SUBLANES, :] = jnp.zeros((SUBLANES, D_MODEL), F32)

        @pl.when(i >= 1)
        def _lagged():
            def convt_chunk(cc, carry):
                c0 = pl.multiple_of(cc * LANES, LANES)
                lanes = pl.ds(c0, LANES)

                def col(base):
                    return pl.ds(pl.multiple_of(base + cc * LANES, LANES), LANES)

                ua0 = ua0_buf[before, HALO:HALO + T, lanes]
                acc = jnp.zeros((T, LANES), F32)
                for r in range(SUBLANES):
                    shifts = [j for j in range(CONV_A) if j % SUBLANES == r]
                    rows = T + shifts[-1] - r
                    if r:
                        shift_buf[r, 0:rows, :] = dua1_buf[before, pl.ds(r, rows), lanes]
                    for j in shifts:
                        k = CONV_A - 1 - j
                        if r:
                            later = shift_buf[r, j - r:j - r + T, :]
                        else:
                            later = dua1_buf[before, pl.ds(j, T), lanes]
                        acc = acc + caw_ref[k:k + 1, lanes] * later
                        dcaw8[SUBLANES * k:SUBLANES * (k + 1), lanes] += _fold8(ua0 * later)
                a_val = aprev[:, col(0)]
                sg = _sigmoid(aprev[:, col(D_MODEL)])
                dproj_ref[:, col(C_AVAL)] = (acc * sg).astype(BF16)
                dproj_ref[:, col(C_AGLU)] = (acc * a_val * (sg * (1.0 - sg))).astype(BF16)

                cb = cb_buf[before, SUBLANES:SUBLANES + T, lanes]
                acc3 = jnp.zeros((T, LANES), F32)
                for j in range(CONV_B):
                    k = CONV_B - 1 - j
                    later = dc3_buf[before, pl.ds(j, T), lanes]
                    acc3 = acc3 + cbw_ref[k:k + 1, lanes] * later
                    dcbw8[SUBLANES * k:SUBLANES * (k + 1), lanes] += _fold8(cb * later)
                dproj_ref[:, col(C_BC)] = (acc3 * cprev[:, col(D_MODEL)]).astype(BF16)
                dproj_ref[:, col(C_BX)] = (acc3 * cprev[:, col(0)]).astype(BF16)
                return carry

            lax.fori_loop(0, N_CHUNK, convt_chunk, 0)

        @pl.when(i == n_tiles)
        def _finish():
            for k in range(CONV_A):
                small_ref[ROW_CONV_A_W + k:ROW_CONV_A_W + k + 1, :] = _colsum(dcaw8[SUBLANES * k:SUBLANES * (k + 1), :])
            for k in range(CONV_B):
                small_ref[ROW_CONV_B_W + k:ROW_CONV_B_W + k + 1, :] = _colsum(dcbw8[SUBLANES * k:SUBLANES * (k + 1), :])

    pair = 2 * D_MODEL
    return pl.pallas_call(
        body, name="fused_pass", grid=(n_tiles + 1,),
        out_shape=[
            jax.ShapeDtypeStruct((tp, D_IN), BF16),
            jax.ShapeDtypeStruct((tp, D_MODEL), F32),
            jax.ShapeDtypeStruct((3, tp, D_MODEL), BF16),
            jax.ShapeDtypeStruct((3, tp, D_MODEL), BF16),
            jax.ShapeDtypeStruct((SMALL_A_ROWS, D_MODEL), F32),
        ],
        in_specs=[
            pl.BlockSpec((T, D_IN), lambda i: (cur(i), 0)),
            pl.BlockSpec((T, pair), lambda i: (prev(i), C_AVAL // pair)),
            pl.BlockSpec((T, pair), lambda i: (prev(i), C_BC // pair)),
            pl.BlockSpec((T, D_MODEL), lambda i: (xblk(i), 0)),
            pl.BlockSpec((T, D_MODEL), lambda i: (xblk(i), 0)),
            _VMEM, _VMEM, _VMEM, _VMEM, _VMEM, _VMEM, _VMEM, _VMEM,
            *[_resident((D_MODEL, D_MODEL)) for _ in range(6)],
        ],
        out_specs=[
            pl.BlockSpec((T, D_IN), lambda i: (prev(i), 0)),
            pl.BlockSpec((T, D_MODEL), lambda i: (cur(i), 0)),
            pl.BlockSpec((3, T, D_MODEL), lambda i: (0, cur(i), 0)),
            pl.BlockSpec((3, T, D_MODEL), lambda i: (0, cur(i), 0)),
            _VMEM,
        ],
        scratch_shapes=[
            pltpu.VMEM((2, HALO + T, D_MODEL), F32),
            pltpu.VMEM((2, SUBLANES + T, D_MODEL), F32),
            pltpu.VMEM((2, T + HALO, D_MODEL), F32),
            pltpu.VMEM((2, T + SUBLANES, D_MODEL), F32),
            pltpu.VMEM((T, 5 * D_MODEL), BF16),
            pltpu.VMEM((T, D_MODEL), F32),
            pltpu.VMEM((T, D_MODEL), F32),
            pltpu.VMEM((T, D_MODEL), F32),
            pltpu.VMEM((T, D_MODEL), F32),
            pltpu.VMEM((32 * SUBLANES, D_MODEL), F32),
            pltpu.VMEM((SUBLANES * SUBLANES, D_MODEL), F32),
            pltpu.VMEM((SUBLANES, T + HALO, LANES), F32),
        ],
        compiler_params=pltpu.CompilerParams(dimension_semantics=("arbitrary",), vmem_limit_bytes=VMEM_LIMIT),
    )(proj, proj, proj, x2d, tgt2d, meta_tile, conv_a_w, conv_a_b, ln_a_g, ln_a_b, b_a_out, conv_b_w, final_g,
      w_a, w_b, w_o, w_a_t, w_b_t, w_o_t)


def _input_bwd(dproj, ds1, x2d, meta_tile, norm_g, w_in_all, row_tile):
    seq = x2d.shape[0]
    n_steps = seq // row_tile
    meta_block = seq // TILE

    def backward(dp_ref, ds1_ref, s0_ref, g_ref, w_ref, out_ref, vec_ref):
        dh = _dot_nt(dp_ref[:, 0:COLS], w_ref[0])
        for j in range(1, N_DEV):
            dh = dh + _dot_nt(dp_ref[:, j * COLS:(j + 1) * COLS], w_ref[j])
        s0v = s0_ref[...]
        r = lax.rsqrt(_rowmean(s0v * s0v) + EPS)
        gh = dh * g_ref[...]
        out_ref[...] = ds1_ref[...] + r * gh - s0v * ((r * r * r) * _rowmean(gh * s0v))
        vec_ref[ROW_NORM_G:ROW_NORM_G + 1, :] += _colsum(dh * (s0v * r))

    def body(dp_ref, ds1_ref, x_ref, dpm_ref, ds1m_ref, meta_ref, g_ref, w_ref, gx_ref, small_ref, gmeta_buf):
        t = pl.program_id(0)

        @pl.when(t == 0)
        def _():
            small_ref[...] = jnp.zeros(small_ref.shape, F32)

        backward(dp_ref, ds1_ref, x_ref, g_ref, w_ref, gx_ref, small_ref)

        @pl.when(t == n_steps - 1)
        def _():
            backward(dpm_ref, ds1m_ref, meta_ref, g_ref, w_ref, gmeta_buf, small_ref)
            small_ref[ROW_META:ROW_META + N_META, :] = gmeta_buf[TILE - N_META:TILE, :]

    return pl.pallas_call(
        body, name="input_bwd", grid=(n_steps,),
        out_shape=[jax.ShapeDtypeStruct(x2d.shape, F32), jax.ShapeDtypeStruct((SMALL_B_ROWS, D_MODEL), F32)],
        in_specs=[pl.BlockSpec((row_tile, D_IN), lambda t: (t, 0)),
                  pl.BlockSpec((row_tile, D_MODEL), lambda t: (t, 0)),
                  pl.BlockSpec((row_tile, D_MODEL), lambda t: (t, 0)),
                  pl.BlockSpec((TILE, D_IN), lambda t: (meta_block, 0)),
                  pl.BlockSpec((TILE, D_MODEL), lambda t: (meta_block, 0)),
                  _VMEM, _VMEM, _resident((N_DEV, D_MODEL, COLS))],
        out_specs=[pl.BlockSpec((row_tile, D_MODEL), lambda t: (t, 0)), _VMEM],
        scratch_shapes=[pltpu.VMEM((TILE, D_MODEL), F32)],
        compiler_params=pltpu.CompilerParams(dimension_semantics=("arbitrary",), vmem_limit_bytes=VMEM_LIMIT),
    )(dproj, ds1, x2d, dproj, ds1, meta_tile, norm_g, w_in_all)


def _grad_w_in_half(pos, h_t, dproj, k_tile, other_side, rides, name, after=None):
    tp = h_t.shape[1]
    n_k = tp // k_tile
    order = [] if after is None else [after]

    def column_block(q, k, pos_ref):
        return k, 2 * q + (1 - pos_ref[2] if other_side else pos_ref[2])

    def body(pos_ref, h_ref, dp_ref, *refs):
        o_ref = refs[-1]

        @pl.when(pl.program_id(1) == 0)
        def _():
            o_ref[...] = jnp.zeros(o_ref.shape, F32)

        o_ref[...] += _dot(h_ref[...], dp_ref[...])

    ride = [a for _, arrays in rides for a in arrays]
    n_arr = len(ride)
    ride_shapes, ride_sems = _ride_shapes(rides)
    body = _riding(body, 3 + len(order), 1, rides,
                   lambda: (pl.program_id(0) == 0) & (pl.program_id(1) == 0),
                   lambda: (pl.program_id(0) == 3) & (pl.program_id(1) == n_k - 1))
    return pl.pallas_call(
        body, name=name,
        out_shape=[jax.ShapeDtypeStruct((4, D_MODEL, COLS), F32)] + ride_shapes,
        grid_spec=pltpu.PrefetchScalarGridSpec(
            num_scalar_prefetch=1, grid=(4, n_k),
            in_specs=[pl.BlockSpec((D_MODEL, k_tile), lambda q, k, pos_ref: (0, k)),
                      pl.BlockSpec((k_tile, COLS), column_block)] + [_ANY] * (len(order) + n_arr),
            out_specs=[pl.BlockSpec((None, D_MODEL, COLS), lambda q, k, pos_ref: (q, 0, 0))] + [_ANY] * n_arr,
            scratch_shapes=ride_sems),
        compiler_params=pltpu.CompilerParams(dimension_semantics=("arbitrary", "arbitrary"),
                                             vmem_limit_bytes=VMEM_LIMIT),
    )(pos, h_t, dproj, *order, *ride)


def _grad_w_out(lhs, rhs, k_tile):
    tp = lhs.shape[1]

    def body(a_ref, b_ref, o_ref):
        @pl.when(pl.program_id(1) == 0)
        def _():
            o_ref[...] = jnp.zeros(o_ref.shape, F32)

        o_ref[...] += _dot_tn(a_ref[...], b_ref[...]).reshape(N_DEV, ROWS_OUT, D_MODEL)

    return pl.pallas_call(
        body, name="grad_w_out", grid=(3, tp // k_tile),
        out_shape=jax.ShapeDtypeStruct((N_DEV, 3, ROWS_OUT, D_MODEL), F32),
        in_specs=[pl.BlockSpec((None, k_tile, D_MODEL), lambda w, k: (w, k, 0)),
                  pl.BlockSpec((None, k_tile, D_MODEL), lambda w, k: (w, k, 0))],
        out_specs=pl.BlockSpec((N_DEV, None, ROWS_OUT, D_MODEL), lambda w, k: (0, w, 0, 0)),
        compiler_params=pltpu.CompilerParams(dimension_semantics=("arbitrary", "arbitrary"),
                                             vmem_limit_bytes=VMEM_LIMIT),
    )(lhs, rhs)


def _adamw_math(w, g, m, v):
    m = ADAM_B1 * m + (1.0 - ADAM_B1) * g
    v = ADAM_B2 * v + (1.0 - ADAM_B2) * (g * g)
    m_hat = m / (1.0 - ADAM_B1 ** ADAM_STEP)
    v_hat = v / (1.0 - ADAM_B2 ** ADAM_STEP)
    delta = -ADAM_LR * (m_hat / (jnp.sqrt(v_hat) + ADAM_EPS) + ADAM_WD * w)
    return delta, m, v


def _adamw_sharded(pos, mine, theirs, landed, weights, row_tile, name, after=None):
    order = [] if after is None else [after]
    rows, n = weights[0][0].shape
    n_slots = mine.shape[0]
    per_shard = rows // row_tile
    assert per_shard == 1 or len(weights) == 1

    def mine_map(j, t, pos_ref):
        chip = 2 * pos_ref[0] + pos_ref[1]
        return (2 * chip + pos_ref[2] if n_slots == N_DEV else chip), j * per_shard + t, 0

    def theirs_map(j, t, pos_ref):
        return 2 * pos_ref[0] + pos_ref[1], j * per_shard + t, 0

    def body(pos_ref, mine_ref, theirs_ref, land_ref, *refs):
        ins, outs = refs[:3 * len(weights)], refs[3 * len(weights) + len(order):]
        g = mine_ref[...] + theirs_ref[...]
        for k in range(3):
            g = g + land_ref[k].astype(F32)
        for j in range(len(weights)):
            @pl.when(pl.program_id(0) == j)
            def _(j=j):
                w_ref, m_ref, v_ref = ins[3 * j:3 * j + 3]
                delta, m_new, v_new = _adamw_math(w_ref[...], g, m_ref[...], v_ref[...])
                for ref, val in zip(outs[4 * j:4 * j + 4], (g, delta, m_new, v_new)):
                    ref[...] = val

    tile = pl.BlockSpec((row_tile, n), lambda j, t, pos_ref: (t, 0))
    res = pl.pallas_call(
        body, name=name,
        out_shape=[jax.ShapeDtypeStruct((rows, n), F32)] * (4 * len(weights)),
        grid_spec=pltpu.PrefetchScalarGridSpec(
            num_scalar_prefetch=1, grid=(len(weights), per_shard),
            in_specs=[pl.BlockSpec((None, row_tile, n), mine_map), pl.BlockSpec((None, row_tile, n), theirs_map),
                      pl.BlockSpec((3, row_tile, n), lambda j, t, pos_ref: (0, j * per_shard + t, 0))]
            + [tile] * (3 * len(weights)) + [_ANY] * len(order),
            out_specs=[tile] * (4 * len(weights))),
        compiler_params=pltpu.CompilerParams(dimension_semantics=("arbitrary", "arbitrary")),
    )(pos, mine, theirs, landed, *[a for wmv in weights for a in wmv], *order)
    return [res[4 * j:4 * j + 4] for j in range(len(weights))]


def _adamw_small(gathered, gathered_cols, params):
    n_par, n_src = len(params), len(gathered)

    def body(*refs):
        g_refs, gc_refs = refs[:n_src], refs[n_src:2 * n_src]
        ins = refs[2 * n_src:2 * n_src + 3 * n_par]
        outs = refs[2 * n_src + 3 * n_par:]
        loss_ref = outs[4 * n_par]

        def reduced(ref, row, n_rows):
            g = ref[0, row:row + n_rows, :]
            for d in range(1, N_DEV):
                g = g + ref[d, row:row + n_rows, :]
            return g

        for p, (src, row, n_rows, sharded, _, _, _) in enumerate(params):
            g = reduced((gc_refs if sharded else g_refs)[src], row, n_rows)
            w_ref, m_ref, v_ref = ins[3 * p:3 * p + 3]
            delta, m_new, v_new = _adamw_math(w_ref[...], g, m_ref[...], v_ref[...])
            outs[4 * p][...] = g
            outs[4 * p + 1][...] = delta
            outs[4 * p + 2][...] = m_new
            outs[4 * p + 3][...] = v_new
        loss = jnp.sum(reduced(g_refs[0], ROW_LOSS, 1), axis=1, keepdims=True)
        loss_ref[...] = jnp.broadcast_to(loss, loss_ref.shape)

    out_shape = []
    for (_, _, _, _, w, _, _) in params:
        out_shape += [jax.ShapeDtypeStruct(w.shape, F32)] * 4
    out_shape.append(jax.ShapeDtypeStruct((1, LANES), F32))
    flat = [a for (_, _, _, _, w, m, v) in params for a in (w, m, v)]
    return pl.pallas_call(
        body, name="adamw_small", out_shape=out_shape,
        in_specs=[_VMEM] * (2 * n_src + len(flat)), out_specs=[_VMEM] * len(out_shape),
    )(*gathered, *gathered_cols, *flat)


def _pad_rows(a, rows):
    return jnp.concatenate([a, jnp.zeros((rows - a.shape[0], a.shape[1]), a.dtype)], axis=0)


def kernel(x, meta_tokens, norm_g, w_in, conv_a_w, conv_a_b, ln_a_g, ln_a_b, w_a_out, b_a_out, conv_b_w, w_b_out, w_out, final_g, loss_target, m_meta_tokens, m_norm_g, m_w_in, m_conv_a_w, m_conv_a_b, m_ln_a_g, m_ln_a_b, m_w_a_out, m_b_a_out, m_conv_b_w, m_w_b_out, m_w_out, m_final_g, v_meta_tokens, v_norm_g, v_w_in, v_conv_a_w, v_conv_a_b, v_ln_a_g, v_ln_a_b, v_w_a_out, v_b_a_out, v_conv_b_w, v_w_b_out, v_w_out, v_final_g):
    seq = x.shape[1]
    assert x.shape == (1, seq, D_MODEL) and seq % TILE == 0 and w_in.shape == (1, D_MODEL, COLS)
    n_tiles = seq // TILE + 1
    tp = n_tiles * TILE
    pos = jnp.stack([lax.axis_index("x"), lax.axis_index("y"), lax.axis_index("c")]).astype(jnp.int32)
    me = 4 * pos[0] + 2 * pos[1] + pos[2]
    x2d = x[0]
    tgt2d = loss_target[0]

    small = jnp.concatenate([meta_tokens, _pad_rows(conv_a_w[0], 32), _pad_rows(conv_b_w[0], SUBLANES)], axis=0)
    final_g2 = final_g.reshape(1, D_MODEL)

    w_out_shards = [w[0].astype(BF16) for w in (w_a_out, w_b_out, w_out)]
    h_t, proj, meta_tile, small_params, w_in_all, *w_out_all = _gather_norm_proj(
        pos, x2d, small[None], norm_g, w_in[0].astype(BF16), w_out_shards, 3)
    small_params = small_params.transpose(1, 0, 2).reshape(small.shape[0], D_MODEL)
    conv_a_full, conv_b_full = small_params[N_META:N_META + 32], small_params[N_META + 32:]
    w_out_all = [w.reshape(D_MODEL, D_MODEL) for w in w_out_all]
    w_out_all_t = [w.T for w in w_out_all]
    dproj, ds1, lhs, rhs, small_a = _fused_pass(
        proj, x2d, tgt2d, meta_tile, conv_a_full, conv_a_b, ln_a_g, ln_a_b, b_a_out, conv_b_full, final_g2,
        w_out_all[0], w_out_all[1], w_out_all[2], w_out_all_t[0], w_out_all_t[1], w_out_all_t[2], n_tiles)
    k_tile = tp // 3
    gw_out = _grad_w_out(lhs, rhs, k_tile).reshape(N_DEV, 3 * ROWS_OUT, D_MODEL)
    gw_far, their_out, small_a_all = _grad_w_in_half(
        pos, h_t, dproj, k_tile, True, [("sibling", (gw_out,)), ("all", (small_a[None],))], "grad_w_in_far")
    parts_out = _chip_partial(pos, gw_out, their_out, (1, 2, 3), BF16, ROWS_OUT, "rs_parts_w_out")
    near_rides = [("sibling_half", (gw_far,)), ("chips", (parts_out,))]
    sems, sent, landing, token = _start_exchanges(near_rides, "rs_near_start")
    (gw_near,) = _grad_w_in_half(pos, h_t, dproj, k_tile, False, [], "grad_w_in_near", after=token)
    their_in, land_out = _wait_exchanges([("sibling_half", 1), ("chips", 1)], sems, sent, landing, gw_near,
                                         "rs_near_wait")
    parts_in = _chip_partial(pos, gw_near, their_in, (1, 2, 3), BF16, 256, "rs_parts_w_in")
    sems, sent, landing, token = _start_exchanges([("chips", (parts_in,))], "rs_chips_start")
    grad_x, small_b = _input_bwd(dproj, ds1, x2d, meta_tile, norm_g + token[0, 0], w_in_all, min(512, seq))

    res_out = _adamw_sharded(
        pos, gw_out, their_out, land_out,
        [(w_a_out[0], m_w_a_out[0], v_w_a_out[0]), (w_b_out[0], m_w_b_out[0], v_w_b_out[0]),
         (w_out[0], m_w_out[0], v_w_out[0])], ROWS_OUT, "adamw_w_out", after=small_b)
    (small_b_all,) = _exchange("all", [small_b[None]], res_out[2][0], "gather_small_grads")
    small_grads = [small_a_all, small_b_all]
    small_cols = [lax.dynamic_slice_in_dim(g, me * LANES, LANES, axis=2) for g in small_grads]
    (land_in,) = _wait_exchanges([("chips", 1)], sems, sent, landing, small_b_all, "rs_chips_wait")
    (res_in,) = _adamw_sharded(pos, gw_near, their_in, land_in, [(w_in[0], m_w_in[0], v_w_in[0])], 128, "adamw_w_in")
    params = [
        (1, ROW_META, N_META, True, meta_tokens, m_meta_tokens, v_meta_tokens),
        (1, ROW_NORM_G, 1, False, norm_g, m_norm_g, v_norm_g),
        (0, ROW_CONV_A_W, CONV_A, True, conv_a_w[0], m_conv_a_w[0], v_conv_a_w[0]),
        (0, ROW_CONV_A_B, 1, False, conv_a_b, m_conv_a_b, v_conv_a_b),
        (0, ROW_LN_G, 1, False, ln_a_g, m_ln_a_g, v_ln_a_g),
        (0, ROW_LN_B, 1, False, ln_a_b, m_ln_a_b, v_ln_a_b),
        (0, ROW_B_A_OUT, 1, False, b_a_out, m_b_a_out, v_b_a_out),
        (0, ROW_CONV_B_W, CONV_B, True, conv_b_w[0], m_conv_b_w[0], v_conv_b_w[0]),
        (0, ROW_FINAL_G, 1, False, final_g2, m_final_g.reshape(1, D_MODEL), v_final_g.reshape(1, D_MODEL)),
    ]
    res_small = _adamw_small(small_grads, small_cols, params)
    loss = res_small[-1][0, 0]

    def small_res(p, kind, shape):
        return res_small[4 * p + kind].reshape(shape)

    per_weight = []
    for kind in range(4):
        per_weight.append([
            small_res(0, kind, meta_tokens.shape),
            small_res(1, kind, norm_g.shape),
            res_in[kind].reshape(w_in.shape),
            small_res(2, kind, conv_a_w.shape),
            small_res(3, kind, conv_a_b.shape),
            small_res(4, kind, ln_a_g.shape),
            small_res(5, kind, ln_a_b.shape),
            res_out[0][kind].reshape(w_a_out.shape),
            small_res(6, kind, b_a_out.shape),
            small_res(7, kind, conv_b_w.shape),
            res_out[1][kind].reshape(w_b_out.shape),
            res_out[2][kind].reshape(w_out.shape),
            small_res(8, kind, final_g.shape),
        ])
    return (loss, grad_x.reshape(x.shape), *per_weight[0], *per_weight[1], *per_weight[2], *per_weight[3])
```

```python
import functools

import jax
import jax.numpy as jnp
from jax import lax
from jax.experimental import pallas as pl
from jax.experimental.pallas import tpu as pltpu

D_MODEL = 1024
N_META = 16
N_DEV = 8
D_IN = 9 * D_MODEL
COLS = D_IN // N_DEV
ROWS_OUT = D_MODEL // N_DEV
CONV_A = 31
CONV_B = 3
EPS = 1e-6

ADAM_LR = 0.001
ADAM_B1 = 0.9
ADAM_B2 = 0.999
ADAM_EPS = 1e-08
ADAM_WD = 0.01
ADAM_STEP = 10

TILE = 128
LANES = 128
N_CHUNK = D_MODEL // LANES
HALO = 32
SUBLANES = 8
VMEM_LIMIT = 56 * 1024 * 1024

ROW_FINAL_G, ROW_B_A_OUT, ROW_LN_G, ROW_LN_B, ROW_CONV_A_B, ROW_LOSS = 0, 1, 2, 3, 4, 5
ROW_CONV_A_W, ROW_CONV_B_W, SMALL_A_ROWS = 8, 40, 48
ROW_NORM_G, ROW_META, SMALL_B_ROWS = 0, 8, 24

MESH = pl.DeviceIdType.MESH
_ANY = pl.BlockSpec(memory_space=pl.ANY)
_VMEM = pl.BlockSpec(memory_space=pltpu.VMEM)
BF16 = jnp.bfloat16
F32 = jnp.float32


def _resident(shape):
    return pl.BlockSpec(shape, lambda *_: (0,) * len(shape), pipeline_mode=pl.Buffered(1))


def _sigmoid(v):
    return jax.nn.sigmoid(v)


def _dot(a, b):
    return jnp.dot(a, b, preferred_element_type=F32)


def _dot_nt(a, b):
    return lax.dot_general(a, b, (((1,), (1,)), ((), ())), preferred_element_type=F32)


def _dot_tn(a, b):
    return lax.dot_general(a, b, (((0,), (0,)), ((), ())), preferred_element_type=F32)


def _colsum(v):
    return jnp.sum(v, axis=0, keepdims=True)


def _rowmean(v):
    parts = [v[:, LANES * c:LANES * (c + 1)] for c in range(v.shape[1] // LANES)]
    return jnp.sum(functools.reduce(jnp.add, parts), axis=-1, keepdims=True) * (1.0 / v.shape[1])


def _fold8(v):
    parts = [v[SUBLANES * g:SUBLANES * (g + 1)] for g in range(v.shape[0] // SUBLANES)]
    return functools.reduce(jnp.add, parts)


def _sibling_copies(srcs, dsts, send_sems, recv_sems):
    x, y, c = lax.axis_index("x"), lax.axis_index("y"), lax.axis_index("c")
    return [pltpu.make_async_remote_copy(
        src_ref=src.at[2 * q + (1 - c)], dst_ref=dst.at[q],
        send_sem=send_sems.at[4 * a + q], recv_sem=recv_sems.at[4 * a + q],
        device_id=(x, y, 1 - c), device_id_type=MESH)
        for a, (src, dst) in enumerate(zip(srcs, dsts)) for q in range(4)]


def _chip_copies(srcs, dsts, send_sems, recv_sems):
    x, y, c = lax.axis_index("x"), lax.axis_index("y"), lax.axis_index("c")
    targets = [(x, 1 - y, c), (1 - x, y, c), (1 - x, 1 - y, c)]
    return [pltpu.make_async_remote_copy(
        src_ref=src.at[k], dst_ref=dst.at[k],
        send_sem=send_sems.at[3 * a + k], recv_sem=recv_sems.at[3 * a + k],
        device_id=targets[k], device_id_type=MESH)
        for a, (src, dst) in enumerate(zip(srcs, dsts)) for k in range(3)]


def _sibling_half_copies(srcs, dsts, send_sems, recv_sems):
    x, y, c = lax.axis_index("x"), lax.axis_index("y"), lax.axis_index("c")
    return [pltpu.make_async_remote_copy(
        src_ref=src.at[q], dst_ref=dst.at[q],
        send_sem=send_sems.at[4 * a + q], recv_sem=recv_sems.at[4 * a + q],
        device_id=(x, y, 1 - c), device_id_type=MESH)
        for a, (src, dst) in enumerate(zip(srcs, dsts)) for q in range(4)]


def _all_copies(srcs, dsts, send_sems, recv_sems):
    x, y, c = lax.axis_index("x"), lax.axis_index("y"), lax.axis_index("c")
    mine = 4 * x + 2 * y + c
    copies = []
    for a, (src, dst) in enumerate(zip(srcs, dsts)):
        copies.append(pltpu.make_async_copy(src.at[0], dst.at[mine], send_sems.at[N_DEV * a]))
        for k in range(1, N_DEV):
            copies.append(pltpu.make_async_remote_copy(
                src_ref=src.at[0], dst_ref=dst.at[mine],
                send_sem=send_sems.at[N_DEV * a + k], recv_sem=recv_sems.at[N_DEV * a + k],
                device_id=(x ^ (k >> 2), y ^ ((k >> 1) & 1), c ^ (k & 1)), device_id_type=MESH))
    return copies


_EXCHANGES = {"sibling": (4, _sibling_copies, 4), "sibling_half": (4, _sibling_half_copies, 4),
              "chips": (3, _chip_copies, 3), "all": (N_DEV, _all_copies, N_DEV)}


def _exchange_shapes(kind, arrays):
    per_array, _, slots = _EXCHANGES[kind]
    out_shape = [jax.ShapeDtypeStruct((slots,) + a.shape[1:], a.dtype) for a in arrays]
    sems = [pltpu.SemaphoreType.DMA((per_array * len(arrays),))] * 2
    return out_shape, sems


def _ride_shapes(rides):
    shapes, sems = [], []
    for kind, arrays in rides:
        ride_shapes, ride_sems = _exchange_shapes(kind, arrays)
        shapes += ride_shapes
        sems += ride_sems
    return shapes, sems


def _riding(body, n_in, n_out, rides, is_first, is_last):
    counts = [len(arrays) for _, arrays in rides]
    n_arr = sum(counts)

    def wrapped(*refs):
        ins, srcs = refs[:n_in], refs[n_in:n_in + n_arr]
        outs = refs[n_in + n_arr:n_in + n_arr + n_out]
        dsts = refs[n_in + n_arr + n_out:n_in + 2 * n_arr + n_out]
        first_sem = len(refs) - 2 * len(rides)
        scratch, sems = refs[n_in + 2 * n_arr + n_out:first_sem], refs[first_sem:]

        def copies():
            made, at = [], 0
            for r, ((kind, _), n) in enumerate(zip(rides, counts)):
                made += _EXCHANGES[kind][1](srcs[at:at + n], dsts[at:at + n], sems[2 * r], sems[2 * r + 1])
                at += n
            return made

        @pl.when(is_first())
        def _():
            for cp in copies():
                cp.start()

        body(*ins, *outs, *scratch)

        @pl.when(is_last())
        def _():
            for cp in copies():
                cp.wait()

    return wrapped


_HBM = pl.BlockSpec(memory_space=pltpu.HBM)
_SEM = pl.BlockSpec(memory_space=pltpu.SEMAPHORE)
_FLOWS = pltpu.SideEffectType.DATAFLOW_SIDE_EFFECTING


def _start_exchanges(rides, name):
    arrays = [a for _, group in rides for a in group]
    shapes, sems = _ride_shapes(rides)
    n_arr, n_sem = len(arrays), len(sems)

    def body(*refs):
        srcs, lands = refs[:n_arr], refs[n_arr:2 * n_arr]
        sem_refs, token = refs[2 * n_arr:2 * n_arr + n_sem], refs[-1]
        at = 0
        for r, (kind, group) in enumerate(rides):
            n = len(group)
            for cp in _EXCHANGES[kind][1](srcs[at:at + n], lands[at:at + n], sem_refs[2 * r], sem_refs[2 * r + 1]):
                cp.start()
            at += n
        token[...] = jnp.zeros(token.shape, token.dtype)

    in_hbm = [pltpu.HBM(a.shape, a.dtype) for a in arrays]
    land_hbm = [pltpu.HBM(sh.shape, sh.dtype) for sh in shapes]
    res = pl.pallas_call(
        body, name=name,
        out_shape=(*sems, *in_hbm, *land_hbm, jax.ShapeDtypeStruct((SUBLANES, LANES), F32)),
        in_specs=[_HBM] * (2 * n_arr), out_specs=(*[_SEM] * n_sem, *[_HBM] * (2 * n_arr), _VMEM),
        input_output_aliases={i: n_sem + i for i in range(2 * n_arr)},
        compiler_params=pltpu.CompilerParams(has_side_effects=_FLOWS),
    )(*[pltpu.with_memory_space_constraint(a, pltpu.HBM) for a in arrays],
      *[pltpu.with_memory_space_constraint(lax.empty(sh.shape, sh.dtype), pltpu.HBM) for sh in shapes])
    return res[:n_sem], res[n_sem:n_sem + n_arr], res[n_sem + n_arr:n_sem + 2 * n_arr], res[-1]


def _wait_exchanges(kinds, sems, arrays, lands, after, name):
    n_arr, n_sem = len(arrays), len(sems)

    def body(*refs):
        srcs, dsts = refs[:n_arr], refs[n_arr:2 * n_arr]
        sem_refs = refs[2 * n_arr:2 * n_arr + n_sem]
        at = 0
        for r, (kind, n) in enumerate(kinds):
            for cp in _EXCHANGES[kind][1](srcs[at:at + n], dsts[at:at + n], sem_refs[2 * r], sem_refs[2 * r + 1]):
                cp.wait()
            at += n

    hbm = [pltpu.HBM(a.shape, a.dtype) for a in (*arrays, *lands)]
    return pl.pallas_call(
        body, name=name, out_shape=tuple(hbm),
        in_specs=[_HBM] * (2 * n_arr) + [_SEM] * n_sem + [_ANY], out_specs=tuple([_HBM] * (2 * n_arr)),
        input_output_aliases={i: i for i in range(2 * n_arr)},
        compiler_params=pltpu.CompilerParams(has_side_effects=_FLOWS),
    )(*arrays, *lands, *sems, after)[n_arr:]


def _chip_partial(pos, mine, theirs, relations, out_dtype, row_tile, name):
    n_slots, m, n = mine.shape
    q0 = relations[0]

    def chip_of(qi, pos_ref):
        q = qi + q0
        return pos_ref[0] ^ (q >> 1), pos_ref[1] ^ (q & 1)

    def mine_map(qi, t, pos_ref):
        px, py = chip_of(qi, pos_ref)
        return (4 * px + 2 * py + pos_ref[2] if n_slots == N_DEV else 2 * px + py), t, 0

    def theirs_map(qi, t, pos_ref):
        px, py = chip_of(qi, pos_ref)
        return 2 * px + py, t, 0

    def body(pos_ref, a_ref, b_ref, o_ref):
        o_ref[...] = (a_ref[...] + b_ref[...]).astype(out_dtype)

    return pl.pallas_call(
        body, name=name,
        out_shape=jax.ShapeDtypeStruct((len(relations), m, n), out_dtype),
        grid_spec=pltpu.PrefetchScalarGridSpec(
            num_scalar_prefetch=1, grid=(len(relations), m // row_tile),
            in_specs=[pl.BlockSpec((None, row_tile, n), mine_map), pl.BlockSpec((None, row_tile, n), theirs_map)],
            out_specs=pl.BlockSpec((None, row_tile, n), lambda qi, t, pos_ref: (qi, t, 0))),
        compiler_params=pltpu.CompilerParams(dimension_semantics=("arbitrary", "arbitrary")),
    )(pos, mine, theirs)


PARTS = ((0, 512), (512, 640))


def _gather_norm_proj(pos, x2d, small_shard, norm_g, w_in_shard, w_out_shards, n_chunk):
    seq = x2d.shape[0]
    n_tiles = seq // TILE + 1
    tp = n_tiles * TILE
    n_parts = len(PARTS)
    widest = max(width for _, width in PARTS)
    units = [(s, u) for s in range(2) for u in range(n_parts)]
    for first in (2, 5):
        units += [(first + j, u) for u in range(n_parts) for j in range(2)] + [(first + 2, u) for u in range(n_parts)]
    n_units = len(units)
    n_steps = n_tiles + n_units
    chunk = tp // n_chunk

    def body(pos_ref, x_ref, g_ref, small_ref, win_ref, wa_ref, wb_ref, wo_ref,
             ht_ref, proj_ref, meta_ref, small_all, win_all, wa_all, wb_all, wo_all,
             h_all, wbuf, rbuf, small_buf, send_sems, recv_sems, local_sems, small_send, small_recv):
        g = pl.program_id(0)
        x, y, c = lax.axis_index("x"), lax.axis_index("y"), lax.axis_index("c")
        me, sibling = (x, y, c), (x, y, 1 - c)
        chips = [(1 - x, y), (x, 1 - y), (1 - x, 1 - y)]
        shards = (win_ref, wa_ref, wb_ref, wo_ref)
        gathered = (win_all, wa_all, wb_all, wo_all)
        n_arrays = len(shards)
        blocks = [me, sibling] + [(*chip, c) for chip in chips] + [(*chip, 1 - c) for chip in chips]

        def index(block):
            px, py, pc = block
            return 4 * px + 2 * py + pc

        def part(ref, a, u):
            return ref.at[:, pl.ds(PARTS[u][0], PARTS[u][1])] if a == 0 else ref

        def slot(a, block, u):
            return part(gathered[a].at[index(block)], a, u)

        def sem(a, k, u):
            return n_parts * k + u if a == 0 else 7 * n_parts + 7 * (a - 1) + k

        def copy(a, k, block, to, u=0, from_shard=False):
            return pltpu.make_async_remote_copy(
                src_ref=part(shards[a], a, u) if from_shard else slot(a, block, u), dst_ref=slot(a, block, u),
                send_sem=send_sems.at[sem(a, k, u)], recv_sem=recv_sems.at[sem(a, k, u)],
                device_id=to, device_id_type=MESH)

        def keep(a):
            return pltpu.make_async_copy(shards[a], gathered[a].at[index(me)], local_sems.at[a])

        def load(m):
            s, u = units[m]
            src = part(win_ref, 0, u) if s == 0 else slot(0, blocks[s], u)
            return pltpu.make_async_copy(src, wbuf.at[m % 2, :, 0:PARTS[u][1]], local_sems.at[n_arrays + m % 2])

        def store(m):
            s, u = units[m]
            col0 = pl.multiple_of(index(blocks[s]) * COLS + PARTS[u][0], LANES)
            return pltpu.make_async_copy(rbuf.at[m % 2, :, 0:PARTS[u][1]],
                                         proj_ref.at[:, pl.ds(col0, PARTS[u][1])], local_sems.at[n_arrays + 2 + m % 2])

        def by_x(a, u):
            return u == 0 if a == 0 else a < 3

        def relay(a, u=0):
            src, to = (blocks[3], blocks[2]) if by_x(a, u) else (blocks[2], blocks[3])
            return copy(a, 3, src, to, u)

        def arrive(m):
            s, u = units[m]
            if s == 1:
                copy(0, 0, sibling, me, u).wait_recv()
            elif 2 <= s <= 4:
                copy(0, s - 1, blocks[s], me, u).wait_recv()
                copy(0, s + 2, blocks[s], sibling, u).start()
                if s < 4 and by_x(0, u) == (s == 3):
                    relay(0, u).start()
            elif s >= 5:
                copy(0, s - 1, blocks[s], me, u).wait_recv()
                if u == 0:
                    for a in range(1, 4):
                        copy(a, s - 4, blocks[s - 3], me).wait_recv()
                        copy(a, s - 1, blocks[s - 3], sibling).start()
                        if s < 7 and by_x(a, 0) == (s == 6):
                            relay(a).start()

        targets = [sibling, blocks[2], blocks[3]]

        def small_copies():
            return _all_copies([small_ref], [small_all], small_send, small_recv)

        @pl.when(g == 0)
        def _():
            for cp in small_copies():
                cp.start()
            for a in range(n_arrays):
                keep(a).start()
            for u in range(n_parts):
                for k, to in enumerate(targets):
                    copy(0, k, me, to, u, from_shard=True).start()
            for a in range(1, 4):
                for k, to in enumerate(targets):
                    copy(a, k, me, to, from_shard=True).start()
            load(0).start()

        @pl.when(g == n_tiles - 2)
        def _():
            for cp in small_copies():
                cp.wait()
            fetch = pltpu.make_async_copy(small_all, small_buf, local_sems.at[n_arrays + 4])
            fetch.start()
            fetch.wait()
            meta_ref[0:TILE - N_META, :] = jnp.zeros((TILE - N_META, D_MODEL), F32)
            meta_ref[TILE - N_META:TILE, :] = jnp.concatenate([small_buf[d, 0:N_META, :] for d in range(N_DEV)], axis=1)

        @pl.when(g < n_tiles)
        def _():
            s0 = jnp.where(g == n_tiles - 1, meta_ref[...], x_ref[...])
            r = lax.rsqrt(_rowmean(s0 * s0) + EPS)
            h32 = (s0 * r) * g_ref[...]
            ht_ref[...] = h32.T.astype(BF16)
            h_all[pl.ds(pl.multiple_of(g * TILE, TILE), TILE), :] = h32.astype(BF16)

        for m in range(n_units):
            @pl.when(g == n_tiles + m)
            def _(m=m):
                load(m).wait()
                if m + 1 < n_units:
                    arrive(m + 1)
                    load(m + 1).start()
                if m >= 2:
                    store(m - 2).wait()

        m_now = jnp.maximum(g - n_tiles, 0)
        u_now = functools.reduce(jnp.add, [jnp.where(m_now == m, u, 0) for m, (_, u) in enumerate(units)])
        for u, (_, width) in enumerate(PARTS):
            @pl.when((g >= n_tiles) & (u_now == u))
            def _(width=width):
                w = wbuf[m_now % 2, :, 0:width]
                for r in range(n_chunk):
                    rbuf[m_now % 2, r * chunk:(r + 1) * chunk, 0:width] = _dot(h_all[r * chunk:(r + 1) * chunk, :], w)

        for m in range(n_units):
            @pl.when(g == n_tiles + m)
            def _(m=m):
                store(m).start()

        @pl.when(g == n_steps - 1)
        def _():
            store(n_units - 2).wait()
            store(n_units - 1).wait()
            for a in range(1, 4):
                copy(a, 0, sibling, me).wait_recv()
                for j in range(3):
                    copy(a, 4 + j, blocks[5 + j], me).wait_recv()
            for a in range(n_arrays):
                for u in range(n_parts if a == 0 else 1):
                    for k, to in enumerate(targets):
                        copy(a, k, me, to, u, from_shard=True).wait_send()
                    relay(a, u).wait_send()
                    for j in range(3):
                        copy(a, 4 + j, blocks[2 + j], sibling, u).wait_send()
                keep(a).wait()

    n_x = n_tiles - 1
    return pl.pallas_call(
        body, name="gather_norm_proj",
        out_shape=[jax.ShapeDtypeStruct((D_MODEL, tp), BF16), jax.ShapeDtypeStruct((tp, D_IN), F32),
                   jax.ShapeDtypeStruct((TILE, D_MODEL), F32), jax.ShapeDtypeStruct((N_DEV,) + small_shard.shape[1:], F32),
                   jax.ShapeDtypeStruct((N_DEV,) + w_in_shard.shape, BF16)]
                  + [jax.ShapeDtypeStruct((N_DEV,) + w.shape, BF16) for w in w_out_shards],
        grid_spec=pltpu.PrefetchScalarGridSpec(
            num_scalar_prefetch=1, grid=(n_steps,),
            in_specs=[pl.BlockSpec((TILE, D_MODEL), lambda g, pos_ref: (jnp.minimum(g, n_x - 1), 0)),
                      _VMEM, _ANY, _ANY, _ANY, _ANY, _ANY],
            out_specs=[pl.BlockSpec((D_MODEL, TILE), lambda g, pos_ref: (0, jnp.minimum(g, n_tiles - 1))),
                       _ANY, _VMEM, _ANY, _ANY, _ANY, _ANY, _ANY],
            scratch_shapes=[pltpu.VMEM((tp, D_MODEL), BF16), pltpu.VMEM((2, D_MODEL, widest), BF16),
                            pltpu.VMEM((2, tp, widest), F32), pltpu.VMEM((N_DEV,) + small_shard.shape[1:], F32),
                            pltpu.SemaphoreType.DMA((7 * n_parts + 21,)), pltpu.SemaphoreType.DMA((7 * n_parts + 21,)),
                            pltpu.SemaphoreType.DMA((9,)),
                            pltpu.SemaphoreType.DMA((N_DEV,)), pltpu.SemaphoreType.DMA((N_DEV,))]),
        compiler_params=pltpu.CompilerParams(dimension_semantics=("arbitrary",), vmem_limit_bytes=VMEM_LIMIT),
    )(pos, x2d, norm_g, small_shard, w_in_shard, *w_out_shards)


C_AVAL, C_AGLU, C_AZ, C_BB, C_BC, C_BX, C_BZ, C_GA, C_GB = (k * D_MODEL for k in range(9))
S_AZ, S_BB, S_BZ, S_GA, S_GB = (k * D_MODEL for k in range(5))


def _fused_pass(proj, x2d, tgt2d, meta_tile, conv_a_w, conv_a_b, ln_a_g, ln_a_b, b_a_out, conv_b_w, final_g,
                w_a, w_b, w_o, w_a_t, w_b_t, w_o_t, n_tiles):
    T = TILE
    tp = n_tiles * T
    inv_d = 1.0 / D_MODEL

    def block_of(tile):
        return jnp.where(tile == 0, n_tiles - 1, tile - 1)

    def cur(i):
        return block_of(jnp.minimum(i, n_tiles - 1))

    def prev(i):
        return block_of(jnp.clip(i - 1, 0, n_tiles - 1))

    def xblk(i):
        return jnp.maximum(jnp.minimum(i, n_tiles - 1) - 1, 0)

    def body(proj_ref, aprev, cprev, x_ref, tgt_ref, meta_ref, caw_ref, cab_ref, lng_ref, lnb_ref, bao_ref, cbw_ref,
             fg_ref, wa_ref, wb_ref, wo_ref, wat_ref, wbt_ref, wot_ref,
             dproj_ref, ds1_ref, lhs_ref, rhs_ref, small_ref,
             ua0_buf, cb_buf, dua1_buf, dc3_buf, stage, ua1_buf, c3_buf,
             dpa_buf, dpb_buf, dcaw8, dcbw8, shift_buf):
        i = pl.program_id(0)
        this, before = i % 2, 1 - i % 2

        @pl.when(i == 0)
        def _init():
            for buf in (ua0_buf, cb_buf, dua1_buf, dc3_buf, dcaw8, dcbw8):
                buf[...] = jnp.zeros(buf.shape, buf.dtype)
            small_ref[...] = jnp.zeros(small_ref.shape, F32)

        @pl.when(i >= 1)
        def _emit_stage():
            dproj_ref[:, C_AZ:C_BC] = stage[:, S_AZ:S_BZ]
            dproj_ref[:, C_BZ:D_IN] = stage[:, S_BZ:S_GB + D_MODEL]

        @pl.when(i < n_tiles)
        def _front():
            def conv_chunk(cc, carry):
                c0 = pl.multiple_of(cc * LANES, LANES)
                lanes = pl.ds(c0, LANES)

                def col(base):
                    return pl.ds(pl.multiple_of(base + cc * LANES, LANES), LANES)

                ua0 = proj_ref[:, col(C_AVAL)] * _sigmoid(proj_ref[:, col(C_AGLU)])
                ua0_buf[this, 0:HALO, lanes] = ua0_buf[before, T:T + HALO, lanes]
                ua0_buf[this, HALO:HALO + T, lanes] = ua0
                acc = jnp.broadcast_to(cab_ref[:, lanes], (T, LANES))
                lead = HALO - (CONV_A - 1)
                for r in range(SUBLANES):
                    taps = [k for k in range(CONV_A) if (k + lead) % SUBLANES == r]
                    rows = T + SUBLANES * max((k + lead) // SUBLANES for k in taps)
                    if r:
                        shift_buf[r, 0:rows, :] = ua0_buf[this, pl.ds(r, rows), lanes]
                    for k in taps:
                        q = (k + lead) // SUBLANES
                        if r:
                            win = shift_buf[r, SUBLANES * q:SUBLANES * q + T, :]
                        else:
                            win = ua0_buf[this, pl.ds(SUBLANES * q, T), lanes]
                        acc = acc + caw_ref[k:k + 1, lanes] * win
                ua1_buf[:, lanes] = acc
                cb = proj_ref[:, col(C_BC)] * proj_ref[:, col(C_BX)]
                cb_buf[this, 0:SUBLANES, lanes] = cb_buf[before, T:T + SUBLANES, lanes]
                cb_buf[this, SUBLANES:SUBLANES + T, lanes] = cb
                lead_b = SUBLANES - (CONV_B - 1)
                acc3 = cbw_ref[0:1, lanes] * cb_buf[this, pl.ds(lead_b, T), lanes]
                for k in range(1, CONV_B):
                    acc3 = acc3 + cbw_ref[k:k + 1, lanes] * cb_buf[this, pl.ds(lead_b + k, T), lanes]
                c3_buf[:, lanes] = acc3
                return carry

            lax.fori_loop(0, N_CHUNK, conv_chunk, 0)

            ua1 = ua1_buf[...]
            xc = ua1 - _rowmean(ua1)
            rstd = lax.rsqrt(_rowmean(xc * xc) + EPS)
            xhat = xc * rstd
            ua2 = xhat * lng_ref[...] + lnb_ref[...]
            sg2 = _sigmoid(ua2)
            ua3 = ua2 * sg2
            a_z = proj_ref[:, C_AZ:C_AZ + D_MODEL]
            sz = _sigmoid(a_z)
            silu_az = a_z * sz
            lhs_ref[0] = (ua3 * silu_az).astype(BF16)
            b_z = proj_ref[:, C_BZ:C_BZ + D_MODEL]
            sbz = _sigmoid(b_z)
            silu_bz = b_z * sbz
            b_b = proj_ref[:, C_BB:C_BB + D_MODEL]
            c3 = c3_buf[...]
            ub = b_b * c3
            lhs_ref[1] = (ub * silu_bz).astype(BF16)

            ya = _dot(lhs_ref[0], wa_ref[...]) + bao_ref[...]
            yb = _dot(lhs_ref[1], wb_ref[...])
            sga = _sigmoid(proj_ref[:, C_GA:C_GA + D_MODEL])
            sgb = _sigmoid(proj_ref[:, C_GB:C_GB + D_MODEL])
            m_b = (sga * ya + sgb * yb).astype(BF16)
            lhs_ref[2] = m_b
            s0 = jnp.where(i == 0, meta_ref[...], x_ref[...])
            s1 = s0 + _dot(m_b, wo_ref[...])
            r1 = lax.rsqrt(_rowmean(s1 * s1) + EPS)
            y = (s1 * r1) * fg_ref[...]
            is_token = (i >= 1).astype(F32)
            err = (y - tgt_ref[...]) * is_token
            small_ref[ROW_LOSS:ROW_LOSS + 1, :] += (0.5 * inv_d) * _colsum(err * err)
            dy = err * inv_d
            small_ref[ROW_FINAL_G:ROW_FINAL_G + 1, :] += _colsum(dy * (s1 * r1))
            gy = dy * fg_ref[...]
            ds1 = r1 * gy - s1 * ((r1 * r1 * r1) * _rowmean(gy * s1))
            ds1_ref[...] = ds1
            ds1_b = ds1.astype(BF16)
            rhs_ref[2] = ds1_b
            dm = _dot(ds1_b, wot_ref[...])
            dya = dm * sga
            dyb = dm * sgb
            stage[:, S_GA:S_GA + D_MODEL] = (dya * ya * (1.0 - sga)).astype(BF16)
            stage[:, S_GB:S_GB + D_MODEL] = (dyb * yb * (1.0 - sgb)).astype(BF16)
            small_ref[ROW_B_A_OUT:ROW_B_A_OUT + 1, :] += _colsum(dya)
            dya_b = dya.astype(BF16)
            dyb_b = dyb.astype(BF16)
            rhs_ref[0] = dya_b
            rhs_ref[1] = dyb_b
            dpa_buf[...] = _dot(dya_b, wat_ref[...])
            dpb_buf[...] = _dot(dyb_b, wbt_ref[...])

            dpa = dpa_buf[...]
            stage[:, S_AZ:S_AZ + D_MODEL] = (dpa * ua3 * (sz + silu_az * (1.0 - sz))).astype(BF16)
            dua2 = dpa * silu_az * (sg2 + ua3 * (1.0 - sg2))
            small_ref[ROW_LN_G:ROW_LN_G + 1, :] += _colsum(dua2 * xhat)
            small_ref[ROW_LN_B:ROW_LN_B + 1, :] += _colsum(dua2)
            dxh = dua2 * lng_ref[...]
            dua1 = rstd * (dxh - _rowmean(dxh) - xhat * _rowmean(dxh * xhat))
            small_ref[ROW_CONV_A_B:ROW_CONV_A_B + 1, :] += _colsum(dua1)
            dua1_buf[this, 0:T, :] = dua1
            dua1_buf[before, T:T + HALO, :] = dua1[0:HALO]
            dpb = dpb_buf[...]
            stage[:, S_BZ:S_BZ + D_MODEL] = (dpb * ub * (sbz + silu_bz * (1.0 - sbz))).astype(BF16)
            dub = dpb * silu_bz
            stage[:, S_BB:S_BB + D_MODEL] = (dub * c3).astype(BF16)
            dc3 = dub * b_b
            dc3_buf[this, 0:T, :] = dc3
            dc3_buf[before, T:T + SUBLANES, :] = dc3[0:SUBLANES]

        @pl.when(i == n_tiles)
        def _no_later_tile():
            dua1_buf[before, T:T + HALO, :] = jnp.zeros((HALO, D_MODEL), F32)
            dc3_buf[before, T:T + SUBLANES, :] = jnp.zeros((SUBLANES, D_MODEL), F32)

        @pl.when(i >= 1)
        def _lagged():
            def convt_chunk(cc, carry):
                c0 = pl.multiple_of(cc * LANES, LANES)
                lanes = pl.ds(c0, LANES)

                def col(base):
                    return pl.ds(pl.multiple_of(base + cc * LANES, LANES), LANES)

                ua0 = ua0_buf[before, HALO:HALO + T, lanes]
                acc = jnp.zeros((T, LANES), F32)
                for r in range(SUBLANES):
                    shifts = [j for j in range(CONV_A) if j % SUBLANES == r]
                    rows = T + shifts[-1] - r
                    if r:
                        shift_buf[r, 0:rows, :] = dua1_buf[before, pl.ds(r, rows), lanes]
                    for j in shifts:
                        k = CONV_A - 1 - j
                        if r:
                            later = shift_buf[r, j - r:j - r + T, :]
                        else:
                            later = dua1_buf[before, pl.ds(j, T), lanes]
                        acc = acc + caw_ref[k:k + 1, lanes] * later
                        dcaw8[SUBLANES * k:SUBLANES * (k + 1), lanes] += _fold8(ua0 * later)
                a_val = aprev[:, col(0)]
                sg = _sigmoid(aprev[:, col(D_MODEL)])
                dproj_ref[:, col(C_AVAL)] = (acc * sg).astype(BF16)
                dproj_ref[:, col(C_AGLU)] = (acc * a_val * (sg * (1.0 - sg))).astype(BF16)

                cb = cb_buf[before, SUBLANES:SUBLANES + T, lanes]
                acc3 = jnp.zeros((T, LANES), F32)
                for j in range(CONV_B):
                    k = CONV_B - 1 - j
                    later = dc3_buf[before, pl.ds(j, T), lanes]
                    acc3 = acc3 + cbw_ref[k:k + 1, lanes] * later
                    dcbw8[SUBLANES * k:SUBLANES * (k + 1), lanes] += _fold8(cb * later)
                dproj_ref[:, col(C_BC)] = (acc3 * cprev[:, col(D_MODEL)]).astype(BF16)
                dproj_ref[:, col(C_BX)] = (acc3 * cprev[:, col(0)]).astype(BF16)
                return carry

            lax.fori_loop(0, N_CHUNK, convt_chunk, 0)

        @pl.when(i == n_tiles)
        def _finish():
            for k in range(CONV_A):
                small_ref[ROW_CONV_A_W + k:ROW_CONV_A_W + k + 1, :] = _colsum(dcaw8[SUBLANES * k:SUBLANES * (k + 1), :])
            for k in range(CONV_B):
                small_ref[ROW_CONV_B_W + k:ROW_CONV_B_W + k + 1, :] = _colsum(dcbw8[SUBLANES * k:SUBLANES * (k + 1), :])

    pair = 2 * D_MODEL
    return pl.pallas_call(
        body, name="fused_pass", grid=(n_tiles + 1,),
        out_shape=[
            jax.ShapeDtypeStruct((tp, D_IN), BF16),
            jax.ShapeDtypeStruct((tp, D_MODEL), F32),
            jax.ShapeDtypeStruct((3, tp, D_MODEL), BF16),
            jax.ShapeDtypeStruct((3, tp, D_MODEL), BF16),
            jax.ShapeDtypeStruct((SMALL_A_ROWS, D_MODEL), F32),
        ],
        in_specs=[
            pl.BlockSpec((T, D_IN), lambda i: (cur(i), 0)),
            pl.BlockSpec((T, pair), lambda i: (prev(i), C_AVAL // pair)),
            pl.BlockSpec((T, pair), lambda i: (prev(i), C_BC // pair)),
            pl.BlockSpec((T, D_MODEL), lambda i: (xblk(i), 0)),
            pl.BlockSpec((T, D_MODEL), lambda i: (xblk(i), 0)),
            _VMEM, _VMEM, _VMEM, _VMEM, _VMEM, _VMEM, _VMEM, _VMEM,
            *[_resident((D_MODEL, D_MODEL)) for _ in range(6)],
        ],
        out_specs=[
            pl.BlockSpec((T, D_IN), lambda i: (prev(i), 0)),
            pl.BlockSpec((T, D_MODEL), lambda i: (cur(i), 0)),
            pl.BlockSpec((3, T, D_MODEL), lambda i: (0, cur(i), 0)),
            pl.BlockSpec((3, T, D_MODEL), lambda i: (0, cur(i), 0)),
            _VMEM,
        ],
        scratch_shapes=[
            pltpu.VMEM((2, HALO + T, D_MODEL), F32),
            pltpu.VMEM((2, SUBLANES + T, D_MODEL), F32),
            pltpu.VMEM((2, T + HALO, D_MODEL), F32),
            pltpu.VMEM((2, T + SUBLANES, D_MODEL), F32),
            pltpu.VMEM((T, 5 * D_MODEL), BF16),
            pltpu.VMEM((T, D_MODEL), F32),
            pltpu.VMEM((T, D_MODEL), F32),
            pltpu.VMEM((T, D_MODEL), F32),
            pltpu.VMEM((T, D_MODEL), F32),
            pltpu.VMEM((32 * SUBLANES, D_MODEL), F32),
            pltpu.VMEM((SUBLANES * SUBLANES, D_MODEL), F32),
            pltpu.VMEM((SUBLANES, T + HALO, LANES), F32),
        ],
        compiler_params=pltpu.CompilerParams(dimension_semantics=("arbitrary",), vmem_limit_bytes=VMEM_LIMIT),
    )(proj, proj, proj, x2d, tgt2d, meta_tile, conv_a_w, conv_a_b, ln_a_g, ln_a_b, b_a_out, conv_b_w, final_g,
      w_a, w_b, w_o, w_a_t, w_b_t, w_o_t)


def _input_bwd(dproj, ds1, x2d, meta_tile, norm_g, w_in_all, row_tile):
    seq = x2d.shape[0]
    n_steps = seq // row_tile
    meta_block = seq // TILE

    def backward(dp_ref, ds1_ref, s0_ref, g_ref, w_ref, out_ref, vec_ref):
        dh = _dot_nt(dp_ref[:, 0:COLS], w_ref[0])
        for j in range(1, N_DEV):
            dh = dh + _dot_nt(dp_ref[:, j * COLS:(j + 1) * COLS], w_ref[j])
        s0v = s0_ref[...]
        r = lax.rsqrt(_rowmean(s0v * s0v) + EPS)
        gh = dh * g_ref[...]
        out_ref[...] = ds1_ref[...] + r * gh - s0v * ((r * r * r) * _rowmean(gh * s0v))
        vec_ref[ROW_NORM_G:ROW_NORM_G + 1, :] += _colsum(dh * (s0v * r))

    def body(dp_ref, ds1_ref, x_ref, dpm_ref, ds1m_ref, meta_ref, g_ref, w_ref, gx_ref, small_ref, gmeta_buf):
        t = pl.program_id(0)

        @pl.when(t == 0)
        def _():
            small_ref[...] = jnp.zeros(small_ref.shape, F32)

        backward(dp_ref, ds1_ref, x_ref, g_ref, w_ref, gx_ref, small_ref)

        @pl.when(t == n_steps - 1)
        def _():
            backward(dpm_ref, ds1m_ref, meta_ref, g_ref, w_ref, gmeta_buf, small_ref)
            small_ref[ROW_META:ROW_META + N_META, :] = gmeta_buf[TILE - N_META:TILE, :]

    return pl.pallas_call(
        body, name="input_bwd", grid=(n_steps,),
        out_shape=[jax.ShapeDtypeStruct(x2d.shape, F32), jax.ShapeDtypeStruct((SMALL_B_ROWS, D_MODEL), F32)],
        in_specs=[pl.BlockSpec((row_tile, D_IN), lambda t: (t, 0)),
                  pl.BlockSpec((row_tile, D_MODEL), lambda t: (t, 0)),
                  pl.BlockSpec((row_tile, D_MODEL), lambda t: (t, 0)),
                  pl.BlockSpec((TILE, D_IN), lambda t: (meta_block, 0)),
                  pl.BlockSpec((TILE, D_MODEL), lambda t: (meta_block, 0)),
                  _VMEM, _VMEM, _resident((N_DEV, D_MODEL, COLS))],
        out_specs=[pl.BlockSpec((row_tile, D_MODEL), lambda t: (t, 0)), _VMEM],
        scratch_shapes=[pltpu.VMEM((TILE, D_MODEL), F32)],
        compiler_params=pltpu.CompilerParams(dimension_semantics=("arbitrary",), vmem_limit_bytes=VMEM_LIMIT),
    )(dproj, ds1, x2d, dproj, ds1, meta_tile, norm_g, w_in_all)


def _grad_w_in_half(pos, h_t, dproj, k_tile, other_side, rides, name, after=None):
    tp = h_t.shape[1]
    n_k = tp // k_tile
    order = [] if after is None else [after]

    def column_block(q, k, pos_ref):
        return k, 2 * q + (1 - pos_ref[2] if other_side else pos_ref[2])

    def body(pos_ref, h_ref, dp_ref, *refs):
        o_ref = refs[-1]

        @pl.when(pl.program_id(1) == 0)
        def _():
            o_ref[...] = jnp.zeros(o_ref.shape, F32)

        o_ref[...] += _dot(h_ref[...], dp_ref[...])

    ride = [a for _, arrays in rides for a in arrays]
    n_arr = len(ride)
    ride_shapes, ride_sems = _ride_shapes(rides)
    body = _riding(body, 3 + len(order), 1, rides,
                   lambda: (pl.program_id(0) == 0) & (pl.program_id(1) == 0),
                   lambda: (pl.program_id(0) == 3) & (pl.program_id(1) == n_k - 1))
    return pl.pallas_call(
        body, name=name,
        out_shape=[jax.ShapeDtypeStruct((4, D_MODEL, COLS), F32)] + ride_shapes,
        grid_spec=pltpu.PrefetchScalarGridSpec(
            num_scalar_prefetch=1, grid=(4, n_k),
            in_specs=[pl.BlockSpec((D_MODEL, k_tile), lambda q, k, pos_ref: (0, k)),
                      pl.BlockSpec((k_tile, COLS), column_block)] + [_ANY] * (len(order) + n_arr),
            out_specs=[pl.BlockSpec((None, D_MODEL, COLS), lambda q, k, pos_ref: (q, 0, 0))] + [_ANY] * n_arr,
            scratch_shapes=ride_sems),
        compiler_params=pltpu.CompilerParams(dimension_semantics=("arbitrary", "arbitrary"),
                                             vmem_limit_bytes=VMEM_LIMIT),
    )(pos, h_t, dproj, *order, *ride)


def _grad_w_out(lhs, rhs, k_tile):
    tp = lhs.shape[1]

    def body(a_ref, b_ref, o_ref):
        @pl.when(pl.program_id(1) == 0)
        def _():
            o_ref[...] = jnp.zeros(o_ref.shape, F32)

        o_ref[...] += _dot_tn(a_ref[...], b_ref[...]).reshape(N_DEV, ROWS_OUT, D_MODEL)

    return pl.pallas_call(
        body, name="grad_w_out", grid=(3, tp // k_tile),
        out_shape=jax.ShapeDtypeStruct((N_DEV, 3, ROWS_OUT, D_MODEL), F32),
        in_specs=[pl.BlockSpec((None, k_tile, D_MODEL), lambda w, k: (w, k, 0)),
                  pl.BlockSpec((None, k_tile, D_MODEL), lambda w, k: (w, k, 0))],
        out_specs=pl.BlockSpec((N_DEV, None, ROWS_OUT, D_MODEL), lambda w, k: (0, w, 0, 0)),
        compiler_params=pltpu.CompilerParams(dimension_semantics=("arbitrary", "arbitrary"),
                                             vmem_limit_bytes=VMEM_LIMIT),
    )(lhs, rhs)


def _adamw_math(w, g, m, v):
    m = ADAM_B1 * m + (1.0 - ADAM_B1) * g
    v = ADAM_B2 * v + (1.0 - ADAM_B2) * (g * g)
    m_hat = m / (1.0 - ADAM_B1 ** ADAM_STEP)
    v_hat = v / (1.0 - ADAM_B2 ** ADAM_STEP)
    delta = -ADAM_LR * (m_hat / (jnp.sqrt(v_hat) + ADAM_EPS) + ADAM_WD * w)
    return delta, m, v


def _adamw_sharded(pos, mine, theirs, landed, weights, row_tile, name, after=None):
    order = [] if after is None else [after]
    rows, n = weights[0][0].shape
    n_slots = mine.shape[0]
    per_shard = rows // row_tile
    assert per_shard == 1 or len(weights) == 1

    def mine_map(j, t, pos_ref):
        chip = 2 * pos_ref[0] + pos_ref[1]
        return (2 * chip + pos_ref[2] if n_slots == N_DEV else chip), j * per_shard + t, 0

    def theirs_map(j, t, pos_ref):
        return 2 * pos_ref[0] + pos_ref[1], j * per_shard + t, 0

    def body(pos_ref, mine_ref, theirs_ref, land_ref, *refs):
        ins, outs = refs[:3 * len(weights)], refs[3 * len(weights) + len(order):]
        g = mine_ref[...] + theirs_ref[...]
        for k in range(3):
            g = g + land_ref[k].astype(F32)
        for j in range(len(weights)):
            @pl.when(pl.program_id(0) == j)
            def _(j=j):
                w_ref, m_ref, v_ref = ins[3 * j:3 * j + 3]
                delta, m_new, v_new = _adamw_math(w_ref[...], g, m_ref[...], v_ref[...])
                for ref, val in zip(outs[4 * j:4 * j + 4], (g, delta, m_new, v_new)):
                    ref[...] = val

    tile = pl.BlockSpec((row_tile, n), lambda j, t, pos_ref: (t, 0))
    res = pl.pallas_call(
        body, name=name,
        out_shape=[jax.ShapeDtypeStruct((rows, n), F32)] * (4 * len(weights)),
        grid_spec=pltpu.PrefetchScalarGridSpec(
            num_scalar_prefetch=1, grid=(len(weights), per_shard),
            in_specs=[pl.BlockSpec((None, row_tile, n), mine_map), pl.BlockSpec((None, row_tile, n), theirs_map),
                      pl.BlockSpec((3, row_tile, n), lambda j, t, pos_ref: (0, j * per_shard + t, 0))]
            + [tile] * (3 * len(weights)) + [_ANY] * len(order),
            out_specs=[tile] * (4 * len(weights))),
        compiler_params=pltpu.CompilerParams(dimension_semantics=("arbitrary", "arbitrary")),
    )(pos, mine, theirs, landed, *[a for wmv in weights for a in wmv], *order)
    return [res[4 * j:4 * j + 4] for j in range(len(weights))]


def _adamw_small(gathered, gathered_cols, params):
    n_par, n_src = len(params), len(gathered)

    def body(*refs):
        g_refs, gc_refs = refs[:n_src], refs[n_src:2 * n_src]
        ins = refs[2 * n_src:2 * n_src + 3 * n_par]
        outs = refs[2 * n_src + 3 * n_par:]
        loss_ref = outs[4 * n_par]

        def reduced(ref, row, n_rows):
            g = ref[0, row:row + n_rows, :]
            for d in range(1, N_DEV):
                g = g + ref[d, row:row + n_rows, :]
            return g

        for p, (src, row, n_rows, sharded, _, _, _) in enumerate(params):
            g = reduced((gc_refs if sharded else g_refs)[src], row, n_rows)
            w_ref, m_ref, v_ref = ins[3 * p:3 * p + 3]
            delta, m_new, v_new = _adamw_math(w_ref[...], g, m_ref[...], v_ref[...])
            outs[4 * p][...] = g
            outs[4 * p + 1][...] = delta
            outs[4 * p + 2][...] = m_new
            outs[4 * p + 3][...] = v_new
        loss = jnp.sum(reduced(g_refs[0], ROW_LOSS, 1), axis=1, keepdims=True)
        loss_ref[...] = jnp.broadcast_to(loss, loss_ref.shape)

    out_shape = []
    for (_, _, _, _, w, _, _) in params:
        out_shape += [jax.ShapeDtypeStruct(w.shape, F32)] * 4
    out_shape.append(jax.ShapeDtypeStruct((1, LANES), F32))
    flat = [a for (_, _, _, _, w, m, v) in params for a in (w, m, v)]
    return pl.pallas_call(
        body, name="adamw_small", out_shape=out_shape,
        in_specs=[_VMEM] * (2 * n_src + len(flat)), out_specs=[_VMEM] * len(out_shape),
    )(*gathered, *gathered_cols, *flat)


def _pad_rows(a, rows):
    return jnp.concatenate([a, jnp.zeros((rows - a.shape[0], a.shape[1]), a.dtype)], axis=0)


def kernel(x, meta_tokens, norm_g, w_in, conv_a_w, conv_a_b, ln_a_g, ln_a_b, w_a_out, b_a_out, conv_b_w, w_b_out, w_out, final_g, loss_target, m_meta_tokens, m_norm_g, m_w_in, m_conv_a_w, m_conv_a_b, m_ln_a_g, m_ln_a_b, m_w_a_out, m_b_a_out, m_conv_b_w, m_w_b_out, m_w_out, m_final_g, v_meta_tokens, v_norm_g, v_w_in, v_conv_a_w, v_conv_a_b, v_ln_a_g, v_ln_a_b, v_w_a_out, v_b_a_out, v_conv_b_w, v_w_b_out, v_w_out, v_final_g):
    seq = x.shape[1]
    assert x.shape == (1, seq, D_MODEL) and seq % TILE == 0 and w_in.shape == (1, D_MODEL, COLS)
    n_tiles = seq // TILE + 1
    tp = n_tiles * TILE
    pos = jnp.stack([lax.axis_index("x"), lax.axis_index("y"), lax.axis_index("c")]).astype(jnp.int32)
    me = 4 * pos[0] + 2 * pos[1] + pos[2]
    x2d = x[0]
    tgt2d = loss_target[0]

    small = jnp.concatenate([meta_tokens, _pad_rows(conv_a_w[0], 32), _pad_rows(conv_b_w[0], SUBLANES)], axis=0)
    final_g2 = final_g.reshape(1, D_MODEL)

    w_out_shards = [w[0].astype(BF16) for w in (w_a_out, w_b_out, w_out)]
    h_t, proj, meta_tile, small_params, w_in_all, *w_out_all = _gather_norm_proj(
        pos, x2d, small[None], norm_g, w_in[0].astype(BF16), w_out_shards, 3)
    small_params = small_params.transpose(1, 0, 2).reshape(small.shape[0], D_MODEL)
    conv_a_full, conv_b_full = small_params[N_META:N_META + 32], small_params[N_META + 32:]
    w_out_all = [w.reshape(D_MODEL, D_MODEL) for w in w_out_all]
    w_out_all_t = [w.T for w in w_out_all]
    dproj, ds1, lhs, rhs, small_a = _fused_pass(
        proj, x2d, tgt2d, meta_tile, conv_a_full, conv_a_b, ln_a_g, ln_a_b, b_a_out, conv_b_full, final_g2,
        w_out_all[0], w_out_all[1], w_out_all[2], w_out_all_t[0], w_out_all_t[1], w_out_all_t[2], n_tiles)
    k_tile = tp // 3
    gw_out = _grad_w_out(lhs, rhs, k_tile).reshape(N_DEV, 3 * ROWS_OUT, D_MODEL)
    gw_far, their_out, small_a_all = _grad_w_in_half(
        pos, h_t, dproj, k_tile, True, [("sibling", (gw_out,)), ("all", (small_a[None],))], "grad_w_in_far")
    parts_out = _chip_partial(pos, gw_out, their_out, (1, 2, 3), BF16, ROWS_OUT, "rs_parts_w_out")
    near_rides = [("sibling_half", (gw_far,)), ("chips", (parts_out,))]
    sems, sent, landing, token = _start_exchanges(near_rides, "rs_near_start")
    (gw_near,) = _grad_w_in_half(pos, h_t, dproj, k_tile, False, [], "grad_w_in_near", after=token)
    their_in, land_out = _wait_exchanges([("sibling_half", 1), ("chips", 1)], sems, sent, landing, gw_near,
                                         "rs_near_wait")
    parts_in = _chip_partial(pos, gw_near, their_in, (1, 2, 3), BF16, 256, "rs_parts_w_in")
    sems, sent, landing, token = _start_exchanges([("chips", (parts_in,))], "rs_chips_start")
    grad_x, small_b = _input_bwd(dproj, ds1, x2d, meta_tile, norm_g + token[0, 0], w_in_all, min(512, seq))

    res_out = _adamw_sharded(
        pos, gw_out, their_out, land_out,
        [(w_a_out[0], m_w_a_out[0], v_w_a_out[0]), (w_b_out[0], m_w_b_out[0], v_w_b_out[0]),
         (w_out[0], m_w_out[0], v_w_out[0])], ROWS_OUT, "adamw_w_out", after=small_b)
    (land_in,) = _wait_exchanges([("chips", 1)], sems, sent, landing, res_out[2][0], "rs_chips_wait")
    sems, sent, landing, token = _start_exchanges([("all", (small_b[None],))], "gather_small_grads_start")
    (res_in,) = _adamw_sharded(pos, gw_near, their_in, land_in, [(w_in[0], m_w_in[0], v_w_in[0])], 128, "adamw_w_in",
                               after=token)
    (small_b_all,) = _wait_exchanges([("all", 1)], sems, sent, landing, res_in[0], "gather_small_grads_wait")
    small_grads = [small_a_all, small_b_all]
    small_cols = [lax.dynamic_slice_in_dim(g, me * LANES, LANES, axis=2) for g in small_grads]
    params = [
        (1, ROW_META, N_META, True, meta_tokens, m_meta_tokens, v_meta_tokens),
        (1, ROW_NORM_G, 1, False, norm_g, m_norm_g, v_norm_g),
        (0, ROW_CONV_A_W, CONV_A, True, conv_a_w[0], m_conv_a_w[0], v_conv_a_w[0]),
        (0, ROW_CONV_A_B, 1, False, conv_a_b, m_conv_a_b, v_conv_a_b),
        (0, ROW_LN_G, 1, False, ln_a_g, m_ln_a_g, v_ln_a_g),
        (0, ROW_LN_B, 1, False, ln_a_b, m_ln_a_b, v_ln_a_b),
        (0, ROW_B_A_OUT, 1, False, b_a_out, m_b_a_out, v_b_a_out),
        (0, ROW_CONV_B_W, CONV_B, True, conv_b_w[0], m_conv_b_w[0], v_conv_b_w[0]),
        (0, ROW_FINAL_G, 1, False, final_g2, m_final_g.reshape(1, D_MODEL), v_final_g.reshape(1, D_MODEL)),
    ]
    res_small = _adamw_small(small_grads, small_cols, params)
    loss = res_small[-1][0, 0]

    def small_res(p, kind, shape):
        return res_small[4 * p + kind].reshape(shape)

    per_weight = []
    for kind in range(4):
        per_weight.append([
            small_res(0, kind, meta_tokens.shape),
            small_res(1, kind, norm_g.shape),
            res_in[kind].reshape(w_in.shape),
            small_res(2, kind, conv_a_w.shape),
            small_res(3, kind, conv_a_b.shape),
            small_res(4, kind, ln_a_g.shape),
            small_res(5, kind, ln_a_b.shape),
            res_out[0][kind].reshape(w_a_out.shape),
            small_res(6, kind, b_a_out.shape),
            small_res(7, kind, conv_b_w.shape),
            res_out[1][kind].reshape(w_b_out.shape),
            res_out[2][kind].reshape(w_out.shape),
            small_res(8, kind, final_g.shape),
        ])
    return (loss, grad_x.reshape(x.shape), *per_weight[0], *per_weight[1], *per_weight[2], *per_weight[3])
```

```python
import functools

import jax
import jax.numpy as jnp
from jax import lax
from jax.experimental import pallas as pl
from jax.experimental.pallas import tpu as pltpu

D_MODEL = 1024
N_META = 16
N_DEV = 8
D_IN = 9 * D_MODEL
COLS = D_IN // N_DEV
ROWS_OUT = D_MODEL // N_DEV
CONV_A = 31
CONV_B = 3
EPS = 1e-6

ADAM_LR = 0.001
ADAM_B1 = 0.9
ADAM_B2 = 0.999
ADAM_EPS = 1e-08
ADAM_WD = 0.01
ADAM_STEP = 10

TILE = 128
LANES = 128
N_CHUNK = D_MODEL // LANES
HALO = 32
SUBLANES = 8
VMEM_LIMIT = 56 * 1024 * 1024

ROW_FINAL_G, ROW_B_A_OUT, ROW_LN_G, ROW_LN_B, ROW_CONV_A_B, ROW_LOSS = 0, 1, 2, 3, 4, 5
ROW_CONV_A_W, ROW_CONV_B_W, SMALL_A_ROWS = 8, 40, 48
ROW_NORM_G, ROW_META, SMALL_B_ROWS = 0, 8, 24

MESH = pl.DeviceIdType.MESH
_ANY = pl.BlockSpec(memory_space=pl.ANY)
_VMEM = pl.BlockSpec(memory_space=pltpu.VMEM)
BF16 = jnp.bfloat16
F32 = jnp.float32


def _resident(shape):
    return pl.BlockSpec(shape, lambda *_: (0,) * len(shape), pipeline_mode=pl.Buffered(1))


def _sigmoid(v):
    return jax.nn.sigmoid(v)


def _dot(a, b):
    return jnp.dot(a, b, preferred_element_type=F32)


def _dot_nt(a, b):
    return lax.dot_general(a, b, (((1,), (1,)), ((), ())), preferred_element_type=F32)


def _dot_tn(a, b):
    return lax.dot_general(a, b, (((0,), (0,)), ((), ())), preferred_element_type=F32)


def _colsum(v):
    return jnp.sum(v, axis=0, keepdims=True)


def _rowmean(v):
    parts = [v[:, LANES * c:LANES * (c + 1)] for c in range(v.shape[1] // LANES)]
    return jnp.sum(functools.reduce(jnp.add, parts), axis=-1, keepdims=True) * (1.0 / v.shape[1])


def _fold8(v):
    parts = [v[SUBLANES * g:SUBLANES * (g + 1)] for g in range(v.shape[0] // SUBLANES)]
    return functools.reduce(jnp.add, parts)


def _sibling_copies(srcs, dsts, send_sems, recv_sems):
    x, y, c = lax.axis_index("x"), lax.axis_index("y"), lax.axis_index("c")
    return [pltpu.make_async_remote_copy(
        src_ref=src.at[2 * q + (1 - c)], dst_ref=dst.at[q],
        send_sem=send_sems.at[4 * a + q], recv_sem=recv_sems.at[4 * a + q],
        device_id=(x, y, 1 - c), device_id_type=MESH)
        for a, (src, dst) in enumerate(zip(srcs, dsts)) for q in range(4)]


def _chip_copies(srcs, dsts, send_sems, recv_sems):
    x, y, c = lax.axis_index("x"), lax.axis_index("y"), lax.axis_index("c")
    targets = [(x, 1 - y, c), (1 - x, y, c), (1 - x, 1 - y, c)]
    return [pltpu.make_async_remote_copy(
        src_ref=src.at[k], dst_ref=dst.at[k],
        send_sem=send_sems.at[3 * a + k], recv_sem=recv_sems.at[3 * a + k],
        device_id=targets[k], device_id_type=MESH)
        for a, (src, dst) in enumerate(zip(srcs, dsts)) for k in range(3)]


def _sibling_half_copies(srcs, dsts, send_sems, recv_sems):
    x, y, c = lax.axis_index("x"), lax.axis_index("y"), lax.axis_index("c")
    return [pltpu.make_async_remote_copy(
        src_ref=src.at[q], dst_ref=dst.at[q],
        send_sem=send_sems.at[4 * a + q], recv_sem=recv_sems.at[4 * a + q],
        device_id=(x, y, 1 - c), device_id_type=MESH)
        for a, (src, dst) in enumerate(zip(srcs, dsts)) for q in range(4)]


def _all_copies(srcs, dsts, send_sems, recv_sems):
    x, y, c = lax.axis_index("x"), lax.axis_index("y"), lax.axis_index("c")
    mine = 4 * x + 2 * y + c
    copies = []
    for a, (src, dst) in enumerate(zip(srcs, dsts)):
        copies.append(pltpu.make_async_copy(src.at[0], dst.at[mine], send_sems.at[N_DEV * a]))
        for k in range(1, N_DEV):
            copies.append(pltpu.make_async_remote_copy(
                src_ref=src.at[0], dst_ref=dst.at[mine],
                send_sem=send_sems.at[N_DEV * a + k], recv_sem=recv_sems.at[N_DEV * a + k],
                device_id=(x ^ (k >> 2), y ^ ((k >> 1) & 1), c ^ (k & 1)), device_id_type=MESH))
    return copies


def _chip_copies_by_relation(srcs, dsts, send_sems, recv_sems):
    return _chip_copies([src.at[pl.ds(1, 3)] for src in srcs], dsts, send_sems, recv_sems)


_EXCHANGES = {"sibling": (4, _sibling_copies, 4), "sibling_half": (4, _sibling_half_copies, 4),
              "chips": (3, _chip_copies, 3), "chips_by_relation": (3, _chip_copies_by_relation, 3),
              "all": (N_DEV, _all_copies, N_DEV)}


def _exchange_shapes(kind, arrays):
    per_array, _, slots = _EXCHANGES[kind]
    out_shape = [jax.ShapeDtypeStruct((slots,) + a.shape[1:], a.dtype) for a in arrays]
    sems = [pltpu.SemaphoreType.DMA((per_array * len(arrays),))] * 2
    return out_shape, sems


def _ride_shapes(rides):
    shapes, sems = [], []
    for kind, arrays in rides:
        ride_shapes, ride_sems = _exchange_shapes(kind, arrays)
        shapes += ride_shapes
        sems += ride_sems
    return shapes, sems


def _riding(body, n_in, n_out, rides, is_first, is_last):
    counts = [len(arrays) for _, arrays in rides]
    n_arr = sum(counts)

    def wrapped(*refs):
        ins, srcs = refs[:n_in], refs[n_in:n_in + n_arr]
        outs = refs[n_in + n_arr:n_in + n_arr + n_out]
        dsts = refs[n_in + n_arr + n_out:n_in + 2 * n_arr + n_out]
        first_sem = len(refs) - 2 * len(rides)
        scratch, sems = refs[n_in + 2 * n_arr + n_out:first_sem], refs[first_sem:]

        def copies():
            made, at = [], 0
            for r, ((kind, _), n) in enumerate(zip(rides, counts)):
                made += _EXCHANGES[kind][1](srcs[at:at + n], dsts[at:at + n], sems[2 * r], sems[2 * r + 1])
                at += n
            return made

        @pl.when(is_first())
        def _():
            for cp in copies():
                cp.start()

        body(*ins, *outs, *scratch)

        @pl.when(is_last())
        def _():
            for cp in copies():
                cp.wait()

    return wrapped


_HBM = pl.BlockSpec(memory_space=pltpu.HBM)
_SEM = pl.BlockSpec(memory_space=pltpu.SEMAPHORE)
_FLOWS = pltpu.SideEffectType.DATAFLOW_SIDE_EFFECTING


def _start_exchanges(rides, name):
    arrays = [a for _, group in rides for a in group]
    shapes, sems = _ride_shapes(rides)
    n_arr, n_sem = len(arrays), len(sems)

    def body(*refs):
        srcs, lands = refs[:n_arr], refs[n_arr:2 * n_arr]
        sem_refs, token = refs[2 * n_arr:2 * n_arr + n_sem], refs[-1]
        at = 0
        for r, (kind, group) in enumerate(rides):
            n = len(group)
            for cp in _EXCHANGES[kind][1](srcs[at:at + n], lands[at:at + n], sem_refs[2 * r], sem_refs[2 * r + 1]):
                cp.start()
            at += n
        token[...] = jnp.zeros(token.shape, token.dtype)

    in_hbm = [pltpu.HBM(a.shape, a.dtype) for a in arrays]
    land_hbm = [pltpu.HBM(sh.shape, sh.dtype) for sh in shapes]
    res = pl.pallas_call(
        body, name=name,
        out_shape=(*sems, *in_hbm, *land_hbm, jax.ShapeDtypeStruct((SUBLANES, LANES), F32)),
        in_specs=[_HBM] * (2 * n_arr), out_specs=(*[_SEM] * n_sem, *[_HBM] * (2 * n_arr), _VMEM),
        input_output_aliases={i: n_sem + i for i in range(2 * n_arr)},
        compiler_params=pltpu.CompilerParams(has_side_effects=_FLOWS),
    )(*[pltpu.with_memory_space_constraint(a, pltpu.HBM) for a in arrays],
      *[pltpu.with_memory_space_constraint(lax.empty(sh.shape, sh.dtype), pltpu.HBM) for sh in shapes])
    return res[:n_sem], res[n_sem:n_sem + n_arr], res[n_sem + n_arr:n_sem + 2 * n_arr], res[-1]


def _wait_exchanges(kinds, sems, arrays, lands, after, name):
    n_arr, n_sem = len(arrays), len(sems)

    def body(*refs):
        srcs, dsts = refs[:n_arr], refs[n_arr:2 * n_arr]
        sem_refs = refs[2 * n_arr:2 * n_arr + n_sem]
        at = 0
        for r, (kind, n) in enumerate(kinds):
            for cp in _EXCHANGES[kind][1](srcs[at:at + n], dsts[at:at + n], sem_refs[2 * r], sem_refs[2 * r + 1]):
                cp.wait()
            at += n

    hbm = [pltpu.HBM(a.shape, a.dtype) for a in (*arrays, *lands)]
    return pl.pallas_call(
        body, name=name, out_shape=tuple(hbm),
        in_specs=[_HBM] * (2 * n_arr) + [_SEM] * n_sem + [_ANY], out_specs=tuple([_HBM] * (2 * n_arr)),
        input_output_aliases={i: i for i in range(2 * n_arr)},
        compiler_params=pltpu.CompilerParams(has_side_effects=_FLOWS),
    )(*arrays, *lands, *sems, after)[n_arr:]


def _chip_partial(pos, mine, theirs, relations, out_dtype, row_tile, name):
    n_slots, m, n = mine.shape
    q0 = relations[0]

    def chip_of(qi, pos_ref):
        q = qi + q0
        return pos_ref[0] ^ (q >> 1), pos_ref[1] ^ (q & 1)

    def mine_map(qi, t, pos_ref):
        px, py = chip_of(qi, pos_ref)
        return (4 * px + 2 * py + pos_ref[2] if n_slots == N_DEV else 2 * px + py), t, 0

    def theirs_map(qi, t, pos_ref):
        px, py = chip_of(qi, pos_ref)
        return 2 * px + py, t, 0

    def body(pos_ref, a_ref, b_ref, o_ref):
        o_ref[...] = (a_ref[...] + b_ref[...]).astype(out_dtype)

    return pl.pallas_call(
        body, name=name,
        out_shape=jax.ShapeDtypeStruct((len(relations), m, n), out_dtype),
        grid_spec=pltpu.PrefetchScalarGridSpec(
            num_scalar_prefetch=1, grid=(len(relations), m // row_tile),
            in_specs=[pl.BlockSpec((None, row_tile, n), mine_map), pl.BlockSpec((None, row_tile, n), theirs_map)],
            out_specs=pl.BlockSpec((None, row_tile, n), lambda qi, t, pos_ref: (qi, t, 0))),
        compiler_params=pltpu.CompilerParams(dimension_semantics=("arbitrary", "arbitrary")),
    )(pos, mine, theirs)


PARTS = ((0, 512), (512, 640))


def _gather_norm_proj(pos, x2d, small_shard, norm_g, w_in_shard, w_out_shards, n_chunk):
    seq = x2d.shape[0]
    n_tiles = seq // TILE + 1
    tp = n_tiles * TILE
    n_parts = len(PARTS)
    widest = max(width for _, width in PARTS)
    units = [(s, u) for s in range(2) for u in range(n_parts)]
    for first in (2, 5):
        units += [(first + j, u) for u in range(n_parts) for j in range(2)] + [(first + 2, u) for u in range(n_parts)]
    n_units = len(units)
    n_steps = n_tiles + n_units
    chunk = tp // n_chunk

    def body(pos_ref, x_ref, g_ref, small_ref, win_ref, wa_ref, wb_ref, wo_ref,
             ht_ref, proj_ref, meta_ref, small_all, win_all, wa_all, wb_all, wo_all,
             h_all, wbuf, rbuf, small_buf, send_sems, recv_sems, local_sems, small_send, small_recv):
        g = pl.program_id(0)
        x, y, c = lax.axis_index("x"), lax.axis_index("y"), lax.axis_index("c")
        me, sibling = (x, y, c), (x, y, 1 - c)
        chips = [(1 - x, y), (x, 1 - y), (1 - x, 1 - y)]
        shards = (win_ref, wa_ref, wb_ref, wo_ref)
        gathered = (win_all, wa_all, wb_all, wo_all)
        n_arrays = len(shards)
        blocks = [me, sibling] + [(*chip, c) for chip in chips] + [(*chip, 1 - c) for chip in chips]

        def index(block):
            px, py, pc = block
            return 4 * px + 2 * py + pc

        def part(ref, a, u):
            return ref.at[:, pl.ds(PARTS[u][0], PARTS[u][1])] if a == 0 else ref

        def slot(a, block, u):
            return part(gathered[a].at[index(block)], a, u)

        def sem(a, k, u):
            return n_parts * k + u if a == 0 else 7 * n_parts + 7 * (a - 1) + k

        def copy(a, k, block, to, u=0, from_shard=False):
            return pltpu.make_async_remote_copy(
                src_ref=part(shards[a], a, u) if from_shard else slot(a, block, u), dst_ref=slot(a, block, u),
                send_sem=send_sems.at[sem(a, k, u)], recv_sem=recv_sems.at[sem(a, k, u)],
                device_id=to, device_id_type=MESH)

        def keep(a):
            return pltpu.make_async_copy(shards[a], gathered[a].at[index(me)], local_sems.at[a])

        def load(m):
            s, u = units[m]
            src = part(win_ref, 0, u) if s == 0 else slot(0, blocks[s], u)
            return pltpu.make_async_copy(src, wbuf.at[m % 2, :, 0:PARTS[u][1]], local_sems.at[n_arrays + m % 2])

        def store(m):
            s, u = units[m]
            col0 = pl.multiple_of(index(blocks[s]) * COLS + PARTS[u][0], LANES)
            return pltpu.make_async_copy(rbuf.at[m % 2, :, 0:PARTS[u][1]],
                                         proj_ref.at[:, pl.ds(col0, PARTS[u][1])], local_sems.at[n_arrays + 2 + m % 2])

        def by_x(a, u):
            return u == 0 if a == 0 else a < 3

        def relay(a, u=0):
            src, to = (blocks[3], blocks[2]) if by_x(a, u) else (blocks[2], blocks[3])
            return copy(a, 3, src, to, u)

        def arrive(m):
            s, u = units[m]
            if s == 1:
                copy(0, 0, sibling, me, u).wait_recv()
            elif 2 <= s <= 4:
                copy(0, s - 1, blocks[s], me, u).wait_recv()
                copy(0, s + 2, blocks[s], sibling, u).start()
                if s < 4 and by_x(0, u) == (s == 3):
                    relay(0, u).start()
            elif s >= 5:
                copy(0, s - 1, blocks[s], me, u).wait_recv()
                if u == 0:
                    for a in range(1, 4):
                        copy(a, s - 4, blocks[s - 3], me).wait_recv()
                        copy(a, s - 1, blocks[s - 3], sibling).start()
                        if s < 7 and by_x(a, 0) == (s == 6):
                            relay(a).start()

        targets = [sibling, blocks[2], blocks[3]]

        def small_copies():
            return _all_copies([small_ref], [small_all], small_send, small_recv)

        @pl.when(g == 0)
        def _():
            for cp in small_copies():
                cp.start()
            for a in range(n_arrays):
                keep(a).start()
            for u in range(n_parts):
                for k, to in enumerate(targets):
                    copy(0, k, me, to, u, from_shard=True).start()
            for a in range(1, 4):
                for k, to in enumerate(targets):
                    copy(a, k, me, to, from_shard=True).start()
            load(0).start()

        @pl.when(g == n_tiles - 2)
        def _():
            for cp in small_copies():
                cp.wait()
            fetch = pltpu.make_async_copy(small_all, small_buf, local_sems.at[n_arrays + 4])
            fetch.start()
            fetch.wait()
            meta_ref[0:TILE - N_META, :] = jnp.zeros((TILE - N_META, D_MODEL), F32)
            meta_ref[TILE - N_META:TILE, :] = jnp.concatenate([small_buf[d, 0:N_META, :] for d in range(N_DEV)], axis=1)

        @pl.when(g < n_tiles)
        def _():
            s0 = jnp.where(g == n_tiles - 1, meta_ref[...], x_ref[...])
            r = lax.rsqrt(_rowmean(s0 * s0) + EPS)
            h32 = (s0 * r) * g_ref[...]
            ht_ref[...] = h32.T.astype(BF16)
            h_all[pl.ds(pl.multiple_of(g * TILE, TILE), TILE), :] = h32.astype(BF16)

        for m in range(n_units):
            @pl.when(g == n_tiles + m)
            def _(m=m):
                load(m).wait()
                if m + 1 < n_units:
                    arrive(m + 1)
                    load(m + 1).start()
                if m >= 2:
                    store(m - 2).wait()

        m_now = jnp.maximum(g - n_tiles, 0)
        u_now = functools.reduce(jnp.add, [jnp.where(m_now == m, u, 0) for m, (_, u) in enumerate(units)])
        for u, (_, width) in enumerate(PARTS):
            @pl.when((g >= n_tiles) & (u_now == u))
            def _(width=width):
                w = wbuf[m_now % 2, :, 0:width]
                for r in range(n_chunk):
                    rbuf[m_now % 2, r * chunk:(r + 1) * chunk, 0:width] = _dot(h_all[r * chunk:(r + 1) * chunk, :], w)

        for m in range(n_units):
            @pl.when(g == n_tiles + m)
            def _(m=m):
                store(m).start()

        @pl.when(g == n_steps - 1)
        def _():
            store(n_units - 2).wait()
            store(n_units - 1).wait()
            for a in range(1, 4):
                copy(a, 0, sibling, me).wait_recv()
                for j in range(3):
                    copy(a, 4 + j, blocks[5 + j], me).wait_recv()
            for a in range(n_arrays):
                for u in range(n_parts if a == 0 else 1):
                    for k, to in enumerate(targets):
                        copy(a, k, me, to, u, from_shard=True).wait_send()
                    relay(a, u).wait_send()
                    for j in range(3):
                        copy(a, 4 + j, blocks[2 + j], sibling, u).wait_send()
                keep(a).wait()

    n_x = n_tiles - 1
    return pl.pallas_call(
        body, name="gather_norm_proj",
        out_shape=[jax.ShapeDtypeStruct((D_MODEL, tp), BF16), jax.ShapeDtypeStruct((tp, D_IN), F32),
                   jax.ShapeDtypeStruct((TILE, D_MODEL), F32), jax.ShapeDtypeStruct((N_DEV,) + small_shard.shape[1:], F32),
                   jax.ShapeDtypeStruct((N_DEV,) + w_in_shard.shape, BF16)]
                  + [jax.ShapeDtypeStruct((N_DEV,) + w.shape, BF16) for w in w_out_shards],
        grid_spec=pltpu.PrefetchScalarGridSpec(
            num_scalar_prefetch=1, grid=(n_steps,),
            in_specs=[pl.BlockSpec((TILE, D_MODEL), lambda g, pos_ref: (jnp.minimum(g, n_x - 1), 0)),
                      _VMEM, _ANY, _ANY, _ANY, _ANY, _ANY],
            out_specs=[pl.BlockSpec((D_MODEL, TILE), lambda g, pos_ref: (0, jnp.minimum(g, n_tiles - 1))),
                       _ANY, _VMEM, _ANY, _ANY, _ANY, _ANY, _ANY],
            scratch_shapes=[pltpu.VMEM((tp, D_MODEL), BF16), pltpu.VMEM((2, D_MODEL, widest), BF16),
                            pltpu.VMEM((2, tp, widest), F32), pltpu.VMEM((N_DEV,) + small_shard.shape[1:], F32),
                            pltpu.SemaphoreType.DMA((7 * n_parts + 21,)), pltpu.SemaphoreType.DMA((7 * n_parts + 21,)),
                            pltpu.SemaphoreType.DMA((9,)),
                            pltpu.SemaphoreType.DMA((N_DEV,)), pltpu.SemaphoreType.DMA((N_DEV,))]),
        compiler_params=pltpu.CompilerParams(dimension_semantics=("arbitrary",), vmem_limit_bytes=VMEM_LIMIT),
    )(pos, x2d, norm_g, small_shard, w_in_shard, *w_out_shards)


C_AVAL, C_AGLU, C_AZ, C_BB, C_BC, C_BX, C_BZ, C_GA, C_GB = (k * D_MODEL for k in range(9))
S_AZ, S_BB, S_BZ, S_GA, S_GB = (k * D_MODEL for k in range(5))


def _fused_pass(proj, x2d, tgt2d, meta_tile, conv_a_w, conv_a_b, ln_a_g, ln_a_b, b_a_out, conv_b_w, final_g,
                w_a, w_b, w_o, w_a_t, w_b_t, w_o_t, n_tiles):
    T = TILE
    tp = n_tiles * T
    inv_d = 1.0 / D_MODEL

    def block_of(tile):
        return jnp.where(tile == 0, n_tiles - 1, tile - 1)

    def cur(i):
        return block_of(jnp.minimum(i, n_tiles - 1))

    def prev(i):
        return block_of(jnp.clip(i - 1, 0, n_tiles - 1))

    def xblk(i):
        return jnp.maximum(jnp.minimum(i, n_tiles - 1) - 1, 0)

    def body(proj_ref, aprev, cprev, x_ref, tgt_ref, meta_ref, caw_ref, cab_ref, lng_ref, lnb_ref, bao_ref, cbw_ref,
             fg_ref, wa_ref, wb_ref, wo_ref, wat_ref, wbt_ref, wot_ref,
             dproj_ref, ds1_ref, lhs_ref, rhs_ref, small_ref,
             ua0_buf, cb_buf, dua1_buf, dc3_buf, stage, ua1_buf, c3_buf,
             dpa_buf, dpb_buf, dcaw8, dcbw8, shift_buf):
        i = pl.program_id(0)
        this, before = i % 2, 1 - i % 2

        @pl.when(i == 0)
        def _init():
            for buf in (ua0_buf, cb_buf, dua1_buf, dc3_buf, dcaw8, dcbw8):
                buf[...] = jnp.zeros(buf.shape, buf.dtype)
            small_ref[...] = jnp.zeros(small_ref.shape, F32)

        @pl.when(i >= 1)
        def _emit_stage():
            dproj_ref[:, C_AZ:C_BC] = stage[:, S_AZ:S_BZ]
            dproj_ref[:, C_BZ:D_IN] = stage[:, S_BZ:S_GB + D_MODEL]

        @pl.when(i < n_tiles)
        def _front():
            def conv_chunk(cc, carry):
                c0 = pl.multiple_of(cc * LANES, LANES)
                lanes = pl.ds(c0, LANES)

                def col(base):
                    return pl.ds(pl.multiple_of(base + cc * LANES, LANES), LANES)

                ua0 = proj_ref[:, col(C_AVAL)] * _sigmoid(proj_ref[:, col(C_AGLU)])
                ua0_buf[this, 0:HALO, lanes] = ua0_buf[before, T:T + HALO, lanes]
                ua0_buf[this, HALO:HALO + T, lanes] = ua0
                acc = jnp.broadcast_to(cab_ref[:, lanes], (T, LANES))
                lead = HALO - (CONV_A - 1)
                for r in range(SUBLANES):
                    taps = [k for k in range(CONV_A) if (k + lead) % SUBLANES == r]
                    rows = T + SUBLANES * max((k + lead) // SUBLANES for k in taps)
                    if r:
                        shift_buf[r, 0:rows, :] = ua0_buf[this, pl.ds(r, rows), lanes]
                    for k in taps:
                        q = (k + lead) // SUBLANES
                        if r:
                            win = shift_buf[r, SUBLANES * q:SUBLANES * q + T, :]
                        else:
                            win = ua0_buf[this, pl.ds(SUBLANES * q, T), lanes]
                        acc = acc + caw_ref[k:k + 1, lanes] * win
                ua1_buf[:, lanes] = acc
                cb = proj_ref[:, col(C_BC)] * proj_ref[:, col(C_BX)]
                cb_buf[this, 0:SUBLANES, lanes] = cb_buf[before, T:T + SUBLANES, lanes]
                cb_buf[this, SUBLANES:SUBLANES + T, lanes] = cb
                lead_b = SUBLANES - (CONV_B - 1)
                acc3 = cbw_ref[0:1, lanes] * cb_buf[this, pl.ds(lead_b, T), lanes]
                for k in range(1, CONV_B):
                    acc3 = acc3 + cbw_ref[k:k + 1, lanes] * cb_buf[this, pl.ds(lead_b + k, T), lanes]
                c3_buf[:, lanes] = acc3
                return carry

            lax.fori_loop(0, N_CHUNK, conv_chunk, 0)

            ua1 = ua1_buf[...]
            xc = ua1 - _rowmean(ua1)
            rstd = lax.rsqrt(_rowmean(xc * xc) + EPS)
            xhat = xc * rstd
            ua2 = xhat * lng_ref[...] + lnb_ref[...]
            sg2 = _sigmoid(ua2)
            ua3 = ua2 * sg2
            a_z = proj_ref[:, C_AZ:C_AZ + D_MODEL]
            sz = _sigmoid(a_z)
            silu_az = a_z * sz
            lhs_ref[0] = (ua3 * silu_az).astype(BF16)
            b_z = proj_ref[:, C_BZ:C_BZ + D_MODEL]
            sbz = _sigmoid(b_z)
            silu_bz = b_z * sbz
            b_b = proj_ref[:, C_BB:C_BB + D_MODEL]
            c3 = c3_buf[...]
            ub = b_b * c3
            lhs_ref[1] = (ub * silu_bz).astype(BF16)

            ya = _dot(lhs_ref[0], wa_ref[...]) + bao_ref[...]
            yb = _dot(lhs_ref[1], wb_ref[...])
            sga = _sigmoid(proj_ref[:, C_GA:C_GA + D_MODEL])
            sgb = _sigmoid(proj_ref[:, C_GB:C_GB + D_MODEL])
            m_b = (sga * ya + sgb * yb).astype(BF16)
            lhs_ref[2] = m_b
            s0 = jnp.where(i == 0, meta_ref[...], x_ref[...])
            s1 = s0 + _dot(m_b, wo_ref[...])
            r1 = lax.rsqrt(_rowmean(s1 * s1) + EPS)
            y = (s1 * r1) * fg_ref[...]
            is_token = (i >= 1).astype(F32)
            err = (y - tgt_ref[...]) * is_token
            small_ref[ROW_LOSS:ROW_LOSS + 1, :] += (0.5 * inv_d) * _colsum(err * err)
            dy = err * inv_d
            small_ref[ROW_FINAL_G:ROW_FINAL_G + 1, :] += _colsum(dy * (s1 * r1))
            gy = dy * fg_ref[...]
            ds1 = r1 * gy - s1 * ((r1 * r1 * r1) * _rowmean(gy * s1))
            ds1_ref[...] = ds1
            ds1_b = ds1.astype(BF16)
            rhs_ref[2] = ds1_b
            dm = _dot(ds1_b, wot_ref[...])
            dya = dm * sga
            dyb = dm * sgb
            stage[:, S_GA:S_GA + D_MODEL] = (dya * ya * (1.0 - sga)).astype(BF16)
            stage[:, S_GB:S_GB + D_MODEL] = (dyb * yb * (1.0 - sgb)).astype(BF16)
            small_ref[ROW_B_A_OUT:ROW_B_A_OUT + 1, :] += _colsum(dya)
            dya_b = dya.astype(BF16)
            dyb_b = dyb.astype(BF16)
            rhs_ref[0] = dya_b
            rhs_ref[1] = dyb_b
            dpa_buf[...] = _dot(dya_b, wat_ref[...])
            dpb_buf[...] = _dot(dyb_b, wbt_ref[...])

            dpa = dpa_buf[...]
            stage[:, S_AZ:S_AZ + D_MODEL] = (dpa * ua3 * (sz + silu_az * (1.0 - sz))).astype(BF16)
            dua2 = dpa * silu_az * (sg2 + ua3 * (1.0 - sg2))
            small_ref[ROW_LN_G:ROW_LN_G + 1, :] += _colsum(dua2 * xhat)
            small_ref[ROW_LN_B:ROW_LN_B + 1, :] += _colsum(dua2)
            dxh = dua2 * lng_ref[...]
            dua1 = rstd * (dxh - _rowmean(dxh) - xhat * _rowmean(dxh * xhat))
            small_ref[ROW_CONV_A_B:ROW_CONV_A_B + 1, :] += _colsum(dua1)
            dua1_buf[this, 0:T, :] = dua1
            dua1_buf[before, T:T + HALO, :] = dua1[0:HALO]
            dpb = dpb_buf[...]
            stage[:, S_BZ:S_BZ + D_MODEL] = (dpb * ub * (sbz + silu_bz * (1.0 - sbz))).astype(BF16)
            dub = dpb * silu_bz
            stage[:, S_BB:S_BB + D_MODEL] = (dub * c3).astype(BF16)
            dc3 = dub * b_b
            dc3_buf[this, 0:T, :] = dc3
            dc3_buf[before, T:T + SUBLANES, :] = dc3[0:SUBLANES]

        @pl.when(i == n_tiles)
        def _no_later_tile():
            dua1_buf[before, T:T + HALO, :] = jnp.zeros((HALO, D_MODEL), F32)
            dc3_buf[before, T:T + SUBLANES, :] = jnp.zeros((SUBLANES, D_MODEL), F32)

        @pl.when(i >= 1)
        def _lagged():
            def convt_chunk(cc, carry):
                c0 = pl.multiple_of(cc * LANES, LANES)
                lanes = pl.ds(c0, LANES)

                def col(base):
                    return pl.ds(pl.multiple_of(base + cc * LANES, LANES), LANES)

                ua0 = ua0_buf[before, HALO:HALO + T, lanes]
                acc = jnp.zeros((T, LANES), F32)
                for r in range(SUBLANES):
                    shifts = [j for j in range(CONV_A) if j % SUBLANES == r]
                    rows = T + shifts[-1] - r
                    if r:
                        shift_buf[r, 0:rows, :] = dua1_buf[before, pl.ds(r, rows), lanes]
                    for j in shifts:
                        k = CONV_A - 1 - j
                        if r:
                            later = shift_buf[r, j - r:j - r + T, :]
                        else:
                            later = dua1_buf[before, pl.ds(j, T), lanes]
                        acc = acc + caw_ref[k:k + 1, lanes] * later
                        dcaw8[SUBLANES * k:SUBLANES * (k + 1), lanes] += _fold8(ua0 * later)
                a_val = aprev[:, col(0)]
                sg = _sigmoid(aprev[:, col(D_MODEL)])
                dproj_ref[:, col(C_AVAL)] = (acc * sg).astype(BF16)
                dproj_ref[:, col(C_AGLU)] = (acc * a_val * (sg * (1.0 - sg))).astype(BF16)

                cb = cb_buf[before, SUBLANES:SUBLANES + T, lanes]
                acc3 = jnp.zeros((T, LANES), F32)
                for j in range(CONV_B):
                    k = CONV_B - 1 - j
                    later = dc3_buf[before, pl.ds(j, T), lanes]
                    acc3 = acc3 + cbw_ref[k:k + 1, lanes] * later
                    dcbw8[SUBLANES * k:SUBLANES * (k + 1), lanes] += _fold8(cb * later)
                dproj_ref[:, col(C_BC)] = (acc3 * cprev[:, col(D_MODEL)]).astype(BF16)
                dproj_ref[:, col(C_BX)] = (acc3 * cprev[:, col(0)]).astype(BF16)
                return carry

            lax.fori_loop(0, N_CHUNK, convt_chunk, 0)

        @pl.when(i == n_tiles)
        def _finish():
            for k in range(CONV_A):
                small_ref[ROW_CONV_A_W + k:ROW_CONV_A_W + k + 1, :] = _colsum(dcaw8[SUBLANES * k:SUBLANES * (k + 1), :])
            for k in range(CONV_B):
                small_ref[ROW_CONV_B_W + k:ROW_CONV_B_W + k + 1, :] = _colsum(dcbw8[SUBLANES * k:SUBLANES * (k + 1), :])

    pair = 2 * D_MODEL
    return pl.pallas_call(
        body, name="fused_pass", grid=(n_tiles + 1,),
        out_shape=[
            jax.ShapeDtypeStruct((tp, D_IN), BF16),
            jax.ShapeDtypeStruct((tp, D_MODEL), F32),
            jax.ShapeDtypeStruct((3, tp, D_MODEL), BF16),
            jax.ShapeDtypeStruct((3, tp, D_MODEL), BF16),
            jax.ShapeDtypeStruct((SMALL_A_ROWS, D_MODEL), F32),
        ],
        in_specs=[
            pl.BlockSpec((T, D_IN), lambda i: (cur(i), 0)),
            pl.BlockSpec((T, pair), lambda i: (prev(i), C_AVAL // pair)),
            pl.BlockSpec((T, pair), lambda i: (prev(i), C_BC // pair)),
            pl.BlockSpec((T, D_MODEL), lambda i: (xblk(i), 0)),
            pl.BlockSpec((T, D_MODEL), lambda i: (xblk(i), 0)),
            _VMEM, _VMEM, _VMEM, _VMEM, _VMEM, _VMEM, _VMEM, _VMEM,
            *[_resident((D_MODEL, D_MODEL)) for _ in range(6)],
        ],
        out_specs=[
            pl.BlockSpec((T, D_IN), lambda i: (prev(i), 0)),
            pl.BlockSpec((T, D_MODEL), lambda i: (cur(i), 0)),
            pl.BlockSpec((3, T, D_MODEL), lambda i: (0, cur(i), 0)),
            pl.BlockSpec((3, T, D_MODEL), lambda i: (0, cur(i), 0)),
            _VMEM,
        ],
        scratch_shapes=[
            pltpu.VMEM((2, HALO + T, D_MODEL), F32),
            pltpu.VMEM((2, SUBLANES + T, D_MODEL), F32),
            pltpu.VMEM((2, T + HALO, D_MODEL), F32),
            pltpu.VMEM((2, T + SUBLANES, D_MODEL), F32),
            pltpu.VMEM((T, 5 * D_MODEL), BF16),
            pltpu.VMEM((T, D_MODEL), F32),
            pltpu.VMEM((T, D_MODEL), F32),
            pltpu.VMEM((T, D_MODEL), F32),
            pltpu.VMEM((T, D_MODEL), F32),
            pltpu.VMEM((32 * SUBLANES, D_MODEL), F32),
            pltpu.VMEM((SUBLANES * SUBLANES, D_MODEL), F32),
            pltpu.VMEM((SUBLANES, T + HALO, LANES), F32),
        ],
        compiler_params=pltpu.CompilerParams(dimension_semantics=("arbitrary",), vmem_limit_bytes=VMEM_LIMIT),
    )(proj, proj, proj, x2d, tgt2d, meta_tile, conv_a_w, conv_a_b, ln_a_g, ln_a_b, b_a_out, conv_b_w, final_g,
      w_a, w_b, w_o, w_a_t, w_b_t, w_o_t)


def _input_bwd(dproj, ds1, x2d, meta_tile, norm_g, w_in_all, row_tile):
    seq = x2d.shape[0]
    n_steps = seq // row_tile
    meta_block = seq // TILE

    def backward(dp_ref, ds1_ref, s0_ref, g_ref, w_ref, out_ref, vec_ref):
        dh = _dot_nt(dp_ref[:, 0:COLS], w_ref[0])
        for j in range(1, N_DEV):
            dh = dh + _dot_nt(dp_ref[:, j * COLS:(j + 1) * COLS], w_ref[j])
        s0v = s0_ref[...]
        r = lax.rsqrt(_rowmean(s0v * s0v) + EPS)
        gh = dh * g_ref[...]
        out_ref[...] = ds1_ref[...] + r * gh - s0v * ((r * r * r) * _rowmean(gh * s0v))
        vec_ref[ROW_NORM_G:ROW_NORM_G + 1, :] += _colsum(dh * (s0v * r))

    def body(dp_ref, ds1_ref, x_ref, dpm_ref, ds1m_ref, meta_ref, g_ref, w_ref, gx_ref, small_ref, gmeta_buf):
        t = pl.program_id(0)

        @pl.when(t == 0)
        def _():
            small_ref[...] = jnp.zeros(small_ref.shape, F32)

        backward(dp_ref, ds1_ref, x_ref, g_ref, w_ref, gx_ref, small_ref)

        @pl.when(t == n_steps - 1)
        def _():
            backward(dpm_ref, ds1m_ref, meta_ref, g_ref, w_ref, gmeta_buf, small_ref)
            small_ref[ROW_META:ROW_META + N_META, :] = gmeta_buf[TILE - N_META:TILE, :]

    return pl.pallas_call(
        body, name="input_bwd", grid=(n_steps,),
        out_shape=[jax.ShapeDtypeStruct(x2d.shape, F32), jax.ShapeDtypeStruct((SMALL_B_ROWS, D_MODEL), F32)],
        in_specs=[pl.BlockSpec((row_tile, D_IN), lambda t: (t, 0)),
                  pl.BlockSpec((row_tile, D_MODEL), lambda t: (t, 0)),
                  pl.BlockSpec((row_tile, D_MODEL), lambda t: (t, 0)),
                  pl.BlockSpec((TILE, D_IN), lambda t: (meta_block, 0)),
                  pl.BlockSpec((TILE, D_MODEL), lambda t: (meta_block, 0)),
                  _VMEM, _VMEM, _resident((N_DEV, D_MODEL, COLS))],
        out_specs=[pl.BlockSpec((row_tile, D_MODEL), lambda t: (t, 0)), _VMEM],
        scratch_shapes=[pltpu.VMEM((TILE, D_MODEL), F32)],
        compiler_params=pltpu.CompilerParams(dimension_semantics=("arbitrary",), vmem_limit_bytes=VMEM_LIMIT),
    )(dproj, ds1, x2d, dproj, ds1, meta_tile, norm_g, w_in_all)


def _grad_w_in_half(pos, h_t, dproj, k_tile, other_side, rides, name, after=None, add_to=None):
    tp = h_t.shape[1]
    n_k = tp // k_tile
    order = [] if after is None else [after]
    summing = add_to is not None

    def column_block(q, k, pos_ref):
        return k, 2 * q + (1 - pos_ref[2] if other_side else pos_ref[2])

    def body(pos_ref, h_ref, dp_ref, *refs):
        o_ref = refs[-2] if summing else refs[-1]

        @pl.when(pl.program_id(1) == 0)
        def _():
            o_ref[...] = refs[0][...] if summing else jnp.zeros(o_ref.shape, F32)

        o_ref[...] += _dot(h_ref[...], dp_ref[...])

        if summing:
            @pl.when(pl.program_id(1) == n_k - 1)
            def _():
                refs[-1][...] = o_ref[...].astype(BF16)

    ride = [a for _, arrays in rides for a in arrays]
    n_arr = len(ride)
    ride_shapes, ride_sems = _ride_shapes(rides)
    block = (None, D_MODEL, COLS)
    extra_in = [add_to] if summing else []
    extra_in_specs = [pl.BlockSpec(block, lambda q, k, pos_ref: (q, 0, 0))] if summing else []
    extra_out = [jax.ShapeDtypeStruct((4, D_MODEL, COLS), BF16)] if summing else []
    extra_out_specs = [pl.BlockSpec(block, lambda q, k, pos_ref: (q ^ (2 * pos_ref[0] + pos_ref[1]), 0, 0))] \
        if summing else []
    body = _riding(body, 3 + len(extra_in) + len(order), 1 + len(extra_out), rides,
                   lambda: (pl.program_id(0) == 0) & (pl.program_id(1) == 0),
                   lambda: (pl.program_id(0) == 3) & (pl.program_id(1) == n_k - 1))
    return pl.pallas_call(
        body, name=name,
        out_shape=[jax.ShapeDtypeStruct((4, D_MODEL, COLS), F32)] + extra_out + ride_shapes,
        grid_spec=pltpu.PrefetchScalarGridSpec(
            num_scalar_prefetch=1, grid=(4, n_k),
            in_specs=[pl.BlockSpec((D_MODEL, k_tile), lambda q, k, pos_ref: (0, k)),
                      pl.BlockSpec((k_tile, COLS), column_block)] + extra_in_specs + [_ANY] * (len(order) + n_arr),
            out_specs=[pl.BlockSpec(block, lambda q, k, pos_ref: (q, 0, 0))] + extra_out_specs + [_ANY] * n_arr,
            scratch_shapes=ride_sems),
        compiler_params=pltpu.CompilerParams(dimension_semantics=("arbitrary", "arbitrary"),
                                             vmem_limit_bytes=VMEM_LIMIT),
    )(pos, h_t, dproj, *extra_in, *order, *ride)


def _grad_w_out(lhs, rhs, k_tile, after):
    tp = lhs.shape[1]

    def body(a_ref, b_ref, after_ref, o_ref):
        @pl.when(pl.program_id(1) == 0)
        def _():
            o_ref[...] = jnp.zeros(o_ref.shape, F32)

        o_ref[...] += _dot_tn(a_ref[...], b_ref[...]).reshape(N_DEV, ROWS_OUT, D_MODEL)

    return pl.pallas_call(
        body, name="grad_w_out", grid=(3, tp // k_tile),
        out_shape=jax.ShapeDtypeStruct((N_DEV, 3, ROWS_OUT, D_MODEL), F32),
        in_specs=[pl.BlockSpec((None, k_tile, D_MODEL), lambda w, k: (w, k, 0)),
                  pl.BlockSpec((None, k_tile, D_MODEL), lambda w, k: (w, k, 0)), _ANY],
        out_specs=pl.BlockSpec((N_DEV, None, ROWS_OUT, D_MODEL), lambda w, k: (0, w, 0, 0)),
        compiler_params=pltpu.CompilerParams(dimension_semantics=("arbitrary", "arbitrary"),
                                             vmem_limit_bytes=VMEM_LIMIT),
    )(lhs, rhs, after)


def _adamw_math(w, g, m, v):
    m = ADAM_B1 * m + (1.0 - ADAM_B1) * g
    v = ADAM_B2 * v + (1.0 - ADAM_B2) * (g * g)
    m_hat = m / (1.0 - ADAM_B1 ** ADAM_STEP)
    v_hat = v / (1.0 - ADAM_B2 ** ADAM_STEP)
    delta = -ADAM_LR * (m_hat / (jnp.sqrt(v_hat) + ADAM_EPS) + ADAM_WD * w)
    return delta, m, v


def _adamw_sharded(pos, mine, theirs, landed, weights, row_tile, name, after=None):
    order = [] if after is None else [after]
    rows, n = weights[0][0].shape
    n_slots = mine.shape[0]
    per_shard = rows // row_tile
    assert per_shard == 1 or len(weights) == 1

    def mine_map(j, t, pos_ref):
        chip = 2 * pos_ref[0] + pos_ref[1]
        return (2 * chip + pos_ref[2] if n_slots == N_DEV else chip), j * per_shard + t, 0

    def theirs_map(j, t, pos_ref):
        return 2 * pos_ref[0] + pos_ref[1], j * per_shard + t, 0

    def body(pos_ref, mine_ref, *refs):
        if theirs is not None:
            g = mine_ref[...] + refs[0][...]
            refs = refs[1:]
        else:
            g = mine_ref[...]
        land_ref, refs = refs[0], refs[1:]
        ins, outs = refs[:3 * len(weights)], refs[3 * len(weights) + len(order):]
        for k in range(3):
            g = g + land_ref[k].astype(F32)
        for j in range(len(weights)):
            @pl.when(pl.program_id(0) == j)
            def _(j=j):
                w_ref, m_ref, v_ref = ins[3 * j:3 * j + 3]
                delta, m_new, v_new = _adamw_math(w_ref[...], g, m_ref[...], v_ref[...])
                for ref, val in zip(outs[4 * j:4 * j + 4], (g, delta, m_new, v_new)):
                    ref[...] = val

    tile = pl.BlockSpec((row_tile, n), lambda j, t, pos_ref: (t, 0))
    res = pl.pallas_call(
        body, name=name,
        out_shape=[jax.ShapeDtypeStruct((rows, n), F32)] * (4 * len(weights)),
        grid_spec=pltpu.PrefetchScalarGridSpec(
            num_scalar_prefetch=1, grid=(len(weights), per_shard),
            in_specs=[pl.BlockSpec((None, row_tile, n), mine_map)]
            + ([pl.BlockSpec((None, row_tile, n), theirs_map)] if theirs is not None else [])
            + [pl.BlockSpec((3, row_tile, n), lambda j, t, pos_ref: (0, j * per_shard + t, 0))]
            + [tile] * (3 * len(weights)) + [_ANY] * len(order),
            out_specs=[tile] * (4 * len(weights))),
        compiler_params=pltpu.CompilerParams(dimension_semantics=("arbitrary", "arbitrary")),
    )(pos, mine, *([theirs] if theirs is not None else []), landed, *[a for wmv in weights for a in wmv], *order)
    return [res[4 * j:4 * j + 4] for j in range(len(weights))]


def _adamw_small(gathered, gathered_cols, params):
    n_par, n_src = len(params), len(gathered)

    def body(*refs):
        g_refs, gc_refs = refs[:n_src], refs[n_src:2 * n_src]
        ins = refs[2 * n_src:2 * n_src + 3 * n_par]
        outs = refs[2 * n_src + 3 * n_par:]
        loss_ref = outs[4 * n_par]

        def reduced(ref, row, n_rows):
            g = ref[0, row:row + n_rows, :]
            for d in range(1, N_DEV):
                g = g + ref[d, row:row + n_rows, :]
            return g

        for p, (src, row, n_rows, sharded, _, _, _) in enumerate(params):
            g = reduced((gc_refs if sharded else g_refs)[src], row, n_rows)
            w_ref, m_ref, v_ref = ins[3 * p:3 * p + 3]
            delta, m_new, v_new = _adamw_math(w_ref[...], g, m_ref[...], v_ref[...])
            outs[4 * p][...] = g
            outs[4 * p + 1][...] = delta
            outs[4 * p + 2][...] = m_new
            outs[4 * p + 3][...] = v_new
        loss = jnp.sum(reduced(g_refs[0], ROW_LOSS, 1), axis=1, keepdims=True)
        loss_ref[...] = jnp.broadcast_to(loss, loss_ref.shape)

    out_shape = []
    for (_, _, _, _, w, _, _) in params:
        out_shape += [jax.ShapeDtypeStruct(w.shape, F32)] * 4
    out_shape.append(jax.ShapeDtypeStruct((1, LANES), F32))
    flat = [a for (_, _, _, _, w, m, v) in params for a in (w, m, v)]
    return pl.pallas_call(
        body, name="adamw_small", out_shape=out_shape,
        in_specs=[_VMEM] * (2 * n_src + len(flat)), out_specs=[_VMEM] * len(out_shape),
    )(*gathered, *gathered_cols, *flat)


def _pad_rows(a, rows):
    return jnp.concatenate([a, jnp.zeros((rows - a.shape[0], a.shape[1]), a.dtype)], axis=0)


def kernel(x, meta_tokens, norm_g, w_in, conv_a_w, conv_a_b, ln_a_g, ln_a_b, w_a_out, b_a_out, conv_b_w, w_b_out, w_out, final_g, loss_target, m_meta_tokens, m_norm_g, m_w_in, m_conv_a_w, m_conv_a_b, m_ln_a_g, m_ln_a_b, m_w_a_out, m_b_a_out, m_conv_b_w, m_w_b_out, m_w_out, m_final_g, v_meta_tokens, v_norm_g, v_w_in, v_conv_a_w, v_conv_a_b, v_ln_a_g, v_ln_a_b, v_w_a_out, v_b_a_out, v_conv_b_w, v_w_b_out, v_w_out, v_final_g):
    seq = x.shape[1]
    assert x.shape == (1, seq, D_MODEL) and seq % TILE == 0 and w_in.shape == (1, D_MODEL, COLS)
    n_tiles = seq // TILE + 1
    tp = n_tiles * TILE
    pos = jnp.stack([lax.axis_index("x"), lax.axis_index("y"), lax.axis_index("c")]).astype(jnp.int32)
    me = 4 * pos[0] + 2 * pos[1] + pos[2]
    x2d = x[0]
    tgt2d = loss_target[0]

    small = jnp.concatenate([meta_tokens, _pad_rows(conv_a_w[0], 32), _pad_rows(conv_b_w[0], SUBLANES)], axis=0)
    final_g2 = final_g.reshape(1, D_MODEL)

    w_out_shards = [w[0].astype(BF16) for w in (w_a_out, w_b_out, w_out)]
    h_t, proj, meta_tile, small_params, w_in_all, *w_out_all = _gather_norm_proj(
        pos, x2d, small[None], norm_g, w_in[0].astype(BF16), w_out_shards, 3)
    small_params = small_params.transpose(1, 0, 2).reshape(small.shape[0], D_MODEL)
    conv_a_full, conv_b_full = small_params[N_META:N_META + 32], small_params[N_META + 32:]
    w_out_all = [w.reshape(D_MODEL, D_MODEL) for w in w_out_all]
    w_out_all_t = [w.T for w in w_out_all]
    dproj, ds1, lhs, rhs, small_a = _fused_pass(
        proj, x2d, tgt2d, meta_tile, conv_a_full, conv_a_b, ln_a_g, ln_a_b, b_a_out, conv_b_full, final_g2,
        w_out_all[0], w_out_all[1], w_out_all[2], w_out_all_t[0], w_out_all_t[1], w_out_all_t[2], n_tiles)
    k_tile = tp // 3
    gw_far, small_a_all = _grad_w_in_half(pos, h_t, dproj, k_tile, True, [("all", (small_a[None],))], "grad_w_in_far")
    sems, sent, landing, token = _start_exchanges([("sibling_half", (gw_far,))], "rs_far_start")
    gw_out = _grad_w_out(lhs, rhs, k_tile, token).reshape(N_DEV, 3 * ROWS_OUT, D_MODEL)
    (their_in,) = _wait_exchanges([("sibling_half", 1)], sems, sent, landing, gw_out, "rs_far_wait")
    sems_o, sent_o, landing_o, token = _start_exchanges([("sibling", (gw_out,))], "rs_out_start")
    gw_near, parts_in = _grad_w_in_half(pos, h_t, dproj, k_tile, False, [], "grad_w_in_near", after=token,
                                        add_to=their_in)
    sems_i, sent_i, landing_i, token = _start_exchanges([("chips_by_relation", (parts_in,))], "rs_chips_in_start")
    (their_out,) = _wait_exchanges([("sibling", 1)], sems_o, sent_o, landing_o, token, "rs_out_wait")
    parts_out = _chip_partial(pos, gw_out, their_out, (1, 2, 3), BF16, ROWS_OUT, "rs_parts_w_out")
    sems_o, sent_o, landing_o, token = _start_exchanges([("chips", (parts_out,))], "rs_chips_out_start")
    grad_x, small_b = _input_bwd(dproj, ds1, x2d, meta_tile, norm_g + token[0, 0], w_in_all, min(512, seq))

    sems_s, sent_s, landing_s, token = _start_exchanges([("all", (small_b[None],))], "gather_small_grads_start")
    (land_in,) = _wait_exchanges([("chips_by_relation", 1)], sems_i, sent_i, landing_i, token, "rs_chips_in_wait")
    (res_in,) = _adamw_sharded(pos, gw_near, None, land_in, [(w_in[0], m_w_in[0], v_w_in[0])], 128, "adamw_w_in")
    (land_out,) = _wait_exchanges([("chips", 1)], sems_o, sent_o, landing_o, res_in[0], "rs_chips_out_wait")
    res_out = _adamw_sharded(
        pos, gw_out, their_out, land_out,
        [(w_a_out[0], m_w_a_out[0], v_w_a_out[0]), (w_b_out[0], m_w_b_out[0], v_w_b_out[0]),
         (w_out[0], m_w_out[0], v_w_out[0])], ROWS_OUT, "adamw_w_out")
    (small_b_all,) = _wait_exchanges([("all", 1)], sems_s, sent_s, landing_s, res_out[2][0], "gather_small_grads_wait")
    small_grads = [small_a_all, small_b_all]
    small_cols = [lax.dynamic_slice_in_dim(g, me * LANES, LANES, axis=2) for g in small_grads]
    params = [
        (1, ROW_META, N_META, True, meta_tokens, m_meta_tokens, v_meta_tokens),
        (1, ROW_NORM_G, 1, False, norm_g, m_norm_g, v_norm_g),
        (0, ROW_CONV_A_W, CONV_A, True, conv_a_w[0], m_conv_a_w[0], v_conv_a_w[0]),
        (0, ROW_CONV_A_B, 1, False, conv_a_b, m_conv_a_b, v_conv_a_b),
        (0, ROW_LN_G, 1, False, ln_a_g, m_ln_a_g, v_ln_a_g),
        (0, ROW_LN_B, 1, False, ln_a_b, m_ln_a_b, v_ln_a_b),
        (0, ROW_B_A_OUT, 1, False, b_a_out, m_b_a_out, v_b_a_out),
        (0, ROW_CONV_B_W, CONV_B, True, conv_b_w[0], m_conv_b_w[0], v_conv_b_w[0]),
        (0, ROW_FINAL_G, 1, False, final_g2, m_final_g.reshape(1, D_MODEL), v_final_g.reshape(1, D_MODEL)),
    ]
    res_small = _adamw_small(small_grads, small_cols, params)
    loss = res_small[-1][0, 0]

    def small_res(p, kind, shape):
        return res_small[4 * p + kind].reshape(shape)

    per_weight = []
    for kind in range(4):
        per_weight.append([
            small_res(0, kind, meta_tokens.shape),
            small_res(1, kind, norm_g.shape),
            res_in[kind].reshape(w_in.shape),
            small_res(2, kind, conv_a_w.shape),
            small_res(3, kind, conv_a_b.shape),
            small_res(4, kind, ln_a_g.shape),
            small_res(5, kind, ln_a_b.shape),
            res_out[0][kind].reshape(w_a_out.shape),
            small_res(6, kind, b_a_out.shape),
            small_res(7, kind, conv_b_w.shape),
            res_out[1][kind].reshape(w_b_out.shape),
            res_out[2][kind].reshape(w_out.shape),
            small_res(8, kind, final_g.shape),
        ])
    return (loss, grad_x.reshape(x.shape), *per_weight[0], *per_weight[1], *per_weight[2], *per_weight[3])
```

```python
import functools

import jax
import jax.numpy as jnp
from jax import lax
from jax.experimental import pallas as pl
from jax.experimental.pallas import tpu as pltpu

D_MODEL = 1024
N_META = 16
N_DEV = 8
D_IN = 9 * D_MODEL
COLS = D_IN // N_DEV
ROWS_OUT = D_MODEL // N_DEV
CONV_A = 31
CONV_B = 3
EPS = 1e-6

ADAM_LR = 0.001
ADAM_B1 = 0.9
ADAM_B2 = 0.999
ADAM_EPS = 1e-08
ADAM_WD = 0.01
ADAM_STEP = 10

TILE = 128
LANES = 128
N_CHUNK = D_MODEL // LANES
HALO = 32
SUBLANES = 8
VMEM_LIMIT = 56 * 1024 * 1024

ROW_FINAL_G, ROW_B_A_OUT, ROW_LN_G, ROW_LN_B, ROW_CONV_A_B, ROW_LOSS = 0, 1, 2, 3, 4, 5
ROW_CONV_A_W, ROW_CONV_B_W, SMALL_A_ROWS = 8, 40, 48
ROW_NORM_G, ROW_META, SMALL_B_ROWS = 0, 8, 24

MESH = pl.DeviceIdType.MESH
_ANY = pl.BlockSpec(memory_space=pl.ANY)
_VMEM = pl.BlockSpec(memory_space=pltpu.VMEM)
BF16 = jnp.bfloat16
F32 = jnp.float32


def _resident(shape):
    return pl.BlockSpec(shape, lambda *_: (0,) * len(shape), pipeline_mode=pl.Buffered(1))


def _sigmoid(v):
    return jax.nn.sigmoid(v)


def _dot(a, b):
    return jnp.dot(a, b, preferred_element_type=F32)


def _dot_nt(a, b):
    return lax.dot_general(a, b, (((1,), (1,)), ((), ())), preferred_element_type=F32)


def _dot_tn(a, b):
    return lax.dot_general(a, b, (((0,), (0,)), ((), ())), preferred_element_type=F32)


def _colsum(v):
    return jnp.sum(v, axis=0, keepdims=True)


def _rowmean(v):
    parts = [v[:, LANES * c:LANES * (c + 1)] for c in range(v.shape[1] // LANES)]
    return jnp.sum(functools.reduce(jnp.add, parts), axis=-1, keepdims=True) * (1.0 / v.shape[1])


def _fold8(v):
    parts = [v[SUBLANES * g:SUBLANES * (g + 1)] for g in range(v.shape[0] // SUBLANES)]
    return functools.reduce(jnp.add, parts)


def _sibling_copies(srcs, dsts, send_sems, recv_sems):
    x, y, c = lax.axis_index("x"), lax.axis_index("y"), lax.axis_index("c")
    return [pltpu.make_async_remote_copy(
        src_ref=src.at[2 * q + (1 - c)], dst_ref=dst.at[q],
        send_sem=send_sems.at[4 * a + q], recv_sem=recv_sems.at[4 * a + q],
        device_id=(x, y, 1 - c), device_id_type=MESH)
        for a, (src, dst) in enumerate(zip(srcs, dsts)) for q in range(4)]


def _chip_copies(srcs, dsts, send_sems, recv_sems):
    x, y, c = lax.axis_index("x"), lax.axis_index("y"), lax.axis_index("c")
    targets = [(x, 1 - y, c), (1 - x, y, c), (1 - x, 1 - y, c)]
    return [pltpu.make_async_remote_copy(
        src_ref=src.at[k], dst_ref=dst.at[k],
        send_sem=send_sems.at[3 * a + k], recv_sem=recv_sems.at[3 * a + k],
        device_id=targets[k], device_id_type=MESH)
        for a, (src, dst) in enumerate(zip(srcs, dsts)) for k in range(3)]


def _sibling_half_copies(srcs, dsts, send_sems, recv_sems):
    x, y, c = lax.axis_index("x"), lax.axis_index("y"), lax.axis_index("c")
    return [pltpu.make_async_remote_copy(
        src_ref=src.at[q], dst_ref=dst.at[q],
        send_sem=send_sems.at[4 * a + q], recv_sem=recv_sems.at[4 * a + q],
        device_id=(x, y, 1 - c), device_id_type=MESH)
        for a, (src, dst) in enumerate(zip(srcs, dsts)) for q in range(4)]


def _all_copies(srcs, dsts, send_sems, recv_sems):
    x, y, c = lax.axis_index("x"), lax.axis_index("y"), lax.axis_index("c")
    mine = 4 * x + 2 * y + c
    copies = []
    for a, (src, dst) in enumerate(zip(srcs, dsts)):
        copies.append(pltpu.make_async_copy(src.at[0], dst.at[mine], send_sems.at[N_DEV * a]))
        for k in range(1, N_DEV):
            copies.append(pltpu.make_async_remote_copy(
                src_ref=src.at[0], dst_ref=dst.at[mine],
                send_sem=send_sems.at[N_DEV * a + k], recv_sem=recv_sems.at[N_DEV * a + k],
                device_id=(x ^ (k >> 2), y ^ ((k >> 1) & 1), c ^ (k & 1)), device_id_type=MESH))
    return copies


def _chip_copies_by_relation(srcs, dsts, send_sems, recv_sems):
    return _chip_copies([src.at[pl.ds(1, 3)] for src in srcs], dsts, send_sems, recv_sems)


_EXCHANGES = {"sibling": (4, _sibling_copies, 4), "sibling_half": (4, _sibling_half_copies, 4),
              "chips": (3, _chip_copies, 3), "chips_by_relation": (3, _chip_copies_by_relation, 3),
              "all": (N_DEV, _all_copies, N_DEV)}


def _exchange_shapes(kind, arrays):
    per_array, _, slots = _EXCHANGES[kind]
    out_shape = [jax.ShapeDtypeStruct((slots,) + a.shape[1:], a.dtype) for a in arrays]
    sems = [pltpu.SemaphoreType.DMA((per_array * len(arrays),))] * 2
    return out_shape, sems


def _ride_shapes(rides):
    shapes, sems = [], []
    for kind, arrays in rides:
        ride_shapes, ride_sems = _exchange_shapes(kind, arrays)
        shapes += ride_shapes
        sems += ride_sems
    return shapes, sems


def _riding(body, n_in, n_out, rides, is_first, is_last):
    counts = [len(arrays) for _, arrays in rides]
    n_arr = sum(counts)

    def wrapped(*refs):
        ins, srcs = refs[:n_in], refs[n_in:n_in + n_arr]
        outs = refs[n_in + n_arr:n_in + n_arr + n_out]
        dsts = refs[n_in + n_arr + n_out:n_in + 2 * n_arr + n_out]
        first_sem = len(refs) - 2 * len(rides)
        scratch, sems = refs[n_in + 2 * n_arr + n_out:first_sem], refs[first_sem:]

        def copies():
            made, at = [], 0
            for r, ((kind, _), n) in enumerate(zip(rides, counts)):
                made += _EXCHANGES[kind][1](srcs[at:at + n], dsts[at:at + n], sems[2 * r], sems[2 * r + 1])
                at += n
            return made

        @pl.when(is_first())
        def _():
            for cp in copies():
                cp.start()

        body(*ins, *outs, *scratch)

        @pl.when(is_last())
        def _():
            for cp in copies():
                cp.wait()

    return wrapped


_HBM = pl.BlockSpec(memory_space=pltpu.HBM)
_SEM = pl.BlockSpec(memory_space=pltpu.SEMAPHORE)
_FLOWS = pltpu.SideEffectType.DATAFLOW_SIDE_EFFECTING


def _start_exchanges(rides, name):
    arrays = [a for _, group in rides for a in group]
    shapes, sems = _ride_shapes(rides)
    n_arr, n_sem = len(arrays), len(sems)

    def body(*refs):
        srcs, lands = refs[:n_arr], refs[n_arr:2 * n_arr]
        sem_refs, token = refs[2 * n_arr:2 * n_arr + n_sem], refs[-1]
        at = 0
        for r, (kind, group) in enumerate(rides):
            n = len(group)
            for cp in _EXCHANGES[kind][1](srcs[at:at + n], lands[at:at + n], sem_refs[2 * r], sem_refs[2 * r + 1]):
                cp.start()
            at += n
        token[...] = jnp.zeros(token.shape, token.dtype)

    in_hbm = [pltpu.HBM(a.shape, a.dtype) for a in arrays]
    land_hbm = [pltpu.HBM(sh.shape, sh.dtype) for sh in shapes]
    res = pl.pallas_call(
        body, name=name,
        out_shape=(*sems, *in_hbm, *land_hbm, jax.ShapeDtypeStruct((SUBLANES, LANES), F32)),
        in_specs=[_HBM] * (2 * n_arr), out_specs=(*[_SEM] * n_sem, *[_HBM] * (2 * n_arr), _VMEM),
        input_output_aliases={i: n_sem + i for i in range(2 * n_arr)},
        compiler_params=pltpu.CompilerParams(has_side_effects=_FLOWS),
    )(*[pltpu.with_memory_space_constraint(a, pltpu.HBM) for a in arrays],
      *[pltpu.with_memory_space_constraint(lax.empty(sh.shape, sh.dtype), pltpu.HBM) for sh in shapes])
    return res[:n_sem], res[n_sem:n_sem + n_arr], res[n_sem + n_arr:n_sem + 2 * n_arr], res[-1]


def _wait_exchanges(kinds, sems, arrays, lands, after, name, keep_sources=False):
    n_arr, n_sem = len(arrays), len(sems)

    def body(*refs):
        srcs, dsts = refs[:n_arr], refs[n_arr:2 * n_arr]
        sem_refs = refs[2 * n_arr:2 * n_arr + n_sem]
        at = 0
        for r, (kind, n) in enumerate(kinds):
            for cp in _EXCHANGES[kind][1](srcs[at:at + n], dsts[at:at + n], sem_refs[2 * r], sem_refs[2 * r + 1]):
                cp.wait()
            at += n

    hbm = [pltpu.HBM(a.shape, a.dtype) for a in (*arrays, *lands)]
    return pl.pallas_call(
        body, name=name, out_shape=tuple(hbm),
        in_specs=[_HBM] * (2 * n_arr) + [_SEM] * n_sem + [_ANY], out_specs=tuple([_HBM] * (2 * n_arr)),
        input_output_aliases={i: i for i in range(2 * n_arr)},
        compiler_params=pltpu.CompilerParams(has_side_effects=_FLOWS),
    )(*arrays, *lands, *sems, after)[0 if keep_sources else n_arr:]


def _chip_partial(pos, mine, theirs, relations, out_dtype, row_tile, name):
    n_slots, m, n = mine.shape
    q0 = relations[0]

    def chip_of(qi, pos_ref):
        q = qi + q0
        return pos_ref[0] ^ (q >> 1), pos_ref[1] ^ (q & 1)

    def mine_map(qi, t, pos_ref):
        px, py = chip_of(qi, pos_ref)
        return (4 * px + 2 * py + pos_ref[2] if n_slots == N_DEV else 2 * px + py), t, 0

    def theirs_map(qi, t, pos_ref):
        px, py = chip_of(qi, pos_ref)
        return 2 * px + py, t, 0

    def body(pos_ref, a_ref, b_ref, o_ref):
        o_ref[...] = (a_ref[...] + b_ref[...]).astype(out_dtype)

    return pl.pallas_call(
        body, name=name,
        out_shape=jax.ShapeDtypeStruct((len(relations), m, n), out_dtype),
        grid_spec=pltpu.PrefetchScalarGridSpec(
            num_scalar_prefetch=1, grid=(len(relations), m // row_tile),
            in_specs=[pl.BlockSpec((None, row_tile, n), mine_map), pl.BlockSpec((None, row_tile, n), theirs_map)],
            out_specs=pl.BlockSpec((None, row_tile, n), lambda qi, t, pos_ref: (qi, t, 0))),
        compiler_params=pltpu.CompilerParams(dimension_semantics=("arbitrary", "arbitrary")),
    )(pos, mine, theirs)


PARTS = ((0, 512), (512, 640))


def _gather_norm_proj(pos, x2d, small_shard, norm_g, w_in_shard, w_out_shards, n_chunk):
    seq = x2d.shape[0]
    n_tiles = seq // TILE + 1
    tp = n_tiles * TILE
    n_parts = len(PARTS)
    widest = max(width for _, width in PARTS)
    units = [(s, u) for s in range(2) for u in range(n_parts)]
    for first in (2, 5):
        units += [(first + j, u) for u in range(n_parts) for j in range(2)] + [(first + 2, u) for u in range(n_parts)]
    n_units = len(units)
    n_steps = n_tiles + n_units
    chunk = tp // n_chunk

    def body(pos_ref, x_ref, g_ref, small_ref, win_ref, wa_ref, wb_ref, wo_ref,
             ht_ref, proj_ref, meta_ref, small_all, win_all, wa_all, wb_all, wo_all,
             h_all, wbuf, rbuf, small_buf, send_sems, recv_sems, local_sems, small_send, small_recv):
        g = pl.program_id(0)
        x, y, c = lax.axis_index("x"), lax.axis_index("y"), lax.axis_index("c")
        me, sibling = (x, y, c), (x, y, 1 - c)
        chips = [(1 - x, y), (x, 1 - y), (1 - x, 1 - y)]
        shards = (win_ref, wa_ref, wb_ref, wo_ref)
        gathered = (win_all, wa_all, wb_all, wo_all)
        n_arrays = len(shards)
        blocks = [me, sibling] + [(*chip, c) for chip in chips] + [(*chip, 1 - c) for chip in chips]

        def index(block):
            px, py, pc = block
            return 4 * px + 2 * py + pc

        def part(ref, a, u):
            return ref.at[:, pl.ds(PARTS[u][0], PARTS[u][1])] if a == 0 else ref

        def slot(a, block, u):
            return part(gathered[a].at[index(block)], a, u)

        def sem(a, k, u):
            return n_parts * k + u if a == 0 else 7 * n_parts + 7 * (a - 1) + k

        def copy(a, k, block, to, u=0, from_shard=False):
            return pltpu.make_async_remote_copy(
                src_ref=part(shards[a], a, u) if from_shard else slot(a, block, u), dst_ref=slot(a, block, u),
                send_sem=send_sems.at[sem(a, k, u)], recv_sem=recv_sems.at[sem(a, k, u)],
                device_id=to, device_id_type=MESH)

        def keep(a):
            return pltpu.make_async_copy(shards[a], gathered[a].at[index(me)], local_sems.at[a])

        def load(m):
            s, u = units[m]
            src = part(win_ref, 0, u) if s == 0 else slot(0, blocks[s], u)
            return pltpu.make_async_copy(src, wbuf.at[m % 2, :, 0:PARTS[u][1]], local_sems.at[n_arrays + m % 2])

        def store(m):
            s, u = units[m]
            col0 = pl.multiple_of(index(blocks[s]) * COLS + PARTS[u][0], LANES)
            return pltpu.make_async_copy(rbuf.at[m % 2, :, 0:PARTS[u][1]],
                                         proj_ref.at[:, pl.ds(col0, PARTS[u][1])], local_sems.at[n_arrays + 2 + m % 2])

        def by_x(a, u):
            return u == 0 if a == 0 else a < 3

        def relay(a, u=0):
            src, to = (blocks[3], blocks[2]) if by_x(a, u) else (blocks[2], blocks[3])
            return copy(a, 3, src, to, u)

        def arrive(m):
            s, u = units[m]
            if s == 1:
                copy(0, 0, sibling, me, u).wait_recv()
            elif 2 <= s <= 4:
                copy(0, s - 1, blocks[s], me, u).wait_recv()
                copy(0, s + 2, blocks[s], sibling, u).start()
                if s < 4 and by_x(0, u) == (s == 3):
                    relay(0, u).start()
            elif s >= 5:
                copy(0, s - 1, blocks[s], me, u).wait_recv()
                if u == 0:
                    for a in range(1, 4):
                        copy(a, s - 4, blocks[s - 3], me).wait_recv()
                        copy(a, s - 1, blocks[s - 3], sibling).start()
                        if s < 7 and by_x(a, 0) == (s == 6):
                            relay(a).start()

        targets = [sibling, blocks[2], blocks[3]]

        def small_copies():
            return _all_copies([small_ref], [small_all], small_send, small_recv)

        @pl.when(g == 0)
        def _():
            for cp in small_copies():
                cp.start()
            for a in range(n_arrays):
                keep(a).start()
            for u in range(n_parts):
                for k, to in enumerate(targets):
                    copy(0, k, me, to, u, from_shard=True).start()
            for a in range(1, 4):
                for k, to in enumerate(targets):
                    copy(a, k, me, to, from_shard=True).start()
            load(0).start()

        @pl.when(g == n_tiles - 2)
        def _():
            for cp in small_copies():
                cp.wait()
            fetch = pltpu.make_async_copy(small_all, small_buf, local_sems.at[n_arrays + 4])
            fetch.start()
            fetch.wait()
            meta_ref[0:TILE - N_META, :] = jnp.zeros((TILE - N_META, D_MODEL), F32)
            meta_ref[TILE - N_META:TILE, :] = jnp.concatenate([small_buf[d, 0:N_META, :] for d in range(N_DEV)], axis=1)

        @pl.when(g < n_tiles)
        def _():
            s0 = jnp.where(g == n_tiles - 1, meta_ref[...], x_ref[...])
            r = lax.rsqrt(_rowmean(s0 * s0) + EPS)
            h32 = (s0 * r) * g_ref[...]
            ht_ref[...] = h32.T.astype(BF16)
            h_all[pl.ds(pl.multiple_of(g * TILE, TILE), TILE), :] = h32.astype(BF16)

        for m in range(n_units):
            @pl.when(g == n_tiles + m)
            def _(m=m):
                load(m).wait()
                if m + 1 < n_units:
                    arrive(m + 1)
                    load(m + 1).start()
                if m >= 2:
                    store(m - 2).wait()

        m_now = jnp.maximum(g - n_tiles, 0)
        u_now = functools.reduce(jnp.add, [jnp.where(m_now == m, u, 0) for m, (_, u) in enumerate(units)])
        for u, (_, width) in enumerate(PARTS):
            @pl.when((g >= n_tiles) & (u_now == u))
            def _(width=width):
                w = wbuf[m_now % 2, :, 0:width]
                for r in range(n_chunk):
                    rbuf[m_now % 2, r * chunk:(r + 1) * chunk, 0:width] = _dot(h_all[r * chunk:(r + 1) * chunk, :], w)

        for m in range(n_units):
            @pl.when(g == n_tiles + m)
            def _(m=m):
                store(m).start()

        @pl.when(g == n_steps - 1)
        def _():
            store(n_units - 2).wait()
            store(n_units - 1).wait()
            for a in range(1, 4):
                copy(a, 0, sibling, me).wait_recv()
                for j in range(3):
                    copy(a, 4 + j, blocks[5 + j], me).wait_recv()
            for a in range(n_arrays):
                for u in range(n_parts if a == 0 else 1):
                    for k, to in enumerate(targets):
                        copy(a, k, me, to, u, from_shard=True).wait_send()
                    relay(a, u).wait_send()
                    for j in range(3):
                        copy(a, 4 + j, blocks[2 + j], sibling, u).wait_send()
                keep(a).wait()

    n_x = n_tiles - 1
    return pl.pallas_call(
        body, name="gather_norm_proj",
        out_shape=[jax.ShapeDtypeStruct((D_MODEL, tp), BF16), jax.ShapeDtypeStruct((tp, D_IN), F32),
                   jax.ShapeDtypeStruct((TILE, D_MODEL), F32), jax.ShapeDtypeStruct((N_DEV,) + small_shard.shape[1:], F32),
                   jax.ShapeDtypeStruct((N_DEV,) + w_in_shard.shape, BF16)]
                  + [jax.ShapeDtypeStruct((N_DEV,) + w.shape, BF16) for w in w_out_shards],
        grid_spec=pltpu.PrefetchScalarGridSpec(
            num_scalar_prefetch=1, grid=(n_steps,),
            in_specs=[pl.BlockSpec((TILE, D_MODEL), lambda g, pos_ref: (jnp.minimum(g, n_x - 1), 0)),
                      _VMEM, _ANY, _ANY, _ANY, _ANY, _ANY],
            out_specs=[pl.BlockSpec((D_MODEL, TILE), lambda g, pos_ref: (0, jnp.minimum(g, n_tiles - 1))),
                       _ANY, _VMEM, _ANY, _ANY, _ANY, _ANY, _ANY],
            scratch_shapes=[pltpu.VMEM((tp, D_MODEL), BF16), pltpu.VMEM((2, D_MODEL, widest), BF16),
                            pltpu.VMEM((2, tp, widest), F32), pltpu.VMEM((N_DEV,) + small_shard.shape[1:], F32),
                            pltpu.SemaphoreType.DMA((7 * n_parts + 21,)), pltpu.SemaphoreType.DMA((7 * n_parts + 21,)),
                            pltpu.SemaphoreType.DMA((9,)),
                            pltpu.SemaphoreType.DMA((N_DEV,)), pltpu.SemaphoreType.DMA((N_DEV,))]),
        compiler_params=pltpu.CompilerParams(dimension_semantics=("arbitrary",), vmem_limit_bytes=VMEM_LIMIT),
    )(pos, x2d, norm_g, small_shard, w_in_shard, *w_out_shards)


C_AVAL, C_AGLU, C_AZ, C_BB, C_BC, C_BX, C_BZ, C_GA, C_GB = (k * D_MODEL for k in range(9))
S_AZ, S_BB, S_BZ, S_GA, S_GB = (k * D_MODEL for k in range(5))


def _fused_pass(proj, x2d, tgt2d, meta_tile, conv_a_w, conv_a_b, ln_a_g, ln_a_b, b_a_out, conv_b_w, final_g,
                w_a, w_b, w_o, w_a_t, w_b_t, w_o_t, n_tiles):
    T = TILE
    tp = n_tiles * T
    inv_d = 1.0 / D_MODEL

    def block_of(tile):
        return jnp.where(tile == 0, n_tiles - 1, tile - 1)

    def cur(i):
        return block_of(jnp.minimum(i, n_tiles - 1))

    def prev(i):
        return block_of(jnp.clip(i - 1, 0, n_tiles - 1))

    def xblk(i):
        return jnp.maximum(jnp.minimum(i, n_tiles - 1) - 1, 0)

    def body(proj_ref, aprev, cprev, x_ref, tgt_ref, meta_ref, caw_ref, cab_ref, lng_ref, lnb_ref, bao_ref, cbw_ref,
             fg_ref, wa_ref, wb_ref, wo_ref, wat_ref, wbt_ref, wot_ref,
             dproj_ref, ds1_ref, lhs_ref, rhs_ref, small_ref,
             ua0_buf, cb_buf, dua1_buf, dc3_buf, stage, ua1_buf, c3_buf,
             dpa_buf, dpb_buf, dcaw8, dcbw8, shift_buf):
        i = pl.program_id(0)
        this, before = i % 2, 1 - i % 2

        @pl.when(i == 0)
        def _init():
            for buf in (ua0_buf, cb_buf, dua1_buf, dc3_buf, dcaw8, dcbw8):
                buf[...] = jnp.zeros(buf.shape, buf.dtype)
            small_ref[...] = jnp.zeros(small_ref.shape, F32)

        @pl.when(i >= 1)
        def _emit_stage():
            dproj_ref[:, C_AZ:C_BC] = stage[:, S_AZ:S_BZ]
            dproj_ref[:, C_BZ:D_IN] = stage[:, S_BZ:S_GB + D_MODEL]

        @pl.when(i < n_tiles)
        def _front():
            def conv_chunk(cc, carry):
                c0 = pl.multiple_of(cc * LANES, LANES)
                lanes = pl.ds(c0, LANES)

                def col(base):
                    return pl.ds(pl.multiple_of(base + cc * LANES, LANES), LANES)

                ua0 = proj_ref[:, col(C_AVAL)] * _sigmoid(proj_ref[:, col(C_AGLU)])
                ua0_buf[this, 0:HALO, lanes] = ua0_buf[before, T:T + HALO, lanes]
                ua0_buf[this, HALO:HALO + T, lanes] = ua0
                acc = jnp.broadcast_to(cab_ref[:, lanes], (T, LANES))
                lead = HALO - (CONV_A - 1)
                for r in range(SUBLANES):
                    taps = [k for k in range(CONV_A) if (k + lead) % SUBLANES == r]
                    rows = T + SUBLANES * max((k + lead) // SUBLANES for k in taps)
                    if r:
                        shift_buf[r, 0:rows, :] = ua0_buf[this, pl.ds(r, rows), lanes]
                    for k in taps:
                        q = (k + lead) // SUBLANES
                        if r:
                            win = shift_buf[r, SUBLANES * q:SUBLANES * q + T, :]
                        else:
                            win = ua0_buf[this, pl.ds(SUBLANES * q, T), lanes]
                        acc = acc + caw_ref[k:k + 1, lanes] * win
                ua1_buf[:, lanes] = acc
                cb = proj_ref[:, col(C_BC)] * proj_ref[:, col(C_BX)]
                cb_buf[this, 0:SUBLANES, lanes] = cb_buf[before, T:T + SUBLANES, lanes]
                cb_buf[this, SUBLANES:SUBLANES + T, lanes] = cb
                lead_b = SUBLANES - (CONV_B - 1)
                acc3 = cbw_ref[0:1, lanes] * cb_buf[this, pl.ds(lead_b, T), lanes]
                for k in range(1, CONV_B):
                    acc3 = acc3 + cbw_ref[k:k + 1, lanes] * cb_buf[this, pl.ds(lead_b + k, T), lanes]
                c3_buf[:, lanes] = acc3
                return carry

            lax.fori_loop(0, N_CHUNK, conv_chunk, 0)

            ua1 = ua1_buf[...]
            xc = ua1 - _rowmean(ua1)
            rstd = lax.rsqrt(_rowmean(xc * xc) + EPS)
            xhat = xc * rstd
            ua2 = xhat * lng_ref[...] + lnb_ref[...]
            sg2 = _sigmoid(ua2)
            ua3 = ua2 * sg2
            a_z = proj_ref[:, C_AZ:C_AZ + D_MODEL]
            sz = _sigmoid(a_z)
            silu_az = a_z * sz
            lhs_ref[0] = (ua3 * silu_az).astype(BF16)
            b_z = proj_ref[:, C_BZ:C_BZ + D_MODEL]
            sbz = _sigmoid(b_z)
            silu_bz = b_z * sbz
            b_b = proj_ref[:, C_BB:C_BB + D_MODEL]
            c3 = c3_buf[...]
            ub = b_b * c3
            lhs_ref[1] = (ub * silu_bz).astype(BF16)

            ya = _dot(lhs_ref[0], wa_ref[...]) + bao_ref[...]
            yb = _dot(lhs_ref[1], wb_ref[...])
            sga = _sigmoid(proj_ref[:, C_GA:C_GA + D_MODEL])
            sgb = _sigmoid(proj_ref[:, C_GB:C_GB + D_MODEL])
            m_b = (sga * ya + sgb * yb).astype(BF16)
            lhs_ref[2] = m_b
            s0 = jnp.where(i == 0, meta_ref[...], x_ref[...])
            s1 = s0 + _dot(m_b, wo_ref[...])
            r1 = lax.rsqrt(_rowmean(s1 * s1) + EPS)
            y = (s1 * r1) * fg_ref[...]
            is_token = (i >= 1).astype(F32)
            err = (y - tgt_ref[...]) * is_token
            small_ref[ROW_LOSS:ROW_LOSS + 1, :] += (0.5 * inv_d) * _colsum(err * err)
            dy = err * inv_d
            small_ref[ROW_FINAL_G:ROW_FINAL_G + 1, :] += _colsum(dy * (s1 * r1))
            gy = dy * fg_ref[...]
            ds1 = r1 * gy - s1 * ((r1 * r1 * r1) * _rowmean(gy * s1))
            ds1_ref[...] = ds1
            ds1_b = ds1.astype(BF16)
            rhs_ref[2] = ds1_b
            dm = _dot(ds1_b, wot_ref[...])
            dya = dm * sga
            dyb = dm * sgb
            stage[:, S_GA:S_GA + D_MODEL] = (dya * ya * (1.0 - sga)).astype(BF16)
            stage[:, S_GB:S_GB + D_MODEL] = (dyb * yb * (1.0 - sgb)).astype(BF16)
            small_ref[ROW_B_A_OUT:ROW_B_A_OUT + 1, :] += _colsum(dya)
            dya_b = dya.astype(BF16)
            dyb_b = dyb.astype(BF16)
            rhs_ref[0] = dya_b
            rhs_ref[1] = dyb_b
            dpa_buf[...] = _dot(dya_b, wat_ref[...])
            dpb_buf[...] = _dot(dyb_b, wbt_ref[...])

            dpa = dpa_buf[...]
            stage[:, S_AZ:S_AZ + D_MODEL] = (dpa * ua3 * (sz + silu_az * (1.0 - sz))).astype(BF16)
            dua2 = dpa * silu_az * (sg2 + ua3 * (1.0 - sg2))
            small_ref[ROW_LN_G:ROW_LN_G + 1, :] += _colsum(dua2 * xhat)
            small_ref[ROW_LN_B:ROW_LN_B + 1, :] += _colsum(dua2)
            dxh = dua2 * lng_ref[...]
            dua1 = rstd * (dxh - _rowmean(dxh) - xhat * _rowmean(dxh * xhat))
            small_ref[ROW_CONV_A_B:ROW_CONV_A_B + 1, :] += _colsum(dua1)
            dua1_buf[this, 0:T, :] = dua1
            dua1_buf[before, T:T + HALO, :] = dua1[0:HALO]
            dpb = dpb_buf[...]
            stage[:, S_BZ:S_BZ + D_MODEL] = (dpb * ub * (sbz + silu_bz * (1.0 - sbz))).astype(BF16)
            dub = dpb * silu_bz
            stage[:, S_BB:S_BB + D_MODEL] = (dub * c3).astype(BF16)
            dc3 = dub * b_b
            dc3_buf[this, 0:T, :] = dc3
            dc3_buf[before, T:T + SUBLANES, :] = dc3[0:SUBLANES]

        @pl.when(i == n_tiles)
        def _no_later_tile():
            dua1_buf[before, T:T + HALO, :] = jnp.zeros((HALO, D_MODEL), F32)
            dc3_buf[before, T:T + SUBLANES, :] = jnp.zeros((SUBLANES, D_MODEL), F32)

        @pl.when(i >= 1)
        def _lagged():
            def convt_chunk(cc, carry):
                c0 = pl.multiple_of(cc * LANES, LANES)
                lanes = pl.ds(c0, LANES)

                def col(base):
                    return pl.ds(pl.multiple_of(base + cc * LANES, LANES), LANES)

                ua0 = ua0_buf[before, HALO:HALO + T, lanes]
                acc = jnp.zeros((T, LANES), F32)
                for r in range(SUBLANES):
                    shifts = [j for j in range(CONV_A) if j % SUBLANES == r]
                    rows = T + shifts[-1] - r
                    if r:
                        shift_buf[r, 0:rows, :] = dua1_buf[before, pl.ds(r, rows), lanes]
                    for j in shifts:
                        k = CONV_A - 1 - j
                        if r:
                            later = shift_buf[r, j - r:j - r + T, :]
                        else:
                            later = dua1_buf[before, pl.ds(j, T), lanes]
                        acc = acc + caw_ref[k:k + 1, lanes] * later
                        dcaw8[SUBLANES * k:SUBLANES * (k + 1), lanes] += _fold8(ua0 * later)
                a_val = aprev[:, col(0)]
                sg = _sigmoid(aprev[:, col(D_MODEL)])
                dproj_ref[:, col(C_AVAL)] = (acc * sg).astype(BF16)
                dproj_ref[:, col(C_AGLU)] = (acc * a_val * (sg * (1.0 - sg))).astype(BF16)

                cb = cb_buf[before, SUBLANES:SUBLANES + T, lanes]
                acc3 = jnp.zeros((T, LANES), F32)
                for j in range(CONV_B):
                    k = CONV_B - 1 - j
                    later = dc3_buf[before, pl.ds(j, T), lanes]
                    acc3 = acc3 + cbw_ref[k:k + 1, lanes] * later
                    dcbw8[SUBLANES * k:SUBLANES * (k + 1), lanes] += _fold8(cb * later)
                dproj_ref[:, col(C_BC)] = (acc3 * cprev[:, col(D_MODEL)]).astype(BF16)
                dproj_ref[:, col(C_BX)] = (acc3 * cprev[:, col(0)]).astype(BF16)
                return carry

            lax.fori_loop(0, N_CHUNK, convt_chunk, 0)

        @pl.when(i == n_tiles)
        def _finish():
            for k in range(CONV_A):
                small_ref[ROW_CONV_A_W + k:ROW_CONV_A_W + k + 1, :] = _colsum(dcaw8[SUBLANES * k:SUBLANES * (k + 1), :])
            for k in range(CONV_B):
                small_ref[ROW_CONV_B_W + k:ROW_CONV_B_W + k + 1, :] = _colsum(dcbw8[SUBLANES * k:SUBLANES * (k + 1), :])

    pair = 2 * D_MODEL
    return pl.pallas_call(
        body, name="fused_pass", grid=(n_tiles + 1,),
        out_shape=[
            jax.ShapeDtypeStruct((tp, D_IN), BF16),
            jax.ShapeDtypeStruct((tp, D_MODEL), F32),
            jax.ShapeDtypeStruct((3, tp, D_MODEL), BF16),
            jax.ShapeDtypeStruct((3, tp, D_MODEL), BF16),
            jax.ShapeDtypeStruct((SMALL_A_ROWS, D_MODEL), F32),
        ],
        in_specs=[
            pl.BlockSpec((T, D_IN), lambda i: (cur(i), 0)),
            pl.BlockSpec((T, pair), lambda i: (prev(i), C_AVAL // pair)),
            pl.BlockSpec((T, pair), lambda i: (prev(i), C_BC // pair)),
            pl.BlockSpec((T, D_MODEL), lambda i: (xblk(i), 0)),
            pl.BlockSpec((T, D_MODEL), lambda i: (xblk(i), 0)),
            _VMEM, _VMEM, _VMEM, _VMEM, _VMEM, _VMEM, _VMEM, _VMEM,
            *[_resident((D_MODEL, D_MODEL)) for _ in range(6)],
        ],
        out_specs=[
            pl.BlockSpec((T, D_IN), lambda i: (prev(i), 0)),
            pl.BlockSpec((T, D_MODEL), lambda i: (cur(i), 0)),
            pl.BlockSpec((3, T, D_MODEL), lambda i: (0, cur(i), 0)),
            pl.BlockSpec((3, T, D_MODEL), lambda i: (0, cur(i), 0)),
            _VMEM,
        ],
        scratch_shapes=[
            pltpu.VMEM((2, HALO + T, D_MODEL), F32),
            pltpu.VMEM((2, SUBLANES + T, D_MODEL), F32),
            pltpu.VMEM((2, T + HALO, D_MODEL), F32),
            pltpu.VMEM((2, T + SUBLANES, D_MODEL), F32),
            pltpu.VMEM((T, 5 * D_MODEL), BF16),
            pltpu.VMEM((T, D_MODEL), F32),
            pltpu.VMEM((T, D_MODEL), F32),
            pltpu.VMEM((T, D_MODEL), F32),
            pltpu.VMEM((T, D_MODEL), F32),
            pltpu.VMEM((32 * SUBLANES, D_MODEL), F32),
            pltpu.VMEM((SUBLANES * SUBLANES, D_MODEL), F32),
            pltpu.VMEM((SUBLANES, T + HALO, LANES), F32),
        ],
        compiler_params=pltpu.CompilerParams(dimension_semantics=("arbitrary",), vmem_limit_bytes=VMEM_LIMIT),
    )(proj, proj, proj, x2d, tgt2d, meta_tile, conv_a_w, conv_a_b, ln_a_g, ln_a_b, b_a_out, conv_b_w, final_g,
      w_a, w_b, w_o, w_a_t, w_b_t, w_o_t)


def _input_bwd(dproj, ds1, x2d, meta_tile, norm_g, w_in_all, row_tile):
    seq = x2d.shape[0]
    n_steps = seq // row_tile
    meta_block = seq // TILE

    def backward(dp_ref, ds1_ref, s0_ref, g_ref, w_ref, out_ref, vec_ref):
        dh = _dot_nt(dp_ref[:, 0:COLS], w_ref[0])
        for j in range(1, N_DEV):
            dh = dh + _dot_nt(dp_ref[:, j * COLS:(j + 1) * COLS], w_ref[j])
        s0v = s0_ref[...]
        r = lax.rsqrt(_rowmean(s0v * s0v) + EPS)
        gh = dh * g_ref[...]
        out_ref[...] = ds1_ref[...] + r * gh - s0v * ((r * r * r) * _rowmean(gh * s0v))
        vec_ref[ROW_NORM_G:ROW_NORM_G + 1, :] += _colsum(dh * (s0v * r))

    def body(dp_ref, ds1_ref, x_ref, dpm_ref, ds1m_ref, meta_ref, g_ref, w_ref, gx_ref, small_ref, gmeta_buf):
        t = pl.program_id(0)

        @pl.when(t == 0)
        def _():
            small_ref[...] = jnp.zeros(small_ref.shape, F32)

        backward(dp_ref, ds1_ref, x_ref, g_ref, w_ref, gx_ref, small_ref)

        @pl.when(t == n_steps - 1)
        def _():
            backward(dpm_ref, ds1m_ref, meta_ref, g_ref, w_ref, gmeta_buf, small_ref)
            small_ref[ROW_META:ROW_META + N_META, :] = gmeta_buf[TILE - N_META:TILE, :]

    return pl.pallas_call(
        body, name="input_bwd", grid=(n_steps,),
        out_shape=[jax.ShapeDtypeStruct(x2d.shape, F32), jax.ShapeDtypeStruct((SMALL_B_ROWS, D_MODEL), F32)],
        in_specs=[pl.BlockSpec((row_tile, D_IN), lambda t: (t, 0)),
                  pl.BlockSpec((row_tile, D_MODEL), lambda t: (t, 0)),
                  pl.BlockSpec((row_tile, D_MODEL), lambda t: (t, 0)),
                  pl.BlockSpec((TILE, D_IN), lambda t: (meta_block, 0)),
                  pl.BlockSpec((TILE, D_MODEL), lambda t: (meta_block, 0)),
                  _VMEM, _VMEM, _resident((N_DEV, D_MODEL, COLS))],
        out_specs=[pl.BlockSpec((row_tile, D_MODEL), lambda t: (t, 0)), _VMEM],
        scratch_shapes=[pltpu.VMEM((TILE, D_MODEL), F32)],
        compiler_params=pltpu.CompilerParams(dimension_semantics=("arbitrary",), vmem_limit_bytes=VMEM_LIMIT),
    )(dproj, ds1, x2d, dproj, ds1, meta_tile, norm_g, w_in_all)


def _grad_w_in_half(pos, h_t, dproj, k_tile, other_side, rides, name, after=None, add_to=None):
    tp = h_t.shape[1]
    n_k = tp // k_tile
    order = [] if after is None else [after]
    summing = add_to is not None

    def column_block(q, k, pos_ref):
        return k, 2 * q + (1 - pos_ref[2] if other_side else pos_ref[2])

    def body(pos_ref, h_ref, dp_ref, *refs):
        o_ref = refs[-2] if summing else refs[-1]

        @pl.when(pl.program_id(1) == 0)
        def _():
            o_ref[...] = refs[0][...] if summing else jnp.zeros(o_ref.shape, F32)

        o_ref[...] += _dot(h_ref[...], dp_ref[...])

        if summing:
            @pl.when(pl.program_id(1) == n_k - 1)
            def _():
                refs[-1][...] = o_ref[...].astype(BF16)

    ride = [a for _, arrays in rides for a in arrays]
    n_arr = len(ride)
    ride_shapes, ride_sems = _ride_shapes(rides)
    block = (None, D_MODEL, COLS)
    extra_in = [add_to] if summing else []
    extra_in_specs = [pl.BlockSpec(block, lambda q, k, pos_ref: (q, 0, 0))] if summing else []
    extra_out = [jax.ShapeDtypeStruct((4, D_MODEL, COLS), BF16)] if summing else []
    extra_out_specs = [pl.BlockSpec(block, lambda q, k, pos_ref: (q ^ (2 * pos_ref[0] + pos_ref[1]), 0, 0))] \
        if summing else []
    body = _riding(body, 3 + len(extra_in) + len(order), 1 + len(extra_out), rides,
                   lambda: (pl.program_id(0) == 0) & (pl.program_id(1) == 0),
                   lambda: (pl.program_id(0) == 3) & (pl.program_id(1) == n_k - 1))
    return pl.pallas_call(
        body, name=name,
        out_shape=[jax.ShapeDtypeStruct((4, D_MODEL, COLS), F32)] + extra_out + ride_shapes,
        grid_spec=pltpu.PrefetchScalarGridSpec(
            num_scalar_prefetch=1, grid=(4, n_k),
            in_specs=[pl.BlockSpec((D_MODEL, k_tile), lambda q, k, pos_ref: (0, k)),
                      pl.BlockSpec((k_tile, COLS), column_block)] + extra_in_specs + [_ANY] * (len(order) + n_arr),
            out_specs=[pl.BlockSpec(block, lambda q, k, pos_ref: (q, 0, 0))] + extra_out_specs + [_ANY] * n_arr,
            scratch_shapes=ride_sems),
        compiler_params=pltpu.CompilerParams(dimension_semantics=("arbitrary", "arbitrary"),
                                             vmem_limit_bytes=VMEM_LIMIT),
    )(pos, h_t, dproj, *extra_in, *order, *ride)


def _grad_w_out(lhs, rhs, k_tile, after):
    tp = lhs.shape[1]

    def body(a_ref, b_ref, after_ref, o_ref):
        @pl.when(pl.program_id(1) == 0)
        def _():
            o_ref[...] = jnp.zeros(o_ref.shape, F32)

        o_ref[...] += _dot_tn(a_ref[...], b_ref[...]).reshape(N_DEV, ROWS_OUT, D_MODEL)

    return pl.pallas_call(
        body, name="grad_w_out", grid=(3, tp // k_tile),
        out_shape=jax.ShapeDtypeStruct((N_DEV, 3, ROWS_OUT, D_MODEL), F32),
        in_specs=[pl.BlockSpec((None, k_tile, D_MODEL), lambda w, k: (w, k, 0)),
                  pl.BlockSpec((None, k_tile, D_MODEL), lambda w, k: (w, k, 0)), _ANY],
        out_specs=pl.BlockSpec((N_DEV, None, ROWS_OUT, D_MODEL), lambda w, k: (0, w, 0, 0)),
        compiler_params=pltpu.CompilerParams(dimension_semantics=("arbitrary", "arbitrary"),
                                             vmem_limit_bytes=VMEM_LIMIT),
    )(lhs, rhs, after)


def _adamw_math(w, g, m, v):
    m = ADAM_B1 * m + (1.0 - ADAM_B1) * g
    v = ADAM_B2 * v + (1.0 - ADAM_B2) * (g * g)
    m_hat = m / (1.0 - ADAM_B1 ** ADAM_STEP)
    v_hat = v / (1.0 - ADAM_B2 ** ADAM_STEP)
    delta = -ADAM_LR * (m_hat / (jnp.sqrt(v_hat) + ADAM_EPS) + ADAM_WD * w)
    return delta, m, v


def _adamw_sharded(pos, mine, theirs, landed, weights, row_tile, name, after=None):
    order = [] if after is None else [after]
    rows, n = weights[0][0].shape
    n_slots = mine.shape[0]
    per_shard = rows // row_tile
    assert per_shard == 1 or len(weights) == 1

    def mine_map(j, t, pos_ref):
        chip = 2 * pos_ref[0] + pos_ref[1]
        return (2 * chip + pos_ref[2] if n_slots == N_DEV else chip), j * per_shard + t, 0

    def theirs_map(j, t, pos_ref):
        return 2 * pos_ref[0] + pos_ref[1], j * per_shard + t, 0

    def body(pos_ref, mine_ref, *refs):
        if theirs is not None:
            g = mine_ref[...] + refs[0][...]
            refs = refs[1:]
        else:
            g = mine_ref[...]
        land_ref, refs = refs[0], refs[1:]
        ins, outs = refs[:3 * len(weights)], refs[3 * len(weights) + len(order):]
        for k in range(3):
            g = g + land_ref[k].astype(F32)
        for j in range(len(weights)):
            @pl.when(pl.program_id(0) == j)
            def _(j=j):
                w_ref, m_ref, v_ref = ins[3 * j:3 * j + 3]
                delta, m_new, v_new = _adamw_math(w_ref[...], g, m_ref[...], v_ref[...])
                for ref, val in zip(outs[4 * j:4 * j + 4], (g, delta, m_new, v_new)):
                    ref[...] = val

    tile = pl.BlockSpec((row_tile, n), lambda j, t, pos_ref: (t, 0))
    res = pl.pallas_call(
        body, name=name,
        out_shape=[jax.ShapeDtypeStruct((rows, n), F32)] * (4 * len(weights)),
        grid_spec=pltpu.PrefetchScalarGridSpec(
            num_scalar_prefetch=1, grid=(len(weights), per_shard),
            in_specs=[pl.BlockSpec((None, row_tile, n), mine_map)]
            + ([pl.BlockSpec((None, row_tile, n), theirs_map)] if theirs is not None else [])
            + [pl.BlockSpec((3, row_tile, n), lambda j, t, pos_ref: (0, j * per_shard + t, 0))]
            + [tile] * (3 * len(weights)) + [_ANY] * len(order),
            out_specs=[tile] * (4 * len(weights))),
        compiler_params=pltpu.CompilerParams(dimension_semantics=("arbitrary", "arbitrary")),
    )(pos, mine, *([theirs] if theirs is not None else []), landed, *[a for wmv in weights for a in wmv], *order)
    return [res[4 * j:4 * j + 4] for j in range(len(weights))]


def _adamw_small(gathered, gathered_cols, params):
    n_par, n_src = len(params), len(gathered)

    def body(*refs):
        g_refs, gc_refs = refs[:n_src], refs[n_src:2 * n_src]
        ins = refs[2 * n_src:2 * n_src + 3 * n_par]
        outs = refs[2 * n_src + 3 * n_par:]
        loss_ref = outs[4 * n_par]

        def reduced(ref, row, n_rows):
            g = ref[0, row:row + n_rows, :]
            for d in range(1, N_DEV):
                g = g + ref[d, row:row + n_rows, :]
            return g

        for p, (src, row, n_rows, sharded, _, _, _) in enumerate(params):
            g = reduced((gc_refs if sharded else g_refs)[src], row, n_rows)
            w_ref, m_ref, v_ref = ins[3 * p:3 * p + 3]
            delta, m_new, v_new = _adamw_math(w_ref[...], g, m_ref[...], v_ref[...])
            outs[4 * p][...] = g
            outs[4 * p + 1][...] = delta
            outs[4 * p + 2][...] = m_new
            outs[4 * p + 3][...] = v_new
        loss = jnp.sum(reduced(g_refs[0], ROW_LOSS, 1), axis=1, keepdims=True)
        loss_ref[...] = jnp.broadcast_to(loss, loss_ref.shape)

    out_shape = []
    for (_, _, _, _, w, _, _) in params:
        out_shape += [jax.ShapeDtypeStruct(w.shape, F32)] * 4
    out_shape.append(jax.ShapeDtypeStruct((1, LANES), F32))
    flat = [a for (_, _, _, _, w, m, v) in params for a in (w, m, v)]
    return pl.pallas_call(
        body, name="adamw_small", out_shape=out_shape,
        in_specs=[_VMEM] * (2 * n_src + len(flat)), out_specs=[_VMEM] * len(out_shape),
    )(*gathered, *gathered_cols, *flat)


def _pad_rows(a, rows):
    return jnp.concatenate([a, jnp.zeros((rows - a.shape[0], a.shape[1]), a.dtype)], axis=0)


def kernel(x, meta_tokens, norm_g, w_in, conv_a_w, conv_a_b, ln_a_g, ln_a_b, w_a_out, b_a_out, conv_b_w, w_b_out, w_out, final_g, loss_target, m_meta_tokens, m_norm_g, m_w_in, m_conv_a_w, m_conv_a_b, m_ln_a_g, m_ln_a_b, m_w_a_out, m_b_a_out, m_conv_b_w, m_w_b_out, m_w_out, m_final_g, v_meta_tokens, v_norm_g, v_w_in, v_conv_a_w, v_conv_a_b, v_ln_a_g, v_ln_a_b, v_w_a_out, v_b_a_out, v_conv_b_w, v_w_b_out, v_w_out, v_final_g):
    seq = x.shape[1]
    assert x.shape == (1, seq, D_MODEL) and seq % TILE == 0 and w_in.shape == (1, D_MODEL, COLS)
    n_tiles = seq // TILE + 1
    tp = n_tiles * TILE
    pos = jnp.stack([lax.axis_index("x"), lax.axis_index("y"), lax.axis_index("c")]).astype(jnp.int32)
    me = 4 * pos[0] + 2 * pos[1] + pos[2]
    x2d = x[0]
    tgt2d = loss_target[0]

    small = jnp.concatenate([meta_tokens, _pad_rows(conv_a_w[0], 32), _pad_rows(conv_b_w[0], SUBLANES)], axis=0)
    final_g2 = final_g.reshape(1, D_MODEL)

    w_out_shards = [w[0].astype(BF16) for w in (w_a_out, w_b_out, w_out)]
    h_t, proj, meta_tile, small_params, w_in_all, *w_out_all = _gather_norm_proj(
        pos, x2d, small[None], norm_g, w_in[0].astype(BF16), w_out_shards, 3)
    small_params = small_params.transpose(1, 0, 2).reshape(small.shape[0], D_MODEL)
    conv_a_full, conv_b_full = small_params[N_META:N_META + 32], small_params[N_META + 32:]
    w_out_all = [w.reshape(D_MODEL, D_MODEL) for w in w_out_all]
    w_out_all_t = [w.T for w in w_out_all]
    dproj, ds1, lhs, rhs, small_a = _fused_pass(
        proj, x2d, tgt2d, meta_tile, conv_a_full, conv_a_b, ln_a_g, ln_a_b, b_a_out, conv_b_full, final_g2,
        w_out_all[0], w_out_all[1], w_out_all[2], w_out_all_t[0], w_out_all_t[1], w_out_all_t[2], n_tiles)
    k_tile = tp // 3
    gw_far, small_a_all = _grad_w_in_half(pos, h_t, dproj, k_tile, True, [("all", (small_a[None],))], "grad_w_in_far")
    sems, sent, landing, token = _start_exchanges([("sibling_half", (gw_far,))], "rs_far_start")
    gw_out = _grad_w_out(lhs, rhs, k_tile, token).reshape(N_DEV, 3 * ROWS_OUT, D_MODEL)
    (their_in,) = _wait_exchanges([("sibling_half", 1)], sems, sent, landing, gw_out, "rs_far_wait")
    sems_o, sent_o, landing_o, token = _start_exchanges([("sibling", (gw_out,))], "rs_out_start")
    gw_near, parts_in = _grad_w_in_half(pos, h_t, dproj, k_tile, False, [], "grad_w_in_near", after=token,
                                        add_to=their_in)
    sems_i, sent_i, landing_i, token = _start_exchanges([("chips_by_relation", (parts_in,))], "rs_chips_in_start")
    gw_out, their_out = _wait_exchanges([("sibling", 1)], sems_o, sent_o, landing_o, token, "rs_out_wait",
                                        keep_sources=True)
    parts_out = _chip_partial(pos, gw_out, their_out, (1, 2, 3), BF16, ROWS_OUT, "rs_parts_w_out")
    sems_o, sent_o, landing_o, token = _start_exchanges([("chips", (parts_out,))], "rs_chips_out_start")
    grad_x, small_b = _input_bwd(dproj, ds1, x2d, meta_tile, norm_g + token[0, 0], w_in_all, min(512, seq))

    sems_s, sent_s, landing_s, token = _start_exchanges([("all", (small_b[None],))], "gather_small_grads_start")
    (land_in,) = _wait_exchanges([("chips_by_relation", 1)], sems_i, sent_i, landing_i, token, "rs_chips_in_wait")
    (res_in,) = _adamw_sharded(pos, gw_near, None, land_in, [(w_in[0], m_w_in[0], v_w_in[0])], 128, "adamw_w_in")
    (land_out,) = _wait_exchanges([("chips", 1)], sems_o, sent_o, landing_o, res_in[0], "rs_chips_out_wait")
    res_out = _adamw_sharded(
        pos, gw_out, their_out, land_out,
        [(w_a_out[0], m_w_a_out[0], v_w_a_out[0]), (w_b_out[0], m_w_b_out[0], v_w_b_out[0]),
         (w_out[0], m_w_out[0], v_w_out[0])], ROWS_OUT, "adamw_w_out")
    (small_b_all,) = _wait_exchanges([("all", 1)], sems_s, sent_s, landing_s, res_out[2][0], "gather_small_grads_wait")
    small_grads = [small_a_all, small_b_all]
    small_cols = [lax.dynamic_slice_in_dim(g, me * LANES, LANES, axis=2) for g in small_grads]
    params = [
        (1, ROW_META, N_META, True, meta_tokens, m_meta_tokens, v_meta_tokens),
        (1, ROW_NORM_G, 1, False, norm_g, m_norm_g, v_norm_g),
        (0, ROW_CONV_A_W, CONV_A, True, conv_a_w[0], m_conv_a_w[0], v_conv_a_w[0]),
        (0, ROW_CONV_A_B, 1, False, conv_a_b, m_conv_a_b, v_conv_a_b),
        (0, ROW_LN_G, 1, False, ln_a_g, m_ln_a_g, v_ln_a_g),
        (0, ROW_LN_B, 1, False, ln_a_b, m_ln_a_b, v_ln_a_b),
        (0, ROW_B_A_OUT, 1, False, b_a_out, m_b_a_out, v_b_a_out),
        (0, ROW_CONV_B_W, CONV_B, True, conv_b_w[0], m_conv_b_w[0], v_conv_b_w[0]),
        (0, ROW_FINAL_G, 1, False, final_g2, m_final_g.reshape(1, D_MODEL), v_final_g.reshape(1, D_MODEL)),
    ]
    res_small = _adamw_small(small_grads, small_cols, params)
    loss = res_small[-1][0, 0]

    def small_res(p, kind, shape):
        return res_small[4 * p + kind].reshape(shape)

    per_weight = []
    for kind in range(4):
        per_weight.append([
            small_res(0, kind, meta_tokens.shape),
            small_res(1, kind, norm_g.shape),
            res_in[kind].reshape(w_in.shape),
            small_res(2, kind, conv_a_w.shape),
            small_res(3, kind, conv_a_b.shape),
            small_res(4, kind, ln_a_g.shape),
            small_res(5, kind, ln_a_b.shape),
            res_out[0][kind].reshape(w_a_out.shape),
            small_res(6, kind, b_a_out.shape),
            small_res(7, kind, conv_b_w.shape),
            res_out[1][kind].reshape(w_b_out.shape),
            res_out[2][kind].reshape(w_out.shape),
            small_res(8, kind, final_g.shape),
        ])
    return (loss, grad_x.reshape(x.shape), *per_weight[0], *per_weight[1], *per_weight[2], *per_weight[3])
```

```python
import functools

import jax
import jax.numpy as jnp
from jax import lax
from jax.experimental import pallas as pl
from jax.experimental.pallas import tpu as pltpu

D_MODEL = 1024
N_META = 16
N_DEV = 8
D_IN = 9 * D_MODEL
COLS = D_IN // N_DEV
ROWS_OUT = D_MODEL // N_DEV
CONV_A = 31
CONV_B = 3
EPS = 1e-6

ADAM_LR = 0.001
ADAM_B1 = 0.9
ADAM_B2 = 0.999
ADAM_EPS = 1e-08
ADAM_WD = 0.01
ADAM_STEP = 10

TILE = 128
LANES = 128
N_CHUNK = D_MODEL // LANES
HALO = 32
SUBLANES = 8
VMEM_LIMIT = 56 * 1024 * 1024

ROW_FINAL_G, ROW_B_A_OUT, ROW_LN_G, ROW_LN_B, ROW_CONV_A_B, ROW_LOSS = 0, 1, 2, 3, 4, 5
ROW_CONV_A_W, ROW_CONV_B_W, SMALL_A_ROWS = 8, 40, 48
ROW_NORM_G, ROW_META, SMALL_B_ROWS = 0, 8, 24

MESH = pl.DeviceIdType.MESH
_ANY = pl.BlockSpec(memory_space=pl.ANY)
_VMEM = pl.BlockSpec(memory_space=pltpu.VMEM)
BF16 = jnp.bfloat16
F32 = jnp.float32


def _resident(shape):
    return pl.BlockSpec(shape, lambda *_: (0,) * len(shape), pipeline_mode=pl.Buffered(1))


def _sigmoid(v):
    return jax.nn.sigmoid(v)


def _dot(a, b):
    return jnp.dot(a, b, preferred_element_type=F32)


def _dot_nt(a, b):
    return lax.dot_general(a, b, (((1,), (1,)), ((), ())), preferred_element_type=F32)


def _dot_tn(a, b):
    return lax.dot_general(a, b, (((0,), (0,)), ((), ())), preferred_element_type=F32)


def _colsum(v):
    return jnp.sum(v, axis=0, keepdims=True)


def _rowmean(v):
    parts = [v[:, LANES * c:LANES * (c + 1)] for c in range(v.shape[1] // LANES)]
    return jnp.sum(functools.reduce(jnp.add, parts), axis=-1, keepdims=True) * (1.0 / v.shape[1])


def _fold8(v):
    parts = [v[SUBLANES * g:SUBLANES * (g + 1)] for g in range(v.shape[0] // SUBLANES)]
    return functools.reduce(jnp.add, parts)


def _sibling_copies(srcs, dsts, send_sems, recv_sems):
    x, y, c = lax.axis_index("x"), lax.axis_index("y"), lax.axis_index("c")
    return [pltpu.make_async_remote_copy(
        src_ref=src.at[2 * q + (1 - c)], dst_ref=dst.at[q],
        send_sem=send_sems.at[4 * a + q], recv_sem=recv_sems.at[4 * a + q],
        device_id=(x, y, 1 - c), device_id_type=MESH)
        for a, (src, dst) in enumerate(zip(srcs, dsts)) for q in range(4)]


def _chip_copies(srcs, dsts, send_sems, recv_sems):
    x, y, c = lax.axis_index("x"), lax.axis_index("y"), lax.axis_index("c")
    targets = [(x, 1 - y, c), (1 - x, y, c), (1 - x, 1 - y, c)]
    return [pltpu.make_async_remote_copy(
        src_ref=src.at[k], dst_ref=dst.at[k],
        send_sem=send_sems.at[3 * a + k], recv_sem=recv_sems.at[3 * a + k],
        device_id=targets[k], device_id_type=MESH)
        for a, (src, dst) in enumerate(zip(srcs, dsts)) for k in range(3)]


def _sibling_half_copies(srcs, dsts, send_sems, recv_sems):
    x, y, c = lax.axis_index("x"), lax.axis_index("y"), lax.axis_index("c")
    return [pltpu.make_async_remote_copy(
        src_ref=src.at[q], dst_ref=dst.at[q],
        send_sem=send_sems.at[4 * a + q], recv_sem=recv_sems.at[4 * a + q],
        device_id=(x, y, 1 - c), device_id_type=MESH)
        for a, (src, dst) in enumerate(zip(srcs, dsts)) for q in range(4)]


def _all_copies(srcs, dsts, send_sems, recv_sems):
    x, y, c = lax.axis_index("x"), lax.axis_index("y"), lax.axis_index("c")
    mine = 4 * x + 2 * y + c
    copies = []
    for a, (src, dst) in enumerate(zip(srcs, dsts)):
        copies.append(pltpu.make_async_copy(src.at[0], dst.at[mine], send_sems.at[N_DEV * a]))
        for k in range(1, N_DEV):
            copies.append(pltpu.make_async_remote_copy(
                src_ref=src.at[0], dst_ref=dst.at[mine],
                send_sem=send_sems.at[N_DEV * a + k], recv_sem=recv_sems.at[N_DEV * a + k],
                device_id=(x ^ (k >> 2), y ^ ((k >> 1) & 1), c ^ (k & 1)), device_id_type=MESH))
    return copies


def _chip_copies_by_relation(srcs, dsts, send_sems, recv_sems):
    return _chip_copies([src.at[pl.ds(1, 3)] for src in srcs], dsts, send_sems, recv_sems)


_EXCHANGES = {"sibling": (4, _sibling_copies, 4), "sibling_half": (4, _sibling_half_copies, 4),
              "chips": (3, _chip_copies, 3), "chips_by_relation": (3, _chip_copies_by_relation, 3),
              "all": (N_DEV, _all_copies, N_DEV)}


def _exchange_shapes(kind, arrays):
    per_array, _, slots = _EXCHANGES[kind]
    out_shape = [jax.ShapeDtypeStruct((slots,) + a.shape[1:], a.dtype) for a in arrays]
    sems = [pltpu.SemaphoreType.DMA((per_array * len(arrays),))] * 2
    return out_shape, sems


def _ride_shapes(rides):
    shapes, sems = [], []
    for kind, arrays in rides:
        ride_shapes, ride_sems = _exchange_shapes(kind, arrays)
        shapes += ride_shapes
        sems += ride_sems
    return shapes, sems


def _riding(body, n_in, n_out, rides, is_first, is_last):
    counts = [len(arrays) for _, arrays in rides]
    n_arr = sum(counts)

    def wrapped(*refs):
        ins, srcs = refs[:n_in], refs[n_in:n_in + n_arr]
        outs = refs[n_in + n_arr:n_in + n_arr + n_out]
        dsts = refs[n_in + n_arr + n_out:n_in + 2 * n_arr + n_out]
        first_sem = len(refs) - 2 * len(rides)
        scratch, sems = refs[n_in + 2 * n_arr + n_out:first_sem], refs[first_sem:]

        def copies():
            made, at = [], 0
            for r, ((kind, _), n) in enumerate(zip(rides, counts)):
                made += _EXCHANGES[kind][1](srcs[at:at + n], dsts[at:at + n], sems[2 * r], sems[2 * r + 1])
                at += n
            return made

        @pl.when(is_first())
        def _():
            for cp in copies():
                cp.start()

        body(*ins, *outs, *scratch)

        @pl.when(is_last())
        def _():
            for cp in copies():
                cp.wait()

    return wrapped


_HBM = pl.BlockSpec(memory_space=pltpu.HBM)
_SEM = pl.BlockSpec(memory_space=pltpu.SEMAPHORE)
_FLOWS = pltpu.SideEffectType.DATAFLOW_SIDE_EFFECTING


def _start_exchanges(rides, name):
    arrays = [a for _, group in rides for a in group]
    shapes, sems = _ride_shapes(rides)
    n_arr, n_sem = len(arrays), len(sems)

    def body(*refs):
        srcs, lands = refs[:n_arr], refs[n_arr:2 * n_arr]
        sem_refs, token = refs[2 * n_arr:2 * n_arr + n_sem], refs[-1]
        at = 0
        for r, (kind, group) in enumerate(rides):
            n = len(group)
            for cp in _EXCHANGES[kind][1](srcs[at:at + n], lands[at:at + n], sem_refs[2 * r], sem_refs[2 * r + 1]):
                cp.start()
            at += n
        token[...] = jnp.zeros(token.shape, token.dtype)

    in_hbm = [pltpu.HBM(a.shape, a.dtype) for a in arrays]
    land_hbm = [pltpu.HBM(sh.shape, sh.dtype) for sh in shapes]
    res = pl.pallas_call(
        body, name=name,
        out_shape=(*sems, *in_hbm, *land_hbm, jax.ShapeDtypeStruct((SUBLANES, LANES), F32)),
        in_specs=[_HBM] * (2 * n_arr), out_specs=(*[_SEM] * n_sem, *[_HBM] * (2 * n_arr), _VMEM),
        input_output_aliases={i: n_sem + i for i in range(2 * n_arr)},
        compiler_params=pltpu.CompilerParams(has_side_effects=_FLOWS),
    )(*[pltpu.with_memory_space_constraint(a, pltpu.HBM) for a in arrays],
      *[pltpu.with_memory_space_constraint(lax.empty(sh.shape, sh.dtype), pltpu.HBM) for sh in shapes])
    return res[:n_sem], res[n_sem:n_sem + n_arr], res[n_sem + n_arr:n_sem + 2 * n_arr], res[-1]


def _wait_exchanges(kinds, sems, arrays, lands, after, name, keep_sources=False):
    n_arr, n_sem = len(arrays), len(sems)

    def body(*refs):
        srcs, dsts = refs[:n_arr], refs[n_arr:2 * n_arr]
        sem_refs = refs[2 * n_arr:2 * n_arr + n_sem]
        at = 0
        for r, (kind, n) in enumerate(kinds):
            for cp in _EXCHANGES[kind][1](srcs[at:at + n], dsts[at:at + n], sem_refs[2 * r], sem_refs[2 * r + 1]):
                cp.wait()
            at += n

    hbm = [pltpu.HBM(a.shape, a.dtype) for a in (*arrays, *lands)]
    return pl.pallas_call(
        body, name=name, out_shape=tuple(hbm),
        in_specs=[_HBM] * (2 * n_arr) + [_SEM] * n_sem + [_ANY], out_specs=tuple([_HBM] * (2 * n_arr)),
        input_output_aliases={i: i for i in range(2 * n_arr)},
        compiler_params=pltpu.CompilerParams(has_side_effects=_FLOWS),
    )(*arrays, *lands, *sems, after)[0 if keep_sources else n_arr:]


def _chip_partial(pos, mine, theirs, relations, out_dtype, row_tile, name):
    n_slots, m, n = mine.shape
    q0 = relations[0]

    def chip_of(qi, pos_ref):
        q = qi + q0
        return pos_ref[0] ^ (q >> 1), pos_ref[1] ^ (q & 1)

    def mine_map(qi, t, pos_ref):
        px, py = chip_of(qi, pos_ref)
        return (4 * px + 2 * py + pos_ref[2] if n_slots == N_DEV else 2 * px + py), t, 0

    def theirs_map(qi, t, pos_ref):
        px, py = chip_of(qi, pos_ref)
        return 2 * px + py, t, 0

    def body(pos_ref, a_ref, b_ref, o_ref):
        o_ref[...] = (a_ref[...] + b_ref[...]).astype(out_dtype)

    return pl.pallas_call(
        body, name=name,
        out_shape=jax.ShapeDtypeStruct((len(relations), m, n), out_dtype),
        grid_spec=pltpu.PrefetchScalarGridSpec(
            num_scalar_prefetch=1, grid=(len(relations), m // row_tile),
            in_specs=[pl.BlockSpec((None, row_tile, n), mine_map), pl.BlockSpec((None, row_tile, n), theirs_map)],
            out_specs=pl.BlockSpec((None, row_tile, n), lambda qi, t, pos_ref: (qi, t, 0))),
        compiler_params=pltpu.CompilerParams(dimension_semantics=("arbitrary", "arbitrary")),
    )(pos, mine, theirs)


PARTS = ((0, 512), (512, 640))


def _gather_norm_proj(pos, x2d, small_shard, norm_g, w_in_shard, w_out_shards, n_chunk):
    seq = x2d.shape[0]
    n_tiles = seq // TILE + 1
    tp = n_tiles * TILE
    n_parts = len(PARTS)
    widest = max(width for _, width in PARTS)
    units = [(s, u) for s in range(2) for u in range(n_parts)]
    for first in (2, 5):
        units += [(first + j, u) for u in range(n_parts) for j in range(2)] + [(first + 2, u) for u in range(n_parts)]
    n_units = len(units)
    n_steps = n_tiles + n_units
    chunk = tp // n_chunk

    def body(pos_ref, x_ref, g_ref, small_ref, win_ref, wa_ref, wb_ref, wo_ref,
             ht_ref, proj_ref, meta_ref, small_all, win_all, wa_all, wb_all, wo_all,
             h_all, wbuf, rbuf, small_buf, send_sems, recv_sems, local_sems, small_send, small_recv):
        g = pl.program_id(0)
        x, y, c = lax.axis_index("x"), lax.axis_index("y"), lax.axis_index("c")
        me, sibling = (x, y, c), (x, y, 1 - c)
        chips = [(1 - x, y), (x, 1 - y), (1 - x, 1 - y)]
        shards = (win_ref, wa_ref, wb_ref, wo_ref)
        gathered = (win_all, wa_all, wb_all, wo_all)
        n_arrays = len(shards)
        blocks = [me, sibling] + [(*chip, c) for chip in chips] + [(*chip, 1 - c) for chip in chips]

        def index(block):
            px, py, pc = block
            return 4 * px + 2 * py + pc

        def part(ref, a, u):
            return ref.at[:, pl.ds(PARTS[u][0], PARTS[u][1])] if a == 0 else ref

        def slot(a, block, u):
            return part(gathered[a].at[index(block)], a, u)

        def sem(a, k, u):
            return n_parts * k + u if a == 0 else 7 * n_parts + 7 * (a - 1) + k

        def copy(a, k, block, to, u=0, from_shard=False):
            return pltpu.make_async_remote_copy(
                src_ref=part(shards[a], a, u) if from_shard else slot(a, block, u), dst_ref=slot(a, block, u),
                send_sem=send_sems.at[sem(a, k, u)], recv_sem=recv_sems.at[sem(a, k, u)],
                device_id=to, device_id_type=MESH)

        def keep(a):
            return pltpu.make_async_copy(shards[a], gathered[a].at[index(me)], local_sems.at[a])

        def load(m):
            s, u = units[m]
            src = part(win_ref, 0, u) if s == 0 else slot(0, blocks[s], u)
            return pltpu.make_async_copy(src, wbuf.at[m % 2, :, 0:PARTS[u][1]], local_sems.at[n_arrays + m % 2])

        def store(m):
            s, u = units[m]
            col0 = pl.multiple_of(index(blocks[s]) * COLS + PARTS[u][0], LANES)
            return pltpu.make_async_copy(rbuf.at[m % 2, :, 0:PARTS[u][1]],
                                         proj_ref.at[:, pl.ds(col0, PARTS[u][1])], local_sems.at[n_arrays + 2 + m % 2])

        def by_x(a, u):
            return u == 0 if a == 0 else a < 3

        def relay(a, u=0):
            src, to = (blocks[3], blocks[2]) if by_x(a, u) else (blocks[2], blocks[3])
            return copy(a, 3, src, to, u)

        def arrive(m):
            s, u = units[m]
            if s == 1:
                copy(0, 0, sibling, me, u).wait_recv()
            elif 2 <= s <= 4:
                copy(0, s - 1, blocks[s], me, u).wait_recv()
                copy(0, s + 2, blocks[s], sibling, u).start()
                if s < 4 and by_x(0, u) == (s == 3):
                    relay(0, u).start()
            elif s >= 5:
                copy(0, s - 1, blocks[s], me, u).wait_recv()
                if u == 0:
                    for a in range(1, 4):
                        copy(a, s - 4, blocks[s - 3], me).wait_recv()
                        copy(a, s - 1, blocks[s - 3], sibling).start()
                        if s < 7 and by_x(a, 0) == (s == 6):
                            relay(a).start()

        targets = [sibling, blocks[2], blocks[3]]

        def small_copies():
            return _all_copies([small_ref], [small_all], small_send, small_recv)

        @pl.when(g == 0)
        def _():
            for cp in small_copies():
                cp.start()
            for a in range(n_arrays):
                keep(a).start()
            for u in range(n_parts):
                for k, to in enumerate(targets):
                    copy(0, k, me, to, u, from_shard=True).start()
            for a in range(1, 4):
                for k, to in enumerate(targets):
                    copy(a, k, me, to, from_shard=True).start()
            load(0).start()

        @pl.when(g == n_tiles - 2)
        def _():
            for cp in small_copies():
                cp.wait()
            fetch = pltpu.make_async_copy(small_all, small_buf, local_sems.at[n_arrays + 4])
            fetch.start()
            fetch.wait()
            meta_ref[0:TILE - N_META, :] = jnp.zeros((TILE - N_META, D_MODEL), F32)
            meta_ref[TILE - N_META:TILE, :] = jnp.concatenate([small_buf[d, 0:N_META, :] for d in range(N_DEV)], axis=1)

        @pl.when(g < n_tiles)
        def _():
            s0 = jnp.where(g == n_tiles - 1, meta_ref[...], x_ref[...])
            r = lax.rsqrt(_rowmean(s0 * s0) + EPS)
            h32 = (s0 * r) * g_ref[...]
            ht_ref[...] = h32.T.astype(BF16)
            h_all[pl.ds(pl.multiple_of(g * TILE, TILE), TILE), :] = h32.astype(BF16)

        for m in range(n_units):
            @pl.when(g == n_tiles + m)
            def _(m=m):
                load(m).wait()
                if m + 1 < n_units:
                    arrive(m + 1)
                    load(m + 1).start()
                if m >= 2:
                    store(m - 2).wait()

        m_now = jnp.maximum(g - n_tiles, 0)
        u_now = functools.reduce(jnp.add, [jnp.where(m_now == m, u, 0) for m, (_, u) in enumerate(units)])
        for u, (_, width) in enumerate(PARTS):
            @pl.when((g >= n_tiles) & (u_now == u))
            def _(width=width):
                w = wbuf[m_now % 2, :, 0:width]
                for r in range(n_chunk):
                    rbuf[m_now % 2, r * chunk:(r + 1) * chunk, 0:width] = _dot(h_all[r * chunk:(r + 1) * chunk, :], w)

        for m in range(n_units):
            @pl.when(g == n_tiles + m)
            def _(m=m):
                store(m).start()

        @pl.when(g == n_steps - 1)
        def _():
            store(n_units - 2).wait()
            store(n_units - 1).wait()
            for a in range(1, 4):
                copy(a, 0, sibling, me).wait_recv()
                for j in range(3):
                    copy(a, 4 + j, blocks[5 + j], me).wait_recv()
            for a in range(n_arrays):
                for u in range(n_parts if a == 0 else 1):
                    for k, to in enumerate(targets):
                        copy(a, k, me, to, u, from_shard=True).wait_send()
                    relay(a, u).wait_send()
                    for j in range(3):
                        copy(a, 4 + j, blocks[2 + j], sibling, u).wait_send()
                keep(a).wait()

    n_x = n_tiles - 1
    return pl.pallas_call(
        body, name="gather_norm_proj",
        out_shape=[jax.ShapeDtypeStruct((D_MODEL, tp), BF16), jax.ShapeDtypeStruct((tp, D_IN), F32),
                   jax.ShapeDtypeStruct((TILE, D_MODEL), F32), jax.ShapeDtypeStruct((N_DEV,) + small_shard.shape[1:], F32),
                   jax.ShapeDtypeStruct((N_DEV,) + w_in_shard.shape, BF16)]
                  + [jax.ShapeDtypeStruct((N_DEV,) + w.shape, BF16) for w in w_out_shards],
        grid_spec=pltpu.PrefetchScalarGridSpec(
            num_scalar_prefetch=1, grid=(n_steps,),
            in_specs=[pl.BlockSpec((TILE, D_MODEL), lambda g, pos_ref: (jnp.minimum(g, n_x - 1), 0)),
                      _VMEM, _ANY, _ANY, _ANY, _ANY, _ANY],
            out_specs=[pl.BlockSpec((D_MODEL, TILE), lambda g, pos_ref: (0, jnp.minimum(g, n_tiles - 1))),
                       _ANY, _VMEM, _ANY, _ANY, _ANY, _ANY, _ANY],
            scratch_shapes=[pltpu.VMEM((tp, D_MODEL), BF16), pltpu.VMEM((2, D_MODEL, widest), BF16),
                            pltpu.VMEM((2, tp, widest), F32), pltpu.VMEM((N_DEV,) + small_shard.shape[1:], F32),
                            pltpu.SemaphoreType.DMA((7 * n_parts + 21,)), pltpu.SemaphoreType.DMA((7 * n_parts + 21,)),
                            pltpu.SemaphoreType.DMA((9,)),
                            pltpu.SemaphoreType.DMA((N_DEV,)), pltpu.SemaphoreType.DMA((N_DEV,))]),
        compiler_params=pltpu.CompilerParams(dimension_semantics=("arbitrary",), vmem_limit_bytes=VMEM_LIMIT),
    )(pos, x2d, norm_g, small_shard, w_in_shard, *w_out_shards)


C_AVAL, C_AGLU, C_AZ, C_BB, C_BC, C_BX, C_BZ, C_GA, C_GB = (k * D_MODEL for k in range(9))
S_AZ, S_BB, S_BZ, S_GA, S_GB = (k * D_MODEL for k in range(5))


def _fused_pass(proj, x2d, tgt2d, meta_tile, conv_a_w, conv_a_b, ln_a_g, ln_a_b, b_a_out, conv_b_w, final_g,
                w_a, w_b, w_o, w_a_t, w_b_t, w_o_t, n_tiles):
    T = TILE
    tp = n_tiles * T
    inv_d = 1.0 / D_MODEL

    def block_of(tile):
        return jnp.where(tile == 0, n_tiles - 1, tile - 1)

    def cur(i):
        return block_of(jnp.minimum(i, n_tiles - 1))

    def prev(i):
        return block_of(jnp.clip(i - 1, 0, n_tiles - 1))

    def xblk(i):
        return jnp.maximum(jnp.minimum(i, n_tiles - 1) - 1, 0)

    def body(proj_ref, aprev, cprev, x_ref, tgt_ref, meta_ref, caw_ref, cab_ref, lng_ref, lnb_ref, bao_ref, cbw_ref,
             fg_ref, wa_ref, wb_ref, wo_ref, wat_ref, wbt_ref, wot_ref,
             dproj_ref, ds1_ref, lhs_ref, rhs_ref, small_ref,
             ua0_buf, cb_buf, dua1_buf, dc3_buf, stage, ua1_buf, c3_buf,
             dpa_buf, dpb_buf, dcaw8, dcbw8, shift_buf):
        i = pl.program_id(0)
        this, before = i % 2, 1 - i % 2

        @pl.when(i == 0)
        def _init():
            for buf in (ua0_buf, cb_buf, dua1_buf, dc3_buf, dcaw8, dcbw8):
                buf[...] = jnp.zeros(buf.shape, buf.dtype)
            small_ref[...] = jnp.zeros(small_ref.shape, F32)

        @pl.when(i >= 1)
        def _emit_stage():
            dproj_ref[:, C_AZ:C_BC] = stage[:, S_AZ:S_BZ]
            dproj_ref[:, C_BZ:D_IN] = stage[:, S_BZ:S_GB + D_MODEL]

        @pl.when(i < n_tiles)
        def _front():
            def conv_chunk(cc, carry):
                c0 = pl.multiple_of(cc * LANES, LANES)
                lanes = pl.ds(c0, LANES)

                def col(base):
                    return pl.ds(pl.multiple_of(base + cc * LANES, LANES), LANES)

                ua0 = proj_ref[:, col(C_AVAL)] * _sigmoid(proj_ref[:, col(C_AGLU)])
                ua0_buf[this, 0:HALO, lanes] = ua0_buf[before, T:T + HALO, lanes]
                ua0_buf[this, HALO:HALO + T, lanes] = ua0
                acc = jnp.broadcast_to(cab_ref[:, lanes], (T, LANES))
                lead = HALO - (CONV_A - 1)
                for r in range(SUBLANES):
                    taps = [k for k in range(CONV_A) if (k + lead) % SUBLANES == r]
                    rows = T + SUBLANES * max((k + lead) // SUBLANES for k in taps)
                    if r:
                        shift_buf[r, 0:rows, :] = ua0_buf[this, pl.ds(r, rows), lanes]
                    for k in taps:
                        q = (k + lead) // SUBLANES
                        if r:
                            win = shift_buf[r, SUBLANES * q:SUBLANES * q + T, :]
                        else:
                            win = ua0_buf[this, pl.ds(SUBLANES * q, T), lanes]
                        acc = acc + caw_ref[k:k + 1, lanes] * win
                ua1_buf[:, lanes] = acc
                cb = proj_ref[:, col(C_BC)] * proj_ref[:, col(C_BX)]
                cb_buf[this, 0:SUBLANES, lanes] = cb_buf[before, T:T + SUBLANES, lanes]
                cb_buf[this, SUBLANES:SUBLANES + T, lanes] = cb
                lead_b = SUBLANES - (CONV_B - 1)
                acc3 = cbw_ref[0:1, lanes] * cb_buf[this, pl.ds(lead_b, T), lanes]
                for k in range(1, CONV_B):
                    acc3 = acc3 + cbw_ref[k:k + 1, lanes] * cb_buf[this, pl.ds(lead_b + k, T), lanes]
                c3_buf[:, lanes] = acc3
                return carry

            lax.fori_loop(0, N_CHUNK, conv_chunk, 0)

            ua1 = ua1_buf[...]
            xc = ua1 - _rowmean(ua1)
            rstd = lax.rsqrt(_rowmean(xc * xc) + EPS)
            xhat = xc * rstd
            ua2 = xhat * lng_ref[...] + lnb_ref[...]
            sg2 = _sigmoid(ua2)
            ua3 = ua2 * sg2
            a_z = proj_ref[:, C_AZ:C_AZ + D_MODEL]
            sz = _sigmoid(a_z)
            silu_az = a_z * sz
            lhs_ref[0] = (ua3 * silu_az).astype(BF16)
            b_z = proj_ref[:, C_BZ:C_BZ + D_MODEL]
            sbz = _sigmoid(b_z)
            silu_bz = b_z * sbz
            b_b = proj_ref[:, C_BB:C_BB + D_MODEL]
            c3 = c3_buf[...]
            ub = b_b * c3
            lhs_ref[1] = (ub * silu_bz).astype(BF16)

            ya = _dot(lhs_ref[0], wa_ref[...]) + bao_ref[...]
            yb = _dot(lhs_ref[1], wb_ref[...])
            sga = _sigmoid(proj_ref[:, C_GA:C_GA + D_MODEL])
            sgb = _sigmoid(proj_ref[:, C_GB:C_GB + D_MODEL])
            m_b = (sga * ya + sgb * yb).astype(BF16)
            lhs_ref[2] = m_b
            s0 = jnp.where(i == 0, meta_ref[...], x_ref[...])
            s1 = s0 + _dot(m_b, wo_ref[...])
            r1 = lax.rsqrt(_rowmean(s1 * s1) + EPS)
            y = (s1 * r1) * fg_ref[...]
            is_token = (i >= 1).astype(F32)
            err = (y - tgt_ref[...]) * is_token
            small_ref[ROW_LOSS:ROW_LOSS + 1, :] += (0.5 * inv_d) * _colsum(err * err)
            dy = err * inv_d
            small_ref[ROW_FINAL_G:ROW_FINAL_G + 1, :] += _colsum(dy * (s1 * r1))
            gy = dy * fg_ref[...]
            ds1 = r1 * gy - s1 * ((r1 * r1 * r1) * _rowmean(gy * s1))
            ds1_ref[...] = ds1
            ds1_b = ds1.astype(BF16)
            rhs_ref[2] = ds1_b
            dm = _dot(ds1_b, wot_ref[...])
            dya = dm * sga
            dyb = dm * sgb
            stage[:, S_GA:S_GA + D_MODEL] = (dya * ya * (1.0 - sga)).astype(BF16)
            stage[:, S_GB:S_GB + D_MODEL] = (dyb * yb * (1.0 - sgb)).astype(BF16)
            small_ref[ROW_B_A_OUT:ROW_B_A_OUT + 1, :] += _colsum(dya)
            dya_b = dya.astype(BF16)
            dyb_b = dyb.astype(BF16)
            rhs_ref[0] = dya_b
            rhs_ref[1] = dyb_b
            dpa_buf[...] = _dot(dya_b, wat_ref[...])
            dpb_buf[...] = _dot(dyb_b, wbt_ref[...])

            dpa = dpa_buf[...]
            stage[:, S_AZ:S_AZ + D_MODEL] = (dpa * ua3 * (sz + silu_az * (1.0 - sz))).astype(BF16)
            dua2 = dpa * silu_az * (sg2 + ua3 * (1.0 - sg2))
            small_ref[ROW_LN_G:ROW_LN_G + 1, :] += _colsum(dua2 * xhat)
            small_ref[ROW_LN_B:ROW_LN_B + 1, :] += _colsum(dua2)
            dxh = dua2 * lng_ref[...]
            dua1 = rstd * (dxh - _rowmean(dxh) - xhat * _rowmean(dxh * xhat))
            small_ref[ROW_CONV_A_B:ROW_CONV_A_B + 1, :] += _colsum(dua1)
            dua1_buf[this, 0:T, :] = dua1
            dua1_buf[before, T:T + HALO, :] = dua1[0:HALO]
            dpb = dpb_buf[...]
            stage[:, S_BZ:S_BZ + D_MODEL] = (dpb * ub * (sbz + silu_bz * (1.0 - sbz))).astype(BF16)
            dub = dpb * silu_bz
            stage[:, S_BB:S_BB + D_MODEL] = (dub * c3).astype(BF16)
            dc3 = dub * b_b
            dc3_buf[this, 0:T, :] = dc3
            dc3_buf[before, T:T + SUBLANES, :] = dc3[0:SUBLANES]

        @pl.when(i == n_tiles)
        def _no_later_tile():
            dua1_buf[before, T:T + HALO, :] = jnp.zeros((HALO, D_MODEL), F32)
            dc3_buf[before, T:T + SUBLANES, :] = jnp.zeros((SUBLANES, D_MODEL), F32)

        @pl.when(i >= 1)
        def _lagged():
            def convt_chunk(cc, carry):
                c0 = pl.multiple_of(cc * LANES, LANES)
                lanes = pl.ds(c0, LANES)

                def col(base):
                    return pl.ds(pl.multiple_of(base + cc * LANES, LANES), LANES)

                ua0 = ua0_buf[before, HALO:HALO + T, lanes]
                acc = jnp.zeros((T, LANES), F32)
                for r in range(SUBLANES):
                    shifts = [j for j in range(CONV_A) if j % SUBLANES == r]
                    rows = T + shifts[-1] - r
                    if r:
                        shift_buf[r, 0:rows, :] = dua1_buf[before, pl.ds(r, rows), lanes]
                    for j in shifts:
                        k = CONV_A - 1 - j
                        if r:
                            later = shift_buf[r, j - r:j - r + T, :]
                        else:
                            later = dua1_buf[before, pl.ds(j, T), lanes]
                        acc = acc + caw_ref[k:k + 1, lanes] * later
                        dcaw8[SUBLANES * k:SUBLANES * (k + 1), lanes] += _fold8(ua0 * later)
                a_val = aprev[:, col(0)]
                sg = _sigmoid(aprev[:, col(D_MODEL)])
                dproj_ref[:, col(C_AVAL)] = (acc * sg).astype(BF16)
                dproj_ref[:, col(C_AGLU)] = (acc * a_val * (sg * (1.0 - sg))).astype(BF16)

                cb = cb_buf[before, SUBLANES:SUBLANES + T, lanes]
                acc3 = jnp.zeros((T, LANES), F32)
                for j in range(CONV_B):
                    k = CONV_B - 1 - j
                    later = dc3_buf[before, pl.ds(j, T), lanes]
                    acc3 = acc3 + cbw_ref[k:k + 1, lanes] * later
                    dcbw8[SUBLANES * k:SUBLANES * (k + 1), lanes] += _fold8(cb * later)
                dproj_ref[:, col(C_BC)] = (acc3 * cprev[:, col(D_MODEL)]).astype(BF16)
                dproj_ref[:, col(C_BX)] = (acc3 * cprev[:, col(0)]).astype(BF16)
                return carry

            lax.fori_loop(0, N_CHUNK, convt_chunk, 0)

        @pl.when(i == n_tiles)
        def _finish():
            for k in range(CONV_A):
                small_ref[ROW_CONV_A_W + k:ROW_CONV_A_W + k + 1, :] = _colsum(dcaw8[SUBLANES * k:SUBLANES * (k + 1), :])
            for k in range(CONV_B):
                small_ref[ROW_CONV_B_W + k:ROW_CONV_B_W + k + 1, :] = _colsum(dcbw8[SUBLANES * k:SUBLANES * (k + 1), :])

    pair = 2 * D_MODEL
    return pl.pallas_call(
        body, name="fused_pass", grid=(n_tiles + 1,),
        out_shape=[
            jax.ShapeDtypeStruct((tp, D_IN), BF16),
            jax.ShapeDtypeStruct((tp, D_MODEL), F32),
            jax.ShapeDtypeStruct((3, tp, D_MODEL), BF16),
            jax.ShapeDtypeStruct((3, tp, D_MODEL), BF16),
            jax.ShapeDtypeStruct((SMALL_A_ROWS, D_MODEL), F32),
        ],
        in_specs=[
            pl.BlockSpec((T, D_IN), lambda i: (cur(i), 0)),
            pl.BlockSpec((T, pair), lambda i: (prev(i), C_AVAL // pair)),
            pl.BlockSpec((T, pair), lambda i: (prev(i), C_BC // pair)),
            pl.BlockSpec((T, D_MODEL), lambda i: (xblk(i), 0)),
            pl.BlockSpec((T, D_MODEL), lambda i: (xblk(i), 0)),
            _VMEM, _VMEM, _VMEM, _VMEM, _VMEM, _VMEM, _VMEM, _VMEM,
            *[_resident((D_MODEL, D_MODEL)) for _ in range(6)],
        ],
        out_specs=[
            pl.BlockSpec((T, D_IN), lambda i: (prev(i), 0)),
            pl.BlockSpec((T, D_MODEL), lambda i: (cur(i), 0)),
            pl.BlockSpec((3, T, D_MODEL), lambda i: (0, cur(i), 0)),
            pl.BlockSpec((3, T, D_MODEL), lambda i: (0, cur(i), 0)),
            _VMEM,
        ],
        scratch_shapes=[
            pltpu.VMEM((2, HALO + T, D_MODEL), F32),
            pltpu.VMEM((2, SUBLANES + T, D_MODEL), F32),
            pltpu.VMEM((2, T + HALO, D_MODEL), F32),
            pltpu.VMEM((2, T + SUBLANES, D_MODEL), F32),
            pltpu.VMEM((T, 5 * D_MODEL), BF16),
            pltpu.VMEM((T, D_MODEL), F32),
            pltpu.VMEM((T, D_MODEL), F32),
            pltpu.VMEM((T, D_MODEL), F32),
            pltpu.VMEM((T, D_MODEL), F32),
            pltpu.VMEM((32 * SUBLANES, D_MODEL), F32),
            pltpu.VMEM((SUBLANES * SUBLANES, D_MODEL), F32),
            pltpu.VMEM((SUBLANES, T + HALO, LANES), F32),
        ],
        compiler_params=pltpu.CompilerParams(dimension_semantics=("arbitrary",), vmem_limit_bytes=VMEM_LIMIT),
    )(proj, proj, proj, x2d, tgt2d, meta_tile, conv_a_w, conv_a_b, ln_a_g, ln_a_b, b_a_out, conv_b_w, final_g,
      w_a, w_b, w_o, w_a_t, w_b_t, w_o_t)


def _input_bwd(dproj, ds1, x2d, meta_tile, norm_g, w_in_all, row_tile):
    seq = x2d.shape[0]
    n_steps = seq // row_tile
    meta_block = seq // TILE

    def backward(dp_ref, ds1_ref, s0_ref, g_ref, w_ref, out_ref, vec_ref):
        dh = _dot_nt(dp_ref[:, 0:COLS], w_ref[0])
        for j in range(1, N_DEV):
            dh = dh + _dot_nt(dp_ref[:, j * COLS:(j + 1) * COLS], w_ref[j])
        s0v = s0_ref[...]
        r = lax.rsqrt(_rowmean(s0v * s0v) + EPS)
        gh = dh * g_ref[...]
        out_ref[...] = ds1_ref[...] + r * gh - s0v * ((r * r * r) * _rowmean(gh * s0v))
        vec_ref[ROW_NORM_G:ROW_NORM_G + 1, :] += _colsum(dh * (s0v * r))

    def body(dp_ref, ds1_ref, x_ref, dpm_ref, ds1m_ref, meta_ref, g_ref, w_ref, gx_ref, small_ref, gmeta_buf):
        t = pl.program_id(0)

        @pl.when(t == 0)
        def _():
            small_ref[...] = jnp.zeros(small_ref.shape, F32)

        backward(dp_ref, ds1_ref, x_ref, g_ref, w_ref, gx_ref, small_ref)

        @pl.when(t == n_steps - 1)
        def _():
            backward(dpm_ref, ds1m_ref, meta_ref, g_ref, w_ref, gmeta_buf, small_ref)
            small_ref[ROW_META:ROW_META + N_META, :] = gmeta_buf[TILE - N_META:TILE, :]

    return pl.pallas_call(
        body, name="input_bwd", grid=(n_steps,),
        out_shape=[jax.ShapeDtypeStruct(x2d.shape, F32), jax.ShapeDtypeStruct((SMALL_B_ROWS, D_MODEL), F32)],
        in_specs=[pl.BlockSpec((row_tile, D_IN), lambda t: (t, 0)),
                  pl.BlockSpec((row_tile, D_MODEL), lambda t: (t, 0)),
                  pl.BlockSpec((row_tile, D_MODEL), lambda t: (t, 0)),
                  pl.BlockSpec((TILE, D_IN), lambda t: (meta_block, 0)),
                  pl.BlockSpec((TILE, D_MODEL), lambda t: (meta_block, 0)),
                  _VMEM, _VMEM, _resident((N_DEV, D_MODEL, COLS))],
        out_specs=[pl.BlockSpec((row_tile, D_MODEL), lambda t: (t, 0)), _VMEM],
        scratch_shapes=[pltpu.VMEM((TILE, D_MODEL), F32)],
        compiler_params=pltpu.CompilerParams(dimension_semantics=("arbitrary",), vmem_limit_bytes=VMEM_LIMIT),
    )(dproj, ds1, x2d, dproj, ds1, meta_tile, norm_g, w_in_all)


def _grad_w_in_half(pos, h_t, dproj, k_tile, other_side, rides, name, after=None, add_to=None, narrow=False):
    tp = h_t.shape[1]
    n_k = tp // k_tile
    order = [] if after is None else [after]
    summing = add_to is not None
    assert not (summing and narrow)

    def column_block(q, k, pos_ref):
        return k, 2 * q + (1 - pos_ref[2] if other_side else pos_ref[2])

    def body(pos_ref, h_ref, dp_ref, *refs):
        acc = refs[-2] if summing else refs[-1]

        @pl.when(pl.program_id(1) == 0)
        def _():
            acc[...] = refs[0][...].astype(F32) if summing else jnp.zeros(acc.shape, F32)

        acc[...] += _dot(h_ref[...], dp_ref[...])

        if summing or narrow:
            @pl.when(pl.program_id(1) == n_k - 1)
            def _():
                (refs[-1] if summing else refs[-2])[...] = acc[...].astype(BF16)

    ride = [a for _, arrays in rides for a in arrays]
    n_arr = len(ride)
    ride_shapes, ride_sems = _ride_shapes(rides)
    block = (None, D_MODEL, COLS)
    extra_in = [add_to] if summing else []
    extra_in_specs = [pl.BlockSpec(block, lambda q, k, pos_ref: (q, 0, 0))] if summing else []
    extra_out = [jax.ShapeDtypeStruct((4, D_MODEL, COLS), BF16)] if summing else []
    extra_out_specs = [pl.BlockSpec(block, lambda q, k, pos_ref: (q ^ (2 * pos_ref[0] + pos_ref[1]), 0, 0))] \
        if summing else []
    body = _riding(body, 3 + len(extra_in) + len(order), 1 + len(extra_out), rides,
                   lambda: (pl.program_id(0) == 0) & (pl.program_id(1) == 0),
                   lambda: (pl.program_id(0) == 3) & (pl.program_id(1) == n_k - 1))
    return pl.pallas_call(
        body, name=name,
        out_shape=[jax.ShapeDtypeStruct((4, D_MODEL, COLS), BF16 if narrow else F32)] + extra_out + ride_shapes,
        grid_spec=pltpu.PrefetchScalarGridSpec(
            num_scalar_prefetch=1, grid=(4, n_k),
            in_specs=[pl.BlockSpec((D_MODEL, k_tile), lambda q, k, pos_ref: (0, k)),
                      pl.BlockSpec((k_tile, COLS), column_block)] + extra_in_specs + [_ANY] * (len(order) + n_arr),
            out_specs=[pl.BlockSpec(block, lambda q, k, pos_ref: (q, 0, 0))] + extra_out_specs + [_ANY] * n_arr,
            scratch_shapes=([pltpu.VMEM((D_MODEL, COLS), F32)] if narrow else []) + ride_sems),
        compiler_params=pltpu.CompilerParams(dimension_semantics=("arbitrary", "arbitrary"),
                                             vmem_limit_bytes=VMEM_LIMIT),
    )(pos, h_t, dproj, *extra_in, *order, *ride)


def _grad_w_out(lhs, rhs, k_tile, after):
    tp = lhs.shape[1]

    def body(a_ref, b_ref, after_ref, o_ref):
        @pl.when(pl.program_id(1) == 0)
        def _():
            o_ref[...] = jnp.zeros(o_ref.shape, F32)

        o_ref[...] += _dot_tn(a_ref[...], b_ref[...]).reshape(N_DEV, ROWS_OUT, D_MODEL)

    return pl.pallas_call(
        body, name="grad_w_out", grid=(3, tp // k_tile),
        out_shape=jax.ShapeDtypeStruct((N_DEV, 3, ROWS_OUT, D_MODEL), F32),
        in_specs=[pl.BlockSpec((None, k_tile, D_MODEL), lambda w, k: (w, k, 0)),
                  pl.BlockSpec((None, k_tile, D_MODEL), lambda w, k: (w, k, 0)), _ANY],
        out_specs=pl.BlockSpec((N_DEV, None, ROWS_OUT, D_MODEL), lambda w, k: (0, w, 0, 0)),
        compiler_params=pltpu.CompilerParams(dimension_semantics=("arbitrary", "arbitrary"),
                                             vmem_limit_bytes=VMEM_LIMIT),
    )(lhs, rhs, after)


def _adamw_math(w, g, m, v):
    m = ADAM_B1 * m + (1.0 - ADAM_B1) * g
    v = ADAM_B2 * v + (1.0 - ADAM_B2) * (g * g)
    m_hat = m / (1.0 - ADAM_B1 ** ADAM_STEP)
    v_hat = v / (1.0 - ADAM_B2 ** ADAM_STEP)
    delta = -ADAM_LR * (m_hat / (jnp.sqrt(v_hat) + ADAM_EPS) + ADAM_WD * w)
    return delta, m, v


def _adamw_sharded(pos, mine, theirs, landed, weights, row_tile, name, after=None):
    order = [] if after is None else [after]
    rows, n = weights[0][0].shape
    n_slots = mine.shape[0]
    per_shard = rows // row_tile
    assert per_shard == 1 or len(weights) == 1

    def mine_map(j, t, pos_ref):
        chip = 2 * pos_ref[0] + pos_ref[1]
        return (2 * chip + pos_ref[2] if n_slots == N_DEV else chip), j * per_shard + t, 0

    def theirs_map(j, t, pos_ref):
        return 2 * pos_ref[0] + pos_ref[1], j * per_shard + t, 0

    def body(pos_ref, mine_ref, *refs):
        if theirs is not None:
            g = mine_ref[...] + refs[0][...]
            refs = refs[1:]
        else:
            g = mine_ref[...]
        land_ref, refs = refs[0], refs[1:]
        ins, outs = refs[:3 * len(weights)], refs[3 * len(weights) + len(order):]
        for k in range(3):
            g = g + land_ref[k].astype(F32)
        for j in range(len(weights)):
            @pl.when(pl.program_id(0) == j)
            def _(j=j):
                w_ref, m_ref, v_ref = ins[3 * j:3 * j + 3]
                delta, m_new, v_new = _adamw_math(w_ref[...], g, m_ref[...], v_ref[...])
                for ref, val in zip(outs[4 * j:4 * j + 4], (g, delta, m_new, v_new)):
                    ref[...] = val

    tile = pl.BlockSpec((row_tile, n), lambda j, t, pos_ref: (t, 0))
    res = pl.pallas_call(
        body, name=name,
        out_shape=[jax.ShapeDtypeStruct((rows, n), F32)] * (4 * len(weights)),
        grid_spec=pltpu.PrefetchScalarGridSpec(
            num_scalar_prefetch=1, grid=(len(weights), per_shard),
            in_specs=[pl.BlockSpec((None, row_tile, n), mine_map)]
            + ([pl.BlockSpec((None, row_tile, n), theirs_map)] if theirs is not None else [])
            + [pl.BlockSpec((3, row_tile, n), lambda j, t, pos_ref: (0, j * per_shard + t, 0))]
            + [tile] * (3 * len(weights)) + [_ANY] * len(order),
            out_specs=[tile] * (4 * len(weights))),
        compiler_params=pltpu.CompilerParams(dimension_semantics=("arbitrary", "arbitrary")),
    )(pos, mine, *([theirs] if theirs is not None else []), landed, *[a for wmv in weights for a in wmv], *order)
    return [res[4 * j:4 * j + 4] for j in range(len(weights))]


def _adamw_small(gathered, gathered_cols, params):
    n_par, n_src = len(params), len(gathered)

    def body(*refs):
        g_refs, gc_refs = refs[:n_src], refs[n_src:2 * n_src]
        ins = refs[2 * n_src:2 * n_src + 3 * n_par]
        outs = refs[2 * n_src + 3 * n_par:]
        loss_ref = outs[4 * n_par]

        def reduced(ref, row, n_rows):
            g = ref[0, row:row + n_rows, :]
            for d in range(1, N_DEV):
                g = g + ref[d, row:row + n_rows, :]
            return g

        for p, (src, row, n_rows, sharded, _, _, _) in enumerate(params):
            g = reduced((gc_refs if sharded else g_refs)[src], row, n_rows)
            w_ref, m_ref, v_ref = ins[3 * p:3 * p + 3]
            delta, m_new, v_new = _adamw_math(w_ref[...], g, m_ref[...], v_ref[...])
            outs[4 * p][...] = g
            outs[4 * p + 1][...] = delta
            outs[4 * p + 2][...] = m_new
            outs[4 * p + 3][...] = v_new
        loss = jnp.sum(reduced(g_refs[0], ROW_LOSS, 1), axis=1, keepdims=True)
        loss_ref[...] = jnp.broadcast_to(loss, loss_ref.shape)

    out_shape = []
    for (_, _, _, _, w, _, _) in params:
        out_shape += [jax.ShapeDtypeStruct(w.shape, F32)] * 4
    out_shape.append(jax.ShapeDtypeStruct((1, LANES), F32))
    flat = [a for (_, _, _, _, w, m, v) in params for a in (w, m, v)]
    return pl.pallas_call(
        body, name="adamw_small", out_shape=out_shape,
        in_specs=[_VMEM] * (2 * n_src + len(flat)), out_specs=[_VMEM] * len(out_shape),
    )(*gathered, *gathered_cols, *flat)


def _pad_rows(a, rows):
    return jnp.concatenate([a, jnp.zeros((rows - a.shape[0], a.shape[1]), a.dtype)], axis=0)


def kernel(x, meta_tokens, norm_g, w_in, conv_a_w, conv_a_b, ln_a_g, ln_a_b, w_a_out, b_a_out, conv_b_w, w_b_out, w_out, final_g, loss_target, m_meta_tokens, m_norm_g, m_w_in, m_conv_a_w, m_conv_a_b, m_ln_a_g, m_ln_a_b, m_w_a_out, m_b_a_out, m_conv_b_w, m_w_b_out, m_w_out, m_final_g, v_meta_tokens, v_norm_g, v_w_in, v_conv_a_w, v_conv_a_b, v_ln_a_g, v_ln_a_b, v_w_a_out, v_b_a_out, v_conv_b_w, v_w_b_out, v_w_out, v_final_g):
    seq = x.shape[1]
    assert x.shape == (1, seq, D_MODEL) and seq % TILE == 0 and w_in.shape == (1, D_MODEL, COLS)
    n_tiles = seq // TILE + 1
    tp = n_tiles * TILE
    pos = jnp.stack([lax.axis_index("x"), lax.axis_index("y"), lax.axis_index("c")]).astype(jnp.int32)
    me = 4 * pos[0] + 2 * pos[1] + pos[2]
    x2d = x[0]
    tgt2d = loss_target[0]

    small = jnp.concatenate([meta_tokens, _pad_rows(conv_a_w[0], 32), _pad_rows(conv_b_w[0], SUBLANES)], axis=0)
    final_g2 = final_g.reshape(1, D_MODEL)

    w_out_shards = [w[0].astype(BF16) for w in (w_a_out, w_b_out, w_out)]
    h_t, proj, meta_tile, small_params, w_in_all, *w_out_all = _gather_norm_proj(
        pos, x2d, small[None], norm_g, w_in[0].astype(BF16), w_out_shards, 3)
    small_params = small_params.transpose(1, 0, 2).reshape(small.shape[0], D_MODEL)
    conv_a_full, conv_b_full = small_params[N_META:N_META + 32], small_params[N_META + 32:]
    w_out_all = [w.reshape(D_MODEL, D_MODEL) for w in w_out_all]
    w_out_all_t = [w.T for w in w_out_all]
    dproj, ds1, lhs, rhs, small_a = _fused_pass(
        proj, x2d, tgt2d, meta_tile, conv_a_full, conv_a_b, ln_a_g, ln_a_b, b_a_out, conv_b_full, final_g2,
        w_out_all[0], w_out_all[1], w_out_all[2], w_out_all_t[0], w_out_all_t[1], w_out_all_t[2], n_tiles)
    k_tile = tp // 3
    gw_far, small_a_all = _grad_w_in_half(pos, h_t, dproj, k_tile, True, [("all", (small_a[None],))], "grad_w_in_far",
                                          narrow=True)
    sems, sent, landing, token = _start_exchanges([("sibling_half", (gw_far,))], "rs_far_start")
    gw_out = _grad_w_out(lhs, rhs, k_tile, token).reshape(N_DEV, 3 * ROWS_OUT, D_MODEL)
    (their_in,) = _wait_exchanges([("sibling_half", 1)], sems, sent, landing, gw_out, "rs_far_wait")
    sems_o, sent_o, landing_o, token = _start_exchanges([("sibling", (gw_out,))], "rs_out_start")
    gw_near, parts_in = _grad_w_in_half(pos, h_t, dproj, k_tile, False, [], "grad_w_in_near", after=token,
                                        add_to=their_in)
    sems_i, sent_i, landing_i, token = _start_exchanges([("chips_by_relation", (parts_in,))], "rs_chips_in_start")
    gw_out, their_out = _wait_exchanges([("sibling", 1)], sems_o, sent_o, landing_o, token, "rs_out_wait",
                                        keep_sources=True)
    parts_out = _chip_partial(pos, gw_out, their_out, (1, 2, 3), BF16, ROWS_OUT, "rs_parts_w_out")
    sems_o, sent_o, landing_o, token = _start_exchanges([("chips", (parts_out,))], "rs_chips_out_start")
    grad_x, small_b = _input_bwd(dproj, ds1, x2d, meta_tile, norm_g + token[0, 0], w_in_all, min(512, seq))

    sems_s, sent_s, landing_s, token = _start_exchanges([("all", (small_b[None],))], "gather_small_grads_start")
    (land_in,) = _wait_exchanges([("chips_by_relation", 1)], sems_i, sent_i, landing_i, token, "rs_chips_in_wait")
    (res_in,) = _adamw_sharded(pos, gw_near, None, land_in, [(w_in[0], m_w_in[0], v_w_in[0])], 128, "adamw_w_in")
    (land_out,) = _wait_exchanges([("chips", 1)], sems_o, sent_o, landing_o, res_in[0], "rs_chips_out_wait")
    res_out = _adamw_sharded(
        pos, gw_out, their_out, land_out,
        [(w_a_out[0], m_w_a_out[0], v_w_a_out[0]), (w_b_out[0], m_w_b_out[0], v_w_b_out[0]),
         (w_out[0], m_w_out[0], v_w_out[0])], ROWS_OUT, "adamw_w_out")
    (small_b_all,) = _wait_exchanges([("all", 1)], sems_s, sent_s, landing_s, res_out[2][0], "gather_small_grads_wait")
    small_grads = [small_a_all, small_b_all]
    small_cols = [lax.dynamic_slice_in_dim(g, me * LANES, LANES, axis=2) for g in small_grads]
    params = [
        (1, ROW_META, N_META, True, meta_tokens, m_meta_tokens, v_meta_tokens),
        (1, ROW_NORM_G, 1, False, norm_g, m_norm_g, v_norm_g),
        (0, ROW_CONV_A_W, CONV_A, True, conv_a_w[0], m_conv_a_w[0], v_conv_a_w[0]),
        (0, ROW_CONV_A_B, 1, False, conv_a_b, m_conv_a_b, v_conv_a_b),
        (0, ROW_LN_G, 1, False, ln_a_g, m_ln_a_g, v_ln_a_g),
        (0, ROW_LN_B, 1, False, ln_a_b, m_ln_a_b, v_ln_a_b),
        (0, ROW_B_A_OUT, 1, False, b_a_out, m_b_a_out, v_b_a_out),
        (0, ROW_CONV_B_W, CONV_B, True, conv_b_w[0], m_conv_b_w[0], v_conv_b_w[0]),
        (0, ROW_FINAL_G, 1, False, final_g2, m_final_g.reshape(1, D_MODEL), v_final_g.reshape(1, D_MODEL)),
    ]
    res_small = _adamw_small(small_grads, small_cols, params)
    loss = res_small[-1][0, 0]

    def small_res(p, kind, shape):
        return res_small[4 * p + kind].reshape(shape)

    per_weight = []
    for kind in range(4):
        per_weight.append([
            small_res(0, kind, meta_tokens.shape),
            small_res(1, kind, norm_g.shape),
            res_in[kind].reshape(w_in.shape),
            small_res(2, kind, conv_a_w.shape),
            small_res(3, kind, conv_a_b.shape),
            small_res(4, kind, ln_a_g.shape),
            small_res(5, kind, ln_a_b.shape),
            res_out[0][kind].reshape(w_a_out.shape),
            small_res(6, kind, b_a_out.shape),
            small_res(7, kind, conv_b_w.shape),
            res_out[1][kind].reshape(w_b_out.shape),
            res_out[2][kind].reshape(w_out.shape),
            small_res(8, kind, final_g.shape),
        ])
    return (loss, grad_x.reshape(x.shape), *per_weight[0], *per_weight[1], *per_weight[2], *per_weight[3])
```

```python
import functools

import jax
import jax.numpy as jnp
from jax import lax
from jax.experimental import pallas as pl
from jax.experimental.pallas import tpu as pltpu

D_MODEL = 1024
N_META = 16
N_DEV = 8
D_IN = 9 * D_MODEL
COLS = D_IN // N_DEV
ROWS_OUT = D_MODEL // N_DEV
CONV_A = 31
CONV_B = 3
EPS = 1e-6

ADAM_LR = 0.001
ADAM_B1 = 0.9
ADAM_B2 = 0.999
ADAM_EPS = 1e-08
ADAM_WD = 0.01
ADAM_STEP = 10

TILE = 128
LANES = 128
N_CHUNK = D_MODEL // LANES
HALO = 32
SUBLANES = 8
VMEM_LIMIT = 56 * 1024 * 1024

ROW_FINAL_G, ROW_B_A_OUT, ROW_LN_G, ROW_LN_B, ROW_CONV_A_B, ROW_LOSS = 0, 1, 2, 3, 4, 5
ROW_CONV_A_W, ROW_CONV_B_W, SMALL_A_ROWS = 8, 40, 48
ROW_NORM_G, ROW_META, SMALL_B_ROWS = 0, 8, 24

MESH = pl.DeviceIdType.MESH
_ANY = pl.BlockSpec(memory_space=pl.ANY)
_VMEM = pl.BlockSpec(memory_space=pltpu.VMEM)
BF16 = jnp.bfloat16
F32 = jnp.float32


def _resident(shape):
    return pl.BlockSpec(shape, lambda *_: (0,) * len(shape), pipeline_mode=pl.Buffered(1))


def _sigmoid(v):
    return jax.nn.sigmoid(v)


def _dot(a, b):
    return jnp.dot(a, b, preferred_element_type=F32)


def _dot_nt(a, b):
    return lax.dot_general(a, b, (((1,), (1,)), ((), ())), preferred_element_type=F32)


def _dot_tn(a, b):
    return lax.dot_general(a, b, (((0,), (0,)), ((), ())), preferred_element_type=F32)


def _colsum(v):
    return jnp.sum(v, axis=0, keepdims=True)


def _rowmean(v):
    parts = [v[:, LANES * c:LANES * (c + 1)] for c in range(v.shape[1] // LANES)]
    return jnp.sum(functools.reduce(jnp.add, parts), axis=-1, keepdims=True) * (1.0 / v.shape[1])


def _fold8(v):
    parts = [v[SUBLANES * g:SUBLANES * (g + 1)] for g in range(v.shape[0] // SUBLANES)]
    return functools.reduce(jnp.add, parts)


def _sibling_copies(srcs, dsts, send_sems, recv_sems):
    x, y, c = lax.axis_index("x"), lax.axis_index("y"), lax.axis_index("c")
    return [pltpu.make_async_remote_copy(
        src_ref=src.at[2 * q + (1 - c)], dst_ref=dst.at[q],
        send_sem=send_sems.at[4 * a + q], recv_sem=recv_sems.at[4 * a + q],
        device_id=(x, y, 1 - c), device_id_type=MESH)
        for a, (src, dst) in enumerate(zip(srcs, dsts)) for q in range(4)]


def _chip_copies(srcs, dsts, send_sems, recv_sems):
    x, y, c = lax.axis_index("x"), lax.axis_index("y"), lax.axis_index("c")
    targets = [(x, 1 - y, c), (1 - x, y, c), (1 - x, 1 - y, c)]
    return [pltpu.make_async_remote_copy(
        src_ref=src.at[k], dst_ref=dst.at[k],
        send_sem=send_sems.at[3 * a + k], recv_sem=recv_sems.at[3 * a + k],
        device_id=targets[k], device_id_type=MESH)
        for a, (src, dst) in enumerate(zip(srcs, dsts)) for k in range(3)]


def _sibling_half_copies(srcs, dsts, send_sems, recv_sems):
    x, y, c = lax.axis_index("x"), lax.axis_index("y"), lax.axis_index("c")
    return [pltpu.make_async_remote_copy(
        src_ref=src.at[q], dst_ref=dst.at[q],
        send_sem=send_sems.at[4 * a + q], recv_sem=recv_sems.at[4 * a + q],
        device_id=(x, y, 1 - c), device_id_type=MESH)
        for a, (src, dst) in enumerate(zip(srcs, dsts)) for q in range(4)]


def _all_copies(srcs, dsts, send_sems, recv_sems):
    x, y, c = lax.axis_index("x"), lax.axis_index("y"), lax.axis_index("c")
    mine = 4 * x + 2 * y + c
    copies = []
    for a, (src, dst) in enumerate(zip(srcs, dsts)):
        copies.append(pltpu.make_async_copy(src.at[0], dst.at[mine], send_sems.at[N_DEV * a]))
        for k in range(1, N_DEV):
            copies.append(pltpu.make_async_remote_copy(
                src_ref=src.at[0], dst_ref=dst.at[mine],
                send_sem=send_sems.at[N_DEV * a + k], recv_sem=recv_sems.at[N_DEV * a + k],
                device_id=(x ^ (k >> 2), y ^ ((k >> 1) & 1), c ^ (k & 1)), device_id_type=MESH))
    return copies


def _chip_copies_by_relation(srcs, dsts, send_sems, recv_sems):
    return _chip_copies([src.at[pl.ds(1, 3)] for src in srcs], dsts, send_sems, recv_sems)


_EXCHANGES = {"sibling": (4, _sibling_copies, 4), "sibling_half": (4, _sibling_half_copies, 4),
              "chips": (3, _chip_copies, 3), "chips_by_relation": (3, _chip_copies_by_relation, 3),
              "all": (N_DEV, _all_copies, N_DEV)}


def _exchange_shapes(kind, arrays):
    per_array, _, slots = _EXCHANGES[kind]
    out_shape = [jax.ShapeDtypeStruct((slots,) + a.shape[1:], a.dtype) for a in arrays]
    sems = [pltpu.SemaphoreType.DMA((per_array * len(arrays),))] * 2
    return out_shape, sems


def _ride_shapes(rides):
    shapes, sems = [], []
    for kind, arrays in rides:
        ride_shapes, ride_sems = _exchange_shapes(kind, arrays)
        shapes += ride_shapes
        sems += ride_sems
    return shapes, sems


def _riding(body, n_in, n_out, rides, is_first, is_last):
    counts = [len(arrays) for _, arrays in rides]
    n_arr = sum(counts)

    def wrapped(*refs):
        ins, srcs = refs[:n_in], refs[n_in:n_in + n_arr]
        outs = refs[n_in + n_arr:n_in + n_arr + n_out]
        dsts = refs[n_in + n_arr + n_out:n_in + 2 * n_arr + n_out]
        first_sem = len(refs) - 2 * len(rides)
        scratch, sems = refs[n_in + 2 * n_arr + n_out:first_sem], refs[first_sem:]

        def copies():
            made, at = [], 0
            for r, ((kind, _), n) in enumerate(zip(rides, counts)):
                made += _EXCHANGES[kind][1](srcs[at:at + n], dsts[at:at + n], sems[2 * r], sems[2 * r + 1])
                at += n
            return made

        @pl.when(is_first())
        def _():
            for cp in copies():
                cp.start()

        body(*ins, *outs, *scratch)

        @pl.when(is_last())
        def _():
            for cp in copies():
                cp.wait()

    return wrapped


_HBM = pl.BlockSpec(memory_space=pltpu.HBM)
_SEM = pl.BlockSpec(memory_space=pltpu.SEMAPHORE)
_FLOWS = pltpu.SideEffectType.DATAFLOW_SIDE_EFFECTING


def _start_exchanges(rides, name):
    arrays = [a for _, group in rides for a in group]
    shapes, sems = _ride_shapes(rides)
    n_arr, n_sem = len(arrays), len(sems)

    def body(*refs):
        srcs, lands = refs[:n_arr], refs[n_arr:2 * n_arr]
        sem_refs, token = refs[2 * n_arr:2 * n_arr + n_sem], refs[-1]
        at = 0
        for r, (kind, group) in enumerate(rides):
            n = len(group)
            for cp in _EXCHANGES[kind][1](srcs[at:at + n], lands[at:at + n], sem_refs[2 * r], sem_refs[2 * r + 1]):
                cp.start()
            at += n
        token[...] = jnp.zeros(token.shape, token.dtype)

    in_hbm = [pltpu.HBM(a.shape, a.dtype) for a in arrays]
    land_hbm = [pltpu.HBM(sh.shape, sh.dtype) for sh in shapes]
    res = pl.pallas_call(
        body, name=name,
        out_shape=(*sems, *in_hbm, *land_hbm, jax.ShapeDtypeStruct((SUBLANES, LANES), F32)),
        in_specs=[_HBM] * (2 * n_arr), out_specs=(*[_SEM] * n_sem, *[_HBM] * (2 * n_arr), _VMEM),
        input_output_aliases={i: n_sem + i for i in range(2 * n_arr)},
        compiler_params=pltpu.CompilerParams(has_side_effects=_FLOWS),
    )(*[pltpu.with_memory_space_constraint(a, pltpu.HBM) for a in arrays],
      *[pltpu.with_memory_space_constraint(lax.empty(sh.shape, sh.dtype), pltpu.HBM) for sh in shapes])
    return res[:n_sem], res[n_sem:n_sem + n_arr], res[n_sem + n_arr:n_sem + 2 * n_arr], res[-1]


def _wait_exchanges(kinds, sems, arrays, lands, after, name, keep_sources=False):
    n_arr, n_sem = len(arrays), len(sems)

    def body(*refs):
        srcs, dsts = refs[:n_arr], refs[n_arr:2 * n_arr]
        sem_refs = refs[2 * n_arr:2 * n_arr + n_sem]
        at = 0
        for r, (kind, n) in enumerate(kinds):
            for cp in _EXCHANGES[kind][1](srcs[at:at + n], dsts[at:at + n], sem_refs[2 * r], sem_refs[2 * r + 1]):
                cp.wait()
            at += n

    hbm = [pltpu.HBM(a.shape, a.dtype) for a in (*arrays, *lands)]
    return pl.pallas_call(
        body, name=name, out_shape=tuple(hbm),
        in_specs=[_HBM] * (2 * n_arr) + [_SEM] * n_sem + [_ANY], out_specs=tuple([_HBM] * (2 * n_arr)),
        input_output_aliases={i: i for i in range(2 * n_arr)},
        compiler_params=pltpu.CompilerParams(has_side_effects=_FLOWS),
    )(*arrays, *lands, *sems, after)[0 if keep_sources else n_arr:]


def _chip_partial(pos, mine, theirs, relations, out_dtype, row_tile, name):
    n_slots, m, n = mine.shape
    q0 = relations[0]

    def chip_of(qi, pos_ref):
        q = qi + q0
        return pos_ref[0] ^ (q >> 1), pos_ref[1] ^ (q & 1)

    def mine_map(qi, t, pos_ref):
        px, py = chip_of(qi, pos_ref)
        return (4 * px + 2 * py + pos_ref[2] if n_slots == N_DEV else 2 * px + py), t, 0

    def theirs_map(qi, t, pos_ref):
        px, py = chip_of(qi, pos_ref)
        return 2 * px + py, t, 0

    def body(pos_ref, a_ref, b_ref, o_ref):
        o_ref[...] = (a_ref[...] + b_ref[...]).astype(out_dtype)

    return pl.pallas_call(
        body, name=name,
        out_shape=jax.ShapeDtypeStruct((len(relations), m, n), out_dtype),
        grid_spec=pltpu.PrefetchScalarGridSpec(
            num_scalar_prefetch=1, grid=(len(relations), m // row_tile),
            in_specs=[pl.BlockSpec((None, row_tile, n), mine_map), pl.BlockSpec((None, row_tile, n), theirs_map)],
            out_specs=pl.BlockSpec((None, row_tile, n), lambda qi, t, pos_ref: (qi, t, 0))),
        compiler_params=pltpu.CompilerParams(dimension_semantics=("arbitrary", "arbitrary")),
    )(pos, mine, theirs)


PARTS = ((0, 512), (512, 640))


def _gather_norm_proj(pos, x2d, small_shard, norm_g, w_in_shard, w_out_shards, n_chunk):
    seq = x2d.shape[0]
    n_tiles = seq // TILE + 1
    tp = n_tiles * TILE
    n_parts = len(PARTS)
    widest = max(width for _, width in PARTS)
    units = [(s, u) for s in range(2) for u in range(n_parts)]
    for first in (2, 5):
        units += [(first + j, u) for u in range(n_parts) for j in range(2)] + [(first + 2, u) for u in range(n_parts)]
    n_units = len(units)
    n_steps = n_tiles + n_units
    chunk = tp // n_chunk

    def body(pos_ref, x_ref, g_ref, small_ref, win_ref, wa_ref, wb_ref, wo_ref,
             ht_ref, proj_ref, meta_ref, small_all, win_all, wa_all, wb_all, wo_all,
             h_all, wbuf, rbuf, small_buf, send_sems, recv_sems, local_sems, small_send, small_recv):
        g = pl.program_id(0)
        x, y, c = lax.axis_index("x"), lax.axis_index("y"), lax.axis_index("c")
        me, sibling = (x, y, c), (x, y, 1 - c)
        chips = [(1 - x, y), (x, 1 - y), (1 - x, 1 - y)]
        shards = (win_ref, wa_ref, wb_ref, wo_ref)
        gathered = (win_all, wa_all, wb_all, wo_all)
        n_arrays = len(shards)
        blocks = [me, sibling] + [(*chip, c) for chip in chips] + [(*chip, 1 - c) for chip in chips]

        def index(block):
            px, py, pc = block
            return 4 * px + 2 * py + pc

        def part(ref, a, u):
            return ref.at[:, pl.ds(PARTS[u][0], PARTS[u][1])] if a == 0 else ref

        def slot(a, block, u):
            return part(gathered[a].at[index(block)], a, u)

        def sem(a, k, u):
            return n_parts * k + u if a == 0 else 7 * n_parts + 7 * (a - 1) + k

        def copy(a, k, block, to, u=0, from_shard=False):
            return pltpu.make_async_remote_copy(
                src_ref=part(shards[a], a, u) if from_shard else slot(a, block, u), dst_ref=slot(a, block, u),
                send_sem=send_sems.at[sem(a, k, u)], recv_sem=recv_sems.at[sem(a, k, u)],
                device_id=to, device_id_type=MESH)

        def keep(a):
            return pltpu.make_async_copy(shards[a], gathered[a].at[index(me)], local_sems.at[a])

        def load(m):
            s, u = units[m]
            src = part(win_ref, 0, u) if s == 0 else slot(0, blocks[s], u)
            return pltpu.make_async_copy(src, wbuf.at[m % 2, :, 0:PARTS[u][1]], local_sems.at[n_arrays + m % 2])

        def store(m):
            s, u = units[m]
            col0 = pl.multiple_of(index(blocks[s]) * COLS + PARTS[u][0], LANES)
            return pltpu.make_async_copy(rbuf.at[m % 2, :, 0:PARTS[u][1]],
                                         proj_ref.at[:, pl.ds(col0, PARTS[u][1])], local_sems.at[n_arrays + 2 + m % 2])

        def by_x(a, u):
            return u == 0 if a == 0 else a < 3

        def relay(a, u=0):
            src, to = (blocks[3], blocks[2]) if by_x(a, u) else (blocks[2], blocks[3])
            return copy(a, 3, src, to, u)

        def arrive(m):
            s, u = units[m]
            if s == 1:
                copy(0, 0, sibling, me, u).wait_recv()
            elif 2 <= s <= 4:
                copy(0, s - 1, blocks[s], me, u).wait_recv()
                copy(0, s + 2, blocks[s], sibling, u).start()
                if s < 4 and by_x(0, u) == (s == 3):
                    relay(0, u).start()
            elif s >= 5:
                copy(0, s - 1, blocks[s], me, u).wait_recv()
                if u == 0:
                    for a in range(1, 4):
                        copy(a, s - 4, blocks[s - 3], me).wait_recv()
                        copy(a, s - 1, blocks[s - 3], sibling).start()
                        if s < 7 and by_x(a, 0) == (s == 6):
                            relay(a).start()

        targets = [sibling, blocks[2], blocks[3]]

        def small_copies():
            return _all_copies([small_ref], [small_all], small_send, small_recv)

        @pl.when(g == 0)
        def _():
            for cp in small_copies():
                cp.start()
            for a in range(n_arrays):
                keep(a).start()
            for u in range(n_parts):
                for k, to in enumerate(targets):
                    copy(0, k, me, to, u, from_shard=True).start()
            for a in range(1, 4):
                for k, to in enumerate(targets):
                    copy(a, k, me, to, from_shard=True).start()
            load(0).start()

        @pl.when(g == n_tiles - 2)
        def _():
            for cp in small_copies():
                cp.wait()
            fetch = pltpu.make_async_copy(small_all, small_buf, local_sems.at[n_arrays + 4])
            fetch.start()
            fetch.wait()
            meta_ref[0:TILE - N_META, :] = jnp.zeros((TILE - N_META, D_MODEL), F32)
            meta_ref[TILE - N_META:TILE, :] = jnp.concatenate([small_buf[d, 0:N_META, :] for d in range(N_DEV)], axis=1)

        @pl.when(g < n_tiles)
        def _():
            s0 = jnp.where(g == n_tiles - 1, meta_ref[...], x_ref[...])
            r = lax.rsqrt(_rowmean(s0 * s0) + EPS)
            h32 = (s0 * r) * g_ref[...]
            ht_ref[...] = h32.T.astype(BF16)
            h_all[pl.ds(pl.multiple_of(g * TILE, TILE), TILE), :] = h32.astype(BF16)

        for m in range(n_units):
            @pl.when(g == n_tiles + m)
            def _(m=m):
                load(m).wait()
                if m + 1 < n_units:
                    arrive(m + 1)
                    load(m + 1).start()
                if m >= 2:
                    store(m - 2).wait()

        m_now = jnp.maximum(g - n_tiles, 0)
        u_now = functools.reduce(jnp.add, [jnp.where(m_now == m, u, 0) for m, (_, u) in enumerate(units)])
        for u, (_, width) in enumerate(PARTS):
            @pl.when((g >= n_tiles) & (u_now == u))
            def _(width=width):
                w = wbuf[m_now % 2, :, 0:width]
                for r in range(n_chunk):
                    rbuf[m_now % 2, r * chunk:(r + 1) * chunk, 0:width] = _dot(h_all[r * chunk:(r + 1) * chunk, :], w)

        for m in range(n_units):
            @pl.when(g == n_tiles + m)
            def _(m=m):
                store(m).start()

        @pl.when(g == n_steps - 1)
        def _():
            store(n_units - 2).wait()
            store(n_units - 1).wait()
            for a in range(1, 4):
                copy(a, 0, sibling, me).wait_recv()
                for j in range(3):
                    copy(a, 4 + j, blocks[5 + j], me).wait_recv()
            for a in range(n_arrays):
                for u in range(n_parts if a == 0 else 1):
                    for k, to in enumerate(targets):
                        copy(a, k, me, to, u, from_shard=True).wait_send()
                    relay(a, u).wait_send()
                    for j in range(3):
                        copy(a, 4 + j, blocks[2 + j], sibling, u).wait_send()
                keep(a).wait()

    n_x = n_tiles - 1
    return pl.pallas_call(
        body, name="gather_norm_proj",
        out_shape=[jax.ShapeDtypeStruct((D_MODEL, tp), BF16), jax.ShapeDtypeStruct((tp, D_IN), F32),
                   jax.ShapeDtypeStruct((TILE, D_MODEL), F32), jax.ShapeDtypeStruct((N_DEV,) + small_shard.shape[1:], F32),
                   jax.ShapeDtypeStruct((N_DEV,) + w_in_shard.shape, BF16)]
                  + [jax.ShapeDtypeStruct((N_DEV,) + w.shape, BF16) for w in w_out_shards],
        grid_spec=pltpu.PrefetchScalarGridSpec(
            num_scalar_prefetch=1, grid=(n_steps,),
            in_specs=[pl.BlockSpec((TILE, D_MODEL), lambda g, pos_ref: (jnp.minimum(g, n_x - 1), 0)),
                      _VMEM, _ANY, _ANY, _ANY, _ANY, _ANY],
            out_specs=[pl.BlockSpec((D_MODEL, TILE), lambda g, pos_ref: (0, jnp.minimum(g, n_tiles - 1))),
                       _ANY, _VMEM, _ANY, _ANY, _ANY, _ANY, _ANY],
            scratch_shapes=[pltpu.VMEM((tp, D_MODEL), BF16), pltpu.VMEM((2, D_MODEL, widest), BF16),
                            pltpu.VMEM((2, tp, widest), F32), pltpu.VMEM((N_DEV,) + small_shard.shape[1:], F32),
                            pltpu.SemaphoreType.DMA((7 * n_parts + 21,)), pltpu.SemaphoreType.DMA((7 * n_parts + 21,)),
                            pltpu.SemaphoreType.DMA((9,)),
                            pltpu.SemaphoreType.DMA((N_DEV,)), pltpu.SemaphoreType.DMA((N_DEV,))]),
        compiler_params=pltpu.CompilerParams(dimension_semantics=("arbitrary",), vmem_limit_bytes=VMEM_LIMIT),
    )(pos, x2d, norm_g, small_shard, w_in_shard, *w_out_shards)


C_AVAL, C_AGLU, C_AZ, C_BB, C_BC, C_BX, C_BZ, C_GA, C_GB = (k * D_MODEL for k in range(9))
S_AZ, S_BB, S_BZ, S_GA, S_GB = (k * D_MODEL for k in range(5))


def _fused_pass(proj, x2d, tgt2d, meta_tile, conv_a_w, conv_a_b, ln_a_g, ln_a_b, b_a_out, conv_b_w, final_g,
                w_a, w_b, w_o, w_a_t, w_b_t, w_o_t, n_tiles):
    T = TILE
    tp = n_tiles * T
    inv_d = 1.0 / D_MODEL

    def block_of(tile):
        return jnp.where(tile == 0, n_tiles - 1, tile - 1)

    def cur(i):
        return block_of(jnp.minimum(i, n_tiles - 1))

    def prev(i):
        return block_of(jnp.clip(i - 1, 0, n_tiles - 1))

    def xblk(i):
        return jnp.maximum(jnp.minimum(i, n_tiles - 1) - 1, 0)

    def body(proj_ref, aprev, cprev, x_ref, tgt_ref, meta_ref, caw_ref, cab_ref, lng_ref, lnb_ref, bao_ref, cbw_ref,
             fg_ref, wa_ref, wb_ref, wo_ref, wat_ref, wbt_ref, wot_ref,
             dproj_ref, ds1_ref, lhs_ref, rhs_ref, small_ref,
             ua0_buf, cb_buf, dua1_buf, dc3_buf, stage, ua1_buf, c3_buf,
             dpa_buf, dpb_buf, dcaw8, dcbw8, shift_buf):
        i = pl.program_id(0)
        this, before = i % 2, 1 - i % 2

        @pl.when(i == 0)
        def _init():
            for buf in (ua0_buf, cb_buf, dua1_buf, dc3_buf, dcaw8, dcbw8):
                buf[...] = jnp.zeros(buf.shape, buf.dtype)
            small_ref[...] = jnp.zeros(small_ref.shape, F32)

        @pl.when(i >= 1)
        def _emit_stage():
            dproj_ref[:, C_AZ:C_BC] = stage[:, S_AZ:S_BZ]
            dproj_ref[:, C_BZ:D_IN] = stage[:, S_BZ:S_GB + D_MODEL]

        @pl.when(i < n_tiles)
        def _front():
            def conv_chunk(cc, carry):
                c0 = pl.multiple_of(cc * LANES, LANES)
                lanes = pl.ds(c0, LANES)

                def col(base):
                    return pl.ds(pl.multiple_of(base + cc * LANES, LANES), LANES)

                ua0 = proj_ref[:, col(C_AVAL)] * _sigmoid(proj_ref[:, col(C_AGLU)])
                ua0_buf[this, 0:HALO, lanes] = ua0_buf[before, T:T + HALO, lanes]
                ua0_buf[this, HALO:HALO + T, lanes] = ua0
                acc = jnp.broadcast_to(cab_ref[:, lanes], (T, LANES))
                lead = HALO - (CONV_A - 1)
                for r in range(SUBLANES):
                    taps = [k for k in range(CONV_A) if (k + lead) % SUBLANES == r]
                    rows = T + SUBLANES * max((k + lead) // SUBLANES for k in taps)
                    if r:
                        shift_buf[r, 0:rows, :] = ua0_buf[this, pl.ds(r, rows), lanes]
                    for k in taps:
                        q = (k + lead) // SUBLANES
                        if r:
                            win = shift_buf[r, SUBLANES * q:SUBLANES * q + T, :]
                        else:
                            win = ua0_buf[this, pl.ds(SUBLANES * q, T), lanes]
                        acc = acc + caw_ref[k:k + 1, lanes] * win
                ua1_buf[:, lanes] = acc
                cb = proj_ref[:, col(C_BC)] * proj_ref[:, col(C_BX)]
                cb_buf[this, 0:SUBLANES, lanes] = cb_buf[before, T:T + SUBLANES, lanes]
                cb_buf[this, SUBLANES:SUBLANES + T, lanes] = cb
                lead_b = SUBLANES - (CONV_B - 1)
                acc3 = cbw_ref[0:1, lanes] * cb_buf[this, pl.ds(lead_b, T), lanes]
                for k in range(1, CONV_B):
                    acc3 = acc3 + cbw_ref[k:k + 1, lanes] * cb_buf[this, pl.ds(lead_b + k, T), lanes]
                c3_buf[:, lanes] = acc3
                return carry

            lax.fori_loop(0, N_CHUNK, conv_chunk, 0)

            ua1 = ua1_buf[...]
            xc = ua1 - _rowmean(ua1)
            rstd = lax.rsqrt(_rowmean(xc * xc) + EPS)
            xhat = xc * rstd
            ua2 = xhat * lng_ref[...] + lnb_ref[...]
            sg2 = _sigmoid(ua2)
            ua3 = ua2 * sg2
            a_z = proj_ref[:, C_AZ:C_AZ + D_MODEL]
            sz = _sigmoid(a_z)
            silu_az = a_z * sz
            lhs_ref[0] = (ua3 * silu_az).astype(BF16)
            b_z = proj_ref[:, C_BZ:C_BZ + D_MODEL]
            sbz = _sigmoid(b_z)
            silu_bz = b_z * sbz
            b_b = proj_ref[:, C_BB:C_BB + D_MODEL]
            c3 = c3_buf[...]
            ub = b_b * c3
            lhs_ref[1] = (ub * silu_bz).astype(BF16)

            ya = _dot(lhs_ref[0], wa_ref[...]) + bao_ref[...]
            yb = _dot(lhs_ref[1], wb_ref[...])
            sga = _sigmoid(proj_ref[:, C_GA:C_GA + D_MODEL])
            sgb = _sigmoid(proj_ref[:, C_GB:C_GB + D_MODEL])
            m_b = (sga * ya + sgb * yb).astype(BF16)
            lhs_ref[2] = m_b
            s0 = jnp.where(i == 0, meta_ref[...], x_ref[...])
            s1 = s0 + _dot(m_b, wo_ref[...])
            r1 = lax.rsqrt(_rowmean(s1 * s1) + EPS)
            y = (s1 * r1) * fg_ref[...]
            is_token = (i >= 1).astype(F32)
            err = (y - tgt_ref[...]) * is_token
            small_ref[ROW_LOSS:ROW_LOSS + 1, :] += (0.5 * inv_d) * _colsum(err * err)
            dy = err * inv_d
            small_ref[ROW_FINAL_G:ROW_FINAL_G + 1, :] += _colsum(dy * (s1 * r1))
            gy = dy * fg_ref[...]
            ds1 = r1 * gy - s1 * ((r1 * r1 * r1) * _rowmean(gy * s1))
            ds1_ref[...] = ds1
            ds1_b = ds1.astype(BF16)
            rhs_ref[2] = ds1_b
            dm = _dot(ds1_b, wot_ref[...])
            dya = dm * sga
            dyb = dm * sgb
            stage[:, S_GA:S_GA + D_MODEL] = (dya * ya * (1.0 - sga)).astype(BF16)
            stage[:, S_GB:S_GB + D_MODEL] = (dyb * yb * (1.0 - sgb)).astype(BF16)
            small_ref[ROW_B_A_OUT:ROW_B_A_OUT + 1, :] += _colsum(dya)
            dya_b = dya.astype(BF16)
            dyb_b = dyb.astype(BF16)
            rhs_ref[0] = dya_b
            rhs_ref[1] = dyb_b
            dpa_buf[...] = _dot(dya_b, wat_ref[...])
            dpb_buf[...] = _dot(dyb_b, wbt_ref[...])

            dpa = dpa_buf[...]
            stage[:, S_AZ:S_AZ + D_MODEL] = (dpa * ua3 * (sz + silu_az * (1.0 - sz))).astype(BF16)
            dua2 = dpa * silu_az * (sg2 + ua3 * (1.0 - sg2))
            small_ref[ROW_LN_G:ROW_LN_G + 1, :] += _colsum(dua2 * xhat)
            small_ref[ROW_LN_B:ROW_LN_B + 1, :] += _colsum(dua2)
            dxh = dua2 * lng_ref[...]
            dua1 = rstd * (dxh - _rowmean(dxh) - xhat * _rowmean(dxh * xhat))
            small_ref[ROW_CONV_A_B:ROW_CONV_A_B + 1, :] += _colsum(dua1)
            dua1_buf[this, 0:T, :] = dua1
            dua1_buf[before, T:T + HALO, :] = dua1[0:HALO]
            dpb = dpb_buf[...]
            stage[:, S_BZ:S_BZ + D_MODEL] = (dpb * ub * (sbz + silu_bz * (1.0 - sbz))).astype(BF16)
            dub = dpb * silu_bz
            stage[:, S_BB:S_BB + D_MODEL] = (dub * c3).astype(BF16)
            dc3 = dub * b_b
            dc3_buf[this, 0:T, :] = dc3
            dc3_buf[before, T:T + SUBLANES, :] = dc3[0:SUBLANES]

        @pl.when(i == n_tiles)
        def _no_later_tile():
            dua1_buf[before, T:T + HALO, :] = jnp.zeros((HALO, D_MODEL), F32)
            dc3_buf[before, T:T + SUBLANES, :] = jnp.zeros((SUBLANES, D_MODEL), F32)

        @pl.when(i >= 1)
        def _lagged():
            def convt_chunk(cc, carry):
                c0 = pl.multiple_of(cc * LANES, LANES)
                lanes = pl.ds(c0, LANES)

                def col(base):
                    return pl.ds(pl.multiple_of(base + cc * LANES, LANES), LANES)

                ua0 = ua0_buf[before, HALO:HALO + T, lanes]
                acc = jnp.zeros((T, LANES), F32)
                for r in range(SUBLANES):
                    shifts = [j for j in range(CONV_A) if j % SUBLANES == r]
                    rows = T + shifts[-1] - r
                    if r:
                        shift_buf[r, 0:rows, :] = dua1_buf[before, pl.ds(r, rows), lanes]
                    for j in shifts:
                        k = CONV_A - 1 - j
                        if r:
                            later = shift_buf[r, j - r:j - r + T, :]
                        else:
                            later = dua1_buf[before, pl.ds(j, T), lanes]
                        acc = acc + caw_ref[k:k + 1, lanes] * later
                        dcaw8[SUBLANES * k:SUBLANES * (k + 1), lanes] += _fold8(ua0 * later)
                a_val = aprev[:, col(0)]
                sg = _sigmoid(aprev[:, col(D_MODEL)])
                dproj_ref[:, col(C_AVAL)] = (acc * sg).astype(BF16)
                dproj_ref[:, col(C_AGLU)] = (acc * a_val * (sg * (1.0 - sg))).astype(BF16)

                cb = cb_buf[before, SUBLANES:SUBLANES + T, lanes]
                acc3 = jnp.zeros((T, LANES), F32)
                for j in range(CONV_B):
                    k = CONV_B - 1 - j
                    later = dc3_buf[before, pl.ds(j, T), lanes]
                    acc3 = acc3 + cbw_ref[k:k + 1, lanes] * later
                    dcbw8[SUBLANES * k:SUBLANES * (k + 1), lanes] += _fold8(cb * later)
                dproj_ref[:, col(C_BC)] = (acc3 * cprev[:, col(D_MODEL)]).astype(BF16)
                dproj_ref[:, col(C_BX)] = (acc3 * cprev[:, col(0)]).astype(BF16)
                return carry

            lax.fori_loop(0, N_CHUNK, convt_chunk, 0)

        @pl.when(i == n_tiles)
        def _finish():
            for k in range(CONV_A):
                small_ref[ROW_CONV_A_W + k:ROW_CONV_A_W + k + 1, :] = _colsum(dcaw8[SUBLANES * k:SUBLANES * (k + 1), :])
            for k in range(CONV_B):
                small_ref[ROW_CONV_B_W + k:ROW_CONV_B_W + k + 1, :] = _colsum(dcbw8[SUBLANES * k:SUBLANES * (k + 1), :])

    pair = 2 * D_MODEL
    return pl.pallas_call(
        body, name="fused_pass", grid=(n_tiles + 1,),
        out_shape=[
            jax.ShapeDtypeStruct((tp, D_IN), BF16),
            jax.ShapeDtypeStruct((tp, D_MODEL), F32),
            jax.ShapeDtypeStruct((3, tp, D_MODEL), BF16),
            jax.ShapeDtypeStruct((3, tp, D_MODEL), BF16),
            jax.ShapeDtypeStruct((SMALL_A_ROWS, D_MODEL), F32),
        ],
        in_specs=[
            pl.BlockSpec((T, D_IN), lambda i: (cur(i), 0)),
            pl.BlockSpec((T, pair), lambda i: (prev(i), C_AVAL // pair)),
            pl.BlockSpec((T, pair), lambda i: (prev(i), C_BC // pair)),
            pl.BlockSpec((T, D_MODEL), lambda i: (xblk(i), 0)),
            pl.BlockSpec((T, D_MODEL), lambda i: (xblk(i), 0)),
            _VMEM, _VMEM, _VMEM, _VMEM, _VMEM, _VMEM, _VMEM, _VMEM,
            *[_resident((D_MODEL, D_MODEL)) for _ in range(6)],
        ],
        out_specs=[
            pl.BlockSpec((T, D_IN), lambda i: (prev(i), 0)),
            pl.BlockSpec((T, D_MODEL), lambda i: (cur(i), 0)),
            pl.BlockSpec((3, T, D_MODEL), lambda i: (0, cur(i), 0)),
            pl.BlockSpec((3, T, D_MODEL), lambda i: (0, cur(i), 0)),
            _VMEM,
        ],
        scratch_shapes=[
            pltpu.VMEM((2, HALO + T, D_MODEL), F32),
            pltpu.VMEM((2, SUBLANES + T, D_MODEL), F32),
            pltpu.VMEM((2, T + HALO, D_MODEL), F32),
            pltpu.VMEM((2, T + SUBLANES, D_MODEL), F32),
            pltpu.VMEM((T, 5 * D_MODEL), BF16),
            pltpu.VMEM((T, D_MODEL), F32),
            pltpu.VMEM((T, D_MODEL), F32),
            pltpu.VMEM((T, D_MODEL), F32),
            pltpu.VMEM((T, D_MODEL), F32),
            pltpu.VMEM((32 * SUBLANES, D_MODEL), F32),
            pltpu.VMEM((SUBLANES * SUBLANES, D_MODEL), F32),
            pltpu.VMEM((SUBLANES, T + HALO, LANES), F32),
        ],
        compiler_params=pltpu.CompilerParams(dimension_semantics=("arbitrary",), vmem_limit_bytes=VMEM_LIMIT),
    )(proj, proj, proj, x2d, tgt2d, meta_tile, conv_a_w, conv_a_b, ln_a_g, ln_a_b, b_a_out, conv_b_w, final_g,
      w_a, w_b, w_o, w_a_t, w_b_t, w_o_t)


def _input_bwd(dproj, ds1, x2d, meta_tile, norm_g, w_in_all, row_tile):
    seq = x2d.shape[0]
    n_steps = seq // row_tile
    meta_block = seq // TILE

    def backward(dp_ref, ds1_ref, s0_ref, g_ref, w_ref, out_ref, vec_ref):
        dh = _dot_nt(dp_ref[:, 0:COLS], w_ref[0])
        for j in range(1, N_DEV):
            dh = dh + _dot_nt(dp_ref[:, j * COLS:(j + 1) * COLS], w_ref[j])
        s0v = s0_ref[...]
        r = lax.rsqrt(_rowmean(s0v * s0v) + EPS)
        gh = dh * g_ref[...]
        out_ref[...] = ds1_ref[...] + r * gh - s0v * ((r * r * r) * _rowmean(gh * s0v))
        vec_ref[ROW_NORM_G:ROW_NORM_G + 1, :] += _colsum(dh * (s0v * r))

    def body(dp_ref, ds1_ref, x_ref, dpm_ref, ds1m_ref, meta_ref, g_ref, w_ref, gx_ref, small_ref, gmeta_buf):
        t = pl.program_id(0)

        @pl.when(t == 0)
        def _():
            small_ref[...] = jnp.zeros(small_ref.shape, F32)

        backward(dp_ref, ds1_ref, x_ref, g_ref, w_ref, gx_ref, small_ref)

        @pl.when(t == n_steps - 1)
        def _():
            backward(dpm_ref, ds1m_ref, meta_ref, g_ref, w_ref, gmeta_buf, small_ref)
            small_ref[ROW_META:ROW_META + N_META, :] = gmeta_buf[TILE - N_META:TILE, :]

    return pl.pallas_call(
        body, name="input_bwd", grid=(n_steps,),
        out_shape=[jax.ShapeDtypeStruct(x2d.shape, F32), jax.ShapeDtypeStruct((SMALL_B_ROWS, D_MODEL), F32)],
        in_specs=[pl.BlockSpec((row_tile, D_IN), lambda t: (t, 0)),
                  pl.BlockSpec((row_tile, D_MODEL), lambda t: (t, 0)),
                  pl.BlockSpec((row_tile, D_MODEL), lambda t: (t, 0)),
                  pl.BlockSpec((TILE, D_IN), lambda t: (meta_block, 0)),
                  pl.BlockSpec((TILE, D_MODEL), lambda t: (meta_block, 0)),
                  _VMEM, _VMEM, _resident((N_DEV, D_MODEL, COLS))],
        out_specs=[pl.BlockSpec((row_tile, D_MODEL), lambda t: (t, 0)), _VMEM],
        scratch_shapes=[pltpu.VMEM((TILE, D_MODEL), F32)],
        compiler_params=pltpu.CompilerParams(dimension_semantics=("arbitrary",), vmem_limit_bytes=VMEM_LIMIT),
    )(dproj, ds1, x2d, dproj, ds1, meta_tile, norm_g, w_in_all)


def _grad_w_in_half(pos, h_t, dproj, k_tile, other_side, rides, name, after=None, add_to=None, narrow=False):
    tp = h_t.shape[1]
    n_k = tp // k_tile
    order = [] if after is None else [after]
    summing = add_to is not None
    assert not (summing and narrow)

    def column_block(q, k, pos_ref):
        return k, 2 * q + (1 - pos_ref[2] if other_side else pos_ref[2])

    def body(pos_ref, h_ref, dp_ref, *refs):
        acc = refs[-1]

        @pl.when(pl.program_id(1) == 0)
        def _():
            acc[...] = refs[0][...].astype(F32) if summing else jnp.zeros(acc.shape, F32)

        acc[...] += _dot(h_ref[...], dp_ref[...])

        if summing or narrow:
            @pl.when(pl.program_id(1) == n_k - 1)
            def _():
                refs[-2][...] = acc[...].astype(BF16)

        if summing:
            @pl.when((pl.program_id(1) == n_k - 1) & (pl.program_id(0) == 2 * pos_ref[0] + pos_ref[1]))
            def _():
                refs[-3][...] = acc[...]

    ride = [a for _, arrays in rides for a in arrays]
    n_arr = len(ride)
    ride_shapes, ride_sems = _ride_shapes(rides)
    block = (None, D_MODEL, COLS)
    extra_in = [add_to] if summing else []
    extra_in_specs = [pl.BlockSpec(block, lambda q, k, pos_ref: (q, 0, 0))] if summing else []
    extra_out = [jax.ShapeDtypeStruct((4, D_MODEL, COLS), BF16)] if summing else []
    extra_out_specs = [pl.BlockSpec(block, lambda q, k, pos_ref: (q ^ (2 * pos_ref[0] + pos_ref[1]), 0, 0))] \
        if summing else []
    body = _riding(body, 3 + len(extra_in) + len(order), 1 + len(extra_out), rides,
                   lambda: (pl.program_id(0) == 0) & (pl.program_id(1) == 0),
                   lambda: (pl.program_id(0) == 3) & (pl.program_id(1) == n_k - 1))
    return pl.pallas_call(
        body, name=name,
        out_shape=[jax.ShapeDtypeStruct((1 if summing else 4, D_MODEL, COLS), BF16 if narrow else F32)]
        + extra_out + ride_shapes,
        grid_spec=pltpu.PrefetchScalarGridSpec(
            num_scalar_prefetch=1, grid=(4, n_k),
            in_specs=[pl.BlockSpec((D_MODEL, k_tile), lambda q, k, pos_ref: (0, k)),
                      pl.BlockSpec((k_tile, COLS), column_block)] + extra_in_specs + [_ANY] * (len(order) + n_arr),
            out_specs=[pl.BlockSpec(block, lambda q, k, pos_ref: (0 if summing else q, 0, 0))]
            + extra_out_specs + [_ANY] * n_arr,
            scratch_shapes=([pltpu.VMEM((D_MODEL, COLS), F32)] if summing or narrow else []) + ride_sems),
        compiler_params=pltpu.CompilerParams(dimension_semantics=("arbitrary", "arbitrary"),
                                             vmem_limit_bytes=VMEM_LIMIT),
    )(pos, h_t, dproj, *extra_in, *order, *ride)


def _grad_w_out(lhs, rhs, k_tile, after):
    tp = lhs.shape[1]

    def body(a_ref, b_ref, after_ref, o_ref):
        @pl.when(pl.program_id(1) == 0)
        def _():
            o_ref[...] = jnp.zeros(o_ref.shape, F32)

        o_ref[...] += _dot_tn(a_ref[...], b_ref[...]).reshape(N_DEV, ROWS_OUT, D_MODEL)

    return pl.pallas_call(
        body, name="grad_w_out", grid=(3, tp // k_tile),
        out_shape=jax.ShapeDtypeStruct((N_DEV, 3, ROWS_OUT, D_MODEL), F32),
        in_specs=[pl.BlockSpec((None, k_tile, D_MODEL), lambda w, k: (w, k, 0)),
                  pl.BlockSpec((None, k_tile, D_MODEL), lambda w, k: (w, k, 0)), _ANY],
        out_specs=pl.BlockSpec((N_DEV, None, ROWS_OUT, D_MODEL), lambda w, k: (0, w, 0, 0)),
        compiler_params=pltpu.CompilerParams(dimension_semantics=("arbitrary", "arbitrary"),
                                             vmem_limit_bytes=VMEM_LIMIT),
    )(lhs, rhs, after)


def _adamw_math(w, g, m, v):
    m = ADAM_B1 * m + (1.0 - ADAM_B1) * g
    v = ADAM_B2 * v + (1.0 - ADAM_B2) * (g * g)
    m_hat = m / (1.0 - ADAM_B1 ** ADAM_STEP)
    v_hat = v / (1.0 - ADAM_B2 ** ADAM_STEP)
    delta = -ADAM_LR * (m_hat / (jnp.sqrt(v_hat) + ADAM_EPS) + ADAM_WD * w)
    return delta, m, v


def _adamw_sharded(pos, mine, theirs, landed, weights, row_tile, name, after=None):
    order = [] if after is None else [after]
    rows, n = weights[0][0].shape
    n_slots = mine.shape[0]
    per_shard = rows // row_tile
    assert per_shard == 1 or len(weights) == 1

    def mine_map(j, t, pos_ref):
        chip = 2 * pos_ref[0] + pos_ref[1]
        return {N_DEV: 2 * chip + pos_ref[2], 4: chip, 1: 0}[n_slots], j * per_shard + t, 0

    def theirs_map(j, t, pos_ref):
        return 2 * pos_ref[0] + pos_ref[1], j * per_shard + t, 0

    def body(pos_ref, mine_ref, *refs):
        if theirs is not None:
            g = mine_ref[...] + refs[0][...]
            refs = refs[1:]
        else:
            g = mine_ref[...]
        land_ref, refs = refs[0], refs[1:]
        ins, outs = refs[:3 * len(weights)], refs[3 * len(weights) + len(order):]
        for k in range(3):
            g = g + land_ref[k].astype(F32)
        for j in range(len(weights)):
            @pl.when(pl.program_id(0) == j)
            def _(j=j):
                w_ref, m_ref, v_ref = ins[3 * j:3 * j + 3]
                delta, m_new, v_new = _adamw_math(w_ref[...], g, m_ref[...], v_ref[...])
                for ref, val in zip(outs[4 * j:4 * j + 4], (g, delta, m_new, v_new)):
                    ref[...] = val

    tile = pl.BlockSpec((row_tile, n), lambda j, t, pos_ref: (t, 0))
    res = pl.pallas_call(
        body, name=name,
        out_shape=[jax.ShapeDtypeStruct((rows, n), F32)] * (4 * len(weights)),
        grid_spec=pltpu.PrefetchScalarGridSpec(
            num_scalar_prefetch=1, grid=(len(weights), per_shard),
            in_specs=[pl.BlockSpec((None, row_tile, n), mine_map)]
            + ([pl.BlockSpec((None, row_tile, n), theirs_map)] if theirs is not None else [])
            + [pl.BlockSpec((3, row_tile, n), lambda j, t, pos_ref: (0, j * per_shard + t, 0))]
            + [tile] * (3 * len(weights)) + [_ANY] * len(order),
            out_specs=[tile] * (4 * len(weights))),
        compiler_params=pltpu.CompilerParams(dimension_semantics=("arbitrary", "arbitrary")),
    )(pos, mine, *([theirs] if theirs is not None else []), landed, *[a for wmv in weights for a in wmv], *order)
    return [res[4 * j:4 * j + 4] for j in range(len(weights))]


def _adamw_small(gathered, gathered_cols, params):
    n_par, n_src = len(params), len(gathered)

    def body(*refs):
        g_refs, gc_refs = refs[:n_src], refs[n_src:2 * n_src]
        ins = refs[2 * n_src:2 * n_src + 3 * n_par]
        outs = refs[2 * n_src + 3 * n_par:]
        loss_ref = outs[4 * n_par]

        def reduced(ref, row, n_rows):
            g = ref[0, row:row + n_rows, :]
            for d in range(1, N_DEV):
                g = g + ref[d, row:row + n_rows, :]
            return g

        for p, (src, row, n_rows, sharded, _, _, _) in enumerate(params):
            g = reduced((gc_refs if sharded else g_refs)[src], row, n_rows)
            w_ref, m_ref, v_ref = ins[3 * p:3 * p + 3]
            delta, m_new, v_new = _adamw_math(w_ref[...], g, m_ref[...], v_ref[...])
            outs[4 * p][...] = g
            outs[4 * p + 1][...] = delta
            outs[4 * p + 2][...] = m_new
            outs[4 * p + 3][...] = v_new
        loss = jnp.sum(reduced(g_refs[0], ROW_LOSS, 1), axis=1, keepdims=True)
        loss_ref[...] = jnp.broadcast_to(loss, loss_ref.shape)

    out_shape = []
    for (_, _, _, _, w, _, _) in params:
        out_shape += [jax.ShapeDtypeStruct(w.shape, F32)] * 4
    out_shape.append(jax.ShapeDtypeStruct((1, LANES), F32))
    flat = [a for (_, _, _, _, w, m, v) in params for a in (w, m, v)]
    return pl.pallas_call(
        body, name="adamw_small", out_shape=out_shape,
        in_specs=[_VMEM] * (2 * n_src + len(flat)), out_specs=[_VMEM] * len(out_shape),
    )(*gathered, *gathered_cols, *flat)


def _pad_rows(a, rows):
    return jnp.concatenate([a, jnp.zeros((rows - a.shape[0], a.shape[1]), a.dtype)], axis=0)


def kernel(x, meta_tokens, norm_g, w_in, conv_a_w, conv_a_b, ln_a_g, ln_a_b, w_a_out, b_a_out, conv_b_w, w_b_out, w_out, final_g, loss_target, m_meta_tokens, m_norm_g, m_w_in, m_conv_a_w, m_conv_a_b, m_ln_a_g, m_ln_a_b, m_w_a_out, m_b_a_out, m_conv_b_w, m_w_b_out, m_w_out, m_final_g, v_meta_tokens, v_norm_g, v_w_in, v_conv_a_w, v_conv_a_b, v_ln_a_g, v_ln_a_b, v_w_a_out, v_b_a_out, v_conv_b_w, v_w_b_out, v_w_out, v_final_g):
    seq = x.shape[1]
    assert x.shape == (1, seq, D_MODEL) and seq % TILE == 0 and w_in.shape == (1, D_MODEL, COLS)
    n_tiles = seq // TILE + 1
    tp = n_tiles * TILE
    pos = jnp.stack([lax.axis_index("x"), lax.axis_index("y"), lax.axis_index("c")]).astype(jnp.int32)
    me = 4 * pos[0] + 2 * pos[1] + pos[2]
    x2d = x[0]
    tgt2d = loss_target[0]

    small = jnp.concatenate([meta_tokens, _pad_rows(conv_a_w[0], 32), _pad_rows(conv_b_w[0], SUBLANES)], axis=0)
    final_g2 = final_g.reshape(1, D_MODEL)

    w_out_shards = [w[0].astype(BF16) for w in (w_a_out, w_b_out, w_out)]
    h_t, proj, meta_tile, small_params, w_in_all, *w_out_all = _gather_norm_proj(
        pos, x2d, small[None], norm_g, w_in[0].astype(BF16), w_out_shards, 3)
    small_params = small_params.transpose(1, 0, 2).reshape(small.shape[0], D_MODEL)
    conv_a_full, conv_b_full = small_params[N_META:N_META + 32], small_params[N_META + 32:]
    w_out_all = [w.reshape(D_MODEL, D_MODEL) for w in w_out_all]
    w_out_all_t = [w.T for w in w_out_all]
    dproj, ds1, lhs, rhs, small_a = _fused_pass(
        proj, x2d, tgt2d, meta_tile, conv_a_full, conv_a_b, ln_a_g, ln_a_b, b_a_out, conv_b_full, final_g2,
        w_out_all[0], w_out_all[1], w_out_all[2], w_out_all_t[0], w_out_all_t[1], w_out_all_t[2], n_tiles)
    k_tile = tp // 3
    gw_far, small_a_all = _grad_w_in_half(pos, h_t, dproj, k_tile, True, [("all", (small_a[None],))], "grad_w_in_far",
                                          narrow=True)
    sems, sent, landing, token = _start_exchanges([("sibling_half", (gw_far,))], "rs_far_start")
    gw_out = _grad_w_out(lhs, rhs, k_tile, token).reshape(N_DEV, 3 * ROWS_OUT, D_MODEL)
    (their_in,) = _wait_exchanges([("sibling_half", 1)], sems, sent, landing, gw_out, "rs_far_wait")
    sems_o, sent_o, landing_o, token = _start_exchanges([("sibling", (gw_out,))], "rs_out_start")
    gw_near, parts_in = _grad_w_in_half(pos, h_t, dproj, k_tile, False, [], "grad_w_in_near", after=token,
                                        add_to=their_in)
    sems_i, sent_i, landing_i, token = _start_exchanges([("chips_by_relation", (parts_in,))], "rs_chips_in_start")
    gw_out, their_out = _wait_exchanges([("sibling", 1)], sems_o, sent_o, landing_o, token, "rs_out_wait",
                                        keep_sources=True)
    parts_out = _chip_partial(pos, gw_out, their_out, (1, 2, 3), BF16, ROWS_OUT, "rs_parts_w_out")
    sems_o, sent_o, landing_o, token = _start_exchanges([("chips", (parts_out,))], "rs_chips_out_start")
    grad_x, small_b = _input_bwd(dproj, ds1, x2d, meta_tile, norm_g + token[0, 0], w_in_all, min(512, seq))

    sems_s, sent_s, landing_s, token = _start_exchanges([("all", (small_b[None],))], "gather_small_grads_start")
    (land_in,) = _wait_exchanges([("chips_by_relation", 1)], sems_i, sent_i, landing_i, token, "rs_chips_in_wait")
    (res_in,) = _adamw_sharded(pos, gw_near, None, land_in, [(w_in[0], m_w_in[0], v_w_in[0])], 128, "adamw_w_in")
    (land_out,) = _wait_exchanges([("chips", 1)], sems_o, sent_o, landing_o, res_in[0], "rs_chips_out_wait")
    res_out = _adamw_sharded(
        pos, gw_out, their_out, land_out,
        [(w_a_out[0], m_w_a_out[0], v_w_a_out[0]), (w_b_out[0], m_w_b_out[0], v_w_b_out[0]),
         (w_out[0], m_w_out[0], v_w_out[0])], ROWS_OUT, "adamw_w_out")
    (small_b_all,) = _wait_exchanges([("all", 1)], sems_s, sent_s, landing_s, res_out[2][0], "gather_small_grads_wait")
    small_grads = [small_a_all, small_b_all]
    small_cols = [lax.dynamic_slice_in_dim(g, me * LANES, LANES, axis=2) for g in small_grads]
    params = [
        (1, ROW_META, N_META, True, meta_tokens, m_meta_tokens, v_meta_tokens),
        (1, ROW_NORM_G, 1, False, norm_g, m_norm_g, v_norm_g),
        (0, ROW_CONV_A_W, CONV_A, True, conv_a_w[0], m_conv_a_w[0], v_conv_a_w[0]),
        (0, ROW_CONV_A_B, 1, False, conv_a_b, m_conv_a_b, v_conv_a_b),
        (0, ROW_LN_G, 1, False, ln_a_g, m_ln_a_g, v_ln_a_g),
        (0, ROW_LN_B, 1, False, ln_a_b, m_ln_a_b, v_ln_a_b),
        (0, ROW_B_A_OUT, 1, False, b_a_out, m_b_a_out, v_b_a_out),
        (0, ROW_CONV_B_W, CONV_B, True, conv_b_w[0], m_conv_b_w[0], v_conv_b_w[0]),
        (0, ROW_FINAL_G, 1, False, final_g2, m_final_g.reshape(1, D_MODEL), v_final_g.reshape(1, D_MODEL)),
    ]
    res_small = _adamw_small(small_grads, small_cols, params)
    loss = res_small[-1][0, 0]

    def small_res(p, kind, shape):
        return res_small[4 * p + kind].reshape(shape)

    per_weight = []
    for kind in range(4):
        per_weight.append([
            small_res(0, kind, meta_tokens.shape),
            small_res(1, kind, norm_g.shape),
            res_in[kind].reshape(w_in.shape),
            small_res(2, kind, conv_a_w.shape),
            small_res(3, kind, conv_a_b.shape),
            small_res(4, kind, ln_a_g.shape),
            small_res(5, kind, ln_a_b.shape),
            res_out[0][kind].reshape(w_a_out.shape),
            small_res(6, kind, b_a_out.shape),
            small_res(7, kind, conv_b_w.shape),
            res_out[1][kind].reshape(w_b_out.shape),
            res_out[2][kind].reshape(w_out.shape),
            small_res(8, kind, final_g.shape),
        ])
    return (loss, grad_x.reshape(x.shape), *per_weight[0], *per_weight[1], *per_weight[2], *per_weight[3])
```

```python
import functools

import jax
import jax.numpy as jnp
from jax import lax
from jax.experimental import pallas as pl
from jax.experimental.pallas import tpu as pltpu

D_MODEL = 1024
N_META = 16
N_DEV = 8
D_IN = 9 * D_MODEL
COLS = D_IN // N_DEV
ROWS_OUT = D_MODEL // N_DEV
CONV_A = 31
CONV_B = 3
EPS = 1e-6

ADAM_LR = 0.001
ADAM_B1 = 0.9
ADAM_B2 = 0.999
ADAM_EPS = 1e-08
ADAM_WD = 0.01
ADAM_STEP = 10

TILE = 128
LANES = 128
N_CHUNK = D_MODEL // LANES
HALO = 32
SUBLANES = 8
VMEM_LIMIT = 56 * 1024 * 1024

ROW_FINAL_G, ROW_B_A_OUT, ROW_LN_G, ROW_LN_B, ROW_CONV_A_B, ROW_LOSS = 0, 1, 2, 3, 4, 5
ROW_CONV_A_W, ROW_CONV_B_W, SMALL_A_ROWS = 8, 40, 48
ROW_NORM_G, ROW_META, SMALL_B_ROWS = 0, 8, 24

MESH = pl.DeviceIdType.MESH
_ANY = pl.BlockSpec(memory_space=pl.ANY)
_VMEM = pl.BlockSpec(memory_space=pltpu.VMEM)
BF16 = jnp.bfloat16
F32 = jnp.float32


def _resident(shape):
    return pl.BlockSpec(shape, lambda *_: (0,) * len(shape), pipeline_mode=pl.Buffered(1))


def _sigmoid(v):
    return jax.nn.sigmoid(v)


def _dot(a, b):
    return jnp.dot(a, b, preferred_element_type=F32)


def _dot_nt(a, b):
    return lax.dot_general(a, b, (((1,), (1,)), ((), ())), preferred_element_type=F32)


def _dot_tn(a, b):
    return lax.dot_general(a, b, (((0,), (0,)), ((), ())), preferred_element_type=F32)


def _colsum(v):
    return jnp.sum(v, axis=0, keepdims=True)


def _rowmean(v):
    parts = [v[:, LANES * c:LANES * (c + 1)] for c in range(v.shape[1] // LANES)]
    return jnp.sum(functools.reduce(jnp.add, parts), axis=-1, keepdims=True) * (1.0 / v.shape[1])


def _fold8(v):
    parts = [v[SUBLANES * g:SUBLANES * (g + 1)] for g in range(v.shape[0] // SUBLANES)]
    return functools.reduce(jnp.add, parts)


def _sibling_copies(srcs, dsts, send_sems, recv_sems):
    x, y, c = lax.axis_index("x"), lax.axis_index("y"), lax.axis_index("c")
    return [pltpu.make_async_remote_copy(
        src_ref=src.at[2 * q + (1 - c)], dst_ref=dst.at[q],
        send_sem=send_sems.at[4 * a + q], recv_sem=recv_sems.at[4 * a + q],
        device_id=(x, y, 1 - c), device_id_type=MESH)
        for a, (src, dst) in enumerate(zip(srcs, dsts)) for q in range(4)]


def _chip_copies(srcs, dsts, send_sems, recv_sems):
    x, y, c = lax.axis_index("x"), lax.axis_index("y"), lax.axis_index("c")
    targets = [(x, 1 - y, c), (1 - x, y, c), (1 - x, 1 - y, c)]
    return [pltpu.make_async_remote_copy(
        src_ref=src.at[k], dst_ref=dst.at[k],
        send_sem=send_sems.at[3 * a + k], recv_sem=recv_sems.at[3 * a + k],
        device_id=targets[k], device_id_type=MESH)
        for a, (src, dst) in enumerate(zip(srcs, dsts)) for k in range(3)]


def _sibling_half_copies(srcs, dsts, send_sems, recv_sems):
    x, y, c = lax.axis_index("x"), lax.axis_index("y"), lax.axis_index("c")
    return [pltpu.make_async_remote_copy(
        src_ref=src.at[q], dst_ref=dst.at[q],
        send_sem=send_sems.at[4 * a + q], recv_sem=recv_sems.at[4 * a + q],
        device_id=(x, y, 1 - c), device_id_type=MESH)
        for a, (src, dst) in enumerate(zip(srcs, dsts)) for q in range(4)]


def _all_copies(srcs, dsts, send_sems, recv_sems):
    x, y, c = lax.axis_index("x"), lax.axis_index("y"), lax.axis_index("c")
    mine = 4 * x + 2 * y + c
    copies = []
    for a, (src, dst) in enumerate(zip(srcs, dsts)):
        copies.append(pltpu.make_async_copy(src.at[0], dst.at[mine], send_sems.at[N_DEV * a]))
        for k in range(1, N_DEV):
            copies.append(pltpu.make_async_remote_copy(
                src_ref=src.at[0], dst_ref=dst.at[mine],
                send_sem=send_sems.at[N_DEV * a + k], recv_sem=recv_sems.at[N_DEV * a + k],
                device_id=(x ^ (k >> 2), y ^ ((k >> 1) & 1), c ^ (k & 1)), device_id_type=MESH))
    return copies


def _chip_copies_by_relation(srcs, dsts, send_sems, recv_sems):
    return _chip_copies([src.at[pl.ds(1, 3)] for src in srcs], dsts, send_sems, recv_sems)


_EXCHANGES = {"sibling": (4, _sibling_copies, 4), "sibling_half": (4, _sibling_half_copies, 4),
              "chips": (3, _chip_copies, 3), "chips_by_relation": (3, _chip_copies_by_relation, 3),
              "all": (N_DEV, _all_copies, N_DEV)}


def _exchange_shapes(kind, arrays):
    per_array, _, slots = _EXCHANGES[kind]
    out_shape = [jax.ShapeDtypeStruct((slots,) + a.shape[1:], a.dtype) for a in arrays]
    sems = [pltpu.SemaphoreType.DMA((per_array * len(arrays),))] * 2
    return out_shape, sems


def _ride_shapes(rides):
    shapes, sems = [], []
    for kind, arrays in rides:
        ride_shapes, ride_sems = _exchange_shapes(kind, arrays)
        shapes += ride_shapes
        sems += ride_sems
    return shapes, sems


def _riding(body, n_in, n_out, rides, is_first, is_last):
    counts = [len(arrays) for _, arrays in rides]
    n_arr = sum(counts)

    def wrapped(*refs):
        ins, srcs = refs[:n_in], refs[n_in:n_in + n_arr]
        outs = refs[n_in + n_arr:n_in + n_arr + n_out]
        dsts = refs[n_in + n_arr + n_out:n_in + 2 * n_arr + n_out]
        first_sem = len(refs) - 2 * len(rides)
        scratch, sems = refs[n_in + 2 * n_arr + n_out:first_sem], refs[first_sem:]

        def copies():
            made, at = [], 0
            for r, ((kind, _), n) in enumerate(zip(rides, counts)):
                made += _EXCHANGES[kind][1](srcs[at:at + n], dsts[at:at + n], sems[2 * r], sems[2 * r + 1])
                at += n
            return made

        @pl.when(is_first())
        def _():
            for cp in copies():
                cp.start()

        body(*ins, *outs, *scratch)

        @pl.when(is_last())
        def _():
            for cp in copies():
                cp.wait()

    return wrapped


_HBM = pl.BlockSpec(memory_space=pltpu.HBM)
_SEM = pl.BlockSpec(memory_space=pltpu.SEMAPHORE)
_FLOWS = pltpu.SideEffectType.DATAFLOW_SIDE_EFFECTING


def _start_exchanges(rides, name):
    arrays = [a for _, group in rides for a in group]
    shapes, sems = _ride_shapes(rides)
    n_arr, n_sem = len(arrays), len(sems)

    def body(*refs):
        srcs, lands = refs[:n_arr], refs[n_arr:2 * n_arr]
        sem_refs, token = refs[2 * n_arr:2 * n_arr + n_sem], refs[-1]
        at = 0
        for r, (kind, group) in enumerate(rides):
            n = len(group)
            for cp in _EXCHANGES[kind][1](srcs[at:at + n], lands[at:at + n], sem_refs[2 * r], sem_refs[2 * r + 1]):
                cp.start()
            at += n
        token[...] = jnp.zeros(token.shape, token.dtype)

    in_hbm = [pltpu.HBM(a.shape, a.dtype) for a in arrays]
    land_hbm = [pltpu.HBM(sh.shape, sh.dtype) for sh in shapes]
    res = pl.pallas_call(
        body, name=name,
        out_shape=(*sems, *in_hbm, *land_hbm, jax.ShapeDtypeStruct((SUBLANES, LANES), F32)),
        in_specs=[_HBM] * (2 * n_arr), out_specs=(*[_SEM] * n_sem, *[_HBM] * (2 * n_arr), _VMEM),
        input_output_aliases={i: n_sem + i for i in range(2 * n_arr)},
        compiler_params=pltpu.CompilerParams(has_side_effects=_FLOWS),
    )(*[pltpu.with_memory_space_constraint(a, pltpu.HBM) for a in arrays],
      *[pltpu.with_memory_space_constraint(lax.empty(sh.shape, sh.dtype), pltpu.HBM) for sh in shapes])
    return res[:n_sem], res[n_sem:n_sem + n_arr], res[n_sem + n_arr:n_sem + 2 * n_arr], res[-1]


def _wait_exchanges(kinds, sems, arrays, lands, after, name, keep_sources=False):
    n_arr, n_sem = len(arrays), len(sems)

    def body(*refs):
        srcs, dsts = refs[:n_arr], refs[n_arr:2 * n_arr]
        sem_refs = refs[2 * n_arr:2 * n_arr + n_sem]
        at = 0
        for r, (kind, n) in enumerate(kinds):
            for cp in _EXCHANGES[kind][1](srcs[at:at + n], dsts[at:at + n], sem_refs[2 * r], sem_refs[2 * r + 1]):
                cp.wait()
            at += n

    hbm = [pltpu.HBM(a.shape, a.dtype) for a in (*arrays, *lands)]
    return pl.pallas_call(
        body, name=name, out_shape=tuple(hbm),
        in_specs=[_HBM] * (2 * n_arr) + [_SEM] * n_sem + [_ANY], out_specs=tuple([_HBM] * (2 * n_arr)),
        input_output_aliases={i: i for i in range(2 * n_arr)},
        compiler_params=pltpu.CompilerParams(has_side_effects=_FLOWS),
    )(*arrays, *lands, *sems, after)[0 if keep_sources else n_arr:]


def _chip_partial(pos, mine, theirs, relations, out_dtype, row_tile, name):
    n_slots, m, n = mine.shape
    q0 = relations[0]

    def chip_of(qi, pos_ref):
        q = qi + q0
        return pos_ref[0] ^ (q >> 1), pos_ref[1] ^ (q & 1)

    def mine_map(qi, t, pos_ref):
        px, py = chip_of(qi, pos_ref)
        return (4 * px + 2 * py + pos_ref[2] if n_slots == N_DEV else 2 * px + py), t, 0

    def theirs_map(qi, t, pos_ref):
        px, py = chip_of(qi, pos_ref)
        return 2 * px + py, t, 0

    def body(pos_ref, a_ref, b_ref, o_ref):
        o_ref[...] = (a_ref[...] + b_ref[...]).astype(out_dtype)

    return pl.pallas_call(
        body, name=name,
        out_shape=jax.ShapeDtypeStruct((len(relations), m, n), out_dtype),
        grid_spec=pltpu.PrefetchScalarGridSpec(
            num_scalar_prefetch=1, grid=(len(relations), m // row_tile),
            in_specs=[pl.BlockSpec((None, row_tile, n), mine_map), pl.BlockSpec((None, row_tile, n), theirs_map)],
            out_specs=pl.BlockSpec((None, row_tile, n), lambda qi, t, pos_ref: (qi, t, 0))),
        compiler_params=pltpu.CompilerParams(dimension_semantics=("arbitrary", "arbitrary")),
    )(pos, mine, theirs)


PARTS = ((0, 512), (512, 640))


def _gather_norm_proj(pos, x2d, small_shard, norm_g, w_in_shard, w_out_shards, n_chunk):
    seq = x2d.shape[0]
    n_tiles = seq // TILE + 1
    tp = n_tiles * TILE
    n_parts = len(PARTS)
    widest = max(width for _, width in PARTS)
    units = [(s, u) for s in range(2) for u in range(n_parts)]
    for first in (2, 5):
        units += [(first + j, u) for u in range(n_parts) for j in range(2)] + [(first + 2, u) for u in range(n_parts)]
    n_units = len(units)
    norm_rows = 2 * TILE if seq % (2 * TILE) == 0 else TILE
    n_norm = seq // norm_rows + 1
    n_steps = n_norm + n_units
    chunk = tp // n_chunk

    def body(pos_ref, x_ref, g_ref, small_ref, win_ref, wa_ref, wb_ref, wo_ref,
             ht_ref, proj_ref, meta_ref, small_all, win_all, wa_all, wb_all, wo_all,
             h_all, wbuf, rbuf, small_buf, send_sems, recv_sems, local_sems, small_send, small_recv):
        g = pl.program_id(0)
        x, y, c = lax.axis_index("x"), lax.axis_index("y"), lax.axis_index("c")
        me, sibling = (x, y, c), (x, y, 1 - c)
        chips = [(1 - x, y), (x, 1 - y), (1 - x, 1 - y)]
        shards = (win_ref, wa_ref, wb_ref, wo_ref)
        gathered = (win_all, wa_all, wb_all, wo_all)
        n_arrays = len(shards)
        blocks = [me, sibling] + [(*chip, c) for chip in chips] + [(*chip, 1 - c) for chip in chips]

        def index(block):
            px, py, pc = block
            return 4 * px + 2 * py + pc

        def part(ref, a, u):
            return ref.at[:, pl.ds(PARTS[u][0], PARTS[u][1])] if a == 0 else ref

        def slot(a, block, u):
            return part(gathered[a].at[index(block)], a, u)

        def sem(a, k, u):
            return n_parts * k + u if a == 0 else 7 * n_parts + 7 * (a - 1) + k

        def copy(a, k, block, to, u=0, from_shard=False):
            return pltpu.make_async_remote_copy(
                src_ref=part(shards[a], a, u) if from_shard else slot(a, block, u), dst_ref=slot(a, block, u),
                send_sem=send_sems.at[sem(a, k, u)], recv_sem=recv_sems.at[sem(a, k, u)],
                device_id=to, device_id_type=MESH)

        def keep(a):
            return pltpu.make_async_copy(shards[a], gathered[a].at[index(me)], local_sems.at[a])

        def load(m):
            s, u = units[m]
            src = part(win_ref, 0, u) if s == 0 else slot(0, blocks[s], u)
            return pltpu.make_async_copy(src, wbuf.at[m % 2, :, 0:PARTS[u][1]], local_sems.at[n_arrays + m % 2])

        def store(m):
            s, u = units[m]
            col0 = pl.multiple_of(index(blocks[s]) * COLS + PARTS[u][0], LANES)
            return pltpu.make_async_copy(rbuf.at[m % 2, :, 0:PARTS[u][1]],
                                         proj_ref.at[:, pl.ds(col0, PARTS[u][1])], local_sems.at[n_arrays + 2 + m % 2])

        def by_x(a, u):
            return u == 0 if a == 0 else a < 3

        def relay(a, u=0):
            src, to = (blocks[3], blocks[2]) if by_x(a, u) else (blocks[2], blocks[3])
            return copy(a, 3, src, to, u)

        def arrive(m):
            s, u = units[m]
            if s == 1:
                copy(0, 0, sibling, me, u).wait_recv()
            elif 2 <= s <= 4:
                copy(0, s - 1, blocks[s], me, u).wait_recv()
                copy(0, s + 2, blocks[s], sibling, u).start()
                if s < 4 and by_x(0, u) == (s == 3):
                    relay(0, u).start()
            elif s >= 5:
                copy(0, s - 1, blocks[s], me, u).wait_recv()
                if u == 0:
                    for a in range(1, 4):
                        copy(a, s - 4, blocks[s - 3], me).wait_recv()
                        copy(a, s - 1, blocks[s - 3], sibling).start()
                        if s < 7 and by_x(a, 0) == (s == 6):
                            relay(a).start()

        targets = [sibling, blocks[2], blocks[3]]

        def small_copies():
            return _all_copies([small_ref], [small_all], small_send, small_recv)

        @pl.when(g == 0)
        def _():
            for cp in small_copies():
                cp.start()
            for a in range(n_arrays):
                keep(a).start()
            for u in range(n_parts):
                for k, to in enumerate(targets):
                    copy(0, k, me, to, u, from_shard=True).start()
            for a in range(1, 4):
                for k, to in enumerate(targets):
                    copy(a, k, me, to, from_shard=True).start()
            load(0).start()

        @pl.when(g == n_norm - 2)
        def _():
            for cp in small_copies():
                cp.wait()
            fetch = pltpu.make_async_copy(small_all, small_buf, local_sems.at[n_arrays + 4])
            fetch.start()
            fetch.wait()
            meta_ref[0:TILE - N_META, :] = jnp.zeros((TILE - N_META, D_MODEL), F32)
            meta_ref[TILE - N_META:TILE, :] = jnp.concatenate([small_buf[d, 0:N_META, :] for d in range(N_DEV)], axis=1)

        def rmsnorm(s0):
            r = lax.rsqrt(_rowmean(s0 * s0) + EPS)
            return (s0 * r) * g_ref[...]

        @pl.when(g < n_norm - 1)
        def _():
            h32 = rmsnorm(x_ref[...])
            ht_ref[...] = h32.T.astype(BF16)
            h_all[pl.ds(pl.multiple_of(g * norm_rows, norm_rows), norm_rows), :] = h32.astype(BF16)

        @pl.when(g == n_norm - 1)
        def _():
            h32 = rmsnorm(meta_ref[...])
            ht_ref[:, 0:TILE] = h32.T.astype(BF16)
            if norm_rows > TILE:
                ht_ref[:, TILE:norm_rows] = jnp.zeros((D_MODEL, norm_rows - TILE), BF16)
            h_all[seq:seq + TILE, :] = h32.astype(BF16)

        for m in range(n_units):
            @pl.when(g == n_norm + m)
            def _(m=m):
                load(m).wait()
                if m + 1 < n_units:
                    arrive(m + 1)
                    load(m + 1).start()
                if m >= 2:
                    store(m - 2).wait()

        m_now = jnp.maximum(g - n_norm, 0)
        u_now = functools.reduce(jnp.add, [jnp.where(m_now == m, u, 0) for m, (_, u) in enumerate(units)])
        for u, (_, width) in enumerate(PARTS):
            @pl.when((g >= n_norm) & (u_now == u))
            def _(width=width):
                w = wbuf[m_now % 2, :, 0:width]
                for r in range(n_chunk):
                    rbuf[m_now % 2, r * chunk:(r + 1) * chunk, 0:width] = _dot(h_all[r * chunk:(r + 1) * chunk, :], w)

        for m in range(n_units):
            @pl.when(g == n_norm + m)
            def _(m=m):
                store(m).start()

        @pl.when(g == n_steps - 1)
        def _():
            store(n_units - 2).wait()
            store(n_units - 1).wait()
            for a in range(1, 4):
                copy(a, 0, sibling, me).wait_recv()
                for j in range(3):
                    copy(a, 4 + j, blocks[5 + j], me).wait_recv()
            for a in range(n_arrays):
                for u in range(n_parts if a == 0 else 1):
                    for k, to in enumerate(targets):
                        copy(a, k, me, to, u, from_shard=True).wait_send()
                    relay(a, u).wait_send()
                    for j in range(3):
                        copy(a, 4 + j, blocks[2 + j], sibling, u).wait_send()
                keep(a).wait()

    return pl.pallas_call(
        body, name="gather_norm_proj",
        out_shape=[jax.ShapeDtypeStruct((D_MODEL, n_norm * norm_rows), BF16), jax.ShapeDtypeStruct((tp, D_IN), F32),
                   jax.ShapeDtypeStruct((TILE, D_MODEL), F32), jax.ShapeDtypeStruct((N_DEV,) + small_shard.shape[1:], F32),
                   jax.ShapeDtypeStruct((N_DEV,) + w_in_shard.shape, BF16)]
                  + [jax.ShapeDtypeStruct((N_DEV,) + w.shape, BF16) for w in w_out_shards],
        grid_spec=pltpu.PrefetchScalarGridSpec(
            num_scalar_prefetch=1, grid=(n_steps,),
            in_specs=[pl.BlockSpec((norm_rows, D_MODEL), lambda g, pos_ref: (jnp.minimum(g, n_norm - 2), 0)),
                      _VMEM, _ANY, _ANY, _ANY, _ANY, _ANY],
            out_specs=[pl.BlockSpec((D_MODEL, norm_rows), lambda g, pos_ref: (0, jnp.minimum(g, n_norm - 1))),
                       _ANY, _VMEM, _ANY, _ANY, _ANY, _ANY, _ANY],
            scratch_shapes=[pltpu.VMEM((tp, D_MODEL), BF16), pltpu.VMEM((2, D_MODEL, widest), BF16),
                            pltpu.VMEM((2, tp, widest), F32), pltpu.VMEM((N_DEV,) + small_shard.shape[1:], F32),
                            pltpu.SemaphoreType.DMA((7 * n_parts + 21,)), pltpu.SemaphoreType.DMA((7 * n_parts + 21,)),
                            pltpu.SemaphoreType.DMA((9,)),
                            pltpu.SemaphoreType.DMA((N_DEV,)), pltpu.SemaphoreType.DMA((N_DEV,))]),
        compiler_params=pltpu.CompilerParams(dimension_semantics=("arbitrary",), vmem_limit_bytes=VMEM_LIMIT),
    )(pos, x2d, norm_g, small_shard, w_in_shard, *w_out_shards)


C_AVAL, C_AGLU, C_AZ, C_BB, C_BC, C_BX, C_BZ, C_GA, C_GB = (k * D_MODEL for k in range(9))
S_AZ, S_BB, S_BZ, S_GA, S_GB = (k * D_MODEL for k in range(5))


def _fused_pass(proj, x2d, tgt2d, meta_tile, conv_a_w, conv_a_b, ln_a_g, ln_a_b, b_a_out, conv_b_w, final_g,
                w_a, w_b, w_o, w_a_t, w_b_t, w_o_t, n_tiles):
    T = TILE
    tp = n_tiles * T
    inv_d = 1.0 / D_MODEL

    def block_of(tile):
        return jnp.where(tile == 0, n_tiles - 1, tile - 1)

    def cur(i):
        return block_of(jnp.minimum(i, n_tiles - 1))

    def prev(i):
        return block_of(jnp.clip(i - 1, 0, n_tiles - 1))

    def xblk(i):
        return jnp.maximum(jnp.minimum(i, n_tiles - 1) - 1, 0)

    def body(proj_ref, aprev, cprev, x_ref, tgt_ref, meta_ref, caw_ref, cab_ref, lng_ref, lnb_ref, bao_ref, cbw_ref,
             fg_ref, wa_ref, wb_ref, wo_ref, wat_ref, wbt_ref, wot_ref,
             dproj_ref, ds1_ref, lhs_ref, rhs_ref, small_ref,
             ua0_buf, cb_buf, dua1_buf, dc3_buf, stage, ua1_buf, c3_buf,
             dpa_buf, dpb_buf, dcaw8, dcbw8, shift_buf):
        i = pl.program_id(0)
        this, before = i % 2, 1 - i % 2

        @pl.when(i == 0)
        def _init():
            for buf in (ua0_buf, cb_buf, dua1_buf, dc3_buf, dcaw8, dcbw8):
                buf[...] = jnp.zeros(buf.shape, buf.dtype)
            small_ref[...] = jnp.zeros(small_ref.shape, F32)

        @pl.when(i >= 1)
        def _emit_stage():
            dproj_ref[:, C_AZ:C_BC] = stage[:, S_AZ:S_BZ]
            dproj_ref[:, C_BZ:D_IN] = stage[:, S_BZ:S_GB + D_MODEL]

        @pl.when(i < n_tiles)
        def _front():
            def conv_chunk(cc, carry):
                c0 = pl.multiple_of(cc * LANES, LANES)
                lanes = pl.ds(c0, LANES)

                def col(base):
                    return pl.ds(pl.multiple_of(base + cc * LANES, LANES), LANES)

                ua0 = proj_ref[:, col(C_AVAL)] * _sigmoid(proj_ref[:, col(C_AGLU)])
                ua0_buf[this, 0:HALO, lanes] = ua0_buf[before, T:T + HALO, lanes]
                ua0_buf[this, HALO:HALO + T, lanes] = ua0
                acc = jnp.broadcast_to(cab_ref[:, lanes], (T, LANES))
                lead = HALO - (CONV_A - 1)
                for r in range(SUBLANES):
                    taps = [k for k in range(CONV_A) if (k + lead) % SUBLANES == r]
                    rows = T + SUBLANES * max((k + lead) // SUBLANES for k in taps)
                    if r:
                        shift_buf[r, 0:rows, :] = ua0_buf[this, pl.ds(r, rows), lanes]
                    for k in taps:
                        q = (k + lead) // SUBLANES
                        if r:
                            win = shift_buf[r, SUBLANES * q:SUBLANES * q + T, :]
                        else:
                            win = ua0_buf[this, pl.ds(SUBLANES * q, T), lanes]
                        acc = acc + caw_ref[k:k + 1, lanes] * win
                ua1_buf[:, lanes] = acc
                cb = proj_ref[:, col(C_BC)] * proj_ref[:, col(C_BX)]
                cb_buf[this, 0:SUBLANES, lanes] = cb_buf[before, T:T + SUBLANES, lanes]
                cb_buf[this, SUBLANES:SUBLANES + T, lanes] = cb
                lead_b = SUBLANES - (CONV_B - 1)
                acc3 = cbw_ref[0:1, lanes] * cb_buf[this, pl.ds(lead_b, T), lanes]
                for k in range(1, CONV_B):
                    acc3 = acc3 + cbw_ref[k:k + 1, lanes] * cb_buf[this, pl.ds(lead_b + k, T), lanes]
                c3_buf[:, lanes] = acc3
                return carry

            lax.fori_loop(0, N_CHUNK, conv_chunk, 0)

            ua1 = ua1_buf[...]
            xc = ua1 - _rowmean(ua1)
            rstd = lax.rsqrt(_rowmean(xc * xc) + EPS)
            xhat = xc * rstd
            ua2 = xhat * lng_ref[...] + lnb_ref[...]
            sg2 = _sigmoid(ua2)
            ua3 = ua2 * sg2
            a_z = proj_ref[:, C_AZ:C_AZ + D_MODEL]
            sz = _sigmoid(a_z)
            silu_az = a_z * sz
            lhs_ref[0] = (ua3 * silu_az).astype(BF16)
            b_z = proj_ref[:, C_BZ:C_BZ + D_MODEL]
            sbz = _sigmoid(b_z)
            silu_bz = b_z * sbz
            b_b = proj_ref[:, C_BB:C_BB + D_MODEL]
            c3 = c3_buf[...]
            ub = b_b * c3
            lhs_ref[1] = (ub * silu_bz).astype(BF16)

            ya = _dot(lhs_ref[0], wa_ref[...]) + bao_ref[...]
            yb = _dot(lhs_ref[1], wb_ref[...])
            sga = _sigmoid(proj_ref[:, C_GA:C_GA + D_MODEL])
            sgb = _sigmoid(proj_ref[:, C_GB:C_GB + D_MODEL])
            m_b = (sga * ya + sgb * yb).astype(BF16)
            lhs_ref[2] = m_b
            s0 = jnp.where(i == 0, meta_ref[...], x_ref[...])
            s1 = s0 + _dot(m_b, wo_ref[...])
            r1 = lax.rsqrt(_rowmean(s1 * s1) + EPS)
            y = (s1 * r1) * fg_ref[...]
            is_token = (i >= 1).astype(F32)
            err = (y - tgt_ref[...]) * is_token
            small_ref[ROW_LOSS:ROW_LOSS + 1, :] += (0.5 * inv_d) * _colsum(err * err)
            dy = err * inv_d
            small_ref[ROW_FINAL_G:ROW_FINAL_G + 1, :] += _colsum(dy * (s1 * r1))
            gy = dy * fg_ref[...]
            ds1 = r1 * gy - s1 * ((r1 * r1 * r1) * _rowmean(gy * s1))
            ds1_ref[...] = ds1
            ds1_b = ds1.astype(BF16)
            rhs_ref[2] = ds1_b
            dm = _dot(ds1_b, wot_ref[...])
            dya = dm * sga
            dyb = dm * sgb
            stage[:, S_GA:S_GA + D_MODEL] = (dya * ya * (1.0 - sga)).astype(BF16)
            stage[:, S_GB:S_GB + D_MODEL] = (dyb * yb * (1.0 - sgb)).astype(BF16)
            small_ref[ROW_B_A_OUT:ROW_B_A_OUT + 1, :] += _colsum(dya)
            dya_b = dya.astype(BF16)
            dyb_b = dyb.astype(BF16)
            rhs_ref[0] = dya_b
            rhs_ref[1] = dyb_b
            dpa_buf[...] = _dot(dya_b, wat_ref[...])
            dpb_buf[...] = _dot(dyb_b, wbt_ref[...])

            dpa = dpa_buf[...]
            stage[:, S_AZ:S_AZ + D_MODEL] = (dpa * ua3 * (sz + silu_az * (1.0 - sz))).astype(BF16)
            dua2 = dpa * silu_az * (sg2 + ua3 * (1.0 - sg2))
            small_ref[ROW_LN_G:ROW_LN_G + 1, :] += _colsum(dua2 * xhat)
            small_ref[ROW_LN_B:ROW_LN_B + 1, :] += _colsum(dua2)
            dxh = dua2 * lng_ref[...]
            dua1 = rstd * (dxh - _rowmean(dxh) - xhat * _rowmean(dxh * xhat))
            small_ref[ROW_CONV_A_B:ROW_CONV_A_B + 1, :] += _colsum(dua1)
            dua1_buf[this, 0:T, :] = dua1
            dua1_buf[before, T:T + HALO, :] = dua1[0:HALO]
            dpb = dpb_buf[...]
            stage[:, S_BZ:S_BZ + D_MODEL] = (dpb * ub * (sbz + silu_bz * (1.0 - sbz))).astype(BF16)
            dub = dpb * silu_bz
            stage[:, S_BB:S_BB + D_MODEL] = (dub * c3).astype(BF16)
            dc3 = dub * b_b
            dc3_buf[this, 0:T, :] = dc3
            dc3_buf[before, T:T + SUBLANES, :] = dc3[0:SUBLANES]

        @pl.when(i == n_tiles)
        def _no_later_tile():
            dua1_buf[before, T:T + HALO, :] = jnp.zeros((HALO, D_MODEL), F32)
            dc3_buf[before, T:T + SUBLANES, :] = jnp.zeros((SUBLANES, D_MODEL), F32)

        @pl.when(i >= 1)
        def _lagged():
            def convt_chunk(cc, carry):
                c0 = pl.multiple_of(cc * LANES, LANES)
                lanes = pl.ds(c0, LANES)

                def col(base):
                    return pl.ds(pl.multiple_of(base + cc * LANES, LANES), LANES)

                ua0 = ua0_buf[before, HALO:HALO + T, lanes]
                acc = jnp.zeros((T, LANES), F32)
                for r in range(SUBLANES):
                    shifts = [j for j in range(CONV_A) if j % SUBLANES == r]
                    rows = T + shifts[-1] - r
                    if r:
                        shift_buf[r, 0:rows, :] = dua1_buf[before, pl.ds(r, rows), lanes]
                    for j in shifts:
                        k = CONV_A - 1 - j
                        if r:
                            later = shift_buf[r, j - r:j - r + T, :]
                        else:
                            later = dua1_buf[before, pl.ds(j, T), lanes]
                        acc = acc + caw_ref[k:k + 1, lanes] * later
                        dcaw8[SUBLANES * k:SUBLANES * (k + 1), lanes] += _fold8(ua0 * later)
                a_val = aprev[:, col(0)]
                sg = _sigmoid(aprev[:, col(D_MODEL)])
                dproj_ref[:, col(C_AVAL)] = (acc * sg).astype(BF16)
                dproj_ref[:, col(C_AGLU)] = (acc * a_val * (sg * (1.0 - sg))).astype(BF16)

                cb = cb_buf[before, SUBLANES:SUBLANES + T, lanes]
                acc3 = jnp.zeros((T, LANES), F32)
                for j in range(CONV_B):
                    k = CONV_B - 1 - j
                    later = dc3_buf[before, pl.ds(j, T), lanes]
                    acc3 = acc3 + cbw_ref[k:k + 1, lanes] * later
                    dcbw8[SUBLANES * k:SUBLANES * (k + 1), lanes] += _fold8(cb * later)
                dproj_ref[:, col(C_BC)] = (acc3 * cprev[:, col(D_MODEL)]).astype(BF16)
                dproj_ref[:, col(C_BX)] = (acc3 * cprev[:, col(0)]).astype(BF16)
                return carry

            lax.fori_loop(0, N_CHUNK, convt_chunk, 0)

        @pl.when(i == n_tiles)
        def _finish():
            for k in range(CONV_A):
                small_ref[ROW_CONV_A_W + k:ROW_CONV_A_W + k + 1, :] = _colsum(dcaw8[SUBLANES * k:SUBLANES * (k + 1), :])
            for k in range(CONV_B):
                small_ref[ROW_CONV_B_W + k:ROW_CONV_B_W + k + 1, :] = _colsum(dcbw8[SUBLANES * k:SUBLANES * (k + 1), :])

    pair = 2 * D_MODEL
    return pl.pallas_call(
        body, name="fused_pass", grid=(n_tiles + 1,),
        out_shape=[
            jax.ShapeDtypeStruct((tp, D_IN), BF16),
            jax.ShapeDtypeStruct((tp, D_MODEL), F32),
            jax.ShapeDtypeStruct((3, tp, D_MODEL), BF16),
            jax.ShapeDtypeStruct((3, tp, D_MODEL), BF16),
            jax.ShapeDtypeStruct((SMALL_A_ROWS, D_MODEL), F32),
        ],
        in_specs=[
            pl.BlockSpec((T, D_IN), lambda i: (cur(i), 0)),
            pl.BlockSpec((T, pair), lambda i: (prev(i), C_AVAL // pair)),
            pl.BlockSpec((T, pair), lambda i: (prev(i), C_BC // pair)),
            pl.BlockSpec((T, D_MODEL), lambda i: (xblk(i), 0)),
            pl.BlockSpec((T, D_MODEL), lambda i: (xblk(i), 0)),
            _VMEM, _VMEM, _VMEM, _VMEM, _VMEM, _VMEM, _VMEM, _VMEM,
            *[_resident((D_MODEL, D_MODEL)) for _ in range(6)],
        ],
        out_specs=[
            pl.BlockSpec((T, D_IN), lambda i: (prev(i), 0)),
            pl.BlockSpec((T, D_MODEL), lambda i: (cur(i), 0)),
            pl.BlockSpec((3, T, D_MODEL), lambda i: (0, cur(i), 0)),
            pl.BlockSpec((3, T, D_MODEL), lambda i: (0, cur(i), 0)),
            _VMEM,
        ],
        scratch_shapes=[
            pltpu.VMEM((2, HALO + T, D_MODEL), F32),
            pltpu.VMEM((2, SUBLANES + T, D_MODEL), F32),
            pltpu.VMEM((2, T + HALO, D_MODEL), F32),
            pltpu.VMEM((2, T + SUBLANES, D_MODEL), F32),
            pltpu.VMEM((T, 5 * D_MODEL), BF16),
            pltpu.VMEM((T, D_MODEL), F32),
            pltpu.VMEM((T, D_MODEL), F32),
            pltpu.VMEM((T, D_MODEL), F32),
            pltpu.VMEM((T, D_MODEL), F32),
            pltpu.VMEM((32 * SUBLANES, D_MODEL), F32),
            pltpu.VMEM((SUBLANES * SUBLANES, D_MODEL), F32),
            pltpu.VMEM((SUBLANES, T + HALO, LANES), F32),
        ],
        compiler_params=pltpu.CompilerParams(dimension_semantics=("arbitrary",), vmem_limit_bytes=VMEM_LIMIT),
    )(proj, proj, proj, x2d, tgt2d, meta_tile, conv_a_w, conv_a_b, ln_a_g, ln_a_b, b_a_out, conv_b_w, final_g,
      w_a, w_b, w_o, w_a_t, w_b_t, w_o_t)


def _input_bwd(dproj, ds1, x2d, meta_tile, norm_g, w_in_all, row_tile):
    seq = x2d.shape[0]
    n_steps = seq // row_tile
    meta_block = seq // TILE

    def backward(dp_ref, ds1_ref, s0_ref, g_ref, w_ref, out_ref, vec_ref):
        dh = _dot_nt(dp_ref[:, 0:COLS], w_ref[0])
        for j in range(1, N_DEV):
            dh = dh + _dot_nt(dp_ref[:, j * COLS:(j + 1) * COLS], w_ref[j])
        s0v = s0_ref[...]
        r = lax.rsqrt(_rowmean(s0v * s0v) + EPS)
        gh = dh * g_ref[...]
        out_ref[...] = ds1_ref[...] + r * gh - s0v * ((r * r * r) * _rowmean(gh * s0v))
        vec_ref[ROW_NORM_G:ROW_NORM_G + 1, :] += _colsum(dh * (s0v * r))

    def body(dp_ref, ds1_ref, x_ref, dpm_ref, ds1m_ref, meta_ref, g_ref, w_ref, gx_ref, small_ref, gmeta_buf):
        t = pl.program_id(0)

        @pl.when(t == 0)
        def _():
            small_ref[...] = jnp.zeros(small_ref.shape, F32)

        backward(dp_ref, ds1_ref, x_ref, g_ref, w_ref, gx_ref, small_ref)

        @pl.when(t == n_steps - 1)
        def _():
            backward(dpm_ref, ds1m_ref, meta_ref, g_ref, w_ref, gmeta_buf, small_ref)
            small_ref[ROW_META:ROW_META + N_META, :] = gmeta_buf[TILE - N_META:TILE, :]

    return pl.pallas_call(
        body, name="input_bwd", grid=(n_steps,),
        out_shape=[jax.ShapeDtypeStruct(x2d.shape, F32), jax.ShapeDtypeStruct((SMALL_B_ROWS, D_MODEL), F32)],
        in_specs=[pl.BlockSpec((row_tile, D_IN), lambda t: (t, 0)),
                  pl.BlockSpec((row_tile, D_MODEL), lambda t: (t, 0)),
                  pl.BlockSpec((row_tile, D_MODEL), lambda t: (t, 0)),
                  pl.BlockSpec((TILE, D_IN), lambda t: (meta_block, 0)),
                  pl.BlockSpec((TILE, D_MODEL), lambda t: (meta_block, 0)),
                  _VMEM, _VMEM, _resident((N_DEV, D_MODEL, COLS))],
        out_specs=[pl.BlockSpec((row_tile, D_MODEL), lambda t: (t, 0)), _VMEM],
        scratch_shapes=[pltpu.VMEM((TILE, D_MODEL), F32)],
        compiler_params=pltpu.CompilerParams(dimension_semantics=("arbitrary",), vmem_limit_bytes=VMEM_LIMIT),
    )(dproj, ds1, x2d, dproj, ds1, meta_tile, norm_g, w_in_all)


def _grad_w_in_half(pos, h_t, dproj, k_tile, other_side, rides, name, after=None, add_to=None, narrow=False):
    n_k = dproj.shape[0] // k_tile
    order = [] if after is None else [after]
    summing = add_to is not None
    assert not (summing and narrow)

    def column_block(q, k, pos_ref):
        return k, 2 * q + (1 - pos_ref[2] if other_side else pos_ref[2])

    def body(pos_ref, h_ref, dp_ref, *refs):
        acc = refs[-1]

        @pl.when(pl.program_id(1) == 0)
        def _():
            acc[...] = refs[0][...].astype(F32) if summing else jnp.zeros(acc.shape, F32)

        acc[...] += _dot(h_ref[...], dp_ref[...])

        if summing or narrow:
            @pl.when(pl.program_id(1) == n_k - 1)
            def _():
                refs[-2][...] = acc[...].astype(BF16)

        if summing:
            @pl.when((pl.program_id(1) == n_k - 1) & (pl.program_id(0) == 2 * pos_ref[0] + pos_ref[1]))
            def _():
                refs[-3][...] = acc[...]

    ride = [a for _, arrays in rides for a in arrays]
    n_arr = len(ride)
    ride_shapes, ride_sems = _ride_shapes(rides)
    block = (None, D_MODEL, COLS)
    extra_in = [add_to] if summing else []
    extra_in_specs = [pl.BlockSpec(block, lambda q, k, pos_ref: (q, 0, 0))] if summing else []
    extra_out = [jax.ShapeDtypeStruct((4, D_MODEL, COLS), BF16)] if summing else []
    extra_out_specs = [pl.BlockSpec(block, lambda q, k, pos_ref: (q ^ (2 * pos_ref[0] + pos_ref[1]), 0, 0))] \
        if summing else []
    body = _riding(body, 3 + len(extra_in) + len(order), 1 + len(extra_out), rides,
                   lambda: (pl.program_id(0) == 0) & (pl.program_id(1) == 0),
                   lambda: (pl.program_id(0) == 3) & (pl.program_id(1) == n_k - 1))
    return pl.pallas_call(
        body, name=name,
        out_shape=[jax.ShapeDtypeStruct((1 if summing else 4, D_MODEL, COLS), BF16 if narrow else F32)]
        + extra_out + ride_shapes,
        grid_spec=pltpu.PrefetchScalarGridSpec(
            num_scalar_prefetch=1, grid=(4, n_k),
            in_specs=[pl.BlockSpec((D_MODEL, k_tile), lambda q, k, pos_ref: (0, k)),
                      pl.BlockSpec((k_tile, COLS), column_block)] + extra_in_specs + [_ANY] * (len(order) + n_arr),
            out_specs=[pl.BlockSpec(block, lambda q, k, pos_ref: (0 if summing else q, 0, 0))]
            + extra_out_specs + [_ANY] * n_arr,
            scratch_shapes=([pltpu.VMEM((D_MODEL, COLS), F32)] if summing or narrow else []) + ride_sems),
        compiler_params=pltpu.CompilerParams(dimension_semantics=("arbitrary", "arbitrary"),
                                             vmem_limit_bytes=VMEM_LIMIT),
    )(pos, h_t, dproj, *extra_in, *order, *ride)


def _grad_w_out(lhs, rhs, k_tile, after):
    tp = lhs.shape[1]

    def body(a_ref, b_ref, after_ref, o_ref):
        @pl.when(pl.program_id(1) == 0)
        def _():
            o_ref[...] = jnp.zeros(o_ref.shape, F32)

        o_ref[...] += _dot_tn(a_ref[...], b_ref[...]).reshape(N_DEV, ROWS_OUT, D_MODEL)

    return pl.pallas_call(
        body, name="grad_w_out", grid=(3, tp // k_tile),
        out_shape=jax.ShapeDtypeStruct((N_DEV, 3, ROWS_OUT, D_MODEL), F32),
        in_specs=[pl.BlockSpec((None, k_tile, D_MODEL), lambda w, k: (w, k, 0)),
                  pl.BlockSpec((None, k_tile, D_MODEL), lambda w, k: (w, k, 0)), _ANY],
        out_specs=pl.BlockSpec((N_DEV, None, ROWS_OUT, D_MODEL), lambda w, k: (0, w, 0, 0)),
        compiler_params=pltpu.CompilerParams(dimension_semantics=("arbitrary", "arbitrary"),
                                             vmem_limit_bytes=VMEM_LIMIT),
    )(lhs, rhs, after)


def _adamw_math(w, g, m, v):
    m = ADAM_B1 * m + (1.0 - ADAM_B1) * g
    v = ADAM_B2 * v + (1.0 - ADAM_B2) * (g * g)
    m_hat = m / (1.0 - ADAM_B1 ** ADAM_STEP)
    v_hat = v / (1.0 - ADAM_B2 ** ADAM_STEP)
    delta = -ADAM_LR * (m_hat / (jnp.sqrt(v_hat) + ADAM_EPS) + ADAM_WD * w)
    return delta, m, v


def _adamw_sharded(pos, mine, theirs, landed, weights, row_tile, name, after=None):
    order = [] if after is None else [after]
    rows, n = weights[0][0].shape
    n_slots = mine.shape[0]
    per_shard = rows // row_tile
    assert per_shard == 1 or len(weights) == 1

    def mine_map(j, t, pos_ref):
        chip = 2 * pos_ref[0] + pos_ref[1]
        return {N_DEV: 2 * chip + pos_ref[2], 4: chip, 1: 0}[n_slots], j * per_shard + t, 0

    def theirs_map(j, t, pos_ref):
        return 2 * pos_ref[0] + pos_ref[1], j * per_shard + t, 0

    def body(pos_ref, mine_ref, *refs):
        if theirs is not None:
            g = mine_ref[...] + refs[0][...]
            refs = refs[1:]
        else:
            g = mine_ref[...]
        land_ref, refs = refs[0], refs[1:]
        ins, outs = refs[:3 * len(weights)], refs[3 * len(weights) + len(order):]
        for k in range(3):
            g = g + land_ref[k].astype(F32)
        for j in range(len(weights)):
            @pl.when(pl.program_id(0) == j)
            def _(j=j):
                w_ref, m_ref, v_ref = ins[3 * j:3 * j + 3]
                delta, m_new, v_new = _adamw_math(w_ref[...], g, m_ref[...], v_ref[...])
                for ref, val in zip(outs[4 * j:4 * j + 4], (g, delta, m_new, v_new)):
                    ref[...] = val

    tile = pl.BlockSpec((row_tile, n), lambda j, t, pos_ref: (t, 0))
    res = pl.pallas_call(
        body, name=name,
        out_shape=[jax.ShapeDtypeStruct((rows, n), F32)] * (4 * len(weights)),
        grid_spec=pltpu.PrefetchScalarGridSpec(
            num_scalar_prefetch=1, grid=(len(weights), per_shard),
            in_specs=[pl.BlockSpec((None, row_tile, n), mine_map)]
            + ([pl.BlockSpec((None, row_tile, n), theirs_map)] if theirs is not None else [])
            + [pl.BlockSpec((3, row_tile, n), lambda j, t, pos_ref: (0, j * per_shard + t, 0))]
            + [tile] * (3 * len(weights)) + [_ANY] * len(order),
            out_specs=[tile] * (4 * len(weights))),
        compiler_params=pltpu.CompilerParams(dimension_semantics=("arbitrary", "arbitrary")),
    )(pos, mine, *([theirs] if theirs is not None else []), landed, *[a for wmv in weights for a in wmv], *order)
    return [res[4 * j:4 * j + 4] for j in range(len(weights))]


def _adamw_small(gathered, gathered_cols, params):
    n_par, n_src = len(params), len(gathered)

    def body(*refs):
        g_refs, gc_refs = refs[:n_src], refs[n_src:2 * n_src]
        ins = refs[2 * n_src:2 * n_src + 3 * n_par]
        outs = refs[2 * n_src + 3 * n_par:]
        loss_ref = outs[4 * n_par]

        def reduced(ref, row, n_rows):
            g = ref[0, row:row + n_rows, :]
            for d in range(1, N_DEV):
                g = g + ref[d, row:row + n_rows, :]
            return g

        for p, (src, row, n_rows, sharded, _, _, _) in enumerate(params):
            g = reduced((gc_refs if sharded else g_refs)[src], row, n_rows)
            w_ref, m_ref, v_ref = ins[3 * p:3 * p + 3]
            delta, m_new, v_new = _adamw_math(w_ref[...], g, m_ref[...], v_ref[...])
            outs[4 * p][...] = g
            outs[4 * p + 1][...] = delta
            outs[4 * p + 2][...] = m_new
            outs[4 * p + 3][...] = v_new
        loss = jnp.sum(reduced(g_refs[0], ROW_LOSS, 1), axis=1, keepdims=True)
        loss_ref[...] = jnp.broadcast_to(loss, loss_ref.shape)

    out_shape = []
    for (_, _, _, _, w, _, _) in params:
        out_shape += [jax.ShapeDtypeStruct(w.shape, F32)] * 4
    out_shape.append(jax.ShapeDtypeStruct((1, LANES), F32))
    flat = [a for (_, _, _, _, w, m, v) in params for a in (w, m, v)]
    return pl.pallas_call(
        body, name="adamw_small", out_shape=out_shape,
        in_specs=[_VMEM] * (2 * n_src + len(flat)), out_specs=[_VMEM] * len(out_shape),
    )(*gathered, *gathered_cols, *flat)


def _pad_rows(a, rows):
    return jnp.concatenate([a, jnp.zeros((rows - a.shape[0], a.shape[1]), a.dtype)], axis=0)


def kernel(x, meta_tokens, norm_g, w_in, conv_a_w, conv_a_b, ln_a_g, ln_a_b, w_a_out, b_a_out, conv_b_w, w_b_out, w_out, final_g, loss_target, m_meta_tokens, m_norm_g, m_w_in, m_conv_a_w, m_conv_a_b, m_ln_a_g, m_ln_a_b, m_w_a_out, m_b_a_out, m_conv_b_w, m_w_b_out, m_w_out, m_final_g, v_meta_tokens, v_norm_g, v_w_in, v_conv_a_w, v_conv_a_b, v_ln_a_g, v_ln_a_b, v_w_a_out, v_b_a_out, v_conv_b_w, v_w_b_out, v_w_out, v_final_g):
    seq = x.shape[1]
    assert x.shape == (1, seq, D_MODEL) and seq % TILE == 0 and w_in.shape == (1, D_MODEL, COLS)
    n_tiles = seq // TILE + 1
    tp = n_tiles * TILE
    pos = jnp.stack([lax.axis_index("x"), lax.axis_index("y"), lax.axis_index("c")]).astype(jnp.int32)
    me = 4 * pos[0] + 2 * pos[1] + pos[2]
    x2d = x[0]
    tgt2d = loss_target[0]

    small = jnp.concatenate([meta_tokens, _pad_rows(conv_a_w[0], 32), _pad_rows(conv_b_w[0], SUBLANES)], axis=0)
    final_g2 = final_g.reshape(1, D_MODEL)

    w_out_shards = [w[0].astype(BF16) for w in (w_a_out, w_b_out, w_out)]
    h_t, proj, meta_tile, small_params, w_in_all, *w_out_all = _gather_norm_proj(
        pos, x2d, small[None], norm_g, w_in[0].astype(BF16), w_out_shards, 3)
    small_params = small_params.transpose(1, 0, 2).reshape(small.shape[0], D_MODEL)
    conv_a_full, conv_b_full = small_params[N_META:N_META + 32], small_params[N_META + 32:]
    w_out_all = [w.reshape(D_MODEL, D_MODEL) for w in w_out_all]
    w_out_all_t = [w.T for w in w_out_all]
    dproj, ds1, lhs, rhs, small_a = _fused_pass(
        proj, x2d, tgt2d, meta_tile, conv_a_full, conv_a_b, ln_a_g, ln_a_b, b_a_out, conv_b_full, final_g2,
        w_out_all[0], w_out_all[1], w_out_all[2], w_out_all_t[0], w_out_all_t[1], w_out_all_t[2], n_tiles)
    k_tile = tp // 3
    gw_far, small_a_all = _grad_w_in_half(pos, h_t, dproj, k_tile, True, [("all", (small_a[None],))], "grad_w_in_far",
                                          narrow=True)
    sems, sent, landing, token = _start_exchanges([("sibling_half", (gw_far,))], "rs_far_start")
    gw_out = _grad_w_out(lhs, rhs, k_tile, token).reshape(N_DEV, 3 * ROWS_OUT, D_MODEL)
    (their_in,) = _wait_exchanges([("sibling_half", 1)], sems, sent, landing, gw_out, "rs_far_wait")
    sems_o, sent_o, landing_o, token = _start_exchanges([("sibling", (gw_out,))], "rs_out_start")
    gw_near, parts_in = _grad_w_in_half(pos, h_t, dproj, k_tile, False, [], "grad_w_in_near", after=token,
                                        add_to=their_in)
    sems_i, sent_i, landing_i, token = _start_exchanges([("chips_by_relation", (parts_in,))], "rs_chips_in_start")
    gw_out, their_out = _wait_exchanges([("sibling", 1)], sems_o, sent_o, landing_o, token, "rs_out_wait",
                                        keep_sources=True)
    parts_out = _chip_partial(pos, gw_out, their_out, (1, 2, 3), BF16, ROWS_OUT, "rs_parts_w_out")
    sems_o, sent_o, landing_o, token = _start_exchanges([("chips", (parts_out,))], "rs_chips_out_start")
    grad_x, small_b = _input_bwd(dproj, ds1, x2d, meta_tile, norm_g + token[0, 0], w_in_all, min(512, seq))

    sems_s, sent_s, landing_s, token = _start_exchanges([("all", (small_b[None],))], "gather_small_grads_start")
    (land_in,) = _wait_exchanges([("chips_by_relation", 1)], sems_i, sent_i, landing_i, token, "rs_chips_in_wait")
    (res_in,) = _adamw_sharded(pos, gw_near, None, land_in, [(w_in[0], m_w_in[0], v_w_in[0])], 128, "adamw_w_in")
    (land_out,) = _wait_exchanges([("chips", 1)], sems_o, sent_o, landing_o, res_in[0], "rs_chips_out_wait")
    res_out = _adamw_sharded(
        pos, gw_out, their_out, land_out,
        [(w_a_out[0], m_w_a_out[0], v_w_a_out[0]), (w_b_out[0], m_w_b_out[0], v_w_b_out[0]),
         (w_out[0], m_w_out[0], v_w_out[0])], ROWS_OUT, "adamw_w_out")
    (small_b_all,) = _wait_exchanges([("all", 1)], sems_s, sent_s, landing_s, res_out[2][0], "gather_small_grads_wait")
    small_grads = [small_a_all, small_b_all]
    small_cols = [lax.dynamic_slice_in_dim(g, me * LANES, LANES, axis=2) for g in small_grads]
    params = [
        (1, ROW_META, N_META, True, meta_tokens, m_meta_tokens, v_meta_tokens),
        (1, ROW_NORM_G, 1, False, norm_g, m_norm_g, v_norm_g),
        (0, ROW_CONV_A_W, CONV_A, True, conv_a_w[0], m_conv_a_w[0], v_conv_a_w[0]),
        (0, ROW_CONV_A_B, 1, False, conv_a_b, m_conv_a_b, v_conv_a_b),
        (0, ROW_LN_G, 1, False, ln_a_g, m_ln_a_g, v_ln_a_g),
        (0, ROW_LN_B, 1, False, ln_a_b, m_ln_a_b, v_ln_a_b),
        (0, ROW_B_A_OUT, 1, False, b_a_out, m_b_a_out, v_b_a_out),
        (0, ROW_CONV_B_W, CONV_B, True, conv_b_w[0], m_conv_b_w[0], v_conv_b_w[0]),
        (0, ROW_FINAL_G, 1, False, final_g2, m_final_g.reshape(1, D_MODEL), v_final_g.reshape(1, D_MODEL)),
    ]
    res_small = _adamw_small(small_grads, small_cols, params)
    loss = res_small[-1][0, 0]

    def small_res(p, kind, shape):
        return res_small[4 * p + kind].reshape(shape)

    per_weight = []
    for kind in range(4):
        per_weight.append([
            small_res(0, kind, meta_tokens.shape),
            small_res(1, kind, norm_g.shape),
            res_in[kind].reshape(w_in.shape),
            small_res(2, kind, conv_a_w.shape),
            small_res(3, kind, conv_a_b.shape),
            small_res(4, kind, ln_a_g.shape),
            small_res(5, kind, ln_a_b.shape),
            res_out[0][kind].reshape(w_a_out.shape),
            small_res(6, kind, b_a_out.shape),
            small_res(7, kind, conv_b_w.shape),
            res_out[1][kind].reshape(w_b_out.shape),
            res_out[2][kind].reshape(w_out.shape),
            small_res(8, kind, final_g.shape),
        ])
    return (loss, grad_x.reshape(x.shape), *per_weight[0], *per_weight[1], *per_weight[2], *per_weight[3])
```

```python
import functools

import jax
import jax.numpy as jnp
from jax import lax
from jax.experimental import pallas as pl
from jax.experimental.pallas import tpu as pltpu

D_MODEL = 1024
N_META = 16
N_DEV = 8
D_IN = 9 * D_MODEL
COLS = D_IN // N_DEV
ROWS_OUT = D_MODEL // N_DEV
CONV_A = 31
CONV_B = 3
EPS = 1e-6

ADAM_LR = 0.001
ADAM_B1 = 0.9
ADAM_B2 = 0.999
ADAM_EPS = 1e-08
ADAM_WD = 0.01
ADAM_STEP = 10

TILE = 128
LANES = 128
N_CHUNK = D_MODEL // LANES
HALO = 32
SUBLANES = 8
VMEM_LIMIT = 56 * 1024 * 1024

ROW_FINAL_G, ROW_B_A_OUT, ROW_LN_G, ROW_LN_B, ROW_CONV_A_B, ROW_LOSS = 0, 1, 2, 3, 4, 5
ROW_CONV_A_W, ROW_CONV_B_W, SMALL_A_ROWS = 8, 40, 48
ROW_NORM_G, ROW_META, SMALL_B_ROWS = 0, 8, 24

MESH = pl.DeviceIdType.MESH
_ANY = pl.BlockSpec(memory_space=pl.ANY)
_VMEM = pl.BlockSpec(memory_space=pltpu.VMEM)
BF16 = jnp.bfloat16
F32 = jnp.float32


def _resident(shape):
    return pl.BlockSpec(shape, lambda *_: (0,) * len(shape), pipeline_mode=pl.Buffered(1))


def _sigmoid(v):
    return jax.nn.sigmoid(v)


def _dot(a, b):
    return jnp.dot(a, b, preferred_element_type=F32)


def _dot_nt(a, b):
    return lax.dot_general(a, b, (((1,), (1,)), ((), ())), preferred_element_type=F32)


def _dot_tn(a, b):
    return lax.dot_general(a, b, (((0,), (0,)), ((), ())), preferred_element_type=F32)


def _colsum(v):
    return jnp.sum(v, axis=0, keepdims=True)


def _rowmean(v):
    parts = [v[:, LANES * c:LANES * (c + 1)] for c in range(v.shape[1] // LANES)]
    return jnp.sum(functools.reduce(jnp.add, parts), axis=-1, keepdims=True) * (1.0 / v.shape[1])


def _fold8(v):
    parts = [v[SUBLANES * g:SUBLANES * (g + 1)] for g in range(v.shape[0] // SUBLANES)]
    return functools.reduce(jnp.add, parts)


def _sibling_copies(srcs, dsts, send_sems, recv_sems):
    x, y, c = lax.axis_index("x"), lax.axis_index("y"), lax.axis_index("c")
    return [pltpu.make_async_remote_copy(
        src_ref=src.at[2 * q + (1 - c)], dst_ref=dst.at[q],
        send_sem=send_sems.at[4 * a + q], recv_sem=recv_sems.at[4 * a + q],
        device_id=(x, y, 1 - c), device_id_type=MESH)
        for a, (src, dst) in enumerate(zip(srcs, dsts)) for q in range(4)]


def _chip_copies(srcs, dsts, send_sems, recv_sems):
    x, y, c = lax.axis_index("x"), lax.axis_index("y"), lax.axis_index("c")
    targets = [(x, 1 - y, c), (1 - x, y, c), (1 - x, 1 - y, c)]
    return [pltpu.make_async_remote_copy(
        src_ref=src.at[k], dst_ref=dst.at[k],
        send_sem=send_sems.at[3 * a + k], recv_sem=recv_sems.at[3 * a + k],
        device_id=targets[k], device_id_type=MESH)
        for a, (src, dst) in enumerate(zip(srcs, dsts)) for k in range(3)]


def _sibling_half_copies(srcs, dsts, send_sems, recv_sems):
    x, y, c = lax.axis_index("x"), lax.axis_index("y"), lax.axis_index("c")
    return [pltpu.make_async_remote_copy(
        src_ref=src.at[q], dst_ref=dst.at[q],
        send_sem=send_sems.at[4 * a + q], recv_sem=recv_sems.at[4 * a + q],
        device_id=(x, y, 1 - c), device_id_type=MESH)
        for a, (src, dst) in enumerate(zip(srcs, dsts)) for q in range(4)]


def _all_copies(srcs, dsts, send_sems, recv_sems):
    x, y, c = lax.axis_index("x"), lax.axis_index("y"), lax.axis_index("c")
    mine = 4 * x + 2 * y + c
    copies = []
    for a, (src, dst) in enumerate(zip(srcs, dsts)):
        copies.append(pltpu.make_async_copy(src.at[0], dst.at[mine], send_sems.at[N_DEV * a]))
        for k in range(1, N_DEV):
            copies.append(pltpu.make_async_remote_copy(
                src_ref=src.at[0], dst_ref=dst.at[mine],
                send_sem=send_sems.at[N_DEV * a + k], recv_sem=recv_sems.at[N_DEV * a + k],
                device_id=(x ^ (k >> 2), y ^ ((k >> 1) & 1), c ^ (k & 1)), device_id_type=MESH))
    return copies


def _chip_copies_by_relation(srcs, dsts, send_sems, recv_sems):
    return _chip_copies([src.at[pl.ds(1, 3)] for src in srcs], dsts, send_sems, recv_sems)


_EXCHANGES = {"sibling": (4, _sibling_copies, 4), "sibling_half": (4, _sibling_half_copies, 4),
              "chips": (3, _chip_copies, 3), "chips_by_relation": (3, _chip_copies_by_relation, 3),
              "all": (N_DEV, _all_copies, N_DEV)}


def _exchange_shapes(kind, arrays):
    per_array, _, slots = _EXCHANGES[kind]
    out_shape = [jax.ShapeDtypeStruct((slots,) + a.shape[1:], a.dtype) for a in arrays]
    sems = [pltpu.SemaphoreType.DMA((per_array * len(arrays),))] * 2
    return out_shape, sems


def _ride_shapes(rides):
    shapes, sems = [], []
    for kind, arrays in rides:
        ride_shapes, ride_sems = _exchange_shapes(kind, arrays)
        shapes += ride_shapes
        sems += ride_sems
    return shapes, sems


def _riding(body, n_in, n_out, rides, is_first, is_last):
    counts = [len(arrays) for _, arrays in rides]
    n_arr = sum(counts)

    def wrapped(*refs):
        ins, srcs = refs[:n_in], refs[n_in:n_in + n_arr]
        outs = refs[n_in + n_arr:n_in + n_arr + n_out]
        dsts = refs[n_in + n_arr + n_out:n_in + 2 * n_arr + n_out]
        first_sem = len(refs) - 2 * len(rides)
        scratch, sems = refs[n_in + 2 * n_arr + n_out:first_sem], refs[first_sem:]

        def copies():
            made, at = [], 0
            for r, ((kind, _), n) in enumerate(zip(rides, counts)):
                made += _EXCHANGES[kind][1](srcs[at:at + n], dsts[at:at + n], sems[2 * r], sems[2 * r + 1])
                at += n
            return made

        @pl.when(is_first())
        def _():
            for cp in copies():
                cp.start()

        body(*ins, *outs, *scratch)

        @pl.when(is_last())
        def _():
            for cp in copies():
                cp.wait()

    return wrapped


_HBM = pl.BlockSpec(memory_space=pltpu.HBM)
_SEM = pl.BlockSpec(memory_space=pltpu.SEMAPHORE)
_FLOWS = pltpu.SideEffectType.DATAFLOW_SIDE_EFFECTING


def _start_exchanges(rides, name):
    arrays = [a for _, group in rides for a in group]
    shapes, sems = _ride_shapes(rides)
    n_arr, n_sem = len(arrays), len(sems)

    def body(*refs):
        srcs, lands = refs[:n_arr], refs[n_arr:2 * n_arr]
        sem_refs, token = refs[2 * n_arr:2 * n_arr + n_sem], refs[-1]
        at = 0
        for r, (kind, group) in enumerate(rides):
            n = len(group)
            for cp in _EXCHANGES[kind][1](srcs[at:at + n], lands[at:at + n], sem_refs[2 * r], sem_refs[2 * r + 1]):
                cp.start()
            at += n
        token[...] = jnp.zeros(token.shape, token.dtype)

    in_hbm = [pltpu.HBM(a.shape, a.dtype) for a in arrays]
    land_hbm = [pltpu.HBM(sh.shape, sh.dtype) for sh in shapes]
    res = pl.pallas_call(
        body, name=name,
        out_shape=(*sems, *in_hbm, *land_hbm, jax.ShapeDtypeStruct((SUBLANES, LANES), F32)),
        in_specs=[_HBM] * (2 * n_arr), out_specs=(*[_SEM] * n_sem, *[_HBM] * (2 * n_arr), _VMEM),
        input_output_aliases={i: n_sem + i for i in range(2 * n_arr)},
        compiler_params=pltpu.CompilerParams(has_side_effects=_FLOWS),
    )(*[pltpu.with_memory_space_constraint(a, pltpu.HBM) for a in arrays],
      *[pltpu.with_memory_space_constraint(lax.empty(sh.shape, sh.dtype), pltpu.HBM) for sh in shapes])
    return res[:n_sem], res[n_sem:n_sem + n_arr], res[n_sem + n_arr:n_sem + 2 * n_arr], res[-1]


def _wait_exchanges(kinds, sems, arrays, lands, after, name, keep_sources=False):
    n_arr, n_sem = len(arrays), len(sems)

    def body(*refs):
        srcs, dsts = refs[:n_arr], refs[n_arr:2 * n_arr]
        sem_refs = refs[2 * n_arr:2 * n_arr + n_sem]
        at = 0
        for r, (kind, n) in enumerate(kinds):
            for cp in _EXCHANGES[kind][1](srcs[at:at + n], dsts[at:at + n], sem_refs[2 * r], sem_refs[2 * r + 1]):
                cp.wait()
            at += n

    hbm = [pltpu.HBM(a.shape, a.dtype) for a in (*arrays, *lands)]
    return pl.pallas_call(
        body, name=name, out_shape=tuple(hbm),
        in_specs=[_HBM] * (2 * n_arr) + [_SEM] * n_sem + [_ANY], out_specs=tuple([_HBM] * (2 * n_arr)),
        input_output_aliases={i: i for i in range(2 * n_arr)},
        compiler_params=pltpu.CompilerParams(has_side_effects=_FLOWS),
    )(*arrays, *lands, *sems, after)[0 if keep_sources else n_arr:]


def _chip_partial(pos, mine, theirs, relations, out_dtype, row_tile, name):
    n_slots, m, n = mine.shape
    q0 = relations[0]

    def chip_of(qi, pos_ref):
        q = qi + q0
        return pos_ref[0] ^ (q >> 1), pos_ref[1] ^ (q & 1)

    def mine_map(qi, t, pos_ref):
        px, py = chip_of(qi, pos_ref)
        return (4 * px + 2 * py + pos_ref[2] if n_slots == N_DEV else 2 * px + py), t, 0

    def theirs_map(qi, t, pos_ref):
        px, py = chip_of(qi, pos_ref)
        return 2 * px + py, t, 0

    def body(pos_ref, a_ref, b_ref, o_ref):
        o_ref[...] = (a_ref[...] + b_ref[...]).astype(out_dtype)

    return pl.pallas_call(
        body, name=name,
        out_shape=jax.ShapeDtypeStruct((len(relations), m, n), out_dtype),
        grid_spec=pltpu.PrefetchScalarGridSpec(
            num_scalar_prefetch=1, grid=(len(relations), m // row_tile),
            in_specs=[pl.BlockSpec((None, row_tile, n), mine_map), pl.BlockSpec((None, row_tile, n), theirs_map)],
            out_specs=pl.BlockSpec((None, row_tile, n), lambda qi, t, pos_ref: (qi, t, 0))),
        compiler_params=pltpu.CompilerParams(dimension_semantics=("arbitrary", "arbitrary")),
    )(pos, mine, theirs)


PARTS = ((0, 512), (512, 640))


def _gather_norm_proj(pos, x2d, small_shard, norm_g, w_in_shard, w_out_shards, n_chunk):
    seq = x2d.shape[0]
    n_tiles = seq // TILE + 1
    tp = n_tiles * TILE
    n_parts = len(PARTS)
    widest = max(width for _, width in PARTS)
    units = [(s, u) for s in range(2) for u in range(n_parts)]
    for first in (2, 5):
        units += [(first + j, u) for u in range(n_parts) for j in range(2)] + [(first + 2, u) for u in range(n_parts)]
    n_units = len(units)
    n_steps = n_tiles + n_units
    chunk = tp // n_chunk

    def body(pos_ref, x_ref, g_ref, small_ref, win_ref, wa_ref, wb_ref, wo_ref,
             ht_ref, proj_ref, meta_ref, small_all, win_all, wa_all, wb_all, wo_all,
             h_all, wbuf, rbuf, small_buf, send_sems, recv_sems, local_sems, small_send, small_recv):
        g = pl.program_id(0)
        x, y, c = lax.axis_index("x"), lax.axis_index("y"), lax.axis_index("c")
        me, sibling = (x, y, c), (x, y, 1 - c)
        chips = [(1 - x, y), (x, 1 - y), (1 - x, 1 - y)]
        shards = (win_ref, wa_ref, wb_ref, wo_ref)
        gathered = (win_all, wa_all, wb_all, wo_all)
        n_arrays = len(shards)
        blocks = [me, sibling] + [(*chip, c) for chip in chips] + [(*chip, 1 - c) for chip in chips]

        def index(block):
            px, py, pc = block
            return 4 * px + 2 * py + pc

        def part(ref, a, u):
            return ref.at[:, pl.ds(PARTS[u][0], PARTS[u][1])] if a == 0 else ref

        def slot(a, block, u):
            return part(gathered[a].at[index(block)], a, u)

        def sem(a, k, u):
            return n_parts * k + u if a == 0 else 7 * n_parts + 7 * (a - 1) + k

        def copy(a, k, block, to, u=0, from_shard=False):
            return pltpu.make_async_remote_copy(
                src_ref=part(shards[a], a, u) if from_shard else slot(a, block, u), dst_ref=slot(a, block, u),
                send_sem=send_sems.at[sem(a, k, u)], recv_sem=recv_sems.at[sem(a, k, u)],
                device_id=to, device_id_type=MESH)

        def keep(a):
            return pltpu.make_async_copy(shards[a], gathered[a].at[index(me)], local_sems.at[a])

        def load(m):
            s, u = units[m]
            src = part(win_ref, 0, u) if s == 0 else slot(0, blocks[s], u)
            return pltpu.make_async_copy(src, wbuf.at[m % 2, :, 0:PARTS[u][1]], local_sems.at[n_arrays + m % 2])

        def store(m):
            s, u = units[m]
            col0 = pl.multiple_of(index(blocks[s]) * COLS + PARTS[u][0], LANES)
            return pltpu.make_async_copy(rbuf.at[m % 2, :, 0:PARTS[u][1]],
                                         proj_ref.at[:, pl.ds(col0, PARTS[u][1])], local_sems.at[n_arrays + 2 + m % 2])

        def by_x(a, u):
            return u == 0 if a == 0 else a < 3

        def relay(a, u=0):
            src, to = (blocks[3], blocks[2]) if by_x(a, u) else (blocks[2], blocks[3])
            return copy(a, 3, src, to, u)

        def arrive(m):
            s, u = units[m]
            if s == 1:
                copy(0, 0, sibling, me, u).wait_recv()
            elif 2 <= s <= 4:
                copy(0, s - 1, blocks[s], me, u).wait_recv()
                copy(0, s + 2, blocks[s], sibling, u).start()
                if s < 4 and by_x(0, u) == (s == 3):
                    relay(0, u).start()
            elif s >= 5:
                copy(0, s - 1, blocks[s], me, u).wait_recv()

        def pass_on_out(j):
            for a in range(1, 4):
                copy(a, j + 1, blocks[2 + j], me).wait_recv()
                copy(a, j + 4, blocks[2 + j], sibling).start()
                if j < 2 and by_x(a, 0) == (j == 1):
                    relay(a).start()

        targets = [sibling, blocks[2], blocks[3]]
        relays_started = max(m for m, (s, u) in enumerate(units) if s in (2, 3) and by_x(0, u) == (s == 3)) - 1

        def small_copies():
            return _all_copies([small_ref], [small_all], small_send, small_recv)

        @pl.when(g == 0)
        def _():
            for cp in small_copies():
                cp.start()
            for a in range(n_arrays):
                keep(a).start()
            for u in range(n_parts):
                for k, to in enumerate(targets):
                    copy(0, k, me, to, u, from_shard=True).start()
            for a in range(1, 4):
                copy(a, 0, me, sibling, from_shard=True).start()
            load(0).start()

        @pl.when(g == n_tiles - 2)
        def _():
            for cp in small_copies():
                cp.wait()
            fetch = pltpu.make_async_copy(small_all, small_buf, local_sems.at[n_arrays + 4])
            fetch.start()
            fetch.wait()
            meta_ref[0:TILE - N_META, :] = jnp.zeros((TILE - N_META, D_MODEL), F32)
            meta_ref[TILE - N_META:TILE, :] = jnp.concatenate([small_buf[d, 0:N_META, :] for d in range(N_DEV)], axis=1)

        @pl.when(g < n_tiles)
        def _():
            s0 = jnp.where(g == n_tiles - 1, meta_ref[...], x_ref[...])
            r = lax.rsqrt(_rowmean(s0 * s0) + EPS)
            h32 = (s0 * r) * g_ref[...]
            ht_ref[...] = h32.T.astype(BF16)
            h_all[pl.ds(pl.multiple_of(g * TILE, TILE), TILE), :] = h32.astype(BF16)

        for m in range(n_units):
            @pl.when(g == n_tiles + m)
            def _(m=m):
                load(m).wait()
                if m + 1 < n_units:
                    arrive(m + 1)
                    load(m + 1).start()
                if m == relays_started:
                    for a in range(1, 4):
                        for k in (1, 2):
                            copy(a, k, me, targets[k], from_shard=True).start()
                if m == n_units - 2:
                    pass_on_out(0)
                    pass_on_out(1)
                if m >= 2:
                    store(m - 2).wait()

        m_now = jnp.maximum(g - n_tiles, 0)
        u_now = functools.reduce(jnp.add, [jnp.where(m_now == m, u, 0) for m, (_, u) in enumerate(units)])
        for u, (_, width) in enumerate(PARTS):
            @pl.when((g >= n_tiles) & (u_now == u))
            def _(width=width):
                w = wbuf[m_now % 2, :, 0:width]
                for r in range(n_chunk):
                    rbuf[m_now % 2, r * chunk:(r + 1) * chunk, 0:width] = _dot(h_all[r * chunk:(r + 1) * chunk, :], w)

        for m in range(n_units):
            @pl.when(g == n_tiles + m)
            def _(m=m):
                store(m).start()

        @pl.when(g == n_steps - 1)
        def _():
            pass_on_out(2)
            store(n_units - 2).wait()
            store(n_units - 1).wait()
            for a in range(1, 4):
                copy(a, 0, sibling, me).wait_recv()
                for j in range(3):
                    copy(a, 4 + j, blocks[5 + j], me).wait_recv()
            for a in range(n_arrays):
                for u in range(n_parts if a == 0 else 1):
                    for k, to in enumerate(targets):
                        copy(a, k, me, to, u, from_shard=True).wait_send()
                    relay(a, u).wait_send()
                    for j in range(3):
                        copy(a, 4 + j, blocks[2 + j], sibling, u).wait_send()
                keep(a).wait()

    n_x = n_tiles - 1
    return pl.pallas_call(
        body, name="gather_norm_proj",
        out_shape=[jax.ShapeDtypeStruct((D_MODEL, tp), BF16), jax.ShapeDtypeStruct((tp, D_IN), F32),
                   jax.ShapeDtypeStruct((TILE, D_MODEL), F32), jax.ShapeDtypeStruct((N_DEV,) + small_shard.shape[1:], F32),
                   jax.ShapeDtypeStruct((N_DEV,) + w_in_shard.shape, BF16)]
                  + [jax.ShapeDtypeStruct((N_DEV,) + w.shape, BF16) for w in w_out_shards],
        grid_spec=pltpu.PrefetchScalarGridSpec(
            num_scalar_prefetch=1, grid=(n_steps,),
            in_specs=[pl.BlockSpec((TILE, D_MODEL), lambda g, pos_ref: (jnp.minimum(g, n_x - 1), 0)),
                      _VMEM, _ANY, _ANY, _ANY, _ANY, _ANY],
            out_specs=[pl.BlockSpec((D_MODEL, TILE), lambda g, pos_ref: (0, jnp.minimum(g, n_tiles - 1))),
                       _ANY, _VMEM, _ANY, _ANY, _ANY, _ANY, _ANY],
            scratch_shapes=[pltpu.VMEM((tp, D_MODEL), BF16), pltpu.VMEM((2, D_MODEL, widest), BF16),
                            pltpu.VMEM((2, tp, widest), F32), pltpu.VMEM((N_DEV,) + small_shard.shape[1:], F32),
                            pltpu.SemaphoreType.DMA((7 * n_parts + 21,)), pltpu.SemaphoreType.DMA((7 * n_parts + 21,)),
                            pltpu.SemaphoreType.DMA((9,)),
                            pltpu.SemaphoreType.DMA((N_DEV,)), pltpu.SemaphoreType.DMA((N_DEV,))]),
        compiler_params=pltpu.CompilerParams(dimension_semantics=("arbitrary",), vmem_limit_bytes=VMEM_LIMIT),
    )(pos, x2d, norm_g, small_shard, w_in_shard, *w_out_shards)


C_AVAL, C_AGLU, C_AZ, C_BB, C_BC, C_BX, C_BZ, C_GA, C_GB = (k * D_MODEL for k in range(9))
S_AZ, S_BB, S_BZ, S_GA, S_GB = (k * D_MODEL for k in range(5))


def _fused_pass(proj, x2d, tgt2d, meta_tile, conv_a_w, conv_a_b, ln_a_g, ln_a_b, b_a_out, conv_b_w, final_g,
                w_a, w_b, w_o, w_a_t, w_b_t, w_o_t, n_tiles):
    T = TILE
    tp = n_tiles * T
    inv_d = 1.0 / D_MODEL

    def block_of(tile):
        return jnp.where(tile == 0, n_tiles - 1, tile - 1)

    def cur(i):
        return block_of(jnp.minimum(i, n_tiles - 1))

    def prev(i):
        return block_of(jnp.clip(i - 1, 0, n_tiles - 1))

    def xblk(i):
        return jnp.maximum(jnp.minimum(i, n_tiles - 1) - 1, 0)

    def body(proj_ref, aprev, cprev, x_ref, tgt_ref, meta_ref, caw_ref, cab_ref, lng_ref, lnb_ref, bao_ref, cbw_ref,
             fg_ref, wa_ref, wb_ref, wo_ref, wat_ref, wbt_ref, wot_ref,
             dproj_ref, ds1_ref, lhs_ref, rhs_ref, small_ref,
             ua0_buf, cb_buf, dua1_buf, dc3_buf, stage, ua1_buf, c3_buf,
             dpa_buf, dpb_buf, dcaw8, dcbw8, shift_buf):
        i = pl.program_id(0)
        this, before = i % 2, 1 - i % 2

        @pl.when(i == 0)
        def _init():
            for buf in (ua0_buf, cb_buf, dua1_buf, dc3_buf, dcaw8, dcbw8):
                buf[...] = jnp.zeros(buf.shape, buf.dtype)
            small_ref[...] = jnp.zeros(small_ref.shape, F32)

        @pl.when(i >= 1)
        def _emit_stage():
            dproj_ref[:, C_AZ:C_BC] = stage[:, S_AZ:S_BZ]
            dproj_ref[:, C_BZ:D_IN] = stage[:, S_BZ:S_GB + D_MODEL]

        @pl.when(i < n_tiles)
        def _front():
            def conv_chunk(cc, carry):
                c0 = pl.multiple_of(cc * LANES, LANES)
                lanes = pl.ds(c0, LANES)

                def col(base):
                    return pl.ds(pl.multiple_of(base + cc * LANES, LANES), LANES)

                ua0 = proj_ref[:, col(C_AVAL)] * _sigmoid(proj_ref[:, col(C_AGLU)])
                ua0_buf[this, 0:HALO, lanes] = ua0_buf[before, T:T + HALO, lanes]
                ua0_buf[this, HALO:HALO + T, lanes] = ua0
                acc = jnp.broadcast_to(cab_ref[:, lanes], (T, LANES))
                lead = HALO - (CONV_A - 1)
                for r in range(SUBLANES):
                    taps = [k for k in range(CONV_A) if (k + lead) % SUBLANES == r]
                    rows = T + SUBLANES * max((k + lead) // SUBLANES for k in taps)
                    if r:
                        shift_buf[r, 0:rows, :] = ua0_buf[this, pl.ds(r, rows), lanes]
                    for k in taps:
                        q = (k + lead) // SUBLANES
                        if r:
                            win = shift_buf[r, SUBLANES * q:SUBLANES * q + T, :]
                        else:
                            win = ua0_buf[this, pl.ds(SUBLANES * q, T), lanes]
                        acc = acc + caw_ref[k:k + 1, lanes] * win
                ua1_buf[:, lanes] = acc
                cb = proj_ref[:, col(C_BC)] * proj_ref[:, col(C_BX)]
                cb_buf[this, 0:SUBLANES, lanes] = cb_buf[before, T:T + SUBLANES, lanes]
                cb_buf[this, SUBLANES:SUBLANES + T, lanes] = cb
                lead_b = SUBLANES - (CONV_B - 1)
                acc3 = cbw_ref[0:1, lanes] * cb_buf[this, pl.ds(lead_b, T), lanes]
                for k in range(1, CONV_B):
                    acc3 = acc3 + cbw_ref[k:k + 1, lanes] * cb_buf[this, pl.ds(lead_b + k, T), lanes]
                c3_buf[:, lanes] = acc3
                return carry

            lax.fori_loop(0, N_CHUNK, conv_chunk, 0)

            ua1 = ua1_buf[...]
            xc = ua1 - _rowmean(ua1)
            rstd = lax.rsqrt(_rowmean(xc * xc) + EPS)
            xhat = xc * rstd
            ua2 = xhat * lng_ref[...] + lnb_ref[...]
            sg2 = _sigmoid(ua2)
            ua3 = ua2 * sg2
            a_z = proj_ref[:, C_AZ:C_AZ + D_MODEL]
            sz = _sigmoid(a_z)
            silu_az = a_z * sz
            lhs_ref[0] = (ua3 * silu_az).astype(BF16)
            b_z = proj_ref[:, C_BZ:C_BZ + D_MODEL]
            sbz = _sigmoid(b_z)
            silu_bz = b_z * sbz
            b_b = proj_ref[:, C_BB:C_BB + D_MODEL]
            c3 = c3_buf[...]
            ub = b_b * c3
            lhs_ref[1] = (ub * silu_bz).astype(BF16)

            ya = _dot(lhs_ref[0], wa_ref[...]) + bao_ref[...]
            yb = _dot(lhs_ref[1], wb_ref[...])
            sga = _sigmoid(proj_ref[:, C_GA:C_GA + D_MODEL])
            sgb = _sigmoid(proj_ref[:, C_GB:C_GB + D_MODEL])
            m_b = (sga * ya + sgb * yb).astype(BF16)
            lhs_ref[2] = m_b
            s0 = jnp.where(i == 0, meta_ref[...], x_ref[...])
            s1 = s0 + _dot(m_b, wo_ref[...])
            r1 = lax.rsqrt(_rowmean(s1 * s1) + EPS)
            y = (s1 * r1) * fg_ref[...]
            is_token = (i >= 1).astype(F32)
            err = (y - tgt_ref[...]) * is_token
            small_ref[ROW_LOSS:ROW_LOSS + 1, :] += (0.5 * inv_d) * _colsum(err * err)
            dy = err * inv_d
            small_ref[ROW_FINAL_G:ROW_FINAL_G + 1, :] += _colsum(dy * (s1 * r1))
            gy = dy * fg_ref[...]
            ds1 = r1 * gy - s1 * ((r1 * r1 * r1) * _rowmean(gy * s1))
            ds1_ref[...] = ds1
            ds1_b = ds1.astype(BF16)
            rhs_ref[2] = ds1_b
            dm = _dot(ds1_b, wot_ref[...])
            dya = dm * sga
            dyb = dm * sgb
            stage[:, S_GA:S_GA + D_MODEL] = (dya * ya * (1.0 - sga)).astype(BF16)
            stage[:, S_GB:S_GB + D_MODEL] = (dyb * yb * (1.0 - sgb)).astype(BF16)
            small_ref[ROW_B_A_OUT:ROW_B_A_OUT + 1, :] += _colsum(dya)
            dya_b = dya.astype(BF16)
            dyb_b = dyb.astype(BF16)
            rhs_ref[0] = dya_b
            rhs_ref[1] = dyb_b
            dpa_buf[...] = _dot(dya_b, wat_ref[...])
            dpb_buf[...] = _dot(dyb_b, wbt_ref[...])

            dpa = dpa_buf[...]
            stage[:, S_AZ:S_AZ + D_MODEL] = (dpa * ua3 * (sz + silu_az * (1.0 - sz))).astype(BF16)
            dua2 = dpa * silu_az * (sg2 + ua3 * (1.0 - sg2))
            small_ref[ROW_LN_G:ROW_LN_G + 1, :] += _colsum(dua2 * xhat)
            small_ref[ROW_LN_B:ROW_LN_B + 1, :] += _colsum(dua2)
            dxh = dua2 * lng_ref[...]
            dua1 = rstd * (dxh - _rowmean(dxh) - xhat * _rowmean(dxh * xhat))
            small_ref[ROW_CONV_A_B:ROW_CONV_A_B + 1, :] += _colsum(dua1)
            dua1_buf[this, 0:T, :] = dua1
            dua1_buf[before, T:T + HALO, :] = dua1[0:HALO]
            dpb = dpb_buf[...]
            stage[:, S_BZ:S_BZ + D_MODEL] = (dpb * ub * (sbz + silu_bz * (1.0 - sbz))).astype(BF16)
            dub = dpb * silu_bz
            stage[:, S_BB:S_BB + D_MODEL] = (dub * c3).astype(BF16)
            dc3 = dub * b_b
            dc3_buf[this, 0:T, :] = dc3
            dc3_buf[before, T:T + SUBLANES, :] = dc3[0:SUBLANES]

        @pl.when(i == n_tiles)
        def _no_later_tile():
            dua1_buf[before, T:T + HALO, :] = jnp.zeros((HALO, D_MODEL), F32)
            dc3_buf[before, T:T + SUBLANES, :] = jnp.zeros((SUBLANES, D_MODEL), F32)

        @pl.when(i >= 1)
        def _lagged():
            def convt_chunk(cc, carry):
                c0 = pl.multiple_of(cc * LANES, LANES)
                lanes = pl.ds(c0, LANES)

                def col(base):
                    return pl.ds(pl.multiple_of(base + cc * LANES, LANES), LANES)

                ua0 = ua0_buf[before, HALO:HALO + T, lanes]
                acc = jnp.zeros((T, LANES), F32)
                for r in range(SUBLANES):
                    shifts = [j for j in range(CONV_A) if j % SUBLANES == r]
                    rows = T + shifts[-1] - r
                    if r:
                        shift_buf[r, 0:rows, :] = dua1_buf[before, pl.ds(r, rows), lanes]
                    for j in shifts:
                        k = CONV_A - 1 - j
                        if r:
                            later = shift_buf[r, j - r:j - r + T, :]
                        else:
                            later = dua1_buf[before, pl.ds(j, T), lanes]
                        acc = acc + caw_ref[k:k + 1, lanes] * later
                        dcaw8[SUBLANES * k:SUBLANES * (k + 1), lanes] += _fold8(ua0 * later)
                a_val = aprev[:, col(0)]
                sg = _sigmoid(aprev[:, col(D_MODEL)])
                dproj_ref[:, col(C_AVAL)] = (acc * sg).astype(BF16)
                dproj_ref[:, col(C_AGLU)] = (acc * a_val * (sg * (1.0 - sg))).astype(BF16)

                cb = cb_buf[before, SUBLANES:SUBLANES + T, lanes]
                acc3 = jnp.zeros((T, LANES), F32)
                for j in range(CONV_B):
                    k = CONV_B - 1 - j
                    later = dc3_buf[before, pl.ds(j, T), lanes]
                    acc3 = acc3 + cbw_ref[k:k + 1, lanes] * later
                    dcbw8[SUBLANES * k:SUBLANES * (k + 1), lanes] += _fold8(cb * later)
                dproj_ref[:, col(C_BC)] = (acc3 * cprev[:, col(D_MODEL)]).astype(BF16)
                dproj_ref[:, col(C_BX)] = (acc3 * cprev[:, col(0)]).astype(BF16)
                return carry

            lax.fori_loop(0, N_CHUNK, convt_chunk, 0)

        @pl.when(i == n_tiles)
        def _finish():
            for k in range(CONV_A):
                small_ref[ROW_CONV_A_W + k:ROW_CONV_A_W + k + 1, :] = _colsum(dcaw8[SUBLANES * k:SUBLANES * (k + 1), :])
            for k in range(CONV_B):
                small_ref[ROW_CONV_B_W + k:ROW_CONV_B_W + k + 1, :] = _colsum(dcbw8[SUBLANES * k:SUBLANES * (k + 1), :])

    pair = 2 * D_MODEL
    return pl.pallas_call(
        body, name="fused_pass", grid=(n_tiles + 1,),
        out_shape=[
            jax.ShapeDtypeStruct((tp, D_IN), BF16),
            jax.ShapeDtypeStruct((tp, D_MODEL), F32),
            jax.ShapeDtypeStruct((3, tp, D_MODEL), BF16),
            jax.ShapeDtypeStruct((3, tp, D_MODEL), BF16),
            jax.ShapeDtypeStruct((SMALL_A_ROWS, D_MODEL), F32),
        ],
        in_specs=[
            pl.BlockSpec((T, D_IN), lambda i: (cur(i), 0)),
            pl.BlockSpec((T, pair), lambda i: (prev(i), C_AVAL // pair)),
            pl.BlockSpec((T, pair), lambda i: (prev(i), C_BC // pair)),
            pl.BlockSpec((T, D_MODEL), lambda i: (xblk(i), 0)),
            pl.BlockSpec((T, D_MODEL), lambda i: (xblk(i), 0)),
            _VMEM, _VMEM, _VMEM, _VMEM, _VMEM, _VMEM, _VMEM, _VMEM,
            *[_resident((D_MODEL, D_MODEL)) for _ in range(6)],
        ],
        out_specs=[
            pl.BlockSpec((T, D_IN), lambda i: (prev(i), 0)),
            pl.BlockSpec((T, D_MODEL), lambda i: (cur(i), 0)),
            pl.BlockSpec((3, T, D_MODEL), lambda i: (0, cur(i), 0)),
            pl.BlockSpec((3, T, D_MODEL), lambda i: (0, cur(i), 0)),
            _VMEM,
        ],
        scratch_shapes=[
            pltpu.VMEM((2, HALO + T, D_MODEL), F32),
            pltpu.VMEM((2, SUBLANES + T, D_MODEL), F32),
            pltpu.VMEM((2, T + HALO, D_MODEL), F32),
            pltpu.VMEM((2, T + SUBLANES, D_MODEL), F32),
            pltpu.VMEM((T, 5 * D_MODEL), BF16),
            pltpu.VMEM((T, D_MODEL), F32),
            pltpu.VMEM((T, D_MODEL), F32),
            pltpu.VMEM((T, D_MODEL), F32),
            pltpu.VMEM((T, D_MODEL), F32),
            pltpu.VMEM((32 * SUBLANES, D_MODEL), F32),
            pltpu.VMEM((SUBLANES * SUBLANES, D_MODEL), F32),
            pltpu.VMEM((SUBLANES, T + HALO, LANES), F32),
        ],
        compiler_params=pltpu.CompilerParams(dimension_semantics=("arbitrary",), vmem_limit_bytes=VMEM_LIMIT),
    )(proj, proj, proj, x2d, tgt2d, meta_tile, conv_a_w, conv_a_b, ln_a_g, ln_a_b, b_a_out, conv_b_w, final_g,
      w_a, w_b, w_o, w_a_t, w_b_t, w_o_t)


def _input_bwd(dproj, ds1, x2d, meta_tile, norm_g, w_in_all, row_tile):
    seq = x2d.shape[0]
    n_steps = seq // row_tile
    meta_block = seq // TILE

    def backward(dp_ref, ds1_ref, s0_ref, g_ref, w_ref, out_ref, vec_ref):
        dh = _dot_nt(dp_ref[:, 0:COLS], w_ref[0])
        for j in range(1, N_DEV):
            dh = dh + _dot_nt(dp_ref[:, j * COLS:(j + 1) * COLS], w_ref[j])
        s0v = s0_ref[...]
        r = lax.rsqrt(_rowmean(s0v * s0v) + EPS)
        gh = dh * g_ref[...]
        out_ref[...] = ds1_ref[...] + r * gh - s0v * ((r * r * r) * _rowmean(gh * s0v))
        vec_ref[ROW_NORM_G:ROW_NORM_G + 1, :] += _colsum(dh * (s0v * r))

    def body(dp_ref, ds1_ref, x_ref, dpm_ref, ds1m_ref, meta_ref, g_ref, w_ref, gx_ref, small_ref, gmeta_buf):
        t = pl.program_id(0)

        @pl.when(t == 0)
        def _():
            small_ref[...] = jnp.zeros(small_ref.shape, F32)

        backward(dp_ref, ds1_ref, x_ref, g_ref, w_ref, gx_ref, small_ref)

        @pl.when(t == n_steps - 1)
        def _():
            backward(dpm_ref, ds1m_ref, meta_ref, g_ref, w_ref, gmeta_buf, small_ref)
            small_ref[ROW_META:ROW_META + N_META, :] = gmeta_buf[TILE - N_META:TILE, :]

    return pl.pallas_call(
        body, name="input_bwd", grid=(n_steps,),
        out_shape=[jax.ShapeDtypeStruct(x2d.shape, F32), jax.ShapeDtypeStruct((SMALL_B_ROWS, D_MODEL), F32)],
        in_specs=[pl.BlockSpec((row_tile, D_IN), lambda t: (t, 0)),
                  pl.BlockSpec((row_tile, D_MODEL), lambda t: (t, 0)),
                  pl.BlockSpec((row_tile, D_MODEL), lambda t: (t, 0)),
                  pl.BlockSpec((TILE, D_IN), lambda t: (meta_block, 0)),
                  pl.BlockSpec((TILE, D_MODEL), lambda t: (meta_block, 0)),
                  _VMEM, _VMEM, _resident((N_DEV, D_MODEL, COLS))],
        out_specs=[pl.BlockSpec((row_tile, D_MODEL), lambda t: (t, 0)), _VMEM],
        scratch_shapes=[pltpu.VMEM((TILE, D_MODEL), F32)],
        compiler_params=pltpu.CompilerParams(dimension_semantics=("arbitrary",), vmem_limit_bytes=VMEM_LIMIT),
    )(dproj, ds1, x2d, dproj, ds1, meta_tile, norm_g, w_in_all)


def _grad_w_in_half(pos, h_t, dproj, k_tile, other_side, rides, name, after=None, add_to=None, narrow=False):
    tp = h_t.shape[1]
    n_k = tp // k_tile
    order = [] if after is None else [after]
    summing = add_to is not None
    assert not (summing and narrow)

    def column_block(q, k, pos_ref):
        return k, 2 * q + (1 - pos_ref[2] if other_side else pos_ref[2])

    def body(pos_ref, h_ref, dp_ref, *refs):
        acc = refs[-1]

        @pl.when(pl.program_id(1) == 0)
        def _():
            acc[...] = refs[0][...].astype(F32) if summing else jnp.zeros(acc.shape, F32)

        acc[...] += _dot(h_ref[...], dp_ref[...])

        if summing or narrow:
            @pl.when(pl.program_id(1) == n_k - 1)
            def _():
                refs[-2][...] = acc[...].astype(BF16)

        if summing:
            @pl.when((pl.program_id(1) == n_k - 1) & (pl.program_id(0) == 2 * pos_ref[0] + pos_ref[1]))
            def _():
                refs[-3][...] = acc[...]

    ride = [a for _, arrays in rides for a in arrays]
    n_arr = len(ride)
    ride_shapes, ride_sems = _ride_shapes(rides)
    block = (None, D_MODEL, COLS)
    extra_in = [add_to] if summing else []
    extra_in_specs = [pl.BlockSpec(block, lambda q, k, pos_ref: (q, 0, 0))] if summing else []
    extra_out = [jax.ShapeDtypeStruct((4, D_MODEL, COLS), BF16)] if summing else []
    extra_out_specs = [pl.BlockSpec(block, lambda q, k, pos_ref: (q ^ (2 * pos_ref[0] + pos_ref[1]), 0, 0))] \
        if summing else []
    body = _riding(body, 3 + len(extra_in) + len(order), 1 + len(extra_out), rides,
                   lambda: (pl.program_id(0) == 0) & (pl.program_id(1) == 0),
                   lambda: (pl.program_id(0) == 3) & (pl.program_id(1) == n_k - 1))
    return pl.pallas_call(
        body, name=name,
        out_shape=[jax.ShapeDtypeStruct((1 if summing else 4, D_MODEL, COLS), BF16 if narrow else F32)]
        + extra_out + ride_shapes,
        grid_spec=pltpu.PrefetchScalarGridSpec(
            num_scalar_prefetch=1, grid=(4, n_k),
            in_specs=[pl.BlockSpec((D_MODEL, k_tile), lambda q, k, pos_ref: (0, k)),
                      pl.BlockSpec((k_tile, COLS), column_block)] + extra_in_specs + [_ANY] * (len(order) + n_arr),
            out_specs=[pl.BlockSpec(block, lambda q, k, pos_ref: (0 if summing else q, 0, 0))]
            + extra_out_specs + [_ANY] * n_arr,
            scratch_shapes=([pltpu.VMEM((D_MODEL, COLS), F32)] if summing or narrow else []) + ride_sems),
        compiler_params=pltpu.CompilerParams(dimension_semantics=("arbitrary", "arbitrary"),
                                             vmem_limit_bytes=VMEM_LIMIT),
    )(pos, h_t, dproj, *extra_in, *order, *ride)


def _grad_w_out(lhs, rhs, k_tile, after):
    tp = lhs.shape[1]

    def body(a_ref, b_ref, after_ref, o_ref):
        @pl.when(pl.program_id(1) == 0)
        def _():
            o_ref[...] = jnp.zeros(o_ref.shape, F32)

        o_ref[...] += _dot_tn(a_ref[...], b_ref[...]).reshape(N_DEV, ROWS_OUT, D_MODEL)

    return pl.pallas_call(
        body, name="grad_w_out", grid=(3, tp // k_tile),
        out_shape=jax.ShapeDtypeStruct((N_DEV, 3, ROWS_OUT, D_MODEL), F32),
        in_specs=[pl.BlockSpec((None, k_tile, D_MODEL), lambda w, k: (w, k, 0)),
                  pl.BlockSpec((None, k_tile, D_MODEL), lambda w, k: (w, k, 0)), _ANY],
        out_specs=pl.BlockSpec((N_DEV, None, ROWS_OUT, D_MODEL), lambda w, k: (0, w, 0, 0)),
        compiler_params=pltpu.CompilerParams(dimension_semantics=("arbitrary", "arbitrary"),
                                             vmem_limit_bytes=VMEM_LIMIT),
    )(lhs, rhs, after)


def _adamw_math(w, g, m, v):
    m = ADAM_B1 * m + (1.0 - ADAM_B1) * g
    v = ADAM_B2 * v + (1.0 - ADAM_B2) * (g * g)
    m_hat = m / (1.0 - ADAM_B1 ** ADAM_STEP)
    v_hat = v / (1.0 - ADAM_B2 ** ADAM_STEP)
    delta = -ADAM_LR * (m_hat / (jnp.sqrt(v_hat) + ADAM_EPS) + ADAM_WD * w)
    return delta, m, v


def _adamw_sharded(pos, mine, theirs, landed, weights, row_tile, name, after=None):
    order = [] if after is None else [after]
    rows, n = weights[0][0].shape
    n_slots = mine.shape[0]
    per_shard = rows // row_tile
    assert per_shard == 1 or len(weights) == 1

    def mine_map(j, t, pos_ref):
        chip = 2 * pos_ref[0] + pos_ref[1]
        return {N_DEV: 2 * chip + pos_ref[2], 4: chip, 1: 0}[n_slots], j * per_shard + t, 0

    def theirs_map(j, t, pos_ref):
        return 2 * pos_ref[0] + pos_ref[1], j * per_shard + t, 0

    def body(pos_ref, mine_ref, *refs):
        if theirs is not None:
            g = mine_ref[...] + refs[0][...]
            refs = refs[1:]
        else:
            g = mine_ref[...]
        land_ref, refs = refs[0], refs[1:]
        ins, outs = refs[:3 * len(weights)], refs[3 * len(weights) + len(order):]
        for k in range(3):
            g = g + land_ref[k].astype(F32)
        for j in range(len(weights)):
            @pl.when(pl.program_id(0) == j)
            def _(j=j):
                w_ref, m_ref, v_ref = ins[3 * j:3 * j + 3]
                delta, m_new, v_new = _adamw_math(w_ref[...], g, m_ref[...], v_ref[...])
                for ref, val in zip(outs[4 * j:4 * j + 4], (g, delta, m_new, v_new)):
                    ref[...] = val

    tile = pl.BlockSpec((row_tile, n), lambda j, t, pos_ref: (t, 0))
    res = pl.pallas_call(
        body, name=name,
        out_shape=[jax.ShapeDtypeStruct((rows, n), F32)] * (4 * len(weights)),
        grid_spec=pltpu.PrefetchScalarGridSpec(
            num_scalar_prefetch=1, grid=(len(weights), per_shard),
            in_specs=[pl.BlockSpec((None, row_tile, n), mine_map)]
            + ([pl.BlockSpec((None, row_tile, n), theirs_map)] if theirs is not None else [])
            + [pl.BlockSpec((3, row_tile, n), lambda j, t, pos_ref: (0, j * per_shard + t, 0))]
            + [tile] * (3 * len(weights)) + [_ANY] * len(order),
            out_specs=[tile] * (4 * len(weights))),
        compiler_params=pltpu.CompilerParams(dimension_semantics=("arbitrary", "arbitrary")),
    )(pos, mine, *([theirs] if theirs is not None else []), landed, *[a for wmv in weights for a in wmv], *order)
    return [res[4 * j:4 * j + 4] for j in range(len(weights))]


def _adamw_small(gathered, gathered_cols, params):
    n_par, n_src = len(params), len(gathered)

    def body(*refs):
        g_refs, gc_refs = refs[:n_src], refs[n_src:2 * n_src]
        ins = refs[2 * n_src:2 * n_src + 3 * n_par]
        outs = refs[2 * n_src + 3 * n_par:]
        loss_ref = outs[4 * n_par]

        def reduced(ref, row, n_rows):
            g = ref[0, row:row + n_rows, :]
            for d in range(1, N_DEV):
                g = g + ref[d, row:row + n_rows, :]
            return g

        for p, (src, row, n_rows, sharded, _, _, _) in enumerate(params):
            g = reduced((gc_refs if sharded else g_refs)[src], row, n_rows)
            w_ref, m_ref, v_ref = ins[3 * p:3 * p + 3]
            delta, m_new, v_new = _adamw_math(w_ref[...], g, m_ref[...], v_ref[...])
            outs[4 * p][...] = g
            outs[4 * p + 1][...] = delta
            outs[4 * p + 2][...] = m_new
            outs[4 * p + 3][...] = v_new
        loss = jnp.sum(reduced(g_refs[0], ROW_LOSS, 1), axis=1, keepdims=True)
        loss_ref[...] = jnp.broadcast_to(loss, loss_ref.shape)

    out_shape = []
    for (_, _, _, _, w, _, _) in params:
        out_shape += [jax.ShapeDtypeStruct(w.shape, F32)] * 4
    out_shape.append(jax.ShapeDtypeStruct((1, LANES), F32))
    flat = [a for (_, _, _, _, w, m, v) in params for a in (w, m, v)]
    return pl.pallas_call(
        body, name="adamw_small", out_shape=out_shape,
        in_specs=[_VMEM] * (2 * n_src + len(flat)), out_specs=[_VMEM] * len(out_shape),
    )(*gathered, *gathered_cols, *flat)


def _pad_rows(a, rows):
    return jnp.concatenate([a, jnp.zeros((rows - a.shape[0], a.shape[1]), a.dtype)], axis=0)


def kernel(x, meta_tokens, norm_g, w_in, conv_a_w, conv_a_b, ln_a_g, ln_a_b, w_a_out, b_a_out, conv_b_w, w_b_out, w_out, final_g, loss_target, m_meta_tokens, m_norm_g, m_w_in, m_conv_a_w, m_conv_a_b, m_ln_a_g, m_ln_a_b, m_w_a_out, m_b_a_out, m_conv_b_w, m_w_b_out, m_w_out, m_final_g, v_meta_tokens, v_norm_g, v_w_in, v_conv_a_w, v_conv_a_b, v_ln_a_g, v_ln_a_b, v_w_a_out, v_b_a_out, v_conv_b_w, v_w_b_out, v_w_out, v_final_g):
    seq = x.shape[1]
    assert x.shape == (1, seq, D_MODEL) and seq % TILE == 0 and w_in.shape == (1, D_MODEL, COLS)
    n_tiles = seq // TILE + 1
    tp = n_tiles * TILE
    pos = jnp.stack([lax.axis_index("x"), lax.axis_index("y"), lax.axis_index("c")]).astype(jnp.int32)
    me = 4 * pos[0] + 2 * pos[1] + pos[2]
    x2d = x[0]
    tgt2d = loss_target[0]

    small = jnp.concatenate([meta_tokens, _pad_rows(conv_a_w[0], 32), _pad_rows(conv_b_w[0], SUBLANES)], axis=0)
    final_g2 = final_g.reshape(1, D_MODEL)

    w_out_shards = [w[0].astype(BF16) for w in (w_a_out, w_b_out, w_out)]
    h_t, proj, meta_tile, small_params, w_in_all, *w_out_all = _gather_norm_proj(
        pos, x2d, small[None], norm_g, w_in[0].astype(BF16), w_out_shards, 3)
    small_params = small_params.transpose(1, 0, 2).reshape(small.shape[0], D_MODEL)
    conv_a_full, conv_b_full = small_params[N_META:N_META + 32], small_params[N_META + 32:]
    w_out_all = [w.reshape(D_MODEL, D_MODEL) for w in w_out_all]
    w_out_all_t = [w.T for w in w_out_all]
    dproj, ds1, lhs, rhs, small_a = _fused_pass(
        proj, x2d, tgt2d, meta_tile, conv_a_full, conv_a_b, ln_a_g, ln_a_b, b_a_out, conv_b_full, final_g2,
        w_out_all[0], w_out_all[1], w_out_all[2], w_out_all_t[0], w_out_all_t[1], w_out_all_t[2], n_tiles)
    k_tile = tp // 3
    gw_far, small_a_all = _grad_w_in_half(pos, h_t, dproj, k_tile, True, [("all", (small_a[None],))], "grad_w_in_far",
                                          narrow=True)
    sems, sent, landing, token = _start_exchanges([("sibling_half", (gw_far,))], "rs_far_start")
    gw_out = _grad_w_out(lhs, rhs, k_tile, token).reshape(N_DEV, 3 * ROWS_OUT, D_MODEL)
    (their_in,) = _wait_exchanges([("sibling_half", 1)], sems, sent, landing, gw_out, "rs_far_wait")
    sems_o, sent_o, landing_o, token = _start_exchanges([("sibling", (gw_out,))], "rs_out_start")
    gw_near, parts_in = _grad_w_in_half(pos, h_t, dproj, k_tile, False, [], "grad_w_in_near", after=token,
                                        add_to=their_in)
    sems_i, sent_i, landing_i, token = _start_exchanges([("chips_by_relation", (parts_in,))], "rs_chips_in_start")
    gw_out, their_out = _wait_exchanges([("sibling", 1)], sems_o, sent_o, landing_o, token, "rs_out_wait",
                                        keep_sources=True)
    parts_out = _chip_partial(pos, gw_out, their_out, (1, 2, 3), BF16, ROWS_OUT, "rs_parts_w_out")
    sems_o, sent_o, landing_o, token = _start_exchanges([("chips", (parts_out,))], "rs_chips_out_start")
    grad_x, small_b = _input_bwd(dproj, ds1, x2d, meta_tile, norm_g + token[0, 0], w_in_all, min(512, seq))

    sems_s, sent_s, landing_s, token = _start_exchanges([("all", (small_b[None],))], "gather_small_grads_start")
    (land_in,) = _wait_exchanges([("chips_by_relation", 1)], sems_i, sent_i, landing_i, token, "rs_chips_in_wait")
    (res_in,) = _adamw_sharded(pos, gw_near, None, land_in, [(w_in[0], m_w_in[0], v_w_in[0])], 128, "adamw_w_in")
    (land_out,) = _wait_exchanges([("chips", 1)], sems_o, sent_o, landing_o, res_in[0], "rs_chips_out_wait")
    res_out = _adamw_sharded(
        pos, gw_out, their_out, land_out,
        [(w_a_out[0], m_w_a_out[0], v_w_a_out[0]), (w_b_out[0], m_w_b_out[0], v_w_b_out[0]),
         (w_out[0], m_w_out[0], v_w_out[0])], ROWS_OUT, "adamw_w_out")
    (small_b_all,) = _wait_exchanges([("all", 1)], sems_s, sent_s, landing_s, res_out[2][0], "gather_small_grads_wait")
    small_grads = [small_a_all, small_b_all]
    small_cols = [lax.dynamic_slice_in_dim(g, me * LANES, LANES, axis=2) for g in small_grads]
    params = [
        (1, ROW_META, N_META, True, meta_tokens, m_meta_tokens, v_meta_tokens),
        (1, ROW_NORM_G, 1, False, norm_g, m_norm_g, v_norm_g),
        (0, ROW_CONV_A_W, CONV_A, True, conv_a_w[0], m_conv_a_w[0], v_conv_a_w[0]),
        (0, ROW_CONV_A_B, 1, False, conv_a_b, m_conv_a_b, v_conv_a_b),
        (0, ROW_LN_G, 1, False, ln_a_g, m_ln_a_g, v_ln_a_g),
        (0, ROW_LN_B, 1, False, ln_a_b, m_ln_a_b, v_ln_a_b),
        (0, ROW_B_A_OUT, 1, False, b_a_out, m_b_a_out, v_b_a_out),
        (0, ROW_CONV_B_W, CONV_B, True, conv_b_w[0], m_conv_b_w[0], v_conv_b_w[0]),
        (0, ROW_FINAL_G, 1, False, final_g2, m_final_g.reshape(1, D_MODEL), v_final_g.reshape(1, D_MODEL)),
    ]
    res_small = _adamw_small(small_grads, small_cols, params)
    loss = res_small[-1][0, 0]

    def small_res(p, kind, shape):
        return res_small[4 * p + kind].reshape(shape)

    per_weight = []
    for kind in range(4):
        per_weight.append([
            small_res(0, kind, meta_tokens.shape),
            small_res(1, kind, norm_g.shape),
            res_in[kind].reshape(w_in.shape),
            small_res(2, kind, conv_a_w.shape),
            small_res(3, kind, conv_a_b.shape),
            small_res(4, kind, ln_a_g.shape),
            small_res(5, kind, ln_a_b.shape),
            res_out[0][kind].reshape(w_a_out.shape),
            small_res(6, kind, b_a_out.shape),
            small_res(7, kind, conv_b_w.shape),
            res_out[1][kind].reshape(w_b_out.shape),
            res_out[2][kind].reshape(w_out.shape),
            small_res(8, kind, final_g.shape),
        ])
    return (loss, grad_x.reshape(x.shape), *per_weight[0], *per_weight[1], *per_weight[2], *per_weight[3])
```

```python
import functools

import jax
import jax.numpy as jnp
from jax import lax
from jax.experimental import pallas as pl
from jax.experimental.pallas import tpu as pltpu

D_MODEL = 1024
N_META = 16
N_DEV = 8
D_IN = 9 * D_MODEL
COLS = D_IN // N_DEV
ROWS_OUT = D_MODEL // N_DEV
CONV_A = 31
CONV_B = 3
EPS = 1e-6

ADAM_LR = 0.001
ADAM_B1 = 0.9
ADAM_B2 = 0.999
ADAM_EPS = 1e-08
ADAM_WD = 0.01
ADAM_STEP = 10

TILE = 128
LANES = 128
N_CHUNK = D_MODEL // LANES
HALO = 32
SUBLANES = 8
VMEM_LIMIT = 56 * 1024 * 1024

ROW_FINAL_G, ROW_B_A_OUT, ROW_LN_G, ROW_LN_B, ROW_CONV_A_B, ROW_LOSS = 0, 1, 2, 3, 4, 5
ROW_CONV_A_W, ROW_CONV_B_W, SMALL_A_ROWS = 8, 40, 48
ROW_NORM_G, ROW_META, SMALL_B_ROWS = 0, 8, 24

MESH = pl.DeviceIdType.MESH
_ANY = pl.BlockSpec(memory_space=pl.ANY)
_VMEM = pl.BlockSpec(memory_space=pltpu.VMEM)
BF16 = jnp.bfloat16
F32 = jnp.float32


def _resident(shape):
    return pl.BlockSpec(shape, lambda *_: (0,) * len(shape), pipeline_mode=pl.Buffered(1))


def _sigmoid(v):
    return jax.nn.sigmoid(v)


def _dot(a, b):
    return jnp.dot(a, b, preferred_element_type=F32)


def _dot_nt(a, b):
    return lax.dot_general(a, b, (((1,), (1,)), ((), ())), preferred_element_type=F32)


def _dot_tn(a, b):
    return lax.dot_general(a, b, (((0,), (0,)), ((), ())), preferred_element_type=F32)


def _colsum(v):
    return jnp.sum(v, axis=0, keepdims=True)


def _rowmean(v):
    parts = [v[:, LANES * c:LANES * (c + 1)] for c in range(v.shape[1] // LANES)]
    return jnp.sum(functools.reduce(jnp.add, parts), axis=-1, keepdims=True) * (1.0 / v.shape[1])


def _fold8(v):
    parts = [v[SUBLANES * g:SUBLANES * (g + 1)] for g in range(v.shape[0] // SUBLANES)]
    return functools.reduce(jnp.add, parts)


def _sibling_copies(srcs, dsts, send_sems, recv_sems):
    x, y, c = lax.axis_index("x"), lax.axis_index("y"), lax.axis_index("c")
    return [pltpu.make_async_remote_copy(
        src_ref=src.at[2 * q + (1 - c)], dst_ref=dst.at[q],
        send_sem=send_sems.at[4 * a + q], recv_sem=recv_sems.at[4 * a + q],
        device_id=(x, y, 1 - c), device_id_type=MESH)
        for a, (src, dst) in enumerate(zip(srcs, dsts)) for q in range(4)]


def _chip_copies(srcs, dsts, send_sems, recv_sems):
    x, y, c = lax.axis_index("x"), lax.axis_index("y"), lax.axis_index("c")
    targets = [(x, 1 - y, c), (1 - x, y, c), (1 - x, 1 - y, c)]
    return [pltpu.make_async_remote_copy(
        src_ref=src.at[k], dst_ref=dst.at[k],
        send_sem=send_sems.at[3 * a + k], recv_sem=recv_sems.at[3 * a + k],
        device_id=targets[k], device_id_type=MESH)
        for a, (src, dst) in enumerate(zip(srcs, dsts)) for k in range(3)]


def _sibling_half_copies(srcs, dsts, send_sems, recv_sems):
    x, y, c = lax.axis_index("x"), lax.axis_index("y"), lax.axis_index("c")
    return [pltpu.make_async_remote_copy(
        src_ref=src.at[q], dst_ref=dst.at[q],
        send_sem=send_sems.at[4 * a + q], recv_sem=recv_sems.at[4 * a + q],
        device_id=(x, y, 1 - c), device_id_type=MESH)
        for a, (src, dst) in enumerate(zip(srcs, dsts)) for q in range(4)]


def _all_copies(srcs, dsts, send_sems, recv_sems):
    x, y, c = lax.axis_index("x"), lax.axis_index("y"), lax.axis_index("c")
    mine = 4 * x + 2 * y + c
    copies = []
    for a, (src, dst) in enumerate(zip(srcs, dsts)):
        copies.append(pltpu.make_async_copy(src.at[0], dst.at[mine], send_sems.at[N_DEV * a]))
        for k in range(1, N_DEV):
            copies.append(pltpu.make_async_remote_copy(
                src_ref=src.at[0], dst_ref=dst.at[mine],
                send_sem=send_sems.at[N_DEV * a + k], recv_sem=recv_sems.at[N_DEV * a + k],
                device_id=(x ^ (k >> 2), y ^ ((k >> 1) & 1), c ^ (k & 1)), device_id_type=MESH))
    return copies


def _chip_copies_by_relation(srcs, dsts, send_sems, recv_sems):
    return _chip_copies([src.at[pl.ds(1, 3)] for src in srcs], dsts, send_sems, recv_sems)


_EXCHANGES = {"sibling": (4, _sibling_copies, 4), "sibling_half": (4, _sibling_half_copies, 4),
              "chips": (3, _chip_copies, 3), "chips_by_relation": (3, _chip_copies_by_relation, 3),
              "all": (N_DEV, _all_copies, N_DEV)}


def _exchange_shapes(kind, arrays):
    per_array, _, slots = _EXCHANGES[kind]
    out_shape = [jax.ShapeDtypeStruct((slots,) + a.shape[1:], a.dtype) for a in arrays]
    sems = [pltpu.SemaphoreType.DMA((per_array * len(arrays),))] * 2
    return out_shape, sems


def _ride_shapes(rides):
    shapes, sems = [], []
    for kind, arrays in rides:
        ride_shapes, ride_sems = _exchange_shapes(kind, arrays)
        shapes += ride_shapes
        sems += ride_sems
    return shapes, sems


def _riding(body, n_in, n_out, rides, is_first, is_last):
    counts = [len(arrays) for _, arrays in rides]
    n_arr = sum(counts)

    def wrapped(*refs):
        ins, srcs = refs[:n_in], refs[n_in:n_in + n_arr]
        outs = refs[n_in + n_arr:n_in + n_arr + n_out]
        dsts = refs[n_in + n_arr + n_out:n_in + 2 * n_arr + n_out]
        first_sem = len(refs) - 2 * len(rides)
        scratch, sems = refs[n_in + 2 * n_arr + n_out:first_sem], refs[first_sem:]

        def copies():
            made, at = [], 0
            for r, ((kind, _), n) in enumerate(zip(rides, counts)):
                made += _EXCHANGES[kind][1](srcs[at:at + n], dsts[at:at + n], sems[2 * r], sems[2 * r + 1])
                at += n
            return made

        @pl.when(is_first())
        def _():
            for cp in copies():
                cp.start()

        body(*ins, *outs, *scratch)

        @pl.when(is_last())
        def _():
            for cp in copies():
                cp.wait()

    return wrapped


_HBM = pl.BlockSpec(memory_space=pltpu.HBM)
_SEM = pl.BlockSpec(memory_space=pltpu.SEMAPHORE)
_FLOWS = pltpu.SideEffectType.DATAFLOW_SIDE_EFFECTING


def _start_exchanges(rides, name):
    arrays = [a for _, group in rides for a in group]
    shapes, sems = _ride_shapes(rides)
    n_arr, n_sem = len(arrays), len(sems)

    def body(*refs):
        srcs, lands = refs[:n_arr], refs[n_arr:2 * n_arr]
        sem_refs, token = refs[2 * n_arr:2 * n_arr + n_sem], refs[-1]
        at = 0
        for r, (kind, group) in enumerate(rides):
            n = len(group)
            for cp in _EXCHANGES[kind][1](srcs[at:at + n], lands[at:at + n], sem_refs[2 * r], sem_refs[2 * r + 1]):
                cp.start()
            at += n
        token[...] = jnp.zeros(token.shape, token.dtype)

    in_hbm = [pltpu.HBM(a.shape, a.dtype) for a in arrays]
    land_hbm = [pltpu.HBM(sh.shape, sh.dtype) for sh in shapes]
    res = pl.pallas_call(
        body, name=name,
        out_shape=(*sems, *in_hbm, *land_hbm, jax.ShapeDtypeStruct((SUBLANES, LANES), F32)),
        in_specs=[_HBM] * (2 * n_arr), out_specs=(*[_SEM] * n_sem, *[_HBM] * (2 * n_arr), _VMEM),
        input_output_aliases={i: n_sem + i for i in range(2 * n_arr)},
        compiler_params=pltpu.CompilerParams(has_side_effects=_FLOWS),
    )(*[pltpu.with_memory_space_constraint(a, pltpu.HBM) for a in arrays],
      *[pltpu.with_memory_space_constraint(lax.empty(sh.shape, sh.dtype), pltpu.HBM) for sh in shapes])
    return res[:n_sem], res[n_sem:n_sem + n_arr], res[n_sem + n_arr:n_sem + 2 * n_arr], res[-1]


def _wait_exchanges(kinds, sems, arrays, lands, after, name, keep_sources=False):
    n_arr, n_sem = len(arrays), len(sems)

    def body(*refs):
        srcs, dsts = refs[:n_arr], refs[n_arr:2 * n_arr]
        sem_refs = refs[2 * n_arr:2 * n_arr + n_sem]
        at = 0
        for r, (kind, n) in enumerate(kinds):
            for cp in _EXCHANGES[kind][1](srcs[at:at + n], dsts[at:at + n], sem_refs[2 * r], sem_refs[2 * r + 1]):
                cp.wait()
            at += n

    hbm = [pltpu.HBM(a.shape, a.dtype) for a in (*arrays, *lands)]
    return pl.pallas_call(
        body, name=name, out_shape=tuple(hbm),
        in_specs=[_HBM] * (2 * n_arr) + [_SEM] * n_sem + [_ANY], out_specs=tuple([_HBM] * (2 * n_arr)),
        input_output_aliases={i: i for i in range(2 * n_arr)},
        compiler_params=pltpu.CompilerParams(has_side_effects=_FLOWS),
    )(*arrays, *lands, *sems, after)[0 if keep_sources else n_arr:]


def _chip_partial(pos, mine, theirs, relations, out_dtype, row_tile, name):
    n_slots, m, n = mine.shape
    q0 = relations[0]

    def chip_of(qi, pos_ref):
        q = qi + q0
        return pos_ref[0] ^ (q >> 1), pos_ref[1] ^ (q & 1)

    def mine_map(qi, t, pos_ref):
        px, py = chip_of(qi, pos_ref)
        return (4 * px + 2 * py + pos_ref[2] if n_slots == N_DEV else 2 * px + py), t, 0

    def theirs_map(qi, t, pos_ref):
        px, py = chip_of(qi, pos_ref)
        return 2 * px + py, t, 0

    def body(pos_ref, a_ref, b_ref, o_ref):
        o_ref[...] = (a_ref[...] + b_ref[...]).astype(out_dtype)

    return pl.pallas_call(
        body, name=name,
        out_shape=jax.ShapeDtypeStruct((len(relations), m, n), out_dtype),
        grid_spec=pltpu.PrefetchScalarGridSpec(
            num_scalar_prefetch=1, grid=(len(relations), m // row_tile),
            in_specs=[pl.BlockSpec((None, row_tile, n), mine_map), pl.BlockSpec((None, row_tile, n), theirs_map)],
            out_specs=pl.BlockSpec((None, row_tile, n), lambda qi, t, pos_ref: (qi, t, 0))),
        compiler_params=pltpu.CompilerParams(dimension_semantics=("arbitrary", "arbitrary")),
    )(pos, mine, theirs)


PARTS = ((0, 512), (512, 640))


def _gather_norm_proj(pos, x2d, small_shard, norm_g, w_in_shard, w_out_shards, n_chunk):
    seq = x2d.shape[0]
    n_tiles = seq // TILE + 1
    tp = n_tiles * TILE
    n_parts = len(PARTS)
    widest = max(width for _, width in PARTS)
    units = [(s, u) for s in range(2) for u in range(n_parts)]
    units += [(s, u) for u in range(n_parts) for s in (2, 3, 5, 6)] + [(s, u) for u in range(n_parts) for s in (4, 7)]
    n_units = len(units)
    over_ici = [m for m, (s, _) in enumerate(units) if s in (2, 3)]
    handled_at = {m: m - 1 for m in range(1, n_units)}
    handled_at.update({m: over_ici[0] - 1 + i for i, m in enumerate(over_ici)})
    assert all(step < m for m, step in handled_at.items())
    n_steps = n_tiles + n_units
    chunk = tp // n_chunk

    def body(pos_ref, x_ref, g_ref, small_ref, win_ref, wa_ref, wb_ref, wo_ref,
             ht_ref, proj_ref, meta_ref, small_all, win_all, wa_all, wb_all, wo_all,
             h_all, wbuf, rbuf, small_buf, send_sems, recv_sems, local_sems, small_send, small_recv):
        g = pl.program_id(0)
        x, y, c = lax.axis_index("x"), lax.axis_index("y"), lax.axis_index("c")
        me, sibling = (x, y, c), (x, y, 1 - c)
        chips = [(1 - x, y), (x, 1 - y), (1 - x, 1 - y)]
        shards = (win_ref, wa_ref, wb_ref, wo_ref)
        gathered = (win_all, wa_all, wb_all, wo_all)
        n_arrays = len(shards)
        blocks = [me, sibling] + [(*chip, c) for chip in chips] + [(*chip, 1 - c) for chip in chips]

        def index(block):
            px, py, pc = block
            return 4 * px + 2 * py + pc

        def part(ref, a, u):
            return ref.at[:, pl.ds(PARTS[u][0], PARTS[u][1])] if a == 0 else ref

        def slot(a, block, u):
            return part(gathered[a].at[index(block)], a, u)

        def sem(a, k, u):
            return n_parts * k + u if a == 0 else 7 * n_parts + 7 * (a - 1) + k

        def copy(a, k, block, to, u=0, from_shard=False):
            return pltpu.make_async_remote_copy(
                src_ref=part(shards[a], a, u) if from_shard else slot(a, block, u), dst_ref=slot(a, block, u),
                send_sem=send_sems.at[sem(a, k, u)], recv_sem=recv_sems.at[sem(a, k, u)],
                device_id=to, device_id_type=MESH)

        def keep(a):
            return pltpu.make_async_copy(shards[a], gathered[a].at[index(me)], local_sems.at[a])

        def load(m):
            s, u = units[m]
            src = part(win_ref, 0, u) if s == 0 else slot(0, blocks[s], u)
            return pltpu.make_async_copy(src, wbuf.at[m % 2, :, 0:PARTS[u][1]], local_sems.at[n_arrays + m % 2])

        def store(m):
            s, u = units[m]
            col0 = pl.multiple_of(index(blocks[s]) * COLS + PARTS[u][0], LANES)
            return pltpu.make_async_copy(rbuf.at[m % 2, :, 0:PARTS[u][1]],
                                         proj_ref.at[:, pl.ds(col0, PARTS[u][1])], local_sems.at[n_arrays + 2 + m % 2])

        def by_x(a, u):
            return u == 0 if a == 0 else a < 3

        def relay(a, u=0):
            src, to = (blocks[3], blocks[2]) if by_x(a, u) else (blocks[2], blocks[3])
            return copy(a, 3, src, to, u)

        def arrive(m):
            s, u = units[m]
            if s == 1:
                copy(0, 0, sibling, me, u).wait_recv()
            elif 2 <= s <= 4:
                copy(0, s - 1, blocks[s], me, u).wait_recv()
                copy(0, s + 2, blocks[s], sibling, u).start()
                if s < 4 and by_x(0, u) == (s == 3):
                    relay(0, u).start()
            elif s >= 5:
                copy(0, s - 1, blocks[s], me, u).wait_recv()

        def pass_on_out(j):
            for a in range(1, 4):
                copy(a, j + 1, blocks[2 + j], me).wait_recv()
                copy(a, j + 4, blocks[2 + j], sibling).start()
                if j < 2 and by_x(a, 0) == (j == 1):
                    relay(a).start()

        targets = [sibling, blocks[2], blocks[3]]
        relays_started = max(handled_at[m] for m, (s, u) in enumerate(units) if s in (2, 3) and by_x(0, u) == (s == 3))

        def small_copies():
            return _all_copies([small_ref], [small_all], small_send, small_recv)

        @pl.when(g == 0)
        def _():
            for cp in small_copies():
                cp.start()
            for a in range(n_arrays):
                keep(a).start()
            for u in range(n_parts):
                for k, to in enumerate(targets):
                    copy(0, k, me, to, u, from_shard=True).start()
            for a in range(1, 4):
                copy(a, 0, me, sibling, from_shard=True).start()
            load(0).start()

        @pl.when(g == n_tiles - 2)
        def _():
            for cp in small_copies():
                cp.wait()
            fetch = pltpu.make_async_copy(small_all, small_buf, local_sems.at[n_arrays + 4])
            fetch.start()
            fetch.wait()
            meta_ref[0:TILE - N_META, :] = jnp.zeros((TILE - N_META, D_MODEL), F32)
            meta_ref[TILE - N_META:TILE, :] = jnp.concatenate([small_buf[d, 0:N_META, :] for d in range(N_DEV)], axis=1)

        @pl.when(g < n_tiles)
        def _():
            s0 = jnp.where(g == n_tiles - 1, meta_ref[...], x_ref[...])
            r = lax.rsqrt(_rowmean(s0 * s0) + EPS)
            h32 = (s0 * r) * g_ref[...]
            ht_ref[...] = h32.T.astype(BF16)
            h_all[pl.ds(pl.multiple_of(g * TILE, TILE), TILE), :] = h32.astype(BF16)

        for m in range(n_units):
            @pl.when(g == n_tiles + m)
            def _(m=m):
                load(m).wait()
                for later in range(m + 1, n_units):
                    if handled_at[later] == m:
                        arrive(later)
                if m + 1 < n_units:
                    load(m + 1).start()
                if m == relays_started:
                    for a in range(1, 4):
                        for k in (1, 2):
                            copy(a, k, me, targets[k], from_shard=True).start()
                if m == n_units - 2:
                    pass_on_out(0)
                    pass_on_out(1)
                if m >= 2:
                    store(m - 2).wait()

        m_now = jnp.maximum(g - n_tiles, 0)
        u_now = functools.reduce(jnp.add, [jnp.where(m_now == m, u, 0) for m, (_, u) in enumerate(units)])
        for u, (_, width) in enumerate(PARTS):
            @pl.when((g >= n_tiles) & (u_now == u))
            def _(width=width):
                w = wbuf[m_now % 2, :, 0:width]
                for r in range(n_chunk):
                    rbuf[m_now % 2, r * chunk:(r + 1) * chunk, 0:width] = _dot(h_all[r * chunk:(r + 1) * chunk, :], w)

        for m in range(n_units):
            @pl.when(g == n_tiles + m)
            def _(m=m):
                store(m).start()

        @pl.when(g == n_steps - 1)
        def _():
            pass_on_out(2)
            store(n_units - 2).wait()
            store(n_units - 1).wait()
            for a in range(1, 4):
                copy(a, 0, sibling, me).wait_recv()
                for j in range(3):
                    copy(a, 4 + j, blocks[5 + j], me).wait_recv()
            for a in range(n_arrays):
                for u in range(n_parts if a == 0 else 1):
                    for k, to in enumerate(targets):
                        copy(a, k, me, to, u, from_shard=True).wait_send()
                    relay(a, u).wait_send()
                    for j in range(3):
                        copy(a, 4 + j, blocks[2 + j], sibling, u).wait_send()
                keep(a).wait()

    n_x = n_tiles - 1
    return pl.pallas_call(
        body, name="gather_norm_proj",
        out_shape=[jax.ShapeDtypeStruct((D_MODEL, tp), BF16), jax.ShapeDtypeStruct((tp, D_IN), F32),
                   jax.ShapeDtypeStruct((TILE, D_MODEL), F32), jax.ShapeDtypeStruct((N_DEV,) + small_shard.shape[1:], F32),
                   jax.ShapeDtypeStruct((N_DEV,) + w_in_shard.shape, BF16)]
                  + [jax.ShapeDtypeStruct((N_DEV,) + w.shape, BF16) for w in w_out_shards],
        grid_spec=pltpu.PrefetchScalarGridSpec(
            num_scalar_prefetch=1, grid=(n_steps,),
            in_specs=[pl.BlockSpec((TILE, D_MODEL), lambda g, pos_ref: (jnp.minimum(g, n_x - 1), 0)),
                      _VMEM, _ANY, _ANY, _ANY, _ANY, _ANY],
            out_specs=[pl.BlockSpec((D_MODEL, TILE), lambda g, pos_ref: (0, jnp.minimum(g, n_tiles - 1))),
                       _ANY, _VMEM, _ANY, _ANY, _ANY, _ANY, _ANY],
            scratch_shapes=[pltpu.VMEM((tp, D_MODEL), BF16), pltpu.VMEM((2, D_MODEL, widest), BF16),
                            pltpu.VMEM((2, tp, widest), F32), pltpu.VMEM((N_DEV,) + small_shard.shape[1:], F32),
                            pltpu.SemaphoreType.DMA((7 * n_parts + 21,)), pltpu.SemaphoreType.DMA((7 * n_parts + 21,)),
                            pltpu.SemaphoreType.DMA((9,)),
                            pltpu.SemaphoreType.DMA((N_DEV,)), pltpu.SemaphoreType.DMA((N_DEV,))]),
        compiler_params=pltpu.CompilerParams(dimension_semantics=("arbitrary",), vmem_limit_bytes=VMEM_LIMIT),
    )(pos, x2d, norm_g, small_shard, w_in_shard, *w_out_shards)


C_AVAL, C_AGLU, C_AZ, C_BB, C_BC, C_BX, C_BZ, C_GA, C_GB = (k * D_MODEL for k in range(9))
S_AZ, S_BB, S_BZ, S_GA, S_GB = (k * D_MODEL for k in range(5))


def _fused_pass(proj, x2d, tgt2d, meta_tile, conv_a_w, conv_a_b, ln_a_g, ln_a_b, b_a_out, conv_b_w, final_g,
                w_a, w_b, w_o, w_a_t, w_b_t, w_o_t, n_tiles):
    T = TILE
    tp = n_tiles * T
    inv_d = 1.0 / D_MODEL

    def block_of(tile):
        return jnp.where(tile == 0, n_tiles - 1, tile - 1)

    def cur(i):
        return block_of(jnp.minimum(i, n_tiles - 1))

    def prev(i):
        return block_of(jnp.clip(i - 1, 0, n_tiles - 1))

    def xblk(i):
        return jnp.maximum(jnp.minimum(i, n_tiles - 1) - 1, 0)

    def body(proj_ref, aprev, cprev, x_ref, tgt_ref, meta_ref, caw_ref, cab_ref, lng_ref, lnb_ref, bao_ref, cbw_ref,
             fg_ref, wa_ref, wb_ref, wo_ref, wat_ref, wbt_ref, wot_ref,
             dproj_ref, ds1_ref, lhs_ref, rhs_ref, small_ref,
             ua0_buf, cb_buf, dua1_buf, dc3_buf, stage, ua1_buf, c3_buf,
             dpa_buf, dpb_buf, dcaw8, dcbw8, shift_buf):
        i = pl.program_id(0)
        this, before = i % 2, 1 - i % 2

        @pl.when(i == 0)
        def _init():
            for buf in (ua0_buf, cb_buf, dua1_buf, dc3_buf, dcaw8, dcbw8):
                buf[...] = jnp.zeros(buf.shape, buf.dtype)
            small_ref[...] = jnp.zeros(small_ref.shape, F32)

        @pl.when(i >= 1)
        def _emit_stage():
            dproj_ref[:, C_AZ:C_BC] = stage[:, S_AZ:S_BZ]
            dproj_ref[:, C_BZ:D_IN] = stage[:, S_BZ:S_GB + D_MODEL]

        @pl.when(i < n_tiles)
        def _front():
            def conv_chunk(cc, carry):
                c0 = pl.multiple_of(cc * LANES, LANES)
                lanes = pl.ds(c0, LANES)

                def col(base):
                    return pl.ds(pl.multiple_of(base + cc * LANES, LANES), LANES)

                ua0 = proj_ref[:, col(C_AVAL)] * _sigmoid(proj_ref[:, col(C_AGLU)])
                ua0_buf[this, 0:HALO, lanes] = ua0_buf[before, T:T + HALO, lanes]
                ua0_buf[this, HALO:HALO + T, lanes] = ua0
                acc = jnp.broadcast_to(cab_ref[:, lanes], (T, LANES))
                lead = HALO - (CONV_A - 1)
                for r in range(SUBLANES):
                    taps = [k for k in range(CONV_A) if (k + lead) % SUBLANES == r]
                    rows = T + SUBLANES * max((k + lead) // SUBLANES for k in taps)
                    if r:
                        shift_buf[r, 0:rows, :] = ua0_buf[this, pl.ds(r, rows), lanes]
                    for k in taps:
                        q = (k + lead) // SUBLANES
                        if r:
                            win = shift_buf[r, SUBLANES * q:SUBLANES * q + T, :]
                        else:
                            win = ua0_buf[this, pl.ds(SUBLANES * q, T), lanes]
                        acc = acc + caw_ref[k:k + 1, lanes] * win
                ua1_buf[:, lanes] = acc
                cb = proj_ref[:, col(C_BC)] * proj_ref[:, col(C_BX)]
                cb_buf[this, 0:SUBLANES, lanes] = cb_buf[before, T:T + SUBLANES, lanes]
                cb_buf[this, SUBLANES:SUBLANES + T, lanes] = cb
                lead_b = SUBLANES - (CONV_B - 1)
                acc3 = cbw_ref[0:1, lanes] * cb_buf[this, pl.ds(lead_b, T), lanes]
                for k in range(1, CONV_B):
                    acc3 = acc3 + cbw_ref[k:k + 1, lanes] * cb_buf[this, pl.ds(lead_b + k, T), lanes]
                c3_buf[:, lanes] = acc3
                return carry

            lax.fori_loop(0, N_CHUNK, conv_chunk, 0)

            ua1 = ua1_buf[...]
            xc = ua1 - _rowmean(ua1)
            rstd = lax.rsqrt(_rowmean(xc * xc) + EPS)
            xhat = xc * rstd
            ua2 = xhat * lng_ref[...] + lnb_ref[...]
            sg2 = _sigmoid(ua2)
            ua3 = ua2 * sg2
            a_z = proj_ref[:, C_AZ:C_AZ + D_MODEL]
            sz = _sigmoid(a_z)
            silu_az = a_z * sz
            lhs_ref[0] = (ua3 * silu_az).astype(BF16)
            b_z = proj_ref[:, C_BZ:C_BZ + D_MODEL]
            sbz = _sigmoid(b_z)
            silu_bz = b_z * sbz
            b_b = proj_ref[:, C_BB:C_BB + D_MODEL]
            c3 = c3_buf[...]
            ub = b_b * c3
            lhs_ref[1] = (ub * silu_bz).astype(BF16)

            ya = _dot(lhs_ref[0], wa_ref[...]) + bao_ref[...]
            yb = _dot(lhs_ref[1], wb_ref[...])
            sga = _sigmoid(proj_ref[:, C_GA:C_GA + D_MODEL])
            sgb = _sigmoid(proj_ref[:, C_GB:C_GB + D_MODEL])
            m_b = (sga * ya + sgb * yb).astype(BF16)
            lhs_ref[2] = m_b
            s0 = jnp.where(i == 0, meta_ref[...], x_ref[...])
            s1 = s0 + _dot(m_b, wo_ref[...])
            r1 = lax.rsqrt(_rowmean(s1 * s1) + EPS)
            y = (s1 * r1) * fg_ref[...]
            is_token = (i >= 1).astype(F32)
            err = (y - tgt_ref[...]) * is_token
            small_ref[ROW_LOSS:ROW_LOSS + 1, :] += (0.5 * inv_d) * _colsum(err * err)
            dy = err * inv_d
            small_ref[ROW_FINAL_G:ROW_FINAL_G + 1, :] += _colsum(dy * (s1 * r1))
            gy = dy * fg_ref[...]
            ds1 = r1 * gy - s1 * ((r1 * r1 * r1) * _rowmean(gy * s1))
            ds1_ref[...] = ds1
            ds1_b = ds1.astype(BF16)
            rhs_ref[2] = ds1_b
            dm = _dot(ds1_b, wot_ref[...])
            dya = dm * sga
            dyb = dm * sgb
            stage[:, S_GA:S_GA + D_MODEL] = (dya * ya * (1.0 - sga)).astype(BF16)
            stage[:, S_GB:S_GB + D_MODEL] = (dyb * yb * (1.0 - sgb)).astype(BF16)
            small_ref[ROW_B_A_OUT:ROW_B_A_OUT + 1, :] += _colsum(dya)
            dya_b = dya.astype(BF16)
            dyb_b = dyb.astype(BF16)
            rhs_ref[0] = dya_b
            rhs_ref[1] = dyb_b
            dpa_buf[...] = _dot(dya_b, wat_ref[...])
            dpb_buf[...] = _dot(dyb_b, wbt_ref[...])

            dpa = dpa_buf[...]
            stage[:, S_AZ:S_AZ + D_MODEL] = (dpa * ua3 * (sz + silu_az * (1.0 - sz))).astype(BF16)
            dua2 = dpa * silu_az * (sg2 + ua3 * (1.0 - sg2))
            small_ref[ROW_LN_G:ROW_LN_G + 1, :] += _colsum(dua2 * xhat)
            small_ref[ROW_LN_B:ROW_LN_B + 1, :] += _colsum(dua2)
            dxh = dua2 * lng_ref[...]
            dua1 = rstd * (dxh - _rowmean(dxh) - xhat * _rowmean(dxh * xhat))
            small_ref[ROW_CONV_A_B:ROW_CONV_A_B + 1, :] += _colsum(dua1)
            dua1_buf[this, 0:T, :] = dua1
            dua1_buf[before, T:T + HALO, :] = dua1[0:HALO]
            dpb = dpb_buf[...]
            stage[:, S_BZ:S_BZ + D_MODEL] = (dpb * ub * (sbz + silu_bz * (1.0 - sbz))).astype(BF16)
            dub = dpb * silu_bz
            stage[:, S_BB:S_BB + D_MODEL] = (dub * c3).astype(BF16)
            dc3 = dub * b_b
            dc3_buf[this, 0:T, :] = dc3
            dc3_buf[before, T:T + SUBLANES, :] = dc3[0:SUBLANES]

        @pl.when(i == n_tiles)
        def _no_later_tile():
            dua1_buf[before, T:T + HALO, :] = jnp.zeros((HALO, D_MODEL), F32)
            dc3_buf[before, T:T + SUBLANES, :] = jnp.zeros((SUBLANES, D_MODEL), F32)

        @pl.when(i >= 1)
        def _lagged():
            def convt_chunk(cc, carry):
                c0 = pl.multiple_of(cc * LANES, LANES)
                lanes = pl.ds(c0, LANES)

                def col(base):
                    return pl.ds(pl.multiple_of(base + cc * LANES, LANES), LANES)

                ua0 = ua0_buf[before, HALO:HALO + T, lanes]
                acc = jnp.zeros((T, LANES), F32)
                for r in range(SUBLANES):
                    shifts = [j for j in range(CONV_A) if j % SUBLANES == r]
                    rows = T + shifts[-1] - r
                    if r:
                        shift_buf[r, 0:rows, :] = dua1_buf[before, pl.ds(r, rows), lanes]
                    for j in shifts:
                        k = CONV_A - 1 - j
                        if r:
                            later = shift_buf[r, j - r:j - r + T, :]
                        else:
                            later = dua1_buf[before, pl.ds(j, T), lanes]
                        acc = acc + caw_ref[k:k + 1, lanes] * later
                        dcaw8[SUBLANES * k:SUBLANES * (k + 1), lanes] += _fold8(ua0 * later)
                a_val = aprev[:, col(0)]
                sg = _sigmoid(aprev[:, col(D_MODEL)])
                dproj_ref[:, col(C_AVAL)] = (acc * sg).astype(BF16)
                dproj_ref[:, col(C_AGLU)] = (acc * a_val * (sg * (1.0 - sg))).astype(BF16)

                cb = cb_buf[before, SUBLANES:SUBLANES + T, lanes]
                acc3 = jnp.zeros((T, LANES), F32)
                for j in range(CONV_B):
                    k = CONV_B - 1 - j
                    later = dc3_buf[before, pl.ds(j, T), lanes]
                    acc3 = acc3 + cbw_ref[k:k + 1, lanes] * later
                    dcbw8[SUBLANES * k:SUBLANES * (k + 1), lanes] += _fold8(cb * later)
                dproj_ref[:, col(C_BC)] = (acc3 * cprev[:, col(D_MODEL)]).astype(BF16)
                dproj_ref[:, col(C_BX)] = (acc3 * cprev[:, col(0)]).astype(BF16)
                return carry

            lax.fori_loop(0, N_CHUNK, convt_chunk, 0)

        @pl.when(i == n_tiles)
        def _finish():
            for k in range(CONV_A):
                small_ref[ROW_CONV_A_W + k:ROW_CONV_A_W + k + 1, :] = _colsum(dcaw8[SUBLANES * k:SUBLANES * (k + 1), :])
            for k in range(CONV_B):
                small_ref[ROW_CONV_B_W + k:ROW_CONV_B_W + k + 1, :] = _colsum(dcbw8[SUBLANES * k:SUBLANES * (k + 1), :])

    pair = 2 * D_MODEL
    return pl.pallas_call(
        body, name="fused_pass", grid=(n_tiles + 1,),
        out_shape=[
            jax.ShapeDtypeStruct((tp, D_IN), BF16),
            jax.ShapeDtypeStruct((tp, D_MODEL), F32),
            jax.ShapeDtypeStruct((3, tp, D_MODEL), BF16),
            jax.ShapeDtypeStruct((3, tp, D_MODEL), BF16),
            jax.ShapeDtypeStruct((SMALL_A_ROWS, D_MODEL), F32),
        ],
        in_specs=[
            pl.BlockSpec((T, D_IN), lambda i: (cur(i), 0)),
            pl.BlockSpec((T, pair), lambda i: (prev(i), C_AVAL // pair)),
            pl.BlockSpec((T, pair), lambda i: (prev(i), C_BC // pair)),
            pl.BlockSpec((T, D_MODEL), lambda i: (xblk(i), 0)),
            pl.BlockSpec((T, D_MODEL), lambda i: (xblk(i), 0)),
            _VMEM, _VMEM, _VMEM, _VMEM, _VMEM, _VMEM, _VMEM, _VMEM,
            *[_resident((D_MODEL, D_MODEL)) for _ in range(6)],
        ],
        out_specs=[
            pl.BlockSpec((T, D_IN), lambda i: (prev(i), 0)),
            pl.BlockSpec((T, D_MODEL), lambda i: (cur(i), 0)),
            pl.BlockSpec((3, T, D_MODEL), lambda i: (0, cur(i), 0)),
            pl.BlockSpec((3, T, D_MODEL), lambda i: (0, cur(i), 0)),
            _VMEM,
        ],
        scratch_shapes=[
            pltpu.VMEM((2, HALO + T, D_MODEL), F32),
            pltpu.VMEM((2, SUBLANES + T, D_MODEL), F32),
            pltpu.VMEM((2, T + HALO, D_MODEL), F32),
            pltpu.VMEM((2, T + SUBLANES, D_MODEL), F32),
            pltpu.VMEM((T, 5 * D_MODEL), BF16),
            pltpu.VMEM((T, D_MODEL), F32),
            pltpu.VMEM((T, D_MODEL), F32),
            pltpu.VMEM((T, D_MODEL), F32),
            pltpu.VMEM((T, D_MODEL), F32),
            pltpu.VMEM((32 * SUBLANES, D_MODEL), F32),
            pltpu.VMEM((SUBLANES * SUBLANES, D_MODEL), F32),
            pltpu.VMEM((SUBLANES, T + HALO, LANES), F32),
        ],
        compiler_params=pltpu.CompilerParams(dimension_semantics=("arbitrary",), vmem_limit_bytes=VMEM_LIMIT),
    )(proj, proj, proj, x2d, tgt2d, meta_tile, conv_a_w, conv_a_b, ln_a_g, ln_a_b, b_a_out, conv_b_w, final_g,
      w_a, w_b, w_o, w_a_t, w_b_t, w_o_t)


def _input_bwd(dproj, ds1, x2d, meta_tile, norm_g, w_in_all, row_tile):
    seq = x2d.shape[0]
    n_steps = seq // row_tile
    meta_block = seq // TILE

    def backward(dp_ref, ds1_ref, s0_ref, g_ref, w_ref, out_ref, vec_ref):
        dh = _dot_nt(dp_ref[:, 0:COLS], w_ref[0])
        for j in range(1, N_DEV):
            dh = dh + _dot_nt(dp_ref[:, j * COLS:(j + 1) * COLS], w_ref[j])
        s0v = s0_ref[...]
        r = lax.rsqrt(_rowmean(s0v * s0v) + EPS)
        gh = dh * g_ref[...]
        out_ref[...] = ds1_ref[...] + r * gh - s0v * ((r * r * r) * _rowmean(gh * s0v))
        vec_ref[ROW_NORM_G:ROW_NORM_G + 1, :] += _colsum(dh * (s0v * r))

    def body(dp_ref, ds1_ref, x_ref, dpm_ref, ds1m_ref, meta_ref, g_ref, w_ref, gx_ref, small_ref, gmeta_buf):
        t = pl.program_id(0)

        @pl.when(t == 0)
        def _():
            small_ref[...] = jnp.zeros(small_ref.shape, F32)

        backward(dp_ref, ds1_ref, x_ref, g_ref, w_ref, gx_ref, small_ref)

        @pl.when(t == n_steps - 1)
        def _():
            backward(dpm_ref, ds1m_ref, meta_ref, g_ref, w_ref, gmeta_buf, small_ref)
            small_ref[ROW_META:ROW_META + N_META, :] = gmeta_buf[TILE - N_META:TILE, :]

    return pl.pallas_call(
        body, name="input_bwd", grid=(n_steps,),
        out_shape=[jax.ShapeDtypeStruct(x2d.shape, F32), jax.ShapeDtypeStruct((SMALL_B_ROWS, D_MODEL), F32)],
        in_specs=[pl.BlockSpec((row_tile, D_IN), lambda t: (t, 0)),
                  pl.BlockSpec((row_tile, D_MODEL), lambda t: (t, 0)),
                  pl.BlockSpec((row_tile, D_MODEL), lambda t: (t, 0)),
                  pl.BlockSpec((TILE, D_IN), lambda t: (meta_block, 0)),
                  pl.BlockSpec((TILE, D_MODEL), lambda t: (meta_block, 0)),
                  _VMEM, _VMEM, _resident((N_DEV, D_MODEL, COLS))],
        out_specs=[pl.BlockSpec((row_tile, D_MODEL), lambda t: (t, 0)), _VMEM],
        scratch_shapes=[pltpu.VMEM((TILE, D_MODEL), F32)],
        compiler_params=pltpu.CompilerParams(dimension_semantics=("arbitrary",), vmem_limit_bytes=VMEM_LIMIT),
    )(dproj, ds1, x2d, dproj, ds1, meta_tile, norm_g, w_in_all)


def _grad_w_in_half(pos, h_t, dproj, k_tile, other_side, rides, name, after=None, add_to=None, narrow=False):
    tp = h_t.shape[1]
    n_k = tp // k_tile
    order = [] if after is None else [after]
    summing = add_to is not None
    assert not (summing and narrow)

    def column_block(q, k, pos_ref):
        return k, 2 * q + (1 - pos_ref[2] if other_side else pos_ref[2])

    def body(pos_ref, h_ref, dp_ref, *refs):
        acc = refs[-1]

        @pl.when(pl.program_id(1) == 0)
        def _():
            acc[...] = refs[0][...].astype(F32) if summing else jnp.zeros(acc.shape, F32)

        acc[...] += _dot(h_ref[...], dp_ref[...])

        if summing or narrow:
            @pl.when(pl.program_id(1) == n_k - 1)
            def _():
                refs[-2][...] = acc[...].astype(BF16)

        if summing:
            @pl.when((pl.program_id(1) == n_k - 1) & (pl.program_id(0) == 2 * pos_ref[0] + pos_ref[1]))
            def _():
                refs[-3][...] = acc[...]

    ride = [a for _, arrays in rides for a in arrays]
    n_arr = len(ride)
    ride_shapes, ride_sems = _ride_shapes(rides)
    block = (None, D_MODEL, COLS)
    extra_in = [add_to] if summing else []
    extra_in_specs = [pl.BlockSpec(block, lambda q, k, pos_ref: (q, 0, 0))] if summing else []
    extra_out = [jax.ShapeDtypeStruct((4, D_MODEL, COLS), BF16)] if summing else []
    extra_out_specs = [pl.BlockSpec(block, lambda q, k, pos_ref: (q ^ (2 * pos_ref[0] + pos_ref[1]), 0, 0))] \
        if summing else []
    body = _riding(body, 3 + len(extra_in) + len(order), 1 + len(extra_out), rides,
                   lambda: (pl.program_id(0) == 0) & (pl.program_id(1) == 0),
                   lambda: (pl.program_id(0) == 3) & (pl.program_id(1) == n_k - 1))
    return pl.pallas_call(
        body, name=name,
        out_shape=[jax.ShapeDtypeStruct((1 if summing else 4, D_MODEL, COLS), BF16 if narrow else F32)]
        + extra_out + ride_shapes,
        grid_spec=pltpu.PrefetchScalarGridSpec(
            num_scalar_prefetch=1, grid=(4, n_k),
            in_specs=[pl.BlockSpec((D_MODEL, k_tile), lambda q, k, pos_ref: (0, k)),
                      pl.BlockSpec((k_tile, COLS), column_block)] + extra_in_specs + [_ANY] * (len(order) + n_arr),
            out_specs=[pl.BlockSpec(block, lambda q, k, pos_ref: (0 if summing else q, 0, 0))]
            + extra_out_specs + [_ANY] * n_arr,
            scratch_shapes=([pltpu.VMEM((D_MODEL, COLS), F32)] if summing or narrow else []) + ride_sems),
        compiler_params=pltpu.CompilerParams(dimension_semantics=("arbitrary", "arbitrary"),
                                             vmem_limit_bytes=VMEM_LIMIT),
    )(pos, h_t, dproj, *extra_in, *order, *ride)


def _grad_w_out(lhs, rhs, k_tile, after):
    tp = lhs.shape[1]

    def body(a_ref, b_ref, after_ref, o_ref):
        @pl.when(pl.program_id(1) == 0)
        def _():
            o_ref[...] = jnp.zeros(o_ref.shape, F32)

        o_ref[...] += _dot_tn(a_ref[...], b_ref[...]).reshape(N_DEV, ROWS_OUT, D_MODEL)

    return pl.pallas_call(
        body, name="grad_w_out", grid=(3, tp // k_tile),
        out_shape=jax.ShapeDtypeStruct((N_DEV, 3, ROWS_OUT, D_MODEL), F32),
        in_specs=[pl.BlockSpec((None, k_tile, D_MODEL), lambda w, k: (w, k, 0)),
                  pl.BlockSpec((None, k_tile, D_MODEL), lambda w, k: (w, k, 0)), _ANY],
        out_specs=pl.BlockSpec((N_DEV, None, ROWS_OUT, D_MODEL), lambda w, k: (0, w, 0, 0)),
        compiler_params=pltpu.CompilerParams(dimension_semantics=("arbitrary", "arbitrary"),
                                             vmem_limit_bytes=VMEM_LIMIT),
    )(lhs, rhs, after)


def _adamw_math(w, g, m, v):
    m = ADAM_B1 * m + (1.0 - ADAM_B1) * g
    v = ADAM_B2 * v + (1.0 - ADAM_B2) * (g * g)
    m_hat = m / (1.0 - ADAM_B1 ** ADAM_STEP)
    v_hat = v / (1.0 - ADAM_B2 ** ADAM_STEP)
    delta = -ADAM_LR * (m_hat / (jnp.sqrt(v_hat) + ADAM_EPS) + ADAM_WD * w)
    return delta, m, v


def _adamw_sharded(pos, mine, theirs, landed, weights, row_tile, name, after=None):
    order = [] if after is None else [after]
    rows, n = weights[0][0].shape
    n_slots = mine.shape[0]
    per_shard = rows // row_tile
    assert per_shard == 1 or len(weights) == 1

    def mine_map(j, t, pos_ref):
        chip = 2 * pos_ref[0] + pos_ref[1]
        return {N_DEV: 2 * chip + pos_ref[2], 4: chip, 1: 0}[n_slots], j * per_shard + t, 0

    def theirs_map(j, t, pos_ref):
        return 2 * pos_ref[0] + pos_ref[1], j * per_shard + t, 0

    def body(pos_ref, mine_ref, *refs):
        if theirs is not None:
            g = mine_ref[...] + refs[0][...]
            refs = refs[1:]
        else:
            g = mine_ref[...]
        land_ref, refs = refs[0], refs[1:]
        ins, outs = refs[:3 * len(weights)], refs[3 * len(weights) + len(order):]
        for k in range(3):
            g = g + land_ref[k].astype(F32)
        for j in range(len(weights)):
            @pl.when(pl.program_id(0) == j)
            def _(j=j):
                w_ref, m_ref, v_ref = ins[3 * j:3 * j + 3]
                delta, m_new, v_new = _adamw_math(w_ref[...], g, m_ref[...], v_ref[...])
                for ref, val in zip(outs[4 * j:4 * j + 4], (g, delta, m_new, v_new)):
                    ref[...] = val

    tile = pl.BlockSpec((row_tile, n), lambda j, t, pos_ref: (t, 0))
    res = pl.pallas_call(
        body, name=name,
        out_shape=[jax.ShapeDtypeStruct((rows, n), F32)] * (4 * len(weights)),
        grid_spec=pltpu.PrefetchScalarGridSpec(
            num_scalar_prefetch=1, grid=(len(weights), per_shard),
            in_specs=[pl.BlockSpec((None, row_tile, n), mine_map)]
            + ([pl.BlockSpec((None, row_tile, n), theirs_map)] if theirs is not None else [])
            + [pl.BlockSpec((3, row_tile, n), lambda j, t, pos_ref: (0, j * per_shard + t, 0))]
            + [tile] * (3 * len(weights)) + [_ANY] * len(order),
            out_specs=[tile] * (4 * len(weights))),
        compiler_params=pltpu.CompilerParams(dimension_semantics=("arbitrary", "arbitrary")),
    )(pos, mine, *([theirs] if theirs is not None else []), landed, *[a for wmv in weights for a in wmv], *order)
    return [res[4 * j:4 * j + 4] for j in range(len(weights))]


def _adamw_small(gathered, gathered_cols, params):
    n_par, n_src = len(params), len(gathered)

    def body(*refs):
        g_refs, gc_refs = refs[:n_src], refs[n_src:2 * n_src]
        ins = refs[2 * n_src:2 * n_src + 3 * n_par]
        outs = refs[2 * n_src + 3 * n_par:]
        loss_ref = outs[4 * n_par]

        def reduced(ref, row, n_rows):
            g = ref[0, row:row + n_rows, :]
            for d in range(1, N_DEV):
                g = g + ref[d, row:row + n_rows, :]
            return g

        for p, (src, row, n_rows, sharded, _, _, _) in enumerate(params):
            g = reduced((gc_refs if sharded else g_refs)[src], row, n_rows)
            w_ref, m_ref, v_ref = ins[3 * p:3 * p + 3]
            delta, m_new, v_new = _adamw_math(w_ref[...], g, m_ref[...], v_ref[...])
            outs[4 * p][...] = g
            outs[4 * p + 1][...] = delta
            outs[4 * p + 2][...] = m_new
            outs[4 * p + 3][...] = v_new
        loss = jnp.sum(reduced(g_refs[0], ROW_LOSS, 1), axis=1, keepdims=True)
        loss_ref[...] = jnp.broadcast_to(loss, loss_ref.shape)

    out_shape = []
    for (_, _, _, _, w, _, _) in params:
        out_shape += [jax.ShapeDtypeStruct(w.shape, F32)] * 4
    out_shape.append(jax.ShapeDtypeStruct((1, LANES), F32))
    flat = [a for (_, _, _, _, w, m, v) in params for a in (w, m, v)]
    return pl.pallas_call(
        body, name="adamw_small", out_shape=out_shape,
        in_specs=[_VMEM] * (2 * n_src + len(flat)), out_specs=[_VMEM] * len(out_shape),
    )(*gathered, *gathered_cols, *flat)


def _pad_rows(a, rows):
    return jnp.concatenate([a, jnp.zeros((rows - a.shape[0], a.shape[1]), a.dtype)], axis=0)


def kernel(x, meta_tokens, norm_g, w_in, conv_a_w, conv_a_b, ln_a_g, ln_a_b, w_a_out, b_a_out, conv_b_w, w_b_out, w_out, final_g, loss_target, m_meta_tokens, m_norm_g, m_w_in, m_conv_a_w, m_conv_a_b, m_ln_a_g, m_ln_a_b, m_w_a_out, m_b_a_out, m_conv_b_w, m_w_b_out, m_w_out, m_final_g, v_meta_tokens, v_norm_g, v_w_in, v_conv_a_w, v_conv_a_b, v_ln_a_g, v_ln_a_b, v_w_a_out, v_b_a_out, v_conv_b_w, v_w_b_out, v_w_out, v_final_g):
    seq = x.shape[1]
    assert x.shape == (1, seq, D_MODEL) and seq % TILE == 0 and w_in.shape == (1, D_MODEL, COLS)
    n_tiles = seq // TILE + 1
    tp = n_tiles * TILE
    pos = jnp.stack([lax.axis_index("x"), lax.axis_index("y"), lax.axis_index("c")]).astype(jnp.int32)
    me = 4 * pos[0] + 2 * pos[1] + pos[2]
    x2d = x[0]
    tgt2d = loss_target[0]

    small = jnp.concatenate([meta_tokens, _pad_rows(conv_a_w[0], 32), _pad_rows(conv_b_w[0], SUBLANES)], axis=0)
    final_g2 = final_g.reshape(1, D_MODEL)

    w_out_shards = [w[0].astype(BF16) for w in (w_a_out, w_b_out, w_out)]
    h_t, proj, meta_tile, small_params, w_in_all, *w_out_all = _gather_norm_proj(
        pos, x2d, small[None], norm_g, w_in[0].astype(BF16), w_out_shards, 3)
    small_params = small_params.transpose(1, 0, 2).reshape(small.shape[0], D_MODEL)
    conv_a_full, conv_b_full = small_params[N_META:N_META + 32], small_params[N_META + 32:]
    w_out_all = [w.reshape(D_MODEL, D_MODEL) for w in w_out_all]
    w_out_all_t = [w.T for w in w_out_all]
    dproj, ds1, lhs, rhs, small_a = _fused_pass(
        proj, x2d, tgt2d, meta_tile, conv_a_full, conv_a_b, ln_a_g, ln_a_b, b_a_out, conv_b_full, final_g2,
        w_out_all[0], w_out_all[1], w_out_all[2], w_out_all_t[0], w_out_all_t[1], w_out_all_t[2], n_tiles)
    k_tile = tp // 3
    gw_far, small_a_all = _grad_w_in_half(pos, h_t, dproj, k_tile, True, [("all", (small_a[None],))], "grad_w_in_far",
                                          narrow=True)
    sems, sent, landing, token = _start_exchanges([("sibling_half", (gw_far,))], "rs_far_start")
    gw_out = _grad_w_out(lhs, rhs, k_tile, token).reshape(N_DEV, 3 * ROWS_OUT, D_MODEL)
    (their_in,) = _wait_exchanges([("sibling_half", 1)], sems, sent, landing, gw_out, "rs_far_wait")
    sems_o, sent_o, landing_o, token = _start_exchanges([("sibling", (gw_out,))], "rs_out_start")
    gw_near, parts_in = _grad_w_in_half(pos, h_t, dproj, k_tile, False, [], "grad_w_in_near", after=token,
                                        add_to=their_in)
    sems_i, sent_i, landing_i, token = _start_exchanges([("chips_by_relation", (parts_in,))], "rs_chips_in_start")
    gw_out, their_out = _wait_exchanges([("sibling", 1)], sems_o, sent_o, landing_o, token, "rs_out_wait",
                                        keep_sources=True)
    parts_out = _chip_partial(pos, gw_out, their_out, (1, 2, 3), BF16, ROWS_OUT, "rs_parts_w_out")
    sems_o, sent_o, landing_o, token = _start_exchanges([("chips", (parts_out,))], "rs_chips_out_start")
    grad_x, small_b = _input_bwd(dproj, ds1, x2d, meta_tile, norm_g + token[0, 0], w_in_all, min(512, seq))

    sems_s, sent_s, landing_s, token = _start_exchanges([("all", (small_b[None],))], "gather_small_grads_start")
    (land_in,) = _wait_exchanges([("chips_by_relation", 1)], sems_i, sent_i, landing_i, token, "rs_chips_in_wait")
    (res_in,) = _adamw_sharded(pos, gw_near, None, land_in, [(w_in[0], m_w_in[0], v_w_in[0])], 128, "adamw_w_in")
    (land_out,) = _wait_exchanges([("chips", 1)], sems_o, sent_o, landing_o, res_in[0], "rs_chips_out_wait")
    res_out = _adamw_sharded(
        pos, gw_out, their_out, land_out,
        [(w_a_out[0], m_w_a_out[0], v_w_a_out[0]), (w_b_out[0], m_w_b_out[0], v_w_b_out[0]),
         (w_out[0], m_w_out[0], v_w_out[0])], ROWS_OUT, "adamw_w_out")
    (small_b_all,) = _wait_exchanges([("all", 1)], sems_s, sent_s, landing_s, res_out[2][0], "gather_small_grads_wait")
    small_grads = [small_a_all, small_b_all]
    small_cols = [lax.dynamic_slice_in_dim(g, me * LANES, LANES, axis=2) for g in small_grads]
    params = [
        (1, ROW_META, N_META, True, meta_tokens, m_meta_tokens, v_meta_tokens),
        (1, ROW_NORM_G, 1, False, norm_g, m_norm_g, v_norm_g),
        (0, ROW_CONV_A_W, CONV_A, True, conv_a_w[0], m_conv_a_w[0], v_conv_a_w[0]),
        (0, ROW_CONV_A_B, 1, False, conv_a_b, m_conv_a_b, v_conv_a_b),
        (0, ROW_LN_G, 1, False, ln_a_g, m_ln_a_g, v_ln_a_g),
        (0, ROW_LN_B, 1, False, ln_a_b, m_ln_a_b, v_ln_a_b),
        (0, ROW_B_A_OUT, 1, False, b_a_out, m_b_a_out, v_b_a_out),
        (0, ROW_CONV_B_W, CONV_B, True, conv_b_w[0], m_conv_b_w[0], v_conv_b_w[0]),
        (0, ROW_FINAL_G, 1, False, final_g2, m_final_g.reshape(1, D_MODEL), v_final_g.reshape(1, D_MODEL)),
    ]
    res_small = _adamw_small(small_grads, small_cols, params)
    loss = res_small[-1][0, 0]

    def small_res(p, kind, shape):
        return res_small[4 * p + kind].reshape(shape)

    per_weight = []
    for kind in range(4):
        per_weight.append([
            small_res(0, kind, meta_tokens.shape),
            small_res(1, kind, norm_g.shape),
            res_in[kind].reshape(w_in.shape),
            small_res(2, kind, conv_a_w.shape),
            small_res(3, kind, conv_a_b.shape),
            small_res(4, kind, ln_a_g.shape),
            small_res(5, kind, ln_a_b.shape),
            res_out[0][kind].reshape(w_a_out.shape),
            small_res(6, kind, b_a_out.shape),
            small_res(7, kind, conv_b_w.shape),
            res_out[1][kind].reshape(w_b_out.shape),
            res_out[2][kind].reshape(w_out.shape),
            small_res(8, kind, final_g.shape),
        ])
    return (loss, grad_x.reshape(x.shape), *per_weight[0], *per_weight[1], *per_weight[2], *per_weight[3])
```

```python
import functools

import jax
import jax.numpy as jnp
from jax import lax
from jax.experimental import pallas as pl
from jax.experimental.pallas import tpu as pltpu

D_MODEL = 1024
N_META = 16
N_DEV = 8
D_IN = 9 * D_MODEL
COLS = D_IN // N_DEV
ROWS_OUT = D_MODEL // N_DEV
CONV_A = 31
CONV_B = 3
EPS = 1e-6

ADAM_LR = 0.001
ADAM_B1 = 0.9
ADAM_B2 = 0.999
ADAM_EPS = 1e-08
ADAM_WD = 0.01
ADAM_STEP = 10

TILE = 128
LANES = 128
N_CHUNK = D_MODEL // LANES
HALO = 32
SUBLANES = 8
VMEM_LIMIT = 56 * 1024 * 1024

ROW_FINAL_G, ROW_B_A_OUT, ROW_LN_G, ROW_LN_B, ROW_CONV_A_B, ROW_LOSS = 0, 1, 2, 3, 4, 5
ROW_CONV_A_W, ROW_CONV_B_W, SMALL_A_ROWS = 8, 40, 48
ROW_NORM_G, ROW_META, SMALL_B_ROWS = 0, 8, 24

MESH = pl.DeviceIdType.MESH
_ANY = pl.BlockSpec(memory_space=pl.ANY)
_VMEM = pl.BlockSpec(memory_space=pltpu.VMEM)
BF16 = jnp.bfloat16
F32 = jnp.float32


def _resident(shape):
    return pl.BlockSpec(shape, lambda *_: (0,) * len(shape), pipeline_mode=pl.Buffered(1))


def _sigmoid(v):
    return jax.nn.sigmoid(v)


def _dot(a, b):
    return jnp.dot(a, b, preferred_element_type=F32)


def _dot_nt(a, b):
    return lax.dot_general(a, b, (((1,), (1,)), ((), ())), preferred_element_type=F32)


def _dot_tn(a, b):
    return lax.dot_general(a, b, (((0,), (0,)), ((), ())), preferred_element_type=F32)


def _colsum(v):
    return jnp.sum(v, axis=0, keepdims=True)


def _rowmean(v):
    parts = [v[:, LANES * c:LANES * (c + 1)] for c in range(v.shape[1] // LANES)]
    return jnp.sum(functools.reduce(jnp.add, parts), axis=-1, keepdims=True) * (1.0 / v.shape[1])


def _fold8(v):
    parts = [v[SUBLANES * g:SUBLANES * (g + 1)] for g in range(v.shape[0] // SUBLANES)]
    return functools.reduce(jnp.add, parts)


def _sibling_copies(srcs, dsts, send_sems, recv_sems):
    x, y, c = lax.axis_index("x"), lax.axis_index("y"), lax.axis_index("c")
    return [pltpu.make_async_remote_copy(
        src_ref=src.at[2 * q + (1 - c)], dst_ref=dst.at[q],
        send_sem=send_sems.at[4 * a + q], recv_sem=recv_sems.at[4 * a + q],
        device_id=(x, y, 1 - c), device_id_type=MESH)
        for a, (src, dst) in enumerate(zip(srcs, dsts)) for q in range(4)]


def _chip_copies(srcs, dsts, send_sems, recv_sems):
    x, y, c = lax.axis_index("x"), lax.axis_index("y"), lax.axis_index("c")
    targets = [(x, 1 - y, c), (1 - x, y, c), (1 - x, 1 - y, c)]
    return [pltpu.make_async_remote_copy(
        src_ref=src.at[k], dst_ref=dst.at[k],
        send_sem=send_sems.at[3 * a + k], recv_sem=recv_sems.at[3 * a + k],
        device_id=targets[k], device_id_type=MESH)
        for a, (src, dst) in enumerate(zip(srcs, dsts)) for k in range(3)]


def _sibling_half_copies(srcs, dsts, send_sems, recv_sems):
    x, y, c = lax.axis_index("x"), lax.axis_index("y"), lax.axis_index("c")
    return [pltpu.make_async_remote_copy(
        src_ref=src.at[q], dst_ref=dst.at[q],
        send_sem=send_sems.at[4 * a + q], recv_sem=recv_sems.at[4 * a + q],
        device_id=(x, y, 1 - c), device_id_type=MESH)
        for a, (src, dst) in enumerate(zip(srcs, dsts)) for q in range(4)]


def _all_copies(srcs, dsts, send_sems, recv_sems):
    x, y, c = lax.axis_index("x"), lax.axis_index("y"), lax.axis_index("c")
    mine = 4 * x + 2 * y + c
    copies = []
    for a, (src, dst) in enumerate(zip(srcs, dsts)):
        copies.append(pltpu.make_async_copy(src.at[0], dst.at[mine], send_sems.at[N_DEV * a]))
        for k in range(1, N_DEV):
            copies.append(pltpu.make_async_remote_copy(
                src_ref=src.at[0], dst_ref=dst.at[mine],
                send_sem=send_sems.at[N_DEV * a + k], recv_sem=recv_sems.at[N_DEV * a + k],
                device_id=(x ^ (k >> 2), y ^ ((k >> 1) & 1), c ^ (k & 1)), device_id_type=MESH))
    return copies


def _chip_copies_by_relation(srcs, dsts, send_sems, recv_sems):
    return _chip_copies([src.at[pl.ds(1, 3)] for src in srcs], dsts, send_sems, recv_sems)


_EXCHANGES = {"sibling": (4, _sibling_copies, 4), "sibling_half": (4, _sibling_half_copies, 4),
              "chips": (3, _chip_copies, 3), "chips_by_relation": (3, _chip_copies_by_relation, 3),
              "all": (N_DEV, _all_copies, N_DEV)}


def _exchange_shapes(kind, arrays):
    per_array, _, slots = _EXCHANGES[kind]
    out_shape = [jax.ShapeDtypeStruct((slots,) + a.shape[1:], a.dtype) for a in arrays]
    sems = [pltpu.SemaphoreType.DMA((per_array * len(arrays),))] * 2
    return out_shape, sems


def _ride_shapes(rides):
    shapes, sems = [], []
    for kind, arrays in rides:
        ride_shapes, ride_sems = _exchange_shapes(kind, arrays)
        shapes += ride_shapes
        sems += ride_sems
    return shapes, sems


def _riding(body, n_in, n_out, rides, is_first, is_last):
    counts = [len(arrays) for _, arrays in rides]
    n_arr = sum(counts)

    def wrapped(*refs):
        ins, srcs = refs[:n_in], refs[n_in:n_in + n_arr]
        outs = refs[n_in + n_arr:n_in + n_arr + n_out]
        dsts = refs[n_in + n_arr + n_out:n_in + 2 * n_arr + n_out]
        first_sem = len(refs) - 2 * len(rides)
        scratch, sems = refs[n_in + 2 * n_arr + n_out:first_sem], refs[first_sem:]

        def copies():
            made, at = [], 0
            for r, ((kind, _), n) in enumerate(zip(rides, counts)):
                made += _EXCHANGES[kind][1](srcs[at:at + n], dsts[at:at + n], sems[2 * r], sems[2 * r + 1])
                at += n
            return made

        @pl.when(is_first())
        def _():
            for cp in copies():
                cp.start()

        body(*ins, *outs, *scratch)

        @pl.when(is_last())
        def _():
            for cp in copies():
                cp.wait()

    return wrapped


_HBM = pl.BlockSpec(memory_space=pltpu.HBM)
_SEM = pl.BlockSpec(memory_space=pltpu.SEMAPHORE)
_FLOWS = pltpu.SideEffectType.DATAFLOW_SIDE_EFFECTING


def _start_exchanges(rides, name):
    arrays = [a for _, group in rides for a in group]
    shapes, sems = _ride_shapes(rides)
    n_arr, n_sem = len(arrays), len(sems)

    def body(*refs):
        srcs, lands = refs[:n_arr], refs[n_arr:2 * n_arr]
        sem_refs, token = refs[2 * n_arr:2 * n_arr + n_sem], refs[-1]
        at = 0
        for r, (kind, group) in enumerate(rides):
            n = len(group)
            for cp in _EXCHANGES[kind][1](srcs[at:at + n], lands[at:at + n], sem_refs[2 * r], sem_refs[2 * r + 1]):
                cp.start()
            at += n
        token[...] = jnp.zeros(token.shape, token.dtype)

    in_hbm = [pltpu.HBM(a.shape, a.dtype) for a in arrays]
    land_hbm = [pltpu.HBM(sh.shape, sh.dtype) for sh in shapes]
    res = pl.pallas_call(
        body, name=name,
        out_shape=(*sems, *in_hbm, *land_hbm, jax.ShapeDtypeStruct((SUBLANES, LANES), F32)),
        in_specs=[_HBM] * (2 * n_arr), out_specs=(*[_SEM] * n_sem, *[_HBM] * (2 * n_arr), _VMEM),
        input_output_aliases={i: n_sem + i for i in range(2 * n_arr)},
        compiler_params=pltpu.CompilerParams(has_side_effects=_FLOWS),
    )(*[pltpu.with_memory_space_constraint(a, pltpu.HBM) for a in arrays],
      *[pltpu.with_memory_space_constraint(lax.empty(sh.shape, sh.dtype), pltpu.HBM) for sh in shapes])
    return res[:n_sem], res[n_sem:n_sem + n_arr], res[n_sem + n_arr:n_sem + 2 * n_arr], res[-1]


def _wait_exchanges(kinds, sems, arrays, lands, after, name, keep_sources=False):
    n_arr, n_sem = len(arrays), len(sems)

    def body(*refs):
        srcs, dsts = refs[:n_arr], refs[n_arr:2 * n_arr]
        sem_refs = refs[2 * n_arr:2 * n_arr + n_sem]
        at = 0
        for r, (kind, n) in enumerate(kinds):
            for cp in _EXCHANGES[kind][1](srcs[at:at + n], dsts[at:at + n], sem_refs[2 * r], sem_refs[2 * r + 1]):
                cp.wait()
            at += n

    hbm = [pltpu.HBM(a.shape, a.dtype) for a in (*arrays, *lands)]
    return pl.pallas_call(
        body, name=name, out_shape=tuple(hbm),
        in_specs=[_HBM] * (2 * n_arr) + [_SEM] * n_sem + [_ANY], out_specs=tuple([_HBM] * (2 * n_arr)),
        input_output_aliases={i: i for i in range(2 * n_arr)},
        compiler_params=pltpu.CompilerParams(has_side_effects=_FLOWS),
    )(*arrays, *lands, *sems, after)[0 if keep_sources else n_arr:]


def _chip_partial(pos, mine, theirs, relations, out_dtype, row_tile, name):
    n_slots, m, n = mine.shape
    q0 = relations[0]

    def chip_of(qi, pos_ref):
        q = qi + q0
        return pos_ref[0] ^ (q >> 1), pos_ref[1] ^ (q & 1)

    def mine_map(qi, t, pos_ref):
        px, py = chip_of(qi, pos_ref)
        return (4 * px + 2 * py + pos_ref[2] if n_slots == N_DEV else 2 * px + py), t, 0

    def theirs_map(qi, t, pos_ref):
        px, py = chip_of(qi, pos_ref)
        return 2 * px + py, t, 0

    def body(pos_ref, a_ref, b_ref, o_ref):
        o_ref[...] = (a_ref[...] + b_ref[...]).astype(out_dtype)

    return pl.pallas_call(
        body, name=name,
        out_shape=jax.ShapeDtypeStruct((len(relations), m, n), out_dtype),
        grid_spec=pltpu.PrefetchScalarGridSpec(
            num_scalar_prefetch=1, grid=(len(relations), m // row_tile),
            in_specs=[pl.BlockSpec((None, row_tile, n), mine_map), pl.BlockSpec((None, row_tile, n), theirs_map)],
            out_specs=pl.BlockSpec((None, row_tile, n), lambda qi, t, pos_ref: (qi, t, 0))),
        compiler_params=pltpu.CompilerParams(dimension_semantics=("arbitrary", "arbitrary")),
    )(pos, mine, theirs)


PARTS = ((0, 512), (512, 640))


def _gather_norm_proj(pos, x2d, small_shard, norm_g, w_in_shard, w_out_shards, n_chunk):
    seq = x2d.shape[0]
    n_tiles = seq // TILE + 1
    tp = n_tiles * TILE
    n_parts = len(PARTS)
    widest = max(width for _, width in PARTS)
    units = [(s, u) for s in range(2) for u in range(n_parts)]
    units += [(s, u) for u in range(n_parts) for s in (2, 3, 5, 6)] + [(s, u) for u in range(n_parts) for s in (4, 7)]
    n_units = len(units)
    over_ici = [m for m, (s, _) in enumerate(units) if s in (2, 3)]
    handled_at = {m: m - 1 for m in range(1, n_units)}
    handled_at.update({m: over_ici[0] - 1 + i for i, m in enumerate(over_ici)})
    assert all(step < m for m, step in handled_at.items())
    n_steps = n_tiles + n_units
    chunk = tp // n_chunk

    def body(pos_ref, x_ref, g_ref, small_ref, win_ref, wa_ref, wb_ref, wo_ref,
             ht_ref, proj_ref, meta_ref, small_all, win_all, wa_all, wb_all, wo_all,
             h_all, wbuf, rbuf, small_buf, send_sems, recv_sems, local_sems, small_send, small_recv):
        g = pl.program_id(0)
        x, y, c = lax.axis_index("x"), lax.axis_index("y"), lax.axis_index("c")
        me, sibling = (x, y, c), (x, y, 1 - c)
        chips = [(1 - x, y), (x, 1 - y), (1 - x, 1 - y)]
        shards = (win_ref, wa_ref, wb_ref, wo_ref)
        gathered = (win_all, wa_all, wb_all, wo_all)
        n_arrays = len(shards)
        blocks = [me, sibling] + [(*chip, c) for chip in chips] + [(*chip, 1 - c) for chip in chips]

        def index(block):
            px, py, pc = block
            return 4 * px + 2 * py + pc

        def part(ref, a, u):
            return ref.at[:, pl.ds(PARTS[u][0], PARTS[u][1])] if a == 0 else ref

        def slot(a, block, u):
            return part(gathered[a].at[index(block)], a, u)

        def sem(a, k, u):
            return n_parts * k + u if a == 0 else 7 * n_parts + 7 * (a - 1) + k

        def copy(a, k, block, to, u=0, from_shard=False):
            return pltpu.make_async_remote_copy(
                src_ref=part(shards[a], a, u) if from_shard else slot(a, block, u), dst_ref=slot(a, block, u),
                send_sem=send_sems.at[sem(a, k, u)], recv_sem=recv_sems.at[sem(a, k, u)],
                device_id=to, device_id_type=MESH)

        def keep(a):
            return pltpu.make_async_copy(shards[a], gathered[a].at[index(me)], local_sems.at[a])

        def load(m):
            s, u = units[m]
            src = part(win_ref, 0, u) if s == 0 else slot(0, blocks[s], u)
            return pltpu.make_async_copy(src, wbuf.at[m % 2, :, 0:PARTS[u][1]], local_sems.at[n_arrays + m % 2])

        def store(m):
            s, u = units[m]
            col0 = pl.multiple_of(index(blocks[s]) * COLS + PARTS[u][0], LANES)
            return pltpu.make_async_copy(rbuf.at[m % 2, :, 0:PARTS[u][1]],
                                         proj_ref.at[:, pl.ds(col0, PARTS[u][1])], local_sems.at[n_arrays + 2 + m % 2])

        def by_x(a, u):
            return u == 0 if a == 0 else a < 3

        def relay(a, u=0):
            src, to = (blocks[3], blocks[2]) if by_x(a, u) else (blocks[2], blocks[3])
            return copy(a, 3, src, to, u)

        def arrive(m):
            s, u = units[m]
            if s == 1:
                copy(0, 0, sibling, me, u).wait_recv()
            elif 2 <= s <= 4:
                copy(0, s - 1, blocks[s], me, u).wait_recv()
                copy(0, s + 2, blocks[s], sibling, u).start()
                if s < 4 and by_x(0, u) == (s == 3):
                    relay(0, u).start()
            elif s >= 5:
                copy(0, s - 1, blocks[s], me, u).wait_recv()

        def pass_on_out(j, arrays=(1, 2, 3)):
            for a in arrays:
                copy(a, j + 1, blocks[2 + j], me).wait_recv()
                copy(a, j + 4, blocks[2 + j], sibling).start()
                if j < 2 and by_x(a, 0) == (j == 1):
                    relay(a).start()

        to_relay = [a for a in range(1, 4) if by_x(a, 0)]
        targets = [sibling, blocks[2], blocks[3]]
        relays_started = max(handled_at[m] for m, (s, u) in enumerate(units) if s in (2, 3) and by_x(0, u) == (s == 3))

        def small_copies():
            return _all_copies([small_ref], [small_all], small_send, small_recv)

        @pl.when(g == 0)
        def _():
            for cp in small_copies():
                cp.start()
            for a in range(n_arrays):
                keep(a).start()
            for u in range(n_parts):
                for k, to in enumerate(targets):
                    copy(0, k, me, to, u, from_shard=True).start()
            for a in range(1, 4):
                copy(a, 0, me, sibling, from_shard=True).start()
            load(0).start()

        @pl.when(g == n_tiles - 2)
        def _():
            for cp in small_copies():
                cp.wait()
            fetch = pltpu.make_async_copy(small_all, small_buf, local_sems.at[n_arrays + 4])
            fetch.start()
            fetch.wait()
            meta_ref[0:TILE - N_META, :] = jnp.zeros((TILE - N_META, D_MODEL), F32)
            meta_ref[TILE - N_META:TILE, :] = jnp.concatenate([small_buf[d, 0:N_META, :] for d in range(N_DEV)], axis=1)

        @pl.when(g < n_tiles)
        def _():
            s0 = jnp.where(g == n_tiles - 1, meta_ref[...], x_ref[...])
            r = lax.rsqrt(_rowmean(s0 * s0) + EPS)
            h32 = (s0 * r) * g_ref[...]
            ht_ref[...] = h32.T.astype(BF16)
            h_all[pl.ds(pl.multiple_of(g * TILE, TILE), TILE), :] = h32.astype(BF16)

        for m in range(n_units):
            @pl.when(g == n_tiles + m)
            def _(m=m):
                load(m).wait()
                for later in range(m + 1, n_units):
                    if handled_at[later] == m:
                        arrive(later)
                if m + 1 < n_units:
                    load(m + 1).start()
                if m == relays_started:
                    for a in range(1, 4):
                        for k in (1, 2):
                            copy(a, k, me, targets[k], from_shard=True).start()
                if m == n_units - 1:
                    pass_on_out(0)
                    pass_on_out(1, to_relay)
                if m >= 2:
                    store(m - 2).wait()

        m_now = jnp.maximum(g - n_tiles, 0)
        u_now = functools.reduce(jnp.add, [jnp.where(m_now == m, u, 0) for m, (_, u) in enumerate(units)])
        for u, (_, width) in enumerate(PARTS):
            @pl.when((g >= n_tiles) & (u_now == u))
            def _(width=width):
                w = wbuf[m_now % 2, :, 0:width]
                for r in range(n_chunk):
                    rbuf[m_now % 2, r * chunk:(r + 1) * chunk, 0:width] = _dot(h_all[r * chunk:(r + 1) * chunk, :], w)

        for m in range(n_units):
            @pl.when(g == n_tiles + m)
            def _(m=m):
                store(m).start()

        @pl.when(g == n_steps - 1)
        def _():
            pass_on_out(1, [a for a in range(1, 4) if a not in to_relay])
            pass_on_out(2)
            store(n_units - 2).wait()
            store(n_units - 1).wait()
            for a in range(1, 4):
                copy(a, 0, sibling, me).wait_recv()
                for j in range(3):
                    copy(a, 4 + j, blocks[5 + j], me).wait_recv()
            for a in range(n_arrays):
                for u in range(n_parts if a == 0 else 1):
                    for k, to in enumerate(targets):
                        copy(a, k, me, to, u, from_shard=True).wait_send()
                    relay(a, u).wait_send()
                    for j in range(3):
                        copy(a, 4 + j, blocks[2 + j], sibling, u).wait_send()
                keep(a).wait()

    n_x = n_tiles - 1
    return pl.pallas_call(
        body, name="gather_norm_proj",
        out_shape=[jax.ShapeDtypeStruct((D_MODEL, tp), BF16), jax.ShapeDtypeStruct((tp, D_IN), F32),
                   jax.ShapeDtypeStruct((TILE, D_MODEL), F32), jax.ShapeDtypeStruct((N_DEV,) + small_shard.shape[1:], F32),
                   jax.ShapeDtypeStruct((N_DEV,) + w_in_shard.shape, BF16)]
                  + [jax.ShapeDtypeStruct((N_DEV,) + w.shape, BF16) for w in w_out_shards],
        grid_spec=pltpu.PrefetchScalarGridSpec(
            num_scalar_prefetch=1, grid=(n_steps,),
            in_specs=[pl.BlockSpec((TILE, D_MODEL), lambda g, pos_ref: (jnp.minimum(g, n_x - 1), 0)),
                      _VMEM, _ANY, _ANY, _ANY, _ANY, _ANY],
            out_specs=[pl.BlockSpec((D_MODEL, TILE), lambda g, pos_ref: (0, jnp.minimum(g, n_tiles - 1))),
                       _ANY, _VMEM, _ANY, _ANY, _ANY, _ANY, _ANY],
            scratch_shapes=[pltpu.VMEM((tp, D_MODEL), BF16), pltpu.VMEM((2, D_MODEL, widest), BF16),
                            pltpu.VMEM((2, tp, widest), F32), pltpu.VMEM((N_DEV,) + small_shard.shape[1:], F32),
                            pltpu.SemaphoreType.DMA((7 * n_parts + 21,)), pltpu.SemaphoreType.DMA((7 * n_parts + 21,)),
                            pltpu.SemaphoreType.DMA((9,)),
                            pltpu.SemaphoreType.DMA((N_DEV,)), pltpu.SemaphoreType.DMA((N_DEV,))]),
        compiler_params=pltpu.CompilerParams(dimension_semantics=("arbitrary",), vmem_limit_bytes=VMEM_LIMIT),
    )(pos, x2d, norm_g, small_shard, w_in_shard, *w_out_shards)


C_AVAL, C_AGLU, C_AZ, C_BB, C_BC, C_BX, C_BZ, C_GA, C_GB = (k * D_MODEL for k in range(9))
S_AZ, S_BB, S_BZ, S_GA, S_GB = (k * D_MODEL for k in range(5))


def _fused_pass(proj, x2d, tgt2d, meta_tile, conv_a_w, conv_a_b, ln_a_g, ln_a_b, b_a_out, conv_b_w, final_g,
                w_a, w_b, w_o, w_a_t, w_b_t, w_o_t, n_tiles):
    T = TILE
    tp = n_tiles * T
    inv_d = 1.0 / D_MODEL

    def block_of(tile):
        return jnp.where(tile == 0, n_tiles - 1, tile - 1)

    def cur(i):
        return block_of(jnp.minimum(i, n_tiles - 1))

    def prev(i):
        return block_of(jnp.clip(i - 1, 0, n_tiles - 1))

    def xblk(i):
        return jnp.maximum(jnp.minimum(i, n_tiles - 1) - 1, 0)

    def body(proj_ref, aprev, cprev, x_ref, tgt_ref, meta_ref, caw_ref, cab_ref, lng_ref, lnb_ref, bao_ref, cbw_ref,
             fg_ref, wa_ref, wb_ref, wo_ref, wat_ref, wbt_ref, wot_ref,
             dproj_ref, ds1_ref, lhs_ref, rhs_ref, small_ref,
             ua0_buf, cb_buf, dua1_buf, dc3_buf, stage, ua1_buf, c3_buf,
             dpa_buf, dpb_buf, dcaw8, dcbw8, shift_buf):
        i = pl.program_id(0)
        this, before = i % 2, 1 - i % 2

        @pl.when(i == 0)
        def _init():
            for buf in (ua0_buf, cb_buf, dua1_buf, dc3_buf, dcaw8, dcbw8):
                buf[...] = jnp.zeros(buf.shape, buf.dtype)
            small_ref[...] = jnp.zeros(small_ref.shape, F32)

        @pl.when(i >= 1)
        def _emit_stage():
            dproj_ref[:, C_AZ:C_BC] = stage[:, S_AZ:S_BZ]
            dproj_ref[:, C_BZ:D_IN] = stage[:, S_BZ:S_GB + D_MODEL]

        @pl.when(i < n_tiles)
        def _front():
            def conv_chunk(cc, carry):
                c0 = pl.multiple_of(cc * LANES, LANES)
                lanes = pl.ds(c0, LANES)

                def col(base):
                    return pl.ds(pl.multiple_of(base + cc * LANES, LANES), LANES)

                ua0 = proj_ref[:, col(C_AVAL)] * _sigmoid(proj_ref[:, col(C_AGLU)])
                ua0_buf[this, 0:HALO, lanes] = ua0_buf[before, T:T + HALO, lanes]
                ua0_buf[this, HALO:HALO + T, lanes] = ua0
                acc = jnp.broadcast_to(cab_ref[:, lanes], (T, LANES))
                lead = HALO - (CONV_A - 1)
                for r in range(SUBLANES):
                    taps = [k for k in range(CONV_A) if (k + lead) % SUBLANES == r]
                    rows = T + SUBLANES * max((k + lead) // SUBLANES for k in taps)
                    if r:
                        shift_buf[r, 0:rows, :] = ua0_buf[this, pl.ds(r, rows), lanes]
                    for k in taps:
                        q = (k + lead) // SUBLANES
                        if r:
                            win = shift_buf[r, SUBLANES * q:SUBLANES * q + T, :]
                        else:
                            win = ua0_buf[this, pl.ds(SUBLANES * q, T), lanes]
                        acc = acc + caw_ref[k:k + 1, lanes] * win
                ua1_buf[:, lanes] = acc
                cb = proj_ref[:, col(C_BC)] * proj_ref[:, col(C_BX)]
                cb_buf[this, 0:SUBLANES, lanes] = cb_buf[before, T:T + SUBLANES, lanes]
                cb_buf[this, SUBLANES:SUBLANES + T, lanes] = cb
                lead_b = SUBLANES - (CONV_B - 1)
                acc3 = cbw_ref[0:1, lanes] * cb_buf[this, pl.ds(lead_b, T), lanes]
                for k in range(1, CONV_B):
                    acc3 = acc3 + cbw_ref[k:k + 1, lanes] * cb_buf[this, pl.ds(lead_b + k, T), lanes]
                c3_buf[:, lanes] = acc3
                return carry

            lax.fori_loop(0, N_CHUNK, conv_chunk, 0)

            ua1 = ua1_buf[...]
            xc = ua1 - _rowmean(ua1)
            rstd = lax.rsqrt(_rowmean(xc * xc) + EPS)
            xhat = xc * rstd
            ua2 = xhat * lng_ref[...] + lnb_ref[...]
            sg2 = _sigmoid(ua2)
            ua3 = ua2 * sg2
            a_z = proj_ref[:, C_AZ:C_AZ + D_MODEL]
            sz = _sigmoid(a_z)
            silu_az = a_z * sz
            lhs_ref[0] = (ua3 * silu_az).astype(BF16)
            b_z = proj_ref[:, C_BZ:C_BZ + D_MODEL]
            sbz = _sigmoid(b_z)
            silu_bz = b_z * sbz
            b_b = proj_ref[:, C_BB:C_BB + D_MODEL]
            c3 = c3_buf[...]
            ub = b_b * c3
            lhs_ref[1] = (ub * silu_bz).astype(BF16)

            ya = _dot(lhs_ref[0], wa_ref[...]) + bao_ref[...]
            yb = _dot(lhs_ref[1], wb_ref[...])
            sga = _sigmoid(proj_ref[:, C_GA:C_GA + D_MODEL])
            sgb = _sigmoid(proj_ref[:, C_GB:C_GB + D_MODEL])
            m_b = (sga * ya + sgb * yb).astype(BF16)
            lhs_ref[2] = m_b
            s0 = jnp.where(i == 0, meta_ref[...], x_ref[...])
            s1 = s0 + _dot(m_b, wo_ref[...])
            r1 = lax.rsqrt(_rowmean(s1 * s1) + EPS)
            y = (s1 * r1) * fg_ref[...]
            is_token = (i >= 1).astype(F32)
            err = (y - tgt_ref[...]) * is_token
            small_ref[ROW_LOSS:ROW_LOSS + 1, :] += (0.5 * inv_d) * _colsum(err * err)
            dy = err * inv_d
            small_ref[ROW_FINAL_G:ROW_FINAL_G + 1, :] += _colsum(dy * (s1 * r1))
            gy = dy * fg_ref[...]
            ds1 = r1 * gy - s1 * ((r1 * r1 * r1) * _rowmean(gy * s1))
            ds1_ref[...] = ds1
            ds1_b = ds1.astype(BF16)
            rhs_ref[2] = ds1_b
            dm = _dot(ds1_b, wot_ref[...])
            dya = dm * sga
            dyb = dm * sgb
            stage[:, S_GA:S_GA + D_MODEL] = (dya * ya * (1.0 - sga)).astype(BF16)
            stage[:, S_GB:S_GB + D_MODEL] = (dyb * yb * (1.0 - sgb)).astype(BF16)
            small_ref[ROW_B_A_OUT:ROW_B_A_OUT + 1, :] += _colsum(dya)
            dya_b = dya.astype(BF16)
            dyb_b = dyb.astype(BF16)
            rhs_ref[0] = dya_b
            rhs_ref[1] = dyb_b
            dpa_buf[...] = _dot(dya_b, wat_ref[...])
            dpb_buf[...] = _dot(dyb_b, wbt_ref[...])

            dpa = dpa_buf[...]
            stage[:, S_AZ:S_AZ + D_MODEL] = (dpa * ua3 * (sz + silu_az * (1.0 - sz))).astype(BF16)
            dua2 = dpa * silu_az * (sg2 + ua3 * (1.0 - sg2))
            small_ref[ROW_LN_G:ROW_LN_G + 1, :] += _colsum(dua2 * xhat)
            small_ref[ROW_LN_B:ROW_LN_B + 1, :] += _colsum(dua2)
            dxh = dua2 * lng_ref[...]
            dua1 = rstd * (dxh - _rowmean(dxh) - xhat * _rowmean(dxh * xhat))
            small_ref[ROW_CONV_A_B:ROW_CONV_A_B + 1, :] += _colsum(dua1)
            dua1_buf[this, 0:T, :] = dua1
            dua1_buf[before, T:T + HALO, :] = dua1[0:HALO]
            dpb = dpb_buf[...]
            stage[:, S_BZ:S_BZ + D_MODEL] = (dpb * ub * (sbz + silu_bz * (1.0 - sbz))).astype(BF16)
            dub = dpb * silu_bz
            stage[:, S_BB:S_BB + D_MODEL] = (dub * c3).astype(BF16)
            dc3 = dub * b_b
            dc3_buf[this, 0:T, :] = dc3
            dc3_buf[before, T:T + SUBLANES, :] = dc3[0:SUBLANES]

        @pl.when(i == n_tiles)
        def _no_later_tile():
            dua1_buf[before, T:T + HALO, :] = jnp.zeros((HALO, D_MODEL), F32)
            dc3_buf[before, T:T + SUBLANES, :] = jnp.zeros((SUBLANES, D_MODEL), F32)

        @pl.when(i >= 1)
        def _lagged():
            def convt_chunk(cc, carry):
                c0 = pl.multiple_of(cc * LANES, LANES)
                lanes = pl.ds(c0, LANES)

                def col(base):
                    return pl.ds(pl.multiple_of(base + cc * LANES, LANES), LANES)

                ua0 = ua0_buf[before, HALO:HALO + T, lanes]
                acc = jnp.zeros((T, LANES), F32)
                for r in range(SUBLANES):
                    shifts = [j for j in range(CONV_A) if j % SUBLANES == r]
                    rows = T + shifts[-1] - r
                    if r:
                        shift_buf[r, 0:rows, :] = dua1_buf[before, pl.ds(r, rows), lanes]
                    for j in shifts:
                        k = CONV_A - 1 - j
                        if r:
                            later = shift_buf[r, j - r:j - r + T, :]
                        else:
                            later = dua1_buf[before, pl.ds(j, T), lanes]
                        acc = acc + caw_ref[k:k + 1, lanes] * later
                        dcaw8[SUBLANES * k:SUBLANES * (k + 1), lanes] += _fold8(ua0 * later)
                a_val = aprev[:, col(0)]
                sg = _sigmoid(aprev[:, col(D_MODEL)])
                dproj_ref[:, col(C_AVAL)] = (acc * sg).astype(BF16)
                dproj_ref[:, col(C_AGLU)] = (acc * a_val * (sg * (1.0 - sg))).astype(BF16)

                cb = cb_buf[before, SUBLANES:SUBLANES + T, lanes]
                acc3 = jnp.zeros((T, LANES), F32)
                for j in range(CONV_B):
                    k = CONV_B - 1 - j
                    later = dc3_buf[before, pl.ds(j, T), lanes]
                    acc3 = acc3 + cbw_ref[k:k + 1, lanes] * later
                    dcbw8[SUBLANES * k:SUBLANES * (k + 1), lanes] += _fold8(cb * later)
                dproj_ref[:, col(C_BC)] = (acc3 * cprev[:, col(D_MODEL)]).astype(BF16)
                dproj_ref[:, col(C_BX)] = (acc3 * cprev[:, col(0)]).astype(BF16)
                return carry

            lax.fori_loop(0, N_CHUNK, convt_chunk, 0)

        @pl.when(i == n_tiles)
        def _finish():
            for k in range(CONV_A):
                small_ref[ROW_CONV_A_W + k:ROW_CONV_A_W + k + 1, :] = _colsum(dcaw8[SUBLANES * k:SUBLANES * (k + 1), :])
            for k in range(CONV_B):
                small_ref[ROW_CONV_B_W + k:ROW_CONV_B_W + k + 1, :] = _colsum(dcbw8[SUBLANES * k:SUBLANES * (k + 1), :])

    pair = 2 * D_MODEL
    return pl.pallas_call(
        body, name="fused_pass", grid=(n_tiles + 1,),
        out_shape=[
            jax.ShapeDtypeStruct((tp, D_IN), BF16),
            jax.ShapeDtypeStruct((tp, D_MODEL), F32),
            jax.ShapeDtypeStruct((3, tp, D_MODEL), BF16),
            jax.ShapeDtypeStruct((3, tp, D_MODEL), BF16),
            jax.ShapeDtypeStruct((SMALL_A_ROWS, D_MODEL), F32),
        ],
        in_specs=[
            pl.BlockSpec((T, D_IN), lambda i: (cur(i), 0)),
            pl.BlockSpec((T, pair), lambda i: (prev(i), C_AVAL // pair)),
            pl.BlockSpec((T, pair), lambda i: (prev(i), C_BC // pair)),
            pl.BlockSpec((T, D_MODEL), lambda i: (xblk(i), 0)),
            pl.BlockSpec((T, D_MODEL), lambda i: (xblk(i), 0)),
            _VMEM, _VMEM, _VMEM, _VMEM, _VMEM, _VMEM, _VMEM, _VMEM,
            *[_resident((D_MODEL, D_MODEL)) for _ in range(6)],
        ],
        out_specs=[
            pl.BlockSpec((T, D_IN), lambda i: (prev(i), 0)),
            pl.BlockSpec((T, D_MODEL), lambda i: (cur(i), 0)),
            pl.BlockSpec((3, T, D_MODEL), lambda i: (0, cur(i), 0)),
            pl.BlockSpec((3, T, D_MODEL), lambda i: (0, cur(i), 0)),
            _VMEM,
        ],
        scratch_shapes=[
            pltpu.VMEM((2, HALO + T, D_MODEL), F32),
            pltpu.VMEM((2, SUBLANES + T, D_MODEL), F32),
            pltpu.VMEM((2, T + HALO, D_MODEL), F32),
            pltpu.VMEM((2, T + SUBLANES, D_MODEL), F32),
            pltpu.VMEM((T, 5 * D_MODEL), BF16),
            pltpu.VMEM((T, D_MODEL), F32),
            pltpu.VMEM((T, D_MODEL), F32),
            pltpu.VMEM((T, D_MODEL), F32),
            pltpu.VMEM((T, D_MODEL), F32),
            pltpu.VMEM((32 * SUBLANES, D_MODEL), F32),
            pltpu.VMEM((SUBLANES * SUBLANES, D_MODEL), F32),
            pltpu.VMEM((SUBLANES, T + HALO, LANES), F32),
        ],
        compiler_params=pltpu.CompilerParams(dimension_semantics=("arbitrary",), vmem_limit_bytes=VMEM_LIMIT),
    )(proj, proj, proj, x2d, tgt2d, meta_tile, conv_a_w, conv_a_b, ln_a_g, ln_a_b, b_a_out, conv_b_w, final_g,
      w_a, w_b, w_o, w_a_t, w_b_t, w_o_t)


def _input_bwd(dproj, ds1, x2d, meta_tile, norm_g, w_in_all, row_tile):
    seq = x2d.shape[0]
    n_steps = seq // row_tile
    meta_block = seq // TILE

    def backward(dp_ref, ds1_ref, s0_ref, g_ref, w_ref, out_ref, vec_ref):
        dh = _dot_nt(dp_ref[:, 0:COLS], w_ref[0])
        for j in range(1, N_DEV):
            dh = dh + _dot_nt(dp_ref[:, j * COLS:(j + 1) * COLS], w_ref[j])
        s0v = s0_ref[...]
        r = lax.rsqrt(_rowmean(s0v * s0v) + EPS)
        gh = dh * g_ref[...]
        out_ref[...] = ds1_ref[...] + r * gh - s0v * ((r * r * r) * _rowmean(gh * s0v))
        vec_ref[ROW_NORM_G:ROW_NORM_G + 1, :] += _colsum(dh * (s0v * r))

    def body(dp_ref, ds1_ref, x_ref, dpm_ref, ds1m_ref, meta_ref, g_ref, w_ref, gx_ref, small_ref, gmeta_buf):
        t = pl.program_id(0)

        @pl.when(t == 0)
        def _():
            small_ref[...] = jnp.zeros(small_ref.shape, F32)

        backward(dp_ref, ds1_ref, x_ref, g_ref, w_ref, gx_ref, small_ref)

        @pl.when(t == n_steps - 1)
        def _():
            backward(dpm_ref, ds1m_ref, meta_ref, g_ref, w_ref, gmeta_buf, small_ref)
            small_ref[ROW_META:ROW_META + N_META, :] = gmeta_buf[TILE - N_META:TILE, :]

    return pl.pallas_call(
        body, name="input_bwd", grid=(n_steps,),
        out_shape=[jax.ShapeDtypeStruct(x2d.shape, F32), jax.ShapeDtypeStruct((SMALL_B_ROWS, D_MODEL), F32)],
        in_specs=[pl.BlockSpec((row_tile, D_IN), lambda t: (t, 0)),
                  pl.BlockSpec((row_tile, D_MODEL), lambda t: (t, 0)),
                  pl.BlockSpec((row_tile, D_MODEL), lambda t: (t, 0)),
                  pl.BlockSpec((TILE, D_IN), lambda t: (meta_block, 0)),
                  pl.BlockSpec((TILE, D_MODEL), lambda t: (meta_block, 0)),
                  _VMEM, _VMEM, _resident((N_DEV, D_MODEL, COLS))],
        out_specs=[pl.BlockSpec((row_tile, D_MODEL), lambda t: (t, 0)), _VMEM],
        scratch_shapes=[pltpu.VMEM((TILE, D_MODEL), F32)],
        compiler_params=pltpu.CompilerParams(dimension_semantics=("arbitrary",), vmem_limit_bytes=VMEM_LIMIT),
    )(dproj, ds1, x2d, dproj, ds1, meta_tile, norm_g, w_in_all)


def _grad_w_in_half(pos, h_t, dproj, k_tile, other_side, rides, name, after=None, add_to=None, narrow=False):
    tp = h_t.shape[1]
    n_k = tp // k_tile
    order = [] if after is None else [after]
    summing = add_to is not None
    assert not (summing and narrow)

    def column_block(q, k, pos_ref):
        return k, 2 * q + (1 - pos_ref[2] if other_side else pos_ref[2])

    def body(pos_ref, h_ref, dp_ref, *refs):
        acc = refs[-1]

        @pl.when(pl.program_id(1) == 0)
        def _():
            acc[...] = refs[0][...].astype(F32) if summing else jnp.zeros(acc.shape, F32)

        acc[...] += _dot(h_ref[...], dp_ref[...])

        if summing or narrow:
            @pl.when(pl.program_id(1) == n_k - 1)
            def _():
                refs[-2][...] = acc[...].astype(BF16)

        if summing:
            @pl.when((pl.program_id(1) == n_k - 1) & (pl.program_id(0) == 2 * pos_ref[0] + pos_ref[1]))
            def _():
                refs[-3][...] = acc[...]

    ride = [a for _, arrays in rides for a in arrays]
    n_arr = len(ride)
    ride_shapes, ride_sems = _ride_shapes(rides)
    block = (None, D_MODEL, COLS)
    extra_in = [add_to] if summing else []
    extra_in_specs = [pl.BlockSpec(block, lambda q, k, pos_ref: (q, 0, 0))] if summing else []
    extra_out = [jax.ShapeDtypeStruct((4, D_MODEL, COLS), BF16)] if summing else []
    extra_out_specs = [pl.BlockSpec(block, lambda q, k, pos_ref: (q ^ (2 * pos_ref[0] + pos_ref[1]), 0, 0))] \
        if summing else []
    body = _riding(body, 3 + len(extra_in) + len(order), 1 + len(extra_out), rides,
                   lambda: (pl.program_id(0) == 0) & (pl.program_id(1) == 0),
                   lambda: (pl.program_id(0) == 3) & (pl.program_id(1) == n_k - 1))
    return pl.pallas_call(
        body, name=name,
        out_shape=[jax.ShapeDtypeStruct((1 if summing else 4, D_MODEL, COLS), BF16 if narrow else F32)]
        + extra_out + ride_shapes,
        grid_spec=pltpu.PrefetchScalarGridSpec(
            num_scalar_prefetch=1, grid=(4, n_k),
            in_specs=[pl.BlockSpec((D_MODEL, k_tile), lambda q, k, pos_ref: (0, k)),
                      pl.BlockSpec((k_tile, COLS), column_block)] + extra_in_specs + [_ANY] * (len(order) + n_arr),
            out_specs=[pl.BlockSpec(block, lambda q, k, pos_ref: (0 if summing else q, 0, 0))]
            + extra_out_specs + [_ANY] * n_arr,
            scratch_shapes=([pltpu.VMEM((D_MODEL, COLS), F32)] if summing or narrow else []) + ride_sems),
        compiler_params=pltpu.CompilerParams(dimension_semantics=("arbitrary", "arbitrary"),
                                             vmem_limit_bytes=VMEM_LIMIT),
    )(pos, h_t, dproj, *extra_in, *order, *ride)


def _grad_w_out(lhs, rhs, k_tile, after):
    tp = lhs.shape[1]

    def body(a_ref, b_ref, after_ref, o_ref):
        @pl.when(pl.program_id(1) == 0)
        def _():
            o_ref[...] = jnp.zeros(o_ref.shape, F32)

        o_ref[...] += _dot_tn(a_ref[...], b_ref[...]).reshape(N_DEV, ROWS_OUT, D_MODEL)

    return pl.pallas_call(
        body, name="grad_w_out", grid=(3, tp // k_tile),
        out_shape=jax.ShapeDtypeStruct((N_DEV, 3, ROWS_OUT, D_MODEL), F32),
        in_specs=[pl.BlockSpec((None, k_tile, D_MODEL), lambda w, k: (w, k, 0)),
                  pl.BlockSpec((None, k_tile, D_MODEL), lambda w, k: (w, k, 0)), _ANY],
        out_specs=pl.BlockSpec((N_DEV, None, ROWS_OUT, D_MODEL), lambda w, k: (0, w, 0, 0)),
        compiler_params=pltpu.CompilerParams(dimension_semantics=("arbitrary", "arbitrary"),
                                             vmem_limit_bytes=VMEM_LIMIT),
    )(lhs, rhs, after)


def _adamw_math(w, g, m, v):
    m = ADAM_B1 * m + (1.0 - ADAM_B1) * g
    v = ADAM_B2 * v + (1.0 - ADAM_B2) * (g * g)
    m_hat = m / (1.0 - ADAM_B1 ** ADAM_STEP)
    v_hat = v / (1.0 - ADAM_B2 ** ADAM_STEP)
    delta = -ADAM_LR * (m_hat / (jnp.sqrt(v_hat) + ADAM_EPS) + ADAM_WD * w)
    return delta, m, v


def _adamw_sharded(pos, mine, theirs, landed, weights, row_tile, name, after=None):
    order = [] if after is None else [after]
    rows, n = weights[0][0].shape
    n_slots = mine.shape[0]
    per_shard = rows // row_tile
    assert per_shard == 1 or len(weights) == 1

    def mine_map(j, t, pos_ref):
        chip = 2 * pos_ref[0] + pos_ref[1]
        return {N_DEV: 2 * chip + pos_ref[2], 4: chip, 1: 0}[n_slots], j * per_shard + t, 0

    def theirs_map(j, t, pos_ref):
        return 2 * pos_ref[0] + pos_ref[1], j * per_shard + t, 0

    def body(pos_ref, mine_ref, *refs):
        if theirs is not None:
            g = mine_ref[...] + refs[0][...]
            refs = refs[1:]
        else:
            g = mine_ref[...]
        land_ref, refs = refs[0], refs[1:]
        ins, outs = refs[:3 * len(weights)], refs[3 * len(weights) + len(order):]
        for k in range(3):
            g = g + land_ref[k].astype(F32)
        for j in range(len(weights)):
            @pl.when(pl.program_id(0) == j)
            def _(j=j):
                w_ref, m_ref, v_ref = ins[3 * j:3 * j + 3]
                delta, m_new, v_new = _adamw_math(w_ref[...], g, m_ref[...], v_ref[...])
                for ref, val in zip(outs[4 * j:4 * j + 4], (g, delta, m_new, v_new)):
                    ref[...] = val

    tile = pl.BlockSpec((row_tile, n), lambda j, t, pos_ref: (t, 0))
    res = pl.pallas_call(
        body, name=name,
        out_shape=[jax.ShapeDtypeStruct((rows, n), F32)] * (4 * len(weights)),
        grid_spec=pltpu.PrefetchScalarGridSpec(
            num_scalar_prefetch=1, grid=(len(weights), per_shard),
            in_specs=[pl.BlockSpec((None, row_tile, n), mine_map)]
            + ([pl.BlockSpec((None, row_tile, n), theirs_map)] if theirs is not None else [])
            + [pl.BlockSpec((3, row_tile, n), lambda j, t, pos_ref: (0, j * per_shard + t, 0))]
            + [tile] * (3 * len(weights)) + [_ANY] * len(order),
            out_specs=[tile] * (4 * len(weights))),
        compiler_params=pltpu.CompilerParams(dimension_semantics=("arbitrary", "arbitrary")),
    )(pos, mine, *([theirs] if theirs is not None else []), landed, *[a for wmv in weights for a in wmv], *order)
    return [res[4 * j:4 * j + 4] for j in range(len(weights))]


def _adamw_small(gathered, gathered_cols, params):
    n_par, n_src = len(params), len(gathered)

    def body(*refs):
        g_refs, gc_refs = refs[:n_src], refs[n_src:2 * n_src]
        ins = refs[2 * n_src:2 * n_src + 3 * n_par]
        outs = refs[2 * n_src + 3 * n_par:]
        loss_ref = outs[4 * n_par]

        def reduced(ref, row, n_rows):
            g = ref[0, row:row + n_rows, :]
            for d in range(1, N_DEV):
                g = g + ref[d, row:row + n_rows, :]
            return g

        for p, (src, row, n_rows, sharded, _, _, _) in enumerate(params):
            g = reduced((gc_refs if sharded else g_refs)[src], row, n_rows)
            w_ref, m_ref, v_ref = ins[3 * p:3 * p + 3]
            delta, m_new, v_new = _adamw_math(w_ref[...], g, m_ref[...], v_ref[...])
            outs[4 * p][...] = g
            outs[4 * p + 1][...] = delta
            outs[4 * p + 2][...] = m_new
            outs[4 * p + 3][...] = v_new
        loss = jnp.sum(reduced(g_refs[0], ROW_LOSS, 1), axis=1, keepdims=True)
        loss_ref[...] = jnp.broadcast_to(loss, loss_ref.shape)

    out_shape = []
    for (_, _, _, _, w, _, _) in params:
        out_shape += [jax.ShapeDtypeStruct(w.shape, F32)] * 4
    out_shape.append(jax.ShapeDtypeStruct((1, LANES), F32))
    flat = [a for (_, _, _, _, w, m, v) in params for a in (w, m, v)]
    return pl.pallas_call(
        body, name="adamw_small", out_shape=out_shape,
        in_specs=[_VMEM] * (2 * n_src + len(flat)), out_specs=[_VMEM] * len(out_shape),
    )(*gathered, *gathered_cols, *flat)


def _pad_rows(a, rows):
    return jnp.concatenate([a, jnp.zeros((rows - a.shape[0], a.shape[1]), a.dtype)], axis=0)


def kernel(x, meta_tokens, norm_g, w_in, conv_a_w, conv_a_b, ln_a_g, ln_a_b, w_a_out, b_a_out, conv_b_w, w_b_out, w_out, final_g, loss_target, m_meta_tokens, m_norm_g, m_w_in, m_conv_a_w, m_conv_a_b, m_ln_a_g, m_ln_a_b, m_w_a_out, m_b_a_out, m_conv_b_w, m_w_b_out, m_w_out, m_final_g, v_meta_tokens, v_norm_g, v_w_in, v_conv_a_w, v_conv_a_b, v_ln_a_g, v_ln_a_b, v_w_a_out, v_b_a_out, v_conv_b_w, v_w_b_out, v_w_out, v_final_g):
    seq = x.shape[1]
    assert x.shape == (1, seq, D_MODEL) and seq % TILE == 0 and w_in.shape == (1, D_MODEL, COLS)
    n_tiles = seq // TILE + 1
    tp = n_tiles * TILE
    pos = jnp.stack([lax.axis_index("x"), lax.axis_index("y"), lax.axis_index("c")]).astype(jnp.int32)
    me = 4 * pos[0] + 2 * pos[1] + pos[2]
    x2d = x[0]
    tgt2d = loss_target[0]

    small = jnp.concatenate([meta_tokens, _pad_rows(conv_a_w[0], 32), _pad_rows(conv_b_w[0], SUBLANES)], axis=0)
    final_g2 = final_g.reshape(1, D_MODEL)

    w_out_shards = [w[0].astype(BF16) for w in (w_a_out, w_b_out, w_out)]
    h_t, proj, meta_tile, small_params, w_in_all, *w_out_all = _gather_norm_proj(
        pos, x2d, small[None], norm_g, w_in[0].astype(BF16), w_out_shards, 3)
    small_params = small_params.transpose(1, 0, 2).reshape(small.shape[0], D_MODEL)
    conv_a_full, conv_b_full = small_params[N_META:N_META + 32], small_params[N_META + 32:]
    w_out_all = [w.reshape(D_MODEL, D_MODEL) for w in w_out_all]
    w_out_all_t = [w.T for w in w_out_all]
    dproj, ds1, lhs, rhs, small_a = _fused_pass(
        proj, x2d, tgt2d, meta_tile, conv_a_full, conv_a_b, ln_a_g, ln_a_b, b_a_out, conv_b_full, final_g2,
        w_out_all[0], w_out_all[1], w_out_all[2], w_out_all_t[0], w_out_all_t[1], w_out_all_t[2], n_tiles)
    k_tile = tp // 3
    gw_far, small_a_all = _grad_w_in_half(pos, h_t, dproj, k_tile, True, [("all", (small_a[None],))], "grad_w_in_far",
                                          narrow=True)
    sems, sent, landing, token = _start_exchanges([("sibling_half", (gw_far,))], "rs_far_start")
    gw_out = _grad_w_out(lhs, rhs, k_tile, token).reshape(N_DEV, 3 * ROWS_OUT, D_MODEL)
    (their_in,) = _wait_exchanges([("sibling_half", 1)], sems, sent, landing, gw_out, "rs_far_wait")
    sems_o, sent_o, landing_o, token = _start_exchanges([("sibling", (gw_out,))], "rs_out_start")
    gw_near, parts_in = _grad_w_in_half(pos, h_t, dproj, k_tile, False, [], "grad_w_in_near", after=token,
                                        add_to=their_in)
    sems_i, sent_i, landing_i, token = _start_exchanges([("chips_by_relation", (parts_in,))], "rs_chips_in_start")
    gw_out, their_out = _wait_exchanges([("sibling", 1)], sems_o, sent_o, landing_o, token, "rs_out_wait",
                                        keep_sources=True)
    parts_out = _chip_partial(pos, gw_out, their_out, (1, 2, 3), BF16, ROWS_OUT, "rs_parts_w_out")
    sems_o, sent_o, landing_o, token = _start_exchanges([("chips", (parts_out,))], "rs_chips_out_start")
    grad_x, small_b = _input_bwd(dproj, ds1, x2d, meta_tile, norm_g + token[0, 0], w_in_all, min(512, seq))

    sems_s, sent_s, landing_s, token = _start_exchanges([("all", (small_b[None],))], "gather_small_grads_start")
    (land_in,) = _wait_exchanges([("chips_by_relation", 1)], sems_i, sent_i, landing_i, token, "rs_chips_in_wait")
    (res_in,) = _adamw_sharded(pos, gw_near, None, land_in, [(w_in[0], m_w_in[0], v_w_in[0])], 128, "adamw_w_in")
    (land_out,) = _wait_exchanges([("chips", 1)], sems_o, sent_o, landing_o, res_in[0], "rs_chips_out_wait")
    res_out = _adamw_sharded(
        pos, gw_out, their_out, land_out,
        [(w_a_out[0], m_w_a_out[0], v_w_a_out[0]), (w_b_out[0], m_w_b_out[0], v_w_b_out[0]),
         (w_out[0], m_w_out[0], v_w_out[0])], ROWS_OUT, "adamw_w_out")
    (small_b_all,) = _wait_exchanges([("all", 1)], sems_s, sent_s, landing_s, res_out[2][0], "gather_small_grads_wait")
    small_grads = [small_a_all, small_b_all]
    small_cols = [lax.dynamic_slice_in_dim(g, me * LANES, LANES, axis=2) for g in small_grads]
    params = [
        (1, ROW_META, N_META, True, meta_tokens, m_meta_tokens, v_meta_tokens),
        (1, ROW_NORM_G, 1, False, norm_g, m_norm_g, v_norm_g),
        (0, ROW_CONV_A_W, CONV_A, True, conv_a_w[0], m_conv_a_w[0], v_conv_a_w[0]),
        (0, ROW_CONV_A_B, 1, False, conv_a_b, m_conv_a_b, v_conv_a_b),
        (0, ROW_LN_G, 1, False, ln_a_g, m_ln_a_g, v_ln_a_g),
        (0, ROW_LN_B, 1, False, ln_a_b, m_ln_a_b, v_ln_a_b),
        (0, ROW_B_A_OUT, 1, False, b_a_out, m_b_a_out, v_b_a_out),
        (0, ROW_CONV_B_W, CONV_B, True, conv_b_w[0], m_conv_b_w[0], v_conv_b_w[0]),
        (0, ROW_FINAL_G, 1, False, final_g2, m_final_g.reshape(1, D_MODEL), v_final_g.reshape(1, D_MODEL)),
    ]
    res_small = _adamw_small(small_grads, small_cols, params)
    loss = res_small[-1][0, 0]

    def small_res(p, kind, shape):
        return res_small[4 * p + kind].reshape(shape)

    per_weight = []
    for kind in range(4):
        per_weight.append([
            small_res(0, kind, meta_tokens.shape),
            small_res(1, kind, norm_g.shape),
            res_in[kind].reshape(w_in.shape),
            small_res(2, kind, conv_a_w.shape),
            small_res(3, kind, conv_a_b.shape),
            small_res(4, kind, ln_a_g.shape),
            small_res(5, kind, ln_a_b.shape),
            res_out[0][kind].reshape(w_a_out.shape),
            small_res(6, kind, b_a_out.shape),
            small_res(7, kind, conv_b_w.shape),
            res_out[1][kind].reshape(w_b_out.shape),
            res_out[2][kind].reshape(w_out.shape),
            small_res(8, kind, final_g.shape),
        ])
    return (loss, grad_x.reshape(x.shape), *per_weight[0], *per_weight[1], *per_weight[2], *per_weight[3])
```

```python
import functools

import jax
import jax.numpy as jnp
from jax import lax
from jax.experimental import pallas as pl
from jax.experimental.pallas import tpu as pltpu

D_MODEL = 1024
N_META = 16
N_DEV = 8
D_IN = 9 * D_MODEL
COLS = D_IN // N_DEV
ROWS_OUT = D_MODEL // N_DEV
CONV_A = 31
CONV_B = 3
EPS = 1e-6

ADAM_LR = 0.001
ADAM_B1 = 0.9
ADAM_B2 = 0.999
ADAM_EPS = 1e-08
ADAM_WD = 0.01
ADAM_STEP = 10

TILE = 128
LANES = 128
N_CHUNK = D_MODEL // LANES
HALO = 32
SUBLANES = 8
VMEM_LIMIT = 56 * 1024 * 1024

ROW_FINAL_G, ROW_B_A_OUT, ROW_LN_G, ROW_LN_B, ROW_CONV_A_B, ROW_LOSS = 0, 1, 2, 3, 4, 5
ROW_CONV_A_W, ROW_CONV_B_W, SMALL_A_ROWS = 8, 40, 48
ROW_NORM_G, ROW_META, SMALL_B_ROWS = 0, 8, 24

MESH = pl.DeviceIdType.MESH
_ANY = pl.BlockSpec(memory_space=pl.ANY)
_VMEM = pl.BlockSpec(memory_space=pltpu.VMEM)
BF16 = jnp.bfloat16
F32 = jnp.float32


def _resident(shape):
    return pl.BlockSpec(shape, lambda *_: (0,) * len(shape), pipeline_mode=pl.Buffered(1))


def _sigmoid(v):
    return jax.nn.sigmoid(v)


def _dot(a, b):
    return jnp.dot(a, b, preferred_element_type=F32)


def _dot_nt(a, b):
    return lax.dot_general(a, b, (((1,), (1,)), ((), ())), preferred_element_type=F32)


def _dot_tn(a, b):
    return lax.dot_general(a, b, (((0,), (0,)), ((), ())), preferred_element_type=F32)


def _colsum(v):
    return jnp.sum(v, axis=0, keepdims=True)


def _rowmean(v):
    parts = [v[:, LANES * c:LANES * (c + 1)] for c in range(v.shape[1] // LANES)]
    return jnp.sum(functools.reduce(jnp.add, parts), axis=-1, keepdims=True) * (1.0 / v.shape[1])


def _fold8(v):
    parts = [v[SUBLANES * g:SUBLANES * (g + 1)] for g in range(v.shape[0] // SUBLANES)]
    return functools.reduce(jnp.add, parts)


def _sibling_copies(srcs, dsts, send_sems, recv_sems):
    x, y, c = lax.axis_index("x"), lax.axis_index("y"), lax.axis_index("c")
    return [pltpu.make_async_remote_copy(
        src_ref=src.at[2 * q + (1 - c)], dst_ref=dst.at[q],
        send_sem=send_sems.at[4 * a + q], recv_sem=recv_sems.at[4 * a + q],
        device_id=(x, y, 1 - c), device_id_type=MESH)
        for a, (src, dst) in enumerate(zip(srcs, dsts)) for q in range(4)]


def _chip_copies(srcs, dsts, send_sems, recv_sems):
    x, y, c = lax.axis_index("x"), lax.axis_index("y"), lax.axis_index("c")
    targets = [(x, 1 - y, c), (1 - x, y, c), (1 - x, 1 - y, c)]
    return [pltpu.make_async_remote_copy(
        src_ref=src.at[k], dst_ref=dst.at[k],
        send_sem=send_sems.at[3 * a + k], recv_sem=recv_sems.at[3 * a + k],
        device_id=targets[k], device_id_type=MESH)
        for a, (src, dst) in enumerate(zip(srcs, dsts)) for k in range(3)]


def _sibling_half_copies(srcs, dsts, send_sems, recv_sems):
    x, y, c = lax.axis_index("x"), lax.axis_index("y"), lax.axis_index("c")
    return [pltpu.make_async_remote_copy(
        src_ref=src.at[q], dst_ref=dst.at[q],
        send_sem=send_sems.at[4 * a + q], recv_sem=recv_sems.at[4 * a + q],
        device_id=(x, y, 1 - c), device_id_type=MESH)
        for a, (src, dst) in enumerate(zip(srcs, dsts)) for q in range(4)]


def _all_copies(srcs, dsts, send_sems, recv_sems):
    x, y, c = lax.axis_index("x"), lax.axis_index("y"), lax.axis_index("c")
    mine = 4 * x + 2 * y + c
    copies = []
    for a, (src, dst) in enumerate(zip(srcs, dsts)):
        copies.append(pltpu.make_async_copy(src.at[0], dst.at[mine], send_sems.at[N_DEV * a]))
        for k in range(1, N_DEV):
            copies.append(pltpu.make_async_remote_copy(
                src_ref=src.at[0], dst_ref=dst.at[mine],
                send_sem=send_sems.at[N_DEV * a + k], recv_sem=recv_sems.at[N_DEV * a + k],
                device_id=(x ^ (k >> 2), y ^ ((k >> 1) & 1), c ^ (k & 1)), device_id_type=MESH))
    return copies


def _chip_copies_by_relation(srcs, dsts, send_sems, recv_sems):
    return _chip_copies([src.at[pl.ds(1, 3)] for src in srcs], dsts, send_sems, recv_sems)


_EXCHANGES = {"sibling": (4, _sibling_copies, 4), "sibling_half": (4, _sibling_half_copies, 4),
              "chips": (3, _chip_copies, 3), "chips_by_relation": (3, _chip_copies_by_relation, 3),
              "all": (N_DEV, _all_copies, N_DEV)}


def _exchange_shapes(kind, arrays):
    per_array, _, slots = _EXCHANGES[kind]
    out_shape = [jax.ShapeDtypeStruct((slots,) + a.shape[1:], a.dtype) for a in arrays]
    sems = [pltpu.SemaphoreType.DMA((per_array * len(arrays),))] * 2
    return out_shape, sems


def _ride_shapes(rides):
    shapes, sems = [], []
    for kind, arrays in rides:
        ride_shapes, ride_sems = _exchange_shapes(kind, arrays)
        shapes += ride_shapes
        sems += ride_sems
    return shapes, sems


def _riding(body, n_in, n_out, rides, is_first, is_last):
    counts = [len(arrays) for _, arrays in rides]
    n_arr = sum(counts)

    def wrapped(*refs):
        ins, srcs = refs[:n_in], refs[n_in:n_in + n_arr]
        outs = refs[n_in + n_arr:n_in + n_arr + n_out]
        dsts = refs[n_in + n_arr + n_out:n_in + 2 * n_arr + n_out]
        first_sem = len(refs) - 2 * len(rides)
        scratch, sems = refs[n_in + 2 * n_arr + n_out:first_sem], refs[first_sem:]

        def copies():
            made, at = [], 0
            for r, ((kind, _), n) in enumerate(zip(rides, counts)):
                made += _EXCHANGES[kind][1](srcs[at:at + n], dsts[at:at + n], sems[2 * r], sems[2 * r + 1])
                at += n
            return made

        @pl.when(is_first())
        def _():
            for cp in copies():
                cp.start()

        body(*ins, *outs, *scratch)

        @pl.when(is_last())
        def _():
            for cp in copies():
                cp.wait()

    return wrapped


_HBM = pl.BlockSpec(memory_space=pltpu.HBM)
_SEM = pl.BlockSpec(memory_space=pltpu.SEMAPHORE)
_FLOWS = pltpu.SideEffectType.DATAFLOW_SIDE_EFFECTING


def _start_exchanges(rides, name):
    arrays = [a for _, group in rides for a in group]
    shapes, sems = _ride_shapes(rides)
    n_arr, n_sem = len(arrays), len(sems)

    def body(*refs):
        srcs, lands = refs[:n_arr], refs[n_arr:2 * n_arr]
        sem_refs, token = refs[2 * n_arr:2 * n_arr + n_sem], refs[-1]
        at = 0
        for r, (kind, group) in enumerate(rides):
            n = len(group)
            for cp in _EXCHANGES[kind][1](srcs[at:at + n], lands[at:at + n], sem_refs[2 * r], sem_refs[2 * r + 1]):
                cp.start()
            at += n
        token[...] = jnp.zeros(token.shape, token.dtype)

    in_hbm = [pltpu.HBM(a.shape, a.dtype) for a in arrays]
    land_hbm = [pltpu.HBM(sh.shape, sh.dtype) for sh in shapes]
    res = pl.pallas_call(
        body, name=name,
        out_shape=(*sems, *in_hbm, *land_hbm, jax.ShapeDtypeStruct((SUBLANES, LANES), F32)),
        in_specs=[_HBM] * (2 * n_arr), out_specs=(*[_SEM] * n_sem, *[_HBM] * (2 * n_arr), _VMEM),
        input_output_aliases={i: n_sem + i for i in range(2 * n_arr)},
        compiler_params=pltpu.CompilerParams(has_side_effects=_FLOWS),
    )(*[pltpu.with_memory_space_constraint(a, pltpu.HBM) for a in arrays],
      *[pltpu.with_memory_space_constraint(lax.empty(sh.shape, sh.dtype), pltpu.HBM) for sh in shapes])
    return res[:n_sem], res[n_sem:n_sem + n_arr], res[n_sem + n_arr:n_sem + 2 * n_arr], res[-1]


def _wait_exchanges(kinds, sems, arrays, lands, after, name, keep_sources=False):
    n_arr, n_sem = len(arrays), len(sems)

    def body(*refs):
        srcs, dsts = refs[:n_arr], refs[n_arr:2 * n_arr]
        sem_refs = refs[2 * n_arr:2 * n_arr + n_sem]
        at = 0
        for r, (kind, n) in enumerate(kinds):
            for cp in _EXCHANGES[kind][1](srcs[at:at + n], dsts[at:at + n], sem_refs[2 * r], sem_refs[2 * r + 1]):
                cp.wait()
            at += n

    hbm = [pltpu.HBM(a.shape, a.dtype) for a in (*arrays, *lands)]
    return pl.pallas_call(
        body, name=name, out_shape=tuple(hbm),
        in_specs=[_HBM] * (2 * n_arr) + [_SEM] * n_sem + [_ANY], out_specs=tuple([_HBM] * (2 * n_arr)),
        input_output_aliases={i: i for i in range(2 * n_arr)},
        compiler_params=pltpu.CompilerParams(has_side_effects=_FLOWS),
    )(*arrays, *lands, *sems, after)[0 if keep_sources else n_arr:]


def _chip_partial(pos, mine, theirs, relations, out_dtype, row_tile, name):
    n_slots, m, n = mine.shape
    q0 = relations[0]

    def chip_of(qi, pos_ref):
        q = qi + q0
        return pos_ref[0] ^ (q >> 1), pos_ref[1] ^ (q & 1)

    def mine_map(qi, t, pos_ref):
        px, py = chip_of(qi, pos_ref)
        return (4 * px + 2 * py + pos_ref[2] if n_slots == N_DEV else 2 * px + py), t, 0

    def theirs_map(qi, t, pos_ref):
        px, py = chip_of(qi, pos_ref)
        return 2 * px + py, t, 0

    def body(pos_ref, a_ref, b_ref, o_ref):
        o_ref[...] = (a_ref[...] + b_ref[...]).astype(out_dtype)

    return pl.pallas_call(
        body, name=name,
        out_shape=jax.ShapeDtypeStruct((len(relations), m, n), out_dtype),
        grid_spec=pltpu.PrefetchScalarGridSpec(
            num_scalar_prefetch=1, grid=(len(relations), m // row_tile),
            in_specs=[pl.BlockSpec((None, row_tile, n), mine_map), pl.BlockSpec((None, row_tile, n), theirs_map)],
            out_specs=pl.BlockSpec((None, row_tile, n), lambda qi, t, pos_ref: (qi, t, 0))),
        compiler_params=pltpu.CompilerParams(dimension_semantics=("arbitrary", "arbitrary")),
    )(pos, mine, theirs)


PARTS = ((0, 512), (512, 640))


def _gather_norm_proj(pos, x2d, small_shard, norm_g, w_in_shard, w_out_shards, n_chunk):
    seq = x2d.shape[0]
    n_tiles = seq // TILE + 1
    tp = n_tiles * TILE
    n_parts = len(PARTS)
    widest = max(width for _, width in PARTS)
    units = [(s, u) for s in range(2) for u in range(n_parts)]
    units += [(s, u) for u in range(n_parts) for s in (2, 3, 5, 6)] + [(s, u) for u in range(n_parts) for s in (4, 7)]
    n_units = len(units)
    over_ici = [m for m, (s, _) in enumerate(units) if s in (2, 3)]
    handled_at = {m: m - 1 for m in range(1, n_units)}
    handled_at.update({m: over_ici[0] - 1 + i for i, m in enumerate(over_ici)})
    assert all(step < m for m, step in handled_at.items())
    n_steps = n_tiles + n_units
    chunk = tp // n_chunk

    def body(pos_ref, x_ref, g_ref, small_ref, win_ref, wa_ref, wb_ref, wo_ref,
             ht_ref, proj_ref, meta_ref, small_all, win_all, wa_all, wb_all, wo_all,
             h_all, wbuf, rbuf, small_buf, send_sems, recv_sems, local_sems, small_send, small_recv):
        g = pl.program_id(0)
        x, y, c = lax.axis_index("x"), lax.axis_index("y"), lax.axis_index("c")
        me, sibling = (x, y, c), (x, y, 1 - c)
        chips = [(1 - x, y), (x, 1 - y), (1 - x, 1 - y)]
        shards = (win_ref, wa_ref, wb_ref, wo_ref)
        gathered = (win_all, wa_all, wb_all, wo_all)
        n_arrays = len(shards)
        blocks = [me, sibling] + [(*chip, c) for chip in chips] + [(*chip, 1 - c) for chip in chips]

        def index(block):
            px, py, pc = block
            return 4 * px + 2 * py + pc

        def part(ref, a, u):
            return ref.at[:, pl.ds(PARTS[u][0], PARTS[u][1])] if a == 0 else ref

        def slot(a, block, u):
            return part(gathered[a].at[index(block)], a, u)

        def sem(a, k, u):
            return n_parts * k + u if a == 0 else 7 * n_parts + 7 * (a - 1) + k

        def copy(a, k, block, to, u=0, from_shard=False):
            return pltpu.make_async_remote_copy(
                src_ref=part(shards[a], a, u) if from_shard else slot(a, block, u), dst_ref=slot(a, block, u),
                send_sem=send_sems.at[sem(a, k, u)], recv_sem=recv_sems.at[sem(a, k, u)],
                device_id=to, device_id_type=MESH)

        def keep(a):
            return pltpu.make_async_copy(shards[a], gathered[a].at[index(me)], local_sems.at[a])

        def load(m):
            s, u = units[m]
            src = part(win_ref, 0, u) if s == 0 else slot(0, blocks[s], u)
            return pltpu.make_async_copy(src, wbuf.at[m % 2, :, 0:PARTS[u][1]], local_sems.at[n_arrays + m % 2])

        def store(m):
            s, u = units[m]
            col0 = pl.multiple_of(index(blocks[s]) * COLS + PARTS[u][0], LANES)
            return pltpu.make_async_copy(rbuf.at[m % 2, :, 0:PARTS[u][1]],
                                         proj_ref.at[:, pl.ds(col0, PARTS[u][1])], local_sems.at[n_arrays + 2 + m % 2])

        def by_x(a, u):
            return u == 0 if a == 0 else a < 3

        def relay(a, u=0):
            src, to = (blocks[3], blocks[2]) if by_x(a, u) else (blocks[2], blocks[3])
            return copy(a, 3, src, to, u)

        def arrive(m):
            s, u = units[m]
            if s == 1:
                copy(0, 0, sibling, me, u).wait_recv()
            elif 2 <= s <= 4:
                copy(0, s - 1, blocks[s], me, u).wait_recv()
                copy(0, s + 2, blocks[s], sibling, u).start()
                if s < 4 and by_x(0, u) == (s == 3):
                    relay(0, u).start()
            elif s >= 5:
                copy(0, s - 1, blocks[s], me, u).wait_recv()

        def pass_on_out(j, arrays=(1, 2, 3)):
            for a in arrays:
                copy(a, j + 1, blocks[2 + j], me).wait_recv()
                copy(a, j + 4, blocks[2 + j], sibling).start()
                if j < 2 and by_x(a, 0) == (j == 1):
                    relay(a).start()

        to_relay = [a for a in range(1, 4) if by_x(a, 0)]
        targets = [sibling, blocks[2], blocks[3]]
        relays_started = max(handled_at[m] for m, (s, u) in enumerate(units) if s in (2, 3) and by_x(0, u) == (s == 3))

        def small_copies():
            return _all_copies([small_ref], [small_all], small_send, small_recv)

        @pl.when(g == 0)
        def _():
            for cp in small_copies():
                cp.start()
            for a in range(n_arrays):
                keep(a).start()
            for u in range(n_parts):
                for k, to in enumerate(targets):
                    copy(0, k, me, to, u, from_shard=True).start()
            for a in range(1, 4):
                copy(a, 0, me, sibling, from_shard=True).start()
            load(0).start()

        @pl.when(g == n_tiles - 2)
        def _():
            for cp in small_copies():
                cp.wait()
            fetch = pltpu.make_async_copy(small_all, small_buf, local_sems.at[n_arrays + 4])
            fetch.start()
            fetch.wait()
            meta_ref[0:TILE - N_META, :] = jnp.zeros((TILE - N_META, D_MODEL), F32)
            meta_ref[TILE - N_META:TILE, :] = jnp.concatenate([small_buf[d, 0:N_META, :] for d in range(N_DEV)], axis=1)

        @pl.when(g < n_tiles)
        def _():
            s0 = jnp.where(g == n_tiles - 1, meta_ref[...], x_ref[...])
            r = lax.rsqrt(_rowmean(s0 * s0) + EPS)
            h32 = (s0 * r) * g_ref[...]
            ht_ref[...] = h32.T.astype(BF16)
            h_all[pl.ds(pl.multiple_of(g * TILE, TILE), TILE), :] = h32.astype(BF16)

        for m in range(n_units):
            @pl.when(g == n_tiles + m)
            def _(m=m):
                load(m).wait()
                for later in range(m + 1, n_units):
                    if handled_at[later] == m:
                        arrive(later)
                if m + 1 < n_units:
                    load(m + 1).start()
                if m == relays_started:
                    for a in range(1, 4):
                        for k in (1, 2):
                            copy(a, k, me, targets[k], from_shard=True).start()
                if m == n_units - 3:
                    pass_on_out(1, to_relay)
                if m == n_units - 2:
                    pass_on_out(0)
                if m >= 2:
                    store(m - 2).wait()

        m_now = jnp.maximum(g - n_tiles, 0)
        u_now = functools.reduce(jnp.add, [jnp.where(m_now == m, u, 0) for m, (_, u) in enumerate(units)])
        for u, (_, width) in enumerate(PARTS):
            @pl.when((g >= n_tiles) & (u_now == u))
            def _(width=width):
                w = wbuf[m_now % 2, :, 0:width]
                for r in range(n_chunk):
                    rbuf[m_now % 2, r * chunk:(r + 1) * chunk, 0:width] = _dot(h_all[r * chunk:(r + 1) * chunk, :], w)

        for m in range(n_units):
            @pl.when(g == n_tiles + m)
            def _(m=m):
                store(m).start()

        @pl.when(g == n_steps - 1)
        def _():
            pass_on_out(1, [a for a in range(1, 4) if a not in to_relay])
            pass_on_out(2)
            store(n_units - 2).wait()
            store(n_units - 1).wait()
            for a in range(1, 4):
                copy(a, 0, sibling, me).wait_recv()
                for j in range(3):
                    copy(a, 4 + j, blocks[5 + j], me).wait_recv()
            for a in range(n_arrays):
                for u in range(n_parts if a == 0 else 1):
                    for k, to in enumerate(targets):
                        copy(a, k, me, to, u, from_shard=True).wait_send()
                    relay(a, u).wait_send()
                    for j in range(3):
                        copy(a, 4 + j, blocks[2 + j], sibling, u).wait_send()
                keep(a).wait()

    n_x = n_tiles - 1
    return pl.pallas_call(
        body, name="gather_norm_proj",
        out_shape=[jax.ShapeDtypeStruct((D_MODEL, tp), BF16), jax.ShapeDtypeStruct((tp, D_IN), F32),
                   jax.ShapeDtypeStruct((TILE, D_MODEL), F32), jax.ShapeDtypeStruct((N_DEV,) + small_shard.shape[1:], F32),
                   jax.ShapeDtypeStruct((N_DEV,) + w_in_shard.shape, BF16)]
                  + [jax.ShapeDtypeStruct((N_DEV,) + w.shape, BF16) for w in w_out_shards],
        grid_spec=pltpu.PrefetchScalarGridSpec(
            num_scalar_prefetch=1, grid=(n_steps,),
            in_specs=[pl.BlockSpec((TILE, D_MODEL), lambda g, pos_ref: (jnp.minimum(g, n_x - 1), 0)),
                      _VMEM, _ANY, _ANY, _ANY, _ANY, _ANY],
            out_specs=[pl.BlockSpec((D_MODEL, TILE), lambda g, pos_ref: (0, jnp.minimum(g, n_tiles - 1))),
                       _ANY, _VMEM, _ANY, _ANY, _ANY, _ANY, _ANY],
            scratch_shapes=[pltpu.VMEM((tp, D_MODEL), BF16), pltpu.VMEM((2, D_MODEL, widest), BF16),
                            pltpu.VMEM((2, tp, widest), F32), pltpu.VMEM((N_DEV,) + small_shard.shape[1:], F32),
                            pltpu.SemaphoreType.DMA((7 * n_parts + 21,)), pltpu.SemaphoreType.DMA((7 * n_parts + 21,)),
                            pltpu.SemaphoreType.DMA((9,)),
                            pltpu.SemaphoreType.DMA((N_DEV,)), pltpu.SemaphoreType.DMA((N_DEV,))]),
        compiler_params=pltpu.CompilerParams(dimension_semantics=("arbitrary",), vmem_limit_bytes=VMEM_LIMIT),
    )(pos, x2d, norm_g, small_shard, w_in_shard, *w_out_shards)


C_AVAL, C_AGLU, C_AZ, C_BB, C_BC, C_BX, C_BZ, C_GA, C_GB = (k * D_MODEL for k in range(9))
S_AZ, S_BB, S_BZ, S_GA, S_GB = (k * D_MODEL for k in range(5))


def _fused_pass(proj, x2d, tgt2d, meta_tile, conv_a_w, conv_a_b, ln_a_g, ln_a_b, b_a_out, conv_b_w, final_g,
                w_a, w_b, w_o, w_a_t, w_b_t, w_o_t, n_tiles):
    T = TILE
    tp = n_tiles * T
    inv_d = 1.0 / D_MODEL

    def block_of(tile):
        return jnp.where(tile == 0, n_tiles - 1, tile - 1)

    def cur(i):
        return block_of(jnp.minimum(i, n_tiles - 1))

    def prev(i):
        return block_of(jnp.clip(i - 1, 0, n_tiles - 1))

    def xblk(i):
        return jnp.maximum(jnp.minimum(i, n_tiles - 1) - 1, 0)

    def body(proj_ref, aprev, cprev, x_ref, tgt_ref, meta_ref, caw_ref, cab_ref, lng_ref, lnb_ref, bao_ref, cbw_ref,
             fg_ref, wa_ref, wb_ref, wo_ref, wat_ref, wbt_ref, wot_ref,
             dproj_ref, ds1_ref, lhs_ref, rhs_ref, small_ref,
             ua0_buf, cb_buf, dua1_buf, dc3_buf, stage, ua1_buf, c3_buf,
             dpa_buf, dpb_buf, dcaw8, dcbw8, shift_buf):
        i = pl.program_id(0)
        this, before = i % 2, 1 - i % 2

        @pl.when(i == 0)
        def _init():
            for buf in (ua0_buf, cb_buf, dua1_buf, dc3_buf, dcaw8, dcbw8):
                buf[...] = jnp.zeros(buf.shape, buf.dtype)
            small_ref[...] = jnp.zeros(small_ref.shape, F32)

        @pl.when(i >= 1)
        def _emit_stage():
            dproj_ref[:, C_AZ:C_BC] = stage[:, S_AZ:S_BZ]
            dproj_ref[:, C_BZ:D_IN] = stage[:, S_BZ:S_GB + D_MODEL]

        @pl.when(i < n_tiles)
        def _front():
            def conv_chunk(cc, carry):
                c0 = pl.multiple_of(cc * LANES, LANES)
                lanes = pl.ds(c0, LANES)

                def col(base):
                    return pl.ds(pl.multiple_of(base + cc * LANES, LANES), LANES)

                ua0 = proj_ref[:, col(C_AVAL)] * _sigmoid(proj_ref[:, col(C_AGLU)])
                ua0_buf[this, 0:HALO, lanes] = ua0_buf[before, T:T + HALO, lanes]
                ua0_buf[this, HALO:HALO + T, lanes] = ua0
                acc = jnp.broadcast_to(cab_ref[:, lanes], (T, LANES))
                lead = HALO - (CONV_A - 1)
                for r in range(SUBLANES):
                    taps = [k for k in range(CONV_A) if (k + lead) % SUBLANES == r]
                    rows = T + SUBLANES * max((k + lead) // SUBLANES for k in taps)
                    if r:
                        shift_buf[r, 0:rows, :] = ua0_buf[this, pl.ds(r, rows), lanes]
                    for k in taps:
                        q = (k + lead) // SUBLANES
                        if r:
                            win = shift_buf[r, SUBLANES * q:SUBLANES * q + T, :]
                        else:
                            win = ua0_buf[this, pl.ds(SUBLANES * q, T), lanes]
                        acc = acc + caw_ref[k:k + 1, lanes] * win
                ua1_buf[:, lanes] = acc
                cb = proj_ref[:, col(C_BC)] * proj_ref[:, col(C_BX)]
                cb_buf[this, 0:SUBLANES, lanes] = cb_buf[before, T:T + SUBLANES, lanes]
                cb_buf[this, SUBLANES:SUBLANES + T, lanes] = cb
                lead_b = SUBLANES - (CONV_B - 1)
                acc3 = cbw_ref[0:1, lanes] * cb_buf[this, pl.ds(lead_b, T), lanes]
                for k in range(1, CONV_B):
                    acc3 = acc3 + cbw_ref[k:k + 1, lanes] * cb_buf[this, pl.ds(lead_b + k, T), lanes]
                c3_buf[:, lanes] = acc3
                return carry

            lax.fori_loop(0, N_CHUNK, conv_chunk, 0)

            ua1 = ua1_buf[...]
            xc = ua1 - _rowmean(ua1)
            rstd = lax.rsqrt(_rowmean(xc * xc) + EPS)
            xhat = xc * rstd
            ua2 = xhat * lng_ref[...] + lnb_ref[...]
            sg2 = _sigmoid(ua2)
            ua3 = ua2 * sg2
            a_z = proj_ref[:, C_AZ:C_AZ + D_MODEL]
            sz = _sigmoid(a_z)
            silu_az = a_z * sz
            lhs_ref[0] = (ua3 * silu_az).astype(BF16)
            b_z = proj_ref[:, C_BZ:C_BZ + D_MODEL]
            sbz = _sigmoid(b_z)
            silu_bz = b_z * sbz
            b_b = proj_ref[:, C_BB:C_BB + D_MODEL]
            c3 = c3_buf[...]
            ub = b_b * c3
            lhs_ref[1] = (ub * silu_bz).astype(BF16)

            ya = _dot(lhs_ref[0], wa_ref[...]) + bao_ref[...]
            yb = _dot(lhs_ref[1], wb_ref[...])
            sga = _sigmoid(proj_ref[:, C_GA:C_GA + D_MODEL])
            sgb = _sigmoid(proj_ref[:, C_GB:C_GB + D_MODEL])
            m_b = (sga * ya + sgb * yb).astype(BF16)
            lhs_ref[2] = m_b
            s0 = jnp.where(i == 0, meta_ref[...], x_ref[...])
            s1 = s0 + _dot(m_b, wo_ref[...])
            r1 = lax.rsqrt(_rowmean(s1 * s1) + EPS)
            y = (s1 * r1) * fg_ref[...]
            is_token = (i >= 1).astype(F32)
            err = (y - tgt_ref[...]) * is_token
            small_ref[ROW_LOSS:ROW_LOSS + 1, :] += (0.5 * inv_d) * _colsum(err * err)
            dy = err * inv_d
            small_ref[ROW_FINAL_G:ROW_FINAL_G + 1, :] += _colsum(dy * (s1 * r1))
            gy = dy * fg_ref[...]
            ds1 = r1 * gy - s1 * ((r1 * r1 * r1) * _rowmean(gy * s1))
            ds1_ref[...] = ds1
            ds1_b = ds1.astype(BF16)
            rhs_ref[2] = ds1_b
            dm = _dot(ds1_b, wot_ref[...])
            dya = dm * sga
            dyb = dm * sgb
            stage[:, S_GA:S_GA + D_MODEL] = (dya * ya * (1.0 - sga)).astype(BF16)
            stage[:, S_GB:S_GB + D_MODEL] = (dyb * yb * (1.0 - sgb)).astype(BF16)
            small_ref[ROW_B_A_OUT:ROW_B_A_OUT + 1, :] += _colsum(dya)
            dya_b = dya.astype(BF16)
            dyb_b = dyb.astype(BF16)
            rhs_ref[0] = dya_b
            rhs_ref[1] = dyb_b
            dpa_buf[...] = _dot(dya_b, wat_ref[...])
            dpb_buf[...] = _dot(dyb_b, wbt_ref[...])

            dpa = dpa_buf[...]
            stage[:, S_AZ:S_AZ + D_MODEL] = (dpa * ua3 * (sz + silu_az * (1.0 - sz))).astype(BF16)
            dua2 = dpa * silu_az * (sg2 + ua3 * (1.0 - sg2))
            small_ref[ROW_LN_G:ROW_LN_G + 1, :] += _colsum(dua2 * xhat)
            small_ref[ROW_LN_B:ROW_LN_B + 1, :] += _colsum(dua2)
            dxh = dua2 * lng_ref[...]
            dua1 = rstd * (dxh - _rowmean(dxh) - xhat * _rowmean(dxh * xhat))
            small_ref[ROW_CONV_A_B:ROW_CONV_A_B + 1, :] += _colsum(dua1)
            dua1_buf[this, 0:T, :] = dua1
            dua1_buf[before, T:T + HALO, :] = dua1[0:HALO]
            dpb = dpb_buf[...]
            stage[:, S_BZ:S_BZ + D_MODEL] = (dpb * ub * (sbz + silu_bz * (1.0 - sbz))).astype(BF16)
            dub = dpb * silu_bz
            stage[:, S_BB:S_BB + D_MODEL] = (dub * c3).astype(BF16)
            dc3 = dub * b_b
            dc3_buf[this, 0:T, :] = dc3
            dc3_buf[before, T:T + SUBLANES, :] = dc3[0:SUBLANES]

        @pl.when(i == n_tiles)
        def _no_later_tile():
            dua1_buf[before, T:T + HALO, :] = jnp.zeros((HALO, D_MODEL), F32)
            dc3_buf[before, T:T + SUBLANES, :] = jnp.zeros((SUBLANES, D_MODEL), F32)

        @pl.when(i >= 1)
        def _lagged():
            def convt_chunk(cc, carry):
                c0 = pl.multiple_of(cc * LANES, LANES)
                lanes = pl.ds(c0, LANES)

                def col(base):
                    return pl.ds(pl.multiple_of(base + cc * LANES, LANES), LANES)

                ua0 = ua0_buf[before, HALO:HALO + T, lanes]
                acc = jnp.zeros((T, LANES), F32)
                for r in range(SUBLANES):
                    shifts = [j for j in range(CONV_A) if j % SUBLANES == r]
                    rows = T + shifts[-1] - r
                    if r:
                        shift_buf[r, 0:rows, :] = dua1_buf[before, pl.ds(r, rows), lanes]
                    for j in shifts:
                        k = CONV_A - 1 - j
                        if r:
                            later = shift_buf[r, j - r:j - r + T, :]
                        else:
                            later = dua1_buf[before, pl.ds(j, T), lanes]
                        acc = acc + caw_ref[k:k + 1, lanes] * later
                        dcaw8[SUBLANES * k:SUBLANES * (k + 1), lanes] += _fold8(ua0 * later)
                a_val = aprev[:, col(0)]
                sg = _sigmoid(aprev[:, col(D_MODEL)])
                dproj_ref[:, col(C_AVAL)] = (acc * sg).astype(BF16)
                dproj_ref[:, col(C_AGLU)] = (acc * a_val * (sg * (1.0 - sg))).astype(BF16)

                cb = cb_buf[before, SUBLANES:SUBLANES + T, lanes]
                acc3 = jnp.zeros((T, LANES), F32)
                for j in range(CONV_B):
                    k = CONV_B - 1 - j
                    later = dc3_buf[before, pl.ds(j, T), lanes]
                    acc3 = acc3 + cbw_ref[k:k + 1, lanes] * later
                    dcbw8[SUBLANES * k:SUBLANES * (k + 1), lanes] += _fold8(cb * later)
                dproj_ref[:, col(C_BC)] = (acc3 * cprev[:, col(D_MODEL)]).astype(BF16)
                dproj_ref[:, col(C_BX)] = (acc3 * cprev[:, col(0)]).astype(BF16)
                return carry

            lax.fori_loop(0, N_CHUNK, convt_chunk, 0)

        @pl.when(i == n_tiles)
        def _finish():
            for k in range(CONV_A):
                small_ref[ROW_CONV_A_W + k:ROW_CONV_A_W + k + 1, :] = _colsum(dcaw8[SUBLANES * k:SUBLANES * (k + 1), :])
            for k in range(CONV_B):
                small_ref[ROW_CONV_B_W + k:ROW_CONV_B_W + k + 1, :] = _colsum(dcbw8[SUBLANES * k:SUBLANES * (k + 1), :])

    pair = 2 * D_MODEL
    return pl.pallas_call(
        body, name="fused_pass", grid=(n_tiles + 1,),
        out_shape=[
            jax.ShapeDtypeStruct((tp, D_IN), BF16),
            jax.ShapeDtypeStruct((tp, D_MODEL), F32),
            jax.ShapeDtypeStruct((3, tp, D_MODEL), BF16),
            jax.ShapeDtypeStruct((3, tp, D_MODEL), BF16),
            jax.ShapeDtypeStruct((SMALL_A_ROWS, D_MODEL), F32),
        ],
        in_specs=[
            pl.BlockSpec((T, D_IN), lambda i: (cur(i), 0)),
            pl.BlockSpec((T, pair), lambda i: (prev(i), C_AVAL // pair)),
            pl.BlockSpec((T, pair), lambda i: (prev(i), C_BC // pair)),
            pl.BlockSpec((T, D_MODEL), lambda i: (xblk(i), 0)),
            pl.BlockSpec((T, D_MODEL), lambda i: (xblk(i), 0)),
            _VMEM, _VMEM, _VMEM, _VMEM, _VMEM, _VMEM, _VMEM, _VMEM,
            *[_resident((D_MODEL, D_MODEL)) for _ in range(6)],
        ],
        out_specs=[
            pl.BlockSpec((T, D_IN), lambda i: (prev(i), 0)),
            pl.BlockSpec((T, D_MODEL), lambda i: (cur(i), 0)),
            pl.BlockSpec((3, T, D_MODEL), lambda i: (0, cur(i), 0)),
            pl.BlockSpec((3, T, D_MODEL), lambda i: (0, cur(i), 0)),
            _VMEM,
        ],
        scratch_shapes=[
            pltpu.VMEM((2, HALO + T, D_MODEL), F32),
            pltpu.VMEM((2, SUBLANES + T, D_MODEL), F32),
            pltpu.VMEM((2, T + HALO, D_MODEL), F32),
            pltpu.VMEM((2, T + SUBLANES, D_MODEL), F32),
            pltpu.VMEM((T, 5 * D_MODEL), BF16),
            pltpu.VMEM((T, D_MODEL), F32),
            pltpu.VMEM((T, D_MODEL), F32),
            pltpu.VMEM((T, D_MODEL), F32),
            pltpu.VMEM((T, D_MODEL), F32),
            pltpu.VMEM((32 * SUBLANES, D_MODEL), F32),
            pltpu.VMEM((SUBLANES * SUBLANES, D_MODEL), F32),
            pltpu.VMEM((SUBLANES, T + HALO, LANES), F32),
        ],
        compiler_params=pltpu.CompilerParams(dimension_semantics=("arbitrary",), vmem_limit_bytes=VMEM_LIMIT),
    )(proj, proj, proj, x2d, tgt2d, meta_tile, conv_a_w, conv_a_b, ln_a_g, ln_a_b, b_a_out, conv_b_w, final_g,
      w_a, w_b, w_o, w_a_t, w_b_t, w_o_t)


def _input_bwd(dproj, ds1, x2d, meta_tile, norm_g, w_in_all, row_tile):
    seq = x2d.shape[0]
    n_steps = seq // row_tile
    meta_block = seq // TILE

    def backward(dp_ref, ds1_ref, s0_ref, g_ref, w_ref, out_ref, vec_ref):
        dh = _dot_nt(dp_ref[:, 0:COLS], w_ref[0])
        for j in range(1, N_DEV):
            dh = dh + _dot_nt(dp_ref[:, j * COLS:(j + 1) * COLS], w_ref[j])
        s0v = s0_ref[...]
        r = lax.rsqrt(_rowmean(s0v * s0v) + EPS)
        gh = dh * g_ref[...]
        out_ref[...] = ds1_ref[...] + r * gh - s0v * ((r * r * r) * _rowmean(gh * s0v))
        vec_ref[ROW_NORM_G:ROW_NORM_G + 1, :] += _colsum(dh * (s0v * r))

    def body(dp_ref, ds1_ref, x_ref, dpm_ref, ds1m_ref, meta_ref, g_ref, w_ref, gx_ref, small_ref, gmeta_buf):
        t = pl.program_id(0)

        @pl.when(t == 0)
        def _():
            small_ref[...] = jnp.zeros(small_ref.shape, F32)

        backward(dp_ref, ds1_ref, x_ref, g_ref, w_ref, gx_ref, small_ref)

        @pl.when(t == n_steps - 1)
        def _():
            backward(dpm_ref, ds1m_ref, meta_ref, g_ref, w_ref, gmeta_buf, small_ref)
            small_ref[ROW_META:ROW_META + N_META, :] = gmeta_buf[TILE - N_META:TILE, :]

    return pl.pallas_call(
        body, name="input_bwd", grid=(n_steps,),
        out_shape=[jax.ShapeDtypeStruct(x2d.shape, F32), jax.ShapeDtypeStruct((SMALL_B_ROWS, D_MODEL), F32)],
        in_specs=[pl.BlockSpec((row_tile, D_IN), lambda t: (t, 0)),
                  pl.BlockSpec((row_tile, D_MODEL), lambda t: (t, 0)),
                  pl.BlockSpec((row_tile, D_MODEL), lambda t: (t, 0)),
                  pl.BlockSpec((TILE, D_IN), lambda t: (meta_block, 0)),
                  pl.BlockSpec((TILE, D_MODEL), lambda t: (meta_block, 0)),
                  _VMEM, _VMEM, _resident((N_DEV, D_MODEL, COLS))],
        out_specs=[pl.BlockSpec((row_tile, D_MODEL), lambda t: (t, 0)), _VMEM],
        scratch_shapes=[pltpu.VMEM((TILE, D_MODEL), F32)],
        compiler_params=pltpu.CompilerParams(dimension_semantics=("arbitrary",), vmem_limit_bytes=VMEM_LIMIT),
    )(dproj, ds1, x2d, dproj, ds1, meta_tile, norm_g, w_in_all)


def _grad_w_in_half(pos, h_t, dproj, k_tile, other_side, rides, name, after=None, add_to=None, narrow=False):
    tp = h_t.shape[1]
    n_k = tp // k_tile
    order = [] if after is None else [after]
    summing = add_to is not None
    assert not (summing and narrow)

    def column_block(q, k, pos_ref):
        return k, 2 * q + (1 - pos_ref[2] if other_side else pos_ref[2])

    def body(pos_ref, h_ref, dp_ref, *refs):
        acc = refs[-1]

        @pl.when(pl.program_id(1) == 0)
        def _():
            acc[...] = refs[0][...].astype(F32) if summing else jnp.zeros(acc.shape, F32)

        acc[...] += _dot(h_ref[...], dp_ref[...])

        if summing or narrow:
            @pl.when(pl.program_id(1) == n_k - 1)
            def _():
                refs[-2][...] = acc[...].astype(BF16)

        if summing:
            @pl.when((pl.program_id(1) == n_k - 1) & (pl.program_id(0) == 2 * pos_ref[0] + pos_ref[1]))
            def _():
                refs[-3][...] = acc[...]

    ride = [a for _, arrays in rides for a in arrays]
    n_arr = len(ride)
    ride_shapes, ride_sems = _ride_shapes(rides)
    block = (None, D_MODEL, COLS)
    extra_in = [add_to] if summing else []
    extra_in_specs = [pl.BlockSpec(block, lambda q, k, pos_ref: (q, 0, 0))] if summing else []
    extra_out = [jax.ShapeDtypeStruct((4, D_MODEL, COLS), BF16)] if summing else []
    extra_out_specs = [pl.BlockSpec(block, lambda q, k, pos_ref: (q ^ (2 * pos_ref[0] + pos_ref[1]), 0, 0))] \
        if summing else []
    body = _riding(body, 3 + len(extra_in) + len(order), 1 + len(extra_out), rides,
                   lambda: (pl.program_id(0) == 0) & (pl.program_id(1) == 0),
                   lambda: (pl.program_id(0) == 3) & (pl.program_id(1) == n_k - 1))
    return pl.pallas_call(
        body, name=name,
        out_shape=[jax.ShapeDtypeStruct((1 if summing else 4, D_MODEL, COLS), BF16 if narrow else F32)]
        + extra_out + ride_shapes,
        grid_spec=pltpu.PrefetchScalarGridSpec(
            num_scalar_prefetch=1, grid=(4, n_k),
            in_specs=[pl.BlockSpec((D_MODEL, k_tile), lambda q, k, pos_ref: (0, k)),
                      pl.BlockSpec((k_tile, COLS), column_block)] + extra_in_specs + [_ANY] * (len(order) + n_arr),
            out_specs=[pl.BlockSpec(block, lambda q, k, pos_ref: (0 if summing else q, 0, 0))]
            + extra_out_specs + [_ANY] * n_arr,
            scratch_shapes=([pltpu.VMEM((D_MODEL, COLS), F32)] if summing or narrow else []) + ride_sems),
        compiler_params=pltpu.CompilerParams(dimension_semantics=("arbitrary", "arbitrary"),
                                             vmem_limit_bytes=VMEM_LIMIT),
    )(pos, h_t, dproj, *extra_in, *order, *ride)


def _grad_w_out(lhs, rhs, k_tile, after):
    tp = lhs.shape[1]

    def body(a_ref, b_ref, after_ref, o_ref):
        @pl.when(pl.program_id(1) == 0)
        def _():
            o_ref[...] = jnp.zeros(o_ref.shape, F32)

        o_ref[...] += _dot_tn(a_ref[...], b_ref[...]).reshape(N_DEV, ROWS_OUT, D_MODEL)

    return pl.pallas_call(
        body, name="grad_w_out", grid=(3, tp // k_tile),
        out_shape=jax.ShapeDtypeStruct((N_DEV, 3, ROWS_OUT, D_MODEL), F32),
        in_specs=[pl.BlockSpec((None, k_tile, D_MODEL), lambda w, k: (w, k, 0)),
                  pl.BlockSpec((None, k_tile, D_MODEL), lambda w, k: (w, k, 0)), _ANY],
        out_specs=pl.BlockSpec((N_DEV, None, ROWS_OUT, D_MODEL), lambda w, k: (0, w, 0, 0)),
        compiler_params=pltpu.CompilerParams(dimension_semantics=("arbitrary", "arbitrary"),
                                             vmem_limit_bytes=VMEM_LIMIT),
    )(lhs, rhs, after)


def _adamw_math(w, g, m, v):
    m = ADAM_B1 * m + (1.0 - ADAM_B1) * g
    v = ADAM_B2 * v + (1.0 - ADAM_B2) * (g * g)
    m_hat = m / (1.0 - ADAM_B1 ** ADAM_STEP)
    v_hat = v / (1.0 - ADAM_B2 ** ADAM_STEP)
    delta = -ADAM_LR * (m_hat / (jnp.sqrt(v_hat) + ADAM_EPS) + ADAM_WD * w)
    return delta, m, v


def _adamw_sharded(pos, mine, theirs, landed, weights, row_tile, name, after=None):
    order = [] if after is None else [after]
    rows, n = weights[0][0].shape
    n_slots = mine.shape[0]
    per_shard = rows // row_tile
    assert per_shard == 1 or len(weights) == 1

    def mine_map(j, t, pos_ref):
        chip = 2 * pos_ref[0] + pos_ref[1]
        return {N_DEV: 2 * chip + pos_ref[2], 4: chip, 1: 0}[n_slots], j * per_shard + t, 0

    def theirs_map(j, t, pos_ref):
        return 2 * pos_ref[0] + pos_ref[1], j * per_shard + t, 0

    def body(pos_ref, mine_ref, *refs):
        if theirs is not None:
            g = mine_ref[...] + refs[0][...]
            refs = refs[1:]
        else:
            g = mine_ref[...]
        land_ref, refs = refs[0], refs[1:]
        ins, outs = refs[:3 * len(weights)], refs[3 * len(weights) + len(order):]
        for k in range(3):
            g = g + land_ref[k].astype(F32)
        for j in range(len(weights)):
            @pl.when(pl.program_id(0) == j)
            def _(j=j):
                w_ref, m_ref, v_ref = ins[3 * j:3 * j + 3]
                delta, m_new, v_new = _adamw_math(w_ref[...], g, m_ref[...], v_ref[...])
                for ref, val in zip(outs[4 * j:4 * j + 4], (g, delta, m_new, v_new)):
                    ref[...] = val

    tile = pl.BlockSpec((row_tile, n), lambda j, t, pos_ref: (t, 0))
    res = pl.pallas_call(
        body, name=name,
        out_shape=[jax.ShapeDtypeStruct((rows, n), F32)] * (4 * len(weights)),
        grid_spec=pltpu.PrefetchScalarGridSpec(
            num_scalar_prefetch=1, grid=(len(weights), per_shard),
            in_specs=[pl.BlockSpec((None, row_tile, n), mine_map)]
            + ([pl.BlockSpec((None, row_tile, n), theirs_map)] if theirs is not None else [])
            + [pl.BlockSpec((3, row_tile, n), lambda j, t, pos_ref: (0, j * per_shard + t, 0))]
            + [tile] * (3 * len(weights)) + [_ANY] * len(order),
            out_specs=[tile] * (4 * len(weights))),
        compiler_params=pltpu.CompilerParams(dimension_semantics=("arbitrary", "arbitrary")),
    )(pos, mine, *([theirs] if theirs is not None else []), landed, *[a for wmv in weights for a in wmv], *order)
    return [res[4 * j:4 * j + 4] for j in range(len(weights))]


def _adamw_small(gathered, gathered_cols, params):
    n_par, n_src = len(params), len(gathered)

    def body(*refs):
        g_refs, gc_refs = refs[:n_src], refs[n_src:2 * n_src]
        ins = refs[2 * n_src:2 * n_src + 3 * n_par]
        outs = refs[2 * n_src + 3 * n_par:]
        loss_ref = outs[4 * n_par]

        def reduced(ref, row, n_rows):
            g = ref[0, row:row + n_rows, :]
            for d in range(1, N_DEV):
                g = g + ref[d, row:row + n_rows, :]
            return g

        for p, (src, row, n_rows, sharded, _, _, _) in enumerate(params):
            g = reduced((gc_refs if sharded else g_refs)[src], row, n_rows)
            w_ref, m_ref, v_ref = ins[3 * p:3 * p + 3]
            delta, m_new, v_new = _adamw_math(w_ref[...], g, m_ref[...], v_ref[...])
            outs[4 * p][...] = g
            outs[4 * p + 1][...] = delta
            outs[4 * p + 2][...] = m_new
            outs[4 * p + 3][...] = v_new
        loss = jnp.sum(reduced(g_refs[0], ROW_LOSS, 1), axis=1, keepdims=True)
        loss_ref[...] = jnp.broadcast_to(loss, loss_ref.shape)

    out_shape = []
    for (_, _, _, _, w, _, _) in params:
        out_shape += [jax.ShapeDtypeStruct(w.shape, F32)] * 4
    out_shape.append(jax.ShapeDtypeStruct((1, LANES), F32))
    flat = [a for (_, _, _, _, w, m, v) in params for a in (w, m, v)]
    return pl.pallas_call(
        body, name="adamw_small", out_shape=out_shape,
        in_specs=[_VMEM] * (2 * n_src + len(flat)), out_specs=[_VMEM] * len(out_shape),
    )(*gathered, *gathered_cols, *flat)


def _pad_rows(a, rows):
    return jnp.concatenate([a, jnp.zeros((rows - a.shape[0], a.shape[1]), a.dtype)], axis=0)


def kernel(x, meta_tokens, norm_g, w_in, conv_a_w, conv_a_b, ln_a_g, ln_a_b, w_a_out, b_a_out, conv_b_w, w_b_out, w_out, final_g, loss_target, m_meta_tokens, m_norm_g, m_w_in, m_conv_a_w, m_conv_a_b, m_ln_a_g, m_ln_a_b, m_w_a_out, m_b_a_out, m_conv_b_w, m_w_b_out, m_w_out, m_final_g, v_meta_tokens, v_norm_g, v_w_in, v_conv_a_w, v_conv_a_b, v_ln_a_g, v_ln_a_b, v_w_a_out, v_b_a_out, v_conv_b_w, v_w_b_out, v_w_out, v_final_g):
    seq = x.shape[1]
    assert x.shape == (1, seq, D_MODEL) and seq % TILE == 0 and w_in.shape == (1, D_MODEL, COLS)
    n_tiles = seq // TILE + 1
    tp = n_tiles * TILE
    pos = jnp.stack([lax.axis_index("x"), lax.axis_index("y"), lax.axis_index("c")]).astype(jnp.int32)
    me = 4 * pos[0] + 2 * pos[1] + pos[2]
    x2d = x[0]
    tgt2d = loss_target[0]

    small = jnp.concatenate([meta_tokens, _pad_rows(conv_a_w[0], 32), _pad_rows(conv_b_w[0], SUBLANES)], axis=0)
    final_g2 = final_g.reshape(1, D_MODEL)

    w_out_shards = [w[0].astype(BF16) for w in (w_a_out, w_b_out, w_out)]
    h_t, proj, meta_tile, small_params, w_in_all, *w_out_all = _gather_norm_proj(
        pos, x2d, small[None], norm_g, w_in[0].astype(BF16), w_out_shards, 3)
    small_params = small_params.transpose(1, 0, 2).reshape(small.shape[0], D_MODEL)
    conv_a_full, conv_b_full = small_params[N_META:N_META + 32], small_params[N_META + 32:]
    w_out_all = [w.reshape(D_MODEL, D_MODEL) for w in w_out_all]
    w_out_all_t = [w.T for w in w_out_all]
    dproj, ds1, lhs, rhs, small_a = _fused_pass(
        proj, x2d, tgt2d, meta_tile, conv_a_full, conv_a_b, ln_a_g, ln_a_b, b_a_out, conv_b_full, final_g2,
        w_out_all[0], w_out_all[1], w_out_all[2], w_out_all_t[0], w_out_all_t[1], w_out_all_t[2], n_tiles)
    k_tile = tp // 3
    gw_far, small_a_all = _grad_w_in_half(pos, h_t, dproj, k_tile, True, [("all", (small_a[None],))], "grad_w_in_far",
                                          narrow=True)
    sems, sent, landing, token = _start_exchanges([("sibling_half", (gw_far,))], "rs_far_start")
    gw_out = _grad_w_out(lhs, rhs, k_tile, token).reshape(N_DEV, 3 * ROWS_OUT, D_MODEL)
    (their_in,) = _wait_exchanges([("sibling_half", 1)], sems, sent, landing, gw_out, "rs_far_wait")
    sems_o, sent_o, landing_o, token = _start_exchanges([("sibling", (gw_out,))], "rs_out_start")
    gw_near, parts_in = _grad_w_in_half(pos, h_t, dproj, k_tile, False, [], "grad_w_in_near", after=token,
                                        add_to=their_in)
    sems_i, sent_i, landing_i, token = _start_exchanges([("chips_by_relation", (parts_in,))], "rs_chips_in_start")
    gw_out, their_out = _wait_exchanges([("sibling", 1)], sems_o, sent_o, landing_o, token, "rs_out_wait",
                                        keep_sources=True)
    parts_out = _chip_partial(pos, gw_out, their_out, (1, 2, 3), BF16, ROWS_OUT, "rs_parts_w_out")
    sems_o, sent_o, landing_o, token = _start_exchanges([("chips", (parts_out,))], "rs_chips_out_start")
    grad_x, small_b = _input_bwd(dproj, ds1, x2d, meta_tile, norm_g + token[0, 0], w_in_all, min(512, seq))

    sems_s, sent_s, landing_s, token = _start_exchanges([("all", (small_b[None],))], "gather_small_grads_start")
    (land_in,) = _wait_exchanges([("chips_by_relation", 1)], sems_i, sent_i, landing_i, token, "rs_chips_in_wait")
    (res_in,) = _adamw_sharded(pos, gw_near, None, land_in, [(w_in[0], m_w_in[0], v_w_in[0])], 128, "adamw_w_in")
    (land_out,) = _wait_exchanges([("chips", 1)], sems_o, sent_o, landing_o, res_in[0], "rs_chips_out_wait")
    res_out = _adamw_sharded(
        pos, gw_out, their_out, land_out,
        [(w_a_out[0], m_w_a_out[0], v_w_a_out[0]), (w_b_out[0], m_w_b_out[0], v_w_b_out[0]),
         (w_out[0], m_w_out[0], v_w_out[0])], ROWS_OUT, "adamw_w_out")
    (small_b_all,) = _wait_exchanges([("all", 1)], sems_s, sent_s, landing_s, res_out[2][0], "gather_small_grads_wait")
    small_grads = [small_a_all, small_b_all]
    small_cols = [lax.dynamic_slice_in_dim(g, me * LANES, LANES, axis=2) for g in small_grads]
    params = [
        (1, ROW_META, N_META, True, meta_tokens, m_meta_tokens, v_meta_tokens),
        (1, ROW_NORM_G, 1, False, norm_g, m_norm_g, v_norm_g),
        (0, ROW_CONV_A_W, CONV_A, True, conv_a_w[0], m_conv_a_w[0], v_conv_a_w[0]),
        (0, ROW_CONV_A_B, 1, False, conv_a_b, m_conv_a_b, v_conv_a_b),
        (0, ROW_LN_G, 1, False, ln_a_g, m_ln_a_g, v_ln_a_g),
        (0, ROW_LN_B, 1, False, ln_a_b, m_ln_a_b, v_ln_a_b),
        (0, ROW_B_A_OUT, 1, False, b_a_out, m_b_a_out, v_b_a_out),
        (0, ROW_CONV_B_W, CONV_B, True, conv_b_w[0], m_conv_b_w[0], v_conv_b_w[0]),
        (0, ROW_FINAL_G, 1, False, final_g2, m_final_g.reshape(1, D_MODEL), v_final_g.reshape(1, D_MODEL)),
    ]
    res_small = _adamw_small(small_grads, small_cols, params)
    loss = res_small[-1][0, 0]

    def small_res(p, kind, shape):
        return res_small[4 * p + kind].reshape(shape)

    per_weight = []
    for kind in range(4):
        per_weight.append([
            small_res(0, kind, meta_tokens.shape),
            small_res(1, kind, norm_g.shape),
            res_in[kind].reshape(w_in.shape),
            small_res(2, kind, conv_a_w.shape),
            small_res(3, kind, conv_a_b.shape),
            small_res(4, kind, ln_a_g.shape),
            small_res(5, kind, ln_a_b.shape),
            res_out[0][kind].reshape(w_a_out.shape),
            small_res(6, kind, b_a_out.shape),
            small_res(7, kind, conv_b_w.shape),
            res_out[1][kind].reshape(w_b_out.shape),
            res_out[2][kind].reshape(w_out.shape),
            small_res(8, kind, final_g.shape),
        ])
    return (loss, grad_x.reshape(x.shape), *per_weight[0], *per_weight[1], *per_weight[2], *per_weight[3])
```

```python
import functools

import jax
import jax.numpy as jnp
from jax import lax
from jax.experimental import pallas as pl
from jax.experimental.pallas import tpu as pltpu

D_MODEL = 1024
N_META = 16
N_DEV = 8
D_IN = 9 * D_MODEL
COLS = D_IN // N_DEV
ROWS_OUT = D_MODEL // N_DEV
CONV_A = 31
CONV_B = 3
EPS = 1e-6

ADAM_LR = 0.001
ADAM_B1 = 0.9
ADAM_B2 = 0.999
ADAM_EPS = 1e-08
ADAM_WD = 0.01
ADAM_STEP = 10

TILE = 128
LANES = 128
N_CHUNK = D_MODEL // LANES
HALO = 32
SUBLANES = 8
VMEM_LIMIT = 56 * 1024 * 1024

ROW_FINAL_G, ROW_B_A_OUT, ROW_LN_G, ROW_LN_B, ROW_CONV_A_B, ROW_LOSS = 0, 1, 2, 3, 4, 5
ROW_CONV_A_W, ROW_CONV_B_W, SMALL_A_ROWS = 8, 40, 48
ROW_NORM_G, ROW_META, SMALL_B_ROWS = 0, 8, 24

MESH = pl.DeviceIdType.MESH
_ANY = pl.BlockSpec(memory_space=pl.ANY)
_VMEM = pl.BlockSpec(memory_space=pltpu.VMEM)
BF16 = jnp.bfloat16
F32 = jnp.float32


def _resident(shape):
    return pl.BlockSpec(shape, lambda *_: (0,) * len(shape), pipeline_mode=pl.Buffered(1))


def _sigmoid(v):
    return jax.nn.sigmoid(v)


def _dot(a, b):
    return jnp.dot(a, b, preferred_element_type=F32)


def _dot_nt(a, b):
    return lax.dot_general(a, b, (((1,), (1,)), ((), ())), preferred_element_type=F32)


def _dot_tn(a, b):
    return lax.dot_general(a, b, (((0,), (0,)), ((), ())), preferred_element_type=F32)


def _colsum(v):
    return jnp.sum(v, axis=0, keepdims=True)


def _rowmean(v):
    parts = [v[:, LANES * c:LANES * (c + 1)] for c in range(v.shape[1] // LANES)]
    return jnp.sum(functools.reduce(jnp.add, parts), axis=-1, keepdims=True) * (1.0 / v.shape[1])


def _fold8(v):
    parts = [v[SUBLANES * g:SUBLANES * (g + 1)] for g in range(v.shape[0] // SUBLANES)]
    return functools.reduce(jnp.add, parts)


def _sibling_copies(srcs, dsts, send_sems, recv_sems):
    x, y, c = lax.axis_index("x"), lax.axis_index("y"), lax.axis_index("c")
    return [pltpu.make_async_remote_copy(
        src_ref=src.at[2 * q + (1 - c)], dst_ref=dst.at[q],
        send_sem=send_sems.at[4 * a + q], recv_sem=recv_sems.at[4 * a + q],
        device_id=(x, y, 1 - c), device_id_type=MESH)
        for a, (src, dst) in enumerate(zip(srcs, dsts)) for q in range(4)]


def _chip_copies(srcs, dsts, send_sems, recv_sems):
    x, y, c = lax.axis_index("x"), lax.axis_index("y"), lax.axis_index("c")
    targets = [(x, 1 - y, c), (1 - x, y, c), (1 - x, 1 - y, c)]
    return [pltpu.make_async_remote_copy(
        src_ref=src.at[k], dst_ref=dst.at[k],
        send_sem=send_sems.at[3 * a + k], recv_sem=recv_sems.at[3 * a + k],
        device_id=targets[k], device_id_type=MESH)
        for a, (src, dst) in enumerate(zip(srcs, dsts)) for k in range(3)]


def _sibling_half_copies(srcs, dsts, send_sems, recv_sems):
    x, y, c = lax.axis_index("x"), lax.axis_index("y"), lax.axis_index("c")
    return [pltpu.make_async_remote_copy(
        src_ref=src.at[q], dst_ref=dst.at[q],
        send_sem=send_sems.at[4 * a + q], recv_sem=recv_sems.at[4 * a + q],
        device_id=(x, y, 1 - c), device_id_type=MESH)
        for a, (src, dst) in enumerate(zip(srcs, dsts)) for q in range(4)]


def _all_copies(srcs, dsts, send_sems, recv_sems):
    x, y, c = lax.axis_index("x"), lax.axis_index("y"), lax.axis_index("c")
    mine = 4 * x + 2 * y + c
    copies = []
    for a, (src, dst) in enumerate(zip(srcs, dsts)):
        copies.append(pltpu.make_async_copy(src.at[0], dst.at[mine], send_sems.at[N_DEV * a]))
        for k in range(1, N_DEV):
            copies.append(pltpu.make_async_remote_copy(
                src_ref=src.at[0], dst_ref=dst.at[mine],
                send_sem=send_sems.at[N_DEV * a + k], recv_sem=recv_sems.at[N_DEV * a + k],
                device_id=(x ^ (k >> 2), y ^ ((k >> 1) & 1), c ^ (k & 1)), device_id_type=MESH))
    return copies


def _chip_copies_by_relation(srcs, dsts, send_sems, recv_sems):
    return _chip_copies([src.at[pl.ds(1, 3)] for src in srcs], dsts, send_sems, recv_sems)


_EXCHANGES = {"sibling": (4, _sibling_copies, 4), "sibling_half": (4, _sibling_half_copies, 4),
              "chips": (3, _chip_copies, 3), "chips_by_relation": (3, _chip_copies_by_relation, 3),
              "all": (N_DEV, _all_copies, N_DEV)}


def _exchange_shapes(kind, arrays):
    per_array, _, slots = _EXCHANGES[kind]
    out_shape = [jax.ShapeDtypeStruct((slots,) + a.shape[1:], a.dtype) for a in arrays]
    sems = [pltpu.SemaphoreType.DMA((per_array * len(arrays),))] * 2
    return out_shape, sems


def _ride_shapes(rides):
    shapes, sems = [], []
    for kind, arrays in rides:
        ride_shapes, ride_sems = _exchange_shapes(kind, arrays)
        shapes += ride_shapes
        sems += ride_sems
    return shapes, sems


def _riding(body, n_in, n_out, rides, is_first, is_last):
    counts = [len(arrays) for _, arrays in rides]
    n_arr = sum(counts)

    def wrapped(*refs):
        ins, srcs = refs[:n_in], refs[n_in:n_in + n_arr]
        outs = refs[n_in + n_arr:n_in + n_arr + n_out]
        dsts = refs[n_in + n_arr + n_out:n_in + 2 * n_arr + n_out]
        first_sem = len(refs) - 2 * len(rides)
        scratch, sems = refs[n_in + 2 * n_arr + n_out:first_sem], refs[first_sem:]

        def copies():
            made, at = [], 0
            for r, ((kind, _), n) in enumerate(zip(rides, counts)):
                made += _EXCHANGES[kind][1](srcs[at:at + n], dsts[at:at + n], sems[2 * r], sems[2 * r + 1])
                at += n
            return made

        @pl.when(is_first())
        def _():
            for cp in copies():
                cp.start()

        body(*ins, *outs, *scratch)

        @pl.when(is_last())
        def _():
            for cp in copies():
                cp.wait()

    return wrapped


_HBM = pl.BlockSpec(memory_space=pltpu.HBM)
_SEM = pl.BlockSpec(memory_space=pltpu.SEMAPHORE)
_FLOWS = pltpu.SideEffectType.DATAFLOW_SIDE_EFFECTING


def _start_exchanges(rides, name):
    arrays = [a for _, group in rides for a in group]
    shapes, sems = _ride_shapes(rides)
    n_arr, n_sem = len(arrays), len(sems)

    def body(*refs):
        srcs, lands = refs[:n_arr], refs[n_arr:2 * n_arr]
        sem_refs, token = refs[2 * n_arr:2 * n_arr + n_sem], refs[-1]
        at = 0
        for r, (kind, group) in enumerate(rides):
            n = len(group)
            for cp in _EXCHANGES[kind][1](srcs[at:at + n], lands[at:at + n], sem_refs[2 * r], sem_refs[2 * r + 1]):
                cp.start()
            at += n
        token[...] = jnp.zeros(token.shape, token.dtype)

    in_hbm = [pltpu.HBM(a.shape, a.dtype) for a in arrays]
    land_hbm = [pltpu.HBM(sh.shape, sh.dtype) for sh in shapes]
    res = pl.pallas_call(
        body, name=name,
        out_shape=(*sems, *in_hbm, *land_hbm, jax.ShapeDtypeStruct((SUBLANES, LANES), F32)),
        in_specs=[_HBM] * (2 * n_arr), out_specs=(*[_SEM] * n_sem, *[_HBM] * (2 * n_arr), _VMEM),
        input_output_aliases={i: n_sem + i for i in range(2 * n_arr)},
        compiler_params=pltpu.CompilerParams(has_side_effects=_FLOWS),
    )(*[pltpu.with_memory_space_constraint(a, pltpu.HBM) for a in arrays],
      *[pltpu.with_memory_space_constraint(lax.empty(sh.shape, sh.dtype), pltpu.HBM) for sh in shapes])
    return res[:n_sem], res[n_sem:n_sem + n_arr], res[n_sem + n_arr:n_sem + 2 * n_arr], res[-1]


def _wait_exchanges(kinds, sems, arrays, lands, after, name, keep_sources=False):
    n_arr, n_sem = len(arrays), len(sems)

    def body(*refs):
        srcs, dsts = refs[:n_arr], refs[n_arr:2 * n_arr]
        sem_refs = refs[2 * n_arr:2 * n_arr + n_sem]
        at = 0
        for r, (kind, n) in enumerate(kinds):
            for cp in _EXCHANGES[kind][1](srcs[at:at + n], dsts[at:at + n], sem_refs[2 * r], sem_refs[2 * r + 1]):
                cp.wait()
            at += n

    hbm = [pltpu.HBM(a.shape, a.dtype) for a in (*arrays, *lands)]
    return pl.pallas_call(
        body, name=name, out_shape=tuple(hbm),
        in_specs=[_HBM] * (2 * n_arr) + [_SEM] * n_sem + [_ANY], out_specs=tuple([_HBM] * (2 * n_arr)),
        input_output_aliases={i: i for i in range(2 * n_arr)},
        compiler_params=pltpu.CompilerParams(has_side_effects=_FLOWS),
    )(*arrays, *lands, *sems, after)[0 if keep_sources else n_arr:]


def _chip_partial(pos, mine, theirs, relations, out_dtype, row_tile, name):
    n_slots, m, n = mine.shape
    q0 = relations[0]

    def chip_of(qi, pos_ref):
        q = qi + q0
        return pos_ref[0] ^ (q >> 1), pos_ref[1] ^ (q & 1)

    def mine_map(qi, t, pos_ref):
        px, py = chip_of(qi, pos_ref)
        return (4 * px + 2 * py + pos_ref[2] if n_slots == N_DEV else 2 * px + py), t, 0

    def theirs_map(qi, t, pos_ref):
        px, py = chip_of(qi, pos_ref)
        return 2 * px + py, t, 0

    def body(pos_ref, a_ref, b_ref, o_ref):
        o_ref[...] = (a_ref[...] + b_ref[...]).astype(out_dtype)

    return pl.pallas_call(
        body, name=name,
        out_shape=jax.ShapeDtypeStruct((len(relations), m, n), out_dtype),
        grid_spec=pltpu.PrefetchScalarGridSpec(
            num_scalar_prefetch=1, grid=(len(relations), m // row_tile),
            in_specs=[pl.BlockSpec((None, row_tile, n), mine_map), pl.BlockSpec((None, row_tile, n), theirs_map)],
            out_specs=pl.BlockSpec((None, row_tile, n), lambda qi, t, pos_ref: (qi, t, 0))),
        compiler_params=pltpu.CompilerParams(dimension_semantics=("arbitrary", "arbitrary")),
    )(pos, mine, theirs)


PARTS = ((0, 512), (512, 640))


def _gather_norm_proj(pos, x2d, small_shard, norm_g, w_in_shard, w_out_shards, n_chunk):
    seq = x2d.shape[0]
    n_tiles = seq // TILE + 1
    tp = n_tiles * TILE
    n_parts = len(PARTS)
    widest = max(width for _, width in PARTS)
    units = [(s, u) for s in range(2) for u in range(n_parts)]
    units += [(s, u) for u in range(n_parts) for s in (2, 3, 5, 6)] + [(s, u) for u in range(n_parts) for s in (4, 7)]
    n_units = len(units)
    over_ici = [m for m, (s, _) in enumerate(units) if s in (2, 3)]
    handled_at = {m: m - 1 for m in range(1, n_units)}
    handled_at.update({m: over_ici[0] - 1 + i for i, m in enumerate(over_ici)})
    assert all(step < m for m, step in handled_at.items())
    n_steps = n_tiles + n_units
    chunk = tp // n_chunk

    def body(pos_ref, x_ref, g_ref, small_ref, win_ref, wa_ref, wb_ref, wo_ref,
             ht_ref, proj_ref, meta_ref, small_all, win_all, wa_all, wb_all, wo_all,
             h_all, wbuf, rbuf, small_buf, send_sems, recv_sems, local_sems, small_send, small_recv):
        g = pl.program_id(0)
        x, y, c = lax.axis_index("x"), lax.axis_index("y"), lax.axis_index("c")
        me, sibling = (x, y, c), (x, y, 1 - c)
        chips = [(1 - x, y), (x, 1 - y), (1 - x, 1 - y)]
        shards = (win_ref, wa_ref, wb_ref, wo_ref)
        gathered = (win_all, wa_all, wb_all, wo_all)
        n_arrays = len(shards)
        blocks = [me, sibling] + [(*chip, c) for chip in chips] + [(*chip, 1 - c) for chip in chips]

        def index(block):
            px, py, pc = block
            return 4 * px + 2 * py + pc

        def part(ref, a, u):
            return ref.at[:, pl.ds(PARTS[u][0], PARTS[u][1])] if a == 0 else ref

        def slot(a, block, u):
            return part(gathered[a].at[index(block)], a, u)

        def sem(a, k, u):
            return n_parts * k + u if a == 0 else 7 * n_parts + 7 * (a - 1) + k

        def copy(a, k, block, to, u=0, from_shard=False):
            return pltpu.make_async_remote_copy(
                src_ref=part(shards[a], a, u) if from_shard else slot(a, block, u), dst_ref=slot(a, block, u),
                send_sem=send_sems.at[sem(a, k, u)], recv_sem=recv_sems.at[sem(a, k, u)],
                device_id=to, device_id_type=MESH)

        def keep(a):
            return pltpu.make_async_copy(shards[a], gathered[a].at[index(me)], local_sems.at[a])

        def load(m):
            s, u = units[m]
            src = part(win_ref, 0, u) if s == 0 else slot(0, blocks[s], u)
            return pltpu.make_async_copy(src, wbuf.at[m % 2, :, 0:PARTS[u][1]], local_sems.at[n_arrays + m % 2])

        def store(m):
            s, u = units[m]
            col0 = pl.multiple_of(index(blocks[s]) * COLS + PARTS[u][0], LANES)
            return pltpu.make_async_copy(rbuf.at[m % 2, :, 0:PARTS[u][1]],
                                         proj_ref.at[:, pl.ds(col0, PARTS[u][1])], local_sems.at[n_arrays + 2 + m % 2])

        def by_x(a, u):
            return u == 0 if a == 0 else a < 3

        def relay(a, u=0):
            src, to = (blocks[3], blocks[2]) if by_x(a, u) else (blocks[2], blocks[3])
            return copy(a, 3, src, to, u)

        def arrive(m):
            s, u = units[m]
            if s == 1:
                copy(0, 0, sibling, me, u).wait_recv()
            elif 2 <= s <= 4:
                copy(0, s - 1, blocks[s], me, u).wait_recv()
                copy(0, s + 2, blocks[s], sibling, u).start()
                if s < 4 and by_x(0, u) == (s == 3):
                    relay(0, u).start()
            elif s >= 5:
                copy(0, s - 1, blocks[s], me, u).wait_recv()

        def pass_on_out(j):
            for a in range(1, 4):
                copy(a, j + 1, blocks[2 + j], me).wait_recv()
                copy(a, j + 4, blocks[2 + j], sibling).start()
                if j < 2 and by_x(a, 0) == (j == 1):
                    relay(a).start()

        targets = [sibling, blocks[2], blocks[3]]
        relays_started = max(handled_at[m] for m, (s, u) in enumerate(units) if s in (2, 3) and by_x(0, u) == (s == 3))

        def small_copies():
            return _all_copies([small_ref], [small_all], small_send, small_recv)

        @pl.when(g == 0)
        def _():
            for cp in small_copies():
                cp.start()
            for a in range(n_arrays):
                keep(a).start()
            for u in range(n_parts):
                for k, to in enumerate(targets):
                    copy(0, k, me, to, u, from_shard=True).start()
            for a in range(1, 4):
                copy(a, 0, me, sibling, from_shard=True).start()
            load(0).start()

        @pl.when(g == n_tiles - 2)
        def _():
            for cp in small_copies():
                cp.wait()
            fetch = pltpu.make_async_copy(small_all, small_buf, local_sems.at[n_arrays + 4])
            fetch.start()
            fetch.wait()
            meta_ref[0:TILE - N_META, :] = jnp.zeros((TILE - N_META, D_MODEL), F32)
            meta_ref[TILE - N_META:TILE, :] = jnp.concatenate([small_buf[d, 0:N_META, :] for d in range(N_DEV)], axis=1)

        @pl.when(g < n_tiles)
        def _():
            s0 = jnp.where(g == n_tiles - 1, meta_ref[...], x_ref[...])
            r = lax.rsqrt(_rowmean(s0 * s0) + EPS)
            h32 = (s0 * r) * g_ref[...]
            ht_ref[...] = h32.T.astype(BF16)
            h_all[pl.ds(pl.multiple_of(g * TILE, TILE), TILE), :] = h32.astype(BF16)

        for m in range(n_units):
            @pl.when(g == n_tiles + m)
            def _(m=m):
                load(m).wait()
                for later in range(m + 1, n_units):
                    if handled_at[later] == m:
                        arrive(later)
                if m + 1 < n_units:
                    load(m + 1).start()
                if m == relays_started:
                    for a in range(1, 4):
                        for k in (1, 2):
                            copy(a, k, me, targets[k], from_shard=True).start()
                if m == n_units - 2:
                    pass_on_out(0)
                    pass_on_out(1)
                if m >= 2:
                    store(m - 2).wait()

        m_now = jnp.maximum(g - n_tiles, 0)
        u_now = functools.reduce(jnp.add, [jnp.where(m_now == m, u, 0) for m, (_, u) in enumerate(units)])
        for u, (_, width) in enumerate(PARTS):
            @pl.when((g >= n_tiles) & (u_now == u))
            def _(width=width):
                w = wbuf[m_now % 2, :, 0:width]
                for r in range(n_chunk):
                    rbuf[m_now % 2, r * chunk:(r + 1) * chunk, 0:width] = _dot(h_all[r * chunk:(r + 1) * chunk, :], w)

        for m in range(n_units):
            @pl.when(g == n_tiles + m)
            def _(m=m):
                store(m).start()

        @pl.when(g == n_steps - 1)
        def _():
            pass_on_out(2)
            store(n_units - 2).wait()
            store(n_units - 1).wait()
            for a in range(1, 4):
                copy(a, 0, sibling, me).wait_recv()
                for j in range(3):
                    copy(a, 4 + j, blocks[5 + j], me).wait_recv()
            for a in range(n_arrays):
                for u in range(n_parts if a == 0 else 1):
                    for k, to in enumerate(targets):
                        copy(a, k, me, to, u, from_shard=True).wait_send()
                    relay(a, u).wait_send()
                    for j in range(3):
                        copy(a, 4 + j, blocks[2 + j], sibling, u).wait_send()
                keep(a).wait()

    n_x = n_tiles - 1
    return pl.pallas_call(
        body, name="gather_norm_proj",
        out_shape=[jax.ShapeDtypeStruct((D_MODEL, tp), BF16), jax.ShapeDtypeStruct((tp, D_IN), F32),
                   jax.ShapeDtypeStruct((TILE, D_MODEL), F32), jax.ShapeDtypeStruct((N_DEV,) + small_shard.shape[1:], F32),
                   jax.ShapeDtypeStruct((N_DEV,) + w_in_shard.shape, BF16)]
                  + [jax.ShapeDtypeStruct((N_DEV,) + w.shape, BF16) for w in w_out_shards],
        grid_spec=pltpu.PrefetchScalarGridSpec(
            num_scalar_prefetch=1, grid=(n_steps,),
            in_specs=[pl.BlockSpec((TILE, D_MODEL), lambda g, pos_ref: (jnp.minimum(g, n_x - 1), 0)),
                      _VMEM, _ANY, _ANY, _ANY, _ANY, _ANY],
            out_specs=[pl.BlockSpec((D_MODEL, TILE), lambda g, pos_ref: (0, jnp.minimum(g, n_tiles - 1))),
                       _ANY, _VMEM, _ANY, _ANY, _ANY, _ANY, _ANY],
            scratch_shapes=[pltpu.VMEM((tp, D_MODEL), BF16), pltpu.VMEM((2, D_MODEL, widest), BF16),
                            pltpu.VMEM((2, tp, widest), F32), pltpu.VMEM((N_DEV,) + small_shard.shape[1:], F32),
                            pltpu.SemaphoreType.DMA((7 * n_parts + 21,)), pltpu.SemaphoreType.DMA((7 * n_parts + 21,)),
                            pltpu.SemaphoreType.DMA((9,)),
                            pltpu.SemaphoreType.DMA((N_DEV,)), pltpu.SemaphoreType.DMA((N_DEV,))]),
        compiler_params=pltpu.CompilerParams(dimension_semantics=("arbitrary",), vmem_limit_bytes=VMEM_LIMIT),
    )(pos, x2d, norm_g, small_shard, w_in_shard, *w_out_shards)


C_AVAL, C_AGLU, C_AZ, C_BB, C_BC, C_BX, C_BZ, C_GA, C_GB = (k * D_MODEL for k in range(9))
S_AZ, S_BB, S_BZ, S_GA, S_GB = (k * D_MODEL for k in range(5))


def _fused_pass(proj, x2d, tgt2d, meta_tile, conv_a_w, conv_a_b, ln_a_g, ln_a_b, b_a_out, conv_b_w, final_g,
                w_a, w_b, w_o, w_a_t, w_b_t, w_o_t, n_tiles):
    T = TILE
    tp = n_tiles * T
    inv_d = 1.0 / D_MODEL

    def block_of(tile):
        return jnp.where(tile == 0, n_tiles - 1, tile - 1)

    def cur(i):
        return block_of(jnp.minimum(i, n_tiles - 1))

    def prev(i):
        return block_of(jnp.clip(i - 1, 0, n_tiles - 1))

    def xblk(i):
        return jnp.maximum(jnp.minimum(i, n_tiles - 1) - 1, 0)

    def body(proj_ref, aprev, cprev, x_ref, tgt_ref, meta_ref, caw_ref, cab_ref, lng_ref, lnb_ref, bao_ref, cbw_ref,
             fg_ref, wa_ref, wb_ref, wo_ref, wat_ref, wbt_ref, wot_ref,
             dproj_ref, ds1_ref, lhs_ref, rhs_ref, small_ref,
             ua0_buf, cb_buf, dua1_buf, dc3_buf, stage, ua1_buf, c3_buf,
             dpa_buf, dpb_buf, dcaw8, dcbw8, shift_buf):
        i = pl.program_id(0)
        this, before = i % 2, 1 - i % 2

        @pl.when(i == 0)
        def _init():
            for buf in (ua0_buf, cb_buf, dua1_buf, dc3_buf, dcaw8, dcbw8):
                buf[...] = jnp.zeros(buf.shape, buf.dtype)
            small_ref[...] = jnp.zeros(small_ref.shape, F32)

        @pl.when(i >= 1)
        def _emit_stage():
            dproj_ref[:, C_AZ:C_BC] = stage[:, S_AZ:S_BZ]
            dproj_ref[:, C_BZ:D_IN] = stage[:, S_BZ:S_GB + D_MODEL]

        @pl.when(i < n_tiles)
        def _front():
            def conv_chunk(cc, carry):
                c0 = pl.multiple_of(cc * LANES, LANES)
                lanes = pl.ds(c0, LANES)

                def col(base):
                    return pl.ds(pl.multiple_of(base + cc * LANES, LANES), LANES)

                ua0 = proj_ref[:, col(C_AVAL)] * _sigmoid(proj_ref[:, col(C_AGLU)])
                ua0_buf[this, 0:HALO, lanes] = ua0_buf[before, T:T + HALO, lanes]
                ua0_buf[this, HALO:HALO + T, lanes] = ua0
                acc = jnp.broadcast_to(cab_ref[:, lanes], (T, LANES))
                lead = HALO - (CONV_A - 1)
                for r in range(SUBLANES):
                    taps = [k for k in range(CONV_A) if (k + lead) % SUBLANES == r]
                    rows = T + SUBLANES * max((k + lead) // SUBLANES for k in taps)
                    if r:
                        shift_buf[r, 0:rows, :] = ua0_buf[this, pl.ds(r, rows), lanes]
                    for k in taps:
                        q = (k + lead) // SUBLANES
                        if r:
                            win = shift_buf[r, SUBLANES * q:SUBLANES * q + T, :]
                        else:
                            win = ua0_buf[this, pl.ds(SUBLANES * q, T), lanes]
                        acc = acc + caw_ref[k:k + 1, lanes] * win
                ua1_buf[:, lanes] = acc
                cb = proj_ref[:, col(C_BC)] * proj_ref[:, col(C_BX)]
                cb_buf[this, 0:SUBLANES, lanes] = cb_buf[before, T:T + SUBLANES, lanes]
                cb_buf[this, SUBLANES:SUBLANES + T, lanes] = cb
                lead_b = SUBLANES - (CONV_B - 1)
                acc3 = cbw_ref[0:1, lanes] * cb_buf[this, pl.ds(lead_b, T), lanes]
                for k in range(1, CONV_B):
                    acc3 = acc3 + cbw_ref[k:k + 1, lanes] * cb_buf[this, pl.ds(lead_b + k, T), lanes]
                c3_buf[:, lanes] = acc3
                return carry

            lax.fori_loop(0, N_CHUNK, conv_chunk, 0)

            ua1 = ua1_buf[...]
            xc = ua1 - _rowmean(ua1)
            rstd = lax.rsqrt(_rowmean(xc * xc) + EPS)
            xhat = xc * rstd
            ua2 = xhat * lng_ref[...] + lnb_ref[...]
            sg2 = _sigmoid(ua2)
            ua3 = ua2 * sg2
            a_z = proj_ref[:, C_AZ:C_AZ + D_MODEL]
            sz = _sigmoid(a_z)
            silu_az = a_z * sz
            lhs_ref[0] = (ua3 * silu_az).astype(BF16)
            b_z = proj_ref[:, C_BZ:C_BZ + D_MODEL]
            sbz = _sigmoid(b_z)
            silu_bz = b_z * sbz
            b_b = proj_ref[:, C_BB:C_BB + D_MODEL]
            c3 = c3_buf[...]
            ub = b_b * c3
            lhs_ref[1] = (ub * silu_bz).astype(BF16)

            ya = _dot(lhs_ref[0], wa_ref[...]) + bao_ref[...]
            yb = _dot(lhs_ref[1], wb_ref[...])
            sga = _sigmoid(proj_ref[:, C_GA:C_GA + D_MODEL])
            sgb = _sigmoid(proj_ref[:, C_GB:C_GB + D_MODEL])
            m_b = (sga * ya + sgb * yb).astype(BF16)
            lhs_ref[2] = m_b
            s0 = jnp.where(i == 0, meta_ref[...], x_ref[...])
            s1 = s0 + _dot(m_b, wo_ref[...])
            r1 = lax.rsqrt(_rowmean(s1 * s1) + EPS)
            y = (s1 * r1) * fg_ref[...]
            is_token = (i >= 1).astype(F32)
            err = (y - tgt_ref[...]) * is_token
            small_ref[ROW_LOSS:ROW_LOSS + 1, :] += (0.5 * inv_d) * _colsum(err * err)
            dy = err * inv_d
            small_ref[ROW_FINAL_G:ROW_FINAL_G + 1, :] += _colsum(dy * (s1 * r1))
            gy = dy * fg_ref[...]
            ds1 = r1 * gy - s1 * ((r1 * r1 * r1) * _rowmean(gy * s1))
            ds1_ref[...] = ds1
            ds1_b = ds1.astype(BF16)
            rhs_ref[2] = ds1_b
            dm = _dot(ds1_b, wot_ref[...])
            dya = dm * sga
            dyb = dm * sgb
            stage[:, S_GA:S_GA + D_MODEL] = (dya * ya * (1.0 - sga)).astype(BF16)
            stage[:, S_GB:S_GB + D_MODEL] = (dyb * yb * (1.0 - sgb)).astype(BF16)
            small_ref[ROW_B_A_OUT:ROW_B_A_OUT + 1, :] += _colsum(dya)
            dya_b = dya.astype(BF16)
            dyb_b = dyb.astype(BF16)
            rhs_ref[0] = dya_b
            rhs_ref[1] = dyb_b
            dpa_buf[...] = _dot(dya_b, wat_ref[...])
            dpb_buf[...] = _dot(dyb_b, wbt_ref[...])

            dpa = dpa_buf[...]
            stage[:, S_AZ:S_AZ + D_MODEL] = (dpa * ua3 * (sz + silu_az * (1.0 - sz))).astype(BF16)
            dua2 = dpa * silu_az * (sg2 + ua3 * (1.0 - sg2))
            small_ref[ROW_LN_G:ROW_LN_G + 1, :] += _colsum(dua2 * xhat)
            small_ref[ROW_LN_B:ROW_LN_B + 1, :] += _colsum(dua2)
            dxh = dua2 * lng_ref[...]
            dua1 = rstd * (dxh - _rowmean(dxh) - xhat * _rowmean(dxh * xhat))
            small_ref[ROW_CONV_A_B:ROW_CONV_A_B + 1, :] += _colsum(dua1)
            dua1_buf[this, 0:T, :] = dua1
            dua1_buf[before, T:T + HALO, :] = dua1[0:HALO]
            dpb = dpb_buf[...]
            stage[:, S_BZ:S_BZ + D_MODEL] = (dpb * ub * (sbz + silu_bz * (1.0 - sbz))).astype(BF16)
            dub = dpb * silu_bz
            stage[:, S_BB:S_BB + D_MODEL] = (dub * c3).astype(BF16)
            dc3 = dub * b_b
            dc3_buf[this, 0:T, :] = dc3
            dc3_buf[before, T:T + SUBLANES, :] = dc3[0:SUBLANES]

        @pl.when(i == n_tiles)
        def _no_later_tile():
            dua1_buf[before, T:T + HALO, :] = jnp.zeros((HALO, D_MODEL), F32)
            dc3_buf[before, T:T + SUBLANES, :] = jnp.zeros((SUBLANES, D_MODEL), F32)

        @pl.when(i >= 1)
        def _lagged():
            def convt_chunk(cc, carry):
                c0 = pl.multiple_of(cc * LANES, LANES)
                lanes = pl.ds(c0, LANES)

                def col(base):
                    return pl.ds(pl.multiple_of(base + cc * LANES, LANES), LANES)

                ua0 = ua0_buf[before, HALO:HALO + T, lanes]
                acc = jnp.zeros((T, LANES), F32)
                for r in range(SUBLANES):
                    shifts = [j for j in range(CONV_A) if j % SUBLANES == r]
                    rows = T + shifts[-1] - r
                    if r:
                        shift_buf[r, 0:rows, :] = dua1_buf[before, pl.ds(r, rows), lanes]
                    for j in shifts:
                        k = CONV_A - 1 - j
                        if r:
                            later = shift_buf[r, j - r:j - r + T, :]
                        else:
                            later = dua1_buf[before, pl.ds(j, T), lanes]
                        acc = acc + caw_ref[k:k + 1, lanes] * later
                        dcaw8[SUBLANES * k:SUBLANES * (k + 1), lanes] += _fold8(ua0 * later)
                a_val = aprev[:, col(0)]
                sg = _sigmoid(aprev[:, col(D_MODEL)])
                dproj_ref[:, col(C_AVAL)] = (acc * sg).astype(BF16)
                dproj_ref[:, col(C_AGLU)] = (acc * a_val * (sg * (1.0 - sg))).astype(BF16)

                cb = cb_buf[before, SUBLANES:SUBLANES + T, lanes]
                acc3 = jnp.zeros((T, LANES), F32)
                for j in range(CONV_B):
                    k = CONV_B - 1 - j
                    later = dc3_buf[before, pl.ds(j, T), lanes]
                    acc3 = acc3 + cbw_ref[k:k + 1, lanes] * later
                    dcbw8[SUBLANES * k:SUBLANES * (k + 1), lanes] += _fold8(cb * later)
                dproj_ref[:, col(C_BC)] = (acc3 * cprev[:, col(D_MODEL)]).astype(BF16)
                dproj_ref[:, col(C_BX)] = (acc3 * cprev[:, col(0)]).astype(BF16)
                return carry

            lax.fori_loop(0, N_CHUNK, convt_chunk, 0)

        @pl.when(i == n_tiles)
        def _finish():
            for k in range(CONV_A):
                small_ref[ROW_CONV_A_W + k:ROW_CONV_A_W + k + 1, :] = _colsum(dcaw8[SUBLANES * k:SUBLANES * (k + 1), :])
            for k in range(CONV_B):
                small_ref[ROW_CONV_B_W + k:ROW_CONV_B_W + k + 1, :] = _colsum(dcbw8[SUBLANES * k:SUBLANES * (k + 1), :])

    pair = 2 * D_MODEL
    return pl.pallas_call(
        body, name="fused_pass", grid=(n_tiles + 1,),
        out_shape=[
            jax.ShapeDtypeStruct((tp, D_IN), BF16),
            jax.ShapeDtypeStruct((tp, D_MODEL), F32),
            jax.ShapeDtypeStruct((3, tp, D_MODEL), BF16),
            jax.ShapeDtypeStruct((3, tp, D_MODEL), BF16),
            jax.ShapeDtypeStruct((SMALL_A_ROWS, D_MODEL), F32),
        ],
        in_specs=[
            pl.BlockSpec((T, D_IN), lambda i: (cur(i), 0)),
            pl.BlockSpec((T, pair), lambda i: (prev(i), C_AVAL // pair)),
            pl.BlockSpec((T, pair), lambda i: (prev(i), C_BC // pair)),
            pl.BlockSpec((T, D_MODEL), lambda i: (xblk(i), 0)),
            pl.BlockSpec((T, D_MODEL), lambda i: (xblk(i), 0)),
            _VMEM, _VMEM, _VMEM, _VMEM, _VMEM, _VMEM, _VMEM, _VMEM,
            *[_resident((D_MODEL, D_MODEL)) for _ in range(6)],
        ],
        out_specs=[
            pl.BlockSpec((T, D_IN), lambda i: (prev(i), 0)),
            pl.BlockSpec((T, D_MODEL), lambda i: (cur(i), 0)),
            pl.BlockSpec((3, T, D_MODEL), lambda i: (0, cur(i), 0)),
            pl.BlockSpec((3, T, D_MODEL), lambda i: (0, cur(i), 0)),
            _VMEM,
        ],
        scratch_shapes=[
            pltpu.VMEM((2, HALO + T, D_MODEL), F32),
            pltpu.VMEM((2, SUBLANES + T, D_MODEL), F32),
            pltpu.VMEM((2, T + HALO, D_MODEL), F32),
            pltpu.VMEM((2, T + SUBLANES, D_MODEL), F32),
            pltpu.VMEM((T, 5 * D_MODEL), BF16),
            pltpu.VMEM((T, D_MODEL), F32),
            pltpu.VMEM((T, D_MODEL), F32),
            pltpu.VMEM((T, D_MODEL), F32),
            pltpu.VMEM((T, D_MODEL), F32),
            pltpu.VMEM((32 * SUBLANES, D_MODEL), F32),
            pltpu.VMEM((SUBLANES * SUBLANES, D_MODEL), F32),
            pltpu.VMEM((SUBLANES, T + HALO, LANES), F32),
        ],
        compiler_params=pltpu.CompilerParams(dimension_semantics=("arbitrary",), vmem_limit_bytes=VMEM_LIMIT),
    )(proj, proj, proj, x2d, tgt2d, meta_tile, conv_a_w, conv_a_b, ln_a_g, ln_a_b, b_a_out, conv_b_w, final_g,
      w_a, w_b, w_o, w_a_t, w_b_t, w_o_t)


def _input_bwd(dproj, ds1, x2d, meta_tile, norm_g, w_in_all, row_tile, after):
    seq = x2d.shape[0]
    n_steps = seq // row_tile
    meta_block = seq // TILE

    def backward(dp_ref, ds1_ref, s0_ref, g_ref, w_ref, out_ref, vec_ref):
        dh = _dot_nt(dp_ref[:, 0:COLS], w_ref[0])
        for j in range(1, N_DEV):
            dh = dh + _dot_nt(dp_ref[:, j * COLS:(j + 1) * COLS], w_ref[j])
        s0v = s0_ref[...]
        r = lax.rsqrt(_rowmean(s0v * s0v) + EPS)
        gh = dh * g_ref[...]
        out_ref[...] = ds1_ref[...] + r * gh - s0v * ((r * r * r) * _rowmean(gh * s0v))
        vec_ref[ROW_NORM_G:ROW_NORM_G + 1, :] += _colsum(dh * (s0v * r))

    def body(dp_ref, ds1_ref, x_ref, dpm_ref, ds1m_ref, meta_ref, g_ref, w_ref, after_ref, gx_ref, small_ref, gmeta_buf):
        t = pl.program_id(0)

        @pl.when(t == 0)
        def _():
            small_ref[...] = jnp.zeros(small_ref.shape, F32)

        backward(dp_ref, ds1_ref, x_ref, g_ref, w_ref, gx_ref, small_ref)

        @pl.when(t == n_steps - 1)
        def _():
            backward(dpm_ref, ds1m_ref, meta_ref, g_ref, w_ref, gmeta_buf, small_ref)
            small_ref[ROW_META:ROW_META + N_META, :] = gmeta_buf[TILE - N_META:TILE, :]

    return pl.pallas_call(
        body, name="input_bwd", grid=(n_steps,),
        out_shape=[jax.ShapeDtypeStruct(x2d.shape, F32), jax.ShapeDtypeStruct((SMALL_B_ROWS, D_MODEL), F32)],
        in_specs=[pl.BlockSpec((row_tile, D_IN), lambda t: (t, 0)),
                  pl.BlockSpec((row_tile, D_MODEL), lambda t: (t, 0)),
                  pl.BlockSpec((row_tile, D_MODEL), lambda t: (t, 0)),
                  pl.BlockSpec((TILE, D_IN), lambda t: (meta_block, 0)),
                  pl.BlockSpec((TILE, D_MODEL), lambda t: (meta_block, 0)),
                  _VMEM, _VMEM, _resident((N_DEV, D_MODEL, COLS)), _ANY],
        out_specs=[pl.BlockSpec((row_tile, D_MODEL), lambda t: (t, 0)), _VMEM],
        scratch_shapes=[pltpu.VMEM((TILE, D_MODEL), F32)],
        compiler_params=pltpu.CompilerParams(dimension_semantics=("arbitrary",), vmem_limit_bytes=VMEM_LIMIT),
    )(dproj, ds1, x2d, dproj, ds1, meta_tile, norm_g, w_in_all, after)


def _grad_w_in_half(pos, h_t, dproj, k_tile, other_side, rides, name, after=None, add_to=None, narrow=False):
    tp = h_t.shape[1]
    n_k = tp // k_tile
    order = [] if after is None else [after]
    summing = add_to is not None
    assert not (summing and narrow)

    def column_block(q, k, pos_ref):
        return k, 2 * q + (1 - pos_ref[2] if other_side else pos_ref[2])

    def body(pos_ref, h_ref, dp_ref, *refs):
        acc = refs[-1]

        @pl.when(pl.program_id(1) == 0)
        def _():
            acc[...] = refs[0][...].astype(F32) if summing else jnp.zeros(acc.shape, F32)

        acc[...] += _dot(h_ref[...], dp_ref[...])

        if summing or narrow:
            @pl.when(pl.program_id(1) == n_k - 1)
            def _():
                refs[-2][...] = acc[...].astype(BF16)

        if summing:
            @pl.when((pl.program_id(1) == n_k - 1) & (pl.program_id(0) == 2 * pos_ref[0] + pos_ref[1]))
            def _():
                refs[-3][...] = acc[...]

    ride = [a for _, arrays in rides for a in arrays]
    n_arr = len(ride)
    ride_shapes, ride_sems = _ride_shapes(rides)
    block = (None, D_MODEL, COLS)
    extra_in = [add_to] if summing else []
    extra_in_specs = [pl.BlockSpec(block, lambda q, k, pos_ref: (q, 0, 0))] if summing else []
    extra_out = [jax.ShapeDtypeStruct((4, D_MODEL, COLS), BF16)] if summing else []
    extra_out_specs = [pl.BlockSpec(block, lambda q, k, pos_ref: (q ^ (2 * pos_ref[0] + pos_ref[1]), 0, 0))] \
        if summing else []
    body = _riding(body, 3 + len(extra_in) + len(order), 1 + len(extra_out), rides,
                   lambda: (pl.program_id(0) == 0) & (pl.program_id(1) == 0),
                   lambda: (pl.program_id(0) == 3) & (pl.program_id(1) == n_k - 1))
    return pl.pallas_call(
        body, name=name,
        out_shape=[jax.ShapeDtypeStruct((1 if summing else 4, D_MODEL, COLS), BF16 if narrow else F32)]
        + extra_out + ride_shapes,
        grid_spec=pltpu.PrefetchScalarGridSpec(
            num_scalar_prefetch=1, grid=(4, n_k),
            in_specs=[pl.BlockSpec((D_MODEL, k_tile), lambda q, k, pos_ref: (0, k)),
                      pl.BlockSpec((k_tile, COLS), column_block)] + extra_in_specs + [_ANY] * (len(order) + n_arr),
            out_specs=[pl.BlockSpec(block, lambda q, k, pos_ref: (0 if summing else q, 0, 0))]
            + extra_out_specs + [_ANY] * n_arr,
            scratch_shapes=([pltpu.VMEM((D_MODEL, COLS), F32)] if summing or narrow else []) + ride_sems),
        compiler_params=pltpu.CompilerParams(dimension_semantics=("arbitrary", "arbitrary"),
                                             vmem_limit_bytes=VMEM_LIMIT),
    )(pos, h_t, dproj, *extra_in, *order, *ride)


def _grad_w_out(lhs, rhs, k_tile, after):
    tp = lhs.shape[1]

    def body(a_ref, b_ref, after_ref, o_ref):
        @pl.when(pl.program_id(1) == 0)
        def _():
            o_ref[...] = jnp.zeros(o_ref.shape, F32)

        o_ref[...] += _dot_tn(a_ref[...], b_ref[...]).reshape(N_DEV, ROWS_OUT, D_MODEL)

    return pl.pallas_call(
        body, name="grad_w_out", grid=(3, tp // k_tile),
        out_shape=jax.ShapeDtypeStruct((N_DEV, 3, ROWS_OUT, D_MODEL), F32),
        in_specs=[pl.BlockSpec((None, k_tile, D_MODEL), lambda w, k: (w, k, 0)),
                  pl.BlockSpec((None, k_tile, D_MODEL), lambda w, k: (w, k, 0)), _ANY],
        out_specs=pl.BlockSpec((N_DEV, None, ROWS_OUT, D_MODEL), lambda w, k: (0, w, 0, 0)),
        compiler_params=pltpu.CompilerParams(dimension_semantics=("arbitrary", "arbitrary"),
                                             vmem_limit_bytes=VMEM_LIMIT),
    )(lhs, rhs, after)


def _adamw_math(w, g, m, v):
    m = ADAM_B1 * m + (1.0 - ADAM_B1) * g
    v = ADAM_B2 * v + (1.0 - ADAM_B2) * (g * g)
    m_hat = m / (1.0 - ADAM_B1 ** ADAM_STEP)
    v_hat = v / (1.0 - ADAM_B2 ** ADAM_STEP)
    delta = -ADAM_LR * (m_hat / (jnp.sqrt(v_hat) + ADAM_EPS) + ADAM_WD * w)
    return delta, m, v


def _adamw_sharded(pos, mine, theirs, landed, weights, row_tile, name, after=None):
    order = [] if after is None else [after]
    rows, n = weights[0][0].shape
    n_slots = mine.shape[0]
    per_shard = rows // row_tile
    assert per_shard == 1 or len(weights) == 1

    def mine_map(j, t, pos_ref):
        chip = 2 * pos_ref[0] + pos_ref[1]
        return {N_DEV: 2 * chip + pos_ref[2], 4: chip, 1: 0}[n_slots], j * per_shard + t, 0

    def theirs_map(j, t, pos_ref):
        return 2 * pos_ref[0] + pos_ref[1], j * per_shard + t, 0

    def body(pos_ref, mine_ref, *refs):
        if theirs is not None:
            g = mine_ref[...] + refs[0][...]
            refs = refs[1:]
        else:
            g = mine_ref[...]
        land_ref, refs = refs[0], refs[1:]
        ins, outs = refs[:3 * len(weights)], refs[3 * len(weights) + len(order):]
        for k in range(3):
            g = g + land_ref[k].astype(F32)
        for j in range(len(weights)):
            @pl.when(pl.program_id(0) == j)
            def _(j=j):
                w_ref, m_ref, v_ref = ins[3 * j:3 * j + 3]
                delta, m_new, v_new = _adamw_math(w_ref[...], g, m_ref[...], v_ref[...])
                for ref, val in zip(outs[4 * j:4 * j + 4], (g, delta, m_new, v_new)):
                    ref[...] = val

    tile = pl.BlockSpec((row_tile, n), lambda j, t, pos_ref: (t, 0))
    res = pl.pallas_call(
        body, name=name,
        out_shape=[jax.ShapeDtypeStruct((rows, n), F32)] * (4 * len(weights)),
        grid_spec=pltpu.PrefetchScalarGridSpec(
            num_scalar_prefetch=1, grid=(len(weights), per_shard),
            in_specs=[pl.BlockSpec((None, row_tile, n), mine_map)]
            + ([pl.BlockSpec((None, row_tile, n), theirs_map)] if theirs is not None else [])
            + [pl.BlockSpec((3, row_tile, n), lambda j, t, pos_ref: (0, j * per_shard + t, 0))]
            + [tile] * (3 * len(weights)) + [_ANY] * len(order),
            out_specs=[tile] * (4 * len(weights))),
        compiler_params=pltpu.CompilerParams(dimension_semantics=("arbitrary", "arbitrary")),
    )(pos, mine, *([theirs] if theirs is not None else []), landed, *[a for wmv in weights for a in wmv], *order)
    return [res[4 * j:4 * j + 4] for j in range(len(weights))]


def _adamw_small(gathered, gathered_cols, params, row_by_row=()):
    n_par, n_src = len(params), len(gathered)

    def body(*refs):
        g_refs, gc_refs = refs[:n_src], refs[n_src:2 * n_src]
        ins = refs[2 * n_src:2 * n_src + 3 * n_par]
        outs = refs[2 * n_src + 3 * n_par:]
        loss_ref = outs[4 * n_par]

        def reduced(ref, row, n_rows):
            g = ref[0, row:row + n_rows, :]
            for d in range(1, N_DEV):
                g = g + ref[d, row:row + n_rows, :]
            return g

        for p, (src, row, n_rows, sharded, _, _, _) in enumerate(params):
            g = reduced((gc_refs if sharded else g_refs)[src], row, n_rows)
            w_ref, m_ref, v_ref = ins[3 * p:3 * p + 3]
            delta, m_new, v_new = _adamw_math(w_ref[...], g, m_ref[...], v_ref[...])
            for kind, res in enumerate((g, delta, m_new, v_new)):
                if p in row_by_row:
                    for k in range(n_rows):
                        outs[4 * p + kind][k] = res[k:k + 1, :]
                else:
                    outs[4 * p + kind][...] = res
        loss = jnp.sum(reduced(g_refs[0], ROW_LOSS, 1), axis=1, keepdims=True)
        loss_ref[...] = jnp.broadcast_to(loss, loss_ref.shape)

    out_shape = []
    for p, (_, _, n_rows, _, w, _, _) in enumerate(params):
        out_shape += [jax.ShapeDtypeStruct((n_rows, 1, LANES) if p in row_by_row else w.shape, F32)] * 4
    out_shape.append(jax.ShapeDtypeStruct((1, LANES), F32))
    flat = [a for (_, _, _, _, w, m, v) in params for a in (w, m, v)]
    return pl.pallas_call(
        body, name="adamw_small", out_shape=out_shape,
        in_specs=[_VMEM] * (2 * n_src + len(flat)), out_specs=[_VMEM] * len(out_shape),
    )(*gathered, *gathered_cols, *flat)


def _pad_rows(a, rows):
    return jnp.concatenate([a, jnp.zeros((rows - a.shape[0], a.shape[1]), a.dtype)], axis=0)


def kernel(x, meta_tokens, norm_g, w_in, conv_a_w, conv_a_b, ln_a_g, ln_a_b, w_a_out, b_a_out, conv_b_w, w_b_out, w_out, final_g, loss_target, m_meta_tokens, m_norm_g, m_w_in, m_conv_a_w, m_conv_a_b, m_ln_a_g, m_ln_a_b, m_w_a_out, m_b_a_out, m_conv_b_w, m_w_b_out, m_w_out, m_final_g, v_meta_tokens, v_norm_g, v_w_in, v_conv_a_w, v_conv_a_b, v_ln_a_g, v_ln_a_b, v_w_a_out, v_b_a_out, v_conv_b_w, v_w_b_out, v_w_out, v_final_g):
    seq = x.shape[1]
    assert x.shape == (1, seq, D_MODEL) and seq % TILE == 0 and w_in.shape == (1, D_MODEL, COLS)
    n_tiles = seq // TILE + 1
    tp = n_tiles * TILE
    pos = jnp.stack([lax.axis_index("x"), lax.axis_index("y"), lax.axis_index("c")]).astype(jnp.int32)
    me = 4 * pos[0] + 2 * pos[1] + pos[2]
    x2d = x[0]
    tgt2d = loss_target[0]

    small = jnp.concatenate([meta_tokens, _pad_rows(conv_a_w[0], 32), _pad_rows(conv_b_w[0], SUBLANES)], axis=0)
    final_g2 = final_g.reshape(1, D_MODEL)

    w_out_shards = [w[0].astype(BF16) for w in (w_a_out, w_b_out, w_out)]
    h_t, proj, meta_tile, small_params, w_in_all, *w_out_all = _gather_norm_proj(
        pos, x2d, small[None], norm_g, w_in[0].astype(BF16), w_out_shards, 3)
    small_params = small_params.transpose(1, 0, 2).reshape(small.shape[0], D_MODEL)
    conv_a_full, conv_b_full = small_params[N_META:N_META + 32], small_params[N_META + 32:]
    w_out_all = [w.reshape(D_MODEL, D_MODEL) for w in w_out_all]
    w_out_all_t = [w.T for w in w_out_all]
    dproj, ds1, lhs, rhs, small_a = _fused_pass(
        proj, x2d, tgt2d, meta_tile, conv_a_full, conv_a_b, ln_a_g, ln_a_b, b_a_out, conv_b_full, final_g2,
        w_out_all[0], w_out_all[1], w_out_all[2], w_out_all_t[0], w_out_all_t[1], w_out_all_t[2], n_tiles)
    k_tile = tp // 3
    gw_far, small_a_all = _grad_w_in_half(pos, h_t, dproj, k_tile, True, [("all", (small_a[None],))], "grad_w_in_far",
                                          narrow=True)
    sems, sent, landing, token = _start_exchanges([("sibling_half", (gw_far,))], "rs_far_start")
    gw_out = _grad_w_out(lhs, rhs, k_tile, token).reshape(N_DEV, 3 * ROWS_OUT, D_MODEL)
    (their_in,) = _wait_exchanges([("sibling_half", 1)], sems, sent, landing, gw_out, "rs_far_wait")
    sems_o, sent_o, landing_o, token = _start_exchanges([("sibling", (gw_out,))], "rs_out_start")
    gw_near, parts_in = _grad_w_in_half(pos, h_t, dproj, k_tile, False, [], "grad_w_in_near", after=token,
                                        add_to=their_in)
    sems_i, sent_i, landing_i, token = _start_exchanges([("chips_by_relation", (parts_in,))], "rs_chips_in_start")
    gw_out, their_out = _wait_exchanges([("sibling", 1)], sems_o, sent_o, landing_o, token, "rs_out_wait",
                                        keep_sources=True)
    parts_out = _chip_partial(pos, gw_out, their_out, (1, 2, 3), BF16, ROWS_OUT, "rs_parts_w_out")
    sems_o, sent_o, landing_o, token = _start_exchanges([("chips", (parts_out,))], "rs_chips_out_start")
    grad_x, small_b = _input_bwd(dproj, ds1, x2d, meta_tile, norm_g, w_in_all, min(512, seq), token)

    sems_s, sent_s, landing_s, token = _start_exchanges([("all", (small_b[None],))], "gather_small_grads_start")
    (land_in,) = _wait_exchanges([("chips_by_relation", 1)], sems_i, sent_i, landing_i, token, "rs_chips_in_wait")
    (res_in,) = _adamw_sharded(pos, gw_near, None, land_in, [(w_in[0], m_w_in[0], v_w_in[0])], 128, "adamw_w_in")
    (land_out,) = _wait_exchanges([("chips", 1)], sems_o, sent_o, landing_o, res_in[0], "rs_chips_out_wait")
    res_out = _adamw_sharded(
        pos, gw_out, their_out, land_out,
        [(w_a_out[0], m_w_a_out[0], v_w_a_out[0]), (w_b_out[0], m_w_b_out[0], v_w_b_out[0]),
         (w_out[0], m_w_out[0], v_w_out[0])], ROWS_OUT, "adamw_w_out")
    (small_b_all,) = _wait_exchanges([("all", 1)], sems_s, sent_s, landing_s, res_out[2][0], "gather_small_grads_wait")
    small_grads = [small_a_all, small_b_all]
    small_cols = [lax.dynamic_slice_in_dim(g, me * LANES, LANES, axis=2) for g in small_grads]
    params = [
        (1, ROW_META, N_META, True, meta_tokens, m_meta_tokens, v_meta_tokens),
        (1, ROW_NORM_G, 1, False, norm_g, m_norm_g, v_norm_g),
        (0, ROW_CONV_A_W, CONV_A, True, conv_a_w[0], m_conv_a_w[0], v_conv_a_w[0]),
        (0, ROW_CONV_A_B, 1, False, conv_a_b, m_conv_a_b, v_conv_a_b),
        (0, ROW_LN_G, 1, False, ln_a_g, m_ln_a_g, v_ln_a_g),
        (0, ROW_LN_B, 1, False, ln_a_b, m_ln_a_b, v_ln_a_b),
        (0, ROW_B_A_OUT, 1, False, b_a_out, m_b_a_out, v_b_a_out),
        (0, ROW_CONV_B_W, CONV_B, True, conv_b_w[0], m_conv_b_w[0], v_conv_b_w[0]),
        (0, ROW_FINAL_G, 1, False, final_g2, m_final_g.reshape(1, D_MODEL), v_final_g.reshape(1, D_MODEL)),
    ]
    conv_weights = (2, 7)
    res_small = _adamw_small(small_grads, small_cols, params, conv_weights)
    loss = res_small[-1][0, 0]

    def small_res(p, kind, shape):
        res = res_small[4 * p + kind]
        return res.transpose(1, 0, 2) if p in conv_weights else res.reshape(shape)

    per_weight = []
    for kind in range(4):
        per_weight.append([
            small_res(0, kind, meta_tokens.shape),
            small_res(1, kind, norm_g.shape),
            res_in[kind].reshape(w_in.shape),
            small_res(2, kind, conv_a_w.shape),
            small_res(3, kind, conv_a_b.shape),
            small_res(4, kind, ln_a_g.shape),
            small_res(5, kind, ln_a_b.shape),
            res_out[0][kind].reshape(w_a_out.shape),
            small_res(6, kind, b_a_out.shape),
            small_res(7, kind, conv_b_w.shape),
            res_out[1][kind].reshape(w_b_out.shape),
            res_out[2][kind].reshape(w_out.shape),
            small_res(8, kind, final_g.shape),
        ])
    return (loss, grad_x.reshape(x.shape), *per_weight[0], *per_weight[1], *per_weight[2], *per_weight[3])
```

```python
import functools

import jax
import jax.numpy as jnp
from jax import lax
from jax.experimental import pallas as pl
from jax.experimental.pallas import tpu as pltpu

D_MODEL = 1024
N_META = 16
N_DEV = 8
D_IN = 9 * D_MODEL
COLS = D_IN // N_DEV
ROWS_OUT = D_MODEL // N_DEV
CONV_A = 31
CONV_B = 3
EPS = 1e-6

ADAM_LR = 0.001
ADAM_B1 = 0.9
ADAM_B2 = 0.999
ADAM_EPS = 1e-08
ADAM_WD = 0.01
ADAM_STEP = 10

TILE = 128
LANES = 128
N_CHUNK = D_MODEL // LANES
HALO = 32
SUBLANES = 8
VMEM_LIMIT = 56 * 1024 * 1024

ROW_FINAL_G, ROW_B_A_OUT, ROW_LN_G, ROW_LN_B, ROW_CONV_A_B, ROW_LOSS = 0, 1, 2, 3, 4, 5
ROW_CONV_A_W, ROW_CONV_B_W, SMALL_A_ROWS = 8, 40, 48
ROW_NORM_G, ROW_META, SMALL_B_ROWS = 0, 8, 24

MESH = pl.DeviceIdType.MESH
_ANY = pl.BlockSpec(memory_space=pl.ANY)
_VMEM = pl.BlockSpec(memory_space=pltpu.VMEM)
BF16 = jnp.bfloat16
F32 = jnp.float32


def _resident(shape):
    return pl.BlockSpec(shape, lambda *_: (0,) * len(shape), pipeline_mode=pl.Buffered(1))


def _sigmoid(v):
    return jax.nn.sigmoid(v)


def _dot(a, b):
    return jnp.dot(a, b, preferred_element_type=F32)


def _dot_nt(a, b):
    return lax.dot_general(a, b, (((1,), (1,)), ((), ())), preferred_element_type=F32)


def _dot_tn(a, b):
    return lax.dot_general(a, b, (((0,), (0,)), ((), ())), preferred_element_type=F32)


def _colsum(v):
    return jnp.sum(v, axis=0, keepdims=True)


def _rowmean(v):
    parts = [v[:, LANES * c:LANES * (c + 1)] for c in range(v.shape[1] // LANES)]
    return jnp.sum(functools.reduce(jnp.add, parts), axis=-1, keepdims=True) * (1.0 / v.shape[1])


def _fold8(v):
    parts = [v[SUBLANES * g:SUBLANES * (g + 1)] for g in range(v.shape[0] // SUBLANES)]
    return functools.reduce(jnp.add, parts)


def _sibling_copies(srcs, dsts, send_sems, recv_sems):
    x, y, c = lax.axis_index("x"), lax.axis_index("y"), lax.axis_index("c")
    return [pltpu.make_async_remote_copy(
        src_ref=src.at[2 * q + (1 - c)], dst_ref=dst.at[q],
        send_sem=send_sems.at[4 * a + q], recv_sem=recv_sems.at[4 * a + q],
        device_id=(x, y, 1 - c), device_id_type=MESH)
        for a, (src, dst) in enumerate(zip(srcs, dsts)) for q in range(4)]


def _chip_copies(srcs, dsts, send_sems, recv_sems):
    x, y, c = lax.axis_index("x"), lax.axis_index("y"), lax.axis_index("c")
    targets = [(x, 1 - y, c), (1 - x, y, c), (1 - x, 1 - y, c)]
    return [pltpu.make_async_remote_copy(
        src_ref=src.at[k], dst_ref=dst.at[k],
        send_sem=send_sems.at[3 * a + k], recv_sem=recv_sems.at[3 * a + k],
        device_id=targets[k], device_id_type=MESH)
        for a, (src, dst) in enumerate(zip(srcs, dsts)) for k in range(3)]


def _sibling_half_copies(srcs, dsts, send_sems, recv_sems):
    x, y, c = lax.axis_index("x"), lax.axis_index("y"), lax.axis_index("c")
    return [pltpu.make_async_remote_copy(
        src_ref=src.at[q], dst_ref=dst.at[q],
        send_sem=send_sems.at[4 * a + q], recv_sem=recv_sems.at[4 * a + q],
        device_id=(x, y, 1 - c), device_id_type=MESH)
        for a, (src, dst) in enumerate(zip(srcs, dsts)) for q in range(4)]


def _all_copies(srcs, dsts, send_sems, recv_sems):
    x, y, c = lax.axis_index("x"), lax.axis_index("y"), lax.axis_index("c")
    mine = 4 * x + 2 * y + c
    copies = []
    for a, (src, dst) in enumerate(zip(srcs, dsts)):
        copies.append(pltpu.make_async_copy(src.at[0], dst.at[mine], send_sems.at[N_DEV * a]))
        for k in range(1, N_DEV):
            copies.append(pltpu.make_async_remote_copy(
                src_ref=src.at[0], dst_ref=dst.at[mine],
                send_sem=send_sems.at[N_DEV * a + k], recv_sem=recv_sems.at[N_DEV * a + k],
                device_id=(x ^ (k >> 2), y ^ ((k >> 1) & 1), c ^ (k & 1)), device_id_type=MESH))
    return copies


def _chip_copies_by_relation(srcs, dsts, send_sems, recv_sems):
    return _chip_copies([src.at[pl.ds(1, 3)] for src in srcs], dsts, send_sems, recv_sems)


_EXCHANGES = {"sibling": (4, _sibling_copies, 4), "sibling_half": (4, _sibling_half_copies, 4),
              "chips": (3, _chip_copies, 3), "chips_by_relation": (3, _chip_copies_by_relation, 3),
              "all": (N_DEV, _all_copies, N_DEV)}


def _exchange_shapes(kind, arrays):
    per_array, _, slots = _EXCHANGES[kind]
    out_shape = [jax.ShapeDtypeStruct((slots,) + a.shape[1:], a.dtype) for a in arrays]
    sems = [pltpu.SemaphoreType.DMA((per_array * len(arrays),))] * 2
    return out_shape, sems


def _ride_shapes(rides):
    shapes, sems = [], []
    for kind, arrays in rides:
        ride_shapes, ride_sems = _exchange_shapes(kind, arrays)
        shapes += ride_shapes
        sems += ride_sems
    return shapes, sems


def _riding(body, n_in, n_out, rides, is_first, is_last):
    counts = [len(arrays) for _, arrays in rides]
    n_arr = sum(counts)

    def wrapped(*refs):
        ins, srcs = refs[:n_in], refs[n_in:n_in + n_arr]
        outs = refs[n_in + n_arr:n_in + n_arr + n_out]
        dsts = refs[n_in + n_arr + n_out:n_in + 2 * n_arr + n_out]
        first_sem = len(refs) - 2 * len(rides)
        scratch, sems = refs[n_in + 2 * n_arr + n_out:first_sem], refs[first_sem:]

        def copies():
            made, at = [], 0
            for r, ((kind, _), n) in enumerate(zip(rides, counts)):
                made += _EXCHANGES[kind][1](srcs[at:at + n], dsts[at:at + n], sems[2 * r], sems[2 * r + 1])
                at += n
            return made

        @pl.when(is_first())
        def _():
            for cp in copies():
                cp.start()

        body(*ins, *outs, *scratch)

        @pl.when(is_last())
        def _():
            for cp in copies():
                cp.wait()

    return wrapped


_HBM = pl.BlockSpec(memory_space=pltpu.HBM)
_SEM = pl.BlockSpec(memory_space=pltpu.SEMAPHORE)
_FLOWS = pltpu.SideEffectType.DATAFLOW_SIDE_EFFECTING


def _start_exchanges(rides, name):
    arrays = [a for _, group in rides for a in group]
    shapes, sems = _ride_shapes(rides)
    n_arr, n_sem = len(arrays), len(sems)

    def body(*refs):
        srcs, lands = refs[:n_arr], refs[n_arr:2 * n_arr]
        sem_refs, token = refs[2 * n_arr:2 * n_arr + n_sem], refs[-1]
        at = 0
        for r, (kind, group) in enumerate(rides):
            n = len(group)
            for cp in _EXCHANGES[kind][1](srcs[at:at + n], lands[at:at + n], sem_refs[2 * r], sem_refs[2 * r + 1]):
                cp.start()
            at += n
        token[...] = jnp.zeros(token.shape, token.dtype)

    in_hbm = [pltpu.HBM(a.shape, a.dtype) for a in arrays]
    land_hbm = [pltpu.HBM(sh.shape, sh.dtype) for sh in shapes]
    res = pl.pallas_call(
        body, name=name,
        out_shape=(*sems, *in_hbm, *land_hbm, jax.ShapeDtypeStruct((SUBLANES, LANES), F32)),
        in_specs=[_HBM] * (2 * n_arr), out_specs=(*[_SEM] * n_sem, *[_HBM] * (2 * n_arr), _VMEM),
        input_output_aliases={i: n_sem + i for i in range(2 * n_arr)},
        compiler_params=pltpu.CompilerParams(has_side_effects=_FLOWS),
    )(*[pltpu.with_memory_space_constraint(a, pltpu.HBM) for a in arrays],
      *[pltpu.with_memory_space_constraint(lax.empty(sh.shape, sh.dtype), pltpu.HBM) for sh in shapes])
    return res[:n_sem], res[n_sem:n_sem + n_arr], res[n_sem + n_arr:n_sem + 2 * n_arr], res[-1]


def _wait_exchanges(kinds, sems, arrays, lands, after, name, keep_sources=False):
    n_arr, n_sem = len(arrays), len(sems)

    def body(*refs):
        srcs, dsts = refs[:n_arr], refs[n_arr:2 * n_arr]
        sem_refs = refs[2 * n_arr:2 * n_arr + n_sem]
        at = 0
        for r, (kind, n) in enumerate(kinds):
            for cp in _EXCHANGES[kind][1](srcs[at:at + n], dsts[at:at + n], sem_refs[2 * r], sem_refs[2 * r + 1]):
                cp.wait()
            at += n

    hbm = [pltpu.HBM(a.shape, a.dtype) for a in (*arrays, *lands)]
    return pl.pallas_call(
        body, name=name, out_shape=tuple(hbm),
        in_specs=[_HBM] * (2 * n_arr) + [_SEM] * n_sem + [_ANY], out_specs=tuple([_HBM] * (2 * n_arr)),
        input_output_aliases={i: i for i in range(2 * n_arr)},
        compiler_params=pltpu.CompilerParams(has_side_effects=_FLOWS),
    )(*arrays, *lands, *sems, after)[0 if keep_sources else n_arr:]


def _chip_partial(pos, mine, theirs, relations, out_dtype, row_tile, name):
    n_slots, m, n = mine.shape
    q0 = relations[0]

    def chip_of(qi, pos_ref):
        q = qi + q0
        return pos_ref[0] ^ (q >> 1), pos_ref[1] ^ (q & 1)

    def mine_map(qi, t, pos_ref):
        px, py = chip_of(qi, pos_ref)
        return (4 * px + 2 * py + pos_ref[2] if n_slots == N_DEV else 2 * px + py), t, 0

    def theirs_map(qi, t, pos_ref):
        px, py = chip_of(qi, pos_ref)
        return 2 * px + py, t, 0

    def body(pos_ref, a_ref, b_ref, o_ref):
        o_ref[...] = (a_ref[...] + b_ref[...]).astype(out_dtype)

    return pl.pallas_call(
        body, name=name,
        out_shape=jax.ShapeDtypeStruct((len(relations), m, n), out_dtype),
        grid_spec=pltpu.PrefetchScalarGridSpec(
            num_scalar_prefetch=1, grid=(len(relations), m // row_tile),
            in_specs=[pl.BlockSpec((None, row_tile, n), mine_map), pl.BlockSpec((None, row_tile, n), theirs_map)],
            out_specs=pl.BlockSpec((None, row_tile, n), lambda qi, t, pos_ref: (qi, t, 0))),
        compiler_params=pltpu.CompilerParams(dimension_semantics=("arbitrary", "arbitrary")),
    )(pos, mine, theirs)


PARTS = ((0, 512), (512, 640))


def _gather_norm_proj(pos, x2d, small_shard, norm_g, w_in_shard, w_out_shards, n_chunk):
    seq = x2d.shape[0]
    n_tiles = seq // TILE + 1
    tp = n_tiles * TILE
    n_parts = len(PARTS)
    widest = max(width for _, width in PARTS)
    units = [(s, u) for s in range(2) for u in range(n_parts)]
    units += [(s, u) for u in range(n_parts) for s in (2, 3, 5, 6)] + [(s, u) for u in range(n_parts) for s in (4, 7)]
    n_units = len(units)
    over_ici = [m for m, (s, _) in enumerate(units) if s in (2, 3)]
    handled_at = {m: m - 1 for m in range(1, n_units)}
    handled_at.update({m: over_ici[0] - 1 + i for i, m in enumerate(over_ici)})
    assert all(step < m for m, step in handled_at.items())
    n_steps = n_tiles + n_units
    chunk = tp // n_chunk

    def body(pos_ref, x_ref, g_ref, small_ref, win_ref, wa_ref, wb_ref, wo_ref,
             ht_ref, proj_ref, meta_ref, small_all, win_all, wa_all, wb_all, wo_all,
             h_all, wbuf, rbuf, small_buf, send_sems, recv_sems, local_sems, small_send, small_recv):
        g = pl.program_id(0)
        x, y, c = lax.axis_index("x"), lax.axis_index("y"), lax.axis_index("c")
        me, sibling = (x, y, c), (x, y, 1 - c)
        chips = [(1 - x, y), (x, 1 - y), (1 - x, 1 - y)]
        shards = (win_ref, wa_ref, wb_ref, wo_ref)
        gathered = (win_all, wa_all, wb_all, wo_all)
        n_arrays = len(shards)
        blocks = [me, sibling] + [(*chip, c) for chip in chips] + [(*chip, 1 - c) for chip in chips]

        def index(block):
            px, py, pc = block
            return 4 * px + 2 * py + pc

        def part(ref, a, u):
            return ref.at[:, pl.ds(PARTS[u][0], PARTS[u][1])] if a == 0 else ref

        def slot(a, block, u):
            return part(gathered[a].at[index(block)], a, u)

        def sem(a, k, u):
            return n_parts * k + u if a == 0 else 7 * n_parts + 7 * (a - 1) + k

        def copy(a, k, block, to, u=0, from_shard=False):
            return pltpu.make_async_remote_copy(
                src_ref=part(shards[a], a, u) if from_shard else slot(a, block, u), dst_ref=slot(a, block, u),
                send_sem=send_sems.at[sem(a, k, u)], recv_sem=recv_sems.at[sem(a, k, u)],
                device_id=to, device_id_type=MESH)

        def keep(a):
            return pltpu.make_async_copy(shards[a], gathered[a].at[index(me)], local_sems.at[a])

        def load(m):
            s, u = units[m]
            src = part(win_ref, 0, u) if s == 0 else slot(0, blocks[s], u)
            return pltpu.make_async_copy(src, wbuf.at[m % 2, :, 0:PARTS[u][1]], local_sems.at[n_arrays + m % 2])

        def store(m):
            s, u = units[m]
            col0 = pl.multiple_of(index(blocks[s]) * COLS + PARTS[u][0], LANES)
            return pltpu.make_async_copy(rbuf.at[m % 2, :, 0:PARTS[u][1]],
                                         proj_ref.at[:, pl.ds(col0, PARTS[u][1])], local_sems.at[n_arrays + 2 + m % 2])

        def by_x(a, u):
            return u == 0 if a == 0 else a < 3

        def relay(a, u=0):
            src, to = (blocks[3], blocks[2]) if by_x(a, u) else (blocks[2], blocks[3])
            return copy(a, 3, src, to, u)

        def arrive(m):
            s, u = units[m]
            if s == 1:
                copy(0, 0, sibling, me, u).wait_recv()
            elif 2 <= s <= 4:
                copy(0, s - 1, blocks[s], me, u).wait_recv()
                copy(0, s + 2, blocks[s], sibling, u).start()
                if s < 4 and by_x(0, u) == (s == 3):
                    relay(0, u).start()
            elif s >= 5:
                copy(0, s - 1, blocks[s], me, u).wait_recv()

        def pass_on_out(j):
            for a in range(1, 4):
                copy(a, j + 1, blocks[2 + j], me).wait_recv()
                copy(a, j + 4, blocks[2 + j], sibling).start()
                if j < 2 and by_x(a, 0) == (j == 1):
                    relay(a).start()

        targets = [sibling, blocks[2], blocks[3]]
        relays_started = max(handled_at[m] for m, (s, u) in enumerate(units) if s in (2, 3) and by_x(0, u) == (s == 3))

        def small_copies():
            return _all_copies([small_ref], [small_all], small_send, small_recv)

        @pl.when(g == 0)
        def _():
            for cp in small_copies():
                cp.start()
            for a in range(n_arrays):
                keep(a).start()
            for u in range(n_parts):
                for k, to in enumerate(targets):
                    copy(0, k, me, to, u, from_shard=True).start()
            for a in range(1, 4):
                copy(a, 0, me, sibling, from_shard=True).start()
            load(0).start()

        @pl.when(g == n_tiles - 2)
        def _():
            for cp in small_copies():
                cp.wait()
            fetch = pltpu.make_async_copy(small_all, small_buf, local_sems.at[n_arrays + 4])
            fetch.start()
            fetch.wait()
            meta_ref[0:TILE - N_META, :] = jnp.zeros((TILE - N_META, D_MODEL), F32)
            meta_ref[TILE - N_META:TILE, :] = jnp.concatenate([small_buf[d, 0:N_META, :] for d in range(N_DEV)], axis=1)

        @pl.when(g < n_tiles)
        def _():
            s0 = jnp.where(g == n_tiles - 1, meta_ref[...], x_ref[...])
            r = lax.rsqrt(_rowmean(s0 * s0) + EPS)
            h32 = (s0 * r) * g_ref[...]
            ht_ref[...] = h32.T.astype(BF16)
            h_all[pl.ds(pl.multiple_of(g * TILE, TILE), TILE), :] = h32.astype(BF16)

        for m in range(n_units):
            @pl.when(g == n_tiles + m)
            def _(m=m):
                load(m).wait()
                for later in range(m + 1, n_units):
                    if handled_at[later] == m:
                        arrive(later)
                if m + 1 < n_units:
                    load(m + 1).start()
                if m == relays_started:
                    for a in range(1, 4):
                        for k in (1, 2):
                            copy(a, k, me, targets[k], from_shard=True).start()
                if m == n_units - 2:
                    pass_on_out(0)
                    pass_on_out(1)
                if m >= 2:
                    store(m - 2).wait()

        m_now = jnp.maximum(g - n_tiles, 0)
        u_now = functools.reduce(jnp.add, [jnp.where(m_now == m, u, 0) for m, (_, u) in enumerate(units)])
        for u, (_, width) in enumerate(PARTS):
            @pl.when((g >= n_tiles) & (u_now == u))
            def _(width=width):
                w = wbuf[m_now % 2, :, 0:width]
                for r in range(n_chunk):
                    rbuf[m_now % 2, r * chunk:(r + 1) * chunk, 0:width] = _dot(h_all[r * chunk:(r + 1) * chunk, :], w)

        for m in range(n_units):
            @pl.when(g == n_tiles + m)
            def _(m=m):
                store(m).start()

        @pl.when(g == n_steps - 1)
        def _():
            pass_on_out(2)
            store(n_units - 2).wait()
            store(n_units - 1).wait()
            for a in range(1, 4):
                copy(a, 0, sibling, me).wait_recv()
                for j in range(3):
                    copy(a, 4 + j, blocks[5 + j], me).wait_recv()
            for a in range(n_arrays):
                for u in range(n_parts if a == 0 else 1):
                    for k, to in enumerate(targets):
                        copy(a, k, me, to, u, from_shard=True).wait_send()
                    relay(a, u).wait_send()
                    for j in range(3):
                        copy(a, 4 + j, blocks[2 + j], sibling, u).wait_send()
                keep(a).wait()

    n_x = n_tiles - 1
    return pl.pallas_call(
        body, name="gather_norm_proj",
        out_shape=[jax.ShapeDtypeStruct((D_MODEL, tp), BF16), jax.ShapeDtypeStruct((tp, D_IN), F32),
                   jax.ShapeDtypeStruct((TILE, D_MODEL), F32), jax.ShapeDtypeStruct((N_DEV,) + small_shard.shape[1:], F32),
                   jax.ShapeDtypeStruct((N_DEV,) + w_in_shard.shape, BF16)]
                  + [jax.ShapeDtypeStruct((N_DEV,) + w.shape, BF16) for w in w_out_shards],
        grid_spec=pltpu.PrefetchScalarGridSpec(
            num_scalar_prefetch=1, grid=(n_steps,),
            in_specs=[pl.BlockSpec((TILE, D_MODEL), lambda g, pos_ref: (jnp.minimum(g, n_x - 1), 0)),
                      _VMEM, _ANY, _ANY, _ANY, _ANY, _ANY],
            out_specs=[pl.BlockSpec((D_MODEL, TILE), lambda g, pos_ref: (0, jnp.minimum(g, n_tiles - 1))),
                       _ANY, _VMEM, _ANY, _ANY, _ANY, _ANY, _ANY],
            scratch_shapes=[pltpu.VMEM((tp, D_MODEL), BF16), pltpu.VMEM((2, D_MODEL, widest), BF16),
                            pltpu.VMEM((2, tp, widest), F32), pltpu.VMEM((N_DEV,) + small_shard.shape[1:], F32),
                            pltpu.SemaphoreType.DMA((7 * n_parts + 21,)), pltpu.SemaphoreType.DMA((7 * n_parts + 21,)),
                            pltpu.SemaphoreType.DMA((9,)),
                            pltpu.SemaphoreType.DMA((N_DEV,)), pltpu.SemaphoreType.DMA((N_DEV,))]),
        compiler_params=pltpu.CompilerParams(dimension_semantics=("arbitrary",), vmem_limit_bytes=VMEM_LIMIT),
    )(pos, x2d, norm_g, small_shard, w_in_shard, *w_out_shards)


C_AVAL, C_AGLU, C_AZ, C_BB, C_BC, C_BX, C_BZ, C_GA, C_GB = (k * D_MODEL for k in range(9))
S_AZ, S_BB, S_BZ, S_GA, S_GB = (k * D_MODEL for k in range(5))


def _fused_pass(proj, x2d, tgt2d, meta_tile, conv_a_w, conv_a_b, ln_a_g, ln_a_b, b_a_out, conv_b_w, final_g,
                w_a, w_b, w_o, w_a_t, w_b_t, w_o_t, n_tiles):
    T = TILE
    tp = n_tiles * T
    inv_d = 1.0 / D_MODEL

    def block_of(tile):
        return jnp.where(tile == 0, n_tiles - 1, tile - 1)

    def cur(i):
        return block_of(jnp.minimum(i, n_tiles - 1))

    def prev(i):
        return block_of(jnp.clip(i - 1, 0, n_tiles - 1))

    def xblk(i):
        return jnp.maximum(jnp.minimum(i, n_tiles - 1) - 1, 0)

    def body(proj_ref, aprev, cprev, x_ref, tgt_ref, meta_ref, caw_ref, cab_ref, lng_ref, lnb_ref, bao_ref, cbw_ref,
             fg_ref, wa_ref, wb_ref, wo_ref, wat_ref, wbt_ref, wot_ref,
             dproj_ref, ds1_ref, lhs_ref, rhs_ref, small_ref,
             ua0_buf, cb_buf, dua1_buf, dc3_buf, stage, ua1_buf, c3_buf,
             dpa_buf, dpb_buf, dcaw8, dcbw8, shift_buf):
        i = pl.program_id(0)
        this, before = i % 2, 1 - i % 2

        @pl.when(i == 0)
        def _init():
            for buf in (ua0_buf, cb_buf, dua1_buf, dc3_buf, dcaw8, dcbw8):
                buf[...] = jnp.zeros(buf.shape, buf.dtype)
            small_ref[...] = jnp.zeros(small_ref.shape, F32)

        @pl.when(i >= 1)
        def _emit_stage():
            dproj_ref[:, C_AZ:C_BC] = stage[:, S_AZ:S_BZ]
            dproj_ref[:, C_BZ:D_IN] = stage[:, S_BZ:S_GB + D_MODEL]

        @pl.when(i < n_tiles)
        def _front():
            def conv_chunk(cc, carry):
                c0 = pl.multiple_of(cc * LANES, LANES)
                lanes = pl.ds(c0, LANES)

                def col(base):
                    return pl.ds(pl.multiple_of(base + cc * LANES, LANES), LANES)

                ua0 = proj_ref[:, col(C_AVAL)] * _sigmoid(proj_ref[:, col(C_AGLU)])
                ua0_buf[this, 0:HALO, lanes] = ua0_buf[before, T:T + HALO, lanes]
                ua0_buf[this, HALO:HALO + T, lanes] = ua0
                acc = jnp.broadcast_to(cab_ref[:, lanes], (T, LANES))
                lead = HALO - (CONV_A - 1)
                for r in range(SUBLANES):
                    taps = [k for k in range(CONV_A) if (k + lead) % SUBLANES == r]
                    rows = T + SUBLANES * max((k + lead) // SUBLANES for k in taps)
                    if r:
                        shift_buf[r, 0:rows, :] = ua0_buf[this, pl.ds(r, rows), lanes]
                    for k in taps:
                        q = (k + lead) // SUBLANES
                        if r:
                            win = shift_buf[r, SUBLANES * q:SUBLANES * q + T, :]
                        else:
                            win = ua0_buf[this, pl.ds(SUBLANES * q, T), lanes]
                        acc = acc + caw_ref[k:k + 1, lanes] * win
                ua1_buf[:, lanes] = acc
                cb = proj_ref[:, col(C_BC)] * proj_ref[:, col(C_BX)]
                cb_buf[this, 0:SUBLANES, lanes] = cb_buf[before, T:T + SUBLANES, lanes]
                cb_buf[this, SUBLANES:SUBLANES + T, lanes] = cb
                lead_b = SUBLANES - (CONV_B - 1)
                acc3 = cbw_ref[0:1, lanes] * cb_buf[this, pl.ds(lead_b, T), lanes]
                for k in range(1, CONV_B):
                    acc3 = acc3 + cbw_ref[k:k + 1, lanes] * cb_buf[this, pl.ds(lead_b + k, T), lanes]
                c3_buf[:, lanes] = acc3
                return carry

            lax.fori_loop(0, N_CHUNK, conv_chunk, 0)

            ua1 = ua1_buf[...]
            xc = ua1 - _rowmean(ua1)
            rstd = lax.rsqrt(_rowmean(xc * xc) + EPS)
            xhat = xc * rstd
            ua2 = xhat * lng_ref[...] + lnb_ref[...]
            sg2 = _sigmoid(ua2)
            ua3 = ua2 * sg2
            a_z = proj_ref[:, C_AZ:C_AZ + D_MODEL]
            sz = _sigmoid(a_z)
            silu_az = a_z * sz
            lhs_ref[0] = (ua3 * silu_az).astype(BF16)
            b_z = proj_ref[:, C_BZ:C_BZ + D_MODEL]
            sbz = _sigmoid(b_z)
            silu_bz = b_z * sbz
            b_b = proj_ref[:, C_BB:C_BB + D_MODEL]
            c3 = c3_buf[...]
            ub = b_b * c3
            lhs_ref[1] = (ub * silu_bz).astype(BF16)

            ya = _dot(lhs_ref[0], wa_ref[...]) + bao_ref[...]
            yb = _dot(lhs_ref[1], wb_ref[...])
            sga = _sigmoid(proj_ref[:, C_GA:C_GA + D_MODEL])
            sgb = _sigmoid(proj_ref[:, C_GB:C_GB + D_MODEL])
            m_b = (sga * ya + sgb * yb).astype(BF16)
            lhs_ref[2] = m_b
            s0 = jnp.where(i == 0, meta_ref[...], x_ref[...])
            s1 = s0 + _dot(m_b, wo_ref[...])
            r1 = lax.rsqrt(_rowmean(s1 * s1) + EPS)
            y = (s1 * r1) * fg_ref[...]
            is_token = (i >= 1).astype(F32)
            err = (y - tgt_ref[...]) * is_token
            small_ref[ROW_LOSS:ROW_LOSS + 1, :] += (0.5 * inv_d) * _colsum(err * err)
            dy = err * inv_d
            small_ref[ROW_FINAL_G:ROW_FINAL_G + 1, :] += _colsum(dy * (s1 * r1))
            gy = dy * fg_ref[...]
            ds1 = r1 * gy - s1 * ((r1 * r1 * r1) * _rowmean(gy * s1))
            ds1_ref[...] = ds1
            ds1_b = ds1.astype(BF16)
            rhs_ref[2] = ds1_b
            dm = _dot(ds1_b, wot_ref[...])
            dya = dm * sga
            dyb = dm * sgb
            stage[:, S_GA:S_GA + D_MODEL] = (dya * ya * (1.0 - sga)).astype(BF16)
            stage[:, S_GB:S_GB + D_MODEL] = (dyb * yb * (1.0 - sgb)).astype(BF16)
            small_ref[ROW_B_A_OUT:ROW_B_A_OUT + 1, :] += _colsum(dya)
            dya_b = dya.astype(BF16)
            dyb_b = dyb.astype(BF16)
            rhs_ref[0] = dya_b
            rhs_ref[1] = dyb_b
            dpa_buf[...] = _dot(dya_b, wat_ref[...])
            dpb_buf[...] = _dot(dyb_b, wbt_ref[...])

            dpa = dpa_buf[...]
            stage[:, S_AZ:S_AZ + D_MODEL] = (dpa * ua3 * (sz + silu_az * (1.0 - sz))).astype(BF16)
            dua2 = dpa * silu_az * (sg2 + ua3 * (1.0 - sg2))
            small_ref[ROW_LN_G:ROW_LN_G + 1, :] += _colsum(dua2 * xhat)
            small_ref[ROW_LN_B:ROW_LN_B + 1, :] += _colsum(dua2)
            dxh = dua2 * lng_ref[...]
            dua1 = rstd * (dxh - _rowmean(dxh) - xhat * _rowmean(dxh * xhat))
            small_ref[ROW_CONV_A_B:ROW_CONV_A_B + 1, :] += _colsum(dua1)
            dua1_buf[this, 0:T, :] = dua1
            dua1_buf[before, T:T + HALO, :] = dua1[0:HALO]
            dpb = dpb_buf[...]
            stage[:, S_BZ:S_BZ + D_MODEL] = (dpb * ub * (sbz + silu_bz * (1.0 - sbz))).astype(BF16)
            dub = dpb * silu_bz
            stage[:, S_BB:S_BB + D_MODEL] = (dub * c3).astype(BF16)
            dc3 = dub * b_b
            dc3_buf[this, 0:T, :] = dc3
            dc3_buf[before, T:T + SUBLANES, :] = dc3[0:SUBLANES]

        @pl.when(i == n_tiles)
        def _no_later_tile():
            dua1_buf[before, T:T + HALO, :] = jnp.zeros((HALO, D_MODEL), F32)
            dc3_buf[before, T:T + SUBLANES, :] = jnp.zeros((SUBLANES, D_MODEL), F32)

        @pl.when(i >= 1)
        def _lagged():
            def convt_chunk(cc, carry):
                c0 = pl.multiple_of(cc * LANES, LANES)
                lanes = pl.ds(c0, LANES)

                def col(base):
                    return pl.ds(pl.multiple_of(base + cc * LANES, LANES), LANES)

                ua0 = ua0_buf[before, HALO:HALO + T, lanes]
                acc = jnp.zeros((T, LANES), F32)
                for r in range(SUBLANES):
                    shifts = [j for j in range(CONV_A) if j % SUBLANES == r]
                    rows = T + shifts[-1] - r
                    if r:
                        shift_buf[r, 0:rows, :] = dua1_buf[before, pl.ds(r, rows), lanes]
                    for j in shifts:
                        k = CONV_A - 1 - j
                        if r:
                            later = shift_buf[r, j - r:j - r + T, :]
                        else:
                            later = dua1_buf[before, pl.ds(j, T), lanes]
                        acc = acc + caw_ref[k:k + 1, lanes] * later
                        dcaw8[SUBLANES * k:SUBLANES * (k + 1), lanes] += _fold8(ua0 * later)
                a_val = aprev[:, col(0)]
                sg = _sigmoid(aprev[:, col(D_MODEL)])
                dproj_ref[:, col(C_AVAL)] = (acc * sg).astype(BF16)
                dproj_ref[:, col(C_AGLU)] = (acc * a_val * (sg * (1.0 - sg))).astype(BF16)

                cb = cb_buf[before, SUBLANES:SUBLANES + T, lanes]
                acc3 = jnp.zeros((T, LANES), F32)
                for j in range(CONV_B):
                    k = CONV_B - 1 - j
                    later = dc3_buf[before, pl.ds(j, T), lanes]
                    acc3 = acc3 + cbw_ref[k:k + 1, lanes] * later
                    dcbw8[SUBLANES * k:SUBLANES * (k + 1), lanes] += _fold8(cb * later)
                dproj_ref[:, col(C_BC)] = (acc3 * cprev[:, col(D_MODEL)]).astype(BF16)
                dproj_ref[:, col(C_BX)] = (acc3 * cprev[:, col(0)]).astype(BF16)
                return carry

            lax.fori_loop(0, N_CHUNK, convt_chunk, 0)

        @pl.when(i == n_tiles)
        def _finish():
            for k in range(CONV_A):
                small_ref[ROW_CONV_A_W + k:ROW_CONV_A_W + k + 1, :] = _colsum(dcaw8[SUBLANES * k:SUBLANES * (k + 1), :])
            for k in range(CONV_B):
                small_ref[ROW_CONV_B_W + k:ROW_CONV_B_W + k + 1, :] = _colsum(dcbw8[SUBLANES * k:SUBLANES * (k + 1), :])

    pair = 2 * D_MODEL
    return pl.pallas_call(
        body, name="fused_pass", grid=(n_tiles + 1,),
        out_shape=[
            jax.ShapeDtypeStruct((tp, D_IN), BF16),
            jax.ShapeDtypeStruct((tp, D_MODEL), F32),
            jax.ShapeDtypeStruct((3, tp, D_MODEL), BF16),
            jax.ShapeDtypeStruct((3, tp, D_MODEL), BF16),
            jax.ShapeDtypeStruct((SMALL_A_ROWS, D_MODEL), F32),
        ],
        in_specs=[
            pl.BlockSpec((T, D_IN), lambda i: (cur(i), 0)),
            pl.BlockSpec((T, pair), lambda i: (prev(i), C_AVAL // pair)),
            pl.BlockSpec((T, pair), lambda i: (prev(i), C_BC // pair)),
            pl.BlockSpec((T, D_MODEL), lambda i: (xblk(i), 0)),
            pl.BlockSpec((T, D_MODEL), lambda i: (xblk(i), 0)),
            _VMEM, _VMEM, _VMEM, _VMEM, _VMEM, _VMEM, _VMEM, _VMEM,
            *[_resident((D_MODEL, D_MODEL)) for _ in range(6)],
        ],
        out_specs=[
            pl.BlockSpec((T, D_IN), lambda i: (prev(i), 0)),
            pl.BlockSpec((T, D_MODEL), lambda i: (cur(i), 0)),
            pl.BlockSpec((3, T, D_MODEL), lambda i: (0, cur(i), 0)),
            pl.BlockSpec((3, T, D_MODEL), lambda i: (0, cur(i), 0)),
            _VMEM,
        ],
        scratch_shapes=[
            pltpu.VMEM((2, HALO + T, D_MODEL), F32),
            pltpu.VMEM((2, SUBLANES + T, D_MODEL), F32),
            pltpu.VMEM((2, T + HALO, D_MODEL), F32),
            pltpu.VMEM((2, T + SUBLANES, D_MODEL), F32),
            pltpu.VMEM((T, 5 * D_MODEL), BF16),
            pltpu.VMEM((T, D_MODEL), F32),
            pltpu.VMEM((T, D_MODEL), F32),
            pltpu.VMEM((T, D_MODEL), F32),
            pltpu.VMEM((T, D_MODEL), F32),
            pltpu.VMEM((32 * SUBLANES, D_MODEL), F32),
            pltpu.VMEM((SUBLANES * SUBLANES, D_MODEL), F32),
            pltpu.VMEM((SUBLANES, T + HALO, LANES), F32),
        ],
        compiler_params=pltpu.CompilerParams(dimension_semantics=("arbitrary",), vmem_limit_bytes=VMEM_LIMIT),
    )(proj, proj, proj, x2d, tgt2d, meta_tile, conv_a_w, conv_a_b, ln_a_g, ln_a_b, b_a_out, conv_b_w, final_g,
      w_a, w_b, w_o, w_a_t, w_b_t, w_o_t)


def _input_bwd(dproj, ds1, x2d, meta_tile, norm_g, w_in_all, row_tile, after):
    seq = x2d.shape[0]
    n_steps = seq // row_tile
    meta_block = seq // TILE

    def backward(dp_ref, ds1_ref, s0_ref, g_ref, w_ref, out_ref, vec_ref):
        dh = _dot_nt(dp_ref[:, 0:COLS], w_ref[0])
        for j in range(1, N_DEV):
            dh = dh + _dot_nt(dp_ref[:, j * COLS:(j + 1) * COLS], w_ref[j])
        s0v = s0_ref[...]
        r = lax.rsqrt(_rowmean(s0v * s0v) + EPS)
        gh = dh * g_ref[...]
        out_ref[...] = ds1_ref[...] + r * gh - s0v * ((r * r * r) * _rowmean(gh * s0v))
        vec_ref[ROW_NORM_G:ROW_NORM_G + 1, :] += _colsum(dh * (s0v * r))

    def body(dp_ref, ds1_ref, x_ref, dpm_ref, ds1m_ref, meta_ref, g_ref, w_ref, after_ref, gx_ref, small_ref, gmeta_buf):
        t = pl.program_id(0)

        @pl.when(t == 0)
        def _():
            small_ref[...] = jnp.zeros(small_ref.shape, F32)

        backward(dp_ref, ds1_ref, x_ref, g_ref, w_ref, gx_ref, small_ref)

        @pl.when(t == n_steps - 1)
        def _():
            backward(dpm_ref, ds1m_ref, meta_ref, g_ref, w_ref, gmeta_buf, small_ref)
            small_ref[ROW_META:ROW_META + N_META, :] = gmeta_buf[TILE - N_META:TILE, :]

    return pl.pallas_call(
        body, name="input_bwd", grid=(n_steps,),
        out_shape=[jax.ShapeDtypeStruct(x2d.shape, F32), jax.ShapeDtypeStruct((SMALL_B_ROWS, D_MODEL), F32)],
        in_specs=[pl.BlockSpec((row_tile, D_IN), lambda t: (t, 0)),
                  pl.BlockSpec((row_tile, D_MODEL), lambda t: (t, 0)),
                  pl.BlockSpec((row_tile, D_MODEL), lambda t: (t, 0)),
                  pl.BlockSpec((TILE, D_IN), lambda t: (meta_block, 0)),
                  pl.BlockSpec((TILE, D_MODEL), lambda t: (meta_block, 0)),
                  _VMEM, _VMEM, _resident((N_DEV, D_MODEL, COLS)), _ANY],
        out_specs=[pl.BlockSpec((row_tile, D_MODEL), lambda t: (t, 0)), _VMEM],
        scratch_shapes=[pltpu.VMEM((TILE, D_MODEL), F32)],
        compiler_params=pltpu.CompilerParams(dimension_semantics=("arbitrary",), vmem_limit_bytes=VMEM_LIMIT),
    )(dproj, ds1, x2d, dproj, ds1, meta_tile, norm_g, w_in_all, after)


def _grad_w_in_half(pos, h_t, dproj, k_tile, other_side, rides, name, after=None, add_to=None, narrow=False):
    tp = h_t.shape[1]
    n_k = tp // k_tile
    order = [] if after is None else [after]
    summing = add_to is not None
    assert not (summing and narrow)

    def column_block(q, k, pos_ref):
        return k, 2 * q + (1 - pos_ref[2] if other_side else pos_ref[2])

    def body(pos_ref, h_ref, dp_ref, *refs):
        acc = refs[-1]

        @pl.when(pl.program_id(1) == 0)
        def _():
            acc[...] = refs[0][...].astype(F32) if summing else jnp.zeros(acc.shape, F32)

        acc[...] += _dot(h_ref[...], dp_ref[...])

        if summing or narrow:
            @pl.when(pl.program_id(1) == n_k - 1)
            def _():
                refs[-2][...] = acc[...].astype(BF16)

        if summing:
            @pl.when((pl.program_id(1) == n_k - 1) & (pl.program_id(0) == 2 * pos_ref[0] + pos_ref[1]))
            def _():
                refs[-3][...] = acc[...]

    ride = [a for _, arrays in rides for a in arrays]
    n_arr = len(ride)
    ride_shapes, ride_sems = _ride_shapes(rides)
    block = (None, D_MODEL, COLS)
    extra_in = [add_to] if summing else []
    extra_in_specs = [pl.BlockSpec(block, lambda q, k, pos_ref: (q, 0, 0))] if summing else []
    extra_out = [jax.ShapeDtypeStruct((4, D_MODEL, COLS), BF16)] if summing else []
    extra_out_specs = [pl.BlockSpec(block, lambda q, k, pos_ref: (q ^ (2 * pos_ref[0] + pos_ref[1]), 0, 0))] \
        if summing else []
    body = _riding(body, 3 + len(extra_in) + len(order), 1 + len(extra_out), rides,
                   lambda: (pl.program_id(0) == 0) & (pl.program_id(1) == 0),
                   lambda: (pl.program_id(0) == 3) & (pl.program_id(1) == n_k - 1))
    return pl.pallas_call(
        body, name=name,
        out_shape=[jax.ShapeDtypeStruct((1 if summing else 4, D_MODEL, COLS), BF16 if narrow else F32)]
        + extra_out + ride_shapes,
        grid_spec=pltpu.PrefetchScalarGridSpec(
            num_scalar_prefetch=1, grid=(4, n_k),
            in_specs=[pl.BlockSpec((D_MODEL, k_tile), lambda q, k, pos_ref: (0, k)),
                      pl.BlockSpec((k_tile, COLS), column_block)] + extra_in_specs + [_ANY] * (len(order) + n_arr),
            out_specs=[pl.BlockSpec(block, lambda q, k, pos_ref: (0 if summing else q, 0, 0))]
            + extra_out_specs + [_ANY] * n_arr,
            scratch_shapes=([pltpu.VMEM((D_MODEL, COLS), F32)] if summing or narrow else []) + ride_sems),
        compiler_params=pltpu.CompilerParams(dimension_semantics=("arbitrary", "arbitrary"),
                                             vmem_limit_bytes=VMEM_LIMIT),
    )(pos, h_t, dproj, *extra_in, *order, *ride)


def _grad_w_out(lhs, rhs, k_tile, after):
    tp = lhs.shape[1]

    def body(a_ref, b_ref, after_ref, o_ref):
        @pl.when(pl.program_id(1) == 0)
        def _():
            o_ref[...] = jnp.zeros(o_ref.shape, F32)

        o_ref[...] += _dot_tn(a_ref[...], b_ref[...]).reshape(N_DEV, ROWS_OUT, D_MODEL)

    return pl.pallas_call(
        body, name="grad_w_out", grid=(3, tp // k_tile),
        out_shape=jax.ShapeDtypeStruct((N_DEV, 3, ROWS_OUT, D_MODEL), F32),
        in_specs=[pl.BlockSpec((None, k_tile, D_MODEL), lambda w, k: (w, k, 0)),
                  pl.BlockSpec((None, k_tile, D_MODEL), lambda w, k: (w, k, 0)), _ANY],
        out_specs=pl.BlockSpec((N_DEV, None, ROWS_OUT, D_MODEL), lambda w, k: (0, w, 0, 0)),
        compiler_params=pltpu.CompilerParams(dimension_semantics=("arbitrary", "arbitrary"),
                                             vmem_limit_bytes=VMEM_LIMIT),
    )(lhs, rhs, after)


def _adamw_math(w, g, m, v):
    m = ADAM_B1 * m + (1.0 - ADAM_B1) * g
    v = ADAM_B2 * v + (1.0 - ADAM_B2) * (g * g)
    m_hat = m / (1.0 - ADAM_B1 ** ADAM_STEP)
    v_hat = v / (1.0 - ADAM_B2 ** ADAM_STEP)
    delta = -ADAM_LR * (m_hat / (jnp.sqrt(v_hat) + ADAM_EPS) + ADAM_WD * w)
    return delta, m, v


def _adamw_sharded(pos, mine, theirs, landed, weights, row_tile, name, after=None):
    order = [] if after is None else [after]
    rows, n = weights[0][0].shape
    n_slots = mine.shape[0]
    per_shard = rows // row_tile
    assert per_shard == 1 or len(weights) == 1

    def mine_map(j, t, pos_ref):
        chip = 2 * pos_ref[0] + pos_ref[1]
        return {N_DEV: 2 * chip + pos_ref[2], 4: chip, 1: 0}[n_slots], j * per_shard + t, 0

    def theirs_map(j, t, pos_ref):
        return 2 * pos_ref[0] + pos_ref[1], j * per_shard + t, 0

    def body(pos_ref, mine_ref, *refs):
        if theirs is not None:
            g = mine_ref[...] + refs[0][...]
            refs = refs[1:]
        else:
            g = mine_ref[...]
        land_ref, refs = refs[0], refs[1:]
        ins, outs = refs[:3 * len(weights)], refs[3 * len(weights) + len(order):]
        for k in range(3):
            g = g + land_ref[k].astype(F32)
        for j in range(len(weights)):
            @pl.when(pl.program_id(0) == j)
            def _(j=j):
                w_ref, m_ref, v_ref = ins[3 * j:3 * j + 3]
                delta, m_new, v_new = _adamw_math(w_ref[...], g, m_ref[...], v_ref[...])
                for ref, val in zip(outs[4 * j:4 * j + 4], (g, delta, m_new, v_new)):
                    ref[...] = val

    tile = pl.BlockSpec((row_tile, n), lambda j, t, pos_ref: (t, 0))
    res = pl.pallas_call(
        body, name=name,
        out_shape=[jax.ShapeDtypeStruct((rows, n), F32)] * (4 * len(weights)),
        grid_spec=pltpu.PrefetchScalarGridSpec(
            num_scalar_prefetch=1, grid=(len(weights), per_shard),
            in_specs=[pl.BlockSpec((None, row_tile, n), mine_map)]
            + ([pl.BlockSpec((None, row_tile, n), theirs_map)] if theirs is not None else [])
            + [pl.BlockSpec((3, row_tile, n), lambda j, t, pos_ref: (0, j * per_shard + t, 0))]
            + [tile] * (3 * len(weights)) + [_ANY] * len(order),
            out_specs=[tile] * (4 * len(weights))),
        compiler_params=pltpu.CompilerParams(dimension_semantics=("arbitrary", "arbitrary")),
    )(pos, mine, *([theirs] if theirs is not None else []), landed, *[a for wmv in weights for a in wmv], *order)
    return [res[4 * j:4 * j + 4] for j in range(len(weights))]


def _adamw_small(pos, gathered, params, row_by_row=()):
    n_par, n_src = len(params), len(gathered)

    def body(pos_ref, *refs):
        g_refs, gc_refs = refs[:n_src], refs[n_src:2 * n_src]
        ins = refs[2 * n_src:2 * n_src + 3 * n_par]
        outs = refs[2 * n_src + 3 * n_par:]
        loss_ref = outs[4 * n_par]

        def reduced(ref, row, n_rows):
            g = ref[0, row:row + n_rows, :]
            for d in range(1, N_DEV):
                g = g + ref[d, row:row + n_rows, :]
            return g

        for p, (src, row, n_rows, sharded, _, _, _) in enumerate(params):
            g = reduced((gc_refs if sharded else g_refs)[src], row, n_rows)
            w_ref, m_ref, v_ref = ins[3 * p:3 * p + 3]
            delta, m_new, v_new = _adamw_math(w_ref[...], g, m_ref[...], v_ref[...])
            for kind, res in enumerate((g, delta, m_new, v_new)):
                if p in row_by_row:
                    for k in range(n_rows):
                        outs[4 * p + kind][k] = res[k:k + 1, :]
                else:
                    outs[4 * p + kind][...] = res
        loss = jnp.sum(reduced(g_refs[0], ROW_LOSS, 1), axis=1, keepdims=True)
        loss_ref[...] = jnp.broadcast_to(loss, loss_ref.shape)

    out_shape = []
    for p, (_, _, n_rows, _, w, _, _) in enumerate(params):
        out_shape += [jax.ShapeDtypeStruct((n_rows, 1, LANES) if p in row_by_row else w.shape, F32)] * 4
    out_shape.append(jax.ShapeDtypeStruct((1, LANES), F32))
    flat = [a for (_, _, _, _, w, m, v) in params for a in (w, m, v)]
    my_lanes = [pl.BlockSpec((N_DEV, g.shape[1], LANES), lambda i, pos_ref: (0, 0, 4 * pos_ref[0] + 2 * pos_ref[1] + pos_ref[2]))
                for g in gathered]
    return pl.pallas_call(
        body, name="adamw_small", out_shape=out_shape,
        grid_spec=pltpu.PrefetchScalarGridSpec(
            num_scalar_prefetch=1, grid=(1,),
            in_specs=[_VMEM] * n_src + my_lanes + [_VMEM] * len(flat), out_specs=[_VMEM] * len(out_shape)),
    )(pos, *gathered, *gathered, *flat)


def _pad_rows(a, rows):
    return jnp.concatenate([a, jnp.zeros((rows - a.shape[0], a.shape[1]), a.dtype)], axis=0)


def kernel(x, meta_tokens, norm_g, w_in, conv_a_w, conv_a_b, ln_a_g, ln_a_b, w_a_out, b_a_out, conv_b_w, w_b_out, w_out, final_g, loss_target, m_meta_tokens, m_norm_g, m_w_in, m_conv_a_w, m_conv_a_b, m_ln_a_g, m_ln_a_b, m_w_a_out, m_b_a_out, m_conv_b_w, m_w_b_out, m_w_out, m_final_g, v_meta_tokens, v_norm_g, v_w_in, v_conv_a_w, v_conv_a_b, v_ln_a_g, v_ln_a_b, v_w_a_out, v_b_a_out, v_conv_b_w, v_w_b_out, v_w_out, v_final_g):
    seq = x.shape[1]
    assert x.shape == (1, seq, D_MODEL) and seq % TILE == 0 and w_in.shape == (1, D_MODEL, COLS)
    n_tiles = seq // TILE + 1
    tp = n_tiles * TILE
    pos = jnp.stack([lax.axis_index("x"), lax.axis_index("y"), lax.axis_index("c")]).astype(jnp.int32)
    x2d = x[0]
    tgt2d = loss_target[0]

    small = jnp.concatenate([meta_tokens, _pad_rows(conv_a_w[0], 32), _pad_rows(conv_b_w[0], SUBLANES)], axis=0)
    final_g2 = final_g.reshape(1, D_MODEL)

    w_out_shards = [w[0].astype(BF16) for w in (w_a_out, w_b_out, w_out)]
    h_t, proj, meta_tile, small_params, w_in_all, *w_out_all = _gather_norm_proj(
        pos, x2d, small[None], norm_g, w_in[0].astype(BF16), w_out_shards, 3)
    small_params = small_params.transpose(1, 0, 2).reshape(small.shape[0], D_MODEL)
    conv_a_full, conv_b_full = small_params[N_META:N_META + 32], small_params[N_META + 32:]
    w_out_all = [w.reshape(D_MODEL, D_MODEL) for w in w_out_all]
    w_out_all_t = [w.T for w in w_out_all]
    dproj, ds1, lhs, rhs, small_a = _fused_pass(
        proj, x2d, tgt2d, meta_tile, conv_a_full, conv_a_b, ln_a_g, ln_a_b, b_a_out, conv_b_full, final_g2,
        w_out_all[0], w_out_all[1], w_out_all[2], w_out_all_t[0], w_out_all_t[1], w_out_all_t[2], n_tiles)
    k_tile = tp // 3
    gw_far, small_a_all = _grad_w_in_half(pos, h_t, dproj, k_tile, True, [("all", (small_a[None],))], "grad_w_in_far",
                                          narrow=True)
    sems, sent, landing, token = _start_exchanges([("sibling_half", (gw_far,))], "rs_far_start")
    gw_out = _grad_w_out(lhs, rhs, k_tile, token).reshape(N_DEV, 3 * ROWS_OUT, D_MODEL)
    (their_in,) = _wait_exchanges([("sibling_half", 1)], sems, sent, landing, gw_out, "rs_far_wait")
    sems_o, sent_o, landing_o, token = _start_exchanges([("sibling", (gw_out,))], "rs_out_start")
    gw_near, parts_in = _grad_w_in_half(pos, h_t, dproj, k_tile, False, [], "grad_w_in_near", after=token,
                                        add_to=their_in)
    sems_i, sent_i, landing_i, token = _start_exchanges([("chips_by_relation", (parts_in,))], "rs_chips_in_start")
    gw_out, their_out = _wait_exchanges([("sibling", 1)], sems_o, sent_o, landing_o, token, "rs_out_wait",
                                        keep_sources=True)
    parts_out = _chip_partial(pos, gw_out, their_out, (1, 2, 3), BF16, ROWS_OUT, "rs_parts_w_out")
    sems_o, sent_o, landing_o, token = _start_exchanges([("chips", (parts_out,))], "rs_chips_out_start")
    grad_x, small_b = _input_bwd(dproj, ds1, x2d, meta_tile, norm_g, w_in_all, min(512, seq), token)

    sems_s, sent_s, landing_s, token = _start_exchanges([("all", (small_b[None],))], "gather_small_grads_start")
    (land_in,) = _wait_exchanges([("chips_by_relation", 1)], sems_i, sent_i, landing_i, token, "rs_chips_in_wait")
    (res_in,) = _adamw_sharded(pos, gw_near, None, land_in, [(w_in[0], m_w_in[0], v_w_in[0])], 128, "adamw_w_in")
    (land_out,) = _wait_exchanges([("chips", 1)], sems_o, sent_o, landing_o, res_in[0], "rs_chips_out_wait")
    res_out = _adamw_sharded(
        pos, gw_out, their_out, land_out,
        [(w_a_out[0], m_w_a_out[0], v_w_a_out[0]), (w_b_out[0], m_w_b_out[0], v_w_b_out[0]),
         (w_out[0], m_w_out[0], v_w_out[0])], ROWS_OUT, "adamw_w_out")
    (small_b_all,) = _wait_exchanges([("all", 1)], sems_s, sent_s, landing_s, res_out[2][0], "gather_small_grads_wait")
    small_grads = [small_a_all, small_b_all]
    params = [
        (1, ROW_META, N_META, True, meta_tokens, m_meta_tokens, v_meta_tokens),
        (1, ROW_NORM_G, 1, False, norm_g, m_norm_g, v_norm_g),
        (0, ROW_CONV_A_W, CONV_A, True, conv_a_w[0], m_conv_a_w[0], v_conv_a_w[0]),
        (0, ROW_CONV_A_B, 1, False, conv_a_b, m_conv_a_b, v_conv_a_b),
        (0, ROW_LN_G, 1, False, ln_a_g, m_ln_a_g, v_ln_a_g),
        (0, ROW_LN_B, 1, False, ln_a_b, m_ln_a_b, v_ln_a_b),
        (0, ROW_B_A_OUT, 1, False, b_a_out, m_b_a_out, v_b_a_out),
        (0, ROW_CONV_B_W, CONV_B, True, conv_b_w[0], m_conv_b_w[0], v_conv_b_w[0]),
        (0, ROW_FINAL_G, 1, False, final_g2, m_final_g.reshape(1, D_MODEL), v_final_g.reshape(1, D_MODEL)),
    ]
    conv_weights = (2, 7)
    res_small = _adamw_small(pos, small_grads, params, conv_weights)
    loss = res_small[-1][0, 0]

    def small_res(p, kind, shape):
        res = res_small[4 * p + kind]
        return res.transpose(1, 0, 2) if p in conv_weights else res.reshape(shape)

    per_weight = []
    for kind in range(4):
        per_weight.append([
            small_res(0, kind, meta_tokens.shape),
            small_res(1, kind, norm_g.shape),
            res_in[kind].reshape(w_in.shape),
            small_res(2, kind, conv_a_w.shape),
            small_res(3, kind, conv_a_b.shape),
            small_res(4, kind, ln_a_g.shape),
            small_res(5, kind, ln_a_b.shape),
            res_out[0][kind].reshape(w_a_out.shape),
            small_res(6, kind, b_a_out.shape),
            small_res(7, kind, conv_b_w.shape),
            res_out[1][kind].reshape(w_b_out.shape),
            res_out[2][kind].reshape(w_out.shape),
            small_res(8, kind, final_g.shape),
        ])
    return (loss, grad_x.reshape(x.shape), *per_weight[0], *per_weight[1], *per_weight[2], *per_weight[3])
```

```python
import functools

import jax
import jax.numpy as jnp
from jax import lax
from jax.experimental import pallas as pl
from jax.experimental.pallas import tpu as pltpu

D_MODEL = 1024
N_META = 16
N_DEV = 8
D_IN = 9 * D_MODEL
COLS = D_IN // N_DEV
ROWS_OUT = D_MODEL // N_DEV
CONV_A = 31
CONV_B = 3
EPS = 1e-6

ADAM_LR = 0.001
ADAM_B1 = 0.9
ADAM_B2 = 0.999
ADAM_EPS = 1e-08
ADAM_WD = 0.01
ADAM_STEP = 10

TILE = 128
LANES = 128
N_CHUNK = D_MODEL // LANES
HALO = 32
SUBLANES = 8
VMEM_LIMIT = 56 * 1024 * 1024

ROW_FINAL_G, ROW_B_A_OUT, ROW_LN_G, ROW_LN_B, ROW_CONV_A_B, ROW_LOSS = 0, 1, 2, 3, 4, 5
ROW_CONV_A_W, ROW_CONV_B_W, SMALL_A_ROWS = 8, 40, 48
ROW_NORM_G, ROW_META, SMALL_B_ROWS = 0, 8, 24

MESH = pl.DeviceIdType.MESH
_ANY = pl.BlockSpec(memory_space=pl.ANY)
_VMEM = pl.BlockSpec(memory_space=pltpu.VMEM)
BF16 = jnp.bfloat16
F32 = jnp.float32


def _resident(shape):
    return pl.BlockSpec(shape, lambda *_: (0,) * len(shape), pipeline_mode=pl.Buffered(1))


def _sigmoid(v):
    return jax.nn.sigmoid(v)


def _dot(a, b):
    return jnp.dot(a, b, preferred_element_type=F32)


def _dot_nt(a, b):
    return lax.dot_general(a, b, (((1,), (1,)), ((), ())), preferred_element_type=F32)


def _dot_tn(a, b):
    return lax.dot_general(a, b, (((0,), (0,)), ((), ())), preferred_element_type=F32)


def _colsum(v):
    return jnp.sum(v, axis=0, keepdims=True)


def _rowmean(v):
    parts = [v[:, LANES * c:LANES * (c + 1)] for c in range(v.shape[1] // LANES)]
    return jnp.sum(functools.reduce(jnp.add, parts), axis=-1, keepdims=True) * (1.0 / v.shape[1])


def _fold8(v):
    parts = [v[SUBLANES * g:SUBLANES * (g + 1)] for g in range(v.shape[0] // SUBLANES)]
    return functools.reduce(jnp.add, parts)


def _sibling_copies(srcs, dsts, send_sems, recv_sems):
    x, y, c = lax.axis_index("x"), lax.axis_index("y"), lax.axis_index("c")
    return [pltpu.make_async_remote_copy(
        src_ref=src.at[2 * q + (1 - c)], dst_ref=dst.at[q],
        send_sem=send_sems.at[4 * a + q], recv_sem=recv_sems.at[4 * a + q],
        device_id=(x, y, 1 - c), device_id_type=MESH)
        for a, (src, dst) in enumerate(zip(srcs, dsts)) for q in range(4)]


def _chip_copies(srcs, dsts, send_sems, recv_sems):
    x, y, c = lax.axis_index("x"), lax.axis_index("y"), lax.axis_index("c")
    targets = [(x, 1 - y, c), (1 - x, y, c), (1 - x, 1 - y, c)]
    return [pltpu.make_async_remote_copy(
        src_ref=src.at[k], dst_ref=dst.at[k],
        send_sem=send_sems.at[3 * a + k], recv_sem=recv_sems.at[3 * a + k],
        device_id=targets[k], device_id_type=MESH)
        for a, (src, dst) in enumerate(zip(srcs, dsts)) for k in range(3)]


def _sibling_half_copies(srcs, dsts, send_sems, recv_sems):
    x, y, c = lax.axis_index("x"), lax.axis_index("y"), lax.axis_index("c")
    return [pltpu.make_async_remote_copy(
        src_ref=src.at[q], dst_ref=dst.at[q],
        send_sem=send_sems.at[4 * a + q], recv_sem=recv_sems.at[4 * a + q],
        device_id=(x, y, 1 - c), device_id_type=MESH)
        for a, (src, dst) in enumerate(zip(srcs, dsts)) for q in range(4)]


def _all_copies(srcs, dsts, send_sems, recv_sems):
    x, y, c = lax.axis_index("x"), lax.axis_index("y"), lax.axis_index("c")
    mine = 4 * x + 2 * y + c
    copies = []
    for a, (src, dst) in enumerate(zip(srcs, dsts)):
        copies.append(pltpu.make_async_copy(src.at[0], dst.at[mine], send_sems.at[N_DEV * a]))
        for k in range(1, N_DEV):
            copies.append(pltpu.make_async_remote_copy(
                src_ref=src.at[0], dst_ref=dst.at[mine],
                send_sem=send_sems.at[N_DEV * a + k], recv_sem=recv_sems.at[N_DEV * a + k],
                device_id=(x ^ (k >> 2), y ^ ((k >> 1) & 1), c ^ (k & 1)), device_id_type=MESH))
    return copies


def _chip_copies_by_relation(srcs, dsts, send_sems, recv_sems):
    return _chip_copies([src.at[pl.ds(1, 3)] for src in srcs], dsts, send_sems, recv_sems)


_EXCHANGES = {"sibling": (4, _sibling_copies, 4), "sibling_half": (4, _sibling_half_copies, 4),
              "chips": (3, _chip_copies, 3), "chips_by_relation": (3, _chip_copies_by_relation, 3),
              "all": (N_DEV, _all_copies, N_DEV)}


def _exchange_shapes(kind, arrays):
    per_array, _, slots = _EXCHANGES[kind]
    out_shape = [jax.ShapeDtypeStruct((slots,) + a.shape[1:], a.dtype) for a in arrays]
    sems = [pltpu.SemaphoreType.DMA((per_array * len(arrays),))] * 2
    return out_shape, sems


def _ride_shapes(rides):
    shapes, sems = [], []
    for kind, arrays in rides:
        ride_shapes, ride_sems = _exchange_shapes(kind, arrays)
        shapes += ride_shapes
        sems += ride_sems
    return shapes, sems


def _riding(body, n_in, n_out, rides, is_first, is_last):
    counts = [len(arrays) for _, arrays in rides]
    n_arr = sum(counts)

    def wrapped(*refs):
        ins, srcs = refs[:n_in], refs[n_in:n_in + n_arr]
        outs = refs[n_in + n_arr:n_in + n_arr + n_out]
        dsts = refs[n_in + n_arr + n_out:n_in + 2 * n_arr + n_out]
        first_sem = len(refs) - 2 * len(rides)
        scratch, sems = refs[n_in + 2 * n_arr + n_out:first_sem], refs[first_sem:]

        def copies():
            made, at = [], 0
            for r, ((kind, _), n) in enumerate(zip(rides, counts)):
                made += _EXCHANGES[kind][1](srcs[at:at + n], dsts[at:at + n], sems[2 * r], sems[2 * r + 1])
                at += n
            return made

        @pl.when(is_first())
        def _():
            for cp in copies():
                cp.start()

        body(*ins, *outs, *scratch)

        @pl.when(is_last())
        def _():
            for cp in copies():
                cp.wait()

    return wrapped


_HBM = pl.BlockSpec(memory_space=pltpu.HBM)
_SEM = pl.BlockSpec(memory_space=pltpu.SEMAPHORE)
_FLOWS = pltpu.SideEffectType.DATAFLOW_SIDE_EFFECTING


def _start_exchanges(rides, name):
    arrays = [a for _, group in rides for a in group]
    shapes, sems = _ride_shapes(rides)
    n_arr, n_sem = len(arrays), len(sems)

    def body(*refs):
        srcs, lands = refs[:n_arr], refs[n_arr:2 * n_arr]
        sem_refs, token = refs[2 * n_arr:2 * n_arr + n_sem], refs[-1]
        at = 0
        for r, (kind, group) in enumerate(rides):
            n = len(group)
            for cp in _EXCHANGES[kind][1](srcs[at:at + n], lands[at:at + n], sem_refs[2 * r], sem_refs[2 * r + 1]):
                cp.start()
            at += n
        token[...] = jnp.zeros(token.shape, token.dtype)

    in_hbm = [pltpu.HBM(a.shape, a.dtype) for a in arrays]
    land_hbm = [pltpu.HBM(sh.shape, sh.dtype) for sh in shapes]
    res = pl.pallas_call(
        body, name=name,
        out_shape=(*sems, *in_hbm, *land_hbm, jax.ShapeDtypeStruct((SUBLANES, LANES), F32)),
        in_specs=[_HBM] * (2 * n_arr), out_specs=(*[_SEM] * n_sem, *[_HBM] * (2 * n_arr), _VMEM),
        input_output_aliases={i: n_sem + i for i in range(2 * n_arr)},
        compiler_params=pltpu.CompilerParams(has_side_effects=_FLOWS),
    )(*[pltpu.with_memory_space_constraint(a, pltpu.HBM) for a in arrays],
      *[pltpu.with_memory_space_constraint(lax.empty(sh.shape, sh.dtype), pltpu.HBM) for sh in shapes])
    return res[:n_sem], res[n_sem:n_sem + n_arr], res[n_sem + n_arr:n_sem + 2 * n_arr], res[-1]


def _wait_exchanges(kinds, sems, arrays, lands, after, name, keep_sources=False):
    n_arr, n_sem = len(arrays), len(sems)

    def body(*refs):
        srcs, dsts = refs[:n_arr], refs[n_arr:2 * n_arr]
        sem_refs = refs[2 * n_arr:2 * n_arr + n_sem]
        at = 0
        for r, (kind, n) in enumerate(kinds):
            for cp in _EXCHANGES[kind][1](srcs[at:at + n], dsts[at:at + n], sem_refs[2 * r], sem_refs[2 * r + 1]):
                cp.wait()
            at += n

    hbm = [pltpu.HBM(a.shape, a.dtype) for a in (*arrays, *lands)]
    return pl.pallas_call(
        body, name=name, out_shape=tuple(hbm),
        in_specs=[_HBM] * (2 * n_arr) + [_SEM] * n_sem + [_ANY], out_specs=tuple([_HBM] * (2 * n_arr)),
        input_output_aliases={i: i for i in range(2 * n_arr)},
        compiler_params=pltpu.CompilerParams(has_side_effects=_FLOWS),
    )(*arrays, *lands, *sems, after)[0 if keep_sources else n_arr:]


def _chip_partial(pos, mine, theirs, relations, out_dtype, row_tile, name):
    n_slots, m, n = mine.shape
    q0 = relations[0]

    def chip_of(qi, pos_ref):
        q = qi + q0
        return pos_ref[0] ^ (q >> 1), pos_ref[1] ^ (q & 1)

    def mine_map(qi, t, pos_ref):
        px, py = chip_of(qi, pos_ref)
        return (4 * px + 2 * py + pos_ref[2] if n_slots == N_DEV else 2 * px + py), t, 0

    def theirs_map(qi, t, pos_ref):
        px, py = chip_of(qi, pos_ref)
        return 2 * px + py, t, 0

    def body(pos_ref, a_ref, b_ref, o_ref):
        o_ref[...] = (a_ref[...] + b_ref[...]).astype(out_dtype)

    return pl.pallas_call(
        body, name=name,
        out_shape=jax.ShapeDtypeStruct((len(relations), m, n), out_dtype),
        grid_spec=pltpu.PrefetchScalarGridSpec(
            num_scalar_prefetch=1, grid=(len(relations), m // row_tile),
            in_specs=[pl.BlockSpec((None, row_tile, n), mine_map), pl.BlockSpec((None, row_tile, n), theirs_map)],
            out_specs=pl.BlockSpec((None, row_tile, n), lambda qi, t, pos_ref: (qi, t, 0))),
        compiler_params=pltpu.CompilerParams(dimension_semantics=("arbitrary", "arbitrary")),
    )(pos, mine, theirs)


PARTS = ((0, 512), (512, 640))


def _gather_norm_proj(pos, x2d, small_shard, norm_g, w_in_shard, w_out_shards, n_chunk):
    seq = x2d.shape[0]
    n_tiles = seq // TILE + 1
    tp = n_tiles * TILE
    n_parts = len(PARTS)
    widest = max(width for _, width in PARTS)
    units = [(s, u) for s in range(2) for u in range(n_parts)]
    units += [(s, u) for u in range(n_parts) for s in (2, 3, 5, 6)] + [(s, u) for u in range(n_parts) for s in (4, 7)]
    n_units = len(units)
    over_ici = [m for m, (s, _) in enumerate(units) if s in (2, 3)]
    handled_at = {m: m - 1 for m in range(1, n_units)}
    handled_at.update({m: over_ici[0] - 1 + i for i, m in enumerate(over_ici)})
    assert all(step < m for m, step in handled_at.items())
    n_steps = n_tiles + n_units
    chunk = tp // n_chunk

    def body(pos_ref, x_ref, g_ref, small_ref, win_ref, wa_ref, wb_ref, wo_ref,
             ht_ref, proj_ref, meta_ref, small_all, win_all, wa_all, wb_all, wo_all,
             h_all, wbuf, rbuf, small_buf, send_sems, recv_sems, local_sems, small_send, small_recv):
        g = pl.program_id(0)
        x, y, c = lax.axis_index("x"), lax.axis_index("y"), lax.axis_index("c")
        me, sibling = (x, y, c), (x, y, 1 - c)
        chips = [(1 - x, y), (x, 1 - y), (1 - x, 1 - y)]
        shards = (win_ref, wa_ref, wb_ref, wo_ref)
        gathered = (win_all, wa_all, wb_all, wo_all)
        n_arrays = len(shards)
        blocks = [me, sibling] + [(*chip, c) for chip in chips] + [(*chip, 1 - c) for chip in chips]

        def index(block):
            px, py, pc = block
            return 4 * px + 2 * py + pc

        def part(ref, a, u):
            return ref.at[:, pl.ds(PARTS[u][0], PARTS[u][1])] if a == 0 else ref

        def slot(a, block, u):
            return part(gathered[a].at[index(block)], a, u)

        def sem(a, k, u):
            return n_parts * k + u if a == 0 else 7 * n_parts + 7 * (a - 1) + k

        def copy(a, k, block, to, u=0, from_shard=False):
            return pltpu.make_async_remote_copy(
                src_ref=part(shards[a], a, u) if from_shard else slot(a, block, u), dst_ref=slot(a, block, u),
                send_sem=send_sems.at[sem(a, k, u)], recv_sem=recv_sems.at[sem(a, k, u)],
                device_id=to, device_id_type=MESH)

        def keep(a):
            return pltpu.make_async_copy(shards[a], gathered[a].at[index(me)], local_sems.at[a])

        def load(m):
            s, u = units[m]
            src = part(win_ref, 0, u) if s == 0 else slot(0, blocks[s], u)
            return pltpu.make_async_copy(src, wbuf.at[m % 2, :, 0:PARTS[u][1]], local_sems.at[n_arrays + m % 2])

        def store(m):
            s, u = units[m]
            col0 = pl.multiple_of(index(blocks[s]) * COLS + PARTS[u][0], LANES)
            return pltpu.make_async_copy(rbuf.at[m % 2, :, 0:PARTS[u][1]],
                                         proj_ref.at[:, pl.ds(col0, PARTS[u][1])], local_sems.at[n_arrays + 2 + m % 2])

        def by_x(a, u):
            return u == 0 if a == 0 else a < 3

        def relay(a, u=0):
            src, to = (blocks[3], blocks[2]) if by_x(a, u) else (blocks[2], blocks[3])
            return copy(a, 3, src, to, u)

        def arrive(m):
            s, u = units[m]
            if s == 1:
                copy(0, 0, sibling, me, u).wait_recv()
            elif 2 <= s <= 4:
                copy(0, s - 1, blocks[s], me, u).wait_recv()
                copy(0, s + 2, blocks[s], sibling, u).start()
                if s < 4 and by_x(0, u) == (s == 3):
                    relay(0, u).start()
            elif s >= 5:
                copy(0, s - 1, blocks[s], me, u).wait_recv()

        def pass_on_out(j):
            for a in range(1, 4):
                copy(a, j + 1, blocks[2 + j], me).wait_recv()
                copy(a, j + 4, blocks[2 + j], sibling).start()
                if j < 2 and by_x(a, 0) == (j == 1):
                    relay(a).start()

        targets = [sibling, blocks[2], blocks[3]]
        relays_started = max(handled_at[m] for m, (s, u) in enumerate(units) if s in (2, 3) and by_x(0, u) == (s == 3))

        def small_copies():
            return _all_copies([small_ref], [small_all], small_send, small_recv)

        @pl.when(g == 0)
        def _():
            for cp in small_copies():
                cp.start()
            for a in range(n_arrays):
                keep(a).start()
            for u in range(n_parts):
                for k, to in enumerate(targets):
                    copy(0, k, me, to, u, from_shard=True).start()
            for a in range(1, 4):
                copy(a, 0, me, sibling, from_shard=True).start()
            load(0).start()

        @pl.when(g == n_tiles - 2)
        def _():
            for cp in small_copies():
                cp.wait()
            fetch = pltpu.make_async_copy(small_all, small_buf, local_sems.at[n_arrays + 4])
            fetch.start()
            fetch.wait()
            meta_ref[0:TILE - N_META, :] = jnp.zeros((TILE - N_META, D_MODEL), F32)
            meta_ref[TILE - N_META:TILE, :] = jnp.concatenate([small_buf[d, 0:N_META, :] for d in range(N_DEV)], axis=1)

        @pl.when(g < n_tiles)
        def _():
            s0 = jnp.where(g == n_tiles - 1, meta_ref[...], x_ref[...])
            r = lax.rsqrt(_rowmean(s0 * s0) + EPS)
            h32 = (s0 * r) * g_ref[...]
            ht_ref[...] = h32.T.astype(BF16)
            h_all[pl.ds(pl.multiple_of(g * TILE, TILE), TILE), :] = h32.astype(BF16)

        for m in range(n_units):
            @pl.when(g == n_tiles + m)
            def _(m=m):
                load(m).wait()
                for later in range(m + 1, n_units):
                    if handled_at[later] == m:
                        arrive(later)
                if m + 1 < n_units:
                    load(m + 1).start()
                if m == relays_started:
                    for a in range(1, 4):
                        for k in (1, 2):
                            copy(a, k, me, targets[k], from_shard=True).start()
                if m == n_units - 2:
                    pass_on_out(0)
                    pass_on_out(1)
                if m >= 2:
                    store(m - 2).wait()

        m_now = jnp.maximum(g - n_tiles, 0)
        u_now = functools.reduce(jnp.add, [jnp.where(m_now == m, u, 0) for m, (_, u) in enumerate(units)])
        for u, (_, width) in enumerate(PARTS):
            @pl.when((g >= n_tiles) & (u_now == u))
            def _(width=width):
                w = wbuf[m_now % 2, :, 0:width]
                for r in range(n_chunk):
                    rbuf[m_now % 2, r * chunk:(r + 1) * chunk, 0:width] = _dot(h_all[r * chunk:(r + 1) * chunk, :], w)

        for m in range(n_units):
            @pl.when(g == n_tiles + m)
            def _(m=m):
                store(m).start()

        @pl.when(g == n_steps - 1)
        def _():
            pass_on_out(2)
            store(n_units - 2).wait()
            store(n_units - 1).wait()
            for a in range(1, 4):
                copy(a, 0, sibling, me).wait_recv()
                for j in range(3):
                    copy(a, 4 + j, blocks[5 + j], me).wait_recv()
            for a in range(n_arrays):
                for u in range(n_parts if a == 0 else 1):
                    for k, to in enumerate(targets):
                        copy(a, k, me, to, u, from_shard=True).wait_send()
                    relay(a, u).wait_send()
                    for j in range(3):
                        copy(a, 4 + j, blocks[2 + j], sibling, u).wait_send()
                keep(a).wait()

    n_x = n_tiles - 1
    return pl.pallas_call(
        body, name="gather_norm_proj",
        out_shape=[jax.ShapeDtypeStruct((D_MODEL, tp), BF16), jax.ShapeDtypeStruct((tp, D_IN), F32),
                   jax.ShapeDtypeStruct((TILE, D_MODEL), F32), jax.ShapeDtypeStruct((N_DEV,) + small_shard.shape[1:], F32),
                   jax.ShapeDtypeStruct((N_DEV,) + w_in_shard.shape, BF16)]
                  + [jax.ShapeDtypeStruct((N_DEV,) + w.shape, BF16) for w in w_out_shards],
        grid_spec=pltpu.PrefetchScalarGridSpec(
            num_scalar_prefetch=1, grid=(n_steps,),
            in_specs=[pl.BlockSpec((TILE, D_MODEL), lambda g, pos_ref: (jnp.minimum(g, n_x - 1), 0)),
                      _VMEM, _ANY, _ANY, _ANY, _ANY, _ANY],
            out_specs=[pl.BlockSpec((D_MODEL, TILE), lambda g, pos_ref: (0, jnp.minimum(g, n_tiles - 1))),
                       _ANY, _VMEM, _ANY, _ANY, _ANY, _ANY, _ANY],
            scratch_shapes=[pltpu.VMEM((tp, D_MODEL), BF16), pltpu.VMEM((2, D_MODEL, widest), BF16),
                            pltpu.VMEM((2, tp, widest), F32), pltpu.VMEM((N_DEV,) + small_shard.shape[1:], F32),
                            pltpu.SemaphoreType.DMA((7 * n_parts + 21,)), pltpu.SemaphoreType.DMA((7 * n_parts + 21,)),
                            pltpu.SemaphoreType.DMA((9,)),
                            pltpu.SemaphoreType.DMA((N_DEV,)), pltpu.SemaphoreType.DMA((N_DEV,))]),
        compiler_params=pltpu.CompilerParams(dimension_semantics=("arbitrary",), vmem_limit_bytes=VMEM_LIMIT),
    )(pos, x2d, norm_g, small_shard, w_in_shard, *w_out_shards)


C_AVAL, C_AGLU, C_AZ, C_BB, C_BC, C_BX, C_BZ, C_GA, C_GB = (k * D_MODEL for k in range(9))
S_AZ, S_BB, S_BZ, S_GA, S_GB = (k * D_MODEL for k in range(5))


def _fused_pass(proj, x2d, tgt2d, meta_tile, conv_a_w, conv_a_b, ln_a_g, ln_a_b, b_a_out, conv_b_w, final_g,
                w_a, w_b, w_o, w_a_t, w_b_t, w_o_t, n_tiles):
    T = TILE
    tp = n_tiles * T
    inv_d = 1.0 / D_MODEL

    def block_of(tile):
        return jnp.where(tile == 0, n_tiles - 1, tile - 1)

    def cur(i):
        return block_of(jnp.minimum(i, n_tiles - 1))

    def prev(i):
        return block_of(jnp.clip(i - 1, 0, n_tiles - 1))

    def xblk(i):
        return jnp.maximum(jnp.minimum(i, n_tiles - 1) - 1, 0)

    def body(proj_ref, aprev, cprev, x_ref, tgt_ref, meta_ref, caw_ref, cab_ref, lng_ref, lnb_ref, bao_ref, cbw_ref,
             fg_ref, wa_ref, wb_ref, wo_ref, wat_ref, wbt_ref, wot_ref,
             dproj_ref, ds1_ref, lhs_ref, rhs_ref, small_ref,
             ua0_buf, cb_buf, dua1_buf, dc3_buf, stage, ua1_buf, c3_buf,
             dpa_buf, dpb_buf, dcaw8, dcbw8, shift_buf):
        i = pl.program_id(0)
        this, before = i % 2, 1 - i % 2

        @pl.when(i == 0)
        def _init():
            for buf in (ua0_buf, cb_buf, dua1_buf, dc3_buf, dcaw8, dcbw8):
                buf[...] = jnp.zeros(buf.shape, buf.dtype)
            small_ref[...] = jnp.zeros(small_ref.shape, F32)

        @pl.when(i >= 1)
        def _emit_stage():
            dproj_ref[:, C_AZ:C_BC] = stage[:, S_AZ:S_BZ]
            dproj_ref[:, C_BZ:D_IN] = stage[:, S_BZ:S_GB + D_MODEL]

        @pl.when(i < n_tiles)
        def _front():
            def conv_chunk(cc, carry):
                c0 = pl.multiple_of(cc * LANES, LANES)
                lanes = pl.ds(c0, LANES)

                def col(base):
                    return pl.ds(pl.multiple_of(base + cc * LANES, LANES), LANES)

                ua0 = proj_ref[:, col(C_AVAL)] * _sigmoid(proj_ref[:, col(C_AGLU)])
                ua0_buf[this, 0:HALO, lanes] = ua0_buf[before, T:T + HALO, lanes]
                ua0_buf[this, HALO:HALO + T, lanes] = ua0
                acc = jnp.broadcast_to(cab_ref[:, lanes], (T, LANES))
                lead = HALO - (CONV_A - 1)
                for r in range(SUBLANES):
                    taps = [k for k in range(CONV_A) if (k + lead) % SUBLANES == r]
                    rows = T + SUBLANES * max((k + lead) // SUBLANES for k in taps)
                    if r:
                        shift_buf[r, 0:rows, :] = ua0_buf[this, pl.ds(r, rows), lanes]
                    for k in taps:
                        q = (k + lead) // SUBLANES
                        if r:
                            win = shift_buf[r, SUBLANES * q:SUBLANES * q + T, :]
                        else:
                            win = ua0_buf[this, pl.ds(SUBLANES * q, T), lanes]
                        acc = acc + caw_ref[k:k + 1, lanes] * win
                ua1_buf[:, lanes] = acc
                cb = proj_ref[:, col(C_BC)] * proj_ref[:, col(C_BX)]
                cb_buf[this, 0:SUBLANES, lanes] = cb_buf[before, T:T + SUBLANES, lanes]
                cb_buf[this, SUBLANES:SUBLANES + T, lanes] = cb
                lead_b = SUBLANES - (CONV_B - 1)
                acc3 = cbw_ref[0:1, lanes] * cb_buf[this, pl.ds(lead_b, T), lanes]
                for k in range(1, CONV_B):
                    acc3 = acc3 + cbw_ref[k:k + 1, lanes] * cb_buf[this, pl.ds(lead_b + k, T), lanes]
                c3_buf[:, lanes] = acc3
                return carry

            lax.fori_loop(0, N_CHUNK, conv_chunk, 0)

            ua1 = ua1_buf[...]
            xc = ua1 - _rowmean(ua1)
            rstd = lax.rsqrt(_rowmean(xc * xc) + EPS)
            xhat = xc * rstd
            ua2 = xhat * lng_ref[...] + lnb_ref[...]
            sg2 = _sigmoid(ua2)
            ua3 = ua2 * sg2
            a_z = proj_ref[:, C_AZ:C_AZ + D_MODEL]
            sz = _sigmoid(a_z)
            silu_az = a_z * sz
            lhs_ref[0] = (ua3 * silu_az).astype(BF16)
            b_z = proj_ref[:, C_BZ:C_BZ + D_MODEL]
            sbz = _sigmoid(b_z)
            silu_bz = b_z * sbz
            b_b = proj_ref[:, C_BB:C_BB + D_MODEL]
            c3 = c3_buf[...]
            ub = b_b * c3
            lhs_ref[1] = (ub * silu_bz).astype(BF16)

            ya = _dot(lhs_ref[0], wa_ref[...]) + bao_ref[...]
            yb = _dot(lhs_ref[1], wb_ref[...])
            sga = _sigmoid(proj_ref[:, C_GA:C_GA + D_MODEL])
            sgb = _sigmoid(proj_ref[:, C_GB:C_GB + D_MODEL])
            m_b = (sga * ya + sgb * yb).astype(BF16)
            lhs_ref[2] = m_b
            s0 = jnp.where(i == 0, meta_ref[...], x_ref[...])
            s1 = s0 + _dot(m_b, wo_ref[...])
            r1 = lax.rsqrt(_rowmean(s1 * s1) + EPS)
            y = (s1 * r1) * fg_ref[...]
            is_token = (i >= 1).astype(F32)
            err = (y - tgt_ref[...]) * is_token
            small_ref[ROW_LOSS:ROW_LOSS + 1, :] += (0.5 * inv_d) * _colsum(err * err)
            dy = err * inv_d
            small_ref[ROW_FINAL_G:ROW_FINAL_G + 1, :] += _colsum(dy * (s1 * r1))
            gy = dy * fg_ref[...]
            ds1 = r1 * gy - s1 * ((r1 * r1 * r1) * _rowmean(gy * s1))
            ds1_ref[...] = ds1
            ds1_b = ds1.astype(BF16)
            rhs_ref[2] = ds1_b
            dm = _dot(ds1_b, wot_ref[...])
            dya = dm * sga
            dyb = dm * sgb
            stage[:, S_GA:S_GA + D_MODEL] = (dya * ya * (1.0 - sga)).astype(BF16)
            stage[:, S_GB:S_GB + D_MODEL] = (dyb * yb * (1.0 - sgb)).astype(BF16)
            small_ref[ROW_B_A_OUT:ROW_B_A_OUT + 1, :] += _colsum(dya)
            dya_b = dya.astype(BF16)
            dyb_b = dyb.astype(BF16)
            rhs_ref[0] = dya_b
            rhs_ref[1] = dyb_b
            dpa_buf[...] = _dot(dya_b, wat_ref[...])
            dpb_buf[...] = _dot(dyb_b, wbt_ref[...])

            dpa = dpa_buf[...]
            stage[:, S_AZ:S_AZ + D_MODEL] = (dpa * ua3 * (sz + silu_az * (1.0 - sz))).astype(BF16)
            dua2 = dpa * silu_az * (sg2 + ua3 * (1.0 - sg2))
            small_ref[ROW_LN_G:ROW_LN_G + 1, :] += _colsum(dua2 * xhat)
            small_ref[ROW_LN_B:ROW_LN_B + 1, :] += _colsum(dua2)
            dxh = dua2 * lng_ref[...]
            dua1 = rstd * (dxh - _rowmean(dxh) - xhat * _rowmean(dxh * xhat))
            small_ref[ROW_CONV_A_B:ROW_CONV_A_B + 1, :] += _colsum(dua1)
            dua1_buf[this, 0:T, :] = dua1
            dua1_buf[before, T:T + HALO, :] = dua1[0:HALO]
            dpb = dpb_buf[...]
            stage[:, S_BZ:S_BZ + D_MODEL] = (dpb * ub * (sbz + silu_bz * (1.0 - sbz))).astype(BF16)
            dub = dpb * silu_bz
            stage[:, S_BB:S_BB + D_MODEL] = (dub * c3).astype(BF16)
            dc3 = dub * b_b
            dc3_buf[this, 0:T, :] = dc3
            dc3_buf[before, T:T + SUBLANES, :] = dc3[0:SUBLANES]

        @pl.when(i == n_tiles)
        def _no_later_tile():
            dua1_buf[before, T:T + HALO, :] = jnp.zeros((HALO, D_MODEL), F32)
            dc3_buf[before, T:T + SUBLANES, :] = jnp.zeros((SUBLANES, D_MODEL), F32)

        @pl.when(i >= 1)
        def _lagged():
            def convt_chunk(cc, carry):
                c0 = pl.multiple_of(cc * LANES, LANES)
                lanes = pl.ds(c0, LANES)

                def col(base):
                    return pl.ds(pl.multiple_of(base + cc * LANES, LANES), LANES)

                ua0 = ua0_buf[before, HALO:HALO + T, lanes]
                acc = jnp.zeros((T, LANES), F32)
                for r in range(SUBLANES):
                    shifts = [j for j in range(CONV_A) if j % SUBLANES == r]
                    rows = T + shifts[-1] - r
                    if r:
                        shift_buf[r, 0:rows, :] = dua1_buf[before, pl.ds(r, rows), lanes]
                    for j in shifts:
                        k = CONV_A - 1 - j
                        if r:
                            later = shift_buf[r, j - r:j - r + T, :]
                        else:
                            later = dua1_buf[before, pl.ds(j, T), lanes]
                        acc = acc + caw_ref[k:k + 1, lanes] * later
                        dcaw8[SUBLANES * k:SUBLANES * (k + 1), lanes] += _fold8(ua0 * later)
                a_val = aprev[:, col(0)]
                sg = _sigmoid(aprev[:, col(D_MODEL)])
                dproj_ref[:, col(C_AVAL)] = (acc * sg).astype(BF16)
                dproj_ref[:, col(C_AGLU)] = (acc * a_val * (sg * (1.0 - sg))).astype(BF16)

                cb = cb_buf[before, SUBLANES:SUBLANES + T, lanes]
                acc3 = jnp.zeros((T, LANES), F32)
                for j in range(CONV_B):
                    k = CONV_B - 1 - j
                    later = dc3_buf[before, pl.ds(j, T), lanes]
                    acc3 = acc3 + cbw_ref[k:k + 1, lanes] * later
                    dcbw8[SUBLANES * k:SUBLANES * (k + 1), lanes] += _fold8(cb * later)
                dproj_ref[:, col(C_BC)] = (acc3 * cprev[:, col(D_MODEL)]).astype(BF16)
                dproj_ref[:, col(C_BX)] = (acc3 * cprev[:, col(0)]).astype(BF16)
                return carry

            lax.fori_loop(0, N_CHUNK, convt_chunk, 0)

        @pl.when(i == n_tiles)
        def _finish():
            for k in range(CONV_A):
                small_ref[ROW_CONV_A_W + k:ROW_CONV_A_W + k + 1, :] = _colsum(dcaw8[SUBLANES * k:SUBLANES * (k + 1), :])
            for k in range(CONV_B):
                small_ref[ROW_CONV_B_W + k:ROW_CONV_B_W + k + 1, :] = _colsum(dcbw8[SUBLANES * k:SUBLANES * (k + 1), :])

    pair = 2 * D_MODEL
    return pl.pallas_call(
        body, name="fused_pass", grid=(n_tiles + 1,),
        out_shape=[
            jax.ShapeDtypeStruct((tp, D_IN), BF16),
            jax.ShapeDtypeStruct((tp, D_MODEL), F32),
            jax.ShapeDtypeStruct((3, tp, D_MODEL), BF16),
            jax.ShapeDtypeStruct((3, tp, D_MODEL), BF16),
            jax.ShapeDtypeStruct((SMALL_A_ROWS, D_MODEL), F32),
        ],
        in_specs=[
            pl.BlockSpec((T, D_IN), lambda i: (cur(i), 0)),
            pl.BlockSpec((T, pair), lambda i: (prev(i), C_AVAL // pair)),
            pl.BlockSpec((T, pair), lambda i: (prev(i), C_BC // pair)),
            pl.BlockSpec((T, D_MODEL), lambda i: (xblk(i), 0)),
            pl.BlockSpec((T, D_MODEL), lambda i: (xblk(i), 0)),
            _VMEM, _VMEM, _VMEM, _VMEM, _VMEM, _VMEM, _VMEM, _VMEM,
            *[_resident((D_MODEL, D_MODEL)) for _ in range(6)],
        ],
        out_specs=[
            pl.BlockSpec((T, D_IN), lambda i: (prev(i), 0)),
            pl.BlockSpec((T, D_MODEL), lambda i: (cur(i), 0)),
            pl.BlockSpec((3, T, D_MODEL), lambda i: (0, cur(i), 0)),
            pl.BlockSpec((3, T, D_MODEL), lambda i: (0, cur(i), 0)),
            _VMEM,
        ],
        scratch_shapes=[
            pltpu.VMEM((2, HALO + T, D_MODEL), F32),
            pltpu.VMEM((2, SUBLANES + T, D_MODEL), F32),
            pltpu.VMEM((2, T + HALO, D_MODEL), F32),
            pltpu.VMEM((2, T + SUBLANES, D_MODEL), F32),
            pltpu.VMEM((T, 5 * D_MODEL), BF16),
            pltpu.VMEM((T, D_MODEL), F32),
            pltpu.VMEM((T, D_MODEL), F32),
            pltpu.VMEM((T, D_MODEL), F32),
            pltpu.VMEM((T, D_MODEL), F32),
            pltpu.VMEM((32 * SUBLANES, D_MODEL), F32),
            pltpu.VMEM((SUBLANES * SUBLANES, D_MODEL), F32),
            pltpu.VMEM((SUBLANES, T + HALO, LANES), F32),
        ],
        compiler_params=pltpu.CompilerParams(dimension_semantics=("arbitrary",), vmem_limit_bytes=VMEM_LIMIT),
    )(proj, proj, proj, x2d, tgt2d, meta_tile, conv_a_w, conv_a_b, ln_a_g, ln_a_b, b_a_out, conv_b_w, final_g,
      w_a, w_b, w_o, w_a_t, w_b_t, w_o_t)


def _input_bwd(dproj, ds1, x2d, meta_tile, norm_g, w_in_all, row_tile, after):
    seq = x2d.shape[0]
    n_steps = seq // row_tile
    meta_block = seq // TILE

    def backward(dp_ref, ds1_ref, s0_ref, g_ref, w_ref, out_ref, vec_ref):
        dh = _dot_nt(dp_ref[:, 0:COLS], w_ref[0])
        for j in range(1, N_DEV):
            dh = dh + _dot_nt(dp_ref[:, j * COLS:(j + 1) * COLS], w_ref[j])
        s0v = s0_ref[...]
        r = lax.rsqrt(_rowmean(s0v * s0v) + EPS)
        gh = dh * g_ref[...]
        out_ref[...] = ds1_ref[...] + r * gh - s0v * ((r * r * r) * _rowmean(gh * s0v))
        vec_ref[ROW_NORM_G:ROW_NORM_G + 1, :] += _colsum(dh * (s0v * r))

    def body(dp_ref, ds1_ref, x_ref, dpm_ref, ds1m_ref, meta_ref, g_ref, w_ref, after_ref, gx_ref, small_ref, gmeta_buf):
        t = pl.program_id(0)

        @pl.when(t == 0)
        def _():
            small_ref[...] = jnp.zeros(small_ref.shape, F32)

        backward(dp_ref, ds1_ref, x_ref, g_ref, w_ref, gx_ref, small_ref)

        @pl.when(t == n_steps - 1)
        def _():
            backward(dpm_ref, ds1m_ref, meta_ref, g_ref, w_ref, gmeta_buf, small_ref)
            small_ref[ROW_META:ROW_META + N_META, :] = gmeta_buf[TILE - N_META:TILE, :]

    return pl.pallas_call(
        body, name="input_bwd", grid=(n_steps,),
        out_shape=[jax.ShapeDtypeStruct(x2d.shape, F32), jax.ShapeDtypeStruct((SMALL_B_ROWS, D_MODEL), F32)],
        in_specs=[pl.BlockSpec((row_tile, D_IN), lambda t: (t, 0)),
                  pl.BlockSpec((row_tile, D_MODEL), lambda t: (t, 0)),
                  pl.BlockSpec((row_tile, D_MODEL), lambda t: (t, 0)),
                  pl.BlockSpec((TILE, D_IN), lambda t: (meta_block, 0)),
                  pl.BlockSpec((TILE, D_MODEL), lambda t: (meta_block, 0)),
                  _VMEM, _VMEM, _resident((N_DEV, D_MODEL, COLS)), _ANY],
        out_specs=[pl.BlockSpec((row_tile, D_MODEL), lambda t: (t, 0)), _VMEM],
        scratch_shapes=[pltpu.VMEM((TILE, D_MODEL), F32)],
        compiler_params=pltpu.CompilerParams(dimension_semantics=("arbitrary",), vmem_limit_bytes=VMEM_LIMIT),
    )(dproj, ds1, x2d, dproj, ds1, meta_tile, norm_g, w_in_all, after)


def _grad_w_in_half(pos, h_t, dproj, k_tile, other_side, rides, name, after=None, add_to=None, narrow=False):
    tp = h_t.shape[1]
    n_k = tp // k_tile
    order = [] if after is None else [after]
    summing = add_to is not None
    assert not (summing and narrow)

    def column_block(q, k, pos_ref):
        return k, 2 * q + (1 - pos_ref[2] if other_side else pos_ref[2])

    def body(pos_ref, h_ref, dp_ref, *refs):
        acc = refs[-1]

        @pl.when(pl.program_id(1) == 0)
        def _():
            acc[...] = refs[0][...].astype(F32) if summing else jnp.zeros(acc.shape, F32)

        acc[...] += _dot(h_ref[...], dp_ref[...])

        if summing or narrow:
            @pl.when(pl.program_id(1) == n_k - 1)
            def _():
                refs[-2][...] = acc[...].astype(BF16)

        if summing:
            @pl.when((pl.program_id(1) == n_k - 1) & (pl.program_id(0) == 2 * pos_ref[0] + pos_ref[1]))
            def _():
                refs[-3][...] = acc[...]

    ride = [a for _, arrays in rides for a in arrays]
    n_arr = len(ride)
    ride_shapes, ride_sems = _ride_shapes(rides)
    block = (None, D_MODEL, COLS)
    extra_in = [add_to] if summing else []
    extra_in_specs = [pl.BlockSpec(block, lambda q, k, pos_ref: (q, 0, 0))] if summing else []
    extra_out = [jax.ShapeDtypeStruct((4, D_MODEL, COLS), BF16)] if summing else []
    extra_out_specs = [pl.BlockSpec(block, lambda q, k, pos_ref: (q ^ (2 * pos_ref[0] + pos_ref[1]), 0, 0))] \
        if summing else []
    body = _riding(body, 3 + len(extra_in) + len(order), 1 + len(extra_out), rides,
                   lambda: (pl.program_id(0) == 0) & (pl.program_id(1) == 0),
                   lambda: (pl.program_id(0) == 3) & (pl.program_id(1) == n_k - 1))
    return pl.pallas_call(
        body, name=name,
        out_shape=[jax.ShapeDtypeStruct((1 if summing else 4, D_MODEL, COLS), BF16 if narrow else F32)]
        + extra_out + ride_shapes,
        grid_spec=pltpu.PrefetchScalarGridSpec(
            num_scalar_prefetch=1, grid=(4, n_k),
            in_specs=[pl.BlockSpec((D_MODEL, k_tile), lambda q, k, pos_ref: (0, k)),
                      pl.BlockSpec((k_tile, COLS), column_block)] + extra_in_specs + [_ANY] * (len(order) + n_arr),
            out_specs=[pl.BlockSpec(block, lambda q, k, pos_ref: (0 if summing else q, 0, 0))]
            + extra_out_specs + [_ANY] * n_arr,
            scratch_shapes=([pltpu.VMEM((D_MODEL, COLS), F32)] if summing or narrow else []) + ride_sems),
        compiler_params=pltpu.CompilerParams(dimension_semantics=("arbitrary", "arbitrary"),
                                             vmem_limit_bytes=VMEM_LIMIT),
    )(pos, h_t, dproj, *extra_in, *order, *ride)


def _grad_w_out(lhs, rhs, k_tile, after):
    tp = lhs.shape[1]

    def body(a_ref, b_ref, after_ref, o_ref):
        @pl.when(pl.program_id(1) == 0)
        def _():
            o_ref[...] = jnp.zeros(o_ref.shape, F32)

        o_ref[...] += _dot_tn(a_ref[...], b_ref[...]).reshape(N_DEV, ROWS_OUT, D_MODEL)

    return pl.pallas_call(
        body, name="grad_w_out", grid=(3, tp // k_tile),
        out_shape=jax.ShapeDtypeStruct((N_DEV, 3, ROWS_OUT, D_MODEL), F32),
        in_specs=[pl.BlockSpec((None, k_tile, D_MODEL), lambda w, k: (w, k, 0)),
                  pl.BlockSpec((None, k_tile, D_MODEL), lambda w, k: (w, k, 0)), _ANY],
        out_specs=pl.BlockSpec((N_DEV, None, ROWS_OUT, D_MODEL), lambda w, k: (0, w, 0, 0)),
        compiler_params=pltpu.CompilerParams(dimension_semantics=("arbitrary", "arbitrary"),
                                             vmem_limit_bytes=VMEM_LIMIT),
    )(lhs, rhs, after)


def _adamw_math(w, g, m, v):
    m = ADAM_B1 * m + (1.0 - ADAM_B1) * g
    v = ADAM_B2 * v + (1.0 - ADAM_B2) * (g * g)
    m_hat = m / (1.0 - ADAM_B1 ** ADAM_STEP)
    v_hat = v / (1.0 - ADAM_B2 ** ADAM_STEP)
    delta = -ADAM_LR * (m_hat / (jnp.sqrt(v_hat) + ADAM_EPS) + ADAM_WD * w)
    return delta, m, v


def _adamw_sharded(pos, mine, theirs, landed, weights, row_tile, name, after=None):
    order = [] if after is None else [after]
    rows, n = weights[0][0].shape
    n_slots = mine.shape[0]
    per_shard = rows // row_tile
    assert per_shard == 1 or len(weights) == 1

    def mine_map(j, t, pos_ref):
        chip = 2 * pos_ref[0] + pos_ref[1]
        return {N_DEV: 2 * chip + pos_ref[2], 4: chip, 1: 0}[n_slots], j * per_shard + t, 0

    def theirs_map(j, t, pos_ref):
        return 2 * pos_ref[0] + pos_ref[1], j * per_shard + t, 0

    def body(pos_ref, mine_ref, *refs):
        if theirs is not None:
            g = mine_ref[...] + refs[0][...]
            refs = refs[1:]
        else:
            g = mine_ref[...]
        land_ref, refs = refs[0], refs[1:]
        ins, outs = refs[:3 * len(weights)], refs[3 * len(weights) + len(order):]
        for k in range(3):
            g = g + land_ref[k].astype(F32)
        for j in range(len(weights)):
            @pl.when(pl.program_id(0) == j)
            def _(j=j):
                w_ref, m_ref, v_ref = ins[3 * j:3 * j + 3]
                delta, m_new, v_new = _adamw_math(w_ref[...], g, m_ref[...], v_ref[...])
                for ref, val in zip(outs[4 * j:4 * j + 4], (g, delta, m_new, v_new)):
                    ref[...] = val

    tile = pl.BlockSpec((row_tile, n), lambda j, t, pos_ref: (t, 0))
    res = pl.pallas_call(
        body, name=name,
        out_shape=[jax.ShapeDtypeStruct((rows, n), F32)] * (4 * len(weights)),
        grid_spec=pltpu.PrefetchScalarGridSpec(
            num_scalar_prefetch=1, grid=(len(weights), per_shard),
            in_specs=[pl.BlockSpec((None, row_tile, n), mine_map)]
            + ([pl.BlockSpec((None, row_tile, n), theirs_map)] if theirs is not None else [])
            + [pl.BlockSpec((3, row_tile, n), lambda j, t, pos_ref: (0, j * per_shard + t, 0))]
            + [tile] * (3 * len(weights)) + [_ANY] * len(order),
            out_specs=[tile] * (4 * len(weights))),
        compiler_params=pltpu.CompilerParams(dimension_semantics=("arbitrary", "arbitrary")),
    )(pos, mine, *([theirs] if theirs is not None else []), landed, *[a for wmv in weights for a in wmv], *order)
    return [res[4 * j:4 * j + 4] for j in range(len(weights))]


def _adamw_small(pos, gathered, params, row_by_row=()):
    n_par, n_src = len(params), len(gathered)

    def body(pos_ref, *refs):
        g_refs, gc_refs = refs[:n_src], refs[n_src:2 * n_src]
        ins = refs[2 * n_src:2 * n_src + 3 * n_par]
        outs = refs[2 * n_src + 3 * n_par:]
        loss_ref = outs[4 * n_par]

        def reduced(ref, row, n_rows):
            g = ref[0, row:row + n_rows, :]
            for d in range(1, N_DEV):
                g = g + ref[d, row:row + n_rows, :]
            return g

        for p, (src, row, n_rows, sharded, _, _, _) in enumerate(params):
            g = reduced((gc_refs if sharded else g_refs)[src], row, n_rows)
            w_ref, m_ref, v_ref = ins[3 * p:3 * p + 3]
            delta, m_new, v_new = _adamw_math(w_ref[...], g, m_ref[...], v_ref[...])
            for kind, res in enumerate((g, delta, m_new, v_new)):
                if p in row_by_row:
                    for k in range(n_rows):
                        outs[4 * p + kind][k] = res[k:k + 1, :]
                else:
                    outs[4 * p + kind][...] = res
        loss = jnp.sum(reduced(g_refs[0], ROW_LOSS, 1), axis=1, keepdims=True)
        loss_ref[...] = jnp.broadcast_to(loss, loss_ref.shape)

    out_shape = []
    for p, (_, _, n_rows, _, w, _, _) in enumerate(params):
        out_shape += [jax.ShapeDtypeStruct((n_rows, 1, LANES) if p in row_by_row else w.shape, F32)] * 4
    out_shape.append(jax.ShapeDtypeStruct((1, LANES), F32))
    flat = [a for (_, _, _, _, w, m, v) in params for a in (w, m, v)]
    my_lanes = [pl.BlockSpec((N_DEV, g.shape[1], LANES), lambda i, pos_ref: (0, 0, 4 * pos_ref[0] + 2 * pos_ref[1] + pos_ref[2]))
                for g in gathered]
    return pl.pallas_call(
        body, name="adamw_small", out_shape=out_shape,
        grid_spec=pltpu.PrefetchScalarGridSpec(
            num_scalar_prefetch=1, grid=(1,),
            in_specs=[_VMEM] * n_src + my_lanes + [_VMEM] * len(flat), out_specs=[_VMEM] * len(out_shape)),
    )(pos, *gathered, *gathered, *flat)


def _pad_rows(a, rows):
    return jnp.concatenate([a, jnp.zeros((rows - a.shape[0], a.shape[1]), a.dtype)], axis=0)


def kernel(x, meta_tokens, norm_g, w_in, conv_a_w, conv_a_b, ln_a_g, ln_a_b, w_a_out, b_a_out, conv_b_w, w_b_out, w_out, final_g, loss_target, m_meta_tokens, m_norm_g, m_w_in, m_conv_a_w, m_conv_a_b, m_ln_a_g, m_ln_a_b, m_w_a_out, m_b_a_out, m_conv_b_w, m_w_b_out, m_w_out, m_final_g, v_meta_tokens, v_norm_g, v_w_in, v_conv_a_w, v_conv_a_b, v_ln_a_g, v_ln_a_b, v_w_a_out, v_b_a_out, v_conv_b_w, v_w_b_out, v_w_out, v_final_g):
    seq = x.shape[1]
    assert x.shape == (1, seq, D_MODEL) and seq % TILE == 0 and w_in.shape == (1, D_MODEL, COLS)
    n_tiles = seq // TILE + 1
    tp = n_tiles * TILE
    pos = jnp.stack([lax.axis_index("x"), lax.axis_index("y"), lax.axis_index("c")]).astype(jnp.int32)
    x2d = x[0]
    tgt2d = loss_target[0]

    small = jnp.concatenate([meta_tokens, _pad_rows(conv_a_w[0], 32), _pad_rows(conv_b_w[0], SUBLANES)], axis=0)
    final_g2 = final_g.reshape(1, D_MODEL)

    w_out_shards = [w[0].astype(BF16) for w in (w_a_out, w_b_out, w_out)]
    h_t, proj, meta_tile, small_params, w_in_all, *w_out_all = _gather_norm_proj(
        pos, x2d, small[None], norm_g, w_in[0].astype(BF16), w_out_shards, 3)
    small_params = small_params.transpose(1, 0, 2).reshape(small.shape[0], D_MODEL)
    conv_a_full, conv_b_full = small_params[N_META:N_META + 32], small_params[N_META + 32:]
    w_out_all = [w.reshape(D_MODEL, D_MODEL) for w in w_out_all]
    w_out_all_t = [w.T for w in w_out_all]
    dproj, ds1, lhs, rhs, small_a = _fused_pass(
        proj, x2d, tgt2d, meta_tile, conv_a_full, conv_a_b, ln_a_g, ln_a_b, b_a_out, conv_b_full, final_g2,
        w_out_all[0], w_out_all[1], w_out_all[2], w_out_all_t[0], w_out_all_t[1], w_out_all_t[2], n_tiles)
    k_tile = tp // 3
    gw_far, small_a_all = _grad_w_in_half(pos, h_t, dproj, k_tile, True, [("all", (small_a[None],))], "grad_w_in_far",
                                          narrow=True)
    sems, sent, landing, token = _start_exchanges([("sibling_half", (gw_far,))], "rs_far_start")
    gw_out = _grad_w_out(lhs, rhs, k_tile, token).reshape(N_DEV, 3 * ROWS_OUT, D_MODEL)
    (their_in,) = _wait_exchanges([("sibling_half", 1)], sems, sent, landing, gw_out, "rs_far_wait")
    sems_o, sent_o, landing_o, token = _start_exchanges([("sibling", (gw_out,))], "rs_out_start")
    gw_near, parts_in = _grad_w_in_half(pos, h_t, dproj, k_tile, False, [], "grad_w_in_near", after=token,
                                        add_to=their_in)
    sems_i, sent_i, landing_i, token = _start_exchanges([("chips_by_relation", (parts_in,))], "rs_chips_in_start")
    gw_out, their_out = _wait_exchanges([("sibling", 1)], sems_o, sent_o, landing_o, token, "rs_out_wait",
                                        keep_sources=True)
    parts_out = _chip_partial(pos, gw_out, their_out, (1, 2, 3), BF16, 3 * ROWS_OUT, "rs_parts_w_out")
    sems_o, sent_o, landing_o, token = _start_exchanges([("chips", (parts_out,))], "rs_chips_out_start")
    grad_x, small_b = _input_bwd(dproj, ds1, x2d, meta_tile, norm_g, w_in_all, min(512, seq), token)

    sems_s, sent_s, landing_s, token = _start_exchanges([("all", (small_b[None],))], "gather_small_grads_start")
    (land_in,) = _wait_exchanges([("chips_by_relation", 1)], sems_i, sent_i, landing_i, token, "rs_chips_in_wait")
    (res_in,) = _adamw_sharded(pos, gw_near, None, land_in, [(w_in[0], m_w_in[0], v_w_in[0])], 128, "adamw_w_in")
    (land_out,) = _wait_exchanges([("chips", 1)], sems_o, sent_o, landing_o, res_in[0], "rs_chips_out_wait")
    res_out = _adamw_sharded(
        pos, gw_out, their_out, land_out,
        [(w_a_out[0], m_w_a_out[0], v_w_a_out[0]), (w_b_out[0], m_w_b_out[0], v_w_b_out[0]),
         (w_out[0], m_w_out[0], v_w_out[0])], ROWS_OUT, "adamw_w_out")
    (small_b_all,) = _wait_exchanges([("all", 1)], sems_s, sent_s, landing_s, res_out[2][0], "gather_small_grads_wait")
    small_grads = [small_a_all, small_b_all]
    params = [
        (1, ROW_META, N_META, True, meta_tokens, m_meta_tokens, v_meta_tokens),
        (1, ROW_NORM_G, 1, False, norm_g, m_norm_g, v_norm_g),
        (0, ROW_CONV_A_W, CONV_A, True, conv_a_w[0], m_conv_a_w[0], v_conv_a_w[0]),
        (0, ROW_CONV_A_B, 1, False, conv_a_b, m_conv_a_b, v_conv_a_b),
        (0, ROW_LN_G, 1, False, ln_a_g, m_ln_a_g, v_ln_a_g),
        (0, ROW_LN_B, 1, False, ln_a_b, m_ln_a_b, v_ln_a_b),
        (0, ROW_B_A_OUT, 1, False, b_a_out, m_b_a_out, v_b_a_out),
        (0, ROW_CONV_B_W, CONV_B, True, conv_b_w[0], m_conv_b_w[0], v_conv_b_w[0]),
        (0, ROW_FINAL_G, 1, False, final_g2, m_final_g.reshape(1, D_MODEL), v_final_g.reshape(1, D_MODEL)),
    ]
    conv_weights = (2, 7)
    res_small = _adamw_small(pos, small_grads, params, conv_weights)
    loss = res_small[-1][0, 0]

    def small_res(p, kind, shape):
        res = res_small[4 * p + kind]
        return res.transpose(1, 0, 2) if p in conv_weights else res.reshape(shape)

    per_weight = []
    for kind in range(4):
        per_weight.append([
            small_res(0, kind, meta_tokens.shape),
            small_res(1, kind, norm_g.shape),
            res_in[kind].reshape(w_in.shape),
            small_res(2, kind, conv_a_w.shape),
            small_res(3, kind, conv_a_b.shape),
            small_res(4, kind, ln_a_g.shape),
            small_res(5, kind, ln_a_b.shape),
            res_out[0][kind].reshape(w_a_out.shape),
            small_res(6, kind, b_a_out.shape),
            small_res(7, kind, conv_b_w.shape),
            res_out[1][kind].reshape(w_b_out.shape),
            res_out[2][kind].reshape(w_out.shape),
            small_res(8, kind, final_g.shape),
        ])
    return (loss, grad_x.reshape(x.shape), *per_weight[0], *per_weight[1], *per_weight[2], *per_weight[3])
```

```python
import functools

import jax
import jax.numpy as jnp
from jax import lax
from jax.experimental import pallas as pl
from jax.experimental.pallas import tpu as pltpu

D_MODEL = 1024
N_META = 16
N_DEV = 8
D_IN = 9 * D_MODEL
COLS = D_IN // N_DEV
ROWS_OUT = D_MODEL // N_DEV
CONV_A = 31
CONV_B = 3
EPS = 1e-6

ADAM_LR = 0.001
ADAM_B1 = 0.9
ADAM_B2 = 0.999
ADAM_EPS = 1e-08
ADAM_WD = 0.01
ADAM_STEP = 10

TILE = 128
LANES = 128
N_CHUNK = D_MODEL // LANES
HALO = 32
SUBLANES = 8
VMEM_LIMIT = 56 * 1024 * 1024

ROW_FINAL_G, ROW_B_A_OUT, ROW_LN_G, ROW_LN_B, ROW_CONV_A_B, ROW_LOSS = 0, 1, 2, 3, 4, 5
ROW_CONV_A_W, ROW_CONV_B_W, SMALL_A_ROWS = 8, 40, 48
ROW_NORM_G, ROW_META, SMALL_B_ROWS = 0, 8, 24

MESH = pl.DeviceIdType.MESH
_ANY = pl.BlockSpec(memory_space=pl.ANY)
_VMEM = pl.BlockSpec(memory_space=pltpu.VMEM)
BF16 = jnp.bfloat16
F32 = jnp.float32


def _resident(shape):
    return pl.BlockSpec(shape, lambda *_: (0,) * len(shape), pipeline_mode=pl.Buffered(1))


def _sigmoid(v):
    return jax.nn.sigmoid(v)


def _dot(a, b):
    return jnp.dot(a, b, preferred_element_type=F32)


def _dot_nt(a, b):
    return lax.dot_general(a, b, (((1,), (1,)), ((), ())), preferred_element_type=F32)


def _dot_tn(a, b):
    return lax.dot_general(a, b, (((0,), (0,)), ((), ())), preferred_element_type=F32)


def _colsum(v):
    return jnp.sum(v, axis=0, keepdims=True)


def _rowmean(v):
    parts = [v[:, LANES * c:LANES * (c + 1)] for c in range(v.shape[1] // LANES)]
    return jnp.sum(functools.reduce(jnp.add, parts), axis=-1, keepdims=True) * (1.0 / v.shape[1])


def _fold8(v):
    parts = [v[SUBLANES * g:SUBLANES * (g + 1)] for g in range(v.shape[0] // SUBLANES)]
    return functools.reduce(jnp.add, parts)


def _sibling_copies(srcs, dsts, send_sems, recv_sems):
    x, y, c = lax.axis_index("x"), lax.axis_index("y"), lax.axis_index("c")
    return [pltpu.make_async_remote_copy(
        src_ref=src.at[2 * q + (1 - c)], dst_ref=dst.at[q],
        send_sem=send_sems.at[4 * a + q], recv_sem=recv_sems.at[4 * a + q],
        device_id=(x, y, 1 - c), device_id_type=MESH)
        for a, (src, dst) in enumerate(zip(srcs, dsts)) for q in range(4)]


def _chip_copies(srcs, dsts, send_sems, recv_sems):
    x, y, c = lax.axis_index("x"), lax.axis_index("y"), lax.axis_index("c")
    targets = [(x, 1 - y, c), (1 - x, y, c), (1 - x, 1 - y, c)]
    return [pltpu.make_async_remote_copy(
        src_ref=src.at[k], dst_ref=dst.at[k],
        send_sem=send_sems.at[3 * a + k], recv_sem=recv_sems.at[3 * a + k],
        device_id=targets[k], device_id_type=MESH)
        for a, (src, dst) in enumerate(zip(srcs, dsts)) for k in range(3)]


def _sibling_half_copies(srcs, dsts, send_sems, recv_sems):
    x, y, c = lax.axis_index("x"), lax.axis_index("y"), lax.axis_index("c")
    return [pltpu.make_async_remote_copy(
        src_ref=src.at[q], dst_ref=dst.at[q],
        send_sem=send_sems.at[4 * a + q], recv_sem=recv_sems.at[4 * a + q],
        device_id=(x, y, 1 - c), device_id_type=MESH)
        for a, (src, dst) in enumerate(zip(srcs, dsts)) for q in range(4)]


def _all_copies(srcs, dsts, send_sems, recv_sems):
    x, y, c = lax.axis_index("x"), lax.axis_index("y"), lax.axis_index("c")
    mine = 4 * x + 2 * y + c
    copies = []
    for a, (src, dst) in enumerate(zip(srcs, dsts)):
        copies.append(pltpu.make_async_copy(src.at[0], dst.at[mine], send_sems.at[N_DEV * a]))
        for k in range(1, N_DEV):
            copies.append(pltpu.make_async_remote_copy(
                src_ref=src.at[0], dst_ref=dst.at[mine],
                send_sem=send_sems.at[N_DEV * a + k], recv_sem=recv_sems.at[N_DEV * a + k],
                device_id=(x ^ (k >> 2), y ^ ((k >> 1) & 1), c ^ (k & 1)), device_id_type=MESH))
    return copies


def _chip_copies_by_relation(srcs, dsts, send_sems, recv_sems):
    return _chip_copies([src.at[pl.ds(1, 3)] for src in srcs], dsts, send_sems, recv_sems)


_EXCHANGES = {"sibling": (4, _sibling_copies, 4), "sibling_half": (4, _sibling_half_copies, 4),
              "chips": (3, _chip_copies, 3), "chips_by_relation": (3, _chip_copies_by_relation, 3),
              "all": (N_DEV, _all_copies, N_DEV)}


def _exchange_shapes(kind, arrays):
    per_array, _, slots = _EXCHANGES[kind]
    out_shape = [jax.ShapeDtypeStruct((slots,) + a.shape[1:], a.dtype) for a in arrays]
    sems = [pltpu.SemaphoreType.DMA((per_array * len(arrays),))] * 2
    return out_shape, sems


def _ride_shapes(rides):
    shapes, sems = [], []
    for kind, arrays in rides:
        ride_shapes, ride_sems = _exchange_shapes(kind, arrays)
        shapes += ride_shapes
        sems += ride_sems
    return shapes, sems


def _riding(body, n_in, n_out, rides, is_first, is_last):
    counts = [len(arrays) for _, arrays in rides]
    n_arr = sum(counts)

    def wrapped(*refs):
        ins, srcs = refs[:n_in], refs[n_in:n_in + n_arr]
        outs = refs[n_in + n_arr:n_in + n_arr + n_out]
        dsts = refs[n_in + n_arr + n_out:n_in + 2 * n_arr + n_out]
        first_sem = len(refs) - 2 * len(rides)
        scratch, sems = refs[n_in + 2 * n_arr + n_out:first_sem], refs[first_sem:]

        def copies():
            made, at = [], 0
            for r, ((kind, _), n) in enumerate(zip(rides, counts)):
                made += _EXCHANGES[kind][1](srcs[at:at + n], dsts[at:at + n], sems[2 * r], sems[2 * r + 1])
                at += n
            return made

        @pl.when(is_first())
        def _():
            for cp in copies():
                cp.start()

        body(*ins, *outs, *scratch)

        @pl.when(is_last())
        def _():
            for cp in copies():
                cp.wait()

    return wrapped


_HBM = pl.BlockSpec(memory_space=pltpu.HBM)
_SEM = pl.BlockSpec(memory_space=pltpu.SEMAPHORE)
_FLOWS = pltpu.SideEffectType.DATAFLOW_SIDE_EFFECTING


def _start_exchanges(rides, name):
    arrays = [a for _, group in rides for a in group]
    shapes, sems = _ride_shapes(rides)
    n_arr, n_sem = len(arrays), len(sems)

    def body(*refs):
        srcs, lands = refs[:n_arr], refs[n_arr:2 * n_arr]
        sem_refs, token = refs[2 * n_arr:2 * n_arr + n_sem], refs[-1]
        at = 0
        for r, (kind, group) in enumerate(rides):
            n = len(group)
            for cp in _EXCHANGES[kind][1](srcs[at:at + n], lands[at:at + n], sem_refs[2 * r], sem_refs[2 * r + 1]):
                cp.start()
            at += n
        token[...] = jnp.zeros(token.shape, token.dtype)

    in_hbm = [pltpu.HBM(a.shape, a.dtype) for a in arrays]
    land_hbm = [pltpu.HBM(sh.shape, sh.dtype) for sh in shapes]
    res = pl.pallas_call(
        body, name=name,
        out_shape=(*sems, *in_hbm, *land_hbm, jax.ShapeDtypeStruct((SUBLANES, LANES), F32)),
        in_specs=[_HBM] * (2 * n_arr), out_specs=(*[_SEM] * n_sem, *[_HBM] * (2 * n_arr), _VMEM),
        input_output_aliases={i: n_sem + i for i in range(2 * n_arr)},
        compiler_params=pltpu.CompilerParams(has_side_effects=_FLOWS),
    )(*[pltpu.with_memory_space_constraint(a, pltpu.HBM) for a in arrays],
      *[pltpu.with_memory_space_constraint(lax.empty(sh.shape, sh.dtype), pltpu.HBM) for sh in shapes])
    return res[:n_sem], res[n_sem:n_sem + n_arr], res[n_sem + n_arr:n_sem + 2 * n_arr], res[-1]


def _wait_exchanges(kinds, sems, arrays, lands, after, name, keep_sources=False):
    n_arr, n_sem = len(arrays), len(sems)

    def body(*refs):
        srcs, dsts = refs[:n_arr], refs[n_arr:2 * n_arr]
        sem_refs = refs[2 * n_arr:2 * n_arr + n_sem]
        at = 0
        for r, (kind, n) in enumerate(kinds):
            for cp in _EXCHANGES[kind][1](srcs[at:at + n], dsts[at:at + n], sem_refs[2 * r], sem_refs[2 * r + 1]):
                cp.wait()
            at += n

    hbm = [pltpu.HBM(a.shape, a.dtype) for a in (*arrays, *lands)]
    return pl.pallas_call(
        body, name=name, out_shape=tuple(hbm),
        in_specs=[_HBM] * (2 * n_arr) + [_SEM] * n_sem + [_ANY], out_specs=tuple([_HBM] * (2 * n_arr)),
        input_output_aliases={i: i for i in range(2 * n_arr)},
        compiler_params=pltpu.CompilerParams(has_side_effects=_FLOWS),
    )(*arrays, *lands, *sems, after)[0 if keep_sources else n_arr:]


def _chip_partial(pos, mine, theirs, relations, out_dtype, row_tile, name):
    n_slots, m, n = mine.shape
    q0 = relations[0]

    def chip_of(qi, pos_ref):
        q = qi + q0
        return pos_ref[0] ^ (q >> 1), pos_ref[1] ^ (q & 1)

    def mine_map(qi, t, pos_ref):
        px, py = chip_of(qi, pos_ref)
        return (4 * px + 2 * py + pos_ref[2] if n_slots == N_DEV else 2 * px + py), t, 0

    def theirs_map(qi, t, pos_ref):
        px, py = chip_of(qi, pos_ref)
        return 2 * px + py, t, 0

    def body(pos_ref, a_ref, b_ref, o_ref):
        o_ref[...] = (a_ref[...] + b_ref[...]).astype(out_dtype)

    return pl.pallas_call(
        body, name=name,
        out_shape=jax.ShapeDtypeStruct((len(relations), m, n), out_dtype),
        grid_spec=pltpu.PrefetchScalarGridSpec(
            num_scalar_prefetch=1, grid=(len(relations), m // row_tile),
            in_specs=[pl.BlockSpec((None, row_tile, n), mine_map), pl.BlockSpec((None, row_tile, n), theirs_map)],
            out_specs=pl.BlockSpec((None, row_tile, n), lambda qi, t, pos_ref: (qi, t, 0))),
        compiler_params=pltpu.CompilerParams(dimension_semantics=("arbitrary", "arbitrary")),
    )(pos, mine, theirs)


PARTS = ((0, 512), (512, 640))


def _gather_norm_proj(pos, x2d, small_shard, norm_g, w_in_shard, w_out_shards, n_chunk):
    seq = x2d.shape[0]
    n_tiles = seq // TILE + 1
    tp = n_tiles * TILE
    n_parts = len(PARTS)
    widest = max(width for _, width in PARTS)
    units = [(s, u) for s in range(2) for u in range(n_parts)]
    units += [(s, u) for u in range(n_parts) for s in (2, 3, 5, 6)] + [(s, u) for u in range(n_parts) for s in (4, 7)]
    n_units = len(units)
    over_ici = [m for m, (s, _) in enumerate(units) if s in (2, 3)]
    handled_at = {m: m - 1 for m in range(1, n_units)}
    handled_at.update({m: over_ici[0] - 1 + i for i, m in enumerate(over_ici)})
    assert all(step < m for m, step in handled_at.items())
    n_steps = n_tiles + n_units
    chunk = tp // n_chunk

    def body(pos_ref, x_ref, g_ref, small_ref, win_ref, wa_ref, wb_ref, wo_ref,
             ht_ref, proj_ref, meta_ref, small_all, win_all, wa_all, wb_all, wo_all,
             h_all, wbuf, rbuf, small_buf, send_sems, recv_sems, local_sems, small_send, small_recv):
        g = pl.program_id(0)
        x, y, c = lax.axis_index("x"), lax.axis_index("y"), lax.axis_index("c")
        me, sibling = (x, y, c), (x, y, 1 - c)
        chips = [(1 - x, y), (x, 1 - y), (1 - x, 1 - y)]
        shards = (win_ref, wa_ref, wb_ref, wo_ref)
        gathered = (win_all, wa_all, wb_all, wo_all)
        n_arrays = len(shards)
        blocks = [me, sibling] + [(*chip, c) for chip in chips] + [(*chip, 1 - c) for chip in chips]

        def index(block):
            px, py, pc = block
            return 4 * px + 2 * py + pc

        def part(ref, a, u):
            return ref.at[:, pl.ds(PARTS[u][0], PARTS[u][1])] if a == 0 else ref

        def slot(a, block, u):
            return part(gathered[a].at[index(block)], a, u)

        def sem(a, k, u):
            return n_parts * k + u if a == 0 else 7 * n_parts + 7 * (a - 1) + k

        def copy(a, k, block, to, u=0, from_shard=False):
            return pltpu.make_async_remote_copy(
                src_ref=part(shards[a], a, u) if from_shard else slot(a, block, u), dst_ref=slot(a, block, u),
                send_sem=send_sems.at[sem(a, k, u)], recv_sem=recv_sems.at[sem(a, k, u)],
                device_id=to, device_id_type=MESH)

        def keep(a):
            return pltpu.make_async_copy(shards[a], gathered[a].at[index(me)], local_sems.at[a])

        def load(m):
            s, u = units[m]
            src = part(win_ref, 0, u) if s == 0 else slot(0, blocks[s], u)
            return pltpu.make_async_copy(src, wbuf.at[m % 2, :, 0:PARTS[u][1]], local_sems.at[n_arrays + m % 2])

        def store(m):
            s, u = units[m]
            col0 = pl.multiple_of(index(blocks[s]) * COLS + PARTS[u][0], LANES)
            return pltpu.make_async_copy(rbuf.at[m % 2, :, 0:PARTS[u][1]],
                                         proj_ref.at[:, pl.ds(col0, PARTS[u][1])], local_sems.at[n_arrays + 2 + m % 2])

        def by_x(a, u):
            return u == 0 if a == 0 else a < 3

        def relay(a, u=0):
            src, to = (blocks[3], blocks[2]) if by_x(a, u) else (blocks[2], blocks[3])
            return copy(a, 3, src, to, u)

        def arrive(m):
            s, u = units[m]
            if s == 1:
                copy(0, 0, sibling, me, u).wait_recv()
            elif 2 <= s <= 4:
                copy(0, s - 1, blocks[s], me, u).wait_recv()
                copy(0, s + 2, blocks[s], sibling, u).start()
                if s < 4 and by_x(0, u) == (s == 3):
                    relay(0, u).start()
            elif s >= 5:
                copy(0, s - 1, blocks[s], me, u).wait_recv()

        def pass_on_out(j):
            for a in range(1, 4):
                copy(a, j + 1, blocks[2 + j], me).wait_recv()
                copy(a, j + 4, blocks[2 + j], sibling).start()
                if j < 2 and by_x(a, 0) == (j == 1):
                    relay(a).start()

        targets = [sibling, blocks[2], blocks[3]]
        relays_started = max(handled_at[m] for m, (s, u) in enumerate(units) if s in (2, 3) and by_x(0, u) == (s == 3))

        def small_copies():
            return _all_copies([small_ref], [small_all], small_send, small_recv)

        @pl.when(g == 0)
        def _():
            for cp in small_copies():
                cp.start()
            for a in range(n_arrays):
                keep(a).start()
            for u in range(n_parts):
                for k, to in enumerate(targets):
                    copy(0, k, me, to, u, from_shard=True).start()
            for a in range(1, 4):
                copy(a, 0, me, sibling, from_shard=True).start()
            load(0).start()

        @pl.when(g == n_tiles - 2)
        def _():
            for cp in small_copies():
                cp.wait()
            fetch = pltpu.make_async_copy(small_all, small_buf, local_sems.at[n_arrays + 4])
            fetch.start()
            fetch.wait()
            meta_ref[0:TILE - N_META, :] = jnp.zeros((TILE - N_META, D_MODEL), F32)
            meta_ref[TILE - N_META:TILE, :] = jnp.concatenate([small_buf[d, 0:N_META, :] for d in range(N_DEV)], axis=1)

        @pl.when(g < n_tiles)
        def _():
            s0 = jnp.where(g == n_tiles - 1, meta_ref[...], x_ref[...])
            r = lax.rsqrt(_rowmean(s0 * s0) + EPS)
            h32 = (s0 * r) * g_ref[...]
            ht_ref[...] = h32.T.astype(BF16)
            h_all[pl.ds(pl.multiple_of(g * TILE, TILE), TILE), :] = h32.astype(BF16)

        for m in range(n_units):
            @pl.when(g == n_tiles + m)
            def _(m=m):
                load(m).wait()
                for later in range(m + 1, n_units):
                    if handled_at[later] == m:
                        arrive(later)
                if m + 1 < n_units:
                    load(m + 1).start()
                if m == relays_started:
                    for a in range(1, 4):
                        for k in (1, 2):
                            copy(a, k, me, targets[k], from_shard=True).start()
                if m == n_units - 2:
                    pass_on_out(0)
                    pass_on_out(1)
                if m >= 2:
                    store(m - 2).wait()

        m_now = jnp.maximum(g - n_tiles, 0)
        u_now = functools.reduce(jnp.add, [jnp.where(m_now == m, u, 0) for m, (_, u) in enumerate(units)])
        for u, (_, width) in enumerate(PARTS):
            @pl.when((g >= n_tiles) & (u_now == u))
            def _(width=width):
                w = wbuf[m_now % 2, :, 0:width]
                for r in range(n_chunk):
                    rbuf[m_now % 2, r * chunk:(r + 1) * chunk, 0:width] = _dot(h_all[r * chunk:(r + 1) * chunk, :], w)

        for m in range(n_units):
            @pl.when(g == n_tiles + m)
            def _(m=m):
                store(m).start()

        @pl.when(g == n_steps - 1)
        def _():
            pass_on_out(2)
            store(n_units - 2).wait()
            store(n_units - 1).wait()
            for a in range(1, 4):
                copy(a, 0, sibling, me).wait_recv()
                for j in range(3):
                    copy(a, 4 + j, blocks[5 + j], me).wait_recv()
            for a in range(n_arrays):
                for u in range(n_parts if a == 0 else 1):
                    for k, to in enumerate(targets):
                        copy(a, k, me, to, u, from_shard=True).wait_send()
                    relay(a, u).wait_send()
                    for j in range(3):
                        copy(a, 4 + j, blocks[2 + j], sibling, u).wait_send()
                keep(a).wait()

    n_x = n_tiles - 1
    return pl.pallas_call(
        body, name="gather_norm_proj",
        out_shape=[jax.ShapeDtypeStruct((D_MODEL, tp), BF16), jax.ShapeDtypeStruct((tp, D_IN), F32),
                   jax.ShapeDtypeStruct((TILE, D_MODEL), F32), jax.ShapeDtypeStruct((N_DEV,) + small_shard.shape[1:], F32),
                   jax.ShapeDtypeStruct((N_DEV,) + w_in_shard.shape, BF16)]
                  + [jax.ShapeDtypeStruct((N_DEV,) + w.shape, BF16) for w in w_out_shards],
        grid_spec=pltpu.PrefetchScalarGridSpec(
            num_scalar_prefetch=1, grid=(n_steps,),
            in_specs=[pl.BlockSpec((TILE, D_MODEL), lambda g, pos_ref: (jnp.minimum(g, n_x - 1), 0)),
                      _VMEM, _ANY, _ANY, _ANY, _ANY, _ANY],
            out_specs=[pl.BlockSpec((D_MODEL, TILE), lambda g, pos_ref: (0, jnp.minimum(g, n_tiles - 1))),
                       _ANY, _VMEM, _ANY, _ANY, _ANY, _ANY, _ANY],
            scratch_shapes=[pltpu.VMEM((tp, D_MODEL), BF16), pltpu.VMEM((2, D_MODEL, widest), BF16),
                            pltpu.VMEM((2, tp, widest), F32), pltpu.VMEM((N_DEV,) + small_shard.shape[1:], F32),
                            pltpu.SemaphoreType.DMA((7 * n_parts + 21,)), pltpu.SemaphoreType.DMA((7 * n_parts + 21,)),
                            pltpu.SemaphoreType.DMA((9,)),
                            pltpu.SemaphoreType.DMA((N_DEV,)), pltpu.SemaphoreType.DMA((N_DEV,))]),
        compiler_params=pltpu.CompilerParams(dimension_semantics=("arbitrary",), vmem_limit_bytes=VMEM_LIMIT),
    )(pos, x2d, norm_g, small_shard, w_in_shard, *w_out_shards)


C_AVAL, C_AGLU, C_AZ, C_BB, C_BC, C_BX, C_BZ, C_GA, C_GB = (k * D_MODEL for k in range(9))
S_AZ, S_BB, S_BZ, S_GA, S_GB = (k * D_MODEL for k in range(5))


def _fused_pass(proj, x2d, tgt2d, meta_tile, conv_a_w, conv_a_b, ln_a_g, ln_a_b, b_a_out, conv_b_w, final_g,
                w_a, w_b, w_o, w_a_t, w_b_t, w_o_t, n_tiles):
    T = TILE
    tp = n_tiles * T
    inv_d = 1.0 / D_MODEL

    def block_of(tile):
        return jnp.where(tile == 0, n_tiles - 1, tile - 1)

    def cur(i):
        return block_of(jnp.minimum(i, n_tiles - 1))

    def prev(i):
        return block_of(jnp.clip(i - 1, 0, n_tiles - 1))

    def xblk(i):
        return jnp.maximum(jnp.minimum(i, n_tiles - 1) - 1, 0)

    def body(proj_ref, aprev, cprev, x_ref, tgt_ref, meta_ref, caw_ref, cab_ref, lng_ref, lnb_ref, bao_ref, cbw_ref,
             fg_ref, wa_ref, wb_ref, wo_ref, wat_ref, wbt_ref, wot_ref,
             dproj_ref, ds1_ref, lhs_ref, rhs_ref, small_ref,
             ua0_buf, cb_buf, dua1_buf, dc3_buf, stage, ua1_buf, c3_buf,
             dpa_buf, dpb_buf, dcaw8, dcbw8, shift_buf):
        i = pl.program_id(0)
        this, before = i % 2, 1 - i % 2

        @pl.when(i == 0)
        def _init():
            for buf in (ua0_buf, cb_buf, dua1_buf, dc3_buf, dcaw8, dcbw8):
                buf[...] = jnp.zeros(buf.shape, buf.dtype)
            small_ref[...] = jnp.zeros(small_ref.shape, F32)

        @pl.when(i >= 1)
        def _emit_stage():
            dproj_ref[:, C_AZ:C_BC] = stage[:, S_AZ:S_BZ]
            dproj_ref[:, C_BZ:D_IN] = stage[:, S_BZ:S_GB + D_MODEL]

        @pl.when(i < n_tiles)
        def _front():
            def conv_chunk(cc, carry):
                c0 = pl.multiple_of(cc * LANES, LANES)
                lanes = pl.ds(c0, LANES)

                def col(base):
                    return pl.ds(pl.multiple_of(base + cc * LANES, LANES), LANES)

                ua0 = proj_ref[:, col(C_AVAL)] * _sigmoid(proj_ref[:, col(C_AGLU)])
                ua0_buf[this, 0:HALO, lanes] = ua0_buf[before, T:T + HALO, lanes]
                ua0_buf[this, HALO:HALO + T, lanes] = ua0
                acc = jnp.broadcast_to(cab_ref[:, lanes], (T, LANES))
                lead = HALO - (CONV_A - 1)
                for r in range(SUBLANES):
                    taps = [k for k in range(CONV_A) if (k + lead) % SUBLANES == r]
                    rows = T + SUBLANES * max((k + lead) // SUBLANES for k in taps)
                    if r:
                        shift_buf[r, 0:rows, :] = ua0_buf[this, pl.ds(r, rows), lanes]
                    for k in taps:
                        q = (k + lead) // SUBLANES
                        if r:
                            win = shift_buf[r, SUBLANES * q:SUBLANES * q + T, :]
                        else:
                            win = ua0_buf[this, pl.ds(SUBLANES * q, T), lanes]
                        acc = acc + caw_ref[k:k + 1, lanes] * win
                ua1_buf[:, lanes] = acc
                cb = proj_ref[:, col(C_BC)] * proj_ref[:, col(C_BX)]
                cb_buf[this, 0:SUBLANES, lanes] = cb_buf[before, T:T + SUBLANES, lanes]
                cb_buf[this, SUBLANES:SUBLANES + T, lanes] = cb
                lead_b = SUBLANES - (CONV_B - 1)
                acc3 = cbw_ref[0:1, lanes] * cb_buf[this, pl.ds(lead_b, T), lanes]
                for k in range(1, CONV_B):
                    acc3 = acc3 + cbw_ref[k:k + 1, lanes] * cb_buf[this, pl.ds(lead_b + k, T), lanes]
                c3_buf[:, lanes] = acc3
                return carry

            lax.fori_loop(0, N_CHUNK, conv_chunk, 0)

            ua1 = ua1_buf[...]
            xc = ua1 - _rowmean(ua1)
            rstd = lax.rsqrt(_rowmean(xc * xc) + EPS)
            xhat = xc * rstd
            ua2 = xhat * lng_ref[...] + lnb_ref[...]
            sg2 = _sigmoid(ua2)
            ua3 = ua2 * sg2
            a_z = proj_ref[:, C_AZ:C_AZ + D_MODEL]
            sz = _sigmoid(a_z)
            silu_az = a_z * sz
            lhs_ref[0] = (ua3 * silu_az).astype(BF16)
            b_z = proj_ref[:, C_BZ:C_BZ + D_MODEL]
            sbz = _sigmoid(b_z)
            silu_bz = b_z * sbz
            b_b = proj_ref[:, C_BB:C_BB + D_MODEL]
            c3 = c3_buf[...]
            ub = b_b * c3
            lhs_ref[1] = (ub * silu_bz).astype(BF16)

            ya = _dot(lhs_ref[0], wa_ref[...]) + bao_ref[...]
            yb = _dot(lhs_ref[1], wb_ref[...])
            sga = _sigmoid(proj_ref[:, C_GA:C_GA + D_MODEL])
            sgb = _sigmoid(proj_ref[:, C_GB:C_GB + D_MODEL])
            m_b = (sga * ya + sgb * yb).astype(BF16)
            lhs_ref[2] = m_b
            s0 = jnp.where(i == 0, meta_ref[...], x_ref[...])
            s1 = s0 + _dot(m_b, wo_ref[...])
            r1 = lax.rsqrt(_rowmean(s1 * s1) + EPS)
            y = (s1 * r1) * fg_ref[...]
            is_token = (i >= 1).astype(F32)
            err = (y - tgt_ref[...]) * is_token
            small_ref[ROW_LOSS:ROW_LOSS + 1, :] += (0.5 * inv_d) * _colsum(err * err)
            dy = err * inv_d
            small_ref[ROW_FINAL_G:ROW_FINAL_G + 1, :] += _colsum(dy * (s1 * r1))
            gy = dy * fg_ref[...]
            ds1 = r1 * gy - s1 * ((r1 * r1 * r1) * _rowmean(gy * s1))
            ds1_ref[...] = ds1
            ds1_b = ds1.astype(BF16)
            rhs_ref[2] = ds1_b
            dm = _dot(ds1_b, wot_ref[...])
            dya = dm * sga
            dyb = dm * sgb
            stage[:, S_GA:S_GA + D_MODEL] = (dya * ya * (1.0 - sga)).astype(BF16)
            stage[:, S_GB:S_GB + D_MODEL] = (dyb * yb * (1.0 - sgb)).astype(BF16)
            small_ref[ROW_B_A_OUT:ROW_B_A_OUT + 1, :] += _colsum(dya)
            dya_b = dya.astype(BF16)
            dyb_b = dyb.astype(BF16)
            rhs_ref[0] = dya_b
            rhs_ref[1] = dyb_b
            dpa_buf[...] = _dot(dya_b, wat_ref[...])
            dpb_buf[...] = _dot(dyb_b, wbt_ref[...])

            dpa = dpa_buf[...]
            stage[:, S_AZ:S_AZ + D_MODEL] = (dpa * ua3 * (sz + silu_az * (1.0 - sz))).astype(BF16)
            dua2 = dpa * silu_az * (sg2 + ua3 * (1.0 - sg2))
            small_ref[ROW_LN_G:ROW_LN_G + 1, :] += _colsum(dua2 * xhat)
            small_ref[ROW_LN_B:ROW_LN_B + 1, :] += _colsum(dua2)
            dxh = dua2 * lng_ref[...]
            dua1 = rstd * (dxh - _rowmean(dxh) - xhat * _rowmean(dxh * xhat))
            small_ref[ROW_CONV_A_B:ROW_CONV_A_B + 1, :] += _colsum(dua1)
            dua1_buf[this, 0:T, :] = dua1
            dua1_buf[before, T:T + HALO, :] = dua1[0:HALO]
            dpb = dpb_buf[...]
            stage[:, S_BZ:S_BZ + D_MODEL] = (dpb * ub * (sbz + silu_bz * (1.0 - sbz))).astype(BF16)
            dub = dpb * silu_bz
            stage[:, S_BB:S_BB + D_MODEL] = (dub * c3).astype(BF16)
            dc3 = dub * b_b
            dc3_buf[this, 0:T, :] = dc3
            dc3_buf[before, T:T + SUBLANES, :] = dc3[0:SUBLANES]

        @pl.when(i == n_tiles)
        def _no_later_tile():
            dua1_buf[before, T:T + HALO, :] = jnp.zeros((HALO, D_MODEL), F32)
            dc3_buf[before, T:T + SUBLANES, :] = jnp.zeros((SUBLANES, D_MODEL), F32)

        @pl.when(i >= 1)
        def _lagged():
            def convt_chunk(cc, carry):
                c0 = pl.multiple_of(cc * LANES, LANES)
                lanes = pl.ds(c0, LANES)

                def col(base):
                    return pl.ds(pl.multiple_of(base + cc * LANES, LANES), LANES)

                ua0 = ua0_buf[before, HALO:HALO + T, lanes]
                acc = jnp.zeros((T, LANES), F32)
                for r in range(SUBLANES):
                    shifts = [j for j in range(CONV_A) if j % SUBLANES == r]
                    rows = T + shifts[-1] - r
                    if r:
                        shift_buf[r, 0:rows, :] = dua1_buf[before, pl.ds(r, rows), lanes]
                    for j in shifts:
                        k = CONV_A - 1 - j
                        if r:
                            later = shift_buf[r, j - r:j - r + T, :]
                        else:
                            later = dua1_buf[before, pl.ds(j, T), lanes]
                        acc = acc + caw_ref[k:k + 1, lanes] * later
                        dcaw8[SUBLANES * k:SUBLANES * (k + 1), lanes] += _fold8(ua0 * later)
                a_val = aprev[:, col(0)]
                sg = _sigmoid(aprev[:, col(D_MODEL)])
                dproj_ref[:, col(C_AVAL)] = (acc * sg).astype(BF16)
                dproj_ref[:, col(C_AGLU)] = (acc * a_val * (sg * (1.0 - sg))).astype(BF16)

                cb = cb_buf[before, SUBLANES:SUBLANES + T, lanes]
                acc3 = jnp.zeros((T, LANES), F32)
                for j in range(CONV_B):
                    k = CONV_B - 1 - j
                    later = dc3_buf[before, pl.ds(j, T), lanes]
                    acc3 = acc3 + cbw_ref[k:k + 1, lanes] * later
                    dcbw8[SUBLANES * k:SUBLANES * (k + 1), lanes] += _fold8(cb * later)
                dproj_ref[:, col(C_BC)] = (acc3 * cprev[:, col(D_MODEL)]).astype(BF16)
                dproj_ref[:, col(C_BX)] = (acc3 * cprev[:, col(0)]).astype(BF16)
                return carry

            lax.fori_loop(0, N_CHUNK, convt_chunk, 0)

        @pl.when(i == n_tiles)
        def _finish():
            for k in range(CONV_A):
                small_ref[ROW_CONV_A_W + k:ROW_CONV_A_W + k + 1, :] = _colsum(dcaw8[SUBLANES * k:SUBLANES * (k + 1), :])
            for k in range(CONV_B):
                small_ref[ROW_CONV_B_W + k:ROW_CONV_B_W + k + 1, :] = _colsum(dcbw8[SUBLANES * k:SUBLANES * (k + 1), :])

    pair = 2 * D_MODEL
    return pl.pallas_call(
        body, name="fused_pass", grid=(n_tiles + 1,),
        out_shape=[
            jax.ShapeDtypeStruct((tp, D_IN), BF16),
            jax.ShapeDtypeStruct((tp, D_MODEL), F32),
            jax.ShapeDtypeStruct((3, tp, D_MODEL), BF16),
            jax.ShapeDtypeStruct((3, tp, D_MODEL), BF16),
            jax.ShapeDtypeStruct((SMALL_A_ROWS, D_MODEL), F32),
        ],
        in_specs=[
            pl.BlockSpec((T, D_IN), lambda i: (cur(i), 0)),
            pl.BlockSpec((T, pair), lambda i: (prev(i), C_AVAL // pair)),
            pl.BlockSpec((T, pair), lambda i: (prev(i), C_BC // pair)),
            pl.BlockSpec((T, D_MODEL), lambda i: (xblk(i), 0)),
            pl.BlockSpec((T, D_MODEL), lambda i: (xblk(i), 0)),
            _VMEM, _VMEM, _VMEM, _VMEM, _VMEM, _VMEM, _VMEM, _VMEM,
            *[_resident((D_MODEL, D_MODEL)) for _ in range(6)],
        ],
        out_specs=[
            pl.BlockSpec((T, D_IN), lambda i: (prev(i), 0)),
            pl.BlockSpec((T, D_MODEL), lambda i: (cur(i), 0)),
            pl.BlockSpec((3, T, D_MODEL), lambda i: (0, cur(i), 0)),
            pl.BlockSpec((3, T, D_MODEL), lambda i: (0, cur(i), 0)),
            _VMEM,
        ],
        scratch_shapes=[
            pltpu.VMEM((2, HALO + T, D_MODEL), F32),
            pltpu.VMEM((2, SUBLANES + T, D_MODEL), F32),
            pltpu.VMEM((2, T + HALO, D_MODEL), F32),
            pltpu.VMEM((2, T + SUBLANES, D_MODEL), F32),
            pltpu.VMEM((T, 5 * D_MODEL), BF16),
            pltpu.VMEM((T, D_MODEL), F32),
            pltpu.VMEM((T, D_MODEL), F32),
            pltpu.VMEM((T, D_MODEL), F32),
            pltpu.VMEM((T, D_MODEL), F32),
            pltpu.VMEM((32 * SUBLANES, D_MODEL), F32),
            pltpu.VMEM((SUBLANES * SUBLANES, D_MODEL), F32),
            pltpu.VMEM((SUBLANES, T + HALO, LANES), F32),
        ],
        compiler_params=pltpu.CompilerParams(dimension_semantics=("arbitrary",), vmem_limit_bytes=VMEM_LIMIT),
    )(proj, proj, proj, x2d, tgt2d, meta_tile, conv_a_w, conv_a_b, ln_a_g, ln_a_b, b_a_out, conv_b_w, final_g,
      w_a, w_b, w_o, w_a_t, w_b_t, w_o_t)


def _input_bwd(dproj, ds1, x2d, meta_tile, norm_g, w_in_all, row_tile, after):
    seq = x2d.shape[0]
    n_steps = seq // row_tile
    meta_block = seq // TILE

    def backward(dp_ref, ds1_ref, s0_ref, g_ref, w_ref, out_ref, vec_ref, ready=None):
        for j in range(N_DEV):
            if ready is not None:
                ready(j)
            part = _dot_nt(dp_ref[:, j * COLS:(j + 1) * COLS], w_ref[j])
            dh = part if j == 0 else dh + part
        s0v = s0_ref[...]
        r = lax.rsqrt(_rowmean(s0v * s0v) + EPS)
        gh = dh * g_ref[...]
        out_ref[...] = ds1_ref[...] + r * gh - s0v * ((r * r * r) * _rowmean(gh * s0v))
        vec_ref[ROW_NORM_G:ROW_NORM_G + 1, :] += _colsum(dh * (s0v * r))

    def body(dp_ref, ds1_ref, x_ref, dpm_ref, ds1m_ref, meta_ref, g_ref, w_hbm, after_ref, gx_ref, small_ref,
             gmeta_buf, w_ref, w_sems):
        t = pl.program_id(0)

        def fetch(j):
            return pltpu.make_async_copy(w_hbm.at[j], w_ref.at[j], w_sems.at[j])

        @pl.when(t == 0)
        def _():
            small_ref[...] = jnp.zeros(small_ref.shape, F32)
            for j in range(N_DEV):
                fetch(j).start()

        def ready(j):
            @pl.when(t == 0)
            def _():
                fetch(j).wait()

        backward(dp_ref, ds1_ref, x_ref, g_ref, w_ref, gx_ref, small_ref, ready)

        @pl.when(t == n_steps - 1)
        def _():
            backward(dpm_ref, ds1m_ref, meta_ref, g_ref, w_ref, gmeta_buf, small_ref)
            small_ref[ROW_META:ROW_META + N_META, :] = gmeta_buf[TILE - N_META:TILE, :]

    return pl.pallas_call(
        body, name="input_bwd", grid=(n_steps,),
        out_shape=[jax.ShapeDtypeStruct(x2d.shape, F32), jax.ShapeDtypeStruct((SMALL_B_ROWS, D_MODEL), F32)],
        in_specs=[pl.BlockSpec((row_tile, D_IN), lambda t: (t, 0)),
                  pl.BlockSpec((row_tile, D_MODEL), lambda t: (t, 0)),
                  pl.BlockSpec((row_tile, D_MODEL), lambda t: (t, 0)),
                  pl.BlockSpec((TILE, D_IN), lambda t: (meta_block, 0)),
                  pl.BlockSpec((TILE, D_MODEL), lambda t: (meta_block, 0)),
                  _VMEM, _VMEM, _ANY, _ANY],
        out_specs=[pl.BlockSpec((row_tile, D_MODEL), lambda t: (t, 0)), _VMEM],
        scratch_shapes=[pltpu.VMEM((TILE, D_MODEL), F32), pltpu.VMEM((N_DEV, D_MODEL, COLS), BF16),
                        pltpu.SemaphoreType.DMA((N_DEV,))],
        compiler_params=pltpu.CompilerParams(dimension_semantics=("arbitrary",), vmem_limit_bytes=VMEM_LIMIT),
    )(dproj, ds1, x2d, dproj, ds1, meta_tile, norm_g, w_in_all, after)


def _grad_w_in_half(pos, h_t, dproj, k_tile, other_side, rides, name, after=None, add_to=None, narrow=False):
    tp = h_t.shape[1]
    n_k = tp // k_tile
    order = [] if after is None else [after]
    summing = add_to is not None
    assert not (summing and narrow)

    def column_block(q, k, pos_ref):
        return k, 2 * q + (1 - pos_ref[2] if other_side else pos_ref[2])

    def body(pos_ref, h_ref, dp_ref, *refs):
        acc = refs[-1]

        @pl.when(pl.program_id(1) == 0)
        def _():
            acc[...] = refs[0][...].astype(F32) if summing else jnp.zeros(acc.shape, F32)

        acc[...] += _dot(h_ref[...], dp_ref[...])

        if summing or narrow:
            @pl.when(pl.program_id(1) == n_k - 1)
            def _():
                refs[-2][...] = acc[...].astype(BF16)

        if summing:
            @pl.when((pl.program_id(1) == n_k - 1) & (pl.program_id(0) == 2 * pos_ref[0] + pos_ref[1]))
            def _():
                refs[-3][...] = acc[...]

    ride = [a for _, arrays in rides for a in arrays]
    n_arr = len(ride)
    ride_shapes, ride_sems = _ride_shapes(rides)
    block = (None, D_MODEL, COLS)
    extra_in = [add_to] if summing else []
    extra_in_specs = [pl.BlockSpec(block, lambda q, k, pos_ref: (q, 0, 0))] if summing else []
    extra_out = [jax.ShapeDtypeStruct((4, D_MODEL, COLS), BF16)] if summing else []
    extra_out_specs = [pl.BlockSpec(block, lambda q, k, pos_ref: (q ^ (2 * pos_ref[0] + pos_ref[1]), 0, 0))] \
        if summing else []
    body = _riding(body, 3 + len(extra_in) + len(order), 1 + len(extra_out), rides,
                   lambda: (pl.program_id(0) == 0) & (pl.program_id(1) == 0),
                   lambda: (pl.program_id(0) == 3) & (pl.program_id(1) == n_k - 1))
    return pl.pallas_call(
        body, name=name,
        out_shape=[jax.ShapeDtypeStruct((1 if summing else 4, D_MODEL, COLS), BF16 if narrow else F32)]
        + extra_out + ride_shapes,
        grid_spec=pltpu.PrefetchScalarGridSpec(
            num_scalar_prefetch=1, grid=(4, n_k),
            in_specs=[pl.BlockSpec((D_MODEL, k_tile), lambda q, k, pos_ref: (0, k)),
                      pl.BlockSpec((k_tile, COLS), column_block)] + extra_in_specs + [_ANY] * (len(order) + n_arr),
            out_specs=[pl.BlockSpec(block, lambda q, k, pos_ref: (0 if summing else q, 0, 0))]
            + extra_out_specs + [_ANY] * n_arr,
            scratch_shapes=([pltpu.VMEM((D_MODEL, COLS), F32)] if summing or narrow else []) + ride_sems),
        compiler_params=pltpu.CompilerParams(dimension_semantics=("arbitrary", "arbitrary"),
                                             vmem_limit_bytes=VMEM_LIMIT),
    )(pos, h_t, dproj, *extra_in, *order, *ride)


def _grad_w_out(lhs, rhs, k_tile, after):
    tp = lhs.shape[1]

    def body(a_ref, b_ref, after_ref, o_ref):
        @pl.when(pl.program_id(1) == 0)
        def _():
            o_ref[...] = jnp.zeros(o_ref.shape, F32)

        o_ref[...] += _dot_tn(a_ref[...], b_ref[...]).reshape(N_DEV, ROWS_OUT, D_MODEL)

    return pl.pallas_call(
        body, name="grad_w_out", grid=(3, tp // k_tile),
        out_shape=jax.ShapeDtypeStruct((N_DEV, 3, ROWS_OUT, D_MODEL), F32),
        in_specs=[pl.BlockSpec((None, k_tile, D_MODEL), lambda w, k: (w, k, 0)),
                  pl.BlockSpec((None, k_tile, D_MODEL), lambda w, k: (w, k, 0)), _ANY],
        out_specs=pl.BlockSpec((N_DEV, None, ROWS_OUT, D_MODEL), lambda w, k: (0, w, 0, 0)),
        compiler_params=pltpu.CompilerParams(dimension_semantics=("arbitrary", "arbitrary"),
                                             vmem_limit_bytes=VMEM_LIMIT),
    )(lhs, rhs, after)


def _adamw_math(w, g, m, v):
    m = ADAM_B1 * m + (1.0 - ADAM_B1) * g
    v = ADAM_B2 * v + (1.0 - ADAM_B2) * (g * g)
    m_hat = m / (1.0 - ADAM_B1 ** ADAM_STEP)
    v_hat = v / (1.0 - ADAM_B2 ** ADAM_STEP)
    delta = -ADAM_LR * (m_hat / (jnp.sqrt(v_hat) + ADAM_EPS) + ADAM_WD * w)
    return delta, m, v


def _adamw_sharded(pos, mine, theirs, landed, weights, row_tile, name, after=None):
    order = [] if after is None else [after]
    rows, n = weights[0][0].shape
    n_slots = mine.shape[0]
    per_shard = rows // row_tile
    assert per_shard == 1 or len(weights) == 1

    def mine_map(j, t, pos_ref):
        chip = 2 * pos_ref[0] + pos_ref[1]
        return {N_DEV: 2 * chip + pos_ref[2], 4: chip, 1: 0}[n_slots], j * per_shard + t, 0

    def theirs_map(j, t, pos_ref):
        return 2 * pos_ref[0] + pos_ref[1], j * per_shard + t, 0

    def body(pos_ref, mine_ref, *refs):
        if theirs is not None:
            g = mine_ref[...] + refs[0][...]
            refs = refs[1:]
        else:
            g = mine_ref[...]
        land_ref, refs = refs[0], refs[1:]
        ins, outs = refs[:3 * len(weights)], refs[3 * len(weights) + len(order):]
        for k in range(3):
            g = g + land_ref[k].astype(F32)
        for j in range(len(weights)):
            @pl.when(pl.program_id(0) == j)
            def _(j=j):
                w_ref, m_ref, v_ref = ins[3 * j:3 * j + 3]
                delta, m_new, v_new = _adamw_math(w_ref[...], g, m_ref[...], v_ref[...])
                for ref, val in zip(outs[4 * j:4 * j + 4], (g, delta, m_new, v_new)):
                    ref[...] = val

    tile = pl.BlockSpec((row_tile, n), lambda j, t, pos_ref: (t, 0))
    res = pl.pallas_call(
        body, name=name,
        out_shape=[jax.ShapeDtypeStruct((rows, n), F32)] * (4 * len(weights)),
        grid_spec=pltpu.PrefetchScalarGridSpec(
            num_scalar_prefetch=1, grid=(len(weights), per_shard),
            in_specs=[pl.BlockSpec((None, row_tile, n), mine_map)]
            + ([pl.BlockSpec((None, row_tile, n), theirs_map)] if theirs is not None else [])
            + [pl.BlockSpec((3, row_tile, n), lambda j, t, pos_ref: (0, j * per_shard + t, 0))]
            + [tile] * (3 * len(weights)) + [_ANY] * len(order),
            out_specs=[tile] * (4 * len(weights))),
        compiler_params=pltpu.CompilerParams(dimension_semantics=("arbitrary", "arbitrary")),
    )(pos, mine, *([theirs] if theirs is not None else []), landed, *[a for wmv in weights for a in wmv], *order)
    return [res[4 * j:4 * j + 4] for j in range(len(weights))]


def _adamw_small(pos, gathered, params, row_by_row=()):
    n_par, n_src = len(params), len(gathered)

    def body(pos_ref, *refs):
        g_refs, gc_refs = refs[:n_src], refs[n_src:2 * n_src]
        ins = refs[2 * n_src:2 * n_src + 3 * n_par]
        outs = refs[2 * n_src + 3 * n_par:]
        loss_ref = outs[4 * n_par]

        def reduced(ref, row, n_rows):
            g = ref[0, row:row + n_rows, :]
            for d in range(1, N_DEV):
                g = g + ref[d, row:row + n_rows, :]
            return g

        for p, (src, row, n_rows, sharded, _, _, _) in enumerate(params):
            g = reduced((gc_refs if sharded else g_refs)[src], row, n_rows)
            w_ref, m_ref, v_ref = ins[3 * p:3 * p + 3]
            delta, m_new, v_new = _adamw_math(w_ref[...], g, m_ref[...], v_ref[...])
            for kind, res in enumerate((g, delta, m_new, v_new)):
                if p in row_by_row:
                    for k in range(n_rows):
                        outs[4 * p + kind][k] = res[k:k + 1, :]
                else:
                    outs[4 * p + kind][...] = res
        loss = jnp.sum(reduced(g_refs[0], ROW_LOSS, 1), axis=1, keepdims=True)
        loss_ref[...] = jnp.broadcast_to(loss, loss_ref.shape)

    out_shape = []
    for p, (_, _, n_rows, _, w, _, _) in enumerate(params):
        out_shape += [jax.ShapeDtypeStruct((n_rows, 1, LANES) if p in row_by_row else w.shape, F32)] * 4
    out_shape.append(jax.ShapeDtypeStruct((1, LANES), F32))
    flat = [a for (_, _, _, _, w, m, v) in params for a in (w, m, v)]
    my_lanes = [pl.BlockSpec((N_DEV, g.shape[1], LANES), lambda i, pos_ref: (0, 0, 4 * pos_ref[0] + 2 * pos_ref[1] + pos_ref[2]))
                for g in gathered]
    return pl.pallas_call(
        body, name="adamw_small", out_shape=out_shape,
        grid_spec=pltpu.PrefetchScalarGridSpec(
            num_scalar_prefetch=1, grid=(1,),
            in_specs=[_VMEM] * n_src + my_lanes + [_VMEM] * len(flat), out_specs=[_VMEM] * len(out_shape)),
    )(pos, *gathered, *gathered, *flat)


def _pad_rows(a, rows):
    return jnp.concatenate([a, jnp.zeros((rows - a.shape[0], a.shape[1]), a.dtype)], axis=0)


def kernel(x, meta_tokens, norm_g, w_in, conv_a_w, conv_a_b, ln_a_g, ln_a_b, w_a_out, b_a_out, conv_b_w, w_b_out, w_out, final_g, loss_target, m_meta_tokens, m_norm_g, m_w_in, m_conv_a_w, m_conv_a_b, m_ln_a_g, m_ln_a_b, m_w_a_out, m_b_a_out, m_conv_b_w, m_w_b_out, m_w_out, m_final_g, v_meta_tokens, v_norm_g, v_w_in, v_conv_a_w, v_conv_a_b, v_ln_a_g, v_ln_a_b, v_w_a_out, v_b_a_out, v_conv_b_w, v_w_b_out, v_w_out, v_final_g):
    seq = x.shape[1]
    assert x.shape == (1, seq, D_MODEL) and seq % TILE == 0 and w_in.shape == (1, D_MODEL, COLS)
    n_tiles = seq // TILE + 1
    tp = n_tiles * TILE
    pos = jnp.stack([lax.axis_index("x"), lax.axis_index("y"), lax.axis_index("c")]).astype(jnp.int32)
    x2d = x[0]
    tgt2d = loss_target[0]

    small = jnp.concatenate([meta_tokens, _pad_rows(conv_a_w[0], 32), _pad_rows(conv_b_w[0], SUBLANES)], axis=0)
    final_g2 = final_g.reshape(1, D_MODEL)

    w_out_shards = [w[0].astype(BF16) for w in (w_a_out, w_b_out, w_out)]
    h_t, proj, meta_tile, small_params, w_in_all, *w_out_all = _gather_norm_proj(
        pos, x2d, small[None], norm_g, w_in[0].astype(BF16), w_out_shards, 3)
    small_params = small_params.transpose(1, 0, 2).reshape(small.shape[0], D_MODEL)
    conv_a_full, conv_b_full = small_params[N_META:N_META + 32], small_params[N_META + 32:]
    w_out_all = [w.reshape(D_MODEL, D_MODEL) for w in w_out_all]
    w_out_all_t = [w.T for w in w_out_all]
    dproj, ds1, lhs, rhs, small_a = _fused_pass(
        proj, x2d, tgt2d, meta_tile, conv_a_full, conv_a_b, ln_a_g, ln_a_b, b_a_out, conv_b_full, final_g2,
        w_out_all[0], w_out_all[1], w_out_all[2], w_out_all_t[0], w_out_all_t[1], w_out_all_t[2], n_tiles)
    k_tile = tp // 3
    gw_far, small_a_all = _grad_w_in_half(pos, h_t, dproj, k_tile, True, [("all", (small_a[None],))], "grad_w_in_far",
                                          narrow=True)
    sems, sent, landing, token = _start_exchanges([("sibling_half", (gw_far,))], "rs_far_start")
    gw_out = _grad_w_out(lhs, rhs, k_tile, token).reshape(N_DEV, 3 * ROWS_OUT, D_MODEL)
    (their_in,) = _wait_exchanges([("sibling_half", 1)], sems, sent, landing, gw_out, "rs_far_wait")
    sems_o, sent_o, landing_o, token = _start_exchanges([("sibling", (gw_out,))], "rs_out_start")
    gw_near, parts_in = _grad_w_in_half(pos, h_t, dproj, k_tile, False, [], "grad_w_in_near", after=token,
                                        add_to=their_in)
    sems_i, sent_i, landing_i, token = _start_exchanges([("chips_by_relation", (parts_in,))], "rs_chips_in_start")
    gw_out, their_out = _wait_exchanges([("sibling", 1)], sems_o, sent_o, landing_o, token, "rs_out_wait",
                                        keep_sources=True)
    parts_out = _chip_partial(pos, gw_out, their_out, (1, 2, 3), BF16, 3 * ROWS_OUT, "rs_parts_w_out")
    sems_o, sent_o, landing_o, token = _start_exchanges([("chips", (parts_out,))], "rs_chips_out_start")
    grad_x, small_b = _input_bwd(dproj, ds1, x2d, meta_tile, norm_g, w_in_all, min(256, seq), token)

    sems_s, sent_s, landing_s, token = _start_exchanges([("all", (small_b[None],))], "gather_small_grads_start")
    (land_in,) = _wait_exchanges([("chips_by_relation", 1)], sems_i, sent_i, landing_i, token, "rs_chips_in_wait")
    (res_in,) = _adamw_sharded(pos, gw_near, None, land_in, [(w_in[0], m_w_in[0], v_w_in[0])], 128, "adamw_w_in")
    (land_out,) = _wait_exchanges([("chips", 1)], sems_o, sent_o, landing_o, res_in[0], "rs_chips_out_wait")
    res_out = _adamw_sharded(
        pos, gw_out, their_out, land_out,
        [(w_a_out[0], m_w_a_out[0], v_w_a_out[0]), (w_b_out[0], m_w_b_out[0], v_w_b_out[0]),
         (w_out[0], m_w_out[0], v_w_out[0])], ROWS_OUT, "adamw_w_out")
    (small_b_all,) = _wait_exchanges([("all", 1)], sems_s, sent_s, landing_s, res_out[2][0], "gather_small_grads_wait")
    small_grads = [small_a_all, small_b_all]
    params = [
        (1, ROW_META, N_META, True, meta_tokens, m_meta_tokens, v_meta_tokens),
        (1, ROW_NORM_G, 1, False, norm_g, m_norm_g, v_norm_g),
        (0, ROW_CONV_A_W, CONV_A, True, conv_a_w[0], m_conv_a_w[0], v_conv_a_w[0]),
        (0, ROW_CONV_A_B, 1, False, conv_a_b, m_conv_a_b, v_conv_a_b),
        (0, ROW_LN_G, 1, False, ln_a_g, m_ln_a_g, v_ln_a_g),
        (0, ROW_LN_B, 1, False, ln_a_b, m_ln_a_b, v_ln_a_b),
        (0, ROW_B_A_OUT, 1, False, b_a_out, m_b_a_out, v_b_a_out),
        (0, ROW_CONV_B_W, CONV_B, True, conv_b_w[0], m_conv_b_w[0], v_conv_b_w[0]),
        (0, ROW_FINAL_G, 1, False, final_g2, m_final_g.reshape(1, D_MODEL), v_final_g.reshape(1, D_MODEL)),
    ]
    conv_weights = (2, 7)
    res_small = _adamw_small(pos, small_grads, params, conv_weights)
    loss = res_small[-1][0, 0]

    def small_res(p, kind, shape):
        res = res_small[4 * p + kind]
        return res.transpose(1, 0, 2) if p in conv_weights else res.reshape(shape)

    per_weight = []
    for kind in range(4):
        per_weight.append([
            small_res(0, kind, meta_tokens.shape),
            small_res(1, kind, norm_g.shape),
            res_in[kind].reshape(w_in.shape),
            small_res(2, kind, conv_a_w.shape),
            small_res(3, kind, conv_a_b.shape),
            small_res(4, kind, ln_a_g.shape),
            small_res(5, kind, ln_a_b.shape),
            res_out[0][kind].reshape(w_a_out.shape),
            small_res(6, kind, b_a_out.shape),
            small_res(7, kind, conv_b_w.shape),
            res_out[1][kind].reshape(w_b_out.shape),
            res_out[2][kind].reshape(w_out.shape),
            small_res(8, kind, final_g.shape),
        ])
    return (loss, grad_x.reshape(x.shape), *per_weight[0], *per_weight[1], *per_weight[2], *per_weight[3])
```

```python
import functools

import jax
import jax.numpy as jnp
from jax import lax
from jax.experimental import pallas as pl
from jax.experimental.pallas import tpu as pltpu

D_MODEL = 1024
N_META = 16
N_DEV = 8
D_IN = 9 * D_MODEL
COLS = D_IN // N_DEV
ROWS_OUT = D_MODEL // N_DEV
CONV_A = 31
CONV_B = 3
EPS = 1e-6

ADAM_LR = 0.001
ADAM_B1 = 0.9
ADAM_B2 = 0.999
ADAM_EPS = 1e-08
ADAM_WD = 0.01
ADAM_STEP = 10

TILE = 128
LANES = 128
N_CHUNK = D_MODEL // LANES
HALO = 32
SUBLANES = 8
VMEM_LIMIT = 56 * 1024 * 1024

ROW_FINAL_G, ROW_B_A_OUT, ROW_LN_G, ROW_LN_B, ROW_CONV_A_B, ROW_LOSS = 0, 1, 2, 3, 4, 5
ROW_CONV_A_W, ROW_CONV_B_W, SMALL_A_ROWS = 8, 40, 48
ROW_NORM_G, ROW_META, SMALL_B_ROWS = 0, 8, 24

MESH = pl.DeviceIdType.MESH
_ANY = pl.BlockSpec(memory_space=pl.ANY)
_VMEM = pl.BlockSpec(memory_space=pltpu.VMEM)
BF16 = jnp.bfloat16
F32 = jnp.float32


def _resident(shape):
    return pl.BlockSpec(shape, lambda *_: (0,) * len(shape), pipeline_mode=pl.Buffered(1))


def _sigmoid(v):
    return jax.nn.sigmoid(v)


def _dot(a, b):
    return jnp.dot(a, b, preferred_element_type=F32)


def _dot_nt(a, b):
    return lax.dot_general(a, b, (((1,), (1,)), ((), ())), preferred_element_type=F32)


def _dot_tn(a, b):
    return lax.dot_general(a, b, (((0,), (0,)), ((), ())), preferred_element_type=F32)


def _colsum(v):
    return jnp.sum(v, axis=0, keepdims=True)


def _rowmean(v):
    parts = [v[:, LANES * c:LANES * (c + 1)] for c in range(v.shape[1] // LANES)]
    return jnp.sum(functools.reduce(jnp.add, parts), axis=-1, keepdims=True) * (1.0 / v.shape[1])


def _fold8(v):
    parts = [v[SUBLANES * g:SUBLANES * (g + 1)] for g in range(v.shape[0] // SUBLANES)]
    return functools.reduce(jnp.add, parts)


def _sibling_copies(srcs, dsts, send_sems, recv_sems):
    x, y, c = lax.axis_index("x"), lax.axis_index("y"), lax.axis_index("c")
    return [pltpu.make_async_remote_copy(
        src_ref=src.at[2 * q + (1 - c)], dst_ref=dst.at[q],
        send_sem=send_sems.at[4 * a + q], recv_sem=recv_sems.at[4 * a + q],
        device_id=(x, y, 1 - c), device_id_type=MESH)
        for a, (src, dst) in enumerate(zip(srcs, dsts)) for q in range(4)]


def _chip_copies(srcs, dsts, send_sems, recv_sems):
    x, y, c = lax.axis_index("x"), lax.axis_index("y"), lax.axis_index("c")
    targets = [(x, 1 - y, c), (1 - x, y, c), (1 - x, 1 - y, c)]
    return [pltpu.make_async_remote_copy(
        src_ref=src.at[k], dst_ref=dst.at[k],
        send_sem=send_sems.at[3 * a + k], recv_sem=recv_sems.at[3 * a + k],
        device_id=targets[k], device_id_type=MESH)
        for a, (src, dst) in enumerate(zip(srcs, dsts)) for k in range(3)]


def _sibling_half_copies(srcs, dsts, send_sems, recv_sems):
    x, y, c = lax.axis_index("x"), lax.axis_index("y"), lax.axis_index("c")
    return [pltpu.make_async_remote_copy(
        src_ref=src.at[q], dst_ref=dst.at[q],
        send_sem=send_sems.at[4 * a + q], recv_sem=recv_sems.at[4 * a + q],
        device_id=(x, y, 1 - c), device_id_type=MESH)
        for a, (src, dst) in enumerate(zip(srcs, dsts)) for q in range(4)]


def _all_copies(srcs, dsts, send_sems, recv_sems):
    x, y, c = lax.axis_index("x"), lax.axis_index("y"), lax.axis_index("c")
    mine = 4 * x + 2 * y + c
    copies = []
    for a, (src, dst) in enumerate(zip(srcs, dsts)):
        copies.append(pltpu.make_async_copy(src.at[0], dst.at[mine], send_sems.at[N_DEV * a]))
        for k in range(1, N_DEV):
            copies.append(pltpu.make_async_remote_copy(
                src_ref=src.at[0], dst_ref=dst.at[mine],
                send_sem=send_sems.at[N_DEV * a + k], recv_sem=recv_sems.at[N_DEV * a + k],
                device_id=(x ^ (k >> 2), y ^ ((k >> 1) & 1), c ^ (k & 1)), device_id_type=MESH))
    return copies


def _chip_copies_by_relation(srcs, dsts, send_sems, recv_sems):
    return _chip_copies([src.at[pl.ds(1, 3)] for src in srcs], dsts, send_sems, recv_sems)


_EXCHANGES = {"sibling": (4, _sibling_copies, 4), "sibling_half": (4, _sibling_half_copies, 4),
              "chips": (3, _chip_copies, 3), "chips_by_relation": (3, _chip_copies_by_relation, 3),
              "all": (N_DEV, _all_copies, N_DEV)}


def _exchange_shapes(kind, arrays):
    per_array, _, slots = _EXCHANGES[kind]
    out_shape = [jax.ShapeDtypeStruct((slots,) + a.shape[1:], a.dtype) for a in arrays]
    sems = [pltpu.SemaphoreType.DMA((per_array * len(arrays),))] * 2
    return out_shape, sems


def _ride_shapes(rides):
    shapes, sems = [], []
    for kind, arrays in rides:
        ride_shapes, ride_sems = _exchange_shapes(kind, arrays)
        shapes += ride_shapes
        sems += ride_sems
    return shapes, sems


def _riding(body, n_in, n_out, rides, is_first, is_last):
    counts = [len(arrays) for _, arrays in rides]
    n_arr = sum(counts)

    def wrapped(*refs):
        ins, srcs = refs[:n_in], refs[n_in:n_in + n_arr]
        outs = refs[n_in + n_arr:n_in + n_arr + n_out]
        dsts = refs[n_in + n_arr + n_out:n_in + 2 * n_arr + n_out]
        first_sem = len(refs) - 2 * len(rides)
        scratch, sems = refs[n_in + 2 * n_arr + n_out:first_sem], refs[first_sem:]

        def copies():
            made, at = [], 0
            for r, ((kind, _), n) in enumerate(zip(rides, counts)):
                made += _EXCHANGES[kind][1](srcs[at:at + n], dsts[at:at + n], sems[2 * r], sems[2 * r + 1])
                at += n
            return made

        @pl.when(is_first())
        def _():
            for cp in copies():
                cp.start()

        body(*ins, *outs, *scratch)

        @pl.when(is_last())
        def _():
            for cp in copies():
                cp.wait()

    return wrapped


_HBM = pl.BlockSpec(memory_space=pltpu.HBM)
_SEM = pl.BlockSpec(memory_space=pltpu.SEMAPHORE)
_FLOWS = pltpu.SideEffectType.DATAFLOW_SIDE_EFFECTING


def _start_exchanges(rides, name):
    arrays = [a for _, group in rides for a in group]
    shapes, sems = _ride_shapes(rides)
    n_arr, n_sem = len(arrays), len(sems)

    def body(*refs):
        srcs, lands = refs[:n_arr], refs[n_arr:2 * n_arr]
        sem_refs, token = refs[2 * n_arr:2 * n_arr + n_sem], refs[-1]
        at = 0
        for r, (kind, group) in enumerate(rides):
            n = len(group)
            for cp in _EXCHANGES[kind][1](srcs[at:at + n], lands[at:at + n], sem_refs[2 * r], sem_refs[2 * r + 1]):
                cp.start()
            at += n
        token[...] = jnp.zeros(token.shape, token.dtype)

    in_hbm = [pltpu.HBM(a.shape, a.dtype) for a in arrays]
    land_hbm = [pltpu.HBM(sh.shape, sh.dtype) for sh in shapes]
    res = pl.pallas_call(
        body, name=name,
        out_shape=(*sems, *in_hbm, *land_hbm, jax.ShapeDtypeStruct((SUBLANES, LANES), F32)),
        in_specs=[_HBM] * (2 * n_arr), out_specs=(*[_SEM] * n_sem, *[_HBM] * (2 * n_arr), _VMEM),
        input_output_aliases={i: n_sem + i for i in range(2 * n_arr)},
        compiler_params=pltpu.CompilerParams(has_side_effects=_FLOWS),
    )(*[pltpu.with_memory_space_constraint(a, pltpu.HBM) for a in arrays],
      *[pltpu.with_memory_space_constraint(lax.empty(sh.shape, sh.dtype), pltpu.HBM) for sh in shapes])
    return res[:n_sem], res[n_sem:n_sem + n_arr], res[n_sem + n_arr:n_sem + 2 * n_arr], res[-1]


def _wait_exchanges(kinds, sems, arrays, lands, after, name, keep_sources=False):
    n_arr, n_sem = len(arrays), len(sems)

    def body(*refs):
        srcs, dsts = refs[:n_arr], refs[n_arr:2 * n_arr]
        sem_refs = refs[2 * n_arr:2 * n_arr + n_sem]
        at = 0
        for r, (kind, n) in enumerate(kinds):
            for cp in _EXCHANGES[kind][1](srcs[at:at + n], dsts[at:at + n], sem_refs[2 * r], sem_refs[2 * r + 1]):
                cp.wait()
            at += n

    hbm = [pltpu.HBM(a.shape, a.dtype) for a in (*arrays, *lands)]
    return pl.pallas_call(
        body, name=name, out_shape=tuple(hbm),
        in_specs=[_HBM] * (2 * n_arr) + [_SEM] * n_sem + [_ANY], out_specs=tuple([_HBM] * (2 * n_arr)),
        input_output_aliases={i: i for i in range(2 * n_arr)},
        compiler_params=pltpu.CompilerParams(has_side_effects=_FLOWS),
    )(*arrays, *lands, *sems, after)[0 if keep_sources else n_arr:]


def _chip_partial(pos, mine, theirs, relations, out_dtype, row_tile, name):
    n_slots, m, n = mine.shape
    q0 = relations[0]

    def chip_of(qi, pos_ref):
        q = qi + q0
        return pos_ref[0] ^ (q >> 1), pos_ref[1] ^ (q & 1)

    def mine_map(qi, t, pos_ref):
        px, py = chip_of(qi, pos_ref)
        return (4 * px + 2 * py + pos_ref[2] if n_slots == N_DEV else 2 * px + py), t, 0

    def theirs_map(qi, t, pos_ref):
        px, py = chip_of(qi, pos_ref)
        return 2 * px + py, t, 0

    def body(pos_ref, a_ref, b_ref, o_ref):
        o_ref[...] = (a_ref[...] + b_ref[...]).astype(out_dtype)

    return pl.pallas_call(
        body, name=name,
        out_shape=jax.ShapeDtypeStruct((len(relations), m, n), out_dtype),
        grid_spec=pltpu.PrefetchScalarGridSpec(
            num_scalar_prefetch=1, grid=(len(relations), m // row_tile),
            in_specs=[pl.BlockSpec((None, row_tile, n), mine_map), pl.BlockSpec((None, row_tile, n), theirs_map)],
            out_specs=pl.BlockSpec((None, row_tile, n), lambda qi, t, pos_ref: (qi, t, 0))),
        compiler_params=pltpu.CompilerParams(dimension_semantics=("arbitrary", "arbitrary")),
    )(pos, mine, theirs)


PARTS = ((0, 512), (512, 640))
LOCAL_QUEUE = 1


def _gather_norm_proj(pos, x2d, small_shard, norm_g, w_in_shard, w_out_shards, n_chunk):
    seq = x2d.shape[0]
    n_tiles = seq // TILE + 1
    tp = n_tiles * TILE
    n_parts = len(PARTS)
    widest = max(width for _, width in PARTS)
    units = [(s, u) for s in range(2) for u in range(n_parts)]
    units += [(s, u) for u in range(n_parts) for s in (2, 3, 5, 6)] + [(s, u) for u in range(n_parts) for s in (4, 7)]
    n_units = len(units)
    over_ici = [m for m, (s, _) in enumerate(units) if s in (2, 3)]
    handled_at = {m: m - 1 for m in range(1, n_units)}
    handled_at.update({m: over_ici[0] - 1 + i for i, m in enumerate(over_ici)})
    assert all(step < m for m, step in handled_at.items())
    n_steps = n_tiles + n_units
    chunk = tp // n_chunk

    def body(pos_ref, x_ref, g_ref, small_ref, win_ref, wa_ref, wb_ref, wo_ref,
             ht_ref, proj_ref, meta_ref, small_all, win_all, wa_all, wb_all, wo_all,
             h_all, wbuf, rbuf, small_buf, send_sems, recv_sems, local_sems, small_send, small_recv):
        g = pl.program_id(0)
        x, y, c = lax.axis_index("x"), lax.axis_index("y"), lax.axis_index("c")
        me, sibling = (x, y, c), (x, y, 1 - c)
        chips = [(1 - x, y), (x, 1 - y), (1 - x, 1 - y)]
        shards = (win_ref, wa_ref, wb_ref, wo_ref)
        gathered = (win_all, wa_all, wb_all, wo_all)
        n_arrays = len(shards)
        blocks = [me, sibling] + [(*chip, c) for chip in chips] + [(*chip, 1 - c) for chip in chips]

        def index(block):
            px, py, pc = block
            return 4 * px + 2 * py + pc

        def part(ref, a, u):
            return ref.at[:, pl.ds(PARTS[u][0], PARTS[u][1])] if a == 0 else ref

        def slot(a, block, u):
            return part(gathered[a].at[index(block)], a, u)

        def sem(a, k, u):
            return n_parts * k + u if a == 0 else 7 * n_parts + 7 * (a - 1) + k

        def copy(a, k, block, to, u=0, from_shard=False):
            return pltpu.make_async_remote_copy(
                src_ref=part(shards[a], a, u) if from_shard else slot(a, block, u), dst_ref=slot(a, block, u),
                send_sem=send_sems.at[sem(a, k, u)], recv_sem=recv_sems.at[sem(a, k, u)],
                device_id=to, device_id_type=MESH)

        def keep(a):
            return pltpu.make_async_copy(shards[a], gathered[a].at[index(me)], local_sems.at[a])

        def load(m):
            s, u = units[m]
            src = part(win_ref, 0, u) if s == 0 else slot(0, blocks[s], u)
            return pltpu.make_async_copy(src, wbuf.at[m % 2, :, 0:PARTS[u][1]], local_sems.at[n_arrays + m % 2])

        def store(m):
            s, u = units[m]
            col0 = pl.multiple_of(index(blocks[s]) * COLS + PARTS[u][0], LANES)
            return pltpu.make_async_copy(rbuf.at[m % 2, :, 0:PARTS[u][1]],
                                         proj_ref.at[:, pl.ds(col0, PARTS[u][1])], local_sems.at[n_arrays + 2 + m % 2])

        def by_x(a, u):
            return u == 0 if a == 0 else a < 3

        def relay(a, u=0):
            src, to = (blocks[3], blocks[2]) if by_x(a, u) else (blocks[2], blocks[3])
            return copy(a, 3, src, to, u)

        def arrive(m):
            s, u = units[m]
            if s == 1:
                copy(0, 0, sibling, me, u).wait_recv()
            elif 2 <= s <= 4:
                copy(0, s - 1, blocks[s], me, u).wait_recv()
                copy(0, s + 2, blocks[s], sibling, u).start()
                if s < 4 and by_x(0, u) == (s == 3):
                    relay(0, u).start()
            elif s >= 5:
                copy(0, s - 1, blocks[s], me, u).wait_recv()

        def pass_on_out(j):
            for a in range(1, 4):
                copy(a, j + 1, blocks[2 + j], me).wait_recv()
                copy(a, j + 4, blocks[2 + j], sibling).start()
                if j < 2 and by_x(a, 0) == (j == 1):
                    relay(a).start()

        targets = [sibling, blocks[2], blocks[3]]
        relays_started = max(handled_at[m] for m, (s, u) in enumerate(units) if s in (2, 3) and by_x(0, u) == (s == 3))

        def small_copies():
            return _all_copies([small_ref], [small_all], small_send, small_recv)

        @pl.when(g == 0)
        def _():
            for cp in small_copies():
                cp.start()
            for a in range(n_arrays):
                keep(a).start()
            for u in range(n_parts):
                for k, to in enumerate(targets):
                    copy(0, k, me, to, u, from_shard=True).start()
            for a in range(1, 4):
                copy(a, 0, me, sibling, from_shard=True).start()
            load(0).start(priority=LOCAL_QUEUE)

        @pl.when(g == n_tiles - 2)
        def _():
            for cp in small_copies():
                cp.wait()
            fetch = pltpu.make_async_copy(small_all, small_buf, local_sems.at[n_arrays + 4])
            fetch.start()
            fetch.wait()
            meta_ref[0:TILE - N_META, :] = jnp.zeros((TILE - N_META, D_MODEL), F32)
            meta_ref[TILE - N_META:TILE, :] = jnp.concatenate([small_buf[d, 0:N_META, :] for d in range(N_DEV)], axis=1)

        @pl.when(g < n_tiles)
        def _():
            s0 = jnp.where(g == n_tiles - 1, meta_ref[...], x_ref[...])
            r = lax.rsqrt(_rowmean(s0 * s0) + EPS)
            h32 = (s0 * r) * g_ref[...]
            ht_ref[...] = h32.T.astype(BF16)
            h_all[pl.ds(pl.multiple_of(g * TILE, TILE), TILE), :] = h32.astype(BF16)

        for m in range(n_units):
            @pl.when(g == n_tiles + m)
            def _(m=m):
                load(m).wait()
                for later in range(m + 1, n_units):
                    if handled_at[later] == m:
                        arrive(later)
                if m + 1 < n_units:
                    load(m + 1).start(priority=LOCAL_QUEUE)
                if m == relays_started:
                    for a in range(1, 4):
                        for k in (1, 2):
                            copy(a, k, me, targets[k], from_shard=True).start()
                if m == n_units - 2:
                    pass_on_out(0)
                    pass_on_out(1)
                if m >= 2:
                    store(m - 2).wait()

        m_now = jnp.maximum(g - n_tiles, 0)
        u_now = functools.reduce(jnp.add, [jnp.where(m_now == m, u, 0) for m, (_, u) in enumerate(units)])
        for u, (_, width) in enumerate(PARTS):
            @pl.when((g >= n_tiles) & (u_now == u))
            def _(width=width):
                w = wbuf[m_now % 2, :, 0:width]
                for r in range(n_chunk):
                    rbuf[m_now % 2, r * chunk:(r + 1) * chunk, 0:width] = _dot(h_all[r * chunk:(r + 1) * chunk, :], w)

        for m in range(n_units):
            @pl.when(g == n_tiles + m)
            def _(m=m):
                store(m).start(priority=LOCAL_QUEUE)

        @pl.when(g == n_steps - 1)
        def _():
            pass_on_out(2)
            store(n_units - 2).wait()
            store(n_units - 1).wait()
            for a in range(1, 4):
                copy(a, 0, sibling, me).wait_recv()
                for j in range(3):
                    copy(a, 4 + j, blocks[5 + j], me).wait_recv()
            for a in range(n_arrays):
                for u in range(n_parts if a == 0 else 1):
                    for k, to in enumerate(targets):
                        copy(a, k, me, to, u, from_shard=True).wait_send()
                    relay(a, u).wait_send()
                    for j in range(3):
                        copy(a, 4 + j, blocks[2 + j], sibling, u).wait_send()
                keep(a).wait()

    n_x = n_tiles - 1
    return pl.pallas_call(
        body, name="gather_norm_proj",
        out_shape=[jax.ShapeDtypeStruct((D_MODEL, tp), BF16), jax.ShapeDtypeStruct((tp, D_IN), F32),
                   jax.ShapeDtypeStruct((TILE, D_MODEL), F32), jax.ShapeDtypeStruct((N_DEV,) + small_shard.shape[1:], F32),
                   jax.ShapeDtypeStruct((N_DEV,) + w_in_shard.shape, BF16)]
                  + [jax.ShapeDtypeStruct((N_DEV,) + w.shape, BF16) for w in w_out_shards],
        grid_spec=pltpu.PrefetchScalarGridSpec(
            num_scalar_prefetch=1, grid=(n_steps,),
            in_specs=[pl.BlockSpec((TILE, D_MODEL), lambda g, pos_ref: (jnp.minimum(g, n_x - 1), 0)),
                      _VMEM, _ANY, _ANY, _ANY, _ANY, _ANY],
            out_specs=[pl.BlockSpec((D_MODEL, TILE), lambda g, pos_ref: (0, jnp.minimum(g, n_tiles - 1))),
                       _ANY, _VMEM, _ANY, _ANY, _ANY, _ANY, _ANY],
            scratch_shapes=[pltpu.VMEM((tp, D_MODEL), BF16), pltpu.VMEM((2, D_MODEL, widest), BF16),
                            pltpu.VMEM((2, tp, widest), F32), pltpu.VMEM((N_DEV,) + small_shard.shape[1:], F32),
                            pltpu.SemaphoreType.DMA((7 * n_parts + 21,)), pltpu.SemaphoreType.DMA((7 * n_parts + 21,)),
                            pltpu.SemaphoreType.DMA((9,)),
                            pltpu.SemaphoreType.DMA((N_DEV,)), pltpu.SemaphoreType.DMA((N_DEV,))]),
        compiler_params=pltpu.CompilerParams(dimension_semantics=("arbitrary",), vmem_limit_bytes=VMEM_LIMIT),
    )(pos, x2d, norm_g, small_shard, w_in_shard, *w_out_shards)


C_AVAL, C_AGLU, C_AZ, C_BB, C_BC, C_BX, C_BZ, C_GA, C_GB = (k * D_MODEL for k in range(9))
S_AZ, S_BB, S_BZ, S_GA, S_GB = (k * D_MODEL for k in range(5))


def _fused_pass(proj, x2d, tgt2d, meta_tile, conv_a_w, conv_a_b, ln_a_g, ln_a_b, b_a_out, conv_b_w, final_g,
                w_a, w_b, w_o, w_a_t, w_b_t, w_o_t, n_tiles):
    T = TILE
    tp = n_tiles * T
    inv_d = 1.0 / D_MODEL

    def block_of(tile):
        return jnp.where(tile == 0, n_tiles - 1, tile - 1)

    def cur(i):
        return block_of(jnp.minimum(i, n_tiles - 1))

    def prev(i):
        return block_of(jnp.clip(i - 1, 0, n_tiles - 1))

    def xblk(i):
        return jnp.maximum(jnp.minimum(i, n_tiles - 1) - 1, 0)

    def body(proj_ref, aprev, cprev, x_ref, tgt_ref, meta_ref, caw_ref, cab_ref, lng_ref, lnb_ref, bao_ref, cbw_ref,
             fg_ref, wa_ref, wb_ref, wo_ref, wat_ref, wbt_ref, wot_ref,
             dproj_ref, ds1_ref, lhs_ref, rhs_ref, small_ref,
             ua0_buf, cb_buf, dua1_buf, dc3_buf, stage, ua1_buf, c3_buf,
             dpa_buf, dpb_buf, dcaw8, dcbw8, shift_buf):
        i = pl.program_id(0)
        this, before = i % 2, 1 - i % 2

        @pl.when(i == 0)
        def _init():
            for buf in (ua0_buf, cb_buf, dua1_buf, dc3_buf, dcaw8, dcbw8):
                buf[...] = jnp.zeros(buf.shape, buf.dtype)
            small_ref[...] = jnp.zeros(small_ref.shape, F32)

        @pl.when(i >= 1)
        def _emit_stage():
            dproj_ref[:, C_AZ:C_BC] = stage[:, S_AZ:S_BZ]
            dproj_ref[:, C_BZ:D_IN] = stage[:, S_BZ:S_GB + D_MODEL]

        @pl.when(i < n_tiles)
        def _front():
            def conv_chunk(cc, carry):
                c0 = pl.multiple_of(cc * LANES, LANES)
                lanes = pl.ds(c0, LANES)

                def col(base):
                    return pl.ds(pl.multiple_of(base + cc * LANES, LANES), LANES)

                ua0 = proj_ref[:, col(C_AVAL)] * _sigmoid(proj_ref[:, col(C_AGLU)])
                ua0_buf[this, 0:HALO, lanes] = ua0_buf[before, T:T + HALO, lanes]
                ua0_buf[this, HALO:HALO + T, lanes] = ua0
                acc = jnp.broadcast_to(cab_ref[:, lanes], (T, LANES))
                lead = HALO - (CONV_A - 1)
                for r in range(SUBLANES):
                    taps = [k for k in range(CONV_A) if (k + lead) % SUBLANES == r]
                    rows = T + SUBLANES * max((k + lead) // SUBLANES for k in taps)
                    if r:
                        shift_buf[r, 0:rows, :] = ua0_buf[this, pl.ds(r, rows), lanes]
                    for k in taps:
                        q = (k + lead) // SUBLANES
                        if r:
                            win = shift_buf[r, SUBLANES * q:SUBLANES * q + T, :]
                        else:
                            win = ua0_buf[this, pl.ds(SUBLANES * q, T), lanes]
                        acc = acc + caw_ref[k:k + 1, lanes] * win
                ua1_buf[:, lanes] = acc
                cb = proj_ref[:, col(C_BC)] * proj_ref[:, col(C_BX)]
                cb_buf[this, 0:SUBLANES, lanes] = cb_buf[before, T:T + SUBLANES, lanes]
                cb_buf[this, SUBLANES:SUBLANES + T, lanes] = cb
                lead_b = SUBLANES - (CONV_B - 1)
                acc3 = cbw_ref[0:1, lanes] * cb_buf[this, pl.ds(lead_b, T), lanes]
                for k in range(1, CONV_B):
                    acc3 = acc3 + cbw_ref[k:k + 1, lanes] * cb_buf[this, pl.ds(lead_b + k, T), lanes]
                c3_buf[:, lanes] = acc3
                return carry

            lax.fori_loop(0, N_CHUNK, conv_chunk, 0)

            ua1 = ua1_buf[...]
            xc = ua1 - _rowmean(ua1)
            rstd = lax.rsqrt(_rowmean(xc * xc) + EPS)
            xhat = xc * rstd
            ua2 = xhat * lng_ref[...] + lnb_ref[...]
            sg2 = _sigmoid(ua2)
            ua3 = ua2 * sg2
            a_z = proj_ref[:, C_AZ:C_AZ + D_MODEL]
            sz = _sigmoid(a_z)
            silu_az = a_z * sz
            lhs_ref[0] = (ua3 * silu_az).astype(BF16)
            b_z = proj_ref[:, C_BZ:C_BZ + D_MODEL]
            sbz = _sigmoid(b_z)
            silu_bz = b_z * sbz
            b_b = proj_ref[:, C_BB:C_BB + D_MODEL]
            c3 = c3_buf[...]
            ub = b_b * c3
            lhs_ref[1] = (ub * silu_bz).astype(BF16)

            ya = _dot(lhs_ref[0], wa_ref[...]) + bao_ref[...]
            yb = _dot(lhs_ref[1], wb_ref[...])
            sga = _sigmoid(proj_ref[:, C_GA:C_GA + D_MODEL])
            sgb = _sigmoid(proj_ref[:, C_GB:C_GB + D_MODEL])
            m_b = (sga * ya + sgb * yb).astype(BF16)
            lhs_ref[2] = m_b
            s0 = jnp.where(i == 0, meta_ref[...], x_ref[...])
            s1 = s0 + _dot(m_b, wo_ref[...])
            r1 = lax.rsqrt(_rowmean(s1 * s1) + EPS)
            y = (s1 * r1) * fg_ref[...]
            is_token = (i >= 1).astype(F32)
            err = (y - tgt_ref[...]) * is_token
            small_ref[ROW_LOSS:ROW_LOSS + 1, :] += (0.5 * inv_d) * _colsum(err * err)
            dy = err * inv_d
            small_ref[ROW_FINAL_G:ROW_FINAL_G + 1, :] += _colsum(dy * (s1 * r1))
            gy = dy * fg_ref[...]
            ds1 = r1 * gy - s1 * ((r1 * r1 * r1) * _rowmean(gy * s1))
            ds1_ref[...] = ds1
            ds1_b = ds1.astype(BF16)
            rhs_ref[2] = ds1_b
            dm = _dot(ds1_b, wot_ref[...])
            dya = dm * sga
            dyb = dm * sgb
            stage[:, S_GA:S_GA + D_MODEL] = (dya * ya * (1.0 - sga)).astype(BF16)
            stage[:, S_GB:S_GB + D_MODEL] = (dyb * yb * (1.0 - sgb)).astype(BF16)
            small_ref[ROW_B_A_OUT:ROW_B_A_OUT + 1, :] += _colsum(dya)
            dya_b = dya.astype(BF16)
            dyb_b = dyb.astype(BF16)
            rhs_ref[0] = dya_b
            rhs_ref[1] = dyb_b
            dpa_buf[...] = _dot(dya_b, wat_ref[...])
            dpb_buf[...] = _dot(dyb_b, wbt_ref[...])

            dpa = dpa_buf[...]
            stage[:, S_AZ:S_AZ + D_MODEL] = (dpa * ua3 * (sz + silu_az * (1.0 - sz))).astype(BF16)
            dua2 = dpa * silu_az * (sg2 + ua3 * (1.0 - sg2))
            small_ref[ROW_LN_G:ROW_LN_G + 1, :] += _colsum(dua2 * xhat)
            small_ref[ROW_LN_B:ROW_LN_B + 1, :] += _colsum(dua2)
            dxh = dua2 * lng_ref[...]
            dua1 = rstd * (dxh - _rowmean(dxh) - xhat * _rowmean(dxh * xhat))
            small_ref[ROW_CONV_A_B:ROW_CONV_A_B + 1, :] += _colsum(dua1)
            dua1_buf[this, 0:T, :] = dua1
            dua1_buf[before, T:T + HALO, :] = dua1[0:HALO]
            dpb = dpb_buf[...]
            stage[:, S_BZ:S_BZ + D_MODEL] = (dpb * ub * (sbz + silu_bz * (1.0 - sbz))).astype(BF16)
            dub = dpb * silu_bz
            stage[:, S_BB:S_BB + D_MODEL] = (dub * c3).astype(BF16)
            dc3 = dub * b_b
            dc3_buf[this, 0:T, :] = dc3
            dc3_buf[before, T:T + SUBLANES, :] = dc3[0:SUBLANES]

        @pl.when(i == n_tiles)
        def _no_later_tile():
            dua1_buf[before, T:T + HALO, :] = jnp.zeros((HALO, D_MODEL), F32)
            dc3_buf[before, T:T + SUBLANES, :] = jnp.zeros((SUBLANES, D_MODEL), F32)

        @pl.when(i >= 1)
        def _lagged():
            def convt_chunk(cc, carry):
                c0 = pl.multiple_of(cc * LANES, LANES)
                lanes = pl.ds(c0, LANES)

                def col(base):
                    return pl.ds(pl.multiple_of(base + cc * LANES, LANES), LANES)

                ua0 = ua0_buf[before, HALO:HALO + T, lanes]
                acc = jnp.zeros((T, LANES), F32)
                for r in range(SUBLANES):
                    shifts = [j for j in range(CONV_A) if j % SUBLANES == r]
                    rows = T + shifts[-1] - r
                    if r:
                        shift_buf[r, 0:rows, :] = dua1_buf[before, pl.ds(r, rows), lanes]
                    for j in shifts:
                        k = CONV_A - 1 - j
                        if r:
                            later = shift_buf[r, j - r:j - r + T, :]
                        else:
                            later = dua1_buf[before, pl.ds(j, T), lanes]
                        acc = acc + caw_ref[k:k + 1, lanes] * later
                        dcaw8[SUBLANES * k:SUBLANES * (k + 1), lanes] += _fold8(ua0 * later)
                a_val = aprev[:, col(0)]
                sg = _sigmoid(aprev[:, col(D_MODEL)])
                dproj_ref[:, col(C_AVAL)] = (acc * sg).astype(BF16)
                dproj_ref[:, col(C_AGLU)] = (acc * a_val * (sg * (1.0 - sg))).astype(BF16)

                cb = cb_buf[before, SUBLANES:SUBLANES + T, lanes]
                acc3 = jnp.zeros((T, LANES), F32)
                for j in range(CONV_B):
                    k = CONV_B - 1 - j
                    later = dc3_buf[before, pl.ds(j, T), lanes]
                    acc3 = acc3 + cbw_ref[k:k + 1, lanes] * later
                    dcbw8[SUBLANES * k:SUBLANES * (k + 1), lanes] += _fold8(cb * later)
                dproj_ref[:, col(C_BC)] = (acc3 * cprev[:, col(D_MODEL)]).astype(BF16)
                dproj_ref[:, col(C_BX)] = (acc3 * cprev[:, col(0)]).astype(BF16)
                return carry

            lax.fori_loop(0, N_CHUNK, convt_chunk, 0)

        @pl.when(i == n_tiles)
        def _finish():
            for k in range(CONV_A):
                small_ref[ROW_CONV_A_W + k:ROW_CONV_A_W + k + 1, :] = _colsum(dcaw8[SUBLANES * k:SUBLANES * (k + 1), :])
            for k in range(CONV_B):
                small_ref[ROW_CONV_B_W + k:ROW_CONV_B_W + k + 1, :] = _colsum(dcbw8[SUBLANES * k:SUBLANES * (k + 1), :])

    pair = 2 * D_MODEL
    return pl.pallas_call(
        body, name="fused_pass", grid=(n_tiles + 1,),
        out_shape=[
            jax.ShapeDtypeStruct((tp, D_IN), BF16),
            jax.ShapeDtypeStruct((tp, D_MODEL), F32),
            jax.ShapeDtypeStruct((3, tp, D_MODEL), BF16),
            jax.ShapeDtypeStruct((3, tp, D_MODEL), BF16),
            jax.ShapeDtypeStruct((SMALL_A_ROWS, D_MODEL), F32),
        ],
        in_specs=[
            pl.BlockSpec((T, D_IN), lambda i: (cur(i), 0)),
            pl.BlockSpec((T, pair), lambda i: (prev(i), C_AVAL // pair)),
            pl.BlockSpec((T, pair), lambda i: (prev(i), C_BC // pair)),
            pl.BlockSpec((T, D_MODEL), lambda i: (xblk(i), 0)),
            pl.BlockSpec((T, D_MODEL), lambda i: (xblk(i), 0)),
            _VMEM, _VMEM, _VMEM, _VMEM, _VMEM, _VMEM, _VMEM, _VMEM,
            *[_resident((D_MODEL, D_MODEL)) for _ in range(6)],
        ],
        out_specs=[
            pl.BlockSpec((T, D_IN), lambda i: (prev(i), 0)),
            pl.BlockSpec((T, D_MODEL), lambda i: (cur(i), 0)),
            pl.BlockSpec((3, T, D_MODEL), lambda i: (0, cur(i), 0)),
            pl.BlockSpec((3, T, D_MODEL), lambda i: (0, cur(i), 0)),
            _VMEM,
        ],
        scratch_shapes=[
            pltpu.VMEM((2, HALO + T, D_MODEL), F32),
            pltpu.VMEM((2, SUBLANES + T, D_MODEL), F32),
            pltpu.VMEM((2, T + HALO, D_MODEL), F32),
            pltpu.VMEM((2, T + SUBLANES, D_MODEL), F32),
            pltpu.VMEM((T, 5 * D_MODEL), BF16),
            pltpu.VMEM((T, D_MODEL), F32),
            pltpu.VMEM((T, D_MODEL), F32),
            pltpu.VMEM((T, D_MODEL), F32),
            pltpu.VMEM((T, D_MODEL), F32),
            pltpu.VMEM((32 * SUBLANES, D_MODEL), F32),
            pltpu.VMEM((SUBLANES * SUBLANES, D_MODEL), F32),
            pltpu.VMEM((SUBLANES, T + HALO, LANES), F32),
        ],
        compiler_params=pltpu.CompilerParams(dimension_semantics=("arbitrary",), vmem_limit_bytes=VMEM_LIMIT),
    )(proj, proj, proj, x2d, tgt2d, meta_tile, conv_a_w, conv_a_b, ln_a_g, ln_a_b, b_a_out, conv_b_w, final_g,
      w_a, w_b, w_o, w_a_t, w_b_t, w_o_t)


def _input_bwd(dproj, ds1, x2d, meta_tile, norm_g, w_in_all, row_tile, after):
    seq = x2d.shape[0]
    n_steps = seq // row_tile
    meta_block = seq // TILE

    def backward(dp_ref, ds1_ref, s0_ref, g_ref, w_ref, out_ref, vec_ref):
        dh = _dot_nt(dp_ref[:, 0:COLS], w_ref[0])
        for j in range(1, N_DEV):
            dh = dh + _dot_nt(dp_ref[:, j * COLS:(j + 1) * COLS], w_ref[j])
        s0v = s0_ref[...]
        r = lax.rsqrt(_rowmean(s0v * s0v) + EPS)
        gh = dh * g_ref[...]
        out_ref[...] = ds1_ref[...] + r * gh - s0v * ((r * r * r) * _rowmean(gh * s0v))
        vec_ref[ROW_NORM_G:ROW_NORM_G + 1, :] += _colsum(dh * (s0v * r))

    def body(dp_ref, ds1_ref, x_ref, dpm_ref, ds1m_ref, meta_ref, g_ref, w_ref, after_ref, gx_ref, small_ref, gmeta_buf):
        t = pl.program_id(0)

        @pl.when(t == 0)
        def _():
            small_ref[...] = jnp.zeros(small_ref.shape, F32)

        backward(dp_ref, ds1_ref, x_ref, g_ref, w_ref, gx_ref, small_ref)

        @pl.when(t == n_steps - 1)
        def _():
            backward(dpm_ref, ds1m_ref, meta_ref, g_ref, w_ref, gmeta_buf, small_ref)
            small_ref[ROW_META:ROW_META + N_META, :] = gmeta_buf[TILE - N_META:TILE, :]

    return pl.pallas_call(
        body, name="input_bwd", grid=(n_steps,),
        out_shape=[jax.ShapeDtypeStruct(x2d.shape, F32), jax.ShapeDtypeStruct((SMALL_B_ROWS, D_MODEL), F32)],
        in_specs=[pl.BlockSpec((row_tile, D_IN), lambda t: (t, 0)),
                  pl.BlockSpec((row_tile, D_MODEL), lambda t: (t, 0)),
                  pl.BlockSpec((row_tile, D_MODEL), lambda t: (t, 0)),
                  pl.BlockSpec((TILE, D_IN), lambda t: (meta_block, 0)),
                  pl.BlockSpec((TILE, D_MODEL), lambda t: (meta_block, 0)),
                  _VMEM, _VMEM, _resident((N_DEV, D_MODEL, COLS)), _ANY],
        out_specs=[pl.BlockSpec((row_tile, D_MODEL), lambda t: (t, 0)), _VMEM],
        scratch_shapes=[pltpu.VMEM((TILE, D_MODEL), F32)],
        compiler_params=pltpu.CompilerParams(dimension_semantics=("arbitrary",), vmem_limit_bytes=VMEM_LIMIT),
    )(dproj, ds1, x2d, dproj, ds1, meta_tile, norm_g, w_in_all, after)


def _grad_w_in_half(pos, h_t, dproj, k_tile, other_side, rides, name, after=None, add_to=None, narrow=False):
    tp = h_t.shape[1]
    n_k = tp // k_tile
    order = [] if after is None else [after]
    summing = add_to is not None
    assert not (summing and narrow)

    def column_block(q, k, pos_ref):
        return k, 2 * q + (1 - pos_ref[2] if other_side else pos_ref[2])

    def body(pos_ref, h_ref, dp_ref, *refs):
        acc = refs[-1]

        @pl.when(pl.program_id(1) == 0)
        def _():
            acc[...] = refs[0][...].astype(F32) if summing else jnp.zeros(acc.shape, F32)

        acc[...] += _dot(h_ref[...], dp_ref[...])

        if summing or narrow:
            @pl.when(pl.program_id(1) == n_k - 1)
            def _():
                refs[-2][...] = acc[...].astype(BF16)

        if summing:
            @pl.when((pl.program_id(1) == n_k - 1) & (pl.program_id(0) == 2 * pos_ref[0] + pos_ref[1]))
            def _():
                refs[-3][...] = acc[...]

    ride = [a for _, arrays in rides for a in arrays]
    n_arr = len(ride)
    ride_shapes, ride_sems = _ride_shapes(rides)
    block = (None, D_MODEL, COLS)
    extra_in = [add_to] if summing else []
    extra_in_specs = [pl.BlockSpec(block, lambda q, k, pos_ref: (q, 0, 0))] if summing else []
    extra_out = [jax.ShapeDtypeStruct((4, D_MODEL, COLS), BF16)] if summing else []
    extra_out_specs = [pl.BlockSpec(block, lambda q, k, pos_ref: (q ^ (2 * pos_ref[0] + pos_ref[1]), 0, 0))] \
        if summing else []
    body = _riding(body, 3 + len(extra_in) + len(order), 1 + len(extra_out), rides,
                   lambda: (pl.program_id(0) == 0) & (pl.program_id(1) == 0),
                   lambda: (pl.program_id(0) == 3) & (pl.program_id(1) == n_k - 1))
    return pl.pallas_call(
        body, name=name,
        out_shape=[jax.ShapeDtypeStruct((1 if summing else 4, D_MODEL, COLS), BF16 if narrow else F32)]
        + extra_out + ride_shapes,
        grid_spec=pltpu.PrefetchScalarGridSpec(
            num_scalar_prefetch=1, grid=(4, n_k),
            in_specs=[pl.BlockSpec((D_MODEL, k_tile), lambda q, k, pos_ref: (0, k)),
                      pl.BlockSpec((k_tile, COLS), column_block)] + extra_in_specs + [_ANY] * (len(order) + n_arr),
            out_specs=[pl.BlockSpec(block, lambda q, k, pos_ref: (0 if summing else q, 0, 0))]
            + extra_out_specs + [_ANY] * n_arr,
            scratch_shapes=([pltpu.VMEM((D_MODEL, COLS), F32)] if summing or narrow else []) + ride_sems),
        compiler_params=pltpu.CompilerParams(dimension_semantics=("arbitrary", "arbitrary"),
                                             vmem_limit_bytes=VMEM_LIMIT),
    )(pos, h_t, dproj, *extra_in, *order, *ride)


def _grad_w_out(lhs, rhs, k_tile, after):
    tp = lhs.shape[1]

    def body(a_ref, b_ref, after_ref, o_ref):
        @pl.when(pl.program_id(1) == 0)
        def _():
            o_ref[...] = jnp.zeros(o_ref.shape, F32)

        o_ref[...] += _dot_tn(a_ref[...], b_ref[...]).reshape(N_DEV, ROWS_OUT, D_MODEL)

    return pl.pallas_call(
        body, name="grad_w_out", grid=(3, tp // k_tile),
        out_shape=jax.ShapeDtypeStruct((N_DEV, 3, ROWS_OUT, D_MODEL), F32),
        in_specs=[pl.BlockSpec((None, k_tile, D_MODEL), lambda w, k: (w, k, 0)),
                  pl.BlockSpec((None, k_tile, D_MODEL), lambda w, k: (w, k, 0)), _ANY],
        out_specs=pl.BlockSpec((N_DEV, None, ROWS_OUT, D_MODEL), lambda w, k: (0, w, 0, 0)),
        compiler_params=pltpu.CompilerParams(dimension_semantics=("arbitrary", "arbitrary"),
                                             vmem_limit_bytes=VMEM_LIMIT),
    )(lhs, rhs, after)


def _adamw_math(w, g, m, v):
    m = ADAM_B1 * m + (1.0 - ADAM_B1) * g
    v = ADAM_B2 * v + (1.0 - ADAM_B2) * (g * g)
    m_hat = m / (1.0 - ADAM_B1 ** ADAM_STEP)
    v_hat = v / (1.0 - ADAM_B2 ** ADAM_STEP)
    delta = -ADAM_LR * (m_hat / (jnp.sqrt(v_hat) + ADAM_EPS) + ADAM_WD * w)
    return delta, m, v


def _adamw_sharded(pos, mine, theirs, landed, weights, row_tile, name, after=None):
    order = [] if after is None else [after]
    rows, n = weights[0][0].shape
    n_slots = mine.shape[0]
    per_shard = rows // row_tile
    assert per_shard == 1 or len(weights) == 1

    def mine_map(j, t, pos_ref):
        chip = 2 * pos_ref[0] + pos_ref[1]
        return {N_DEV: 2 * chip + pos_ref[2], 4: chip, 1: 0}[n_slots], j * per_shard + t, 0

    def theirs_map(j, t, pos_ref):
        return 2 * pos_ref[0] + pos_ref[1], j * per_shard + t, 0

    def body(pos_ref, mine_ref, *refs):
        if theirs is not None:
            g = mine_ref[...] + refs[0][...]
            refs = refs[1:]
        else:
            g = mine_ref[...]
        land_ref, refs = refs[0], refs[1:]
        ins, outs = refs[:3 * len(weights)], refs[3 * len(weights) + len(order):]
        for k in range(3):
            g = g + land_ref[k].astype(F32)
        for j in range(len(weights)):
            @pl.when(pl.program_id(0) == j)
            def _(j=j):
                w_ref, m_ref, v_ref = ins[3 * j:3 * j + 3]
                delta, m_new, v_new = _adamw_math(w_ref[...], g, m_ref[...], v_ref[...])
                for ref, val in zip(outs[4 * j:4 * j + 4], (g, delta, m_new, v_new)):
                    ref[...] = val

    tile = pl.BlockSpec((row_tile, n), lambda j, t, pos_ref: (t, 0))
    res = pl.pallas_call(
        body, name=name,
        out_shape=[jax.ShapeDtypeStruct((rows, n), F32)] * (4 * len(weights)),
        grid_spec=pltpu.PrefetchScalarGridSpec(
            num_scalar_prefetch=1, grid=(len(weights), per_shard),
            in_specs=[pl.BlockSpec((None, row_tile, n), mine_map)]
            + ([pl.BlockSpec((None, row_tile, n), theirs_map)] if theirs is not None else [])
            + [pl.BlockSpec((3, row_tile, n), lambda j, t, pos_ref: (0, j * per_shard + t, 0))]
            + [tile] * (3 * len(weights)) + [_ANY] * len(order),
            out_specs=[tile] * (4 * len(weights))),
        compiler_params=pltpu.CompilerParams(dimension_semantics=("arbitrary", "arbitrary")),
    )(pos, mine, *([theirs] if theirs is not None else []), landed, *[a for wmv in weights for a in wmv], *order)
    return [res[4 * j:4 * j + 4] for j in range(len(weights))]


def _adamw_small(pos, gathered, params, row_by_row=()):
    n_par, n_src = len(params), len(gathered)

    def body(pos_ref, *refs):
        g_refs, gc_refs = refs[:n_src], refs[n_src:2 * n_src]
        ins = refs[2 * n_src:2 * n_src + 3 * n_par]
        outs = refs[2 * n_src + 3 * n_par:]
        loss_ref = outs[4 * n_par]

        def reduced(ref, row, n_rows):
            g = ref[0, row:row + n_rows, :]
            for d in range(1, N_DEV):
                g = g + ref[d, row:row + n_rows, :]
            return g

        for p, (src, row, n_rows, sharded, _, _, _) in enumerate(params):
            g = reduced((gc_refs if sharded else g_refs)[src], row, n_rows)
            w_ref, m_ref, v_ref = ins[3 * p:3 * p + 3]
            delta, m_new, v_new = _adamw_math(w_ref[...], g, m_ref[...], v_ref[...])
            for kind, res in enumerate((g, delta, m_new, v_new)):
                if p in row_by_row:
                    for k in range(n_rows):
                        outs[4 * p + kind][k] = res[k:k + 1, :]
                else:
                    outs[4 * p + kind][...] = res
        loss = jnp.sum(reduced(g_refs[0], ROW_LOSS, 1), axis=1, keepdims=True)
        loss_ref[...] = jnp.broadcast_to(loss, loss_ref.shape)

    out_shape = []
    for p, (_, _, n_rows, _, w, _, _) in enumerate(params):
        out_shape += [jax.ShapeDtypeStruct((n_rows, 1, LANES) if p in row_by_row else w.shape, F32)] * 4
    out_shape.append(jax.ShapeDtypeStruct((1, LANES), F32))
    flat = [a for (_, _, _, _, w, m, v) in params for a in (w, m, v)]
    my_lanes = [pl.BlockSpec((N_DEV, g.shape[1], LANES), lambda i, pos_ref: (0, 0, 4 * pos_ref[0] + 2 * pos_ref[1] + pos_ref[2]))
                for g in gathered]
    return pl.pallas_call(
        body, name="adamw_small", out_shape=out_shape,
        grid_spec=pltpu.PrefetchScalarGridSpec(
            num_scalar_prefetch=1, grid=(1,),
            in_specs=[_VMEM] * n_src + my_lanes + [_VMEM] * len(flat), out_specs=[_VMEM] * len(out_shape)),
    )(pos, *gathered, *gathered, *flat)


def _pad_rows(a, rows):
    return jnp.concatenate([a, jnp.zeros((rows - a.shape[0], a.shape[1]), a.dtype)], axis=0)


def kernel(x, meta_tokens, norm_g, w_in, conv_a_w, conv_a_b, ln_a_g, ln_a_b, w_a_out, b_a_out, conv_b_w, w_b_out, w_out, final_g, loss_target, m_meta_tokens, m_norm_g, m_w_in, m_conv_a_w, m_conv_a_b, m_ln_a_g, m_ln_a_b, m_w_a_out, m_b_a_out, m_conv_b_w, m_w_b_out, m_w_out, m_final_g, v_meta_tokens, v_norm_g, v_w_in, v_conv_a_w, v_conv_a_b, v_ln_a_g, v_ln_a_b, v_w_a_out, v_b_a_out, v_conv_b_w, v_w_b_out, v_w_out, v_final_g):
    seq = x.shape[1]
    assert x.shape == (1, seq, D_MODEL) and seq % TILE == 0 and w_in.shape == (1, D_MODEL, COLS)
    n_tiles = seq // TILE + 1
    tp = n_tiles * TILE
    pos = jnp.stack([lax.axis_index("x"), lax.axis_index("y"), lax.axis_index("c")]).astype(jnp.int32)
    x2d = x[0]
    tgt2d = loss_target[0]

    small = jnp.concatenate([meta_tokens, _pad_rows(conv_a_w[0], 32), _pad_rows(conv_b_w[0], SUBLANES)], axis=0)
    final_g2 = final_g.reshape(1, D_MODEL)

    w_out_shards = [w[0].astype(BF16) for w in (w_a_out, w_b_out, w_out)]
    h_t, proj, meta_tile, small_params, w_in_all, *w_out_all = _gather_norm_proj(
        pos, x2d, small[None], norm_g, w_in[0].astype(BF16), w_out_shards, 3)
    small_params = small_params.transpose(1, 0, 2).reshape(small.shape[0], D_MODEL)
    conv_a_full, conv_b_full = small_params[N_META:N_META + 32], small_params[N_META + 32:]
    w_out_all = [w.reshape(D_MODEL, D_MODEL) for w in w_out_all]
    w_out_all_t = [w.T for w in w_out_all]
    dproj, ds1, lhs, rhs, small_a = _fused_pass(
        proj, x2d, tgt2d, meta_tile, conv_a_full, conv_a_b, ln_a_g, ln_a_b, b_a_out, conv_b_full, final_g2,
        w_out_all[0], w_out_all[1], w_out_all[2], w_out_all_t[0], w_out_all_t[1], w_out_all_t[2], n_tiles)
    k_tile = tp // 3
    gw_far, small_a_all = _grad_w_in_half(pos, h_t, dproj, k_tile, True, [("all", (small_a[None],))], "grad_w_in_far",
                                          narrow=True)
    sems, sent, landing, token = _start_exchanges([("sibling_half", (gw_far,))], "rs_far_start")
    gw_out = _grad_w_out(lhs, rhs, k_tile, token).reshape(N_DEV, 3 * ROWS_OUT, D_MODEL)
    (their_in,) = _wait_exchanges([("sibling_half", 1)], sems, sent, landing, gw_out, "rs_far_wait")
    sems_o, sent_o, landing_o, token = _start_exchanges([("sibling", (gw_out,))], "rs_out_start")
    gw_near, parts_in = _grad_w_in_half(pos, h_t, dproj, k_tile, False, [], "grad_w_in_near", after=token,
                                        add_to=their_in)
    sems_i, sent_i, landing_i, token = _start_exchanges([("chips_by_relation", (parts_in,))], "rs_chips_in_start")
    gw_out, their_out = _wait_exchanges([("sibling", 1)], sems_o, sent_o, landing_o, token, "rs_out_wait",
                                        keep_sources=True)
    parts_out = _chip_partial(pos, gw_out, their_out, (1, 2, 3), BF16, 3 * ROWS_OUT, "rs_parts_w_out")
    sems_o, sent_o, landing_o, token = _start_exchanges([("chips", (parts_out,))], "rs_chips_out_start")
    grad_x, small_b = _input_bwd(dproj, ds1, x2d, meta_tile, norm_g, w_in_all, min(512, seq), token)

    sems_s, sent_s, landing_s, token = _start_exchanges([("all", (small_b[None],))], "gather_small_grads_start")
    (land_in,) = _wait_exchanges([("chips_by_relation", 1)], sems_i, sent_i, landing_i, token, "rs_chips_in_wait")
    (res_in,) = _adamw_sharded(pos, gw_near, None, land_in, [(w_in[0], m_w_in[0], v_w_in[0])], 128, "adamw_w_in")
    (land_out,) = _wait_exchanges([("chips", 1)], sems_o, sent_o, landing_o, res_in[0], "rs_chips_out_wait")
    res_out = _adamw_sharded(
        pos, gw_out, their_out, land_out,
        [(w_a_out[0], m_w_a_out[0], v_w_a_out[0]), (w_b_out[0], m_w_b_out[0], v_w_b_out[0]),
         (w_out[0], m_w_out[0], v_w_out[0])], ROWS_OUT, "adamw_w_out")
    (small_b_all,) = _wait_exchanges([("all", 1)], sems_s, sent_s, landing_s, res_out[2][0], "gather_small_grads_wait")
    small_grads = [small_a_all, small_b_all]
    params = [
        (1, ROW_META, N_META, True, meta_tokens, m_meta_tokens, v_meta_tokens),
        (1, ROW_NORM_G, 1, False, norm_g, m_norm_g, v_norm_g),
        (0, ROW_CONV_A_W, CONV_A, True, conv_a_w[0], m_conv_a_w[0], v_conv_a_w[0]),
        (0, ROW_CONV_A_B, 1, False, conv_a_b, m_conv_a_b, v_conv_a_b),
        (0, ROW_LN_G, 1, False, ln_a_g, m_ln_a_g, v_ln_a_g),
        (0, ROW_LN_B, 1, False, ln_a_b, m_ln_a_b, v_ln_a_b),
        (0, ROW_B_A_OUT, 1, False, b_a_out, m_b_a_out, v_b_a_out),
        (0, ROW_CONV_B_W, CONV_B, True, conv_b_w[0], m_conv_b_w[0], v_conv_b_w[0]),
        (0, ROW_FINAL_G, 1, False, final_g2, m_final_g.reshape(1, D_MODEL), v_final_g.reshape(1, D_MODEL)),
    ]
    conv_weights = (2, 7)
    res_small = _adamw_small(pos, small_grads, params, conv_weights)
    loss = res_small[-1][0, 0]

    def small_res(p, kind, shape):
        res = res_small[4 * p + kind]
        return res.transpose(1, 0, 2) if p in conv_weights else res.reshape(shape)

    per_weight = []
    for kind in range(4):
        per_weight.append([
            small_res(0, kind, meta_tokens.shape),
            small_res(1, kind, norm_g.shape),
            res_in[kind].reshape(w_in.shape),
            small_res(2, kind, conv_a_w.shape),
            small_res(3, kind, conv_a_b.shape),
            small_res(4, kind, ln_a_g.shape),
            small_res(5, kind, ln_a_b.shape),
            res_out[0][kind].reshape(w_a_out.shape),
            small_res(6, kind, b_a_out.shape),
            small_res(7, kind, conv_b_w.shape),
            res_out[1][kind].reshape(w_b_out.shape),
            res_out[2][kind].reshape(w_out.shape),
            small_res(8, kind, final_g.shape),
        ])
    return (loss, grad_x.reshape(x.shape), *per_weight[0], *per_weight[1], *per_weight[2], *per_weight[3])
```
